```python
import jax, jax.numpy as jnp
from jax import lax
import numpy as np

D_MODEL = 1024
BATCH = 8
SEQ = 8192
DEPTH = 1

SGU_WIDTH = D_MODEL
SGU_GROUPS = 8
SGU_GROUP_DIM = SGU_WIDTH // SGU_GROUPS
CHUNK = 128
HEAD_DIM = 64
N_Q_HEADS = D_MODEL // HEAD_DIM
N_KV_HEADS = N_Q_HEADS // 4
Q_PER_KV = N_Q_HEADS // N_KV_HEADS
WINDOW = 128
BLOCK = 128
ROPE_DIM = HEAD_DIM // 4
ROPE_THETA = 500000.0
ATTN_WIDTH = N_Q_HEADS * HEAD_DIM
KV_WIDTH = N_KV_HEADS * HEAD_DIM
D_FF = 4 * D_MODEL
EPS = 1e-6

IN_SPLITS = (SGU_WIDTH, SGU_WIDTH, ATTN_WIDTH, KV_WIDTH, KV_WIDTH, D_MODEL, D_MODEL)
IN_WIDTH = sum(IN_SPLITS)

kernel_name = "hybrid_gated_sgu_swa_sink_block"


def rms_norm(x, g):
    xf = x.astype(jnp.float32)
    y = xf * lax.rsqrt(jnp.mean(xf * xf, axis=-1, keepdims=True) + EPS)
    return (y * g.astype(jnp.float32)).astype(x.dtype)


def layer_norm(x, g, b):
    xf = x.astype(jnp.float32)
    mu = jnp.mean(xf, axis=-1, keepdims=True)
    var = jnp.mean(jnp.square(xf - mu), axis=-1, keepdims=True)
    y = (xf - mu) * lax.rsqrt(var + EPS)
    return (y * g.astype(jnp.float32) + b.astype(jnp.float32)).astype(x.dtype)


def partial_rotary(x, positions):
    inv_freq = ROPE_THETA ** (-jnp.arange(0, ROPE_DIM, 2, dtype=jnp.float32) / ROPE_DIM)
    ang = positions.astype(jnp.float32)[..., None] * inv_freq
    cos = jnp.cos(ang)[:, :, None, :]
    sin = jnp.sin(ang)[:, :, None, :]
    xr = x[..., :ROPE_DIM].astype(jnp.float32)
    x1, x2 = xr[..., : ROPE_DIM // 2], xr[..., ROPE_DIM // 2:]
    rot = jnp.concatenate([x1 * cos - x2 * sin, x2 * cos + x1 * sin], axis=-1).astype(x.dtype)
    return jnp.concatenate([rot, x[..., ROPE_DIM:]], axis=-1)


def chunked_spatial_gating(u, v, ln_g, ln_b, w_s, b_s):
    B, S, _ = v.shape
    n_chunks = S // CHUNK
    vn = layer_norm(v, ln_g, ln_b)
    vc = vn.reshape(B, n_chunks, CHUNK, SGU_GROUPS, SGU_GROUP_DIM)
    causal = jnp.tril(jnp.ones((CHUNK, CHUNK), dtype=bool))
    w = jnp.where(causal[None], w_s, jnp.zeros_like(w_s))
    mixed = jnp.einsum('gts,bcsge->bctge', w, vc) + b_s.T[None, None, :, :, None]
    return u * mixed.reshape(B, S, SGU_WIDTH)


def sliding_window_sink_attention(q, k, v, sinks):
    B, S = q.shape[0], q.shape[1]
    n_blocks = S // BLOCK
    qb = q.reshape(B, n_blocks, BLOCK, N_KV_HEADS, Q_PER_KV, HEAD_DIM)
    kb = k.reshape(B, n_blocks, BLOCK, N_KV_HEADS, HEAD_DIM)
    vb = v.reshape(B, n_blocks, BLOCK, N_KV_HEADS, HEAD_DIM)
    pad = ((0, 0), (1, 0), (0, 0), (0, 0), (0, 0))
    k_band = jnp.concatenate([jnp.pad(kb, pad)[:, :-1], kb], axis=2)
    v_band = jnp.concatenate([jnp.pad(vb, pad)[:, :-1], vb], axis=2)
    scores = jnp.einsum('bnqgrd,bnkgd->bngrqk', qb, k_band).astype(jnp.float32) * (HEAD_DIM ** -0.5)
    blk = jnp.arange(n_blocks)[:, None]
    qpos = blk * BLOCK + jnp.arange(BLOCK)[None, :]
    kpos = (blk - 1) * BLOCK + jnp.arange(2 * BLOCK)[None, :]
    diff = qpos[:, :, None] - kpos[:, None, :]
    allowed = (diff >= 0) & (diff < WINDOW) & (kpos[:, None, :] >= 0)
    scores = jnp.where(allowed[None, :, None, None], scores, jnp.float32(-1e30))
    sink = sinks.astype(jnp.float32).reshape(N_KV_HEADS, Q_PER_KV)[None, None, :, :, None, None]
    m = jnp.maximum(jnp.max(scores, axis=-1, keepdims=True), sink)
    p = jnp.exp(scores - m)
    denom = jnp.sum(p, axis=-1, keepdims=True) + jnp.exp(sink - m)
    p = (p / denom).astype(v.dtype)
    out = jnp.einsum('bngrqk,bnkgd->bnqgrd', p, v_band)
    return out.reshape(B, S, ATTN_WIDTH)


def hybrid_layer(x, positions, w_in, ln_v_gain, ln_v_bias, w_spatial, b_spatial, sinks,
                 w_a, w_b, w_o, norm_mix_pre, norm_mix_post,
                 w_ff_in, w_ff_out, norm_ff_pre, norm_ff_post):
    B, S, _ = x.shape
    h = rms_norm(x, norm_mix_pre)
    proj = h @ w_in
    offs = np.cumsum(IN_SPLITS)[:-1].tolist()
    u, v_sgu, q, k, v_att, gate_a, gate_b = jnp.split(proj, offs, axis=-1)
    a = chunked_spatial_gating(jax.nn.gelu(u), jax.nn.gelu(v_sgu), ln_v_gain, ln_v_bias, w_spatial, b_spatial)
    q = partial_rotary(q.reshape(B, S, N_Q_HEADS, HEAD_DIM), positions)
    k = partial_rotary(k.reshape(B, S, N_KV_HEADS, HEAD_DIM), positions)
    v_att = v_att.reshape(B, S, N_KV_HEADS, HEAD_DIM)
    att = sliding_window_sink_attention(q, k, v_att, sinks)
    merged = jax.nn.sigmoid(gate_a) * (a @ w_a) + jax.nn.sigmoid(gate_b) * (att @ w_b)
    x = x + rms_norm(merged @ w_o, norm_mix_post)
    hf = rms_norm(x, norm_ff_pre)
    ff = jnp.square(jax.nn.relu(hf @ w_ff_in)) @ w_ff_out
    return x + rms_norm(ff, norm_ff_post)


def _fwd_setup_inputs(seed: int = 0) -> dict:
    key = jax.random.key(seed)
    ks = jax.random.split(key, 20)
    f32 = jnp.float32
    def nrm(k, shape, scale):
        return jax.random.normal(k, shape, dtype=f32) * scale
    def gain(k, shape):
        return 1.0 + 0.05 * jax.random.normal(k, shape, dtype=f32)
    x = jax.random.normal(ks[0], (BATCH, SEQ, D_MODEL), dtype=f32)
    offset = jax.random.randint(ks[1], (BATCH, 1), 0, 4096, dtype=jnp.int32)
    positions = offset + jnp.arange(SEQ, dtype=jnp.int32)[None, :]
    return {
        "x": x,
        "positions": positions,
        "w_in": nrm(ks[2], (DEPTH, D_MODEL, IN_WIDTH), D_MODEL ** -0.5),
        "ln_v_gain": gain(ks[3], (DEPTH, SGU_WIDTH)),
        "ln_v_bias": nrm(ks[4], (DEPTH, SGU_WIDTH), 0.02),
        "w_spatial": nrm(ks[5], (DEPTH, SGU_GROUPS, CHUNK, CHUNK), CHUNK ** -0.5),
        "b_spatial": gain(ks[6], (DEPTH, SGU_GROUPS, CHUNK)),
        "sinks": nrm(ks[7], (DEPTH, N_Q_HEADS), 0.5),
        "w_a": nrm(ks[8], (DEPTH, SGU_WIDTH, D_MODEL), SGU_WIDTH ** -0.5),
        "w_b": nrm(ks[9], (DEPTH, ATTN_WIDTH, D_MODEL), ATTN_WIDTH ** -0.5),
        "w_o": nrm(ks[10], (DEPTH, D_MODEL, D_MODEL), D_MODEL ** -0.5),
        "norm_mix_pre": gain(ks[11], (DEPTH, D_MODEL)),
        "norm_mix_post": gain(ks[12], (DEPTH, D_MODEL)),
        "w_ff_in": nrm(ks[13], (DEPTH, D_MODEL, D_FF), D_MODEL ** -0.5),
        "w_ff_out": nrm(ks[14], (DEPTH, D_FF, D_MODEL), D_FF ** -0.5),
        "norm_ff_pre": gain(ks[15], (DEPTH, D_MODEL)),
        "norm_ff_post": gain(ks[16], (DEPTH, D_MODEL)),
    }


def _fwd_reference(x, positions, w_in, ln_v_gain, ln_v_bias, w_spatial, b_spatial, sinks,
              w_a, w_b, w_o, norm_mix_pre, norm_mix_post,
              w_ff_in, w_ff_out, norm_ff_pre, norm_ff_post):
    for l in range(DEPTH):
        x = hybrid_layer(x, positions, w_in[l], ln_v_gain[l], ln_v_bias[l], w_spatial[l], b_spatial[l],
                         sinks[l], w_a[l], w_b[l], w_o[l], norm_mix_pre[l], norm_mix_post[l],
                         w_ff_in[l], w_ff_out[l], norm_ff_pre[l], norm_ff_post[l])
    return x


import jax as _jax
import jax.numpy as _jnp

TWIN_FORMAT = 'train_step'
FWD_PARAMS = ['x', 'positions', 'w_in', 'ln_v_gain', 'ln_v_bias', 'w_spatial', 'b_spatial', 'sinks', 'w_a', 'w_b', 'w_o', 'norm_mix_pre', 'norm_mix_post', 'w_ff_in', 'w_ff_out', 'norm_ff_pre', 'norm_ff_post']
TWIN_WEIGHTS = ['w_in', 'ln_v_gain', 'ln_v_bias', 'w_spatial', 'b_spatial', 'sinks', 'w_a', 'w_b', 'w_o', 'norm_mix_pre', 'norm_mix_post', 'w_ff_in', 'w_ff_out', 'norm_ff_pre', 'norm_ff_post']
TWIN_DIFF_INPUT = 'x'
TWIN_INPUTS = ['x', 'positions', 'w_in', 'ln_v_gain', 'ln_v_bias', 'w_spatial', 'b_spatial', 'sinks', 'w_a', 'w_b', 'w_o', 'norm_mix_pre', 'norm_mix_post', 'w_ff_in', 'w_ff_out', 'norm_ff_pre', 'norm_ff_post', 'loss_target', 'm_w_in', 'm_ln_v_gain', 'm_ln_v_bias', 'm_w_spatial', 'm_b_spatial', 'm_sinks', 'm_w_a', 'm_w_b', 'm_w_o', 'm_norm_mix_pre', 'm_norm_mix_post', 'm_w_ff_in', 'm_w_ff_out', 'm_norm_ff_pre', 'm_norm_ff_post', 'v_w_in', 'v_ln_v_gain', 'v_ln_v_bias', 'v_w_spatial', 'v_b_spatial', 'v_sinks', 'v_w_a', 'v_w_b', 'v_w_o', 'v_norm_mix_pre', 'v_norm_mix_post', 'v_w_ff_in', 'v_w_ff_out', 'v_norm_ff_pre', 'v_norm_ff_post']
TWIN_OUTPUTS = ['loss', 'grad_x', 'grad_w_in', 'grad_ln_v_gain', 'grad_ln_v_bias', 'grad_w_spatial', 'grad_b_spatial', 'grad_sinks', 'grad_w_a', 'grad_w_b', 'grad_w_o', 'grad_norm_mix_pre', 'grad_norm_mix_post', 'grad_w_ff_in', 'grad_w_ff_out', 'grad_norm_ff_pre', 'grad_norm_ff_post', 'delta_w_in', 'delta_ln_v_gain', 'delta_ln_v_bias', 'delta_w_spatial', 'delta_b_spatial', 'delta_sinks', 'delta_w_a', 'delta_w_b', 'delta_w_o', 'delta_norm_mix_pre', 'delta_norm_mix_post', 'delta_w_ff_in', 'delta_w_ff_out', 'delta_norm_ff_pre', 'delta_norm_ff_post', 'new_m_w_in', 'new_m_ln_v_gain', 'new_m_ln_v_bias', 'new_m_w_spatial', 'new_m_b_spatial', 'new_m_sinks', 'new_m_w_a', 'new_m_w_b', 'new_m_w_o', 'new_m_norm_mix_pre', 'new_m_norm_mix_post', 'new_m_w_ff_in', 'new_m_w_ff_out', 'new_m_norm_ff_pre', 'new_m_norm_ff_post', 'new_v_w_in', 'new_v_ln_v_gain', 'new_v_ln_v_bias', 'new_v_w_spatial', 'new_v_b_spatial', 'new_v_sinks', 'new_v_w_a', 'new_v_w_b', 'new_v_w_o', 'new_v_norm_mix_pre', 'new_v_norm_mix_post', 'new_v_w_ff_in', 'new_v_w_ff_out', 'new_v_norm_ff_pre', 'new_v_norm_ff_post']
TWIN_LEAF_KINDS = {'loss': 'loss', 'grad_x': 'grad_x', 'grad_w_in': 'grad_w', 'grad_ln_v_gain': 'grad_w', 'grad_ln_v_bias': 'grad_w', 'grad_w_spatial': 'grad_w', 'grad_b_spatial': 'grad_w', 'grad_sinks': 'grad_w', 'grad_w_a': 'grad_w', 'grad_w_b': 'grad_w', 'grad_w_o': 'grad_w', 'grad_norm_mix_pre': 'grad_w', 'grad_norm_mix_post': 'grad_w', 'grad_w_ff_in': 'grad_w', 'grad_w_ff_out': 'grad_w', 'grad_norm_ff_pre': 'grad_w', 'grad_norm_ff_post': 'grad_w', 'delta_w_in': 'delta_w', 'delta_ln_v_gain': 'delta_w', 'delta_ln_v_bias': 'delta_w', 'delta_w_spatial': 'delta_w', 'delta_b_spatial': 'delta_w', 'delta_sinks': 'delta_w', 'delta_w_a': 'delta_w', 'delta_w_b': 'delta_w', 'delta_w_o': 'delta_w', 'delta_norm_mix_pre': 'delta_w', 'delta_norm_mix_post': 'delta_w', 'delta_w_ff_in': 'delta_w', 'delta_w_ff_out': 'delta_w', 'delta_norm_ff_pre': 'delta_w', 'delta_norm_ff_post': 'delta_w', 'new_m_w_in': 'new_m', 'new_m_ln_v_gain': 'new_m', 'new_m_ln_v_bias': 'new_m', 'new_m_w_spatial': 'new_m', 'new_m_b_spatial': 'new_m', 'new_m_sinks': 'new_m', 'new_m_w_a': 'new_m', 'new_m_w_b': 'new_m', 'new_m_w_o': 'new_m', 'new_m_norm_mix_pre': 'new_m', 'new_m_norm_mix_post': 'new_m', 'new_m_w_ff_in': 'new_m', 'new_m_w_ff_out': 'new_m', 'new_m_norm_ff_pre': 'new_m', 'new_m_norm_ff_post': 'new_m', 'new_v_w_in': 'new_v', 'new_v_ln_v_gain': 'new_v', 'new_v_ln_v_bias': 'new_v', 'new_v_w_spatial': 'new_v', 'new_v_b_spatial': 'new_v', 'new_v_sinks': 'new_v', 'new_v_w_a': 'new_v', 'new_v_w_b': 'new_v', 'new_v_w_o': 'new_v', 'new_v_norm_mix_pre': 'new_v', 'new_v_norm_mix_post': 'new_v', 'new_v_w_ff_in': 'new_v', 'new_v_w_ff_out': 'new_v', 'new_v_norm_ff_pre': 'new_v', 'new_v_norm_ff_post': 'new_v'}


def _forward(args):
    return _fwd_reference(*[args[k] for k in FWD_PARAMS])


def _output_shape():
    def fwd():
        inp = _fwd_setup_inputs(0)
        return _fwd_reference(*[inp[k] for k in FWD_PARAMS])
    out = _jax.eval_shape(fwd)
    return out.shape, out.dtype

N_MICROBATCH = 1
ADAM_LR = 0.001
ADAM_B1 = 0.9
ADAM_B2 = 0.999
ADAM_EPS = 1e-08
ADAM_WD = 0.01
ADAM_STEP = 10
PER_EXAMPLE_BATCH_AXIS = {'x': 0, 'positions': 0, 'loss_target': 0}
SHARED_INPUTS = []
_WEIGHT_DTYPES = {'w_in': _jnp.float32, 'ln_v_gain': _jnp.float32, 'ln_v_bias': _jnp.float32, 'w_spatial': _jnp.float32, 'b_spatial': _jnp.float32, 'sinks': _jnp.float32, 'w_a': _jnp.float32, 'w_b': _jnp.float32, 'w_o': _jnp.float32, 'norm_mix_pre': _jnp.float32, 'norm_mix_post': _jnp.float32, 'w_ff_in': _jnp.float32, 'w_ff_out': _jnp.float32, 'norm_ff_pre': _jnp.float32, 'norm_ff_post': _jnp.float32}
MOMENT_SCALE = {'w_in': 5.060009e-01, 'ln_v_gain': 4.331684e-01, 'ln_v_bias': 5.292041e-01, 'w_spatial': 3.136731e-01, 'b_spatial': 5.449729e-01, 'sinks': 1.089691e-01, 'w_a': 9.398909e+00, 'w_b': 2.235494e-01, 'w_o': 9.706372e+00, 'norm_mix_pre': 1.246123e+00, 'norm_mix_post': 6.539125e+01, 'w_ff_in': 1.825947e+00, 'w_ff_out': 9.318762e+00, 'norm_ff_pre': 3.613616e+00, 'norm_ff_post': 6.712530e+01}


def _to_microbatches(a, axis):
    t = _jnp.moveaxis(a, axis, 0)
    t = t.reshape((N_MICROBATCH, t.shape[0] // N_MICROBATCH) + t.shape[1:])
    return _jnp.moveaxis(t, 1, axis + 1)


def setup_inputs(seed: int = 0) -> dict:
    inp = _fwd_setup_inputs(seed)
    key = _jax.random.fold_in(_jax.random.key(seed), 7919)
    shape, _ = _output_shape()
    out = dict(inp)
    out["loss_target"] = _jax.random.normal(_jax.random.fold_in(key, 0), shape, _jnp.float32)
    for i, name in enumerate(TWIN_WEIGHTS):
        w = inp[name].astype(_jnp.float32)
        if MOMENT_SCALE is None:
            s = _jnp.sqrt(_jnp.mean(_jnp.square(w)) + 1e-30)
        else:
            s = MOMENT_SCALE[name]
        km, kv = _jax.random.split(_jax.random.fold_in(key, i + 1))
        out[name] = w
        out["m_" + name] = s * _jax.random.normal(km, w.shape, _jnp.float32)
        out["v_" + name] = (s * s) * _jax.random.uniform(kv, w.shape, _jnp.float32, 0.5, 1.5)
    if N_MICROBATCH > 1:
        for name, axis in PER_EXAMPLE_BATCH_AXIS.items():
            out[name] = _to_microbatches(out[name], axis)
    return {'x': out['x'], 'positions': out['positions'], 'w_in': out['w_in'], 'ln_v_gain': out['ln_v_gain'], 'ln_v_bias': out['ln_v_bias'], 'w_spatial': out['w_spatial'], 'b_spatial': out['b_spatial'], 'sinks': out['sinks'], 'w_a': out['w_a'], 'w_b': out['w_b'], 'w_o': out['w_o'], 'norm_mix_pre': out['norm_mix_pre'], 'norm_mix_post': out['norm_mix_post'], 'w_ff_in': out['w_ff_in'], 'w_ff_out': out['w_ff_out'], 'norm_ff_pre': out['norm_ff_pre'], 'norm_ff_post': out['norm_ff_post'], 'loss_target': out['loss_target'], 'm_w_in': out['m_w_in'], 'm_ln_v_gain': out['m_ln_v_gain'], 'm_ln_v_bias': out['m_ln_v_bias'], 'm_w_spatial': out['m_w_spatial'], 'm_b_spatial': out['m_b_spatial'], 'm_sinks': out['m_sinks'], 'm_w_a': out['m_w_a'], 'm_w_b': out['m_w_b'], 'm_w_o': out['m_w_o'], 'm_norm_mix_pre': out['m_norm_mix_pre'], 'm_norm_mix_post': out['m_norm_mix_post'], 'm_w_ff_in': out['m_w_ff_in'], 'm_w_ff_out': out['m_w_ff_out'], 'm_norm_ff_pre': out['m_norm_ff_pre'], 'm_norm_ff_post': out['m_norm_ff_post'], 'v_w_in': out['v_w_in'], 'v_ln_v_gain': out['v_ln_v_gain'], 'v_ln_v_bias': out['v_ln_v_bias'], 'v_w_spatial': out['v_w_spatial'], 'v_b_spatial': out['v_b_spatial'], 'v_sinks': out['v_sinks'], 'v_w_a': out['v_w_a'], 'v_w_b': out['v_w_b'], 'v_w_o': out['v_w_o'], 'v_norm_mix_pre': out['v_norm_mix_pre'], 'v_norm_mix_post': out['v_norm_mix_post'], 'v_w_ff_in': out['v_w_ff_in'], 'v_w_ff_out': out['v_w_ff_out'], 'v_norm_ff_pre': out['v_norm_ff_pre'], 'v_norm_ff_post': out['v_norm_ff_post']}


def _loss(weights, diff, rest, loss_target):
    with _jax.named_scope("forward"):
        args = {**rest, TWIN_DIFF_INPUT: diff, **{k: w.astype(_WEIGHT_DTYPES[k]) for k, w in weights.items()}}
        y = _forward(args)
    with _jax.named_scope("loss_head"):
        err = _jnp.square(y.astype(_jnp.float32) - loss_target)
        return 0.5 * _jnp.sum(_jnp.mean(err, axis=-1)) if err.ndim else 0.5 * err


def _adamw(w, g, m, v):
    m = ADAM_B1 * m + (1.0 - ADAM_B1) * g
    v = ADAM_B2 * v + (1.0 - ADAM_B2) * _jnp.square(g)
    m_hat = m / (1.0 - ADAM_B1 ** ADAM_STEP)
    v_hat = v / (1.0 - ADAM_B2 ** ADAM_STEP)
    delta = -ADAM_LR * (m_hat / (_jnp.sqrt(v_hat) + ADAM_EPS) + ADAM_WD * w)
    return delta, m, v


def reference(x, positions, w_in, ln_v_gain, ln_v_bias, w_spatial, b_spatial, sinks, w_a, w_b, w_o, norm_mix_pre, norm_mix_post, w_ff_in, w_ff_out, norm_ff_pre, norm_ff_post, loss_target, m_w_in, m_ln_v_gain, m_ln_v_bias, m_w_spatial, m_b_spatial, m_sinks, m_w_a, m_w_b, m_w_o, m_norm_mix_pre, m_norm_mix_post, m_w_ff_in, m_w_ff_out, m_norm_ff_pre, m_norm_ff_post, v_w_in, v_ln_v_gain, v_ln_v_bias, v_w_spatial, v_b_spatial, v_sinks, v_w_a, v_w_b, v_w_o, v_norm_mix_pre, v_norm_mix_post, v_w_ff_in, v_w_ff_out, v_norm_ff_pre, v_norm_ff_post):
    given = dict(x=x, positions=positions, w_in=w_in, ln_v_gain=ln_v_gain, ln_v_bias=ln_v_bias, w_spatial=w_spatial, b_spatial=b_spatial, sinks=sinks, w_a=w_a, w_b=w_b, w_o=w_o, norm_mix_pre=norm_mix_pre, norm_mix_post=norm_mix_post, w_ff_in=w_ff_in, w_ff_out=w_ff_out, norm_ff_pre=norm_ff_pre, norm_ff_post=norm_ff_post, loss_target=loss_target, m_w_in=m_w_in, m_ln_v_gain=m_ln_v_gain, m_ln_v_bias=m_ln_v_bias, m_w_spatial=m_w_spatial, m_b_spatial=m_b_spatial, m_sinks=m_sinks, m_w_a=m_w_a, m_w_b=m_w_b, m_w_o=m_w_o, m_norm_mix_pre=m_norm_mix_pre, m_norm_mix_post=m_norm_mix_post, m_w_ff_in=m_w_ff_in, m_w_ff_out=m_w_ff_out, m_norm_ff_pre=m_norm_ff_pre, m_norm_ff_post=m_norm_ff_post, v_w_in=v_w_in, v_ln_v_gain=v_ln_v_gain, v_ln_v_bias=v_ln_v_bias, v_w_spatial=v_w_spatial, v_b_spatial=v_b_spatial, v_sinks=v_sinks, v_w_a=v_w_a, v_w_b=v_w_b, v_w_o=v_w_o, v_norm_mix_pre=v_norm_mix_pre, v_norm_mix_post=v_norm_mix_post, v_w_ff_in=v_w_ff_in, v_w_ff_out=v_w_ff_out, v_norm_ff_pre=v_norm_ff_pre, v_norm_ff_post=v_norm_ff_post)
    weights = {n: given[n] for n in TWIN_WEIGHTS}
    shared = {n: given[n] for n in SHARED_INPUTS}
    per_example = {n: given[n] for n in ['x', 'positions']}
    grad_fn = _jax.value_and_grad(_loss, argnums=(0, 1))

    def one_microbatch(ex, loss_target):
        ex = dict(ex)
        diff = ex.pop(TWIN_DIFF_INPUT)
        return grad_fn(weights, diff, {**shared, **ex}, loss_target)

    if N_MICROBATCH == 1:
        loss, (grad_w, grad_x) = one_microbatch(per_example, given["loss_target"])
    else:
        def body(carry, xs):
            loss_sum, grad_sum = carry
            l_k, (gw_k, gx_k) = one_microbatch(xs[0], xs[1])
            with _jax.named_scope("update"):
                return (loss_sum + l_k, _jax.tree.map(_jnp.add, grad_sum, gw_k)), gx_k

        init = (_jnp.zeros((), _jnp.float32), _jax.tree.map(_jnp.zeros_like, weights))
        (loss, grad_w), grad_x = _jax.lax.scan(body, init, (per_example, given["loss_target"]))
    with _jax.named_scope("update"):
        delta_w, new_m, new_v = {}, {}, {}
        for n in TWIN_WEIGHTS:
            delta_w[n], new_m[n], new_v[n] = _adamw(weights[n], grad_w[n], given["m_" + n], given["v_" + n])
    return (loss, grad_x, *[grad_w[n] for n in TWIN_WEIGHTS], *[delta_w[n] for n in TWIN_WEIGHTS],
            *[new_m[n] for n in TWIN_WEIGHTS], *[new_v[n] for n in TWIN_WEIGHTS])
```

```python
import functools

import jax
import jax.numpy as jnp
from jax import lax
from jax.experimental import pallas as pl
from jax.experimental.pallas import tpu as pltpu

F32 = jnp.float32
BF16 = jnp.bfloat16

N_DEV = 8
D = 1024
D_FF = 4096
IN_W = 5632
CHUNK = 128
GROUPS = 8
HEAD = 64
N_Q = 16
N_KV = 4
ROPE = 16
ROPE_THETA = 500000.0
EPS = 1e-6
OFF_Q, OFF_K, OFF_VA, OFF_GA, OFF_GB = 2048, 3072, 3328, 3584, 4608

ADAM_LR = 0.001
ADAM_B1 = 0.9
ADAM_B2 = 0.999
ADAM_EPS = 1e-08
ADAM_WD = 0.01
ADAM_STEP = 10

VMEM_LIMIT = 56 * 1024 * 1024

SDS = jax.ShapeDtypeStruct
MESH = pl.DeviceIdType.MESH


def _params(n_axes=None):
    if n_axes is None:
        return pltpu.CompilerParams(vmem_limit_bytes=VMEM_LIMIT)
    return pltpu.CompilerParams(dimension_semantics=("arbitrary",) * n_axes, vmem_limit_bytes=VMEM_LIMIT)


def _nt(a, b):
    return lax.dot_general(a, b, (((1,), (1,)), ((), ())), preferred_element_type=F32)


def _tn(a, b):
    return lax.dot_general(a, b, (((0,), (0,)), ((), ())), preferred_element_type=F32)


def _nn(a, b):
    return jnp.dot(a, b, preferred_element_type=F32)


def _gelu(x):
    t = jnp.tanh(0.7978845608028654 * (x + 0.044715 * (x * x * x)))
    return 0.5 * x * (1.0 + t), t


def _gelu_grad(x, t):
    return 0.5 * (1.0 + t) + 0.5 * x * (1.0 - t * t) * (0.7978845608028654 * (1.0 + 3.0 * 0.044715 * x * x))


def _sigmoid(x):
    return 1.0 / (1.0 + jnp.exp(-x))


def _rms_stats(v):
    r = lax.rsqrt(jnp.mean(v * v, axis=-1, keepdims=True) + EPS)
    return r, v * r


def _rms_bwd(d, vhat, r, g):
    gd = g * d
    return r * (gd - vhat * jnp.mean(gd * vhat, axis=-1, keepdims=True))


def _colsum(v):
    return jnp.sum(v, axis=0, keepdims=True)


def _fwd_in(x, g0, win):
    T = x.shape[0]
    tm, tn = min(T, 1024), 1408

    def body(x_ref, g_ref, w_ref, p_ref, h_ref):
        @pl.when(pl.program_id(1) == 0)
        def _():
            _, xh = _rms_stats(x_ref[...])
            h_ref[...] = (xh * g_ref[...]).astype(BF16)

        p_ref[...] = _nn(h_ref[...], w_ref[...]).astype(BF16)

    return pl.pallas_call(
        body, name="fwd_in", grid=(T // tm, IN_W // tn),
        in_specs=[pl.BlockSpec((tm, D), lambda i, j: (i, 0)), pl.BlockSpec((1, D), lambda i, j: (0, 0)),
                  pl.BlockSpec((D, tn), lambda i, j: (0, j))],
        out_specs=[pl.BlockSpec((tm, tn), lambda i, j: (i, j)), pl.BlockSpec((tm, D), lambda i, j: (i, 0))],
        out_shape=[SDS((T, IN_W), BF16), SDS((T, D), BF16)],
        compiler_params=_params(2),
    )(x, g0, win)


def _sgu_forward_parts(u_ref, vs_ref, lng_ref, lnb_ref):
    u = u_ref[...].astype(F32)
    vs = vs_ref[...].astype(F32)
    gu, tu = _gelu(u)
    gv, tv = _gelu(vs)
    mu = jnp.mean(gv, axis=-1, keepdims=True)
    dv = gv - mu
    rstd = lax.rsqrt(jnp.mean(dv * dv, axis=-1, keepdims=True) + EPS)
    vhat = dv * rstd
    vn = (vhat * lng_ref[...] + lnb_ref[...]).astype(BF16)
    return u, vs, gu, tu, tv, rstd, vhat, vn


def _masked_ws(ws_ref, g):
    row = lax.broadcasted_iota(jnp.int32, (CHUNK, CHUNK), 0)
    col = lax.broadcasted_iota(jnp.int32, (CHUNK, CHUNK), 1)
    return jnp.where(row >= col, ws_ref[g], 0.0).astype(BF16)


def _fwd_sgu(proj, lng, lnb, ws, bst):
    T = proj.shape[0]
    tc = min(T, 512)

    def body(u_ref, vs_ref, lng_ref, lnb_ref, ws_ref, bst_ref, a_ref):
        _, _, gu, _, _, _, _, vn = _sgu_forward_parts(u_ref, vs_ref, lng_ref, lnb_ref)
        for g in range(GROUPS):
            wm = _masked_ws(ws_ref, g)
            cols = slice(g * CHUNK, (g + 1) * CHUNK)
            for c in range(tc // CHUNK):
                rows = slice(c * CHUNK, (c + 1) * CHUNK)
                mixed = _nn(wm, vn[rows, cols]) + bst_ref[:, g:g + 1]
                a_ref[rows, cols] = (gu[rows, cols] * mixed).astype(BF16)

    return pl.pallas_call(
        body, name="fwd_sgu", grid=(T // tc,),
        in_specs=[pl.BlockSpec((tc, D), lambda i: (i, 0)), pl.BlockSpec((tc, D), lambda i: (i, 1)),
                  pl.BlockSpec((1, D), lambda i: (0, 0)), pl.BlockSpec((1, D), lambda i: (0, 0)),
                  pl.BlockSpec((GROUPS, CHUNK, CHUNK), lambda i: (0, 0, 0)), pl.BlockSpec((CHUNK, GROUPS), lambda i: (0, 0))],
        out_specs=pl.BlockSpec((tc, D), lambda i: (i, 0)),
        out_shape=SDS((T, D), BF16),
        compiler_params=_params(1),
    )(proj, proj, lng, lnb, ws, bst)


def _rope_tables(pos, invf, sgn, reps):
    ang = pos * invf
    c = jnp.cos(ang)
    s = jnp.sin(ang) * sgn
    if reps > 1:
        c = jnp.tile(c, (1, reps))
        s = jnp.tile(s, (1, reps))
    return c, s


def _swap_halves(v):
    n = v.shape[1]
    d = lax.broadcasted_iota(jnp.int32, v.shape, 1) % HEAD
    upper = jnp.where(d < ROPE, pltpu.roll(v, ROPE // 2, 1), 0.0)
    return jnp.where(d < ROPE // 2, pltpu.roll(v, n - ROPE // 2, 1), upper)


def _rope(v, c, s):
    return v * c + _swap_halves(v) * s


def _rope_bwd(dv, c, s):
    return dv * c + _swap_halves(dv * s)


def _band_mask(first):
    t = lax.broadcasted_iota(jnp.int32, (CHUNK, 2 * CHUNK), 0)
    j = lax.broadcasted_iota(jnp.int32, (CHUNK, 2 * CHUNK), 1)
    return (j > t) & (j <= t + CHUNK) & (jnp.logical_not(first) | (j >= CHUNK))


def _softmax_sink(s, sink):
    m = jnp.maximum(jnp.max(s, axis=-1, keepdims=True), sink)
    p = jnp.exp(s - m)
    esink = jnp.exp(sink - m)
    den = jnp.sum(p, axis=-1, keepdims=True) + esink
    return p / den, esink / den


def _attn_specs(nb, clamp):
    cur = (lambda i: jnp.minimum(i, nb - 1)) if clamp else (lambda i: i)
    prev = lambda i: jnp.maximum(jnp.minimum(i, nb - 1) - 1, 0)
    kw = N_KV * HEAD
    return cur, prev, [
        pl.BlockSpec((CHUNK, D), lambda i: (cur(i), OFF_Q // D)),
        pl.BlockSpec((CHUNK, kw), lambda i: (prev(i), OFF_K // kw)),
        pl.BlockSpec((CHUNK, kw), lambda i: (cur(i), OFF_K // kw)),
        pl.BlockSpec((CHUNK, kw), lambda i: (prev(i), OFF_VA // kw)),
        pl.BlockSpec((CHUNK, kw), lambda i: (cur(i), OFF_VA // kw)),
        pl.BlockSpec((CHUNK, 1), lambda i: (prev(i), 0)),
        pl.BlockSpec((CHUNK, 1), lambda i: (cur(i), 0)),
        pl.BlockSpec((1, 128), lambda i: (0, 0)),
        pl.BlockSpec((1, 128), lambda i: (0, 0)),
        pl.BlockSpec(memory_space=pltpu.SMEM),
    ]


def _attn_load(q_ref, kp_ref, kc_ref, vp_ref, vc_ref, pp_ref, pc_ref, invf_ref, sgn_ref):
    cq, sq = _rope_tables(pc_ref[...], invf_ref[...], sgn_ref[...], D // 128)
    pos_b = jnp.concatenate([pp_ref[...], pc_ref[...]], axis=0)
    ck, sk = _rope_tables(pos_b, invf_ref[...], sgn_ref[...], N_KV * HEAD // 128)
    q = _rope(q_ref[...].astype(F32), cq, sq).astype(BF16)
    kb = jnp.concatenate([kp_ref[...], kc_ref[...]], axis=0).astype(F32)
    kb = _rope(kb, ck, sk).astype(BF16)
    vb = jnp.concatenate([vp_ref[...], vc_ref[...]], axis=0)
    return q, kb, vb, (cq, sq, ck, sk)


def _fwd_attn(proj, posf, invf, sgn, sinks):
    T = proj.shape[0]
    nb = T // CHUNK
    _, _, specs = _attn_specs(nb, False)

    def body(q_ref, kp_ref, kc_ref, vp_ref, vc_ref, pp_ref, pc_ref, invf_ref, sgn_ref, sink_ref, o_ref):
        q, kb, vb, _ = _attn_load(q_ref, kp_ref, kc_ref, vp_ref, vc_ref, pp_ref, pc_ref, invf_ref, sgn_ref)
        mask = _band_mask(pl.program_id(0) == 0)
        for h in range(N_Q):
            g = h // (N_Q // N_KV)
            kg = kb[:, g * HEAD:(g + 1) * HEAD]
            vg = vb[:, g * HEAD:(g + 1) * HEAD]
            s = _nt(q[:, h * HEAD:(h + 1) * HEAD], kg) * (HEAD ** -0.5)
            s = jnp.where(mask, s, -1e30)
            p, _ = _softmax_sink(s, sink_ref[h])
            o_ref[:, h * HEAD:(h + 1) * HEAD] = _nn(p.astype(BF16), vg).astype(BF16)

    return pl.pallas_call(
        body, name="fwd_attn", grid=(nb,), in_specs=specs,
        out_specs=pl.BlockSpec((CHUNK, D), lambda i: (i, 0)),
        out_shape=SDS((T, D), BF16),
        compiler_params=_params(1),
    )(proj, proj, proj, proj, proj, posf, posf, invf, sgn, sinks)


def _fwd_mix(a, att, proj, x, wa, wb, wo, g1, g2):
    T = x.shape[0]
    tm = min(T, 256)
    half = D // 2

    def body(a_ref, att_ref, ga0, ga1, gb0, gb1, x_ref, wa_ref, wb_ref, wo_ref, g1_ref, g2_ref,
             mg_ref, a2_ref, b2_ref, mix_ref, x1_ref, hf_ref):
        a2 = _nn(a_ref[...], wa_ref[...])
        b2 = _nn(att_ref[...], wb_ref[...])
        ga = jnp.concatenate([ga0[...], ga1[...]], axis=1).astype(F32)
        gb = jnp.concatenate([gb0[...], gb1[...]], axis=1).astype(F32)
        merged = (_sigmoid(ga) * a2 + _sigmoid(gb) * b2).astype(BF16)
        a2_ref[...] = a2.astype(BF16)
        b2_ref[...] = b2.astype(BF16)
        mg_ref[...] = merged
        mix = _nn(merged, wo_ref[...])
        mix_ref[...] = mix
        _, mh = _rms_stats(mix)
        x1 = x_ref[...] + mh * g1_ref[...]
        x1_ref[...] = x1
        _, xh = _rms_stats(x1)
        hf_ref[...] = (xh * g2_ref[...]).astype(BF16)

    row = lambda i: (i, 0)
    const = lambda i: (0, 0)
    gspec = lambda off: pl.BlockSpec((tm, half), lambda i: (i, off // half))
    return pl.pallas_call(
        body, name="fwd_mix", grid=(T // tm,),
        in_specs=[pl.BlockSpec((tm, D), row), pl.BlockSpec((tm, D), row),
                  gspec(OFF_GA), gspec(OFF_GA + half), gspec(OFF_GB), gspec(OFF_GB + half),
                  pl.BlockSpec((tm, D), row), pl.BlockSpec((D, D), const), pl.BlockSpec((D, D), const),
                  pl.BlockSpec((D, D), const), pl.BlockSpec((1, D), const), pl.BlockSpec((1, D), const)],
        out_specs=[pl.BlockSpec((tm, D), row)] * 6,
        out_shape=[SDS((T, D), BF16), SDS((T, D), BF16), SDS((T, D), BF16), SDS((T, D), F32), SDS((T, D), F32),
                   SDS((T, D), BF16)],
        compiler_params=_params(1),
    )(a, att, proj, proj, proj, proj, x, wa, wb, wo, g1, g2)


FF_SPLIT = N_DEV
FF_TILE = D_FF // FF_SPLIT


def _fwd_ff(hf, wfi3, wfo, x1, tgt, g3):
    T = hf.shape[0]
    tm = min(T, 512)
    last = FF_SPLIT - 1

    def body(hf_ref, wfi_ref, wfo_ref, x1_ref, tgt_ref, g3_ref, f_ref, dy_ref, dff_ref, dg3_ref, loss_ref, acc):
        i, p = pl.program_id(0), pl.program_id(1)

        @pl.when((i == 0) & (p == 0))
        def _():
            dg3_ref[...] = jnp.zeros_like(dg3_ref)
            loss_ref[...] = jnp.zeros_like(loss_ref)

        f = _nn(hf_ref[...], wfi_ref[...]).astype(BF16)
        f_ref[...] = f
        rl = jnp.maximum(f.astype(F32), 0.0)
        part = _nn((rl * rl).astype(BF16), wfo_ref[...])

        @pl.when(p == 0)
        def _():
            acc[...] = part

        @pl.when(p > 0)
        def _():
            acc[...] += part

        @pl.when(p == last)
        def _():
            r3, fh = _rms_stats(acc[...])
            e = x1_ref[...] + fh * g3_ref[...] - tgt_ref[...]
            loss_ref[...] += jnp.sum(e * e) * (0.5 / D)
            dy = e * (1.0 / D)
            dy_ref[...] = dy
            dg3_ref[...] += _colsum(dy * fh)
            dff_ref[...] = _rms_bwd(dy, fh, r3, g3_ref[...]).astype(BF16)

    row = lambda i, p: (i, 0)
    const = lambda i, p: (0, 0)
    return pl.pallas_call(
        body, name="fwd_ff", grid=(T // tm, FF_SPLIT),
        in_specs=[pl.BlockSpec((tm, D), row), pl.BlockSpec((None, D, FF_TILE), lambda i, p: (p, 0, 0)),
                  pl.BlockSpec((FF_TILE, D), lambda i, p: (p, 0)), pl.BlockSpec((tm, D), row),
                  pl.BlockSpec((tm, D), row), pl.BlockSpec((1, D), const)],
        out_specs=[pl.BlockSpec((tm, FF_TILE), lambda i, p: (i, p)), pl.BlockSpec((tm, D), row),
                   pl.BlockSpec((tm, D), row), pl.BlockSpec((1, D), const), pl.BlockSpec((1, 128), const)],
        out_shape=[SDS((T, D_FF), BF16), SDS((T, D), F32), SDS((T, D), BF16), SDS((1, D), F32), SDS((1, 128), F32)],
        scratch_shapes=[pltpu.VMEM((tm, D), F32)],
        compiler_params=_params(2),
    )(hf, wfi3, wfo, x1, tgt, g3)


def _bwd_ff(dff, f, wfi3, wfo, x1, dy, mix, g1, g2):
    T = dff.shape[0]
    tm = min(T, 512)
    last = FF_SPLIT - 1

    def body(dff_ref, f_ref, wfi_ref, wfo_ref, x1_ref, dy_ref, mix_ref, g1_ref, g2_ref,
             df_ref, dx1_ref, dmix_ref, dg2_ref, dg1_ref, acc):
        i, p = pl.program_id(0), pl.program_id(1)

        @pl.when((i == 0) & (p == 0))
        def _():
            dg2_ref[...] = jnp.zeros_like(dg2_ref)
            dg1_ref[...] = jnp.zeros_like(dg1_ref)

        dr = _nt(dff_ref[...], wfo_ref[...])
        df = (dr * (2.0 * jnp.maximum(f_ref[...].astype(F32), 0.0))).astype(BF16)
        df_ref[...] = df
        part = _nt(df, wfi_ref[...])

        @pl.when(p == 0)
        def _():
            acc[...] = part

        @pl.when(p > 0)
        def _():
            acc[...] += part

        @pl.when(p == last)
        def _():
            dhf = acc[...]
            r2, xh = _rms_stats(x1_ref[...])
            dg2_ref[...] += _colsum(dhf * xh)
            dx1 = dy_ref[...] + _rms_bwd(dhf, xh, r2, g2_ref[...])
            dx1_ref[...] = dx1
            r1, mh = _rms_stats(mix_ref[...])
            dg1_ref[...] += _colsum(dx1 * mh)
            dmix_ref[...] = _rms_bwd(dx1, mh, r1, g1_ref[...]).astype(BF16)

    row = lambda i, p: (i, 0)
    const = lambda i, p: (0, 0)
    return pl.pallas_call(
        body, name="bwd_ff", grid=(T // tm, FF_SPLIT),
        in_specs=[pl.BlockSpec((tm, D), row), pl.BlockSpec((tm, FF_TILE), lambda i, p: (i, p)),
                  pl.BlockSpec((None, D, FF_TILE), lambda i, p: (p, 0, 0)), pl.BlockSpec((FF_TILE, D), lambda i, p: (p, 0)),
                  pl.BlockSpec((tm, D), row), pl.BlockSpec((tm, D), row), pl.BlockSpec((tm, D), row),
                  pl.BlockSpec((1, D), const), pl.BlockSpec((1, D), const)],
        out_specs=[pl.BlockSpec((tm, FF_TILE), lambda i, p: (i, p)), pl.BlockSpec((tm, D), row),
                   pl.BlockSpec((tm, D), row), pl.BlockSpec((1, D), const), pl.BlockSpec((1, D), const)],
        out_shape=[SDS((T, D_FF), BF16), SDS((T, D), F32), SDS((T, D), BF16), SDS((1, D), F32), SDS((1, D), F32)],
        scratch_shapes=[pltpu.VMEM((tm, D), F32)],
        compiler_params=_params(2),
    )(dff, f, wfi3, wfo, x1, dy, mix, g1, g2)


def _wgrad_ff(hf, df, f, dff):
    T = hf.shape[0]
    tt = min(T, 512)

    def body(hf_ref, df_ref, f_ref, dff_ref, dwfi_ref, dwfo_ref):
        @pl.when(pl.program_id(1) == 0)
        def _():
            dwfi_ref[...] = jnp.zeros_like(dwfi_ref)
            dwfo_ref[...] = jnp.zeros_like(dwfo_ref)

        dwfi_ref[...] += _tn(hf_ref[...], df_ref[...])
        rl = jnp.maximum(f_ref[...].astype(F32), 0.0)
        dwfo_ref[...] += _tn((rl * rl).astype(BF16), dff_ref[...])

    return pl.pallas_call(
        body, name="wgrad_ff", grid=(FF_SPLIT, T // tt),
        in_specs=[pl.BlockSpec((tt, D), lambda p, t: (t, 0)), pl.BlockSpec((tt, FF_TILE), lambda p, t: (t, p)),
                  pl.BlockSpec((tt, FF_TILE), lambda p, t: (t, p)), pl.BlockSpec((tt, D), lambda p, t: (t, 0))],
        out_specs=[pl.BlockSpec((None, D, FF_TILE), lambda p, t: (p, 0, 0)), pl.BlockSpec((FF_TILE, D), lambda p, t: (p, 0))],
        out_shape=[SDS((FF_SPLIT, D, FF_TILE), F32), SDS((D_FF, D), F32)],
        compiler_params=_params(2),
    )(hf, df, f, dff)


def _bwd_mix(dmix, proj, a2, b2, wo, wa, wb):
    T = dmix.shape[0]
    tm = min(T, 256)
    half = D // 2

    def body(dmix_ref, ga0, ga1, gb0, gb1, a2_ref, b2_ref, wo_ref, wa_ref, wb_ref,
             da2_ref, db2_ref, dg_ref, da_ref, datt_ref):
        dmg = _nt(dmix_ref[...], wo_ref[...])
        sa = _sigmoid(jnp.concatenate([ga0[...], ga1[...]], axis=1).astype(F32))
        sb = _sigmoid(jnp.concatenate([gb0[...], gb1[...]], axis=1).astype(F32))
        da2 = (dmg * sa).astype(BF16)
        db2 = (dmg * sb).astype(BF16)
        da2_ref[...] = da2
        db2_ref[...] = db2
        dg_ref[:, :D] = (dmg * a2_ref[...].astype(F32) * (sa * (1.0 - sa))).astype(BF16)
        dg_ref[:, D:] = (dmg * b2_ref[...].astype(F32) * (sb * (1.0 - sb))).astype(BF16)
        da_ref[...] = _nt(da2, wa_ref[...]).astype(BF16)
        datt_ref[...] = _nt(db2, wb_ref[...]).astype(BF16)

    row = lambda i: (i, 0)
    const = lambda i: (0, 0)
    gspec = lambda off: pl.BlockSpec((tm, half), lambda i: (i, off // half))
    return pl.pallas_call(
        body, name="bwd_mix", grid=(T // tm,),
        in_specs=[pl.BlockSpec((tm, D), row), gspec(OFF_GA), gspec(OFF_GA + half), gspec(OFF_GB), gspec(OFF_GB + half),
                  pl.BlockSpec((tm, D), row), pl.BlockSpec((tm, D), row),
                  pl.BlockSpec((D, D), const), pl.BlockSpec((D, D), const), pl.BlockSpec((D, D), const)],
        out_specs=[pl.BlockSpec((tm, D), row), pl.BlockSpec((tm, D), row), pl.BlockSpec((tm, 2 * D), row),
                   pl.BlockSpec((tm, D), row), pl.BlockSpec((tm, D), row)],
        out_shape=[SDS((T, D), BF16), SDS((T, D), BF16), SDS((T, 2 * D), BF16), SDS((T, D), BF16), SDS((T, D), BF16)],
        compiler_params=_params(1),
    )(dmix, proj, proj, proj, proj, a2, b2, wo, wa, wb)


def _wgrad_mix(merged, dmix, a, da2, att, db2):
    T = merged.shape[0]
    tt = min(T, 512)

    def body(mg_ref, dmix_ref, a_ref, da2_ref, att_ref, db2_ref, dwo_ref, dwa_ref, dwb_ref):
        @pl.when(pl.program_id(0) == 0)
        def _():
            dwo_ref[...] = jnp.zeros_like(dwo_ref)
            dwa_ref[...] = jnp.zeros_like(dwa_ref)
            dwb_ref[...] = jnp.zeros_like(dwb_ref)

        dwo_ref[...] += _tn(mg_ref[...], dmix_ref[...])
        dwa_ref[...] += _tn(a_ref[...], da2_ref[...])
        dwb_ref[...] += _tn(att_ref[...], db2_ref[...])

    return pl.pallas_call(
        body, name="wgrad_mix", grid=(T // tt,),
        in_specs=[pl.BlockSpec((tt, D), lambda t: (t, 0))] * 6,
        out_specs=[pl.BlockSpec((D, D), lambda t: (0, 0))] * 3,
        out_shape=[SDS((D, D), F32)] * 3,
        compiler_params=_params(1),
    )(merged, dmix, a, da2, att, db2)


def _bwd_attn(proj, posf, invf, sgn, sinks, datt):
    T = proj.shape[0]
    nb = T // CHUNK
    kw = N_KV * HEAD
    cur, prev, specs = _attn_specs(nb, True)

    def body(q_ref, kp_ref, kc_ref, vp_ref, vc_ref, pp_ref, pc_ref, invf_ref, sgn_ref, sink_ref, do_ref,
             dq_ref, dkv_ref, dsink_ref, carry_k, carry_v, dq_acc):
        i = pl.program_id(0)

        @pl.when(i == 0)
        def _():
            carry_k[...] = jnp.zeros_like(carry_k)
            carry_v[...] = jnp.zeros_like(carry_v)
            dsink_ref[...] = jnp.zeros_like(dsink_ref)

        @pl.when(i < nb)
        def _():
            q, kb, vb, (cq, sq, ck, sk) = _attn_load(q_ref, kp_ref, kc_ref, vp_ref, vc_ref, pp_ref, pc_ref,
                                                     invf_ref, sgn_ref)
            mask = _band_mask(i == 0)
            do = do_ref[...]
            lane = lax.broadcasted_iota(jnp.int32, (1, 128), 1)
            dsink = jnp.zeros((1, 128), F32)
            dks, dvs = [], []
            for g in range(N_KV):
                kg = kb[:, g * HEAD:(g + 1) * HEAD]
                vg = vb[:, g * HEAD:(g + 1) * HEAD]
                dk = jnp.zeros((2 * CHUNK, HEAD), F32)
                dv = jnp.zeros((2 * CHUNK, HEAD), F32)
                for r in range(N_Q // N_KV):
                    h = g * (N_Q // N_KV) + r
                    qh = q[:, h * HEAD:(h + 1) * HEAD]
                    doh = do[:, h * HEAD:(h + 1) * HEAD]
                    s = _nt(qh, kg) * (HEAD ** -0.5)
                    s = jnp.where(mask, s, -1e30)
                    p, psink = _softmax_sink(s, sink_ref[h])
                    dp = _nt(doh, vg)
                    delta = jnp.sum(p * dp, axis=-1, keepdims=True)
                    ds = (p * (dp - delta) * (HEAD ** -0.5)).astype(BF16)
                    dsink = dsink + jnp.where(lane == h, -jnp.sum(psink * delta), 0.0)
                    dq_acc[:, h * HEAD:(h + 1) * HEAD] = _nn(ds, kg)
                    dk = dk + _tn(ds, qh)
                    dv = dv + _tn(p.astype(BF16), doh)
                dks.append(dk)
                dvs.append(dv)
            dsink_ref[...] += dsink
            dq_ref[...] = _rope_bwd(dq_acc[...], cq, sq).astype(BF16)
            dkb = _rope_bwd(jnp.concatenate(dks, axis=1), ck, sk)
            dvb = jnp.concatenate(dvs, axis=1)
            dkv_ref[:, :kw] = (carry_k[...] + dkb[:CHUNK]).astype(BF16)
            dkv_ref[:, kw:] = (carry_v[...] + dvb[:CHUNK]).astype(BF16)
            carry_k[...] = dkb[CHUNK:]
            carry_v[...] = dvb[CHUNK:]

        @pl.when(i == nb)
        def _():
            dkv_ref[:, :kw] = carry_k[...].astype(BF16)
            dkv_ref[:, kw:] = carry_v[...].astype(BF16)

    return pl.pallas_call(
        body, name="bwd_attn", grid=(nb + 1,),
        in_specs=specs + [pl.BlockSpec((CHUNK, D), lambda i: (cur(i), 0))],
        out_specs=[pl.BlockSpec((CHUNK, D), lambda i: (cur(i), 0)),
                   pl.BlockSpec((CHUNK, 2 * kw), lambda i: (jnp.maximum(i - 1, 0), 0)),
                   pl.BlockSpec((1, 128), lambda i: (0, 0))],
        out_shape=[SDS((T, D), BF16), SDS((T, 2 * kw), BF16), SDS((1, 128), F32)],
        scratch_shapes=[pltpu.VMEM((CHUNK, kw), F32), pltpu.VMEM((CHUNK, kw), F32), pltpu.VMEM((CHUNK, D), F32)],
        compiler_params=_params(1),
    )(proj, proj, proj, proj, proj, posf, posf, invf, sgn, sinks, datt)


def _bwd_sgu(proj, da, lng, lnb, ws, bst):
    T = proj.shape[0]
    tc = min(T, 512)
    nsteps = T // tc

    def body(u_ref, vs_ref, da_ref, lng_ref, lnb_ref, ws_ref, bst_ref,
             duv_ref, dws_ref, dbs_ref, dlng_ref, dlnb_ref, dvn_s, dgu_s, dmx_sum):
        i = pl.program_id(0)

        @pl.when(i == 0)
        def _():
            dws_ref[...] = jnp.zeros_like(dws_ref)
            dlng_ref[...] = jnp.zeros_like(dlng_ref)
            dlnb_ref[...] = jnp.zeros_like(dlnb_ref)
            dmx_sum[...] = jnp.zeros_like(dmx_sum)

        u, vs, gu, tu, tv, rstd, vhat, vn = _sgu_forward_parts(u_ref, vs_ref, lng_ref, lnb_ref)
        da = da_ref[...].astype(F32)
        for g in range(GROUPS):
            wm = _masked_ws(ws_ref, g)
            cols = slice(g * CHUNK, (g + 1) * CHUNK)
            dws = jnp.zeros((CHUNK, CHUNK), F32)
            dsum = jnp.zeros((CHUNK, CHUNK), F32)
            for c in range(tc // CHUNK):
                rows = slice(c * CHUNK, (c + 1) * CHUNK)
                vn_cg = vn[rows, cols]
                mixed = _nn(wm, vn_cg) + bst_ref[:, g:g + 1]
                dgu_s[rows, cols] = da[rows, cols] * mixed
                dmx = da[rows, cols] * gu[rows, cols]
                dmxb = dmx.astype(BF16)
                dws = dws + _nt(dmxb, vn_cg)
                dsum = dsum + dmx
                dvn_s[rows, cols] = _tn(wm, dmxb)
            dws_ref[g] += dws
            dmx_sum[:, cols] += dsum
        dvn = dvn_s[...]
        dlng_ref[...] += _colsum(dvn * vhat)
        dlnb_ref[...] += _colsum(dvn)
        dvh = dvn * lng_ref[...]
        dgv = rstd * (dvh - jnp.mean(dvh, axis=-1, keepdims=True) - vhat * jnp.mean(dvh * vhat, axis=-1, keepdims=True))
        duv_ref[:, :D] = (dgu_s[...] * _gelu_grad(u, tu)).astype(BF16)
        duv_ref[:, D:] = (dgv * _gelu_grad(vs, tv)).astype(BF16)

        @pl.when(i == nsteps - 1)
        def _():
            row = lax.broadcasted_iota(jnp.int32, (CHUNK, CHUNK), 0)
            col = lax.broadcasted_iota(jnp.int32, (CHUNK, CHUNK), 1)
            for g in range(GROUPS):
                dws_ref[g] = jnp.where(row >= col, dws_ref[g], 0.0)
                dbs_ref[g:g + 1, :] = _colsum(dmx_sum[:, g * CHUNK:(g + 1) * CHUNK].T)

    const2 = lambda i: (0, 0)
    return pl.pallas_call(
        body, name="bwd_sgu", grid=(nsteps,),
        in_specs=[pl.BlockSpec((tc, D), lambda i: (i, 0)), pl.BlockSpec((tc, D), lambda i: (i, 1)),
                  pl.BlockSpec((tc, D), lambda i: (i, 0)), pl.BlockSpec((1, D), const2), pl.BlockSpec((1, D), const2),
                  pl.BlockSpec((GROUPS, CHUNK, CHUNK), lambda i: (0, 0, 0)), pl.BlockSpec((CHUNK, GROUPS), const2)],
        out_specs=[pl.BlockSpec((tc, 2 * D), lambda i: (i, 0)), pl.BlockSpec((GROUPS, CHUNK, CHUNK), lambda i: (0, 0, 0)),
                   pl.BlockSpec((GROUPS, CHUNK), const2), pl.BlockSpec((1, D), const2), pl.BlockSpec((1, D), const2)],
        out_shape=[SDS((T, 2 * D), BF16), SDS((GROUPS, CHUNK, CHUNK), F32), SDS((GROUPS, CHUNK), F32),
                   SDS((1, D), F32), SDS((1, D), F32)],
        scratch_shapes=[pltpu.VMEM((tc, D), F32), pltpu.VMEM((tc, D), F32), pltpu.VMEM((CHUNK, D), F32)],
        compiler_params=_params(1),
    )(proj, proj, da, lng, lnb, ws, bst)


IN_TILE = 512
IN_SEGS = ((0, 4), (4, 2), (6, 1), (7, 4))
IN_TILES = IN_W // IN_TILE


def _seg_specs(tm, row_of, tile_of):
    def spec(first, n):
        return pl.BlockSpec((tm, IN_TILE), lambda a, b: (row_of(a, b), jnp.clip(tile_of(a, b) - first, 0, n - 1)))
    return [spec(first, n) for first, n in IN_SEGS]


def _seg_dot(k, seg_refs, fn):
    for (first, n), ref in zip(IN_SEGS, seg_refs):
        @pl.when((k >= first) & (k < first + n))
        def _(ref=ref):
            fn(ref[...])


def _bwd_in(duv, dq, dkv, dg, win, x, dx1, g0):
    T = x.shape[0]
    tm = min(T, 512)

    def body(duv_ref, dq_ref, dkv_ref, dg_ref, w_ref, x_ref, dx1_ref, g0_ref, gx_ref, dg0_ref, acc):
        i, k = pl.program_id(0), pl.program_id(1)

        @pl.when((i == 0) & (k == 0))
        def _():
            dg0_ref[...] = jnp.zeros_like(dg0_ref)

        @pl.when(k == 0)
        def _():
            acc[...] = jnp.zeros_like(acc)

        def add(blk):
            acc[...] += _nt(blk, w_ref[...])

        _seg_dot(k, (duv_ref, dq_ref, dkv_ref, dg_ref), add)

        @pl.when(k == IN_TILES - 1)
        def _():
            dh = acc[...]
            r0, xh = _rms_stats(x_ref[...])
            dg0_ref[...] += _colsum(dh * xh)
            gx_ref[...] = dx1_ref[...] + _rms_bwd(dh, xh, r0, g0_ref[...])

    row = lambda i, k: (i, 0)
    const = lambda i, k: (0, 0)
    return pl.pallas_call(
        body, name="bwd_in", grid=(T // tm, IN_TILES),
        in_specs=_seg_specs(tm, lambda i, k: i, lambda i, k: k) + [pl.BlockSpec((D, IN_TILE), lambda i, k: (0, k)), pl.BlockSpec((tm, D), row),
                                           pl.BlockSpec((tm, D), row), pl.BlockSpec((1, D), const)],
        out_specs=[pl.BlockSpec((tm, D), row), pl.BlockSpec((1, D), const)],
        out_shape=[SDS((T, D), F32), SDS((1, D), F32)],
        scratch_shapes=[pltpu.VMEM((tm, D), F32)],
        compiler_params=_params(2),
    )(duv, dq, dkv, dg, win, x, dx1, g0)


def _wgrad_in(h, duv, dq, dkv, dg):
    T = h.shape[0]
    tt = min(T, 512)

    def body(h_ref, duv_ref, dq_ref, dkv_ref, dg_ref, dw_ref):
        n = pl.program_id(0)

        @pl.when(pl.program_id(1) == 0)
        def _():
            dw_ref[...] = jnp.zeros_like(dw_ref)

        def add(blk):
            dw_ref[...] += _tn(h_ref[...], blk)

        _seg_dot(n, (duv_ref, dq_ref, dkv_ref, dg_ref), add)

    return pl.pallas_call(
        body, name="wgrad_in", grid=(IN_TILES, T // tt),
        in_specs=[pl.BlockSpec((tt, D), lambda n, t: (t, 0))] + _seg_specs(tt, lambda n, t: t, lambda n, t: n),
        out_specs=pl.BlockSpec((D, IN_TILE), lambda n, t: (0, n)),
        out_shape=SDS((D, IN_W), F32),
        compiler_params=_params(2),
    )(h, duv, dq, dkv, dg)


def _place():
    x, y, c = lax.axis_index("x"), lax.axis_index("y"), lax.axis_index("c")
    return x, y, c, 4 * x + 2 * y + c


def _peers(x, y, c):
    out = []
    for mask in range(1, N_DEV):
        px = 1 - x if mask & 4 else x
        py = 1 - y if mask & 2 else y
        pc = 1 - c if mask & 1 else c
        out.append(((px, py, pc), 4 * px + 2 * py + pc))
    return out


def _all_to_all(arrays, gather, name):
    n = len(arrays)

    def body(*refs):
        ins, outs = refs[:n], refs[n:2 * n]
        send_sems, recv_sems, local_sems = refs[2 * n:]
        x, y, c, me = _place()
        local, sends, recvs = [], [], []
        for a in range(n):
            src_own = ins[a] if gather[a] else ins[a].at[me]
            local.append(pltpu.make_async_copy(src_own, outs[a].at[me], local_sems.at[a]))
            for k, (peer, pid) in enumerate(_peers(x, y, c)):
                sem = a * (N_DEV - 1) + k
                src = ins[a] if gather[a] else ins[a].at[pid]
                sends.append(pltpu.make_async_remote_copy(
                    src_ref=src, dst_ref=outs[a].at[me], send_sem=send_sems.at[sem], recv_sem=recv_sems.at[sem],
                    device_id=peer, device_id_type=MESH))
                recvs.append(pltpu.make_async_remote_copy(
                    src_ref=src, dst_ref=outs[a].at[pid], send_sem=send_sems.at[sem], recv_sem=recv_sems.at[sem],
                    device_id=peer, device_id_type=MESH))
        for cp in local + sends:
            cp.start()
        for cp in recvs:
            cp.wait_recv()
        for cp in sends:
            cp.wait_send()
        for cp in local:
            cp.wait()

    out_shape = [SDS((N_DEV,) + a.shape if gt else a.shape, a.dtype) for a, gt in zip(arrays, gather)]
    nsem = n * (N_DEV - 1)
    return pl.pallas_call(
        body, name=name,
        in_specs=[pl.BlockSpec(memory_space=pl.ANY)] * n,
        out_specs=[pl.BlockSpec(memory_space=pl.ANY)] * n,
        out_shape=out_shape,
        scratch_shapes=[pltpu.SemaphoreType.DMA((nsem,)), pltpu.SemaphoreType.DMA((nsem,)), pltpu.SemaphoreType.DMA((n,))],
    )(*arrays)


def _adamw_math(g, w, m, v):
    m2 = ADAM_B1 * m + (1.0 - ADAM_B1) * g
    v2 = ADAM_B2 * v + (1.0 - ADAM_B2) * (g * g)
    m_hat = m2 / (1.0 - ADAM_B1 ** ADAM_STEP)
    v_hat = v2 / (1.0 - ADAM_B2 ** ADAM_STEP)
    delta = -ADAM_LR * (m_hat / (jnp.sqrt(v_hat) + ADAM_EPS) + ADAM_WD * w)
    return delta, m2, v2


def _sum_adamw(parts, w, m, v, name):
    R, C = w.shape
    tr = max(t for t in (128, 64, 32, 16, 8) if R % t == 0)

    def body(p_ref, w_ref, m_ref, v_ref, g_ref, d_ref, m2_ref, v2_ref):
        g = p_ref[0]
        for k in range(1, N_DEV):
            g = g + p_ref[k]
        g_ref[...] = g
        d_ref[...], m2_ref[...], v2_ref[...] = _adamw_math(g, w_ref[...], m_ref[...], v_ref[...])

    blk = pl.BlockSpec((tr, C), lambda i: (i, 0))
    return pl.pallas_call(
        body, name=name, grid=(R // tr,),
        in_specs=[pl.BlockSpec((N_DEV, tr, C), lambda i: (0, i, 0)), blk, blk, blk],
        out_specs=[blk] * 4,
        out_shape=[SDS((R, C), F32)] * 4,
        compiler_params=_params(1),
    )(parts, w, m, v)


SMALL = ("ln_v_gain", "ln_v_bias", "w_spatial", "b_spatial", "sinks", "norm_mix_pre", "norm_mix_post", "norm_ff_pre",
         "norm_ff_post")
SMALL_ROWS = {"ln_v_gain": 8, "ln_v_bias": 8, "w_spatial": 1024, "b_spatial": 8, "sinks": 8, "norm_mix_pre": 8,
              "norm_mix_post": 8, "norm_ff_pre": 8, "norm_ff_post": 8}


def _pack_small(vals):
    rows = []
    for name in SMALL:
        flat = vals[name].reshape(-1)
        pad = SMALL_ROWS[name] * 128 - flat.shape[0]
        if pad:
            flat = jnp.concatenate([flat, jnp.zeros((pad,), F32)])
        rows.append(flat.reshape(SMALL_ROWS[name], 128))
    return jnp.concatenate(rows, axis=0)


def _unpack_small(packed, shapes):
    out, r = {}, 0
    for name in SMALL:
        n = 1
        for s in shapes[name]:
            n *= s
        out[name] = packed[r:r + SMALL_ROWS[name]].reshape(-1)[:n].reshape(shapes[name])
        r += SMALL_ROWS[name]
    return out


def _rope_rows():
    d = jnp.arange(128) % HEAD
    inv = ROPE_THETA ** (-(2.0 * (d % (ROPE // 2))).astype(F32) / ROPE)
    invf = jnp.where(d < ROPE, inv, 0.0).astype(F32).reshape(1, 128)
    sgn = jnp.where(d < ROPE // 2, -1.0, jnp.where(d < ROPE, 1.0, 0.0)).astype(F32).reshape(1, 128)
    return invf, sgn


def kernel(x, positions, w_in, ln_v_gain, ln_v_bias, w_spatial, b_spatial, sinks, w_a, w_b, w_o, norm_mix_pre, norm_mix_post, w_ff_in, w_ff_out, norm_ff_pre, norm_ff_post, loss_target, m_w_in, m_ln_v_gain, m_ln_v_bias, m_w_spatial, m_b_spatial, m_sinks, m_w_a, m_w_b, m_w_o, m_norm_mix_pre, m_norm_mix_post, m_w_ff_in, m_w_ff_out, m_norm_ff_pre, m_norm_ff_post, v_w_in, v_ln_v_gain, v_ln_v_bias, v_w_spatial, v_b_spatial, v_sinks, v_w_a, v_w_b, v_w_o, v_norm_mix_pre, v_norm_mix_post, v_w_ff_in, v_w_ff_out, v_norm_ff_pre, v_norm_ff_post):
    given = dict(locals())
    T = x.shape[1]
    xt = x[0]
    tgt = loss_target[0]
    posf = positions.astype(F32).reshape(T, 1)
    invf, sgn = _rope_rows()
    bst = b_spatial[0].T
    ws = w_spatial[0]

    big = ("w_in", "w_a", "w_b", "w_o", "w_ff_in", "w_ff_out")
    shards = [given[n][0].astype(BF16) for n in big]
    gw = dict(zip(big, _all_to_all(shards, [True] * len(big), "gather_weights")))
    win = jnp.transpose(gw["w_in"], (1, 0, 2)).reshape(D, IN_W)
    wa, wb, wo = (gw[n].reshape(D, D) for n in ("w_a", "w_b", "w_o"))
    wfi3 = gw["w_ff_in"]
    wfo = gw["w_ff_out"].reshape(D_FF, D)

    proj, h = _fwd_in(xt, norm_mix_pre, win)
    a = _fwd_sgu(proj, ln_v_gain, ln_v_bias, ws, bst)
    att = _fwd_attn(proj, posf, invf, sgn, sinks[0])
    merged, a2, b2, mix, x1, hf = _fwd_mix(a, att, proj, xt, wa, wb, wo, norm_mix_post, norm_ff_pre)
    f, dy, dff, dg3, loss_part = _fwd_ff(hf, wfi3, wfo, x1, tgt, norm_ff_post)

    df, dx1, dmix, dg2, dg1 = _bwd_ff(dff, f, wfi3, wfo, x1, dy, mix, norm_mix_post, norm_ff_pre)
    dwfi3, dwfo = _wgrad_ff(hf, df, f, dff)
    da2, db2, dgate, da, datt = _bwd_mix(dmix, proj, a2, b2, wo, wa, wb)
    dwo, dwa, dwb = _wgrad_mix(merged, dmix, a, da2, att, db2)
    dq, dkv, dsink = _bwd_attn(proj, posf, invf, sgn, sinks[0], datt)
    duv, dws, dbs, dlng, dlnb = _bwd_sgu(proj, da, ln_v_gain, ln_v_bias, ws, bst)
    grad_x, dg0 = _bwd_in(duv, dq, dkv, dgate, win, xt, dx1, norm_mix_pre)
    dwin = _wgrad_in(h, duv, dq, dkv, dgate)

    small_grads = {"ln_v_gain": dlng, "ln_v_bias": dlnb, "w_spatial": dws, "b_spatial": dbs, "sinks": dsink[:, :N_Q],
                   "norm_mix_pre": dg0, "norm_mix_post": dg1, "norm_ff_pre": dg2, "norm_ff_post": dg3}
    slabs = [jnp.transpose(dwin.reshape(D, N_DEV, IN_W // N_DEV), (1, 0, 2)),
             dwa.reshape(N_DEV, D // N_DEV, D), dwb.reshape(N_DEV, D // N_DEV, D), dwo.reshape(N_DEV, D // N_DEV, D),
             dwfi3, dwfo.reshape(N_DEV, D_FF // N_DEV, D), _pack_small(small_grads)]
    parts = _all_to_all(slabs, [False] * 6 + [True], "exchange_grads")

    results = {}
    for n, p in zip(big, parts[:6]):
        results[n] = [r.reshape(given[n].shape) for r in
                      _sum_adamw(p, given[n][0], given["m_" + n][0], given["v_" + n][0], "adamw_" + n)]
    packed = _sum_adamw(parts[6], _pack_small({n: given[n] for n in SMALL}), _pack_small({n: given["m_" + n] for n in SMALL}),
                        _pack_small({n: given["v_" + n] for n in SMALL}), "adamw_small")
    shapes = {n: given[n].shape for n in SMALL}
    unpacked = [_unpack_small(p, shapes) for p in packed]
    for n in SMALL:
        results[n] = [u[n] for u in unpacked]

    loss = lax.psum(loss_part[0, 0], ("x", "y", "c"))
    order = ("w_in", "ln_v_gain", "ln_v_bias", "w_spatial", "b_spatial", "sinks", "w_a", "w_b", "w_o", "norm_mix_pre",
             "norm_mix_post", "w_ff_in", "w_ff_out", "norm_ff_pre", "norm_ff_post")
    out = [loss, grad_x.reshape(x.shape)]
    for k in range(4):
        out += [results[n][k] for n in order]
    return tuple(out)
```

```python
import functools

import jax
import jax.numpy as jnp
from jax import lax
from jax.experimental import pallas as pl
from jax.experimental.pallas import tpu as pltpu

F32 = jnp.float32
BF16 = jnp.bfloat16

N_DEV = 8
D = 1024
D_FF = 4096
IN_W = 5632
CHUNK = 128
GROUPS = 8
HEAD = 64
N_Q = 16
N_KV = 4
ROPE = 16
ROPE_THETA = 500000.0
EPS = 1e-6
OFF_Q, OFF_K, OFF_VA, OFF_GA, OFF_GB = 2048, 3072, 3328, 3584, 4608

ADAM_LR = 0.001
ADAM_B1 = 0.9
ADAM_B2 = 0.999
ADAM_EPS = 1e-08
ADAM_WD = 0.01
ADAM_STEP = 10

VMEM_LIMIT = 56 * 1024 * 1024

SDS = jax.ShapeDtypeStruct
MESH = pl.DeviceIdType.MESH


def _params(n_axes=None):
    if n_axes is None:
        return pltpu.CompilerParams(vmem_limit_bytes=VMEM_LIMIT)
    return pltpu.CompilerParams(dimension_semantics=("arbitrary",) * n_axes, vmem_limit_bytes=VMEM_LIMIT)


def _nt(a, b):
    return lax.dot_general(a, b, (((1,), (1,)), ((), ())), preferred_element_type=F32)


def _tn(a, b):
    return lax.dot_general(a, b, (((0,), (0,)), ((), ())), preferred_element_type=F32)


def _nn(a, b):
    return jnp.dot(a, b, preferred_element_type=F32)


def _gelu(x):
    t = jnp.tanh(0.7978845608028654 * (x + 0.044715 * (x * x * x)))
    return 0.5 * x * (1.0 + t), t


def _gelu_grad(x, t):
    return 0.5 * (1.0 + t) + 0.5 * x * (1.0 - t * t) * (0.7978845608028654 * (1.0 + 3.0 * 0.044715 * x * x))


def _sigmoid(x):
    return 1.0 / (1.0 + jnp.exp(-x))


def _rms_stats(v):
    r = lax.rsqrt(jnp.mean(v * v, axis=-1, keepdims=True) + EPS)
    return r, v * r


def _rms_bwd(d, vhat, r, g):
    gd = g * d
    return r * (gd - vhat * jnp.mean(gd * vhat, axis=-1, keepdims=True))


def _colsum(v):
    return jnp.sum(v, axis=0, keepdims=True)


_ANY = pl.BlockSpec(memory_space=pl.ANY)


def _after(body, n_in, after):
    if after is None:
        return body, [], []

    def ordered(*refs):
        return body(*refs[:n_in], *refs[n_in + 1:])

    return ordered, [_ANY], [after]


def _fwd_in(x, g0, win):
    T = x.shape[0]
    tm, tn = min(T, 1024), 1408

    def body(x_ref, g_ref, w_ref, p_ref, h_ref):
        @pl.when(pl.program_id(1) == 0)
        def _():
            _, xh = _rms_stats(x_ref[...])
            h_ref[...] = (xh * g_ref[...]).astype(BF16)

        p_ref[...] = _nn(h_ref[...], w_ref[...]).astype(BF16)

    return pl.pallas_call(
        body, name="fwd_in", grid=(T // tm, IN_W // tn),
        in_specs=[pl.BlockSpec((tm, D), lambda i, j: (i, 0)), pl.BlockSpec((1, D), lambda i, j: (0, 0)),
                  pl.BlockSpec((D, tn), lambda i, j: (0, j))],
        out_specs=[pl.BlockSpec((tm, tn), lambda i, j: (i, j)), pl.BlockSpec((tm, D), lambda i, j: (i, 0))],
        out_shape=[SDS((T, IN_W), BF16), SDS((T, D), BF16)],
        compiler_params=_params(2),
    )(x, g0, win)


def _sgu_forward_parts(u_ref, vs_ref, lng_ref, lnb_ref):
    u = u_ref[...].astype(F32)
    vs = vs_ref[...].astype(F32)
    gu, tu = _gelu(u)
    gv, tv = _gelu(vs)
    mu = jnp.mean(gv, axis=-1, keepdims=True)
    dv = gv - mu
    rstd = lax.rsqrt(jnp.mean(dv * dv, axis=-1, keepdims=True) + EPS)
    vhat = dv * rstd
    vn = (vhat * lng_ref[...] + lnb_ref[...]).astype(BF16)
    return u, vs, gu, tu, tv, rstd, vhat, vn


def _masked_ws(ws_ref, g):
    row = lax.broadcasted_iota(jnp.int32, (CHUNK, CHUNK), 0)
    col = lax.broadcasted_iota(jnp.int32, (CHUNK, CHUNK), 1)
    return jnp.where(row >= col, ws_ref[g], 0.0).astype(BF16)


def _fwd_sgu(proj, lng, lnb, ws, bst):
    T = proj.shape[0]
    tc = min(T, 512)

    def body(u_ref, vs_ref, lng_ref, lnb_ref, ws_ref, bst_ref, a_ref):
        _, _, gu, _, _, _, _, vn = _sgu_forward_parts(u_ref, vs_ref, lng_ref, lnb_ref)
        for g in range(GROUPS):
            wm = _masked_ws(ws_ref, g)
            cols = slice(g * CHUNK, (g + 1) * CHUNK)
            for c in range(tc // CHUNK):
                rows = slice(c * CHUNK, (c + 1) * CHUNK)
                mixed = _nn(wm, vn[rows, cols]) + bst_ref[:, g:g + 1]
                a_ref[rows, cols] = (gu[rows, cols] * mixed).astype(BF16)

    return pl.pallas_call(
        body, name="fwd_sgu", grid=(T // tc,),
        in_specs=[pl.BlockSpec((tc, D), lambda i: (i, 0)), pl.BlockSpec((tc, D), lambda i: (i, 1)),
                  pl.BlockSpec((1, D), lambda i: (0, 0)), pl.BlockSpec((1, D), lambda i: (0, 0)),
                  pl.BlockSpec((GROUPS, CHUNK, CHUNK), lambda i: (0, 0, 0)), pl.BlockSpec((CHUNK, GROUPS), lambda i: (0, 0))],
        out_specs=pl.BlockSpec((tc, D), lambda i: (i, 0)),
        out_shape=SDS((T, D), BF16),
        compiler_params=_params(1),
    )(proj, proj, lng, lnb, ws, bst)


def _rope_tables(pos, invf, sgn, reps):
    ang = pos * invf
    c = jnp.cos(ang)
    s = jnp.sin(ang) * sgn
    if reps > 1:
        c = jnp.tile(c, (1, reps))
        s = jnp.tile(s, (1, reps))
    return c, s


def _swap_halves(v):
    n = v.shape[1]
    d = lax.broadcasted_iota(jnp.int32, v.shape, 1) % HEAD
    upper = jnp.where(d < ROPE, pltpu.roll(v, ROPE // 2, 1), 0.0)
    return jnp.where(d < ROPE // 2, pltpu.roll(v, n - ROPE // 2, 1), upper)


def _rope(v, c, s):
    return v * c + _swap_halves(v) * s


def _rope_bwd(dv, c, s):
    return dv * c + _swap_halves(dv * s)


def _band_mask(first):
    t = lax.broadcasted_iota(jnp.int32, (CHUNK, 2 * CHUNK), 0)
    j = lax.broadcasted_iota(jnp.int32, (CHUNK, 2 * CHUNK), 1)
    return (j > t) & (j <= t + CHUNK) & (jnp.logical_not(first) | (j >= CHUNK))


def _softmax_sink(s, sink):
    m = jnp.maximum(jnp.max(s, axis=-1, keepdims=True), sink)
    p = jnp.exp(s - m)
    esink = jnp.exp(sink - m)
    den = jnp.sum(p, axis=-1, keepdims=True) + esink
    return p / den, esink / den


def _attn_specs(nb, clamp):
    cur = (lambda i: jnp.minimum(i, nb - 1)) if clamp else (lambda i: i)
    prev = lambda i: jnp.maximum(jnp.minimum(i, nb - 1) - 1, 0)
    kw = N_KV * HEAD
    return cur, prev, [
        pl.BlockSpec((CHUNK, D), lambda i: (cur(i), OFF_Q // D)),
        pl.BlockSpec((CHUNK, kw), lambda i: (prev(i), OFF_K // kw)),
        pl.BlockSpec((CHUNK, kw), lambda i: (cur(i), OFF_K // kw)),
        pl.BlockSpec((CHUNK, kw), lambda i: (prev(i), OFF_VA // kw)),
        pl.BlockSpec((CHUNK, kw), lambda i: (cur(i), OFF_VA // kw)),
        pl.BlockSpec((CHUNK, 1), lambda i: (prev(i), 0)),
        pl.BlockSpec((CHUNK, 1), lambda i: (cur(i), 0)),
        pl.BlockSpec((1, 128), lambda i: (0, 0)),
        pl.BlockSpec((1, 128), lambda i: (0, 0)),
        pl.BlockSpec(memory_space=pltpu.SMEM),
    ]


def _attn_load(q_ref, kp_ref, kc_ref, vp_ref, vc_ref, pp_ref, pc_ref, invf_ref, sgn_ref):
    cq, sq = _rope_tables(pc_ref[...], invf_ref[...], sgn_ref[...], D // 128)
    pos_b = jnp.concatenate([pp_ref[...], pc_ref[...]], axis=0)
    ck, sk = _rope_tables(pos_b, invf_ref[...], sgn_ref[...], N_KV * HEAD // 128)
    q = _rope(q_ref[...].astype(F32), cq, sq).astype(BF16)
    kb = jnp.concatenate([kp_ref[...], kc_ref[...]], axis=0).astype(F32)
    kb = _rope(kb, ck, sk).astype(BF16)
    vb = jnp.concatenate([vp_ref[...], vc_ref[...]], axis=0)
    return q, kb, vb, (cq, sq, ck, sk)


def _fwd_attn(proj, posf, invf, sgn, sinks):
    T = proj.shape[0]
    nb = T // CHUNK
    _, _, specs = _attn_specs(nb, False)

    def body(q_ref, kp_ref, kc_ref, vp_ref, vc_ref, pp_ref, pc_ref, invf_ref, sgn_ref, sink_ref, o_ref):
        q, kb, vb, _ = _attn_load(q_ref, kp_ref, kc_ref, vp_ref, vc_ref, pp_ref, pc_ref, invf_ref, sgn_ref)
        mask = _band_mask(pl.program_id(0) == 0)
        for h in range(N_Q):
            g = h // (N_Q // N_KV)
            kg = kb[:, g * HEAD:(g + 1) * HEAD]
            vg = vb[:, g * HEAD:(g + 1) * HEAD]
            s = _nt(q[:, h * HEAD:(h + 1) * HEAD], kg) * (HEAD ** -0.5)
            s = jnp.where(mask, s, -1e30)
            p, _ = _softmax_sink(s, sink_ref[h])
            o_ref[:, h * HEAD:(h + 1) * HEAD] = _nn(p.astype(BF16), vg).astype(BF16)

    return pl.pallas_call(
        body, name="fwd_attn", grid=(nb,), in_specs=specs,
        out_specs=pl.BlockSpec((CHUNK, D), lambda i: (i, 0)),
        out_shape=SDS((T, D), BF16),
        compiler_params=_params(1),
    )(proj, proj, proj, proj, proj, posf, posf, invf, sgn, sinks)


def _fwd_mix(a, att, proj, x, wa, wb, wo, g1, g2):
    T = x.shape[0]
    tm = min(T, 256)
    half = D // 2

    def body(a_ref, att_ref, ga0, ga1, gb0, gb1, x_ref, wa_ref, wb_ref, wo_ref, g1_ref, g2_ref,
             mg_ref, a2_ref, b2_ref, mix_ref, x1_ref, hf_ref):
        a2 = _nn(a_ref[...], wa_ref[...])
        b2 = _nn(att_ref[...], wb_ref[...])
        ga = jnp.concatenate([ga0[...], ga1[...]], axis=1).astype(F32)
        gb = jnp.concatenate([gb0[...], gb1[...]], axis=1).astype(F32)
        merged = (_sigmoid(ga) * a2 + _sigmoid(gb) * b2).astype(BF16)
        a2_ref[...] = a2.astype(BF16)
        b2_ref[...] = b2.astype(BF16)
        mg_ref[...] = merged
        mix = _nn(merged, wo_ref[...])
        mix_ref[...] = mix
        _, mh = _rms_stats(mix)
        x1 = x_ref[...] + mh * g1_ref[...]
        x1_ref[...] = x1
        _, xh = _rms_stats(x1)
        hf_ref[...] = (xh * g2_ref[...]).astype(BF16)

    row = lambda i: (i, 0)
    const = lambda i: (0, 0)
    gspec = lambda off: pl.BlockSpec((tm, half), lambda i: (i, off // half))
    return pl.pallas_call(
        body, name="fwd_mix", grid=(T // tm,),
        in_specs=[pl.BlockSpec((tm, D), row), pl.BlockSpec((tm, D), row),
                  gspec(OFF_GA), gspec(OFF_GA + half), gspec(OFF_GB), gspec(OFF_GB + half),
                  pl.BlockSpec((tm, D), row), pl.BlockSpec((D, D), const), pl.BlockSpec((D, D), const),
                  pl.BlockSpec((D, D), const), pl.BlockSpec((1, D), const), pl.BlockSpec((1, D), const)],
        out_specs=[pl.BlockSpec((tm, D), row)] * 6,
        out_shape=[SDS((T, D), BF16), SDS((T, D), BF16), SDS((T, D), BF16), SDS((T, D), F32), SDS((T, D), F32),
                   SDS((T, D), BF16)],
        compiler_params=_params(1),
    )(a, att, proj, proj, proj, proj, x, wa, wb, wo, g1, g2)


FF_SPLIT = N_DEV
FF_TILE = D_FF // FF_SPLIT


def _fwd_ff(hf, wfi3, wfo, x1, tgt, g3):
    T = hf.shape[0]
    tm = min(T, 512)
    last = FF_SPLIT - 1

    def body(hf_ref, wfi_ref, wfo_ref, x1_ref, tgt_ref, g3_ref, f_ref, dy_ref, dff_ref, dg3_ref, loss_ref, acc):
        i, p = pl.program_id(0), pl.program_id(1)

        @pl.when((i == 0) & (p == 0))
        def _():
            dg3_ref[...] = jnp.zeros_like(dg3_ref)
            loss_ref[...] = jnp.zeros_like(loss_ref)

        f = _nn(hf_ref[...], wfi_ref[...]).astype(BF16)
        f_ref[...] = f
        rl = jnp.maximum(f.astype(F32), 0.0)
        part = _nn((rl * rl).astype(BF16), wfo_ref[...])

        @pl.when(p == 0)
        def _():
            acc[...] = part

        @pl.when(p > 0)
        def _():
            acc[...] += part

        @pl.when(p == last)
        def _():
            r3, fh = _rms_stats(acc[...])
            e = x1_ref[...] + fh * g3_ref[...] - tgt_ref[...]
            loss_ref[...] += jnp.sum(e * e) * (0.5 / D)
            dy = e * (1.0 / D)
            dy_ref[...] = dy
            dg3_ref[...] += _colsum(dy * fh)
            dff_ref[...] = _rms_bwd(dy, fh, r3, g3_ref[...]).astype(BF16)

    row = lambda i, p: (i, 0)
    const = lambda i, p: (0, 0)
    return pl.pallas_call(
        body, name="fwd_ff", grid=(T // tm, FF_SPLIT),
        in_specs=[pl.BlockSpec((tm, D), row), pl.BlockSpec((None, D, FF_TILE), lambda i, p: (p, 0, 0)),
                  pl.BlockSpec((FF_TILE, D), lambda i, p: (p, 0)), pl.BlockSpec((tm, D), row),
                  pl.BlockSpec((tm, D), row), pl.BlockSpec((1, D), const)],
        out_specs=[pl.BlockSpec((tm, FF_TILE), lambda i, p: (i, p)), pl.BlockSpec((tm, D), row),
                   pl.BlockSpec((tm, D), row), pl.BlockSpec((1, D), const), pl.BlockSpec((1, 128), const)],
        out_shape=[SDS((T, D_FF), BF16), SDS((T, D), F32), SDS((T, D), BF16), SDS((1, D), F32), SDS((1, 128), F32)],
        scratch_shapes=[pltpu.VMEM((tm, D), F32)],
        compiler_params=_params(2),
    )(hf, wfi3, wfo, x1, tgt, g3)


def _bwd_ff(dff, f, wfi3, wfo, x1, dy, mix, g1, g2):
    T = dff.shape[0]
    tm = min(T, 512)
    last = FF_SPLIT - 1

    def body(dff_ref, f_ref, wfi_ref, wfo_ref, x1_ref, dy_ref, mix_ref, g1_ref, g2_ref,
             df_ref, dx1_ref, dmix_ref, dg2_ref, dg1_ref, acc):
        i, p = pl.program_id(0), pl.program_id(1)

        @pl.when((i == 0) & (p == 0))
        def _():
            dg2_ref[...] = jnp.zeros_like(dg2_ref)
            dg1_ref[...] = jnp.zeros_like(dg1_ref)

        dr = _nt(dff_ref[...], wfo_ref[...])
        df = (dr * (2.0 * jnp.maximum(f_ref[...].astype(F32), 0.0))).astype(BF16)
        df_ref[...] = df
        part = _nt(df, wfi_ref[...])

        @pl.when(p == 0)
        def _():
            acc[...] = part

        @pl.when(p > 0)
        def _():
            acc[...] += part

        @pl.when(p == last)
        def _():
            dhf = acc[...]
            r2, xh = _rms_stats(x1_ref[...])
            dg2_ref[...] += _colsum(dhf * xh)
            dx1 = dy_ref[...] + _rms_bwd(dhf, xh, r2, g2_ref[...])
            dx1_ref[...] = dx1
            r1, mh = _rms_stats(mix_ref[...])
            dg1_ref[...] += _colsum(dx1 * mh)
            dmix_ref[...] = _rms_bwd(dx1, mh, r1, g1_ref[...]).astype(BF16)

    row = lambda i, p: (i, 0)
    const = lambda i, p: (0, 0)
    return pl.pallas_call(
        body, name="bwd_ff", grid=(T // tm, FF_SPLIT),
        in_specs=[pl.BlockSpec((tm, D), row), pl.BlockSpec((tm, FF_TILE), lambda i, p: (i, p)),
                  pl.BlockSpec((None, D, FF_TILE), lambda i, p: (p, 0, 0)), pl.BlockSpec((FF_TILE, D), lambda i, p: (p, 0)),
                  pl.BlockSpec((tm, D), row), pl.BlockSpec((tm, D), row), pl.BlockSpec((tm, D), row),
                  pl.BlockSpec((1, D), const), pl.BlockSpec((1, D), const)],
        out_specs=[pl.BlockSpec((tm, FF_TILE), lambda i, p: (i, p)), pl.BlockSpec((tm, D), row),
                   pl.BlockSpec((tm, D), row), pl.BlockSpec((1, D), const), pl.BlockSpec((1, D), const)],
        out_shape=[SDS((T, D_FF), BF16), SDS((T, D), F32), SDS((T, D), BF16), SDS((1, D), F32), SDS((1, D), F32)],
        scratch_shapes=[pltpu.VMEM((tm, D), F32)],
        compiler_params=_params(2),
    )(dff, f, wfi3, wfo, x1, dy, mix, g1, g2)


def _wgrad_ff(hf, df, f, dff):
    T = hf.shape[0]
    tt = min(T, 512)

    def body(hf_ref, df_ref, f_ref, dff_ref, dwfi_ref, dwfo_ref):
        @pl.when(pl.program_id(1) == 0)
        def _():
            dwfi_ref[...] = jnp.zeros_like(dwfi_ref)
            dwfo_ref[...] = jnp.zeros_like(dwfo_ref)

        dwfi_ref[...] += _tn(hf_ref[...], df_ref[...])
        rl = jnp.maximum(f_ref[...].astype(F32), 0.0)
        dwfo_ref[...] += _tn((rl * rl).astype(BF16), dff_ref[...])

    return pl.pallas_call(
        body, name="wgrad_ff", grid=(FF_SPLIT, T // tt),
        in_specs=[pl.BlockSpec((tt, D), lambda p, t: (t, 0)), pl.BlockSpec((tt, FF_TILE), lambda p, t: (t, p)),
                  pl.BlockSpec((tt, FF_TILE), lambda p, t: (t, p)), pl.BlockSpec((tt, D), lambda p, t: (t, 0))],
        out_specs=[pl.BlockSpec((None, D, FF_TILE), lambda p, t: (p, 0, 0)), pl.BlockSpec((FF_TILE, D), lambda p, t: (p, 0))],
        out_shape=[SDS((FF_SPLIT, D, FF_TILE), F32), SDS((D_FF, D), F32)],
        compiler_params=_params(2),
    )(hf, df, f, dff)


def _bwd_mix(dmix, proj, a2, b2, wo, wa, wb, after=None):
    T = dmix.shape[0]
    tm = min(T, 256)
    half = D // 2

    def body(dmix_ref, ga0, ga1, gb0, gb1, a2_ref, b2_ref, wo_ref, wa_ref, wb_ref,
             da2_ref, db2_ref, dg_ref, da_ref, datt_ref):
        dmg = _nt(dmix_ref[...], wo_ref[...])
        sa = _sigmoid(jnp.concatenate([ga0[...], ga1[...]], axis=1).astype(F32))
        sb = _sigmoid(jnp.concatenate([gb0[...], gb1[...]], axis=1).astype(F32))
        da2 = (dmg * sa).astype(BF16)
        db2 = (dmg * sb).astype(BF16)
        da2_ref[...] = da2
        db2_ref[...] = db2
        dg_ref[:, :D] = (dmg * a2_ref[...].astype(F32) * (sa * (1.0 - sa))).astype(BF16)
        dg_ref[:, D:] = (dmg * b2_ref[...].astype(F32) * (sb * (1.0 - sb))).astype(BF16)
        da_ref[...] = _nt(da2, wa_ref[...]).astype(BF16)
        datt_ref[...] = _nt(db2, wb_ref[...]).astype(BF16)

    row = lambda i: (i, 0)
    const = lambda i: (0, 0)
    gspec = lambda off: pl.BlockSpec((tm, half), lambda i: (i, off // half))
    body, dep_specs, deps = _after(body, 10, after)
    return pl.pallas_call(
        body, name="bwd_mix", grid=(T // tm,),
        in_specs=[pl.BlockSpec((tm, D), row), gspec(OFF_GA), gspec(OFF_GA + half), gspec(OFF_GB), gspec(OFF_GB + half),
                  pl.BlockSpec((tm, D), row), pl.BlockSpec((tm, D), row),
                  pl.BlockSpec((D, D), const), pl.BlockSpec((D, D), const), pl.BlockSpec((D, D), const)] + dep_specs,
        out_specs=[pl.BlockSpec((tm, D), row), pl.BlockSpec((tm, D), row), pl.BlockSpec((tm, 2 * D), row),
                   pl.BlockSpec((tm, D), row), pl.BlockSpec((tm, D), row)],
        out_shape=[SDS((T, D), BF16), SDS((T, D), BF16), SDS((T, 2 * D), BF16), SDS((T, D), BF16), SDS((T, D), BF16)],
        compiler_params=_params(1),
    )(dmix, proj, proj, proj, proj, a2, b2, wo, wa, wb, *deps)


def _wgrad_mix(merged, dmix, a, da2, att, db2):
    T = merged.shape[0]
    tt = min(T, 512)

    def body(mg_ref, dmix_ref, a_ref, da2_ref, att_ref, db2_ref, dwo_ref, dwa_ref, dwb_ref):
        @pl.when(pl.program_id(0) == 0)
        def _():
            dwo_ref[...] = jnp.zeros_like(dwo_ref)
            dwa_ref[...] = jnp.zeros_like(dwa_ref)
            dwb_ref[...] = jnp.zeros_like(dwb_ref)

        dwo_ref[...] += _tn(mg_ref[...], dmix_ref[...])
        dwa_ref[...] += _tn(a_ref[...], da2_ref[...])
        dwb_ref[...] += _tn(att_ref[...], db2_ref[...])

    return pl.pallas_call(
        body, name="wgrad_mix", grid=(T // tt,),
        in_specs=[pl.BlockSpec((tt, D), lambda t: (t, 0))] * 6,
        out_specs=[pl.BlockSpec((D, D), lambda t: (0, 0))] * 3,
        out_shape=[SDS((D, D), F32)] * 3,
        compiler_params=_params(1),
    )(merged, dmix, a, da2, att, db2)


def _bwd_attn(proj, posf, invf, sgn, sinks, datt, after=None):
    T = proj.shape[0]
    nb = T // CHUNK
    kw = N_KV * HEAD
    cur, prev, specs = _attn_specs(nb, True)

    def body(q_ref, kp_ref, kc_ref, vp_ref, vc_ref, pp_ref, pc_ref, invf_ref, sgn_ref, sink_ref, do_ref,
             dq_ref, dkv_ref, dsink_ref, carry_k, carry_v, dq_acc):
        i = pl.program_id(0)

        @pl.when(i == 0)
        def _():
            carry_k[...] = jnp.zeros_like(carry_k)
            carry_v[...] = jnp.zeros_like(carry_v)
            dsink_ref[...] = jnp.zeros_like(dsink_ref)

        @pl.when(i < nb)
        def _():
            q, kb, vb, (cq, sq, ck, sk) = _attn_load(q_ref, kp_ref, kc_ref, vp_ref, vc_ref, pp_ref, pc_ref,
                                                     invf_ref, sgn_ref)
            mask = _band_mask(i == 0)
            do = do_ref[...]
            lane = lax.broadcasted_iota(jnp.int32, (1, 128), 1)
            dsink = jnp.zeros((1, 128), F32)
            dks, dvs = [], []
            for g in range(N_KV):
                kg = kb[:, g * HEAD:(g + 1) * HEAD]
                vg = vb[:, g * HEAD:(g + 1) * HEAD]
                dk = jnp.zeros((2 * CHUNK, HEAD), F32)
                dv = jnp.zeros((2 * CHUNK, HEAD), F32)
                for r in range(N_Q // N_KV):
                    h = g * (N_Q // N_KV) + r
                    qh = q[:, h * HEAD:(h + 1) * HEAD]
                    doh = do[:, h * HEAD:(h + 1) * HEAD]
                    s = _nt(qh, kg) * (HEAD ** -0.5)
                    s = jnp.where(mask, s, -1e30)
                    p, psink = _softmax_sink(s, sink_ref[h])
                    dp = _nt(doh, vg)
                    delta = jnp.sum(p * dp, axis=-1, keepdims=True)
                    ds = (p * (dp - delta) * (HEAD ** -0.5)).astype(BF16)
                    dsink = dsink + jnp.where(lane == h, -jnp.sum(psink * delta), 0.0)
                    dq_acc[:, h * HEAD:(h + 1) * HEAD] = _nn(ds, kg)
                    dk = dk + _tn(ds, qh)
                    dv = dv + _tn(p.astype(BF16), doh)
                dks.append(dk)
                dvs.append(dv)
            dsink_ref[...] += dsink
            dq_ref[...] = _rope_bwd(dq_acc[...], cq, sq).astype(BF16)
            dkb = _rope_bwd(jnp.concatenate(dks, axis=1), ck, sk)
            dvb = jnp.concatenate(dvs, axis=1)
            dkv_ref[:, :kw] = (carry_k[...] + dkb[:CHUNK]).astype(BF16)
            dkv_ref[:, kw:] = (carry_v[...] + dvb[:CHUNK]).astype(BF16)
            carry_k[...] = dkb[CHUNK:]
            carry_v[...] = dvb[CHUNK:]

        @pl.when(i == nb)
        def _():
            dkv_ref[:, :kw] = carry_k[...].astype(BF16)
            dkv_ref[:, kw:] = carry_v[...].astype(BF16)

    body, dep_specs, deps = _after(body, 11, after)
    return pl.pallas_call(
        body, name="bwd_attn", grid=(nb + 1,),
        in_specs=specs + [pl.BlockSpec((CHUNK, D), lambda i: (cur(i), 0))] + dep_specs,
        out_specs=[pl.BlockSpec((CHUNK, D), lambda i: (cur(i), 0)),
                   pl.BlockSpec((CHUNK, 2 * kw), lambda i: (jnp.maximum(i - 1, 0), 0)),
                   pl.BlockSpec((1, 128), lambda i: (0, 0))],
        out_shape=[SDS((T, D), BF16), SDS((T, 2 * kw), BF16), SDS((1, 128), F32)],
        scratch_shapes=[pltpu.VMEM((CHUNK, kw), F32), pltpu.VMEM((CHUNK, kw), F32), pltpu.VMEM((CHUNK, D), F32)],
        compiler_params=_params(1),
    )(proj, proj, proj, proj, proj, posf, posf, invf, sgn, sinks, datt, *deps)


def _bwd_sgu(proj, da, lng, lnb, ws, bst):
    T = proj.shape[0]
    tc = min(T, 512)
    nsteps = T // tc

    def body(u_ref, vs_ref, da_ref, lng_ref, lnb_ref, ws_ref, bst_ref,
             duv_ref, dws_ref, dbs_ref, dlng_ref, dlnb_ref, dvn_s, dgu_s, dmx_sum):
        i = pl.program_id(0)

        @pl.when(i == 0)
        def _():
            dws_ref[...] = jnp.zeros_like(dws_ref)
            dlng_ref[...] = jnp.zeros_like(dlng_ref)
            dlnb_ref[...] = jnp.zeros_like(dlnb_ref)
            dmx_sum[...] = jnp.zeros_like(dmx_sum)

        u, vs, gu, tu, tv, rstd, vhat, vn = _sgu_forward_parts(u_ref, vs_ref, lng_ref, lnb_ref)
        da = da_ref[...].astype(F32)
        for g in range(GROUPS):
            wm = _masked_ws(ws_ref, g)
            cols = slice(g * CHUNK, (g + 1) * CHUNK)
            dws = jnp.zeros((CHUNK, CHUNK), F32)
            dsum = jnp.zeros((CHUNK, CHUNK), F32)
            for c in range(tc // CHUNK):
                rows = slice(c * CHUNK, (c + 1) * CHUNK)
                vn_cg = vn[rows, cols]
                mixed = _nn(wm, vn_cg) + bst_ref[:, g:g + 1]
                dgu_s[rows, cols] = da[rows, cols] * mixed
                dmx = da[rows, cols] * gu[rows, cols]
                dmxb = dmx.astype(BF16)
                dws = dws + _nt(dmxb, vn_cg)
                dsum = dsum + dmx
                dvn_s[rows, cols] = _tn(wm, dmxb)
            dws_ref[g] += dws
            dmx_sum[:, cols] += dsum
        dvn = dvn_s[...]
        dlng_ref[...] += _colsum(dvn * vhat)
        dlnb_ref[...] += _colsum(dvn)
        dvh = dvn * lng_ref[...]
        dgv = rstd * (dvh - jnp.mean(dvh, axis=-1, keepdims=True) - vhat * jnp.mean(dvh * vhat, axis=-1, keepdims=True))
        duv_ref[:, :D] = (dgu_s[...] * _gelu_grad(u, tu)).astype(BF16)
        duv_ref[:, D:] = (dgv * _gelu_grad(vs, tv)).astype(BF16)

        @pl.when(i == nsteps - 1)
        def _():
            row = lax.broadcasted_iota(jnp.int32, (CHUNK, CHUNK), 0)
            col = lax.broadcasted_iota(jnp.int32, (CHUNK, CHUNK), 1)
            for g in range(GROUPS):
                dws_ref[g] = jnp.where(row >= col, dws_ref[g], 0.0)
                dbs_ref[g:g + 1, :] = _colsum(dmx_sum[:, g * CHUNK:(g + 1) * CHUNK].T)

    const2 = lambda i: (0, 0)
    return pl.pallas_call(
        body, name="bwd_sgu", grid=(nsteps,),
        in_specs=[pl.BlockSpec((tc, D), lambda i: (i, 0)), pl.BlockSpec((tc, D), lambda i: (i, 1)),
                  pl.BlockSpec((tc, D), lambda i: (i, 0)), pl.BlockSpec((1, D), const2), pl.BlockSpec((1, D), const2),
                  pl.BlockSpec((GROUPS, CHUNK, CHUNK), lambda i: (0, 0, 0)), pl.BlockSpec((CHUNK, GROUPS), const2)],
        out_specs=[pl.BlockSpec((tc, 2 * D), lambda i: (i, 0)), pl.BlockSpec((GROUPS, CHUNK, CHUNK), lambda i: (0, 0, 0)),
                   pl.BlockSpec((GROUPS, CHUNK), const2), pl.BlockSpec((1, D), const2), pl.BlockSpec((1, D), const2)],
        out_shape=[SDS((T, 2 * D), BF16), SDS((GROUPS, CHUNK, CHUNK), F32), SDS((GROUPS, CHUNK), F32),
                   SDS((1, D), F32), SDS((1, D), F32)],
        scratch_shapes=[pltpu.VMEM((tc, D), F32), pltpu.VMEM((tc, D), F32), pltpu.VMEM((CHUNK, D), F32)],
        compiler_params=_params(1),
    )(proj, proj, da, lng, lnb, ws, bst)


IN_TILE = 512
IN_SEGS = ((0, 4), (4, 2), (6, 1), (7, 4))
IN_TILES = IN_W // IN_TILE


def _seg_specs(tm, row_of, tile_of):
    def spec(first, n):
        return pl.BlockSpec((tm, IN_TILE), lambda a, b: (row_of(a, b), jnp.clip(tile_of(a, b) - first, 0, n - 1)))
    return [spec(first, n) for first, n in IN_SEGS]


def _seg_dot(k, seg_refs, fn):
    for (first, n), ref in zip(IN_SEGS, seg_refs):
        @pl.when((k >= first) & (k < first + n))
        def _(ref=ref):
            fn(ref[...])


def _bwd_in(duv, dq, dkv, dg, win, x, dx1, g0, after=None):
    T = x.shape[0]
    tm = min(T, 512)

    def body(duv_ref, dq_ref, dkv_ref, dg_ref, w_ref, x_ref, dx1_ref, g0_ref, gx_ref, dg0_ref, acc):
        i, k = pl.program_id(0), pl.program_id(1)

        @pl.when((i == 0) & (k == 0))
        def _():
            dg0_ref[...] = jnp.zeros_like(dg0_ref)

        @pl.when(k == 0)
        def _():
            acc[...] = jnp.zeros_like(acc)

        def add(blk):
            acc[...] += _nt(blk, w_ref[...])

        _seg_dot(k, (duv_ref, dq_ref, dkv_ref, dg_ref), add)

        @pl.when(k == IN_TILES - 1)
        def _():
            dh = acc[...]
            r0, xh = _rms_stats(x_ref[...])
            dg0_ref[...] += _colsum(dh * xh)
            gx_ref[...] = dx1_ref[...] + _rms_bwd(dh, xh, r0, g0_ref[...])

    row = lambda i, k: (i, 0)
    const = lambda i, k: (0, 0)
    body, dep_specs, deps = _after(body, 8, after)
    return pl.pallas_call(
        body, name="bwd_in", grid=(T // tm, IN_TILES),
        in_specs=_seg_specs(tm, lambda i, k: i, lambda i, k: k) + [pl.BlockSpec((D, IN_TILE), lambda i, k: (0, k)), pl.BlockSpec((tm, D), row),
                                           pl.BlockSpec((tm, D), row), pl.BlockSpec((1, D), const)] + dep_specs,
        out_specs=[pl.BlockSpec((tm, D), row), pl.BlockSpec((1, D), const)],
        out_shape=[SDS((T, D), F32), SDS((1, D), F32)],
        scratch_shapes=[pltpu.VMEM((tm, D), F32)],
        compiler_params=_params(2),
    )(duv, dq, dkv, dg, win, x, dx1, g0, *deps)


def _wgrad_in(h, duv, dq, dkv, dg):
    T = h.shape[0]
    tt = min(T, 512)

    def body(h_ref, duv_ref, dq_ref, dkv_ref, dg_ref, dw_ref):
        n = pl.program_id(0)

        @pl.when(pl.program_id(1) == 0)
        def _():
            dw_ref[...] = jnp.zeros_like(dw_ref)

        def add(blk):
            dw_ref[...] += _tn(h_ref[...], blk)

        _seg_dot(n, (duv_ref, dq_ref, dkv_ref, dg_ref), add)

    return pl.pallas_call(
        body, name="wgrad_in", grid=(IN_TILES, T // tt),
        in_specs=[pl.BlockSpec((tt, D), lambda n, t: (t, 0))] + _seg_specs(tt, lambda n, t: t, lambda n, t: n),
        out_specs=pl.BlockSpec((D, IN_TILE), lambda n, t: (0, n)),
        out_shape=SDS((D, IN_W), F32),
        compiler_params=_params(2),
    )(h, duv, dq, dkv, dg)


def _place():
    x, y, c = lax.axis_index("x"), lax.axis_index("y"), lax.axis_index("c")
    return x, y, c, 4 * x + 2 * y + c


def _peers(x, y, c):
    out = []
    for mask in range(1, N_DEV):
        px = 1 - x if mask & 4 else x
        py = 1 - y if mask & 2 else y
        pc = 1 - c if mask & 1 else c
        out.append(((px, py, pc), 4 * px + 2 * py + pc))
    return out


def _all_to_all(arrays, gather, name):
    n = len(arrays)

    def body(*refs):
        ins, outs = refs[:n], refs[n:2 * n]
        send_sems, recv_sems, local_sems = refs[2 * n:]
        x, y, c, me = _place()
        local, sends, recvs = [], [], []
        for a in range(n):
            src_own = ins[a] if gather[a] else ins[a].at[me]
            local.append(pltpu.make_async_copy(src_own, outs[a].at[me], local_sems.at[a]))
            for k, (peer, pid) in enumerate(_peers(x, y, c)):
                sem = a * (N_DEV - 1) + k
                src = ins[a] if gather[a] else ins[a].at[pid]
                sends.append(pltpu.make_async_remote_copy(
                    src_ref=src, dst_ref=outs[a].at[me], send_sem=send_sems.at[sem], recv_sem=recv_sems.at[sem],
                    device_id=peer, device_id_type=MESH))
                recvs.append(pltpu.make_async_remote_copy(
                    src_ref=src, dst_ref=outs[a].at[pid], send_sem=send_sems.at[sem], recv_sem=recv_sems.at[sem],
                    device_id=peer, device_id_type=MESH))
        for cp in local + sends:
            cp.start()
        for cp in recvs:
            cp.wait_recv()
        for cp in sends:
            cp.wait_send()
        for cp in local:
            cp.wait()

    out_shape = [SDS((N_DEV,) + a.shape if gt else a.shape, a.dtype) for a, gt in zip(arrays, gather)]
    nsem = n * (N_DEV - 1)
    return pl.pallas_call(
        body, name=name,
        in_specs=[pl.BlockSpec(memory_space=pl.ANY)] * n,
        out_specs=[pl.BlockSpec(memory_space=pl.ANY)] * n,
        out_shape=out_shape,
        scratch_shapes=[pltpu.SemaphoreType.DMA((nsem,)), pltpu.SemaphoreType.DMA((nsem,)), pltpu.SemaphoreType.DMA((n,))],
    )(*arrays)


_HBM = pl.BlockSpec(memory_space=pltpu.HBM)
_SEM = pl.BlockSpec(memory_space=pltpu.SEMAPHORE)
_EFFECT = pltpu.SideEffectType.DATAFLOW_SIDE_EFFECTING
GATHER = "gather"
SCATTER = "scatter"
SPREAD = "spread"


def _zone_shape(a, mode):
    if mode == GATHER:
        return (N_DEV,) + a.shape
    return (N_DEV - 1,) + (a.shape[1:] if mode == SCATTER else a.shape)


def _start_copies(arrays, modes, name, after=None):
    n = len(arrays)
    zones = [lax.empty(_zone_shape(a, m), a.dtype) for a, m in zip(arrays, modes)]

    def body(*refs):
        ins, lands = refs[:n], refs[n:2 * n]
        send_sems, recv_sems = refs[-2 * n - 3], refs[-2 * n - 2]
        token = refs[-1]
        x, y, c, me = _place()
        for a in range(n):
            for k, (peer, pid) in enumerate(_peers(x, y, c)):
                src = ins[a].at[pid] if modes[a] == SCATTER else ins[a]
                dst = lands[a].at[me] if modes[a] == GATHER else lands[a].at[k]
                pltpu.make_async_remote_copy(src_ref=src, dst_ref=dst, send_sem=send_sems.at[a], recv_sem=recv_sems.at[a],
                                             device_id=peer, device_id_type=MESH).start()
        token[...] = jnp.zeros_like(token)

    hbm = lambda a: pltpu.HBM(a.shape, a.dtype)
    sems = pltpu.SemaphoreType.DMA((n,))
    extra = [] if after is None else [after]
    operands = [pltpu.with_memory_space_constraint(a, pltpu.HBM) for a in list(arrays) + zones]
    res = pl.pallas_call(
        body, name=name,
        out_shape=(sems, sems, *[hbm(a) for a in arrays], *[hbm(z) for z in zones], SDS((8, 128), F32)),
        in_specs=[_HBM] * (2 * n) + [_ANY] * len(extra),
        out_specs=(_SEM, _SEM, *[_HBM] * (2 * n), pl.BlockSpec(memory_space=pltpu.VMEM)),
        input_output_aliases={i: 2 + i for i in range(2 * n)},
        compiler_params=pltpu.CompilerParams(has_side_effects=_EFFECT),
    )(*operands, *extra)
    return res[0], res[1], list(res[2:2 + n]), list(res[2 + n:2 + 2 * n]), res[-1]


def _wait_copies(started, after, name):
    send_sems, recv_sems, thru, zones, _ = started
    n = len(thru)

    def body(*refs):
        lands = refs[n:2 * n]
        send_ref, recv_ref = refs[2 * n], refs[2 * n + 1]
        x, y, c, _ = _place()
        for a in range(n):
            seven = lands[a].at[pl.ds(0, N_DEV - 1)]
            cp = pltpu.make_async_remote_copy(src_ref=seven, dst_ref=seven, send_sem=send_ref.at[a], recv_sem=recv_ref.at[a],
                                              device_id=(x, y, 1 - c), device_id_type=MESH)
            cp.wait_send()
            cp.wait_recv()

    hbm = lambda a: pltpu.HBM(a.shape, a.dtype)
    res = pl.pallas_call(
        body, name=name,
        out_shape=tuple(hbm(a) for a in thru + zones),
        in_specs=[_HBM] * (2 * n) + [_SEM, _SEM, _ANY],
        out_specs=tuple([_HBM] * (2 * n)),
        input_output_aliases={i: i for i in range(2 * n)},
        compiler_params=pltpu.CompilerParams(has_side_effects=_EFFECT),
    )(*thru, *zones, send_sems, recv_sems, after)
    return list(res[:n]), list(res[n:])


def _adamw_math(g, w, m, v):
    m2 = ADAM_B1 * m + (1.0 - ADAM_B1) * g
    v2 = ADAM_B2 * v + (1.0 - ADAM_B2) * (g * g)
    m_hat = m2 / (1.0 - ADAM_B1 ** ADAM_STEP)
    v_hat = v2 / (1.0 - ADAM_B2 ** ADAM_STEP)
    delta = -ADAM_LR * (m_hat / (jnp.sqrt(v_hat) + ADAM_EPS) + ADAM_WD * w)
    return delta, m2, v2


def _sum_adamw(parts, w, m, v, name):
    R, C = w.shape
    tr = max(t for t in (128, 64, 32, 16, 8) if R % t == 0)

    def body(p_ref, w_ref, m_ref, v_ref, g_ref, d_ref, m2_ref, v2_ref):
        g = p_ref[0]
        for k in range(1, N_DEV):
            g = g + p_ref[k]
        g_ref[...] = g
        d_ref[...], m2_ref[...], v2_ref[...] = _adamw_math(g, w_ref[...], m_ref[...], v_ref[...])

    blk = pl.BlockSpec((tr, C), lambda i: (i, 0))
    return pl.pallas_call(
        body, name=name, grid=(R // tr,),
        in_specs=[pl.BlockSpec((N_DEV, tr, C), lambda i: (0, i, 0)), blk, blk, blk],
        out_specs=[blk] * 4,
        out_shape=[SDS((R, C), F32)] * 4,
        compiler_params=_params(1),
    )(parts, w, m, v)


def _sum_adamw_peers(me, own, parts, w, m, v, name, replicated):
    R, C = w.shape
    tr = max(t for t in (128, 64, 32, 16, 8) if R % t == 0)

    def body(me_ref, own_ref, p_ref, w_ref, m_ref, v_ref, g_ref, d_ref, m2_ref, v2_ref):
        if replicated:
            mine = me_ref[0]
            g = None
            for j in range(N_DEV):
                k = jnp.maximum(jnp.bitwise_xor(mine, j) - 1, 0)
                term = jnp.where(mine == j, own_ref[...], p_ref[k])
                g = term if g is None else g + term
        else:
            g = own_ref[...]
            for k in range(N_DEV - 1):
                g = g + p_ref[k]
        g_ref[...] = g
        d_ref[...], m2_ref[...], v2_ref[...] = _adamw_math(g, w_ref[...], m_ref[...], v_ref[...])

    blk = pl.BlockSpec((tr, C), lambda i, me_ref: (i, 0))
    own_spec = blk if replicated else pl.BlockSpec((None, tr, C), lambda i, me_ref: (me_ref[0], i, 0))
    return pl.pallas_call(
        body, name=name,
        grid_spec=pltpu.PrefetchScalarGridSpec(
            num_scalar_prefetch=1, grid=(R // tr,),
            in_specs=[own_spec, pl.BlockSpec((N_DEV - 1, tr, C), lambda i, me_ref: (0, i, 0)), blk, blk, blk],
            out_specs=[blk] * 4),
        out_shape=[SDS((R, C), F32)] * 4,
        compiler_params=_params(1),
    )(me, own, parts, w, m, v)


SMALL = ("ln_v_gain", "ln_v_bias", "w_spatial", "b_spatial", "sinks", "norm_mix_post", "norm_ff_pre", "norm_ff_post")
SMALL_ROWS = {"ln_v_gain": 8, "ln_v_bias": 8, "w_spatial": 1024, "b_spatial": 8, "sinks": 8,
              "norm_mix_post": 8, "norm_ff_pre": 8, "norm_ff_post": 8}
SMALL_PACK_ROWS = 1152


def _pack_small(vals):
    rows = []
    for name in SMALL:
        flat = vals[name].reshape(-1)
        pad = SMALL_ROWS[name] * 128 - flat.shape[0]
        if pad:
            flat = jnp.concatenate([flat, jnp.zeros((pad,), F32)])
        rows.append(flat.reshape(SMALL_ROWS[name], 128))
    rows.append(jnp.zeros((SMALL_PACK_ROWS - sum(SMALL_ROWS.values()), 128), F32))
    return jnp.concatenate(rows, axis=0)


def _unpack_small(packed, shapes):
    out, r = {}, 0
    for name in SMALL:
        n = 1
        for s in shapes[name]:
            n *= s
        out[name] = packed[r:r + SMALL_ROWS[name]].reshape(-1)[:n].reshape(shapes[name])
        r += SMALL_ROWS[name]
    return out


def _rope_rows():
    d = jnp.arange(128) % HEAD
    inv = ROPE_THETA ** (-(2.0 * (d % (ROPE // 2))).astype(F32) / ROPE)
    invf = jnp.where(d < ROPE, inv, 0.0).astype(F32).reshape(1, 128)
    sgn = jnp.where(d < ROPE // 2, -1.0, jnp.where(d < ROPE, 1.0, 0.0)).astype(F32).reshape(1, 128)
    return invf, sgn


def kernel(x, positions, w_in, ln_v_gain, ln_v_bias, w_spatial, b_spatial, sinks, w_a, w_b, w_o, norm_mix_pre, norm_mix_post, w_ff_in, w_ff_out, norm_ff_pre, norm_ff_post, loss_target, m_w_in, m_ln_v_gain, m_ln_v_bias, m_w_spatial, m_b_spatial, m_sinks, m_w_a, m_w_b, m_w_o, m_norm_mix_pre, m_norm_mix_post, m_w_ff_in, m_w_ff_out, m_norm_ff_pre, m_norm_ff_post, v_w_in, v_ln_v_gain, v_ln_v_bias, v_w_spatial, v_b_spatial, v_sinks, v_w_a, v_w_b, v_w_o, v_norm_mix_pre, v_norm_mix_post, v_w_ff_in, v_w_ff_out, v_norm_ff_pre, v_norm_ff_post):
    given = dict(locals())
    T = x.shape[1]
    xt = x[0]
    tgt = loss_target[0]
    posf = positions.astype(F32).reshape(T, 1)
    invf, sgn = _rope_rows()
    bst = b_spatial[0].T
    ws = w_spatial[0]

    me = 4 * lax.axis_index("x") + 2 * lax.axis_index("y") + lax.axis_index("c")
    me_arr = me.astype(jnp.int32).reshape(1)

    def with_own(zone, shard):
        return lax.dynamic_update_slice(zone, shard[None], (me,) + (0,) * shard.ndim)

    rest = ("w_a", "w_b", "w_o", "w_ff_in", "w_ff_out")
    shard = {n: given[n][0].astype(BF16) for n in ("w_in",) + rest}
    g_in = _start_copies([shard["w_in"]], [GATHER], "gather_in_start")
    g_rest = _start_copies([shard[n] for n in rest], [GATHER] * len(rest), "gather_rest_start", after=g_in[-1])
    (own_win,), (win8,) = _wait_copies(g_in, g_rest[-1], "gather_in_wait")
    win = jnp.transpose(with_own(win8, own_win), (1, 0, 2)).reshape(D, IN_W)

    proj, h = _fwd_in(xt, norm_mix_pre, win)
    a = _fwd_sgu(proj, ln_v_gain, ln_v_bias, ws, bst)
    att = _fwd_attn(proj, posf, invf, sgn, sinks[0])
    gw = {n: with_own(z, own) for n, own, z in zip(rest, *_wait_copies(g_rest, att, "gather_rest_wait"))}
    wa, wb, wo = (gw[n].reshape(D, D) for n in ("w_a", "w_b", "w_o"))
    wfi3 = gw["w_ff_in"]
    wfo = gw["w_ff_out"].reshape(D_FF, D)
    merged, a2, b2, mix, x1, hf = _fwd_mix(a, att, proj, xt, wa, wb, wo, norm_mix_post, norm_ff_pre)
    f, dy, dff, dg3, loss_part = _fwd_ff(hf, wfi3, wfo, x1, tgt, norm_ff_post)

    df, dx1, dmix, dg2, dg1 = _bwd_ff(dff, f, wfi3, wfo, x1, dy, mix, norm_mix_post, norm_ff_pre)
    dwfi3, dwfo = _wgrad_ff(hf, df, f, dff)
    own_ff = [dwfi3, dwfo.reshape(N_DEV, D_FF // N_DEV, D)]
    x_ff = _start_copies(own_ff, [SCATTER] * 2, "exchange_ff_start")
    da2, db2, dgate, da, datt = _bwd_mix(dmix, proj, a2, b2, wo, wa, wb, after=x_ff[-1])
    dwo, dwa, dwb = _wgrad_mix(merged, dmix, a, da2, att, db2)
    own_mix = [g.reshape(N_DEV, D // N_DEV, D) for g in (dwa, dwb, dwo)]
    x_mix = _start_copies(own_mix, [SCATTER] * 3, "exchange_mix_start")
    dq, dkv, dsink = _bwd_attn(proj, posf, invf, sgn, sinks[0], datt, after=x_mix[-1])
    duv, dws, dbs, dlng, dlnb = _bwd_sgu(proj, da, ln_v_gain, ln_v_bias, ws, bst)
    dwin = _wgrad_in(h, duv, dq, dkv, dgate)
    small_grads = {"ln_v_gain": dlng, "ln_v_bias": dlnb, "w_spatial": dws, "b_spatial": dbs, "sinks": dsink[:, :N_Q],
                   "norm_mix_post": dg1, "norm_ff_pre": dg2, "norm_ff_post": dg3}
    own_in = [jnp.transpose(dwin.reshape(D, N_DEV, IN_W // N_DEV), (1, 0, 2)), _pack_small(small_grads)]
    x_in = _start_copies(own_in, [SCATTER, SPREAD], "exchange_in_start")
    grad_x, dg0 = _bwd_in(duv, dq, dkv, dgate, win, xt, dx1, norm_mix_pre, after=x_in[-1])
    (dg0_all,) = _all_to_all([dg0.reshape(8, 128)], [True], "exchange_tail")

    results = {}

    def update(n, own, parts):
        results[n] = [r.reshape(given[n].shape) for r in _sum_adamw_peers(
            me_arr, own, parts, given[n][0], given["m_" + n][0], given["v_" + n][0], "adamw_" + n, False)]

    own_ff, p_ff = _wait_copies(x_ff, dg0_all, "exchange_ff_wait")
    update("w_ff_in", own_ff[0], p_ff[0])
    update("w_ff_out", own_ff[1], p_ff[1])
    own_mix, p_mix = _wait_copies(x_mix, p_ff[0], "exchange_mix_wait")
    for n, own, parts in zip(("w_a", "w_b", "w_o"), own_mix, p_mix):
        update(n, own, parts)
    own_in, p_in = _wait_copies(x_in, p_mix[0], "exchange_in_wait")
    update("w_in", own_in[0], p_in[0])
    packed = _sum_adamw_peers(me_arr, own_in[1], p_in[1], _pack_small({n: given[n] for n in SMALL}),
                              _pack_small({n: given["m_" + n] for n in SMALL}),
                              _pack_small({n: given["v_" + n] for n in SMALL}), "adamw_small", True)
    shapes = {n: given[n].shape for n in SMALL}
    unpacked = [_unpack_small(p, shapes) for p in packed]
    for n in SMALL:
        results[n] = [u[n] for u in unpacked]
    n = "norm_mix_pre"
    results[n] = [r.reshape(given[n].shape) for r in _sum_adamw(
        dg0_all, given[n].reshape(8, 128), given["m_" + n].reshape(8, 128), given["v_" + n].reshape(8, 128), "adamw_" + n)]

    loss = lax.psum(loss_part[0, 0], ("x", "y", "c"))
    order = ("w_in", "ln_v_gain", "ln_v_bias", "w_spatial", "b_spatial", "sinks", "w_a", "w_b", "w_o", "norm_mix_pre",
             "norm_mix_post", "w_ff_in", "w_ff_out", "norm_ff_pre", "norm_ff_post")
    out = [loss, grad_x.reshape(x.shape)]
    for k in range(4):
        out += [results[n][k] for n in order]
    return tuple(out)
```

```python
import functools

import jax
import jax.numpy as jnp
from jax import lax
from jax.experimental import pallas as pl
from jax.experimental.pallas import tpu as pltpu

F32 = jnp.float32
BF16 = jnp.bfloat16

N_DEV = 8
D = 1024
D_FF = 4096
IN_W = 5632
CHUNK = 128
GROUPS = 8
HEAD = 64
N_Q = 16
N_KV = 4
ROPE = 16
ROPE_THETA = 500000.0
EPS = 1e-6
OFF_Q, OFF_K, OFF_VA, OFF_GA, OFF_GB = 2048, 3072, 3328, 3584, 4608

ADAM_LR = 0.001
ADAM_B1 = 0.9
ADAM_B2 = 0.999
ADAM_EPS = 1e-08
ADAM_WD = 0.01
ADAM_STEP = 10

VMEM_LIMIT = 56 * 1024 * 1024

SDS = jax.ShapeDtypeStruct
MESH = pl.DeviceIdType.MESH


def _params(n_axes=None):
    if n_axes is None:
        return pltpu.CompilerParams(vmem_limit_bytes=VMEM_LIMIT)
    return pltpu.CompilerParams(dimension_semantics=("arbitrary",) * n_axes, vmem_limit_bytes=VMEM_LIMIT)


def _nt(a, b):
    return lax.dot_general(a, b, (((1,), (1,)), ((), ())), preferred_element_type=F32)


def _tn(a, b):
    return lax.dot_general(a, b, (((0,), (0,)), ((), ())), preferred_element_type=F32)


def _nn(a, b):
    return jnp.dot(a, b, preferred_element_type=F32)


def _gelu(x):
    t = jnp.tanh(0.7978845608028654 * (x + 0.044715 * (x * x * x)))
    return 0.5 * x * (1.0 + t), t


def _gelu_grad(x, t):
    return 0.5 * (1.0 + t) + 0.5 * x * (1.0 - t * t) * (0.7978845608028654 * (1.0 + 3.0 * 0.044715 * x * x))


def _sigmoid(x):
    return 1.0 / (1.0 + jnp.exp(-x))


def _rms_stats(v):
    r = lax.rsqrt(jnp.mean(v * v, axis=-1, keepdims=True) + EPS)
    return r, v * r


def _rms_bwd(d, vhat, r, g):
    gd = g * d
    return r * (gd - vhat * jnp.mean(gd * vhat, axis=-1, keepdims=True))


def _colsum(v):
    return jnp.sum(v, axis=0, keepdims=True)


_ANY = pl.BlockSpec(memory_space=pl.ANY)


def _after(body, n_in, after):
    if after is None:
        return body, [], []

    def ordered(*refs):
        return body(*refs[:n_in], *refs[n_in + 1:])

    return ordered, [_ANY], [after]


def _fwd_in(x, g0, win):
    T = x.shape[0]
    tm, tn = min(T, 1024), 1408

    def body(x_ref, g_ref, w_ref, p_ref, h_ref):
        @pl.when(pl.program_id(1) == 0)
        def _():
            _, xh = _rms_stats(x_ref[...])
            h_ref[...] = (xh * g_ref[...]).astype(BF16)

        p_ref[...] = _nn(h_ref[...], w_ref[...]).astype(BF16)

    return pl.pallas_call(
        body, name="fwd_in", grid=(T // tm, IN_W // tn),
        in_specs=[pl.BlockSpec((tm, D), lambda i, j: (i, 0)), pl.BlockSpec((1, D), lambda i, j: (0, 0)),
                  pl.BlockSpec((D, tn), lambda i, j: (0, j))],
        out_specs=[pl.BlockSpec((tm, tn), lambda i, j: (i, j)), pl.BlockSpec((tm, D), lambda i, j: (i, 0))],
        out_shape=[SDS((T, IN_W), BF16), SDS((T, D), BF16)],
        compiler_params=_params(2),
    )(x, g0, win)


def _sgu_forward_parts(u_ref, vs_ref, lng_ref, lnb_ref):
    u = u_ref[...].astype(F32)
    vs = vs_ref[...].astype(F32)
    gu, tu = _gelu(u)
    gv, tv = _gelu(vs)
    mu = jnp.mean(gv, axis=-1, keepdims=True)
    dv = gv - mu
    rstd = lax.rsqrt(jnp.mean(dv * dv, axis=-1, keepdims=True) + EPS)
    vhat = dv * rstd
    vn = (vhat * lng_ref[...] + lnb_ref[...]).astype(BF16)
    return u, vs, gu, tu, tv, rstd, vhat, vn


def _masked_ws(ws_ref, g):
    row = lax.broadcasted_iota(jnp.int32, (CHUNK, CHUNK), 0)
    col = lax.broadcasted_iota(jnp.int32, (CHUNK, CHUNK), 1)
    return jnp.where(row >= col, ws_ref[g], 0.0).astype(BF16)


def _fwd_sgu(proj, lng, lnb, ws, bst):
    T = proj.shape[0]
    tc = min(T, 512)

    def body(u_ref, vs_ref, lng_ref, lnb_ref, ws_ref, bst_ref, a_ref):
        _, _, gu, _, _, _, _, vn = _sgu_forward_parts(u_ref, vs_ref, lng_ref, lnb_ref)
        for g in range(GROUPS):
            wm = _masked_ws(ws_ref, g)
            cols = slice(g * CHUNK, (g + 1) * CHUNK)
            for c in range(tc // CHUNK):
                rows = slice(c * CHUNK, (c + 1) * CHUNK)
                mixed = _nn(wm, vn[rows, cols]) + bst_ref[:, g:g + 1]
                a_ref[rows, cols] = (gu[rows, cols] * mixed).astype(BF16)

    return pl.pallas_call(
        body, name="fwd_sgu", grid=(T // tc,),
        in_specs=[pl.BlockSpec((tc, D), lambda i: (i, 0)), pl.BlockSpec((tc, D), lambda i: (i, 1)),
                  pl.BlockSpec((1, D), lambda i: (0, 0)), pl.BlockSpec((1, D), lambda i: (0, 0)),
                  pl.BlockSpec((GROUPS, CHUNK, CHUNK), lambda i: (0, 0, 0)), pl.BlockSpec((CHUNK, GROUPS), lambda i: (0, 0))],
        out_specs=pl.BlockSpec((tc, D), lambda i: (i, 0)),
        out_shape=SDS((T, D), BF16),
        compiler_params=_params(1),
    )(proj, proj, lng, lnb, ws, bst)


def _rope_tables(pos, invf, sgn, reps):
    ang = pos * invf
    c = jnp.cos(ang)
    s = jnp.sin(ang) * sgn
    if reps > 1:
        c = jnp.tile(c, (1, reps))
        s = jnp.tile(s, (1, reps))
    return c, s


def _swap_halves(v):
    n = v.shape[1]
    d = lax.broadcasted_iota(jnp.int32, v.shape, 1) % HEAD
    upper = jnp.where(d < ROPE, pltpu.roll(v, ROPE // 2, 1), 0.0)
    return jnp.where(d < ROPE // 2, pltpu.roll(v, n - ROPE // 2, 1), upper)


def _rope(v, c, s):
    return v * c + _swap_halves(v) * s


def _rope_bwd(dv, c, s):
    return dv * c + _swap_halves(dv * s)


def _band_mask(first, key_axis):
    shape = (CHUNK, 2 * CHUNK) if key_axis == 1 else (2 * CHUNK, CHUNK)
    t = lax.broadcasted_iota(jnp.int32, shape, 1 - key_axis)
    j = lax.broadcasted_iota(jnp.int32, shape, key_axis)
    return (j > t) & (j <= t + CHUNK) & (jnp.logical_not(first) | (j >= CHUNK))


def _softmax_sink(s, sink, key_axis):
    m = jnp.maximum(jnp.max(s, axis=key_axis, keepdims=True), sink)
    p = jnp.exp(s - m)
    esink = jnp.exp(sink - m)
    inv = 1.0 / (jnp.sum(p, axis=key_axis, keepdims=True) + esink)
    return p * inv, esink * inv


def _head_pair_operand(band, g):
    slab = band[:, (g // 2) * 128:(g // 2 + 1) * 128]
    lo = lax.broadcasted_iota(jnp.int32, slab.shape, 1) < HEAD
    if g % 2 == 0:
        first = jnp.where(lo, slab, 0.0)
        second = pltpu.roll(first, HEAD, 1)
    else:
        second = jnp.where(lo, 0.0, slab)
        first = pltpu.roll(second, HEAD, 1)
    return jnp.concatenate([first, second], axis=0).astype(BF16)


def _head_pair_gradient(acc, g):
    top, bot = acc[:2 * CHUNK], acc[2 * CHUNK:]
    lo = lax.broadcasted_iota(jnp.int32, top.shape, 1) < HEAD
    if g % 2 == 0:
        return jnp.where(lo, top, 0.0) + pltpu.roll(jnp.where(lo, 0.0, bot), HEAD, 1)
    return pltpu.roll(jnp.where(lo, top, 0.0), HEAD, 1) + jnp.where(lo, 0.0, bot)


def _attn_specs(nb, clamp):
    cur = (lambda i: jnp.minimum(i, nb - 1)) if clamp else (lambda i: i)
    prev = lambda i: jnp.maximum(jnp.minimum(i, nb - 1) - 1, 0)
    kw = N_KV * HEAD
    return cur, prev, [
        pl.BlockSpec((CHUNK, D), lambda i: (cur(i), OFF_Q // D)),
        pl.BlockSpec((CHUNK, kw), lambda i: (prev(i), OFF_K // kw)),
        pl.BlockSpec((CHUNK, kw), lambda i: (cur(i), OFF_K // kw)),
        pl.BlockSpec((CHUNK, kw), lambda i: (prev(i), OFF_VA // kw)),
        pl.BlockSpec((CHUNK, kw), lambda i: (cur(i), OFF_VA // kw)),
        pl.BlockSpec((CHUNK, 1), lambda i: (prev(i), 0)),
        pl.BlockSpec((CHUNK, 1), lambda i: (cur(i), 0)),
        pl.BlockSpec((1, 128), lambda i: (0, 0)),
        pl.BlockSpec((1, 128), lambda i: (0, 0)),
        pl.BlockSpec(memory_space=pltpu.SMEM),
    ]


def _attn_load(q_ref, kp_ref, kc_ref, vp_ref, vc_ref, pp_ref, pc_ref, invf_ref, sgn_ref):
    cq, sq = _rope_tables(pc_ref[...], invf_ref[...], sgn_ref[...], D // 128)
    pos_b = jnp.concatenate([pp_ref[...], pc_ref[...]], axis=0)
    ck, sk = _rope_tables(pos_b, invf_ref[...], sgn_ref[...], N_KV * HEAD // 128)
    q = _rope(q_ref[...].astype(F32), cq, sq).astype(BF16)
    kb = _rope(jnp.concatenate([kp_ref[...], kc_ref[...]], axis=0).astype(F32), ck, sk)
    vb = jnp.concatenate([vp_ref[...], vc_ref[...]], axis=0).astype(F32)
    return q, kb, vb, (cq, sq, ck, sk)


PAIRS_PER_KV = N_Q // N_KV // 2


def _fwd_attn(proj, posf, invf, sgn, sinks):
    T = proj.shape[0]
    nb = T // CHUNK
    _, _, specs = _attn_specs(nb, False)

    def body(q_ref, kp_ref, kc_ref, vp_ref, vc_ref, pp_ref, pc_ref, invf_ref, sgn_ref, sink_ref, o_ref):
        q, kb, vb, _ = _attn_load(q_ref, kp_ref, kc_ref, vp_ref, vc_ref, pp_ref, pc_ref, invf_ref, sgn_ref)
        mask = _band_mask(pl.program_id(0) == 0, 0)
        for g in range(N_KV):
            k2 = _head_pair_operand(kb, g)
            v2 = _head_pair_operand(vb, g)
            for r in range(PAIRS_PER_KV):
                pair = g * PAIRS_PER_KV + r
                s2 = _nt(k2, q[:, pair * 128:(pair + 1) * 128]) * (HEAD ** -0.5)
                ps = []
                for e in range(2):
                    s = jnp.where(mask, s2[e * 2 * CHUNK:(e + 1) * 2 * CHUNK], -1e30)
                    ps.append(_softmax_sink(s, sink_ref[2 * pair + e], 0)[0].astype(BF16))
                o_ref[:, pair * 128:(pair + 1) * 128] = _tn(jnp.concatenate(ps, axis=0), v2).astype(BF16)

    return pl.pallas_call(
        body, name="fwd_attn", grid=(nb,), in_specs=specs,
        out_specs=pl.BlockSpec((CHUNK, D), lambda i: (i, 0)),
        out_shape=SDS((T, D), BF16),
        compiler_params=_params(1),
    )(proj, proj, proj, proj, proj, posf, posf, invf, sgn, sinks)


def _fwd_mix(a, att, proj, x, wa, wb, wo, g1, g2):
    T = x.shape[0]
    tm = min(T, 256)
    half = D // 2

    def body(a_ref, att_ref, ga0, ga1, gb0, gb1, x_ref, wa_ref, wb_ref, wo_ref, g1_ref, g2_ref,
             mg_ref, a2_ref, b2_ref, mix_ref, x1_ref, hf_ref):
        a2 = _nn(a_ref[...], wa_ref[...])
        b2 = _nn(att_ref[...], wb_ref[...])
        ga = jnp.concatenate([ga0[...], ga1[...]], axis=1).astype(F32)
        gb = jnp.concatenate([gb0[...], gb1[...]], axis=1).astype(F32)
        merged = (_sigmoid(ga) * a2 + _sigmoid(gb) * b2).astype(BF16)
        a2_ref[...] = a2.astype(BF16)
        b2_ref[...] = b2.astype(BF16)
        mg_ref[...] = merged
        mix = _nn(merged, wo_ref[...])
        mix_ref[...] = mix
        _, mh = _rms_stats(mix)
        x1 = x_ref[...] + mh * g1_ref[...]
        x1_ref[...] = x1
        _, xh = _rms_stats(x1)
        hf_ref[...] = (xh * g2_ref[...]).astype(BF16)

    row = lambda i: (i, 0)
    const = lambda i: (0, 0)
    gspec = lambda off: pl.BlockSpec((tm, half), lambda i: (i, off // half))
    return pl.pallas_call(
        body, name="fwd_mix", grid=(T // tm,),
        in_specs=[pl.BlockSpec((tm, D), row), pl.BlockSpec((tm, D), row),
                  gspec(OFF_GA), gspec(OFF_GA + half), gspec(OFF_GB), gspec(OFF_GB + half),
                  pl.BlockSpec((tm, D), row), pl.BlockSpec((D, D), const), pl.BlockSpec((D, D), const),
                  pl.BlockSpec((D, D), const), pl.BlockSpec((1, D), const), pl.BlockSpec((1, D), const)],
        out_specs=[pl.BlockSpec((tm, D), row)] * 6,
        out_shape=[SDS((T, D), BF16), SDS((T, D), BF16), SDS((T, D), BF16), SDS((T, D), F32), SDS((T, D), F32),
                   SDS((T, D), BF16)],
        compiler_params=_params(1),
    )(a, att, proj, proj, proj, proj, x, wa, wb, wo, g1, g2)


FF_SPLIT = N_DEV
FF_TILE = D_FF // FF_SPLIT


def _fwd_ff(hf, wfi3, wfo, x1, tgt, g3):
    T = hf.shape[0]
    tm = min(T, 512)
    last = FF_SPLIT - 1

    def body(hf_ref, wfi_ref, wfo_ref, x1_ref, tgt_ref, g3_ref, f_ref, dy_ref, dff_ref, dg3_ref, loss_ref, acc):
        i, p = pl.program_id(0), pl.program_id(1)

        @pl.when((i == 0) & (p == 0))
        def _():
            dg3_ref[...] = jnp.zeros_like(dg3_ref)
            loss_ref[...] = jnp.zeros_like(loss_ref)

        f = _nn(hf_ref[...], wfi_ref[...]).astype(BF16)
        f_ref[...] = f
        rl = jnp.maximum(f.astype(F32), 0.0)
        part = _nn((rl * rl).astype(BF16), wfo_ref[...])

        @pl.when(p == 0)
        def _():
            acc[...] = part

        @pl.when(p > 0)
        def _():
            acc[...] += part

        @pl.when(p == last)
        def _():
            r3, fh = _rms_stats(acc[...])
            e = x1_ref[...] + fh * g3_ref[...] - tgt_ref[...]
            loss_ref[...] += jnp.sum(e * e) * (0.5 / D)
            dy = e * (1.0 / D)
            dy_ref[...] = dy
            dg3_ref[...] += _colsum(dy * fh)
            dff_ref[...] = _rms_bwd(dy, fh, r3, g3_ref[...]).astype(BF16)

    row = lambda i, p: (i, 0)
    const = lambda i, p: (0, 0)
    return pl.pallas_call(
        body, name="fwd_ff", grid=(T // tm, FF_SPLIT),
        in_specs=[pl.BlockSpec((tm, D), row), pl.BlockSpec((None, D, FF_TILE), lambda i, p: (p, 0, 0)),
                  pl.BlockSpec((FF_TILE, D), lambda i, p: (p, 0)), pl.BlockSpec((tm, D), row),
                  pl.BlockSpec((tm, D), row), pl.BlockSpec((1, D), const)],
        out_specs=[pl.BlockSpec((tm, FF_TILE), lambda i, p: (i, p)), pl.BlockSpec((tm, D), row),
                   pl.BlockSpec((tm, D), row), pl.BlockSpec((1, D), const), pl.BlockSpec((1, 128), const)],
        out_shape=[SDS((T, D_FF), BF16), SDS((T, D), F32), SDS((T, D), BF16), SDS((1, D), F32), SDS((1, 128), F32)],
        scratch_shapes=[pltpu.VMEM((tm, D), F32)],
        compiler_params=_params(2),
    )(hf, wfi3, wfo, x1, tgt, g3)


def _bwd_ff(dff, f, wfi3, wfo, x1, dy, mix, g1, g2):
    T = dff.shape[0]
    tm = min(T, 512)
    last = FF_SPLIT - 1

    def body(dff_ref, f_ref, wfi_ref, wfo_ref, x1_ref, dy_ref, mix_ref, g1_ref, g2_ref,
             df_ref, dx1_ref, dmix_ref, dg2_ref, dg1_ref, acc):
        i, p = pl.program_id(0), pl.program_id(1)

        @pl.when((i == 0) & (p == 0))
        def _():
            dg2_ref[...] = jnp.zeros_like(dg2_ref)
            dg1_ref[...] = jnp.zeros_like(dg1_ref)

        dr = _nt(dff_ref[...], wfo_ref[...])
        df = (dr * (2.0 * jnp.maximum(f_ref[...].astype(F32), 0.0))).astype(BF16)
        df_ref[...] = df
        part = _nt(df, wfi_ref[...])

        @pl.when(p == 0)
        def _():
            acc[...] = part

        @pl.when(p > 0)
        def _():
            acc[...] += part

        @pl.when(p == last)
        def _():
            dhf = acc[...]
            r2, xh = _rms_stats(x1_ref[...])
            dg2_ref[...] += _colsum(dhf * xh)
            dx1 = dy_ref[...] + _rms_bwd(dhf, xh, r2, g2_ref[...])
            dx1_ref[...] = dx1
            r1, mh = _rms_stats(mix_ref[...])
            dg1_ref[...] += _colsum(dx1 * mh)
            dmix_ref[...] = _rms_bwd(dx1, mh, r1, g1_ref[...]).astype(BF16)

    row = lambda i, p: (i, 0)
    const = lambda i, p: (0, 0)
    return pl.pallas_call(
        body, name="bwd_ff", grid=(T // tm, FF_SPLIT),
        in_specs=[pl.BlockSpec((tm, D), row), pl.BlockSpec((tm, FF_TILE), lambda i, p: (i, p)),
                  pl.BlockSpec((None, D, FF_TILE), lambda i, p: (p, 0, 0)), pl.BlockSpec((FF_TILE, D), lambda i, p: (p, 0)),
                  pl.BlockSpec((tm, D), row), pl.BlockSpec((tm, D), row), pl.BlockSpec((tm, D), row),
                  pl.BlockSpec((1, D), const), pl.BlockSpec((1, D), const)],
        out_specs=[pl.BlockSpec((tm, FF_TILE), lambda i, p: (i, p)), pl.BlockSpec((tm, D), row),
                   pl.BlockSpec((tm, D), row), pl.BlockSpec((1, D), const), pl.BlockSpec((1, D), const)],
        out_shape=[SDS((T, D_FF), BF16), SDS((T, D), F32), SDS((T, D), BF16), SDS((1, D), F32), SDS((1, D), F32)],
        scratch_shapes=[pltpu.VMEM((tm, D), F32)],
        compiler_params=_params(2),
    )(dff, f, wfi3, wfo, x1, dy, mix, g1, g2)


def _wgrad_ff(hf, df, f, dff):
    T = hf.shape[0]
    tt = min(T, 512)

    def body(hf_ref, df_ref, f_ref, dff_ref, dwfi_ref, dwfo_ref):
        @pl.when(pl.program_id(1) == 0)
        def _():
            dwfi_ref[...] = jnp.zeros_like(dwfi_ref)
            dwfo_ref[...] = jnp.zeros_like(dwfo_ref)

        dwfi_ref[...] += _tn(hf_ref[...], df_ref[...])
        rl = jnp.maximum(f_ref[...].astype(F32), 0.0)
        dwfo_ref[...] += _tn((rl * rl).astype(BF16), dff_ref[...])

    return pl.pallas_call(
        body, name="wgrad_ff", grid=(FF_SPLIT, T // tt),
        in_specs=[pl.BlockSpec((tt, D), lambda p, t: (t, 0)), pl.BlockSpec((tt, FF_TILE), lambda p, t: (t, p)),
                  pl.BlockSpec((tt, FF_TILE), lambda p, t: (t, p)), pl.BlockSpec((tt, D), lambda p, t: (t, 0))],
        out_specs=[pl.BlockSpec((None, D, FF_TILE), lambda p, t: (p, 0, 0)), pl.BlockSpec((FF_TILE, D), lambda p, t: (p, 0))],
        out_shape=[SDS((FF_SPLIT, D, FF_TILE), F32), SDS((D_FF, D), F32)],
        compiler_params=_params(2),
    )(hf, df, f, dff)


def _bwd_mix(dmix, proj, a2, b2, wo, wa, wb, after=None):
    T = dmix.shape[0]
    tm = min(T, 256)
    half = D // 2

    def body(dmix_ref, ga0, ga1, gb0, gb1, a2_ref, b2_ref, wo_ref, wa_ref, wb_ref,
             da2_ref, db2_ref, dg_ref, da_ref, datt_ref):
        dmg = _nt(dmix_ref[...], wo_ref[...])
        sa = _sigmoid(jnp.concatenate([ga0[...], ga1[...]], axis=1).astype(F32))
        sb = _sigmoid(jnp.concatenate([gb0[...], gb1[...]], axis=1).astype(F32))
        da2 = (dmg * sa).astype(BF16)
        db2 = (dmg * sb).astype(BF16)
        da2_ref[...] = da2
        db2_ref[...] = db2
        dg_ref[:, :D] = (dmg * a2_ref[...].astype(F32) * (sa * (1.0 - sa))).astype(BF16)
        dg_ref[:, D:] = (dmg * b2_ref[...].astype(F32) * (sb * (1.0 - sb))).astype(BF16)
        da_ref[...] = _nt(da2, wa_ref[...]).astype(BF16)
        datt_ref[...] = _nt(db2, wb_ref[...]).astype(BF16)

    row = lambda i: (i, 0)
    const = lambda i: (0, 0)
    gspec = lambda off: pl.BlockSpec((tm, half), lambda i: (i, off // half))
    body, dep_specs, deps = _after(body, 10, after)
    return pl.pallas_call(
        body, name="bwd_mix", grid=(T // tm,),
        in_specs=[pl.BlockSpec((tm, D), row), gspec(OFF_GA), gspec(OFF_GA + half), gspec(OFF_GB), gspec(OFF_GB + half),
                  pl.BlockSpec((tm, D), row), pl.BlockSpec((tm, D), row),
                  pl.BlockSpec((D, D), const), pl.BlockSpec((D, D), const), pl.BlockSpec((D, D), const)] + dep_specs,
        out_specs=[pl.BlockSpec((tm, D), row), pl.BlockSpec((tm, D), row), pl.BlockSpec((tm, 2 * D), row),
                   pl.BlockSpec((tm, D), row), pl.BlockSpec((tm, D), row)],
        out_shape=[SDS((T, D), BF16), SDS((T, D), BF16), SDS((T, 2 * D), BF16), SDS((T, D), BF16), SDS((T, D), BF16)],
        compiler_params=_params(1),
    )(dmix, proj, proj, proj, proj, a2, b2, wo, wa, wb, *deps)


def _wgrad_mix(merged, dmix, a, da2, att, db2):
    T = merged.shape[0]
    tt = min(T, 512)

    def body(mg_ref, dmix_ref, a_ref, da2_ref, att_ref, db2_ref, dwo_ref, dwa_ref, dwb_ref):
        @pl.when(pl.program_id(0) == 0)
        def _():
            dwo_ref[...] = jnp.zeros_like(dwo_ref)
            dwa_ref[...] = jnp.zeros_like(dwa_ref)
            dwb_ref[...] = jnp.zeros_like(dwb_ref)

        dwo_ref[...] += _tn(mg_ref[...], dmix_ref[...])
        dwa_ref[...] += _tn(a_ref[...], da2_ref[...])
        dwb_ref[...] += _tn(att_ref[...], db2_ref[...])

    return pl.pallas_call(
        body, name="wgrad_mix", grid=(T // tt,),
        in_specs=[pl.BlockSpec((tt, D), lambda t: (t, 0))] * 6,
        out_specs=[pl.BlockSpec((D, D), lambda t: (0, 0))] * 3,
        out_shape=[SDS((D, D), F32)] * 3,
        compiler_params=_params(1),
    )(merged, dmix, a, da2, att, db2)


def _bwd_attn(proj, posf, invf, sgn, sinks, datt, after=None):
    T = proj.shape[0]
    nb = T // CHUNK
    kw = N_KV * HEAD
    cur, prev, specs = _attn_specs(nb, True)

    def body(q_ref, kp_ref, kc_ref, vp_ref, vc_ref, pp_ref, pc_ref, invf_ref, sgn_ref, sink_ref, do_ref,
             dq_ref, dkv_ref, dsink_ref, carry_k, carry_v, dq_acc):
        i = pl.program_id(0)

        @pl.when(i == 0)
        def _():
            carry_k[...] = jnp.zeros_like(carry_k)
            carry_v[...] = jnp.zeros_like(carry_v)
            dsink_ref[...] = jnp.zeros_like(dsink_ref)

        @pl.when(i < nb)
        def _():
            q, kb, vb, (cq, sq, ck, sk) = _attn_load(q_ref, kp_ref, kc_ref, vp_ref, vc_ref, pp_ref, pc_ref,
                                                     invf_ref, sgn_ref)
            mask = _band_mask(i == 0, 0)
            do = do_ref[...]
            lane = lax.broadcasted_iota(jnp.int32, (1, 128), 1)
            dsink = jnp.zeros((1, 128), F32)
            dks, dvs = [], []
            for g in range(N_KV):
                k2 = _head_pair_operand(kb, g)
                v2 = _head_pair_operand(vb, g)
                dk2 = jnp.zeros((4 * CHUNK, 128), F32)
                dv2 = jnp.zeros((4 * CHUNK, 128), F32)
                for r in range(PAIRS_PER_KV):
                    pair = g * PAIRS_PER_KV + r
                    qp = q[:, pair * 128:(pair + 1) * 128]
                    dop = do[:, pair * 128:(pair + 1) * 128]
                    s2 = _nt(k2, qp) * (HEAD ** -0.5)
                    dp2 = _nt(v2, dop)
                    ps, dss = [], []
                    for e in range(2):
                        rows = slice(e * 2 * CHUNK, (e + 1) * 2 * CHUNK)
                        p, psink = _softmax_sink(jnp.where(mask, s2[rows], -1e30), sink_ref[2 * pair + e], 0)
                        dp = dp2[rows]
                        delta = jnp.sum(p * dp, axis=0, keepdims=True)
                        ps.append(p.astype(BF16))
                        dss.append((p * (dp - delta) * (HEAD ** -0.5)).astype(BF16))
                        dsink = dsink + jnp.where(lane == 2 * pair + e, -jnp.sum(psink * delta), 0.0)
                    ds2 = jnp.concatenate(dss, axis=0)
                    dq_acc[:, pair * 128:(pair + 1) * 128] = _tn(ds2, k2)
                    dk2 = dk2 + _nn(ds2, qp)
                    dv2 = dv2 + _nn(jnp.concatenate(ps, axis=0), dop)
                dks.append(_head_pair_gradient(dk2, g))
                dvs.append(_head_pair_gradient(dv2, g))
            dsink_ref[...] += dsink
            dq_ref[...] = _rope_bwd(dq_acc[...], cq, sq).astype(BF16)
            dkb = _rope_bwd(jnp.concatenate([dks[0] + dks[1], dks[2] + dks[3]], axis=1), ck, sk)
            dvb = jnp.concatenate([dvs[0] + dvs[1], dvs[2] + dvs[3]], axis=1)
            dkv_ref[:, :kw] = (carry_k[...] + dkb[:CHUNK]).astype(BF16)
            dkv_ref[:, kw:] = (carry_v[...] + dvb[:CHUNK]).astype(BF16)
            carry_k[...] = dkb[CHUNK:]
            carry_v[...] = dvb[CHUNK:]

        @pl.when(i == nb)
        def _():
            dkv_ref[:, :kw] = carry_k[...].astype(BF16)
            dkv_ref[:, kw:] = carry_v[...].astype(BF16)

    body, dep_specs, deps = _after(body, 11, after)
    return pl.pallas_call(
        body, name="bwd_attn", grid=(nb + 1,),
        in_specs=specs + [pl.BlockSpec((CHUNK, D), lambda i: (cur(i), 0))] + dep_specs,
        out_specs=[pl.BlockSpec((CHUNK, D), lambda i: (cur(i), 0)),
                   pl.BlockSpec((CHUNK, 2 * kw), lambda i: (jnp.maximum(i - 1, 0), 0)),
                   pl.BlockSpec((1, 128), lambda i: (0, 0))],
        out_shape=[SDS((T, D), BF16), SDS((T, 2 * kw), BF16), SDS((1, 128), F32)],
        scratch_shapes=[pltpu.VMEM((CHUNK, kw), F32), pltpu.VMEM((CHUNK, kw), F32), pltpu.VMEM((CHUNK, D), F32)],
        compiler_params=_params(1),
    )(proj, proj, proj, proj, proj, posf, posf, invf, sgn, sinks, datt, *deps)


def _bwd_sgu(proj, da, lng, lnb, ws, bst):
    T = proj.shape[0]
    tc = min(T, 512)
    nsteps = T // tc

    def body(u_ref, vs_ref, da_ref, lng_ref, lnb_ref, ws_ref, bst_ref,
             duv_ref, dws_ref, dbs_ref, dlng_ref, dlnb_ref, dvn_s, dgu_s, dmx_sum):
        i = pl.program_id(0)

        @pl.when(i == 0)
        def _():
            dws_ref[...] = jnp.zeros_like(dws_ref)
            dlng_ref[...] = jnp.zeros_like(dlng_ref)
            dlnb_ref[...] = jnp.zeros_like(dlnb_ref)
            dmx_sum[...] = jnp.zeros_like(dmx_sum)

        u, vs, gu, tu, tv, rstd, vhat, vn = _sgu_forward_parts(u_ref, vs_ref, lng_ref, lnb_ref)
        da = da_ref[...].astype(F32)
        for g in range(GROUPS):
            wm = _masked_ws(ws_ref, g)
            cols = slice(g * CHUNK, (g + 1) * CHUNK)
            dws = jnp.zeros((CHUNK, CHUNK), F32)
            dsum = jnp.zeros((CHUNK, CHUNK), F32)
            for c in range(tc // CHUNK):
                rows = slice(c * CHUNK, (c + 1) * CHUNK)
                vn_cg = vn[rows, cols]
                mixed = _nn(wm, vn_cg) + bst_ref[:, g:g + 1]
                dgu_s[rows, cols] = da[rows, cols] * mixed
                dmx = da[rows, cols] * gu[rows, cols]
                dmxb = dmx.astype(BF16)
                dws = dws + _nt(dmxb, vn_cg)
                dsum = dsum + dmx
                dvn_s[rows, cols] = _tn(wm, dmxb)
            dws_ref[g] += dws
            dmx_sum[:, cols] += dsum
        dvn = dvn_s[...]
        dlng_ref[...] += _colsum(dvn * vhat)
        dlnb_ref[...] += _colsum(dvn)
        dvh = dvn * lng_ref[...]
        dgv = rstd * (dvh - jnp.mean(dvh, axis=-1, keepdims=True) - vhat * jnp.mean(dvh * vhat, axis=-1, keepdims=True))
        duv_ref[:, :D] = (dgu_s[...] * _gelu_grad(u, tu)).astype(BF16)
        duv_ref[:, D:] = (dgv * _gelu_grad(vs, tv)).astype(BF16)

        @pl.when(i == nsteps - 1)
        def _():
            row = lax.broadcasted_iota(jnp.int32, (CHUNK, CHUNK), 0)
            col = lax.broadcasted_iota(jnp.int32, (CHUNK, CHUNK), 1)
            for g in range(GROUPS):
                dws_ref[g] = jnp.where(row >= col, dws_ref[g], 0.0)
                dbs_ref[g:g + 1, :] = _colsum(dmx_sum[:, g * CHUNK:(g + 1) * CHUNK].T)

    const2 = lambda i: (0, 0)
    return pl.pallas_call(
        body, name="bwd_sgu", grid=(nsteps,),
        in_specs=[pl.BlockSpec((tc, D), lambda i: (i, 0)), pl.BlockSpec((tc, D), lambda i: (i, 1)),
                  pl.BlockSpec((tc, D), lambda i: (i, 0)), pl.BlockSpec((1, D), const2), pl.BlockSpec((1, D), const2),
                  pl.BlockSpec((GROUPS, CHUNK, CHUNK), lambda i: (0, 0, 0)), pl.BlockSpec((CHUNK, GROUPS), const2)],
        out_specs=[pl.BlockSpec((tc, 2 * D), lambda i: (i, 0)), pl.BlockSpec((GROUPS, CHUNK, CHUNK), lambda i: (0, 0, 0)),
                   pl.BlockSpec((GROUPS, CHUNK), const2), pl.BlockSpec((1, D), const2), pl.BlockSpec((1, D), const2)],
        out_shape=[SDS((T, 2 * D), BF16), SDS((GROUPS, CHUNK, CHUNK), F32), SDS((GROUPS, CHUNK), F32),
                   SDS((1, D), F32), SDS((1, D), F32)],
        scratch_shapes=[pltpu.VMEM((tc, D), F32), pltpu.VMEM((tc, D), F32), pltpu.VMEM((CHUNK, D), F32)],
        compiler_params=_params(1),
    )(proj, proj, da, lng, lnb, ws, bst)


IN_TILE = 512
IN_SEGS = ((0, 4), (4, 2), (6, 1), (7, 4))
IN_TILES = IN_W // IN_TILE


def _seg_specs(tm, row_of, tile_of):
    def spec(first, n):
        return pl.BlockSpec((tm, IN_TILE), lambda a, b: (row_of(a, b), jnp.clip(tile_of(a, b) - first, 0, n - 1)))
    return [spec(first, n) for first, n in IN_SEGS]


def _seg_dot(k, seg_refs, fn):
    for (first, n), ref in zip(IN_SEGS, seg_refs):
        @pl.when((k >= first) & (k < first + n))
        def _(ref=ref):
            fn(ref[...])


def _bwd_in(duv, dq, dkv, dg, win, x, dx1, g0, after=None):
    T = x.shape[0]
    tm = min(T, 512)

    def body(duv_ref, dq_ref, dkv_ref, dg_ref, w_ref, x_ref, dx1_ref, g0_ref, gx_ref, dg0_ref, acc):
        i, k = pl.program_id(0), pl.program_id(1)

        @pl.when((i == 0) & (k == 0))
        def _():
            dg0_ref[...] = jnp.zeros_like(dg0_ref)

        @pl.when(k == 0)
        def _():
            acc[...] = jnp.zeros_like(acc)

        def add(blk):
            acc[...] += _nt(blk, w_ref[...])

        _seg_dot(k, (duv_ref, dq_ref, dkv_ref, dg_ref), add)

        @pl.when(k == IN_TILES - 1)
        def _():
            dh = acc[...]
            r0, xh = _rms_stats(x_ref[...])
            dg0_ref[...] += _colsum(dh * xh)
            gx_ref[...] = dx1_ref[...] + _rms_bwd(dh, xh, r0, g0_ref[...])

    row = lambda i, k: (i, 0)
    const = lambda i, k: (0, 0)
    body, dep_specs, deps = _after(body, 8, after)
    return pl.pallas_call(
        body, name="bwd_in", grid=(T // tm, IN_TILES),
        in_specs=_seg_specs(tm, lambda i, k: i, lambda i, k: k) + [pl.BlockSpec((D, IN_TILE), lambda i, k: (0, k)), pl.BlockSpec((tm, D), row),
                                           pl.BlockSpec((tm, D), row), pl.BlockSpec((1, D), const)] + dep_specs,
        out_specs=[pl.BlockSpec((tm, D), row), pl.BlockSpec((1, D), const)],
        out_shape=[SDS((T, D), F32), SDS((1, D), F32)],
        scratch_shapes=[pltpu.VMEM((tm, D), F32)],
        compiler_params=_params(2),
    )(duv, dq, dkv, dg, win, x, dx1, g0, *deps)


def _wgrad_in(h, duv, dq, dkv, dg):
    T = h.shape[0]
    tt = min(T, 512)

    def body(h_ref, duv_ref, dq_ref, dkv_ref, dg_ref, dw_ref):
        n = pl.program_id(0)

        @pl.when(pl.program_id(1) == 0)
        def _():
            dw_ref[...] = jnp.zeros_like(dw_ref)

        def add(blk):
            dw_ref[...] += _tn(h_ref[...], blk)

        _seg_dot(n, (duv_ref, dq_ref, dkv_ref, dg_ref), add)

    return pl.pallas_call(
        body, name="wgrad_in", grid=(IN_TILES, T // tt),
        in_specs=[pl.BlockSpec((tt, D), lambda n, t: (t, 0))] + _seg_specs(tt, lambda n, t: t, lambda n, t: n),
        out_specs=pl.BlockSpec((D, IN_TILE), lambda n, t: (0, n)),
        out_shape=SDS((D, IN_W), F32),
        compiler_params=_params(2),
    )(h, duv, dq, dkv, dg)


def _place():
    x, y, c = lax.axis_index("x"), lax.axis_index("y"), lax.axis_index("c")
    return x, y, c, 4 * x + 2 * y + c


def _peers(x, y, c):
    out = []
    for mask in range(1, N_DEV):
        px = 1 - x if mask & 4 else x
        py = 1 - y if mask & 2 else y
        pc = 1 - c if mask & 1 else c
        out.append(((px, py, pc), 4 * px + 2 * py + pc))
    return out


def _all_to_all(arrays, gather, name):
    n = len(arrays)

    def body(*refs):
        ins, outs = refs[:n], refs[n:2 * n]
        send_sems, recv_sems, local_sems = refs[2 * n:]
        x, y, c, me = _place()
        local, sends, recvs = [], [], []
        for a in range(n):
            src_own = ins[a] if gather[a] else ins[a].at[me]
            local.append(pltpu.make_async_copy(src_own, outs[a].at[me], local_sems.at[a]))
            for k, (peer, pid) in enumerate(_peers(x, y, c)):
                sem = a * (N_DEV - 1) + k
                src = ins[a] if gather[a] else ins[a].at[pid]
                sends.append(pltpu.make_async_remote_copy(
                    src_ref=src, dst_ref=outs[a].at[me], send_sem=send_sems.at[sem], recv_sem=recv_sems.at[sem],
                    device_id=peer, device_id_type=MESH))
                recvs.append(pltpu.make_async_remote_copy(
                    src_ref=src, dst_ref=outs[a].at[pid], send_sem=send_sems.at[sem], recv_sem=recv_sems.at[sem],
                    device_id=peer, device_id_type=MESH))
        for cp in local + sends:
            cp.start()
        for cp in recvs:
            cp.wait_recv()
        for cp in sends:
            cp.wait_send()
        for cp in local:
            cp.wait()

    out_shape = [SDS((N_DEV,) + a.shape if gt else a.shape, a.dtype) for a, gt in zip(arrays, gather)]
    nsem = n * (N_DEV - 1)
    return pl.pallas_call(
        body, name=name,
        in_specs=[pl.BlockSpec(memory_space=pl.ANY)] * n,
        out_specs=[pl.BlockSpec(memory_space=pl.ANY)] * n,
        out_shape=out_shape,
        scratch_shapes=[pltpu.SemaphoreType.DMA((nsem,)), pltpu.SemaphoreType.DMA((nsem,)), pltpu.SemaphoreType.DMA((n,))],
    )(*arrays)


_HBM = pl.BlockSpec(memory_space=pltpu.HBM)
_SEM = pl.BlockSpec(memory_space=pltpu.SEMAPHORE)
_EFFECT = pltpu.SideEffectType.DATAFLOW_SIDE_EFFECTING
GATHER = "gather"
SCATTER = "scatter"
SPREAD = "spread"


def _zone_shape(a, mode):
    if mode == GATHER:
        return (N_DEV,) + a.shape
    return (N_DEV - 1,) + (a.shape[1:] if mode == SCATTER else a.shape)


def _start_copies(arrays, modes, name, after=None):
    n = len(arrays)
    zones = [lax.empty(_zone_shape(a, m), a.dtype) for a, m in zip(arrays, modes)]

    def body(*refs):
        ins, lands = refs[:n], refs[n:2 * n]
        send_sems, recv_sems = refs[-2 * n - 3], refs[-2 * n - 2]
        token = refs[-1]
        x, y, c, me = _place()
        for a in range(n):
            for k, (peer, pid) in enumerate(_peers(x, y, c)):
                src = ins[a].at[pid] if modes[a] == SCATTER else ins[a]
                dst = lands[a].at[me] if modes[a] == GATHER else lands[a].at[k]
                pltpu.make_async_remote_copy(src_ref=src, dst_ref=dst, send_sem=send_sems.at[a], recv_sem=recv_sems.at[a],
                                             device_id=peer, device_id_type=MESH).start()
        token[...] = jnp.zeros_like(token)

    hbm = lambda a: pltpu.HBM(a.shape, a.dtype)
    sems = pltpu.SemaphoreType.DMA((n,))
    extra = [] if after is None else [after]
    operands = [pltpu.with_memory_space_constraint(a, pltpu.HBM) for a in list(arrays) + zones]
    res = pl.pallas_call(
        body, name=name,
        out_shape=(sems, sems, *[hbm(a) for a in arrays], *[hbm(z) for z in zones], SDS((8, 128), F32)),
        in_specs=[_HBM] * (2 * n) + [_ANY] * len(extra),
        out_specs=(_SEM, _SEM, *[_HBM] * (2 * n), pl.BlockSpec(memory_space=pltpu.VMEM)),
        input_output_aliases={i: 2 + i for i in range(2 * n)},
        compiler_params=pltpu.CompilerParams(has_side_effects=_EFFECT),
    )(*operands, *extra)
    return res[0], res[1], list(res[2:2 + n]), list(res[2 + n:2 + 2 * n]), res[-1]


def _wait_copies(started, after, name):
    send_sems, recv_sems, thru, zones, _ = started
    n = len(thru)

    def body(*refs):
        lands = refs[n:2 * n]
        send_ref, recv_ref = refs[2 * n], refs[2 * n + 1]
        x, y, c, _ = _place()
        for a in range(n):
            seven = lands[a].at[pl.ds(0, N_DEV - 1)]
            cp = pltpu.make_async_remote_copy(src_ref=seven, dst_ref=seven, send_sem=send_ref.at[a], recv_sem=recv_ref.at[a],
                                              device_id=(x, y, 1 - c), device_id_type=MESH)
            cp.wait_send()
            cp.wait_recv()

    hbm = lambda a: pltpu.HBM(a.shape, a.dtype)
    res = pl.pallas_call(
        body, name=name,
        out_shape=tuple(hbm(a) for a in thru + zones),
        in_specs=[_HBM] * (2 * n) + [_SEM, _SEM, _ANY],
        out_specs=tuple([_HBM] * (2 * n)),
        input_output_aliases={i: i for i in range(2 * n)},
        compiler_params=pltpu.CompilerParams(has_side_effects=_EFFECT),
    )(*thru, *zones, send_sems, recv_sems, after)
    return list(res[:n]), list(res[n:])


def _adamw_math(g, w, m, v):
    m2 = ADAM_B1 * m + (1.0 - ADAM_B1) * g
    v2 = ADAM_B2 * v + (1.0 - ADAM_B2) * (g * g)
    m_hat = m2 / (1.0 - ADAM_B1 ** ADAM_STEP)
    v_hat = v2 / (1.0 - ADAM_B2 ** ADAM_STEP)
    delta = -ADAM_LR * (m_hat / (jnp.sqrt(v_hat) + ADAM_EPS) + ADAM_WD * w)
    return delta, m2, v2


def _sum_adamw(parts, w, m, v, name):
    R, C = w.shape
    tr = max(t for t in (128, 64, 32, 16, 8) if R % t == 0)

    def body(p_ref, w_ref, m_ref, v_ref, g_ref, d_ref, m2_ref, v2_ref):
        g = p_ref[0]
        for k in range(1, N_DEV):
            g = g + p_ref[k]
        g_ref[...] = g
        d_ref[...], m2_ref[...], v2_ref[...] = _adamw_math(g, w_ref[...], m_ref[...], v_ref[...])

    blk = pl.BlockSpec((tr, C), lambda i: (i, 0))
    return pl.pallas_call(
        body, name=name, grid=(R // tr,),
        in_specs=[pl.BlockSpec((N_DEV, tr, C), lambda i: (0, i, 0)), blk, blk, blk],
        out_specs=[blk] * 4,
        out_shape=[SDS((R, C), F32)] * 4,
        compiler_params=_params(1),
    )(parts, w, m, v)


def _sum_adamw_peers(me, own, parts, w, m, v, name, replicated):
    R, C = w.shape
    tr = max(t for t in (128, 64, 32, 16, 8) if R % t == 0)

    def body(me_ref, own_ref, p_ref, w_ref, m_ref, v_ref, g_ref, d_ref, m2_ref, v2_ref):
        if replicated:
            mine = me_ref[0]
            g = None
            for j in range(N_DEV):
                k = jnp.maximum(jnp.bitwise_xor(mine, j) - 1, 0)
                term = jnp.where(mine == j, own_ref[...], p_ref[k])
                g = term if g is None else g + term
        else:
            g = own_ref[...]
            for k in range(N_DEV - 1):
                g = g + p_ref[k]
        g_ref[...] = g
        d_ref[...], m2_ref[...], v2_ref[...] = _adamw_math(g, w_ref[...], m_ref[...], v_ref[...])

    blk = pl.BlockSpec((tr, C), lambda i, me_ref: (i, 0))
    own_spec = blk if replicated else pl.BlockSpec((None, tr, C), lambda i, me_ref: (me_ref[0], i, 0))
    return pl.pallas_call(
        body, name=name,
        grid_spec=pltpu.PrefetchScalarGridSpec(
            num_scalar_prefetch=1, grid=(R // tr,),
            in_specs=[own_spec, pl.BlockSpec((N_DEV - 1, tr, C), lambda i, me_ref: (0, i, 0)), blk, blk, blk],
            out_specs=[blk] * 4),
        out_shape=[SDS((R, C), F32)] * 4,
        compiler_params=_params(1),
    )(me, own, parts, w, m, v)


SMALL = ("ln_v_gain", "ln_v_bias", "w_spatial", "b_spatial", "sinks", "norm_mix_post", "norm_ff_pre", "norm_ff_post")
SMALL_ROWS = {"ln_v_gain": 8, "ln_v_bias": 8, "w_spatial": 1024, "b_spatial": 8, "sinks": 8,
              "norm_mix_post": 8, "norm_ff_pre": 8, "norm_ff_post": 8}
SMALL_PACK_ROWS = 1152


def _pack_small(vals):
    rows = []
    for name in SMALL:
        flat = vals[name].reshape(-1)
        pad = SMALL_ROWS[name] * 128 - flat.shape[0]
        if pad:
            flat = jnp.concatenate([flat, jnp.zeros((pad,), F32)])
        rows.append(flat.reshape(SMALL_ROWS[name], 128))
    rows.append(jnp.zeros((SMALL_PACK_ROWS - sum(SMALL_ROWS.values()), 128), F32))
    return jnp.concatenate(rows, axis=0)


def _unpack_small(packed, shapes):
    out, r = {}, 0
    for name in SMALL:
        n = 1
        for s in shapes[name]:
            n *= s
        out[name] = packed[r:r + SMALL_ROWS[name]].reshape(-1)[:n].reshape(shapes[name])
        r += SMALL_ROWS[name]
    return out


def _rope_rows():
    d = jnp.arange(128) % HEAD
    inv = ROPE_THETA ** (-(2.0 * (d % (ROPE // 2))).astype(F32) / ROPE)
    invf = jnp.where(d < ROPE, inv, 0.0).astype(F32).reshape(1, 128)
    sgn = jnp.where(d < ROPE // 2, -1.0, jnp.where(d < ROPE, 1.0, 0.0)).astype(F32).reshape(1, 128)
    return invf, sgn


def kernel(x, positions, w_in, ln_v_gain, ln_v_bias, w_spatial, b_spatial, sinks, w_a, w_b, w_o, norm_mix_pre, norm_mix_post, w_ff_in, w_ff_out, norm_ff_pre, norm_ff_post, loss_target, m_w_in, m_ln_v_gain, m_ln_v_bias, m_w_spatial, m_b_spatial, m_sinks, m_w_a, m_w_b, m_w_o, m_norm_mix_pre, m_norm_mix_post, m_w_ff_in, m_w_ff_out, m_norm_ff_pre, m_norm_ff_post, v_w_in, v_ln_v_gain, v_ln_v_bias, v_w_spatial, v_b_spatial, v_sinks, v_w_a, v_w_b, v_w_o, v_norm_mix_pre, v_norm_mix_post, v_w_ff_in, v_w_ff_out, v_norm_ff_pre, v_norm_ff_post):
    given = dict(locals())
    T = x.shape[1]
    xt = x[0]
    tgt = loss_target[0]
    posf = positions.astype(F32).reshape(T, 1)
    invf, sgn = _rope_rows()
    bst = b_spatial[0].T
    ws = w_spatial[0]

    me = 4 * lax.axis_index("x") + 2 * lax.axis_index("y") + lax.axis_index("c")
    me_arr = me.astype(jnp.int32).reshape(1)

    def with_own(zone, shard):
        return lax.dynamic_update_slice(zone, shard[None], (me,) + (0,) * shard.ndim)

    rest = ("w_a", "w_b", "w_o", "w_ff_in", "w_ff_out")
    shard = {n: given[n][0].astype(BF16) for n in ("w_in",) + rest}
    g_in = _start_copies([shard["w_in"]], [GATHER], "gather_in_start")
    g_rest = _start_copies([shard[n] for n in rest], [GATHER] * len(rest), "gather_rest_start", after=g_in[-1])
    (own_win,), (win8,) = _wait_copies(g_in, g_rest[-1], "gather_in_wait")
    win = jnp.transpose(with_own(win8, own_win), (1, 0, 2)).reshape(D, IN_W)

    proj, h = _fwd_in(xt, norm_mix_pre, win)
    a = _fwd_sgu(proj, ln_v_gain, ln_v_bias, ws, bst)
    att = _fwd_attn(proj, posf, invf, sgn, sinks[0])
    gw = {n: with_own(z, own) for n, own, z in zip(rest, *_wait_copies(g_rest, att, "gather_rest_wait"))}
    wa, wb, wo = (gw[n].reshape(D, D) for n in ("w_a", "w_b", "w_o"))
    wfi3 = gw["w_ff_in"]
    wfo = gw["w_ff_out"].reshape(D_FF, D)
    merged, a2, b2, mix, x1, hf = _fwd_mix(a, att, proj, xt, wa, wb, wo, norm_mix_post, norm_ff_pre)
    f, dy, dff, dg3, loss_part = _fwd_ff(hf, wfi3, wfo, x1, tgt, norm_ff_post)

    df, dx1, dmix, dg2, dg1 = _bwd_ff(dff, f, wfi3, wfo, x1, dy, mix, norm_mix_post, norm_ff_pre)
    dwfi3, dwfo = _wgrad_ff(hf, df, f, dff)
    own_ff = [dwfi3, dwfo.reshape(N_DEV, D_FF // N_DEV, D)]
    x_ff = _start_copies(own_ff, [SCATTER] * 2, "exchange_ff_start")
    da2, db2, dgate, da, datt = _bwd_mix(dmix, proj, a2, b2, wo, wa, wb, after=x_ff[-1])
    dwo, dwa, dwb = _wgrad_mix(merged, dmix, a, da2, att, db2)
    own_mix = [g.reshape(N_DEV, D // N_DEV, D) for g in (dwa, dwb, dwo)]
    x_mix = _start_copies(own_mix, [SCATTER] * 3, "exchange_mix_start")
    dq, dkv, dsink = _bwd_attn(proj, posf, invf, sgn, sinks[0], datt, after=x_mix[-1])
    duv, dws, dbs, dlng, dlnb = _bwd_sgu(proj, da, ln_v_gain, ln_v_bias, ws, bst)
    dwin = _wgrad_in(h, duv, dq, dkv, dgate)
    small_grads = {"ln_v_gain": dlng, "ln_v_bias": dlnb, "w_spatial": dws, "b_spatial": dbs, "sinks": dsink[:, :N_Q],
                   "norm_mix_post": dg1, "norm_ff_pre": dg2, "norm_ff_post": dg3}
    own_in = [jnp.transpose(dwin.reshape(D, N_DEV, IN_W // N_DEV), (1, 0, 2)), _pack_small(small_grads)]
    x_in = _start_copies(own_in, [SCATTER, SPREAD], "exchange_in_start")
    grad_x, dg0 = _bwd_in(duv, dq, dkv, dgate, win, xt, dx1, norm_mix_pre, after=x_in[-1])
    (dg0_all,) = _all_to_all([dg0.reshape(8, 128)], [True], "exchange_tail")

    results = {}

    def update(n, own, parts):
        results[n] = [r.reshape(given[n].shape) for r in _sum_adamw_peers(
            me_arr, own, parts, given[n][0], given["m_" + n][0], given["v_" + n][0], "adamw_" + n, False)]

    own_ff, p_ff = _wait_copies(x_ff, dg0_all, "exchange_ff_wait")
    update("w_ff_in", own_ff[0], p_ff[0])
    update("w_ff_out", own_ff[1], p_ff[1])
    own_mix, p_mix = _wait_copies(x_mix, p_ff[0], "exchange_mix_wait")
    for n, own, parts in zip(("w_a", "w_b", "w_o"), own_mix, p_mix):
        update(n, own, parts)
    own_in, p_in = _wait_copies(x_in, p_mix[0], "exchange_in_wait")
    update("w_in", own_in[0], p_in[0])
    packed = _sum_adamw_peers(me_arr, own_in[1], p_in[1], _pack_small({n: given[n] for n in SMALL}),
                              _pack_small({n: given["m_" + n] for n in SMALL}),
                              _pack_small({n: given["v_" + n] for n in SMALL}), "adamw_small", True)
    shapes = {n: given[n].shape for n in SMALL}
    unpacked = [_unpack_small(p, shapes) for p in packed]
    for n in SMALL:
        results[n] = [u[n] for u in unpacked]
    n = "norm_mix_pre"
    results[n] = [r.reshape(given[n].shape) for r in _sum_adamw(
        dg0_all, given[n].reshape(8, 128), given["m_" + n].reshape(8, 128), given["v_" + n].reshape(8, 128), "adamw_" + n)]

    loss = lax.psum(loss_part[0, 0], ("x", "y", "c"))
    order = ("w_in", "ln_v_gain", "ln_v_bias", "w_spatial", "b_spatial", "sinks", "w_a", "w_b", "w_o", "norm_mix_pre",
             "norm_mix_post", "w_ff_in", "w_ff_out", "norm_ff_pre", "norm_ff_post")
    out = [loss, grad_x.reshape(x.shape)]
    for k in range(4):
        out += [results[n][k] for n in order]
    return tuple(out)
```

```python
import functools

import jax
import jax.numpy as jnp
from jax import lax
from jax.experimental import pallas as pl
from jax.experimental.pallas import tpu as pltpu

F32 = jnp.float32
BF16 = jnp.bfloat16

N_DEV = 8
D = 1024
D_FF = 4096
IN_W = 5632
CHUNK = 128
GROUPS = 8
HEAD = 64
N_Q = 16
N_KV = 4
ROPE = 16
ROPE_THETA = 500000.0
EPS = 1e-6
OFF_Q, OFF_K, OFF_VA, OFF_GA, OFF_GB = 2048, 3072, 3328, 3584, 4608

ADAM_LR = 0.001
ADAM_B1 = 0.9
ADAM_B2 = 0.999
ADAM_EPS = 1e-08
ADAM_WD = 0.01
ADAM_STEP = 10

VMEM_LIMIT = 56 * 1024 * 1024

SDS = jax.ShapeDtypeStruct
MESH = pl.DeviceIdType.MESH


def _params(n_axes=None):
    if n_axes is None:
        return pltpu.CompilerParams(vmem_limit_bytes=VMEM_LIMIT)
    return pltpu.CompilerParams(dimension_semantics=("arbitrary",) * n_axes, vmem_limit_bytes=VMEM_LIMIT)


def _nt(a, b):
    return lax.dot_general(a, b, (((1,), (1,)), ((), ())), preferred_element_type=F32)


def _tn(a, b):
    return lax.dot_general(a, b, (((0,), (0,)), ((), ())), preferred_element_type=F32)


def _nn(a, b):
    return jnp.dot(a, b, preferred_element_type=F32)


def _gelu(x):
    t = jnp.tanh(0.7978845608028654 * (x + 0.044715 * (x * x * x)))
    return 0.5 * x * (1.0 + t), t


def _gelu_grad(x, t):
    return 0.5 * (1.0 + t) + 0.5 * x * (1.0 - t * t) * (0.7978845608028654 * (1.0 + 3.0 * 0.044715 * x * x))


def _sigmoid(x):
    return 1.0 / (1.0 + jnp.exp(-x))


def _rms_stats(v):
    r = lax.rsqrt(jnp.mean(v * v, axis=-1, keepdims=True) + EPS)
    return r, v * r


def _rms_bwd(d, vhat, r, g):
    gd = g * d
    return r * (gd - vhat * jnp.mean(gd * vhat, axis=-1, keepdims=True))


def _colsum(v):
    return jnp.sum(v, axis=0, keepdims=True)


_ANY = pl.BlockSpec(memory_space=pl.ANY)


def _after(body, n_in, after):
    if after is None:
        return body, [], []

    def ordered(*refs):
        return body(*refs[:n_in], *refs[n_in + 1:])

    return ordered, [_ANY], [after]


def _fwd_in(x, g0, win):
    T = x.shape[0]
    tm, tn = min(T, 1024), 1408

    def body(x_ref, g_ref, w_ref, p_ref, h_ref):
        @pl.when(pl.program_id(1) == 0)
        def _():
            _, xh = _rms_stats(x_ref[...])
            h_ref[...] = (xh * g_ref[...]).astype(BF16)

        p_ref[...] = _nn(h_ref[...], w_ref[...]).astype(BF16)

    return pl.pallas_call(
        body, name="fwd_in", grid=(T // tm, IN_W // tn),
        in_specs=[pl.BlockSpec((tm, D), lambda i, j: (i, 0)), pl.BlockSpec((1, D), lambda i, j: (0, 0)),
                  pl.BlockSpec((D, tn), lambda i, j: (0, j))],
        out_specs=[pl.BlockSpec((tm, tn), lambda i, j: (i, j)), pl.BlockSpec((tm, D), lambda i, j: (i, 0))],
        out_shape=[SDS((T, IN_W), BF16), SDS((T, D), BF16)],
        compiler_params=_params(2),
    )(x, g0, win)


def _sgu_forward_parts(u_ref, vs_ref, lng_ref, lnb_ref):
    u = u_ref[...].astype(F32)
    vs = vs_ref[...].astype(F32)
    gu, tu = _gelu(u)
    gv, tv = _gelu(vs)
    mu = jnp.mean(gv, axis=-1, keepdims=True)
    dv = gv - mu
    rstd = lax.rsqrt(jnp.mean(dv * dv, axis=-1, keepdims=True) + EPS)
    vhat = dv * rstd
    vn = (vhat * lng_ref[...] + lnb_ref[...]).astype(BF16)
    return u, vs, gu, tu, tv, rstd, vhat, vn


def _masked_ws(ws_ref, g):
    row = lax.broadcasted_iota(jnp.int32, (CHUNK, CHUNK), 0)
    col = lax.broadcasted_iota(jnp.int32, (CHUNK, CHUNK), 1)
    return jnp.where(row >= col, ws_ref[g], 0.0).astype(BF16)


def _fwd_sgu(proj, lng, lnb, ws, bst):
    T = proj.shape[0]
    tc = min(T, 512)

    def body(u_ref, vs_ref, lng_ref, lnb_ref, ws_ref, bst_ref, a_ref):
        _, _, gu, _, _, _, _, vn = _sgu_forward_parts(u_ref, vs_ref, lng_ref, lnb_ref)
        for g in range(GROUPS):
            wm = _masked_ws(ws_ref, g)
            cols = slice(g * CHUNK, (g + 1) * CHUNK)
            for c in range(tc // CHUNK):
                rows = slice(c * CHUNK, (c + 1) * CHUNK)
                mixed = _nn(wm, vn[rows, cols]) + bst_ref[:, g:g + 1]
                a_ref[rows, cols] = (gu[rows, cols] * mixed).astype(BF16)

    return pl.pallas_call(
        body, name="fwd_sgu", grid=(T // tc,),
        in_specs=[pl.BlockSpec((tc, D), lambda i: (i, 0)), pl.BlockSpec((tc, D), lambda i: (i, 1)),
                  pl.BlockSpec((1, D), lambda i: (0, 0)), pl.BlockSpec((1, D), lambda i: (0, 0)),
                  pl.BlockSpec((GROUPS, CHUNK, CHUNK), lambda i: (0, 0, 0)), pl.BlockSpec((CHUNK, GROUPS), lambda i: (0, 0))],
        out_specs=pl.BlockSpec((tc, D), lambda i: (i, 0)),
        out_shape=SDS((T, D), BF16),
        compiler_params=_params(1),
    )(proj, proj, lng, lnb, ws, bst)


def _rope_tables(pos, invf, sgn, reps):
    ang = pos * invf
    c = jnp.cos(ang)
    s = jnp.sin(ang) * sgn
    if reps > 1:
        c = jnp.tile(c, (1, reps))
        s = jnp.tile(s, (1, reps))
    return c, s


def _swap_halves(v):
    n = v.shape[1]
    d = lax.broadcasted_iota(jnp.int32, v.shape, 1) % HEAD
    upper = jnp.where(d < ROPE, pltpu.roll(v, ROPE // 2, 1), 0.0)
    return jnp.where(d < ROPE // 2, pltpu.roll(v, n - ROPE // 2, 1), upper)


def _rope(v, c, s):
    return v * c + _swap_halves(v) * s


def _rope_bwd(dv, c, s):
    return dv * c + _swap_halves(dv * s)


def _band_mask(first, key_axis):
    shape = (CHUNK, 2 * CHUNK) if key_axis == 1 else (2 * CHUNK, CHUNK)
    t = lax.broadcasted_iota(jnp.int32, shape, 1 - key_axis)
    j = lax.broadcasted_iota(jnp.int32, shape, key_axis)
    return (j > t) & (j <= t + CHUNK) & (jnp.logical_not(first) | (j >= CHUNK))


def _softmax_sink(s, sink, key_axis):
    m = jnp.maximum(jnp.max(s, axis=key_axis, keepdims=True), sink)
    p = jnp.exp(s - m)
    esink = jnp.exp(sink - m)
    inv = 1.0 / (jnp.sum(p, axis=key_axis, keepdims=True) + esink)
    return p * inv, esink * inv


def _head_pair_operand(band, g):
    slab = band[:, (g // 2) * 128:(g // 2 + 1) * 128]
    lo = lax.broadcasted_iota(jnp.int32, slab.shape, 1) < HEAD
    if g % 2 == 0:
        first = jnp.where(lo, slab, 0.0)
        second = pltpu.roll(first, HEAD, 1)
    else:
        second = jnp.where(lo, 0.0, slab)
        first = pltpu.roll(second, HEAD, 1)
    return jnp.concatenate([first, second], axis=0).astype(BF16)


def _head_pair_gradient(acc, g):
    top, bot = acc[:2 * CHUNK], acc[2 * CHUNK:]
    lo = lax.broadcasted_iota(jnp.int32, top.shape, 1) < HEAD
    if g % 2 == 0:
        return jnp.where(lo, top, 0.0) + pltpu.roll(jnp.where(lo, 0.0, bot), HEAD, 1)
    return pltpu.roll(jnp.where(lo, top, 0.0), HEAD, 1) + jnp.where(lo, 0.0, bot)


def _attn_specs(nb, clamp):
    cur = (lambda i: jnp.minimum(i, nb - 1)) if clamp else (lambda i: i)
    prev = lambda i: jnp.maximum(jnp.minimum(i, nb - 1) - 1, 0)
    kw = N_KV * HEAD
    return cur, prev, [
        pl.BlockSpec((CHUNK, D), lambda i: (cur(i), OFF_Q // D)),
        pl.BlockSpec((CHUNK, kw), lambda i: (prev(i), OFF_K // kw)),
        pl.BlockSpec((CHUNK, kw), lambda i: (cur(i), OFF_K // kw)),
        pl.BlockSpec((CHUNK, kw), lambda i: (prev(i), OFF_VA // kw)),
        pl.BlockSpec((CHUNK, kw), lambda i: (cur(i), OFF_VA // kw)),
        pl.BlockSpec((CHUNK, 1), lambda i: (prev(i), 0)),
        pl.BlockSpec((CHUNK, 1), lambda i: (cur(i), 0)),
        pl.BlockSpec((1, 128), lambda i: (0, 0)),
        pl.BlockSpec((1, 128), lambda i: (0, 0)),
        pl.BlockSpec(memory_space=pltpu.SMEM),
    ]


def _attn_load(q_ref, kp_ref, kc_ref, vp_ref, vc_ref, pp_ref, pc_ref, invf_ref, sgn_ref):
    cq, sq = _rope_tables(pc_ref[...], invf_ref[...], sgn_ref[...], D // 128)
    pos_b = jnp.concatenate([pp_ref[...], pc_ref[...]], axis=0)
    ck, sk = _rope_tables(pos_b, invf_ref[...], sgn_ref[...], N_KV * HEAD // 128)
    q = _rope(q_ref[...].astype(F32), cq, sq).astype(BF16)
    kb = _rope(jnp.concatenate([kp_ref[...], kc_ref[...]], axis=0).astype(F32), ck, sk)
    vb = jnp.concatenate([vp_ref[...], vc_ref[...]], axis=0).astype(F32)
    return q, kb, vb, (cq, sq, ck, sk)


PAIRS_PER_KV = N_Q // N_KV // 2


def _fwd_attn(proj, posf, invf, sgn, sinks):
    T = proj.shape[0]
    nb = T // CHUNK
    _, _, specs = _attn_specs(nb, False)

    def body(q_ref, kp_ref, kc_ref, vp_ref, vc_ref, pp_ref, pc_ref, invf_ref, sgn_ref, sink_ref, o_ref):
        q, kb, vb, _ = _attn_load(q_ref, kp_ref, kc_ref, vp_ref, vc_ref, pp_ref, pc_ref, invf_ref, sgn_ref)
        mask = _band_mask(pl.program_id(0) == 0, 0)
        for g in range(N_KV):
            k2 = _head_pair_operand(kb, g)
            v2 = _head_pair_operand(vb, g)
            for r in range(PAIRS_PER_KV):
                pair = g * PAIRS_PER_KV + r
                s2 = _nt(k2, q[:, pair * 128:(pair + 1) * 128]) * (HEAD ** -0.5)
                ps = []
                for e in range(2):
                    s = jnp.where(mask, s2[e * 2 * CHUNK:(e + 1) * 2 * CHUNK], -1e30)
                    ps.append(_softmax_sink(s, sink_ref[2 * pair + e], 0)[0].astype(BF16))
                o_ref[:, pair * 128:(pair + 1) * 128] = _tn(jnp.concatenate(ps, axis=0), v2).astype(BF16)

    return pl.pallas_call(
        body, name="fwd_attn", grid=(nb,), in_specs=specs,
        out_specs=pl.BlockSpec((CHUNK, D), lambda i: (i, 0)),
        out_shape=SDS((T, D), BF16),
        compiler_params=_params(1),
    )(proj, proj, proj, proj, proj, posf, posf, invf, sgn, sinks)


def _fwd_mix(a, att, proj, x, wa, wb, wo, g1, g2):
    T = x.shape[0]
    tm = min(T, 512)
    half = D // 2

    def body(a_ref, att_ref, ga0, ga1, gb0, gb1, x_ref, wa_ref, wb_ref, wo_ref, g1_ref, g2_ref,
             mg_ref, a2_ref, b2_ref, mix_ref, x1_ref, hf_ref):
        a2 = _nn(a_ref[...], wa_ref[...])
        b2 = _nn(att_ref[...], wb_ref[...])
        ga = jnp.concatenate([ga0[...], ga1[...]], axis=1).astype(F32)
        gb = jnp.concatenate([gb0[...], gb1[...]], axis=1).astype(F32)
        merged = (_sigmoid(ga) * a2 + _sigmoid(gb) * b2).astype(BF16)
        a2_ref[...] = a2.astype(BF16)
        b2_ref[...] = b2.astype(BF16)
        mg_ref[...] = merged
        mix = _nn(merged, wo_ref[...])
        mix_ref[...] = mix
        _, mh = _rms_stats(mix)
        x1 = x_ref[...] + mh * g1_ref[...]
        x1_ref[...] = x1
        _, xh = _rms_stats(x1)
        hf_ref[...] = (xh * g2_ref[...]).astype(BF16)

    row = lambda i: (i, 0)
    const = lambda i: (0, 0)
    gspec = lambda off: pl.BlockSpec((tm, half), lambda i: (i, off // half))
    return pl.pallas_call(
        body, name="fwd_mix", grid=(T // tm,),
        in_specs=[pl.BlockSpec((tm, D), row), pl.BlockSpec((tm, D), row),
                  gspec(OFF_GA), gspec(OFF_GA + half), gspec(OFF_GB), gspec(OFF_GB + half),
                  pl.BlockSpec((tm, D), row), _resident((D, D)), _resident((D, D)),
                  _resident((D, D)), pl.BlockSpec((1, D), const), pl.BlockSpec((1, D), const)],
        out_specs=[pl.BlockSpec((tm, D), row)] * 6,
        out_shape=[SDS((T, D), BF16), SDS((T, D), BF16), SDS((T, D), BF16), SDS((T, D), F32), SDS((T, D), F32),
                   SDS((T, D), BF16)],
        compiler_params=_params(1),
    )(a, att, proj, proj, proj, proj, x, wa, wb, wo, g1, g2)


FF_SPLIT = N_DEV
FF_TILE = D_FF // FF_SPLIT
FF_STEP = 2048
FF_SLABS = FF_STEP // FF_TILE
FF_STEPS = D_FF // FF_STEP


def _fwd_ff(hf, wfi3, wfo, x1, tgt, g3):
    T = hf.shape[0]
    tm = min(T, 512)
    last = FF_STEPS - 1

    def body(hf_ref, wfi_ref, wfo_ref, x1_ref, tgt_ref, g3_ref, f_ref, dy_ref, dff_ref, dg3_ref, loss_ref, acc, r_s):
        i, p = pl.program_id(0), pl.program_id(1)

        @pl.when((i == 0) & (p == 0))
        def _():
            dg3_ref[...] = jnp.zeros_like(dg3_ref)
            loss_ref[...] = jnp.zeros_like(loss_ref)

        hf_t = hf_ref[...]
        for s in range(FF_SLABS):
            cols = slice(s * FF_TILE, (s + 1) * FF_TILE)
            f = _nn(hf_t, wfi_ref[s]).astype(BF16)
            f_ref[:, cols] = f
            rl = jnp.maximum(f.astype(F32), 0.0)
            r_s[:, cols] = (rl * rl).astype(BF16)
        part = _nn(r_s[...], wfo_ref[...])

        @pl.when(p == 0)
        def _():
            acc[...] = part

        @pl.when(p > 0)
        def _():
            acc[...] += part

        @pl.when(p == last)
        def _():
            r3, fh = _rms_stats(acc[...])
            e = x1_ref[...] + fh * g3_ref[...] - tgt_ref[...]
            loss_ref[...] += jnp.sum(e * e) * (0.5 / D)
            dy = e * (1.0 / D)
            dy_ref[...] = dy
            dg3_ref[...] += _colsum(dy * fh)
            dff_ref[...] = _rms_bwd(dy, fh, r3, g3_ref[...]).astype(BF16)

    row = lambda i, p: (i, 0)
    const = lambda i, p: (0, 0)
    return pl.pallas_call(
        body, name="fwd_ff", grid=(T // tm, FF_STEPS),
        in_specs=[pl.BlockSpec((tm, D), row), pl.BlockSpec((FF_SLABS, D, FF_TILE), lambda i, p: (p, 0, 0)),
                  pl.BlockSpec((FF_STEP, D), lambda i, p: (p, 0)), pl.BlockSpec((tm, D), row),
                  pl.BlockSpec((tm, D), row), pl.BlockSpec((1, D), const)],
        out_specs=[pl.BlockSpec((tm, FF_STEP), lambda i, p: (i, p)), pl.BlockSpec((tm, D), row),
                   pl.BlockSpec((tm, D), row), pl.BlockSpec((1, D), const), pl.BlockSpec((1, 128), const)],
        out_shape=[SDS((T, D_FF), BF16), SDS((T, D), F32), SDS((T, D), BF16), SDS((1, D), F32), SDS((1, 128), F32)],
        scratch_shapes=[pltpu.VMEM((tm, D), F32), pltpu.VMEM((tm, FF_STEP), BF16)],
        compiler_params=_params(2),
    )(hf, wfi3, wfo, x1, tgt, g3)


def _bwd_ff(dff, f, wfi3, wfo, x1, dy, mix, g1, g2):
    T = dff.shape[0]
    tm = min(T, 512)
    last = FF_STEPS - 1

    def body(dff_ref, f_ref, wfi_ref, wfo_ref, x1_ref, dy_ref, mix_ref, g1_ref, g2_ref,
             df_ref, dx1_ref, dmix_ref, dg2_ref, dg1_ref, acc):
        i, p = pl.program_id(0), pl.program_id(1)

        @pl.when((i == 0) & (p == 0))
        def _():
            dg2_ref[...] = jnp.zeros_like(dg2_ref)
            dg1_ref[...] = jnp.zeros_like(dg1_ref)

        dr = _nt(dff_ref[...], wfo_ref[...])
        df_ref[...] = (dr * (2.0 * jnp.maximum(f_ref[...].astype(F32), 0.0))).astype(BF16)
        part = _nt(df_ref[:, :FF_TILE], wfi_ref[0])
        for s in range(1, FF_SLABS):
            part = part + _nt(df_ref[:, s * FF_TILE:(s + 1) * FF_TILE], wfi_ref[s])

        @pl.when(p == 0)
        def _():
            acc[...] = part

        @pl.when(p > 0)
        def _():
            acc[...] += part

        @pl.when(p == last)
        def _():
            dhf = acc[...]
            r2, xh = _rms_stats(x1_ref[...])
            dg2_ref[...] += _colsum(dhf * xh)
            dx1 = dy_ref[...] + _rms_bwd(dhf, xh, r2, g2_ref[...])
            dx1_ref[...] = dx1
            r1, mh = _rms_stats(mix_ref[...])
            dg1_ref[...] += _colsum(dx1 * mh)
            dmix_ref[...] = _rms_bwd(dx1, mh, r1, g1_ref[...]).astype(BF16)

    row = lambda i, p: (i, 0)
    const = lambda i, p: (0, 0)
    return pl.pallas_call(
        body, name="bwd_ff", grid=(T // tm, FF_STEPS),
        in_specs=[pl.BlockSpec((tm, D), row), pl.BlockSpec((tm, FF_STEP), lambda i, p: (i, p)),
                  pl.BlockSpec((FF_SLABS, D, FF_TILE), lambda i, p: (p, 0, 0)), pl.BlockSpec((FF_STEP, D), lambda i, p: (p, 0)),
                  pl.BlockSpec((tm, D), row), pl.BlockSpec((tm, D), row), pl.BlockSpec((tm, D), row),
                  pl.BlockSpec((1, D), const), pl.BlockSpec((1, D), const)],
        out_specs=[pl.BlockSpec((tm, FF_STEP), lambda i, p: (i, p)), pl.BlockSpec((tm, D), row),
                   pl.BlockSpec((tm, D), row), pl.BlockSpec((1, D), const), pl.BlockSpec((1, D), const)],
        out_shape=[SDS((T, D_FF), BF16), SDS((T, D), F32), SDS((T, D), BF16), SDS((1, D), F32), SDS((1, D), F32)],
        scratch_shapes=[pltpu.VMEM((tm, D), F32)],
        compiler_params=_params(2),
    )(dff, f, wfi3, wfo, x1, dy, mix, g1, g2)


def _wgrad_ff(hf, df, f, dff):
    T = hf.shape[0]
    tt = min(T, 1024)
    wide = 2 * FF_TILE

    def body(hf_ref, df_ref, f_ref, dff_ref, dwfi_ref, dwfo_ref):
        @pl.when(pl.program_id(1) == 0)
        def _():
            dwfi_ref[...] = jnp.zeros_like(dwfi_ref)
            dwfo_ref[...] = jnp.zeros_like(dwfo_ref)

        both = _tn(hf_ref[...], df_ref[...])
        dwfi_ref[0] += both[:, :FF_TILE]
        dwfi_ref[1] += both[:, FF_TILE:]
        rl = jnp.maximum(f_ref[...].astype(F32), 0.0)
        dwfo_ref[...] += _tn((rl * rl).astype(BF16), dff_ref[...])

    return pl.pallas_call(
        body, name="wgrad_ff", grid=(D_FF // wide, T // tt),
        in_specs=[pl.BlockSpec((tt, D), lambda p, t: (t, 0)), pl.BlockSpec((tt, wide), lambda p, t: (t, p)),
                  pl.BlockSpec((tt, wide), lambda p, t: (t, p)), pl.BlockSpec((tt, D), lambda p, t: (t, 0))],
        out_specs=[pl.BlockSpec((2, D, FF_TILE), lambda p, t: (p, 0, 0)), pl.BlockSpec((wide, D), lambda p, t: (p, 0))],
        out_shape=[SDS((FF_SPLIT, D, FF_TILE), F32), SDS((D_FF, D), F32)],
        compiler_params=_params(2),
    )(hf, df, f, dff)


def _bwd_mix(dmix, proj, a2, b2, wo, wa, wb, after=None):
    T = dmix.shape[0]
    tm = min(T, 512)
    half = D // 2

    def body(dmix_ref, ga0, ga1, gb0, gb1, a2_ref, b2_ref, wo_ref, wa_ref, wb_ref,
             da2_ref, db2_ref, dg_ref, da_ref, datt_ref):
        dmg = _nt(dmix_ref[...], wo_ref[...])
        sa = _sigmoid(jnp.concatenate([ga0[...], ga1[...]], axis=1).astype(F32))
        sb = _sigmoid(jnp.concatenate([gb0[...], gb1[...]], axis=1).astype(F32))
        da2 = (dmg * sa).astype(BF16)
        db2 = (dmg * sb).astype(BF16)
        da2_ref[...] = da2
        db2_ref[...] = db2
        dg_ref[:, :D] = (dmg * a2_ref[...].astype(F32) * (sa * (1.0 - sa))).astype(BF16)
        dg_ref[:, D:] = (dmg * b2_ref[...].astype(F32) * (sb * (1.0 - sb))).astype(BF16)
        da_ref[...] = _nt(da2, wa_ref[...]).astype(BF16)
        datt_ref[...] = _nt(db2, wb_ref[...]).astype(BF16)

    row = lambda i: (i, 0)
    const = lambda i: (0, 0)
    gspec = lambda off: pl.BlockSpec((tm, half), lambda i: (i, off // half))
    body, dep_specs, deps = _after(body, 10, after)
    return pl.pallas_call(
        body, name="bwd_mix", grid=(T // tm,),
        in_specs=[pl.BlockSpec((tm, D), row), gspec(OFF_GA), gspec(OFF_GA + half), gspec(OFF_GB), gspec(OFF_GB + half),
                  pl.BlockSpec((tm, D), row), pl.BlockSpec((tm, D), row),
                  _resident((D, D)), _resident((D, D)), _resident((D, D))] + dep_specs,
        out_specs=[pl.BlockSpec((tm, D), row), pl.BlockSpec((tm, D), row), pl.BlockSpec((tm, 2 * D), row),
                   pl.BlockSpec((tm, D), row), pl.BlockSpec((tm, D), row)],
        out_shape=[SDS((T, D), BF16), SDS((T, D), BF16), SDS((T, 2 * D), BF16), SDS((T, D), BF16), SDS((T, D), BF16)],
        compiler_params=_params(1),
    )(dmix, proj, proj, proj, proj, a2, b2, wo, wa, wb, *deps)


def _wgrad_mix(merged, dmix, a, da2, att, db2):
    T = merged.shape[0]
    tt = min(T, 512)

    def body(mg_ref, dmix_ref, a_ref, da2_ref, att_ref, db2_ref, dwo_ref, dwa_ref, dwb_ref):
        @pl.when(pl.program_id(0) == 0)
        def _():
            dwo_ref[...] = jnp.zeros_like(dwo_ref)
            dwa_ref[...] = jnp.zeros_like(dwa_ref)
            dwb_ref[...] = jnp.zeros_like(dwb_ref)

        dwo_ref[...] += _tn(mg_ref[...], dmix_ref[...])
        dwa_ref[...] += _tn(a_ref[...], da2_ref[...])
        dwb_ref[...] += _tn(att_ref[...], db2_ref[...])

    return pl.pallas_call(
        body, name="wgrad_mix", grid=(T // tt,),
        in_specs=[pl.BlockSpec((tt, D), lambda t: (t, 0))] * 6,
        out_specs=[pl.BlockSpec((D, D), lambda t: (0, 0))] * 3,
        out_shape=[SDS((D, D), F32)] * 3,
        compiler_params=_params(1),
    )(merged, dmix, a, da2, att, db2)


def _bwd_attn(proj, posf, invf, sgn, sinks, datt, after=None):
    T = proj.shape[0]
    nb = T // CHUNK
    kw = N_KV * HEAD
    cur, prev, specs = _attn_specs(nb, True)

    def body(q_ref, kp_ref, kc_ref, vp_ref, vc_ref, pp_ref, pc_ref, invf_ref, sgn_ref, sink_ref, do_ref,
             dq_ref, dkv_ref, dsink_ref, carry_k, carry_v, dq_acc):
        i = pl.program_id(0)

        @pl.when(i == 0)
        def _():
            carry_k[...] = jnp.zeros_like(carry_k)
            carry_v[...] = jnp.zeros_like(carry_v)
            dsink_ref[...] = jnp.zeros_like(dsink_ref)

        @pl.when(i < nb)
        def _():
            q, kb, vb, (cq, sq, ck, sk) = _attn_load(q_ref, kp_ref, kc_ref, vp_ref, vc_ref, pp_ref, pc_ref,
                                                     invf_ref, sgn_ref)
            mask = _band_mask(i == 0, 0)
            do = do_ref[...]
            lane = lax.broadcasted_iota(jnp.int32, (1, 128), 1)
            dsink = jnp.zeros((1, 128), F32)
            dks, dvs = [], []
            for g in range(N_KV):
                k2 = _head_pair_operand(kb, g)
                v2 = _head_pair_operand(vb, g)
                dk2 = jnp.zeros((4 * CHUNK, 128), F32)
                dv2 = jnp.zeros((4 * CHUNK, 128), F32)
                for r in range(PAIRS_PER_KV):
                    pair = g * PAIRS_PER_KV + r
                    qp = q[:, pair * 128:(pair + 1) * 128]
                    dop = do[:, pair * 128:(pair + 1) * 128]
                    s2 = _nt(k2, qp) * (HEAD ** -0.5)
                    dp2 = _nt(v2, dop)
                    ps, dss = [], []
                    for e in range(2):
                        rows = slice(e * 2 * CHUNK, (e + 1) * 2 * CHUNK)
                        p, psink = _softmax_sink(jnp.where(mask, s2[rows], -1e30), sink_ref[2 * pair + e], 0)
                        dp = dp2[rows]
                        delta = jnp.sum(p * dp, axis=0, keepdims=True)
                        ps.append(p.astype(BF16))
                        dss.append((p * (dp - delta) * (HEAD ** -0.5)).astype(BF16))
                        dsink = dsink + jnp.where(lane == 2 * pair + e, -jnp.sum(psink * delta), 0.0)
                    ds2 = jnp.concatenate(dss, axis=0)
                    dq_acc[:, pair * 128:(pair + 1) * 128] = _tn(ds2, k2)
                    dk2 = dk2 + _nn(ds2, qp)
                    dv2 = dv2 + _nn(jnp.concatenate(ps, axis=0), dop)
                dks.append(_head_pair_gradient(dk2, g))
                dvs.append(_head_pair_gradient(dv2, g))
            dsink_ref[...] += dsink
            dq_ref[...] = _rope_bwd(dq_acc[...], cq, sq).astype(BF16)
            dkb = _rope_bwd(jnp.concatenate([dks[0] + dks[1], dks[2] + dks[3]], axis=1), ck, sk)
            dvb = jnp.concatenate([dvs[0] + dvs[1], dvs[2] + dvs[3]], axis=1)
            dkv_ref[:, :kw] = (carry_k[...] + dkb[:CHUNK]).astype(BF16)
            dkv_ref[:, kw:] = (carry_v[...] + dvb[:CHUNK]).astype(BF16)
            carry_k[...] = dkb[CHUNK:]
            carry_v[...] = dvb[CHUNK:]

        @pl.when(i == nb)
        def _():
            dkv_ref[:, :kw] = carry_k[...].astype(BF16)
            dkv_ref[:, kw:] = carry_v[...].astype(BF16)

    body, dep_specs, deps = _after(body, 11, after)
    return pl.pallas_call(
        body, name="bwd_attn", grid=(nb + 1,),
        in_specs=specs + [pl.BlockSpec((CHUNK, D), lambda i: (cur(i), 0))] + dep_specs,
        out_specs=[pl.BlockSpec((CHUNK, D), lambda i: (cur(i), 0)),
                   pl.BlockSpec((CHUNK, 2 * kw), lambda i: (jnp.maximum(i - 1, 0), 0)),
                   pl.BlockSpec((1, 128), lambda i: (0, 0))],
        out_shape=[SDS((T, D), BF16), SDS((T, 2 * kw), BF16), SDS((1, 128), F32)],
        scratch_shapes=[pltpu.VMEM((CHUNK, kw), F32), pltpu.VMEM((CHUNK, kw), F32), pltpu.VMEM((CHUNK, D), F32)],
        compiler_params=_params(1),
    )(proj, proj, proj, proj, proj, posf, posf, invf, sgn, sinks, datt, *deps)


def _bwd_sgu(proj, da, lng, lnb, ws, bst):
    T = proj.shape[0]
    tc = min(T, 512)
    nsteps = T // tc

    def body(u_ref, vs_ref, da_ref, lng_ref, lnb_ref, ws_ref, bst_ref,
             duv_ref, dws_ref, dbs_ref, dlng_ref, dlnb_ref, dvn_s, dgu_s, dmx_sum):
        i = pl.program_id(0)

        @pl.when(i == 0)
        def _():
            dws_ref[...] = jnp.zeros_like(dws_ref)
            dlng_ref[...] = jnp.zeros_like(dlng_ref)
            dlnb_ref[...] = jnp.zeros_like(dlnb_ref)
            dmx_sum[...] = jnp.zeros_like(dmx_sum)

        u, vs, gu, tu, tv, rstd, vhat, vn = _sgu_forward_parts(u_ref, vs_ref, lng_ref, lnb_ref)
        da = da_ref[...].astype(F32)
        for g in range(GROUPS):
            wm = _masked_ws(ws_ref, g)
            cols = slice(g * CHUNK, (g + 1) * CHUNK)
            dws = jnp.zeros((CHUNK, CHUNK), F32)
            dsum = jnp.zeros((CHUNK, CHUNK), F32)
            for c in range(tc // CHUNK):
                rows = slice(c * CHUNK, (c + 1) * CHUNK)
                vn_cg = vn[rows, cols]
                mixed = _nn(wm, vn_cg) + bst_ref[:, g:g + 1]
                dgu_s[rows, cols] = da[rows, cols] * mixed
                dmx = da[rows, cols] * gu[rows, cols]
                dmxb = dmx.astype(BF16)
                dws = dws + _nt(dmxb, vn_cg)
                dsum = dsum + dmx
                dvn_s[rows, cols] = _tn(wm, dmxb)
            dws_ref[g] += dws
            dmx_sum[:, cols] += dsum
        dvn = dvn_s[...]
        dlng_ref[...] += _colsum(dvn * vhat)
        dlnb_ref[...] += _colsum(dvn)
        dvh = dvn * lng_ref[...]
        dgv = rstd * (dvh - jnp.mean(dvh, axis=-1, keepdims=True) - vhat * jnp.mean(dvh * vhat, axis=-1, keepdims=True))
        duv_ref[:, :D] = (dgu_s[...] * _gelu_grad(u, tu)).astype(BF16)
        duv_ref[:, D:] = (dgv * _gelu_grad(vs, tv)).astype(BF16)

        @pl.when(i == nsteps - 1)
        def _():
            row = lax.broadcasted_iota(jnp.int32, (CHUNK, CHUNK), 0)
            col = lax.broadcasted_iota(jnp.int32, (CHUNK, CHUNK), 1)
            for g in range(GROUPS):
                dws_ref[g] = jnp.where(row >= col, dws_ref[g], 0.0)
                dbs_ref[g:g + 1, :] = _colsum(dmx_sum[:, g * CHUNK:(g + 1) * CHUNK].T)

    const2 = lambda i: (0, 0)
    return pl.pallas_call(
        body, name="bwd_sgu", grid=(nsteps,),
        in_specs=[pl.BlockSpec((tc, D), lambda i: (i, 0)), pl.BlockSpec((tc, D), lambda i: (i, 1)),
                  pl.BlockSpec((tc, D), lambda i: (i, 0)), pl.BlockSpec((1, D), const2), pl.BlockSpec((1, D), const2),
                  pl.BlockSpec((GROUPS, CHUNK, CHUNK), lambda i: (0, 0, 0)), pl.BlockSpec((CHUNK, GROUPS), const2)],
        out_specs=[pl.BlockSpec((tc, 2 * D), lambda i: (i, 0)), pl.BlockSpec((GROUPS, CHUNK, CHUNK), lambda i: (0, 0, 0)),
                   pl.BlockSpec((GROUPS, CHUNK), const2), pl.BlockSpec((1, D), const2), pl.BlockSpec((1, D), const2)],
        out_shape=[SDS((T, 2 * D), BF16), SDS((GROUPS, CHUNK, CHUNK), F32), SDS((GROUPS, CHUNK), F32),
                   SDS((1, D), F32), SDS((1, D), F32)],
        scratch_shapes=[pltpu.VMEM((tc, D), F32), pltpu.VMEM((tc, D), F32), pltpu.VMEM((CHUNK, D), F32)],
        compiler_params=_params(1),
    )(proj, proj, da, lng, lnb, ws, bst)


IN_SEG_WIDTHS = (2 * D, D, 2 * N_KV * HEAD, 2 * D)


def _resident(shape):
    return pl.BlockSpec(shape, lambda *_: (0,) * len(shape), pipeline_mode=pl.Buffered(1))


def _bwd_in(duv, dq, dkv, dg, win, x, dx1, g0, after=None):
    T = x.shape[0]
    tm = min(T, 512)

    def body(duv_ref, dq_ref, dkv_ref, dg_ref, w_ref, x_ref, dx1_ref, g0_ref, gx_ref, dg0_ref):
        @pl.when(pl.program_id(0) == 0)
        def _():
            dg0_ref[...] = jnp.zeros_like(dg0_ref)

        dh, off = None, 0
        for ref, width in zip((duv_ref, dq_ref, dkv_ref, dg_ref), IN_SEG_WIDTHS):
            part = _nt(ref[...], w_ref[:, off:off + width])
            dh = part if dh is None else dh + part
            off += width
        r0, xh = _rms_stats(x_ref[...])
        dg0_ref[...] += _colsum(dh * xh)
        gx_ref[...] = dx1_ref[...] + _rms_bwd(dh, xh, r0, g0_ref[...])

    row = lambda i: (i, 0)
    body, dep_specs, deps = _after(body, 8, after)
    return pl.pallas_call(
        body, name="bwd_in", grid=(T // tm,),
        in_specs=[pl.BlockSpec((tm, w), row) for w in IN_SEG_WIDTHS] + [
            _resident((D, IN_W)), pl.BlockSpec((tm, D), row), pl.BlockSpec((tm, D), row),
            pl.BlockSpec((1, D), lambda i: (0, 0))] + dep_specs,
        out_specs=[pl.BlockSpec((tm, D), row), pl.BlockSpec((1, D), lambda i: (0, 0))],
        out_shape=[SDS((T, D), F32), SDS((1, D), F32)],
        compiler_params=_params(1),
    )(duv, dq, dkv, dg, win, x, dx1, g0, *deps)


def _wgrad_cols(h, segs, name):
    T = h.shape[0]
    tt = min(T, 1024)
    widths = [s.shape[1] for s in segs]

    def body(h_ref, *refs):
        dw_ref = refs[-1]

        @pl.when(pl.program_id(0) == 0)
        def _():
            dw_ref[...] = jnp.zeros_like(dw_ref)

        off = 0
        for ref, width in zip(refs[:-1], widths):
            dw_ref[:, off:off + width] += _tn(h_ref[...], ref[...])
            off += width

    row = lambda t: (t, 0)
    return pl.pallas_call(
        body, name=name, grid=(T // tt,),
        in_specs=[pl.BlockSpec((tt, D), row)] + [pl.BlockSpec((tt, w), row) for w in widths],
        out_specs=pl.BlockSpec((D, sum(widths)), lambda t: (0, 0)),
        out_shape=SDS((D, sum(widths)), F32),
        compiler_params=_params(1),
    )(h, *segs)


def _wgrad_in(h, duv, dq, dkv, dg):
    return jnp.concatenate([_wgrad_cols(h, [duv], "wgrad_in_uv"), _wgrad_cols(h, [dq, dkv], "wgrad_in_qkv"),
                            _wgrad_cols(h, [dg], "wgrad_in_gates")], axis=1)


def _place():
    x, y, c = lax.axis_index("x"), lax.axis_index("y"), lax.axis_index("c")
    return x, y, c, 4 * x + 2 * y + c


def _peers(x, y, c):
    out = []
    for mask in range(1, N_DEV):
        px = 1 - x if mask & 4 else x
        py = 1 - y if mask & 2 else y
        pc = 1 - c if mask & 1 else c
        out.append(((px, py, pc), 4 * px + 2 * py + pc))
    return out


def _all_to_all(arrays, gather, name):
    n = len(arrays)

    def body(*refs):
        ins, outs = refs[:n], refs[n:2 * n]
        send_sems, recv_sems, local_sems = refs[2 * n:]
        x, y, c, me = _place()
        local, sends, recvs = [], [], []
        for a in range(n):
            src_own = ins[a] if gather[a] else ins[a].at[me]
            local.append(pltpu.make_async_copy(src_own, outs[a].at[me], local_sems.at[a]))
            for k, (peer, pid) in enumerate(_peers(x, y, c)):
                sem = a * (N_DEV - 1) + k
                src = ins[a] if gather[a] else ins[a].at[pid]
                sends.append(pltpu.make_async_remote_copy(
                    src_ref=src, dst_ref=outs[a].at[me], send_sem=send_sems.at[sem], recv_sem=recv_sems.at[sem],
                    device_id=peer, device_id_type=MESH))
                recvs.append(pltpu.make_async_remote_copy(
                    src_ref=src, dst_ref=outs[a].at[pid], send_sem=send_sems.at[sem], recv_sem=recv_sems.at[sem],
                    device_id=peer, device_id_type=MESH))
        for cp in local + sends:
            cp.start()
        for cp in recvs:
            cp.wait_recv()
        for cp in sends:
            cp.wait_send()
        for cp in local:
            cp.wait()

    out_shape = [SDS((N_DEV,) + a.shape if gt else a.shape, a.dtype) for a, gt in zip(arrays, gather)]
    nsem = n * (N_DEV - 1)
    return pl.pallas_call(
        body, name=name,
        in_specs=[pl.BlockSpec(memory_space=pl.ANY)] * n,
        out_specs=[pl.BlockSpec(memory_space=pl.ANY)] * n,
        out_shape=out_shape,
        scratch_shapes=[pltpu.SemaphoreType.DMA((nsem,)), pltpu.SemaphoreType.DMA((nsem,)), pltpu.SemaphoreType.DMA((n,))],
    )(*arrays)


_HBM = pl.BlockSpec(memory_space=pltpu.HBM)
_SEM = pl.BlockSpec(memory_space=pltpu.SEMAPHORE)
_EFFECT = pltpu.SideEffectType.DATAFLOW_SIDE_EFFECTING
GATHER = "gather"
SCATTER = "scatter"
SPREAD = "spread"


def _zone_shape(a, mode):
    if mode == GATHER:
        return (N_DEV,) + a.shape
    return (N_DEV - 1,) + (a.shape[1:] if mode == SCATTER else a.shape)


def _start_copies(arrays, modes, name, after=None):
    n = len(arrays)
    zones = [lax.empty(_zone_shape(a, m), a.dtype) for a, m in zip(arrays, modes)]

    def body(*refs):
        ins, lands = refs[:n], refs[n:2 * n]
        send_sems, recv_sems = refs[-2 * n - 3], refs[-2 * n - 2]
        token = refs[-1]
        x, y, c, me = _place()
        for a in range(n):
            for k, (peer, pid) in enumerate(_peers(x, y, c)):
                src = ins[a].at[pid] if modes[a] == SCATTER else ins[a]
                dst = lands[a].at[me] if modes[a] == GATHER else lands[a].at[k]
                pltpu.make_async_remote_copy(src_ref=src, dst_ref=dst, send_sem=send_sems.at[a], recv_sem=recv_sems.at[a],
                                             device_id=peer, device_id_type=MESH).start()
        token[...] = jnp.zeros_like(token)

    hbm = lambda a: pltpu.HBM(a.shape, a.dtype)
    sems = pltpu.SemaphoreType.DMA((n,))
    extra = [] if after is None else [after]
    operands = [pltpu.with_memory_space_constraint(a, pltpu.HBM) for a in list(arrays) + zones]
    res = pl.pallas_call(
        body, name=name,
        out_shape=(sems, sems, *[hbm(a) for a in arrays], *[hbm(z) for z in zones], SDS((8, 128), F32)),
        in_specs=[_HBM] * (2 * n) + [_ANY] * len(extra),
        out_specs=(_SEM, _SEM, *[_HBM] * (2 * n), pl.BlockSpec(memory_space=pltpu.VMEM)),
        input_output_aliases={i: 2 + i for i in range(2 * n)},
        compiler_params=pltpu.CompilerParams(has_side_effects=_EFFECT),
    )(*operands, *extra)
    return res[0], res[1], list(res[2:2 + n]), list(res[2 + n:2 + 2 * n]), res[-1]


def _wait_copies(started, after, name):
    send_sems, recv_sems, thru, zones, _ = started
    n = len(thru)

    def body(*refs):
        lands = refs[n:2 * n]
        send_ref, recv_ref = refs[2 * n], refs[2 * n + 1]
        x, y, c, _ = _place()
        for a in range(n):
            seven = lands[a].at[pl.ds(0, N_DEV - 1)]
            cp = pltpu.make_async_remote_copy(src_ref=seven, dst_ref=seven, send_sem=send_ref.at[a], recv_sem=recv_ref.at[a],
                                              device_id=(x, y, 1 - c), device_id_type=MESH)
            cp.wait_send()
            cp.wait_recv()

    hbm = lambda a: pltpu.HBM(a.shape, a.dtype)
    res = pl.pallas_call(
        body, name=name,
        out_shape=tuple(hbm(a) for a in thru + zones),
        in_specs=[_HBM] * (2 * n) + [_SEM, _SEM, _ANY],
        out_specs=tuple([_HBM] * (2 * n)),
        input_output_aliases={i: i for i in range(2 * n)},
        compiler_params=pltpu.CompilerParams(has_side_effects=_EFFECT),
    )(*thru, *zones, send_sems, recv_sems, after)
    return list(res[:n]), list(res[n:])


def _adamw_math(g, w, m, v):
    m2 = ADAM_B1 * m + (1.0 - ADAM_B1) * g
    v2 = ADAM_B2 * v + (1.0 - ADAM_B2) * (g * g)
    m_hat = m2 / (1.0 - ADAM_B1 ** ADAM_STEP)
    v_hat = v2 / (1.0 - ADAM_B2 ** ADAM_STEP)
    delta = -ADAM_LR * (m_hat / (jnp.sqrt(v_hat) + ADAM_EPS) + ADAM_WD * w)
    return delta, m2, v2


def _sum_adamw(parts, w, m, v, name):
    R, C = w.shape
    tr = max(t for t in (128, 64, 32, 16, 8) if R % t == 0)

    def body(p_ref, w_ref, m_ref, v_ref, g_ref, d_ref, m2_ref, v2_ref):
        g = p_ref[0]
        for k in range(1, N_DEV):
            g = g + p_ref[k]
        g_ref[...] = g
        d_ref[...], m2_ref[...], v2_ref[...] = _adamw_math(g, w_ref[...], m_ref[...], v_ref[...])

    blk = pl.BlockSpec((tr, C), lambda i: (i, 0))
    return pl.pallas_call(
        body, name=name, grid=(R // tr,),
        in_specs=[pl.BlockSpec((N_DEV, tr, C), lambda i: (0, i, 0)), blk, blk, blk],
        out_specs=[blk] * 4,
        out_shape=[SDS((R, C), F32)] * 4,
        compiler_params=_params(1),
    )(parts, w, m, v)


def _sum_adamw_peers(me, own, parts, w, m, v, name, replicated):
    R, C = w.shape
    tr = max(t for t in (128, 64, 32, 16, 8) if R % t == 0)

    def body(me_ref, own_ref, p_ref, w_ref, m_ref, v_ref, g_ref, d_ref, m2_ref, v2_ref):
        if replicated:
            mine = me_ref[0]
            g = None
            for j in range(N_DEV):
                k = jnp.maximum(jnp.bitwise_xor(mine, j) - 1, 0)
                term = jnp.where(mine == j, own_ref[...], p_ref[k])
                g = term if g is None else g + term
        else:
            g = own_ref[...]
            for k in range(N_DEV - 1):
                g = g + p_ref[k]
        g_ref[...] = g
        d_ref[...], m2_ref[...], v2_ref[...] = _adamw_math(g, w_ref[...], m_ref[...], v_ref[...])

    blk = pl.BlockSpec((tr, C), lambda i, me_ref: (i, 0))
    own_spec = blk if replicated else pl.BlockSpec((None, tr, C), lambda i, me_ref: (me_ref[0], i, 0))
    return pl.pallas_call(
        body, name=name,
        grid_spec=pltpu.PrefetchScalarGridSpec(
            num_scalar_prefetch=1, grid=(R // tr,),
            in_specs=[own_spec, pl.BlockSpec((N_DEV - 1, tr, C), lambda i, me_ref: (0, i, 0)), blk, blk, blk],
            out_specs=[blk] * 4),
        out_shape=[SDS((R, C), F32)] * 4,
        compiler_params=_params(1),
    )(me, own, parts, w, m, v)


SMALL = ("ln_v_gain", "ln_v_bias", "w_spatial", "b_spatial", "sinks", "norm_mix_post", "norm_ff_pre", "norm_ff_post")
SMALL_ROWS = {"ln_v_gain": 8, "ln_v_bias": 8, "w_spatial": 1024, "b_spatial": 8, "sinks": 8,
              "norm_mix_post": 8, "norm_ff_pre": 8, "norm_ff_post": 8}
SMALL_PACK_ROWS = 1152


def _pack_small(vals):
    rows = []
    for name in SMALL:
        flat = vals[name].reshape(-1)
        pad = SMALL_ROWS[name] * 128 - flat.shape[0]
        if pad:
            flat = jnp.concatenate([flat, jnp.zeros((pad,), F32)])
        rows.append(flat.reshape(SMALL_ROWS[name], 128))
    rows.append(jnp.zeros((SMALL_PACK_ROWS - sum(SMALL_ROWS.values()), 128), F32))
    return jnp.concatenate(rows, axis=0)


def _unpack_small(packed, shapes):
    out, r = {}, 0
    for name in SMALL:
        n = 1
        for s in shapes[name]:
            n *= s
        out[name] = packed[r:r + SMALL_ROWS[name]].reshape(-1)[:n].reshape(shapes[name])
        r += SMALL_ROWS[name]
    return out


def _rope_rows():
    d = jnp.arange(128) % HEAD
    inv = ROPE_THETA ** (-(2.0 * (d % (ROPE // 2))).astype(F32) / ROPE)
    invf = jnp.where(d < ROPE, inv, 0.0).astype(F32).reshape(1, 128)
    sgn = jnp.where(d < ROPE // 2, -1.0, jnp.where(d < ROPE, 1.0, 0.0)).astype(F32).reshape(1, 128)
    return invf, sgn


def kernel(x, positions, w_in, ln_v_gain, ln_v_bias, w_spatial, b_spatial, sinks, w_a, w_b, w_o, norm_mix_pre, norm_mix_post, w_ff_in, w_ff_out, norm_ff_pre, norm_ff_post, loss_target, m_w_in, m_ln_v_gain, m_ln_v_bias, m_w_spatial, m_b_spatial, m_sinks, m_w_a, m_w_b, m_w_o, m_norm_mix_pre, m_norm_mix_post, m_w_ff_in, m_w_ff_out, m_norm_ff_pre, m_norm_ff_post, v_w_in, v_ln_v_gain, v_ln_v_bias, v_w_spatial, v_b_spatial, v_sinks, v_w_a, v_w_b, v_w_o, v_norm_mix_pre, v_norm_mix_post, v_w_ff_in, v_w_ff_out, v_norm_ff_pre, v_norm_ff_post):
    given = dict(locals())
    T = x.shape[1]
    xt = x[0]
    tgt = loss_target[0]
    posf = positions.astype(F32).reshape(T, 1)
    invf, sgn = _rope_rows()
    bst = b_spatial[0].T
    ws = w_spatial[0]

    me = 4 * lax.axis_index("x") + 2 * lax.axis_index("y") + lax.axis_index("c")
    me_arr = me.astype(jnp.int32).reshape(1)

    def with_own(zone, shard):
        return lax.dynamic_update_slice(zone, shard[None], (me,) + (0,) * shard.ndim)

    rest = ("w_a", "w_b", "w_o", "w_ff_in", "w_ff_out")
    shard = {n: given[n][0].astype(BF16) for n in ("w_in",) + rest}
    g_in = _start_copies([shard["w_in"]], [GATHER], "gather_in_start")
    g_rest = _start_copies([shard[n] for n in rest], [GATHER] * len(rest), "gather_rest_start", after=g_in[-1])
    (own_win,), (win8,) = _wait_copies(g_in, g_rest[-1], "gather_in_wait")
    win = jnp.transpose(with_own(win8, own_win), (1, 0, 2)).reshape(D, IN_W)

    proj, h = _fwd_in(xt, norm_mix_pre, win)
    a = _fwd_sgu(proj, ln_v_gain, ln_v_bias, ws, bst)
    att = _fwd_attn(proj, posf, invf, sgn, sinks[0])
    gw = {n: with_own(z, own) for n, own, z in zip(rest, *_wait_copies(g_rest, att, "gather_rest_wait"))}
    wa, wb, wo = (gw[n].reshape(D, D) for n in ("w_a", "w_b", "w_o"))
    wfi3 = gw["w_ff_in"]
    wfo = gw["w_ff_out"].reshape(D_FF, D)
    merged, a2, b2, mix, x1, hf = _fwd_mix(a, att, proj, xt, wa, wb, wo, norm_mix_post, norm_ff_pre)
    f, dy, dff, dg3, loss_part = _fwd_ff(hf, wfi3, wfo, x1, tgt, norm_ff_post)

    df, dx1, dmix, dg2, dg1 = _bwd_ff(dff, f, wfi3, wfo, x1, dy, mix, norm_mix_post, norm_ff_pre)
    dwfi3, dwfo = _wgrad_ff(hf, df, f, dff)
    own_ff = [dwfi3, dwfo.reshape(N_DEV, D_FF // N_DEV, D)]
    x_ff = _start_copies(own_ff, [SCATTER] * 2, "exchange_ff_start")
    da2, db2, dgate, da, datt = _bwd_mix(dmix, proj, a2, b2, wo, wa, wb, after=x_ff[-1])
    dwo, dwa, dwb = _wgrad_mix(merged, dmix, a, da2, att, db2)
    own_mix = [g.reshape(N_DEV, D // N_DEV, D) for g in (dwa, dwb, dwo)]
    x_mix = _start_copies(own_mix, [SCATTER] * 3, "exchange_mix_start")
    dq, dkv, dsink = _bwd_attn(proj, posf, invf, sgn, sinks[0], datt, after=x_mix[-1])
    duv, dws, dbs, dlng, dlnb = _bwd_sgu(proj, da, ln_v_gain, ln_v_bias, ws, bst)
    dwin = _wgrad_in(h, duv, dq, dkv, dgate)
    small_grads = {"ln_v_gain": dlng, "ln_v_bias": dlnb, "w_spatial": dws, "b_spatial": dbs, "sinks": dsink[:, :N_Q],
                   "norm_mix_post": dg1, "norm_ff_pre": dg2, "norm_ff_post": dg3}
    own_in = [jnp.transpose(dwin.reshape(D, N_DEV, IN_W // N_DEV), (1, 0, 2)), _pack_small(small_grads)]
    x_in = _start_copies(own_in, [SCATTER, SPREAD], "exchange_in_start")
    grad_x, dg0 = _bwd_in(duv, dq, dkv, dgate, win, xt, dx1, norm_mix_pre, after=x_in[-1])
    (dg0_all,) = _all_to_all([dg0.reshape(8, 128)], [True], "exchange_tail")

    results = {}

    def update(n, own, parts):
        results[n] = [r.reshape(given[n].shape) for r in _sum_adamw_peers(
            me_arr, own, parts, given[n][0], given["m_" + n][0], given["v_" + n][0], "adamw_" + n, False)]

    own_ff, p_ff = _wait_copies(x_ff, dg0_all, "exchange_ff_wait")
    update("w_ff_in", own_ff[0], p_ff[0])
    update("w_ff_out", own_ff[1], p_ff[1])
    own_mix, p_mix = _wait_copies(x_mix, p_ff[0], "exchange_mix_wait")
    for n, own, parts in zip(("w_a", "w_b", "w_o"), own_mix, p_mix):
        update(n, own, parts)
    own_in, p_in = _wait_copies(x_in, p_mix[0], "exchange_in_wait")
    update("w_in", own_in[0], p_in[0])
    packed = _sum_adamw_peers(me_arr, own_in[1], p_in[1], _pack_small({n: given[n] for n in SMALL}),
                              _pack_small({n: given["m_" + n] for n in SMALL}),
                              _pack_small({n: given["v_" + n] for n in SMALL}), "adamw_small", True)
    shapes = {n: given[n].shape for n in SMALL}
    unpacked = [_unpack_small(p, shapes) for p in packed]
    for n in SMALL:
        results[n] = [u[n] for u in unpacked]
    n = "norm_mix_pre"
    results[n] = [r.reshape(given[n].shape) for r in _sum_adamw(
        dg0_all, given[n].reshape(8, 128), given["m_" + n].reshape(8, 128), given["v_" + n].reshape(8, 128), "adamw_" + n)]

    loss = lax.psum(loss_part[0, 0], ("x", "y", "c"))
    order = ("w_in", "ln_v_gain", "ln_v_bias", "w_spatial", "b_spatial", "sinks", "w_a", "w_b", "w_o", "norm_mix_pre",
             "norm_mix_post", "w_ff_in", "w_ff_out", "norm_ff_pre", "norm_ff_post")
    out = [loss, grad_x.reshape(x.shape)]
    for k in range(4):
        out += [results[n][k] for n in order]
    return tuple(out)
```

```python
import functools

import jax
import jax.numpy as jnp
from jax import lax
from jax.experimental import pallas as pl
from jax.experimental.pallas import tpu as pltpu

F32 = jnp.float32
BF16 = jnp.bfloat16

N_DEV = 8
D = 1024
D_FF = 4096
IN_W = 5632
CHUNK = 128
GROUPS = 8
HEAD = 64
N_Q = 16
N_KV = 4
ROPE = 16
ROPE_THETA = 500000.0
EPS = 1e-6
OFF_Q, OFF_K, OFF_VA, OFF_GA, OFF_GB = 2048, 3072, 3328, 3584, 4608

ADAM_LR = 0.001
ADAM_B1 = 0.9
ADAM_B2 = 0.999
ADAM_EPS = 1e-08
ADAM_WD = 0.01
ADAM_STEP = 10

VMEM_LIMIT = 56 * 1024 * 1024

SDS = jax.ShapeDtypeStruct
MESH = pl.DeviceIdType.MESH


def _params(n_axes=None):
    if n_axes is None:
        return pltpu.CompilerParams(vmem_limit_bytes=VMEM_LIMIT)
    return pltpu.CompilerParams(dimension_semantics=("arbitrary",) * n_axes, vmem_limit_bytes=VMEM_LIMIT)


def _nt(a, b):
    return lax.dot_general(a, b, (((1,), (1,)), ((), ())), preferred_element_type=F32)


def _tn(a, b):
    return lax.dot_general(a, b, (((0,), (0,)), ((), ())), preferred_element_type=F32)


def _nn(a, b):
    return jnp.dot(a, b, preferred_element_type=F32)


def _gelu(x):
    t = jnp.tanh(0.7978845608028654 * (x + 0.044715 * (x * x * x)))
    return 0.5 * x * (1.0 + t), t


def _gelu_grad(x, t):
    return 0.5 * (1.0 + t) + 0.5 * x * (1.0 - t * t) * (0.7978845608028654 * (1.0 + 3.0 * 0.044715 * x * x))


def _sigmoid(x):
    return 1.0 / (1.0 + jnp.exp(-x))


def _rms_stats(v):
    r = lax.rsqrt(jnp.mean(v * v, axis=-1, keepdims=True) + EPS)
    return r, v * r


def _rms_bwd(d, vhat, r, g):
    gd = g * d
    return r * (gd - vhat * jnp.mean(gd * vhat, axis=-1, keepdims=True))


def _colsum(v):
    return jnp.sum(v, axis=0, keepdims=True)


_ANY = pl.BlockSpec(memory_space=pl.ANY)


def _after(body, n_in, after):
    if after is None:
        return body, [], []

    def ordered(*refs):
        return body(*refs[:n_in], *refs[n_in + 1:])

    return ordered, [_ANY], [after]


def _fwd_in(x, g0, win):
    T = x.shape[0]
    tm, tn = min(T, 1024), 1408

    def body(x_ref, g_ref, w_ref, p_ref, h_ref):
        @pl.when(pl.program_id(1) == 0)
        def _():
            _, xh = _rms_stats(x_ref[...])
            h_ref[...] = (xh * g_ref[...]).astype(BF16)

        p_ref[...] = _nn(h_ref[...], w_ref[...]).astype(BF16)

    return pl.pallas_call(
        body, name="fwd_in", grid=(T // tm, IN_W // tn),
        in_specs=[pl.BlockSpec((tm, D), lambda i, j: (i, 0)), pl.BlockSpec((1, D), lambda i, j: (0, 0)),
                  pl.BlockSpec((D, tn), lambda i, j: (0, j))],
        out_specs=[pl.BlockSpec((tm, tn), lambda i, j: (i, j)), pl.BlockSpec((tm, D), lambda i, j: (i, 0))],
        out_shape=[SDS((T, IN_W), BF16), SDS((T, D), BF16)],
        compiler_params=_params(2),
    )(x, g0, win)


def _sgu_forward_parts(u_ref, vs_ref, lng_ref, lnb_ref):
    u = u_ref[...].astype(F32)
    vs = vs_ref[...].astype(F32)
    gu, tu = _gelu(u)
    gv, tv = _gelu(vs)
    mu = jnp.mean(gv, axis=-1, keepdims=True)
    dv = gv - mu
    rstd = lax.rsqrt(jnp.mean(dv * dv, axis=-1, keepdims=True) + EPS)
    vhat = dv * rstd
    vn = (vhat * lng_ref[...] + lnb_ref[...]).astype(BF16)
    return u, vs, gu, tu, tv, rstd, vhat, vn


def _masked_ws(ws_ref, g):
    row = lax.broadcasted_iota(jnp.int32, (CHUNK, CHUNK), 0)
    col = lax.broadcasted_iota(jnp.int32, (CHUNK, CHUNK), 1)
    return jnp.where(row >= col, ws_ref[g], 0.0).astype(BF16)


def _fwd_sgu(proj, lng, lnb, ws, bst):
    T = proj.shape[0]
    tc = min(T, 512)

    def body(u_ref, vs_ref, lng_ref, lnb_ref, ws_ref, bst_ref, a_ref):
        _, _, gu, _, _, _, _, vn = _sgu_forward_parts(u_ref, vs_ref, lng_ref, lnb_ref)
        for g in range(GROUPS):
            wm = _masked_ws(ws_ref, g)
            cols = slice(g * CHUNK, (g + 1) * CHUNK)
            for c in range(tc // CHUNK):
                rows = slice(c * CHUNK, (c + 1) * CHUNK)
                mixed = _nn(wm, vn[rows, cols]) + bst_ref[:, g:g + 1]
                a_ref[rows, cols] = (gu[rows, cols] * mixed).astype(BF16)

    return pl.pallas_call(
        body, name="fwd_sgu", grid=(T // tc,),
        in_specs=[pl.BlockSpec((tc, D), lambda i: (i, 0)), pl.BlockSpec((tc, D), lambda i: (i, 1)),
                  pl.BlockSpec((1, D), lambda i: (0, 0)), pl.BlockSpec((1, D), lambda i: (0, 0)),
                  pl.BlockSpec((GROUPS, CHUNK, CHUNK), lambda i: (0, 0, 0)), pl.BlockSpec((CHUNK, GROUPS), lambda i: (0, 0))],
        out_specs=pl.BlockSpec((tc, D), lambda i: (i, 0)),
        out_shape=SDS((T, D), BF16),
        compiler_params=_params(1),
    )(proj, proj, lng, lnb, ws, bst)


def _rope_tables(pos, invf, sgn, reps):
    ang = pos * invf
    c = jnp.cos(ang)
    s = jnp.sin(ang) * sgn
    if reps > 1:
        c = jnp.tile(c, (1, reps))
        s = jnp.tile(s, (1, reps))
    return c, s


def _swap_halves(v):
    n = v.shape[1]
    d = lax.broadcasted_iota(jnp.int32, v.shape, 1) % HEAD
    upper = jnp.where(d < ROPE, pltpu.roll(v, ROPE // 2, 1), 0.0)
    return jnp.where(d < ROPE // 2, pltpu.roll(v, n - ROPE // 2, 1), upper)


def _rope(v, c, s):
    return v * c + _swap_halves(v) * s


def _rope_bwd(dv, c, s):
    return dv * c + _swap_halves(dv * s)


def _band_mask(first, key_axis):
    shape = (CHUNK, 2 * CHUNK) if key_axis == 1 else (2 * CHUNK, CHUNK)
    t = lax.broadcasted_iota(jnp.int32, shape, 1 - key_axis)
    j = lax.broadcasted_iota(jnp.int32, shape, key_axis)
    return (j > t) & (j <= t + CHUNK) & (jnp.logical_not(first) | (j >= CHUNK))


def _softmax_sink(s, sink, key_axis):
    m = jnp.maximum(jnp.max(s, axis=key_axis, keepdims=True), sink)
    p = jnp.exp(s - m)
    esink = jnp.exp(sink - m)
    inv = 1.0 / (jnp.sum(p, axis=key_axis, keepdims=True) + esink)
    return p * inv, esink * inv


def _head_pair_operand(band, g):
    slab = band[:, (g // 2) * 128:(g // 2 + 1) * 128]
    lo = lax.broadcasted_iota(jnp.int32, slab.shape, 1) < HEAD
    if g % 2 == 0:
        first = jnp.where(lo, slab, 0.0)
        second = pltpu.roll(first, HEAD, 1)
    else:
        second = jnp.where(lo, 0.0, slab)
        first = pltpu.roll(second, HEAD, 1)
    return jnp.concatenate([first, second], axis=0).astype(BF16)


def _head_pair_gradient(acc, g):
    top, bot = acc[:2 * CHUNK], acc[2 * CHUNK:]
    lo = lax.broadcasted_iota(jnp.int32, top.shape, 1) < HEAD
    if g % 2 == 0:
        return jnp.where(lo, top, 0.0) + pltpu.roll(jnp.where(lo, 0.0, bot), HEAD, 1)
    return pltpu.roll(jnp.where(lo, top, 0.0), HEAD, 1) + jnp.where(lo, 0.0, bot)


def _attn_specs(nb, clamp):
    cur = (lambda i: jnp.minimum(i, nb - 1)) if clamp else (lambda i: i)
    prev = lambda i: jnp.maximum(jnp.minimum(i, nb - 1) - 1, 0)
    kw = N_KV * HEAD
    return cur, prev, [
        pl.BlockSpec((CHUNK, D), lambda i: (cur(i), OFF_Q // D)),
        pl.BlockSpec((CHUNK, kw), lambda i: (prev(i), OFF_K // kw)),
        pl.BlockSpec((CHUNK, kw), lambda i: (cur(i), OFF_K // kw)),
        pl.BlockSpec((CHUNK, kw), lambda i: (prev(i), OFF_VA // kw)),
        pl.BlockSpec((CHUNK, kw), lambda i: (cur(i), OFF_VA // kw)),
        pl.BlockSpec((CHUNK, 1), lambda i: (prev(i), 0)),
        pl.BlockSpec((CHUNK, 1), lambda i: (cur(i), 0)),
        pl.BlockSpec((1, 128), lambda i: (0, 0)),
        pl.BlockSpec((1, 128), lambda i: (0, 0)),
        pl.BlockSpec(memory_space=pltpu.SMEM),
    ]


def _attn_load(q_ref, kp_ref, kc_ref, vp_ref, vc_ref, pp_ref, pc_ref, invf_ref, sgn_ref):
    cq, sq = _rope_tables(pc_ref[...], invf_ref[...], sgn_ref[...], D // 128)
    pos_b = jnp.concatenate([pp_ref[...], pc_ref[...]], axis=0)
    ck, sk = _rope_tables(pos_b, invf_ref[...], sgn_ref[...], N_KV * HEAD // 128)
    q = _rope(q_ref[...].astype(F32), cq, sq).astype(BF16)
    kb = _rope(jnp.concatenate([kp_ref[...], kc_ref[...]], axis=0).astype(F32), ck, sk)
    vb = jnp.concatenate([vp_ref[...], vc_ref[...]], axis=0).astype(F32)
    return q, kb, vb, (cq, sq, ck, sk)


PAIRS_PER_KV = N_Q // N_KV // 2


def _fwd_attn(proj, posf, invf, sgn, sinks):
    T = proj.shape[0]
    nb = T // CHUNK
    _, _, specs = _attn_specs(nb, False)

    def body(q_ref, kp_ref, kc_ref, vp_ref, vc_ref, pp_ref, pc_ref, invf_ref, sgn_ref, sink_ref, o_ref):
        q, kb, vb, _ = _attn_load(q_ref, kp_ref, kc_ref, vp_ref, vc_ref, pp_ref, pc_ref, invf_ref, sgn_ref)
        mask = _band_mask(pl.program_id(0) == 0, 0)
        for g in range(N_KV):
            k2 = _head_pair_operand(kb, g)
            v2 = _head_pair_operand(vb, g)
            for r in range(PAIRS_PER_KV):
                pair = g * PAIRS_PER_KV + r
                s2 = _nt(k2, q[:, pair * 128:(pair + 1) * 128]) * (HEAD ** -0.5)
                ps = []
                for e in range(2):
                    s = jnp.where(mask, s2[e * 2 * CHUNK:(e + 1) * 2 * CHUNK], -1e30)
                    ps.append(_softmax_sink(s, sink_ref[2 * pair + e], 0)[0].astype(BF16))
                o_ref[:, pair * 128:(pair + 1) * 128] = _tn(jnp.concatenate(ps, axis=0), v2).astype(BF16)

    return pl.pallas_call(
        body, name="fwd_attn", grid=(nb,), in_specs=specs,
        out_specs=pl.BlockSpec((CHUNK, D), lambda i: (i, 0)),
        out_shape=SDS((T, D), BF16),
        compiler_params=_params(1),
    )(proj, proj, proj, proj, proj, posf, posf, invf, sgn, sinks)


def _fwd_mix(a, att, proj, x, wa, wb, wo, g1, g2):
    T = x.shape[0]
    tm = min(T, 512)
    half = D // 2

    def body(a_ref, att_ref, ga0, ga1, gb0, gb1, x_ref, wa_ref, wb_ref, wo_ref, g1_ref, g2_ref,
             mg_ref, a2_ref, b2_ref, mix_ref, x1_ref, hf_ref):
        a2 = _nn(a_ref[...], wa_ref[...])
        b2 = _nn(att_ref[...], wb_ref[...])
        ga = jnp.concatenate([ga0[...], ga1[...]], axis=1).astype(F32)
        gb = jnp.concatenate([gb0[...], gb1[...]], axis=1).astype(F32)
        merged = (_sigmoid(ga) * a2 + _sigmoid(gb) * b2).astype(BF16)
        a2_ref[...] = a2.astype(BF16)
        b2_ref[...] = b2.astype(BF16)
        mg_ref[...] = merged
        mix = _nn(merged, wo_ref[...])
        mix_ref[...] = mix
        _, mh = _rms_stats(mix)
        x1 = x_ref[...] + mh * g1_ref[...]
        x1_ref[...] = x1
        _, xh = _rms_stats(x1)
        hf_ref[...] = (xh * g2_ref[...]).astype(BF16)

    row = lambda i: (i, 0)
    const = lambda i: (0, 0)
    gspec = lambda off: pl.BlockSpec((tm, half), lambda i: (i, off // half))
    return pl.pallas_call(
        body, name="fwd_mix", grid=(T // tm,),
        in_specs=[pl.BlockSpec((tm, D), row), pl.BlockSpec((tm, D), row),
                  gspec(OFF_GA), gspec(OFF_GA + half), gspec(OFF_GB), gspec(OFF_GB + half),
                  pl.BlockSpec((tm, D), row), _resident((D, D)), _resident((D, D)),
                  _resident((D, D)), pl.BlockSpec((1, D), const), pl.BlockSpec((1, D), const)],
        out_specs=[pl.BlockSpec((tm, D), row)] * 6,
        out_shape=[SDS((T, D), BF16), SDS((T, D), BF16), SDS((T, D), BF16), SDS((T, D), F32), SDS((T, D), F32),
                   SDS((T, D), BF16)],
        compiler_params=_params(1),
    )(a, att, proj, proj, proj, proj, x, wa, wb, wo, g1, g2)


FF_SPLIT = N_DEV
FF_TILE = D_FF // FF_SPLIT
FF_STEP = 2048
FF_SLABS = FF_STEP // FF_TILE
FF_STEPS = D_FF // FF_STEP


def _fwd_ff(hf, wfi3, wfo, x1, tgt, g3):
    T = hf.shape[0]
    tm = min(T, 512)
    last = FF_STEPS - 1

    def body(hf_ref, wfi_ref, wfo_ref, x1_ref, tgt_ref, g3_ref, f_ref, dy_ref, dff_ref, dg3_ref, loss_ref, acc, r_s):
        i, p = pl.program_id(0), pl.program_id(1)

        @pl.when((i == 0) & (p == 0))
        def _():
            dg3_ref[...] = jnp.zeros_like(dg3_ref)
            loss_ref[...] = jnp.zeros_like(loss_ref)

        hf_t = hf_ref[...]
        for s in range(FF_SLABS):
            cols = slice(s * FF_TILE, (s + 1) * FF_TILE)
            f = _nn(hf_t, wfi_ref[s]).astype(BF16)
            f_ref[:, cols] = f
            rl = jnp.maximum(f.astype(F32), 0.0)
            r_s[:, cols] = (rl * rl).astype(BF16)
        part = _nn(r_s[...], wfo_ref[...])

        @pl.when(p == 0)
        def _():
            acc[...] = part

        @pl.when(p > 0)
        def _():
            acc[...] += part

        @pl.when(p == last)
        def _():
            r3, fh = _rms_stats(acc[...])
            e = x1_ref[...] + fh * g3_ref[...] - tgt_ref[...]
            loss_ref[...] += jnp.sum(e * e) * (0.5 / D)
            dy = e * (1.0 / D)
            dy_ref[...] = dy
            dg3_ref[...] += _colsum(dy * fh)
            dff_ref[...] = _rms_bwd(dy, fh, r3, g3_ref[...]).astype(BF16)

    row = lambda i, p: (i, 0)
    const = lambda i, p: (0, 0)
    return pl.pallas_call(
        body, name="fwd_ff", grid=(T // tm, FF_STEPS),
        in_specs=[pl.BlockSpec((tm, D), row), pl.BlockSpec((FF_SLABS, D, FF_TILE), lambda i, p: (p, 0, 0)),
                  pl.BlockSpec((FF_STEP, D), lambda i, p: (p, 0)), pl.BlockSpec((tm, D), row),
                  pl.BlockSpec((tm, D), row), pl.BlockSpec((1, D), const)],
        out_specs=[pl.BlockSpec((tm, FF_STEP), lambda i, p: (i, p)), pl.BlockSpec((tm, D), row),
                   pl.BlockSpec((tm, D), row), pl.BlockSpec((1, D), const), pl.BlockSpec((1, 128), const)],
        out_shape=[SDS((T, D_FF), BF16), SDS((T, D), F32), SDS((T, D), BF16), SDS((1, D), F32), SDS((1, 128), F32)],
        scratch_shapes=[pltpu.VMEM((tm, D), F32), pltpu.VMEM((tm, FF_STEP), BF16)],
        compiler_params=_params(2),
    )(hf, wfi3, wfo, x1, tgt, g3)


def _bwd_ff(dff, f, wfi3, wfo, x1, dy, mix, g1, g2):
    T = dff.shape[0]
    tm = min(T, 512)
    last = FF_STEPS - 1

    def body(dff_ref, f_ref, wfi_ref, wfo_ref, x1_ref, dy_ref, mix_ref, g1_ref, g2_ref,
             df_ref, dx1_ref, dmix_ref, dg2_ref, dg1_ref, acc):
        i, p = pl.program_id(0), pl.program_id(1)

        @pl.when((i == 0) & (p == 0))
        def _():
            dg2_ref[...] = jnp.zeros_like(dg2_ref)
            dg1_ref[...] = jnp.zeros_like(dg1_ref)

        dr = _nt(dff_ref[...], wfo_ref[...])
        df_ref[...] = (dr * (2.0 * jnp.maximum(f_ref[...].astype(F32), 0.0))).astype(BF16)
        part = _nt(df_ref[:, :FF_TILE], wfi_ref[0])
        for s in range(1, FF_SLABS):
            part = part + _nt(df_ref[:, s * FF_TILE:(s + 1) * FF_TILE], wfi_ref[s])

        @pl.when(p == 0)
        def _():
            acc[...] = part

        @pl.when(p > 0)
        def _():
            acc[...] += part

        @pl.when(p == last)
        def _():
            dhf = acc[...]
            r2, xh = _rms_stats(x1_ref[...])
            dg2_ref[...] += _colsum(dhf * xh)
            dx1 = dy_ref[...] + _rms_bwd(dhf, xh, r2, g2_ref[...])
            dx1_ref[...] = dx1
            r1, mh = _rms_stats(mix_ref[...])
            dg1_ref[...] += _colsum(dx1 * mh)
            dmix_ref[...] = _rms_bwd(dx1, mh, r1, g1_ref[...]).astype(BF16)

    row = lambda i, p: (i, 0)
    const = lambda i, p: (0, 0)
    return pl.pallas_call(
        body, name="bwd_ff", grid=(T // tm, FF_STEPS),
        in_specs=[pl.BlockSpec((tm, D), row), pl.BlockSpec((tm, FF_STEP), lambda i, p: (i, p)),
                  pl.BlockSpec((FF_SLABS, D, FF_TILE), lambda i, p: (p, 0, 0)), pl.BlockSpec((FF_STEP, D), lambda i, p: (p, 0)),
                  pl.BlockSpec((tm, D), row), pl.BlockSpec((tm, D), row), pl.BlockSpec((tm, D), row),
                  pl.BlockSpec((1, D), const), pl.BlockSpec((1, D), const)],
        out_specs=[pl.BlockSpec((tm, FF_STEP), lambda i, p: (i, p)), pl.BlockSpec((tm, D), row),
                   pl.BlockSpec((tm, D), row), pl.BlockSpec((1, D), const), pl.BlockSpec((1, D), const)],
        out_shape=[SDS((T, D_FF), BF16), SDS((T, D), F32), SDS((T, D), BF16), SDS((1, D), F32), SDS((1, D), F32)],
        scratch_shapes=[pltpu.VMEM((tm, D), F32)],
        compiler_params=_params(2),
    )(dff, f, wfi3, wfo, x1, dy, mix, g1, g2)


def _wgrad_ff(hf, df, f, dff):
    T = hf.shape[0]
    tt = min(T, 1024)
    wide = 2 * FF_TILE

    def body(hf_ref, df_ref, f_ref, dff_ref, dwfi_ref, dwfo_ref, acc_i, acc_o):
        t = pl.program_id(1)

        @pl.when(t == 0)
        def _():
            acc_i[...] = jnp.zeros_like(acc_i)
            acc_o[...] = jnp.zeros_like(acc_o)

        acc_i[...] += _tn(hf_ref[...], df_ref[...])
        rl = jnp.maximum(f_ref[...].astype(F32), 0.0)
        acc_o[...] += _tn((rl * rl).astype(BF16), dff_ref[...])

        @pl.when(t == T // tt - 1)
        def _():
            dwfi_ref[0] = acc_i[:, :FF_TILE].astype(BF16)
            dwfi_ref[1] = acc_i[:, FF_TILE:].astype(BF16)
            dwfo_ref[...] = acc_o[...].astype(BF16)

    return pl.pallas_call(
        body, name="wgrad_ff", grid=(D_FF // wide, T // tt),
        in_specs=[pl.BlockSpec((tt, D), lambda p, t: (t, 0)), pl.BlockSpec((tt, wide), lambda p, t: (t, p)),
                  pl.BlockSpec((tt, wide), lambda p, t: (t, p)), pl.BlockSpec((tt, D), lambda p, t: (t, 0))],
        out_specs=[pl.BlockSpec((2, D, FF_TILE), lambda p, t: (p, 0, 0)), pl.BlockSpec((wide, D), lambda p, t: (p, 0))],
        out_shape=[SDS((FF_SPLIT, D, FF_TILE), BF16), SDS((D_FF, D), BF16)],
        scratch_shapes=[pltpu.VMEM((D, wide), F32), pltpu.VMEM((wide, D), F32)],
        compiler_params=_params(2),
    )(hf, df, f, dff)


def _bwd_mix(dmix, proj, a2, b2, wo, wa, wb, after=None):
    T = dmix.shape[0]
    tm = min(T, 512)
    half = D // 2

    def body(dmix_ref, ga0, ga1, gb0, gb1, a2_ref, b2_ref, wo_ref, wa_ref, wb_ref,
             da2_ref, db2_ref, dg_ref, da_ref, datt_ref):
        dmg = _nt(dmix_ref[...], wo_ref[...])
        sa = _sigmoid(jnp.concatenate([ga0[...], ga1[...]], axis=1).astype(F32))
        sb = _sigmoid(jnp.concatenate([gb0[...], gb1[...]], axis=1).astype(F32))
        da2 = (dmg * sa).astype(BF16)
        db2 = (dmg * sb).astype(BF16)
        da2_ref[...] = da2
        db2_ref[...] = db2
        dg_ref[:, :D] = (dmg * a2_ref[...].astype(F32) * (sa * (1.0 - sa))).astype(BF16)
        dg_ref[:, D:] = (dmg * b2_ref[...].astype(F32) * (sb * (1.0 - sb))).astype(BF16)
        da_ref[...] = _nt(da2, wa_ref[...]).astype(BF16)
        datt_ref[...] = _nt(db2, wb_ref[...]).astype(BF16)

    row = lambda i: (i, 0)
    const = lambda i: (0, 0)
    gspec = lambda off: pl.BlockSpec((tm, half), lambda i: (i, off // half))
    body, dep_specs, deps = _after(body, 10, after)
    return pl.pallas_call(
        body, name="bwd_mix", grid=(T // tm,),
        in_specs=[pl.BlockSpec((tm, D), row), gspec(OFF_GA), gspec(OFF_GA + half), gspec(OFF_GB), gspec(OFF_GB + half),
                  pl.BlockSpec((tm, D), row), pl.BlockSpec((tm, D), row),
                  _resident((D, D)), _resident((D, D)), _resident((D, D))] + dep_specs,
        out_specs=[pl.BlockSpec((tm, D), row), pl.BlockSpec((tm, D), row), pl.BlockSpec((tm, 2 * D), row),
                   pl.BlockSpec((tm, D), row), pl.BlockSpec((tm, D), row)],
        out_shape=[SDS((T, D), BF16), SDS((T, D), BF16), SDS((T, 2 * D), BF16), SDS((T, D), BF16), SDS((T, D), BF16)],
        compiler_params=_params(1),
    )(dmix, proj, proj, proj, proj, a2, b2, wo, wa, wb, *deps)


def _wgrad_mix(merged, dmix, a, da2, att, db2):
    T = merged.shape[0]
    tt = min(T, 512)

    def body(mg_ref, dmix_ref, a_ref, da2_ref, att_ref, db2_ref, dwo_ref, dwa_ref, dwb_ref, acc):
        t = pl.program_id(0)

        @pl.when(t == 0)
        def _():
            acc[...] = jnp.zeros_like(acc)

        acc[0] += _tn(mg_ref[...], dmix_ref[...])
        acc[1] += _tn(a_ref[...], da2_ref[...])
        acc[2] += _tn(att_ref[...], db2_ref[...])

        @pl.when(t == T // tt - 1)
        def _():
            dwo_ref[...] = acc[0].astype(BF16)
            dwa_ref[...] = acc[1].astype(BF16)
            dwb_ref[...] = acc[2].astype(BF16)

    return pl.pallas_call(
        body, name="wgrad_mix", grid=(T // tt,),
        in_specs=[pl.BlockSpec((tt, D), lambda t: (t, 0))] * 6,
        out_specs=[pl.BlockSpec((D, D), lambda t: (0, 0))] * 3,
        out_shape=[SDS((D, D), BF16)] * 3,
        scratch_shapes=[pltpu.VMEM((3, D, D), F32)],
        compiler_params=_params(1),
    )(merged, dmix, a, da2, att, db2)


def _bwd_attn(proj, posf, invf, sgn, sinks, datt, after=None):
    T = proj.shape[0]
    nb = T // CHUNK
    kw = N_KV * HEAD
    cur, prev, specs = _attn_specs(nb, True)

    def body(q_ref, kp_ref, kc_ref, vp_ref, vc_ref, pp_ref, pc_ref, invf_ref, sgn_ref, sink_ref, do_ref,
             dq_ref, dkv_ref, dsink_ref, carry_k, carry_v, dq_acc):
        i = pl.program_id(0)

        @pl.when(i == 0)
        def _():
            carry_k[...] = jnp.zeros_like(carry_k)
            carry_v[...] = jnp.zeros_like(carry_v)
            dsink_ref[...] = jnp.zeros_like(dsink_ref)

        @pl.when(i < nb)
        def _():
            q, kb, vb, (cq, sq, ck, sk) = _attn_load(q_ref, kp_ref, kc_ref, vp_ref, vc_ref, pp_ref, pc_ref,
                                                     invf_ref, sgn_ref)
            mask = _band_mask(i == 0, 0)
            do = do_ref[...]
            lane = lax.broadcasted_iota(jnp.int32, (1, 128), 1)
            dsink = jnp.zeros((1, 128), F32)
            dks, dvs = [], []
            for g in range(N_KV):
                k2 = _head_pair_operand(kb, g)
                v2 = _head_pair_operand(vb, g)
                dk2 = jnp.zeros((4 * CHUNK, 128), F32)
                dv2 = jnp.zeros((4 * CHUNK, 128), F32)
                for r in range(PAIRS_PER_KV):
                    pair = g * PAIRS_PER_KV + r
                    qp = q[:, pair * 128:(pair + 1) * 128]
                    dop = do[:, pair * 128:(pair + 1) * 128]
                    s2 = _nt(k2, qp) * (HEAD ** -0.5)
                    dp2 = _nt(v2, dop)
                    ps, dss = [], []
                    for e in range(2):
                        rows = slice(e * 2 * CHUNK, (e + 1) * 2 * CHUNK)
                        p, psink = _softmax_sink(jnp.where(mask, s2[rows], -1e30), sink_ref[2 * pair + e], 0)
                        dp = dp2[rows]
                        delta = jnp.sum(p * dp, axis=0, keepdims=True)
                        ps.append(p.astype(BF16))
                        dss.append((p * (dp - delta) * (HEAD ** -0.5)).astype(BF16))
                        dsink = dsink + jnp.where(lane == 2 * pair + e, -jnp.sum(psink * delta), 0.0)
                    ds2 = jnp.concatenate(dss, axis=0)
                    dq_acc[:, pair * 128:(pair + 1) * 128] = _tn(ds2, k2)
                    dk2 = dk2 + _nn(ds2, qp)
                    dv2 = dv2 + _nn(jnp.concatenate(ps, axis=0), dop)
                dks.append(_head_pair_gradient(dk2, g))
                dvs.append(_head_pair_gradient(dv2, g))
            dsink_ref[...] += dsink
            dq_ref[...] = _rope_bwd(dq_acc[...], cq, sq).astype(BF16)
            dkb = _rope_bwd(jnp.concatenate([dks[0] + dks[1], dks[2] + dks[3]], axis=1), ck, sk)
            dvb = jnp.concatenate([dvs[0] + dvs[1], dvs[2] + dvs[3]], axis=1)
            dkv_ref[:, :kw] = (carry_k[...] + dkb[:CHUNK]).astype(BF16)
            dkv_ref[:, kw:] = (carry_v[...] + dvb[:CHUNK]).astype(BF16)
            carry_k[...] = dkb[CHUNK:]
            carry_v[...] = dvb[CHUNK:]

        @pl.when(i == nb)
        def _():
            dkv_ref[:, :kw] = carry_k[...].astype(BF16)
            dkv_ref[:, kw:] = carry_v[...].astype(BF16)

    body, dep_specs, deps = _after(body, 11, after)
    return pl.pallas_call(
        body, name="bwd_attn", grid=(nb + 1,),
        in_specs=specs + [pl.BlockSpec((CHUNK, D), lambda i: (cur(i), 0))] + dep_specs,
        out_specs=[pl.BlockSpec((CHUNK, D), lambda i: (cur(i), 0)),
                   pl.BlockSpec((CHUNK, 2 * kw), lambda i: (jnp.maximum(i - 1, 0), 0)),
                   pl.BlockSpec((1, 128), lambda i: (0, 0))],
        out_shape=[SDS((T, D), BF16), SDS((T, 2 * kw), BF16), SDS((1, 128), F32)],
        scratch_shapes=[pltpu.VMEM((CHUNK, kw), F32), pltpu.VMEM((CHUNK, kw), F32), pltpu.VMEM((CHUNK, D), F32)],
        compiler_params=_params(1),
    )(proj, proj, proj, proj, proj, posf, posf, invf, sgn, sinks, datt, *deps)


def _bwd_sgu(proj, da, lng, lnb, ws, bst):
    T = proj.shape[0]
    tc = min(T, 512)
    nsteps = T // tc

    def body(u_ref, vs_ref, da_ref, lng_ref, lnb_ref, ws_ref, bst_ref,
             duv_ref, dws_ref, dbs_ref, dlng_ref, dlnb_ref, dvn_s, dgu_s, dmx_sum):
        i = pl.program_id(0)

        @pl.when(i == 0)
        def _():
            dws_ref[...] = jnp.zeros_like(dws_ref)
            dlng_ref[...] = jnp.zeros_like(dlng_ref)
            dlnb_ref[...] = jnp.zeros_like(dlnb_ref)
            dmx_sum[...] = jnp.zeros_like(dmx_sum)

        u, vs, gu, tu, tv, rstd, vhat, vn = _sgu_forward_parts(u_ref, vs_ref, lng_ref, lnb_ref)
        da = da_ref[...].astype(F32)
        for g in range(GROUPS):
            wm = _masked_ws(ws_ref, g)
            cols = slice(g * CHUNK, (g + 1) * CHUNK)
            dws = jnp.zeros((CHUNK, CHUNK), F32)
            dsum = jnp.zeros((CHUNK, CHUNK), F32)
            for c in range(tc // CHUNK):
                rows = slice(c * CHUNK, (c + 1) * CHUNK)
                vn_cg = vn[rows, cols]
                mixed = _nn(wm, vn_cg) + bst_ref[:, g:g + 1]
                dgu_s[rows, cols] = da[rows, cols] * mixed
                dmx = da[rows, cols] * gu[rows, cols]
                dmxb = dmx.astype(BF16)
                dws = dws + _nt(dmxb, vn_cg)
                dsum = dsum + dmx
                dvn_s[rows, cols] = _tn(wm, dmxb)
            dws_ref[g] += dws
            dmx_sum[:, cols] += dsum
        dvn = dvn_s[...]
        dlng_ref[...] += _colsum(dvn * vhat)
        dlnb_ref[...] += _colsum(dvn)
        dvh = dvn * lng_ref[...]
        dgv = rstd * (dvh - jnp.mean(dvh, axis=-1, keepdims=True) - vhat * jnp.mean(dvh * vhat, axis=-1, keepdims=True))
        duv_ref[:, :D] = (dgu_s[...] * _gelu_grad(u, tu)).astype(BF16)
        duv_ref[:, D:] = (dgv * _gelu_grad(vs, tv)).astype(BF16)

        @pl.when(i == nsteps - 1)
        def _():
            row = lax.broadcasted_iota(jnp.int32, (CHUNK, CHUNK), 0)
            col = lax.broadcasted_iota(jnp.int32, (CHUNK, CHUNK), 1)
            for g in range(GROUPS):
                dws_ref[g] = jnp.where(row >= col, dws_ref[g], 0.0)
                dbs_ref[g:g + 1, :] = _colsum(dmx_sum[:, g * CHUNK:(g + 1) * CHUNK].T)

    const2 = lambda i: (0, 0)
    return pl.pallas_call(
        body, name="bwd_sgu", grid=(nsteps,),
        in_specs=[pl.BlockSpec((tc, D), lambda i: (i, 0)), pl.BlockSpec((tc, D), lambda i: (i, 1)),
                  pl.BlockSpec((tc, D), lambda i: (i, 0)), pl.BlockSpec((1, D), const2), pl.BlockSpec((1, D), const2),
                  pl.BlockSpec((GROUPS, CHUNK, CHUNK), lambda i: (0, 0, 0)), pl.BlockSpec((CHUNK, GROUPS), const2)],
        out_specs=[pl.BlockSpec((tc, 2 * D), lambda i: (i, 0)), pl.BlockSpec((GROUPS, CHUNK, CHUNK), lambda i: (0, 0, 0)),
                   pl.BlockSpec((GROUPS, CHUNK), const2), pl.BlockSpec((1, D), const2), pl.BlockSpec((1, D), const2)],
        out_shape=[SDS((T, 2 * D), BF16), SDS((GROUPS, CHUNK, CHUNK), F32), SDS((GROUPS, CHUNK), F32),
                   SDS((1, D), F32), SDS((1, D), F32)],
        scratch_shapes=[pltpu.VMEM((tc, D), F32), pltpu.VMEM((tc, D), F32), pltpu.VMEM((CHUNK, D), F32)],
        compiler_params=_params(1),
    )(proj, proj, da, lng, lnb, ws, bst)


IN_SEG_WIDTHS = (2 * D, D, 2 * N_KV * HEAD, 2 * D)


def _resident(shape):
    return pl.BlockSpec(shape, lambda *_: (0,) * len(shape), pipeline_mode=pl.Buffered(1))


def _bwd_in(duv, dq, dkv, dg, win, x, dx1, g0, after=None):
    T = x.shape[0]
    tm = min(T, 512)

    def body(duv_ref, dq_ref, dkv_ref, dg_ref, w_ref, x_ref, dx1_ref, g0_ref, gx_ref, dg0_ref):
        @pl.when(pl.program_id(0) == 0)
        def _():
            dg0_ref[...] = jnp.zeros_like(dg0_ref)

        dh, off = None, 0
        for ref, width in zip((duv_ref, dq_ref, dkv_ref, dg_ref), IN_SEG_WIDTHS):
            part = _nt(ref[...], w_ref[:, off:off + width])
            dh = part if dh is None else dh + part
            off += width
        r0, xh = _rms_stats(x_ref[...])
        dg0_ref[...] += _colsum(dh * xh)
        gx_ref[...] = dx1_ref[...] + _rms_bwd(dh, xh, r0, g0_ref[...])

    row = lambda i: (i, 0)
    body, dep_specs, deps = _after(body, 8, after)
    return pl.pallas_call(
        body, name="bwd_in", grid=(T // tm,),
        in_specs=[pl.BlockSpec((tm, w), row) for w in IN_SEG_WIDTHS] + [
            _resident((D, IN_W)), pl.BlockSpec((tm, D), row), pl.BlockSpec((tm, D), row),
            pl.BlockSpec((1, D), lambda i: (0, 0))] + dep_specs,
        out_specs=[pl.BlockSpec((tm, D), row), pl.BlockSpec((1, D), lambda i: (0, 0))],
        out_shape=[SDS((T, D), F32), SDS((1, D), F32)],
        compiler_params=_params(1),
    )(duv, dq, dkv, dg, win, x, dx1, g0, *deps)


def _wgrad_cols(h, segs, name):
    T = h.shape[0]
    tt = min(T, 1024)
    widths = [s.shape[1] for s in segs]

    def body(h_ref, *refs):
        dw_ref, acc = refs[-2], refs[-1]
        t = pl.program_id(0)

        @pl.when(t == 0)
        def _():
            acc[...] = jnp.zeros_like(acc)

        off = 0
        for ref, width in zip(refs[:-2], widths):
            acc[:, off:off + width] += _tn(h_ref[...], ref[...])
            off += width

        @pl.when(t == T // tt - 1)
        def _():
            dw_ref[...] = acc[...].astype(BF16)

    row = lambda t: (t, 0)
    return pl.pallas_call(
        body, name=name, grid=(T // tt,),
        in_specs=[pl.BlockSpec((tt, D), row)] + [pl.BlockSpec((tt, w), row) for w in widths],
        out_specs=pl.BlockSpec((D, sum(widths)), lambda t: (0, 0)),
        out_shape=SDS((D, sum(widths)), BF16),
        scratch_shapes=[pltpu.VMEM((D, sum(widths)), F32)],
        compiler_params=_params(1),
    )(h, *segs)


def _wgrad_in(h, duv, dq, dkv, dg):
    return jnp.concatenate([_wgrad_cols(h, [duv], "wgrad_in_uv"), _wgrad_cols(h, [dq, dkv], "wgrad_in_qkv"),
                            _wgrad_cols(h, [dg], "wgrad_in_gates")], axis=1)


def _place():
    x, y, c = lax.axis_index("x"), lax.axis_index("y"), lax.axis_index("c")
    return x, y, c, 4 * x + 2 * y + c


def _peers(x, y, c):
    out = []
    for mask in range(1, N_DEV):
        px = 1 - x if mask & 4 else x
        py = 1 - y if mask & 2 else y
        pc = 1 - c if mask & 1 else c
        out.append(((px, py, pc), 4 * px + 2 * py + pc))
    return out


def _all_to_all(arrays, gather, name):
    n = len(arrays)

    def body(*refs):
        ins, outs = refs[:n], refs[n:2 * n]
        send_sems, recv_sems, local_sems = refs[2 * n:]
        x, y, c, me = _place()
        local, sends, recvs = [], [], []
        for a in range(n):
            src_own = ins[a] if gather[a] else ins[a].at[me]
            local.append(pltpu.make_async_copy(src_own, outs[a].at[me], local_sems.at[a]))
            for k, (peer, pid) in enumerate(_peers(x, y, c)):
                sem = a * (N_DEV - 1) + k
                src = ins[a] if gather[a] else ins[a].at[pid]
                sends.append(pltpu.make_async_remote_copy(
                    src_ref=src, dst_ref=outs[a].at[me], send_sem=send_sems.at[sem], recv_sem=recv_sems.at[sem],
                    device_id=peer, device_id_type=MESH))
                recvs.append(pltpu.make_async_remote_copy(
                    src_ref=src, dst_ref=outs[a].at[pid], send_sem=send_sems.at[sem], recv_sem=recv_sems.at[sem],
                    device_id=peer, device_id_type=MESH))
        for cp in local + sends:
            cp.start()
        for cp in recvs:
            cp.wait_recv()
        for cp in sends:
            cp.wait_send()
        for cp in local:
            cp.wait()

    out_shape = [SDS((N_DEV,) + a.shape if gt else a.shape, a.dtype) for a, gt in zip(arrays, gather)]
    nsem = n * (N_DEV - 1)
    return pl.pallas_call(
        body, name=name,
        in_specs=[pl.BlockSpec(memory_space=pl.ANY)] * n,
        out_specs=[pl.BlockSpec(memory_space=pl.ANY)] * n,
        out_shape=out_shape,
        scratch_shapes=[pltpu.SemaphoreType.DMA((nsem,)), pltpu.SemaphoreType.DMA((nsem,)), pltpu.SemaphoreType.DMA((n,))],
    )(*arrays)


_HBM = pl.BlockSpec(memory_space=pltpu.HBM)
_SEM = pl.BlockSpec(memory_space=pltpu.SEMAPHORE)
_EFFECT = pltpu.SideEffectType.DATAFLOW_SIDE_EFFECTING
GATHER = "gather"
SCATTER = "scatter"
SPREAD = "spread"


def _zone_shape(a, mode):
    if mode == GATHER:
        return (N_DEV,) + a.shape
    return (N_DEV - 1,) + (a.shape[1:] if mode == SCATTER else a.shape)


def _start_copies(arrays, modes, name, after=None):
    n = len(arrays)
    zones = [lax.empty(_zone_shape(a, m), a.dtype) for a, m in zip(arrays, modes)]

    def body(*refs):
        ins, lands = refs[:n], refs[n:2 * n]
        send_sems, recv_sems = refs[-2 * n - 3], refs[-2 * n - 2]
        token = refs[-1]
        x, y, c, me = _place()
        for a in range(n):
            for k, (peer, pid) in enumerate(_peers(x, y, c)):
                src = ins[a].at[pid] if modes[a] == SCATTER else ins[a]
                dst = lands[a].at[me] if modes[a] == GATHER else lands[a].at[k]
                pltpu.make_async_remote_copy(src_ref=src, dst_ref=dst, send_sem=send_sems.at[a], recv_sem=recv_sems.at[a],
                                             device_id=peer, device_id_type=MESH).start()
        token[...] = jnp.zeros_like(token)

    hbm = lambda a: pltpu.HBM(a.shape, a.dtype)
    sems = pltpu.SemaphoreType.DMA((n,))
    extra = [] if after is None else [after]
    operands = [pltpu.with_memory_space_constraint(a, pltpu.HBM) for a in list(arrays) + zones]
    res = pl.pallas_call(
        body, name=name,
        out_shape=(sems, sems, *[hbm(a) for a in arrays], *[hbm(z) for z in zones], SDS((8, 128), F32)),
        in_specs=[_HBM] * (2 * n) + [_ANY] * len(extra),
        out_specs=(_SEM, _SEM, *[_HBM] * (2 * n), pl.BlockSpec(memory_space=pltpu.VMEM)),
        input_output_aliases={i: 2 + i for i in range(2 * n)},
        compiler_params=pltpu.CompilerParams(has_side_effects=_EFFECT),
    )(*operands, *extra)
    return res[0], res[1], list(res[2:2 + n]), list(res[2 + n:2 + 2 * n]), res[-1]


def _wait_copies(started, after, name):
    send_sems, recv_sems, thru, zones, _ = started
    n = len(thru)

    def body(*refs):
        lands = refs[n:2 * n]
        send_ref, recv_ref = refs[2 * n], refs[2 * n + 1]
        x, y, c, _ = _place()
        for a in range(n):
            seven = lands[a].at[pl.ds(0, N_DEV - 1)]
            cp = pltpu.make_async_remote_copy(src_ref=seven, dst_ref=seven, send_sem=send_ref.at[a], recv_sem=recv_ref.at[a],
                                              device_id=(x, y, 1 - c), device_id_type=MESH)
            cp.wait_send()
            cp.wait_recv()

    hbm = lambda a: pltpu.HBM(a.shape, a.dtype)
    res = pl.pallas_call(
        body, name=name,
        out_shape=tuple(hbm(a) for a in thru + zones),
        in_specs=[_HBM] * (2 * n) + [_SEM, _SEM, _ANY],
        out_specs=tuple([_HBM] * (2 * n)),
        input_output_aliases={i: i for i in range(2 * n)},
        compiler_params=pltpu.CompilerParams(has_side_effects=_EFFECT),
    )(*thru, *zones, send_sems, recv_sems, after)
    return list(res[:n]), list(res[n:])


def _adamw_math(g, w, m, v):
    m2 = ADAM_B1 * m + (1.0 - ADAM_B1) * g
    v2 = ADAM_B2 * v + (1.0 - ADAM_B2) * (g * g)
    m_hat = m2 / (1.0 - ADAM_B1 ** ADAM_STEP)
    v_hat = v2 / (1.0 - ADAM_B2 ** ADAM_STEP)
    delta = -ADAM_LR * (m_hat / (jnp.sqrt(v_hat) + ADAM_EPS) + ADAM_WD * w)
    return delta, m2, v2


def _sum_adamw(parts, w, m, v, name):
    R, C = w.shape
    tr = max(t for t in (128, 64, 32, 16, 8) if R % t == 0)

    def body(p_ref, w_ref, m_ref, v_ref, g_ref, d_ref, m2_ref, v2_ref):
        g = p_ref[0]
        for k in range(1, N_DEV):
            g = g + p_ref[k]
        g_ref[...] = g
        d_ref[...], m2_ref[...], v2_ref[...] = _adamw_math(g, w_ref[...], m_ref[...], v_ref[...])

    blk = pl.BlockSpec((tr, C), lambda i: (i, 0))
    return pl.pallas_call(
        body, name=name, grid=(R // tr,),
        in_specs=[pl.BlockSpec((N_DEV, tr, C), lambda i: (0, i, 0)), blk, blk, blk],
        out_specs=[blk] * 4,
        out_shape=[SDS((R, C), F32)] * 4,
        compiler_params=_params(1),
    )(parts, w, m, v)


def _sum_adamw_peers(me, own, parts, w, m, v, name, replicated):
    R, C = w.shape
    tr = max(t for t in (128, 64, 32, 16, 8) if R % t == 0)

    def body(me_ref, own_ref, p_ref, w_ref, m_ref, v_ref, g_ref, d_ref, m2_ref, v2_ref):
        if replicated:
            mine = me_ref[0]
            g = None
            for j in range(N_DEV):
                k = jnp.maximum(jnp.bitwise_xor(mine, j) - 1, 0)
                term = jnp.where(mine == j, own_ref[...], p_ref[k])
                g = term if g is None else g + term
        else:
            g = own_ref[...].astype(F32)
            for k in range(N_DEV - 1):
                g = g + p_ref[k].astype(F32)
        g_ref[...] = g
        d_ref[...], m2_ref[...], v2_ref[...] = _adamw_math(g, w_ref[...], m_ref[...], v_ref[...])

    blk = pl.BlockSpec((tr, C), lambda i, me_ref: (i, 0))
    own_spec = blk if replicated else pl.BlockSpec((None, tr, C), lambda i, me_ref: (me_ref[0], i, 0))
    return pl.pallas_call(
        body, name=name,
        grid_spec=pltpu.PrefetchScalarGridSpec(
            num_scalar_prefetch=1, grid=(R // tr,),
            in_specs=[own_spec, pl.BlockSpec((N_DEV - 1, tr, C), lambda i, me_ref: (0, i, 0)), blk, blk, blk],
            out_specs=[blk] * 4),
        out_shape=[SDS((R, C), F32)] * 4,
        compiler_params=_params(1),
    )(me, own, parts, w, m, v)


SMALL = ("ln_v_gain", "ln_v_bias", "w_spatial", "b_spatial", "sinks", "norm_mix_post", "norm_ff_pre", "norm_ff_post")
SMALL_ROWS = {"ln_v_gain": 8, "ln_v_bias": 8, "w_spatial": 1024, "b_spatial": 8, "sinks": 8,
              "norm_mix_post": 8, "norm_ff_pre": 8, "norm_ff_post": 8}
SMALL_PACK_ROWS = 1152


def _pack_small(vals):
    rows = []
    for name in SMALL:
        flat = vals[name].reshape(-1)
        pad = SMALL_ROWS[name] * 128 - flat.shape[0]
        if pad:
            flat = jnp.concatenate([flat, jnp.zeros((pad,), F32)])
        rows.append(flat.reshape(SMALL_ROWS[name], 128))
    rows.append(jnp.zeros((SMALL_PACK_ROWS - sum(SMALL_ROWS.values()), 128), F32))
    return jnp.concatenate(rows, axis=0)


def _unpack_small(packed, shapes):
    out, r = {}, 0
    for name in SMALL:
        n = 1
        for s in shapes[name]:
            n *= s
        out[name] = packed[r:r + SMALL_ROWS[name]].reshape(-1)[:n].reshape(shapes[name])
        r += SMALL_ROWS[name]
    return out


def _rope_rows():
    d = jnp.arange(128) % HEAD
    inv = ROPE_THETA ** (-(2.0 * (d % (ROPE // 2))).astype(F32) / ROPE)
    invf = jnp.where(d < ROPE, inv, 0.0).astype(F32).reshape(1, 128)
    sgn = jnp.where(d < ROPE // 2, -1.0, jnp.where(d < ROPE, 1.0, 0.0)).astype(F32).reshape(1, 128)
    return invf, sgn


def kernel(x, positions, w_in, ln_v_gain, ln_v_bias, w_spatial, b_spatial, sinks, w_a, w_b, w_o, norm_mix_pre, norm_mix_post, w_ff_in, w_ff_out, norm_ff_pre, norm_ff_post, loss_target, m_w_in, m_ln_v_gain, m_ln_v_bias, m_w_spatial, m_b_spatial, m_sinks, m_w_a, m_w_b, m_w_o, m_norm_mix_pre, m_norm_mix_post, m_w_ff_in, m_w_ff_out, m_norm_ff_pre, m_norm_ff_post, v_w_in, v_ln_v_gain, v_ln_v_bias, v_w_spatial, v_b_spatial, v_sinks, v_w_a, v_w_b, v_w_o, v_norm_mix_pre, v_norm_mix_post, v_w_ff_in, v_w_ff_out, v_norm_ff_pre, v_norm_ff_post):
    given = dict(locals())
    T = x.shape[1]
    xt = x[0]
    tgt = loss_target[0]
    posf = positions.astype(F32).reshape(T, 1)
    invf, sgn = _rope_rows()
    bst = b_spatial[0].T
    ws = w_spatial[0]

    me = 4 * lax.axis_index("x") + 2 * lax.axis_index("y") + lax.axis_index("c")
    me_arr = me.astype(jnp.int32).reshape(1)

    def with_own(zone, shard):
        return lax.dynamic_update_slice(zone, shard[None], (me,) + (0,) * shard.ndim)

    rest = ("w_a", "w_b", "w_o", "w_ff_in", "w_ff_out")
    shard = {n: given[n][0].astype(BF16) for n in ("w_in",) + rest}
    g_in = _start_copies([shard["w_in"]], [GATHER], "gather_in_start")
    g_rest = _start_copies([shard[n] for n in rest], [GATHER] * len(rest), "gather_rest_start", after=g_in[-1])
    (own_win,), (win8,) = _wait_copies(g_in, g_rest[-1], "gather_in_wait")
    win = jnp.transpose(with_own(win8, own_win), (1, 0, 2)).reshape(D, IN_W)

    proj, h = _fwd_in(xt, norm_mix_pre, win)
    a = _fwd_sgu(proj, ln_v_gain, ln_v_bias, ws, bst)
    att = _fwd_attn(proj, posf, invf, sgn, sinks[0])
    gw = {n: with_own(z, own) for n, own, z in zip(rest, *_wait_copies(g_rest, att, "gather_rest_wait"))}
    wa, wb, wo = (gw[n].reshape(D, D) for n in ("w_a", "w_b", "w_o"))
    wfi3 = gw["w_ff_in"]
    wfo = gw["w_ff_out"].reshape(D_FF, D)
    merged, a2, b2, mix, x1, hf = _fwd_mix(a, att, proj, xt, wa, wb, wo, norm_mix_post, norm_ff_pre)
    f, dy, dff, dg3, loss_part = _fwd_ff(hf, wfi3, wfo, x1, tgt, norm_ff_post)

    df, dx1, dmix, dg2, dg1 = _bwd_ff(dff, f, wfi3, wfo, x1, dy, mix, norm_mix_post, norm_ff_pre)
    dwfi3, dwfo = _wgrad_ff(hf, df, f, dff)
    own_ff = [dwfi3, dwfo.reshape(N_DEV, D_FF // N_DEV, D)]
    x_ff = _start_copies(own_ff, [SCATTER] * 2, "exchange_ff_start")
    da2, db2, dgate, da, datt = _bwd_mix(dmix, proj, a2, b2, wo, wa, wb, after=x_ff[-1])
    dwo, dwa, dwb = _wgrad_mix(merged, dmix, a, da2, att, db2)
    own_mix = [g.reshape(N_DEV, D // N_DEV, D) for g in (dwa, dwb, dwo)]
    x_mix = _start_copies(own_mix, [SCATTER] * 3, "exchange_mix_start")
    dq, dkv, dsink = _bwd_attn(proj, posf, invf, sgn, sinks[0], datt, after=x_mix[-1])
    duv, dws, dbs, dlng, dlnb = _bwd_sgu(proj, da, ln_v_gain, ln_v_bias, ws, bst)
    dwin = _wgrad_in(h, duv, dq, dkv, dgate)
    small_grads = {"ln_v_gain": dlng, "ln_v_bias": dlnb, "w_spatial": dws, "b_spatial": dbs, "sinks": dsink[:, :N_Q],
                   "norm_mix_post": dg1, "norm_ff_pre": dg2, "norm_ff_post": dg3}
    own_in = [jnp.transpose(dwin.reshape(D, N_DEV, IN_W // N_DEV), (1, 0, 2)), _pack_small(small_grads)]
    x_in = _start_copies(own_in, [SCATTER, SPREAD], "exchange_in_start")
    grad_x, dg0 = _bwd_in(duv, dq, dkv, dgate, win, xt, dx1, norm_mix_pre, after=x_in[-1])
    (dg0_all,) = _all_to_all([dg0.reshape(8, 128)], [True], "exchange_tail")

    results = {}

    def update(n, own, parts):
        results[n] = [r.reshape(given[n].shape) for r in _sum_adamw_peers(
            me_arr, own, parts, given[n][0], given["m_" + n][0], given["v_" + n][0], "adamw_" + n, False)]

    own_ff, p_ff = _wait_copies(x_ff, dg0_all, "exchange_ff_wait")
    update("w_ff_in", own_ff[0], p_ff[0])
    update("w_ff_out", own_ff[1], p_ff[1])
    own_mix, p_mix = _wait_copies(x_mix, p_ff[0], "exchange_mix_wait")
    for n, own, parts in zip(("w_a", "w_b", "w_o"), own_mix, p_mix):
        update(n, own, parts)
    own_in, p_in = _wait_copies(x_in, p_mix[0], "exchange_in_wait")
    update("w_in", own_in[0], p_in[0])
    packed = _sum_adamw_peers(me_arr, own_in[1], p_in[1], _pack_small({n: given[n] for n in SMALL}),
                              _pack_small({n: given["m_" + n] for n in SMALL}),
                              _pack_small({n: given["v_" + n] for n in SMALL}), "adamw_small", True)
    shapes = {n: given[n].shape for n in SMALL}
    unpacked = [_unpack_small(p, shapes) for p in packed]
    for n in SMALL:
        results[n] = [u[n] for u in unpacked]
    n = "norm_mix_pre"
    results[n] = [r.reshape(given[n].shape) for r in _sum_adamw(
        dg0_all, given[n].reshape(8, 128), given["m_" + n].reshape(8, 128), given["v_" + n].reshape(8, 128), "adamw_" + n)]

    loss = lax.psum(loss_part[0, 0], ("x", "y", "c"))
    order = ("w_in", "ln_v_gain", "ln_v_bias", "w_spatial", "b_spatial", "sinks", "w_a", "w_b", "w_o", "norm_mix_pre",
             "norm_mix_post", "w_ff_in", "w_ff_out", "norm_ff_pre", "norm_ff_post")
    out = [loss, grad_x.reshape(x.shape)]
    for k in range(4):
        out += [results[n][k] for n in order]
    return tuple(out)
```

```python
import functools

import jax
import jax.numpy as jnp
from jax import lax
from jax.experimental import pallas as pl
from jax.experimental.pallas import tpu as pltpu

F32 = jnp.float32
BF16 = jnp.bfloat16

N_DEV = 8
D = 1024
D_FF = 4096
IN_W = 5632
CHUNK = 128
GROUPS = 8
HEAD = 64
N_Q = 16
N_KV = 4
ROPE = 16
ROPE_THETA = 500000.0
EPS = 1e-6
OFF_Q, OFF_K, OFF_VA, OFF_GA, OFF_GB = 2048, 3072, 3328, 3584, 4608

ADAM_LR = 0.001
ADAM_B1 = 0.9
ADAM_B2 = 0.999
ADAM_EPS = 1e-08
ADAM_WD = 0.01
ADAM_STEP = 10

VMEM_LIMIT = 56 * 1024 * 1024

SDS = jax.ShapeDtypeStruct
MESH = pl.DeviceIdType.MESH


def _params(n_axes=None):
    if n_axes is None:
        return pltpu.CompilerParams(vmem_limit_bytes=VMEM_LIMIT)
    return pltpu.CompilerParams(dimension_semantics=("arbitrary",) * n_axes, vmem_limit_bytes=VMEM_LIMIT)


def _nt(a, b):
    return lax.dot_general(a, b, (((1,), (1,)), ((), ())), preferred_element_type=F32)


def _tn(a, b):
    return lax.dot_general(a, b, (((0,), (0,)), ((), ())), preferred_element_type=F32)


def _nn(a, b):
    return jnp.dot(a, b, preferred_element_type=F32)


def _gelu(x):
    t = jnp.tanh(0.7978845608028654 * (x + 0.044715 * (x * x * x)))
    return 0.5 * x * (1.0 + t), t


def _gelu_grad(x, t):
    return 0.5 * (1.0 + t) + 0.5 * x * (1.0 - t * t) * (0.7978845608028654 * (1.0 + 3.0 * 0.044715 * x * x))


def _sigmoid(x):
    return 1.0 / (1.0 + jnp.exp(-x))


def _rms_stats(v):
    r = lax.rsqrt(jnp.mean(v * v, axis=-1, keepdims=True) + EPS)
    return r, v * r


def _rms_bwd(d, vhat, r, g):
    gd = g * d
    return r * (gd - vhat * jnp.mean(gd * vhat, axis=-1, keepdims=True))


def _colsum(v):
    return jnp.sum(v, axis=0, keepdims=True)


_ANY = pl.BlockSpec(memory_space=pl.ANY)


def _after(body, n_in, after):
    if after is None:
        return body, [], []

    def ordered(*refs):
        return body(*refs[:n_in], *refs[n_in + 1:])

    return ordered, [_ANY], [after]


def _fwd_in(x, g0, win):
    T = x.shape[0]
    tm, tn = min(T, 1024), 1408

    def body(x_ref, g_ref, w_ref, p_ref, h_ref):
        @pl.when(pl.program_id(1) == 0)
        def _():
            _, xh = _rms_stats(x_ref[...])
            h_ref[...] = (xh * g_ref[...]).astype(BF16)

        p_ref[...] = _nn(h_ref[...], w_ref[...]).astype(BF16)

    return pl.pallas_call(
        body, name="fwd_in", grid=(T // tm, IN_W // tn),
        in_specs=[pl.BlockSpec((tm, D), lambda i, j: (i, 0)), pl.BlockSpec((1, D), lambda i, j: (0, 0)),
                  pl.BlockSpec((D, tn), lambda i, j: (0, j))],
        out_specs=[pl.BlockSpec((tm, tn), lambda i, j: (i, j)), pl.BlockSpec((tm, D), lambda i, j: (i, 0))],
        out_shape=[SDS((T, IN_W), BF16), SDS((T, D), BF16)],
        compiler_params=_params(2),
    )(x, g0, win)


def _sgu_forward_parts(u_ref, vs_ref, lng_ref, lnb_ref):
    u = u_ref[...].astype(F32)
    vs = vs_ref[...].astype(F32)
    gu, tu = _gelu(u)
    gv, tv = _gelu(vs)
    mu = jnp.mean(gv, axis=-1, keepdims=True)
    dv = gv - mu
    rstd = lax.rsqrt(jnp.mean(dv * dv, axis=-1, keepdims=True) + EPS)
    vhat = dv * rstd
    vn = (vhat * lng_ref[...] + lnb_ref[...]).astype(BF16)
    return u, vs, gu, tu, tv, rstd, vhat, vn


def _masked_ws(ws_ref, g):
    row = lax.broadcasted_iota(jnp.int32, (CHUNK, CHUNK), 0)
    col = lax.broadcasted_iota(jnp.int32, (CHUNK, CHUNK), 1)
    return jnp.where(row >= col, ws_ref[g], 0.0).astype(BF16)


def _fwd_sgu(proj, lng, lnb, ws, bst):
    T = proj.shape[0]
    tc = min(T, 512)

    def body(u_ref, vs_ref, lng_ref, lnb_ref, ws_ref, bst_ref, a_ref):
        _, _, gu, _, _, _, _, vn = _sgu_forward_parts(u_ref, vs_ref, lng_ref, lnb_ref)
        for g in range(GROUPS):
            wm = _masked_ws(ws_ref, g)
            cols = slice(g * CHUNK, (g + 1) * CHUNK)
            for c in range(tc // CHUNK):
                rows = slice(c * CHUNK, (c + 1) * CHUNK)
                mixed = _nn(wm, vn[rows, cols]) + bst_ref[:, g:g + 1]
                a_ref[rows, cols] = (gu[rows, cols] * mixed).astype(BF16)

    return pl.pallas_call(
        body, name="fwd_sgu", grid=(T // tc,),
        in_specs=[pl.BlockSpec((tc, D), lambda i: (i, 0)), pl.BlockSpec((tc, D), lambda i: (i, 1)),
                  pl.BlockSpec((1, D), lambda i: (0, 0)), pl.BlockSpec((1, D), lambda i: (0, 0)),
                  pl.BlockSpec((GROUPS, CHUNK, CHUNK), lambda i: (0, 0, 0)), pl.BlockSpec((CHUNK, GROUPS), lambda i: (0, 0))],
        out_specs=pl.BlockSpec((tc, D), lambda i: (i, 0)),
        out_shape=SDS((T, D), BF16),
        compiler_params=_params(1),
    )(proj, proj, lng, lnb, ws, bst)


def _rope_tables(posf, invf, sgn):
    T = posf.shape[0]
    tr = min(T, 1024)

    def body(pos_ref, invf_ref, sgn_ref, c_ref, s_ref):
        ang = pos_ref[...] * invf_ref[...]
        c_ref[...] = jnp.cos(ang)
        s_ref[...] = jnp.sin(ang) * sgn_ref[...]

    return pl.pallas_call(
        body, name="rope_tables", grid=(T // tr,),
        in_specs=[pl.BlockSpec((tr, 1), lambda i: (i, 0)), pl.BlockSpec((1, 128), lambda i: (0, 0)),
                  pl.BlockSpec((1, 128), lambda i: (0, 0))],
        out_specs=[pl.BlockSpec((tr, 128), lambda i: (i, 0))] * 2,
        out_shape=[SDS((T, 128), F32)] * 2,
        compiler_params=_params(1),
    )(posf, invf, sgn)


def _swap_halves(v):
    n = v.shape[1]
    d = lax.broadcasted_iota(jnp.int32, v.shape, 1) % HEAD
    upper = jnp.where(d < ROPE, pltpu.roll(v, ROPE // 2, 1), 0.0)
    return jnp.where(d < ROPE // 2, pltpu.roll(v, n - ROPE // 2, 1), upper)


def _rope(v, c, s):
    return v * c + _swap_halves(v) * s


def _rope_bwd(dv, c, s):
    return dv * c + _swap_halves(dv * s)


def _band_mask(first, key_axis):
    shape = (CHUNK, 2 * CHUNK) if key_axis == 1 else (2 * CHUNK, CHUNK)
    t = lax.broadcasted_iota(jnp.int32, shape, 1 - key_axis)
    j = lax.broadcasted_iota(jnp.int32, shape, key_axis)
    return (j > t) & (j <= t + CHUNK) & (jnp.logical_not(first) | (j >= CHUNK))


def _softmax_sink(s, sink, key_axis):
    m = jnp.maximum(jnp.max(s, axis=key_axis, keepdims=True), sink)
    p = jnp.exp(s - m)
    esink = jnp.exp(sink - m)
    inv = 1.0 / (jnp.sum(p, axis=key_axis, keepdims=True) + esink)
    return p * inv, esink * inv


def _head_pair_operand(band, g):
    slab = band[:, (g // 2) * 128:(g // 2 + 1) * 128]
    lo = lax.broadcasted_iota(jnp.int32, slab.shape, 1) < HEAD
    if g % 2 == 0:
        first = jnp.where(lo, slab, 0.0)
        second = pltpu.roll(first, HEAD, 1)
    else:
        second = jnp.where(lo, 0.0, slab)
        first = pltpu.roll(second, HEAD, 1)
    return jnp.concatenate([first, second], axis=0).astype(BF16)


def _head_pair_gradient(acc, g):
    top, bot = acc[:2 * CHUNK], acc[2 * CHUNK:]
    lo = lax.broadcasted_iota(jnp.int32, top.shape, 1) < HEAD
    if g % 2 == 0:
        return jnp.where(lo, top, 0.0) + pltpu.roll(jnp.where(lo, 0.0, bot), HEAD, 1)
    return pltpu.roll(jnp.where(lo, top, 0.0), HEAD, 1) + jnp.where(lo, 0.0, bot)


def _attn_specs(nb, clamp):
    cur = (lambda i: jnp.minimum(i, nb - 1)) if clamp else (lambda i: i)
    prev = lambda i: jnp.maximum(jnp.minimum(i, nb - 1) - 1, 0)
    kw = N_KV * HEAD
    return cur, prev, [
        pl.BlockSpec((CHUNK, D), lambda i: (cur(i), OFF_Q // D)),
        pl.BlockSpec((CHUNK, kw), lambda i: (prev(i), OFF_K // kw)),
        pl.BlockSpec((CHUNK, kw), lambda i: (cur(i), OFF_K // kw)),
        pl.BlockSpec((CHUNK, kw), lambda i: (prev(i), OFF_VA // kw)),
        pl.BlockSpec((CHUNK, kw), lambda i: (cur(i), OFF_VA // kw)),
        pl.BlockSpec((CHUNK, 128), lambda i: (prev(i), 0)),
        pl.BlockSpec((CHUNK, 128), lambda i: (cur(i), 0)),
        pl.BlockSpec((CHUNK, 128), lambda i: (prev(i), 0)),
        pl.BlockSpec((CHUNK, 128), lambda i: (cur(i), 0)),
        pl.BlockSpec(memory_space=pltpu.SMEM),
    ]


def _attn_load(q_ref, kp_ref, kc_ref, vp_ref, vc_ref, cp_ref, cc_ref, sp_ref, sc_ref):
    cq = jnp.tile(cc_ref[...], (1, D // 128))
    sq = jnp.tile(sc_ref[...], (1, D // 128))
    ck = jnp.tile(jnp.concatenate([cp_ref[...], cc_ref[...]], axis=0), (1, N_KV * HEAD // 128))
    sk = jnp.tile(jnp.concatenate([sp_ref[...], sc_ref[...]], axis=0), (1, N_KV * HEAD // 128))
    q = _rope(q_ref[...].astype(F32), cq, sq).astype(BF16)
    kb = _rope(jnp.concatenate([kp_ref[...], kc_ref[...]], axis=0).astype(F32), ck, sk)
    vb = jnp.concatenate([vp_ref[...], vc_ref[...]], axis=0).astype(F32)
    return q, kb, vb, (cq, sq, ck, sk)


PAIRS_PER_KV = N_Q // N_KV // 2


def _fwd_attn(proj, cos, sin, sinks):
    T = proj.shape[0]
    nb = T // CHUNK
    _, _, specs = _attn_specs(nb, False)

    def body(q_ref, kp_ref, kc_ref, vp_ref, vc_ref, cp_ref, cc_ref, sp_ref, sc_ref, sink_ref, o_ref):
        q, kb, vb, _ = _attn_load(q_ref, kp_ref, kc_ref, vp_ref, vc_ref, cp_ref, cc_ref, sp_ref, sc_ref)
        mask = _band_mask(pl.program_id(0) == 0, 0)
        for g in range(N_KV):
            k2 = _head_pair_operand(kb, g)
            v2 = _head_pair_operand(vb, g)
            for r in range(PAIRS_PER_KV):
                pair = g * PAIRS_PER_KV + r
                s2 = _nt(k2, q[:, pair * 128:(pair + 1) * 128]) * (HEAD ** -0.5)
                ps = []
                for e in range(2):
                    s = jnp.where(mask, s2[e * 2 * CHUNK:(e + 1) * 2 * CHUNK], -1e30)
                    ps.append(_softmax_sink(s, sink_ref[2 * pair + e], 0)[0].astype(BF16))
                o_ref[:, pair * 128:(pair + 1) * 128] = _tn(jnp.concatenate(ps, axis=0), v2).astype(BF16)

    return pl.pallas_call(
        body, name="fwd_attn", grid=(nb,), in_specs=specs,
        out_specs=pl.BlockSpec((CHUNK, D), lambda i: (i, 0)),
        out_shape=SDS((T, D), BF16),
        compiler_params=_params(1),
    )(proj, proj, proj, proj, proj, cos, cos, sin, sin, sinks)


def _fwd_mix(a, att, proj, x, wa, wb, wo, g1, g2):
    T = x.shape[0]
    tm = min(T, 512)
    half = D // 2

    def body(a_ref, att_ref, ga0, ga1, gb0, gb1, x_ref, wa_ref, wb_ref, wo_ref, g1_ref, g2_ref,
             mg_ref, a2_ref, b2_ref, mix_ref, x1_ref, hf_ref):
        a2 = _nn(a_ref[...], wa_ref[...])
        b2 = _nn(att_ref[...], wb_ref[...])
        ga = jnp.concatenate([ga0[...], ga1[...]], axis=1).astype(F32)
        gb = jnp.concatenate([gb0[...], gb1[...]], axis=1).astype(F32)
        merged = (_sigmoid(ga) * a2 + _sigmoid(gb) * b2).astype(BF16)
        a2_ref[...] = a2.astype(BF16)
        b2_ref[...] = b2.astype(BF16)
        mg_ref[...] = merged
        mix = _nn(merged, wo_ref[...])
        mix_ref[...] = mix
        _, mh = _rms_stats(mix)
        x1 = x_ref[...] + mh * g1_ref[...]
        x1_ref[...] = x1
        _, xh = _rms_stats(x1)
        hf_ref[...] = (xh * g2_ref[...]).astype(BF16)

    row = lambda i: (i, 0)
    const = lambda i: (0, 0)
    gspec = lambda off: pl.BlockSpec((tm, half), lambda i: (i, off // half))
    return pl.pallas_call(
        body, name="fwd_mix", grid=(T // tm,),
        in_specs=[pl.BlockSpec((tm, D), row), pl.BlockSpec((tm, D), row),
                  gspec(OFF_GA), gspec(OFF_GA + half), gspec(OFF_GB), gspec(OFF_GB + half),
                  pl.BlockSpec((tm, D), row), _resident((D, D)), _resident((D, D)),
                  _resident((D, D)), pl.BlockSpec((1, D), const), pl.BlockSpec((1, D), const)],
        out_specs=[pl.BlockSpec((tm, D), row)] * 6,
        out_shape=[SDS((T, D), BF16), SDS((T, D), BF16), SDS((T, D), BF16), SDS((T, D), F32), SDS((T, D), F32),
                   SDS((T, D), BF16)],
        compiler_params=_params(1),
    )(a, att, proj, proj, proj, proj, x, wa, wb, wo, g1, g2)


FF_SPLIT = N_DEV
FF_TILE = D_FF // FF_SPLIT
FF_STEP = 2048
FF_SLABS = FF_STEP // FF_TILE
FF_STEPS = D_FF // FF_STEP


def _fwd_ff(hf, wfi3, wfo, x1, tgt, g3):
    T = hf.shape[0]
    tm = min(T, 512)
    last = FF_STEPS - 1

    def body(hf_ref, wfi_ref, wfo_ref, x1_ref, tgt_ref, g3_ref, f_ref, dy_ref, dff_ref, dg3_ref, loss_ref, acc, r_s):
        i, p = pl.program_id(0), pl.program_id(1)

        @pl.when((i == 0) & (p == 0))
        def _():
            dg3_ref[...] = jnp.zeros_like(dg3_ref)
            loss_ref[...] = jnp.zeros_like(loss_ref)

        hf_t = hf_ref[...]
        for s in range(FF_SLABS):
            cols = slice(s * FF_TILE, (s + 1) * FF_TILE)
            f = _nn(hf_t, wfi_ref[s]).astype(BF16)
            f_ref[:, cols] = f
            rl = jnp.maximum(f.astype(F32), 0.0)
            r_s[:, cols] = (rl * rl).astype(BF16)
        part = _nn(r_s[...], wfo_ref[...])

        @pl.when(p == 0)
        def _():
            acc[...] = part

        @pl.when(p > 0)
        def _():
            acc[...] += part

        @pl.when(p == last)
        def _():
            r3, fh = _rms_stats(acc[...])
            e = x1_ref[...] + fh * g3_ref[...] - tgt_ref[...]
            loss_ref[...] += jnp.sum(e * e) * (0.5 / D)
            dy = e * (1.0 / D)
            dy_ref[...] = dy
            dg3_ref[...] += _colsum(dy * fh)
            dff_ref[...] = _rms_bwd(dy, fh, r3, g3_ref[...]).astype(BF16)

    row = lambda i, p: (i, 0)
    const = lambda i, p: (0, 0)
    return pl.pallas_call(
        body, name="fwd_ff", grid=(T // tm, FF_STEPS),
        in_specs=[pl.BlockSpec((tm, D), row), pl.BlockSpec((FF_SLABS, D, FF_TILE), lambda i, p: (p, 0, 0)),
                  pl.BlockSpec((FF_STEP, D), lambda i, p: (p, 0)), pl.BlockSpec((tm, D), row),
                  pl.BlockSpec((tm, D), row), pl.BlockSpec((1, D), const)],
        out_specs=[pl.BlockSpec((tm, FF_STEP), lambda i, p: (i, p)), pl.BlockSpec((tm, D), row),
                   pl.BlockSpec((tm, D), row), pl.BlockSpec((1, D), const), pl.BlockSpec((1, 128), const)],
        out_shape=[SDS((T, D_FF), BF16), SDS((T, D), F32), SDS((T, D), BF16), SDS((1, D), F32), SDS((1, 128), F32)],
        scratch_shapes=[pltpu.VMEM((tm, D), F32), pltpu.VMEM((tm, FF_STEP), BF16)],
        compiler_params=_params(2),
    )(hf, wfi3, wfo, x1, tgt, g3)


def _bwd_ff(dff, f, wfi3, wfo, x1, dy, mix, g1, g2):
    T = dff.shape[0]
    tm = min(T, 512)
    last = FF_STEPS - 1

    def body(dff_ref, f_ref, wfi_ref, wfo_ref, x1_ref, dy_ref, mix_ref, g1_ref, g2_ref,
             df_ref, dx1_ref, dmix_ref, dg2_ref, dg1_ref, acc):
        i, p = pl.program_id(0), pl.program_id(1)

        @pl.when((i == 0) & (p == 0))
        def _():
            dg2_ref[...] = jnp.zeros_like(dg2_ref)
            dg1_ref[...] = jnp.zeros_like(dg1_ref)

        dr = _nt(dff_ref[...], wfo_ref[...])
        df_ref[...] = (dr * (2.0 * jnp.maximum(f_ref[...].astype(F32), 0.0))).astype(BF16)
        part = _nt(df_ref[:, :FF_TILE], wfi_ref[0])
        for s in range(1, FF_SLABS):
            part = part + _nt(df_ref[:, s * FF_TILE:(s + 1) * FF_TILE], wfi_ref[s])

        @pl.when(p == 0)
        def _():
            acc[...] = part

        @pl.when(p > 0)
        def _():
            acc[...] += part

        @pl.when(p == last)
        def _():
            dhf = acc[...]
            r2, xh = _rms_stats(x1_ref[...])
            dg2_ref[...] += _colsum(dhf * xh)
            dx1 = dy_ref[...] + _rms_bwd(dhf, xh, r2, g2_ref[...])
            dx1_ref[...] = dx1
            r1, mh = _rms_stats(mix_ref[...])
            dg1_ref[...] += _colsum(dx1 * mh)
            dmix_ref[...] = _rms_bwd(dx1, mh, r1, g1_ref[...]).astype(BF16)

    row = lambda i, p: (i, 0)
    const = lambda i, p: (0, 0)
    return pl.pallas_call(
        body, name="bwd_ff", grid=(T // tm, FF_STEPS),
        in_specs=[pl.BlockSpec((tm, D), row), pl.BlockSpec((tm, FF_STEP), lambda i, p: (i, p)),
                  pl.BlockSpec((FF_SLABS, D, FF_TILE), lambda i, p: (p, 0, 0)), pl.BlockSpec((FF_STEP, D), lambda i, p: (p, 0)),
                  pl.BlockSpec((tm, D), row), pl.BlockSpec((tm, D), row), pl.BlockSpec((tm, D), row),
                  pl.BlockSpec((1, D), const), pl.BlockSpec((1, D), const)],
        out_specs=[pl.BlockSpec((tm, FF_STEP), lambda i, p: (i, p)), pl.BlockSpec((tm, D), row),
                   pl.BlockSpec((tm, D), row), pl.BlockSpec((1, D), const), pl.BlockSpec((1, D), const)],
        out_shape=[SDS((T, D_FF), BF16), SDS((T, D), F32), SDS((T, D), BF16), SDS((1, D), F32), SDS((1, D), F32)],
        scratch_shapes=[pltpu.VMEM((tm, D), F32)],
        compiler_params=_params(2),
    )(dff, f, wfi3, wfo, x1, dy, mix, g1, g2)


def _wgrad_ff(hf, df, f, dff):
    T = hf.shape[0]
    tt = min(T, 1024)
    wide = 2 * FF_TILE

    def body(hf_ref, df_ref, f_ref, dff_ref, dwfi_ref, dwfo_ref, acc_i, acc_o):
        t = pl.program_id(1)

        @pl.when(t == 0)
        def _():
            acc_i[...] = jnp.zeros_like(acc_i)
            acc_o[...] = jnp.zeros_like(acc_o)

        acc_i[...] += _tn(hf_ref[...], df_ref[...])
        rl = jnp.maximum(f_ref[...].astype(F32), 0.0)
        acc_o[...] += _tn((rl * rl).astype(BF16), dff_ref[...])

        @pl.when(t == T // tt - 1)
        def _():
            dwfi_ref[0] = acc_i[:, :FF_TILE].astype(BF16)
            dwfi_ref[1] = acc_i[:, FF_TILE:].astype(BF16)
            dwfo_ref[...] = acc_o[...].astype(BF16)

    return pl.pallas_call(
        body, name="wgrad_ff", grid=(D_FF // wide, T // tt),
        in_specs=[pl.BlockSpec((tt, D), lambda p, t: (t, 0)), pl.BlockSpec((tt, wide), lambda p, t: (t, p)),
                  pl.BlockSpec((tt, wide), lambda p, t: (t, p)), pl.BlockSpec((tt, D), lambda p, t: (t, 0))],
        out_specs=[pl.BlockSpec((2, D, FF_TILE), lambda p, t: (p, 0, 0)), pl.BlockSpec((wide, D), lambda p, t: (p, 0))],
        out_shape=[SDS((FF_SPLIT, D, FF_TILE), BF16), SDS((D_FF, D), BF16)],
        scratch_shapes=[pltpu.VMEM((D, wide), F32), pltpu.VMEM((wide, D), F32)],
        compiler_params=_params(2),
    )(hf, df, f, dff)


def _bwd_mix(dmix, proj, a2, b2, wo, wa, wb, after=None):
    T = dmix.shape[0]
    tm = min(T, 512)
    half = D // 2

    def body(dmix_ref, ga0, ga1, gb0, gb1, a2_ref, b2_ref, wo_ref, wa_ref, wb_ref,
             da2_ref, db2_ref, dg_ref, da_ref, datt_ref):
        dmg = _nt(dmix_ref[...], wo_ref[...])
        sa = _sigmoid(jnp.concatenate([ga0[...], ga1[...]], axis=1).astype(F32))
        sb = _sigmoid(jnp.concatenate([gb0[...], gb1[...]], axis=1).astype(F32))
        da2 = (dmg * sa).astype(BF16)
        db2 = (dmg * sb).astype(BF16)
        da2_ref[...] = da2
        db2_ref[...] = db2
        dg_ref[:, :D] = (dmg * a2_ref[...].astype(F32) * (sa * (1.0 - sa))).astype(BF16)
        dg_ref[:, D:] = (dmg * b2_ref[...].astype(F32) * (sb * (1.0 - sb))).astype(BF16)
        da_ref[...] = _nt(da2, wa_ref[...]).astype(BF16)
        datt_ref[...] = _nt(db2, wb_ref[...]).astype(BF16)

    row = lambda i: (i, 0)
    const = lambda i: (0, 0)
    gspec = lambda off: pl.BlockSpec((tm, half), lambda i: (i, off // half))
    body, dep_specs, deps = _after(body, 10, after)
    return pl.pallas_call(
        body, name="bwd_mix", grid=(T // tm,),
        in_specs=[pl.BlockSpec((tm, D), row), gspec(OFF_GA), gspec(OFF_GA + half), gspec(OFF_GB), gspec(OFF_GB + half),
                  pl.BlockSpec((tm, D), row), pl.BlockSpec((tm, D), row),
                  _resident((D, D)), _resident((D, D)), _resident((D, D))] + dep_specs,
        out_specs=[pl.BlockSpec((tm, D), row), pl.BlockSpec((tm, D), row), pl.BlockSpec((tm, 2 * D), row),
                   pl.BlockSpec((tm, D), row), pl.BlockSpec((tm, D), row)],
        out_shape=[SDS((T, D), BF16), SDS((T, D), BF16), SDS((T, 2 * D), BF16), SDS((T, D), BF16), SDS((T, D), BF16)],
        compiler_params=_params(1),
    )(dmix, proj, proj, proj, proj, a2, b2, wo, wa, wb, *deps)


def _wgrad_mix(merged, dmix, a, da2, att, db2):
    T = merged.shape[0]
    tt = min(T, 512)

    def body(mg_ref, dmix_ref, a_ref, da2_ref, att_ref, db2_ref, dwo_ref, dwa_ref, dwb_ref, acc):
        t = pl.program_id(0)

        @pl.when(t == 0)
        def _():
            acc[...] = jnp.zeros_like(acc)

        acc[0] += _tn(mg_ref[...], dmix_ref[...])
        acc[1] += _tn(a_ref[...], da2_ref[...])
        acc[2] += _tn(att_ref[...], db2_ref[...])

        @pl.when(t == T // tt - 1)
        def _():
            dwo_ref[...] = acc[0].astype(BF16)
            dwa_ref[...] = acc[1].astype(BF16)
            dwb_ref[...] = acc[2].astype(BF16)

    return pl.pallas_call(
        body, name="wgrad_mix", grid=(T // tt,),
        in_specs=[pl.BlockSpec((tt, D), lambda t: (t, 0))] * 6,
        out_specs=[pl.BlockSpec((D, D), lambda t: (0, 0))] * 3,
        out_shape=[SDS((D, D), BF16)] * 3,
        scratch_shapes=[pltpu.VMEM((3, D, D), F32)],
        compiler_params=_params(1),
    )(merged, dmix, a, da2, att, db2)


def _bwd_attn(proj, cos, sin, sinks, datt, after=None):
    T = proj.shape[0]
    nb = T // CHUNK
    kw = N_KV * HEAD
    cur, prev, specs = _attn_specs(nb, True)

    def body(q_ref, kp_ref, kc_ref, vp_ref, vc_ref, cp_ref, cc_ref, sp_ref, sc_ref, sink_ref, do_ref,
             dq_ref, dkv_ref, dsink_ref, carry_k, carry_v, dq_acc):
        i = pl.program_id(0)

        @pl.when(i == 0)
        def _():
            carry_k[...] = jnp.zeros_like(carry_k)
            carry_v[...] = jnp.zeros_like(carry_v)
            dsink_ref[...] = jnp.zeros_like(dsink_ref)

        @pl.when(i < nb)
        def _():
            q, kb, vb, (cq, sq, ck, sk) = _attn_load(q_ref, kp_ref, kc_ref, vp_ref, vc_ref, cp_ref, cc_ref,
                                                     sp_ref, sc_ref)
            mask = _band_mask(i == 0, 0)
            do = do_ref[...]
            lane = lax.broadcasted_iota(jnp.int32, (1, 128), 1)
            dsink = jnp.zeros((1, 128), F32)
            dks, dvs = [], []
            for g in range(N_KV):
                k2 = _head_pair_operand(kb, g)
                v2 = _head_pair_operand(vb, g)
                dk2 = jnp.zeros((4 * CHUNK, 128), F32)
                dv2 = jnp.zeros((4 * CHUNK, 128), F32)
                for r in range(PAIRS_PER_KV):
                    pair = g * PAIRS_PER_KV + r
                    qp = q[:, pair * 128:(pair + 1) * 128]
                    dop = do[:, pair * 128:(pair + 1) * 128]
                    s2 = _nt(k2, qp) * (HEAD ** -0.5)
                    dp2 = _nt(v2, dop)
                    ps, dss = [], []
                    for e in range(2):
                        rows = slice(e * 2 * CHUNK, (e + 1) * 2 * CHUNK)
                        p, psink = _softmax_sink(jnp.where(mask, s2[rows], -1e30), sink_ref[2 * pair + e], 0)
                        dp = dp2[rows]
                        delta = jnp.sum(p * dp, axis=0, keepdims=True)
                        ps.append(p.astype(BF16))
                        dss.append((p * (dp - delta) * (HEAD ** -0.5)).astype(BF16))
                        dsink = dsink + jnp.where(lane == 2 * pair + e, -jnp.sum(psink * delta), 0.0)
                    ds2 = jnp.concatenate(dss, axis=0)
                    dq_acc[:, pair * 128:(pair + 1) * 128] = _tn(ds2, k2)
                    dk2 = dk2 + _nn(ds2, qp)
                    dv2 = dv2 + _nn(jnp.concatenate(ps, axis=0), dop)
                dks.append(_head_pair_gradient(dk2, g))
                dvs.append(_head_pair_gradient(dv2, g))
            dsink_ref[...] += dsink
            dq_ref[...] = _rope_bwd(dq_acc[...], cq, sq).astype(BF16)
            dkb = _rope_bwd(jnp.concatenate([dks[0] + dks[1], dks[2] + dks[3]], axis=1), ck, sk)
            dvb = jnp.concatenate([dvs[0] + dvs[1], dvs[2] + dvs[3]], axis=1)
            dkv_ref[:, :kw] = (carry_k[...] + dkb[:CHUNK]).astype(BF16)
            dkv_ref[:, kw:] = (carry_v[...] + dvb[:CHUNK]).astype(BF16)
            carry_k[...] = dkb[CHUNK:]
            carry_v[...] = dvb[CHUNK:]

        @pl.when(i == nb)
        def _():
            dkv_ref[:, :kw] = carry_k[...].astype(BF16)
            dkv_ref[:, kw:] = carry_v[...].astype(BF16)

    body, dep_specs, deps = _after(body, 11, after)
    return pl.pallas_call(
        body, name="bwd_attn", grid=(nb + 1,),
        in_specs=specs + [pl.BlockSpec((CHUNK, D), lambda i: (cur(i), 0))] + dep_specs,
        out_specs=[pl.BlockSpec((CHUNK, D), lambda i: (cur(i), 0)),
                   pl.BlockSpec((CHUNK, 2 * kw), lambda i: (jnp.maximum(i - 1, 0), 0)),
                   pl.BlockSpec((1, 128), lambda i: (0, 0))],
        out_shape=[SDS((T, D), BF16), SDS((T, 2 * kw), BF16), SDS((1, 128), F32)],
        scratch_shapes=[pltpu.VMEM((CHUNK, kw), F32), pltpu.VMEM((CHUNK, kw), F32), pltpu.VMEM((CHUNK, D), F32)],
        compiler_params=_params(1),
    )(proj, proj, proj, proj, proj, cos, cos, sin, sin, sinks, datt, *deps)


def _bwd_sgu(proj, da, lng, lnb, ws, bst):
    T = proj.shape[0]
    tc = min(T, 512)
    nsteps = T // tc

    def body(u_ref, vs_ref, da_ref, lng_ref, lnb_ref, ws_ref, bst_ref,
             duv_ref, dws_ref, dbs_ref, dlng_ref, dlnb_ref, dvn_s, dgu_s, dmx_sum):
        i = pl.program_id(0)

        @pl.when(i == 0)
        def _():
            dws_ref[...] = jnp.zeros_like(dws_ref)
            dlng_ref[...] = jnp.zeros_like(dlng_ref)
            dlnb_ref[...] = jnp.zeros_like(dlnb_ref)
            dmx_sum[...] = jnp.zeros_like(dmx_sum)

        u, vs, gu, tu, tv, rstd, vhat, vn = _sgu_forward_parts(u_ref, vs_ref, lng_ref, lnb_ref)
        da = da_ref[...].astype(F32)
        for g in range(GROUPS):
            wm = _masked_ws(ws_ref, g)
            cols = slice(g * CHUNK, (g + 1) * CHUNK)
            dws = jnp.zeros((CHUNK, CHUNK), F32)
            dsum = jnp.zeros((CHUNK, CHUNK), F32)
            for c in range(tc // CHUNK):
                rows = slice(c * CHUNK, (c + 1) * CHUNK)
                vn_cg = vn[rows, cols]
                mixed = _nn(wm, vn_cg) + bst_ref[:, g:g + 1]
                dgu_s[rows, cols] = da[rows, cols] * mixed
                dmx = da[rows, cols] * gu[rows, cols]
                dmxb = dmx.astype(BF16)
                dws = dws + _nt(dmxb, vn_cg)
                dsum = dsum + dmx
                dvn_s[rows, cols] = _tn(wm, dmxb)
            dws_ref[g] += dws
            dmx_sum[:, cols] += dsum
        dvn = dvn_s[...]
        dlng_ref[...] += _colsum(dvn * vhat)
        dlnb_ref[...] += _colsum(dvn)
        dvh = dvn * lng_ref[...]
        dgv = rstd * (dvh - jnp.mean(dvh, axis=-1, keepdims=True) - vhat * jnp.mean(dvh * vhat, axis=-1, keepdims=True))
        duv_ref[:, :D] = (dgu_s[...] * _gelu_grad(u, tu)).astype(BF16)
        duv_ref[:, D:] = (dgv * _gelu_grad(vs, tv)).astype(BF16)

        @pl.when(i == nsteps - 1)
        def _():
            row = lax.broadcasted_iota(jnp.int32, (CHUNK, CHUNK), 0)
            col = lax.broadcasted_iota(jnp.int32, (CHUNK, CHUNK), 1)
            for g in range(GROUPS):
                dws_ref[g] = jnp.where(row >= col, dws_ref[g], 0.0)
                dbs_ref[g:g + 1, :] = _colsum(dmx_sum[:, g * CHUNK:(g + 1) * CHUNK].T)

    const2 = lambda i: (0, 0)
    return pl.pallas_call(
        body, name="bwd_sgu", grid=(nsteps,),
        in_specs=[pl.BlockSpec((tc, D), lambda i: (i, 0)), pl.BlockSpec((tc, D), lambda i: (i, 1)),
                  pl.BlockSpec((tc, D), lambda i: (i, 0)), pl.BlockSpec((1, D), const2), pl.BlockSpec((1, D), const2),
                  pl.BlockSpec((GROUPS, CHUNK, CHUNK), lambda i: (0, 0, 0)), pl.BlockSpec((CHUNK, GROUPS), const2)],
        out_specs=[pl.BlockSpec((tc, 2 * D), lambda i: (i, 0)), pl.BlockSpec((GROUPS, CHUNK, CHUNK), lambda i: (0, 0, 0)),
                   pl.BlockSpec((GROUPS, CHUNK), const2), pl.BlockSpec((1, D), const2), pl.BlockSpec((1, D), const2)],
        out_shape=[SDS((T, 2 * D), BF16), SDS((GROUPS, CHUNK, CHUNK), F32), SDS((GROUPS, CHUNK), F32),
                   SDS((1, D), F32), SDS((1, D), F32)],
        scratch_shapes=[pltpu.VMEM((tc, D), F32), pltpu.VMEM((tc, D), F32), pltpu.VMEM((CHUNK, D), F32)],
        compiler_params=_params(1),
    )(proj, proj, da, lng, lnb, ws, bst)


IN_SEG_WIDTHS = (2 * D, D, 2 * N_KV * HEAD, 2 * D)


def _resident(shape):
    return pl.BlockSpec(shape, lambda *_: (0,) * len(shape), pipeline_mode=pl.Buffered(1))


def _bwd_in(duv, dq, dkv, dg, win, x, dx1, g0, after=None):
    T = x.shape[0]
    tm = min(T, 512)

    def body(duv_ref, dq_ref, dkv_ref, dg_ref, w_ref, x_ref, dx1_ref, g0_ref, gx_ref, dg0_ref):
        @pl.when(pl.program_id(0) == 0)
        def _():
            dg0_ref[...] = jnp.zeros_like(dg0_ref)

        dh, off = None, 0
        for ref, width in zip((duv_ref, dq_ref, dkv_ref, dg_ref), IN_SEG_WIDTHS):
            part = _nt(ref[...], w_ref[:, off:off + width])
            dh = part if dh is None else dh + part
            off += width
        r0, xh = _rms_stats(x_ref[...])
        dg0_ref[...] += _colsum(dh * xh)
        gx_ref[...] = dx1_ref[...] + _rms_bwd(dh, xh, r0, g0_ref[...])

    row = lambda i: (i, 0)
    body, dep_specs, deps = _after(body, 8, after)
    return pl.pallas_call(
        body, name="bwd_in", grid=(T // tm,),
        in_specs=[pl.BlockSpec((tm, w), row) for w in IN_SEG_WIDTHS] + [
            _resident((D, IN_W)), pl.BlockSpec((tm, D), row), pl.BlockSpec((tm, D), row),
            pl.BlockSpec((1, D), lambda i: (0, 0))] + dep_specs,
        out_specs=[pl.BlockSpec((tm, D), row), pl.BlockSpec((1, D), lambda i: (0, 0))],
        out_shape=[SDS((T, D), F32), SDS((1, D), F32)],
        compiler_params=_params(1),
    )(duv, dq, dkv, dg, win, x, dx1, g0, *deps)


def _wgrad_cols(h, segs, name):
    T = h.shape[0]
    tt = min(T, 1024)
    widths = [s.shape[1] for s in segs]

    def body(h_ref, *refs):
        dw_ref, acc = refs[-2], refs[-1]
        t = pl.program_id(0)

        @pl.when(t == 0)
        def _():
            acc[...] = jnp.zeros_like(acc)

        off = 0
        for ref, width in zip(refs[:-2], widths):
            acc[:, off:off + width] += _tn(h_ref[...], ref[...])
            off += width

        @pl.when(t == T // tt - 1)
        def _():
            dw_ref[...] = acc[...].astype(BF16)

    row = lambda t: (t, 0)
    return pl.pallas_call(
        body, name=name, grid=(T // tt,),
        in_specs=[pl.BlockSpec((tt, D), row)] + [pl.BlockSpec((tt, w), row) for w in widths],
        out_specs=pl.BlockSpec((D, sum(widths)), lambda t: (0, 0)),
        out_shape=SDS((D, sum(widths)), BF16),
        scratch_shapes=[pltpu.VMEM((D, sum(widths)), F32)],
        compiler_params=_params(1),
    )(h, *segs)


def _wgrad_in(h, duv, dq, dkv, dg):
    return jnp.concatenate([_wgrad_cols(h, [duv], "wgrad_in_uv"), _wgrad_cols(h, [dq, dkv], "wgrad_in_qkv"),
                            _wgrad_cols(h, [dg], "wgrad_in_gates")], axis=1)


def _place():
    x, y, c = lax.axis_index("x"), lax.axis_index("y"), lax.axis_index("c")
    return x, y, c, 4 * x + 2 * y + c


def _peers(x, y, c):
    out = []
    for mask in range(1, N_DEV):
        px = 1 - x if mask & 4 else x
        py = 1 - y if mask & 2 else y
        pc = 1 - c if mask & 1 else c
        out.append(((px, py, pc), 4 * px + 2 * py + pc))
    return out


def _all_to_all(arrays, gather, name):
    n = len(arrays)

    def body(*refs):
        ins, outs = refs[:n], refs[n:2 * n]
        send_sems, recv_sems, local_sems = refs[2 * n:]
        x, y, c, me = _place()
        local, sends, recvs = [], [], []
        for a in range(n):
            src_own = ins[a] if gather[a] else ins[a].at[me]
            local.append(pltpu.make_async_copy(src_own, outs[a].at[me], local_sems.at[a]))
            for k, (peer, pid) in enumerate(_peers(x, y, c)):
                sem = a * (N_DEV - 1) + k
                src = ins[a] if gather[a] else ins[a].at[pid]
                sends.append(pltpu.make_async_remote_copy(
                    src_ref=src, dst_ref=outs[a].at[me], send_sem=send_sems.at[sem], recv_sem=recv_sems.at[sem],
                    device_id=peer, device_id_type=MESH))
                recvs.append(pltpu.make_async_remote_copy(
                    src_ref=src, dst_ref=outs[a].at[pid], send_sem=send_sems.at[sem], recv_sem=recv_sems.at[sem],
                    device_id=peer, device_id_type=MESH))
        for cp in local + sends:
            cp.start()
        for cp in recvs:
            cp.wait_recv()
        for cp in sends:
            cp.wait_send()
        for cp in local:
            cp.wait()

    out_shape = [SDS((N_DEV,) + a.shape if gt else a.shape, a.dtype) for a, gt in zip(arrays, gather)]
    nsem = n * (N_DEV - 1)
    return pl.pallas_call(
        body, name=name,
        in_specs=[pl.BlockSpec(memory_space=pl.ANY)] * n,
        out_specs=[pl.BlockSpec(memory_space=pl.ANY)] * n,
        out_shape=out_shape,
        scratch_shapes=[pltpu.SemaphoreType.DMA((nsem,)), pltpu.SemaphoreType.DMA((nsem,)), pltpu.SemaphoreType.DMA((n,))],
    )(*arrays)


_HBM = pl.BlockSpec(memory_space=pltpu.HBM)
_SEM = pl.BlockSpec(memory_space=pltpu.SEMAPHORE)
_EFFECT = pltpu.SideEffectType.DATAFLOW_SIDE_EFFECTING
GATHER = "gather"
SCATTER = "scatter"
SPREAD = "spread"


def _zone_shape(a, mode):
    if mode == GATHER:
        return (N_DEV,) + a.shape
    return (N_DEV - 1,) + (a.shape[1:] if mode == SCATTER else a.shape)


def _start_copies(arrays, modes, name, after=None):
    n = len(arrays)
    zones = [lax.empty(_zone_shape(a, m), a.dtype) for a, m in zip(arrays, modes)]

    def body(*refs):
        ins, lands = refs[:n], refs[n:2 * n]
        send_sems, recv_sems = refs[-2 * n - 3], refs[-2 * n - 2]
        token = refs[-1]
        x, y, c, me = _place()
        for a in range(n):
            for k, (peer, pid) in enumerate(_peers(x, y, c)):
                src = ins[a].at[pid] if modes[a] == SCATTER else ins[a]
                dst = lands[a].at[me] if modes[a] == GATHER else lands[a].at[k]
                pltpu.make_async_remote_copy(src_ref=src, dst_ref=dst, send_sem=send_sems.at[a], recv_sem=recv_sems.at[a],
                                             device_id=peer, device_id_type=MESH).start()
        token[...] = jnp.zeros_like(token)

    hbm = lambda a: pltpu.HBM(a.shape, a.dtype)
    sems = pltpu.SemaphoreType.DMA((n,))
    extra = [] if after is None else [after]
    operands = [pltpu.with_memory_space_constraint(a, pltpu.HBM) for a in list(arrays) + zones]
    res = pl.pallas_call(
        body, name=name,
        out_shape=(sems, sems, *[hbm(a) for a in arrays], *[hbm(z) for z in zones], SDS((8, 128), F32)),
        in_specs=[_HBM] * (2 * n) + [_ANY] * len(extra),
        out_specs=(_SEM, _SEM, *[_HBM] * (2 * n), pl.BlockSpec(memory_space=pltpu.VMEM)),
        input_output_aliases={i: 2 + i for i in range(2 * n)},
        compiler_params=pltpu.CompilerParams(has_side_effects=_EFFECT),
    )(*operands, *extra)
    return res[0], res[1], list(res[2:2 + n]), list(res[2 + n:2 + 2 * n]), res[-1]


def _wait_copies(started, after, name):
    send_sems, recv_sems, thru, zones, _ = started
    n = len(thru)

    def body(*refs):
        lands = refs[n:2 * n]
        send_ref, recv_ref = refs[2 * n], refs[2 * n + 1]
        x, y, c, _ = _place()
        for a in range(n):
            seven = lands[a].at[pl.ds(0, N_DEV - 1)]
            cp = pltpu.make_async_remote_copy(src_ref=seven, dst_ref=seven, send_sem=send_ref.at[a], recv_sem=recv_ref.at[a],
                                              device_id=(x, y, 1 - c), device_id_type=MESH)
            cp.wait_send()
            cp.wait_recv()

    hbm = lambda a: pltpu.HBM(a.shape, a.dtype)
    res = pl.pallas_call(
        body, name=name,
        out_shape=tuple(hbm(a) for a in thru + zones),
        in_specs=[_HBM] * (2 * n) + [_SEM, _SEM, _ANY],
        out_specs=tuple([_HBM] * (2 * n)),
        input_output_aliases={i: i for i in range(2 * n)},
        compiler_params=pltpu.CompilerParams(has_side_effects=_EFFECT),
    )(*thru, *zones, send_sems, recv_sems, after)
    return list(res[:n]), list(res[n:])


def _adamw_math(g, w, m, v):
    m2 = ADAM_B1 * m + (1.0 - ADAM_B1) * g
    v2 = ADAM_B2 * v + (1.0 - ADAM_B2) * (g * g)
    m_hat = m2 / (1.0 - ADAM_B1 ** ADAM_STEP)
    v_hat = v2 / (1.0 - ADAM_B2 ** ADAM_STEP)
    delta = -ADAM_LR * (m_hat / (jnp.sqrt(v_hat) + ADAM_EPS) + ADAM_WD * w)
    return delta, m2, v2


def _sum_adamw(parts, w, m, v, name):
    R, C = w.shape
    tr = max(t for t in (128, 64, 32, 16, 8) if R % t == 0)

    def body(p_ref, w_ref, m_ref, v_ref, g_ref, d_ref, m2_ref, v2_ref):
        g = p_ref[0]
        for k in range(1, N_DEV):
            g = g + p_ref[k]
        g_ref[...] = g
        d_ref[...], m2_ref[...], v2_ref[...] = _adamw_math(g, w_ref[...], m_ref[...], v_ref[...])

    blk = pl.BlockSpec((tr, C), lambda i: (i, 0))
    return pl.pallas_call(
        body, name=name, grid=(R // tr,),
        in_specs=[pl.BlockSpec((N_DEV, tr, C), lambda i: (0, i, 0)), blk, blk, blk],
        out_specs=[blk] * 4,
        out_shape=[SDS((R, C), F32)] * 4,
        compiler_params=_params(1),
    )(parts, w, m, v)


def _sum_adamw_peers(me, own, parts, w, m, v, name, replicated):
    R, C = w.shape
    tr = max(t for t in (128, 64, 32, 16, 8) if R % t == 0)

    def body(me_ref, own_ref, p_ref, w_ref, m_ref, v_ref, g_ref, d_ref, m2_ref, v2_ref):
        if replicated:
            mine = me_ref[0]
            g = None
            for j in range(N_DEV):
                k = jnp.maximum(jnp.bitwise_xor(mine, j) - 1, 0)
                term = jnp.where(mine == j, own_ref[...], p_ref[k])
                g = term if g is None else g + term
        else:
            g = own_ref[...].astype(F32)
            for k in range(N_DEV - 1):
                g = g + p_ref[k].astype(F32)
        g_ref[...] = g
        d_ref[...], m2_ref[...], v2_ref[...] = _adamw_math(g, w_ref[...], m_ref[...], v_ref[...])

    blk = pl.BlockSpec((tr, C), lambda i, me_ref: (i, 0))
    own_spec = blk if replicated else pl.BlockSpec((None, tr, C), lambda i, me_ref: (me_ref[0], i, 0))
    return pl.pallas_call(
        body, name=name,
        grid_spec=pltpu.PrefetchScalarGridSpec(
            num_scalar_prefetch=1, grid=(R // tr,),
            in_specs=[own_spec, pl.BlockSpec((N_DEV - 1, tr, C), lambda i, me_ref: (0, i, 0)), blk, blk, blk],
            out_specs=[blk] * 4),
        out_shape=[SDS((R, C), F32)] * 4,
        compiler_params=_params(1),
    )(me, own, parts, w, m, v)


SMALL = ("ln_v_gain", "ln_v_bias", "w_spatial", "b_spatial", "sinks", "norm_mix_post", "norm_ff_pre", "norm_ff_post")
SMALL_ROWS = {"ln_v_gain": 8, "ln_v_bias": 8, "w_spatial": 1024, "b_spatial": 8, "sinks": 8,
              "norm_mix_post": 8, "norm_ff_pre": 8, "norm_ff_post": 8}
SMALL_PACK_ROWS = 1152


def _pack_small(vals):
    rows = []
    for name in SMALL:
        flat = vals[name].reshape(-1)
        pad = SMALL_ROWS[name] * 128 - flat.shape[0]
        if pad:
            flat = jnp.concatenate([flat, jnp.zeros((pad,), F32)])
        rows.append(flat.reshape(SMALL_ROWS[name], 128))
    rows.append(jnp.zeros((SMALL_PACK_ROWS - sum(SMALL_ROWS.values()), 128), F32))
    return jnp.concatenate(rows, axis=0)


def _unpack_small(packed, shapes):
    out, r = {}, 0
    for name in SMALL:
        n = 1
        for s in shapes[name]:
            n *= s
        out[name] = packed[r:r + SMALL_ROWS[name]].reshape(-1)[:n].reshape(shapes[name])
        r += SMALL_ROWS[name]
    return out


def _rope_rows():
    d = jnp.arange(128) % HEAD
    inv = ROPE_THETA ** (-(2.0 * (d % (ROPE // 2))).astype(F32) / ROPE)
    invf = jnp.where(d < ROPE, inv, 0.0).astype(F32).reshape(1, 128)
    sgn = jnp.where(d < ROPE // 2, -1.0, jnp.where(d < ROPE, 1.0, 0.0)).astype(F32).reshape(1, 128)
    return invf, sgn


def kernel(x, positions, w_in, ln_v_gain, ln_v_bias, w_spatial, b_spatial, sinks, w_a, w_b, w_o, norm_mix_pre, norm_mix_post, w_ff_in, w_ff_out, norm_ff_pre, norm_ff_post, loss_target, m_w_in, m_ln_v_gain, m_ln_v_bias, m_w_spatial, m_b_spatial, m_sinks, m_w_a, m_w_b, m_w_o, m_norm_mix_pre, m_norm_mix_post, m_w_ff_in, m_w_ff_out, m_norm_ff_pre, m_norm_ff_post, v_w_in, v_ln_v_gain, v_ln_v_bias, v_w_spatial, v_b_spatial, v_sinks, v_w_a, v_w_b, v_w_o, v_norm_mix_pre, v_norm_mix_post, v_w_ff_in, v_w_ff_out, v_norm_ff_pre, v_norm_ff_post):
    given = dict(locals())
    T = x.shape[1]
    xt = x[0]
    tgt = loss_target[0]
    cos, sin = _rope_tables(positions.astype(F32).reshape(T, 1), *_rope_rows())
    bst = b_spatial[0].T
    ws = w_spatial[0]

    me = 4 * lax.axis_index("x") + 2 * lax.axis_index("y") + lax.axis_index("c")
    me_arr = me.astype(jnp.int32).reshape(1)

    def with_own(zone, shard):
        return lax.dynamic_update_slice(zone, shard[None], (me,) + (0,) * shard.ndim)

    rest = ("w_a", "w_b", "w_o", "w_ff_in", "w_ff_out")
    shard = {n: given[n][0].astype(BF16) for n in ("w_in",) + rest}
    g_in = _start_copies([shard["w_in"]], [GATHER], "gather_in_start")
    g_rest = _start_copies([shard[n] for n in rest], [GATHER] * len(rest), "gather_rest_start", after=g_in[-1])
    (own_win,), (win8,) = _wait_copies(g_in, g_rest[-1], "gather_in_wait")
    win = jnp.transpose(with_own(win8, own_win), (1, 0, 2)).reshape(D, IN_W)

    proj, h = _fwd_in(xt, norm_mix_pre, win)
    a = _fwd_sgu(proj, ln_v_gain, ln_v_bias, ws, bst)
    att = _fwd_attn(proj, cos, sin, sinks[0])
    gw = {n: with_own(z, own) for n, own, z in zip(rest, *_wait_copies(g_rest, att, "gather_rest_wait"))}
    wa, wb, wo = (gw[n].reshape(D, D) for n in ("w_a", "w_b", "w_o"))
    wfi3 = gw["w_ff_in"]
    wfo = gw["w_ff_out"].reshape(D_FF, D)
    merged, a2, b2, mix, x1, hf = _fwd_mix(a, att, proj, xt, wa, wb, wo, norm_mix_post, norm_ff_pre)
    f, dy, dff, dg3, loss_part = _fwd_ff(hf, wfi3, wfo, x1, tgt, norm_ff_post)

    df, dx1, dmix, dg2, dg1 = _bwd_ff(dff, f, wfi3, wfo, x1, dy, mix, norm_mix_post, norm_ff_pre)
    dwfi3, dwfo = _wgrad_ff(hf, df, f, dff)
    own_ff = [dwfi3, dwfo.reshape(N_DEV, D_FF // N_DEV, D)]
    x_ff = _start_copies(own_ff, [SCATTER] * 2, "exchange_ff_start")
    da2, db2, dgate, da, datt = _bwd_mix(dmix, proj, a2, b2, wo, wa, wb, after=x_ff[-1])
    dwo, dwa, dwb = _wgrad_mix(merged, dmix, a, da2, att, db2)
    own_mix = [g.reshape(N_DEV, D // N_DEV, D) for g in (dwa, dwb, dwo)]
    x_mix = _start_copies(own_mix, [SCATTER] * 3, "exchange_mix_start")
    dq, dkv, dsink = _bwd_attn(proj, cos, sin, sinks[0], datt, after=x_mix[-1])
    duv, dws, dbs, dlng, dlnb = _bwd_sgu(proj, da, ln_v_gain, ln_v_bias, ws, bst)
    dwin = _wgrad_in(h, duv, dq, dkv, dgate)
    small_grads = {"ln_v_gain": dlng, "ln_v_bias": dlnb, "w_spatial": dws, "b_spatial": dbs, "sinks": dsink[:, :N_Q],
                   "norm_mix_post": dg1, "norm_ff_pre": dg2, "norm_ff_post": dg3}
    own_in = [jnp.transpose(dwin.reshape(D, N_DEV, IN_W // N_DEV), (1, 0, 2)), _pack_small(small_grads)]
    x_in = _start_copies(own_in, [SCATTER, SPREAD], "exchange_in_start")
    grad_x, dg0 = _bwd_in(duv, dq, dkv, dgate, win, xt, dx1, norm_mix_pre, after=x_in[-1])

    results = {}

    def update(n, own, parts):
        results[n] = [r.reshape(given[n].shape) for r in _sum_adamw_peers(
            me_arr, own, parts, given[n][0], given["m_" + n][0], given["v_" + n][0], "adamw_" + n, False)]

    own_ff, p_ff = _wait_copies(x_ff, grad_x, "exchange_ff_wait")
    update("w_ff_in", own_ff[0], p_ff[0])
    update("w_ff_out", own_ff[1], p_ff[1])
    own_mix, p_mix = _wait_copies(x_mix, p_ff[0], "exchange_mix_wait")
    for n, own, parts in zip(("w_a", "w_b", "w_o"), own_mix, p_mix):
        update(n, own, parts)
    tail = jnp.concatenate([dg0.reshape(8, 128), jnp.tile(loss_part, (8, 1))], axis=0)
    (tail_all,) = _all_to_all([tail], [True], "exchange_tail")
    dg0_all = tail_all[:, :8]
    own_in, p_in = _wait_copies(x_in, tail_all, "exchange_in_wait")
    update("w_in", own_in[0], p_in[0])
    packed = _sum_adamw_peers(me_arr, own_in[1], p_in[1], _pack_small({n: given[n] for n in SMALL}),
                              _pack_small({n: given["m_" + n] for n in SMALL}),
                              _pack_small({n: given["v_" + n] for n in SMALL}), "adamw_small", True)
    shapes = {n: given[n].shape for n in SMALL}
    unpacked = [_unpack_small(p, shapes) for p in packed]
    for n in SMALL:
        results[n] = [u[n] for u in unpacked]
    n = "norm_mix_pre"
    results[n] = [r.reshape(given[n].shape) for r in _sum_adamw(
        dg0_all, given[n].reshape(8, 128), given["m_" + n].reshape(8, 128), given["v_" + n].reshape(8, 128), "adamw_" + n)]

    loss = jnp.sum(tail_all[:, 8, 0])
    order = ("w_in", "ln_v_gain", "ln_v_bias", "w_spatial", "b_spatial", "sinks", "w_a", "w_b", "w_o", "norm_mix_pre",
             "norm_mix_post", "w_ff_in", "w_ff_out", "norm_ff_pre", "norm_ff_post")
    out = [loss, grad_x.reshape(x.shape)]
    for k in range(4):
        out += [results[n][k] for n in order]
    return tuple(out)
```

```python
import functools

import jax
import jax.numpy as jnp
from jax import lax
from jax.experimental import pallas as pl
from jax.experimental.pallas import tpu as pltpu

F32 = jnp.float32
BF16 = jnp.bfloat16

N_DEV = 8
D = 1024
D_FF = 4096
IN_W = 5632
CHUNK = 128
GROUPS = 8
HEAD = 64
N_Q = 16
N_KV = 4
ROPE = 16
ROPE_THETA = 500000.0
EPS = 1e-6
OFF_Q, OFF_K, OFF_VA, OFF_GA, OFF_GB = 2048, 3072, 3328, 3584, 4608

ADAM_LR = 0.001
ADAM_B1 = 0.9
ADAM_B2 = 0.999
ADAM_EPS = 1e-08
ADAM_WD = 0.01
ADAM_STEP = 10

VMEM_LIMIT = 56 * 1024 * 1024

SDS = jax.ShapeDtypeStruct
MESH = pl.DeviceIdType.MESH


def _params(n_axes=None):
    if n_axes is None:
        return pltpu.CompilerParams(vmem_limit_bytes=VMEM_LIMIT)
    return pltpu.CompilerParams(dimension_semantics=("arbitrary",) * n_axes, vmem_limit_bytes=VMEM_LIMIT)


def _nt(a, b):
    return lax.dot_general(a, b, (((1,), (1,)), ((), ())), preferred_element_type=F32)


def _tn(a, b):
    return lax.dot_general(a, b, (((0,), (0,)), ((), ())), preferred_element_type=F32)


def _nn(a, b):
    return jnp.dot(a, b, preferred_element_type=F32)


def _gelu(x):
    t = jnp.tanh(0.7978845608028654 * (x + 0.044715 * (x * x * x)))
    return 0.5 * x * (1.0 + t), t


def _gelu_grad(x, t):
    return 0.5 * (1.0 + t) + 0.5 * x * (1.0 - t * t) * (0.7978845608028654 * (1.0 + 3.0 * 0.044715 * x * x))


def _sigmoid(x):
    return 1.0 / (1.0 + jnp.exp(-x))


def _rms_stats(v):
    r = lax.rsqrt(jnp.mean(v * v, axis=-1, keepdims=True) + EPS)
    return r, v * r


def _rms_bwd(d, vhat, r, g):
    gd = g * d
    return r * (gd - vhat * jnp.mean(gd * vhat, axis=-1, keepdims=True))


def _colsum(v):
    return jnp.sum(v, axis=0, keepdims=True)


_ANY = pl.BlockSpec(memory_space=pl.ANY)


def _after(body, n_in, after):
    if after is None:
        return body, [], []

    def ordered(*refs):
        return body(*refs[:n_in], *refs[n_in + 1:])

    return ordered, [_ANY], [after]


def _fwd_in(x, g0, win):
    T = x.shape[0]
    tm, tn = min(T, 1024), 1408

    def body(x_ref, g_ref, w_ref, p_ref, h_ref):
        @pl.when(pl.program_id(1) == 0)
        def _():
            _, xh = _rms_stats(x_ref[...])
            h_ref[...] = (xh * g_ref[...]).astype(BF16)

        p_ref[...] = _nn(h_ref[...], w_ref[...]).astype(BF16)

    return pl.pallas_call(
        body, name="fwd_in", grid=(T // tm, IN_W // tn),
        in_specs=[pl.BlockSpec((tm, D), lambda i, j: (i, 0)), pl.BlockSpec((1, D), lambda i, j: (0, 0)),
                  pl.BlockSpec((D, tn), lambda i, j: (0, j))],
        out_specs=[pl.BlockSpec((tm, tn), lambda i, j: (i, j)), pl.BlockSpec((tm, D), lambda i, j: (i, 0))],
        out_shape=[SDS((T, IN_W), BF16), SDS((T, D), BF16)],
        compiler_params=_params(2),
    )(x, g0, win)


def _sgu_forward_parts(u_ref, vs_ref, lng_ref, lnb_ref):
    u = u_ref[...].astype(F32)
    vs = vs_ref[...].astype(F32)
    gu, tu = _gelu(u)
    gv, tv = _gelu(vs)
    mu = jnp.mean(gv, axis=-1, keepdims=True)
    dv = gv - mu
    rstd = lax.rsqrt(jnp.mean(dv * dv, axis=-1, keepdims=True) + EPS)
    vhat = dv * rstd
    vn = (vhat * lng_ref[...] + lnb_ref[...]).astype(BF16)
    return u, vs, gu, tu, tv, rstd, vhat, vn


def _masked_ws(ws_ref, g):
    row = lax.broadcasted_iota(jnp.int32, (CHUNK, CHUNK), 0)
    col = lax.broadcasted_iota(jnp.int32, (CHUNK, CHUNK), 1)
    return jnp.where(row >= col, ws_ref[g], 0.0).astype(BF16)


def _fwd_sgu(proj, lng, lnb, ws, bst):
    T = proj.shape[0]
    tc = min(T, 512)

    def body(u_ref, vs_ref, lng_ref, lnb_ref, ws_ref, bst_ref, a_ref):
        _, _, gu, _, _, _, _, vn = _sgu_forward_parts(u_ref, vs_ref, lng_ref, lnb_ref)
        for g in range(GROUPS):
            wm = _masked_ws(ws_ref, g)
            cols = slice(g * CHUNK, (g + 1) * CHUNK)
            for c in range(tc // CHUNK):
                rows = slice(c * CHUNK, (c + 1) * CHUNK)
                mixed = _nn(wm, vn[rows, cols]) + bst_ref[:, g:g + 1]
                a_ref[rows, cols] = (gu[rows, cols] * mixed).astype(BF16)

    return pl.pallas_call(
        body, name="fwd_sgu", grid=(T // tc,),
        in_specs=[pl.BlockSpec((tc, D), lambda i: (i, 0)), pl.BlockSpec((tc, D), lambda i: (i, 1)),
                  pl.BlockSpec((1, D), lambda i: (0, 0)), pl.BlockSpec((1, D), lambda i: (0, 0)),
                  pl.BlockSpec((GROUPS, CHUNK, CHUNK), lambda i: (0, 0, 0)), pl.BlockSpec((CHUNK, GROUPS), lambda i: (0, 0))],
        out_specs=pl.BlockSpec((tc, D), lambda i: (i, 0)),
        out_shape=SDS((T, D), BF16),
        compiler_params=_params(1),
    )(proj, proj, lng, lnb, ws, bst)


def _rope_tables(posf, invf, sgn):
    T = posf.shape[0]
    tr = min(T, 1024)

    def body(pos_ref, invf_ref, sgn_ref, c_ref, s_ref):
        ang = pos_ref[...] * invf_ref[...]
        c_ref[...] = jnp.cos(ang)
        s_ref[...] = jnp.sin(ang) * sgn_ref[...]

    return pl.pallas_call(
        body, name="rope_tables", grid=(T // tr,),
        in_specs=[pl.BlockSpec((tr, 1), lambda i: (i, 0)), pl.BlockSpec((1, 128), lambda i: (0, 0)),
                  pl.BlockSpec((1, 128), lambda i: (0, 0))],
        out_specs=[pl.BlockSpec((tr, 128), lambda i: (i, 0))] * 2,
        out_shape=[SDS((T, 128), F32)] * 2,
        compiler_params=_params(1),
    )(posf, invf, sgn)


def _swap_halves(v):
    n = v.shape[1]
    d = lax.broadcasted_iota(jnp.int32, v.shape, 1) % HEAD
    upper = jnp.where(d < ROPE, pltpu.roll(v, ROPE // 2, 1), 0.0)
    return jnp.where(d < ROPE // 2, pltpu.roll(v, n - ROPE // 2, 1), upper)


def _rope(v, c, s):
    return v * c + _swap_halves(v) * s


def _rope_bwd(dv, c, s):
    return dv * c + _swap_halves(dv * s)


def _band_mask(first, key_axis):
    shape = (CHUNK, 2 * CHUNK) if key_axis == 1 else (2 * CHUNK, CHUNK)
    t = lax.broadcasted_iota(jnp.int32, shape, 1 - key_axis)
    j = lax.broadcasted_iota(jnp.int32, shape, key_axis)
    return (j > t) & (j <= t + CHUNK) & (jnp.logical_not(first) | (j >= CHUNK))


def _softmax_sink(s, sink, key_axis):
    m = jnp.maximum(jnp.max(s, axis=key_axis, keepdims=True), sink)
    p = jnp.exp(s - m)
    esink = jnp.exp(sink - m)
    inv = 1.0 / (jnp.sum(p, axis=key_axis, keepdims=True) + esink)
    return p * inv, esink * inv


def _head_pair_operand(band, g):
    slab = band[:, (g // 2) * 128:(g // 2 + 1) * 128]
    lo = lax.broadcasted_iota(jnp.int32, slab.shape, 1) < HEAD
    if g % 2 == 0:
        first = jnp.where(lo, slab, 0.0)
        second = pltpu.roll(first, HEAD, 1)
    else:
        second = jnp.where(lo, 0.0, slab)
        first = pltpu.roll(second, HEAD, 1)
    return jnp.concatenate([first, second], axis=0).astype(BF16)


def _head_pair_gradient(acc, g):
    top, bot = acc[:2 * CHUNK], acc[2 * CHUNK:]
    lo = lax.broadcasted_iota(jnp.int32, top.shape, 1) < HEAD
    if g % 2 == 0:
        return jnp.where(lo, top, 0.0) + pltpu.roll(jnp.where(lo, 0.0, bot), HEAD, 1)
    return pltpu.roll(jnp.where(lo, top, 0.0), HEAD, 1) + jnp.where(lo, 0.0, bot)


def _attn_specs(nb, clamp):
    cur = (lambda i: jnp.minimum(i, nb - 1)) if clamp else (lambda i: i)
    prev = lambda i: jnp.maximum(jnp.minimum(i, nb - 1) - 1, 0)
    kw = N_KV * HEAD
    return cur, prev, [
        pl.BlockSpec((CHUNK, D), lambda i: (cur(i), OFF_Q // D)),
        pl.BlockSpec((CHUNK, kw), lambda i: (prev(i), OFF_K // kw)),
        pl.BlockSpec((CHUNK, kw), lambda i: (cur(i), OFF_K // kw)),
        pl.BlockSpec((CHUNK, kw), lambda i: (prev(i), OFF_VA // kw)),
        pl.BlockSpec((CHUNK, kw), lambda i: (cur(i), OFF_VA // kw)),
        pl.BlockSpec((CHUNK, 128), lambda i: (prev(i), 0)),
        pl.BlockSpec((CHUNK, 128), lambda i: (cur(i), 0)),
        pl.BlockSpec((CHUNK, 128), lambda i: (prev(i), 0)),
        pl.BlockSpec((CHUNK, 128), lambda i: (cur(i), 0)),
        pl.BlockSpec(memory_space=pltpu.SMEM),
    ]


def _attn_load(q_ref, kp_ref, kc_ref, vp_ref, vc_ref, cp_ref, cc_ref, sp_ref, sc_ref):
    cq = jnp.tile(cc_ref[...], (1, D // 128))
    sq = jnp.tile(sc_ref[...], (1, D // 128))
    ck = jnp.tile(jnp.concatenate([cp_ref[...], cc_ref[...]], axis=0), (1, N_KV * HEAD // 128))
    sk = jnp.tile(jnp.concatenate([sp_ref[...], sc_ref[...]], axis=0), (1, N_KV * HEAD // 128))
    q = _rope(q_ref[...].astype(F32), cq, sq).astype(BF16)
    kb = _rope(jnp.concatenate([kp_ref[...], kc_ref[...]], axis=0).astype(F32), ck, sk)
    vb = jnp.concatenate([vp_ref[...], vc_ref[...]], axis=0).astype(F32)
    return q, kb, vb, (cq, sq, ck, sk)


PAIRS_PER_KV = N_Q // N_KV // 2


def _fwd_attn(proj, cos, sin, sinks):
    T = proj.shape[0]
    nb = T // CHUNK
    _, _, specs = _attn_specs(nb, False)

    def body(q_ref, kp_ref, kc_ref, vp_ref, vc_ref, cp_ref, cc_ref, sp_ref, sc_ref, sink_ref, o_ref):
        q, kb, vb, _ = _attn_load(q_ref, kp_ref, kc_ref, vp_ref, vc_ref, cp_ref, cc_ref, sp_ref, sc_ref)
        mask = _band_mask(pl.program_id(0) == 0, 0)
        for g in range(N_KV):
            k2 = _head_pair_operand(kb, g)
            v2 = _head_pair_operand(vb, g)
            for r in range(PAIRS_PER_KV):
                pair = g * PAIRS_PER_KV + r
                s2 = _nt(k2, q[:, pair * 128:(pair + 1) * 128]) * (HEAD ** -0.5)
                ps = []
                for e in range(2):
                    s = jnp.where(mask, s2[e * 2 * CHUNK:(e + 1) * 2 * CHUNK], -1e30)
                    ps.append(_softmax_sink(s, sink_ref[2 * pair + e], 0)[0].astype(BF16))
                o_ref[:, pair * 128:(pair + 1) * 128] = _tn(jnp.concatenate(ps, axis=0), v2).astype(BF16)

    return pl.pallas_call(
        body, name="fwd_attn", grid=(nb,), in_specs=specs,
        out_specs=pl.BlockSpec((CHUNK, D), lambda i: (i, 0)),
        out_shape=SDS((T, D), BF16),
        compiler_params=_params(1),
    )(proj, proj, proj, proj, proj, cos, cos, sin, sin, sinks)


def _fwd_mix(a, att, proj, x, wa, wb, wo, g1, g2):
    T = x.shape[0]
    tm = min(T, 512)
    half = D // 2

    def body(a_ref, att_ref, ga0, ga1, gb0, gb1, x_ref, wa_ref, wb_ref, wo_ref, g1_ref, g2_ref,
             mg_ref, a2_ref, b2_ref, mix_ref, x1_ref, hf_ref):
        a2 = _nn(a_ref[...], wa_ref[...])
        b2 = _nn(att_ref[...], wb_ref[...])
        ga = jnp.concatenate([ga0[...], ga1[...]], axis=1).astype(F32)
        gb = jnp.concatenate([gb0[...], gb1[...]], axis=1).astype(F32)
        merged = (_sigmoid(ga) * a2 + _sigmoid(gb) * b2).astype(BF16)
        a2_ref[...] = a2.astype(BF16)
        b2_ref[...] = b2.astype(BF16)
        mg_ref[...] = merged
        mix = _nn(merged, wo_ref[...])
        mix_ref[...] = mix
        _, mh = _rms_stats(mix)
        x1 = x_ref[...] + mh * g1_ref[...]
        x1_ref[...] = x1
        _, xh = _rms_stats(x1)
        hf_ref[...] = (xh * g2_ref[...]).astype(BF16)

    row = lambda i: (i, 0)
    const = lambda i: (0, 0)
    gspec = lambda off: pl.BlockSpec((tm, half), lambda i: (i, off // half))
    return pl.pallas_call(
        body, name="fwd_mix", grid=(T // tm,),
        in_specs=[pl.BlockSpec((tm, D), row), pl.BlockSpec((tm, D), row),
                  gspec(OFF_GA), gspec(OFF_GA + half), gspec(OFF_GB), gspec(OFF_GB + half),
                  pl.BlockSpec((tm, D), row), _resident((D, D)), _resident((D, D)),
                  _resident((D, D)), pl.BlockSpec((1, D), const), pl.BlockSpec((1, D), const)],
        out_specs=[pl.BlockSpec((tm, D), row)] * 6,
        out_shape=[SDS((T, D), BF16), SDS((T, D), BF16), SDS((T, D), BF16), SDS((T, D), F32), SDS((T, D), F32),
                   SDS((T, D), BF16)],
        compiler_params=_params(1),
    )(a, att, proj, proj, proj, proj, x, wa, wb, wo, g1, g2)


FF_SPLIT = N_DEV
FF_TILE = D_FF // FF_SPLIT
FF_STEP = 2048
FF_SLABS = FF_STEP // FF_TILE
FF_STEPS = D_FF // FF_STEP


def _fwd_ff(hf, wfi3, wfo, x1, tgt, g3):
    T = hf.shape[0]
    tm = min(T, 512)
    last = FF_STEPS - 1

    def body(hf_ref, wfi_ref, wfo_ref, x1_ref, tgt_ref, g3_ref, f_ref, dy_ref, dff_ref, dg3_ref, loss_ref, acc, r_s):
        i, p = pl.program_id(0), pl.program_id(1)

        @pl.when((i == 0) & (p == 0))
        def _():
            dg3_ref[...] = jnp.zeros_like(dg3_ref)
            loss_ref[...] = jnp.zeros_like(loss_ref)

        hf_t = hf_ref[...]
        for s in range(FF_SLABS):
            cols = slice(s * FF_TILE, (s + 1) * FF_TILE)
            f = _nn(hf_t, wfi_ref[s]).astype(BF16)
            f_ref[:, cols] = f
            rl = jnp.maximum(f.astype(F32), 0.0)
            r_s[:, cols] = (rl * rl).astype(BF16)
        part = _nn(r_s[...], wfo_ref[...])

        @pl.when(p == 0)
        def _():
            acc[...] = part

        @pl.when(p > 0)
        def _():
            acc[...] += part

        @pl.when(p == last)
        def _():
            r3, fh = _rms_stats(acc[...])
            e = x1_ref[...] + fh * g3_ref[...] - tgt_ref[...]
            loss_ref[...] += jnp.sum(e * e) * (0.5 / D)
            dy = e * (1.0 / D)
            dy_ref[...] = dy
            dg3_ref[...] += _colsum(dy * fh)
            dff_ref[...] = _rms_bwd(dy, fh, r3, g3_ref[...]).astype(BF16)

    row = lambda i, p: (i, 0)
    const = lambda i, p: (0, 0)
    return pl.pallas_call(
        body, name="fwd_ff", grid=(T // tm, FF_STEPS),
        in_specs=[pl.BlockSpec((tm, D), row), pl.BlockSpec((FF_SLABS, D, FF_TILE), lambda i, p: (p, 0, 0)),
                  pl.BlockSpec((FF_STEP, D), lambda i, p: (p, 0)), pl.BlockSpec((tm, D), row),
                  pl.BlockSpec((tm, D), row), pl.BlockSpec((1, D), const)],
        out_specs=[pl.BlockSpec((tm, FF_STEP), lambda i, p: (i, p)), pl.BlockSpec((tm, D), row),
                   pl.BlockSpec((tm, D), row), pl.BlockSpec((1, D), const), pl.BlockSpec((1, 128), const)],
        out_shape=[SDS((T, D_FF), BF16), SDS((T, D), F32), SDS((T, D), BF16), SDS((1, D), F32), SDS((1, 128), F32)],
        scratch_shapes=[pltpu.VMEM((tm, D), F32), pltpu.VMEM((tm, FF_STEP), BF16)],
        compiler_params=_params(2),
    )(hf, wfi3, wfo, x1, tgt, g3)


def _bwd_ff(dff, f, wfi3, wfo, x1, dy, mix, g1, g2):
    T = dff.shape[0]
    tm = min(T, 512)
    last = FF_STEPS - 1

    def body(dff_ref, f_ref, wfi_ref, wfo_ref, x1_ref, dy_ref, mix_ref, g1_ref, g2_ref,
             df_ref, dx1_ref, dmix_ref, dg2_ref, dg1_ref, acc):
        i, p = pl.program_id(0), pl.program_id(1)

        @pl.when((i == 0) & (p == 0))
        def _():
            dg2_ref[...] = jnp.zeros_like(dg2_ref)
            dg1_ref[...] = jnp.zeros_like(dg1_ref)

        dr = _nt(dff_ref[...], wfo_ref[...])
        df_ref[...] = (dr * (2.0 * jnp.maximum(f_ref[...].astype(F32), 0.0))).astype(BF16)
        part = _nt(df_ref[:, :FF_TILE], wfi_ref[0])
        for s in range(1, FF_SLABS):
            part = part + _nt(df_ref[:, s * FF_TILE:(s + 1) * FF_TILE], wfi_ref[s])

        @pl.when(p == 0)
        def _():
            acc[...] = part

        @pl.when(p > 0)
        def _():
            acc[...] += part

        @pl.when(p == last)
        def _():
            dhf = acc[...]
            r2, xh = _rms_stats(x1_ref[...])
            dg2_ref[...] += _colsum(dhf * xh)
            dx1 = dy_ref[...] + _rms_bwd(dhf, xh, r2, g2_ref[...])
            dx1_ref[...] = dx1
            r1, mh = _rms_stats(mix_ref[...])
            dg1_ref[...] += _colsum(dx1 * mh)
            dmix_ref[...] = _rms_bwd(dx1, mh, r1, g1_ref[...]).astype(BF16)

    row = lambda i, p: (i, 0)
    const = lambda i, p: (0, 0)
    return pl.pallas_call(
        body, name="bwd_ff", grid=(T // tm, FF_STEPS),
        in_specs=[pl.BlockSpec((tm, D), row), pl.BlockSpec((tm, FF_STEP), lambda i, p: (i, p)),
                  pl.BlockSpec((FF_SLABS, D, FF_TILE), lambda i, p: (p, 0, 0)), pl.BlockSpec((FF_STEP, D), lambda i, p: (p, 0)),
                  pl.BlockSpec((tm, D), row), pl.BlockSpec((tm, D), row), pl.BlockSpec((tm, D), row),
                  pl.BlockSpec((1, D), const), pl.BlockSpec((1, D), const)],
        out_specs=[pl.BlockSpec((tm, FF_STEP), lambda i, p: (i, p)), pl.BlockSpec((tm, D), row),
                   pl.BlockSpec((tm, D), row), pl.BlockSpec((1, D), const), pl.BlockSpec((1, D), const)],
        out_shape=[SDS((T, D_FF), BF16), SDS((T, D), F32), SDS((T, D), BF16), SDS((1, D), F32), SDS((1, D), F32)],
        scratch_shapes=[pltpu.VMEM((tm, D), F32)],
        compiler_params=_params(2),
    )(dff, f, wfi3, wfo, x1, dy, mix, g1, g2)


def _wgrad_ff(hf, df, f, dff):
    T = hf.shape[0]
    tt = min(T, 1024)
    wide = 2 * FF_TILE

    def body(hf_ref, df_ref, f_ref, dff_ref, dwfi_ref, dwfo_ref, acc_i, acc_o):
        t = pl.program_id(1)

        @pl.when(t == 0)
        def _():
            acc_i[...] = jnp.zeros_like(acc_i)
            acc_o[...] = jnp.zeros_like(acc_o)

        acc_i[...] += _tn(hf_ref[...], df_ref[...])
        rl = jnp.maximum(f_ref[...].astype(F32), 0.0)
        acc_o[...] += _tn((rl * rl).astype(BF16), dff_ref[...])

        @pl.when(t == T // tt - 1)
        def _():
            dwfi_ref[0] = acc_i[:, :FF_TILE].astype(BF16)
            dwfi_ref[1] = acc_i[:, FF_TILE:].astype(BF16)
            dwfo_ref[...] = acc_o[...].astype(BF16)

    return pl.pallas_call(
        body, name="wgrad_ff", grid=(D_FF // wide, T // tt),
        in_specs=[pl.BlockSpec((tt, D), lambda p, t: (t, 0)), pl.BlockSpec((tt, wide), lambda p, t: (t, p)),
                  pl.BlockSpec((tt, wide), lambda p, t: (t, p)), pl.BlockSpec((tt, D), lambda p, t: (t, 0))],
        out_specs=[pl.BlockSpec((2, D, FF_TILE), lambda p, t: (p, 0, 0)), pl.BlockSpec((wide, D), lambda p, t: (p, 0))],
        out_shape=[SDS((FF_SPLIT, D, FF_TILE), BF16), SDS((D_FF, D), BF16)],
        scratch_shapes=[pltpu.VMEM((D, wide), F32), pltpu.VMEM((wide, D), F32)],
        compiler_params=_params(2),
    )(hf, df, f, dff)


def _bwd_mix(dmix, proj, a2, b2, wo, wa, wb, after=None):
    T = dmix.shape[0]
    tm = min(T, 512)
    half = D // 2

    def body(dmix_ref, ga0, ga1, gb0, gb1, a2_ref, b2_ref, wo_ref, wa_ref, wb_ref,
             da2_ref, db2_ref, dg_ref, da_ref, datt_ref):
        dmg = _nt(dmix_ref[...], wo_ref[...])
        sa = _sigmoid(jnp.concatenate([ga0[...], ga1[...]], axis=1).astype(F32))
        sb = _sigmoid(jnp.concatenate([gb0[...], gb1[...]], axis=1).astype(F32))
        da2 = (dmg * sa).astype(BF16)
        db2 = (dmg * sb).astype(BF16)
        da2_ref[...] = da2
        db2_ref[...] = db2
        dg_ref[:, :D] = (dmg * a2_ref[...].astype(F32) * (sa * (1.0 - sa))).astype(BF16)
        dg_ref[:, D:] = (dmg * b2_ref[...].astype(F32) * (sb * (1.0 - sb))).astype(BF16)
        da_ref[...] = _nt(da2, wa_ref[...]).astype(BF16)
        datt_ref[...] = _nt(db2, wb_ref[...]).astype(BF16)

    row = lambda i: (i, 0)
    const = lambda i: (0, 0)
    gspec = lambda off: pl.BlockSpec((tm, half), lambda i: (i, off // half))
    body, dep_specs, deps = _after(body, 10, after)
    return pl.pallas_call(
        body, name="bwd_mix", grid=(T // tm,),
        in_specs=[pl.BlockSpec((tm, D), row), gspec(OFF_GA), gspec(OFF_GA + half), gspec(OFF_GB), gspec(OFF_GB + half),
                  pl.BlockSpec((tm, D), row), pl.BlockSpec((tm, D), row),
                  _resident((D, D)), _resident((D, D)), _resident((D, D))] + dep_specs,
        out_specs=[pl.BlockSpec((tm, D), row), pl.BlockSpec((tm, D), row), pl.BlockSpec((tm, 2 * D), row),
                   pl.BlockSpec((tm, D), row), pl.BlockSpec((tm, D), row)],
        out_shape=[SDS((T, D), BF16), SDS((T, D), BF16), SDS((T, 2 * D), BF16), SDS((T, D), BF16), SDS((T, D), BF16)],
        compiler_params=_params(1),
    )(dmix, proj, proj, proj, proj, a2, b2, wo, wa, wb, *deps)


def _wgrad_mix(merged, dmix, a, da2, att, db2):
    T = merged.shape[0]
    tt = min(T, 512)

    def body(mg_ref, dmix_ref, a_ref, da2_ref, att_ref, db2_ref, dwo_ref, dwa_ref, dwb_ref, acc):
        t = pl.program_id(0)

        @pl.when(t == 0)
        def _():
            acc[...] = jnp.zeros_like(acc)

        acc[0] += _tn(mg_ref[...], dmix_ref[...])
        acc[1] += _tn(a_ref[...], da2_ref[...])
        acc[2] += _tn(att_ref[...], db2_ref[...])

        @pl.when(t == T // tt - 1)
        def _():
            dwo_ref[...] = acc[0].astype(BF16)
            dwa_ref[...] = acc[1].astype(BF16)
            dwb_ref[...] = acc[2].astype(BF16)

    return pl.pallas_call(
        body, name="wgrad_mix", grid=(T // tt,),
        in_specs=[pl.BlockSpec((tt, D), lambda t: (t, 0))] * 6,
        out_specs=[pl.BlockSpec((D, D), lambda t: (0, 0))] * 3,
        out_shape=[SDS((D, D), BF16)] * 3,
        scratch_shapes=[pltpu.VMEM((3, D, D), F32)],
        compiler_params=_params(1),
    )(merged, dmix, a, da2, att, db2)


def _bwd_attn(proj, cos, sin, sinks, datt, after=None):
    T = proj.shape[0]
    nb = T // CHUNK
    kw = N_KV * HEAD
    cur, prev, specs = _attn_specs(nb, True)

    def body(q_ref, kp_ref, kc_ref, vp_ref, vc_ref, cp_ref, cc_ref, sp_ref, sc_ref, sink_ref, do_ref,
             dq_ref, dkv_ref, dsink_ref, carry_k, carry_v, dq_acc):
        i = pl.program_id(0)

        @pl.when(i == 0)
        def _():
            carry_k[...] = jnp.zeros_like(carry_k)
            carry_v[...] = jnp.zeros_like(carry_v)
            dsink_ref[...] = jnp.zeros_like(dsink_ref)

        @pl.when(i < nb)
        def _():
            q, kb, vb, (cq, sq, ck, sk) = _attn_load(q_ref, kp_ref, kc_ref, vp_ref, vc_ref, cp_ref, cc_ref,
                                                     sp_ref, sc_ref)
            mask = _band_mask(i == 0, 0)
            do = do_ref[...]
            lane = lax.broadcasted_iota(jnp.int32, (1, 128), 1)
            dsink = jnp.zeros((1, 128), F32)
            dks, dvs = [], []
            for g in range(N_KV):
                k2 = _head_pair_operand(kb, g)
                v2 = _head_pair_operand(vb, g)
                dk2 = jnp.zeros((4 * CHUNK, 128), F32)
                dv2 = jnp.zeros((4 * CHUNK, 128), F32)
                for r in range(PAIRS_PER_KV):
                    pair = g * PAIRS_PER_KV + r
                    qp = q[:, pair * 128:(pair + 1) * 128]
                    dop = do[:, pair * 128:(pair + 1) * 128]
                    s2 = _nt(k2, qp) * (HEAD ** -0.5)
                    dp2 = _nt(v2, dop)
                    ps, dss = [], []
                    for e in range(2):
                        rows = slice(e * 2 * CHUNK, (e + 1) * 2 * CHUNK)
                        p, psink = _softmax_sink(jnp.where(mask, s2[rows], -1e30), sink_ref[2 * pair + e], 0)
                        dp = dp2[rows]
                        delta = jnp.sum(p * dp, axis=0, keepdims=True)
                        ps.append(p.astype(BF16))
                        dss.append((p * (dp - delta) * (HEAD ** -0.5)).astype(BF16))
                        dsink = dsink + jnp.where(lane == 2 * pair + e, -jnp.sum(psink * delta), 0.0)
                    ds2 = jnp.concatenate(dss, axis=0)
                    dq_acc[:, pair * 128:(pair + 1) * 128] = _tn(ds2, k2)
                    dk2 = dk2 + _nn(ds2, qp)
                    dv2 = dv2 + _nn(jnp.concatenate(ps, axis=0), dop)
                dks.append(_head_pair_gradient(dk2, g))
                dvs.append(_head_pair_gradient(dv2, g))
            dsink_ref[...] += dsink
            dq_ref[...] = _rope_bwd(dq_acc[...], cq, sq).astype(BF16)
            dkb = _rope_bwd(jnp.concatenate([dks[0] + dks[1], dks[2] + dks[3]], axis=1), ck, sk)
            dvb = jnp.concatenate([dvs[0] + dvs[1], dvs[2] + dvs[3]], axis=1)
            dkv_ref[:, :kw] = (carry_k[...] + dkb[:CHUNK]).astype(BF16)
            dkv_ref[:, kw:] = (carry_v[...] + dvb[:CHUNK]).astype(BF16)
            carry_k[...] = dkb[CHUNK:]
            carry_v[...] = dvb[CHUNK:]

        @pl.when(i == nb)
        def _():
            dkv_ref[:, :kw] = carry_k[...].astype(BF16)
            dkv_ref[:, kw:] = carry_v[...].astype(BF16)

    body, dep_specs, deps = _after(body, 11, after)
    return pl.pallas_call(
        body, name="bwd_attn", grid=(nb + 1,),
        in_specs=specs + [pl.BlockSpec((CHUNK, D), lambda i: (cur(i), 0))] + dep_specs,
        out_specs=[pl.BlockSpec((CHUNK, D), lambda i: (cur(i), 0)),
                   pl.BlockSpec((CHUNK, 2 * kw), lambda i: (jnp.maximum(i - 1, 0), 0)),
                   pl.BlockSpec((1, 128), lambda i: (0, 0))],
        out_shape=[SDS((T, D), BF16), SDS((T, 2 * kw), BF16), SDS((1, 128), F32)],
        scratch_shapes=[pltpu.VMEM((CHUNK, kw), F32), pltpu.VMEM((CHUNK, kw), F32), pltpu.VMEM((CHUNK, D), F32)],
        compiler_params=_params(1),
    )(proj, proj, proj, proj, proj, cos, cos, sin, sin, sinks, datt, *deps)


def _bwd_sgu(proj, da, lng, lnb, ws, bst):
    T = proj.shape[0]
    tc = min(T, 512)
    nsteps = T // tc

    def body(u_ref, vs_ref, da_ref, lng_ref, lnb_ref, ws_ref, bst_ref,
             duv_ref, dws_ref, dbs_ref, dlng_ref, dlnb_ref, dvn_s, dgu_s, dmx_sum):
        i = pl.program_id(0)

        @pl.when(i == 0)
        def _():
            dws_ref[...] = jnp.zeros_like(dws_ref)
            dlng_ref[...] = jnp.zeros_like(dlng_ref)
            dlnb_ref[...] = jnp.zeros_like(dlnb_ref)
            dmx_sum[...] = jnp.zeros_like(dmx_sum)

        u, vs, gu, tu, tv, rstd, vhat, vn = _sgu_forward_parts(u_ref, vs_ref, lng_ref, lnb_ref)
        da = da_ref[...].astype(F32)
        for g in range(GROUPS):
            wm = _masked_ws(ws_ref, g)
            cols = slice(g * CHUNK, (g + 1) * CHUNK)
            dws = jnp.zeros((CHUNK, CHUNK), F32)
            dsum = jnp.zeros((CHUNK, CHUNK), F32)
            for c in range(tc // CHUNK):
                rows = slice(c * CHUNK, (c + 1) * CHUNK)
                vn_cg = vn[rows, cols]
                mixed = _nn(wm, vn_cg) + bst_ref[:, g:g + 1]
                dgu_s[rows, cols] = da[rows, cols] * mixed
                dmx = da[rows, cols] * gu[rows, cols]
                dmxb = dmx.astype(BF16)
                dws = dws + _nt(dmxb, vn_cg)
                dsum = dsum + dmx
                dvn_s[rows, cols] = _tn(wm, dmxb)
            dws_ref[g] += dws
            dmx_sum[:, cols] += dsum
        dvn = dvn_s[...]
        dlng_ref[...] += _colsum(dvn * vhat)
        dlnb_ref[...] += _colsum(dvn)
        dvh = dvn * lng_ref[...]
        dgv = rstd * (dvh - jnp.mean(dvh, axis=-1, keepdims=True) - vhat * jnp.mean(dvh * vhat, axis=-1, keepdims=True))
        duv_ref[:, :D] = (dgu_s[...] * _gelu_grad(u, tu)).astype(BF16)
        duv_ref[:, D:] = (dgv * _gelu_grad(vs, tv)).astype(BF16)

        @pl.when(i == nsteps - 1)
        def _():
            row = lax.broadcasted_iota(jnp.int32, (CHUNK, CHUNK), 0)
            col = lax.broadcasted_iota(jnp.int32, (CHUNK, CHUNK), 1)
            for g in range(GROUPS):
                dws_ref[g] = jnp.where(row >= col, dws_ref[g], 0.0)
                dbs_ref[g:g + 1, :] = _colsum(dmx_sum[:, g * CHUNK:(g + 1) * CHUNK].T)

    const2 = lambda i: (0, 0)
    return pl.pallas_call(
        body, name="bwd_sgu", grid=(nsteps,),
        in_specs=[pl.BlockSpec((tc, D), lambda i: (i, 0)), pl.BlockSpec((tc, D), lambda i: (i, 1)),
                  pl.BlockSpec((tc, D), lambda i: (i, 0)), pl.BlockSpec((1, D), const2), pl.BlockSpec((1, D), const2),
                  pl.BlockSpec((GROUPS, CHUNK, CHUNK), lambda i: (0, 0, 0)), pl.BlockSpec((CHUNK, GROUPS), const2)],
        out_specs=[pl.BlockSpec((tc, 2 * D), lambda i: (i, 0)), pl.BlockSpec((GROUPS, CHUNK, CHUNK), lambda i: (0, 0, 0)),
                   pl.BlockSpec((GROUPS, CHUNK), const2), pl.BlockSpec((1, D), const2), pl.BlockSpec((1, D), const2)],
        out_shape=[SDS((T, 2 * D), BF16), SDS((GROUPS, CHUNK, CHUNK), F32), SDS((GROUPS, CHUNK), F32),
                   SDS((1, D), F32), SDS((1, D), F32)],
        scratch_shapes=[pltpu.VMEM((tc, D), F32), pltpu.VMEM((tc, D), F32), pltpu.VMEM((CHUNK, D), F32)],
        compiler_params=_params(1),
    )(proj, proj, da, lng, lnb, ws, bst)


IN_SEG_WIDTHS = (2 * D, D, 2 * N_KV * HEAD, 2 * D)


def _resident(shape):
    return pl.BlockSpec(shape, lambda *_: (0,) * len(shape), pipeline_mode=pl.Buffered(1))


def _bwd_in(duv, dq, dkv, dg, win, x, dx1, g0, after=None):
    T = x.shape[0]
    tm = min(T, 512)

    def body(duv_ref, dq_ref, dkv_ref, dg_ref, w_ref, x_ref, dx1_ref, g0_ref, gx_ref, dg0_ref):
        @pl.when(pl.program_id(0) == 0)
        def _():
            dg0_ref[...] = jnp.zeros_like(dg0_ref)

        dh, off = None, 0
        for ref, width in zip((duv_ref, dq_ref, dkv_ref, dg_ref), IN_SEG_WIDTHS):
            part = _nt(ref[...], w_ref[:, off:off + width])
            dh = part if dh is None else dh + part
            off += width
        r0, xh = _rms_stats(x_ref[...])
        dg0_ref[...] += _colsum(dh * xh)
        gx_ref[...] = dx1_ref[...] + _rms_bwd(dh, xh, r0, g0_ref[...])

    row = lambda i: (i, 0)
    body, dep_specs, deps = _after(body, 8, after)
    return pl.pallas_call(
        body, name="bwd_in", grid=(T // tm,),
        in_specs=[pl.BlockSpec((tm, w), row) for w in IN_SEG_WIDTHS] + [
            _resident((D, IN_W)), pl.BlockSpec((tm, D), row), pl.BlockSpec((tm, D), row),
            pl.BlockSpec((1, D), lambda i: (0, 0))] + dep_specs,
        out_specs=[pl.BlockSpec((tm, D), row), pl.BlockSpec((1, D), lambda i: (0, 0))],
        out_shape=[SDS((T, D), F32), SDS((1, D), F32)],
        compiler_params=_params(1),
    )(duv, dq, dkv, dg, win, x, dx1, g0, *deps)


def _wgrad_cols(h, segs, name):
    T = h.shape[0]
    tt = min(T, 1024)
    widths = [s.shape[1] for s in segs]

    def body(h_ref, *refs):
        dw_ref, acc = refs[-2], refs[-1]
        t = pl.program_id(0)

        @pl.when(t == 0)
        def _():
            acc[...] = jnp.zeros_like(acc)

        off = 0
        for ref, width in zip(refs[:-2], widths):
            acc[:, off:off + width] += _tn(h_ref[...], ref[...])
            off += width

        @pl.when(t == T // tt - 1)
        def _():
            dw_ref[...] = acc[...].astype(BF16)

    row = lambda t: (t, 0)
    return pl.pallas_call(
        body, name=name, grid=(T // tt,),
        in_specs=[pl.BlockSpec((tt, D), row)] + [pl.BlockSpec((tt, w), row) for w in widths],
        out_specs=pl.BlockSpec((D, sum(widths)), lambda t: (0, 0)),
        out_shape=SDS((D, sum(widths)), BF16),
        scratch_shapes=[pltpu.VMEM((D, sum(widths)), F32)],
        compiler_params=_params(1),
    )(h, *segs)


def _wgrad_in(h, duv, dq, dkv, dg):
    return jnp.concatenate([_wgrad_cols(h, [duv], "wgrad_in_uv"), _wgrad_cols(h, [dq, dkv], "wgrad_in_qkv"),
                            _wgrad_cols(h, [dg], "wgrad_in_gates")], axis=1)


def _place():
    x, y, c = lax.axis_index("x"), lax.axis_index("y"), lax.axis_index("c")
    return x, y, c, 4 * x + 2 * y + c


def _peers(x, y, c):
    out = []
    for mask in range(1, N_DEV):
        px = 1 - x if mask & 4 else x
        py = 1 - y if mask & 2 else y
        pc = 1 - c if mask & 1 else c
        out.append(((px, py, pc), 4 * px + 2 * py + pc))
    return out


def _all_to_all(arrays, gather, name):
    n = len(arrays)

    def body(*refs):
        ins, outs = refs[:n], refs[n:2 * n]
        send_sems, recv_sems, local_sems = refs[2 * n:]
        x, y, c, me = _place()
        local, sends, recvs = [], [], []
        for a in range(n):
            src_own = ins[a] if gather[a] else ins[a].at[me]
            local.append(pltpu.make_async_copy(src_own, outs[a].at[me], local_sems.at[a]))
            for k, (peer, pid) in enumerate(_peers(x, y, c)):
                sem = a * (N_DEV - 1) + k
                src = ins[a] if gather[a] else ins[a].at[pid]
                sends.append(pltpu.make_async_remote_copy(
                    src_ref=src, dst_ref=outs[a].at[me], send_sem=send_sems.at[sem], recv_sem=recv_sems.at[sem],
                    device_id=peer, device_id_type=MESH))
                recvs.append(pltpu.make_async_remote_copy(
                    src_ref=src, dst_ref=outs[a].at[pid], send_sem=send_sems.at[sem], recv_sem=recv_sems.at[sem],
                    device_id=peer, device_id_type=MESH))
        for cp in local + sends:
            cp.start()
        for cp in recvs:
            cp.wait_recv()
        for cp in sends:
            cp.wait_send()
        for cp in local:
            cp.wait()

    out_shape = [SDS((N_DEV,) + a.shape if gt else a.shape, a.dtype) for a, gt in zip(arrays, gather)]
    nsem = n * (N_DEV - 1)
    return pl.pallas_call(
        body, name=name,
        in_specs=[pl.BlockSpec(memory_space=pl.ANY)] * n,
        out_specs=[pl.BlockSpec(memory_space=pl.ANY)] * n,
        out_shape=out_shape,
        scratch_shapes=[pltpu.SemaphoreType.DMA((nsem,)), pltpu.SemaphoreType.DMA((nsem,)), pltpu.SemaphoreType.DMA((n,))],
    )(*arrays)


def _gather_two_level(shard, name):
    def body(x_ref, out_ref, send_sems, recv_sems, local_sem):
        x, y, c = lax.axis_index("x"), lax.axis_index("y"), lax.axis_index("c")
        me, sibling = (x, y, c), (x, y, 1 - c)
        chips = [(1 - x, y), (x, 1 - y), (1 - x, 1 - y)]

        def slot(px, py, pc):
            return out_ref.at[4 * px + 2 * py + pc]

        def copy(k, block, to, src=None):
            return pltpu.make_async_remote_copy(
                src_ref=slot(*block) if src is None else src, dst_ref=slot(*block),
                send_sem=send_sems.at[k], recv_sem=recv_sems.at[k], device_id=to, device_id_type=MESH)

        mine = pltpu.make_async_copy(x_ref, slot(*me), local_sem)
        mine.start()
        first = [copy(0, me, sibling, src=x_ref)]
        first += [copy(1 + j, me, (*chip, c), src=x_ref) for j, chip in enumerate(chips)]
        for cp in first:
            cp.start()
        passed = [copy(4 + j, (*chip, c), sibling) for j, chip in enumerate(chips)]
        for j, chip in enumerate(chips):
            copy(1 + j, (*chip, c), me).wait_recv()
            passed[j].start()
        copy(0, sibling, me).wait_recv()
        for j, chip in enumerate(chips):
            copy(4 + j, (*chip, 1 - c), me).wait_recv()
        for cp in first + passed:
            cp.wait_send()
        mine.wait()

    return pl.pallas_call(
        body, name=name,
        in_specs=[_ANY], out_specs=_ANY,
        out_shape=SDS((N_DEV,) + shard.shape, shard.dtype),
        scratch_shapes=[pltpu.SemaphoreType.DMA((N_DEV - 1,)), pltpu.SemaphoreType.DMA((N_DEV - 1,)),
                        pltpu.SemaphoreType.DMA],
    )(shard)


_HBM = pl.BlockSpec(memory_space=pltpu.HBM)
_SEM = pl.BlockSpec(memory_space=pltpu.SEMAPHORE)
_EFFECT = pltpu.SideEffectType.DATAFLOW_SIDE_EFFECTING
GATHER = "gather"
SCATTER = "scatter"
SPREAD = "spread"


def _zone_shape(a, mode):
    if mode == GATHER:
        return (N_DEV,) + a.shape
    return (N_DEV - 1,) + (a.shape[1:] if mode == SCATTER else a.shape)


def _start_copies(arrays, modes, name, after=None):
    n = len(arrays)
    zones = [lax.empty(_zone_shape(a, m), a.dtype) for a, m in zip(arrays, modes)]

    def body(*refs):
        ins, lands = refs[:n], refs[n:2 * n]
        send_sems, recv_sems = refs[-2 * n - 3], refs[-2 * n - 2]
        token = refs[-1]
        x, y, c, me = _place()
        for a in range(n):
            for k, (peer, pid) in enumerate(_peers(x, y, c)):
                src = ins[a].at[pid] if modes[a] == SCATTER else ins[a]
                dst = lands[a].at[me] if modes[a] == GATHER else lands[a].at[k]
                pltpu.make_async_remote_copy(src_ref=src, dst_ref=dst, send_sem=send_sems.at[a], recv_sem=recv_sems.at[a],
                                             device_id=peer, device_id_type=MESH).start()
        token[...] = jnp.zeros_like(token)

    hbm = lambda a: pltpu.HBM(a.shape, a.dtype)
    sems = pltpu.SemaphoreType.DMA((n,))
    extra = [] if after is None else [after]
    operands = [pltpu.with_memory_space_constraint(a, pltpu.HBM) for a in list(arrays) + zones]
    res = pl.pallas_call(
        body, name=name,
        out_shape=(sems, sems, *[hbm(a) for a in arrays], *[hbm(z) for z in zones], SDS((8, 128), F32)),
        in_specs=[_HBM] * (2 * n) + [_ANY] * len(extra),
        out_specs=(_SEM, _SEM, *[_HBM] * (2 * n), pl.BlockSpec(memory_space=pltpu.VMEM)),
        input_output_aliases={i: 2 + i for i in range(2 * n)},
        compiler_params=pltpu.CompilerParams(has_side_effects=_EFFECT),
    )(*operands, *extra)
    return res[0], res[1], list(res[2:2 + n]), list(res[2 + n:2 + 2 * n]), res[-1]


def _wait_copies(started, after, name):
    send_sems, recv_sems, thru, zones, _ = started
    n = len(thru)

    def body(*refs):
        lands = refs[n:2 * n]
        send_ref, recv_ref = refs[2 * n], refs[2 * n + 1]
        x, y, c, _ = _place()
        for a in range(n):
            seven = lands[a].at[pl.ds(0, N_DEV - 1)]
            cp = pltpu.make_async_remote_copy(src_ref=seven, dst_ref=seven, send_sem=send_ref.at[a], recv_sem=recv_ref.at[a],
                                              device_id=(x, y, 1 - c), device_id_type=MESH)
            cp.wait_send()
            cp.wait_recv()

    hbm = lambda a: pltpu.HBM(a.shape, a.dtype)
    res = pl.pallas_call(
        body, name=name,
        out_shape=tuple(hbm(a) for a in thru + zones),
        in_specs=[_HBM] * (2 * n) + [_SEM, _SEM, _ANY],
        out_specs=tuple([_HBM] * (2 * n)),
        input_output_aliases={i: i for i in range(2 * n)},
        compiler_params=pltpu.CompilerParams(has_side_effects=_EFFECT),
    )(*thru, *zones, send_sems, recv_sems, after)
    return list(res[:n]), list(res[n:])


def _adamw_math(g, w, m, v):
    m2 = ADAM_B1 * m + (1.0 - ADAM_B1) * g
    v2 = ADAM_B2 * v + (1.0 - ADAM_B2) * (g * g)
    m_hat = m2 / (1.0 - ADAM_B1 ** ADAM_STEP)
    v_hat = v2 / (1.0 - ADAM_B2 ** ADAM_STEP)
    delta = -ADAM_LR * (m_hat / (jnp.sqrt(v_hat) + ADAM_EPS) + ADAM_WD * w)
    return delta, m2, v2


def _sum_adamw(parts, w, m, v, name):
    R, C = w.shape
    tr = max(t for t in (128, 64, 32, 16, 8) if R % t == 0)

    def body(p_ref, w_ref, m_ref, v_ref, g_ref, d_ref, m2_ref, v2_ref):
        g = p_ref[0]
        for k in range(1, N_DEV):
            g = g + p_ref[k]
        g_ref[...] = g
        d_ref[...], m2_ref[...], v2_ref[...] = _adamw_math(g, w_ref[...], m_ref[...], v_ref[...])

    blk = pl.BlockSpec((tr, C), lambda i: (i, 0))
    return pl.pallas_call(
        body, name=name, grid=(R // tr,),
        in_specs=[pl.BlockSpec((N_DEV, tr, C), lambda i: (0, i, 0)), blk, blk, blk],
        out_specs=[blk] * 4,
        out_shape=[SDS((R, C), F32)] * 4,
        compiler_params=_params(1),
    )(parts, w, m, v)


def _sum_adamw_peers(me, own, parts, w, m, v, name, replicated):
    R, C = w.shape
    tr = max(t for t in (128, 64, 32, 16, 8) if R % t == 0)

    def body(me_ref, own_ref, p_ref, w_ref, m_ref, v_ref, g_ref, d_ref, m2_ref, v2_ref):
        if replicated:
            mine = me_ref[0]
            g = None
            for j in range(N_DEV):
                k = jnp.maximum(jnp.bitwise_xor(mine, j) - 1, 0)
                term = jnp.where(mine == j, own_ref[...], p_ref[k])
                g = term if g is None else g + term
        else:
            g = own_ref[...].astype(F32)
            for k in range(N_DEV - 1):
                g = g + p_ref[k].astype(F32)
        g_ref[...] = g
        d_ref[...], m2_ref[...], v2_ref[...] = _adamw_math(g, w_ref[...], m_ref[...], v_ref[...])

    blk = pl.BlockSpec((tr, C), lambda i, me_ref: (i, 0))
    own_spec = blk if replicated else pl.BlockSpec((None, tr, C), lambda i, me_ref: (me_ref[0], i, 0))
    return pl.pallas_call(
        body, name=name,
        grid_spec=pltpu.PrefetchScalarGridSpec(
            num_scalar_prefetch=1, grid=(R // tr,),
            in_specs=[own_spec, pl.BlockSpec((N_DEV - 1, tr, C), lambda i, me_ref: (0, i, 0)), blk, blk, blk],
            out_specs=[blk] * 4),
        out_shape=[SDS((R, C), F32)] * 4,
        compiler_params=_params(1),
    )(me, own, parts, w, m, v)


SMALL = ("ln_v_gain", "ln_v_bias", "w_spatial", "b_spatial", "sinks", "norm_mix_post", "norm_ff_pre", "norm_ff_post")
SMALL_ROWS = {"ln_v_gain": 8, "ln_v_bias": 8, "w_spatial": 1024, "b_spatial": 8, "sinks": 8,
              "norm_mix_post": 8, "norm_ff_pre": 8, "norm_ff_post": 8}
SMALL_PACK_ROWS = 1152


def _pack_small(vals):
    rows = []
    for name in SMALL:
        flat = vals[name].reshape(-1)
        pad = SMALL_ROWS[name] * 128 - flat.shape[0]
        if pad:
            flat = jnp.concatenate([flat, jnp.zeros((pad,), F32)])
        rows.append(flat.reshape(SMALL_ROWS[name], 128))
    rows.append(jnp.zeros((SMALL_PACK_ROWS - sum(SMALL_ROWS.values()), 128), F32))
    return jnp.concatenate(rows, axis=0)


def _unpack_small(packed, shapes):
    out, r = {}, 0
    for name in SMALL:
        n = 1
        for s in shapes[name]:
            n *= s
        out[name] = packed[r:r + SMALL_ROWS[name]].reshape(-1)[:n].reshape(shapes[name])
        r += SMALL_ROWS[name]
    return out


def _rope_rows():
    d = jnp.arange(128) % HEAD
    inv = ROPE_THETA ** (-(2.0 * (d % (ROPE // 2))).astype(F32) / ROPE)
    invf = jnp.where(d < ROPE, inv, 0.0).astype(F32).reshape(1, 128)
    sgn = jnp.where(d < ROPE // 2, -1.0, jnp.where(d < ROPE, 1.0, 0.0)).astype(F32).reshape(1, 128)
    return invf, sgn


def kernel(x, positions, w_in, ln_v_gain, ln_v_bias, w_spatial, b_spatial, sinks, w_a, w_b, w_o, norm_mix_pre, norm_mix_post, w_ff_in, w_ff_out, norm_ff_pre, norm_ff_post, loss_target, m_w_in, m_ln_v_gain, m_ln_v_bias, m_w_spatial, m_b_spatial, m_sinks, m_w_a, m_w_b, m_w_o, m_norm_mix_pre, m_norm_mix_post, m_w_ff_in, m_w_ff_out, m_norm_ff_pre, m_norm_ff_post, v_w_in, v_ln_v_gain, v_ln_v_bias, v_w_spatial, v_b_spatial, v_sinks, v_w_a, v_w_b, v_w_o, v_norm_mix_pre, v_norm_mix_post, v_w_ff_in, v_w_ff_out, v_norm_ff_pre, v_norm_ff_post):
    given = dict(locals())
    T = x.shape[1]
    xt = x[0]
    tgt = loss_target[0]
    bst = b_spatial[0].T
    ws = w_spatial[0]

    me = 4 * lax.axis_index("x") + 2 * lax.axis_index("y") + lax.axis_index("c")
    me_arr = me.astype(jnp.int32).reshape(1)

    def with_own(zone, shard):
        return lax.dynamic_update_slice(zone, shard[None], (me,) + (0,) * shard.ndim)

    rest = ("w_a", "w_b", "w_o", "w_ff_in", "w_ff_out")
    shard = {n: given[n][0].astype(BF16) for n in ("w_in",) + rest}
    win8 = _gather_two_level(shard["w_in"], "gather_in")
    g_rest = _start_copies([shard[n] for n in rest], [GATHER] * len(rest), "gather_rest_start", after=win8)
    win = jnp.transpose(win8, (1, 0, 2)).reshape(D, IN_W)
    cos, sin = _rope_tables(positions.astype(F32).reshape(T, 1), *_rope_rows())

    proj, h = _fwd_in(xt, norm_mix_pre, win)
    a = _fwd_sgu(proj, ln_v_gain, ln_v_bias, ws, bst)
    att = _fwd_attn(proj, cos, sin, sinks[0])
    gw = {n: with_own(z, own) for n, own, z in zip(rest, *_wait_copies(g_rest, att, "gather_rest_wait"))}
    wa, wb, wo = (gw[n].reshape(D, D) for n in ("w_a", "w_b", "w_o"))
    wfi3 = gw["w_ff_in"]
    wfo = gw["w_ff_out"].reshape(D_FF, D)
    merged, a2, b2, mix, x1, hf = _fwd_mix(a, att, proj, xt, wa, wb, wo, norm_mix_post, norm_ff_pre)
    f, dy, dff, dg3, loss_part = _fwd_ff(hf, wfi3, wfo, x1, tgt, norm_ff_post)

    df, dx1, dmix, dg2, dg1 = _bwd_ff(dff, f, wfi3, wfo, x1, dy, mix, norm_mix_post, norm_ff_pre)
    dwfi3, dwfo = _wgrad_ff(hf, df, f, dff)
    own_ff = [dwfi3, dwfo.reshape(N_DEV, D_FF // N_DEV, D)]
    x_ff = _start_copies(own_ff, [SCATTER] * 2, "exchange_ff_start")
    da2, db2, dgate, da, datt = _bwd_mix(dmix, proj, a2, b2, wo, wa, wb, after=x_ff[-1])
    dwo, dwa, dwb = _wgrad_mix(merged, dmix, a, da2, att, db2)
    own_mix = [g.reshape(N_DEV, D // N_DEV, D) for g in (dwa, dwb, dwo)]
    x_mix = _start_copies(own_mix, [SCATTER] * 3, "exchange_mix_start")
    dq, dkv, dsink = _bwd_attn(proj, cos, sin, sinks[0], datt, after=x_mix[-1])
    duv, dws, dbs, dlng, dlnb = _bwd_sgu(proj, da, ln_v_gain, ln_v_bias, ws, bst)
    small_grads = {"ln_v_gain": dlng, "ln_v_bias": dlnb, "w_spatial": dws, "b_spatial": dbs, "sinks": dsink[:, :N_Q],
                   "norm_mix_post": dg1, "norm_ff_pre": dg2, "norm_ff_post": dg3}
    x_small = _start_copies([_pack_small(small_grads)], [SPREAD], "exchange_small_start")
    dwin = _wgrad_in(h, duv, dq, dkv, dgate)
    own_in = [jnp.transpose(dwin.reshape(D, N_DEV, IN_W // N_DEV), (1, 0, 2))]
    x_in = _start_copies(own_in, [SCATTER], "exchange_in_start", after=x_small[-1])
    grad_x, dg0 = _bwd_in(duv, dq, dkv, dgate, win, xt, dx1, norm_mix_pre, after=x_in[-1])

    results = {}

    def update(n, own, parts):
        results[n] = [r.reshape(given[n].shape) for r in _sum_adamw_peers(
            me_arr, own, parts, given[n][0], given["m_" + n][0], given["v_" + n][0], "adamw_" + n, False)]

    own_ff, p_ff = _wait_copies(x_ff, grad_x, "exchange_ff_wait")
    update("w_ff_in", own_ff[0], p_ff[0])
    update("w_ff_out", own_ff[1], p_ff[1])
    own_mix, p_mix = _wait_copies(x_mix, p_ff[0], "exchange_mix_wait")
    for n, own, parts in zip(("w_a", "w_b", "w_o"), own_mix, p_mix):
        update(n, own, parts)
    tail = jnp.concatenate([dg0.reshape(8, 128), jnp.tile(loss_part, (8, 1))], axis=0)
    (tail_all,) = _all_to_all([tail], [True], "exchange_tail")
    dg0_all = tail_all[:, :8]
    own_small, p_small = _wait_copies(x_small, tail_all, "exchange_small_wait")
    own_in, p_in = _wait_copies(x_in, p_small[0], "exchange_in_wait")
    update("w_in", own_in[0], p_in[0])
    packed = _sum_adamw_peers(me_arr, own_small[0], p_small[0], _pack_small({n: given[n] for n in SMALL}),
                              _pack_small({n: given["m_" + n] for n in SMALL}),
                              _pack_small({n: given["v_" + n] for n in SMALL}), "adamw_small", True)
    shapes = {n: given[n].shape for n in SMALL}
    unpacked = [_unpack_small(p, shapes) for p in packed]
    for n in SMALL:
        results[n] = [u[n] for u in unpacked]
    n = "norm_mix_pre"
    results[n] = [r.reshape(given[n].shape) for r in _sum_adamw(
        dg0_all, given[n].reshape(8, 128), given["m_" + n].reshape(8, 128), given["v_" + n].reshape(8, 128), "adamw_" + n)]

    loss = jnp.sum(tail_all[:, 8, 0])
    order = ("w_in", "ln_v_gain", "ln_v_bias", "w_spatial", "b_spatial", "sinks", "w_a", "w_b", "w_o", "norm_mix_pre",
             "norm_mix_post", "w_ff_in", "w_ff_out", "norm_ff_pre", "norm_ff_post")
    out = [loss, grad_x.reshape(x.shape)]
    for k in range(4):
        out += [results[n][k] for n in order]
    return tuple(out)
```

```python
import functools

import jax
import jax.numpy as jnp
from jax import lax
from jax.experimental import pallas as pl
from jax.experimental.pallas import tpu as pltpu

F32 = jnp.float32
BF16 = jnp.bfloat16

N_DEV = 8
D = 1024
D_FF = 4096
IN_W = 5632
CHUNK = 128
GROUPS = 8
HEAD = 64
N_Q = 16
N_KV = 4
ROPE = 16
ROPE_THETA = 500000.0
EPS = 1e-6
OFF_Q, OFF_K, OFF_VA, OFF_GA, OFF_GB = 2048, 3072, 3328, 3584, 4608

ADAM_LR = 0.001
ADAM_B1 = 0.9
ADAM_B2 = 0.999
ADAM_EPS = 1e-08
ADAM_WD = 0.01
ADAM_STEP = 10

VMEM_LIMIT = 56 * 1024 * 1024

SDS = jax.ShapeDtypeStruct
MESH = pl.DeviceIdType.MESH


def _params(n_axes=None):
    if n_axes is None:
        return pltpu.CompilerParams(vmem_limit_bytes=VMEM_LIMIT)
    return pltpu.CompilerParams(dimension_semantics=("arbitrary",) * n_axes, vmem_limit_bytes=VMEM_LIMIT)


def _nt(a, b):
    return lax.dot_general(a, b, (((1,), (1,)), ((), ())), preferred_element_type=F32)


def _tn(a, b):
    return lax.dot_general(a, b, (((0,), (0,)), ((), ())), preferred_element_type=F32)


def _nn(a, b):
    return jnp.dot(a, b, preferred_element_type=F32)


def _gelu(x):
    t = jnp.tanh(0.7978845608028654 * (x + 0.044715 * (x * x * x)))
    return 0.5 * x * (1.0 + t), t


def _gelu_grad(x, t):
    return 0.5 * (1.0 + t) + 0.5 * x * (1.0 - t * t) * (0.7978845608028654 * (1.0 + 3.0 * 0.044715 * x * x))


def _sigmoid(x):
    return 1.0 / (1.0 + jnp.exp(-x))


def _rms_stats(v):
    r = lax.rsqrt(jnp.mean(v * v, axis=-1, keepdims=True) + EPS)
    return r, v * r


def _rms_bwd(d, vhat, r, g):
    gd = g * d
    return r * (gd - vhat * jnp.mean(gd * vhat, axis=-1, keepdims=True))


def _colsum(v):
    return jnp.sum(v, axis=0, keepdims=True)


_ANY = pl.BlockSpec(memory_space=pl.ANY)


def _after(body, n_in, after):
    if after is None:
        return body, [], []

    def ordered(*refs):
        return body(*refs[:n_in], *refs[n_in + 1:])

    return ordered, [_ANY], [after]


def _fwd_in(x, g0, win):
    T = x.shape[0]
    tm, tn = min(T, 1024), 1408

    def body(x_ref, g_ref, w_ref, p_ref, h_ref):
        @pl.when(pl.program_id(1) == 0)
        def _():
            _, xh = _rms_stats(x_ref[...])
            h_ref[...] = (xh * g_ref[...]).astype(BF16)

        p_ref[...] = _nn(h_ref[...], w_ref[...]).astype(BF16)

    return pl.pallas_call(
        body, name="fwd_in", grid=(T // tm, IN_W // tn),
        in_specs=[pl.BlockSpec((tm, D), lambda i, j: (i, 0)), pl.BlockSpec((1, D), lambda i, j: (0, 0)),
                  pl.BlockSpec((D, tn), lambda i, j: (0, j))],
        out_specs=[pl.BlockSpec((tm, tn), lambda i, j: (i, j)), pl.BlockSpec((tm, D), lambda i, j: (i, 0))],
        out_shape=[SDS((T, IN_W), BF16), SDS((T, D), BF16)],
        compiler_params=_params(2),
    )(x, g0, win)


def _sgu_forward_parts(u_ref, vs_ref, lng_ref, lnb_ref):
    u = u_ref[...].astype(F32)
    vs = vs_ref[...].astype(F32)
    gu, tu = _gelu(u)
    gv, tv = _gelu(vs)
    mu = jnp.mean(gv, axis=-1, keepdims=True)
    dv = gv - mu
    rstd = lax.rsqrt(jnp.mean(dv * dv, axis=-1, keepdims=True) + EPS)
    vhat = dv * rstd
    vn = (vhat * lng_ref[...] + lnb_ref[...]).astype(BF16)
    return u, vs, gu, tu, tv, rstd, vhat, vn


def _masked_ws(ws_ref, g):
    row = lax.broadcasted_iota(jnp.int32, (CHUNK, CHUNK), 0)
    col = lax.broadcasted_iota(jnp.int32, (CHUNK, CHUNK), 1)
    return jnp.where(row >= col, ws_ref[g], 0.0).astype(BF16)


def _fwd_sgu(proj, lng, lnb, ws, bst):
    T = proj.shape[0]
    tc = min(T, 512)

    def body(u_ref, vs_ref, lng_ref, lnb_ref, ws_ref, bst_ref, a_ref):
        _, _, gu, _, _, _, _, vn = _sgu_forward_parts(u_ref, vs_ref, lng_ref, lnb_ref)
        for g in range(GROUPS):
            wm = _masked_ws(ws_ref, g)
            cols = slice(g * CHUNK, (g + 1) * CHUNK)
            for c in range(tc // CHUNK):
                rows = slice(c * CHUNK, (c + 1) * CHUNK)
                mixed = _nn(wm, vn[rows, cols]) + bst_ref[:, g:g + 1]
                a_ref[rows, cols] = (gu[rows, cols] * mixed).astype(BF16)

    return pl.pallas_call(
        body, name="fwd_sgu", grid=(T // tc,),
        in_specs=[pl.BlockSpec((tc, D), lambda i: (i, 0)), pl.BlockSpec((tc, D), lambda i: (i, 1)),
                  pl.BlockSpec((1, D), lambda i: (0, 0)), pl.BlockSpec((1, D), lambda i: (0, 0)),
                  pl.BlockSpec((GROUPS, CHUNK, CHUNK), lambda i: (0, 0, 0)), pl.BlockSpec((CHUNK, GROUPS), lambda i: (0, 0))],
        out_specs=pl.BlockSpec((tc, D), lambda i: (i, 0)),
        out_shape=SDS((T, D), BF16),
        compiler_params=_params(1),
    )(proj, proj, lng, lnb, ws, bst)


def _rope_tables(posf, invf, sgn):
    T = posf.shape[0]
    tr = min(T, 1024)

    def body(pos_ref, invf_ref, sgn_ref, c_ref, s_ref):
        ang = pos_ref[...] * invf_ref[...]
        c_ref[...] = jnp.cos(ang)
        s = jnp.sin(ang)
        s_ref[:, :128] = jnp.where(sgn_ref[...] < 0.0, -s, 0.0)
        s_ref[:, 128:] = jnp.where(sgn_ref[...] > 0.0, s, 0.0)

    return pl.pallas_call(
        body, name="rope_tables", grid=(T // tr,),
        in_specs=[pl.BlockSpec((tr, 1), lambda i: (i, 0)), pl.BlockSpec((1, 128), lambda i: (0, 0)),
                  pl.BlockSpec((1, 128), lambda i: (0, 0))],
        out_specs=[pl.BlockSpec((tr, 128), lambda i: (i, 0)), pl.BlockSpec((tr, 256), lambda i: (i, 0))],
        out_shape=[SDS((T, 128), F32), SDS((T, 256), F32)],
        compiler_params=_params(1),
    )(posf, invf, sgn)


def _rope(v, c, s_lo, s_hi):
    n = v.shape[1]
    return v * c + pltpu.roll(v, n - ROPE // 2, 1) * s_lo + pltpu.roll(v, ROPE // 2, 1) * s_hi


def _rope_bwd(dv, c, s_lo, s_hi):
    n = dv.shape[1]
    return dv * c + pltpu.roll(dv * s_lo, ROPE // 2, 1) + pltpu.roll(dv * s_hi, n - ROPE // 2, 1)


def _fold_masks(first):
    jj = lax.broadcasted_iota(jnp.int32, (CHUNK, CHUNK), 0)
    t = lax.broadcasted_iota(jnp.int32, (CHUNK, CHUNK), 1)
    prev = jj > t
    return prev, jnp.where(prev & first, -1e30, 0.0)


def _fold(band, prev):
    return jnp.where(prev, band[:CHUNK], band[CHUNK:])


def _unfold(folded, prev):
    return jnp.concatenate([jnp.where(prev, folded, 0.0), jnp.where(prev, 0.0, folded)], axis=0)


def _softmax_sink(s, sink, key_axis):
    m = jnp.maximum(jnp.max(s, axis=key_axis, keepdims=True), sink)
    p = jnp.exp(s - m)
    esink = jnp.exp(sink - m)
    inv = 1.0 / (jnp.sum(p, axis=key_axis, keepdims=True) + esink)
    return p * inv, esink * inv


def _head_pair_operand(band, g):
    slab = band[:, (g // 2) * 128:(g // 2 + 1) * 128]
    lo = lax.broadcasted_iota(jnp.int32, slab.shape, 1) < HEAD
    if g % 2 == 0:
        first = jnp.where(lo, slab, 0.0)
        second = pltpu.roll(first, HEAD, 1)
    else:
        second = jnp.where(lo, 0.0, slab)
        first = pltpu.roll(second, HEAD, 1)
    return jnp.concatenate([first, second], axis=0).astype(BF16)


def _head_pair_gradient(acc, g):
    top, bot = acc[:2 * CHUNK], acc[2 * CHUNK:]
    lo = lax.broadcasted_iota(jnp.int32, top.shape, 1) < HEAD
    if g % 2 == 0:
        return jnp.where(lo, top, 0.0) + pltpu.roll(jnp.where(lo, 0.0, bot), HEAD, 1)
    return pltpu.roll(jnp.where(lo, top, 0.0), HEAD, 1) + jnp.where(lo, 0.0, bot)


def _attn_specs(nb, clamp):
    cur = (lambda i: jnp.minimum(i, nb - 1)) if clamp else (lambda i: i)
    prev = lambda i: jnp.maximum(jnp.minimum(i, nb - 1) - 1, 0)
    kw = N_KV * HEAD
    return cur, prev, [
        pl.BlockSpec((CHUNK, D), lambda i: (cur(i), OFF_Q // D)),
        pl.BlockSpec((CHUNK, kw), lambda i: (prev(i), OFF_K // kw)),
        pl.BlockSpec((CHUNK, kw), lambda i: (cur(i), OFF_K // kw)),
        pl.BlockSpec((CHUNK, kw), lambda i: (prev(i), OFF_VA // kw)),
        pl.BlockSpec((CHUNK, kw), lambda i: (cur(i), OFF_VA // kw)),
        pl.BlockSpec((CHUNK, 128), lambda i: (prev(i), 0)),
        pl.BlockSpec((CHUNK, 128), lambda i: (cur(i), 0)),
        pl.BlockSpec((CHUNK, 256), lambda i: (prev(i), 0)),
        pl.BlockSpec((CHUNK, 256), lambda i: (cur(i), 0)),
        pl.BlockSpec(memory_space=pltpu.SMEM),
    ]


def _attn_load(q_ref, kp_ref, kc_ref, vp_ref, vc_ref, cp_ref, cc_ref, sp_ref, sc_ref):
    q_tab = [jnp.tile(t, (1, D // 128)) for t in (cc_ref[...], sc_ref[:, :128], sc_ref[:, 128:])]
    band = lambda p_ref, c_ref, cols: jnp.concatenate([p_ref[:, cols], c_ref[:, cols]], axis=0)
    k_tab = [jnp.tile(t, (1, N_KV * HEAD // 128)) for t in (
        band(cp_ref, cc_ref, slice(0, 128)), band(sp_ref, sc_ref, slice(0, 128)), band(sp_ref, sc_ref, slice(128, 256)))]
    q = _rope(q_ref[...].astype(F32), *q_tab).astype(BF16)
    kb = _rope(jnp.concatenate([kp_ref[...], kc_ref[...]], axis=0).astype(F32), *k_tab)
    vb = jnp.concatenate([vp_ref[...], vc_ref[...]], axis=0).astype(F32)
    return q, kb, vb, (q_tab, k_tab)


PAIRS_PER_KV = N_Q // N_KV // 2


def _fwd_attn(proj, cos, sin, sinks):
    T = proj.shape[0]
    nb = T // CHUNK
    _, _, specs = _attn_specs(nb, False)

    def body(q_ref, kp_ref, kc_ref, vp_ref, vc_ref, cp_ref, cc_ref, sp_ref, sc_ref, sink_ref, o_ref):
        q, kb, vb, _ = _attn_load(q_ref, kp_ref, kc_ref, vp_ref, vc_ref, cp_ref, cc_ref, sp_ref, sc_ref)
        prev, bias = _fold_masks(pl.program_id(0) == 0)
        for g in range(N_KV):
            k2 = _head_pair_operand(kb, g)
            v2 = _head_pair_operand(vb, g)
            for r in range(PAIRS_PER_KV):
                pair = g * PAIRS_PER_KV + r
                s2 = _nt(k2, q[:, pair * 128:(pair + 1) * 128]) * (HEAD ** -0.5)
                ps = []
                for e in range(2):
                    s = _fold(s2[e * 2 * CHUNK:(e + 1) * 2 * CHUNK], prev) + bias
                    ps.append(_unfold(_softmax_sink(s, sink_ref[2 * pair + e], 0)[0], prev).astype(BF16))
                o_ref[:, pair * 128:(pair + 1) * 128] = _tn(jnp.concatenate(ps, axis=0), v2).astype(BF16)

    return pl.pallas_call(
        body, name="fwd_attn", grid=(nb,), in_specs=specs,
        out_specs=pl.BlockSpec((CHUNK, D), lambda i: (i, 0)),
        out_shape=SDS((T, D), BF16),
        compiler_params=_params(1),
    )(proj, proj, proj, proj, proj, cos, cos, sin, sin, sinks)


def _fwd_mix(a, att, proj, x, wa, wb, wo, g1, g2):
    T = x.shape[0]
    tm = min(T, 512)
    half = D // 2

    def body(a_ref, att_ref, ga0, ga1, gb0, gb1, x_ref, wa_ref, wb_ref, wo_ref, g1_ref, g2_ref,
             mg_ref, a2_ref, b2_ref, mix_ref, x1_ref, hf_ref):
        a2 = _nn(a_ref[...], wa_ref[...])
        b2 = _nn(att_ref[...], wb_ref[...])
        ga = jnp.concatenate([ga0[...], ga1[...]], axis=1).astype(F32)
        gb = jnp.concatenate([gb0[...], gb1[...]], axis=1).astype(F32)
        merged = (_sigmoid(ga) * a2 + _sigmoid(gb) * b2).astype(BF16)
        a2_ref[...] = a2.astype(BF16)
        b2_ref[...] = b2.astype(BF16)
        mg_ref[...] = merged
        mix = _nn(merged, wo_ref[...])
        mix_ref[...] = mix
        _, mh = _rms_stats(mix)
        x1 = x_ref[...] + mh * g1_ref[...]
        x1_ref[...] = x1
        _, xh = _rms_stats(x1)
        hf_ref[...] = (xh * g2_ref[...]).astype(BF16)

    row = lambda i: (i, 0)
    const = lambda i: (0, 0)
    gspec = lambda off: pl.BlockSpec((tm, half), lambda i: (i, off // half))
    return pl.pallas_call(
        body, name="fwd_mix", grid=(T // tm,),
        in_specs=[pl.BlockSpec((tm, D), row), pl.BlockSpec((tm, D), row),
                  gspec(OFF_GA), gspec(OFF_GA + half), gspec(OFF_GB), gspec(OFF_GB + half),
                  pl.BlockSpec((tm, D), row), _resident((D, D)), _resident((D, D)),
                  _resident((D, D)), pl.BlockSpec((1, D), const), pl.BlockSpec((1, D), const)],
        out_specs=[pl.BlockSpec((tm, D), row)] * 6,
        out_shape=[SDS((T, D), BF16), SDS((T, D), BF16), SDS((T, D), BF16), SDS((T, D), F32), SDS((T, D), F32),
                   SDS((T, D), BF16)],
        compiler_params=_params(1),
    )(a, att, proj, proj, proj, proj, x, wa, wb, wo, g1, g2)


FF_SPLIT = N_DEV
FF_TILE = D_FF // FF_SPLIT
FF_STEP = 2048
FF_SLABS = FF_STEP // FF_TILE
FF_STEPS = D_FF // FF_STEP


def _fwd_ff(hf, wfi3, wfo, x1, tgt, g3):
    T = hf.shape[0]
    tm = min(T, 512)
    last = FF_STEPS - 1

    def body(hf_ref, wfi_ref, wfo_ref, x1_ref, tgt_ref, g3_ref, f_ref, dy_ref, dff_ref, dg3_ref, loss_ref, acc, r_s):
        i, p = pl.program_id(0), pl.program_id(1)

        @pl.when((i == 0) & (p == 0))
        def _():
            dg3_ref[...] = jnp.zeros_like(dg3_ref)
            loss_ref[...] = jnp.zeros_like(loss_ref)

        hf_t = hf_ref[...]
        for s in range(FF_SLABS):
            cols = slice(s * FF_TILE, (s + 1) * FF_TILE)
            f = _nn(hf_t, wfi_ref[s]).astype(BF16)
            f_ref[:, cols] = f
            rl = jnp.maximum(f.astype(F32), 0.0)
            r_s[:, cols] = (rl * rl).astype(BF16)
        part = _nn(r_s[...], wfo_ref[...])

        @pl.when(p == 0)
        def _():
            acc[...] = part

        @pl.when(p > 0)
        def _():
            acc[...] += part

        @pl.when(p == last)
        def _():
            r3, fh = _rms_stats(acc[...])
            e = x1_ref[...] + fh * g3_ref[...] - tgt_ref[...]
            loss_ref[...] += jnp.sum(e * e) * (0.5 / D)
            dy = e * (1.0 / D)
            dy_ref[...] = dy
            dg3_ref[...] += _colsum(dy * fh)
            dff_ref[...] = _rms_bwd(dy, fh, r3, g3_ref[...]).astype(BF16)

    row = lambda i, p: (i, 0)
    const = lambda i, p: (0, 0)
    return pl.pallas_call(
        body, name="fwd_ff", grid=(T // tm, FF_STEPS),
        in_specs=[pl.BlockSpec((tm, D), row), pl.BlockSpec((FF_SLABS, D, FF_TILE), lambda i, p: (p, 0, 0)),
                  pl.BlockSpec((FF_STEP, D), lambda i, p: (p, 0)), pl.BlockSpec((tm, D), row),
                  pl.BlockSpec((tm, D), row), pl.BlockSpec((1, D), const)],
        out_specs=[pl.BlockSpec((tm, FF_STEP), lambda i, p: (i, p)), pl.BlockSpec((tm, D), row),
                   pl.BlockSpec((tm, D), row), pl.BlockSpec((1, D), const), pl.BlockSpec((1, 128), const)],
        out_shape=[SDS((T, D_FF), BF16), SDS((T, D), F32), SDS((T, D), BF16), SDS((1, D), F32), SDS((1, 128), F32)],
        scratch_shapes=[pltpu.VMEM((tm, D), F32), pltpu.VMEM((tm, FF_STEP), BF16)],
        compiler_params=_params(2),
    )(hf, wfi3, wfo, x1, tgt, g3)


def _bwd_ff(dff, f, wfi3, wfo, x1, dy, mix, g1, g2):
    T = dff.shape[0]
    tm = min(T, 512)
    last = FF_STEPS - 1

    def body(dff_ref, f_ref, wfi_ref, wfo_ref, x1_ref, dy_ref, mix_ref, g1_ref, g2_ref,
             df_ref, dx1_ref, dmix_ref, dg2_ref, dg1_ref, acc):
        i, p = pl.program_id(0), pl.program_id(1)

        @pl.when((i == 0) & (p == 0))
        def _():
            dg2_ref[...] = jnp.zeros_like(dg2_ref)
            dg1_ref[...] = jnp.zeros_like(dg1_ref)

        dr = _nt(dff_ref[...], wfo_ref[...])
        df_ref[...] = (dr * (2.0 * jnp.maximum(f_ref[...].astype(F32), 0.0))).astype(BF16)
        part = _nt(df_ref[:, :FF_TILE], wfi_ref[0])
        for s in range(1, FF_SLABS):
            part = part + _nt(df_ref[:, s * FF_TILE:(s + 1) * FF_TILE], wfi_ref[s])

        @pl.when(p == 0)
        def _():
            acc[...] = part

        @pl.when(p > 0)
        def _():
            acc[...] += part

        @pl.when(p == last)
        def _():
            dhf = acc[...]
            r2, xh = _rms_stats(x1_ref[...])
            dg2_ref[...] += _colsum(dhf * xh)
            dx1 = dy_ref[...] + _rms_bwd(dhf, xh, r2, g2_ref[...])
            dx1_ref[...] = dx1
            r1, mh = _rms_stats(mix_ref[...])
            dg1_ref[...] += _colsum(dx1 * mh)
            dmix_ref[...] = _rms_bwd(dx1, mh, r1, g1_ref[...]).astype(BF16)

    row = lambda i, p: (i, 0)
    const = lambda i, p: (0, 0)
    return pl.pallas_call(
        body, name="bwd_ff", grid=(T // tm, FF_STEPS),
        in_specs=[pl.BlockSpec((tm, D), row), pl.BlockSpec((tm, FF_STEP), lambda i, p: (i, p)),
                  pl.BlockSpec((FF_SLABS, D, FF_TILE), lambda i, p: (p, 0, 0)), pl.BlockSpec((FF_STEP, D), lambda i, p: (p, 0)),
                  pl.BlockSpec((tm, D), row), pl.BlockSpec((tm, D), row), pl.BlockSpec((tm, D), row),
                  pl.BlockSpec((1, D), const), pl.BlockSpec((1, D), const)],
        out_specs=[pl.BlockSpec((tm, FF_STEP), lambda i, p: (i, p)), pl.BlockSpec((tm, D), row),
                   pl.BlockSpec((tm, D), row), pl.BlockSpec((1, D), const), pl.BlockSpec((1, D), const)],
        out_shape=[SDS((T, D_FF), BF16), SDS((T, D), F32), SDS((T, D), BF16), SDS((1, D), F32), SDS((1, D), F32)],
        scratch_shapes=[pltpu.VMEM((tm, D), F32)],
        compiler_params=_params(2),
    )(dff, f, wfi3, wfo, x1, dy, mix, g1, g2)


def _wgrad_ff(hf, df, f, dff):
    T = hf.shape[0]
    tt = min(T, 1024)
    wide = 2 * FF_TILE

    def body(hf_ref, df_ref, f_ref, dff_ref, dwfi_ref, dwfo_ref, acc_i, acc_o):
        t = pl.program_id(1)

        @pl.when(t == 0)
        def _():
            acc_i[...] = jnp.zeros_like(acc_i)
            acc_o[...] = jnp.zeros_like(acc_o)

        acc_i[...] += _tn(hf_ref[...], df_ref[...])
        rl = jnp.maximum(f_ref[...].astype(F32), 0.0)
        acc_o[...] += _tn((rl * rl).astype(BF16), dff_ref[...])

        @pl.when(t == T // tt - 1)
        def _():
            dwfi_ref[0] = acc_i[:, :FF_TILE].astype(BF16)
            dwfi_ref[1] = acc_i[:, FF_TILE:].astype(BF16)
            dwfo_ref[...] = acc_o[...].astype(BF16)

    return pl.pallas_call(
        body, name="wgrad_ff", grid=(D_FF // wide, T // tt),
        in_specs=[pl.BlockSpec((tt, D), lambda p, t: (t, 0)), pl.BlockSpec((tt, wide), lambda p, t: (t, p)),
                  pl.BlockSpec((tt, wide), lambda p, t: (t, p)), pl.BlockSpec((tt, D), lambda p, t: (t, 0))],
        out_specs=[pl.BlockSpec((2, D, FF_TILE), lambda p, t: (p, 0, 0)), pl.BlockSpec((wide, D), lambda p, t: (p, 0))],
        out_shape=[SDS((FF_SPLIT, D, FF_TILE), BF16), SDS((D_FF, D), BF16)],
        scratch_shapes=[pltpu.VMEM((D, wide), F32), pltpu.VMEM((wide, D), F32)],
        compiler_params=_params(2),
    )(hf, df, f, dff)


def _bwd_mix(dmix, proj, a2, b2, wo, wa, wb, after=None):
    T = dmix.shape[0]
    tm = min(T, 512)
    half = D // 2

    def body(dmix_ref, ga0, ga1, gb0, gb1, a2_ref, b2_ref, wo_ref, wa_ref, wb_ref,
             da2_ref, db2_ref, dg_ref, da_ref, datt_ref):
        dmg = _nt(dmix_ref[...], wo_ref[...])
        sa = _sigmoid(jnp.concatenate([ga0[...], ga1[...]], axis=1).astype(F32))
        sb = _sigmoid(jnp.concatenate([gb0[...], gb1[...]], axis=1).astype(F32))
        da2 = (dmg * sa).astype(BF16)
        db2 = (dmg * sb).astype(BF16)
        da2_ref[...] = da2
        db2_ref[...] = db2
        dg_ref[:, :D] = (dmg * a2_ref[...].astype(F32) * (sa * (1.0 - sa))).astype(BF16)
        dg_ref[:, D:] = (dmg * b2_ref[...].astype(F32) * (sb * (1.0 - sb))).astype(BF16)
        da_ref[...] = _nt(da2, wa_ref[...]).astype(BF16)
        datt_ref[...] = _nt(db2, wb_ref[...]).astype(BF16)

    row = lambda i: (i, 0)
    const = lambda i: (0, 0)
    gspec = lambda off: pl.BlockSpec((tm, half), lambda i: (i, off // half))
    body, dep_specs, deps = _after(body, 10, after)
    return pl.pallas_call(
        body, name="bwd_mix", grid=(T // tm,),
        in_specs=[pl.BlockSpec((tm, D), row), gspec(OFF_GA), gspec(OFF_GA + half), gspec(OFF_GB), gspec(OFF_GB + half),
                  pl.BlockSpec((tm, D), row), pl.BlockSpec((tm, D), row),
                  _resident((D, D)), _resident((D, D)), _resident((D, D))] + dep_specs,
        out_specs=[pl.BlockSpec((tm, D), row), pl.BlockSpec((tm, D), row), pl.BlockSpec((tm, 2 * D), row),
                   pl.BlockSpec((tm, D), row), pl.BlockSpec((tm, D), row)],
        out_shape=[SDS((T, D), BF16), SDS((T, D), BF16), SDS((T, 2 * D), BF16), SDS((T, D), BF16), SDS((T, D), BF16)],
        compiler_params=_params(1),
    )(dmix, proj, proj, proj, proj, a2, b2, wo, wa, wb, *deps)


def _wgrad_mix(merged, dmix, a, da2, att, db2):
    T = merged.shape[0]
    tt = min(T, 512)

    def body(mg_ref, dmix_ref, a_ref, da2_ref, att_ref, db2_ref, dwo_ref, dwa_ref, dwb_ref, acc):
        t = pl.program_id(0)

        @pl.when(t == 0)
        def _():
            acc[...] = jnp.zeros_like(acc)

        acc[0] += _tn(mg_ref[...], dmix_ref[...])
        acc[1] += _tn(a_ref[...], da2_ref[...])
        acc[2] += _tn(att_ref[...], db2_ref[...])

        @pl.when(t == T // tt - 1)
        def _():
            dwo_ref[...] = acc[0].astype(BF16)
            dwa_ref[...] = acc[1].astype(BF16)
            dwb_ref[...] = acc[2].astype(BF16)

    return pl.pallas_call(
        body, name="wgrad_mix", grid=(T // tt,),
        in_specs=[pl.BlockSpec((tt, D), lambda t: (t, 0))] * 6,
        out_specs=[pl.BlockSpec((D, D), lambda t: (0, 0))] * 3,
        out_shape=[SDS((D, D), BF16)] * 3,
        scratch_shapes=[pltpu.VMEM((3, D, D), F32)],
        compiler_params=_params(1),
    )(merged, dmix, a, da2, att, db2)


def _bwd_attn(proj, cos, sin, sinks, datt, after=None):
    T = proj.shape[0]
    nb = T // CHUNK
    kw = N_KV * HEAD
    cur, prev, specs = _attn_specs(nb, True)

    def body(q_ref, kp_ref, kc_ref, vp_ref, vc_ref, cp_ref, cc_ref, sp_ref, sc_ref, sink_ref, do_ref,
             dq_ref, dkv_ref, dsink_ref, carry_k, carry_v, dq_acc):
        i = pl.program_id(0)

        @pl.when(i == 0)
        def _():
            carry_k[...] = jnp.zeros_like(carry_k)
            carry_v[...] = jnp.zeros_like(carry_v)
            dsink_ref[...] = jnp.zeros_like(dsink_ref)

        @pl.when(i < nb)
        def _():
            q, kb, vb, (q_tab, k_tab) = _attn_load(q_ref, kp_ref, kc_ref, vp_ref, vc_ref, cp_ref, cc_ref,
                                                   sp_ref, sc_ref)
            prev, bias = _fold_masks(i == 0)
            do = do_ref[...]
            lane = lax.broadcasted_iota(jnp.int32, (1, 128), 1)
            dsink = jnp.zeros((1, 128), F32)
            dks, dvs = [], []
            for g in range(N_KV):
                k2 = _head_pair_operand(kb, g)
                v2 = _head_pair_operand(vb, g)
                dk2 = jnp.zeros((4 * CHUNK, 128), F32)
                dv2 = jnp.zeros((4 * CHUNK, 128), F32)
                for r in range(PAIRS_PER_KV):
                    pair = g * PAIRS_PER_KV + r
                    qp = q[:, pair * 128:(pair + 1) * 128]
                    dop = do[:, pair * 128:(pair + 1) * 128]
                    s2 = _nt(k2, qp) * (HEAD ** -0.5)
                    dp2 = _nt(v2, dop)
                    ps, dss = [], []
                    for e in range(2):
                        rows = slice(e * 2 * CHUNK, (e + 1) * 2 * CHUNK)
                        p, psink = _softmax_sink(_fold(s2[rows], prev) + bias, sink_ref[2 * pair + e], 0)
                        dp = _fold(dp2[rows], prev)
                        delta = jnp.sum(p * dp, axis=0, keepdims=True)
                        ps.append(_unfold(p, prev).astype(BF16))
                        dss.append(_unfold(p * (dp - delta) * (HEAD ** -0.5), prev).astype(BF16))
                        dsink = dsink + jnp.where(lane == 2 * pair + e, -jnp.sum(psink * delta), 0.0)
                    ds2 = jnp.concatenate(dss, axis=0)
                    dq_acc[:, pair * 128:(pair + 1) * 128] = _tn(ds2, k2)
                    dk2 = dk2 + _nn(ds2, qp)
                    dv2 = dv2 + _nn(jnp.concatenate(ps, axis=0), dop)
                dks.append(_head_pair_gradient(dk2, g))
                dvs.append(_head_pair_gradient(dv2, g))
            dsink_ref[...] += dsink
            dq_ref[...] = _rope_bwd(dq_acc[...], *q_tab).astype(BF16)
            dkb = _rope_bwd(jnp.concatenate([dks[0] + dks[1], dks[2] + dks[3]], axis=1), *k_tab)
            dvb = jnp.concatenate([dvs[0] + dvs[1], dvs[2] + dvs[3]], axis=1)
            dkv_ref[:, :kw] = (carry_k[...] + dkb[:CHUNK]).astype(BF16)
            dkv_ref[:, kw:] = (carry_v[...] + dvb[:CHUNK]).astype(BF16)
            carry_k[...] = dkb[CHUNK:]
            carry_v[...] = dvb[CHUNK:]

        @pl.when(i == nb)
        def _():
            dkv_ref[:, :kw] = carry_k[...].astype(BF16)
            dkv_ref[:, kw:] = carry_v[...].astype(BF16)

    body, dep_specs, deps = _after(body, 11, after)
    return pl.pallas_call(
        body, name="bwd_attn", grid=(nb + 1,),
        in_specs=specs + [pl.BlockSpec((CHUNK, D), lambda i: (cur(i), 0))] + dep_specs,
        out_specs=[pl.BlockSpec((CHUNK, D), lambda i: (cur(i), 0)),
                   pl.BlockSpec((CHUNK, 2 * kw), lambda i: (jnp.maximum(i - 1, 0), 0)),
                   pl.BlockSpec((1, 128), lambda i: (0, 0))],
        out_shape=[SDS((T, D), BF16), SDS((T, 2 * kw), BF16), SDS((1, 128), F32)],
        scratch_shapes=[pltpu.VMEM((CHUNK, kw), F32), pltpu.VMEM((CHUNK, kw), F32), pltpu.VMEM((CHUNK, D), F32)],
        compiler_params=_params(1),
    )(proj, proj, proj, proj, proj, cos, cos, sin, sin, sinks, datt, *deps)


def _bwd_sgu(proj, da, lng, lnb, ws, bst):
    T = proj.shape[0]
    tc = min(T, 512)
    nsteps = T // tc

    def body(u_ref, vs_ref, da_ref, lng_ref, lnb_ref, ws_ref, bst_ref,
             duv_ref, dws_ref, dbs_ref, dlng_ref, dlnb_ref, dvn_s, dgu_s, dmx_sum):
        i = pl.program_id(0)

        @pl.when(i == 0)
        def _():
            dws_ref[...] = jnp.zeros_like(dws_ref)
            dlng_ref[...] = jnp.zeros_like(dlng_ref)
            dlnb_ref[...] = jnp.zeros_like(dlnb_ref)
            dmx_sum[...] = jnp.zeros_like(dmx_sum)

        u, vs, gu, tu, tv, rstd, vhat, vn = _sgu_forward_parts(u_ref, vs_ref, lng_ref, lnb_ref)
        da = da_ref[...].astype(F32)
        for g in range(GROUPS):
            wm = _masked_ws(ws_ref, g)
            cols = slice(g * CHUNK, (g + 1) * CHUNK)
            dws = jnp.zeros((CHUNK, CHUNK), F32)
            dsum = jnp.zeros((CHUNK, CHUNK), F32)
            for c in range(tc // CHUNK):
                rows = slice(c * CHUNK, (c + 1) * CHUNK)
                vn_cg = vn[rows, cols]
                mixed = _nn(wm, vn_cg) + bst_ref[:, g:g + 1]
                dgu_s[rows, cols] = da[rows, cols] * mixed
                dmx = da[rows, cols] * gu[rows, cols]
                dmxb = dmx.astype(BF16)
                dws = dws + _nt(dmxb, vn_cg)
                dsum = dsum + dmx
                dvn_s[rows, cols] = _tn(wm, dmxb)
            dws_ref[g] += dws
            dmx_sum[:, cols] += dsum
        dvn = dvn_s[...]
        dlng_ref[...] += _colsum(dvn * vhat)
        dlnb_ref[...] += _colsum(dvn)
        dvh = dvn * lng_ref[...]
        dgv = rstd * (dvh - jnp.mean(dvh, axis=-1, keepdims=True) - vhat * jnp.mean(dvh * vhat, axis=-1, keepdims=True))
        duv_ref[:, :D] = (dgu_s[...] * _gelu_grad(u, tu)).astype(BF16)
        duv_ref[:, D:] = (dgv * _gelu_grad(vs, tv)).astype(BF16)

        @pl.when(i == nsteps - 1)
        def _():
            row = lax.broadcasted_iota(jnp.int32, (CHUNK, CHUNK), 0)
            col = lax.broadcasted_iota(jnp.int32, (CHUNK, CHUNK), 1)
            for g in range(GROUPS):
                dws_ref[g] = jnp.where(row >= col, dws_ref[g], 0.0)
                dbs_ref[g:g + 1, :] = _colsum(dmx_sum[:, g * CHUNK:(g + 1) * CHUNK].T)

    const2 = lambda i: (0, 0)
    return pl.pallas_call(
        body, name="bwd_sgu", grid=(nsteps,),
        in_specs=[pl.BlockSpec((tc, D), lambda i: (i, 0)), pl.BlockSpec((tc, D), lambda i: (i, 1)),
                  pl.BlockSpec((tc, D), lambda i: (i, 0)), pl.BlockSpec((1, D), const2), pl.BlockSpec((1, D), const2),
                  pl.BlockSpec((GROUPS, CHUNK, CHUNK), lambda i: (0, 0, 0)), pl.BlockSpec((CHUNK, GROUPS), const2)],
        out_specs=[pl.BlockSpec((tc, 2 * D), lambda i: (i, 0)), pl.BlockSpec((GROUPS, CHUNK, CHUNK), lambda i: (0, 0, 0)),
                   pl.BlockSpec((GROUPS, CHUNK), const2), pl.BlockSpec((1, D), const2), pl.BlockSpec((1, D), const2)],
        out_shape=[SDS((T, 2 * D), BF16), SDS((GROUPS, CHUNK, CHUNK), F32), SDS((GROUPS, CHUNK), F32),
                   SDS((1, D), F32), SDS((1, D), F32)],
        scratch_shapes=[pltpu.VMEM((tc, D), F32), pltpu.VMEM((tc, D), F32), pltpu.VMEM((CHUNK, D), F32)],
        compiler_params=_params(1),
    )(proj, proj, da, lng, lnb, ws, bst)


IN_SEG_WIDTHS = (2 * D, D, 2 * N_KV * HEAD, 2 * D)


def _resident(shape):
    return pl.BlockSpec(shape, lambda *_: (0,) * len(shape), pipeline_mode=pl.Buffered(1))


def _bwd_in(duv, dq, dkv, dg, win, x, dx1, g0, after=None):
    T = x.shape[0]
    tm = min(T, 512)

    def body(duv_ref, dq_ref, dkv_ref, dg_ref, w_ref, x_ref, dx1_ref, g0_ref, gx_ref, dg0_ref):
        @pl.when(pl.program_id(0) == 0)
        def _():
            dg0_ref[...] = jnp.zeros_like(dg0_ref)

        dh, off = None, 0
        for ref, width in zip((duv_ref, dq_ref, dkv_ref, dg_ref), IN_SEG_WIDTHS):
            part = _nt(ref[...], w_ref[:, off:off + width])
            dh = part if dh is None else dh + part
            off += width
        r0, xh = _rms_stats(x_ref[...])
        dg0_ref[...] += _colsum(dh * xh)
        gx_ref[...] = dx1_ref[...] + _rms_bwd(dh, xh, r0, g0_ref[...])

    row = lambda i: (i, 0)
    body, dep_specs, deps = _after(body, 8, after)
    return pl.pallas_call(
        body, name="bwd_in", grid=(T // tm,),
        in_specs=[pl.BlockSpec((tm, w), row) for w in IN_SEG_WIDTHS] + [
            _resident((D, IN_W)), pl.BlockSpec((tm, D), row), pl.BlockSpec((tm, D), row),
            pl.BlockSpec((1, D), lambda i: (0, 0))] + dep_specs,
        out_specs=[pl.BlockSpec((tm, D), row), pl.BlockSpec((1, D), lambda i: (0, 0))],
        out_shape=[SDS((T, D), F32), SDS((1, D), F32)],
        compiler_params=_params(1),
    )(duv, dq, dkv, dg, win, x, dx1, g0, *deps)


def _wgrad_cols(h, segs, name):
    T = h.shape[0]
    tt = min(T, 1024)
    widths = [s.shape[1] for s in segs]

    def body(h_ref, *refs):
        dw_ref, acc = refs[-2], refs[-1]
        t = pl.program_id(0)

        @pl.when(t == 0)
        def _():
            acc[...] = jnp.zeros_like(acc)

        off = 0
        for ref, width in zip(refs[:-2], widths):
            acc[:, off:off + width] += _tn(h_ref[...], ref[...])
            off += width

        @pl.when(t == T // tt - 1)
        def _():
            dw_ref[...] = acc[...].astype(BF16)

    row = lambda t: (t, 0)
    return pl.pallas_call(
        body, name=name, grid=(T // tt,),
        in_specs=[pl.BlockSpec((tt, D), row)] + [pl.BlockSpec((tt, w), row) for w in widths],
        out_specs=pl.BlockSpec((D, sum(widths)), lambda t: (0, 0)),
        out_shape=SDS((D, sum(widths)), BF16),
        scratch_shapes=[pltpu.VMEM((D, sum(widths)), F32)],
        compiler_params=_params(1),
    )(h, *segs)


def _wgrad_in(h, duv, dq, dkv, dg):
    return jnp.concatenate([_wgrad_cols(h, [duv], "wgrad_in_uv"), _wgrad_cols(h, [dq, dkv], "wgrad_in_qkv"),
                            _wgrad_cols(h, [dg], "wgrad_in_gates")], axis=1)


def _place():
    x, y, c = lax.axis_index("x"), lax.axis_index("y"), lax.axis_index("c")
    return x, y, c, 4 * x + 2 * y + c


def _peers(x, y, c):
    out = []
    for mask in range(1, N_DEV):
        px = 1 - x if mask & 4 else x
        py = 1 - y if mask & 2 else y
        pc = 1 - c if mask & 1 else c
        out.append(((px, py, pc), 4 * px + 2 * py + pc))
    return out


def _all_to_all(arrays, gather, name):
    n = len(arrays)

    def body(*refs):
        ins, outs = refs[:n], refs[n:2 * n]
        send_sems, recv_sems, local_sems = refs[2 * n:]
        x, y, c, me = _place()
        local, sends, recvs = [], [], []
        for a in range(n):
            src_own = ins[a] if gather[a] else ins[a].at[me]
            local.append(pltpu.make_async_copy(src_own, outs[a].at[me], local_sems.at[a]))
            for k, (peer, pid) in enumerate(_peers(x, y, c)):
                sem = a * (N_DEV - 1) + k
                src = ins[a] if gather[a] else ins[a].at[pid]
                sends.append(pltpu.make_async_remote_copy(
                    src_ref=src, dst_ref=outs[a].at[me], send_sem=send_sems.at[sem], recv_sem=recv_sems.at[sem],
                    device_id=peer, device_id_type=MESH))
                recvs.append(pltpu.make_async_remote_copy(
                    src_ref=src, dst_ref=outs[a].at[pid], send_sem=send_sems.at[sem], recv_sem=recv_sems.at[sem],
                    device_id=peer, device_id_type=MESH))
        for cp in local + sends:
            cp.start()
        for cp in recvs:
            cp.wait_recv()
        for cp in sends:
            cp.wait_send()
        for cp in local:
            cp.wait()

    out_shape = [SDS((N_DEV,) + a.shape if gt else a.shape, a.dtype) for a, gt in zip(arrays, gather)]
    nsem = n * (N_DEV - 1)
    return pl.pallas_call(
        body, name=name,
        in_specs=[pl.BlockSpec(memory_space=pl.ANY)] * n,
        out_specs=[pl.BlockSpec(memory_space=pl.ANY)] * n,
        out_shape=out_shape,
        scratch_shapes=[pltpu.SemaphoreType.DMA((nsem,)), pltpu.SemaphoreType.DMA((nsem,)), pltpu.SemaphoreType.DMA((n,))],
    )(*arrays)


def _gather_two_level(shard, name):
    def body(x_ref, out_ref, send_sems, recv_sems, local_sem):
        x, y, c = lax.axis_index("x"), lax.axis_index("y"), lax.axis_index("c")
        me, sibling = (x, y, c), (x, y, 1 - c)
        chips = [(1 - x, y), (x, 1 - y), (1 - x, 1 - y)]

        def slot(px, py, pc):
            return out_ref.at[4 * px + 2 * py + pc]

        def copy(k, block, to, src=None):
            return pltpu.make_async_remote_copy(
                src_ref=slot(*block) if src is None else src, dst_ref=slot(*block),
                send_sem=send_sems.at[k], recv_sem=recv_sems.at[k], device_id=to, device_id_type=MESH)

        mine = pltpu.make_async_copy(x_ref, slot(*me), local_sem)
        mine.start()
        first = [copy(0, me, sibling, src=x_ref)]
        first += [copy(1 + j, me, (*chip, c), src=x_ref) for j, chip in enumerate(chips)]
        for cp in first:
            cp.start()
        passed = [copy(4 + j, (*chip, c), sibling) for j, chip in enumerate(chips)]
        for j, chip in enumerate(chips):
            copy(1 + j, (*chip, c), me).wait_recv()
            passed[j].start()
        copy(0, sibling, me).wait_recv()
        for j, chip in enumerate(chips):
            copy(4 + j, (*chip, 1 - c), me).wait_recv()
        for cp in first + passed:
            cp.wait_send()
        mine.wait()

    return pl.pallas_call(
        body, name=name,
        in_specs=[_ANY], out_specs=_ANY,
        out_shape=SDS((N_DEV,) + shard.shape, shard.dtype),
        scratch_shapes=[pltpu.SemaphoreType.DMA((N_DEV - 1,)), pltpu.SemaphoreType.DMA((N_DEV - 1,)),
                        pltpu.SemaphoreType.DMA],
    )(shard)


_HBM = pl.BlockSpec(memory_space=pltpu.HBM)
_SEM = pl.BlockSpec(memory_space=pltpu.SEMAPHORE)
_EFFECT = pltpu.SideEffectType.DATAFLOW_SIDE_EFFECTING
GATHER = "gather"
SCATTER = "scatter"
SPREAD = "spread"


def _zone_shape(a, mode):
    if mode == GATHER:
        return (N_DEV,) + a.shape
    return (N_DEV - 1,) + (a.shape[1:] if mode == SCATTER else a.shape)


def _start_copies(arrays, modes, name, after=None):
    n = len(arrays)
    zones = [lax.empty(_zone_shape(a, m), a.dtype) for a, m in zip(arrays, modes)]

    def body(*refs):
        ins, lands = refs[:n], refs[n:2 * n]
        send_sems, recv_sems = refs[-2 * n - 3], refs[-2 * n - 2]
        token = refs[-1]
        x, y, c, me = _place()
        for a in range(n):
            for k, (peer, pid) in enumerate(_peers(x, y, c)):
                src = ins[a].at[pid] if modes[a] == SCATTER else ins[a]
                dst = lands[a].at[me] if modes[a] == GATHER else lands[a].at[k]
                pltpu.make_async_remote_copy(src_ref=src, dst_ref=dst, send_sem=send_sems.at[a], recv_sem=recv_sems.at[a],
                                             device_id=peer, device_id_type=MESH).start()
        token[...] = jnp.zeros_like(token)

    hbm = lambda a: pltpu.HBM(a.shape, a.dtype)
    sems = pltpu.SemaphoreType.DMA((n,))
    extra = [] if after is None else [after]
    operands = [pltpu.with_memory_space_constraint(a, pltpu.HBM) for a in list(arrays) + zones]
    res = pl.pallas_call(
        body, name=name,
        out_shape=(sems, sems, *[hbm(a) for a in arrays], *[hbm(z) for z in zones], SDS((8, 128), F32)),
        in_specs=[_HBM] * (2 * n) + [_ANY] * len(extra),
        out_specs=(_SEM, _SEM, *[_HBM] * (2 * n), pl.BlockSpec(memory_space=pltpu.VMEM)),
        input_output_aliases={i: 2 + i for i in range(2 * n)},
        compiler_params=pltpu.CompilerParams(has_side_effects=_EFFECT),
    )(*operands, *extra)
    return res[0], res[1], list(res[2:2 + n]), list(res[2 + n:2 + 2 * n]), res[-1]


def _wait_copies(started, after, name):
    send_sems, recv_sems, thru, zones, _ = started
    n = len(thru)

    def body(*refs):
        lands = refs[n:2 * n]
        send_ref, recv_ref = refs[2 * n], refs[2 * n + 1]
        x, y, c, _ = _place()
        for a in range(n):
            seven = lands[a].at[pl.ds(0, N_DEV - 1)]
            cp = pltpu.make_async_remote_copy(src_ref=seven, dst_ref=seven, send_sem=send_ref.at[a], recv_sem=recv_ref.at[a],
                                              device_id=(x, y, 1 - c), device_id_type=MESH)
            cp.wait_send()
            cp.wait_recv()

    hbm = lambda a: pltpu.HBM(a.shape, a.dtype)
    res = pl.pallas_call(
        body, name=name,
        out_shape=tuple(hbm(a) for a in thru + zones),
        in_specs=[_HBM] * (2 * n) + [_SEM, _SEM, _ANY],
        out_specs=tuple([_HBM] * (2 * n)),
        input_output_aliases={i: i for i in range(2 * n)},
        compiler_params=pltpu.CompilerParams(has_side_effects=_EFFECT),
    )(*thru, *zones, send_sems, recv_sems, after)
    return list(res[:n]), list(res[n:])


def _adamw_math(g, w, m, v):
    m2 = ADAM_B1 * m + (1.0 - ADAM_B1) * g
    v2 = ADAM_B2 * v + (1.0 - ADAM_B2) * (g * g)
    m_hat = m2 / (1.0 - ADAM_B1 ** ADAM_STEP)
    v_hat = v2 / (1.0 - ADAM_B2 ** ADAM_STEP)
    delta = -ADAM_LR * (m_hat / (jnp.sqrt(v_hat) + ADAM_EPS) + ADAM_WD * w)
    return delta, m2, v2


def _sum_adamw(parts, w, m, v, name):
    R, C = w.shape
    tr = max(t for t in (128, 64, 32, 16, 8) if R % t == 0)

    def body(p_ref, w_ref, m_ref, v_ref, g_ref, d_ref, m2_ref, v2_ref):
        g = p_ref[0]
        for k in range(1, N_DEV):
            g = g + p_ref[k]
        g_ref[...] = g
        d_ref[...], m2_ref[...], v2_ref[...] = _adamw_math(g, w_ref[...], m_ref[...], v_ref[...])

    blk = pl.BlockSpec((tr, C), lambda i: (i, 0))
    return pl.pallas_call(
        body, name=name, grid=(R // tr,),
        in_specs=[pl.BlockSpec((N_DEV, tr, C), lambda i: (0, i, 0)), blk, blk, blk],
        out_specs=[blk] * 4,
        out_shape=[SDS((R, C), F32)] * 4,
        compiler_params=_params(1),
    )(parts, w, m, v)


def _sum_adamw_peers(me, own, parts, w, m, v, name, replicated):
    R, C = w.shape
    tr = max(t for t in (128, 64, 32, 16, 8) if R % t == 0)

    def body(me_ref, own_ref, p_ref, w_ref, m_ref, v_ref, g_ref, d_ref, m2_ref, v2_ref):
        if replicated:
            mine = me_ref[0]
            g = None
            for j in range(N_DEV):
                k = jnp.maximum(jnp.bitwise_xor(mine, j) - 1, 0)
                term = jnp.where(mine == j, own_ref[...], p_ref[k])
                g = term if g is None else g + term
        else:
            g = own_ref[...].astype(F32)
            for k in range(N_DEV - 1):
                g = g + p_ref[k].astype(F32)
        g_ref[...] = g
        d_ref[...], m2_ref[...], v2_ref[...] = _adamw_math(g, w_ref[...], m_ref[...], v_ref[...])

    blk = pl.BlockSpec((tr, C), lambda i, me_ref: (i, 0))
    own_spec = blk if replicated else pl.BlockSpec((None, tr, C), lambda i, me_ref: (me_ref[0], i, 0))
    return pl.pallas_call(
        body, name=name,
        grid_spec=pltpu.PrefetchScalarGridSpec(
            num_scalar_prefetch=1, grid=(R // tr,),
            in_specs=[own_spec, pl.BlockSpec((N_DEV - 1, tr, C), lambda i, me_ref: (0, i, 0)), blk, blk, blk],
            out_specs=[blk] * 4),
        out_shape=[SDS((R, C), F32)] * 4,
        compiler_params=_params(1),
    )(me, own, parts, w, m, v)


SMALL = ("ln_v_gain", "ln_v_bias", "w_spatial", "b_spatial", "sinks", "norm_mix_post", "norm_ff_pre", "norm_ff_post")
SMALL_ROWS = {"ln_v_gain": 8, "ln_v_bias": 8, "w_spatial": 1024, "b_spatial": 8, "sinks": 8,
              "norm_mix_post": 8, "norm_ff_pre": 8, "norm_ff_post": 8}
SMALL_PACK_ROWS = 1152


def _pack_small(vals):
    rows = []
    for name in SMALL:
        flat = vals[name].reshape(-1)
        pad = SMALL_ROWS[name] * 128 - flat.shape[0]
        if pad:
            flat = jnp.concatenate([flat, jnp.zeros((pad,), F32)])
        rows.append(flat.reshape(SMALL_ROWS[name], 128))
    rows.append(jnp.zeros((SMALL_PACK_ROWS - sum(SMALL_ROWS.values()), 128), F32))
    return jnp.concatenate(rows, axis=0)


def _unpack_small(packed, shapes):
    out, r = {}, 0
    for name in SMALL:
        n = 1
        for s in shapes[name]:
            n *= s
        out[name] = packed[r:r + SMALL_ROWS[name]].reshape(-1)[:n].reshape(shapes[name])
        r += SMALL_ROWS[name]
    return out


def _rope_rows():
    d = jnp.arange(128) % HEAD
    inv = ROPE_THETA ** (-(2.0 * (d % (ROPE // 2))).astype(F32) / ROPE)
    invf = jnp.where(d < ROPE, inv, 0.0).astype(F32).reshape(1, 128)
    sgn = jnp.where(d < ROPE // 2, -1.0, jnp.where(d < ROPE, 1.0, 0.0)).astype(F32).reshape(1, 128)
    return invf, sgn


def kernel(x, positions, w_in, ln_v_gain, ln_v_bias, w_spatial, b_spatial, sinks, w_a, w_b, w_o, norm_mix_pre, norm_mix_post, w_ff_in, w_ff_out, norm_ff_pre, norm_ff_post, loss_target, m_w_in, m_ln_v_gain, m_ln_v_bias, m_w_spatial, m_b_spatial, m_sinks, m_w_a, m_w_b, m_w_o, m_norm_mix_pre, m_norm_mix_post, m_w_ff_in, m_w_ff_out, m_norm_ff_pre, m_norm_ff_post, v_w_in, v_ln_v_gain, v_ln_v_bias, v_w_spatial, v_b_spatial, v_sinks, v_w_a, v_w_b, v_w_o, v_norm_mix_pre, v_norm_mix_post, v_w_ff_in, v_w_ff_out, v_norm_ff_pre, v_norm_ff_post):
    given = dict(locals())
    T = x.shape[1]
    xt = x[0]
    tgt = loss_target[0]
    bst = b_spatial[0].T
    ws = w_spatial[0]

    me = 4 * lax.axis_index("x") + 2 * lax.axis_index("y") + lax.axis_index("c")
    me_arr = me.astype(jnp.int32).reshape(1)

    def with_own(zone, shard):
        return lax.dynamic_update_slice(zone, shard[None], (me,) + (0,) * shard.ndim)

    rest = ("w_a", "w_b", "w_o", "w_ff_in", "w_ff_out")
    shard = {n: given[n][0].astype(BF16) for n in ("w_in",) + rest}
    win8 = _gather_two_level(shard["w_in"], "gather_in")
    g_rest = _start_copies([shard[n] for n in rest], [GATHER] * len(rest), "gather_rest_start", after=win8)
    win = jnp.transpose(win8, (1, 0, 2)).reshape(D, IN_W)
    cos, sin = _rope_tables(positions.astype(F32).reshape(T, 1), *_rope_rows())

    proj, h = _fwd_in(xt, norm_mix_pre, win)
    a = _fwd_sgu(proj, ln_v_gain, ln_v_bias, ws, bst)
    att = _fwd_attn(proj, cos, sin, sinks[0])
    gw = {n: with_own(z, own) for n, own, z in zip(rest, *_wait_copies(g_rest, att, "gather_rest_wait"))}
    wa, wb, wo = (gw[n].reshape(D, D) for n in ("w_a", "w_b", "w_o"))
    wfi3 = gw["w_ff_in"]
    wfo = gw["w_ff_out"].reshape(D_FF, D)
    merged, a2, b2, mix, x1, hf = _fwd_mix(a, att, proj, xt, wa, wb, wo, norm_mix_post, norm_ff_pre)
    f, dy, dff, dg3, loss_part = _fwd_ff(hf, wfi3, wfo, x1, tgt, norm_ff_post)

    df, dx1, dmix, dg2, dg1 = _bwd_ff(dff, f, wfi3, wfo, x1, dy, mix, norm_mix_post, norm_ff_pre)
    dwfi3, dwfo = _wgrad_ff(hf, df, f, dff)
    own_ff = [dwfi3, dwfo.reshape(N_DEV, D_FF // N_DEV, D)]
    x_ff = _start_copies(own_ff, [SCATTER] * 2, "exchange_ff_start")
    da2, db2, dgate, da, datt = _bwd_mix(dmix, proj, a2, b2, wo, wa, wb, after=x_ff[-1])
    dwo, dwa, dwb = _wgrad_mix(merged, dmix, a, da2, att, db2)
    own_mix = [g.reshape(N_DEV, D // N_DEV, D) for g in (dwa, dwb, dwo)]
    x_mix = _start_copies(own_mix, [SCATTER] * 3, "exchange_mix_start")
    dq, dkv, dsink = _bwd_attn(proj, cos, sin, sinks[0], datt, after=x_mix[-1])
    duv, dws, dbs, dlng, dlnb = _bwd_sgu(proj, da, ln_v_gain, ln_v_bias, ws, bst)
    small_grads = {"ln_v_gain": dlng, "ln_v_bias": dlnb, "w_spatial": dws, "b_spatial": dbs, "sinks": dsink[:, :N_Q],
                   "norm_mix_post": dg1, "norm_ff_pre": dg2, "norm_ff_post": dg3}
    x_small = _start_copies([_pack_small(small_grads)], [SPREAD], "exchange_small_start")
    dwin = _wgrad_in(h, duv, dq, dkv, dgate)
    own_in = [jnp.transpose(dwin.reshape(D, N_DEV, IN_W // N_DEV), (1, 0, 2))]
    x_in = _start_copies(own_in, [SCATTER], "exchange_in_start", after=x_small[-1])
    grad_x, dg0 = _bwd_in(duv, dq, dkv, dgate, win, xt, dx1, norm_mix_pre, after=x_in[-1])

    results = {}

    def update(n, own, parts):
        results[n] = [r.reshape(given[n].shape) for r in _sum_adamw_peers(
            me_arr, own, parts, given[n][0], given["m_" + n][0], given["v_" + n][0], "adamw_" + n, False)]

    own_ff, p_ff = _wait_copies(x_ff, grad_x, "exchange_ff_wait")
    update("w_ff_in", own_ff[0], p_ff[0])
    update("w_ff_out", own_ff[1], p_ff[1])
    own_mix, p_mix = _wait_copies(x_mix, p_ff[0], "exchange_mix_wait")
    for n, own, parts in zip(("w_a", "w_b", "w_o"), own_mix, p_mix):
        update(n, own, parts)
    tail = jnp.concatenate([dg0.reshape(8, 128), jnp.tile(loss_part, (8, 1))], axis=0)
    (tail_all,) = _all_to_all([tail], [True], "exchange_tail")
    dg0_all = tail_all[:, :8]
    own_small, p_small = _wait_copies(x_small, tail_all, "exchange_small_wait")
    own_in, p_in = _wait_copies(x_in, p_small[0], "exchange_in_wait")
    update("w_in", own_in[0], p_in[0])
    packed = _sum_adamw_peers(me_arr, own_small[0], p_small[0], _pack_small({n: given[n] for n in SMALL}),
                              _pack_small({n: given["m_" + n] for n in SMALL}),
                              _pack_small({n: given["v_" + n] for n in SMALL}), "adamw_small", True)
    shapes = {n: given[n].shape for n in SMALL}
    unpacked = [_unpack_small(p, shapes) for p in packed]
    for n in SMALL:
        results[n] = [u[n] for u in unpacked]
    n = "norm_mix_pre"
    results[n] = [r.reshape(given[n].shape) for r in _sum_adamw(
        dg0_all, given[n].reshape(8, 128), given["m_" + n].reshape(8, 128), given["v_" + n].reshape(8, 128), "adamw_" + n)]

    loss = jnp.sum(tail_all[:, 8, 0])
    order = ("w_in", "ln_v_gain", "ln_v_bias", "w_spatial", "b_spatial", "sinks", "w_a", "w_b", "w_o", "norm_mix_pre",
             "norm_mix_post", "w_ff_in", "w_ff_out", "norm_ff_pre", "norm_ff_post")
    out = [loss, grad_x.reshape(x.shape)]
    for k in range(4):
        out += [results[n][k] for n in order]
    return tuple(out)
```

```python
import functools

import jax
import jax.numpy as jnp
from jax import lax
from jax.experimental import pallas as pl
from jax.experimental.pallas import tpu as pltpu

F32 = jnp.float32
BF16 = jnp.bfloat16

N_DEV = 8
D = 1024
D_FF = 4096
IN_W = 5632
CHUNK = 128
GROUPS = 8
HEAD = 64
N_Q = 16
N_KV = 4
ROPE = 16
ROPE_THETA = 500000.0
EPS = 1e-6
OFF_Q, OFF_K, OFF_VA, OFF_GA, OFF_GB = 2048, 3072, 3328, 3584, 4608

ADAM_LR = 0.001
ADAM_B1 = 0.9
ADAM_B2 = 0.999
ADAM_EPS = 1e-08
ADAM_WD = 0.01
ADAM_STEP = 10

VMEM_LIMIT = 56 * 1024 * 1024

SDS = jax.ShapeDtypeStruct
MESH = pl.DeviceIdType.MESH


def _params(n_axes=None):
    if n_axes is None:
        return pltpu.CompilerParams(vmem_limit_bytes=VMEM_LIMIT)
    return pltpu.CompilerParams(dimension_semantics=("arbitrary",) * n_axes, vmem_limit_bytes=VMEM_LIMIT)


def _nt(a, b):
    return lax.dot_general(a, b, (((1,), (1,)), ((), ())), preferred_element_type=F32)


def _tn(a, b):
    return lax.dot_general(a, b, (((0,), (0,)), ((), ())), preferred_element_type=F32)


def _nn(a, b):
    return jnp.dot(a, b, preferred_element_type=F32)


def _gelu(x):
    t = jnp.tanh(0.7978845608028654 * (x + 0.044715 * (x * x * x)))
    return 0.5 * x * (1.0 + t), t


def _gelu_grad(x, t):
    return 0.5 * (1.0 + t) + 0.5 * x * (1.0 - t * t) * (0.7978845608028654 * (1.0 + 3.0 * 0.044715 * x * x))


def _sigmoid(x):
    return 1.0 / (1.0 + jnp.exp(-x))


def _rms_stats(v):
    r = lax.rsqrt(jnp.mean(v * v, axis=-1, keepdims=True) + EPS)
    return r, v * r


def _rms_bwd(d, vhat, r, g):
    gd = g * d
    return r * (gd - vhat * jnp.mean(gd * vhat, axis=-1, keepdims=True))


def _colsum(v):
    return jnp.sum(v, axis=0, keepdims=True)


_ANY = pl.BlockSpec(memory_space=pl.ANY)


def _after(body, n_in, after):
    if after is None:
        return body, [], []

    def ordered(*refs):
        return body(*refs[:n_in], *refs[n_in + 1:])

    return ordered, [_ANY], [after]


def _fwd_in(x, g0, win_t):
    T = x.shape[0]
    tm, tn = min(T, 1024), 1408

    def body(x_ref, g_ref, w_ref, p_ref, h_ref):
        @pl.when(pl.program_id(1) == 0)
        def _():
            _, xh = _rms_stats(x_ref[...])
            h_ref[...] = (xh * g_ref[...]).astype(BF16)

        p_ref[...] = _nt(h_ref[...], w_ref[...]).astype(BF16)

    return pl.pallas_call(
        body, name="fwd_in", grid=(T // tm, IN_W // tn),
        in_specs=[pl.BlockSpec((tm, D), lambda i, j: (i, 0)), pl.BlockSpec((1, D), lambda i, j: (0, 0)),
                  pl.BlockSpec((tn, D), lambda i, j: (j, 0))],
        out_specs=[pl.BlockSpec((tm, tn), lambda i, j: (i, j)), pl.BlockSpec((tm, D), lambda i, j: (i, 0))],
        out_shape=[SDS((T, IN_W), BF16), SDS((T, D), BF16)],
        compiler_params=_params(2),
    )(x, g0, win_t)


def _sgu_forward_parts(u_ref, vs_ref, lng_ref, lnb_ref):
    u = u_ref[...].astype(F32)
    vs = vs_ref[...].astype(F32)
    gu, tu = _gelu(u)
    gv, tv = _gelu(vs)
    mu = jnp.mean(gv, axis=-1, keepdims=True)
    dv = gv - mu
    rstd = lax.rsqrt(jnp.mean(dv * dv, axis=-1, keepdims=True) + EPS)
    vhat = dv * rstd
    vn = (vhat * lng_ref[...] + lnb_ref[...]).astype(BF16)
    return u, vs, gu, tu, tv, rstd, vhat, vn


def _masked_ws(ws_ref, g):
    row = lax.broadcasted_iota(jnp.int32, (CHUNK, CHUNK), 0)
    col = lax.broadcasted_iota(jnp.int32, (CHUNK, CHUNK), 1)
    return jnp.where(row >= col, ws_ref[g], 0.0).astype(BF16)


def _fwd_sgu(proj, lng, lnb, ws, bst):
    T = proj.shape[0]
    tc = min(T, 512)

    def body(u_ref, vs_ref, lng_ref, lnb_ref, ws_ref, bst_ref, a_ref):
        _, _, gu, _, _, _, _, vn = _sgu_forward_parts(u_ref, vs_ref, lng_ref, lnb_ref)
        for g in range(GROUPS):
            wm = _masked_ws(ws_ref, g)
            cols = slice(g * CHUNK, (g + 1) * CHUNK)
            for c in range(tc // CHUNK):
                rows = slice(c * CHUNK, (c + 1) * CHUNK)
                mixed = _nn(wm, vn[rows, cols]) + bst_ref[:, g:g + 1]
                a_ref[rows, cols] = (gu[rows, cols] * mixed).astype(BF16)

    return pl.pallas_call(
        body, name="fwd_sgu", grid=(T // tc,),
        in_specs=[pl.BlockSpec((tc, D), lambda i: (i, 0)), pl.BlockSpec((tc, D), lambda i: (i, 1)),
                  pl.BlockSpec((1, D), lambda i: (0, 0)), pl.BlockSpec((1, D), lambda i: (0, 0)),
                  pl.BlockSpec((GROUPS, CHUNK, CHUNK), lambda i: (0, 0, 0)), pl.BlockSpec((CHUNK, GROUPS), lambda i: (0, 0))],
        out_specs=pl.BlockSpec((tc, D), lambda i: (i, 0)),
        out_shape=SDS((T, D), BF16),
        compiler_params=_params(1),
    )(proj, proj, lng, lnb, ws, bst)


def _rope_tables(posf, invf, sgn):
    T = posf.shape[0]
    tr = min(T, 1024)

    def body(pos_ref, invf_ref, sgn_ref, c_ref, s_ref):
        ang = pos_ref[...] * invf_ref[...]
        c_ref[...] = jnp.cos(ang)
        s = jnp.sin(ang)
        s_ref[:, :128] = jnp.where(sgn_ref[...] < 0.0, -s, 0.0)
        s_ref[:, 128:] = jnp.where(sgn_ref[...] > 0.0, s, 0.0)

    return pl.pallas_call(
        body, name="rope_tables", grid=(T // tr,),
        in_specs=[pl.BlockSpec((tr, 1), lambda i: (i, 0)), pl.BlockSpec((1, 128), lambda i: (0, 0)),
                  pl.BlockSpec((1, 128), lambda i: (0, 0))],
        out_specs=[pl.BlockSpec((tr, 128), lambda i: (i, 0)), pl.BlockSpec((tr, 256), lambda i: (i, 0))],
        out_shape=[SDS((T, 128), F32), SDS((T, 256), F32)],
        compiler_params=_params(1),
    )(posf, invf, sgn)


def _rope(v, c, s_lo, s_hi):
    n = v.shape[1]
    return v * c + pltpu.roll(v, n - ROPE // 2, 1) * s_lo + pltpu.roll(v, ROPE // 2, 1) * s_hi


def _rope_bwd(dv, c, s_lo, s_hi):
    n = dv.shape[1]
    return dv * c + pltpu.roll(dv * s_lo, ROPE // 2, 1) + pltpu.roll(dv * s_hi, n - ROPE // 2, 1)


def _fold_masks(first):
    jj = lax.broadcasted_iota(jnp.int32, (CHUNK, CHUNK), 0)
    t = lax.broadcasted_iota(jnp.int32, (CHUNK, CHUNK), 1)
    prev = jj > t
    return prev, jnp.where(prev & first, -1e30, 0.0)


def _fold(band, prev):
    return jnp.where(prev, band[:CHUNK], band[CHUNK:])


def _unfold(folded, prev):
    return jnp.concatenate([jnp.where(prev, folded, 0.0), jnp.where(prev, 0.0, folded)], axis=0)


def _softmax_sink(s, sink, key_axis):
    m = jnp.maximum(jnp.max(s, axis=key_axis, keepdims=True), sink)
    p = jnp.exp(s - m)
    esink = jnp.exp(sink - m)
    inv = 1.0 / (jnp.sum(p, axis=key_axis, keepdims=True) + esink)
    return p * inv, esink * inv


def _head_pair_operand(band, g):
    slab = band[:, (g // 2) * 128:(g // 2 + 1) * 128]
    lo = lax.broadcasted_iota(jnp.int32, slab.shape, 1) < HEAD
    if g % 2 == 0:
        first = jnp.where(lo, slab, 0.0)
        second = pltpu.roll(first, HEAD, 1)
    else:
        second = jnp.where(lo, 0.0, slab)
        first = pltpu.roll(second, HEAD, 1)
    return jnp.concatenate([first, second], axis=0).astype(BF16)


def _head_pair_gradient(acc, g):
    top, bot = acc[:2 * CHUNK], acc[2 * CHUNK:]
    lo = lax.broadcasted_iota(jnp.int32, top.shape, 1) < HEAD
    if g % 2 == 0:
        return jnp.where(lo, top, 0.0) + pltpu.roll(jnp.where(lo, 0.0, bot), HEAD, 1)
    return pltpu.roll(jnp.where(lo, top, 0.0), HEAD, 1) + jnp.where(lo, 0.0, bot)


def _attn_specs(nb, clamp):
    cur = (lambda i: jnp.minimum(i, nb - 1)) if clamp else (lambda i: i)
    prev = lambda i: jnp.maximum(jnp.minimum(i, nb - 1) - 1, 0)
    kw = N_KV * HEAD
    return cur, prev, [
        pl.BlockSpec((CHUNK, D), lambda i: (cur(i), OFF_Q // D)),
        pl.BlockSpec((CHUNK, kw), lambda i: (prev(i), OFF_K // kw)),
        pl.BlockSpec((CHUNK, kw), lambda i: (cur(i), OFF_K // kw)),
        pl.BlockSpec((CHUNK, kw), lambda i: (prev(i), OFF_VA // kw)),
        pl.BlockSpec((CHUNK, kw), lambda i: (cur(i), OFF_VA // kw)),
        pl.BlockSpec((CHUNK, 128), lambda i: (prev(i), 0)),
        pl.BlockSpec((CHUNK, 128), lambda i: (cur(i), 0)),
        pl.BlockSpec((CHUNK, 256), lambda i: (prev(i), 0)),
        pl.BlockSpec((CHUNK, 256), lambda i: (cur(i), 0)),
        pl.BlockSpec(memory_space=pltpu.SMEM),
    ]


def _attn_load(q_ref, kp_ref, kc_ref, vp_ref, vc_ref, cp_ref, cc_ref, sp_ref, sc_ref):
    q_tab = [jnp.tile(t, (1, D // 128)) for t in (cc_ref[...], sc_ref[:, :128], sc_ref[:, 128:])]
    band = lambda p_ref, c_ref, cols: jnp.concatenate([p_ref[:, cols], c_ref[:, cols]], axis=0)
    k_tab = [jnp.tile(t, (1, N_KV * HEAD // 128)) for t in (
        band(cp_ref, cc_ref, slice(0, 128)), band(sp_ref, sc_ref, slice(0, 128)), band(sp_ref, sc_ref, slice(128, 256)))]
    q = _rope(q_ref[...].astype(F32), *q_tab).astype(BF16)
    kb = _rope(jnp.concatenate([kp_ref[...], kc_ref[...]], axis=0).astype(F32), *k_tab)
    vb = jnp.concatenate([vp_ref[...], vc_ref[...]], axis=0).astype(F32)
    return q, kb, vb, (q_tab, k_tab)


PAIRS_PER_KV = N_Q // N_KV // 2


def _fwd_attn(proj, cos, sin, sinks):
    T = proj.shape[0]
    nb = T // CHUNK
    _, _, specs = _attn_specs(nb, False)

    def body(q_ref, kp_ref, kc_ref, vp_ref, vc_ref, cp_ref, cc_ref, sp_ref, sc_ref, sink_ref, o_ref):
        q, kb, vb, _ = _attn_load(q_ref, kp_ref, kc_ref, vp_ref, vc_ref, cp_ref, cc_ref, sp_ref, sc_ref)
        prev, bias = _fold_masks(pl.program_id(0) == 0)
        for g in range(N_KV):
            k2 = _head_pair_operand(kb, g)
            v2 = _head_pair_operand(vb, g)
            for r in range(PAIRS_PER_KV):
                pair = g * PAIRS_PER_KV + r
                s2 = _nt(k2, q[:, pair * 128:(pair + 1) * 128]) * (HEAD ** -0.5)
                ps = []
                for e in range(2):
                    s = _fold(s2[e * 2 * CHUNK:(e + 1) * 2 * CHUNK], prev) + bias
                    ps.append(_unfold(_softmax_sink(s, sink_ref[2 * pair + e], 0)[0], prev).astype(BF16))
                o_ref[:, pair * 128:(pair + 1) * 128] = _tn(jnp.concatenate(ps, axis=0), v2).astype(BF16)

    return pl.pallas_call(
        body, name="fwd_attn", grid=(nb,), in_specs=specs,
        out_specs=pl.BlockSpec((CHUNK, D), lambda i: (i, 0)),
        out_shape=SDS((T, D), BF16),
        compiler_params=_params(1),
    )(proj, proj, proj, proj, proj, cos, cos, sin, sin, sinks)


def _fwd_mix(a, att, proj, x, wa, wb, wo, g1, g2):
    T = x.shape[0]
    tm = min(T, 512)
    half = D // 2

    def body(a_ref, att_ref, ga0, ga1, gb0, gb1, x_ref, wa_ref, wb_ref, wo_ref, g1_ref, g2_ref,
             mg_ref, a2_ref, b2_ref, mix_ref, x1_ref, hf_ref):
        a2 = _nn(a_ref[...], wa_ref[...])
        b2 = _nn(att_ref[...], wb_ref[...])
        ga = jnp.concatenate([ga0[...], ga1[...]], axis=1).astype(F32)
        gb = jnp.concatenate([gb0[...], gb1[...]], axis=1).astype(F32)
        merged = (_sigmoid(ga) * a2 + _sigmoid(gb) * b2).astype(BF16)
        a2_ref[...] = a2.astype(BF16)
        b2_ref[...] = b2.astype(BF16)
        mg_ref[...] = merged
        mix = _nn(merged, wo_ref[...])
        mix_ref[...] = mix
        _, mh = _rms_stats(mix)
        x1 = x_ref[...] + mh * g1_ref[...]
        x1_ref[...] = x1
        _, xh = _rms_stats(x1)
        hf_ref[...] = (xh * g2_ref[...]).astype(BF16)

    row = lambda i: (i, 0)
    const = lambda i: (0, 0)
    gspec = lambda off: pl.BlockSpec((tm, half), lambda i: (i, off // half))
    return pl.pallas_call(
        body, name="fwd_mix", grid=(T // tm,),
        in_specs=[pl.BlockSpec((tm, D), row), pl.BlockSpec((tm, D), row),
                  gspec(OFF_GA), gspec(OFF_GA + half), gspec(OFF_GB), gspec(OFF_GB + half),
                  pl.BlockSpec((tm, D), row), _resident((D, D)), _resident((D, D)),
                  _resident((D, D)), pl.BlockSpec((1, D), const), pl.BlockSpec((1, D), const)],
        out_specs=[pl.BlockSpec((tm, D), row)] * 6,
        out_shape=[SDS((T, D), BF16), SDS((T, D), BF16), SDS((T, D), BF16), SDS((T, D), F32), SDS((T, D), F32),
                   SDS((T, D), BF16)],
        compiler_params=_params(1),
    )(a, att, proj, proj, proj, proj, x, wa, wb, wo, g1, g2)


FF_SPLIT = N_DEV
FF_TILE = D_FF // FF_SPLIT
FF_STEP = 2048
FF_SLABS = FF_STEP // FF_TILE
FF_STEPS = D_FF // FF_STEP


def _fwd_ff(hf, wfi3, wfo, x1, tgt, g3):
    T = hf.shape[0]
    tm = min(T, 512)
    last = FF_STEPS - 1

    def body(hf_ref, wfi_ref, wfo_ref, x1_ref, tgt_ref, g3_ref, f_ref, dy_ref, dff_ref, dg3_ref, loss_ref, acc, r_s):
        i, p = pl.program_id(0), pl.program_id(1)

        @pl.when((i == 0) & (p == 0))
        def _():
            dg3_ref[...] = jnp.zeros_like(dg3_ref)
            loss_ref[...] = jnp.zeros_like(loss_ref)

        hf_t = hf_ref[...]
        for s in range(FF_SLABS):
            cols = slice(s * FF_TILE, (s + 1) * FF_TILE)
            f = _nn(hf_t, wfi_ref[s]).astype(BF16)
            f_ref[:, cols] = f
            rl = jnp.maximum(f.astype(F32), 0.0)
            r_s[:, cols] = (rl * rl).astype(BF16)
        part = _nn(r_s[...], wfo_ref[...])

        @pl.when(p == 0)
        def _():
            acc[...] = part

        @pl.when(p > 0)
        def _():
            acc[...] += part

        @pl.when(p == last)
        def _():
            r3, fh = _rms_stats(acc[...])
            e = x1_ref[...] + fh * g3_ref[...] - tgt_ref[...]
            loss_ref[...] += jnp.sum(e * e) * (0.5 / D)
            dy = e * (1.0 / D)
            dy_ref[...] = dy
            dg3_ref[...] += _colsum(dy * fh)
            dff_ref[...] = _rms_bwd(dy, fh, r3, g3_ref[...]).astype(BF16)

    row = lambda i, p: (i, 0)
    const = lambda i, p: (0, 0)
    return pl.pallas_call(
        body, name="fwd_ff", grid=(T // tm, FF_STEPS),
        in_specs=[pl.BlockSpec((tm, D), row), pl.BlockSpec((FF_SLABS, D, FF_TILE), lambda i, p: (p, 0, 0)),
                  pl.BlockSpec((FF_STEP, D), lambda i, p: (p, 0)), pl.BlockSpec((tm, D), row),
                  pl.BlockSpec((tm, D), row), pl.BlockSpec((1, D), const)],
        out_specs=[pl.BlockSpec((tm, FF_STEP), lambda i, p: (i, p)), pl.BlockSpec((tm, D), row),
                   pl.BlockSpec((tm, D), row), pl.BlockSpec((1, D), const), pl.BlockSpec((1, 128), const)],
        out_shape=[SDS((T, D_FF), BF16), SDS((T, D), F32), SDS((T, D), BF16), SDS((1, D), F32), SDS((1, 128), F32)],
        scratch_shapes=[pltpu.VMEM((tm, D), F32), pltpu.VMEM((tm, FF_STEP), BF16)],
        compiler_params=_params(2),
    )(hf, wfi3, wfo, x1, tgt, g3)


def _bwd_ff(dff, f, wfi3, wfo, x1, dy, mix, g1, g2):
    T = dff.shape[0]
    tm = min(T, 512)
    last = FF_STEPS - 1

    def body(dff_ref, f_ref, wfi_ref, wfo_ref, x1_ref, dy_ref, mix_ref, g1_ref, g2_ref,
             df_ref, dx1_ref, dmix_ref, dg2_ref, dg1_ref, acc):
        i, p = pl.program_id(0), pl.program_id(1)

        @pl.when((i == 0) & (p == 0))
        def _():
            dg2_ref[...] = jnp.zeros_like(dg2_ref)
            dg1_ref[...] = jnp.zeros_like(dg1_ref)

        dr = _nt(dff_ref[...], wfo_ref[...])
        df_ref[...] = (dr * (2.0 * jnp.maximum(f_ref[...].astype(F32), 0.0))).astype(BF16)
        part = _nt(df_ref[:, :FF_TILE], wfi_ref[0])
        for s in range(1, FF_SLABS):
            part = part + _nt(df_ref[:, s * FF_TILE:(s + 1) * FF_TILE], wfi_ref[s])

        @pl.when(p == 0)
        def _():
            acc[...] = part

        @pl.when(p > 0)
        def _():
            acc[...] += part

        @pl.when(p == last)
        def _():
            dhf = acc[...]
            r2, xh = _rms_stats(x1_ref[...])
            dg2_ref[...] += _colsum(dhf * xh)
            dx1 = dy_ref[...] + _rms_bwd(dhf, xh, r2, g2_ref[...])
            dx1_ref[...] = dx1
            r1, mh = _rms_stats(mix_ref[...])
            dg1_ref[...] += _colsum(dx1 * mh)
            dmix_ref[...] = _rms_bwd(dx1, mh, r1, g1_ref[...]).astype(BF16)

    row = lambda i, p: (i, 0)
    const = lambda i, p: (0, 0)
    return pl.pallas_call(
        body, name="bwd_ff", grid=(T // tm, FF_STEPS),
        in_specs=[pl.BlockSpec((tm, D), row), pl.BlockSpec((tm, FF_STEP), lambda i, p: (i, p)),
                  pl.BlockSpec((FF_SLABS, D, FF_TILE), lambda i, p: (p, 0, 0)), pl.BlockSpec((FF_STEP, D), lambda i, p: (p, 0)),
                  pl.BlockSpec((tm, D), row), pl.BlockSpec((tm, D), row), pl.BlockSpec((tm, D), row),
                  pl.BlockSpec((1, D), const), pl.BlockSpec((1, D), const)],
        out_specs=[pl.BlockSpec((tm, FF_STEP), lambda i, p: (i, p)), pl.BlockSpec((tm, D), row),
                   pl.BlockSpec((tm, D), row), pl.BlockSpec((1, D), const), pl.BlockSpec((1, D), const)],
        out_shape=[SDS((T, D_FF), BF16), SDS((T, D), F32), SDS((T, D), BF16), SDS((1, D), F32), SDS((1, D), F32)],
        scratch_shapes=[pltpu.VMEM((tm, D), F32)],
        compiler_params=_params(2),
    )(dff, f, wfi3, wfo, x1, dy, mix, g1, g2)


def _wgrad_ff(hf, df, f, dff):
    T = hf.shape[0]
    tt = min(T, 1024)
    wide = 2 * FF_TILE

    def body(hf_ref, df_ref, f_ref, dff_ref, dwfi_ref, dwfo_ref, acc_i, acc_o):
        t = pl.program_id(1)

        @pl.when(t == 0)
        def _():
            acc_i[...] = jnp.zeros_like(acc_i)
            acc_o[...] = jnp.zeros_like(acc_o)

        acc_i[...] += _tn(hf_ref[...], df_ref[...])
        rl = jnp.maximum(f_ref[...].astype(F32), 0.0)
        acc_o[...] += _tn((rl * rl).astype(BF16), dff_ref[...])

        @pl.when(t == T // tt - 1)
        def _():
            dwfi_ref[0] = acc_i[:, :FF_TILE].astype(BF16)
            dwfi_ref[1] = acc_i[:, FF_TILE:].astype(BF16)
            dwfo_ref[...] = acc_o[...].astype(BF16)

    return pl.pallas_call(
        body, name="wgrad_ff", grid=(D_FF // wide, T // tt),
        in_specs=[pl.BlockSpec((tt, D), lambda p, t: (t, 0)), pl.BlockSpec((tt, wide), lambda p, t: (t, p)),
                  pl.BlockSpec((tt, wide), lambda p, t: (t, p)), pl.BlockSpec((tt, D), lambda p, t: (t, 0))],
        out_specs=[pl.BlockSpec((2, D, FF_TILE), lambda p, t: (p, 0, 0)), pl.BlockSpec((wide, D), lambda p, t: (p, 0))],
        out_shape=[SDS((FF_SPLIT, D, FF_TILE), BF16), SDS((D_FF, D), BF16)],
        scratch_shapes=[pltpu.VMEM((D, wide), F32), pltpu.VMEM((wide, D), F32)],
        compiler_params=_params(2),
    )(hf, df, f, dff)


def _bwd_mix(dmix, proj, a2, b2, wo, wa, wb, after=None):
    T = dmix.shape[0]
    tm = min(T, 512)
    half = D // 2

    def body(dmix_ref, ga0, ga1, gb0, gb1, a2_ref, b2_ref, wo_ref, wa_ref, wb_ref,
             da2_ref, db2_ref, dg_ref, da_ref, datt_ref):
        dmg = _nt(dmix_ref[...], wo_ref[...])
        sa = _sigmoid(jnp.concatenate([ga0[...], ga1[...]], axis=1).astype(F32))
        sb = _sigmoid(jnp.concatenate([gb0[...], gb1[...]], axis=1).astype(F32))
        da2 = (dmg * sa).astype(BF16)
        db2 = (dmg * sb).astype(BF16)
        da2_ref[...] = da2
        db2_ref[...] = db2
        dg_ref[:, :D] = (dmg * a2_ref[...].astype(F32) * (sa * (1.0 - sa))).astype(BF16)
        dg_ref[:, D:] = (dmg * b2_ref[...].astype(F32) * (sb * (1.0 - sb))).astype(BF16)
        da_ref[...] = _nt(da2, wa_ref[...]).astype(BF16)
        datt_ref[...] = _nt(db2, wb_ref[...]).astype(BF16)

    row = lambda i: (i, 0)
    const = lambda i: (0, 0)
    gspec = lambda off: pl.BlockSpec((tm, half), lambda i: (i, off // half))
    body, dep_specs, deps = _after(body, 10, after)
    return pl.pallas_call(
        body, name="bwd_mix", grid=(T // tm,),
        in_specs=[pl.BlockSpec((tm, D), row), gspec(OFF_GA), gspec(OFF_GA + half), gspec(OFF_GB), gspec(OFF_GB + half),
                  pl.BlockSpec((tm, D), row), pl.BlockSpec((tm, D), row),
                  _resident((D, D)), _resident((D, D)), _resident((D, D))] + dep_specs,
        out_specs=[pl.BlockSpec((tm, D), row), pl.BlockSpec((tm, D), row), pl.BlockSpec((tm, 2 * D), row),
                   pl.BlockSpec((tm, D), row), pl.BlockSpec((tm, D), row)],
        out_shape=[SDS((T, D), BF16), SDS((T, D), BF16), SDS((T, 2 * D), BF16), SDS((T, D), BF16), SDS((T, D), BF16)],
        compiler_params=_params(1),
    )(dmix, proj, proj, proj, proj, a2, b2, wo, wa, wb, *deps)


def _wgrad_mix(merged, dmix, a, da2, att, db2):
    T = merged.shape[0]
    tt = min(T, 512)

    def body(mg_ref, dmix_ref, a_ref, da2_ref, att_ref, db2_ref, dwo_ref, dwa_ref, dwb_ref, acc):
        t = pl.program_id(0)

        @pl.when(t == 0)
        def _():
            acc[...] = jnp.zeros_like(acc)

        acc[0] += _tn(mg_ref[...], dmix_ref[...])
        acc[1] += _tn(a_ref[...], da2_ref[...])
        acc[2] += _tn(att_ref[...], db2_ref[...])

        @pl.when(t == T // tt - 1)
        def _():
            dwo_ref[...] = acc[0].astype(BF16)
            dwa_ref[...] = acc[1].astype(BF16)
            dwb_ref[...] = acc[2].astype(BF16)

    return pl.pallas_call(
        body, name="wgrad_mix", grid=(T // tt,),
        in_specs=[pl.BlockSpec((tt, D), lambda t: (t, 0))] * 6,
        out_specs=[pl.BlockSpec((D, D), lambda t: (0, 0))] * 3,
        out_shape=[SDS((D, D), BF16)] * 3,
        scratch_shapes=[pltpu.VMEM((3, D, D), F32)],
        compiler_params=_params(1),
    )(merged, dmix, a, da2, att, db2)


def _bwd_attn(proj, cos, sin, sinks, datt, after=None):
    T = proj.shape[0]
    nb = T // CHUNK
    kw = N_KV * HEAD
    cur, prev, specs = _attn_specs(nb, True)

    def body(q_ref, kp_ref, kc_ref, vp_ref, vc_ref, cp_ref, cc_ref, sp_ref, sc_ref, sink_ref, do_ref,
             dq_ref, dkv_ref, dsink_ref, carry_k, carry_v, dq_acc):
        i = pl.program_id(0)

        @pl.when(i == 0)
        def _():
            carry_k[...] = jnp.zeros_like(carry_k)
            carry_v[...] = jnp.zeros_like(carry_v)
            dsink_ref[...] = jnp.zeros_like(dsink_ref)

        @pl.when(i < nb)
        def _():
            q, kb, vb, (q_tab, k_tab) = _attn_load(q_ref, kp_ref, kc_ref, vp_ref, vc_ref, cp_ref, cc_ref,
                                                   sp_ref, sc_ref)
            prev, bias = _fold_masks(i == 0)
            do = do_ref[...]
            lane = lax.broadcasted_iota(jnp.int32, (1, 128), 1)
            dsink = jnp.zeros((1, 128), F32)
            dks, dvs = [], []
            for g in range(N_KV):
                k2 = _head_pair_operand(kb, g)
                v2 = _head_pair_operand(vb, g)
                dk2 = jnp.zeros((4 * CHUNK, 128), F32)
                dv2 = jnp.zeros((4 * CHUNK, 128), F32)
                for r in range(PAIRS_PER_KV):
                    pair = g * PAIRS_PER_KV + r
                    qp = q[:, pair * 128:(pair + 1) * 128]
                    dop = do[:, pair * 128:(pair + 1) * 128]
                    s2 = _nt(k2, qp) * (HEAD ** -0.5)
                    dp2 = _nt(v2, dop)
                    ps, dss = [], []
                    for e in range(2):
                        rows = slice(e * 2 * CHUNK, (e + 1) * 2 * CHUNK)
                        p, psink = _softmax_sink(_fold(s2[rows], prev) + bias, sink_ref[2 * pair + e], 0)
                        dp = _fold(dp2[rows], prev)
                        delta = jnp.sum(p * dp, axis=0, keepdims=True)
                        ps.append(_unfold(p, prev).astype(BF16))
                        dss.append(_unfold(p * (dp - delta) * (HEAD ** -0.5), prev).astype(BF16))
                        dsink = dsink + jnp.where(lane == 2 * pair + e, -jnp.sum(psink * delta), 0.0)
                    ds2 = jnp.concatenate(dss, axis=0)
                    dq_acc[:, pair * 128:(pair + 1) * 128] = _tn(ds2, k2)
                    dk2 = dk2 + _nn(ds2, qp)
                    dv2 = dv2 + _nn(jnp.concatenate(ps, axis=0), dop)
                dks.append(_head_pair_gradient(dk2, g))
                dvs.append(_head_pair_gradient(dv2, g))
            dsink_ref[...] += dsink
            dq_ref[...] = _rope_bwd(dq_acc[...], *q_tab).astype(BF16)
            dkb = _rope_bwd(jnp.concatenate([dks[0] + dks[1], dks[2] + dks[3]], axis=1), *k_tab)
            dvb = jnp.concatenate([dvs[0] + dvs[1], dvs[2] + dvs[3]], axis=1)
            dkv_ref[:, :kw] = (carry_k[...] + dkb[:CHUNK]).astype(BF16)
            dkv_ref[:, kw:] = (carry_v[...] + dvb[:CHUNK]).astype(BF16)
            carry_k[...] = dkb[CHUNK:]
            carry_v[...] = dvb[CHUNK:]

        @pl.when(i == nb)
        def _():
            dkv_ref[:, :kw] = carry_k[...].astype(BF16)
            dkv_ref[:, kw:] = carry_v[...].astype(BF16)

    body, dep_specs, deps = _after(body, 11, after)
    return pl.pallas_call(
        body, name="bwd_attn", grid=(nb + 1,),
        in_specs=specs + [pl.BlockSpec((CHUNK, D), lambda i: (cur(i), 0))] + dep_specs,
        out_specs=[pl.BlockSpec((CHUNK, D), lambda i: (cur(i), 0)),
                   pl.BlockSpec((CHUNK, 2 * kw), lambda i: (jnp.maximum(i - 1, 0), 0)),
                   pl.BlockSpec((1, 128), lambda i: (0, 0))],
        out_shape=[SDS((T, D), BF16), SDS((T, 2 * kw), BF16), SDS((1, 128), F32)],
        scratch_shapes=[pltpu.VMEM((CHUNK, kw), F32), pltpu.VMEM((CHUNK, kw), F32), pltpu.VMEM((CHUNK, D), F32)],
        compiler_params=_params(1),
    )(proj, proj, proj, proj, proj, cos, cos, sin, sin, sinks, datt, *deps)


def _bwd_sgu(proj, da, lng, lnb, ws, bst):
    T = proj.shape[0]
    tc = min(T, 512)
    nsteps = T // tc

    def body(u_ref, vs_ref, da_ref, lng_ref, lnb_ref, ws_ref, bst_ref,
             duv_ref, dws_ref, dbs_ref, dlng_ref, dlnb_ref, dvn_s, dgu_s, dmx_sum):
        i = pl.program_id(0)

        @pl.when(i == 0)
        def _():
            dws_ref[...] = jnp.zeros_like(dws_ref)
            dlng_ref[...] = jnp.zeros_like(dlng_ref)
            dlnb_ref[...] = jnp.zeros_like(dlnb_ref)
            dmx_sum[...] = jnp.zeros_like(dmx_sum)

        u, vs, gu, tu, tv, rstd, vhat, vn = _sgu_forward_parts(u_ref, vs_ref, lng_ref, lnb_ref)
        da = da_ref[...].astype(F32)
        for g in range(GROUPS):
            wm = _masked_ws(ws_ref, g)
            cols = slice(g * CHUNK, (g + 1) * CHUNK)
            dws = jnp.zeros((CHUNK, CHUNK), F32)
            dsum = jnp.zeros((CHUNK, CHUNK), F32)
            for c in range(tc // CHUNK):
                rows = slice(c * CHUNK, (c + 1) * CHUNK)
                vn_cg = vn[rows, cols]
                mixed = _nn(wm, vn_cg) + bst_ref[:, g:g + 1]
                dgu_s[rows, cols] = da[rows, cols] * mixed
                dmx = da[rows, cols] * gu[rows, cols]
                dmxb = dmx.astype(BF16)
                dws = dws + _nt(dmxb, vn_cg)
                dsum = dsum + dmx
                dvn_s[rows, cols] = _tn(wm, dmxb)
            dws_ref[g] += dws
            dmx_sum[:, cols] += dsum
        dvn = dvn_s[...]
        dlng_ref[...] += _colsum(dvn * vhat)
        dlnb_ref[...] += _colsum(dvn)
        dvh = dvn * lng_ref[...]
        dgv = rstd * (dvh - jnp.mean(dvh, axis=-1, keepdims=True) - vhat * jnp.mean(dvh * vhat, axis=-1, keepdims=True))
        duv_ref[:, :D] = (dgu_s[...] * _gelu_grad(u, tu)).astype(BF16)
        duv_ref[:, D:] = (dgv * _gelu_grad(vs, tv)).astype(BF16)

        @pl.when(i == nsteps - 1)
        def _():
            row = lax.broadcasted_iota(jnp.int32, (CHUNK, CHUNK), 0)
            col = lax.broadcasted_iota(jnp.int32, (CHUNK, CHUNK), 1)
            for g in range(GROUPS):
                dws_ref[g] = jnp.where(row >= col, dws_ref[g], 0.0)
                dbs_ref[g:g + 1, :] = _colsum(dmx_sum[:, g * CHUNK:(g + 1) * CHUNK].T)

    const2 = lambda i: (0, 0)
    return pl.pallas_call(
        body, name="bwd_sgu", grid=(nsteps,),
        in_specs=[pl.BlockSpec((tc, D), lambda i: (i, 0)), pl.BlockSpec((tc, D), lambda i: (i, 1)),
                  pl.BlockSpec((tc, D), lambda i: (i, 0)), pl.BlockSpec((1, D), const2), pl.BlockSpec((1, D), const2),
                  pl.BlockSpec((GROUPS, CHUNK, CHUNK), lambda i: (0, 0, 0)), pl.BlockSpec((CHUNK, GROUPS), const2)],
        out_specs=[pl.BlockSpec((tc, 2 * D), lambda i: (i, 0)), pl.BlockSpec((GROUPS, CHUNK, CHUNK), lambda i: (0, 0, 0)),
                   pl.BlockSpec((GROUPS, CHUNK), const2), pl.BlockSpec((1, D), const2), pl.BlockSpec((1, D), const2)],
        out_shape=[SDS((T, 2 * D), BF16), SDS((GROUPS, CHUNK, CHUNK), F32), SDS((GROUPS, CHUNK), F32),
                   SDS((1, D), F32), SDS((1, D), F32)],
        scratch_shapes=[pltpu.VMEM((tc, D), F32), pltpu.VMEM((tc, D), F32), pltpu.VMEM((CHUNK, D), F32)],
        compiler_params=_params(1),
    )(proj, proj, da, lng, lnb, ws, bst)


IN_SEG_WIDTHS = (2 * D, D, 2 * N_KV * HEAD, 2 * D)


def _resident(shape):
    return pl.BlockSpec(shape, lambda *_: (0,) * len(shape), pipeline_mode=pl.Buffered(1))


def _bwd_in(duv, dq, dkv, dg, win_t, x, dx1, g0, after=None):
    T = x.shape[0]
    tm = min(T, 512)

    def body(duv_ref, dq_ref, dkv_ref, dg_ref, w_ref, x_ref, dx1_ref, g0_ref, gx_ref, dg0_ref):
        @pl.when(pl.program_id(0) == 0)
        def _():
            dg0_ref[...] = jnp.zeros_like(dg0_ref)

        dh, off = None, 0
        for ref, width in zip((duv_ref, dq_ref, dkv_ref, dg_ref), IN_SEG_WIDTHS):
            part = _nn(ref[...], w_ref[off:off + width, :])
            dh = part if dh is None else dh + part
            off += width
        r0, xh = _rms_stats(x_ref[...])
        dg0_ref[...] += _colsum(dh * xh)
        gx_ref[...] = dx1_ref[...] + _rms_bwd(dh, xh, r0, g0_ref[...])

    row = lambda i: (i, 0)
    body, dep_specs, deps = _after(body, 8, after)
    return pl.pallas_call(
        body, name="bwd_in", grid=(T // tm,),
        in_specs=[pl.BlockSpec((tm, w), row) for w in IN_SEG_WIDTHS] + [
            _resident((IN_W, D)), pl.BlockSpec((tm, D), row), pl.BlockSpec((tm, D), row),
            pl.BlockSpec((1, D), lambda i: (0, 0))] + dep_specs,
        out_specs=[pl.BlockSpec((tm, D), row), pl.BlockSpec((1, D), lambda i: (0, 0))],
        out_shape=[SDS((T, D), F32), SDS((1, D), F32)],
        compiler_params=_params(1),
    )(duv, dq, dkv, dg, win_t, x, dx1, g0, *deps)


def _wgrad_rows(h, segs, name):
    T = h.shape[0]
    tt = min(T, 1024)
    widths = [s.shape[1] for s in segs]

    def body(h_ref, *refs):
        dw_ref, acc = refs[-2], refs[-1]
        t = pl.program_id(0)

        @pl.when(t == 0)
        def _():
            acc[...] = jnp.zeros_like(acc)

        off = 0
        for ref, width in zip(refs[:-2], widths):
            acc[off:off + width, :] += _tn(ref[...], h_ref[...])
            off += width

        @pl.when(t == T // tt - 1)
        def _():
            dw_ref[...] = acc[...].astype(BF16)

    row = lambda t: (t, 0)
    return pl.pallas_call(
        body, name=name, grid=(T // tt,),
        in_specs=[pl.BlockSpec((tt, D), row)] + [pl.BlockSpec((tt, w), row) for w in widths],
        out_specs=pl.BlockSpec((sum(widths), D), lambda t: (0, 0)),
        out_shape=SDS((sum(widths), D), BF16),
        scratch_shapes=[pltpu.VMEM((sum(widths), D), F32)],
        compiler_params=_params(1),
    )(h, *segs)


def _wgrad_in(h, duv, dq, dkv, dg):
    return jnp.concatenate([_wgrad_rows(h, [duv], "wgrad_in_uv"), _wgrad_rows(h, [dq, dkv], "wgrad_in_qkv"),
                            _wgrad_rows(h, [dg], "wgrad_in_gates")], axis=0)


def _place():
    x, y, c = lax.axis_index("x"), lax.axis_index("y"), lax.axis_index("c")
    return x, y, c, 4 * x + 2 * y + c


def _peers(x, y, c):
    out = []
    for mask in range(1, N_DEV):
        px = 1 - x if mask & 4 else x
        py = 1 - y if mask & 2 else y
        pc = 1 - c if mask & 1 else c
        out.append(((px, py, pc), 4 * px + 2 * py + pc))
    return out


def _all_to_all(arrays, gather, name, after=None):
    n = len(arrays)

    def body(*refs):
        ins, outs = refs[:n], refs[n:2 * n]
        send_sems, recv_sems, local_sems = refs[2 * n:]
        x, y, c, me = _place()
        local, sends, recvs = [], [], []
        for a in range(n):
            src_own = ins[a] if gather[a] else ins[a].at[me]
            local.append(pltpu.make_async_copy(src_own, outs[a].at[me], local_sems.at[a]))
            for k, (peer, pid) in enumerate(_peers(x, y, c)):
                sem = a * (N_DEV - 1) + k
                src = ins[a] if gather[a] else ins[a].at[pid]
                sends.append(pltpu.make_async_remote_copy(
                    src_ref=src, dst_ref=outs[a].at[me], send_sem=send_sems.at[sem], recv_sem=recv_sems.at[sem],
                    device_id=peer, device_id_type=MESH))
                recvs.append(pltpu.make_async_remote_copy(
                    src_ref=src, dst_ref=outs[a].at[pid], send_sem=send_sems.at[sem], recv_sem=recv_sems.at[sem],
                    device_id=peer, device_id_type=MESH))
        for cp in local + sends:
            cp.start()
        for cp in recvs:
            cp.wait_recv()
        for cp in sends:
            cp.wait_send()
        for cp in local:
            cp.wait()

    out_shape = [SDS((N_DEV,) + a.shape if gt else a.shape, a.dtype) for a, gt in zip(arrays, gather)]
    nsem = n * (N_DEV - 1)
    body, dep_specs, deps = _after(body, n, after)
    return pl.pallas_call(
        body, name=name,
        in_specs=[pl.BlockSpec(memory_space=pl.ANY)] * n + dep_specs,
        out_specs=[pl.BlockSpec(memory_space=pl.ANY)] * n,
        out_shape=out_shape,
        scratch_shapes=[pltpu.SemaphoreType.DMA((nsem,)), pltpu.SemaphoreType.DMA((nsem,)), pltpu.SemaphoreType.DMA((n,))],
    )(*arrays, *deps)


def _gather_two_level(shard, name):
    def body(x_ref, out_ref, send_sems, recv_sems, local_sem):
        x, y, c = lax.axis_index("x"), lax.axis_index("y"), lax.axis_index("c")
        me, sibling = (x, y, c), (x, y, 1 - c)
        chips = [(1 - x, y), (x, 1 - y), (1 - x, 1 - y)]

        def slot(px, py, pc):
            return out_ref.at[4 * px + 2 * py + pc]

        def copy(k, block, to, src=None):
            return pltpu.make_async_remote_copy(
                src_ref=slot(*block) if src is None else src, dst_ref=slot(*block),
                send_sem=send_sems.at[k], recv_sem=recv_sems.at[k], device_id=to, device_id_type=MESH)

        mine = pltpu.make_async_copy(x_ref, slot(*me), local_sem)
        mine.start()
        first = [copy(0, me, sibling, src=x_ref)]
        first += [copy(1 + j, me, (*chip, c), src=x_ref) for j, chip in enumerate(chips)]
        for cp in first:
            cp.start()
        passed = [copy(4 + j, (*chip, c), sibling) for j, chip in enumerate(chips)]
        for j, chip in enumerate(chips):
            copy(1 + j, (*chip, c), me).wait_recv()
            passed[j].start()
        copy(0, sibling, me).wait_recv()
        for j, chip in enumerate(chips):
            copy(4 + j, (*chip, 1 - c), me).wait_recv()
        for cp in first + passed:
            cp.wait_send()
        mine.wait()

    return pl.pallas_call(
        body, name=name,
        in_specs=[_ANY], out_specs=_ANY,
        out_shape=SDS((N_DEV,) + shard.shape, shard.dtype),
        scratch_shapes=[pltpu.SemaphoreType.DMA((N_DEV - 1,)), pltpu.SemaphoreType.DMA((N_DEV - 1,)),
                        pltpu.SemaphoreType.DMA],
    )(shard)


_HBM = pl.BlockSpec(memory_space=pltpu.HBM)
_SEM = pl.BlockSpec(memory_space=pltpu.SEMAPHORE)
_EFFECT = pltpu.SideEffectType.DATAFLOW_SIDE_EFFECTING
GATHER = "gather"
SCATTER = "scatter"
SPREAD = "spread"


def _zone_shape(a, mode):
    if mode == GATHER:
        return (N_DEV,) + a.shape
    return (N_DEV - 1,) + (a.shape[1:] if mode == SCATTER else a.shape)


def _start_copies(arrays, modes, name, after=None):
    n = len(arrays)
    zones = [lax.empty(_zone_shape(a, m), a.dtype) for a, m in zip(arrays, modes)]

    def body(*refs):
        ins, lands = refs[:n], refs[n:2 * n]
        send_sems, recv_sems = refs[-2 * n - 3], refs[-2 * n - 2]
        token = refs[-1]
        x, y, c, me = _place()
        for a in range(n):
            for k, (peer, pid) in enumerate(_peers(x, y, c)):
                src = ins[a].at[pid] if modes[a] == SCATTER else ins[a]
                dst = lands[a].at[me] if modes[a] == GATHER else lands[a].at[k]
                pltpu.make_async_remote_copy(src_ref=src, dst_ref=dst, send_sem=send_sems.at[a], recv_sem=recv_sems.at[a],
                                             device_id=peer, device_id_type=MESH).start()
        token[...] = jnp.zeros_like(token)

    hbm = lambda a: pltpu.HBM(a.shape, a.dtype)
    sems = pltpu.SemaphoreType.DMA((n,))
    extra = [] if after is None else [after]
    operands = [pltpu.with_memory_space_constraint(a, pltpu.HBM) for a in list(arrays) + zones]
    res = pl.pallas_call(
        body, name=name,
        out_shape=(sems, sems, *[hbm(a) for a in arrays], *[hbm(z) for z in zones], SDS((8, 128), F32)),
        in_specs=[_HBM] * (2 * n) + [_ANY] * len(extra),
        out_specs=(_SEM, _SEM, *[_HBM] * (2 * n), pl.BlockSpec(memory_space=pltpu.VMEM)),
        input_output_aliases={i: 2 + i for i in range(2 * n)},
        compiler_params=pltpu.CompilerParams(has_side_effects=_EFFECT),
    )(*operands, *extra)
    return res[0], res[1], list(res[2:2 + n]), list(res[2 + n:2 + 2 * n]), res[-1]


def _wait_copies(started, after, name):
    send_sems, recv_sems, thru, zones, _ = started
    n = len(thru)

    def body(*refs):
        lands = refs[n:2 * n]
        send_ref, recv_ref = refs[2 * n], refs[2 * n + 1]
        x, y, c, _ = _place()
        for a in range(n):
            seven = lands[a].at[pl.ds(0, N_DEV - 1)]
            cp = pltpu.make_async_remote_copy(src_ref=seven, dst_ref=seven, send_sem=send_ref.at[a], recv_sem=recv_ref.at[a],
                                              device_id=(x, y, 1 - c), device_id_type=MESH)
            cp.wait_send()
            cp.wait_recv()

    hbm = lambda a: pltpu.HBM(a.shape, a.dtype)
    res = pl.pallas_call(
        body, name=name,
        out_shape=tuple(hbm(a) for a in thru + zones),
        in_specs=[_HBM] * (2 * n) + [_SEM, _SEM, _ANY],
        out_specs=tuple([_HBM] * (2 * n)),
        input_output_aliases={i: i for i in range(2 * n)},
        compiler_params=pltpu.CompilerParams(has_side_effects=_EFFECT),
    )(*thru, *zones, send_sems, recv_sems, after)
    return list(res[:n]), list(res[n:])


def _adamw_math(g, w, m, v):
    m2 = ADAM_B1 * m + (1.0 - ADAM_B1) * g
    v2 = ADAM_B2 * v + (1.0 - ADAM_B2) * (g * g)
    m_hat = m2 / (1.0 - ADAM_B1 ** ADAM_STEP)
    v_hat = v2 / (1.0 - ADAM_B2 ** ADAM_STEP)
    delta = -ADAM_LR * (m_hat / (jnp.sqrt(v_hat) + ADAM_EPS) + ADAM_WD * w)
    return delta, m2, v2


def _sum_adamw(parts, w, m, v, name):
    R, C = w.shape
    tr = max(t for t in (128, 64, 32, 16, 8) if R % t == 0)

    def body(p_ref, w_ref, m_ref, v_ref, g_ref, d_ref, m2_ref, v2_ref):
        g = p_ref[0]
        for k in range(1, N_DEV):
            g = g + p_ref[k]
        g_ref[...] = g
        d_ref[...], m2_ref[...], v2_ref[...] = _adamw_math(g, w_ref[...], m_ref[...], v_ref[...])

    blk = pl.BlockSpec((tr, C), lambda i: (i, 0))
    return pl.pallas_call(
        body, name=name, grid=(R // tr,),
        in_specs=[pl.BlockSpec((N_DEV, tr, C), lambda i: (0, i, 0)), blk, blk, blk],
        out_specs=[blk] * 4,
        out_shape=[SDS((R, C), F32)] * 4,
        compiler_params=_params(1),
    )(parts, w, m, v)


def _sum_adamw_peers(me, own, parts, w, m, v, name, replicated):
    R, C = w.shape
    tr = max(t for t in (128, 64, 32, 16, 8) if R % t == 0)

    def body(me_ref, own_ref, p_ref, w_ref, m_ref, v_ref, g_ref, d_ref, m2_ref, v2_ref):
        if replicated:
            mine = me_ref[0]
            g = None
            for j in range(N_DEV):
                k = jnp.maximum(jnp.bitwise_xor(mine, j) - 1, 0)
                term = jnp.where(mine == j, own_ref[...], p_ref[k])
                g = term if g is None else g + term
        else:
            g = own_ref[...].astype(F32)
            for k in range(N_DEV - 1):
                g = g + p_ref[k].astype(F32)
        g_ref[...] = g
        d_ref[...], m2_ref[...], v2_ref[...] = _adamw_math(g, w_ref[...], m_ref[...], v_ref[...])

    blk = pl.BlockSpec((tr, C), lambda i, me_ref: (i, 0))
    own_spec = blk if replicated else pl.BlockSpec((None, tr, C), lambda i, me_ref: (me_ref[0], i, 0))
    return pl.pallas_call(
        body, name=name,
        grid_spec=pltpu.PrefetchScalarGridSpec(
            num_scalar_prefetch=1, grid=(R // tr,),
            in_specs=[own_spec, pl.BlockSpec((N_DEV - 1, tr, C), lambda i, me_ref: (0, i, 0)), blk, blk, blk],
            out_specs=[blk] * 4),
        out_shape=[SDS((R, C), F32)] * 4,
        compiler_params=_params(1),
    )(me, own, parts, w, m, v)


SMALL = ("ln_v_gain", "ln_v_bias", "w_spatial", "b_spatial", "sinks", "norm_mix_post", "norm_ff_pre", "norm_ff_post")
SMALL_ROWS = {"ln_v_gain": 8, "ln_v_bias": 8, "w_spatial": 1024, "b_spatial": 8, "sinks": 8,
              "norm_mix_post": 8, "norm_ff_pre": 8, "norm_ff_post": 8}
SMALL_PACK_ROWS = 1152


def _pack_small(vals):
    rows = []
    for name in SMALL:
        flat = vals[name].reshape(-1)
        pad = SMALL_ROWS[name] * 128 - flat.shape[0]
        if pad:
            flat = jnp.concatenate([flat, jnp.zeros((pad,), F32)])
        rows.append(flat.reshape(SMALL_ROWS[name], 128))
    rows.append(jnp.zeros((SMALL_PACK_ROWS - sum(SMALL_ROWS.values()), 128), F32))
    return jnp.concatenate(rows, axis=0)


def _unpack_small(packed, shapes):
    out, r = {}, 0
    for name in SMALL:
        n = 1
        for s in shapes[name]:
            n *= s
        out[name] = packed[r:r + SMALL_ROWS[name]].reshape(-1)[:n].reshape(shapes[name])
        r += SMALL_ROWS[name]
    return out


def _rope_rows():
    d = jnp.arange(128) % HEAD
    inv = ROPE_THETA ** (-(2.0 * (d % (ROPE // 2))).astype(F32) / ROPE)
    invf = jnp.where(d < ROPE, inv, 0.0).astype(F32).reshape(1, 128)
    sgn = jnp.where(d < ROPE // 2, -1.0, jnp.where(d < ROPE, 1.0, 0.0)).astype(F32).reshape(1, 128)
    return invf, sgn


def kernel(x, positions, w_in, ln_v_gain, ln_v_bias, w_spatial, b_spatial, sinks, w_a, w_b, w_o, norm_mix_pre, norm_mix_post, w_ff_in, w_ff_out, norm_ff_pre, norm_ff_post, loss_target, m_w_in, m_ln_v_gain, m_ln_v_bias, m_w_spatial, m_b_spatial, m_sinks, m_w_a, m_w_b, m_w_o, m_norm_mix_pre, m_norm_mix_post, m_w_ff_in, m_w_ff_out, m_norm_ff_pre, m_norm_ff_post, v_w_in, v_ln_v_gain, v_ln_v_bias, v_w_spatial, v_b_spatial, v_sinks, v_w_a, v_w_b, v_w_o, v_norm_mix_pre, v_norm_mix_post, v_w_ff_in, v_w_ff_out, v_norm_ff_pre, v_norm_ff_post):
    given = dict(locals())
    T = x.shape[1]
    xt = x[0]
    tgt = loss_target[0]
    bst = b_spatial[0].T
    ws = w_spatial[0]

    me = 4 * lax.axis_index("x") + 2 * lax.axis_index("y") + lax.axis_index("c")
    me_arr = me.astype(jnp.int32).reshape(1)

    def with_own(zone, shard):
        return lax.dynamic_update_slice(zone, shard[None], (me,) + (0,) * shard.ndim)

    rest = ("w_a", "w_b", "w_o", "w_ff_in", "w_ff_out")
    shard = {n: given[n][0].astype(BF16) for n in rest}
    win8 = _gather_two_level(w_in[0].T.astype(BF16), "gather_in")
    g_rest = _start_copies([shard[n] for n in rest], [GATHER] * len(rest), "gather_rest_start", after=win8)
    win = win8.reshape(IN_W, D)
    cos, sin = _rope_tables(positions.astype(F32).reshape(T, 1), *_rope_rows())

    proj, h = _fwd_in(xt, norm_mix_pre, win)
    a = _fwd_sgu(proj, ln_v_gain, ln_v_bias, ws, bst)
    att = _fwd_attn(proj, cos, sin, sinks[0])
    gw = {n: with_own(z, own) for n, own, z in zip(rest, *_wait_copies(g_rest, att, "gather_rest_wait"))}
    wa, wb, wo = (gw[n].reshape(D, D) for n in ("w_a", "w_b", "w_o"))
    wfi3 = gw["w_ff_in"]
    wfo = gw["w_ff_out"].reshape(D_FF, D)
    merged, a2, b2, mix, x1, hf = _fwd_mix(a, att, proj, xt, wa, wb, wo, norm_mix_post, norm_ff_pre)
    f, dy, dff, dg3, loss_part = _fwd_ff(hf, wfi3, wfo, x1, tgt, norm_ff_post)

    df, dx1, dmix, dg2, dg1 = _bwd_ff(dff, f, wfi3, wfo, x1, dy, mix, norm_mix_post, norm_ff_pre)
    dwfi3, dwfo = _wgrad_ff(hf, df, f, dff)
    own_ff = [dwfi3, dwfo.reshape(N_DEV, D_FF // N_DEV, D)]
    x_ff = _start_copies(own_ff, [SCATTER] * 2, "exchange_ff_start")
    da2, db2, dgate, da, datt = _bwd_mix(dmix, proj, a2, b2, wo, wa, wb, after=x_ff[-1])
    dwo, dwa, dwb = _wgrad_mix(merged, dmix, a, da2, att, db2)
    own_mix = [g.reshape(N_DEV, D // N_DEV, D) for g in (dwa, dwb, dwo)]
    x_mix = _start_copies(own_mix, [SCATTER] * 3, "exchange_mix_start")
    dq, dkv, dsink = _bwd_attn(proj, cos, sin, sinks[0], datt, after=x_mix[-1])
    duv, dws, dbs, dlng, dlnb = _bwd_sgu(proj, da, ln_v_gain, ln_v_bias, ws, bst)
    small_grads = {"ln_v_gain": dlng, "ln_v_bias": dlnb, "w_spatial": dws, "b_spatial": dbs, "sinks": dsink[:, :N_Q],
                   "norm_mix_post": dg1, "norm_ff_pre": dg2, "norm_ff_post": dg3}
    x_small = _start_copies([_pack_small(small_grads)], [SPREAD], "exchange_small_start")
    dwin = _wgrad_in(h, duv, dq, dkv, dgate)
    own_in = [dwin.reshape(N_DEV, IN_W // N_DEV, D)]
    x_in = _start_copies(own_in, [SCATTER], "exchange_in_start", after=x_small[-1])
    grad_x, dg0 = _bwd_in(duv, dq, dkv, dgate, win, xt, dx1, norm_mix_pre, after=x_in[-1])

    results = {}

    def update(n, own, parts, transposed=False):
        state = [given[k + n][0].T if transposed else given[k + n][0] for k in ("", "m_", "v_")]
        res = _sum_adamw_peers(me_arr, own, parts, *state, "adamw_" + n, False)
        results[n] = [(r.T if transposed else r).reshape(given[n].shape) for r in res]

    own_ff, p_ff = _wait_copies(x_ff, grad_x, "exchange_ff_wait")
    update("w_ff_in", own_ff[0], p_ff[0])
    update("w_ff_out", own_ff[1], p_ff[1])
    own_mix, p_mix = _wait_copies(x_mix, results["w_ff_out"][0], "exchange_mix_wait")
    for n, own, parts in zip(("w_a", "w_b", "w_o"), own_mix, p_mix):
        update(n, own, parts)
    tail = jnp.concatenate([dg0.reshape(8, 128), jnp.tile(loss_part, (8, 1))], axis=0)
    (tail_all,) = _all_to_all([tail], [True], "exchange_tail", after=results["w_o"][0])
    dg0_all = tail_all[:, :8]
    own_small, p_small = _wait_copies(x_small, tail_all, "exchange_small_wait")
    own_in, p_in = _wait_copies(x_in, p_small[0], "exchange_in_wait")
    update("w_in", own_in[0], p_in[0], transposed=True)
    packed = _sum_adamw_peers(me_arr, own_small[0], p_small[0], _pack_small({n: given[n] for n in SMALL}),
                              _pack_small({n: given["m_" + n] for n in SMALL}),
                              _pack_small({n: given["v_" + n] for n in SMALL}), "adamw_small", True)
    shapes = {n: given[n].shape for n in SMALL}
    unpacked = [_unpack_small(p, shapes) for p in packed]
    for n in SMALL:
        results[n] = [u[n] for u in unpacked]
    n = "norm_mix_pre"
    results[n] = [r.reshape(given[n].shape) for r in _sum_adamw(
        dg0_all, given[n].reshape(8, 128), given["m_" + n].reshape(8, 128), given["v_" + n].reshape(8, 128), "adamw_" + n)]

    loss = jnp.sum(tail_all[:, 8, 0])
    order = ("w_in", "ln_v_gain", "ln_v_bias", "w_spatial", "b_spatial", "sinks", "w_a", "w_b", "w_o", "norm_mix_pre",
             "norm_mix_post", "w_ff_in", "w_ff_out", "norm_ff_pre", "norm_ff_post")
    out = [loss, grad_x.reshape(x.shape)]
    for k in range(4):
        out += [results[n][k] for n in order]
    return tuple(out)
```

```python
import functools

import jax
import jax.numpy as jnp
from jax import lax
from jax.experimental import pallas as pl
from jax.experimental.pallas import tpu as pltpu

F32 = jnp.float32
BF16 = jnp.bfloat16

N_DEV = 8
D = 1024
D_FF = 4096
IN_W = 5632
CHUNK = 128
GROUPS = 8
HEAD = 64
N_Q = 16
N_KV = 4
ROPE = 16
ROPE_THETA = 500000.0
EPS = 1e-6
OFF_Q, OFF_K, OFF_VA, OFF_GA, OFF_GB = 2048, 3072, 3328, 3584, 4608

ADAM_LR = 0.001
ADAM_B1 = 0.9
ADAM_B2 = 0.999
ADAM_EPS = 1e-08
ADAM_WD = 0.01
ADAM_STEP = 10

VMEM_LIMIT = 56 * 1024 * 1024

SDS = jax.ShapeDtypeStruct
MESH = pl.DeviceIdType.MESH


def _params(n_axes=None):
    if n_axes is None:
        return pltpu.CompilerParams(vmem_limit_bytes=VMEM_LIMIT)
    return pltpu.CompilerParams(dimension_semantics=("arbitrary",) * n_axes, vmem_limit_bytes=VMEM_LIMIT)


def _nt(a, b):
    return lax.dot_general(a, b, (((1,), (1,)), ((), ())), preferred_element_type=F32)


def _tn(a, b):
    return lax.dot_general(a, b, (((0,), (0,)), ((), ())), preferred_element_type=F32)


def _nn(a, b):
    return jnp.dot(a, b, preferred_element_type=F32)


def _gelu(x):
    t = jnp.tanh(0.7978845608028654 * (x + 0.044715 * (x * x * x)))
    return 0.5 * x * (1.0 + t), t


def _gelu_grad(x, t):
    return 0.5 * (1.0 + t) + 0.5 * x * (1.0 - t * t) * (0.7978845608028654 * (1.0 + 3.0 * 0.044715 * x * x))


def _sigmoid(x):
    return 1.0 / (1.0 + jnp.exp(-x))


def _rms_stats(v):
    r = lax.rsqrt(jnp.mean(v * v, axis=-1, keepdims=True) + EPS)
    return r, v * r


def _rms_bwd(d, vhat, r, g):
    gd = g * d
    return r * (gd - vhat * jnp.mean(gd * vhat, axis=-1, keepdims=True))


def _colsum(v):
    return jnp.sum(v, axis=0, keepdims=True)


_ANY = pl.BlockSpec(memory_space=pl.ANY)


def _after(body, n_in, after):
    if after is None:
        return body, [], []

    def ordered(*refs):
        return body(*refs[:n_in], *refs[n_in + 1:])

    return ordered, [_ANY], [after]


def _rms_pre(x, g0, after=None):
    T = x.shape[0]
    tm = min(T, 1024)

    def body(x_ref, g_ref, h_ref):
        _, xh = _rms_stats(x_ref[...])
        h_ref[...] = (xh * g_ref[...]).astype(BF16)

    body, dep_specs, deps = _after(body, 2, after)
    return pl.pallas_call(
        body, name="rms_pre", grid=(T // tm,),
        in_specs=[pl.BlockSpec((tm, D), lambda i: (i, 0)), pl.BlockSpec((1, D), lambda i: (0, 0))] + dep_specs,
        out_specs=pl.BlockSpec((tm, D), lambda i: (i, 0)),
        out_shape=SDS((T, D), BF16),
        compiler_params=_params(1),
    )(x, g0, *deps)


def _fwd_in(h, win_t):
    T = h.shape[0]
    tm, tn = min(T, 1024), 1408

    def body(h_ref, w_ref, p_ref):
        p_ref[...] = _nt(h_ref[...], w_ref[...]).astype(BF16)

    return pl.pallas_call(
        body, name="fwd_in", grid=(T // tm, IN_W // tn),
        in_specs=[pl.BlockSpec((tm, D), lambda i, j: (i, 0)), pl.BlockSpec((tn, D), lambda i, j: (j, 0))],
        out_specs=pl.BlockSpec((tm, tn), lambda i, j: (i, j)),
        out_shape=SDS((T, IN_W), BF16),
        compiler_params=_params(2),
    )(h, win_t)


def _sgu_forward_parts(u_ref, vs_ref, lng_ref, lnb_ref):
    u = u_ref[...].astype(F32)
    vs = vs_ref[...].astype(F32)
    gu, tu = _gelu(u)
    gv, tv = _gelu(vs)
    mu = jnp.mean(gv, axis=-1, keepdims=True)
    dv = gv - mu
    rstd = lax.rsqrt(jnp.mean(dv * dv, axis=-1, keepdims=True) + EPS)
    vhat = dv * rstd
    vn = (vhat * lng_ref[...] + lnb_ref[...]).astype(BF16)
    return u, vs, gu, tu, tv, rstd, vhat, vn


def _masked_ws(ws_ref, g):
    row = lax.broadcasted_iota(jnp.int32, (CHUNK, CHUNK), 0)
    col = lax.broadcasted_iota(jnp.int32, (CHUNK, CHUNK), 1)
    return jnp.where(row >= col, ws_ref[g], 0.0).astype(BF16)


def _fwd_sgu(proj, lng, lnb, ws, bst):
    T = proj.shape[0]
    tc = min(T, 512)

    def body(u_ref, vs_ref, lng_ref, lnb_ref, ws_ref, bst_ref, a_ref):
        _, _, gu, _, _, _, _, vn = _sgu_forward_parts(u_ref, vs_ref, lng_ref, lnb_ref)
        for g in range(GROUPS):
            wm = _masked_ws(ws_ref, g)
            cols = slice(g * CHUNK, (g + 1) * CHUNK)
            for c in range(tc // CHUNK):
                rows = slice(c * CHUNK, (c + 1) * CHUNK)
                mixed = _nn(wm, vn[rows, cols]) + bst_ref[:, g:g + 1]
                a_ref[rows, cols] = (gu[rows, cols] * mixed).astype(BF16)

    return pl.pallas_call(
        body, name="fwd_sgu", grid=(T // tc,),
        in_specs=[pl.BlockSpec((tc, D), lambda i: (i, 0)), pl.BlockSpec((tc, D), lambda i: (i, 1)),
                  pl.BlockSpec((1, D), lambda i: (0, 0)), pl.BlockSpec((1, D), lambda i: (0, 0)),
                  pl.BlockSpec((GROUPS, CHUNK, CHUNK), lambda i: (0, 0, 0)), pl.BlockSpec((CHUNK, GROUPS), lambda i: (0, 0))],
        out_specs=pl.BlockSpec((tc, D), lambda i: (i, 0)),
        out_shape=SDS((T, D), BF16),
        compiler_params=_params(1),
    )(proj, proj, lng, lnb, ws, bst)


def _rope_tables(posf, invf, sgn, after=None):
    T = posf.shape[0]
    tr = min(T, 1024)

    def body(pos_ref, invf_ref, sgn_ref, c_ref, s_ref):
        ang = pos_ref[...] * invf_ref[...]
        c_ref[...] = jnp.cos(ang)
        s = jnp.sin(ang)
        s_ref[:, :128] = jnp.where(sgn_ref[...] < 0.0, -s, 0.0)
        s_ref[:, 128:] = jnp.where(sgn_ref[...] > 0.0, s, 0.0)

    body, dep_specs, deps = _after(body, 3, after)
    return pl.pallas_call(
        body, name="rope_tables", grid=(T // tr,),
        in_specs=[pl.BlockSpec((tr, 1), lambda i: (i, 0)), pl.BlockSpec((1, 128), lambda i: (0, 0)),
                  pl.BlockSpec((1, 128), lambda i: (0, 0))] + dep_specs,
        out_specs=[pl.BlockSpec((tr, 128), lambda i: (i, 0)), pl.BlockSpec((tr, 256), lambda i: (i, 0))],
        out_shape=[SDS((T, 128), F32), SDS((T, 256), F32)],
        compiler_params=_params(1),
    )(posf, invf, sgn, *deps)


def _rope(v, c, s_lo, s_hi):
    n = v.shape[1]
    return v * c + pltpu.roll(v, n - ROPE // 2, 1) * s_lo + pltpu.roll(v, ROPE // 2, 1) * s_hi


def _rope_bwd(dv, c, s_lo, s_hi):
    n = dv.shape[1]
    return dv * c + pltpu.roll(dv * s_lo, ROPE // 2, 1) + pltpu.roll(dv * s_hi, n - ROPE // 2, 1)


def _fold_masks(first):
    jj = lax.broadcasted_iota(jnp.int32, (CHUNK, CHUNK), 0)
    t = lax.broadcasted_iota(jnp.int32, (CHUNK, CHUNK), 1)
    prev = jj > t
    return prev, jnp.where(prev & first, -1e30, 0.0)


def _fold(band, prev):
    return jnp.where(prev, band[:CHUNK], band[CHUNK:])


def _unfold(folded, prev):
    return jnp.concatenate([jnp.where(prev, folded, 0.0), jnp.where(prev, 0.0, folded)], axis=0)


def _softmax_sink(s, sink, key_axis):
    m = jnp.maximum(jnp.max(s, axis=key_axis, keepdims=True), sink)
    p = jnp.exp(s - m)
    esink = jnp.exp(sink - m)
    inv = 1.0 / (jnp.sum(p, axis=key_axis, keepdims=True) + esink)
    return p * inv, esink * inv


def _head_pair_operand(band, g):
    slab = band[:, (g // 2) * 128:(g // 2 + 1) * 128]
    lo = lax.broadcasted_iota(jnp.int32, slab.shape, 1) < HEAD
    if g % 2 == 0:
        first = jnp.where(lo, slab, 0.0)
        second = pltpu.roll(first, HEAD, 1)
    else:
        second = jnp.where(lo, 0.0, slab)
        first = pltpu.roll(second, HEAD, 1)
    return jnp.concatenate([first, second], axis=0).astype(BF16)


def _head_pair_gradient(acc, g):
    top, bot = acc[:2 * CHUNK], acc[2 * CHUNK:]
    lo = lax.broadcasted_iota(jnp.int32, top.shape, 1) < HEAD
    if g % 2 == 0:
        return jnp.where(lo, top, 0.0) + pltpu.roll(jnp.where(lo, 0.0, bot), HEAD, 1)
    return pltpu.roll(jnp.where(lo, top, 0.0), HEAD, 1) + jnp.where(lo, 0.0, bot)


def _attn_specs(nb, clamp):
    cur = (lambda i: jnp.minimum(i, nb - 1)) if clamp else (lambda i: i)
    prev = lambda i: jnp.maximum(jnp.minimum(i, nb - 1) - 1, 0)
    kw = N_KV * HEAD
    return cur, prev, [
        pl.BlockSpec((CHUNK, D), lambda i: (cur(i), OFF_Q // D)),
        pl.BlockSpec((CHUNK, kw), lambda i: (prev(i), OFF_K // kw)),
        pl.BlockSpec((CHUNK, kw), lambda i: (cur(i), OFF_K // kw)),
        pl.BlockSpec((CHUNK, kw), lambda i: (prev(i), OFF_VA // kw)),
        pl.BlockSpec((CHUNK, kw), lambda i: (cur(i), OFF_VA // kw)),
        pl.BlockSpec((CHUNK, 128), lambda i: (prev(i), 0)),
        pl.BlockSpec((CHUNK, 128), lambda i: (cur(i), 0)),
        pl.BlockSpec((CHUNK, 256), lambda i: (prev(i), 0)),
        pl.BlockSpec((CHUNK, 256), lambda i: (cur(i), 0)),
        pl.BlockSpec(memory_space=pltpu.SMEM),
    ]


def _attn_load(q_ref, kp_ref, kc_ref, vp_ref, vc_ref, cp_ref, cc_ref, sp_ref, sc_ref):
    q_tab = [jnp.tile(t, (1, D // 128)) for t in (cc_ref[...], sc_ref[:, :128], sc_ref[:, 128:])]
    band = lambda p_ref, c_ref, cols: jnp.concatenate([p_ref[:, cols], c_ref[:, cols]], axis=0)
    k_tab = [jnp.tile(t, (1, N_KV * HEAD // 128)) for t in (
        band(cp_ref, cc_ref, slice(0, 128)), band(sp_ref, sc_ref, slice(0, 128)), band(sp_ref, sc_ref, slice(128, 256)))]
    q = _rope(q_ref[...].astype(F32), *q_tab).astype(BF16)
    kb = _rope(jnp.concatenate([kp_ref[...], kc_ref[...]], axis=0).astype(F32), *k_tab)
    vb = jnp.concatenate([vp_ref[...], vc_ref[...]], axis=0).astype(F32)
    return q, kb, vb, (q_tab, k_tab)


PAIRS_PER_KV = N_Q // N_KV // 2


def _fwd_attn(proj, cos, sin, sinks):
    T = proj.shape[0]
    nb = T // CHUNK
    _, _, specs = _attn_specs(nb, False)

    def body(q_ref, kp_ref, kc_ref, vp_ref, vc_ref, cp_ref, cc_ref, sp_ref, sc_ref, sink_ref, o_ref):
        q, kb, vb, _ = _attn_load(q_ref, kp_ref, kc_ref, vp_ref, vc_ref, cp_ref, cc_ref, sp_ref, sc_ref)
        prev, bias = _fold_masks(pl.program_id(0) == 0)
        for g in range(N_KV):
            k2 = _head_pair_operand(kb, g)
            v2 = _head_pair_operand(vb, g)
            for r in range(PAIRS_PER_KV):
                pair = g * PAIRS_PER_KV + r
                s2 = _nt(k2, q[:, pair * 128:(pair + 1) * 128]) * (HEAD ** -0.5)
                ps = []
                for e in range(2):
                    s = _fold(s2[e * 2 * CHUNK:(e + 1) * 2 * CHUNK], prev) + bias
                    ps.append(_unfold(_softmax_sink(s, sink_ref[2 * pair + e], 0)[0], prev).astype(BF16))
                o_ref[:, pair * 128:(pair + 1) * 128] = _tn(jnp.concatenate(ps, axis=0), v2).astype(BF16)

    return pl.pallas_call(
        body, name="fwd_attn", grid=(nb,), in_specs=specs,
        out_specs=pl.BlockSpec((CHUNK, D), lambda i: (i, 0)),
        out_shape=SDS((T, D), BF16),
        compiler_params=_params(1),
    )(proj, proj, proj, proj, proj, cos, cos, sin, sin, sinks)


def _fwd_mix(a, att, proj, x, wa, wb, wo, g1, g2):
    T = x.shape[0]
    tm = min(T, 512)
    half = D // 2

    def body(a_ref, att_ref, ga0, ga1, gb0, gb1, x_ref, wa_ref, wb_ref, wo_ref, g1_ref, g2_ref,
             mg_ref, a2_ref, b2_ref, mix_ref, x1_ref, hf_ref):
        a2 = _nn(a_ref[...], wa_ref[...])
        b2 = _nn(att_ref[...], wb_ref[...])
        ga = jnp.concatenate([ga0[...], ga1[...]], axis=1).astype(F32)
        gb = jnp.concatenate([gb0[...], gb1[...]], axis=1).astype(F32)
        merged = (_sigmoid(ga) * a2 + _sigmoid(gb) * b2).astype(BF16)
        a2_ref[...] = a2.astype(BF16)
        b2_ref[...] = b2.astype(BF16)
        mg_ref[...] = merged
        mix = _nn(merged, wo_ref[...])
        mix_ref[...] = mix
        _, mh = _rms_stats(mix)
        x1 = x_ref[...] + mh * g1_ref[...]
        x1_ref[...] = x1
        _, xh = _rms_stats(x1)
        hf_ref[...] = (xh * g2_ref[...]).astype(BF16)

    row = lambda i: (i, 0)
    const = lambda i: (0, 0)
    gspec = lambda off: pl.BlockSpec((tm, half), lambda i: (i, off // half))
    return pl.pallas_call(
        body, name="fwd_mix", grid=(T // tm,),
        in_specs=[pl.BlockSpec((tm, D), row), pl.BlockSpec((tm, D), row),
                  gspec(OFF_GA), gspec(OFF_GA + half), gspec(OFF_GB), gspec(OFF_GB + half),
                  pl.BlockSpec((tm, D), row), _resident((D, D)), _resident((D, D)),
                  _resident((D, D)), pl.BlockSpec((1, D), const), pl.BlockSpec((1, D), const)],
        out_specs=[pl.BlockSpec((tm, D), row)] * 6,
        out_shape=[SDS((T, D), BF16), SDS((T, D), BF16), SDS((T, D), BF16), SDS((T, D), F32), SDS((T, D), F32),
                   SDS((T, D), BF16)],
        compiler_params=_params(1),
    )(a, att, proj, proj, proj, proj, x, wa, wb, wo, g1, g2)


FF_SPLIT = N_DEV
FF_TILE = D_FF // FF_SPLIT
FF_STEP = 2048
FF_SLABS = FF_STEP // FF_TILE
FF_STEPS = D_FF // FF_STEP


def _fwd_ff(hf, wfi3, wfo, x1, tgt, g3):
    T = hf.shape[0]
    tm = min(T, 512)
    last = FF_STEPS - 1

    def body(hf_ref, wfi_ref, wfo_ref, x1_ref, tgt_ref, g3_ref, f_ref, dy_ref, dff_ref, dg3_ref, loss_ref, acc, r_s):
        i, p = pl.program_id(0), pl.program_id(1)

        @pl.when((i == 0) & (p == 0))
        def _():
            dg3_ref[...] = jnp.zeros_like(dg3_ref)
            loss_ref[...] = jnp.zeros_like(loss_ref)

        hf_t = hf_ref[...]
        for s in range(FF_SLABS):
            cols = slice(s * FF_TILE, (s + 1) * FF_TILE)
            f = _nn(hf_t, wfi_ref[s]).astype(BF16)
            f_ref[:, cols] = f
            rl = jnp.maximum(f.astype(F32), 0.0)
            r_s[:, cols] = (rl * rl).astype(BF16)
        part = _nn(r_s[...], wfo_ref[...])

        @pl.when(p == 0)
        def _():
            acc[...] = part

        @pl.when(p > 0)
        def _():
            acc[...] += part

        @pl.when(p == last)
        def _():
            r3, fh = _rms_stats(acc[...])
            e = x1_ref[...] + fh * g3_ref[...] - tgt_ref[...]
            loss_ref[...] += jnp.sum(e * e) * (0.5 / D)
            dy = e * (1.0 / D)
            dy_ref[...] = dy
            dg3_ref[...] += _colsum(dy * fh)
            dff_ref[...] = _rms_bwd(dy, fh, r3, g3_ref[...]).astype(BF16)

    row = lambda i, p: (i, 0)
    const = lambda i, p: (0, 0)
    return pl.pallas_call(
        body, name="fwd_ff", grid=(T // tm, FF_STEPS),
        in_specs=[pl.BlockSpec((tm, D), row), pl.BlockSpec((FF_SLABS, D, FF_TILE), lambda i, p: (p, 0, 0)),
                  pl.BlockSpec((FF_STEP, D), lambda i, p: (p, 0)), pl.BlockSpec((tm, D), row),
                  pl.BlockSpec((tm, D), row), pl.BlockSpec((1, D), const)],
        out_specs=[pl.BlockSpec((tm, FF_STEP), lambda i, p: (i, p)), pl.BlockSpec((tm, D), row),
                   pl.BlockSpec((tm, D), row), pl.BlockSpec((1, D), const), pl.BlockSpec((1, 128), const)],
        out_shape=[SDS((T, D_FF), BF16), SDS((T, D), F32), SDS((T, D), BF16), SDS((1, D), F32), SDS((1, 128), F32)],
        scratch_shapes=[pltpu.VMEM((tm, D), F32), pltpu.VMEM((tm, FF_STEP), BF16)],
        compiler_params=_params(2),
    )(hf, wfi3, wfo, x1, tgt, g3)


def _bwd_ff(dff, f, wfi3, wfo, x1, dy, mix, g1, g2):
    T = dff.shape[0]
    tm = min(T, 512)
    last = FF_STEPS - 1

    def body(dff_ref, f_ref, wfi_ref, wfo_ref, x1_ref, dy_ref, mix_ref, g1_ref, g2_ref,
             df_ref, dx1_ref, dmix_ref, dg2_ref, dg1_ref, acc):
        i, p = pl.program_id(0), pl.program_id(1)

        @pl.when((i == 0) & (p == 0))
        def _():
            dg2_ref[...] = jnp.zeros_like(dg2_ref)
            dg1_ref[...] = jnp.zeros_like(dg1_ref)

        dr = _nt(dff_ref[...], wfo_ref[...])
        df_ref[...] = (dr * (2.0 * jnp.maximum(f_ref[...].astype(F32), 0.0))).astype(BF16)
        part = _nt(df_ref[:, :FF_TILE], wfi_ref[0])
        for s in range(1, FF_SLABS):
            part = part + _nt(df_ref[:, s * FF_TILE:(s + 1) * FF_TILE], wfi_ref[s])

        @pl.when(p == 0)
        def _():
            acc[...] = part

        @pl.when(p > 0)
        def _():
            acc[...] += part

        @pl.when(p == last)
        def _():
            dhf = acc[...]
            r2, xh = _rms_stats(x1_ref[...])
            dg2_ref[...] += _colsum(dhf * xh)
            dx1 = dy_ref[...] + _rms_bwd(dhf, xh, r2, g2_ref[...])
            dx1_ref[...] = dx1
            r1, mh = _rms_stats(mix_ref[...])
            dg1_ref[...] += _colsum(dx1 * mh)
            dmix_ref[...] = _rms_bwd(dx1, mh, r1, g1_ref[...]).astype(BF16)

    row = lambda i, p: (i, 0)
    const = lambda i, p: (0, 0)
    return pl.pallas_call(
        body, name="bwd_ff", grid=(T // tm, FF_STEPS),
        in_specs=[pl.BlockSpec((tm, D), row), pl.BlockSpec((tm, FF_STEP), lambda i, p: (i, p)),
                  pl.BlockSpec((FF_SLABS, D, FF_TILE), lambda i, p: (p, 0, 0)), pl.BlockSpec((FF_STEP, D), lambda i, p: (p, 0)),
                  pl.BlockSpec((tm, D), row), pl.BlockSpec((tm, D), row), pl.BlockSpec((tm, D), row),
                  pl.BlockSpec((1, D), const), pl.BlockSpec((1, D), const)],
        out_specs=[pl.BlockSpec((tm, FF_STEP), lambda i, p: (i, p)), pl.BlockSpec((tm, D), row),
                   pl.BlockSpec((tm, D), row), pl.BlockSpec((1, D), const), pl.BlockSpec((1, D), const)],
        out_shape=[SDS((T, D_FF), BF16), SDS((T, D), F32), SDS((T, D), BF16), SDS((1, D), F32), SDS((1, D), F32)],
        scratch_shapes=[pltpu.VMEM((tm, D), F32)],
        compiler_params=_params(2),
    )(dff, f, wfi3, wfo, x1, dy, mix, g1, g2)


def _wgrad_ff(hf, df, f, dff):
    T = hf.shape[0]
    tt = min(T, 1024)
    wide = 2 * FF_TILE

    def body(hf_ref, df_ref, f_ref, dff_ref, dwfi_ref, dwfo_ref, acc_i, acc_o):
        t = pl.program_id(1)

        @pl.when(t == 0)
        def _():
            acc_i[...] = jnp.zeros_like(acc_i)
            acc_o[...] = jnp.zeros_like(acc_o)

        acc_i[...] += _tn(hf_ref[...], df_ref[...])
        rl = jnp.maximum(f_ref[...].astype(F32), 0.0)
        acc_o[...] += _tn((rl * rl).astype(BF16), dff_ref[...])

        @pl.when(t == T // tt - 1)
        def _():
            dwfi_ref[0] = acc_i[:, :FF_TILE].astype(BF16)
            dwfi_ref[1] = acc_i[:, FF_TILE:].astype(BF16)
            dwfo_ref[...] = acc_o[...].astype(BF16)

    return pl.pallas_call(
        body, name="wgrad_ff", grid=(D_FF // wide, T // tt),
        in_specs=[pl.BlockSpec((tt, D), lambda p, t: (t, 0)), pl.BlockSpec((tt, wide), lambda p, t: (t, p)),
                  pl.BlockSpec((tt, wide), lambda p, t: (t, p)), pl.BlockSpec((tt, D), lambda p, t: (t, 0))],
        out_specs=[pl.BlockSpec((2, D, FF_TILE), lambda p, t: (p, 0, 0)), pl.BlockSpec((wide, D), lambda p, t: (p, 0))],
        out_shape=[SDS((FF_SPLIT, D, FF_TILE), BF16), SDS((D_FF, D), BF16)],
        scratch_shapes=[pltpu.VMEM((D, wide), F32), pltpu.VMEM((wide, D), F32)],
        compiler_params=_params(2),
    )(hf, df, f, dff)


def _bwd_mix(dmix, proj, a2, b2, wo, wa, wb, after=None):
    T = dmix.shape[0]
    tm = min(T, 512)
    half = D // 2

    def body(dmix_ref, ga0, ga1, gb0, gb1, a2_ref, b2_ref, wo_ref, wa_ref, wb_ref,
             da2_ref, db2_ref, dg_ref, da_ref, datt_ref):
        dmg = _nt(dmix_ref[...], wo_ref[...])
        sa = _sigmoid(jnp.concatenate([ga0[...], ga1[...]], axis=1).astype(F32))
        sb = _sigmoid(jnp.concatenate([gb0[...], gb1[...]], axis=1).astype(F32))
        da2 = (dmg * sa).astype(BF16)
        db2 = (dmg * sb).astype(BF16)
        da2_ref[...] = da2
        db2_ref[...] = db2
        dg_ref[:, :D] = (dmg * a2_ref[...].astype(F32) * (sa * (1.0 - sa))).astype(BF16)
        dg_ref[:, D:] = (dmg * b2_ref[...].astype(F32) * (sb * (1.0 - sb))).astype(BF16)
        da_ref[...] = _nt(da2, wa_ref[...]).astype(BF16)
        datt_ref[...] = _nt(db2, wb_ref[...]).astype(BF16)

    row = lambda i: (i, 0)
    const = lambda i: (0, 0)
    gspec = lambda off: pl.BlockSpec((tm, half), lambda i: (i, off // half))
    body, dep_specs, deps = _after(body, 10, after)
    return pl.pallas_call(
        body, name="bwd_mix", grid=(T // tm,),
        in_specs=[pl.BlockSpec((tm, D), row), gspec(OFF_GA), gspec(OFF_GA + half), gspec(OFF_GB), gspec(OFF_GB + half),
                  pl.BlockSpec((tm, D), row), pl.BlockSpec((tm, D), row),
                  _resident((D, D)), _resident((D, D)), _resident((D, D))] + dep_specs,
        out_specs=[pl.BlockSpec((tm, D), row), pl.BlockSpec((tm, D), row), pl.BlockSpec((tm, 2 * D), row),
                   pl.BlockSpec((tm, D), row), pl.BlockSpec((tm, D), row)],
        out_shape=[SDS((T, D), BF16), SDS((T, D), BF16), SDS((T, 2 * D), BF16), SDS((T, D), BF16), SDS((T, D), BF16)],
        compiler_params=_params(1),
    )(dmix, proj, proj, proj, proj, a2, b2, wo, wa, wb, *deps)


def _wgrad_mix(merged, dmix, a, da2, att, db2):
    T = merged.shape[0]
    tt = min(T, 512)

    def body(mg_ref, dmix_ref, a_ref, da2_ref, att_ref, db2_ref, dwo_ref, dwa_ref, dwb_ref, acc):
        t = pl.program_id(0)

        @pl.when(t == 0)
        def _():
            acc[...] = jnp.zeros_like(acc)

        acc[0] += _tn(mg_ref[...], dmix_ref[...])
        acc[1] += _tn(a_ref[...], da2_ref[...])
        acc[2] += _tn(att_ref[...], db2_ref[...])

        @pl.when(t == T // tt - 1)
        def _():
            dwo_ref[...] = acc[0].astype(BF16)
            dwa_ref[...] = acc[1].astype(BF16)
            dwb_ref[...] = acc[2].astype(BF16)

    return pl.pallas_call(
        body, name="wgrad_mix", grid=(T // tt,),
        in_specs=[pl.BlockSpec((tt, D), lambda t: (t, 0))] * 6,
        out_specs=[pl.BlockSpec((D, D), lambda t: (0, 0))] * 3,
        out_shape=[SDS((D, D), BF16)] * 3,
        scratch_shapes=[pltpu.VMEM((3, D, D), F32)],
        compiler_params=_params(1),
    )(merged, dmix, a, da2, att, db2)


def _bwd_attn(proj, cos, sin, sinks, datt, after=None):
    T = proj.shape[0]
    nb = T // CHUNK
    kw = N_KV * HEAD
    cur, prev, specs = _attn_specs(nb, True)

    def body(q_ref, kp_ref, kc_ref, vp_ref, vc_ref, cp_ref, cc_ref, sp_ref, sc_ref, sink_ref, do_ref,
             dq_ref, dkv_ref, dsink_ref, carry_k, carry_v, dq_acc):
        i = pl.program_id(0)

        @pl.when(i == 0)
        def _():
            carry_k[...] = jnp.zeros_like(carry_k)
            carry_v[...] = jnp.zeros_like(carry_v)
            dsink_ref[...] = jnp.zeros_like(dsink_ref)

        @pl.when(i < nb)
        def _():
            q, kb, vb, (q_tab, k_tab) = _attn_load(q_ref, kp_ref, kc_ref, vp_ref, vc_ref, cp_ref, cc_ref,
                                                   sp_ref, sc_ref)
            prev, bias = _fold_masks(i == 0)
            do = do_ref[...]
            lane = lax.broadcasted_iota(jnp.int32, (1, 128), 1)
            dsink = jnp.zeros((1, 128), F32)
            dks, dvs = [], []
            for g in range(N_KV):
                k2 = _head_pair_operand(kb, g)
                v2 = _head_pair_operand(vb, g)
                dk2 = jnp.zeros((4 * CHUNK, 128), F32)
                dv2 = jnp.zeros((4 * CHUNK, 128), F32)
                for r in range(PAIRS_PER_KV):
                    pair = g * PAIRS_PER_KV + r
                    qp = q[:, pair * 128:(pair + 1) * 128]
                    dop = do[:, pair * 128:(pair + 1) * 128]
                    s2 = _nt(k2, qp) * (HEAD ** -0.5)
                    dp2 = _nt(v2, dop)
                    ps, dss = [], []
                    for e in range(2):
                        rows = slice(e * 2 * CHUNK, (e + 1) * 2 * CHUNK)
                        p, psink = _softmax_sink(_fold(s2[rows], prev) + bias, sink_ref[2 * pair + e], 0)
                        dp = _fold(dp2[rows], prev)
                        delta = jnp.sum(p * dp, axis=0, keepdims=True)
                        ps.append(_unfold(p, prev).astype(BF16))
                        dss.append(_unfold(p * (dp - delta) * (HEAD ** -0.5), prev).astype(BF16))
                        dsink = dsink + jnp.where(lane == 2 * pair + e, -jnp.sum(psink * delta), 0.0)
                    ds2 = jnp.concatenate(dss, axis=0)
                    dq_acc[:, pair * 128:(pair + 1) * 128] = _tn(ds2, k2)
                    dk2 = dk2 + _nn(ds2, qp)
                    dv2 = dv2 + _nn(jnp.concatenate(ps, axis=0), dop)
                dks.append(_head_pair_gradient(dk2, g))
                dvs.append(_head_pair_gradient(dv2, g))
            dsink_ref[...] += dsink
            dq_ref[...] = _rope_bwd(dq_acc[...], *q_tab).astype(BF16)
            dkb = _rope_bwd(jnp.concatenate([dks[0] + dks[1], dks[2] + dks[3]], axis=1), *k_tab)
            dvb = jnp.concatenate([dvs[0] + dvs[1], dvs[2] + dvs[3]], axis=1)
            dkv_ref[:, :kw] = (carry_k[...] + dkb[:CHUNK]).astype(BF16)
            dkv_ref[:, kw:] = (carry_v[...] + dvb[:CHUNK]).astype(BF16)
            carry_k[...] = dkb[CHUNK:]
            carry_v[...] = dvb[CHUNK:]

        @pl.when(i == nb)
        def _():
            dkv_ref[:, :kw] = carry_k[...].astype(BF16)
            dkv_ref[:, kw:] = carry_v[...].astype(BF16)

    body, dep_specs, deps = _after(body, 11, after)
    return pl.pallas_call(
        body, name="bwd_attn", grid=(nb + 1,),
        in_specs=specs + [pl.BlockSpec((CHUNK, D), lambda i: (cur(i), 0))] + dep_specs,
        out_specs=[pl.BlockSpec((CHUNK, D), lambda i: (cur(i), 0)),
                   pl.BlockSpec((CHUNK, 2 * kw), lambda i: (jnp.maximum(i - 1, 0), 0)),
                   pl.BlockSpec((1, 128), lambda i: (0, 0))],
        out_shape=[SDS((T, D), BF16), SDS((T, 2 * kw), BF16), SDS((1, 128), F32)],
        scratch_shapes=[pltpu.VMEM((CHUNK, kw), F32), pltpu.VMEM((CHUNK, kw), F32), pltpu.VMEM((CHUNK, D), F32)],
        compiler_params=_params(1),
    )(proj, proj, proj, proj, proj, cos, cos, sin, sin, sinks, datt, *deps)


def _bwd_sgu(proj, da, lng, lnb, ws, bst):
    T = proj.shape[0]
    tc = min(T, 512)
    nsteps = T // tc

    def body(u_ref, vs_ref, da_ref, lng_ref, lnb_ref, ws_ref, bst_ref,
             duv_ref, dws_ref, dbs_ref, dlng_ref, dlnb_ref, dvn_s, dgu_s, dmx_sum):
        i = pl.program_id(0)

        @pl.when(i == 0)
        def _():
            dws_ref[...] = jnp.zeros_like(dws_ref)
            dlng_ref[...] = jnp.zeros_like(dlng_ref)
            dlnb_ref[...] = jnp.zeros_like(dlnb_ref)
            dmx_sum[...] = jnp.zeros_like(dmx_sum)

        u, vs, gu, tu, tv, rstd, vhat, vn = _sgu_forward_parts(u_ref, vs_ref, lng_ref, lnb_ref)
        da = da_ref[...].astype(F32)
        for g in range(GROUPS):
            wm = _masked_ws(ws_ref, g)
            cols = slice(g * CHUNK, (g + 1) * CHUNK)
            dws = jnp.zeros((CHUNK, CHUNK), F32)
            dsum = jnp.zeros((CHUNK, CHUNK), F32)
            for c in range(tc // CHUNK):
                rows = slice(c * CHUNK, (c + 1) * CHUNK)
                vn_cg = vn[rows, cols]
                mixed = _nn(wm, vn_cg) + bst_ref[:, g:g + 1]
                dgu_s[rows, cols] = da[rows, cols] * mixed
                dmx = da[rows, cols] * gu[rows, cols]
                dmxb = dmx.astype(BF16)
                dws = dws + _nt(dmxb, vn_cg)
                dsum = dsum + dmx
                dvn_s[rows, cols] = _tn(wm, dmxb)
            dws_ref[g] += dws
            dmx_sum[:, cols] += dsum
        dvn = dvn_s[...]
        dlng_ref[...] += _colsum(dvn * vhat)
        dlnb_ref[...] += _colsum(dvn)
        dvh = dvn * lng_ref[...]
        dgv = rstd * (dvh - jnp.mean(dvh, axis=-1, keepdims=True) - vhat * jnp.mean(dvh * vhat, axis=-1, keepdims=True))
        duv_ref[:, :D] = (dgu_s[...] * _gelu_grad(u, tu)).astype(BF16)
        duv_ref[:, D:] = (dgv * _gelu_grad(vs, tv)).astype(BF16)

        @pl.when(i == nsteps - 1)
        def _():
            row = lax.broadcasted_iota(jnp.int32, (CHUNK, CHUNK), 0)
            col = lax.broadcasted_iota(jnp.int32, (CHUNK, CHUNK), 1)
            for g in range(GROUPS):
                dws_ref[g] = jnp.where(row >= col, dws_ref[g], 0.0)
                dbs_ref[g:g + 1, :] = _colsum(dmx_sum[:, g * CHUNK:(g + 1) * CHUNK].T)

    const2 = lambda i: (0, 0)
    return pl.pallas_call(
        body, name="bwd_sgu", grid=(nsteps,),
        in_specs=[pl.BlockSpec((tc, D), lambda i: (i, 0)), pl.BlockSpec((tc, D), lambda i: (i, 1)),
                  pl.BlockSpec((tc, D), lambda i: (i, 0)), pl.BlockSpec((1, D), const2), pl.BlockSpec((1, D), const2),
                  pl.BlockSpec((GROUPS, CHUNK, CHUNK), lambda i: (0, 0, 0)), pl.BlockSpec((CHUNK, GROUPS), const2)],
        out_specs=[pl.BlockSpec((tc, 2 * D), lambda i: (i, 0)), pl.BlockSpec((GROUPS, CHUNK, CHUNK), lambda i: (0, 0, 0)),
                   pl.BlockSpec((GROUPS, CHUNK), const2), pl.BlockSpec((1, D), const2), pl.BlockSpec((1, D), const2)],
        out_shape=[SDS((T, 2 * D), BF16), SDS((GROUPS, CHUNK, CHUNK), F32), SDS((GROUPS, CHUNK), F32),
                   SDS((1, D), F32), SDS((1, D), F32)],
        scratch_shapes=[pltpu.VMEM((tc, D), F32), pltpu.VMEM((tc, D), F32), pltpu.VMEM((CHUNK, D), F32)],
        compiler_params=_params(1),
    )(proj, proj, da, lng, lnb, ws, bst)


IN_SEG_WIDTHS = (2 * D, D, 2 * N_KV * HEAD, 2 * D)


def _resident(shape):
    return pl.BlockSpec(shape, lambda *_: (0,) * len(shape), pipeline_mode=pl.Buffered(1))


def _bwd_in(duv, dq, dkv, dg, win_t, x, dx1, g0, after=None):
    T = x.shape[0]
    tm = min(T, 512)

    def body(duv_ref, dq_ref, dkv_ref, dg_ref, w_ref, x_ref, dx1_ref, g0_ref, gx_ref, dg0_ref):
        @pl.when(pl.program_id(0) == 0)
        def _():
            dg0_ref[...] = jnp.zeros_like(dg0_ref)

        dh, off = None, 0
        for ref, width in zip((duv_ref, dq_ref, dkv_ref, dg_ref), IN_SEG_WIDTHS):
            part = _nn(ref[...], w_ref[off:off + width, :])
            dh = part if dh is None else dh + part
            off += width
        r0, xh = _rms_stats(x_ref[...])
        dg0_ref[...] += _colsum(dh * xh)
        gx_ref[...] = dx1_ref[...] + _rms_bwd(dh, xh, r0, g0_ref[...])

    row = lambda i: (i, 0)
    body, dep_specs, deps = _after(body, 8, after)
    return pl.pallas_call(
        body, name="bwd_in", grid=(T // tm,),
        in_specs=[pl.BlockSpec((tm, w), row) for w in IN_SEG_WIDTHS] + [
            _resident((IN_W, D)), pl.BlockSpec((tm, D), row), pl.BlockSpec((tm, D), row),
            pl.BlockSpec((1, D), lambda i: (0, 0))] + dep_specs,
        out_specs=[pl.BlockSpec((tm, D), row), pl.BlockSpec((1, D), lambda i: (0, 0))],
        out_shape=[SDS((T, D), F32), SDS((1, D), F32)],
        compiler_params=_params(1),
    )(duv, dq, dkv, dg, win_t, x, dx1, g0, *deps)


def _wgrad_rows(h, segs, first_row, into, name):
    T = h.shape[0]
    tt = min(T, 1024)
    widths = [s.shape[1] for s in segs]
    rows = sum(widths)
    n_in = 1 + len(segs) + (into is not None)

    def body(*refs):
        h_ref, seg_refs = refs[0], refs[1:1 + len(segs)]
        dw_ref, acc, stage, sem = refs[n_in], refs[n_in + 1], refs[n_in + 2], refs[n_in + 3]
        t = pl.program_id(0)

        @pl.when(t == 0)
        def _():
            acc[...] = jnp.zeros_like(acc)

        off = 0
        for ref, width in zip(seg_refs, widths):
            acc[off:off + width, :] += _tn(ref[...], h_ref[...])
            off += width

        @pl.when(t == T // tt - 1)
        def _():
            stage[...] = acc[...].astype(BF16)
            out = pltpu.make_async_copy(stage, dw_ref.at[pl.ds(first_row, rows)], sem)
            out.start()
            out.wait()

    row = lambda t: (t, 0)
    return pl.pallas_call(
        body, name=name, grid=(T // tt,),
        in_specs=[pl.BlockSpec((tt, D), row)] + [pl.BlockSpec((tt, w), row) for w in widths] + [_ANY] * (into is not None),
        out_specs=_ANY,
        out_shape=SDS((IN_W, D), BF16),
        input_output_aliases={} if into is None else {n_in - 1: 0},
        scratch_shapes=[pltpu.VMEM((rows, D), F32), pltpu.VMEM((rows, D), BF16), pltpu.SemaphoreType.DMA],
        compiler_params=_params(1),
    )(h, *segs, *([] if into is None else [into]))


def _wgrad_in(h, duv, dq, dkv, dg):
    dw = _wgrad_rows(h, [dg], IN_SEG_WIDTHS[0] + IN_SEG_WIDTHS[1] + IN_SEG_WIDTHS[2], None, "wgrad_in_gates")
    dw = _wgrad_rows(h, [duv], 0, dw, "wgrad_in_uv")
    return _wgrad_rows(h, [dq, dkv], IN_SEG_WIDTHS[0], dw, "wgrad_in_qkv")


def _place():
    x, y, c = lax.axis_index("x"), lax.axis_index("y"), lax.axis_index("c")
    return x, y, c, 4 * x + 2 * y + c


def _peers(x, y, c):
    out = []
    for mask in range(1, N_DEV):
        px = 1 - x if mask & 4 else x
        py = 1 - y if mask & 2 else y
        pc = 1 - c if mask & 1 else c
        out.append(((px, py, pc), 4 * px + 2 * py + pc))
    return out


def _all_to_all(arrays, gather, name, after=None):
    n = len(arrays)

    def body(*refs):
        ins, outs = refs[:n], refs[n:2 * n]
        send_sems, recv_sems, local_sems = refs[2 * n:]
        x, y, c, me = _place()
        local, sends, recvs = [], [], []
        for a in range(n):
            src_own = ins[a] if gather[a] else ins[a].at[me]
            local.append(pltpu.make_async_copy(src_own, outs[a].at[me], local_sems.at[a]))
            for k, (peer, pid) in enumerate(_peers(x, y, c)):
                sem = a * (N_DEV - 1) + k
                src = ins[a] if gather[a] else ins[a].at[pid]
                sends.append(pltpu.make_async_remote_copy(
                    src_ref=src, dst_ref=outs[a].at[me], send_sem=send_sems.at[sem], recv_sem=recv_sems.at[sem],
                    device_id=peer, device_id_type=MESH))
                recvs.append(pltpu.make_async_remote_copy(
                    src_ref=src, dst_ref=outs[a].at[pid], send_sem=send_sems.at[sem], recv_sem=recv_sems.at[sem],
                    device_id=peer, device_id_type=MESH))
        for cp in local + sends:
            cp.start()
        for cp in recvs:
            cp.wait_recv()
        for cp in sends:
            cp.wait_send()
        for cp in local:
            cp.wait()

    out_shape = [SDS((N_DEV,) + a.shape if gt else a.shape, a.dtype) for a, gt in zip(arrays, gather)]
    nsem = n * (N_DEV - 1)
    body, dep_specs, deps = _after(body, n, after)
    return pl.pallas_call(
        body, name=name,
        in_specs=[pl.BlockSpec(memory_space=pl.ANY)] * n + dep_specs,
        out_specs=[pl.BlockSpec(memory_space=pl.ANY)] * n,
        out_shape=out_shape,
        scratch_shapes=[pltpu.SemaphoreType.DMA((nsem,)), pltpu.SemaphoreType.DMA((nsem,)), pltpu.SemaphoreType.DMA((n,))],
    )(*arrays, *deps)


_HBM = pl.BlockSpec(memory_space=pltpu.HBM)
_SEM = pl.BlockSpec(memory_space=pltpu.SEMAPHORE)
_EFFECT = pltpu.SideEffectType.DATAFLOW_SIDE_EFFECTING
GATHER = "gather"
SCATTER = "scatter"
SPREAD = "spread"


def _zone_shape(a, mode):
    if mode == GATHER:
        return (N_DEV,) + a.shape
    return (N_DEV - 1,) + (a.shape[1:] if mode == SCATTER else a.shape)


def _start_copies(arrays, modes, name, after=None):
    n = len(arrays)
    zones = [lax.empty(_zone_shape(a, m), a.dtype) for a, m in zip(arrays, modes)]

    def body(*refs):
        ins, lands = refs[:n], refs[n:2 * n]
        send_sems, recv_sems = refs[-2 * n - 3], refs[-2 * n - 2]
        token = refs[-1]
        x, y, c, me = _place()
        for a in range(n):
            for k, (peer, pid) in enumerate(_peers(x, y, c)):
                src = ins[a].at[pid] if modes[a] == SCATTER else ins[a]
                dst = lands[a].at[me] if modes[a] == GATHER else lands[a].at[k]
                pltpu.make_async_remote_copy(src_ref=src, dst_ref=dst, send_sem=send_sems.at[a], recv_sem=recv_sems.at[a],
                                             device_id=peer, device_id_type=MESH).start()
        token[...] = jnp.zeros_like(token)

    hbm = lambda a: pltpu.HBM(a.shape, a.dtype)
    sems = pltpu.SemaphoreType.DMA((n,))
    extra = [] if after is None else [after]
    operands = [pltpu.with_memory_space_constraint(a, pltpu.HBM) for a in list(arrays) + zones]
    res = pl.pallas_call(
        body, name=name,
        out_shape=(sems, sems, *[hbm(a) for a in arrays], *[hbm(z) for z in zones], SDS((8, 128), F32)),
        in_specs=[_HBM] * (2 * n) + [_ANY] * len(extra),
        out_specs=(_SEM, _SEM, *[_HBM] * (2 * n), pl.BlockSpec(memory_space=pltpu.VMEM)),
        input_output_aliases={i: 2 + i for i in range(2 * n)},
        compiler_params=pltpu.CompilerParams(has_side_effects=_EFFECT),
    )(*operands, *extra)
    return res[0], res[1], list(res[2:2 + n]), list(res[2 + n:2 + 2 * n]), res[-1]


def _wait_copies(started, after, name, count=N_DEV - 1):
    send_sems, recv_sems, thru, zones, _ = started
    nt, nz = len(thru), len(zones)

    def body(*refs):
        lands = refs[nt:nt + nz]
        send_ref, recv_ref = refs[nt + nz], refs[nt + nz + 1]
        x, y, c, _ = _place()
        for a in range(nz):
            blocks = lands[a].at[pl.ds(0, count)]
            cp = pltpu.make_async_remote_copy(src_ref=blocks, dst_ref=blocks, send_sem=send_ref.at[a], recv_sem=recv_ref.at[a],
                                              device_id=(x, y, 1 - c), device_id_type=MESH)
            cp.wait_send()
            cp.wait_recv()

    hbm = lambda a: pltpu.HBM(a.shape, a.dtype)
    res = pl.pallas_call(
        body, name=name,
        out_shape=tuple(hbm(a) for a in thru + zones),
        in_specs=[_HBM] * (nt + nz) + [_SEM, _SEM, _ANY],
        out_specs=tuple([_HBM] * (nt + nz)),
        input_output_aliases={i: i for i in range(nt + nz)},
        compiler_params=pltpu.CompilerParams(has_side_effects=_EFFECT),
    )(*thru, *zones, send_sems, recv_sems, after)
    return list(res[:nt]), list(res[nt:])


def _split_start(body, arrays, zones, name, after):
    n = len(arrays) + len(zones)
    hbm = lambda a: pltpu.HBM(a.shape, a.dtype)
    sems = pltpu.SemaphoreType.DMA((max(len(zones), 1),))
    extra = [] if after is None else [after]
    operands = [pltpu.with_memory_space_constraint(a, pltpu.HBM) for a in list(arrays) + list(zones)]
    res = pl.pallas_call(
        body, name=name,
        out_shape=(sems, sems, *[hbm(a) for a in operands], SDS((8, 128), F32)),
        in_specs=[_HBM] * n + [_ANY] * len(extra),
        out_specs=(_SEM, _SEM, *[_HBM] * n, pl.BlockSpec(memory_space=pltpu.VMEM)),
        input_output_aliases={i: 2 + i for i in range(n)},
        compiler_params=pltpu.CompilerParams(has_side_effects=_EFFECT),
    )(*operands, *extra)
    return res[0], res[1], list(res[2:2 + len(arrays)]), list(res[2 + len(arrays):2 + n]), res[-1]


def _gather_first_leg(shard, name, after=None):
    zone = lax.empty((N_DEV,) + shard.shape, shard.dtype)
    extra = 0 if after is None else 1

    def body(*refs):
        src, land = refs[0], refs[1]
        send_sem, recv_sem, token = refs[2 + extra], refs[3 + extra], refs[-1]
        x, y, c, me = _place()
        for peer in ((x, y, 1 - c), (1 - x, y, c), (x, 1 - y, c), (1 - x, 1 - y, c)):
            pltpu.make_async_remote_copy(src_ref=src, dst_ref=land.at[me], send_sem=send_sem.at[0], recv_sem=recv_sem.at[0],
                                         device_id=peer, device_id_type=MESH).start()
        token[...] = jnp.zeros_like(token)

    return _split_start(body, [shard], [zone], name, after)


def _gather_second_leg(zone, name, after=None):
    extra = 0 if after is None else 1

    def body(*refs):
        land = refs[0]
        send_sem, recv_sem, token = refs[1 + extra], refs[2 + extra], refs[-1]
        x, y, c, _ = _place()
        for px, py in ((1 - x, y), (x, 1 - y), (1 - x, 1 - y)):
            slot = 4 * px + 2 * py + c
            pltpu.make_async_remote_copy(src_ref=land.at[slot], dst_ref=land.at[slot], send_sem=send_sem.at[0],
                                         recv_sem=recv_sem.at[0], device_id=(x, y, 1 - c), device_id_type=MESH).start()
        token[...] = jnp.zeros_like(token)

    return _split_start(body, [], [zone], name, after)


def _adamw_math(g, w, m, v):
    m2 = ADAM_B1 * m + (1.0 - ADAM_B1) * g
    v2 = ADAM_B2 * v + (1.0 - ADAM_B2) * (g * g)
    m_hat = m2 / (1.0 - ADAM_B1 ** ADAM_STEP)
    v_hat = v2 / (1.0 - ADAM_B2 ** ADAM_STEP)
    delta = -ADAM_LR * (m_hat / (jnp.sqrt(v_hat) + ADAM_EPS) + ADAM_WD * w)
    return delta, m2, v2


def _sum_adamw(parts, w, m, v, name):
    R, C = w.shape
    tr = max(t for t in (128, 64, 32, 16, 8) if R % t == 0)

    def body(p_ref, w_ref, m_ref, v_ref, g_ref, d_ref, m2_ref, v2_ref):
        g = p_ref[0]
        for k in range(1, N_DEV):
            g = g + p_ref[k]
        g_ref[...] = g
        d_ref[...], m2_ref[...], v2_ref[...] = _adamw_math(g, w_ref[...], m_ref[...], v_ref[...])

    blk = pl.BlockSpec((tr, C), lambda i: (i, 0))
    return pl.pallas_call(
        body, name=name, grid=(R // tr,),
        in_specs=[pl.BlockSpec((N_DEV, tr, C), lambda i: (0, i, 0)), blk, blk, blk],
        out_specs=[blk] * 4,
        out_shape=[SDS((R, C), F32)] * 4,
        compiler_params=_params(1),
    )(parts, w, m, v)


def _sum_adamw_peers(me, own, parts, w, m, v, name, replicated):
    R, C = w.shape
    tr = max(t for t in (128, 64, 32, 16, 8) if R % t == 0)

    def body(me_ref, own_ref, p_ref, w_ref, m_ref, v_ref, g_ref, d_ref, m2_ref, v2_ref):
        if replicated:
            mine = me_ref[0]
            g = None
            for j in range(N_DEV):
                k = jnp.maximum(jnp.bitwise_xor(mine, j) - 1, 0)
                term = jnp.where(mine == j, own_ref[...], p_ref[k])
                g = term if g is None else g + term
        else:
            g = own_ref[...].astype(F32)
            for k in range(N_DEV - 1):
                g = g + p_ref[k].astype(F32)
        g_ref[...] = g
        d_ref[...], m2_ref[...], v2_ref[...] = _adamw_math(g, w_ref[...], m_ref[...], v_ref[...])

    blk = pl.BlockSpec((tr, C), lambda i, me_ref: (i, 0))
    own_spec = blk if replicated else pl.BlockSpec((None, tr, C), lambda i, me_ref: (me_ref[0], i, 0))
    return pl.pallas_call(
        body, name=name,
        grid_spec=pltpu.PrefetchScalarGridSpec(
            num_scalar_prefetch=1, grid=(R // tr,),
            in_specs=[own_spec, pl.BlockSpec((N_DEV - 1, tr, C), lambda i, me_ref: (0, i, 0)), blk, blk, blk],
            out_specs=[blk] * 4),
        out_shape=[SDS((R, C), F32)] * 4,
        compiler_params=_params(1),
    )(me, own, parts, w, m, v)


SMALL = ("ln_v_gain", "ln_v_bias", "w_spatial", "b_spatial", "sinks", "norm_mix_post", "norm_ff_pre", "norm_ff_post")
SMALL_ROWS = {"ln_v_gain": 8, "ln_v_bias": 8, "w_spatial": 1024, "b_spatial": 8, "sinks": 8,
              "norm_mix_post": 8, "norm_ff_pre": 8, "norm_ff_post": 8}
SMALL_PACK_ROWS = 1152


def _pack_small(vals):
    rows = []
    for name in SMALL:
        flat = vals[name].reshape(-1)
        pad = SMALL_ROWS[name] * 128 - flat.shape[0]
        if pad:
            flat = jnp.concatenate([flat, jnp.zeros((pad,), F32)])
        rows.append(flat.reshape(SMALL_ROWS[name], 128))
    rows.append(jnp.zeros((SMALL_PACK_ROWS - sum(SMALL_ROWS.values()), 128), F32))
    return jnp.concatenate(rows, axis=0)


def _unpack_small(packed, shapes):
    out, r = {}, 0
    for name in SMALL:
        n = 1
        for s in shapes[name]:
            n *= s
        out[name] = packed[r:r + SMALL_ROWS[name]].reshape(-1)[:n].reshape(shapes[name])
        r += SMALL_ROWS[name]
    return out


def _rope_rows():
    d = jnp.arange(128) % HEAD
    inv = ROPE_THETA ** (-(2.0 * (d % (ROPE // 2))).astype(F32) / ROPE)
    invf = jnp.where(d < ROPE, inv, 0.0).astype(F32).reshape(1, 128)
    sgn = jnp.where(d < ROPE // 2, -1.0, jnp.where(d < ROPE, 1.0, 0.0)).astype(F32).reshape(1, 128)
    return invf, sgn


def kernel(x, positions, w_in, ln_v_gain, ln_v_bias, w_spatial, b_spatial, sinks, w_a, w_b, w_o, norm_mix_pre, norm_mix_post, w_ff_in, w_ff_out, norm_ff_pre, norm_ff_post, loss_target, m_w_in, m_ln_v_gain, m_ln_v_bias, m_w_spatial, m_b_spatial, m_sinks, m_w_a, m_w_b, m_w_o, m_norm_mix_pre, m_norm_mix_post, m_w_ff_in, m_w_ff_out, m_norm_ff_pre, m_norm_ff_post, v_w_in, v_ln_v_gain, v_ln_v_bias, v_w_spatial, v_b_spatial, v_sinks, v_w_a, v_w_b, v_w_o, v_norm_mix_pre, v_norm_mix_post, v_w_ff_in, v_w_ff_out, v_norm_ff_pre, v_norm_ff_post):
    given = dict(locals())
    T = x.shape[1]
    xt = x[0]
    tgt = loss_target[0]
    bst = b_spatial[0].T
    ws = w_spatial[0]

    me = 4 * lax.axis_index("x") + 2 * lax.axis_index("y") + lax.axis_index("c")
    me_arr = me.astype(jnp.int32).reshape(1)

    def with_own(zone, shard):
        return lax.dynamic_update_slice(zone, shard[None], (me,) + (0,) * shard.ndim)

    rest = ("w_a", "w_b", "w_o", "w_ff_in", "w_ff_out")
    shard = {n: given[n][0].astype(BF16) for n in rest}
    g_one = _gather_first_leg(w_in[0].T.astype(BF16), "gather_in_start")
    cos, sin = _rope_tables(positions.astype(F32).reshape(T, 1), *_rope_rows(), after=g_one[-1])
    h = _rms_pre(xt, norm_mix_pre, after=cos)
    (own_win,), (win8,) = _wait_copies(g_one, h, "gather_in_wait", count=4)
    g_two = _gather_second_leg(win8, "gather_in_pass_start")
    g_rest = _start_copies([shard[n] for n in rest], [GATHER] * len(rest), "gather_rest_start", after=g_two[-1])
    _, (win8,) = _wait_copies(g_two, g_rest[-1], "gather_in_pass_wait", count=3)
    win = with_own(win8, own_win).reshape(IN_W, D)

    proj = _fwd_in(h, win)
    a = _fwd_sgu(proj, ln_v_gain, ln_v_bias, ws, bst)
    att = _fwd_attn(proj, cos, sin, sinks[0])
    gw = {n: with_own(z, own) for n, own, z in zip(rest, *_wait_copies(g_rest, att, "gather_rest_wait"))}
    wa, wb, wo = (gw[n].reshape(D, D) for n in ("w_a", "w_b", "w_o"))
    wfi3 = gw["w_ff_in"]
    wfo = gw["w_ff_out"].reshape(D_FF, D)
    merged, a2, b2, mix, x1, hf = _fwd_mix(a, att, proj, xt, wa, wb, wo, norm_mix_post, norm_ff_pre)
    f, dy, dff, dg3, loss_part = _fwd_ff(hf, wfi3, wfo, x1, tgt, norm_ff_post)

    df, dx1, dmix, dg2, dg1 = _bwd_ff(dff, f, wfi3, wfo, x1, dy, mix, norm_mix_post, norm_ff_pre)
    dwfi3, dwfo = _wgrad_ff(hf, df, f, dff)
    own_ff = [dwfi3, dwfo.reshape(N_DEV, D_FF // N_DEV, D)]
    x_ff = _start_copies(own_ff, [SCATTER] * 2, "exchange_ff_start")
    da2, db2, dgate, da, datt = _bwd_mix(dmix, proj, a2, b2, wo, wa, wb, after=x_ff[-1])
    dwo, dwa, dwb = _wgrad_mix(merged, dmix, a, da2, att, db2)
    own_mix = [g.reshape(N_DEV, D // N_DEV, D) for g in (dwa, dwb, dwo)]
    x_mix = _start_copies(own_mix, [SCATTER] * 3, "exchange_mix_start")
    dq, dkv, dsink = _bwd_attn(proj, cos, sin, sinks[0], datt, after=x_mix[-1])
    duv, dws, dbs, dlng, dlnb = _bwd_sgu(proj, da, ln_v_gain, ln_v_bias, ws, bst)
    small_grads = {"ln_v_gain": dlng, "ln_v_bias": dlnb, "w_spatial": dws, "b_spatial": dbs, "sinks": dsink[:, :N_Q],
                   "norm_mix_post": dg1, "norm_ff_pre": dg2, "norm_ff_post": dg3}
    x_small = _start_copies([_pack_small(small_grads)], [SPREAD], "exchange_small_start")
    dwin = _wgrad_in(h, duv, dq, dkv, dgate)
    own_in = [dwin.reshape(N_DEV, IN_W // N_DEV, D)]
    x_in = _start_copies(own_in, [SCATTER], "exchange_in_start", after=x_small[-1])
    grad_x, dg0 = _bwd_in(duv, dq, dkv, dgate, win, xt, dx1, norm_mix_pre, after=x_in[-1])

    results = {}

    def update(n, own, parts, transposed=False):
        state = [given[k + n][0].T if transposed else given[k + n][0] for k in ("", "m_", "v_")]
        res = _sum_adamw_peers(me_arr, own, parts, *state, "adamw_" + n, False)
        results[n] = [(r.T if transposed else r).reshape(given[n].shape) for r in res]

    own_ff, p_ff = _wait_copies(x_ff, grad_x, "exchange_ff_wait")
    update("w_ff_in", own_ff[0], p_ff[0])
    update("w_ff_out", own_ff[1], p_ff[1])
    own_mix, p_mix = _wait_copies(x_mix, results["w_ff_out"][0], "exchange_mix_wait")
    for n, own, parts in zip(("w_a", "w_b", "w_o"), own_mix, p_mix):
        update(n, own, parts)
    tail = jnp.concatenate([dg0.reshape(8, 128), jnp.tile(loss_part, (8, 1))], axis=0)
    (tail_all,) = _all_to_all([tail], [True], "exchange_tail", after=results["w_o"][0])
    dg0_all = tail_all[:, :8]
    own_small, p_small = _wait_copies(x_small, tail_all, "exchange_small_wait")
    own_in, p_in = _wait_copies(x_in, p_small[0], "exchange_in_wait")
    update("w_in", own_in[0], p_in[0], transposed=True)
    packed = _sum_adamw_peers(me_arr, own_small[0], p_small[0], _pack_small({n: given[n] for n in SMALL}),
                              _pack_small({n: given["m_" + n] for n in SMALL}),
                              _pack_small({n: given["v_" + n] for n in SMALL}), "adamw_small", True)
    shapes = {n: given[n].shape for n in SMALL}
    unpacked = [_unpack_small(p, shapes) for p in packed]
    for n in SMALL:
        results[n] = [u[n] for u in unpacked]
    n = "norm_mix_pre"
    results[n] = [r.reshape(given[n].shape) for r in _sum_adamw(
        dg0_all, given[n].reshape(8, 128), given["m_" + n].reshape(8, 128), given["v_" + n].reshape(8, 128), "adamw_" + n)]

    loss = jnp.sum(tail_all[:, 8, 0])
    order = ("w_in", "ln_v_gain", "ln_v_bias", "w_spatial", "b_spatial", "sinks", "w_a", "w_b", "w_o", "norm_mix_pre",
             "norm_mix_post", "w_ff_in", "w_ff_out", "norm_ff_pre", "norm_ff_post")
    out = [loss, grad_x.reshape(x.shape)]
    for k in range(4):
        out += [results[n][k] for n in order]
    return tuple(out)
```

```python
import functools

import jax
import jax.numpy as jnp
from jax import lax
from jax.experimental import pallas as pl
from jax.experimental.pallas import tpu as pltpu

F32 = jnp.float32
BF16 = jnp.bfloat16

N_DEV = 8
D = 1024
D_FF = 4096
IN_W = 5632
CHUNK = 128
GROUPS = 8
HEAD = 64
N_Q = 16
N_KV = 4
ROPE = 16
ROPE_THETA = 500000.0
EPS = 1e-6
OFF_Q, OFF_K, OFF_VA, OFF_GA, OFF_GB = 2048, 3072, 3328, 3584, 4608

ADAM_LR = 0.001
ADAM_B1 = 0.9
ADAM_B2 = 0.999
ADAM_EPS = 1e-08
ADAM_WD = 0.01
ADAM_STEP = 10

VMEM_LIMIT = 56 * 1024 * 1024

SDS = jax.ShapeDtypeStruct
MESH = pl.DeviceIdType.MESH


def _params(n_axes=None):
    if n_axes is None:
        return pltpu.CompilerParams(vmem_limit_bytes=VMEM_LIMIT)
    return pltpu.CompilerParams(dimension_semantics=("arbitrary",) * n_axes, vmem_limit_bytes=VMEM_LIMIT)


def _nt(a, b):
    return lax.dot_general(a, b, (((1,), (1,)), ((), ())), preferred_element_type=F32)


def _tn(a, b):
    return lax.dot_general(a, b, (((0,), (0,)), ((), ())), preferred_element_type=F32)


def _nn(a, b):
    return jnp.dot(a, b, preferred_element_type=F32)


def _gelu(x):
    t = jnp.tanh(0.7978845608028654 * (x + 0.044715 * (x * x * x)))
    return 0.5 * x * (1.0 + t), t


def _gelu_grad(x, t):
    return 0.5 * (1.0 + t) + 0.5 * x * (1.0 - t * t) * (0.7978845608028654 * (1.0 + 3.0 * 0.044715 * x * x))


def _sigmoid(x):
    return 1.0 / (1.0 + jnp.exp(-x))


def _rms_stats(v):
    r = lax.rsqrt(jnp.mean(v * v, axis=-1, keepdims=True) + EPS)
    return r, v * r


def _rms_bwd(d, vhat, r, g):
    gd = g * d
    return r * (gd - vhat * jnp.mean(gd * vhat, axis=-1, keepdims=True))


def _colsum(v):
    return jnp.sum(v, axis=0, keepdims=True)


_ANY = pl.BlockSpec(memory_space=pl.ANY)


def _after(body, n_in, after):
    if after is None:
        return body, [], []

    def ordered(*refs):
        return body(*refs[:n_in], *refs[n_in + 1:])

    return ordered, [_ANY], [after]


def _rms_pre(x, g0, after=None):
    T = x.shape[0]
    tm = min(T, 1024)

    def body(x_ref, g_ref, h_ref):
        _, xh = _rms_stats(x_ref[...])
        h_ref[...] = (xh * g_ref[...]).astype(BF16)

    body, dep_specs, deps = _after(body, 2, after)
    return pl.pallas_call(
        body, name="rms_pre", grid=(T // tm,),
        in_specs=[pl.BlockSpec((tm, D), lambda i: (i, 0)), pl.BlockSpec((1, D), lambda i: (0, 0))] + dep_specs,
        out_specs=pl.BlockSpec((tm, D), lambda i: (i, 0)),
        out_shape=SDS((T, D), BF16),
        compiler_params=_params(1),
    )(x, g0, *deps)


def _fwd_in(h, win_t):
    T = h.shape[0]
    tm, tn = min(T, 1024), 1408

    def body(h_ref, w_ref, p_ref):
        p_ref[...] = _nt(h_ref[...], w_ref[...]).astype(BF16)

    return pl.pallas_call(
        body, name="fwd_in", grid=(T // tm, IN_W // tn),
        in_specs=[pl.BlockSpec((tm, D), lambda i, j: (i, 0)), pl.BlockSpec((tn, D), lambda i, j: (j, 0))],
        out_specs=pl.BlockSpec((tm, tn), lambda i, j: (i, j)),
        out_shape=SDS((T, IN_W), BF16),
        compiler_params=_params(2),
    )(h, win_t)


def _sgu_forward_parts(u_ref, vs_ref, lng_ref, lnb_ref):
    u = u_ref[...].astype(F32)
    vs = vs_ref[...].astype(F32)
    gu, tu = _gelu(u)
    gv, tv = _gelu(vs)
    mu = jnp.mean(gv, axis=-1, keepdims=True)
    dv = gv - mu
    rstd = lax.rsqrt(jnp.mean(dv * dv, axis=-1, keepdims=True) + EPS)
    vhat = dv * rstd
    vn = (vhat * lng_ref[...] + lnb_ref[...]).astype(BF16)
    return u, vs, gu, tu, tv, rstd, vhat, vn


def _masked_ws(ws_ref, g):
    row = lax.broadcasted_iota(jnp.int32, (CHUNK, CHUNK), 0)
    col = lax.broadcasted_iota(jnp.int32, (CHUNK, CHUNK), 1)
    return jnp.where(row >= col, ws_ref[g], 0.0).astype(BF16)


def _fwd_sgu(proj, lng, lnb, ws, bst, after=None):
    T = proj.shape[0]
    tc = min(T, 512)

    def body(u_ref, vs_ref, lng_ref, lnb_ref, ws_ref, bst_ref, a_ref):
        _, _, gu, _, _, _, _, vn = _sgu_forward_parts(u_ref, vs_ref, lng_ref, lnb_ref)
        for g in range(GROUPS):
            wm = _masked_ws(ws_ref, g)
            cols = slice(g * CHUNK, (g + 1) * CHUNK)
            for c in range(tc // CHUNK):
                rows = slice(c * CHUNK, (c + 1) * CHUNK)
                mixed = _nn(wm, vn[rows, cols]) + bst_ref[:, g:g + 1]
                a_ref[rows, cols] = (gu[rows, cols] * mixed).astype(BF16)

    body, dep_specs, deps = _after(body, 6, after)
    return pl.pallas_call(
        body, name="fwd_sgu", grid=(T // tc,),
        in_specs=[pl.BlockSpec((tc, D), lambda i: (i, 0)), pl.BlockSpec((tc, D), lambda i: (i, 1)),
                  pl.BlockSpec((1, D), lambda i: (0, 0)), pl.BlockSpec((1, D), lambda i: (0, 0)),
                  pl.BlockSpec((GROUPS, CHUNK, CHUNK), lambda i: (0, 0, 0)),
                  pl.BlockSpec((CHUNK, GROUPS), lambda i: (0, 0))] + dep_specs,
        out_specs=pl.BlockSpec((tc, D), lambda i: (i, 0)),
        out_shape=SDS((T, D), BF16),
        compiler_params=_params(1),
    )(proj, proj, lng, lnb, ws, bst, *deps)


def _rope_tables(posf, invf, sgn, after=None):
    T = posf.shape[0]
    tr = min(T, 1024)

    def body(pos_ref, invf_ref, sgn_ref, c_ref, s_ref):
        ang = pos_ref[...] * invf_ref[...]
        c_ref[...] = jnp.cos(ang)
        s = jnp.sin(ang)
        s_ref[:, :128] = jnp.where(sgn_ref[...] < 0.0, -s, 0.0)
        s_ref[:, 128:] = jnp.where(sgn_ref[...] > 0.0, s, 0.0)

    body, dep_specs, deps = _after(body, 3, after)
    return pl.pallas_call(
        body, name="rope_tables", grid=(T // tr,),
        in_specs=[pl.BlockSpec((tr, 1), lambda i: (i, 0)), pl.BlockSpec((1, 128), lambda i: (0, 0)),
                  pl.BlockSpec((1, 128), lambda i: (0, 0))] + dep_specs,
        out_specs=[pl.BlockSpec((tr, 128), lambda i: (i, 0)), pl.BlockSpec((tr, 256), lambda i: (i, 0))],
        out_shape=[SDS((T, 128), F32), SDS((T, 256), F32)],
        compiler_params=_params(1),
    )(posf, invf, sgn, *deps)


def _rope(v, c, s_lo, s_hi):
    n = v.shape[1]
    return v * c + pltpu.roll(v, n - ROPE // 2, 1) * s_lo + pltpu.roll(v, ROPE // 2, 1) * s_hi


def _rope_bwd(dv, c, s_lo, s_hi):
    n = dv.shape[1]
    return dv * c + pltpu.roll(dv * s_lo, ROPE // 2, 1) + pltpu.roll(dv * s_hi, n - ROPE // 2, 1)


def _fold_masks(first):
    jj = lax.broadcasted_iota(jnp.int32, (CHUNK, CHUNK), 0)
    t = lax.broadcasted_iota(jnp.int32, (CHUNK, CHUNK), 1)
    prev = jj > t
    return prev, jnp.where(prev & first, -1e30, 0.0)


def _fold(band, prev):
    return jnp.where(prev, band[:CHUNK], band[CHUNK:])


def _unfold(folded, prev):
    return jnp.concatenate([jnp.where(prev, folded, 0.0), jnp.where(prev, 0.0, folded)], axis=0)


def _softmax_sink(s, sink, key_axis):
    m = jnp.maximum(jnp.max(s, axis=key_axis, keepdims=True), sink)
    p = jnp.exp(s - m)
    esink = jnp.exp(sink - m)
    inv = 1.0 / (jnp.sum(p, axis=key_axis, keepdims=True) + esink)
    return p * inv, esink * inv


def _head_pair_operand(band, g):
    slab = band[:, (g // 2) * 128:(g // 2 + 1) * 128]
    lo = lax.broadcasted_iota(jnp.int32, slab.shape, 1) < HEAD
    if g % 2 == 0:
        first = jnp.where(lo, slab, 0.0)
        second = pltpu.roll(first, HEAD, 1)
    else:
        second = jnp.where(lo, 0.0, slab)
        first = pltpu.roll(second, HEAD, 1)
    return jnp.concatenate([first, second], axis=0).astype(BF16)


def _head_pair_gradient(acc, g):
    top, bot = acc[:2 * CHUNK], acc[2 * CHUNK:]
    lo = lax.broadcasted_iota(jnp.int32, top.shape, 1) < HEAD
    if g % 2 == 0:
        return jnp.where(lo, top, 0.0) + pltpu.roll(jnp.where(lo, 0.0, bot), HEAD, 1)
    return pltpu.roll(jnp.where(lo, top, 0.0), HEAD, 1) + jnp.where(lo, 0.0, bot)


def _attn_specs(nb, clamp):
    cur = (lambda i: jnp.minimum(i, nb - 1)) if clamp else (lambda i: i)
    prev = lambda i: jnp.maximum(jnp.minimum(i, nb - 1) - 1, 0)
    kw = N_KV * HEAD
    return cur, prev, [
        pl.BlockSpec((CHUNK, D), lambda i: (cur(i), OFF_Q // D)),
        pl.BlockSpec((CHUNK, kw), lambda i: (prev(i), OFF_K // kw)),
        pl.BlockSpec((CHUNK, kw), lambda i: (cur(i), OFF_K // kw)),
        pl.BlockSpec((CHUNK, kw), lambda i: (prev(i), OFF_VA // kw)),
        pl.BlockSpec((CHUNK, kw), lambda i: (cur(i), OFF_VA // kw)),
        pl.BlockSpec((CHUNK, 128), lambda i: (prev(i), 0)),
        pl.BlockSpec((CHUNK, 128), lambda i: (cur(i), 0)),
        pl.BlockSpec((CHUNK, 256), lambda i: (prev(i), 0)),
        pl.BlockSpec((CHUNK, 256), lambda i: (cur(i), 0)),
        pl.BlockSpec(memory_space=pltpu.SMEM),
    ]


def _attn_load(q_ref, kp_ref, kc_ref, vp_ref, vc_ref, cp_ref, cc_ref, sp_ref, sc_ref):
    q_tab = [jnp.tile(t, (1, D // 128)) for t in (cc_ref[...], sc_ref[:, :128], sc_ref[:, 128:])]
    band = lambda p_ref, c_ref, cols: jnp.concatenate([p_ref[:, cols], c_ref[:, cols]], axis=0)
    k_tab = [jnp.tile(t, (1, N_KV * HEAD // 128)) for t in (
        band(cp_ref, cc_ref, slice(0, 128)), band(sp_ref, sc_ref, slice(0, 128)), band(sp_ref, sc_ref, slice(128, 256)))]
    q = (_rope(q_ref[...].astype(F32), *q_tab) * (HEAD ** -0.5)).astype(BF16)
    kb = _rope(jnp.concatenate([kp_ref[...], kc_ref[...]], axis=0).astype(F32), *k_tab)
    vb = jnp.concatenate([vp_ref[...], vc_ref[...]], axis=0).astype(F32)
    return q, kb, vb, (q_tab, k_tab)


PAIRS_PER_KV = N_Q // N_KV // 2


def _fwd_attn(proj, cos, sin, sinks):
    T = proj.shape[0]
    nb = T // CHUNK
    _, _, specs = _attn_specs(nb, False)

    def body(q_ref, kp_ref, kc_ref, vp_ref, vc_ref, cp_ref, cc_ref, sp_ref, sc_ref, sink_ref, o_ref):
        q, kb, vb, _ = _attn_load(q_ref, kp_ref, kc_ref, vp_ref, vc_ref, cp_ref, cc_ref, sp_ref, sc_ref)
        prev, bias = _fold_masks(pl.program_id(0) == 0)
        for g in range(N_KV):
            k2 = _head_pair_operand(kb, g)
            v2 = _head_pair_operand(vb, g)
            for r in range(PAIRS_PER_KV):
                pair = g * PAIRS_PER_KV + r
                s2 = _nt(k2, q[:, pair * 128:(pair + 1) * 128])
                ps = []
                for e in range(2):
                    s = _fold(s2[e * 2 * CHUNK:(e + 1) * 2 * CHUNK], prev) + bias
                    ps.append(_unfold(_softmax_sink(s, sink_ref[2 * pair + e], 0)[0], prev).astype(BF16))
                o_ref[:, pair * 128:(pair + 1) * 128] = _tn(jnp.concatenate(ps, axis=0), v2).astype(BF16)

    return pl.pallas_call(
        body, name="fwd_attn", grid=(nb,), in_specs=specs,
        out_specs=pl.BlockSpec((CHUNK, D), lambda i: (i, 0)),
        out_shape=SDS((T, D), BF16),
        compiler_params=_params(1),
    )(proj, proj, proj, proj, proj, cos, cos, sin, sin, sinks)


def _row_halves(tm):
    return [slice(0, tm // 2), slice(tm // 2, tm)] if tm % 32 == 0 else [slice(0, tm)]


def _fwd_mix(a, att, proj, x, wa, wb, wo, g1, g2):
    T = x.shape[0]
    tm = min(T, 512)
    half = D // 2

    def body(a_ref, att_ref, ga0, ga1, gb0, gb1, x_ref, wa_ref, wb_ref, wo_ref, g1_ref, g2_ref,
             mg_ref, a2_ref, b2_ref, mix_ref, x1_ref, hf_ref):
        for rows in _row_halves(tm):
            a2 = _nn(a_ref[rows, :], wa_ref[...])
            b2 = _nn(att_ref[rows, :], wb_ref[...])
            ga = jnp.concatenate([ga0[rows, :], ga1[rows, :]], axis=1).astype(F32)
            gb = jnp.concatenate([gb0[rows, :], gb1[rows, :]], axis=1).astype(F32)
            merged = (_sigmoid(ga) * a2 + _sigmoid(gb) * b2).astype(BF16)
            a2_ref[rows, :] = a2.astype(BF16)
            b2_ref[rows, :] = b2.astype(BF16)
            mg_ref[rows, :] = merged
            mix = _nn(merged, wo_ref[...])
            mix_ref[rows, :] = mix
            _, mh = _rms_stats(mix)
            x1 = x_ref[rows, :] + mh * g1_ref[...]
            x1_ref[rows, :] = x1
            _, xh = _rms_stats(x1)
            hf_ref[rows, :] = (xh * g2_ref[...]).astype(BF16)

    row = lambda i: (i, 0)
    const = lambda i: (0, 0)
    gspec = lambda off: pl.BlockSpec((tm, half), lambda i: (i, off // half))
    return pl.pallas_call(
        body, name="fwd_mix", grid=(T // tm,),
        in_specs=[pl.BlockSpec((tm, D), row), pl.BlockSpec((tm, D), row),
                  gspec(OFF_GA), gspec(OFF_GA + half), gspec(OFF_GB), gspec(OFF_GB + half),
                  pl.BlockSpec((tm, D), row), _resident((D, D)), _resident((D, D)),
                  _resident((D, D)), pl.BlockSpec((1, D), const), pl.BlockSpec((1, D), const)],
        out_specs=[pl.BlockSpec((tm, D), row)] * 6,
        out_shape=[SDS((T, D), BF16), SDS((T, D), BF16), SDS((T, D), BF16), SDS((T, D), F32), SDS((T, D), F32),
                   SDS((T, D), BF16)],
        compiler_params=_params(1),
    )(a, att, proj, proj, proj, proj, x, wa, wb, wo, g1, g2)


FF_SPLIT = N_DEV
FF_TILE = D_FF // FF_SPLIT
FF_STEP = 2048
FF_SLABS = FF_STEP // FF_TILE
FF_STEPS = D_FF // FF_STEP


def _fwd_ff(hf, wfi3, wfo, x1, tgt, g3):
    T = hf.shape[0]
    tm = min(T, 512)
    last = FF_STEPS - 1

    def body(hf_ref, wfi_ref, wfo_ref, x1_ref, tgt_ref, g3_ref, f_ref, dy_ref, dff_ref, dg3_ref, loss_ref, acc, r_s):
        i, p = pl.program_id(0), pl.program_id(1)

        @pl.when((i == 0) & (p == 0))
        def _():
            dg3_ref[...] = jnp.zeros_like(dg3_ref)
            loss_ref[...] = jnp.zeros_like(loss_ref)

        hf_t = hf_ref[...]
        for s in range(FF_SLABS):
            cols = slice(s * FF_TILE, (s + 1) * FF_TILE)
            f = _nn(hf_t, wfi_ref[p * FF_SLABS + s]).astype(BF16)
            f_ref[:, cols] = f
            rl = jnp.maximum(f.astype(F32), 0.0)
            r_s[:, cols] = (rl * rl).astype(BF16)
        part = _nn(r_s[...], wfo_ref[pl.ds(pl.multiple_of(p * FF_STEP, FF_STEP), FF_STEP), :])

        @pl.when(p == 0)
        def _():
            acc[...] = part

        @pl.when(p > 0)
        def _():
            acc[...] += part

        @pl.when(p == last)
        def _():
            r3, fh = _rms_stats(acc[...])
            e = x1_ref[...] + fh * g3_ref[...] - tgt_ref[...]
            loss_ref[...] += jnp.sum(e * e) * (0.5 / D)
            dy = e * (1.0 / D)
            dy_ref[...] = dy
            dg3_ref[...] += _colsum(dy * fh)
            dff_ref[...] = _rms_bwd(dy, fh, r3, g3_ref[...]).astype(BF16)

    row = lambda i, p: (i, 0)
    const = lambda i, p: (0, 0)
    return pl.pallas_call(
        body, name="fwd_ff", grid=(T // tm, FF_STEPS),
        in_specs=[pl.BlockSpec((tm, D), row), _resident((FF_SPLIT, D, FF_TILE)), _resident((D_FF, D)),
                  pl.BlockSpec((tm, D), row),
                  pl.BlockSpec((tm, D), row), pl.BlockSpec((1, D), const)],
        out_specs=[pl.BlockSpec((tm, FF_STEP), lambda i, p: (i, p)), pl.BlockSpec((tm, D), row),
                   pl.BlockSpec((tm, D), row), pl.BlockSpec((1, D), const), pl.BlockSpec((1, 128), const)],
        out_shape=[SDS((T, D_FF), BF16), SDS((T, D), F32), SDS((T, D), BF16), SDS((1, D), F32), SDS((1, 128), F32)],
        scratch_shapes=[pltpu.VMEM((tm, D), F32), pltpu.VMEM((tm, FF_STEP), BF16)],
        compiler_params=_params(2),
    )(hf, wfi3, wfo, x1, tgt, g3)


def _bwd_ff(dff, f, wfi3, wfo, x1, dy, mix, g1, g2):
    T = dff.shape[0]
    tm = min(T, 512)
    last = FF_STEPS - 1

    def body(dff_ref, f_ref, wfi_ref, wfo_ref, x1_ref, dy_ref, mix_ref, g1_ref, g2_ref,
             df_ref, dx1_ref, dmix_ref, dg2_ref, dg1_ref, acc):
        i, p = pl.program_id(0), pl.program_id(1)

        @pl.when((i == 0) & (p == 0))
        def _():
            dg2_ref[...] = jnp.zeros_like(dg2_ref)
            dg1_ref[...] = jnp.zeros_like(dg1_ref)

        dr = _nt(dff_ref[...], wfo_ref[pl.ds(pl.multiple_of(p * FF_STEP, FF_STEP), FF_STEP), :])
        df_ref[...] = (dr * (2.0 * jnp.maximum(f_ref[...].astype(F32), 0.0))).astype(BF16)
        part = _nt(df_ref[:, :FF_TILE], wfi_ref[p * FF_SLABS])
        for s in range(1, FF_SLABS):
            part = part + _nt(df_ref[:, s * FF_TILE:(s + 1) * FF_TILE], wfi_ref[p * FF_SLABS + s])

        @pl.when(p == 0)
        def _():
            acc[...] = part

        @pl.when(p > 0)
        def _():
            acc[...] += part

        @pl.when(p == last)
        def _():
            dhf = acc[...]
            r2, xh = _rms_stats(x1_ref[...])
            dg2_ref[...] += _colsum(dhf * xh)
            dx1 = dy_ref[...] + _rms_bwd(dhf, xh, r2, g2_ref[...])
            dx1_ref[...] = dx1
            r1, mh = _rms_stats(mix_ref[...])
            dg1_ref[...] += _colsum(dx1 * mh)
            dmix_ref[...] = _rms_bwd(dx1, mh, r1, g1_ref[...]).astype(BF16)

    row = lambda i, p: (i, 0)
    const = lambda i, p: (0, 0)
    return pl.pallas_call(
        body, name="bwd_ff", grid=(T // tm, FF_STEPS),
        in_specs=[pl.BlockSpec((tm, D), row), pl.BlockSpec((tm, FF_STEP), lambda i, p: (i, p)),
                  _resident((FF_SPLIT, D, FF_TILE)), _resident((D_FF, D)),
                  pl.BlockSpec((tm, D), row), pl.BlockSpec((tm, D), row), pl.BlockSpec((tm, D), row),
                  pl.BlockSpec((1, D), const), pl.BlockSpec((1, D), const)],
        out_specs=[pl.BlockSpec((tm, FF_STEP), lambda i, p: (i, p)), pl.BlockSpec((tm, D), row),
                   pl.BlockSpec((tm, D), row), pl.BlockSpec((1, D), const), pl.BlockSpec((1, D), const)],
        out_shape=[SDS((T, D_FF), BF16), SDS((T, D), F32), SDS((T, D), BF16), SDS((1, D), F32), SDS((1, D), F32)],
        scratch_shapes=[pltpu.VMEM((tm, D), F32)],
        compiler_params=_params(2),
    )(dff, f, wfi3, wfo, x1, dy, mix, g1, g2)


def _wgrad_ff(hf, df, f, dff):
    T = hf.shape[0]
    tt = min(T, 1024)
    wide = 2 * FF_TILE

    def body(hf_ref, df_ref, f_ref, dff_ref, dwfi_ref, dwfo_ref, acc_i, acc_o):
        t = pl.program_id(1)

        @pl.when(t == 0)
        def _():
            acc_i[...] = jnp.zeros_like(acc_i)
            acc_o[...] = jnp.zeros_like(acc_o)

        acc_i[...] += _tn(hf_ref[...], df_ref[...])
        rl = jnp.maximum(f_ref[...].astype(F32), 0.0)
        acc_o[...] += _tn((rl * rl).astype(BF16), dff_ref[...])

        @pl.when(t == T // tt - 1)
        def _():
            dwfi_ref[0] = acc_i[:, :FF_TILE].astype(BF16)
            dwfi_ref[1] = acc_i[:, FF_TILE:].astype(BF16)
            dwfo_ref[...] = acc_o[...].astype(BF16)

    return pl.pallas_call(
        body, name="wgrad_ff", grid=(D_FF // wide, T // tt),
        in_specs=[pl.BlockSpec((tt, D), lambda p, t: (t, 0)), pl.BlockSpec((tt, wide), lambda p, t: (t, p)),
                  pl.BlockSpec((tt, wide), lambda p, t: (t, p)), pl.BlockSpec((tt, D), lambda p, t: (t, 0))],
        out_specs=[pl.BlockSpec((2, D, FF_TILE), lambda p, t: (p, 0, 0)), pl.BlockSpec((wide, D), lambda p, t: (p, 0))],
        out_shape=[SDS((FF_SPLIT, D, FF_TILE), BF16), SDS((D_FF, D), BF16)],
        scratch_shapes=[pltpu.VMEM((D, wide), F32), pltpu.VMEM((wide, D), F32)],
        compiler_params=_params(2),
    )(hf, df, f, dff)


def _bwd_mix(dmix, proj, a2, b2, wo, wa, wb, after=None):
    T = dmix.shape[0]
    tm = min(T, 512)
    half = D // 2

    def body(dmix_ref, ga0, ga1, gb0, gb1, a2_ref, b2_ref, wo_ref, wa_ref, wb_ref,
             da2_ref, db2_ref, dg_ref, da_ref, datt_ref):
        for rows in _row_halves(tm):
            dmg = _nt(dmix_ref[rows, :], wo_ref[...])
            sa = _sigmoid(jnp.concatenate([ga0[rows, :], ga1[rows, :]], axis=1).astype(F32))
            sb = _sigmoid(jnp.concatenate([gb0[rows, :], gb1[rows, :]], axis=1).astype(F32))
            da2 = (dmg * sa).astype(BF16)
            db2 = (dmg * sb).astype(BF16)
            da2_ref[rows, :] = da2
            db2_ref[rows, :] = db2
            dg_ref[rows, :D] = (dmg * a2_ref[rows, :].astype(F32) * (sa * (1.0 - sa))).astype(BF16)
            dg_ref[rows, D:] = (dmg * b2_ref[rows, :].astype(F32) * (sb * (1.0 - sb))).astype(BF16)
            da_ref[rows, :] = _nt(da2, wa_ref[...]).astype(BF16)
            datt_ref[rows, :] = _nt(db2, wb_ref[...]).astype(BF16)

    row = lambda i: (i, 0)
    const = lambda i: (0, 0)
    gspec = lambda off: pl.BlockSpec((tm, half), lambda i: (i, off // half))
    body, dep_specs, deps = _after(body, 10, after)
    return pl.pallas_call(
        body, name="bwd_mix", grid=(T // tm,),
        in_specs=[pl.BlockSpec((tm, D), row), gspec(OFF_GA), gspec(OFF_GA + half), gspec(OFF_GB), gspec(OFF_GB + half),
                  pl.BlockSpec((tm, D), row), pl.BlockSpec((tm, D), row),
                  _resident((D, D)), _resident((D, D)), _resident((D, D))] + dep_specs,
        out_specs=[pl.BlockSpec((tm, D), row), pl.BlockSpec((tm, D), row), pl.BlockSpec((tm, 2 * D), row),
                   pl.BlockSpec((tm, D), row), pl.BlockSpec((tm, D), row)],
        out_shape=[SDS((T, D), BF16), SDS((T, D), BF16), SDS((T, 2 * D), BF16), SDS((T, D), BF16), SDS((T, D), BF16)],
        compiler_params=_params(1),
    )(dmix, proj, proj, proj, proj, a2, b2, wo, wa, wb, *deps)


def _wgrad_mix(merged, dmix, a, da2, att, db2):
    T = merged.shape[0]
    tt = min(T, 512)

    def body(mg_ref, dmix_ref, a_ref, da2_ref, att_ref, db2_ref, dwo_ref, dwa_ref, dwb_ref, acc):
        t = pl.program_id(0)

        @pl.when(t == 0)
        def _():
            acc[...] = jnp.zeros_like(acc)

        acc[0] += _tn(mg_ref[...], dmix_ref[...])
        acc[1] += _tn(a_ref[...], da2_ref[...])
        acc[2] += _tn(att_ref[...], db2_ref[...])

        @pl.when(t == T // tt - 1)
        def _():
            dwo_ref[...] = acc[0].astype(BF16)
            dwa_ref[...] = acc[1].astype(BF16)
            dwb_ref[...] = acc[2].astype(BF16)

    return pl.pallas_call(
        body, name="wgrad_mix", grid=(T // tt,),
        in_specs=[pl.BlockSpec((tt, D), lambda t: (t, 0))] * 6,
        out_specs=[pl.BlockSpec((D, D), lambda t: (0, 0))] * 3,
        out_shape=[SDS((D, D), BF16)] * 3,
        scratch_shapes=[pltpu.VMEM((3, D, D), F32)],
        compiler_params=_params(1),
    )(merged, dmix, a, da2, att, db2)


def _bwd_attn(proj, cos, sin, sinks, datt, after=None):
    T = proj.shape[0]
    nb = T // CHUNK
    kw = N_KV * HEAD
    cur, prev, specs = _attn_specs(nb, True)

    def body(q_ref, kp_ref, kc_ref, vp_ref, vc_ref, cp_ref, cc_ref, sp_ref, sc_ref, sink_ref, do_ref,
             dq_ref, dkv_ref, dsink_ref, carry_k, carry_v, dq_acc):
        i = pl.program_id(0)

        @pl.when(i == 0)
        def _():
            carry_k[...] = jnp.zeros_like(carry_k)
            carry_v[...] = jnp.zeros_like(carry_v)
            dsink_ref[...] = jnp.zeros_like(dsink_ref)

        @pl.when(i < nb)
        def _():
            q, kb, vb, (q_tab, k_tab) = _attn_load(q_ref, kp_ref, kc_ref, vp_ref, vc_ref, cp_ref, cc_ref,
                                                   sp_ref, sc_ref)
            prev, bias = _fold_masks(i == 0)
            do = do_ref[...]
            lane = lax.broadcasted_iota(jnp.int32, (1, 128), 1)
            dsink = jnp.zeros((1, 128), F32)
            dks, dvs = [], []
            for g in range(N_KV):
                k2 = _head_pair_operand(kb, g)
                v2 = _head_pair_operand(vb, g)
                dk2 = jnp.zeros((4 * CHUNK, 128), F32)
                dv2 = jnp.zeros((4 * CHUNK, 128), F32)
                for r in range(PAIRS_PER_KV):
                    pair = g * PAIRS_PER_KV + r
                    qp = q[:, pair * 128:(pair + 1) * 128]
                    dop = do[:, pair * 128:(pair + 1) * 128]
                    s2 = _nt(k2, qp)
                    dp2 = _nt(v2, dop)
                    ps, dss = [], []
                    for e in range(2):
                        rows = slice(e * 2 * CHUNK, (e + 1) * 2 * CHUNK)
                        p, psink = _softmax_sink(_fold(s2[rows], prev) + bias, sink_ref[2 * pair + e], 0)
                        dp = _fold(dp2[rows], prev)
                        delta = jnp.sum(p * dp, axis=0, keepdims=True)
                        ps.append(_unfold(p, prev).astype(BF16))
                        dss.append(_unfold(p * (dp - delta), prev).astype(BF16))
                        dsink = dsink + jnp.where(lane == 2 * pair + e, -jnp.sum(psink * delta), 0.0)
                    ds2 = jnp.concatenate(dss, axis=0)
                    dq_acc[:, pair * 128:(pair + 1) * 128] = _tn(ds2, k2) * (HEAD ** -0.5)
                    dk2 = dk2 + _nn(ds2, qp)
                    dv2 = dv2 + _nn(jnp.concatenate(ps, axis=0), dop)
                dks.append(_head_pair_gradient(dk2, g))
                dvs.append(_head_pair_gradient(dv2, g))
            dsink_ref[...] += dsink
            dq_ref[...] = _rope_bwd(dq_acc[...], *q_tab).astype(BF16)
            dkb = _rope_bwd(jnp.concatenate([dks[0] + dks[1], dks[2] + dks[3]], axis=1), *k_tab)
            dvb = jnp.concatenate([dvs[0] + dvs[1], dvs[2] + dvs[3]], axis=1)
            dkv_ref[:, :kw] = (carry_k[...] + dkb[:CHUNK]).astype(BF16)
            dkv_ref[:, kw:] = (carry_v[...] + dvb[:CHUNK]).astype(BF16)
            carry_k[...] = dkb[CHUNK:]
            carry_v[...] = dvb[CHUNK:]

        @pl.when(i == nb)
        def _():
            dkv_ref[:, :kw] = carry_k[...].astype(BF16)
            dkv_ref[:, kw:] = carry_v[...].astype(BF16)

    body, dep_specs, deps = _after(body, 11, after)
    return pl.pallas_call(
        body, name="bwd_attn", grid=(nb + 1,),
        in_specs=specs + [pl.BlockSpec((CHUNK, D), lambda i: (cur(i), 0))] + dep_specs,
        out_specs=[pl.BlockSpec((CHUNK, D), lambda i: (cur(i), 0)),
                   pl.BlockSpec((CHUNK, 2 * kw), lambda i: (jnp.maximum(i - 1, 0), 0)),
                   pl.BlockSpec((1, 128), lambda i: (0, 0))],
        out_shape=[SDS((T, D), BF16), SDS((T, 2 * kw), BF16), SDS((1, 128), F32)],
        scratch_shapes=[pltpu.VMEM((CHUNK, kw), F32), pltpu.VMEM((CHUNK, kw), F32), pltpu.VMEM((CHUNK, D), F32)],
        compiler_params=_params(1),
    )(proj, proj, proj, proj, proj, cos, cos, sin, sin, sinks, datt, *deps)


def _bwd_sgu(proj, da, lng, lnb, ws, bst):
    T = proj.shape[0]
    tc = min(T, 512)
    nsteps = T // tc

    def body(u_ref, vs_ref, da_ref, lng_ref, lnb_ref, ws_ref, bst_ref,
             duv_ref, dws_ref, dbs_ref, dlng_ref, dlnb_ref, dvn_s, dgu_s, dmx_sum):
        i = pl.program_id(0)

        @pl.when(i == 0)
        def _():
            dws_ref[...] = jnp.zeros_like(dws_ref)
            dlng_ref[...] = jnp.zeros_like(dlng_ref)
            dlnb_ref[...] = jnp.zeros_like(dlnb_ref)
            dmx_sum[...] = jnp.zeros_like(dmx_sum)

        u, vs, gu, tu, tv, rstd, vhat, vn = _sgu_forward_parts(u_ref, vs_ref, lng_ref, lnb_ref)
        da = da_ref[...].astype(F32)
        for g in range(GROUPS):
            wm = _masked_ws(ws_ref, g)
            cols = slice(g * CHUNK, (g + 1) * CHUNK)
            dws = jnp.zeros((CHUNK, CHUNK), F32)
            dsum = jnp.zeros((CHUNK, CHUNK), F32)
            for c in range(tc // CHUNK):
                rows = slice(c * CHUNK, (c + 1) * CHUNK)
                vn_cg = vn[rows, cols]
                mixed = _nn(wm, vn_cg) + bst_ref[:, g:g + 1]
                dgu_s[rows, cols] = da[rows, cols] * mixed
                dmx = da[rows, cols] * gu[rows, cols]
                dmxb = dmx.astype(BF16)
                dws = dws + _nt(dmxb, vn_cg)
                dsum = dsum + dmx
                dvn_s[rows, cols] = _tn(wm, dmxb)
            dws_ref[g] += dws
            dmx_sum[:, cols] += dsum
        dvn = dvn_s[...]
        dlng_ref[...] += _colsum(dvn * vhat)
        dlnb_ref[...] += _colsum(dvn)
        dvh = dvn * lng_ref[...]
        dgv = rstd * (dvh - jnp.mean(dvh, axis=-1, keepdims=True) - vhat * jnp.mean(dvh * vhat, axis=-1, keepdims=True))
        duv_ref[:, :D] = (dgu_s[...] * _gelu_grad(u, tu)).astype(BF16)
        duv_ref[:, D:] = (dgv * _gelu_grad(vs, tv)).astype(BF16)

        @pl.when(i == nsteps - 1)
        def _():
            row = lax.broadcasted_iota(jnp.int32, (CHUNK, CHUNK), 0)
            col = lax.broadcasted_iota(jnp.int32, (CHUNK, CHUNK), 1)
            for g in range(GROUPS):
                dws_ref[g] = jnp.where(row >= col, dws_ref[g], 0.0)
                dbs_ref[g:g + 1, :] = _colsum(dmx_sum[:, g * CHUNK:(g + 1) * CHUNK].T)

    const2 = lambda i: (0, 0)
    return pl.pallas_call(
        body, name="bwd_sgu", grid=(nsteps,),
        in_specs=[pl.BlockSpec((tc, D), lambda i: (i, 0)), pl.BlockSpec((tc, D), lambda i: (i, 1)),
                  pl.BlockSpec((tc, D), lambda i: (i, 0)), pl.BlockSpec((1, D), const2), pl.BlockSpec((1, D), const2),
                  pl.BlockSpec((GROUPS, CHUNK, CHUNK), lambda i: (0, 0, 0)), pl.BlockSpec((CHUNK, GROUPS), const2)],
        out_specs=[pl.BlockSpec((tc, 2 * D), lambda i: (i, 0)), pl.BlockSpec((GROUPS, CHUNK, CHUNK), lambda i: (0, 0, 0)),
                   pl.BlockSpec((GROUPS, CHUNK), const2), pl.BlockSpec((1, D), const2), pl.BlockSpec((1, D), const2)],
        out_shape=[SDS((T, 2 * D), BF16), SDS((GROUPS, CHUNK, CHUNK), F32), SDS((GROUPS, CHUNK), F32),
                   SDS((1, D), F32), SDS((1, D), F32)],
        scratch_shapes=[pltpu.VMEM((tc, D), F32), pltpu.VMEM((tc, D), F32), pltpu.VMEM((CHUNK, D), F32)],
        compiler_params=_params(1),
    )(proj, proj, da, lng, lnb, ws, bst)


IN_SEG_WIDTHS = (2 * D, D, 2 * N_KV * HEAD, 2 * D)


def _resident(shape):
    return pl.BlockSpec(shape, lambda *_: (0,) * len(shape), pipeline_mode=pl.Buffered(1))


def _bwd_in(duv, dq, dkv, dg, win_t, x, dx1, g0, after=None):
    T = x.shape[0]
    tm = min(T, 512)

    def body(duv_ref, dq_ref, dkv_ref, dg_ref, w_ref, x_ref, dx1_ref, g0_ref, gx_ref, dg0_ref):
        @pl.when(pl.program_id(0) == 0)
        def _():
            dg0_ref[...] = jnp.zeros_like(dg0_ref)

        dh, off = None, 0
        for ref, width in zip((duv_ref, dq_ref, dkv_ref, dg_ref), IN_SEG_WIDTHS):
            part = _nn(ref[...], w_ref[off:off + width, :])
            dh = part if dh is None else dh + part
            off += width
        r0, xh = _rms_stats(x_ref[...])
        dg0_ref[...] += _colsum(dh * xh)
        gx_ref[...] = dx1_ref[...] + _rms_bwd(dh, xh, r0, g0_ref[...])

    row = lambda i: (i, 0)
    body, dep_specs, deps = _after(body, 8, after)
    return pl.pallas_call(
        body, name="bwd_in", grid=(T // tm,),
        in_specs=[pl.BlockSpec((tm, w), row) for w in IN_SEG_WIDTHS] + [
            _resident((IN_W, D)), pl.BlockSpec((tm, D), row), pl.BlockSpec((tm, D), row),
            pl.BlockSpec((1, D), lambda i: (0, 0))] + dep_specs,
        out_specs=[pl.BlockSpec((tm, D), row), pl.BlockSpec((1, D), lambda i: (0, 0))],
        out_shape=[SDS((T, D), F32), SDS((1, D), F32)],
        compiler_params=_params(1),
    )(duv, dq, dkv, dg, win_t, x, dx1, g0, *deps)


def _wgrad_rows(h, segs, first_row, into, name):
    T = h.shape[0]
    tt = min(T, 1024)
    widths = [s.shape[1] for s in segs]
    rows = sum(widths)
    n_in = 1 + len(segs) + (into is not None)

    def body(*refs):
        h_ref, seg_refs = refs[0], refs[1:1 + len(segs)]
        dw_ref, acc, stage, sem = refs[n_in], refs[n_in + 1], refs[n_in + 2], refs[n_in + 3]
        t = pl.program_id(0)

        @pl.when(t == 0)
        def _():
            acc[...] = jnp.zeros_like(acc)

        off = 0
        for ref, width in zip(seg_refs, widths):
            acc[off:off + width, :] += _tn(ref[...], h_ref[...])
            off += width

        @pl.when(t == T // tt - 1)
        def _():
            stage[...] = acc[...].astype(BF16)
            out = pltpu.make_async_copy(stage, dw_ref.at[pl.ds(first_row, rows)], sem)
            out.start()
            out.wait()

    row = lambda t: (t, 0)
    return pl.pallas_call(
        body, name=name, grid=(T // tt,),
        in_specs=[pl.BlockSpec((tt, D), row)] + [pl.BlockSpec((tt, w), row) for w in widths] + [_ANY] * (into is not None),
        out_specs=_ANY,
        out_shape=SDS((IN_W, D), BF16),
        input_output_aliases={} if into is None else {n_in - 1: 0},
        scratch_shapes=[pltpu.VMEM((rows, D), F32), pltpu.VMEM((rows, D), BF16), pltpu.SemaphoreType.DMA],
        compiler_params=_params(1),
    )(h, *segs, *([] if into is None else [into]))


def _wgrad_in(h, duv, dq, dkv, dg):
    dw = _wgrad_rows(h, [dg], IN_SEG_WIDTHS[0] + IN_SEG_WIDTHS[1] + IN_SEG_WIDTHS[2], None, "wgrad_in_gates")
    dw = _wgrad_rows(h, [duv], 0, dw, "wgrad_in_uv")
    return _wgrad_rows(h, [dq, dkv], IN_SEG_WIDTHS[0], dw, "wgrad_in_qkv")


def _place():
    x, y, c = lax.axis_index("x"), lax.axis_index("y"), lax.axis_index("c")
    return x, y, c, 4 * x + 2 * y + c


def _peers(x, y, c):
    out = []
    for mask in range(1, N_DEV):
        px = 1 - x if mask & 4 else x
        py = 1 - y if mask & 2 else y
        pc = 1 - c if mask & 1 else c
        out.append(((px, py, pc), 4 * px + 2 * py + pc))
    return out


def _all_to_all(arrays, gather, name, after=None):
    n = len(arrays)

    def body(*refs):
        ins, outs = refs[:n], refs[n:2 * n]
        send_sems, recv_sems, local_sems = refs[2 * n:]
        x, y, c, me = _place()
        local, sends, recvs = [], [], []
        for a in range(n):
            src_own = ins[a] if gather[a] else ins[a].at[me]
            local.append(pltpu.make_async_copy(src_own, outs[a].at[me], local_sems.at[a]))
            for k, (peer, pid) in enumerate(_peers(x, y, c)):
                sem = a * (N_DEV - 1) + k
                src = ins[a] if gather[a] else ins[a].at[pid]
                sends.append(pltpu.make_async_remote_copy(
                    src_ref=src, dst_ref=outs[a].at[me], send_sem=send_sems.at[sem], recv_sem=recv_sems.at[sem],
                    device_id=peer, device_id_type=MESH))
                recvs.append(pltpu.make_async_remote_copy(
                    src_ref=src, dst_ref=outs[a].at[pid], send_sem=send_sems.at[sem], recv_sem=recv_sems.at[sem],
                    device_id=peer, device_id_type=MESH))
        for cp in local + sends:
            cp.start()
        for cp in recvs:
            cp.wait_recv()
        for cp in sends:
            cp.wait_send()
        for cp in local:
            cp.wait()

    out_shape = [SDS((N_DEV,) + a.shape if gt else a.shape, a.dtype) for a, gt in zip(arrays, gather)]
    nsem = n * (N_DEV - 1)
    body, dep_specs, deps = _after(body, n, after)
    return pl.pallas_call(
        body, name=name,
        in_specs=[pl.BlockSpec(memory_space=pl.ANY)] * n + dep_specs,
        out_specs=[pl.BlockSpec(memory_space=pl.ANY)] * n,
        out_shape=out_shape,
        scratch_shapes=[pltpu.SemaphoreType.DMA((nsem,)), pltpu.SemaphoreType.DMA((nsem,)), pltpu.SemaphoreType.DMA((n,))],
    )(*arrays, *deps)


_HBM = pl.BlockSpec(memory_space=pltpu.HBM)
_SEM = pl.BlockSpec(memory_space=pltpu.SEMAPHORE)
_EFFECT = pltpu.SideEffectType.DATAFLOW_SIDE_EFFECTING
GATHER = "gather"
SCATTER = "scatter"
SPREAD = "spread"


def _zone_shape(a, mode):
    if mode == GATHER:
        return (N_DEV,) + a.shape
    return (N_DEV - 1,) + (a.shape[1:] if mode == SCATTER else a.shape)


def _start_copies(arrays, modes, name, after=None):
    n = len(arrays)
    zones = [lax.empty(_zone_shape(a, m), a.dtype) for a, m in zip(arrays, modes)]

    def body(*refs):
        ins, lands = refs[:n], refs[n:2 * n]
        send_sems, recv_sems = refs[-2 * n - 3], refs[-2 * n - 2]
        token = refs[-1]
        x, y, c, me = _place()
        for a in range(n):
            for k, (peer, pid) in enumerate(_peers(x, y, c)):
                src = ins[a].at[pid] if modes[a] == SCATTER else ins[a]
                dst = lands[a].at[me] if modes[a] == GATHER else lands[a].at[k]
                pltpu.make_async_remote_copy(src_ref=src, dst_ref=dst, send_sem=send_sems.at[a], recv_sem=recv_sems.at[a],
                                             device_id=peer, device_id_type=MESH).start()
        token[...] = jnp.zeros_like(token)

    hbm = lambda a: pltpu.HBM(a.shape, a.dtype)
    sems = pltpu.SemaphoreType.DMA((n,))
    extra = [] if after is None else [after]
    operands = [pltpu.with_memory_space_constraint(a, pltpu.HBM) for a in list(arrays) + zones]
    res = pl.pallas_call(
        body, name=name,
        out_shape=(sems, sems, *[hbm(a) for a in arrays], *[hbm(z) for z in zones], SDS((8, 128), F32)),
        in_specs=[_HBM] * (2 * n) + [_ANY] * len(extra),
        out_specs=(_SEM, _SEM, *[_HBM] * (2 * n), pl.BlockSpec(memory_space=pltpu.VMEM)),
        input_output_aliases={i: 2 + i for i in range(2 * n)},
        compiler_params=pltpu.CompilerParams(has_side_effects=_EFFECT),
    )(*operands, *extra)
    return res[0], res[1], list(res[2:2 + n]), list(res[2 + n:2 + 2 * n]), res[-1]


def _wait_copies(started, after, name, count=N_DEV - 1):
    send_sems, recv_sems, thru, zones, _ = started
    nt, nz = len(thru), len(zones)

    def body(*refs):
        lands = refs[nt:nt + nz]
        send_ref, recv_ref = refs[nt + nz], refs[nt + nz + 1]
        x, y, c, _ = _place()
        for a in range(nz):
            blocks = lands[a].at[pl.ds(0, count)]
            cp = pltpu.make_async_remote_copy(src_ref=blocks, dst_ref=blocks, send_sem=send_ref.at[a], recv_sem=recv_ref.at[a],
                                              device_id=(x, y, 1 - c), device_id_type=MESH)
            cp.wait_send()
            cp.wait_recv()

    hbm = lambda a: pltpu.HBM(a.shape, a.dtype)
    res = pl.pallas_call(
        body, name=name,
        out_shape=tuple(hbm(a) for a in thru + zones),
        in_specs=[_HBM] * (nt + nz) + [_SEM, _SEM, _ANY],
        out_specs=tuple([_HBM] * (nt + nz)),
        input_output_aliases={i: i for i in range(nt + nz)},
        compiler_params=pltpu.CompilerParams(has_side_effects=_EFFECT),
    )(*thru, *zones, send_sems, recv_sems, after)
    return list(res[:nt]), list(res[nt:])


def _split_start(body, arrays, zones, name, after):
    n = len(arrays) + len(zones)
    hbm = lambda a: pltpu.HBM(a.shape, a.dtype)
    sems = pltpu.SemaphoreType.DMA((max(len(zones), 1),))
    extra = [] if after is None else [after]
    operands = [pltpu.with_memory_space_constraint(a, pltpu.HBM) for a in list(arrays) + list(zones)]
    res = pl.pallas_call(
        body, name=name,
        out_shape=(sems, sems, *[hbm(a) for a in operands], SDS((8, 128), F32)),
        in_specs=[_HBM] * n + [_ANY] * len(extra),
        out_specs=(_SEM, _SEM, *[_HBM] * n, pl.BlockSpec(memory_space=pltpu.VMEM)),
        input_output_aliases={i: 2 + i for i in range(n)},
        compiler_params=pltpu.CompilerParams(has_side_effects=_EFFECT),
    )(*operands, *extra)
    return res[0], res[1], list(res[2:2 + len(arrays)]), list(res[2 + len(arrays):2 + n]), res[-1]


def _gather_first_leg(shard, name, after=None):
    zone = lax.empty((N_DEV,) + shard.shape, shard.dtype)
    extra = 0 if after is None else 1

    def body(*refs):
        src, land = refs[0], refs[1]
        send_sem, recv_sem, token = refs[2 + extra], refs[3 + extra], refs[-1]
        x, y, c, me = _place()
        for peer in ((x, y, 1 - c), (1 - x, y, c), (x, 1 - y, c), (1 - x, 1 - y, c)):
            pltpu.make_async_remote_copy(src_ref=src, dst_ref=land.at[me], send_sem=send_sem.at[0], recv_sem=recv_sem.at[0],
                                         device_id=peer, device_id_type=MESH).start()
        token[...] = jnp.zeros_like(token)

    return _split_start(body, [shard], [zone], name, after)


def _gather_second_leg(zone, name, after=None):
    extra = 0 if after is None else 1

    def body(*refs):
        land = refs[0]
        send_sem, recv_sem, token = refs[1 + extra], refs[2 + extra], refs[-1]
        x, y, c, _ = _place()
        for px, py in ((1 - x, y), (x, 1 - y), (1 - x, 1 - y)):
            slot = 4 * px + 2 * py + c
            pltpu.make_async_remote_copy(src_ref=land.at[slot], dst_ref=land.at[slot], send_sem=send_sem.at[0],
                                         recv_sem=recv_sem.at[0], device_id=(x, y, 1 - c), device_id_type=MESH).start()
        token[...] = jnp.zeros_like(token)

    return _split_start(body, [], [zone], name, after)


UPDATE_BLOCK_ELEMS = 256 * 1024


def _update_rows(R, C):
    fits = [t for t in range(8, R + 1, 8) if R % t == 0 and t * C <= UPDATE_BLOCK_ELEMS]
    whole = [t for t in fits if t % 16 == 0]
    return max(whole or fits)


def _adamw_math(g, w, m, v):
    m2 = ADAM_B1 * m + (1.0 - ADAM_B1) * g
    v2 = ADAM_B2 * v + (1.0 - ADAM_B2) * (g * g)
    m_hat = m2 / (1.0 - ADAM_B1 ** ADAM_STEP)
    v_hat = v2 / (1.0 - ADAM_B2 ** ADAM_STEP)
    delta = -ADAM_LR * (m_hat / (jnp.sqrt(v_hat) + ADAM_EPS) + ADAM_WD * w)
    return delta, m2, v2


def _sum_adamw(parts, w, m, v, name):
    R, C = w.shape
    tr = _update_rows(R, C)

    def body(p_ref, w_ref, m_ref, v_ref, g_ref, d_ref, m2_ref, v2_ref):
        g = p_ref[0]
        for k in range(1, N_DEV):
            g = g + p_ref[k]
        g_ref[...] = g
        d_ref[...], m2_ref[...], v2_ref[...] = _adamw_math(g, w_ref[...], m_ref[...], v_ref[...])

    blk = pl.BlockSpec((tr, C), lambda i: (i, 0))
    return pl.pallas_call(
        body, name=name, grid=(R // tr,),
        in_specs=[pl.BlockSpec((N_DEV, tr, C), lambda i: (0, i, 0)), blk, blk, blk],
        out_specs=[blk] * 4,
        out_shape=[SDS((R, C), F32)] * 4,
        compiler_params=_params(1),
    )(parts, w, m, v)


def _sum_adamw_peers(me, own, parts, w, m, v, name, replicated):
    R, C = w.shape
    tr = _update_rows(R, C)

    def body(me_ref, own_ref, p_ref, w_ref, m_ref, v_ref, g_ref, d_ref, m2_ref, v2_ref):
        if replicated:
            mine = me_ref[0]
            g = None
            for j in range(N_DEV):
                k = jnp.maximum(jnp.bitwise_xor(mine, j) - 1, 0)
                term = jnp.where(mine == j, own_ref[...], p_ref[k])
                g = term if g is None else g + term
        else:
            g = own_ref[...].astype(F32)
            for k in range(N_DEV - 1):
                g = g + p_ref[k].astype(F32)
        g_ref[...] = g
        d_ref[...], m2_ref[...], v2_ref[...] = _adamw_math(g, w_ref[...], m_ref[...], v_ref[...])

    blk = pl.BlockSpec((tr, C), lambda i, me_ref: (i, 0))
    own_spec = blk if replicated else pl.BlockSpec((None, tr, C), lambda i, me_ref: (me_ref[0], i, 0))
    return pl.pallas_call(
        body, name=name,
        grid_spec=pltpu.PrefetchScalarGridSpec(
            num_scalar_prefetch=1, grid=(R // tr,),
            in_specs=[own_spec, pl.BlockSpec((N_DEV - 1, tr, C), lambda i, me_ref: (0, i, 0)), blk, blk, blk],
            out_specs=[blk] * 4),
        out_shape=[SDS((R, C), F32)] * 4,
        compiler_params=_params(1),
    )(me, own, parts, w, m, v)


SMALL = ("ln_v_gain", "ln_v_bias", "w_spatial", "b_spatial", "sinks", "norm_mix_post", "norm_ff_pre", "norm_ff_post")
SMALL_ROWS = {"ln_v_gain": 8, "ln_v_bias": 8, "w_spatial": 1024, "b_spatial": 8, "sinks": 8,
              "norm_mix_post": 8, "norm_ff_pre": 8, "norm_ff_post": 8}
SMALL_PACK_ROWS = 1152


def _pack_small(vals):
    rows = []
    for name in SMALL:
        flat = vals[name].reshape(-1)
        pad = SMALL_ROWS[name] * 128 - flat.shape[0]
        if pad:
            flat = jnp.concatenate([flat, jnp.zeros((pad,), F32)])
        rows.append(flat.reshape(SMALL_ROWS[name], 128))
    rows.append(jnp.zeros((SMALL_PACK_ROWS - sum(SMALL_ROWS.values()), 128), F32))
    return jnp.concatenate(rows, axis=0)


def _unpack_small(packed, shapes):
    out, r = {}, 0
    for name in SMALL:
        n = 1
        for s in shapes[name]:
            n *= s
        out[name] = packed[r:r + SMALL_ROWS[name]].reshape(-1)[:n].reshape(shapes[name])
        r += SMALL_ROWS[name]
    return out


def _rope_rows():
    d = jnp.arange(128) % HEAD
    inv = ROPE_THETA ** (-(2.0 * (d % (ROPE // 2))).astype(F32) / ROPE)
    invf = jnp.where(d < ROPE, inv, 0.0).astype(F32).reshape(1, 128)
    sgn = jnp.where(d < ROPE // 2, -1.0, jnp.where(d < ROPE, 1.0, 0.0)).astype(F32).reshape(1, 128)
    return invf, sgn


def kernel(x, positions, w_in, ln_v_gain, ln_v_bias, w_spatial, b_spatial, sinks, w_a, w_b, w_o, norm_mix_pre, norm_mix_post, w_ff_in, w_ff_out, norm_ff_pre, norm_ff_post, loss_target, m_w_in, m_ln_v_gain, m_ln_v_bias, m_w_spatial, m_b_spatial, m_sinks, m_w_a, m_w_b, m_w_o, m_norm_mix_pre, m_norm_mix_post, m_w_ff_in, m_w_ff_out, m_norm_ff_pre, m_norm_ff_post, v_w_in, v_ln_v_gain, v_ln_v_bias, v_w_spatial, v_b_spatial, v_sinks, v_w_a, v_w_b, v_w_o, v_norm_mix_pre, v_norm_mix_post, v_w_ff_in, v_w_ff_out, v_norm_ff_pre, v_norm_ff_post):
    given = dict(locals())
    T = x.shape[1]
    xt = x[0]
    tgt = loss_target[0]
    bst = b_spatial[0].T
    ws = w_spatial[0]

    me = 4 * lax.axis_index("x") + 2 * lax.axis_index("y") + lax.axis_index("c")
    me_arr = me.astype(jnp.int32).reshape(1)

    def with_own(zone, shard):
        return lax.dynamic_update_slice(zone, shard[None], (me,) + (0,) * shard.ndim)

    rest = ("w_a", "w_b", "w_o", "w_ff_in", "w_ff_out")
    shard = {n: given[n][0].astype(BF16) for n in rest}
    g_one = _gather_first_leg(w_in[0].T.astype(BF16), "gather_in_start")
    cos, sin = _rope_tables(positions.astype(F32).reshape(T, 1), *_rope_rows(), after=g_one[-1])
    h = _rms_pre(xt, norm_mix_pre, after=cos)
    (own_win,), (win8,) = _wait_copies(g_one, h, "gather_in_wait", count=4)
    g_two = _gather_second_leg(win8, "gather_in_pass_start")
    g_rest = _start_copies([shard[n] for n in rest], [GATHER] * len(rest), "gather_rest_start", after=g_two[-1])
    _, (win8,) = _wait_copies(g_two, g_rest[-1], "gather_in_pass_wait", count=3)
    win = with_own(win8, own_win).reshape(IN_W, D)

    proj = _fwd_in(h, win)
    att = _fwd_attn(proj, cos, sin, sinks[0])
    a = _fwd_sgu(proj, ln_v_gain, ln_v_bias, ws, bst, after=att)
    gw = {n: with_own(z, own) for n, own, z in zip(rest, *_wait_copies(g_rest, a, "gather_rest_wait"))}
    wa, wb, wo = (gw[n].reshape(D, D) for n in ("w_a", "w_b", "w_o"))
    wfi3 = gw["w_ff_in"]
    wfo = gw["w_ff_out"].reshape(D_FF, D)
    merged, a2, b2, mix, x1, hf = _fwd_mix(a, att, proj, xt, wa, wb, wo, norm_mix_post, norm_ff_pre)
    f, dy, dff, dg3, loss_part = _fwd_ff(hf, wfi3, wfo, x1, tgt, norm_ff_post)

    df, dx1, dmix, dg2, dg1 = _bwd_ff(dff, f, wfi3, wfo, x1, dy, mix, norm_mix_post, norm_ff_pre)
    dwfi3, dwfo = _wgrad_ff(hf, df, f, dff)
    own_ff = [dwfi3, dwfo.reshape(N_DEV, D_FF // N_DEV, D)]
    x_ff = _start_copies(own_ff, [SCATTER] * 2, "exchange_ff_start")
    da2, db2, dgate, da, datt = _bwd_mix(dmix, proj, a2, b2, wo, wa, wb, after=x_ff[-1])
    dwo, dwa, dwb = _wgrad_mix(merged, dmix, a, da2, att, db2)
    own_mix = [g.reshape(N_DEV, D // N_DEV, D) for g in (dwa, dwb, dwo)]
    x_mix = _start_copies(own_mix, [SCATTER] * 3, "exchange_mix_start")
    dq, dkv, dsink = _bwd_attn(proj, cos, sin, sinks[0], datt, after=x_mix[-1])
    duv, dws, dbs, dlng, dlnb = _bwd_sgu(proj, da, ln_v_gain, ln_v_bias, ws, bst)
    small_grads = {"ln_v_gain": dlng, "ln_v_bias": dlnb, "w_spatial": dws, "b_spatial": dbs, "sinks": dsink[:, :N_Q],
                   "norm_mix_post": dg1, "norm_ff_pre": dg2, "norm_ff_post": dg3}
    x_small = _start_copies([_pack_small(small_grads)], [SPREAD], "exchange_small_start")
    dwin = _wgrad_in(h, duv, dq, dkv, dgate)
    own_in = [dwin.reshape(N_DEV, IN_W // N_DEV, D)]
    x_in = _start_copies(own_in, [SCATTER], "exchange_in_start", after=x_small[-1])
    grad_x, dg0 = _bwd_in(duv, dq, dkv, dgate, win, xt, dx1, norm_mix_pre, after=x_in[-1])

    results = {}

    def update(n, own, parts, transposed=False):
        state = [given[k + n][0].T if transposed else given[k + n][0] for k in ("", "m_", "v_")]
        res = _sum_adamw_peers(me_arr, own, parts, *state, "adamw_" + n, False)
        results[n] = [(r.T if transposed else r).reshape(given[n].shape) for r in res]

    own_ff, p_ff = _wait_copies(x_ff, grad_x, "exchange_ff_wait")
    update("w_ff_in", own_ff[0], p_ff[0])
    update("w_ff_out", own_ff[1], p_ff[1])
    own_mix, p_mix = _wait_copies(x_mix, results["w_ff_out"][0], "exchange_mix_wait")
    for n, own, parts in zip(("w_a", "w_b", "w_o"), own_mix, p_mix):
        update(n, own, parts)
    tail = jnp.concatenate([dg0.reshape(8, 128), jnp.tile(loss_part, (8, 1))], axis=0)
    (tail_all,) = _all_to_all([tail], [True], "exchange_tail", after=results["w_o"][0])
    dg0_all = tail_all[:, :8]
    own_small, p_small = _wait_copies(x_small, tail_all, "exchange_small_wait")
    own_in, p_in = _wait_copies(x_in, p_small[0], "exchange_in_wait")
    update("w_in", own_in[0], p_in[0], transposed=True)
    packed = _sum_adamw_peers(me_arr, own_small[0], p_small[0], _pack_small({n: given[n] for n in SMALL}),
                              _pack_small({n: given["m_" + n] for n in SMALL}),
                              _pack_small({n: given["v_" + n] for n in SMALL}), "adamw_small", True)
    shapes = {n: given[n].shape for n in SMALL}
    unpacked = [_unpack_small(p, shapes) for p in packed]
    for n in SMALL:
        results[n] = [u[n] for u in unpacked]
    n = "norm_mix_pre"
    results[n] = [r.reshape(given[n].shape) for r in _sum_adamw(
        dg0_all, given[n].reshape(8, 128), given["m_" + n].reshape(8, 128), given["v_" + n].reshape(8, 128), "adamw_" + n)]

    loss = jnp.sum(tail_all[:, 8, 0])
    order = ("w_in", "ln_v_gain", "ln_v_bias", "w_spatial", "b_spatial", "sinks", "w_a", "w_b", "w_o", "norm_mix_pre",
             "norm_mix_post", "w_ff_in", "w_ff_out", "norm_ff_pre", "norm_ff_post")
    out = [loss, grad_x.reshape(x.shape)]
    for k in range(4):
        out += [results[n][k] for n in order]
    return tuple(out)
```

```python
import functools

import jax
import jax.numpy as jnp
from jax import lax
from jax.experimental import pallas as pl
from jax.experimental.pallas import tpu as pltpu

F32 = jnp.float32
BF16 = jnp.bfloat16

N_DEV = 8
D = 1024
D_FF = 4096
IN_W = 5632
CHUNK = 128
GROUPS = 8
HEAD = 64
N_Q = 16
N_KV = 4
ROPE = 16
ROPE_THETA = 500000.0
EPS = 1e-6
OFF_Q, OFF_K, OFF_VA, OFF_GA, OFF_GB = 2048, 3072, 3328, 3584, 4608

ADAM_LR = 0.001
ADAM_B1 = 0.9
ADAM_B2 = 0.999
ADAM_EPS = 1e-08
ADAM_WD = 0.01
ADAM_STEP = 10

VMEM_LIMIT = 56 * 1024 * 1024

SDS = jax.ShapeDtypeStruct
MESH = pl.DeviceIdType.MESH


def _params(n_axes=None):
    if n_axes is None:
        return pltpu.CompilerParams(vmem_limit_bytes=VMEM_LIMIT)
    return pltpu.CompilerParams(dimension_semantics=("arbitrary",) * n_axes, vmem_limit_bytes=VMEM_LIMIT)


def _nt(a, b):
    return lax.dot_general(a, b, (((1,), (1,)), ((), ())), preferred_element_type=F32)


def _tn(a, b):
    return lax.dot_general(a, b, (((0,), (0,)), ((), ())), preferred_element_type=F32)


def _nn(a, b):
    return jnp.dot(a, b, preferred_element_type=F32)


def _gelu(x):
    t = jnp.tanh(0.7978845608028654 * (x + 0.044715 * (x * x * x)))
    return 0.5 * x * (1.0 + t), t


def _gelu_grad(x, t):
    return 0.5 * (1.0 + t) + 0.5 * x * (1.0 - t * t) * (0.7978845608028654 * (1.0 + 3.0 * 0.044715 * x * x))


def _sigmoid(x):
    return 1.0 / (1.0 + jnp.exp(-x))


def _rms_stats(v):
    r = lax.rsqrt(jnp.mean(v * v, axis=-1, keepdims=True) + EPS)
    return r, v * r


def _rms_bwd(d, vhat, r, g):
    gd = g * d
    return r * (gd - vhat * jnp.mean(gd * vhat, axis=-1, keepdims=True))


def _colsum(v):
    return jnp.sum(v, axis=0, keepdims=True)


_ANY = pl.BlockSpec(memory_space=pl.ANY)


def _after(body, n_in, after):
    if after is None:
        return body, [], []

    def ordered(*refs):
        return body(*refs[:n_in], *refs[n_in + 1:])

    return ordered, [_ANY], [after]


def _rms_pre(x, g0, after=None):
    T = x.shape[0]
    tm = min(T, 1024)

    def body(x_ref, g_ref, h_ref):
        _, xh = _rms_stats(x_ref[...])
        h_ref[...] = (xh * g_ref[...]).astype(BF16)

    body, dep_specs, deps = _after(body, 2, after)
    return pl.pallas_call(
        body, name="rms_pre", grid=(T // tm,),
        in_specs=[pl.BlockSpec((tm, D), lambda i: (i, 0)), pl.BlockSpec((1, D), lambda i: (0, 0))] + dep_specs,
        out_specs=pl.BlockSpec((tm, D), lambda i: (i, 0)),
        out_shape=SDS((T, D), BF16),
        compiler_params=_params(1),
    )(x, g0, *deps)


def _fwd_in(h, win_t):
    T = h.shape[0]
    tm, tn = min(T, 1024), 1408

    def body(h_ref, w_ref, p_ref):
        p_ref[...] = _nt(h_ref[...], w_ref[...]).astype(BF16)

    return pl.pallas_call(
        body, name="fwd_in", grid=(T // tm, IN_W // tn),
        in_specs=[pl.BlockSpec((tm, D), lambda i, j: (i, 0)), pl.BlockSpec((tn, D), lambda i, j: (j, 0))],
        out_specs=pl.BlockSpec((tm, tn), lambda i, j: (i, j)),
        out_shape=SDS((T, IN_W), BF16),
        compiler_params=_params(2),
    )(h, win_t)


def _sgu_forward_parts(u_ref, vs_ref, lng_ref, lnb_ref):
    u = u_ref[...].astype(F32)
    vs = vs_ref[...].astype(F32)
    gu, tu = _gelu(u)
    gv, tv = _gelu(vs)
    mu = jnp.mean(gv, axis=-1, keepdims=True)
    dv = gv - mu
    rstd = lax.rsqrt(jnp.mean(dv * dv, axis=-1, keepdims=True) + EPS)
    vhat = dv * rstd
    vn = (vhat * lng_ref[...] + lnb_ref[...]).astype(BF16)
    return u, vs, gu, tu, tv, rstd, vhat, vn


def _masked_ws(ws_ref, g):
    row = lax.broadcasted_iota(jnp.int32, (CHUNK, CHUNK), 0)
    col = lax.broadcasted_iota(jnp.int32, (CHUNK, CHUNK), 1)
    return jnp.where(row >= col, ws_ref[g], 0.0).astype(BF16)


def _fwd_sgu(proj, lng, lnb, ws, bst, after=None):
    T = proj.shape[0]
    tc = min(T, 512)

    def body(u_ref, vs_ref, lng_ref, lnb_ref, ws_ref, bst_ref, a_ref):
        _, _, gu, _, _, _, _, vn = _sgu_forward_parts(u_ref, vs_ref, lng_ref, lnb_ref)
        for g in range(GROUPS):
            wm = _masked_ws(ws_ref, g)
            cols = slice(g * CHUNK, (g + 1) * CHUNK)
            for c in range(tc // CHUNK):
                rows = slice(c * CHUNK, (c + 1) * CHUNK)
                mixed = _nn(wm, vn[rows, cols]) + bst_ref[:, g:g + 1]
                a_ref[rows, cols] = (gu[rows, cols] * mixed).astype(BF16)

    body, dep_specs, deps = _after(body, 6, after)
    return pl.pallas_call(
        body, name="fwd_sgu", grid=(T // tc,),
        in_specs=[pl.BlockSpec((tc, D), lambda i: (i, 0)), pl.BlockSpec((tc, D), lambda i: (i, 1)),
                  pl.BlockSpec((1, D), lambda i: (0, 0)), pl.BlockSpec((1, D), lambda i: (0, 0)),
                  pl.BlockSpec((GROUPS, CHUNK, CHUNK), lambda i: (0, 0, 0)),
                  pl.BlockSpec((CHUNK, GROUPS), lambda i: (0, 0))] + dep_specs,
        out_specs=pl.BlockSpec((tc, D), lambda i: (i, 0)),
        out_shape=SDS((T, D), BF16),
        compiler_params=_params(1),
    )(proj, proj, lng, lnb, ws, bst, *deps)


def _rope_tables(posf, invf, sgn, after=None):
    T = posf.shape[0]
    tr = min(T, 1024)

    def body(pos_ref, invf_ref, sgn_ref, c_ref, s_ref):
        ang = pos_ref[...] * invf_ref[...]
        c_ref[...] = jnp.cos(ang)
        s = jnp.sin(ang)
        s_ref[:, :128] = jnp.where(sgn_ref[...] < 0.0, -s, 0.0)
        s_ref[:, 128:] = jnp.where(sgn_ref[...] > 0.0, s, 0.0)

    body, dep_specs, deps = _after(body, 3, after)
    return pl.pallas_call(
        body, name="rope_tables", grid=(T // tr,),
        in_specs=[pl.BlockSpec((tr, 1), lambda i: (i, 0)), pl.BlockSpec((1, 128), lambda i: (0, 0)),
                  pl.BlockSpec((1, 128), lambda i: (0, 0))] + dep_specs,
        out_specs=[pl.BlockSpec((tr, 128), lambda i: (i, 0)), pl.BlockSpec((tr, 256), lambda i: (i, 0))],
        out_shape=[SDS((T, 128), F32), SDS((T, 256), F32)],
        compiler_params=_params(1),
    )(posf, invf, sgn, *deps)


def _rope(v, c, s):
    v = v.astype(F32)
    return v * c + pltpu.roll(v, 128 - ROPE // 2, 1) * s[:, :128] + pltpu.roll(v, ROPE // 2, 1) * s[:, 128:]


def _rope_bwd(dv, c, s):
    return dv * c + pltpu.roll(dv * s[:, :128], ROPE // 2, 1) + pltpu.roll(dv * s[:, 128:], 128 - ROPE // 2, 1)


def _fold_masks(first):
    jj = lax.broadcasted_iota(jnp.int32, (CHUNK, CHUNK), 0)
    t = lax.broadcasted_iota(jnp.int32, (CHUNK, CHUNK), 1)
    prev = jj > t
    return prev, jnp.where(prev & first, -1e30, 0.0)


def _fold(band, prev):
    return jnp.where(prev, band[:CHUNK], band[CHUNK:])


def _unfold(folded, prev):
    return jnp.concatenate([jnp.where(prev, folded, 0.0), jnp.where(prev, 0.0, folded)], axis=0)


def _softmax_sink(s, sink, key_axis):
    m = jnp.maximum(jnp.max(s, axis=key_axis, keepdims=True), sink)
    p = jnp.exp(s - m)
    esink = jnp.exp(sink - m)
    inv = 1.0 / (jnp.sum(p, axis=key_axis, keepdims=True) + esink)
    return p * inv, esink * inv


def _head_pair_operand(slab, g):
    lo = lax.broadcasted_iota(jnp.int32, slab.shape, 1) < HEAD
    if g % 2 == 0:
        first = jnp.where(lo, slab, 0.0)
        second = pltpu.roll(first, HEAD, 1)
    else:
        second = jnp.where(lo, 0.0, slab)
        first = pltpu.roll(second, HEAD, 1)
    return jnp.concatenate([first, second], axis=0).astype(BF16)


def _head_pair_gradient(acc, g):
    top, bot = acc[:2 * CHUNK], acc[2 * CHUNK:]
    lo = lax.broadcasted_iota(jnp.int32, top.shape, 1) < HEAD
    if g % 2 == 0:
        return jnp.where(lo, top, 0.0) + pltpu.roll(jnp.where(lo, 0.0, bot), HEAD, 1)
    return pltpu.roll(jnp.where(lo, top, 0.0), HEAD, 1) + jnp.where(lo, 0.0, bot)


PAIRS_PER_KV = N_Q // N_KV // 2
KV_W = N_KV * HEAD


def _band(prev_ref, cur_ref, cols=slice(None)):
    return jnp.concatenate([prev_ref[:, cols], cur_ref[:, cols]], axis=0)


def _fwd_attn(proj, cos, sin, sinks):
    T = proj.shape[0]
    nb = T // CHUNK
    cur = lambda i: i
    prev = lambda i: jnp.maximum(i - 1, 0)

    def body(q_ref, kp_ref, kc_ref, vp_ref, vc_ref, cp_ref, cc_ref, sp_ref, sc_ref, sink_ref,
             o_ref, qr_ref, kr_ref):
        prev_slot, bias = _fold_masks(pl.program_id(0) == 0)
        c_band, s_band = _band(cp_ref, cc_ref), _band(sp_ref, sc_ref)
        for j in range(KV_W // 128):
            cols = slice(j * 128, (j + 1) * 128)
            k_slab = _rope(_band(kp_ref, kc_ref, cols), c_band, s_band)
            kr_ref[:, cols] = k_slab[CHUNK:].astype(BF16)
            v_slab = _band(vp_ref, vc_ref, cols).astype(F32)
            for g in (2 * j, 2 * j + 1):
                k2 = _head_pair_operand(k_slab, g)
                v2 = _head_pair_operand(v_slab, g)
                for r in range(PAIRS_PER_KV):
                    lanes = slice((g * PAIRS_PER_KV + r) * 128, (g * PAIRS_PER_KV + r + 1) * 128)
                    qp = (_rope(q_ref[:, lanes], cc_ref[...], sc_ref[...]) * (HEAD ** -0.5)).astype(BF16)
                    qr_ref[:, lanes] = qp
                    s2 = _nt(k2, qp)
                    ps = []
                    for e in range(2):
                        s = _fold(s2[e * 2 * CHUNK:(e + 1) * 2 * CHUNK], prev_slot) + bias
                        p = _softmax_sink(s, sink_ref[2 * (g * PAIRS_PER_KV + r) + e], 0)[0]
                        ps.append(_unfold(p, prev_slot).astype(BF16))
                    o_ref[:, lanes] = _tn(jnp.concatenate(ps, axis=0), v2).astype(BF16)

    table = lambda which, width: pl.BlockSpec((CHUNK, width), lambda i: (which(i), 0))
    return pl.pallas_call(
        body, name="fwd_attn", grid=(nb,),
        in_specs=[pl.BlockSpec((CHUNK, D), lambda i: (i, OFF_Q // D)),
                  pl.BlockSpec((CHUNK, KV_W), lambda i: (prev(i), OFF_K // KV_W)),
                  pl.BlockSpec((CHUNK, KV_W), lambda i: (i, OFF_K // KV_W)),
                  pl.BlockSpec((CHUNK, KV_W), lambda i: (prev(i), OFF_VA // KV_W)),
                  pl.BlockSpec((CHUNK, KV_W), lambda i: (i, OFF_VA // KV_W)),
                  table(prev, 128), table(cur, 128), table(prev, 256), table(cur, 256),
                  pl.BlockSpec(memory_space=pltpu.SMEM)],
        out_specs=[pl.BlockSpec((CHUNK, D), lambda i: (i, 0)), pl.BlockSpec((CHUNK, D), lambda i: (i, 0)),
                   pl.BlockSpec((CHUNK, KV_W), lambda i: (i, 0))],
        out_shape=[SDS((T, D), BF16), SDS((T, D), BF16), SDS((T, KV_W), BF16)],
        compiler_params=_params(1),
    )(proj, proj, proj, proj, proj, cos, cos, sin, sin, sinks)


def _row_halves(tm):
    return [slice(0, tm // 2), slice(tm // 2, tm)] if tm % 32 == 0 else [slice(0, tm)]


def _fwd_mix(a, att, proj, x, wa, wb, wo, g1, g2):
    T = x.shape[0]
    tm = min(T, 512)
    half = D // 2

    def body(a_ref, att_ref, ga0, ga1, gb0, gb1, x_ref, wa_ref, wb_ref, wo_ref, g1_ref, g2_ref,
             mg_ref, a2_ref, b2_ref, mix_ref, x1_ref, hf_ref):
        for rows in _row_halves(tm):
            a2 = _nn(a_ref[rows, :], wa_ref[...])
            b2 = _nn(att_ref[rows, :], wb_ref[...])
            ga = jnp.concatenate([ga0[rows, :], ga1[rows, :]], axis=1).astype(F32)
            gb = jnp.concatenate([gb0[rows, :], gb1[rows, :]], axis=1).astype(F32)
            merged = (_sigmoid(ga) * a2 + _sigmoid(gb) * b2).astype(BF16)
            a2_ref[rows, :] = a2.astype(BF16)
            b2_ref[rows, :] = b2.astype(BF16)
            mg_ref[rows, :] = merged
            mix = _nn(merged, wo_ref[...])
            mix_ref[rows, :] = mix
            _, mh = _rms_stats(mix)
            x1 = x_ref[rows, :] + mh * g1_ref[...]
            x1_ref[rows, :] = x1
            _, xh = _rms_stats(x1)
            hf_ref[rows, :] = (xh * g2_ref[...]).astype(BF16)

    row = lambda i: (i, 0)
    const = lambda i: (0, 0)
    gspec = lambda off: pl.BlockSpec((tm, half), lambda i: (i, off // half))
    return pl.pallas_call(
        body, name="fwd_mix", grid=(T // tm,),
        in_specs=[pl.BlockSpec((tm, D), row), pl.BlockSpec((tm, D), row),
                  gspec(OFF_GA), gspec(OFF_GA + half), gspec(OFF_GB), gspec(OFF_GB + half),
                  pl.BlockSpec((tm, D), row), _resident((D, D)), _resident((D, D)),
                  _resident((D, D)), pl.BlockSpec((1, D), const), pl.BlockSpec((1, D), const)],
        out_specs=[pl.BlockSpec((tm, D), row)] * 6,
        out_shape=[SDS((T, D), BF16), SDS((T, D), BF16), SDS((T, D), BF16), SDS((T, D), F32), SDS((T, D), F32),
                   SDS((T, D), BF16)],
        compiler_params=_params(1),
    )(a, att, proj, proj, proj, proj, x, wa, wb, wo, g1, g2)


FF_SPLIT = N_DEV
FF_TILE = D_FF // FF_SPLIT
FF_STEP = 2048
FF_SLABS = FF_STEP // FF_TILE
FF_STEPS = D_FF // FF_STEP


def _fwd_ff(hf, wfi3, wfo, x1, tgt, g3):
    T = hf.shape[0]
    tm = min(T, 512)
    last = FF_STEPS - 1

    def body(hf_ref, wfi_ref, wfo_ref, x1_ref, tgt_ref, g3_ref, f_ref, dy_ref, dff_ref, dg3_ref, loss_ref, acc, r_s):
        i, p = pl.program_id(0), pl.program_id(1)

        @pl.when((i == 0) & (p == 0))
        def _():
            dg3_ref[...] = jnp.zeros_like(dg3_ref)
            loss_ref[...] = jnp.zeros_like(loss_ref)

        hf_t = hf_ref[...]
        for s in range(FF_SLABS):
            cols = slice(s * FF_TILE, (s + 1) * FF_TILE)
            f = _nn(hf_t, wfi_ref[p * FF_SLABS + s]).astype(BF16)
            f_ref[:, cols] = f
            rl = jnp.maximum(f.astype(F32), 0.0)
            r_s[:, cols] = (rl * rl).astype(BF16)
        part = _nn(r_s[...], wfo_ref[pl.ds(pl.multiple_of(p * FF_STEP, FF_STEP), FF_STEP), :])

        @pl.when(p == 0)
        def _():
            acc[...] = part

        @pl.when(p > 0)
        def _():
            acc[...] += part

        @pl.when(p == last)
        def _():
            r3, fh = _rms_stats(acc[...])
            e = x1_ref[...] + fh * g3_ref[...] - tgt_ref[...]
            loss_ref[...] += jnp.sum(e * e) * (0.5 / D)
            dy = e * (1.0 / D)
            dy_ref[...] = dy
            dg3_ref[...] += _colsum(dy * fh)
            dff_ref[...] = _rms_bwd(dy, fh, r3, g3_ref[...]).astype(BF16)

    row = lambda i, p: (i, 0)
    const = lambda i, p: (0, 0)
    return pl.pallas_call(
        body, name="fwd_ff", grid=(T // tm, FF_STEPS),
        in_specs=[pl.BlockSpec((tm, D), row), _resident((FF_SPLIT, D, FF_TILE)), _resident((D_FF, D)),
                  pl.BlockSpec((tm, D), row),
                  pl.BlockSpec((tm, D), row), pl.BlockSpec((1, D), const)],
        out_specs=[pl.BlockSpec((tm, FF_STEP), lambda i, p: (i, p)), pl.BlockSpec((tm, D), row),
                   pl.BlockSpec((tm, D), row), pl.BlockSpec((1, D), const), pl.BlockSpec((1, 128), const)],
        out_shape=[SDS((T, D_FF), BF16), SDS((T, D), F32), SDS((T, D), BF16), SDS((1, D), F32), SDS((1, 128), F32)],
        scratch_shapes=[pltpu.VMEM((tm, D), F32), pltpu.VMEM((tm, FF_STEP), BF16)],
        compiler_params=_params(2),
    )(hf, wfi3, wfo, x1, tgt, g3)


def _bwd_ff(dff, f, wfi3, wfo, x1, dy, mix, g1, g2):
    T = dff.shape[0]
    tm = min(T, 512)
    last = FF_STEPS - 1

    def body(dff_ref, f_ref, wfi_ref, wfo_ref, x1_ref, dy_ref, mix_ref, g1_ref, g2_ref,
             df_ref, dx1_ref, dmix_ref, dg2_ref, dg1_ref, acc):
        i, p = pl.program_id(0), pl.program_id(1)

        @pl.when((i == 0) & (p == 0))
        def _():
            dg2_ref[...] = jnp.zeros_like(dg2_ref)
            dg1_ref[...] = jnp.zeros_like(dg1_ref)

        dr = _nt(dff_ref[...], wfo_ref[pl.ds(pl.multiple_of(p * FF_STEP, FF_STEP), FF_STEP), :])
        df_ref[...] = (dr * (2.0 * jnp.maximum(f_ref[...].astype(F32), 0.0))).astype(BF16)
        part = _nt(df_ref[:, :FF_TILE], wfi_ref[p * FF_SLABS])
        for s in range(1, FF_SLABS):
            part = part + _nt(df_ref[:, s * FF_TILE:(s + 1) * FF_TILE], wfi_ref[p * FF_SLABS + s])

        @pl.when(p == 0)
        def _():
            acc[...] = part

        @pl.when(p > 0)
        def _():
            acc[...] += part

        @pl.when(p == last)
        def _():
            dhf = acc[...]
            r2, xh = _rms_stats(x1_ref[...])
            dg2_ref[...] += _colsum(dhf * xh)
            dx1 = dy_ref[...] + _rms_bwd(dhf, xh, r2, g2_ref[...])
            dx1_ref[...] = dx1
            r1, mh = _rms_stats(mix_ref[...])
            dg1_ref[...] += _colsum(dx1 * mh)
            dmix_ref[...] = _rms_bwd(dx1, mh, r1, g1_ref[...]).astype(BF16)

    row = lambda i, p: (i, 0)
    const = lambda i, p: (0, 0)
    return pl.pallas_call(
        body, name="bwd_ff", grid=(T // tm, FF_STEPS),
        in_specs=[pl.BlockSpec((tm, D), row), pl.BlockSpec((tm, FF_STEP), lambda i, p: (i, p)),
                  _resident((FF_SPLIT, D, FF_TILE)), _resident((D_FF, D)),
                  pl.BlockSpec((tm, D), row), pl.BlockSpec((tm, D), row), pl.BlockSpec((tm, D), row),
                  pl.BlockSpec((1, D), const), pl.BlockSpec((1, D), const)],
        out_specs=[pl.BlockSpec((tm, FF_STEP), lambda i, p: (i, p)), pl.BlockSpec((tm, D), row),
                   pl.BlockSpec((tm, D), row), pl.BlockSpec((1, D), const), pl.BlockSpec((1, D), const)],
        out_shape=[SDS((T, D_FF), BF16), SDS((T, D), F32), SDS((T, D), BF16), SDS((1, D), F32), SDS((1, D), F32)],
        scratch_shapes=[pltpu.VMEM((tm, D), F32)],
        compiler_params=_params(2),
    )(dff, f, wfi3, wfo, x1, dy, mix, g1, g2)


def _wgrad_ff(hf, df, f, dff):
    T = hf.shape[0]
    tt = min(T, 1024)
    wide = 2 * FF_TILE

    def body(hf_ref, df_ref, f_ref, dff_ref, dwfi_ref, dwfo_ref, acc_i, acc_o):
        t = pl.program_id(1)

        @pl.when(t == 0)
        def _():
            acc_i[...] = jnp.zeros_like(acc_i)
            acc_o[...] = jnp.zeros_like(acc_o)

        acc_i[...] += _tn(hf_ref[...], df_ref[...])
        rl = jnp.maximum(f_ref[...].astype(F32), 0.0)
        acc_o[...] += _tn((rl * rl).astype(BF16), dff_ref[...])

        @pl.when(t == T // tt - 1)
        def _():
            dwfi_ref[0] = acc_i[:, :FF_TILE].astype(BF16)
            dwfi_ref[1] = acc_i[:, FF_TILE:].astype(BF16)
            dwfo_ref[...] = acc_o[...].astype(BF16)

    return pl.pallas_call(
        body, name="wgrad_ff", grid=(D_FF // wide, T // tt),
        in_specs=[pl.BlockSpec((tt, D), lambda p, t: (t, 0)), pl.BlockSpec((tt, wide), lambda p, t: (t, p)),
                  pl.BlockSpec((tt, wide), lambda p, t: (t, p)), pl.BlockSpec((tt, D), lambda p, t: (t, 0))],
        out_specs=[pl.BlockSpec((2, D, FF_TILE), lambda p, t: (p, 0, 0)), pl.BlockSpec((wide, D), lambda p, t: (p, 0))],
        out_shape=[SDS((FF_SPLIT, D, FF_TILE), BF16), SDS((D_FF, D), BF16)],
        scratch_shapes=[pltpu.VMEM((D, wide), F32), pltpu.VMEM((wide, D), F32)],
        compiler_params=_params(2),
    )(hf, df, f, dff)


def _bwd_mix(dmix, proj, a2, b2, wo, wa, wb, after=None):
    T = dmix.shape[0]
    tm = min(T, 512)
    half = D // 2

    def body(dmix_ref, ga0, ga1, gb0, gb1, a2_ref, b2_ref, wo_ref, wa_ref, wb_ref,
             da2_ref, db2_ref, dg_ref, da_ref, datt_ref):
        for rows in _row_halves(tm):
            dmg = _nt(dmix_ref[rows, :], wo_ref[...])
            sa = _sigmoid(jnp.concatenate([ga0[rows, :], ga1[rows, :]], axis=1).astype(F32))
            sb = _sigmoid(jnp.concatenate([gb0[rows, :], gb1[rows, :]], axis=1).astype(F32))
            da2 = (dmg * sa).astype(BF16)
            db2 = (dmg * sb).astype(BF16)
            da2_ref[rows, :] = da2
            db2_ref[rows, :] = db2
            dg_ref[rows, :D] = (dmg * a2_ref[rows, :].astype(F32) * (sa * (1.0 - sa))).astype(BF16)
            dg_ref[rows, D:] = (dmg * b2_ref[rows, :].astype(F32) * (sb * (1.0 - sb))).astype(BF16)
            da_ref[rows, :] = _nt(da2, wa_ref[...]).astype(BF16)
            datt_ref[rows, :] = _nt(db2, wb_ref[...]).astype(BF16)

    row = lambda i: (i, 0)
    const = lambda i: (0, 0)
    gspec = lambda off: pl.BlockSpec((tm, half), lambda i: (i, off // half))
    body, dep_specs, deps = _after(body, 10, after)
    return pl.pallas_call(
        body, name="bwd_mix", grid=(T // tm,),
        in_specs=[pl.BlockSpec((tm, D), row), gspec(OFF_GA), gspec(OFF_GA + half), gspec(OFF_GB), gspec(OFF_GB + half),
                  pl.BlockSpec((tm, D), row), pl.BlockSpec((tm, D), row),
                  _resident((D, D)), _resident((D, D)), _resident((D, D))] + dep_specs,
        out_specs=[pl.BlockSpec((tm, D), row), pl.BlockSpec((tm, D), row), pl.BlockSpec((tm, 2 * D), row),
                   pl.BlockSpec((tm, D), row), pl.BlockSpec((tm, D), row)],
        out_shape=[SDS((T, D), BF16), SDS((T, D), BF16), SDS((T, 2 * D), BF16), SDS((T, D), BF16), SDS((T, D), BF16)],
        compiler_params=_params(1),
    )(dmix, proj, proj, proj, proj, a2, b2, wo, wa, wb, *deps)


def _wgrad_mix(merged, dmix, a, da2, att, db2):
    T = merged.shape[0]
    tt = min(T, 512)

    def body(mg_ref, dmix_ref, a_ref, da2_ref, att_ref, db2_ref, dwo_ref, dwa_ref, dwb_ref, acc):
        t = pl.program_id(0)

        @pl.when(t == 0)
        def _():
            acc[...] = jnp.zeros_like(acc)

        acc[0] += _tn(mg_ref[...], dmix_ref[...])
        acc[1] += _tn(a_ref[...], da2_ref[...])
        acc[2] += _tn(att_ref[...], db2_ref[...])

        @pl.when(t == T // tt - 1)
        def _():
            dwo_ref[...] = acc[0].astype(BF16)
            dwa_ref[...] = acc[1].astype(BF16)
            dwb_ref[...] = acc[2].astype(BF16)

    return pl.pallas_call(
        body, name="wgrad_mix", grid=(T // tt,),
        in_specs=[pl.BlockSpec((tt, D), lambda t: (t, 0))] * 6,
        out_specs=[pl.BlockSpec((D, D), lambda t: (0, 0))] * 3,
        out_shape=[SDS((D, D), BF16)] * 3,
        scratch_shapes=[pltpu.VMEM((3, D, D), F32)],
        compiler_params=_params(1),
    )(merged, dmix, a, da2, att, db2)


def _bwd_attn(qr, kr, proj, cos, sin, sinks, datt, after=None):
    T = proj.shape[0]
    nb = T // CHUNK
    cur = lambda i: jnp.minimum(i, nb - 1)
    prev = lambda i: jnp.maximum(jnp.minimum(i, nb - 1) - 1, 0)

    def body(q_ref, kp_ref, kc_ref, vp_ref, vc_ref, cp_ref, cc_ref, sp_ref, sc_ref, sink_ref, do_ref,
             dq_ref, dkv_ref, dsink_ref, carry_k, carry_v):
        i = pl.program_id(0)

        @pl.when(i == 0)
        def _():
            carry_k[...] = jnp.zeros_like(carry_k)
            carry_v[...] = jnp.zeros_like(carry_v)
            dsink_ref[...] = jnp.zeros_like(dsink_ref)

        @pl.when(i < nb)
        def _():
            prev_slot, bias = _fold_masks(i == 0)
            c_band, s_band = _band(cp_ref, cc_ref), _band(sp_ref, sc_ref)
            lane = lax.broadcasted_iota(jnp.int32, (1, 128), 1)
            dsink = jnp.zeros((1, 128), F32)
            for j in range(KV_W // 128):
                cols = slice(j * 128, (j + 1) * 128)
                k_slab = _band(kp_ref, kc_ref, cols).astype(F32)
                v_slab = _band(vp_ref, vc_ref, cols).astype(F32)
                dk_slab = jnp.zeros((2 * CHUNK, 128), F32)
                dv_slab = jnp.zeros((2 * CHUNK, 128), F32)
                for g in (2 * j, 2 * j + 1):
                    k2 = _head_pair_operand(k_slab, g)
                    v2 = _head_pair_operand(v_slab, g)
                    dk2 = jnp.zeros((4 * CHUNK, 128), F32)
                    dv2 = jnp.zeros((4 * CHUNK, 128), F32)
                    for r in range(PAIRS_PER_KV):
                        pair = g * PAIRS_PER_KV + r
                        lanes = slice(pair * 128, (pair + 1) * 128)
                        qp, dop = q_ref[:, lanes], do_ref[:, lanes]
                        s2 = _nt(k2, qp)
                        dp2 = _nt(v2, dop)
                        ps, dss = [], []
                        for e in range(2):
                            rows = slice(e * 2 * CHUNK, (e + 1) * 2 * CHUNK)
                            p, psink = _softmax_sink(_fold(s2[rows], prev_slot) + bias, sink_ref[2 * pair + e], 0)
                            dp = _fold(dp2[rows], prev_slot)
                            delta = jnp.sum(p * dp, axis=0, keepdims=True)
                            ps.append(_unfold(p, prev_slot).astype(BF16))
                            dss.append(_unfold(p * (dp - delta), prev_slot).astype(BF16))
                            dsink = dsink + jnp.where(lane == 2 * pair + e, -jnp.sum(psink * delta), 0.0)
                        ds2 = jnp.concatenate(dss, axis=0)
                        dq = _tn(ds2, k2) * (HEAD ** -0.5)
                        dq_ref[:, lanes] = _rope_bwd(dq, cc_ref[...], sc_ref[...]).astype(BF16)
                        dk2 = dk2 + _nn(ds2, qp)
                        dv2 = dv2 + _nn(jnp.concatenate(ps, axis=0), dop)
                    dk_slab = dk_slab + _head_pair_gradient(dk2, g)
                    dv_slab = dv_slab + _head_pair_gradient(dv2, g)
                dk_slab = _rope_bwd(dk_slab, c_band, s_band)
                vcols = slice(KV_W + j * 128, KV_W + (j + 1) * 128)
                dkv_ref[:, cols] = (carry_k[:, cols] + dk_slab[:CHUNK]).astype(BF16)
                dkv_ref[:, vcols] = (carry_v[:, cols] + dv_slab[:CHUNK]).astype(BF16)
                carry_k[:, cols] = dk_slab[CHUNK:]
                carry_v[:, cols] = dv_slab[CHUNK:]
            dsink_ref[...] += dsink

        @pl.when(i == nb)
        def _():
            dkv_ref[:, :KV_W] = carry_k[...].astype(BF16)
            dkv_ref[:, KV_W:] = carry_v[...].astype(BF16)

    table = lambda which, width: pl.BlockSpec((CHUNK, width), lambda i: (which(i), 0))
    body, dep_specs, deps = _after(body, 11, after)
    return pl.pallas_call(
        body, name="bwd_attn", grid=(nb + 1,),
        in_specs=[pl.BlockSpec((CHUNK, D), lambda i: (cur(i), 0)),
                  pl.BlockSpec((CHUNK, KV_W), lambda i: (prev(i), 0)),
                  pl.BlockSpec((CHUNK, KV_W), lambda i: (cur(i), 0)),
                  pl.BlockSpec((CHUNK, KV_W), lambda i: (prev(i), OFF_VA // KV_W)),
                  pl.BlockSpec((CHUNK, KV_W), lambda i: (cur(i), OFF_VA // KV_W)),
                  table(prev, 128), table(cur, 128), table(prev, 256), table(cur, 256),
                  pl.BlockSpec(memory_space=pltpu.SMEM),
                  pl.BlockSpec((CHUNK, D), lambda i: (cur(i), 0))] + dep_specs,
        out_specs=[pl.BlockSpec((CHUNK, D), lambda i: (cur(i), 0)),
                   pl.BlockSpec((CHUNK, 2 * KV_W), lambda i: (jnp.maximum(i - 1, 0), 0)),
                   pl.BlockSpec((1, 128), lambda i: (0, 0))],
        out_shape=[SDS((T, D), BF16), SDS((T, 2 * KV_W), BF16), SDS((1, 128), F32)],
        scratch_shapes=[pltpu.VMEM((CHUNK, KV_W), F32), pltpu.VMEM((CHUNK, KV_W), F32)],
        compiler_params=_params(1),
    )(qr, kr, kr, proj, proj, cos, cos, sin, sin, sinks, datt, *deps)


def _bwd_sgu(proj, da, lng, lnb, ws, bst):
    T = proj.shape[0]
    tc = min(T, 512)
    nsteps = T // tc

    def body(u_ref, vs_ref, da_ref, lng_ref, lnb_ref, ws_ref, bst_ref,
             duv_ref, dws_ref, dbs_ref, dlng_ref, dlnb_ref, dvn_s, dgu_s, dmx_sum):
        i = pl.program_id(0)

        @pl.when(i == 0)
        def _():
            dws_ref[...] = jnp.zeros_like(dws_ref)
            dlng_ref[...] = jnp.zeros_like(dlng_ref)
            dlnb_ref[...] = jnp.zeros_like(dlnb_ref)
            dmx_sum[...] = jnp.zeros_like(dmx_sum)

        u, vs, gu, tu, tv, rstd, vhat, vn = _sgu_forward_parts(u_ref, vs_ref, lng_ref, lnb_ref)
        da = da_ref[...].astype(F32)
        for g in range(GROUPS):
            wm = _masked_ws(ws_ref, g)
            cols = slice(g * CHUNK, (g + 1) * CHUNK)
            dws = jnp.zeros((CHUNK, CHUNK), F32)
            dsum = jnp.zeros((CHUNK, CHUNK), F32)
            for c in range(tc // CHUNK):
                rows = slice(c * CHUNK, (c + 1) * CHUNK)
                vn_cg = vn[rows, cols]
                mixed = _nn(wm, vn_cg) + bst_ref[:, g:g + 1]
                dgu_s[rows, cols] = da[rows, cols] * mixed
                dmx = da[rows, cols] * gu[rows, cols]
                dmxb = dmx.astype(BF16)
                dws = dws + _nt(dmxb, vn_cg)
                dsum = dsum + dmx
                dvn_s[rows, cols] = _tn(wm, dmxb)
            dws_ref[g] += dws
            dmx_sum[:, cols] += dsum
        dvn = dvn_s[...]
        dlng_ref[...] += _colsum(dvn * vhat)
        dlnb_ref[...] += _colsum(dvn)
        dvh = dvn * lng_ref[...]
        dgv = rstd * (dvh - jnp.mean(dvh, axis=-1, keepdims=True) - vhat * jnp.mean(dvh * vhat, axis=-1, keepdims=True))
        duv_ref[:, :D] = (dgu_s[...] * _gelu_grad(u, tu)).astype(BF16)
        duv_ref[:, D:] = (dgv * _gelu_grad(vs, tv)).astype(BF16)

        @pl.when(i == nsteps - 1)
        def _():
            row = lax.broadcasted_iota(jnp.int32, (CHUNK, CHUNK), 0)
            col = lax.broadcasted_iota(jnp.int32, (CHUNK, CHUNK), 1)
            for g in range(GROUPS):
                dws_ref[g] = jnp.where(row >= col, dws_ref[g], 0.0)
                dbs_ref[g:g + 1, :] = _colsum(dmx_sum[:, g * CHUNK:(g + 1) * CHUNK].T)

    const2 = lambda i: (0, 0)
    return pl.pallas_call(
        body, name="bwd_sgu", grid=(nsteps,),
        in_specs=[pl.BlockSpec((tc, D), lambda i: (i, 0)), pl.BlockSpec((tc, D), lambda i: (i, 1)),
                  pl.BlockSpec((tc, D), lambda i: (i, 0)), pl.BlockSpec((1, D), const2), pl.BlockSpec((1, D), const2),
                  pl.BlockSpec((GROUPS, CHUNK, CHUNK), lambda i: (0, 0, 0)), pl.BlockSpec((CHUNK, GROUPS), const2)],
        out_specs=[pl.BlockSpec((tc, 2 * D), lambda i: (i, 0)), pl.BlockSpec((GROUPS, CHUNK, CHUNK), lambda i: (0, 0, 0)),
                   pl.BlockSpec((GROUPS, CHUNK), const2), pl.BlockSpec((1, D), const2), pl.BlockSpec((1, D), const2)],
        out_shape=[SDS((T, 2 * D), BF16), SDS((GROUPS, CHUNK, CHUNK), F32), SDS((GROUPS, CHUNK), F32),
                   SDS((1, D), F32), SDS((1, D), F32)],
        scratch_shapes=[pltpu.VMEM((tc, D), F32), pltpu.VMEM((tc, D), F32), pltpu.VMEM((CHUNK, D), F32)],
        compiler_params=_params(1),
    )(proj, proj, da, lng, lnb, ws, bst)


IN_SEG_WIDTHS = (2 * D, D, 2 * N_KV * HEAD, 2 * D)


def _resident(shape):
    return pl.BlockSpec(shape, lambda *_: (0,) * len(shape), pipeline_mode=pl.Buffered(1))


def _bwd_in(duv, dq, dkv, dg, win_t, x, dx1, g0, after=None):
    T = x.shape[0]
    tm = min(T, 512)

    def body(duv_ref, dq_ref, dkv_ref, dg_ref, w_ref, x_ref, dx1_ref, g0_ref, gx_ref, dg0_ref):
        @pl.when(pl.program_id(0) == 0)
        def _():
            dg0_ref[...] = jnp.zeros_like(dg0_ref)

        dh, off = None, 0
        for ref, width in zip((duv_ref, dq_ref, dkv_ref, dg_ref), IN_SEG_WIDTHS):
            part = _nn(ref[...], w_ref[off:off + width, :])
            dh = part if dh is None else dh + part
            off += width
        r0, xh = _rms_stats(x_ref[...])
        dg0_ref[...] += _colsum(dh * xh)
        gx_ref[...] = dx1_ref[...] + _rms_bwd(dh, xh, r0, g0_ref[...])

    row = lambda i: (i, 0)
    body, dep_specs, deps = _after(body, 8, after)
    return pl.pallas_call(
        body, name="bwd_in", grid=(T // tm,),
        in_specs=[pl.BlockSpec((tm, w), row) for w in IN_SEG_WIDTHS] + [
            _resident((IN_W, D)), pl.BlockSpec((tm, D), row), pl.BlockSpec((tm, D), row),
            pl.BlockSpec((1, D), lambda i: (0, 0))] + dep_specs,
        out_specs=[pl.BlockSpec((tm, D), row), pl.BlockSpec((1, D), lambda i: (0, 0))],
        out_shape=[SDS((T, D), F32), SDS((1, D), F32)],
        compiler_params=_params(1),
    )(duv, dq, dkv, dg, win_t, x, dx1, g0, *deps)


def _wgrad_rows(h, segs, first_row, into, name):
    T = h.shape[0]
    tt = min(T, 1024)
    widths = [s.shape[1] for s in segs]
    rows = sum(widths)
    n_in = 1 + len(segs) + (into is not None)

    def body(*refs):
        h_ref, seg_refs = refs[0], refs[1:1 + len(segs)]
        dw_ref, acc, stage, sem = refs[n_in], refs[n_in + 1], refs[n_in + 2], refs[n_in + 3]
        t = pl.program_id(0)

        @pl.when(t == 0)
        def _():
            acc[...] = jnp.zeros_like(acc)

        off = 0
        for ref, width in zip(seg_refs, widths):
            acc[off:off + width, :] += _tn(ref[...], h_ref[...])
            off += width

        @pl.when(t == T // tt - 1)
        def _():
            stage[...] = acc[...].astype(BF16)
            out = pltpu.make_async_copy(stage, dw_ref.at[pl.ds(first_row, rows)], sem)
            out.start()
            out.wait()

    row = lambda t: (t, 0)
    return pl.pallas_call(
        body, name=name, grid=(T // tt,),
        in_specs=[pl.BlockSpec((tt, D), row)] + [pl.BlockSpec((tt, w), row) for w in widths] + [_ANY] * (into is not None),
        out_specs=_ANY,
        out_shape=SDS((IN_W, D), BF16),
        input_output_aliases={} if into is None else {n_in - 1: 0},
        scratch_shapes=[pltpu.VMEM((rows, D), F32), pltpu.VMEM((rows, D), BF16), pltpu.SemaphoreType.DMA],
        compiler_params=_params(1),
    )(h, *segs, *([] if into is None else [into]))


def _wgrad_in(h, duv, dq, dkv, dg):
    dw = _wgrad_rows(h, [dg], IN_SEG_WIDTHS[0] + IN_SEG_WIDTHS[1] + IN_SEG_WIDTHS[2], None, "wgrad_in_gates")
    dw = _wgrad_rows(h, [duv], 0, dw, "wgrad_in_uv")
    return _wgrad_rows(h, [dq, dkv], IN_SEG_WIDTHS[0], dw, "wgrad_in_qkv")


def _place():
    x, y, c = lax.axis_index("x"), lax.axis_index("y"), lax.axis_index("c")
    return x, y, c, 4 * x + 2 * y + c


def _peers(x, y, c):
    out = []
    for mask in range(1, N_DEV):
        px = 1 - x if mask & 4 else x
        py = 1 - y if mask & 2 else y
        pc = 1 - c if mask & 1 else c
        out.append(((px, py, pc), 4 * px + 2 * py + pc))
    return out


def _all_to_all(arrays, gather, name, after=None):
    n = len(arrays)

    def body(*refs):
        ins, outs = refs[:n], refs[n:2 * n]
        send_sems, recv_sems, local_sems = refs[2 * n:]
        x, y, c, me = _place()
        local, sends, recvs = [], [], []
        for a in range(n):
            src_own = ins[a] if gather[a] else ins[a].at[me]
            local.append(pltpu.make_async_copy(src_own, outs[a].at[me], local_sems.at[a]))
            for k, (peer, pid) in enumerate(_peers(x, y, c)):
                sem = a * (N_DEV - 1) + k
                src = ins[a] if gather[a] else ins[a].at[pid]
                sends.append(pltpu.make_async_remote_copy(
                    src_ref=src, dst_ref=outs[a].at[me], send_sem=send_sems.at[sem], recv_sem=recv_sems.at[sem],
                    device_id=peer, device_id_type=MESH))
                recvs.append(pltpu.make_async_remote_copy(
                    src_ref=src, dst_ref=outs[a].at[pid], send_sem=send_sems.at[sem], recv_sem=recv_sems.at[sem],
                    device_id=peer, device_id_type=MESH))
        for cp in local + sends:
            cp.start()
        for cp in recvs:
            cp.wait_recv()
        for cp in sends:
            cp.wait_send()
        for cp in local:
            cp.wait()

    out_shape = [SDS((N_DEV,) + a.shape if gt else a.shape, a.dtype) for a, gt in zip(arrays, gather)]
    nsem = n * (N_DEV - 1)
    body, dep_specs, deps = _after(body, n, after)
    return pl.pallas_call(
        body, name=name,
        in_specs=[pl.BlockSpec(memory_space=pl.ANY)] * n + dep_specs,
        out_specs=[pl.BlockSpec(memory_space=pl.ANY)] * n,
        out_shape=out_shape,
        scratch_shapes=[pltpu.SemaphoreType.DMA((nsem,)), pltpu.SemaphoreType.DMA((nsem,)), pltpu.SemaphoreType.DMA((n,))],
    )(*arrays, *deps)


_HBM = pl.BlockSpec(memory_space=pltpu.HBM)
_SEM = pl.BlockSpec(memory_space=pltpu.SEMAPHORE)
_EFFECT = pltpu.SideEffectType.DATAFLOW_SIDE_EFFECTING
GATHER = "gather"
SCATTER = "scatter"
SPREAD = "spread"


def _zone_shape(a, mode):
    if mode == GATHER:
        return (N_DEV,) + a.shape
    return (N_DEV - 1,) + (a.shape[1:] if mode == SCATTER else a.shape)


def _start_copies(arrays, modes, name, after=None):
    n = len(arrays)
    zones = [lax.empty(_zone_shape(a, m), a.dtype) for a, m in zip(arrays, modes)]

    def body(*refs):
        ins, lands = refs[:n], refs[n:2 * n]
        send_sems, recv_sems = refs[-2 * n - 3], refs[-2 * n - 2]
        token = refs[-1]
        x, y, c, me = _place()
        for a in range(n):
            for k, (peer, pid) in enumerate(_peers(x, y, c)):
                src = ins[a].at[pid] if modes[a] == SCATTER else ins[a]
                dst = lands[a].at[me] if modes[a] == GATHER else lands[a].at[k]
                pltpu.make_async_remote_copy(src_ref=src, dst_ref=dst, send_sem=send_sems.at[a], recv_sem=recv_sems.at[a],
                                             device_id=peer, device_id_type=MESH).start()
        token[...] = jnp.zeros_like(token)

    hbm = lambda a: pltpu.HBM(a.shape, a.dtype)
    sems = pltpu.SemaphoreType.DMA((n,))
    extra = [] if after is None else [after]
    operands = [pltpu.with_memory_space_constraint(a, pltpu.HBM) for a in list(arrays) + zones]
    res = pl.pallas_call(
        body, name=name,
        out_shape=(sems, sems, *[hbm(a) for a in arrays], *[hbm(z) for z in zones], SDS((8, 128), F32)),
        in_specs=[_HBM] * (2 * n) + [_ANY] * len(extra),
        out_specs=(_SEM, _SEM, *[_HBM] * (2 * n), pl.BlockSpec(memory_space=pltpu.VMEM)),
        input_output_aliases={i: 2 + i for i in range(2 * n)},
        compiler_params=pltpu.CompilerParams(has_side_effects=_EFFECT),
    )(*operands, *extra)
    return res[0], res[1], list(res[2:2 + n]), list(res[2 + n:2 + 2 * n]), res[-1]


def _wait_copies(started, after, name, count=N_DEV - 1):
    send_sems, recv_sems, thru, zones, _ = started
    nt, nz = len(thru), len(zones)

    def body(*refs):
        lands = refs[nt:nt + nz]
        send_ref, recv_ref = refs[nt + nz], refs[nt + nz + 1]
        x, y, c, _ = _place()
        for a in range(nz):
            blocks = lands[a].at[pl.ds(0, count)]
            cp = pltpu.make_async_remote_copy(src_ref=blocks, dst_ref=blocks, send_sem=send_ref.at[a], recv_sem=recv_ref.at[a],
                                              device_id=(x, y, 1 - c), device_id_type=MESH)
            cp.wait_send()
            cp.wait_recv()

    hbm = lambda a: pltpu.HBM(a.shape, a.dtype)
    res = pl.pallas_call(
        body, name=name,
        out_shape=tuple(hbm(a) for a in thru + zones),
        in_specs=[_HBM] * (nt + nz) + [_SEM, _SEM, _ANY],
        out_specs=tuple([_HBM] * (nt + nz)),
        input_output_aliases={i: i for i in range(nt + nz)},
        compiler_params=pltpu.CompilerParams(has_side_effects=_EFFECT),
    )(*thru, *zones, send_sems, recv_sems, after)
    return list(res[:nt]), list(res[nt:])


def _split_start(body, arrays, zones, name, after):
    n = len(arrays) + len(zones)
    hbm = lambda a: pltpu.HBM(a.shape, a.dtype)
    sems = pltpu.SemaphoreType.DMA((max(len(zones), 1),))
    extra = [] if after is None else [after]
    operands = [pltpu.with_memory_space_constraint(a, pltpu.HBM) for a in list(arrays) + list(zones)]
    res = pl.pallas_call(
        body, name=name,
        out_shape=(sems, sems, *[hbm(a) for a in operands], SDS((8, 128), F32)),
        in_specs=[_HBM] * n + [_ANY] * len(extra),
        out_specs=(_SEM, _SEM, *[_HBM] * n, pl.BlockSpec(memory_space=pltpu.VMEM)),
        input_output_aliases={i: 2 + i for i in range(n)},
        compiler_params=pltpu.CompilerParams(has_side_effects=_EFFECT),
    )(*operands, *extra)
    return res[0], res[1], list(res[2:2 + len(arrays)]), list(res[2 + len(arrays):2 + n]), res[-1]


def _gather_first_leg(shard, name, after=None):
    zone = lax.empty((N_DEV,) + shard.shape, shard.dtype)
    extra = 0 if after is None else 1

    def body(*refs):
        src, land = refs[0], refs[1]
        send_sem, recv_sem, token = refs[2 + extra], refs[3 + extra], refs[-1]
        x, y, c, me = _place()
        for peer in ((x, y, 1 - c), (1 - x, y, c), (x, 1 - y, c), (1 - x, 1 - y, c)):
            pltpu.make_async_remote_copy(src_ref=src, dst_ref=land.at[me], send_sem=send_sem.at[0], recv_sem=recv_sem.at[0],
                                         device_id=peer, device_id_type=MESH).start()
        token[...] = jnp.zeros_like(token)

    return _split_start(body, [shard], [zone], name, after)


def _gather_second_leg(zone, name, after=None):
    extra = 0 if after is None else 1

    def body(*refs):
        land = refs[0]
        send_sem, recv_sem, token = refs[1 + extra], refs[2 + extra], refs[-1]
        x, y, c, _ = _place()
        for px, py in ((1 - x, y), (x, 1 - y), (1 - x, 1 - y)):
            slot = 4 * px + 2 * py + c
            pltpu.make_async_remote_copy(src_ref=land.at[slot], dst_ref=land.at[slot], send_sem=send_sem.at[0],
                                         recv_sem=recv_sem.at[0], device_id=(x, y, 1 - c), device_id_type=MESH).start()
        token[...] = jnp.zeros_like(token)

    return _split_start(body, [], [zone], name, after)


UPDATE_BLOCK_ELEMS = 256 * 1024


def _update_rows(R, C):
    fits = [t for t in range(8, R + 1, 8) if R % t == 0 and t * C <= UPDATE_BLOCK_ELEMS]
    whole = [t for t in fits if t % 16 == 0]
    return max(whole or fits)


def _adamw_math(g, w, m, v):
    m2 = ADAM_B1 * m + (1.0 - ADAM_B1) * g
    v2 = ADAM_B2 * v + (1.0 - ADAM_B2) * (g * g)
    m_hat = m2 / (1.0 - ADAM_B1 ** ADAM_STEP)
    v_hat = v2 / (1.0 - ADAM_B2 ** ADAM_STEP)
    delta = -ADAM_LR * (m_hat / (jnp.sqrt(v_hat) + ADAM_EPS) + ADAM_WD * w)
    return delta, m2, v2


def _sum_adamw(parts, w, m, v, name):
    R, C = w.shape
    tr = _update_rows(R, C)

    def body(p_ref, w_ref, m_ref, v_ref, g_ref, d_ref, m2_ref, v2_ref):
        g = p_ref[0]
        for k in range(1, N_DEV):
            g = g + p_ref[k]
        g_ref[...] = g
        d_ref[...], m2_ref[...], v2_ref[...] = _adamw_math(g, w_ref[...], m_ref[...], v_ref[...])

    blk = pl.BlockSpec((tr, C), lambda i: (i, 0))
    return pl.pallas_call(
        body, name=name, grid=(R // tr,),
        in_specs=[pl.BlockSpec((N_DEV, tr, C), lambda i: (0, i, 0)), blk, blk, blk],
        out_specs=[blk] * 4,
        out_shape=[SDS((R, C), F32)] * 4,
        compiler_params=_params(1),
    )(parts, w, m, v)


def _sum_adamw_peers(me, own, parts, w, m, v, name, replicated):
    R, C = w.shape
    tr = _update_rows(R, C)

    def body(me_ref, own_ref, p_ref, w_ref, m_ref, v_ref, g_ref, d_ref, m2_ref, v2_ref):
        if replicated:
            mine = me_ref[0]
            g = None
            for j in range(N_DEV):
                k = jnp.maximum(jnp.bitwise_xor(mine, j) - 1, 0)
                term = jnp.where(mine == j, own_ref[...], p_ref[k])
                g = term if g is None else g + term
        else:
            g = own_ref[...].astype(F32)
            for k in range(N_DEV - 1):
                g = g + p_ref[k].astype(F32)
        g_ref[...] = g
        d_ref[...], m2_ref[...], v2_ref[...] = _adamw_math(g, w_ref[...], m_ref[...], v_ref[...])

    blk = pl.BlockSpec((tr, C), lambda i, me_ref: (i, 0))
    own_spec = blk if replicated else pl.BlockSpec((None, tr, C), lambda i, me_ref: (me_ref[0], i, 0))
    return pl.pallas_call(
        body, name=name,
        grid_spec=pltpu.PrefetchScalarGridSpec(
            num_scalar_prefetch=1, grid=(R // tr,),
            in_specs=[own_spec, pl.BlockSpec((N_DEV - 1, tr, C), lambda i, me_ref: (0, i, 0)), blk, blk, blk],
            out_specs=[blk] * 4),
        out_shape=[SDS((R, C), F32)] * 4,
        compiler_params=_params(1),
    )(me, own, parts, w, m, v)


SMALL = ("ln_v_gain", "ln_v_bias", "w_spatial", "b_spatial", "sinks", "norm_mix_post", "norm_ff_pre", "norm_ff_post")
SMALL_ROWS = {"ln_v_gain": 8, "ln_v_bias": 8, "w_spatial": 1024, "b_spatial": 8, "sinks": 8,
              "norm_mix_post": 8, "norm_ff_pre": 8, "norm_ff_post": 8}
SMALL_PACK_ROWS = 1152


def _pack_small(vals):
    rows = []
    for name in SMALL:
        flat = vals[name].reshape(-1)
        pad = SMALL_ROWS[name] * 128 - flat.shape[0]
        if pad:
            flat = jnp.concatenate([flat, jnp.zeros((pad,), F32)])
        rows.append(flat.reshape(SMALL_ROWS[name], 128))
    rows.append(jnp.zeros((SMALL_PACK_ROWS - sum(SMALL_ROWS.values()), 128), F32))
    return jnp.concatenate(rows, axis=0)


def _unpack_small(packed, shapes):
    out, r = {}, 0
    for name in SMALL:
        n = 1
        for s in shapes[name]:
            n *= s
        out[name] = packed[r:r + SMALL_ROWS[name]].reshape(-1)[:n].reshape(shapes[name])
        r += SMALL_ROWS[name]
    return out


def _rope_rows():
    d = jnp.arange(128) % HEAD
    inv = ROPE_THETA ** (-(2.0 * (d % (ROPE // 2))).astype(F32) / ROPE)
    invf = jnp.where(d < ROPE, inv, 0.0).astype(F32).reshape(1, 128)
    sgn = jnp.where(d < ROPE // 2, -1.0, jnp.where(d < ROPE, 1.0, 0.0)).astype(F32).reshape(1, 128)
    return invf, sgn


def kernel(x, positions, w_in, ln_v_gain, ln_v_bias, w_spatial, b_spatial, sinks, w_a, w_b, w_o, norm_mix_pre, norm_mix_post, w_ff_in, w_ff_out, norm_ff_pre, norm_ff_post, loss_target, m_w_in, m_ln_v_gain, m_ln_v_bias, m_w_spatial, m_b_spatial, m_sinks, m_w_a, m_w_b, m_w_o, m_norm_mix_pre, m_norm_mix_post, m_w_ff_in, m_w_ff_out, m_norm_ff_pre, m_norm_ff_post, v_w_in, v_ln_v_gain, v_ln_v_bias, v_w_spatial, v_b_spatial, v_sinks, v_w_a, v_w_b, v_w_o, v_norm_mix_pre, v_norm_mix_post, v_w_ff_in, v_w_ff_out, v_norm_ff_pre, v_norm_ff_post):
    given = dict(locals())
    T = x.shape[1]
    xt = x[0]
    tgt = loss_target[0]
    bst = b_spatial[0].T
    ws = w_spatial[0]

    me = 4 * lax.axis_index("x") + 2 * lax.axis_index("y") + lax.axis_index("c")
    me_arr = me.astype(jnp.int32).reshape(1)

    def with_own(zone, shard):
        return lax.dynamic_update_slice(zone, shard[None], (me,) + (0,) * shard.ndim)

    rest = ("w_a", "w_b", "w_o", "w_ff_in", "w_ff_out")
    shard = {n: given[n][0].astype(BF16) for n in rest}
    g_one = _gather_first_leg(w_in[0].T.astype(BF16), "gather_in_start")
    cos, sin = _rope_tables(positions.astype(F32).reshape(T, 1), *_rope_rows(), after=g_one[-1])
    h = _rms_pre(xt, norm_mix_pre, after=cos)
    (own_win,), (win8,) = _wait_copies(g_one, h, "gather_in_wait", count=4)
    g_two = _gather_second_leg(win8, "gather_in_pass_start")
    g_rest = _start_copies([shard[n] for n in rest], [GATHER] * len(rest), "gather_rest_start", after=g_two[-1])
    _, (win8,) = _wait_copies(g_two, g_rest[-1], "gather_in_pass_wait", count=3)
    win = with_own(win8, own_win).reshape(IN_W, D)

    proj = _fwd_in(h, win)
    att, qr, kr = _fwd_attn(proj, cos, sin, sinks[0])
    a = _fwd_sgu(proj, ln_v_gain, ln_v_bias, ws, bst, after=att)
    gw = {n: with_own(z, own) for n, own, z in zip(rest, *_wait_copies(g_rest, a, "gather_rest_wait"))}
    wa, wb, wo = (gw[n].reshape(D, D) for n in ("w_a", "w_b", "w_o"))
    wfi3 = gw["w_ff_in"]
    wfo = gw["w_ff_out"].reshape(D_FF, D)
    merged, a2, b2, mix, x1, hf = _fwd_mix(a, att, proj, xt, wa, wb, wo, norm_mix_post, norm_ff_pre)
    f, dy, dff, dg3, loss_part = _fwd_ff(hf, wfi3, wfo, x1, tgt, norm_ff_post)

    df, dx1, dmix, dg2, dg1 = _bwd_ff(dff, f, wfi3, wfo, x1, dy, mix, norm_mix_post, norm_ff_pre)
    dwfi3, dwfo = _wgrad_ff(hf, df, f, dff)
    own_ff = [dwfi3, dwfo.reshape(N_DEV, D_FF // N_DEV, D)]
    x_ff = _start_copies(own_ff, [SCATTER] * 2, "exchange_ff_start")
    da2, db2, dgate, da, datt = _bwd_mix(dmix, proj, a2, b2, wo, wa, wb, after=x_ff[-1])
    dwo, dwa, dwb = _wgrad_mix(merged, dmix, a, da2, att, db2)
    own_mix = [g.reshape(N_DEV, D // N_DEV, D) for g in (dwa, dwb, dwo)]
    x_mix = _start_copies(own_mix, [SCATTER] * 3, "exchange_mix_start")
    dq, dkv, dsink = _bwd_attn(qr, kr, proj, cos, sin, sinks[0], datt, after=x_mix[-1])
    duv, dws, dbs, dlng, dlnb = _bwd_sgu(proj, da, ln_v_gain, ln_v_bias, ws, bst)
    small_grads = {"ln_v_gain": dlng, "ln_v_bias": dlnb, "w_spatial": dws, "b_spatial": dbs, "sinks": dsink[:, :N_Q],
                   "norm_mix_post": dg1, "norm_ff_pre": dg2, "norm_ff_post": dg3}
    x_small = _start_copies([_pack_small(small_grads)], [SPREAD], "exchange_small_start")
    dwin = _wgrad_in(h, duv, dq, dkv, dgate)
    own_in = [dwin.reshape(N_DEV, IN_W // N_DEV, D)]
    x_in = _start_copies(own_in, [SCATTER], "exchange_in_start", after=x_small[-1])
    grad_x, dg0 = _bwd_in(duv, dq, dkv, dgate, win, xt, dx1, norm_mix_pre, after=x_in[-1])

    results = {}

    def update(n, own, parts, transposed=False):
        state = [given[k + n][0].T if transposed else given[k + n][0] for k in ("", "m_", "v_")]
        res = _sum_adamw_peers(me_arr, own, parts, *state, "adamw_" + n, False)
        results[n] = [(r.T if transposed else r).reshape(given[n].shape) for r in res]

    own_ff, p_ff = _wait_copies(x_ff, grad_x, "exchange_ff_wait")
    update("w_ff_in", own_ff[0], p_ff[0])
    update("w_ff_out", own_ff[1], p_ff[1])
    own_mix, p_mix = _wait_copies(x_mix, results["w_ff_out"][0], "exchange_mix_wait")
    for n, own, parts in zip(("w_a", "w_b", "w_o"), own_mix, p_mix):
        update(n, own, parts)
    tail = jnp.concatenate([dg0.reshape(8, 128), jnp.tile(loss_part, (8, 1))], axis=0)
    (tail_all,) = _all_to_all([tail], [True], "exchange_tail", after=results["w_o"][0])
    dg0_all = tail_all[:, :8]
    own_small, p_small = _wait_copies(x_small, tail_all, "exchange_small_wait")
    own_in, p_in = _wait_copies(x_in, p_small[0], "exchange_in_wait")
    update("w_in", own_in[0], p_in[0], transposed=True)
    packed = _sum_adamw_peers(me_arr, own_small[0], p_small[0], _pack_small({n: given[n] for n in SMALL}),
                              _pack_small({n: given["m_" + n] for n in SMALL}),
                              _pack_small({n: given["v_" + n] for n in SMALL}), "adamw_small", True)
    shapes = {n: given[n].shape for n in SMALL}
    unpacked = [_unpack_small(p, shapes) for p in packed]
    for n in SMALL:
        results[n] = [u[n] for u in unpacked]
    n = "norm_mix_pre"
    results[n] = [r.reshape(given[n].shape) for r in _sum_adamw(
        dg0_all, given[n].reshape(8, 128), given["m_" + n].reshape(8, 128), given["v_" + n].reshape(8, 128), "adamw_" + n)]

    loss = jnp.sum(tail_all[:, 8, 0])
    order = ("w_in", "ln_v_gain", "ln_v_bias", "w_spatial", "b_spatial", "sinks", "w_a", "w_b", "w_o", "norm_mix_pre",
             "norm_mix_post", "w_ff_in", "w_ff_out", "norm_ff_pre", "norm_ff_post")
    out = [loss, grad_x.reshape(x.shape)]
    for k in range(4):
        out += [results[n][k] for n in order]
    return tuple(out)
```

```python
import functools

import jax
import jax.numpy as jnp
from jax import lax
from jax.experimental import pallas as pl
from jax.experimental.pallas import tpu as pltpu

F32 = jnp.float32
BF16 = jnp.bfloat16

N_DEV = 8
D = 1024
D_FF = 4096
IN_W = 5632
CHUNK = 128
GROUPS = 8
HEAD = 64
N_Q = 16
N_KV = 4
ROPE = 16
ROPE_THETA = 500000.0
EPS = 1e-6
OFF_Q, OFF_K, OFF_VA, OFF_GA, OFF_GB = 2048, 3072, 3328, 3584, 4608

ADAM_LR = 0.001
ADAM_B1 = 0.9
ADAM_B2 = 0.999
ADAM_EPS = 1e-08
ADAM_WD = 0.01
ADAM_STEP = 10

VMEM_LIMIT = 56 * 1024 * 1024

SDS = jax.ShapeDtypeStruct
MESH = pl.DeviceIdType.MESH


def _params(n_axes=None):
    if n_axes is None:
        return pltpu.CompilerParams(vmem_limit_bytes=VMEM_LIMIT)
    return pltpu.CompilerParams(dimension_semantics=("arbitrary",) * n_axes, vmem_limit_bytes=VMEM_LIMIT)


def _nt(a, b):
    return lax.dot_general(a, b, (((1,), (1,)), ((), ())), preferred_element_type=F32)


def _tn(a, b):
    return lax.dot_general(a, b, (((0,), (0,)), ((), ())), preferred_element_type=F32)


def _nn(a, b):
    return jnp.dot(a, b, preferred_element_type=F32)


def _gelu(x):
    t = jnp.tanh(0.7978845608028654 * (x + 0.044715 * (x * x * x)))
    return 0.5 * x * (1.0 + t), t


def _gelu_grad(x, t):
    return 0.5 * (1.0 + t) + 0.5 * x * (1.0 - t * t) * (0.7978845608028654 * (1.0 + 3.0 * 0.044715 * x * x))


def _sigmoid(x):
    return 1.0 / (1.0 + jnp.exp(-x))


def _rms_stats(v):
    r = lax.rsqrt(jnp.mean(v * v, axis=-1, keepdims=True) + EPS)
    return r, v * r


def _rms_bwd(d, vhat, r, g):
    gd = g * d
    return r * (gd - vhat * jnp.mean(gd * vhat, axis=-1, keepdims=True))


def _colsum(v):
    return jnp.sum(v, axis=0, keepdims=True)


_ANY = pl.BlockSpec(memory_space=pl.ANY)


def _after(body, n_in, after):
    if after is None:
        return body, [], []

    def ordered(*refs):
        return body(*refs[:n_in], *refs[n_in + 1:])

    return ordered, [_ANY], [after]


def _rms_pre(x, g0, after=None):
    T = x.shape[0]
    tm = min(T, 1024)

    def body(x_ref, g_ref, h_ref):
        _, xh = _rms_stats(x_ref[...])
        h_ref[...] = (xh * g_ref[...]).astype(BF16)

    body, dep_specs, deps = _after(body, 2, after)
    return pl.pallas_call(
        body, name="rms_pre", grid=(T // tm,),
        in_specs=[pl.BlockSpec((tm, D), lambda i: (i, 0)), pl.BlockSpec((1, D), lambda i: (0, 0))] + dep_specs,
        out_specs=pl.BlockSpec((tm, D), lambda i: (i, 0)),
        out_shape=SDS((T, D), BF16),
        compiler_params=_params(1),
    )(x, g0, *deps)


def _fwd_in(h, win_t):
    T = h.shape[0]
    tm, tn = min(T, 1024), 1408

    def body(h_ref, w_ref, p_ref):
        p_ref[...] = _nt(h_ref[...], w_ref[...]).astype(BF16)

    return pl.pallas_call(
        body, name="fwd_in", grid=(T // tm, IN_W // tn),
        in_specs=[pl.BlockSpec((tm, D), lambda i, j: (i, 0)), pl.BlockSpec((tn, D), lambda i, j: (j, 0))],
        out_specs=pl.BlockSpec((tm, tn), lambda i, j: (i, j)),
        out_shape=SDS((T, IN_W), BF16),
        compiler_params=_params(2),
    )(h, win_t)


def _sgu_forward_parts(u_ref, vs_ref, lng_ref, lnb_ref):
    u = u_ref[...].astype(F32)
    vs = vs_ref[...].astype(F32)
    gu, tu = _gelu(u)
    gv, tv = _gelu(vs)
    mu = jnp.mean(gv, axis=-1, keepdims=True)
    dv = gv - mu
    rstd = lax.rsqrt(jnp.mean(dv * dv, axis=-1, keepdims=True) + EPS)
    vhat = dv * rstd
    vn = (vhat * lng_ref[...] + lnb_ref[...]).astype(BF16)
    return u, vs, gu, tu, tv, rstd, vhat, vn


def _masked_ws(ws_ref, g):
    row = lax.broadcasted_iota(jnp.int32, (CHUNK, CHUNK), 0)
    col = lax.broadcasted_iota(jnp.int32, (CHUNK, CHUNK), 1)
    return jnp.where(row >= col, ws_ref[g], 0.0).astype(BF16)


def _fwd_sgu(proj, lng, lnb, ws, bst, after=None):
    T = proj.shape[0]
    tc = min(T, 512)

    def body(u_ref, vs_ref, lng_ref, lnb_ref, ws_ref, bst_ref, a_ref):
        _, _, gu, _, _, _, _, vn = _sgu_forward_parts(u_ref, vs_ref, lng_ref, lnb_ref)
        for g in range(GROUPS):
            wm = _masked_ws(ws_ref, g)
            cols = slice(g * CHUNK, (g + 1) * CHUNK)
            for c in range(tc // CHUNK):
                rows = slice(c * CHUNK, (c + 1) * CHUNK)
                mixed = _nn(wm, vn[rows, cols]) + bst_ref[:, g:g + 1]
                a_ref[rows, cols] = (gu[rows, cols] * mixed).astype(BF16)

    body, dep_specs, deps = _after(body, 6, after)
    return pl.pallas_call(
        body, name="fwd_sgu", grid=(T // tc,),
        in_specs=[pl.BlockSpec((tc, D), lambda i: (i, 0)), pl.BlockSpec((tc, D), lambda i: (i, 1)),
                  pl.BlockSpec((1, D), lambda i: (0, 0)), pl.BlockSpec((1, D), lambda i: (0, 0)),
                  pl.BlockSpec((GROUPS, CHUNK, CHUNK), lambda i: (0, 0, 0)),
                  pl.BlockSpec((CHUNK, GROUPS), lambda i: (0, 0))] + dep_specs,
        out_specs=pl.BlockSpec((tc, D), lambda i: (i, 0)),
        out_shape=SDS((T, D), BF16),
        compiler_params=_params(1),
    )(proj, proj, lng, lnb, ws, bst, *deps)


def _rope_tables(posf, invf, sgn, after=None):
    T = posf.shape[0]
    tr = min(T, 1024)

    def body(pos_ref, invf_ref, sgn_ref, c_ref, s_ref):
        ang = pos_ref[...] * invf_ref[...]
        c_ref[...] = jnp.cos(ang)
        s = jnp.sin(ang)
        s_ref[:, :128] = jnp.where(sgn_ref[...] < 0.0, -s, 0.0)
        s_ref[:, 128:] = jnp.where(sgn_ref[...] > 0.0, s, 0.0)

    body, dep_specs, deps = _after(body, 3, after)
    return pl.pallas_call(
        body, name="rope_tables", grid=(T // tr,),
        in_specs=[pl.BlockSpec((tr, 1), lambda i: (i, 0)), pl.BlockSpec((1, 128), lambda i: (0, 0)),
                  pl.BlockSpec((1, 128), lambda i: (0, 0))] + dep_specs,
        out_specs=[pl.BlockSpec((tr, 128), lambda i: (i, 0)), pl.BlockSpec((tr, 256), lambda i: (i, 0))],
        out_shape=[SDS((T, 128), F32), SDS((T, 256), F32)],
        compiler_params=_params(1),
    )(posf, invf, sgn, *deps)


def _rope(v, c, s):
    v = v.astype(F32)
    return v * c + pltpu.roll(v, 128 - ROPE // 2, 1) * s[:, :128] + pltpu.roll(v, ROPE // 2, 1) * s[:, 128:]


def _rope_bwd(dv, c, s):
    return dv * c + pltpu.roll(dv * s[:, :128], ROPE // 2, 1) + pltpu.roll(dv * s[:, 128:], 128 - ROPE // 2, 1)


def _fold_masks(first):
    jj = lax.broadcasted_iota(jnp.int32, (CHUNK, CHUNK), 0)
    t = lax.broadcasted_iota(jnp.int32, (CHUNK, CHUNK), 1)
    prev = jj > t
    return prev, jnp.where(prev & first, -1e30, 0.0)


def _fold(band, prev):
    return jnp.where(prev, band[:CHUNK], band[CHUNK:])


def _unfold(folded, prev):
    return jnp.concatenate([jnp.where(prev, folded, 0.0), jnp.where(prev, 0.0, folded)], axis=0)


def _softmax_sink(s, sink, key_axis):
    m = jnp.maximum(jnp.max(s, axis=key_axis, keepdims=True), sink)
    p = jnp.exp(s - m)
    esink = jnp.exp(sink - m)
    inv = 1.0 / (jnp.sum(p, axis=key_axis, keepdims=True) + esink)
    return p * inv, esink * inv


def _head_pair_operand(slab, g):
    lo = lax.broadcasted_iota(jnp.int32, slab.shape, 1) < HEAD
    if g % 2 == 0:
        first = jnp.where(lo, slab, 0.0)
        second = pltpu.roll(first, HEAD, 1)
    else:
        second = jnp.where(lo, 0.0, slab)
        first = pltpu.roll(second, HEAD, 1)
    return jnp.concatenate([first, second], axis=0).astype(BF16)


def _head_pair_gradient(acc, g):
    top, bot = acc[:2 * CHUNK], acc[2 * CHUNK:]
    lo = lax.broadcasted_iota(jnp.int32, top.shape, 1) < HEAD
    if g % 2 == 0:
        return jnp.where(lo, top, 0.0) + pltpu.roll(jnp.where(lo, 0.0, bot), HEAD, 1)
    return pltpu.roll(jnp.where(lo, top, 0.0), HEAD, 1) + jnp.where(lo, 0.0, bot)


PAIRS_PER_KV = N_Q // N_KV // 2
KV_W = N_KV * HEAD


def _band(prev_ref, cur_ref, cols=slice(None)):
    return jnp.concatenate([prev_ref[:, cols], cur_ref[:, cols]], axis=0)


def _fwd_attn(proj, cos, sin, sinks):
    T = proj.shape[0]
    nb = T // CHUNK
    cur = lambda i: i
    prev = lambda i: jnp.maximum(i - 1, 0)

    def body(q_ref, kp_ref, kc_ref, vp_ref, vc_ref, cp_ref, cc_ref, sp_ref, sc_ref, sink_ref,
             o_ref, qr_ref, kr_ref, p_ref, psink_ref):
        prev_slot, bias = _fold_masks(pl.program_id(0) == 0)
        c_band, s_band = _band(cp_ref, cc_ref), _band(sp_ref, sc_ref)
        for j in range(KV_W // 128):
            cols = slice(j * 128, (j + 1) * 128)
            k_slab = _rope(_band(kp_ref, kc_ref, cols), c_band, s_band)
            kr_ref[:, cols] = k_slab[CHUNK:].astype(BF16)
            v_slab = _band(vp_ref, vc_ref, cols).astype(F32)
            for g in (2 * j, 2 * j + 1):
                k2 = _head_pair_operand(k_slab, g)
                v2 = _head_pair_operand(v_slab, g)
                for r in range(PAIRS_PER_KV):
                    lanes = slice((g * PAIRS_PER_KV + r) * 128, (g * PAIRS_PER_KV + r + 1) * 128)
                    qp = (_rope(q_ref[:, lanes], cc_ref[...], sc_ref[...]) * (HEAD ** -0.5)).astype(BF16)
                    qr_ref[:, lanes] = qp
                    s2 = _nt(k2, qp)
                    ps = []
                    for e in range(2):
                        head = 2 * (g * PAIRS_PER_KV + r) + e
                        s = _fold(s2[e * 2 * CHUNK:(e + 1) * 2 * CHUNK], prev_slot) + bias
                        p, psink = _softmax_sink(s, sink_ref[head], 0)
                        p = p.astype(BF16)
                        p_ref[head] = p
                        psink_ref[head:head + 1, :] = psink
                        ps.append(_unfold(p, prev_slot))
                    o_ref[:, lanes] = _tn(jnp.concatenate(ps, axis=0), v2).astype(BF16)

    table = lambda which, width: pl.BlockSpec((CHUNK, width), lambda i: (which(i), 0))
    return pl.pallas_call(
        body, name="fwd_attn", grid=(nb,),
        in_specs=[pl.BlockSpec((CHUNK, D), lambda i: (i, OFF_Q // D)),
                  pl.BlockSpec((CHUNK, KV_W), lambda i: (prev(i), OFF_K // KV_W)),
                  pl.BlockSpec((CHUNK, KV_W), lambda i: (i, OFF_K // KV_W)),
                  pl.BlockSpec((CHUNK, KV_W), lambda i: (prev(i), OFF_VA // KV_W)),
                  pl.BlockSpec((CHUNK, KV_W), lambda i: (i, OFF_VA // KV_W)),
                  table(prev, 128), table(cur, 128), table(prev, 256), table(cur, 256),
                  pl.BlockSpec(memory_space=pltpu.SMEM)],
        out_specs=[pl.BlockSpec((CHUNK, D), lambda i: (i, 0)), pl.BlockSpec((CHUNK, D), lambda i: (i, 0)),
                   pl.BlockSpec((CHUNK, KV_W), lambda i: (i, 0)),
                   pl.BlockSpec((None, N_Q, CHUNK, CHUNK), lambda i: (i, 0, 0, 0)),
                   pl.BlockSpec((None, N_Q, CHUNK), lambda i: (i, 0, 0))],
        out_shape=[SDS((T, D), BF16), SDS((T, D), BF16), SDS((T, KV_W), BF16),
                   SDS((nb, N_Q, CHUNK, CHUNK), BF16), SDS((nb, N_Q, CHUNK), F32)],
        compiler_params=_params(1),
    )(proj, proj, proj, proj, proj, cos, cos, sin, sin, sinks)


def _row_halves(tm):
    return [slice(0, tm // 2), slice(tm // 2, tm)] if tm % 32 == 0 else [slice(0, tm)]


def _fwd_mix(a, att, proj, x, wa, wb, wo, g1, g2):
    T = x.shape[0]
    tm = min(T, 512)
    half = D // 2

    def body(a_ref, att_ref, ga0, ga1, gb0, gb1, x_ref, wa_ref, wb_ref, wo_ref, g1_ref, g2_ref,
             mg_ref, a2_ref, b2_ref, mix_ref, x1_ref, hf_ref):
        for rows in _row_halves(tm):
            a2 = _nn(a_ref[rows, :], wa_ref[...])
            b2 = _nn(att_ref[rows, :], wb_ref[...])
            ga = jnp.concatenate([ga0[rows, :], ga1[rows, :]], axis=1).astype(F32)
            gb = jnp.concatenate([gb0[rows, :], gb1[rows, :]], axis=1).astype(F32)
            merged = (_sigmoid(ga) * a2 + _sigmoid(gb) * b2).astype(BF16)
            a2_ref[rows, :] = a2.astype(BF16)
            b2_ref[rows, :] = b2.astype(BF16)
            mg_ref[rows, :] = merged
            mix = _nn(merged, wo_ref[...])
            mix_ref[rows, :] = mix
            _, mh = _rms_stats(mix)
            x1 = x_ref[rows, :] + mh * g1_ref[...]
            x1_ref[rows, :] = x1
            _, xh = _rms_stats(x1)
            hf_ref[rows, :] = (xh * g2_ref[...]).astype(BF16)

    row = lambda i: (i, 0)
    const = lambda i: (0, 0)
    gspec = lambda off: pl.BlockSpec((tm, half), lambda i: (i, off // half))
    return pl.pallas_call(
        body, name="fwd_mix", grid=(T // tm,),
        in_specs=[pl.BlockSpec((tm, D), row), pl.BlockSpec((tm, D), row),
                  gspec(OFF_GA), gspec(OFF_GA + half), gspec(OFF_GB), gspec(OFF_GB + half),
                  pl.BlockSpec((tm, D), row), _resident((D, D)), _resident((D, D)),
                  _resident((D, D)), pl.BlockSpec((1, D), const), pl.BlockSpec((1, D), const)],
        out_specs=[pl.BlockSpec((tm, D), row)] * 6,
        out_shape=[SDS((T, D), BF16), SDS((T, D), BF16), SDS((T, D), BF16), SDS((T, D), F32), SDS((T, D), F32),
                   SDS((T, D), BF16)],
        compiler_params=_params(1),
    )(a, att, proj, proj, proj, proj, x, wa, wb, wo, g1, g2)


FF_SPLIT = N_DEV
FF_TILE = D_FF // FF_SPLIT
FF_STEP = 2048
FF_SLABS = FF_STEP // FF_TILE
FF_STEPS = D_FF // FF_STEP


def _fwd_ff(hf, wfi3, wfo, x1, tgt, g3):
    T = hf.shape[0]
    tm = min(T, 512)
    last = FF_STEPS - 1

    def body(hf_ref, wfi_ref, wfo_ref, x1_ref, tgt_ref, g3_ref, f_ref, dy_ref, dff_ref, dg3_ref, loss_ref, acc, r_s):
        i, p = pl.program_id(0), pl.program_id(1)

        @pl.when((i == 0) & (p == 0))
        def _():
            dg3_ref[...] = jnp.zeros_like(dg3_ref)
            loss_ref[...] = jnp.zeros_like(loss_ref)

        hf_t = hf_ref[...]
        for s in range(FF_SLABS):
            cols = slice(s * FF_TILE, (s + 1) * FF_TILE)
            f = _nn(hf_t, wfi_ref[p * FF_SLABS + s]).astype(BF16)
            f_ref[:, cols] = f
            rl = jnp.maximum(f.astype(F32), 0.0)
            r_s[:, cols] = (rl * rl).astype(BF16)
        part = _nn(r_s[...], wfo_ref[pl.ds(pl.multiple_of(p * FF_STEP, FF_STEP), FF_STEP), :])

        @pl.when(p == 0)
        def _():
            acc[...] = part

        @pl.when(p > 0)
        def _():
            acc[...] += part

        @pl.when(p == last)
        def _():
            r3, fh = _rms_stats(acc[...])
            e = x1_ref[...] + fh * g3_ref[...] - tgt_ref[...]
            loss_ref[...] += jnp.sum(e * e) * (0.5 / D)
            dy = e * (1.0 / D)
            dy_ref[...] = dy
            dg3_ref[...] += _colsum(dy * fh)
            dff_ref[...] = _rms_bwd(dy, fh, r3, g3_ref[...]).astype(BF16)

    row = lambda i, p: (i, 0)
    const = lambda i, p: (0, 0)
    return pl.pallas_call(
        body, name="fwd_ff", grid=(T // tm, FF_STEPS),
        in_specs=[pl.BlockSpec((tm, D), row), _resident((FF_SPLIT, D, FF_TILE)), _resident((D_FF, D)),
                  pl.BlockSpec((tm, D), row),
                  pl.BlockSpec((tm, D), row), pl.BlockSpec((1, D), const)],
        out_specs=[pl.BlockSpec((tm, FF_STEP), lambda i, p: (i, p)), pl.BlockSpec((tm, D), row),
                   pl.BlockSpec((tm, D), row), pl.BlockSpec((1, D), const), pl.BlockSpec((1, 128), const)],
        out_shape=[SDS((T, D_FF), BF16), SDS((T, D), F32), SDS((T, D), BF16), SDS((1, D), F32), SDS((1, 128), F32)],
        scratch_shapes=[pltpu.VMEM((tm, D), F32), pltpu.VMEM((tm, FF_STEP), BF16)],
        compiler_params=_params(2),
    )(hf, wfi3, wfo, x1, tgt, g3)


def _bwd_ff(dff, f, wfi3, wfo, x1, dy, mix, g1, g2):
    T = dff.shape[0]
    tm = min(T, 512)
    last = FF_STEPS - 1

    def body(dff_ref, f_ref, wfi_ref, wfo_ref, x1_ref, dy_ref, mix_ref, g1_ref, g2_ref,
             df_ref, dx1_ref, dmix_ref, dg2_ref, dg1_ref, acc):
        i, p = pl.program_id(0), pl.program_id(1)

        @pl.when((i == 0) & (p == 0))
        def _():
            dg2_ref[...] = jnp.zeros_like(dg2_ref)
            dg1_ref[...] = jnp.zeros_like(dg1_ref)

        dr = _nt(dff_ref[...], wfo_ref[pl.ds(pl.multiple_of(p * FF_STEP, FF_STEP), FF_STEP), :])
        df_ref[...] = (dr * (2.0 * jnp.maximum(f_ref[...].astype(F32), 0.0))).astype(BF16)
        part = _nt(df_ref[:, :FF_TILE], wfi_ref[p * FF_SLABS])
        for s in range(1, FF_SLABS):
            part = part + _nt(df_ref[:, s * FF_TILE:(s + 1) * FF_TILE], wfi_ref[p * FF_SLABS + s])

        @pl.when(p == 0)
        def _():
            acc[...] = part

        @pl.when(p > 0)
        def _():
            acc[...] += part

        @pl.when(p == last)
        def _():
            dhf = acc[...]
            r2, xh = _rms_stats(x1_ref[...])
            dg2_ref[...] += _colsum(dhf * xh)
            dx1 = dy_ref[...] + _rms_bwd(dhf, xh, r2, g2_ref[...])
            dx1_ref[...] = dx1
            r1, mh = _rms_stats(mix_ref[...])
            dg1_ref[...] += _colsum(dx1 * mh)
            dmix_ref[...] = _rms_bwd(dx1, mh, r1, g1_ref[...]).astype(BF16)

    row = lambda i, p: (i, 0)
    const = lambda i, p: (0, 0)
    return pl.pallas_call(
        body, name="bwd_ff", grid=(T // tm, FF_STEPS),
        in_specs=[pl.BlockSpec((tm, D), row), pl.BlockSpec((tm, FF_STEP), lambda i, p: (i, p)),
                  _resident((FF_SPLIT, D, FF_TILE)), _resident((D_FF, D)),
                  pl.BlockSpec((tm, D), row), pl.BlockSpec((tm, D), row), pl.BlockSpec((tm, D), row),
                  pl.BlockSpec((1, D), const), pl.BlockSpec((1, D), const)],
        out_specs=[pl.BlockSpec((tm, FF_STEP), lambda i, p: (i, p)), pl.BlockSpec((tm, D), row),
                   pl.BlockSpec((tm, D), row), pl.BlockSpec((1, D), const), pl.BlockSpec((1, D), const)],
        out_shape=[SDS((T, D_FF), BF16), SDS((T, D), F32), SDS((T, D), BF16), SDS((1, D), F32), SDS((1, D), F32)],
        scratch_shapes=[pltpu.VMEM((tm, D), F32)],
        compiler_params=_params(2),
    )(dff, f, wfi3, wfo, x1, dy, mix, g1, g2)


def _wgrad_ff(hf, df, f, dff):
    T = hf.shape[0]
    tt = min(T, 1024)
    wide = 2 * FF_TILE

    def body(hf_ref, df_ref, f_ref, dff_ref, dwfi_ref, dwfo_ref, acc_i, acc_o):
        t = pl.program_id(1)

        @pl.when(t == 0)
        def _():
            acc_i[...] = jnp.zeros_like(acc_i)
            acc_o[...] = jnp.zeros_like(acc_o)

        acc_i[...] += _tn(hf_ref[...], df_ref[...])
        rl = jnp.maximum(f_ref[...].astype(F32), 0.0)
        acc_o[...] += _tn((rl * rl).astype(BF16), dff_ref[...])

        @pl.when(t == T // tt - 1)
        def _():
            dwfi_ref[0] = acc_i[:, :FF_TILE].astype(BF16)
            dwfi_ref[1] = acc_i[:, FF_TILE:].astype(BF16)
            dwfo_ref[...] = acc_o[...].astype(BF16)

    return pl.pallas_call(
        body, name="wgrad_ff", grid=(D_FF // wide, T // tt),
        in_specs=[pl.BlockSpec((tt, D), lambda p, t: (t, 0)), pl.BlockSpec((tt, wide), lambda p, t: (t, p)),
                  pl.BlockSpec((tt, wide), lambda p, t: (t, p)), pl.BlockSpec((tt, D), lambda p, t: (t, 0))],
        out_specs=[pl.BlockSpec((2, D, FF_TILE), lambda p, t: (p, 0, 0)), pl.BlockSpec((wide, D), lambda p, t: (p, 0))],
        out_shape=[SDS((FF_SPLIT, D, FF_TILE), BF16), SDS((D_FF, D), BF16)],
        scratch_shapes=[pltpu.VMEM((D, wide), F32), pltpu.VMEM((wide, D), F32)],
        compiler_params=_params(2),
    )(hf, df, f, dff)


def _bwd_mix(dmix, proj, a2, b2, wo, wa, wb, after=None):
    T = dmix.shape[0]
    tm = min(T, 512)
    half = D // 2

    def body(dmix_ref, ga0, ga1, gb0, gb1, a2_ref, b2_ref, wo_ref, wa_ref, wb_ref,
             da2_ref, db2_ref, dg_ref, da_ref, datt_ref):
        for rows in _row_halves(tm):
            dmg = _nt(dmix_ref[rows, :], wo_ref[...])
            sa = _sigmoid(jnp.concatenate([ga0[rows, :], ga1[rows, :]], axis=1).astype(F32))
            sb = _sigmoid(jnp.concatenate([gb0[rows, :], gb1[rows, :]], axis=1).astype(F32))
            da2 = (dmg * sa).astype(BF16)
            db2 = (dmg * sb).astype(BF16)
            da2_ref[rows, :] = da2
            db2_ref[rows, :] = db2
            dg_ref[rows, :D] = (dmg * a2_ref[rows, :].astype(F32) * (sa * (1.0 - sa))).astype(BF16)
            dg_ref[rows, D:] = (dmg * b2_ref[rows, :].astype(F32) * (sb * (1.0 - sb))).astype(BF16)
            da_ref[rows, :] = _nt(da2, wa_ref[...]).astype(BF16)
            datt_ref[rows, :] = _nt(db2, wb_ref[...]).astype(BF16)

    row = lambda i: (i, 0)
    const = lambda i: (0, 0)
    gspec = lambda off: pl.BlockSpec((tm, half), lambda i: (i, off // half))
    body, dep_specs, deps = _after(body, 10, after)
    return pl.pallas_call(
        body, name="bwd_mix", grid=(T // tm,),
        in_specs=[pl.BlockSpec((tm, D), row), gspec(OFF_GA), gspec(OFF_GA + half), gspec(OFF_GB), gspec(OFF_GB + half),
                  pl.BlockSpec((tm, D), row), pl.BlockSpec((tm, D), row),
                  _resident((D, D)), _resident((D, D)), _resident((D, D))] + dep_specs,
        out_specs=[pl.BlockSpec((tm, D), row), pl.BlockSpec((tm, D), row), pl.BlockSpec((tm, 2 * D), row),
                   pl.BlockSpec((tm, D), row), pl.BlockSpec((tm, D), row)],
        out_shape=[SDS((T, D), BF16), SDS((T, D), BF16), SDS((T, 2 * D), BF16), SDS((T, D), BF16), SDS((T, D), BF16)],
        compiler_params=_params(1),
    )(dmix, proj, proj, proj, proj, a2, b2, wo, wa, wb, *deps)


def _wgrad_mix(merged, dmix, a, da2, att, db2):
    T = merged.shape[0]
    tt = min(T, 512)

    def body(mg_ref, dmix_ref, a_ref, da2_ref, att_ref, db2_ref, dwo_ref, dwa_ref, dwb_ref, acc):
        t = pl.program_id(0)

        @pl.when(t == 0)
        def _():
            acc[...] = jnp.zeros_like(acc)

        acc[0] += _tn(mg_ref[...], dmix_ref[...])
        acc[1] += _tn(a_ref[...], da2_ref[...])
        acc[2] += _tn(att_ref[...], db2_ref[...])

        @pl.when(t == T // tt - 1)
        def _():
            dwo_ref[...] = acc[0].astype(BF16)
            dwa_ref[...] = acc[1].astype(BF16)
            dwb_ref[...] = acc[2].astype(BF16)

    return pl.pallas_call(
        body, name="wgrad_mix", grid=(T // tt,),
        in_specs=[pl.BlockSpec((tt, D), lambda t: (t, 0))] * 6,
        out_specs=[pl.BlockSpec((D, D), lambda t: (0, 0))] * 3,
        out_shape=[SDS((D, D), BF16)] * 3,
        scratch_shapes=[pltpu.VMEM((3, D, D), F32)],
        compiler_params=_params(1),
    )(merged, dmix, a, da2, att, db2)


def _bwd_attn(qr, kr, probs, psink, proj, cos, sin, datt, after=None):
    T = proj.shape[0]
    nb = T // CHUNK
    cur = lambda i: jnp.minimum(i, nb - 1)
    prev = lambda i: jnp.maximum(jnp.minimum(i, nb - 1) - 1, 0)

    def body(q_ref, kp_ref, kc_ref, vp_ref, vc_ref, cp_ref, cc_ref, sp_ref, sc_ref, p_ref, psink_ref, do_ref,
             dq_ref, dkv_ref, dsink_ref, carry_k, carry_v):
        i = pl.program_id(0)

        @pl.when(i == 0)
        def _():
            carry_k[...] = jnp.zeros_like(carry_k)
            carry_v[...] = jnp.zeros_like(carry_v)
            dsink_ref[...] = jnp.zeros_like(dsink_ref)

        @pl.when(i < nb)
        def _():
            prev_slot, _ = _fold_masks(i == 0)
            c_band, s_band = _band(cp_ref, cc_ref), _band(sp_ref, sc_ref)
            lane = lax.broadcasted_iota(jnp.int32, (1, 128), 1)
            dsink = jnp.zeros((1, 128), F32)
            for j in range(KV_W // 128):
                cols = slice(j * 128, (j + 1) * 128)
                k_slab = _band(kp_ref, kc_ref, cols).astype(F32)
                v_slab = _band(vp_ref, vc_ref, cols).astype(F32)
                dk_slab = jnp.zeros((2 * CHUNK, 128), F32)
                dv_slab = jnp.zeros((2 * CHUNK, 128), F32)
                for g in (2 * j, 2 * j + 1):
                    k2 = _head_pair_operand(k_slab, g)
                    v2 = _head_pair_operand(v_slab, g)
                    dk2 = jnp.zeros((4 * CHUNK, 128), F32)
                    dv2 = jnp.zeros((4 * CHUNK, 128), F32)
                    for r in range(PAIRS_PER_KV):
                        pair = g * PAIRS_PER_KV + r
                        lanes = slice(pair * 128, (pair + 1) * 128)
                        qp, dop = q_ref[:, lanes], do_ref[:, lanes]
                        dp2 = _nt(v2, dop)
                        ps, dss = [], []
                        for e in range(2):
                            head = 2 * pair + e
                            rows = slice(e * 2 * CHUNK, (e + 1) * 2 * CHUNK)
                            p_b = p_ref[head]
                            p = p_b.astype(F32)
                            dp = _fold(dp2[rows], prev_slot)
                            delta = jnp.sum(p * dp, axis=0, keepdims=True)
                            ps.append(_unfold(p_b, prev_slot))
                            dss.append(_unfold((p * (dp - delta)).astype(BF16), prev_slot))
                            dsink = dsink + jnp.where(lane == head, -jnp.sum(psink_ref[head:head + 1, :] * delta), 0.0)
                        ds2 = jnp.concatenate(dss, axis=0)
                        dq = _tn(ds2, k2) * (HEAD ** -0.5)
                        dq_ref[:, lanes] = _rope_bwd(dq, cc_ref[...], sc_ref[...]).astype(BF16)
                        dk2 = dk2 + _nn(ds2, qp)
                        dv2 = dv2 + _nn(jnp.concatenate(ps, axis=0), dop)
                    dk_slab = dk_slab + _head_pair_gradient(dk2, g)
                    dv_slab = dv_slab + _head_pair_gradient(dv2, g)
                dk_slab = _rope_bwd(dk_slab, c_band, s_band)
                vcols = slice(KV_W + j * 128, KV_W + (j + 1) * 128)
                dkv_ref[:, cols] = (carry_k[:, cols] + dk_slab[:CHUNK]).astype(BF16)
                dkv_ref[:, vcols] = (carry_v[:, cols] + dv_slab[:CHUNK]).astype(BF16)
                carry_k[:, cols] = dk_slab[CHUNK:]
                carry_v[:, cols] = dv_slab[CHUNK:]
            dsink_ref[...] += dsink

        @pl.when(i == nb)
        def _():
            dkv_ref[:, :KV_W] = carry_k[...].astype(BF16)
            dkv_ref[:, KV_W:] = carry_v[...].astype(BF16)

    table = lambda which, width: pl.BlockSpec((CHUNK, width), lambda i: (which(i), 0))
    body, dep_specs, deps = _after(body, 12, after)
    return pl.pallas_call(
        body, name="bwd_attn", grid=(nb + 1,),
        in_specs=[pl.BlockSpec((CHUNK, D), lambda i: (cur(i), 0)),
                  pl.BlockSpec((CHUNK, KV_W), lambda i: (prev(i), 0)),
                  pl.BlockSpec((CHUNK, KV_W), lambda i: (cur(i), 0)),
                  pl.BlockSpec((CHUNK, KV_W), lambda i: (prev(i), OFF_VA // KV_W)),
                  pl.BlockSpec((CHUNK, KV_W), lambda i: (cur(i), OFF_VA // KV_W)),
                  table(prev, 128), table(cur, 128), table(prev, 256), table(cur, 256),
                  pl.BlockSpec((None, N_Q, CHUNK, CHUNK), lambda i: (cur(i), 0, 0, 0)),
                  pl.BlockSpec((None, N_Q, CHUNK), lambda i: (cur(i), 0, 0)),
                  pl.BlockSpec((CHUNK, D), lambda i: (cur(i), 0))] + dep_specs,
        out_specs=[pl.BlockSpec((CHUNK, D), lambda i: (cur(i), 0)),
                   pl.BlockSpec((CHUNK, 2 * KV_W), lambda i: (jnp.maximum(i - 1, 0), 0)),
                   pl.BlockSpec((1, 128), lambda i: (0, 0))],
        out_shape=[SDS((T, D), BF16), SDS((T, 2 * KV_W), BF16), SDS((1, 128), F32)],
        scratch_shapes=[pltpu.VMEM((CHUNK, KV_W), F32), pltpu.VMEM((CHUNK, KV_W), F32)],
        compiler_params=_params(1),
    )(qr, kr, kr, proj, proj, cos, cos, sin, sin, probs, psink, datt, *deps)


def _bwd_sgu(proj, da, lng, lnb, ws, bst):
    T = proj.shape[0]
    tc = min(T, 512)
    nsteps = T // tc

    def body(u_ref, vs_ref, da_ref, lng_ref, lnb_ref, ws_ref, bst_ref,
             duv_ref, dws_ref, dbs_ref, dlng_ref, dlnb_ref, dvn_s, dgu_s, dmx_sum):
        i = pl.program_id(0)

        @pl.when(i == 0)
        def _():
            dws_ref[...] = jnp.zeros_like(dws_ref)
            dlng_ref[...] = jnp.zeros_like(dlng_ref)
            dlnb_ref[...] = jnp.zeros_like(dlnb_ref)
            dmx_sum[...] = jnp.zeros_like(dmx_sum)

        u, vs, gu, tu, tv, rstd, vhat, vn = _sgu_forward_parts(u_ref, vs_ref, lng_ref, lnb_ref)
        da = da_ref[...].astype(F32)
        for g in range(GROUPS):
            wm = _masked_ws(ws_ref, g)
            cols = slice(g * CHUNK, (g + 1) * CHUNK)
            dws = jnp.zeros((CHUNK, CHUNK), F32)
            dsum = jnp.zeros((CHUNK, CHUNK), F32)
            for c in range(tc // CHUNK):
                rows = slice(c * CHUNK, (c + 1) * CHUNK)
                vn_cg = vn[rows, cols]
                mixed = _nn(wm, vn_cg) + bst_ref[:, g:g + 1]
                dgu_s[rows, cols] = da[rows, cols] * mixed
                dmx = da[rows, cols] * gu[rows, cols]
                dmxb = dmx.astype(BF16)
                dws = dws + _nt(dmxb, vn_cg)
                dsum = dsum + dmx
                dvn_s[rows, cols] = _tn(wm, dmxb)
            dws_ref[g] += dws
            dmx_sum[:, cols] += dsum
        dvn = dvn_s[...]
        dlng_ref[...] += _colsum(dvn * vhat)
        dlnb_ref[...] += _colsum(dvn)
        dvh = dvn * lng_ref[...]
        dgv = rstd * (dvh - jnp.mean(dvh, axis=-1, keepdims=True) - vhat * jnp.mean(dvh * vhat, axis=-1, keepdims=True))
        duv_ref[:, :D] = (dgu_s[...] * _gelu_grad(u, tu)).astype(BF16)
        duv_ref[:, D:] = (dgv * _gelu_grad(vs, tv)).astype(BF16)

        @pl.when(i == nsteps - 1)
        def _():
            row = lax.broadcasted_iota(jnp.int32, (CHUNK, CHUNK), 0)
            col = lax.broadcasted_iota(jnp.int32, (CHUNK, CHUNK), 1)
            for g in range(GROUPS):
                dws_ref[g] = jnp.where(row >= col, dws_ref[g], 0.0)
                dbs_ref[g:g + 1, :] = _colsum(dmx_sum[:, g * CHUNK:(g + 1) * CHUNK].T)

    const2 = lambda i: (0, 0)
    return pl.pallas_call(
        body, name="bwd_sgu", grid=(nsteps,),
        in_specs=[pl.BlockSpec((tc, D), lambda i: (i, 0)), pl.BlockSpec((tc, D), lambda i: (i, 1)),
                  pl.BlockSpec((tc, D), lambda i: (i, 0)), pl.BlockSpec((1, D), const2), pl.BlockSpec((1, D), const2),
                  pl.BlockSpec((GROUPS, CHUNK, CHUNK), lambda i: (0, 0, 0)), pl.BlockSpec((CHUNK, GROUPS), const2)],
        out_specs=[pl.BlockSpec((tc, 2 * D), lambda i: (i, 0)), pl.BlockSpec((GROUPS, CHUNK, CHUNK), lambda i: (0, 0, 0)),
                   pl.BlockSpec((GROUPS, CHUNK), const2), pl.BlockSpec((1, D), const2), pl.BlockSpec((1, D), const2)],
        out_shape=[SDS((T, 2 * D), BF16), SDS((GROUPS, CHUNK, CHUNK), F32), SDS((GROUPS, CHUNK), F32),
                   SDS((1, D), F32), SDS((1, D), F32)],
        scratch_shapes=[pltpu.VMEM((tc, D), F32), pltpu.VMEM((tc, D), F32), pltpu.VMEM((CHUNK, D), F32)],
        compiler_params=_params(1),
    )(proj, proj, da, lng, lnb, ws, bst)


IN_SEG_WIDTHS = (2 * D, D, 2 * N_KV * HEAD, 2 * D)


def _resident(shape):
    return pl.BlockSpec(shape, lambda *_: (0,) * len(shape), pipeline_mode=pl.Buffered(1))


def _bwd_in(duv, dq, dkv, dg, win_t, x, dx1, g0, after=None):
    T = x.shape[0]
    tm = min(T, 512)

    def body(duv_ref, dq_ref, dkv_ref, dg_ref, w_ref, x_ref, dx1_ref, g0_ref, gx_ref, dg0_ref):
        @pl.when(pl.program_id(0) == 0)
        def _():
            dg0_ref[...] = jnp.zeros_like(dg0_ref)

        dh, off = None, 0
        for ref, width in zip((duv_ref, dq_ref, dkv_ref, dg_ref), IN_SEG_WIDTHS):
            part = _nn(ref[...], w_ref[off:off + width, :])
            dh = part if dh is None else dh + part
            off += width
        r0, xh = _rms_stats(x_ref[...])
        dg0_ref[...] += _colsum(dh * xh)
        gx_ref[...] = dx1_ref[...] + _rms_bwd(dh, xh, r0, g0_ref[...])

    row = lambda i: (i, 0)
    body, dep_specs, deps = _after(body, 8, after)
    return pl.pallas_call(
        body, name="bwd_in", grid=(T // tm,),
        in_specs=[pl.BlockSpec((tm, w), row) for w in IN_SEG_WIDTHS] + [
            _resident((IN_W, D)), pl.BlockSpec((tm, D), row), pl.BlockSpec((tm, D), row),
            pl.BlockSpec((1, D), lambda i: (0, 0))] + dep_specs,
        out_specs=[pl.BlockSpec((tm, D), row), pl.BlockSpec((1, D), lambda i: (0, 0))],
        out_shape=[SDS((T, D), F32), SDS((1, D), F32)],
        compiler_params=_params(1),
    )(duv, dq, dkv, dg, win_t, x, dx1, g0, *deps)


def _wgrad_rows(h, segs, first_row, into, name):
    T = h.shape[0]
    tt = min(T, 1024)
    widths = [s.shape[1] for s in segs]
    rows = sum(widths)
    n_in = 1 + len(segs) + (into is not None)

    def body(*refs):
        h_ref, seg_refs = refs[0], refs[1:1 + len(segs)]
        dw_ref, acc, stage, sem = refs[n_in], refs[n_in + 1], refs[n_in + 2], refs[n_in + 3]
        t = pl.program_id(0)

        @pl.when(t == 0)
        def _():
            acc[...] = jnp.zeros_like(acc)

        off = 0
        for ref, width in zip(seg_refs, widths):
            acc[off:off + width, :] += _tn(ref[...], h_ref[...])
            off += width

        @pl.when(t == T // tt - 1)
        def _():
            stage[...] = acc[...].astype(BF16)
            out = pltpu.make_async_copy(stage, dw_ref.at[pl.ds(first_row, rows)], sem)
            out.start()
            out.wait()

    row = lambda t: (t, 0)
    return pl.pallas_call(
        body, name=name, grid=(T // tt,),
        in_specs=[pl.BlockSpec((tt, D), row)] + [pl.BlockSpec((tt, w), row) for w in widths] + [_ANY] * (into is not None),
        out_specs=_ANY,
        out_shape=SDS((IN_W, D), BF16),
        input_output_aliases={} if into is None else {n_in - 1: 0},
        scratch_shapes=[pltpu.VMEM((rows, D), F32), pltpu.VMEM((rows, D), BF16), pltpu.SemaphoreType.DMA],
        compiler_params=_params(1),
    )(h, *segs, *([] if into is None else [into]))


def _wgrad_in(h, duv, dq, dkv, dg):
    dw = _wgrad_rows(h, [dg], IN_SEG_WIDTHS[0] + IN_SEG_WIDTHS[1] + IN_SEG_WIDTHS[2], None, "wgrad_in_gates")
    dw = _wgrad_rows(h, [duv], 0, dw, "wgrad_in_uv")
    return _wgrad_rows(h, [dq, dkv], IN_SEG_WIDTHS[0], dw, "wgrad_in_qkv")


def _place():
    x, y, c = lax.axis_index("x"), lax.axis_index("y"), lax.axis_index("c")
    return x, y, c, 4 * x + 2 * y + c


def _peers(x, y, c):
    out = []
    for mask in range(1, N_DEV):
        px = 1 - x if mask & 4 else x
        py = 1 - y if mask & 2 else y
        pc = 1 - c if mask & 1 else c
        out.append(((px, py, pc), 4 * px + 2 * py + pc))
    return out


def _all_to_all(arrays, gather, name, after=None):
    n = len(arrays)

    def body(*refs):
        ins, outs = refs[:n], refs[n:2 * n]
        send_sems, recv_sems, local_sems = refs[2 * n:]
        x, y, c, me = _place()
        local, sends, recvs = [], [], []
        for a in range(n):
            src_own = ins[a] if gather[a] else ins[a].at[me]
            local.append(pltpu.make_async_copy(src_own, outs[a].at[me], local_sems.at[a]))
            for k, (peer, pid) in enumerate(_peers(x, y, c)):
                sem = a * (N_DEV - 1) + k
                src = ins[a] if gather[a] else ins[a].at[pid]
                sends.append(pltpu.make_async_remote_copy(
                    src_ref=src, dst_ref=outs[a].at[me], send_sem=send_sems.at[sem], recv_sem=recv_sems.at[sem],
                    device_id=peer, device_id_type=MESH))
                recvs.append(pltpu.make_async_remote_copy(
                    src_ref=src, dst_ref=outs[a].at[pid], send_sem=send_sems.at[sem], recv_sem=recv_sems.at[sem],
                    device_id=peer, device_id_type=MESH))
        for cp in local + sends:
            cp.start()
        for cp in recvs:
            cp.wait_recv()
        for cp in sends:
            cp.wait_send()
        for cp in local:
            cp.wait()

    out_shape = [SDS((N_DEV,) + a.shape if gt else a.shape, a.dtype) for a, gt in zip(arrays, gather)]
    nsem = n * (N_DEV - 1)
    body, dep_specs, deps = _after(body, n, after)
    return pl.pallas_call(
        body, name=name,
        in_specs=[pl.BlockSpec(memory_space=pl.ANY)] * n + dep_specs,
        out_specs=[pl.BlockSpec(memory_space=pl.ANY)] * n,
        out_shape=out_shape,
        scratch_shapes=[pltpu.SemaphoreType.DMA((nsem,)), pltpu.SemaphoreType.DMA((nsem,)), pltpu.SemaphoreType.DMA((n,))],
    )(*arrays, *deps)


_HBM = pl.BlockSpec(memory_space=pltpu.HBM)
_SEM = pl.BlockSpec(memory_space=pltpu.SEMAPHORE)
_EFFECT = pltpu.SideEffectType.DATAFLOW_SIDE_EFFECTING
GATHER = "gather"
SCATTER = "scatter"
SPREAD = "spread"


def _zone_shape(a, mode):
    if mode == GATHER:
        return (N_DEV,) + a.shape
    return (N_DEV - 1,) + (a.shape[1:] if mode == SCATTER else a.shape)


def _start_copies(arrays, modes, name, after=None):
    n = len(arrays)
    zones = [lax.empty(_zone_shape(a, m), a.dtype) for a, m in zip(arrays, modes)]

    def body(*refs):
        ins, lands = refs[:n], refs[n:2 * n]
        send_sems, recv_sems = refs[-2 * n - 3], refs[-2 * n - 2]
        token = refs[-1]
        x, y, c, me = _place()
        for a in range(n):
            for k, (peer, pid) in enumerate(_peers(x, y, c)):
                src = ins[a].at[pid] if modes[a] == SCATTER else ins[a]
                dst = lands[a].at[me] if modes[a] == GATHER else lands[a].at[k]
                pltpu.make_async_remote_copy(src_ref=src, dst_ref=dst, send_sem=send_sems.at[a], recv_sem=recv_sems.at[a],
                                             device_id=peer, device_id_type=MESH).start()
        token[...] = jnp.zeros_like(token)

    hbm = lambda a: pltpu.HBM(a.shape, a.dtype)
    sems = pltpu.SemaphoreType.DMA((n,))
    extra = [] if after is None else [after]
    operands = [pltpu.with_memory_space_constraint(a, pltpu.HBM) for a in list(arrays) + zones]
    res = pl.pallas_call(
        body, name=name,
        out_shape=(sems, sems, *[hbm(a) for a in arrays], *[hbm(z) for z in zones], SDS((8, 128), F32)),
        in_specs=[_HBM] * (2 * n) + [_ANY] * len(extra),
        out_specs=(_SEM, _SEM, *[_HBM] * (2 * n), pl.BlockSpec(memory_space=pltpu.VMEM)),
        input_output_aliases={i: 2 + i for i in range(2 * n)},
        compiler_params=pltpu.CompilerParams(has_side_effects=_EFFECT),
    )(*operands, *extra)
    return res[0], res[1], list(res[2:2 + n]), list(res[2 + n:2 + 2 * n]), res[-1]


def _wait_copies(started, after, name, count=N_DEV - 1):
    send_sems, recv_sems, thru, zones, _ = started
    nt, nz = len(thru), len(zones)

    def body(*refs):
        lands = refs[nt:nt + nz]
        send_ref, recv_ref = refs[nt + nz], refs[nt + nz + 1]
        x, y, c, _ = _place()
        for a in range(nz):
            blocks = lands[a].at[pl.ds(0, count)]
            cp = pltpu.make_async_remote_copy(src_ref=blocks, dst_ref=blocks, send_sem=send_ref.at[a], recv_sem=recv_ref.at[a],
                                              device_id=(x, y, 1 - c), device_id_type=MESH)
            cp.wait_send()
            cp.wait_recv()

    hbm = lambda a: pltpu.HBM(a.shape, a.dtype)
    res = pl.pallas_call(
        body, name=name,
        out_shape=tuple(hbm(a) for a in thru + zones),
        in_specs=[_HBM] * (nt + nz) + [_SEM, _SEM, _ANY],
        out_specs=tuple([_HBM] * (nt + nz)),
        input_output_aliases={i: i for i in range(nt + nz)},
        compiler_params=pltpu.CompilerParams(has_side_effects=_EFFECT),
    )(*thru, *zones, send_sems, recv_sems, after)
    return list(res[:nt]), list(res[nt:])


def _split_start(body, arrays, zones, name, after):
    n = len(arrays) + len(zones)
    hbm = lambda a: pltpu.HBM(a.shape, a.dtype)
    sems = pltpu.SemaphoreType.DMA((max(len(zones), 1),))
    extra = [] if after is None else [after]
    operands = [pltpu.with_memory_space_constraint(a, pltpu.HBM) for a in list(arrays) + list(zones)]
    res = pl.pallas_call(
        body, name=name,
        out_shape=(sems, sems, *[hbm(a) for a in operands], SDS((8, 128), F32)),
        in_specs=[_HBM] * n + [_ANY] * len(extra),
        out_specs=(_SEM, _SEM, *[_HBM] * n, pl.BlockSpec(memory_space=pltpu.VMEM)),
        input_output_aliases={i: 2 + i for i in range(n)},
        compiler_params=pltpu.CompilerParams(has_side_effects=_EFFECT),
    )(*operands, *extra)
    return res[0], res[1], list(res[2:2 + len(arrays)]), list(res[2 + len(arrays):2 + n]), res[-1]


def _gather_first_leg(shard, name, after=None):
    zone = lax.empty((N_DEV,) + shard.shape, shard.dtype)
    extra = 0 if after is None else 1

    def body(*refs):
        src, land = refs[0], refs[1]
        send_sem, recv_sem, token = refs[2 + extra], refs[3 + extra], refs[-1]
        x, y, c, me = _place()
        for peer in ((x, y, 1 - c), (1 - x, y, c), (x, 1 - y, c), (1 - x, 1 - y, c)):
            pltpu.make_async_remote_copy(src_ref=src, dst_ref=land.at[me], send_sem=send_sem.at[0], recv_sem=recv_sem.at[0],
                                         device_id=peer, device_id_type=MESH).start()
        token[...] = jnp.zeros_like(token)

    return _split_start(body, [shard], [zone], name, after)


def _gather_second_leg(zone, name, after=None):
    extra = 0 if after is None else 1

    def body(*refs):
        land = refs[0]
        send_sem, recv_sem, token = refs[1 + extra], refs[2 + extra], refs[-1]
        x, y, c, _ = _place()
        for px, py in ((1 - x, y), (x, 1 - y), (1 - x, 1 - y)):
            slot = 4 * px + 2 * py + c
            pltpu.make_async_remote_copy(src_ref=land.at[slot], dst_ref=land.at[slot], send_sem=send_sem.at[0],
                                         recv_sem=recv_sem.at[0], device_id=(x, y, 1 - c), device_id_type=MESH).start()
        token[...] = jnp.zeros_like(token)

    return _split_start(body, [], [zone], name, after)


UPDATE_BLOCK_ELEMS = 256 * 1024


def _update_rows(R, C):
    fits = [t for t in range(8, R + 1, 8) if R % t == 0 and t * C <= UPDATE_BLOCK_ELEMS]
    whole = [t for t in fits if t % 16 == 0]
    return max(whole or fits)


def _adamw_math(g, w, m, v):
    m2 = ADAM_B1 * m + (1.0 - ADAM_B1) * g
    v2 = ADAM_B2 * v + (1.0 - ADAM_B2) * (g * g)
    m_hat = m2 / (1.0 - ADAM_B1 ** ADAM_STEP)
    v_hat = v2 / (1.0 - ADAM_B2 ** ADAM_STEP)
    delta = -ADAM_LR * (m_hat / (jnp.sqrt(v_hat) + ADAM_EPS) + ADAM_WD * w)
    return delta, m2, v2


def _sum_adamw(parts, w, m, v, name):
    R, C = w.shape
    tr = _update_rows(R, C)

    def body(p_ref, w_ref, m_ref, v_ref, g_ref, d_ref, m2_ref, v2_ref):
        g = p_ref[0]
        for k in range(1, N_DEV):
            g = g + p_ref[k]
        g_ref[...] = g
        d_ref[...], m2_ref[...], v2_ref[...] = _adamw_math(g, w_ref[...], m_ref[...], v_ref[...])

    blk = pl.BlockSpec((tr, C), lambda i: (i, 0))
    return pl.pallas_call(
        body, name=name, grid=(R // tr,),
        in_specs=[pl.BlockSpec((N_DEV, tr, C), lambda i: (0, i, 0)), blk, blk, blk],
        out_specs=[blk] * 4,
        out_shape=[SDS((R, C), F32)] * 4,
        compiler_params=_params(1),
    )(parts, w, m, v)


def _sum_adamw_peers(me, own, parts, w, m, v, name, replicated):
    R, C = w.shape
    tr = _update_rows(R, C)

    def body(me_ref, own_ref, p_ref, w_ref, m_ref, v_ref, g_ref, d_ref, m2_ref, v2_ref):
        if replicated:
            mine = me_ref[0]
            g = None
            for j in range(N_DEV):
                k = jnp.maximum(jnp.bitwise_xor(mine, j) - 1, 0)
                term = jnp.where(mine == j, own_ref[...], p_ref[k])
                g = term if g is None else g + term
        else:
            g = own_ref[...].astype(F32)
            for k in range(N_DEV - 1):
                g = g + p_ref[k].astype(F32)
        g_ref[...] = g
        d_ref[...], m2_ref[...], v2_ref[...] = _adamw_math(g, w_ref[...], m_ref[...], v_ref[...])

    blk = pl.BlockSpec((tr, C), lambda i, me_ref: (i, 0))
    own_spec = blk if replicated else pl.BlockSpec((None, tr, C), lambda i, me_ref: (me_ref[0], i, 0))
    return pl.pallas_call(
        body, name=name,
        grid_spec=pltpu.PrefetchScalarGridSpec(
            num_scalar_prefetch=1, grid=(R // tr,),
            in_specs=[own_spec, pl.BlockSpec((N_DEV - 1, tr, C), lambda i, me_ref: (0, i, 0)), blk, blk, blk],
            out_specs=[blk] * 4),
        out_shape=[SDS((R, C), F32)] * 4,
        compiler_params=_params(1),
    )(me, own, parts, w, m, v)


SMALL = ("ln_v_gain", "ln_v_bias", "w_spatial", "b_spatial", "sinks", "norm_mix_post", "norm_ff_pre", "norm_ff_post")
SMALL_ROWS = {"ln_v_gain": 8, "ln_v_bias": 8, "w_spatial": 1024, "b_spatial": 8, "sinks": 8,
              "norm_mix_post": 8, "norm_ff_pre": 8, "norm_ff_post": 8}
SMALL_PACK_ROWS = 1152


def _pack_small(vals):
    rows = []
    for name in SMALL:
        flat = vals[name].reshape(-1)
        pad = SMALL_ROWS[name] * 128 - flat.shape[0]
        if pad:
            flat = jnp.concatenate([flat, jnp.zeros((pad,), F32)])
        rows.append(flat.reshape(SMALL_ROWS[name], 128))
    rows.append(jnp.zeros((SMALL_PACK_ROWS - sum(SMALL_ROWS.values()), 128), F32))
    return jnp.concatenate(rows, axis=0)


def _unpack_small(packed, shapes):
    out, r = {}, 0
    for name in SMALL:
        n = 1
        for s in shapes[name]:
            n *= s
        out[name] = packed[r:r + SMALL_ROWS[name]].reshape(-1)[:n].reshape(shapes[name])
        r += SMALL_ROWS[name]
    return out


def _rope_rows():
    d = jnp.arange(128) % HEAD
    inv = ROPE_THETA ** (-(2.0 * (d % (ROPE // 2))).astype(F32) / ROPE)
    invf = jnp.where(d < ROPE, inv, 0.0).astype(F32).reshape(1, 128)
    sgn = jnp.where(d < ROPE // 2, -1.0, jnp.where(d < ROPE, 1.0, 0.0)).astype(F32).reshape(1, 128)
    return invf, sgn


def kernel(x, positions, w_in, ln_v_gain, ln_v_bias, w_spatial, b_spatial, sinks, w_a, w_b, w_o, norm_mix_pre, norm_mix_post, w_ff_in, w_ff_out, norm_ff_pre, norm_ff_post, loss_target, m_w_in, m_ln_v_gain, m_ln_v_bias, m_w_spatial, m_b_spatial, m_sinks, m_w_a, m_w_b, m_w_o, m_norm_mix_pre, m_norm_mix_post, m_w_ff_in, m_w_ff_out, m_norm_ff_pre, m_norm_ff_post, v_w_in, v_ln_v_gain, v_ln_v_bias, v_w_spatial, v_b_spatial, v_sinks, v_w_a, v_w_b, v_w_o, v_norm_mix_pre, v_norm_mix_post, v_w_ff_in, v_w_ff_out, v_norm_ff_pre, v_norm_ff_post):
    given = dict(locals())
    T = x.shape[1]
    xt = x[0]
    tgt = loss_target[0]
    bst = b_spatial[0].T
    ws = w_spatial[0]

    me = 4 * lax.axis_index("x") + 2 * lax.axis_index("y") + lax.axis_index("c")
    me_arr = me.astype(jnp.int32).reshape(1)

    def with_own(zone, shard):
        return lax.dynamic_update_slice(zone, shard[None], (me,) + (0,) * shard.ndim)

    rest = ("w_a", "w_b", "w_o", "w_ff_in", "w_ff_out")
    shard = {n: given[n][0].astype(BF16) for n in rest}
    g_one = _gather_first_leg(w_in[0].T.astype(BF16), "gather_in_start")
    cos, sin = _rope_tables(positions.astype(F32).reshape(T, 1), *_rope_rows(), after=g_one[-1])
    h = _rms_pre(xt, norm_mix_pre, after=cos)
    (own_win,), (win8,) = _wait_copies(g_one, h, "gather_in_wait", count=4)
    g_two = _gather_second_leg(win8, "gather_in_pass_start")
    g_rest = _start_copies([shard[n] for n in rest], [GATHER] * len(rest), "gather_rest_start", after=g_two[-1])
    _, (win8,) = _wait_copies(g_two, g_rest[-1], "gather_in_pass_wait", count=3)
    win = with_own(win8, own_win).reshape(IN_W, D)

    proj = _fwd_in(h, win)
    att, qr, kr, probs, psink = _fwd_attn(proj, cos, sin, sinks[0])
    a = _fwd_sgu(proj, ln_v_gain, ln_v_bias, ws, bst, after=att)
    gw = {n: with_own(z, own) for n, own, z in zip(rest, *_wait_copies(g_rest, a, "gather_rest_wait"))}
    wa, wb, wo = (gw[n].reshape(D, D) for n in ("w_a", "w_b", "w_o"))
    wfi3 = gw["w_ff_in"]
    wfo = gw["w_ff_out"].reshape(D_FF, D)
    merged, a2, b2, mix, x1, hf = _fwd_mix(a, att, proj, xt, wa, wb, wo, norm_mix_post, norm_ff_pre)
    f, dy, dff, dg3, loss_part = _fwd_ff(hf, wfi3, wfo, x1, tgt, norm_ff_post)

    df, dx1, dmix, dg2, dg1 = _bwd_ff(dff, f, wfi3, wfo, x1, dy, mix, norm_mix_post, norm_ff_pre)
    dwfi3, dwfo = _wgrad_ff(hf, df, f, dff)
    own_ff = [dwfi3, dwfo.reshape(N_DEV, D_FF // N_DEV, D)]
    x_ff = _start_copies(own_ff, [SCATTER] * 2, "exchange_ff_start")
    da2, db2, dgate, da, datt = _bwd_mix(dmix, proj, a2, b2, wo, wa, wb, after=x_ff[-1])
    dwo, dwa, dwb = _wgrad_mix(merged, dmix, a, da2, att, db2)
    own_mix = [g.reshape(N_DEV, D // N_DEV, D) for g in (dwa, dwb, dwo)]
    x_mix = _start_copies(own_mix, [SCATTER] * 3, "exchange_mix_start")
    dq, dkv, dsink = _bwd_attn(qr, kr, probs, psink, proj, cos, sin, datt, after=x_mix[-1])
    duv, dws, dbs, dlng, dlnb = _bwd_sgu(proj, da, ln_v_gain, ln_v_bias, ws, bst)
    small_grads = {"ln_v_gain": dlng, "ln_v_bias": dlnb, "w_spatial": dws, "b_spatial": dbs, "sinks": dsink[:, :N_Q],
                   "norm_mix_post": dg1, "norm_ff_pre": dg2, "norm_ff_post": dg3}
    x_small = _start_copies([_pack_small(small_grads)], [SPREAD], "exchange_small_start")
    dwin = _wgrad_in(h, duv, dq, dkv, dgate)
    own_in = [dwin.reshape(N_DEV, IN_W // N_DEV, D)]
    x_in = _start_copies(own_in, [SCATTER], "exchange_in_start", after=x_small[-1])
    grad_x, dg0 = _bwd_in(duv, dq, dkv, dgate, win, xt, dx1, norm_mix_pre, after=x_in[-1])

    results = {}

    def update(n, own, parts, transposed=False):
        state = [given[k + n][0].T if transposed else given[k + n][0] for k in ("", "m_", "v_")]
        res = _sum_adamw_peers(me_arr, own, parts, *state, "adamw_" + n, False)
        results[n] = [(r.T if transposed else r).reshape(given[n].shape) for r in res]

    own_ff, p_ff = _wait_copies(x_ff, grad_x, "exchange_ff_wait")
    update("w_ff_in", own_ff[0], p_ff[0])
    update("w_ff_out", own_ff[1], p_ff[1])
    own_mix, p_mix = _wait_copies(x_mix, results["w_ff_out"][0], "exchange_mix_wait")
    for n, own, parts in zip(("w_a", "w_b", "w_o"), own_mix, p_mix):
        update(n, own, parts)
    tail = jnp.concatenate([dg0.reshape(8, 128), jnp.tile(loss_part, (8, 1))], axis=0)
    (tail_all,) = _all_to_all([tail], [True], "exchange_tail", after=results["w_o"][0])
    dg0_all = tail_all[:, :8]
    own_small, p_small = _wait_copies(x_small, tail_all, "exchange_small_wait")
    own_in, p_in = _wait_copies(x_in, p_small[0], "exchange_in_wait")
    update("w_in", own_in[0], p_in[0], transposed=True)
    packed = _sum_adamw_peers(me_arr, own_small[0], p_small[0], _pack_small({n: given[n] for n in SMALL}),
                              _pack_small({n: given["m_" + n] for n in SMALL}),
                              _pack_small({n: given["v_" + n] for n in SMALL}), "adamw_small", True)
    shapes = {n: given[n].shape for n in SMALL}
    unpacked = [_unpack_small(p, shapes) for p in packed]
    for n in SMALL:
        results[n] = [u[n] for u in unpacked]
    n = "norm_mix_pre"
    results[n] = [r.reshape(given[n].shape) for r in _sum_adamw(
        dg0_all, given[n].reshape(8, 128), given["m_" + n].reshape(8, 128), given["v_" + n].reshape(8, 128), "adamw_" + n)]

    loss = jnp.sum(tail_all[:, 8, 0])
    order = ("w_in", "ln_v_gain", "ln_v_bias", "w_spatial", "b_spatial", "sinks", "w_a", "w_b", "w_o", "norm_mix_pre",
             "norm_mix_post", "w_ff_in", "w_ff_out", "norm_ff_pre", "norm_ff_post")
    out = [loss, grad_x.reshape(x.shape)]
    for k in range(4):
        out += [results[n][k] for n in order]
    return tuple(out)
```

```python
import functools

import jax
import jax.numpy as jnp
from jax import lax
from jax.experimental import pallas as pl
from jax.experimental.pallas import tpu as pltpu

F32 = jnp.float32
BF16 = jnp.bfloat16

N_DEV = 8
D = 1024
D_FF = 4096
IN_W = 5632
CHUNK = 128
GROUPS = 8
HEAD = 64
N_Q = 16
N_KV = 4
ROPE = 16
ROPE_THETA = 500000.0
EPS = 1e-6
OFF_Q, OFF_K, OFF_VA, OFF_GA, OFF_GB = 2048, 3072, 3328, 3584, 4608

ADAM_LR = 0.001
ADAM_B1 = 0.9
ADAM_B2 = 0.999
ADAM_EPS = 1e-08
ADAM_WD = 0.01
ADAM_STEP = 10

VMEM_LIMIT = 56 * 1024 * 1024

SDS = jax.ShapeDtypeStruct
MESH = pl.DeviceIdType.MESH


def _params(n_axes=None):
    if n_axes is None:
        return pltpu.CompilerParams(vmem_limit_bytes=VMEM_LIMIT)
    return pltpu.CompilerParams(dimension_semantics=("arbitrary",) * n_axes, vmem_limit_bytes=VMEM_LIMIT)


def _nt(a, b):
    return lax.dot_general(a, b, (((1,), (1,)), ((), ())), preferred_element_type=F32)


def _tn(a, b):
    return lax.dot_general(a, b, (((0,), (0,)), ((), ())), preferred_element_type=F32)


def _nn(a, b):
    return jnp.dot(a, b, preferred_element_type=F32)


GELU_C = 0.7978845608028654
GELU_A = 0.044715


def _gelu(x):
    s = _sigmoid(2.0 * GELU_C * (x + GELU_A * (x * x * x)))
    return x * s, s


def _gelu_grad(x, s):
    return s + x * s * (1.0 - s) * (2.0 * GELU_C * (1.0 + 3.0 * GELU_A * x * x))


def _sigmoid(x):
    return 1.0 / (1.0 + jnp.exp(-x))


def _rms_stats(v):
    r = lax.rsqrt(jnp.mean(v * v, axis=-1, keepdims=True) + EPS)
    return r, v * r


def _rms_bwd(d, vhat, r, g):
    gd = g * d
    return r * (gd - vhat * jnp.mean(gd * vhat, axis=-1, keepdims=True))


def _colsum(v):
    return jnp.sum(v, axis=0, keepdims=True)


_ANY = pl.BlockSpec(memory_space=pl.ANY)


def _after(body, n_in, after):
    if after is None:
        return body, [], []

    def ordered(*refs):
        return body(*refs[:n_in], *refs[n_in + 1:])

    return ordered, [_ANY], [after]


def _rms_pre(x, g0, after=None):
    T = x.shape[0]
    tm = min(T, 1024)

    def body(x_ref, g_ref, h_ref):
        _, xh = _rms_stats(x_ref[...])
        h_ref[...] = (xh * g_ref[...]).astype(BF16)

    body, dep_specs, deps = _after(body, 2, after)
    return pl.pallas_call(
        body, name="rms_pre", grid=(T // tm,),
        in_specs=[pl.BlockSpec((tm, D), lambda i: (i, 0)), pl.BlockSpec((1, D), lambda i: (0, 0))] + dep_specs,
        out_specs=pl.BlockSpec((tm, D), lambda i: (i, 0)),
        out_shape=SDS((T, D), BF16),
        compiler_params=_params(1),
    )(x, g0, *deps)


def _fwd_in(h, win_t):
    T = h.shape[0]
    tm, tn = min(T, 1024), 1408

    def body(h_ref, w_ref, p_ref):
        p_ref[...] = _nt(h_ref[...], w_ref[...]).astype(BF16)

    return pl.pallas_call(
        body, name="fwd_in", grid=(T // tm, IN_W // tn),
        in_specs=[pl.BlockSpec((tm, D), lambda i, j: (i, 0)), pl.BlockSpec((tn, D), lambda i, j: (j, 0))],
        out_specs=pl.BlockSpec((tm, tn), lambda i, j: (i, j)),
        out_shape=SDS((T, IN_W), BF16),
        compiler_params=_params(2),
    )(h, win_t)


def _sgu_forward_parts(u_ref, vs_ref, lng_ref, lnb_ref):
    u = u_ref[...].astype(F32)
    vs = vs_ref[...].astype(F32)
    gu, tu = _gelu(u)
    gv, tv = _gelu(vs)
    mu = jnp.mean(gv, axis=-1, keepdims=True)
    dv = gv - mu
    rstd = lax.rsqrt(jnp.mean(dv * dv, axis=-1, keepdims=True) + EPS)
    vhat = dv * rstd
    vn = (vhat * lng_ref[...] + lnb_ref[...]).astype(BF16)
    return u, vs, gu, tu, tv, rstd, vhat, vn


def _masked_ws(ws_ref, g):
    row = lax.broadcasted_iota(jnp.int32, (CHUNK, CHUNK), 0)
    col = lax.broadcasted_iota(jnp.int32, (CHUNK, CHUNK), 1)
    return jnp.where(row >= col, ws_ref[g], 0.0).astype(BF16)


def _fwd_sgu(proj, lng, lnb, ws, bst, after=None):
    T = proj.shape[0]
    tc = min(T, 512)

    def body(u_ref, vs_ref, lng_ref, lnb_ref, ws_ref, bst_ref, a_ref):
        _, _, gu, _, _, _, _, vn = _sgu_forward_parts(u_ref, vs_ref, lng_ref, lnb_ref)
        for g in range(GROUPS):
            wm = _masked_ws(ws_ref, g)
            cols = slice(g * CHUNK, (g + 1) * CHUNK)
            for c in range(tc // CHUNK):
                rows = slice(c * CHUNK, (c + 1) * CHUNK)
                mixed = _nn(wm, vn[rows, cols]) + bst_ref[:, g:g + 1]
                a_ref[rows, cols] = (gu[rows, cols] * mixed).astype(BF16)

    body, dep_specs, deps = _after(body, 6, after)
    return pl.pallas_call(
        body, name="fwd_sgu", grid=(T // tc,),
        in_specs=[pl.BlockSpec((tc, D), lambda i: (i, 0)), pl.BlockSpec((tc, D), lambda i: (i, 1)),
                  pl.BlockSpec((1, D), lambda i: (0, 0)), pl.BlockSpec((1, D), lambda i: (0, 0)),
                  pl.BlockSpec((GROUPS, CHUNK, CHUNK), lambda i: (0, 0, 0)),
                  pl.BlockSpec((CHUNK, GROUPS), lambda i: (0, 0))] + dep_specs,
        out_specs=pl.BlockSpec((tc, D), lambda i: (i, 0)),
        out_shape=SDS((T, D), BF16),
        compiler_params=_params(1),
    )(proj, proj, lng, lnb, ws, bst, *deps)


def _rope_tables(posf, invf, sgn, after=None):
    T = posf.shape[0]
    tr = min(T, 1024)

    def body(pos_ref, invf_ref, sgn_ref, c_ref, s_ref):
        ang = pos_ref[...] * invf_ref[...]
        c_ref[...] = jnp.cos(ang)
        s = jnp.sin(ang)
        s_ref[:, :128] = jnp.where(sgn_ref[...] < 0.0, -s, 0.0)
        s_ref[:, 128:] = jnp.where(sgn_ref[...] > 0.0, s, 0.0)

    body, dep_specs, deps = _after(body, 3, after)
    return pl.pallas_call(
        body, name="rope_tables", grid=(T // tr,),
        in_specs=[pl.BlockSpec((tr, 1), lambda i: (i, 0)), pl.BlockSpec((1, 128), lambda i: (0, 0)),
                  pl.BlockSpec((1, 128), lambda i: (0, 0))] + dep_specs,
        out_specs=[pl.BlockSpec((tr, 128), lambda i: (i, 0)), pl.BlockSpec((tr, 256), lambda i: (i, 0))],
        out_shape=[SDS((T, 128), F32), SDS((T, 256), F32)],
        compiler_params=_params(1),
    )(posf, invf, sgn, *deps)


def _rope(v, c, s):
    v = v.astype(F32)
    return v * c + pltpu.roll(v, 128 - ROPE // 2, 1) * s[:, :128] + pltpu.roll(v, ROPE // 2, 1) * s[:, 128:]


def _rope_bwd(dv, c, s):
    return dv * c + pltpu.roll(dv * s[:, :128], ROPE // 2, 1) + pltpu.roll(dv * s[:, 128:], 128 - ROPE // 2, 1)


def _fold_masks(first):
    jj = lax.broadcasted_iota(jnp.int32, (CHUNK, CHUNK), 0)
    t = lax.broadcasted_iota(jnp.int32, (CHUNK, CHUNK), 1)
    prev = jj > t
    return prev, jnp.where(prev & first, -1e30, 0.0)


def _fold(band, prev):
    return jnp.where(prev, band[:CHUNK], band[CHUNK:])


def _unfold(folded, prev):
    return jnp.concatenate([jnp.where(prev, folded, 0.0), jnp.where(prev, 0.0, folded)], axis=0)


def _softmax_sink(s, sink, key_axis):
    m = jnp.maximum(jnp.max(s, axis=key_axis, keepdims=True), sink)
    p = jnp.exp(s - m)
    esink = jnp.exp(sink - m)
    inv = 1.0 / (jnp.sum(p, axis=key_axis, keepdims=True) + esink)
    return p * inv, esink * inv


def _head_pair_operand(slab, g):
    lo = lax.broadcasted_iota(jnp.int32, slab.shape, 1) < HEAD
    if g % 2 == 0:
        first = jnp.where(lo, slab, 0.0)
        second = pltpu.roll(first, HEAD, 1)
    else:
        second = jnp.where(lo, 0.0, slab)
        first = pltpu.roll(second, HEAD, 1)
    return jnp.concatenate([first, second], axis=0).astype(BF16)


def _head_pair_gradient(acc, g):
    top, bot = acc[:2 * CHUNK], acc[2 * CHUNK:]
    lo = lax.broadcasted_iota(jnp.int32, top.shape, 1) < HEAD
    if g % 2 == 0:
        return jnp.where(lo, top, 0.0) + pltpu.roll(jnp.where(lo, 0.0, bot), HEAD, 1)
    return pltpu.roll(jnp.where(lo, top, 0.0), HEAD, 1) + jnp.where(lo, 0.0, bot)


PAIRS_PER_KV = N_Q // N_KV // 2
KV_W = N_KV * HEAD


def _band(prev_ref, cur_ref, cols=slice(None)):
    return jnp.concatenate([prev_ref[:, cols], cur_ref[:, cols]], axis=0)


def _fwd_attn(proj, cos, sin, sinks):
    T = proj.shape[0]
    nb = T // CHUNK
    cur = lambda i: i
    prev = lambda i: jnp.maximum(i - 1, 0)

    def body(q_ref, kp_ref, kc_ref, vp_ref, vc_ref, cp_ref, cc_ref, sp_ref, sc_ref, sink_ref,
             o_ref, qr_ref, kr_ref, p_ref, psink_ref):
        prev_slot, bias = _fold_masks(pl.program_id(0) == 0)
        c_band, s_band = _band(cp_ref, cc_ref), _band(sp_ref, sc_ref)
        for j in range(KV_W // 128):
            cols = slice(j * 128, (j + 1) * 128)
            k_slab = _rope(_band(kp_ref, kc_ref, cols), c_band, s_band)
            kr_ref[:, cols] = k_slab[CHUNK:].astype(BF16)
            v_slab = _band(vp_ref, vc_ref, cols).astype(F32)
            for g in (2 * j, 2 * j + 1):
                k2 = _head_pair_operand(k_slab, g)
                v2 = _head_pair_operand(v_slab, g)
                for r in range(PAIRS_PER_KV):
                    lanes = slice((g * PAIRS_PER_KV + r) * 128, (g * PAIRS_PER_KV + r + 1) * 128)
                    qp = (_rope(q_ref[:, lanes], cc_ref[...], sc_ref[...]) * (HEAD ** -0.5)).astype(BF16)
                    qr_ref[:, lanes] = qp
                    s2 = _nt(k2, qp)
                    ps = []
                    for e in range(2):
                        head = 2 * (g * PAIRS_PER_KV + r) + e
                        s = _fold(s2[e * 2 * CHUNK:(e + 1) * 2 * CHUNK], prev_slot) + bias
                        p, psink = _softmax_sink(s, sink_ref[head], 0)
                        p = p.astype(BF16)
                        p_ref[head] = p
                        psink_ref[head:head + 1, :] = psink
                        ps.append(_unfold(p, prev_slot))
                    o_ref[:, lanes] = _tn(jnp.concatenate(ps, axis=0), v2).astype(BF16)

    table = lambda which, width: pl.BlockSpec((CHUNK, width), lambda i: (which(i), 0))
    return pl.pallas_call(
        body, name="fwd_attn", grid=(nb,),
        in_specs=[pl.BlockSpec((CHUNK, D), lambda i: (i, OFF_Q // D)),
                  pl.BlockSpec((CHUNK, KV_W), lambda i: (prev(i), OFF_K // KV_W)),
                  pl.BlockSpec((CHUNK, KV_W), lambda i: (i, OFF_K // KV_W)),
                  pl.BlockSpec((CHUNK, KV_W), lambda i: (prev(i), OFF_VA // KV_W)),
                  pl.BlockSpec((CHUNK, KV_W), lambda i: (i, OFF_VA // KV_W)),
                  table(prev, 128), table(cur, 128), table(prev, 256), table(cur, 256),
                  pl.BlockSpec(memory_space=pltpu.SMEM)],
        out_specs=[pl.BlockSpec((CHUNK, D), lambda i: (i, 0)), pl.BlockSpec((CHUNK, D), lambda i: (i, 0)),
                   pl.BlockSpec((CHUNK, KV_W), lambda i: (i, 0)),
                   pl.BlockSpec((None, N_Q, CHUNK, CHUNK), lambda i: (i, 0, 0, 0)),
                   pl.BlockSpec((None, N_Q, CHUNK), lambda i: (i, 0, 0))],
        out_shape=[SDS((T, D), BF16), SDS((T, D), BF16), SDS((T, KV_W), BF16),
                   SDS((nb, N_Q, CHUNK, CHUNK), BF16), SDS((nb, N_Q, CHUNK), F32)],
        compiler_params=_params(1),
    )(proj, proj, proj, proj, proj, cos, cos, sin, sin, sinks)


def _row_halves(tm):
    return [slice(0, tm // 2), slice(tm // 2, tm)] if tm % 32 == 0 else [slice(0, tm)]


def _fwd_mix(a, att, proj, x, wa, wb, wo, g1, g2):
    T = x.shape[0]
    tm = min(T, 512)
    half = D // 2

    def body(a_ref, att_ref, ga0, ga1, gb0, gb1, x_ref, wa_ref, wb_ref, wo_ref, g1_ref, g2_ref,
             mg_ref, a2_ref, b2_ref, mix_ref, x1_ref, hf_ref):
        for rows in _row_halves(tm):
            a2 = _nn(a_ref[rows, :], wa_ref[...])
            b2 = _nn(att_ref[rows, :], wb_ref[...])
            ga = jnp.concatenate([ga0[rows, :], ga1[rows, :]], axis=1).astype(F32)
            gb = jnp.concatenate([gb0[rows, :], gb1[rows, :]], axis=1).astype(F32)
            merged = (_sigmoid(ga) * a2 + _sigmoid(gb) * b2).astype(BF16)
            a2_ref[rows, :] = a2.astype(BF16)
            b2_ref[rows, :] = b2.astype(BF16)
            mg_ref[rows, :] = merged
            mix = _nn(merged, wo_ref[...])
            mix_ref[rows, :] = mix
            _, mh = _rms_stats(mix)
            x1 = x_ref[rows, :] + mh * g1_ref[...]
            x1_ref[rows, :] = x1
            _, xh = _rms_stats(x1)
            hf_ref[rows, :] = (xh * g2_ref[...]).astype(BF16)

    row = lambda i: (i, 0)
    const = lambda i: (0, 0)
    gspec = lambda off: pl.BlockSpec((tm, half), lambda i: (i, off // half))
    return pl.pallas_call(
        body, name="fwd_mix", grid=(T // tm,),
        in_specs=[pl.BlockSpec((tm, D), row), pl.BlockSpec((tm, D), row),
                  gspec(OFF_GA), gspec(OFF_GA + half), gspec(OFF_GB), gspec(OFF_GB + half),
                  pl.BlockSpec((tm, D), row), _resident((D, D)), _resident((D, D)),
                  _resident((D, D)), pl.BlockSpec((1, D), const), pl.BlockSpec((1, D), const)],
        out_specs=[pl.BlockSpec((tm, D), row)] * 6,
        out_shape=[SDS((T, D), BF16), SDS((T, D), BF16), SDS((T, D), BF16), SDS((T, D), F32), SDS((T, D), F32),
                   SDS((T, D), BF16)],
        compiler_params=_params(1),
    )(a, att, proj, proj, proj, proj, x, wa, wb, wo, g1, g2)


FF_SPLIT = N_DEV
FF_TILE = D_FF // FF_SPLIT
FF_STEP = 2048
FF_SLABS = FF_STEP // FF_TILE
FF_STEPS = D_FF // FF_STEP


def _fwd_ff(hf, wfi3, wfo, x1, tgt, g3):
    T = hf.shape[0]
    tm = min(T, 512)
    last = FF_STEPS - 1

    def body(hf_ref, wfi_ref, wfo_ref, x1_ref, tgt_ref, g3_ref, f_ref, dy_ref, dff_ref, dg3_ref, loss_ref, acc, r_s):
        i, p = pl.program_id(0), pl.program_id(1)

        @pl.when((i == 0) & (p == 0))
        def _():
            dg3_ref[...] = jnp.zeros_like(dg3_ref)
            loss_ref[...] = jnp.zeros_like(loss_ref)

        hf_t = hf_ref[...]
        for s in range(FF_SLABS):
            cols = slice(s * FF_TILE, (s + 1) * FF_TILE)
            f = _nn(hf_t, wfi_ref[p * FF_SLABS + s]).astype(BF16)
            f_ref[:, cols] = f
            rl = jnp.maximum(f.astype(F32), 0.0)
            r_s[:, cols] = (rl * rl).astype(BF16)
        part = _nn(r_s[...], wfo_ref[pl.ds(pl.multiple_of(p * FF_STEP, FF_STEP), FF_STEP), :])

        @pl.when(p == 0)
        def _():
            acc[...] = part

        @pl.when(p > 0)
        def _():
            acc[...] += part

        @pl.when(p == last)
        def _():
            r3, fh = _rms_stats(acc[...])
            e = x1_ref[...] + fh * g3_ref[...] - tgt_ref[...]
            loss_ref[...] += jnp.sum(e * e) * (0.5 / D)
            dy = e * (1.0 / D)
            dy_ref[...] = dy
            dg3_ref[...] += _colsum(dy * fh)
            dff_ref[...] = _rms_bwd(dy, fh, r3, g3_ref[...]).astype(BF16)

    row = lambda i, p: (i, 0)
    const = lambda i, p: (0, 0)
    return pl.pallas_call(
        body, name="fwd_ff", grid=(T // tm, FF_STEPS),
        in_specs=[pl.BlockSpec((tm, D), row), _resident((FF_SPLIT, D, FF_TILE)), _resident((D_FF, D)),
                  pl.BlockSpec((tm, D), row),
                  pl.BlockSpec((tm, D), row), pl.BlockSpec((1, D), const)],
        out_specs=[pl.BlockSpec((tm, FF_STEP), lambda i, p: (i, p)), pl.BlockSpec((tm, D), row),
                   pl.BlockSpec((tm, D), row), pl.BlockSpec((1, D), const), pl.BlockSpec((1, 128), const)],
        out_shape=[SDS((T, D_FF), BF16), SDS((T, D), F32), SDS((T, D), BF16), SDS((1, D), F32), SDS((1, 128), F32)],
        scratch_shapes=[pltpu.VMEM((tm, D), F32), pltpu.VMEM((tm, FF_STEP), BF16)],
        compiler_params=_params(2),
    )(hf, wfi3, wfo, x1, tgt, g3)


def _bwd_ff(dff, f, wfi3, wfo, x1, dy, mix, g1, g2):
    T = dff.shape[0]
    tm = min(T, 512)
    last = FF_STEPS - 1

    def body(dff_ref, f_ref, wfi_ref, wfo_ref, x1_ref, dy_ref, mix_ref, g1_ref, g2_ref,
             df_ref, dx1_ref, dmix_ref, dg2_ref, dg1_ref, acc):
        i, p = pl.program_id(0), pl.program_id(1)

        @pl.when((i == 0) & (p == 0))
        def _():
            dg2_ref[...] = jnp.zeros_like(dg2_ref)
            dg1_ref[...] = jnp.zeros_like(dg1_ref)

        dr = _nt(dff_ref[...], wfo_ref[pl.ds(pl.multiple_of(p * FF_STEP, FF_STEP), FF_STEP), :])
        df_ref[...] = (dr * (2.0 * jnp.maximum(f_ref[...].astype(F32), 0.0))).astype(BF16)
        part = _nt(df_ref[:, :FF_TILE], wfi_ref[p * FF_SLABS])
        for s in range(1, FF_SLABS):
            part = part + _nt(df_ref[:, s * FF_TILE:(s + 1) * FF_TILE], wfi_ref[p * FF_SLABS + s])

        @pl.when(p == 0)
        def _():
            acc[...] = part

        @pl.when(p > 0)
        def _():
            acc[...] += part

        @pl.when(p == last)
        def _():
            dhf = acc[...]
            r2, xh = _rms_stats(x1_ref[...])
            dg2_ref[...] += _colsum(dhf * xh)
            dx1 = dy_ref[...] + _rms_bwd(dhf, xh, r2, g2_ref[...])
            dx1_ref[...] = dx1
            r1, mh = _rms_stats(mix_ref[...])
            dg1_ref[...] += _colsum(dx1 * mh)
            dmix_ref[...] = _rms_bwd(dx1, mh, r1, g1_ref[...]).astype(BF16)

    row = lambda i, p: (i, 0)
    const = lambda i, p: (0, 0)
    return pl.pallas_call(
        body, name="bwd_ff", grid=(T // tm, FF_STEPS),
        in_specs=[pl.BlockSpec((tm, D), row), pl.BlockSpec((tm, FF_STEP), lambda i, p: (i, p)),
                  _resident((FF_SPLIT, D, FF_TILE)), _resident((D_FF, D)),
                  pl.BlockSpec((tm, D), row), pl.BlockSpec((tm, D), row), pl.BlockSpec((tm, D), row),
                  pl.BlockSpec((1, D), const), pl.BlockSpec((1, D), const)],
        out_specs=[pl.BlockSpec((tm, FF_STEP), lambda i, p: (i, p)), pl.BlockSpec((tm, D), row),
                   pl.BlockSpec((tm, D), row), pl.BlockSpec((1, D), const), pl.BlockSpec((1, D), const)],
        out_shape=[SDS((T, D_FF), BF16), SDS((T, D), F32), SDS((T, D), BF16), SDS((1, D), F32), SDS((1, D), F32)],
        scratch_shapes=[pltpu.VMEM((tm, D), F32)],
        compiler_params=_params(2),
    )(dff, f, wfi3, wfo, x1, dy, mix, g1, g2)


def _wgrad_ff(hf, df, f, dff):
    T = hf.shape[0]
    tt = min(T, 1024)
    wide = 2 * FF_TILE

    def body(hf_ref, df_ref, f_ref, dff_ref, dwfi_ref, dwfo_ref, acc_i, acc_o):
        t = pl.program_id(1)

        @pl.when(t == 0)
        def _():
            acc_i[...] = jnp.zeros_like(acc_i)
            acc_o[...] = jnp.zeros_like(acc_o)

        acc_i[...] += _tn(hf_ref[...], df_ref[...])
        rl = jnp.maximum(f_ref[...].astype(F32), 0.0)
        acc_o[...] += _tn((rl * rl).astype(BF16), dff_ref[...])

        @pl.when(t == T // tt - 1)
        def _():
            dwfi_ref[0] = acc_i[:, :FF_TILE].astype(BF16)
            dwfi_ref[1] = acc_i[:, FF_TILE:].astype(BF16)
            dwfo_ref[...] = acc_o[...].astype(BF16)

    return pl.pallas_call(
        body, name="wgrad_ff", grid=(D_FF // wide, T // tt),
        in_specs=[pl.BlockSpec((tt, D), lambda p, t: (t, 0)), pl.BlockSpec((tt, wide), lambda p, t: (t, p)),
                  pl.BlockSpec((tt, wide), lambda p, t: (t, p)), pl.BlockSpec((tt, D), lambda p, t: (t, 0))],
        out_specs=[pl.BlockSpec((2, D, FF_TILE), lambda p, t: (p, 0, 0)), pl.BlockSpec((wide, D), lambda p, t: (p, 0))],
        out_shape=[SDS((FF_SPLIT, D, FF_TILE), BF16), SDS((D_FF, D), BF16)],
        scratch_shapes=[pltpu.VMEM((D, wide), F32), pltpu.VMEM((wide, D), F32)],
        compiler_params=_params(2),
    )(hf, df, f, dff)


def _bwd_mix(dmix, proj, a2, b2, wo, wa, wb, after=None):
    T = dmix.shape[0]
    tm = min(T, 512)
    half = D // 2

    def body(dmix_ref, ga0, ga1, gb0, gb1, a2_ref, b2_ref, wo_ref, wa_ref, wb_ref,
             da2_ref, db2_ref, dg_ref, da_ref, datt_ref):
        for rows in _row_halves(tm):
            dmg = _nt(dmix_ref[rows, :], wo_ref[...])
            sa = _sigmoid(jnp.concatenate([ga0[rows, :], ga1[rows, :]], axis=1).astype(F32))
            sb = _sigmoid(jnp.concatenate([gb0[rows, :], gb1[rows, :]], axis=1).astype(F32))
            da2 = (dmg * sa).astype(BF16)
            db2 = (dmg * sb).astype(BF16)
            da2_ref[rows, :] = da2
            db2_ref[rows, :] = db2
            dg_ref[rows, :D] = (dmg * a2_ref[rows, :].astype(F32) * (sa * (1.0 - sa))).astype(BF16)
            dg_ref[rows, D:] = (dmg * b2_ref[rows, :].astype(F32) * (sb * (1.0 - sb))).astype(BF16)
            da_ref[rows, :] = _nt(da2, wa_ref[...]).astype(BF16)
            datt_ref[rows, :] = _nt(db2, wb_ref[...]).astype(BF16)

    row = lambda i: (i, 0)
    const = lambda i: (0, 0)
    gspec = lambda off: pl.BlockSpec((tm, half), lambda i: (i, off // half))
    body, dep_specs, deps = _after(body, 10, after)
    return pl.pallas_call(
        body, name="bwd_mix", grid=(T // tm,),
        in_specs=[pl.BlockSpec((tm, D), row), gspec(OFF_GA), gspec(OFF_GA + half), gspec(OFF_GB), gspec(OFF_GB + half),
                  pl.BlockSpec((tm, D), row), pl.BlockSpec((tm, D), row),
                  _resident((D, D)), _resident((D, D)), _resident((D, D))] + dep_specs,
        out_specs=[pl.BlockSpec((tm, D), row), pl.BlockSpec((tm, D), row), pl.BlockSpec((tm, 2 * D), row),
                   pl.BlockSpec((tm, D), row), pl.BlockSpec((tm, D), row)],
        out_shape=[SDS((T, D), BF16), SDS((T, D), BF16), SDS((T, 2 * D), BF16), SDS((T, D), BF16), SDS((T, D), BF16)],
        compiler_params=_params(1),
    )(dmix, proj, proj, proj, proj, a2, b2, wo, wa, wb, *deps)


def _wgrad_mix(merged, dmix, a, da2, att, db2):
    T = merged.shape[0]
    tt = min(T, 512)

    def body(mg_ref, dmix_ref, a_ref, da2_ref, att_ref, db2_ref, dwo_ref, dwa_ref, dwb_ref, acc):
        t = pl.program_id(0)

        @pl.when(t == 0)
        def _():
            acc[...] = jnp.zeros_like(acc)

        acc[0] += _tn(mg_ref[...], dmix_ref[...])
        acc[1] += _tn(a_ref[...], da2_ref[...])
        acc[2] += _tn(att_ref[...], db2_ref[...])

        @pl.when(t == T // tt - 1)
        def _():
            dwo_ref[...] = acc[0].astype(BF16)
            dwa_ref[...] = acc[1].astype(BF16)
            dwb_ref[...] = acc[2].astype(BF16)

    return pl.pallas_call(
        body, name="wgrad_mix", grid=(T // tt,),
        in_specs=[pl.BlockSpec((tt, D), lambda t: (t, 0))] * 6,
        out_specs=[pl.BlockSpec((D, D), lambda t: (0, 0))] * 3,
        out_shape=[SDS((D, D), BF16)] * 3,
        scratch_shapes=[pltpu.VMEM((3, D, D), F32)],
        compiler_params=_params(1),
    )(merged, dmix, a, da2, att, db2)


def _bwd_attn(qr, kr, probs, psink, proj, cos, sin, datt, after=None):
    T = proj.shape[0]
    nb = T // CHUNK
    cur = lambda i: jnp.minimum(i, nb - 1)
    prev = lambda i: jnp.maximum(jnp.minimum(i, nb - 1) - 1, 0)

    def body(q_ref, kp_ref, kc_ref, vp_ref, vc_ref, cp_ref, cc_ref, sp_ref, sc_ref, p_ref, psink_ref, do_ref,
             dq_ref, dkv_ref, dsink_ref, carry_k, carry_v):
        i = pl.program_id(0)

        @pl.when(i == 0)
        def _():
            carry_k[...] = jnp.zeros_like(carry_k)
            carry_v[...] = jnp.zeros_like(carry_v)
            dsink_ref[...] = jnp.zeros_like(dsink_ref)

        @pl.when(i < nb)
        def _():
            prev_slot, _ = _fold_masks(i == 0)
            c_band, s_band = _band(cp_ref, cc_ref), _band(sp_ref, sc_ref)
            lane = lax.broadcasted_iota(jnp.int32, (1, 128), 1)
            dsink = jnp.zeros((1, 128), F32)
            for j in range(KV_W // 128):
                cols = slice(j * 128, (j + 1) * 128)
                k_slab = _band(kp_ref, kc_ref, cols).astype(F32)
                v_slab = _band(vp_ref, vc_ref, cols).astype(F32)
                dk_slab = jnp.zeros((2 * CHUNK, 128), F32)
                dv_slab = jnp.zeros((2 * CHUNK, 128), F32)
                for g in (2 * j, 2 * j + 1):
                    k2 = _head_pair_operand(k_slab, g)
                    v2 = _head_pair_operand(v_slab, g)
                    dk2 = jnp.zeros((4 * CHUNK, 128), F32)
                    dv2 = jnp.zeros((4 * CHUNK, 128), F32)
                    for r in range(PAIRS_PER_KV):
                        pair = g * PAIRS_PER_KV + r
                        lanes = slice(pair * 128, (pair + 1) * 128)
                        qp, dop = q_ref[:, lanes], do_ref[:, lanes]
                        dp2 = _nt(v2, dop)
                        ps, dss = [], []
                        for e in range(2):
                            head = 2 * pair + e
                            rows = slice(e * 2 * CHUNK, (e + 1) * 2 * CHUNK)
                            p_b = p_ref[head]
                            p = p_b.astype(F32)
                            dp = _fold(dp2[rows], prev_slot)
                            delta = jnp.sum(p * dp, axis=0, keepdims=True)
                            ps.append(_unfold(p_b, prev_slot))
                            dss.append(_unfold((p * (dp - delta)).astype(BF16), prev_slot))
                            dsink = dsink + jnp.where(lane == head, -jnp.sum(psink_ref[head:head + 1, :] * delta), 0.0)
                        ds2 = jnp.concatenate(dss, axis=0)
                        dq = _tn(ds2, k2) * (HEAD ** -0.5)
                        dq_ref[:, lanes] = _rope_bwd(dq, cc_ref[...], sc_ref[...]).astype(BF16)
                        dk2 = dk2 + _nn(ds2, qp)
                        dv2 = dv2 + _nn(jnp.concatenate(ps, axis=0), dop)
                    dk_slab = dk_slab + _head_pair_gradient(dk2, g)
                    dv_slab = dv_slab + _head_pair_gradient(dv2, g)
                dk_slab = _rope_bwd(dk_slab, c_band, s_band)
                vcols = slice(KV_W + j * 128, KV_W + (j + 1) * 128)
                dkv_ref[:, cols] = (carry_k[:, cols] + dk_slab[:CHUNK]).astype(BF16)
                dkv_ref[:, vcols] = (carry_v[:, cols] + dv_slab[:CHUNK]).astype(BF16)
                carry_k[:, cols] = dk_slab[CHUNK:]
                carry_v[:, cols] = dv_slab[CHUNK:]
            dsink_ref[...] += dsink

        @pl.when(i == nb)
        def _():
            dkv_ref[:, :KV_W] = carry_k[...].astype(BF16)
            dkv_ref[:, KV_W:] = carry_v[...].astype(BF16)

    table = lambda which, width: pl.BlockSpec((CHUNK, width), lambda i: (which(i), 0))
    body, dep_specs, deps = _after(body, 12, after)
    return pl.pallas_call(
        body, name="bwd_attn", grid=(nb + 1,),
        in_specs=[pl.BlockSpec((CHUNK, D), lambda i: (cur(i), 0)),
                  pl.BlockSpec((CHUNK, KV_W), lambda i: (prev(i), 0)),
                  pl.BlockSpec((CHUNK, KV_W), lambda i: (cur(i), 0)),
                  pl.BlockSpec((CHUNK, KV_W), lambda i: (prev(i), OFF_VA // KV_W)),
                  pl.BlockSpec((CHUNK, KV_W), lambda i: (cur(i), OFF_VA // KV_W)),
                  table(prev, 128), table(cur, 128), table(prev, 256), table(cur, 256),
                  pl.BlockSpec((None, N_Q, CHUNK, CHUNK), lambda i: (cur(i), 0, 0, 0)),
                  pl.BlockSpec((None, N_Q, CHUNK), lambda i: (cur(i), 0, 0)),
                  pl.BlockSpec((CHUNK, D), lambda i: (cur(i), 0))] + dep_specs,
        out_specs=[pl.BlockSpec((CHUNK, D), lambda i: (cur(i), 0)),
                   pl.BlockSpec((CHUNK, 2 * KV_W), lambda i: (jnp.maximum(i - 1, 0), 0)),
                   pl.BlockSpec((1, 128), lambda i: (0, 0))],
        out_shape=[SDS((T, D), BF16), SDS((T, 2 * KV_W), BF16), SDS((1, 128), F32)],
        scratch_shapes=[pltpu.VMEM((CHUNK, KV_W), F32), pltpu.VMEM((CHUNK, KV_W), F32)],
        compiler_params=_params(1),
    )(qr, kr, kr, proj, proj, cos, cos, sin, sin, probs, psink, datt, *deps)


def _bwd_sgu(proj, da, lng, lnb, ws, bst):
    T = proj.shape[0]
    tc = min(T, 512)
    nsteps = T // tc

    def body(u_ref, vs_ref, da_ref, lng_ref, lnb_ref, ws_ref, bst_ref,
             duv_ref, dws_ref, dbs_ref, dlng_ref, dlnb_ref, dvn_s, dgu_s, dmx_sum):
        i = pl.program_id(0)

        @pl.when(i == 0)
        def _():
            dws_ref[...] = jnp.zeros_like(dws_ref)
            dlng_ref[...] = jnp.zeros_like(dlng_ref)
            dlnb_ref[...] = jnp.zeros_like(dlnb_ref)
            dmx_sum[...] = jnp.zeros_like(dmx_sum)

        u, vs, gu, tu, tv, rstd, vhat, vn = _sgu_forward_parts(u_ref, vs_ref, lng_ref, lnb_ref)
        da = da_ref[...].astype(F32)
        for g in range(GROUPS):
            wm = _masked_ws(ws_ref, g)
            cols = slice(g * CHUNK, (g + 1) * CHUNK)
            dws = jnp.zeros((CHUNK, CHUNK), F32)
            dsum = jnp.zeros((CHUNK, CHUNK), F32)
            for c in range(tc // CHUNK):
                rows = slice(c * CHUNK, (c + 1) * CHUNK)
                vn_cg = vn[rows, cols]
                mixed = _nn(wm, vn_cg) + bst_ref[:, g:g + 1]
                dgu_s[rows, cols] = da[rows, cols] * mixed
                dmx = da[rows, cols] * gu[rows, cols]
                dmxb = dmx.astype(BF16)
                dws = dws + _nt(dmxb, vn_cg)
                dsum = dsum + dmx
                dvn_s[rows, cols] = _tn(wm, dmxb)
            dws_ref[g] += dws
            dmx_sum[:, cols] += dsum
        dvn = dvn_s[...]
        dlng_ref[...] += _colsum(dvn * vhat)
        dlnb_ref[...] += _colsum(dvn)
        dvh = dvn * lng_ref[...]
        dgv = rstd * (dvh - jnp.mean(dvh, axis=-1, keepdims=True) - vhat * jnp.mean(dvh * vhat, axis=-1, keepdims=True))
        duv_ref[:, :D] = (dgu_s[...] * _gelu_grad(u, tu)).astype(BF16)
        duv_ref[:, D:] = (dgv * _gelu_grad(vs, tv)).astype(BF16)

        @pl.when(i == nsteps - 1)
        def _():
            row = lax.broadcasted_iota(jnp.int32, (CHUNK, CHUNK), 0)
            col = lax.broadcasted_iota(jnp.int32, (CHUNK, CHUNK), 1)
            for g in range(GROUPS):
                dws_ref[g] = jnp.where(row >= col, dws_ref[g], 0.0)
                dbs_ref[g:g + 1, :] = _colsum(dmx_sum[:, g * CHUNK:(g + 1) * CHUNK].T)

    const2 = lambda i: (0, 0)
    return pl.pallas_call(
        body, name="bwd_sgu", grid=(nsteps,),
        in_specs=[pl.BlockSpec((tc, D), lambda i: (i, 0)), pl.BlockSpec((tc, D), lambda i: (i, 1)),
                  pl.BlockSpec((tc, D), lambda i: (i, 0)), pl.BlockSpec((1, D), const2), pl.BlockSpec((1, D), const2),
                  pl.BlockSpec((GROUPS, CHUNK, CHUNK), lambda i: (0, 0, 0)), pl.BlockSpec((CHUNK, GROUPS), const2)],
        out_specs=[pl.BlockSpec((tc, 2 * D), lambda i: (i, 0)), pl.BlockSpec((GROUPS, CHUNK, CHUNK), lambda i: (0, 0, 0)),
                   pl.BlockSpec((GROUPS, CHUNK), const2), pl.BlockSpec((1, D), const2), pl.BlockSpec((1, D), const2)],
        out_shape=[SDS((T, 2 * D), BF16), SDS((GROUPS, CHUNK, CHUNK), F32), SDS((GROUPS, CHUNK), F32),
                   SDS((1, D), F32), SDS((1, D), F32)],
        scratch_shapes=[pltpu.VMEM((tc, D), F32), pltpu.VMEM((tc, D), F32), pltpu.VMEM((CHUNK, D), F32)],
        compiler_params=_params(1),
    )(proj, proj, da, lng, lnb, ws, bst)


IN_SEG_WIDTHS = (2 * D, D, 2 * N_KV * HEAD, 2 * D)


def _resident(shape):
    return pl.BlockSpec(shape, lambda *_: (0,) * len(shape), pipeline_mode=pl.Buffered(1))


def _bwd_in(duv, dq, dkv, dg, win_t, x, dx1, g0, after=None):
    T = x.shape[0]
    tm = min(T, 512)

    def body(duv_ref, dq_ref, dkv_ref, dg_ref, w_ref, x_ref, dx1_ref, g0_ref, gx_ref, dg0_ref):
        @pl.when(pl.program_id(0) == 0)
        def _():
            dg0_ref[...] = jnp.zeros_like(dg0_ref)

        dh, off = None, 0
        for ref, width in zip((duv_ref, dq_ref, dkv_ref, dg_ref), IN_SEG_WIDTHS):
            part = _nn(ref[...], w_ref[off:off + width, :])
            dh = part if dh is None else dh + part
            off += width
        r0, xh = _rms_stats(x_ref[...])
        dg0_ref[...] += _colsum(dh * xh)
        gx_ref[...] = dx1_ref[...] + _rms_bwd(dh, xh, r0, g0_ref[...])

    row = lambda i: (i, 0)
    body, dep_specs, deps = _after(body, 8, after)
    return pl.pallas_call(
        body, name="bwd_in", grid=(T // tm,),
        in_specs=[pl.BlockSpec((tm, w), row) for w in IN_SEG_WIDTHS] + [
            _resident((IN_W, D)), pl.BlockSpec((tm, D), row), pl.BlockSpec((tm, D), row),
            pl.BlockSpec((1, D), lambda i: (0, 0))] + dep_specs,
        out_specs=[pl.BlockSpec((tm, D), row), pl.BlockSpec((1, D), lambda i: (0, 0))],
        out_shape=[SDS((T, D), F32), SDS((1, D), F32)],
        compiler_params=_params(1),
    )(duv, dq, dkv, dg, win_t, x, dx1, g0, *deps)


def _wgrad_rows(h, segs, first_row, into, name):
    T = h.shape[0]
    tt = min(T, 1024)
    widths = [s.shape[1] for s in segs]
    rows = sum(widths)
    n_in = 1 + len(segs) + (into is not None)

    def body(*refs):
        h_ref, seg_refs = refs[0], refs[1:1 + len(segs)]
        dw_ref, acc, stage, sem = refs[n_in], refs[n_in + 1], refs[n_in + 2], refs[n_in + 3]
        t = pl.program_id(0)

        @pl.when(t == 0)
        def _():
            acc[...] = jnp.zeros_like(acc)

        off = 0
        for ref, width in zip(seg_refs, widths):
            acc[off:off + width, :] += _tn(ref[...], h_ref[...])
            off += width

        @pl.when(t == T // tt - 1)
        def _():
            stage[...] = acc[...].astype(BF16)
            out = pltpu.make_async_copy(stage, dw_ref.at[pl.ds(first_row, rows)], sem)
            out.start()
            out.wait()

    row = lambda t: (t, 0)
    return pl.pallas_call(
        body, name=name, grid=(T // tt,),
        in_specs=[pl.BlockSpec((tt, D), row)] + [pl.BlockSpec((tt, w), row) for w in widths] + [_ANY] * (into is not None),
        out_specs=_ANY,
        out_shape=SDS((IN_W, D), BF16),
        input_output_aliases={} if into is None else {n_in - 1: 0},
        scratch_shapes=[pltpu.VMEM((rows, D), F32), pltpu.VMEM((rows, D), BF16), pltpu.SemaphoreType.DMA],
        compiler_params=_params(1),
    )(h, *segs, *([] if into is None else [into]))


def _wgrad_in(h, duv, dq, dkv, dg):
    dw = _wgrad_rows(h, [dg], IN_SEG_WIDTHS[0] + IN_SEG_WIDTHS[1] + IN_SEG_WIDTHS[2], None, "wgrad_in_gates")
    dw = _wgrad_rows(h, [duv], 0, dw, "wgrad_in_uv")
    return _wgrad_rows(h, [dq, dkv], IN_SEG_WIDTHS[0], dw, "wgrad_in_qkv")


def _place():
    x, y, c = lax.axis_index("x"), lax.axis_index("y"), lax.axis_index("c")
    return x, y, c, 4 * x + 2 * y + c


def _peers(x, y, c):
    out = []
    for mask in range(1, N_DEV):
        px = 1 - x if mask & 4 else x
        py = 1 - y if mask & 2 else y
        pc = 1 - c if mask & 1 else c
        out.append(((px, py, pc), 4 * px + 2 * py + pc))
    return out


def _all_to_all(arrays, gather, name, after=None):
    n = len(arrays)

    def body(*refs):
        ins, outs = refs[:n], refs[n:2 * n]
        send_sems, recv_sems, local_sems = refs[2 * n:]
        x, y, c, me = _place()
        local, sends, recvs = [], [], []
        for a in range(n):
            src_own = ins[a] if gather[a] else ins[a].at[me]
            local.append(pltpu.make_async_copy(src_own, outs[a].at[me], local_sems.at[a]))
            for k, (peer, pid) in enumerate(_peers(x, y, c)):
                sem = a * (N_DEV - 1) + k
                src = ins[a] if gather[a] else ins[a].at[pid]
                sends.append(pltpu.make_async_remote_copy(
                    src_ref=src, dst_ref=outs[a].at[me], send_sem=send_sems.at[sem], recv_sem=recv_sems.at[sem],
                    device_id=peer, device_id_type=MESH))
                recvs.append(pltpu.make_async_remote_copy(
                    src_ref=src, dst_ref=outs[a].at[pid], send_sem=send_sems.at[sem], recv_sem=recv_sems.at[sem],
                    device_id=peer, device_id_type=MESH))
        for cp in local + sends:
            cp.start()
        for cp in recvs:
            cp.wait_recv()
        for cp in sends:
            cp.wait_send()
        for cp in local:
            cp.wait()

    out_shape = [SDS((N_DEV,) + a.shape if gt else a.shape, a.dtype) for a, gt in zip(arrays, gather)]
    nsem = n * (N_DEV - 1)
    body, dep_specs, deps = _after(body, n, after)
    return pl.pallas_call(
        body, name=name,
        in_specs=[pl.BlockSpec(memory_space=pl.ANY)] * n + dep_specs,
        out_specs=[pl.BlockSpec(memory_space=pl.ANY)] * n,
        out_shape=out_shape,
        scratch_shapes=[pltpu.SemaphoreType.DMA((nsem,)), pltpu.SemaphoreType.DMA((nsem,)), pltpu.SemaphoreType.DMA((n,))],
    )(*arrays, *deps)


_HBM = pl.BlockSpec(memory_space=pltpu.HBM)
_SEM = pl.BlockSpec(memory_space=pltpu.SEMAPHORE)
_EFFECT = pltpu.SideEffectType.DATAFLOW_SIDE_EFFECTING
GATHER = "gather"
SCATTER = "scatter"
SPREAD = "spread"


def _zone_shape(a, mode):
    if mode == GATHER:
        return (N_DEV,) + a.shape
    return (N_DEV - 1,) + (a.shape[1:] if mode == SCATTER else a.shape)


def _start_copies(arrays, modes, name, after=None):
    n = len(arrays)
    zones = [lax.empty(_zone_shape(a, m), a.dtype) for a, m in zip(arrays, modes)]

    def body(*refs):
        ins, lands = refs[:n], refs[n:2 * n]
        send_sems, recv_sems = refs[-2 * n - 3], refs[-2 * n - 2]
        token = refs[-1]
        x, y, c, me = _place()
        for a in range(n):
            for k, (peer, pid) in enumerate(_peers(x, y, c)):
                src = ins[a].at[pid] if modes[a] == SCATTER else ins[a]
                dst = lands[a].at[me] if modes[a] == GATHER else lands[a].at[k]
                pltpu.make_async_remote_copy(src_ref=src, dst_ref=dst, send_sem=send_sems.at[a], recv_sem=recv_sems.at[a],
                                             device_id=peer, device_id_type=MESH).start()
        token[...] = jnp.zeros_like(token)

    hbm = lambda a: pltpu.HBM(a.shape, a.dtype)
    sems = pltpu.SemaphoreType.DMA((n,))
    extra = [] if after is None else [after]
    operands = [pltpu.with_memory_space_constraint(a, pltpu.HBM) for a in list(arrays) + zones]
    res = pl.pallas_call(
        body, name=name,
        out_shape=(sems, sems, *[hbm(a) for a in arrays], *[hbm(z) for z in zones], SDS((8, 128), F32)),
        in_specs=[_HBM] * (2 * n) + [_ANY] * len(extra),
        out_specs=(_SEM, _SEM, *[_HBM] * (2 * n), pl.BlockSpec(memory_space=pltpu.VMEM)),
        input_output_aliases={i: 2 + i for i in range(2 * n)},
        compiler_params=pltpu.CompilerParams(has_side_effects=_EFFECT),
    )(*operands, *extra)
    return res[0], res[1], list(res[2:2 + n]), list(res[2 + n:2 + 2 * n]), res[-1]


def _wait_copies(started, after, name, count=N_DEV - 1):
    send_sems, recv_sems, thru, zones, _ = started
    nt, nz = len(thru), len(zones)

    def body(*refs):
        lands = refs[nt:nt + nz]
        send_ref, recv_ref = refs[nt + nz], refs[nt + nz + 1]
        x, y, c, _ = _place()
        for a in range(nz):
            blocks = lands[a].at[pl.ds(0, count)]
            cp = pltpu.make_async_remote_copy(src_ref=blocks, dst_ref=blocks, send_sem=send_ref.at[a], recv_sem=recv_ref.at[a],
                                              device_id=(x, y, 1 - c), device_id_type=MESH)
            cp.wait_send()
            cp.wait_recv()

    hbm = lambda a: pltpu.HBM(a.shape, a.dtype)
    res = pl.pallas_call(
        body, name=name,
        out_shape=tuple(hbm(a) for a in thru + zones),
        in_specs=[_HBM] * (nt + nz) + [_SEM, _SEM, _ANY],
        out_specs=tuple([_HBM] * (nt + nz)),
        input_output_aliases={i: i for i in range(nt + nz)},
        compiler_params=pltpu.CompilerParams(has_side_effects=_EFFECT),
    )(*thru, *zones, send_sems, recv_sems, after)
    return list(res[:nt]), list(res[nt:])


def _split_start(body, arrays, zones, name, after):
    n = len(arrays) + len(zones)
    hbm = lambda a: pltpu.HBM(a.shape, a.dtype)
    sems = pltpu.SemaphoreType.DMA((max(len(zones), 1),))
    extra = [] if after is None else [after]
    operands = [pltpu.with_memory_space_constraint(a, pltpu.HBM) for a in list(arrays) + list(zones)]
    res = pl.pallas_call(
        body, name=name,
        out_shape=(sems, sems, *[hbm(a) for a in operands], SDS((8, 128), F32)),
        in_specs=[_HBM] * n + [_ANY] * len(extra),
        out_specs=(_SEM, _SEM, *[_HBM] * n, pl.BlockSpec(memory_space=pltpu.VMEM)),
        input_output_aliases={i: 2 + i for i in range(n)},
        compiler_params=pltpu.CompilerParams(has_side_effects=_EFFECT),
    )(*operands, *extra)
    return res[0], res[1], list(res[2:2 + len(arrays)]), list(res[2 + len(arrays):2 + n]), res[-1]


def _gather_first_leg(shard, name, after=None):
    zone = lax.empty((N_DEV,) + shard.shape, shard.dtype)
    extra = 0 if after is None else 1

    def body(*refs):
        src, land = refs[0], refs[1]
        send_sem, recv_sem, token = refs[2 + extra], refs[3 + extra], refs[-1]
        x, y, c, me = _place()
        for peer in ((x, y, 1 - c), (1 - x, y, c), (x, 1 - y, c), (1 - x, 1 - y, c)):
            pltpu.make_async_remote_copy(src_ref=src, dst_ref=land.at[me], send_sem=send_sem.at[0], recv_sem=recv_sem.at[0],
                                         device_id=peer, device_id_type=MESH).start()
        token[...] = jnp.zeros_like(token)

    return _split_start(body, [shard], [zone], name, after)


def _gather_second_leg(zone, name, after=None):
    extra = 0 if after is None else 1

    def body(*refs):
        land = refs[0]
        send_sem, recv_sem, token = refs[1 + extra], refs[2 + extra], refs[-1]
        x, y, c, _ = _place()
        for px, py in ((1 - x, y), (x, 1 - y), (1 - x, 1 - y)):
            slot = 4 * px + 2 * py + c
            pltpu.make_async_remote_copy(src_ref=land.at[slot], dst_ref=land.at[slot], send_sem=send_sem.at[0],
                                         recv_sem=recv_sem.at[0], device_id=(x, y, 1 - c), device_id_type=MESH).start()
        token[...] = jnp.zeros_like(token)

    return _split_start(body, [], [zone], name, after)


UPDATE_BLOCK_ELEMS = 256 * 1024


def _update_rows(R, C):
    fits = [t for t in range(8, R + 1, 8) if R % t == 0 and t * C <= UPDATE_BLOCK_ELEMS]
    whole = [t for t in fits if t % 16 == 0]
    return max(whole or fits)


def _adamw_math(g, w, m, v):
    m2 = ADAM_B1 * m + (1.0 - ADAM_B1) * g
    v2 = ADAM_B2 * v + (1.0 - ADAM_B2) * (g * g)
    m_hat = m2 / (1.0 - ADAM_B1 ** ADAM_STEP)
    v_hat = v2 / (1.0 - ADAM_B2 ** ADAM_STEP)
    delta = -ADAM_LR * (m_hat / (jnp.sqrt(v_hat) + ADAM_EPS) + ADAM_WD * w)
    return delta, m2, v2


def _sum_adamw(parts, w, m, v, name):
    R, C = w.shape
    tr = _update_rows(R, C)

    def body(p_ref, w_ref, m_ref, v_ref, g_ref, d_ref, m2_ref, v2_ref):
        g = p_ref[0]
        for k in range(1, N_DEV):
            g = g + p_ref[k]
        g_ref[...] = g
        d_ref[...], m2_ref[...], v2_ref[...] = _adamw_math(g, w_ref[...], m_ref[...], v_ref[...])

    blk = pl.BlockSpec((tr, C), lambda i: (i, 0))
    return pl.pallas_call(
        body, name=name, grid=(R // tr,),
        in_specs=[pl.BlockSpec((N_DEV, tr, C), lambda i: (0, i, 0)), blk, blk, blk],
        out_specs=[blk] * 4,
        out_shape=[SDS((R, C), F32)] * 4,
        compiler_params=_params(1),
    )(parts, w, m, v)


def _sum_adamw_peers(me, own, parts, w, m, v, name, replicated):
    R, C = w.shape
    tr = _update_rows(R, C)

    def body(me_ref, own_ref, p_ref, w_ref, m_ref, v_ref, g_ref, d_ref, m2_ref, v2_ref):
        if replicated:
            mine = me_ref[0]
            g = None
            for j in range(N_DEV):
                k = jnp.maximum(jnp.bitwise_xor(mine, j) - 1, 0)
                term = jnp.where(mine == j, own_ref[...], p_ref[k])
                g = term if g is None else g + term
        else:
            g = own_ref[...].astype(F32)
            for k in range(N_DEV - 1):
                g = g + p_ref[k].astype(F32)
        g_ref[...] = g
        d_ref[...], m2_ref[...], v2_ref[...] = _adamw_math(g, w_ref[...], m_ref[...], v_ref[...])

    blk = pl.BlockSpec((tr, C), lambda i, me_ref: (i, 0))
    own_spec = blk if replicated else pl.BlockSpec((None, tr, C), lambda i, me_ref: (me_ref[0], i, 0))
    return pl.pallas_call(
        body, name=name,
        grid_spec=pltpu.PrefetchScalarGridSpec(
            num_scalar_prefetch=1, grid=(R // tr,),
            in_specs=[own_spec, pl.BlockSpec((N_DEV - 1, tr, C), lambda i, me_ref: (0, i, 0)), blk, blk, blk],
            out_specs=[blk] * 4),
        out_shape=[SDS((R, C), F32)] * 4,
        compiler_params=_params(1),
    )(me, own, parts, w, m, v)


SMALL = ("ln_v_gain", "ln_v_bias", "w_spatial", "b_spatial", "sinks", "norm_mix_post", "norm_ff_pre", "norm_ff_post")
SMALL_ROWS = {"ln_v_gain": 8, "ln_v_bias": 8, "w_spatial": 1024, "b_spatial": 8, "sinks": 8,
              "norm_mix_post": 8, "norm_ff_pre": 8, "norm_ff_post": 8}
SMALL_PACK_ROWS = 1152


def _pack_small(vals):
    rows = []
    for name in SMALL:
        flat = vals[name].reshape(-1)
        pad = SMALL_ROWS[name] * 128 - flat.shape[0]
        if pad:
            flat = jnp.concatenate([flat, jnp.zeros((pad,), F32)])
        rows.append(flat.reshape(SMALL_ROWS[name], 128))
    rows.append(jnp.zeros((SMALL_PACK_ROWS - sum(SMALL_ROWS.values()), 128), F32))
    return jnp.concatenate(rows, axis=0)


def _unpack_small(packed, shapes):
    out, r = {}, 0
    for name in SMALL:
        n = 1
        for s in shapes[name]:
            n *= s
        out[name] = packed[r:r + SMALL_ROWS[name]].reshape(-1)[:n].reshape(shapes[name])
        r += SMALL_ROWS[name]
    return out


def _rope_rows():
    d = jnp.arange(128) % HEAD
    inv = ROPE_THETA ** (-(2.0 * (d % (ROPE // 2))).astype(F32) / ROPE)
    invf = jnp.where(d < ROPE, inv, 0.0).astype(F32).reshape(1, 128)
    sgn = jnp.where(d < ROPE // 2, -1.0, jnp.where(d < ROPE, 1.0, 0.0)).astype(F32).reshape(1, 128)
    return invf, sgn


def kernel(x, positions, w_in, ln_v_gain, ln_v_bias, w_spatial, b_spatial, sinks, w_a, w_b, w_o, norm_mix_pre, norm_mix_post, w_ff_in, w_ff_out, norm_ff_pre, norm_ff_post, loss_target, m_w_in, m_ln_v_gain, m_ln_v_bias, m_w_spatial, m_b_spatial, m_sinks, m_w_a, m_w_b, m_w_o, m_norm_mix_pre, m_norm_mix_post, m_w_ff_in, m_w_ff_out, m_norm_ff_pre, m_norm_ff_post, v_w_in, v_ln_v_gain, v_ln_v_bias, v_w_spatial, v_b_spatial, v_sinks, v_w_a, v_w_b, v_w_o, v_norm_mix_pre, v_norm_mix_post, v_w_ff_in, v_w_ff_out, v_norm_ff_pre, v_norm_ff_post):
    given = dict(locals())
    T = x.shape[1]
    xt = x[0]
    tgt = loss_target[0]
    bst = b_spatial[0].T
    ws = w_spatial[0]

    me = 4 * lax.axis_index("x") + 2 * lax.axis_index("y") + lax.axis_index("c")
    me_arr = me.astype(jnp.int32).reshape(1)

    def with_own(zone, shard):
        return lax.dynamic_update_slice(zone, shard[None], (me,) + (0,) * shard.ndim)

    rest = ("w_a", "w_b", "w_o", "w_ff_in", "w_ff_out")
    shard = {n: given[n][0].astype(BF16) for n in rest}
    g_one = _gather_first_leg(w_in[0].T.astype(BF16), "gather_in_start")
    cos, sin = _rope_tables(positions.astype(F32).reshape(T, 1), *_rope_rows(), after=g_one[-1])
    h = _rms_pre(xt, norm_mix_pre, after=cos)
    (own_win,), (win8,) = _wait_copies(g_one, h, "gather_in_wait", count=4)
    g_two = _gather_second_leg(win8, "gather_in_pass_start")
    g_rest = _start_copies([shard[n] for n in rest], [GATHER] * len(rest), "gather_rest_start", after=g_two[-1])
    _, (win8,) = _wait_copies(g_two, g_rest[-1], "gather_in_pass_wait", count=3)
    win = with_own(win8, own_win).reshape(IN_W, D)

    proj = _fwd_in(h, win)
    att, qr, kr, probs, psink = _fwd_attn(proj, cos, sin, sinks[0])
    a = _fwd_sgu(proj, ln_v_gain, ln_v_bias, ws, bst, after=att)
    gw = {n: with_own(z, own) for n, own, z in zip(rest, *_wait_copies(g_rest, a, "gather_rest_wait"))}
    wa, wb, wo = (gw[n].reshape(D, D) for n in ("w_a", "w_b", "w_o"))
    wfi3 = gw["w_ff_in"]
    wfo = gw["w_ff_out"].reshape(D_FF, D)
    merged, a2, b2, mix, x1, hf = _fwd_mix(a, att, proj, xt, wa, wb, wo, norm_mix_post, norm_ff_pre)
    f, dy, dff, dg3, loss_part = _fwd_ff(hf, wfi3, wfo, x1, tgt, norm_ff_post)

    df, dx1, dmix, dg2, dg1 = _bwd_ff(dff, f, wfi3, wfo, x1, dy, mix, norm_mix_post, norm_ff_pre)
    dwfi3, dwfo = _wgrad_ff(hf, df, f, dff)
    own_ff = [dwfi3, dwfo.reshape(N_DEV, D_FF // N_DEV, D)]
    x_ff = _start_copies(own_ff, [SCATTER] * 2, "exchange_ff_start")
    da2, db2, dgate, da, datt = _bwd_mix(dmix, proj, a2, b2, wo, wa, wb, after=x_ff[-1])
    dwo, dwa, dwb = _wgrad_mix(merged, dmix, a, da2, att, db2)
    own_mix = [g.reshape(N_DEV, D // N_DEV, D) for g in (dwa, dwb, dwo)]
    x_mix = _start_copies(own_mix, [SCATTER] * 3, "exchange_mix_start")
    dq, dkv, dsink = _bwd_attn(qr, kr, probs, psink, proj, cos, sin, datt, after=x_mix[-1])
    duv, dws, dbs, dlng, dlnb = _bwd_sgu(proj, da, ln_v_gain, ln_v_bias, ws, bst)
    small_grads = {"ln_v_gain": dlng, "ln_v_bias": dlnb, "w_spatial": dws, "b_spatial": dbs, "sinks": dsink[:, :N_Q],
                   "norm_mix_post": dg1, "norm_ff_pre": dg2, "norm_ff_post": dg3}
    x_small = _start_copies([_pack_small(small_grads)], [SPREAD], "exchange_small_start")
    dwin = _wgrad_in(h, duv, dq, dkv, dgate)
    own_in = [dwin.reshape(N_DEV, IN_W // N_DEV, D)]
    x_in = _start_copies(own_in, [SCATTER], "exchange_in_start", after=x_small[-1])
    grad_x, dg0 = _bwd_in(duv, dq, dkv, dgate, win, xt, dx1, norm_mix_pre, after=x_in[-1])

    results = {}

    def update(n, own, parts, transposed=False):
        state = [given[k + n][0].T if transposed else given[k + n][0] for k in ("", "m_", "v_")]
        res = _sum_adamw_peers(me_arr, own, parts, *state, "adamw_" + n, False)
        results[n] = [(r.T if transposed else r).reshape(given[n].shape) for r in res]

    own_ff, p_ff = _wait_copies(x_ff, grad_x, "exchange_ff_wait")
    update("w_ff_in", own_ff[0], p_ff[0])
    update("w_ff_out", own_ff[1], p_ff[1])
    own_mix, p_mix = _wait_copies(x_mix, results["w_ff_out"][0], "exchange_mix_wait")
    for n, own, parts in zip(("w_a", "w_b", "w_o"), own_mix, p_mix):
        update(n, own, parts)
    tail = jnp.concatenate([dg0.reshape(8, 128), jnp.tile(loss_part, (8, 1))], axis=0)
    (tail_all,) = _all_to_all([tail], [True], "exchange_tail", after=results["w_o"][0])
    dg0_all = tail_all[:, :8]
    own_small, p_small = _wait_copies(x_small, tail_all, "exchange_small_wait")
    own_in, p_in = _wait_copies(x_in, p_small[0], "exchange_in_wait")
    update("w_in", own_in[0], p_in[0], transposed=True)
    packed = _sum_adamw_peers(me_arr, own_small[0], p_small[0], _pack_small({n: given[n] for n in SMALL}),
                              _pack_small({n: given["m_" + n] for n in SMALL}),
                              _pack_small({n: given["v_" + n] for n in SMALL}), "adamw_small", True)
    shapes = {n: given[n].shape for n in SMALL}
    unpacked = [_unpack_small(p, shapes) for p in packed]
    for n in SMALL:
        results[n] = [u[n] for u in unpacked]
    n = "norm_mix_pre"
    results[n] = [r.reshape(given[n].shape) for r in _sum_adamw(
        dg0_all, given[n].reshape(8, 128), given["m_" + n].reshape(8, 128), given["v_" + n].reshape(8, 128), "adamw_" + n)]

    loss = jnp.sum(tail_all[:, 8, 0])
    order = ("w_in", "ln_v_gain", "ln_v_bias", "w_spatial", "b_spatial", "sinks", "w_a", "w_b", "w_o", "norm_mix_pre",
             "norm_mix_post", "w_ff_in", "w_ff_out", "norm_ff_pre", "norm_ff_post")
    out = [loss, grad_x.reshape(x.shape)]
    for k in range(4):
        out += [results[n][k] for n in order]
    return tuple(out)
```

```python
import jax
import jax.numpy as jnp
from jax import lax
from jax.experimental import pallas as pl
from jax.experimental.pallas import tpu as pltpu

F32 = jnp.float32
BF16 = jnp.bfloat16

N_DEV = 8
D = 1024
D_FF = 4096
IN_W = 5632
CHUNK = 128
GROUPS = 8
HEAD = 64
N_Q = 16
N_KV = 4
ROPE = 16
ROPE_THETA = 500000.0
EPS = 1e-6
OFF_Q, OFF_K, OFF_VA, OFF_GA, OFF_GB = 2048, 3072, 3328, 3584, 4608

ADAM_LR = 0.001
ADAM_B1 = 0.9
ADAM_B2 = 0.999
ADAM_EPS = 1e-08
ADAM_WD = 0.01
ADAM_STEP = 10

VMEM_LIMIT = 56 * 1024 * 1024

SDS = jax.ShapeDtypeStruct
MESH = pl.DeviceIdType.MESH


def _params(n_axes):
    return pltpu.CompilerParams(dimension_semantics=("arbitrary",) * n_axes, vmem_limit_bytes=VMEM_LIMIT)


def _nt(a, b):
    return lax.dot_general(a, b, (((1,), (1,)), ((), ())), preferred_element_type=F32)


def _tn(a, b):
    return lax.dot_general(a, b, (((0,), (0,)), ((), ())), preferred_element_type=F32)


def _nn(a, b):
    return jnp.dot(a, b, preferred_element_type=F32)


def _gelu(x):
    t = jnp.tanh(0.7978845608028654 * (x + 0.044715 * (x * x * x)))
    return 0.5 * x * (1.0 + t), t


def _gelu_grad(x, t):
    return 0.5 * (1.0 + t) + 0.5 * x * (1.0 - t * t) * (0.7978845608028654 * (1.0 + 3.0 * 0.044715 * x * x))


def _sigmoid(x):
    return 1.0 / (1.0 + jnp.exp(-x))


def _rms_stats(v):
    r = lax.rsqrt(jnp.mean(v * v, axis=-1, keepdims=True) + EPS)
    return r, v * r


def _rms_bwd(d, vhat, r, g):
    gd = g * d
    return r * (gd - vhat * jnp.mean(gd * vhat, axis=-1, keepdims=True))


def _colsum(v):
    return jnp.sum(v, axis=0, keepdims=True)


_ANY = pl.BlockSpec(memory_space=pl.ANY)


def _after(body, n_in, after):
    if after is None:
        return body, [], []

    def ordered(*refs):
        return body(*refs[:n_in], *refs[n_in + 1:])

    return ordered, [_ANY], [after]


def _rms_pre(x, g0, after=None):
    T = x.shape[0]
    tm = min(T, 1024)

    def body(x_ref, g_ref, h_ref):
        _, xh = _rms_stats(x_ref[...])
        h_ref[...] = (xh * g_ref[...]).astype(BF16)

    body, dep_specs, deps = _after(body, 2, after)
    return pl.pallas_call(
        body, name="rms_pre", grid=(T // tm,),
        in_specs=[pl.BlockSpec((tm, D), lambda i: (i, 0)), pl.BlockSpec((1, D), lambda i: (0, 0))] + dep_specs,
        out_specs=pl.BlockSpec((tm, D), lambda i: (i, 0)),
        out_shape=SDS((T, D), BF16),
        compiler_params=_params(1),
    )(x, g0, *deps)


def _fwd_in(h, win_t):
    T = h.shape[0]
    tm, tn = min(T, 1024), 1408

    def body(h_ref, w_ref, p_ref):
        p_ref[...] = _nt(h_ref[...], w_ref[...]).astype(BF16)

    return pl.pallas_call(
        body, name="fwd_in", grid=(T // tm, IN_W // tn),
        in_specs=[pl.BlockSpec((tm, D), lambda i, j: (i, 0)), pl.BlockSpec((tn, D), lambda i, j: (j, 0))],
        out_specs=pl.BlockSpec((tm, tn), lambda i, j: (i, j)),
        out_shape=SDS((T, IN_W), BF16),
        compiler_params=_params(2),
    )(h, win_t)


def _sgu_forward_parts(u_ref, vs_ref, lng_ref, lnb_ref):
    u = u_ref[...].astype(F32)
    vs = vs_ref[...].astype(F32)
    gu, tu = _gelu(u)
    gv, tv = _gelu(vs)
    mu = jnp.mean(gv, axis=-1, keepdims=True)
    dv = gv - mu
    rstd = lax.rsqrt(jnp.mean(dv * dv, axis=-1, keepdims=True) + EPS)
    vhat = dv * rstd
    vn = (vhat * lng_ref[...] + lnb_ref[...]).astype(BF16)
    return u, vs, gu, tu, tv, rstd, vhat, vn


def _masked_ws(ws_ref, g):
    row = lax.broadcasted_iota(jnp.int32, (CHUNK, CHUNK), 0)
    col = lax.broadcasted_iota(jnp.int32, (CHUNK, CHUNK), 1)
    return jnp.where(row >= col, ws_ref[g], 0.0).astype(BF16)


def _fwd_sgu(proj, lng, lnb, ws, bst, after=None):
    T = proj.shape[0]
    tc = min(T, 512)

    def body(u_ref, vs_ref, lng_ref, lnb_ref, ws_ref, bst_ref, a_ref):
        _, _, gu, _, _, _, _, vn = _sgu_forward_parts(u_ref, vs_ref, lng_ref, lnb_ref)
        for g in range(GROUPS):
            wm = _masked_ws(ws_ref, g)
            cols = slice(g * CHUNK, (g + 1) * CHUNK)
            for c in range(tc // CHUNK):
                rows = slice(c * CHUNK, (c + 1) * CHUNK)
                mixed = _nn(wm, vn[rows, cols]) + bst_ref[:, g:g + 1]
                a_ref[rows, cols] = (gu[rows, cols] * mixed).astype(BF16)

    body, dep_specs, deps = _after(body, 6, after)
    return pl.pallas_call(
        body, name="fwd_sgu", grid=(T // tc,),
        in_specs=[pl.BlockSpec((tc, D), lambda i: (i, 0)), pl.BlockSpec((tc, D), lambda i: (i, 1)),
                  pl.BlockSpec((1, D), lambda i: (0, 0)), pl.BlockSpec((1, D), lambda i: (0, 0)),
                  pl.BlockSpec((GROUPS, CHUNK, CHUNK), lambda i: (0, 0, 0)),
                  pl.BlockSpec((CHUNK, GROUPS), lambda i: (0, 0))] + dep_specs,
        out_specs=pl.BlockSpec((tc, D), lambda i: (i, 0)),
        out_shape=SDS((T, D), BF16),
        compiler_params=_params(1),
    )(proj, proj, lng, lnb, ws, bst, *deps)


def _rope_tables(posf, invf, sgn, after=None):
    T = posf.shape[0]
    tr = min(T, 1024)

    def body(pos_ref, invf_ref, sgn_ref, c_ref, s_ref):
        ang = pos_ref[...] * invf_ref[...]
        c_ref[...] = jnp.cos(ang)
        s = jnp.sin(ang)
        s_ref[:, :128] = jnp.where(sgn_ref[...] < 0.0, -s, 0.0)
        s_ref[:, 128:] = jnp.where(sgn_ref[...] > 0.0, s, 0.0)

    body, dep_specs, deps = _after(body, 3, after)
    return pl.pallas_call(
        body, name="rope_tables", grid=(T // tr,),
        in_specs=[pl.BlockSpec((tr, 1), lambda i: (i, 0)), pl.BlockSpec((1, 128), lambda i: (0, 0)),
                  pl.BlockSpec((1, 128), lambda i: (0, 0))] + dep_specs,
        out_specs=[pl.BlockSpec((tr, 128), lambda i: (i, 0)), pl.BlockSpec((tr, 256), lambda i: (i, 0))],
        out_shape=[SDS((T, 128), F32), SDS((T, 256), F32)],
        compiler_params=_params(1),
    )(posf, invf, sgn, *deps)


def _rope(v, c, s):
    v = v.astype(F32)
    return v * c + pltpu.roll(v, 128 - ROPE // 2, 1) * s[:, :128] + pltpu.roll(v, ROPE // 2, 1) * s[:, 128:]


def _rope_bwd(dv, c, s):
    return dv * c + pltpu.roll(dv * s[:, :128], ROPE // 2, 1) + pltpu.roll(dv * s[:, 128:], 128 - ROPE // 2, 1)


def _fold_masks(first):
    jj = lax.broadcasted_iota(jnp.int32, (CHUNK, CHUNK), 0)
    t = lax.broadcasted_iota(jnp.int32, (CHUNK, CHUNK), 1)
    prev = jj > t
    return prev, jnp.where(prev & first, -1e30, 0.0)


def _fold(band, prev):
    return jnp.where(prev, band[:CHUNK], band[CHUNK:])


def _unfold(folded, prev):
    return jnp.concatenate([jnp.where(prev, folded, 0.0), jnp.where(prev, 0.0, folded)], axis=0)


def _softmax_sink(s, sink, key_axis):
    m = jnp.maximum(jnp.max(s, axis=key_axis, keepdims=True), sink)
    p = jnp.exp(s - m)
    esink = jnp.exp(sink - m)
    inv = 1.0 / (jnp.sum(p, axis=key_axis, keepdims=True) + esink)
    return p * inv, esink * inv


def _head_pair_operand(slab, g):
    lo = lax.broadcasted_iota(jnp.int32, slab.shape, 1) < HEAD
    if g % 2 == 0:
        first = jnp.where(lo, slab, 0.0)
        second = pltpu.roll(first, HEAD, 1)
    else:
        second = jnp.where(lo, 0.0, slab)
        first = pltpu.roll(second, HEAD, 1)
    return jnp.concatenate([first, second], axis=0).astype(BF16)


def _head_pair_gradient(acc, g):
    top, bot = acc[:2 * CHUNK], acc[2 * CHUNK:]
    lo = lax.broadcasted_iota(jnp.int32, top.shape, 1) < HEAD
    if g % 2 == 0:
        return jnp.where(lo, top, 0.0) + pltpu.roll(jnp.where(lo, 0.0, bot), HEAD, 1)
    return pltpu.roll(jnp.where(lo, top, 0.0), HEAD, 1) + jnp.where(lo, 0.0, bot)


PAIRS_PER_KV = N_Q // N_KV // 2
KV_W = N_KV * HEAD


def _band(prev_ref, cur_ref, cols=slice(None)):
    return jnp.concatenate([prev_ref[:, cols], cur_ref[:, cols]], axis=0)


def _fwd_attn(proj, cos, sin, sinks):
    T = proj.shape[0]
    nb = T // CHUNK
    cur = lambda i: i
    prev = lambda i: jnp.maximum(i - 1, 0)

    def body(q_ref, kp_ref, kc_ref, vp_ref, vc_ref, cp_ref, cc_ref, sp_ref, sc_ref, sink_ref,
             o_ref, qr_ref, kr_ref, p_ref, psink_ref):
        prev_slot, bias = _fold_masks(pl.program_id(0) == 0)
        c_band, s_band = _band(cp_ref, cc_ref), _band(sp_ref, sc_ref)
        for j in range(KV_W // 128):
            cols = slice(j * 128, (j + 1) * 128)
            k_slab = _rope(_band(kp_ref, kc_ref, cols), c_band, s_band)
            kr_ref[:, cols] = k_slab[CHUNK:].astype(BF16)
            v_slab = _band(vp_ref, vc_ref, cols).astype(F32)
            for g in (2 * j, 2 * j + 1):
                k2 = _head_pair_operand(k_slab, g)
                v2 = _head_pair_operand(v_slab, g)
                pairs = [g * PAIRS_PER_KV + r for r in range(PAIRS_PER_KV)]
                qps = []
                for pair in pairs:
                    lanes = slice(pair * 128, (pair + 1) * 128)
                    qps.append((_rope(q_ref[:, lanes], cc_ref[...], sc_ref[...]) * (HEAD ** -0.5)).astype(BF16))
                    qr_ref[:, lanes] = qps[-1]
                s2 = _nt(k2, jnp.concatenate(qps, axis=0))
                pcols = []
                for r, pair in enumerate(pairs):
                    ps = []
                    for e in range(2):
                        head = 2 * pair + e
                        s = _fold(s2[e * 2 * CHUNK:(e + 1) * 2 * CHUNK, r * 128:(r + 1) * 128], prev_slot) + bias
                        p, psink = _softmax_sink(s, sink_ref[head], 0)
                        p = p.astype(BF16)
                        p_ref[head] = p
                        psink_ref[head:head + 1, :] = psink
                        ps.append(_unfold(p, prev_slot))
                    pcols.append(jnp.concatenate(ps, axis=0))
                o = _tn(jnp.concatenate(pcols, axis=1), v2).astype(BF16)
                for r, pair in enumerate(pairs):
                    o_ref[:, pair * 128:(pair + 1) * 128] = o[r * CHUNK:(r + 1) * CHUNK]

    table = lambda which, width: pl.BlockSpec((CHUNK, width), lambda i: (which(i), 0))
    return pl.pallas_call(
        body, name="fwd_attn", grid=(nb,),
        in_specs=[pl.BlockSpec((CHUNK, D), lambda i: (i, OFF_Q // D)),
                  pl.BlockSpec((CHUNK, KV_W), lambda i: (prev(i), OFF_K // KV_W)),
                  pl.BlockSpec((CHUNK, KV_W), lambda i: (i, OFF_K // KV_W)),
                  pl.BlockSpec((CHUNK, KV_W), lambda i: (prev(i), OFF_VA // KV_W)),
                  pl.BlockSpec((CHUNK, KV_W), lambda i: (i, OFF_VA // KV_W)),
                  table(prev, 128), table(cur, 128), table(prev, 256), table(cur, 256),
                  pl.BlockSpec(memory_space=pltpu.SMEM)],
        out_specs=[pl.BlockSpec((CHUNK, D), lambda i: (i, 0)), pl.BlockSpec((CHUNK, D), lambda i: (i, 0)),
                   pl.BlockSpec((CHUNK, KV_W), lambda i: (i, 0)),
                   pl.BlockSpec((None, N_Q, CHUNK, CHUNK), lambda i: (i, 0, 0, 0)),
                   pl.BlockSpec((None, N_Q, CHUNK), lambda i: (i, 0, 0))],
        out_shape=[SDS((T, D), BF16), SDS((T, D), BF16), SDS((T, KV_W), BF16),
                   SDS((nb, N_Q, CHUNK, CHUNK), BF16), SDS((nb, N_Q, CHUNK), F32)],
        compiler_params=_params(1),
    )(proj, proj, proj, proj, proj, cos, cos, sin, sin, sinks)


def _row_halves(tm):
    return [slice(0, tm // 2), slice(tm // 2, tm)] if tm % 32 == 0 else [slice(0, tm)]


def _fwd_mix(a, att, proj, x, wa, wb, wo, g1, g2):
    T = x.shape[0]
    tm = min(T, 512)
    half = D // 2

    def body(a_ref, att_ref, ga0, ga1, gb0, gb1, x_ref, wa_ref, wb_ref, wo_ref, g1_ref, g2_ref,
             mg_ref, a2_ref, b2_ref, mix_ref, x1_ref, hf_ref):
        for rows in _row_halves(tm):
            a2 = _nn(a_ref[rows, :], wa_ref[...])
            b2 = _nn(att_ref[rows, :], wb_ref[...])
            ga = jnp.concatenate([ga0[rows, :], ga1[rows, :]], axis=1).astype(F32)
            gb = jnp.concatenate([gb0[rows, :], gb1[rows, :]], axis=1).astype(F32)
            merged = (_sigmoid(ga) * a2 + _sigmoid(gb) * b2).astype(BF16)
            a2_ref[rows, :] = a2.astype(BF16)
            b2_ref[rows, :] = b2.astype(BF16)
            mg_ref[rows, :] = merged
            mix = _nn(merged, wo_ref[...])
            mix_ref[rows, :] = mix
            _, mh = _rms_stats(mix)
            x1 = x_ref[rows, :] + mh * g1_ref[...]
            x1_ref[rows, :] = x1
            _, xh = _rms_stats(x1)
            hf_ref[rows, :] = (xh * g2_ref[...]).astype(BF16)

    row = lambda i: (i, 0)
    const = lambda i: (0, 0)
    gspec = lambda off: pl.BlockSpec((tm, half), lambda i: (i, off // half))
    return pl.pallas_call(
        body, name="fwd_mix", grid=(T // tm,),
        in_specs=[pl.BlockSpec((tm, D), row), pl.BlockSpec((tm, D), row),
                  gspec(OFF_GA), gspec(OFF_GA + half), gspec(OFF_GB), gspec(OFF_GB + half),
                  pl.BlockSpec((tm, D), row), _resident((D, D)), _resident((D, D)),
                  _resident((D, D)), pl.BlockSpec((1, D), const), pl.BlockSpec((1, D), const)],
        out_specs=[pl.BlockSpec((tm, D), row)] * 6,
        out_shape=[SDS((T, D), BF16), SDS((T, D), BF16), SDS((T, D), BF16), SDS((T, D), F32), SDS((T, D), F32),
                   SDS((T, D), BF16)],
        compiler_params=_params(1),
    )(a, att, proj, proj, proj, proj, x, wa, wb, wo, g1, g2)


FF_SPLIT = N_DEV
FF_TILE = D_FF // FF_SPLIT
FF_STEP = 2048
FF_SLABS = FF_STEP // FF_TILE
FF_STEPS = D_FF // FF_STEP


def _fwd_ff(hf, wfi3, wfo, x1, tgt, g3):
    T = hf.shape[0]
    tm = min(T, 512)
    last = FF_STEPS - 1

    def body(hf_ref, wfi_ref, wfo_ref, x1_ref, tgt_ref, g3_ref, f_ref, dy_ref, dff_ref, dg3_ref, loss_ref, acc, r_s):
        i, p = pl.program_id(0), pl.program_id(1)

        @pl.when((i == 0) & (p == 0))
        def _():
            dg3_ref[...] = jnp.zeros_like(dg3_ref)
            loss_ref[...] = jnp.zeros_like(loss_ref)

        hf_t = hf_ref[...]
        for s in range(FF_SLABS):
            cols = slice(s * FF_TILE, (s + 1) * FF_TILE)
            f = _nn(hf_t, wfi_ref[p * FF_SLABS + s]).astype(BF16)
            f_ref[:, cols] = f
            rl = jnp.maximum(f.astype(F32), 0.0)
            r_s[:, cols] = (rl * rl).astype(BF16)
        part = _nn(r_s[...], wfo_ref[pl.ds(pl.multiple_of(p * FF_STEP, FF_STEP), FF_STEP), :])

        @pl.when(p == 0)
        def _():
            acc[...] = part

        @pl.when(p > 0)
        def _():
            acc[...] += part

        @pl.when(p == last)
        def _():
            r3, fh = _rms_stats(acc[...])
            e = x1_ref[...] + fh * g3_ref[...] - tgt_ref[...]
            loss_ref[...] += jnp.sum(e * e) * (0.5 / D)
            dy = e * (1.0 / D)
            dy_ref[...] = dy
            dg3_ref[...] += _colsum(dy * fh)
            dff_ref[...] = _rms_bwd(dy, fh, r3, g3_ref[...]).astype(BF16)

    row = lambda i, p: (i, 0)
    const = lambda i, p: (0, 0)
    return pl.pallas_call(
        body, name="fwd_ff", grid=(T // tm, FF_STEPS),
        in_specs=[pl.BlockSpec((tm, D), row), _resident((FF_SPLIT, D, FF_TILE)), _resident((D_FF, D)),
                  pl.BlockSpec((tm, D), row),
                  pl.BlockSpec((tm, D), row), pl.BlockSpec((1, D), const)],
        out_specs=[pl.BlockSpec((tm, FF_STEP), lambda i, p: (i, p)), pl.BlockSpec((tm, D), row),
                   pl.BlockSpec((tm, D), row), pl.BlockSpec((1, D), const), pl.BlockSpec((1, 128), const)],
        out_shape=[SDS((T, D_FF), BF16), SDS((T, D), F32), SDS((T, D), BF16), SDS((1, D), F32), SDS((1, 128), F32)],
        scratch_shapes=[pltpu.VMEM((tm, D), F32), pltpu.VMEM((tm, FF_STEP), BF16)],
        compiler_params=_params(2),
    )(hf, wfi3, wfo, x1, tgt, g3)


def _bwd_ff(dff, f, wfi3, wfo, x1, dy, mix, g1, g2):
    T = dff.shape[0]
    tm = min(T, 512)
    last = FF_STEPS - 1

    def body(dff_ref, f_ref, wfi_ref, wfo_ref, x1_ref, dy_ref, mix_ref, g1_ref, g2_ref,
             df_ref, dx1_ref, dmix_ref, dg2_ref, dg1_ref, acc):
        i, p = pl.program_id(0), pl.program_id(1)

        @pl.when((i == 0) & (p == 0))
        def _():
            dg2_ref[...] = jnp.zeros_like(dg2_ref)
            dg1_ref[...] = jnp.zeros_like(dg1_ref)

        dr = _nt(dff_ref[...], wfo_ref[pl.ds(pl.multiple_of(p * FF_STEP, FF_STEP), FF_STEP), :])
        df_ref[...] = (dr * (2.0 * jnp.maximum(f_ref[...].astype(F32), 0.0))).astype(BF16)
        part = _nt(df_ref[:, :FF_TILE], wfi_ref[p * FF_SLABS])
        for s in range(1, FF_SLABS):
            part = part + _nt(df_ref[:, s * FF_TILE:(s + 1) * FF_TILE], wfi_ref[p * FF_SLABS + s])

        @pl.when(p == 0)
        def _():
            acc[...] = part

        @pl.when(p > 0)
        def _():
            acc[...] += part

        @pl.when(p == last)
        def _():
            dhf = acc[...]
            r2, xh = _rms_stats(x1_ref[...])
            dg2_ref[...] += _colsum(dhf * xh)
            dx1 = dy_ref[...] + _rms_bwd(dhf, xh, r2, g2_ref[...])
            dx1_ref[...] = dx1
            r1, mh = _rms_stats(mix_ref[...])
            dg1_ref[...] += _colsum(dx1 * mh)
            dmix_ref[...] = _rms_bwd(dx1, mh, r1, g1_ref[...]).astype(BF16)

    row = lambda i, p: (i, 0)
    const = lambda i, p: (0, 0)
    return pl.pallas_call(
        body, name="bwd_ff", grid=(T // tm, FF_STEPS),
        in_specs=[pl.BlockSpec((tm, D), row), pl.BlockSpec((tm, FF_STEP), lambda i, p: (i, p)),
                  _resident((FF_SPLIT, D, FF_TILE)), _resident((D_FF, D)),
                  pl.BlockSpec((tm, D), row), pl.BlockSpec((tm, D), row), pl.BlockSpec((tm, D), row),
                  pl.BlockSpec((1, D), const), pl.BlockSpec((1, D), const)],
        out_specs=[pl.BlockSpec((tm, FF_STEP), lambda i, p: (i, p)), pl.BlockSpec((tm, D), row),
                   pl.BlockSpec((tm, D), row), pl.BlockSpec((1, D), const), pl.BlockSpec((1, D), const)],
        out_shape=[SDS((T, D_FF), BF16), SDS((T, D), F32), SDS((T, D), BF16), SDS((1, D), F32), SDS((1, D), F32)],
        scratch_shapes=[pltpu.VMEM((tm, D), F32)],
        compiler_params=_params(2),
    )(dff, f, wfi3, wfo, x1, dy, mix, g1, g2)


def _wgrad_ff(hf, df, f, dff):
    T = hf.shape[0]
    tt = min(T, 1024)
    wide = 2 * FF_TILE

    def body(hf_ref, df_ref, f_ref, dff_ref, dwfi_ref, dwfo_ref, acc_i, acc_o):
        t = pl.program_id(1)

        @pl.when(t == 0)
        def _():
            acc_i[...] = jnp.zeros_like(acc_i)
            acc_o[...] = jnp.zeros_like(acc_o)

        acc_i[...] += _tn(hf_ref[...], df_ref[...])
        rl = jnp.maximum(f_ref[...].astype(F32), 0.0)
        acc_o[...] += _tn((rl * rl).astype(BF16), dff_ref[...])

        @pl.when(t == T // tt - 1)
        def _():
            dwfi_ref[0] = acc_i[:, :FF_TILE].astype(BF16)
            dwfi_ref[1] = acc_i[:, FF_TILE:].astype(BF16)
            dwfo_ref[...] = acc_o[...].astype(BF16)

    return pl.pallas_call(
        body, name="wgrad_ff", grid=(D_FF // wide, T // tt),
        in_specs=[pl.BlockSpec((tt, D), lambda p, t: (t, 0)), pl.BlockSpec((tt, wide), lambda p, t: (t, p)),
                  pl.BlockSpec((tt, wide), lambda p, t: (t, p)), pl.BlockSpec((tt, D), lambda p, t: (t, 0))],
        out_specs=[pl.BlockSpec((2, D, FF_TILE), lambda p, t: (p, 0, 0)), pl.BlockSpec((wide, D), lambda p, t: (p, 0))],
        out_shape=[SDS((FF_SPLIT, D, FF_TILE), BF16), SDS((D_FF, D), BF16)],
        scratch_shapes=[pltpu.VMEM((D, wide), F32), pltpu.VMEM((wide, D), F32)],
        compiler_params=_params(2),
    )(hf, df, f, dff)


def _bwd_mix(dmix, proj, a2, b2, wo, wa, wb, after=None):
    T = dmix.shape[0]
    tm = min(T, 512)
    half = D // 2

    def body(dmix_ref, ga0, ga1, gb0, gb1, a2_ref, b2_ref, wo_ref, wa_ref, wb_ref,
             da2_ref, db2_ref, dg_ref, da_ref, datt_ref):
        for rows in _row_halves(tm):
            dmg = _nt(dmix_ref[rows, :], wo_ref[...])
            sa = _sigmoid(jnp.concatenate([ga0[rows, :], ga1[rows, :]], axis=1).astype(F32))
            sb = _sigmoid(jnp.concatenate([gb0[rows, :], gb1[rows, :]], axis=1).astype(F32))
            da2 = (dmg * sa).astype(BF16)
            db2 = (dmg * sb).astype(BF16)
            da2_ref[rows, :] = da2
            db2_ref[rows, :] = db2
            dg_ref[rows, :D] = (dmg * a2_ref[rows, :].astype(F32) * (sa * (1.0 - sa))).astype(BF16)
            dg_ref[rows, D:] = (dmg * b2_ref[rows, :].astype(F32) * (sb * (1.0 - sb))).astype(BF16)
            da_ref[rows, :] = _nt(da2, wa_ref[...]).astype(BF16)
            datt_ref[rows, :] = _nt(db2, wb_ref[...]).astype(BF16)

    row = lambda i: (i, 0)
    const = lambda i: (0, 0)
    gspec = lambda off: pl.BlockSpec((tm, half), lambda i: (i, off // half))
    body, dep_specs, deps = _after(body, 10, after)
    return pl.pallas_call(
        body, name="bwd_mix", grid=(T // tm,),
        in_specs=[pl.BlockSpec((tm, D), row), gspec(OFF_GA), gspec(OFF_GA + half), gspec(OFF_GB), gspec(OFF_GB + half),
                  pl.BlockSpec((tm, D), row), pl.BlockSpec((tm, D), row),
                  _resident((D, D)), _resident((D, D)), _resident((D, D))] + dep_specs,
        out_specs=[pl.BlockSpec((tm, D), row), pl.BlockSpec((tm, D), row), pl.BlockSpec((tm, 2 * D), row),
                   pl.BlockSpec((tm, D), row), pl.BlockSpec((tm, D), row)],
        out_shape=[SDS((T, D), BF16), SDS((T, D), BF16), SDS((T, 2 * D), BF16), SDS((T, D), BF16), SDS((T, D), BF16)],
        compiler_params=_params(1),
    )(dmix, proj, proj, proj, proj, a2, b2, wo, wa, wb, *deps)


def _wgrad_mix(merged, dmix, a, da2, att, db2):
    T = merged.shape[0]
    tt = min(T, 512)

    def body(mg_ref, dmix_ref, a_ref, da2_ref, att_ref, db2_ref, dwo_ref, dwa_ref, dwb_ref, acc):
        t = pl.program_id(0)

        @pl.when(t == 0)
        def _():
            acc[...] = jnp.zeros_like(acc)

        acc[0] += _tn(mg_ref[...], dmix_ref[...])
        acc[1] += _tn(a_ref[...], da2_ref[...])
        acc[2] += _tn(att_ref[...], db2_ref[...])

        @pl.when(t == T // tt - 1)
        def _():
            dwo_ref[...] = acc[0].astype(BF16)
            dwa_ref[...] = acc[1].astype(BF16)
            dwb_ref[...] = acc[2].astype(BF16)

    return pl.pallas_call(
        body, name="wgrad_mix", grid=(T // tt,),
        in_specs=[pl.BlockSpec((tt, D), lambda t: (t, 0))] * 6,
        out_specs=[pl.BlockSpec((D, D), lambda t: (0, 0))] * 3,
        out_shape=[SDS((D, D), BF16)] * 3,
        scratch_shapes=[pltpu.VMEM((3, D, D), F32)],
        compiler_params=_params(1),
    )(merged, dmix, a, da2, att, db2)


def _bwd_attn(qr, kr, probs, psink, proj, cos, sin, datt, after=None):
    T = proj.shape[0]
    nb = T // CHUNK
    cur = lambda i: jnp.minimum(i, nb - 1)
    prev = lambda i: jnp.maximum(jnp.minimum(i, nb - 1) - 1, 0)

    def body(q_ref, kp_ref, kc_ref, vp_ref, vc_ref, cp_ref, cc_ref, sp_ref, sc_ref, p_ref, psink_ref, do_ref,
             dq_ref, dkv_ref, dsink_ref, carry_k, carry_v):
        i = pl.program_id(0)

        @pl.when(i == 0)
        def _():
            carry_k[...] = jnp.zeros_like(carry_k)
            carry_v[...] = jnp.zeros_like(carry_v)
            dsink_ref[...] = jnp.zeros_like(dsink_ref)

        @pl.when(i < nb)
        def _():
            prev_slot, _ = _fold_masks(i == 0)
            c_band, s_band = _band(cp_ref, cc_ref), _band(sp_ref, sc_ref)
            lane = lax.broadcasted_iota(jnp.int32, (1, 128), 1)
            dsink = jnp.zeros((1, 128), F32)
            for j in range(KV_W // 128):
                cols = slice(j * 128, (j + 1) * 128)
                k_slab = _band(kp_ref, kc_ref, cols).astype(F32)
                v_slab = _band(vp_ref, vc_ref, cols).astype(F32)
                dk_slab = jnp.zeros((2 * CHUNK, 128), F32)
                dv_slab = jnp.zeros((2 * CHUNK, 128), F32)
                for g in (2 * j, 2 * j + 1):
                    k2 = _head_pair_operand(k_slab, g)
                    v2 = _head_pair_operand(v_slab, g)
                    pairs = [g * PAIRS_PER_KV + r for r in range(PAIRS_PER_KV)]
                    q_stack = jnp.concatenate([q_ref[:, pr * 128:(pr + 1) * 128] for pr in pairs], axis=0)
                    do_stack = jnp.concatenate([do_ref[:, pr * 128:(pr + 1) * 128] for pr in pairs], axis=0)
                    dp2 = _nt(v2, do_stack)
                    pcols, dscols = [], []
                    for r, pair in enumerate(pairs):
                        ps, dss = [], []
                        for e in range(2):
                            head = 2 * pair + e
                            p_b = p_ref[head]
                            p = p_b.astype(F32)
                            dp = _fold(dp2[e * 2 * CHUNK:(e + 1) * 2 * CHUNK, r * 128:(r + 1) * 128], prev_slot)
                            delta = jnp.sum(p * dp, axis=0, keepdims=True)
                            ps.append(_unfold(p_b, prev_slot))
                            dss.append(_unfold((p * (dp - delta)).astype(BF16), prev_slot))
                            dsink = dsink + jnp.where(lane == head, -jnp.sum(psink_ref[head:head + 1, :] * delta), 0.0)
                        pcols.append(jnp.concatenate(ps, axis=0))
                        dscols.append(jnp.concatenate(dss, axis=0))
                    ds2 = jnp.concatenate(dscols, axis=1)
                    dq = _tn(ds2, k2) * (HEAD ** -0.5)
                    for r, pair in enumerate(pairs):
                        dq_ref[:, pair * 128:(pair + 1) * 128] = _rope_bwd(
                            dq[r * CHUNK:(r + 1) * CHUNK], cc_ref[...], sc_ref[...]).astype(BF16)
                    dk_slab = dk_slab + _head_pair_gradient(_nn(ds2, q_stack), g)
                    dv_slab = dv_slab + _head_pair_gradient(_nn(jnp.concatenate(pcols, axis=1), do_stack), g)
                dk_slab = _rope_bwd(dk_slab, c_band, s_band)
                vcols = slice(KV_W + j * 128, KV_W + (j + 1) * 128)
                dkv_ref[:, cols] = (carry_k[:, cols] + dk_slab[:CHUNK]).astype(BF16)
                dkv_ref[:, vcols] = (carry_v[:, cols] + dv_slab[:CHUNK]).astype(BF16)
                carry_k[:, cols] = dk_slab[CHUNK:]
                carry_v[:, cols] = dv_slab[CHUNK:]
            dsink_ref[...] += dsink

        @pl.when(i == nb)
        def _():
            dkv_ref[:, :KV_W] = carry_k[...].astype(BF16)
            dkv_ref[:, KV_W:] = carry_v[...].astype(BF16)

    table = lambda which, width: pl.BlockSpec((CHUNK, width), lambda i: (which(i), 0))
    body, dep_specs, deps = _after(body, 12, after)
    return pl.pallas_call(
        body, name="bwd_attn", grid=(nb + 1,),
        in_specs=[pl.BlockSpec((CHUNK, D), lambda i: (cur(i), 0)),
                  pl.BlockSpec((CHUNK, KV_W), lambda i: (prev(i), 0)),
                  pl.BlockSpec((CHUNK, KV_W), lambda i: (cur(i), 0)),
                  pl.BlockSpec((CHUNK, KV_W), lambda i: (prev(i), OFF_VA // KV_W)),
                  pl.BlockSpec((CHUNK, KV_W), lambda i: (cur(i), OFF_VA // KV_W)),
                  table(prev, 128), table(cur, 128), table(prev, 256), table(cur, 256),
                  pl.BlockSpec((None, N_Q, CHUNK, CHUNK), lambda i: (cur(i), 0, 0, 0)),
                  pl.BlockSpec((None, N_Q, CHUNK), lambda i: (cur(i), 0, 0)),
                  pl.BlockSpec((CHUNK, D), lambda i: (cur(i), 0))] + dep_specs,
        out_specs=[pl.BlockSpec((CHUNK, D), lambda i: (cur(i), 0)),
                   pl.BlockSpec((CHUNK, 2 * KV_W), lambda i: (jnp.maximum(i - 1, 0), 0)),
                   pl.BlockSpec((1, 128), lambda i: (0, 0))],
        out_shape=[SDS((T, D), BF16), SDS((T, 2 * KV_W), BF16), SDS((1, 128), F32)],
        scratch_shapes=[pltpu.VMEM((CHUNK, KV_W), F32), pltpu.VMEM((CHUNK, KV_W), F32)],
        compiler_params=_params(1),
    )(qr, kr, kr, proj, proj, cos, cos, sin, sin, probs, psink, datt, *deps)


def _bwd_sgu(proj, da, lng, lnb, ws, bst):
    T = proj.shape[0]
    tc = min(T, 512)
    nsteps = T // tc

    def body(u_ref, vs_ref, da_ref, lng_ref, lnb_ref, ws_ref, bst_ref,
             duv_ref, dws_ref, dbs_ref, dlng_ref, dlnb_ref, dvn_s, dgu_s, dmx_sum):
        i = pl.program_id(0)

        @pl.when(i == 0)
        def _():
            dws_ref[...] = jnp.zeros_like(dws_ref)
            dlng_ref[...] = jnp.zeros_like(dlng_ref)
            dlnb_ref[...] = jnp.zeros_like(dlnb_ref)
            dmx_sum[...] = jnp.zeros_like(dmx_sum)

        u, vs, gu, tu, tv, rstd, vhat, vn = _sgu_forward_parts(u_ref, vs_ref, lng_ref, lnb_ref)
        da = da_ref[...].astype(F32)
        for g in range(GROUPS):
            wm = _masked_ws(ws_ref, g)
            cols = slice(g * CHUNK, (g + 1) * CHUNK)
            dws = jnp.zeros((CHUNK, CHUNK), F32)
            dsum = jnp.zeros((CHUNK, CHUNK), F32)
            for c in range(tc // CHUNK):
                rows = slice(c * CHUNK, (c + 1) * CHUNK)
                vn_cg = vn[rows, cols]
                mixed = _nn(wm, vn_cg) + bst_ref[:, g:g + 1]
                dgu_s[rows, cols] = da[rows, cols] * mixed
                dmx = da[rows, cols] * gu[rows, cols]
                dmxb = dmx.astype(BF16)
                dws = dws + _nt(dmxb, vn_cg)
                dsum = dsum + dmx
                dvn_s[rows, cols] = _tn(wm, dmxb)
            dws_ref[g] += dws
            dmx_sum[:, cols] += dsum
        dvn = dvn_s[...]
        dlng_ref[...] += _colsum(dvn * vhat)
        dlnb_ref[...] += _colsum(dvn)
        dvh = dvn * lng_ref[...]
        dgv = rstd * (dvh - jnp.mean(dvh, axis=-1, keepdims=True) - vhat * jnp.mean(dvh * vhat, axis=-1, keepdims=True))
        duv_ref[:, :D] = (dgu_s[...] * _gelu_grad(u, tu)).astype(BF16)
        duv_ref[:, D:] = (dgv * _gelu_grad(vs, tv)).astype(BF16)

        @pl.when(i == nsteps - 1)
        def _():
            row = lax.broadcasted_iota(jnp.int32, (CHUNK, CHUNK), 0)
            col = lax.broadcasted_iota(jnp.int32, (CHUNK, CHUNK), 1)
            for g in range(GROUPS):
                dws_ref[g] = jnp.where(row >= col, dws_ref[g], 0.0)
                dbs_ref[g:g + 1, :] = _colsum(dmx_sum[:, g * CHUNK:(g + 1) * CHUNK].T)

    const2 = lambda i: (0, 0)
    return pl.pallas_call(
        body, name="bwd_sgu", grid=(nsteps,),
        in_specs=[pl.BlockSpec((tc, D), lambda i: (i, 0)), pl.BlockSpec((tc, D), lambda i: (i, 1)),
                  pl.BlockSpec((tc, D), lambda i: (i, 0)), pl.BlockSpec((1, D), const2), pl.BlockSpec((1, D), const2),
                  pl.BlockSpec((GROUPS, CHUNK, CHUNK), lambda i: (0, 0, 0)), pl.BlockSpec((CHUNK, GROUPS), const2)],
        out_specs=[pl.BlockSpec((tc, 2 * D), lambda i: (i, 0)), pl.BlockSpec((GROUPS, CHUNK, CHUNK), lambda i: (0, 0, 0)),
                   pl.BlockSpec((GROUPS, CHUNK), const2), pl.BlockSpec((1, D), const2), pl.BlockSpec((1, D), const2)],
        out_shape=[SDS((T, 2 * D), BF16), SDS((GROUPS, CHUNK, CHUNK), F32), SDS((GROUPS, CHUNK), F32),
                   SDS((1, D), F32), SDS((1, D), F32)],
        scratch_shapes=[pltpu.VMEM((tc, D), F32), pltpu.VMEM((tc, D), F32), pltpu.VMEM((CHUNK, D), F32)],
        compiler_params=_params(1),
    )(proj, proj, da, lng, lnb, ws, bst)


IN_SEG_WIDTHS = (2 * D, D, 2 * N_KV * HEAD, 2 * D)


def _resident(shape):
    return pl.BlockSpec(shape, lambda *_: (0,) * len(shape), pipeline_mode=pl.Buffered(1))


def _bwd_in(duv, dq, dkv, dg, win_t, x, dx1, g0, after=None):
    T = x.shape[0]
    tm = min(T, 512)

    def body(duv_ref, dq_ref, dkv_ref, dg_ref, w_ref, x_ref, dx1_ref, g0_ref, gx_ref, dg0_ref):
        @pl.when(pl.program_id(0) == 0)
        def _():
            dg0_ref[...] = jnp.zeros_like(dg0_ref)

        dh, off = None, 0
        for ref, width in zip((duv_ref, dq_ref, dkv_ref, dg_ref), IN_SEG_WIDTHS):
            part = _nn(ref[...], w_ref[off:off + width, :])
            dh = part if dh is None else dh + part
            off += width
        r0, xh = _rms_stats(x_ref[...])
        dg0_ref[...] += _colsum(dh * xh)
        gx_ref[...] = dx1_ref[...] + _rms_bwd(dh, xh, r0, g0_ref[...])

    row = lambda i: (i, 0)
    body, dep_specs, deps = _after(body, 8, after)
    return pl.pallas_call(
        body, name="bwd_in", grid=(T // tm,),
        in_specs=[pl.BlockSpec((tm, w), row) for w in IN_SEG_WIDTHS] + [
            _resident((IN_W, D)), pl.BlockSpec((tm, D), row), pl.BlockSpec((tm, D), row),
            pl.BlockSpec((1, D), lambda i: (0, 0))] + dep_specs,
        out_specs=[pl.BlockSpec((tm, D), row), pl.BlockSpec((1, D), lambda i: (0, 0))],
        out_shape=[SDS((T, D), F32), SDS((1, D), F32)],
        compiler_params=_params(1),
    )(duv, dq, dkv, dg, win_t, x, dx1, g0, *deps)


def _wgrad_rows(h, segs, first_row, into, name):
    T = h.shape[0]
    tt = min(T, 1024)
    widths = [s.shape[1] for s in segs]
    rows = sum(widths)
    n_in = 1 + len(segs) + (into is not None)

    def body(*refs):
        h_ref, seg_refs = refs[0], refs[1:1 + len(segs)]
        dw_ref, acc, stage, sem = refs[n_in], refs[n_in + 1], refs[n_in + 2], refs[n_in + 3]
        t = pl.program_id(0)

        @pl.when(t == 0)
        def _():
            acc[...] = jnp.zeros_like(acc)

        off = 0
        for ref, width in zip(seg_refs, widths):
            acc[off:off + width, :] += _tn(ref[...], h_ref[...])
            off += width

        @pl.when(t == T // tt - 1)
        def _():
            stage[...] = acc[...].astype(BF16)
            out = pltpu.make_async_copy(stage, dw_ref.at[pl.ds(first_row, rows)], sem)
            out.start()
            out.wait()

    row = lambda t: (t, 0)
    return pl.pallas_call(
        body, name=name, grid=(T // tt,),
        in_specs=[pl.BlockSpec((tt, D), row)] + [pl.BlockSpec((tt, w), row) for w in widths] + [_ANY] * (into is not None),
        out_specs=_ANY,
        out_shape=SDS((IN_W, D), BF16),
        input_output_aliases={} if into is None else {n_in - 1: 0},
        scratch_shapes=[pltpu.VMEM((rows, D), F32), pltpu.VMEM((rows, D), BF16), pltpu.SemaphoreType.DMA],
        compiler_params=_params(1),
    )(h, *segs, *([] if into is None else [into]))


def _wgrad_in(h, duv, dq, dkv, dg):
    dw = _wgrad_rows(h, [dg], IN_SEG_WIDTHS[0] + IN_SEG_WIDTHS[1] + IN_SEG_WIDTHS[2], None, "wgrad_in_gates")
    dw = _wgrad_rows(h, [duv], 0, dw, "wgrad_in_uv")
    return _wgrad_rows(h, [dq, dkv], IN_SEG_WIDTHS[0], dw, "wgrad_in_qkv")


def _place():
    x, y, c = lax.axis_index("x"), lax.axis_index("y"), lax.axis_index("c")
    return x, y, c, 4 * x + 2 * y + c


def _peers(x, y, c):
    out = []
    for mask in range(1, N_DEV):
        px = 1 - x if mask & 4 else x
        py = 1 - y if mask & 2 else y
        pc = 1 - c if mask & 1 else c
        out.append(((px, py, pc), 4 * px + 2 * py + pc))
    return out


def _all_to_all(arrays, gather, name, after=None):
    n = len(arrays)

    def body(*refs):
        ins, outs = refs[:n], refs[n:2 * n]
        send_sems, recv_sems, local_sems = refs[2 * n:]
        x, y, c, me = _place()
        local, sends, recvs = [], [], []
        for a in range(n):
            src_own = ins[a] if gather[a] else ins[a].at[me]
            local.append(pltpu.make_async_copy(src_own, outs[a].at[me], local_sems.at[a]))
            for k, (peer, pid) in enumerate(_peers(x, y, c)):
                sem = a * (N_DEV - 1) + k
                src = ins[a] if gather[a] else ins[a].at[pid]
                sends.append(pltpu.make_async_remote_copy(
                    src_ref=src, dst_ref=outs[a].at[me], send_sem=send_sems.at[sem], recv_sem=recv_sems.at[sem],
                    device_id=peer, device_id_type=MESH))
                recvs.append(pltpu.make_async_remote_copy(
                    src_ref=src, dst_ref=outs[a].at[pid], send_sem=send_sems.at[sem], recv_sem=recv_sems.at[sem],
                    device_id=peer, device_id_type=MESH))
        for cp in local + sends:
            cp.start()
        for cp in recvs:
            cp.wait_recv()
        for cp in sends:
            cp.wait_send()
        for cp in local:
            cp.wait()

    out_shape = [SDS((N_DEV,) + a.shape if gt else a.shape, a.dtype) for a, gt in zip(arrays, gather)]
    nsem = n * (N_DEV - 1)
    body, dep_specs, deps = _after(body, n, after)
    return pl.pallas_call(
        body, name=name,
        in_specs=[pl.BlockSpec(memory_space=pl.ANY)] * n + dep_specs,
        out_specs=[pl.BlockSpec(memory_space=pl.ANY)] * n,
        out_shape=out_shape,
        scratch_shapes=[pltpu.SemaphoreType.DMA((nsem,)), pltpu.SemaphoreType.DMA((nsem,)), pltpu.SemaphoreType.DMA((n,))],
    )(*arrays, *deps)


_HBM = pl.BlockSpec(memory_space=pltpu.HBM)
_SEM = pl.BlockSpec(memory_space=pltpu.SEMAPHORE)
_EFFECT = pltpu.SideEffectType.DATAFLOW_SIDE_EFFECTING
GATHER = "gather"
SCATTER = "scatter"
SPREAD = "spread"


def _zone_shape(a, mode):
    if mode == GATHER:
        return (N_DEV,) + a.shape
    return (N_DEV - 1,) + (a.shape[1:] if mode == SCATTER else a.shape)


def _start_copies(arrays, modes, name, after=None):
    n = len(arrays)
    zones = [lax.empty(_zone_shape(a, m), a.dtype) for a, m in zip(arrays, modes)]

    def body(*refs):
        ins, lands = refs[:n], refs[n:2 * n]
        send_sems, recv_sems = refs[-2 * n - 3], refs[-2 * n - 2]
        token = refs[-1]
        x, y, c, me = _place()
        for a in range(n):
            for k, (peer, pid) in enumerate(_peers(x, y, c)):
                src = ins[a].at[pid] if modes[a] == SCATTER else ins[a]
                dst = lands[a].at[me] if modes[a] == GATHER else lands[a].at[k]
                pltpu.make_async_remote_copy(src_ref=src, dst_ref=dst, send_sem=send_sems.at[a], recv_sem=recv_sems.at[a],
                                             device_id=peer, device_id_type=MESH).start()
        token[...] = jnp.zeros_like(token)

    hbm = lambda a: pltpu.HBM(a.shape, a.dtype)
    sems = pltpu.SemaphoreType.DMA((n,))
    extra = [] if after is None else [after]
    operands = [pltpu.with_memory_space_constraint(a, pltpu.HBM) for a in list(arrays) + zones]
    res = pl.pallas_call(
        body, name=name,
        out_shape=(sems, sems, *[hbm(a) for a in arrays], *[hbm(z) for z in zones], SDS((8, 128), F32)),
        in_specs=[_HBM] * (2 * n) + [_ANY] * len(extra),
        out_specs=(_SEM, _SEM, *[_HBM] * (2 * n), pl.BlockSpec(memory_space=pltpu.VMEM)),
        input_output_aliases={i: 2 + i for i in range(2 * n)},
        compiler_params=pltpu.CompilerParams(has_side_effects=_EFFECT),
    )(*operands, *extra)
    return res[0], res[1], list(res[2:2 + n]), list(res[2 + n:2 + 2 * n]), res[-1]


def _wait_copies(started, after, name, count=N_DEV - 1):
    send_sems, recv_sems, thru, zones, _ = started
    nt, nz = len(thru), len(zones)

    def body(*refs):
        lands = refs[nt:nt + nz]
        send_ref, recv_ref = refs[nt + nz], refs[nt + nz + 1]
        x, y, c, _ = _place()
        for a in range(nz):
            blocks = lands[a].at[pl.ds(0, count)]
            cp = pltpu.make_async_remote_copy(src_ref=blocks, dst_ref=blocks, send_sem=send_ref.at[a], recv_sem=recv_ref.at[a],
                                              device_id=(x, y, 1 - c), device_id_type=MESH)
            cp.wait_send()
            cp.wait_recv()

    hbm = lambda a: pltpu.HBM(a.shape, a.dtype)
    res = pl.pallas_call(
        body, name=name,
        out_shape=tuple(hbm(a) for a in thru + zones),
        in_specs=[_HBM] * (nt + nz) + [_SEM, _SEM, _ANY],
        out_specs=tuple([_HBM] * (nt + nz)),
        input_output_aliases={i: i for i in range(nt + nz)},
        compiler_params=pltpu.CompilerParams(has_side_effects=_EFFECT),
    )(*thru, *zones, send_sems, recv_sems, after)
    return list(res[:nt]), list(res[nt:])


def _split_start(body, arrays, zones, name, after):
    n = len(arrays) + len(zones)
    hbm = lambda a: pltpu.HBM(a.shape, a.dtype)
    sems = pltpu.SemaphoreType.DMA((max(len(zones), 1),))
    extra = [] if after is None else [after]
    operands = [pltpu.with_memory_space_constraint(a, pltpu.HBM) for a in list(arrays) + list(zones)]
    res = pl.pallas_call(
        body, name=name,
        out_shape=(sems, sems, *[hbm(a) for a in operands], SDS((8, 128), F32)),
        in_specs=[_HBM] * n + [_ANY] * len(extra),
        out_specs=(_SEM, _SEM, *[_HBM] * n, pl.BlockSpec(memory_space=pltpu.VMEM)),
        input_output_aliases={i: 2 + i for i in range(n)},
        compiler_params=pltpu.CompilerParams(has_side_effects=_EFFECT),
    )(*operands, *extra)
    return res[0], res[1], list(res[2:2 + len(arrays)]), list(res[2 + len(arrays):2 + n]), res[-1]


def _gather_first_leg(shard, name, after=None):
    zone = lax.empty((N_DEV,) + shard.shape, shard.dtype)
    extra = 0 if after is None else 1

    def body(*refs):
        src, land = refs[0], refs[1]
        send_sem, recv_sem, token = refs[2 + extra], refs[3 + extra], refs[-1]
        x, y, c, me = _place()
        for peer in ((x, y, 1 - c), (1 - x, y, c), (x, 1 - y, c), (1 - x, 1 - y, c)):
            pltpu.make_async_remote_copy(src_ref=src, dst_ref=land.at[me], send_sem=send_sem.at[0], recv_sem=recv_sem.at[0],
                                         device_id=peer, device_id_type=MESH).start()
        token[...] = jnp.zeros_like(token)

    return _split_start(body, [shard], [zone], name, after)


def _gather_second_leg(zone, name, after=None):
    extra = 0 if after is None else 1

    def body(*refs):
        land = refs[0]
        send_sem, recv_sem, token = refs[1 + extra], refs[2 + extra], refs[-1]
        x, y, c, _ = _place()
        for px, py in ((1 - x, y), (x, 1 - y), (1 - x, 1 - y)):
            slot = 4 * px + 2 * py + c
            pltpu.make_async_remote_copy(src_ref=land.at[slot], dst_ref=land.at[slot], send_sem=send_sem.at[0],
                                         recv_sem=recv_sem.at[0], device_id=(x, y, 1 - c), device_id_type=MESH).start()
        token[...] = jnp.zeros_like(token)

    return _split_start(body, [], [zone], name, after)


UPDATE_BLOCK_ELEMS = 256 * 1024


def _update_rows(R, C):
    fits = [t for t in range(8, R + 1, 8) if R % t == 0 and t * C <= UPDATE_BLOCK_ELEMS]
    whole = [t for t in fits if t % 16 == 0]
    return max(whole or fits)


def _adamw_math(g, w, m, v):
    m2 = ADAM_B1 * m + (1.0 - ADAM_B1) * g
    v2 = ADAM_B2 * v + (1.0 - ADAM_B2) * (g * g)
    m_hat = m2 / (1.0 - ADAM_B1 ** ADAM_STEP)
    v_hat = v2 / (1.0 - ADAM_B2 ** ADAM_STEP)
    delta = -ADAM_LR * (m_hat / (jnp.sqrt(v_hat) + ADAM_EPS) + ADAM_WD * w)
    return delta, m2, v2


def _sum_adamw(parts, w, m, v, name):
    R, C = w.shape
    tr = _update_rows(R, C)

    def body(p_ref, w_ref, m_ref, v_ref, g_ref, d_ref, m2_ref, v2_ref):
        g = p_ref[0]
        for k in range(1, N_DEV):
            g = g + p_ref[k]
        g_ref[...] = g
        d_ref[...], m2_ref[...], v2_ref[...] = _adamw_math(g, w_ref[...], m_ref[...], v_ref[...])

    blk = pl.BlockSpec((tr, C), lambda i: (i, 0))
    return pl.pallas_call(
        body, name=name, grid=(R // tr,),
        in_specs=[pl.BlockSpec((N_DEV, tr, C), lambda i: (0, i, 0)), blk, blk, blk],
        out_specs=[blk] * 4,
        out_shape=[SDS((R, C), F32)] * 4,
        compiler_params=_params(1),
    )(parts, w, m, v)


def _sum_adamw_peers(me, own, parts, w, m, v, name, replicated):
    R, C = w.shape
    tr = _update_rows(R, C)

    def body(me_ref, own_ref, p_ref, w_ref, m_ref, v_ref, g_ref, d_ref, m2_ref, v2_ref):
        if replicated:
            mine = me_ref[0]
            g = None
            for j in range(N_DEV):
                k = jnp.maximum(jnp.bitwise_xor(mine, j) - 1, 0)
                term = jnp.where(mine == j, own_ref[...], p_ref[k])
                g = term if g is None else g + term
        else:
            g = own_ref[...].astype(F32)
            for k in range(N_DEV - 1):
                g = g + p_ref[k].astype(F32)
        g_ref[...] = g
        d_ref[...], m2_ref[...], v2_ref[...] = _adamw_math(g, w_ref[...], m_ref[...], v_ref[...])

    blk = pl.BlockSpec((tr, C), lambda i, me_ref: (i, 0))
    own_spec = blk if replicated else pl.BlockSpec((None, tr, C), lambda i, me_ref: (me_ref[0], i, 0))
    return pl.pallas_call(
        body, name=name,
        grid_spec=pltpu.PrefetchScalarGridSpec(
            num_scalar_prefetch=1, grid=(R // tr,),
            in_specs=[own_spec, pl.BlockSpec((N_DEV - 1, tr, C), lambda i, me_ref: (0, i, 0)), blk, blk, blk],
            out_specs=[blk] * 4),
        out_shape=[SDS((R, C), F32)] * 4,
        compiler_params=_params(1),
    )(me, own, parts, w, m, v)


SMALL = ("ln_v_gain", "ln_v_bias", "w_spatial", "b_spatial", "sinks", "norm_mix_post", "norm_ff_pre", "norm_ff_post")
SMALL_ROWS = {"ln_v_gain": 8, "ln_v_bias": 8, "w_spatial": 1024, "b_spatial": 8, "sinks": 8,
              "norm_mix_post": 8, "norm_ff_pre": 8, "norm_ff_post": 8}
SMALL_PACK_ROWS = 1152


def _pack_small(vals):
    rows = []
    for name in SMALL:
        flat = vals[name].reshape(-1)
        pad = SMALL_ROWS[name] * 128 - flat.shape[0]
        if pad:
            flat = jnp.concatenate([flat, jnp.zeros((pad,), F32)])
        rows.append(flat.reshape(SMALL_ROWS[name], 128))
    rows.append(jnp.zeros((SMALL_PACK_ROWS - sum(SMALL_ROWS.values()), 128), F32))
    return jnp.concatenate(rows, axis=0)


def _unpack_small(packed, shapes):
    out, r = {}, 0
    for name in SMALL:
        n = 1
        for s in shapes[name]:
            n *= s
        out[name] = packed[r:r + SMALL_ROWS[name]].reshape(-1)[:n].reshape(shapes[name])
        r += SMALL_ROWS[name]
    return out


def _rope_rows():
    d = jnp.arange(128) % HEAD
    inv = ROPE_THETA ** (-(2.0 * (d % (ROPE // 2))).astype(F32) / ROPE)
    invf = jnp.where(d < ROPE, inv, 0.0).astype(F32).reshape(1, 128)
    sgn = jnp.where(d < ROPE // 2, -1.0, jnp.where(d < ROPE, 1.0, 0.0)).astype(F32).reshape(1, 128)
    return invf, sgn


def kernel(x, positions, w_in, ln_v_gain, ln_v_bias, w_spatial, b_spatial, sinks, w_a, w_b, w_o, norm_mix_pre, norm_mix_post, w_ff_in, w_ff_out, norm_ff_pre, norm_ff_post, loss_target, m_w_in, m_ln_v_gain, m_ln_v_bias, m_w_spatial, m_b_spatial, m_sinks, m_w_a, m_w_b, m_w_o, m_norm_mix_pre, m_norm_mix_post, m_w_ff_in, m_w_ff_out, m_norm_ff_pre, m_norm_ff_post, v_w_in, v_ln_v_gain, v_ln_v_bias, v_w_spatial, v_b_spatial, v_sinks, v_w_a, v_w_b, v_w_o, v_norm_mix_pre, v_norm_mix_post, v_w_ff_in, v_w_ff_out, v_norm_ff_pre, v_norm_ff_post):
    given = dict(locals())
    T = x.shape[1]
    xt = x[0]
    tgt = loss_target[0]
    bst = b_spatial[0].T
    ws = w_spatial[0]

    me = 4 * lax.axis_index("x") + 2 * lax.axis_index("y") + lax.axis_index("c")
    me_arr = me.astype(jnp.int32).reshape(1)

    def with_own(zone, shard):
        return lax.dynamic_update_slice(zone, shard[None], (me,) + (0,) * shard.ndim)

    rest = ("w_a", "w_b", "w_o", "w_ff_in", "w_ff_out")
    shard = {n: given[n][0].astype(BF16) for n in rest}
    g_one = _gather_first_leg(w_in[0].T.astype(BF16), "gather_in_start")
    cos, sin = _rope_tables(positions.astype(F32).reshape(T, 1), *_rope_rows(), after=g_one[-1])
    h = _rms_pre(xt, norm_mix_pre, after=cos)
    (own_win,), (win8,) = _wait_copies(g_one, h, "gather_in_wait", count=4)
    g_two = _gather_second_leg(win8, "gather_in_pass_start")
    g_rest = _start_copies([shard[n] for n in rest], [GATHER] * len(rest), "gather_rest_start", after=g_two[-1])
    _, (win8,) = _wait_copies(g_two, g_rest[-1], "gather_in_pass_wait", count=3)
    win = with_own(win8, own_win).reshape(IN_W, D)

    proj = _fwd_in(h, win)
    att, qr, kr, probs, psink = _fwd_attn(proj, cos, sin, sinks[0])
    a = _fwd_sgu(proj, ln_v_gain, ln_v_bias, ws, bst, after=att)
    gw = {n: with_own(z, own) for n, own, z in zip(rest, *_wait_copies(g_rest, a, "gather_rest_wait"))}
    wa, wb, wo = (gw[n].reshape(D, D) for n in ("w_a", "w_b", "w_o"))
    wfi3 = gw["w_ff_in"]
    wfo = gw["w_ff_out"].reshape(D_FF, D)
    merged, a2, b2, mix, x1, hf = _fwd_mix(a, att, proj, xt, wa, wb, wo, norm_mix_post, norm_ff_pre)
    f, dy, dff, dg3, loss_part = _fwd_ff(hf, wfi3, wfo, x1, tgt, norm_ff_post)

    df, dx1, dmix, dg2, dg1 = _bwd_ff(dff, f, wfi3, wfo, x1, dy, mix, norm_mix_post, norm_ff_pre)
    dwfi3, dwfo = _wgrad_ff(hf, df, f, dff)
    own_ff = [dwfi3, dwfo.reshape(N_DEV, D_FF // N_DEV, D)]
    x_ff = _start_copies(own_ff, [SCATTER] * 2, "exchange_ff_start")
    da2, db2, dgate, da, datt = _bwd_mix(dmix, proj, a2, b2, wo, wa, wb, after=x_ff[-1])
    dwo, dwa, dwb = _wgrad_mix(merged, dmix, a, da2, att, db2)
    own_mix = [g.reshape(N_DEV, D // N_DEV, D) for g in (dwa, dwb, dwo)]
    x_mix = _start_copies(own_mix, [SCATTER] * 3, "exchange_mix_start")
    dq, dkv, dsink = _bwd_attn(qr, kr, probs, psink, proj, cos, sin, datt, after=x_mix[-1])
    duv, dws, dbs, dlng, dlnb = _bwd_sgu(proj, da, ln_v_gain, ln_v_bias, ws, bst)
    small_grads = {"ln_v_gain": dlng, "ln_v_bias": dlnb, "w_spatial": dws, "b_spatial": dbs, "sinks": dsink[:, :N_Q],
                   "norm_mix_post": dg1, "norm_ff_pre": dg2, "norm_ff_post": dg3}
    x_small = _start_copies([_pack_small(small_grads)], [SPREAD], "exchange_small_start")
    dwin = _wgrad_in(h, duv, dq, dkv, dgate)
    own_in = [dwin.reshape(N_DEV, IN_W // N_DEV, D)]
    x_in = _start_copies(own_in, [SCATTER], "exchange_in_start", after=x_small[-1])
    grad_x, dg0 = _bwd_in(duv, dq, dkv, dgate, win, xt, dx1, norm_mix_pre, after=x_in[-1])

    results = {}

    def update(n, own, parts, transposed=False):
        state = [given[k + n][0].T if transposed else given[k + n][0] for k in ("", "m_", "v_")]
        res = _sum_adamw_peers(me_arr, own, parts, *state, "adamw_" + n, False)
        results[n] = [(r.T if transposed else r).reshape(given[n].shape) for r in res]

    own_ff, p_ff = _wait_copies(x_ff, grad_x, "exchange_ff_wait")
    update("w_ff_in", own_ff[0], p_ff[0])
    update("w_ff_out", own_ff[1], p_ff[1])
    own_mix, p_mix = _wait_copies(x_mix, results["w_ff_out"][0], "exchange_mix_wait")
    for n, own, parts in zip(("w_a", "w_b", "w_o"), own_mix, p_mix):
        update(n, own, parts)
    tail = jnp.concatenate([dg0.reshape(8, 128), jnp.tile(loss_part, (8, 1))], axis=0)
    (tail_all,) = _all_to_all([tail], [True], "exchange_tail", after=results["w_o"][0])
    dg0_all = tail_all[:, :8]
    own_small, p_small = _wait_copies(x_small, tail_all, "exchange_small_wait")
    own_in, p_in = _wait_copies(x_in, p_small[0], "exchange_in_wait")
    update("w_in", own_in[0], p_in[0], transposed=True)
    packed = _sum_adamw_peers(me_arr, own_small[0], p_small[0], _pack_small({n: given[n] for n in SMALL}),
                              _pack_small({n: given["m_" + n] for n in SMALL}),
                              _pack_small({n: given["v_" + n] for n in SMALL}), "adamw_small", True)
    shapes = {n: given[n].shape for n in SMALL}
    unpacked = [_unpack_small(p, shapes) for p in packed]
    for n in SMALL:
        results[n] = [u[n] for u in unpacked]
    n = "norm_mix_pre"
    results[n] = [r.reshape(given[n].shape) for r in _sum_adamw(
        dg0_all, given[n].reshape(8, 128), given["m_" + n].reshape(8, 128), given["v_" + n].reshape(8, 128), "adamw_" + n)]

    loss = jnp.sum(tail_all[:, 8, 0])
    order = ("w_in", "ln_v_gain", "ln_v_bias", "w_spatial", "b_spatial", "sinks", "w_a", "w_b", "w_o", "norm_mix_pre",
             "norm_mix_post", "w_ff_in", "w_ff_out", "norm_ff_pre", "norm_ff_post")
    out = [loss, grad_x.reshape(x.shape)]
    for k in range(4):
        out += [results[n][k] for n in order]
    return tuple(out)
```

```python
import jax
import jax.numpy as jnp
from jax import lax
from jax.experimental import pallas as pl
from jax.experimental.pallas import tpu as pltpu

F32 = jnp.float32
BF16 = jnp.bfloat16

N_DEV = 8
D = 1024
D_FF = 4096
IN_W = 5632
CHUNK = 128
GROUPS = 8
HEAD = 64
N_Q = 16
N_KV = 4
ROPE = 16
ROPE_THETA = 500000.0
EPS = 1e-6
OFF_Q, OFF_K, OFF_VA, OFF_GA, OFF_GB = 2048, 3072, 3328, 3584, 4608

ADAM_LR = 0.001
ADAM_B1 = 0.9
ADAM_B2 = 0.999
ADAM_EPS = 1e-08
ADAM_WD = 0.01
ADAM_STEP = 10

VMEM_LIMIT = 56 * 1024 * 1024

SDS = jax.ShapeDtypeStruct
MESH = pl.DeviceIdType.MESH


def _params(n_axes):
    return pltpu.CompilerParams(dimension_semantics=("arbitrary",) * n_axes, vmem_limit_bytes=VMEM_LIMIT)


def _nt(a, b):
    return lax.dot_general(a, b, (((1,), (1,)), ((), ())), preferred_element_type=F32)


def _tn(a, b):
    return lax.dot_general(a, b, (((0,), (0,)), ((), ())), preferred_element_type=F32)


def _nn(a, b):
    return jnp.dot(a, b, preferred_element_type=F32)


def _gelu(x):
    t = jnp.tanh(0.7978845608028654 * (x + 0.044715 * (x * x * x)))
    return 0.5 * x * (1.0 + t), t


def _gelu_grad(x, t):
    return 0.5 * (1.0 + t) + 0.5 * x * (1.0 - t * t) * (0.7978845608028654 * (1.0 + 3.0 * 0.044715 * x * x))


def _sigmoid(x):
    return 1.0 / (1.0 + jnp.exp(-x))


def _rms_stats(v):
    r = lax.rsqrt(jnp.mean(v * v, axis=-1, keepdims=True) + EPS)
    return r, v * r


def _rms_bwd(d, vhat, r, g):
    gd = g * d
    return r * (gd - vhat * jnp.mean(gd * vhat, axis=-1, keepdims=True))


def _colsum(v):
    return jnp.sum(v, axis=0, keepdims=True)


_ANY = pl.BlockSpec(memory_space=pl.ANY)


def _after(body, n_in, after):
    if after is None:
        return body, [], []

    def ordered(*refs):
        return body(*refs[:n_in], *refs[n_in + 1:])

    return ordered, [_ANY], [after]


def _rms_pre(x, g0, after=None):
    T = x.shape[0]
    tm = min(T, 1024)

    def body(x_ref, g_ref, h_ref):
        _, xh = _rms_stats(x_ref[...])
        h_ref[...] = (xh * g_ref[...]).astype(BF16)

    body, dep_specs, deps = _after(body, 2, after)
    return pl.pallas_call(
        body, name="rms_pre", grid=(T // tm,),
        in_specs=[pl.BlockSpec((tm, D), lambda i: (i, 0)), pl.BlockSpec((1, D), lambda i: (0, 0))] + dep_specs,
        out_specs=pl.BlockSpec((tm, D), lambda i: (i, 0)),
        out_shape=SDS((T, D), BF16),
        compiler_params=_params(1),
    )(x, g0, *deps)


def _fwd_in(h, win_t):
    T = h.shape[0]
    tm, tn = min(T, 512), 1408

    def body(h_ref, w_ref, p_ref):
        for j in range(IN_W // tn):
            cols = slice(j * tn, (j + 1) * tn)
            p_ref[:, cols] = _nt(h_ref[...], w_ref[cols, :]).astype(BF16)

    return pl.pallas_call(
        body, name="fwd_in", grid=(T // tm,),
        in_specs=[pl.BlockSpec((tm, D), lambda i: (i, 0)), _resident((IN_W, D))],
        out_specs=pl.BlockSpec((tm, IN_W), lambda i: (i, 0)),
        out_shape=SDS((T, IN_W), BF16),
        compiler_params=_params(1),
    )(h, win_t)


def _sgu_forward_parts(u_ref, vs_ref, lng_ref, lnb_ref):
    u = u_ref[...].astype(F32)
    vs = vs_ref[...].astype(F32)
    gu, tu = _gelu(u)
    gv, tv = _gelu(vs)
    mu = jnp.mean(gv, axis=-1, keepdims=True)
    dv = gv - mu
    rstd = lax.rsqrt(jnp.mean(dv * dv, axis=-1, keepdims=True) + EPS)
    vhat = dv * rstd
    vn = (vhat * lng_ref[...] + lnb_ref[...]).astype(BF16)
    return u, vs, gu, tu, tv, rstd, vhat, vn


def _masked_ws(ws_ref, g):
    row = lax.broadcasted_iota(jnp.int32, (CHUNK, CHUNK), 0)
    col = lax.broadcasted_iota(jnp.int32, (CHUNK, CHUNK), 1)
    return jnp.where(row >= col, ws_ref[g], 0.0).astype(BF16)


def _fwd_sgu(proj, lng, lnb, ws, bst, after=None):
    T = proj.shape[0]
    tc = min(T, 512)

    def body(u_ref, vs_ref, lng_ref, lnb_ref, ws_ref, bst_ref, a_ref):
        _, _, gu, _, _, _, _, vn = _sgu_forward_parts(u_ref, vs_ref, lng_ref, lnb_ref)
        for g in range(GROUPS):
            wm = _masked_ws(ws_ref, g)
            cols = slice(g * CHUNK, (g + 1) * CHUNK)
            for c in range(tc // CHUNK):
                rows = slice(c * CHUNK, (c + 1) * CHUNK)
                mixed = _nn(wm, vn[rows, cols]) + bst_ref[:, g:g + 1]
                a_ref[rows, cols] = (gu[rows, cols] * mixed).astype(BF16)

    body, dep_specs, deps = _after(body, 6, after)
    return pl.pallas_call(
        body, name="fwd_sgu", grid=(T // tc,),
        in_specs=[pl.BlockSpec((tc, D), lambda i: (i, 0)), pl.BlockSpec((tc, D), lambda i: (i, 1)),
                  pl.BlockSpec((1, D), lambda i: (0, 0)), pl.BlockSpec((1, D), lambda i: (0, 0)),
                  pl.BlockSpec((GROUPS, CHUNK, CHUNK), lambda i: (0, 0, 0)),
                  pl.BlockSpec((CHUNK, GROUPS), lambda i: (0, 0))] + dep_specs,
        out_specs=pl.BlockSpec((tc, D), lambda i: (i, 0)),
        out_shape=SDS((T, D), BF16),
        compiler_params=_params(1),
    )(proj, proj, lng, lnb, ws, bst, *deps)


def _rope_tables(posf, invf, sgn, after=None):
    T = posf.shape[0]
    tr = min(T, 1024)

    def body(pos_ref, invf_ref, sgn_ref, c_ref, s_ref):
        ang = pos_ref[...] * invf_ref[...]
        c_ref[...] = jnp.cos(ang)
        s = jnp.sin(ang)
        s_ref[:, :128] = jnp.where(sgn_ref[...] < 0.0, -s, 0.0)
        s_ref[:, 128:] = jnp.where(sgn_ref[...] > 0.0, s, 0.0)

    body, dep_specs, deps = _after(body, 3, after)
    return pl.pallas_call(
        body, name="rope_tables", grid=(T // tr,),
        in_specs=[pl.BlockSpec((tr, 1), lambda i: (i, 0)), pl.BlockSpec((1, 128), lambda i: (0, 0)),
                  pl.BlockSpec((1, 128), lambda i: (0, 0))] + dep_specs,
        out_specs=[pl.BlockSpec((tr, 128), lambda i: (i, 0)), pl.BlockSpec((tr, 256), lambda i: (i, 0))],
        out_shape=[SDS((T, 128), F32), SDS((T, 256), F32)],
        compiler_params=_params(1),
    )(posf, invf, sgn, *deps)


def _rope(v, c, s):
    v = v.astype(F32)
    return v * c + pltpu.roll(v, 128 - ROPE // 2, 1) * s[:, :128] + pltpu.roll(v, ROPE // 2, 1) * s[:, 128:]


def _rope_bwd(dv, c, s):
    return dv * c + pltpu.roll(dv * s[:, :128], ROPE // 2, 1) + pltpu.roll(dv * s[:, 128:], 128 - ROPE // 2, 1)


def _fold_masks(first):
    jj = lax.broadcasted_iota(jnp.int32, (CHUNK, CHUNK), 0)
    t = lax.broadcasted_iota(jnp.int32, (CHUNK, CHUNK), 1)
    prev = jj > t
    return prev, jnp.where(prev & first, -1e30, 0.0)


def _fold(band, prev):
    return jnp.where(prev, band[:CHUNK], band[CHUNK:])


def _unfold(folded, prev):
    return jnp.concatenate([jnp.where(prev, folded, 0.0), jnp.where(prev, 0.0, folded)], axis=0)


def _softmax_sink(s, sink, key_axis):
    m = jnp.maximum(jnp.max(s, axis=key_axis, keepdims=True), sink)
    p = jnp.exp(s - m)
    esink = jnp.exp(sink - m)
    inv = 1.0 / (jnp.sum(p, axis=key_axis, keepdims=True) + esink)
    return p * inv, esink * inv


def _head_pair_operand(slab, g):
    lo = lax.broadcasted_iota(jnp.int32, slab.shape, 1) < HEAD
    if g % 2 == 0:
        first = jnp.where(lo, slab, 0.0)
        second = pltpu.roll(first, HEAD, 1)
    else:
        second = jnp.where(lo, 0.0, slab)
        first = pltpu.roll(second, HEAD, 1)
    return jnp.concatenate([first, second], axis=0).astype(BF16)


def _head_pair_gradient(acc, g):
    top, bot = acc[:2 * CHUNK], acc[2 * CHUNK:]
    lo = lax.broadcasted_iota(jnp.int32, top.shape, 1) < HEAD
    if g % 2 == 0:
        return jnp.where(lo, top, 0.0) + pltpu.roll(jnp.where(lo, 0.0, bot), HEAD, 1)
    return pltpu.roll(jnp.where(lo, top, 0.0), HEAD, 1) + jnp.where(lo, 0.0, bot)


PAIRS_PER_KV = N_Q // N_KV // 2
KV_W = N_KV * HEAD


def _band(prev_ref, cur_ref, cols=slice(None)):
    return jnp.concatenate([prev_ref[:, cols], cur_ref[:, cols]], axis=0)


def _fwd_attn(proj, cos, sin, sinks):
    T = proj.shape[0]
    nb = T // CHUNK
    cur = lambda i: i
    prev = lambda i: jnp.maximum(i - 1, 0)

    def body(q_ref, kp_ref, kc_ref, vp_ref, vc_ref, cp_ref, cc_ref, sp_ref, sc_ref, sink_ref,
             o_ref, qr_ref, kr_ref, p_ref, psink_ref):
        prev_slot, bias = _fold_masks(pl.program_id(0) == 0)
        c_band, s_band = _band(cp_ref, cc_ref), _band(sp_ref, sc_ref)
        for j in range(KV_W // 128):
            cols = slice(j * 128, (j + 1) * 128)
            k_slab = _rope(_band(kp_ref, kc_ref, cols), c_band, s_band)
            kr_ref[:, cols] = k_slab[CHUNK:].astype(BF16)
            v_slab = _band(vp_ref, vc_ref, cols).astype(F32)
            for g in (2 * j, 2 * j + 1):
                k2 = _head_pair_operand(k_slab, g)
                v2 = _head_pair_operand(v_slab, g)
                pairs = [g * PAIRS_PER_KV + r for r in range(PAIRS_PER_KV)]
                qps = []
                for pair in pairs:
                    lanes = slice(pair * 128, (pair + 1) * 128)
                    qps.append((_rope(q_ref[:, lanes], cc_ref[...], sc_ref[...]) * (HEAD ** -0.5)).astype(BF16))
                    qr_ref[:, lanes] = qps[-1]
                s2 = _nt(k2, jnp.concatenate(qps, axis=0))
                pcols = []
                for r, pair in enumerate(pairs):
                    ps = []
                    for e in range(2):
                        head = 2 * pair + e
                        s = _fold(s2[e * 2 * CHUNK:(e + 1) * 2 * CHUNK, r * 128:(r + 1) * 128], prev_slot) + bias
                        p, psink = _softmax_sink(s, sink_ref[head], 0)
                        p = p.astype(BF16)
                        p_ref[head] = p
                        psink_ref[head:head + 1, :] = psink
                        ps.append(_unfold(p, prev_slot))
                    pcols.append(jnp.concatenate(ps, axis=0))
                o = _tn(jnp.concatenate(pcols, axis=1), v2).astype(BF16)
                for r, pair in enumerate(pairs):
                    o_ref[:, pair * 128:(pair + 1) * 128] = o[r * CHUNK:(r + 1) * CHUNK]

    table = lambda which, width: pl.BlockSpec((CHUNK, width), lambda i: (which(i), 0))
    return pl.pallas_call(
        body, name="fwd_attn", grid=(nb,),
        in_specs=[pl.BlockSpec((CHUNK, D), lambda i: (i, OFF_Q // D)),
                  pl.BlockSpec((CHUNK, KV_W), lambda i: (prev(i), OFF_K // KV_W)),
                  pl.BlockSpec((CHUNK, KV_W), lambda i: (i, OFF_K // KV_W)),
                  pl.BlockSpec((CHUNK, KV_W), lambda i: (prev(i), OFF_VA // KV_W)),
                  pl.BlockSpec((CHUNK, KV_W), lambda i: (i, OFF_VA // KV_W)),
                  table(prev, 128), table(cur, 128), table(prev, 256), table(cur, 256),
                  pl.BlockSpec(memory_space=pltpu.SMEM)],
        out_specs=[pl.BlockSpec((CHUNK, D), lambda i: (i, 0)), pl.BlockSpec((CHUNK, D), lambda i: (i, 0)),
                   pl.BlockSpec((CHUNK, KV_W), lambda i: (i, 0)),
                   pl.BlockSpec((None, N_Q, CHUNK, CHUNK), lambda i: (i, 0, 0, 0)),
                   pl.BlockSpec((None, N_Q, CHUNK), lambda i: (i, 0, 0))],
        out_shape=[SDS((T, D), BF16), SDS((T, D), BF16), SDS((T, KV_W), BF16),
                   SDS((nb, N_Q, CHUNK, CHUNK), BF16), SDS((nb, N_Q, CHUNK), F32)],
        compiler_params=_params(1),
    )(proj, proj, proj, proj, proj, cos, cos, sin, sin, sinks)


def _row_halves(tm):
    return [slice(0, tm // 2), slice(tm // 2, tm)] if tm % 32 == 0 else [slice(0, tm)]


def _fwd_mix(a, att, proj, x, wa, wb, wo, g1, g2):
    T = x.shape[0]
    tm = min(T, 512)
    half = D // 2

    def body(a_ref, att_ref, ga0, ga1, gb0, gb1, x_ref, wa_ref, wb_ref, wo_ref, g1_ref, g2_ref,
             mg_ref, a2_ref, b2_ref, mix_ref, x1_ref, hf_ref):
        for rows in _row_halves(tm):
            a2 = _nn(a_ref[rows, :], wa_ref[...])
            b2 = _nn(att_ref[rows, :], wb_ref[...])
            ga = jnp.concatenate([ga0[rows, :], ga1[rows, :]], axis=1).astype(F32)
            gb = jnp.concatenate([gb0[rows, :], gb1[rows, :]], axis=1).astype(F32)
            merged = (_sigmoid(ga) * a2 + _sigmoid(gb) * b2).astype(BF16)
            a2_ref[rows, :] = a2.astype(BF16)
            b2_ref[rows, :] = b2.astype(BF16)
            mg_ref[rows, :] = merged
            mix = _nn(merged, wo_ref[...])
            mix_ref[rows, :] = mix
            _, mh = _rms_stats(mix)
            x1 = x_ref[rows, :] + mh * g1_ref[...]
            x1_ref[rows, :] = x1
            _, xh = _rms_stats(x1)
            hf_ref[rows, :] = (xh * g2_ref[...]).astype(BF16)

    row = lambda i: (i, 0)
    const = lambda i: (0, 0)
    gspec = lambda off: pl.BlockSpec((tm, half), lambda i: (i, off // half))
    return pl.pallas_call(
        body, name="fwd_mix", grid=(T // tm,),
        in_specs=[pl.BlockSpec((tm, D), row), pl.BlockSpec((tm, D), row),
                  gspec(OFF_GA), gspec(OFF_GA + half), gspec(OFF_GB), gspec(OFF_GB + half),
                  pl.BlockSpec((tm, D), row), _resident((D, D)), _resident((D, D)),
                  _resident((D, D)), pl.BlockSpec((1, D), const), pl.BlockSpec((1, D), const)],
        out_specs=[pl.BlockSpec((tm, D), row)] * 6,
        out_shape=[SDS((T, D), BF16), SDS((T, D), BF16), SDS((T, D), BF16), SDS((T, D), F32), SDS((T, D), F32),
                   SDS((T, D), BF16)],
        compiler_params=_params(1),
    )(a, att, proj, proj, proj, proj, x, wa, wb, wo, g1, g2)


FF_SPLIT = N_DEV
FF_TILE = D_FF // FF_SPLIT
FF_STEP = 2048
FF_SLABS = FF_STEP // FF_TILE
FF_STEPS = D_FF // FF_STEP


def _fwd_ff(hf, wfi3, wfo, x1, tgt, g3):
    T = hf.shape[0]
    tm = min(T, 512)

    def body(hf_ref, wfi_ref, wfo_ref, x1_ref, tgt_ref, g3_ref, f_ref, dy_ref, dff_ref, dg3_ref, loss_ref, r_s):
        @pl.when(pl.program_id(0) == 0)
        def _():
            dg3_ref[...] = jnp.zeros_like(dg3_ref)
            loss_ref[...] = jnp.zeros_like(loss_ref)

        hf_t = hf_ref[...]
        for s in range(FF_SPLIT):
            cols = slice(s * FF_TILE, (s + 1) * FF_TILE)
            f = _nn(hf_t, wfi_ref[s]).astype(BF16)
            f_ref[:, cols] = f
            rl = jnp.maximum(f.astype(F32), 0.0)
            r_s[:, cols] = (rl * rl).astype(BF16)
        r3, fh = _rms_stats(_nn(r_s[...], wfo_ref[...]))
        e = x1_ref[...] + fh * g3_ref[...] - tgt_ref[...]
        loss_ref[...] += jnp.sum(e * e) * (0.5 / D)
        dy = e * (1.0 / D)
        dy_ref[...] = dy
        dg3_ref[...] += _colsum(dy * fh)
        dff_ref[...] = _rms_bwd(dy, fh, r3, g3_ref[...]).astype(BF16)

    row = lambda i: (i, 0)
    const = lambda i: (0, 0)
    return pl.pallas_call(
        body, name="fwd_ff", grid=(T // tm,),
        in_specs=[pl.BlockSpec((tm, D), row), _resident((FF_SPLIT, D, FF_TILE)), _resident((D_FF, D)),
                  pl.BlockSpec((tm, D), row),
                  pl.BlockSpec((tm, D), row), pl.BlockSpec((1, D), const)],
        out_specs=[pl.BlockSpec((tm, D_FF), row), pl.BlockSpec((tm, D), row),
                   pl.BlockSpec((tm, D), row), pl.BlockSpec((1, D), const), pl.BlockSpec((1, 128), const)],
        out_shape=[SDS((T, D_FF), BF16), SDS((T, D), F32), SDS((T, D), BF16), SDS((1, D), F32), SDS((1, 128), F32)],
        scratch_shapes=[pltpu.VMEM((tm, D_FF), BF16)],
        compiler_params=_params(1),
    )(hf, wfi3, wfo, x1, tgt, g3)


def _bwd_ff(dff, f, wfi3, wfo, x1, dy, mix, g1, g2):
    T = dff.shape[0]
    tm = min(T, 512)
    last = FF_STEPS - 1

    def body(dff_ref, f_ref, wfi_ref, wfo_ref, x1_ref, dy_ref, mix_ref, g1_ref, g2_ref,
             df_ref, dx1_ref, dmix_ref, dg2_ref, dg1_ref, acc):
        i, p = pl.program_id(0), pl.program_id(1)

        @pl.when((i == 0) & (p == 0))
        def _():
            dg2_ref[...] = jnp.zeros_like(dg2_ref)
            dg1_ref[...] = jnp.zeros_like(dg1_ref)

        dr = _nt(dff_ref[...], wfo_ref[pl.ds(pl.multiple_of(p * FF_STEP, FF_STEP), FF_STEP), :])
        df_ref[...] = (dr * (2.0 * jnp.maximum(f_ref[...].astype(F32), 0.0))).astype(BF16)
        part = _nt(df_ref[:, :FF_TILE], wfi_ref[p * FF_SLABS])
        for s in range(1, FF_SLABS):
            part = part + _nt(df_ref[:, s * FF_TILE:(s + 1) * FF_TILE], wfi_ref[p * FF_SLABS + s])

        @pl.when(p == 0)
        def _():
            acc[...] = part

        @pl.when(p > 0)
        def _():
            acc[...] += part

        @pl.when(p == last)
        def _():
            dhf = acc[...]
            r2, xh = _rms_stats(x1_ref[...])
            dg2_ref[...] += _colsum(dhf * xh)
            dx1 = dy_ref[...] + _rms_bwd(dhf, xh, r2, g2_ref[...])
            dx1_ref[...] = dx1
            r1, mh = _rms_stats(mix_ref[...])
            dg1_ref[...] += _colsum(dx1 * mh)
            dmix_ref[...] = _rms_bwd(dx1, mh, r1, g1_ref[...]).astype(BF16)

    row = lambda i, p: (i, 0)
    const = lambda i, p: (0, 0)
    return pl.pallas_call(
        body, name="bwd_ff", grid=(T // tm, FF_STEPS),
        in_specs=[pl.BlockSpec((tm, D), row), pl.BlockSpec((tm, FF_STEP), lambda i, p: (i, p)),
                  _resident((FF_SPLIT, D, FF_TILE)), _resident((D_FF, D)),
                  pl.BlockSpec((tm, D), row), pl.BlockSpec((tm, D), row), pl.BlockSpec((tm, D), row),
                  pl.BlockSpec((1, D), const), pl.BlockSpec((1, D), const)],
        out_specs=[pl.BlockSpec((tm, FF_STEP), lambda i, p: (i, p)), pl.BlockSpec((tm, D), row),
                   pl.BlockSpec((tm, D), row), pl.BlockSpec((1, D), const), pl.BlockSpec((1, D), const)],
        out_shape=[SDS((T, D_FF), BF16), SDS((T, D), F32), SDS((T, D), BF16), SDS((1, D), F32), SDS((1, D), F32)],
        scratch_shapes=[pltpu.VMEM((tm, D), F32)],
        compiler_params=_params(2),
    )(dff, f, wfi3, wfo, x1, dy, mix, g1, g2)


def _wgrad_ff(hf, df, f, dff):
    T = hf.shape[0]
    tt = min(T, 1024)
    wide = 2 * FF_TILE

    def body(hf_ref, df_ref, f_ref, dff_ref, dwfi_ref, dwfo_ref, acc_i, acc_o):
        t = pl.program_id(1)

        @pl.when(t == 0)
        def _():
            acc_i[...] = jnp.zeros_like(acc_i)
            acc_o[...] = jnp.zeros_like(acc_o)

        acc_i[...] += _tn(hf_ref[...], df_ref[...])
        rl = jnp.maximum(f_ref[...].astype(F32), 0.0)
        acc_o[...] += _tn((rl * rl).astype(BF16), dff_ref[...])

        @pl.when(t == T // tt - 1)
        def _():
            dwfi_ref[0] = acc_i[:, :FF_TILE].astype(BF16)
            dwfi_ref[1] = acc_i[:, FF_TILE:].astype(BF16)
            dwfo_ref[...] = acc_o[...].astype(BF16)

    return pl.pallas_call(
        body, name="wgrad_ff", grid=(D_FF // wide, T // tt),
        in_specs=[pl.BlockSpec((tt, D), lambda p, t: (t, 0)), pl.BlockSpec((tt, wide), lambda p, t: (t, p)),
                  pl.BlockSpec((tt, wide), lambda p, t: (t, p)), pl.BlockSpec((tt, D), lambda p, t: (t, 0))],
        out_specs=[pl.BlockSpec((2, D, FF_TILE), lambda p, t: (p, 0, 0)), pl.BlockSpec((wide, D), lambda p, t: (p, 0))],
        out_shape=[SDS((FF_SPLIT, D, FF_TILE), BF16), SDS((D_FF, D), BF16)],
        scratch_shapes=[pltpu.VMEM((D, wide), F32), pltpu.VMEM((wide, D), F32)],
        compiler_params=_params(2),
    )(hf, df, f, dff)


def _bwd_mix(dmix, proj, a2, b2, wo, wa, wb, after=None):
    T = dmix.shape[0]
    tm = min(T, 512)
    half = D // 2

    def body(dmix_ref, ga0, ga1, gb0, gb1, a2_ref, b2_ref, wo_ref, wa_ref, wb_ref,
             da2_ref, db2_ref, dg_ref, da_ref, datt_ref):
        for rows in _row_halves(tm):
            dmg = _nt(dmix_ref[rows, :], wo_ref[...])
            sa = _sigmoid(jnp.concatenate([ga0[rows, :], ga1[rows, :]], axis=1).astype(F32))
            sb = _sigmoid(jnp.concatenate([gb0[rows, :], gb1[rows, :]], axis=1).astype(F32))
            da2 = (dmg * sa).astype(BF16)
            db2 = (dmg * sb).astype(BF16)
            da2_ref[rows, :] = da2
            db2_ref[rows, :] = db2
            dg_ref[rows, :D] = (dmg * a2_ref[rows, :].astype(F32) * (sa * (1.0 - sa))).astype(BF16)
            dg_ref[rows, D:] = (dmg * b2_ref[rows, :].astype(F32) * (sb * (1.0 - sb))).astype(BF16)
            da_ref[rows, :] = _nt(da2, wa_ref[...]).astype(BF16)
            datt_ref[rows, :] = _nt(db2, wb_ref[...]).astype(BF16)

    row = lambda i: (i, 0)
    const = lambda i: (0, 0)
    gspec = lambda off: pl.BlockSpec((tm, half), lambda i: (i, off // half))
    body, dep_specs, deps = _after(body, 10, after)
    return pl.pallas_call(
        body, name="bwd_mix", grid=(T // tm,),
        in_specs=[pl.BlockSpec((tm, D), row), gspec(OFF_GA), gspec(OFF_GA + half), gspec(OFF_GB), gspec(OFF_GB + half),
                  pl.BlockSpec((tm, D), row), pl.BlockSpec((tm, D), row),
                  _resident((D, D)), _resident((D, D)), _resident((D, D))] + dep_specs,
        out_specs=[pl.BlockSpec((tm, D), row), pl.BlockSpec((tm, D), row), pl.BlockSpec((tm, 2 * D), row),
                   pl.BlockSpec((tm, D), row), pl.BlockSpec((tm, D), row)],
        out_shape=[SDS((T, D), BF16), SDS((T, D), BF16), SDS((T, 2 * D), BF16), SDS((T, D), BF16), SDS((T, D), BF16)],
        compiler_params=_params(1),
    )(dmix, proj, proj, proj, proj, a2, b2, wo, wa, wb, *deps)


def _wgrad_mix(merged, dmix, a, da2, att, db2):
    T = merged.shape[0]
    tt = min(T, 512)

    def body(mg_ref, dmix_ref, a_ref, da2_ref, att_ref, db2_ref, dwo_ref, dwa_ref, dwb_ref, acc):
        t = pl.program_id(0)

        @pl.when(t == 0)
        def _():
            acc[...] = jnp.zeros_like(acc)

        acc[0] += _tn(mg_ref[...], dmix_ref[...])
        acc[1] += _tn(a_ref[...], da2_ref[...])
        acc[2] += _tn(att_ref[...], db2_ref[...])

        @pl.when(t == T // tt - 1)
        def _():
            dwo_ref[...] = acc[0].astype(BF16)
            dwa_ref[...] = acc[1].astype(BF16)
            dwb_ref[...] = acc[2].astype(BF16)

    return pl.pallas_call(
        body, name="wgrad_mix", grid=(T // tt,),
        in_specs=[pl.BlockSpec((tt, D), lambda t: (t, 0))] * 6,
        out_specs=[pl.BlockSpec((D, D), lambda t: (0, 0))] * 3,
        out_shape=[SDS((D, D), BF16)] * 3,
        scratch_shapes=[pltpu.VMEM((3, D, D), F32)],
        compiler_params=_params(1),
    )(merged, dmix, a, da2, att, db2)


def _bwd_attn(qr, kr, probs, psink, proj, cos, sin, datt, after=None):
    T = proj.shape[0]
    nb = T // CHUNK
    cur = lambda i: jnp.minimum(i, nb - 1)
    prev = lambda i: jnp.maximum(jnp.minimum(i, nb - 1) - 1, 0)

    def body(q_ref, kp_ref, kc_ref, vp_ref, vc_ref, cp_ref, cc_ref, sp_ref, sc_ref, p_ref, psink_ref, do_ref,
             dq_ref, dkv_ref, dsink_ref, carry_k, carry_v):
        i = pl.program_id(0)

        @pl.when(i == 0)
        def _():
            carry_k[...] = jnp.zeros_like(carry_k)
            carry_v[...] = jnp.zeros_like(carry_v)
            dsink_ref[...] = jnp.zeros_like(dsink_ref)

        @pl.when(i < nb)
        def _():
            prev_slot, _ = _fold_masks(i == 0)
            c_band, s_band = _band(cp_ref, cc_ref), _band(sp_ref, sc_ref)
            lane = lax.broadcasted_iota(jnp.int32, (1, 128), 1)
            dsink = jnp.zeros((1, 128), F32)
            for j in range(KV_W // 128):
                cols = slice(j * 128, (j + 1) * 128)
                k_slab = _band(kp_ref, kc_ref, cols).astype(F32)
                v_slab = _band(vp_ref, vc_ref, cols).astype(F32)
                dk_slab = jnp.zeros((2 * CHUNK, 128), F32)
                dv_slab = jnp.zeros((2 * CHUNK, 128), F32)
                for g in (2 * j, 2 * j + 1):
                    k2 = _head_pair_operand(k_slab, g)
                    v2 = _head_pair_operand(v_slab, g)
                    pairs = [g * PAIRS_PER_KV + r for r in range(PAIRS_PER_KV)]
                    q_stack = jnp.concatenate([q_ref[:, pr * 128:(pr + 1) * 128] for pr in pairs], axis=0)
                    do_stack = jnp.concatenate([do_ref[:, pr * 128:(pr + 1) * 128] for pr in pairs], axis=0)
                    dp2 = _nt(v2, do_stack)
                    pcols, dscols = [], []
                    for r, pair in enumerate(pairs):
                        ps, dss = [], []
                        for e in range(2):
                            head = 2 * pair + e
                            p_b = p_ref[head]
                            p = p_b.astype(F32)
                            dp = _fold(dp2[e * 2 * CHUNK:(e + 1) * 2 * CHUNK, r * 128:(r + 1) * 128], prev_slot)
                            delta = jnp.sum(p * dp, axis=0, keepdims=True)
                            ps.append(_unfold(p_b, prev_slot))
                            dss.append(_unfold((p * (dp - delta)).astype(BF16), prev_slot))
                            dsink = dsink + jnp.where(lane == head, -jnp.sum(psink_ref[head:head + 1, :] * delta), 0.0)
                        pcols.append(jnp.concatenate(ps, axis=0))
                        dscols.append(jnp.concatenate(dss, axis=0))
                    ds2 = jnp.concatenate(dscols, axis=1)
                    dq = _tn(ds2, k2) * (HEAD ** -0.5)
                    for r, pair in enumerate(pairs):
                        dq_ref[:, pair * 128:(pair + 1) * 128] = _rope_bwd(
                            dq[r * CHUNK:(r + 1) * CHUNK], cc_ref[...], sc_ref[...]).astype(BF16)
                    dk_slab = dk_slab + _head_pair_gradient(_nn(ds2, q_stack), g)
                    dv_slab = dv_slab + _head_pair_gradient(_nn(jnp.concatenate(pcols, axis=1), do_stack), g)
                dk_slab = _rope_bwd(dk_slab, c_band, s_band)
                vcols = slice(KV_W + j * 128, KV_W + (j + 1) * 128)
                dkv_ref[:, cols] = (carry_k[:, cols] + dk_slab[:CHUNK]).astype(BF16)
                dkv_ref[:, vcols] = (carry_v[:, cols] + dv_slab[:CHUNK]).astype(BF16)
                carry_k[:, cols] = dk_slab[CHUNK:]
                carry_v[:, cols] = dv_slab[CHUNK:]
            dsink_ref[...] += dsink

        @pl.when(i == nb)
        def _():
            dkv_ref[:, :KV_W] = carry_k[...].astype(BF16)
            dkv_ref[:, KV_W:] = carry_v[...].astype(BF16)

    table = lambda which, width: pl.BlockSpec((CHUNK, width), lambda i: (which(i), 0))
    body, dep_specs, deps = _after(body, 12, after)
    return pl.pallas_call(
        body, name="bwd_attn", grid=(nb + 1,),
        in_specs=[pl.BlockSpec((CHUNK, D), lambda i: (cur(i), 0)),
                  pl.BlockSpec((CHUNK, KV_W), lambda i: (prev(i), 0)),
                  pl.BlockSpec((CHUNK, KV_W), lambda i: (cur(i), 0)),
                  pl.BlockSpec((CHUNK, KV_W), lambda i: (prev(i), OFF_VA // KV_W)),
                  pl.BlockSpec((CHUNK, KV_W), lambda i: (cur(i), OFF_VA // KV_W)),
                  table(prev, 128), table(cur, 128), table(prev, 256), table(cur, 256),
                  pl.BlockSpec((None, N_Q, CHUNK, CHUNK), lambda i: (cur(i), 0, 0, 0)),
                  pl.BlockSpec((None, N_Q, CHUNK), lambda i: (cur(i), 0, 0)),
                  pl.BlockSpec((CHUNK, D), lambda i: (cur(i), 0))] + dep_specs,
        out_specs=[pl.BlockSpec((CHUNK, D), lambda i: (cur(i), 0)),
                   pl.BlockSpec((CHUNK, 2 * KV_W), lambda i: (jnp.maximum(i - 1, 0), 0)),
                   pl.BlockSpec((1, 128), lambda i: (0, 0))],
        out_shape=[SDS((T, D), BF16), SDS((T, 2 * KV_W), BF16), SDS((1, 128), F32)],
        scratch_shapes=[pltpu.VMEM((CHUNK, KV_W), F32), pltpu.VMEM((CHUNK, KV_W), F32)],
        compiler_params=_params(1),
    )(qr, kr, kr, proj, proj, cos, cos, sin, sin, probs, psink, datt, *deps)


def _bwd_sgu(proj, da, lng, lnb, ws, bst):
    T = proj.shape[0]
    tc = min(T, 512)
    nsteps = T // tc

    def body(u_ref, vs_ref, da_ref, lng_ref, lnb_ref, ws_ref, bst_ref,
             duv_ref, dws_ref, dbs_ref, dlng_ref, dlnb_ref, dvn_s, dgu_s, dmx_sum):
        i = pl.program_id(0)

        @pl.when(i == 0)
        def _():
            dws_ref[...] = jnp.zeros_like(dws_ref)
            dlng_ref[...] = jnp.zeros_like(dlng_ref)
            dlnb_ref[...] = jnp.zeros_like(dlnb_ref)
            dmx_sum[...] = jnp.zeros_like(dmx_sum)

        u, vs, gu, tu, tv, rstd, vhat, vn = _sgu_forward_parts(u_ref, vs_ref, lng_ref, lnb_ref)
        da = da_ref[...].astype(F32)
        for g in range(GROUPS):
            wm = _masked_ws(ws_ref, g)
            cols = slice(g * CHUNK, (g + 1) * CHUNK)
            dws = jnp.zeros((CHUNK, CHUNK), F32)
            dsum = jnp.zeros((CHUNK, CHUNK), F32)
            for c in range(tc // CHUNK):
                rows = slice(c * CHUNK, (c + 1) * CHUNK)
                vn_cg = vn[rows, cols]
                mixed = _nn(wm, vn_cg) + bst_ref[:, g:g + 1]
                dgu_s[rows, cols] = da[rows, cols] * mixed
                dmx = da[rows, cols] * gu[rows, cols]
                dmxb = dmx.astype(BF16)
                dws = dws + _nt(dmxb, vn_cg)
                dsum = dsum + dmx
                dvn_s[rows, cols] = _tn(wm, dmxb)
            dws_ref[g] += dws
            dmx_sum[:, cols] += dsum
        dvn = dvn_s[...]
        dlng_ref[...] += _colsum(dvn * vhat)
        dlnb_ref[...] += _colsum(dvn)
        dvh = dvn * lng_ref[...]
        dgv = rstd * (dvh - jnp.mean(dvh, axis=-1, keepdims=True) - vhat * jnp.mean(dvh * vhat, axis=-1, keepdims=True))
        duv_ref[:, :D] = (dgu_s[...] * _gelu_grad(u, tu)).astype(BF16)
        duv_ref[:, D:] = (dgv * _gelu_grad(vs, tv)).astype(BF16)

        @pl.when(i == nsteps - 1)
        def _():
            row = lax.broadcasted_iota(jnp.int32, (CHUNK, CHUNK), 0)
            col = lax.broadcasted_iota(jnp.int32, (CHUNK, CHUNK), 1)
            for g in range(GROUPS):
                dws_ref[g] = jnp.where(row >= col, dws_ref[g], 0.0)
                dbs_ref[g:g + 1, :] = _colsum(dmx_sum[:, g * CHUNK:(g + 1) * CHUNK].T)

    const2 = lambda i: (0, 0)
    return pl.pallas_call(
        body, name="bwd_sgu", grid=(nsteps,),
        in_specs=[pl.BlockSpec((tc, D), lambda i: (i, 0)), pl.BlockSpec((tc, D), lambda i: (i, 1)),
                  pl.BlockSpec((tc, D), lambda i: (i, 0)), pl.BlockSpec((1, D), const2), pl.BlockSpec((1, D), const2),
                  pl.BlockSpec((GROUPS, CHUNK, CHUNK), lambda i: (0, 0, 0)), pl.BlockSpec((CHUNK, GROUPS), const2)],
        out_specs=[pl.BlockSpec((tc, 2 * D), lambda i: (i, 0)), pl.BlockSpec((GROUPS, CHUNK, CHUNK), lambda i: (0, 0, 0)),
                   pl.BlockSpec((GROUPS, CHUNK), const2), pl.BlockSpec((1, D), const2), pl.BlockSpec((1, D), const2)],
        out_shape=[SDS((T, 2 * D), BF16), SDS((GROUPS, CHUNK, CHUNK), F32), SDS((GROUPS, CHUNK), F32),
                   SDS((1, D), F32), SDS((1, D), F32)],
        scratch_shapes=[pltpu.VMEM((tc, D), F32), pltpu.VMEM((tc, D), F32), pltpu.VMEM((CHUNK, D), F32)],
        compiler_params=_params(1),
    )(proj, proj, da, lng, lnb, ws, bst)


IN_SEG_WIDTHS = (2 * D, D, 2 * N_KV * HEAD, 2 * D)


def _resident(shape):
    return pl.BlockSpec(shape, lambda *_: (0,) * len(shape), pipeline_mode=pl.Buffered(1))


def _bwd_in(duv, dq, dkv, dg, win_t, x, dx1, g0, after=None):
    T = x.shape[0]
    tm = min(T, 512)

    def body(duv_ref, dq_ref, dkv_ref, dg_ref, w_ref, x_ref, dx1_ref, g0_ref, gx_ref, dg0_ref):
        @pl.when(pl.program_id(0) == 0)
        def _():
            dg0_ref[...] = jnp.zeros_like(dg0_ref)

        dh, off = None, 0
        for ref, width in zip((duv_ref, dq_ref, dkv_ref, dg_ref), IN_SEG_WIDTHS):
            part = _nn(ref[...], w_ref[off:off + width, :])
            dh = part if dh is None else dh + part
            off += width
        r0, xh = _rms_stats(x_ref[...])
        dg0_ref[...] += _colsum(dh * xh)
        gx_ref[...] = dx1_ref[...] + _rms_bwd(dh, xh, r0, g0_ref[...])

    row = lambda i: (i, 0)
    body, dep_specs, deps = _after(body, 8, after)
    return pl.pallas_call(
        body, name="bwd_in", grid=(T // tm,),
        in_specs=[pl.BlockSpec((tm, w), row) for w in IN_SEG_WIDTHS] + [
            _resident((IN_W, D)), pl.BlockSpec((tm, D), row), pl.BlockSpec((tm, D), row),
            pl.BlockSpec((1, D), lambda i: (0, 0))] + dep_specs,
        out_specs=[pl.BlockSpec((tm, D), row), pl.BlockSpec((1, D), lambda i: (0, 0))],
        out_shape=[SDS((T, D), F32), SDS((1, D), F32)],
        compiler_params=_params(1),
    )(duv, dq, dkv, dg, win_t, x, dx1, g0, *deps)


def _wgrad_rows(h, segs, first_row, into, name):
    T = h.shape[0]
    tt = min(T, 1024)
    widths = [s.shape[1] for s in segs]
    rows = sum(widths)
    n_in = 1 + len(segs) + (into is not None)

    def body(*refs):
        h_ref, seg_refs = refs[0], refs[1:1 + len(segs)]
        dw_ref, acc, stage, sem = refs[n_in], refs[n_in + 1], refs[n_in + 2], refs[n_in + 3]
        t = pl.program_id(0)

        @pl.when(t == 0)
        def _():
            acc[...] = jnp.zeros_like(acc)

        off = 0
        for ref, width in zip(seg_refs, widths):
            acc[off:off + width, :] += _tn(ref[...], h_ref[...])
            off += width

        @pl.when(t == T // tt - 1)
        def _():
            stage[...] = acc[...].astype(BF16)
            out = pltpu.make_async_copy(stage, dw_ref.at[pl.ds(first_row, rows)], sem)
            out.start()
            out.wait()

    row = lambda t: (t, 0)
    return pl.pallas_call(
        body, name=name, grid=(T // tt,),
        in_specs=[pl.BlockSpec((tt, D), row)] + [pl.BlockSpec((tt, w), row) for w in widths] + [_ANY] * (into is not None),
        out_specs=_ANY,
        out_shape=SDS((IN_W, D), BF16),
        input_output_aliases={} if into is None else {n_in - 1: 0},
        scratch_shapes=[pltpu.VMEM((rows, D), F32), pltpu.VMEM((rows, D), BF16), pltpu.SemaphoreType.DMA],
        compiler_params=_params(1),
    )(h, *segs, *([] if into is None else [into]))


def _wgrad_in(h, duv, dq, dkv, dg):
    dw = _wgrad_rows(h, [dg], IN_SEG_WIDTHS[0] + IN_SEG_WIDTHS[1] + IN_SEG_WIDTHS[2], None, "wgrad_in_gates")
    dw = _wgrad_rows(h, [duv], 0, dw, "wgrad_in_uv")
    return _wgrad_rows(h, [dq, dkv], IN_SEG_WIDTHS[0], dw, "wgrad_in_qkv")


def _place():
    x, y, c = lax.axis_index("x"), lax.axis_index("y"), lax.axis_index("c")
    return x, y, c, 4 * x + 2 * y + c


def _peers(x, y, c):
    out = []
    for mask in range(1, N_DEV):
        px = 1 - x if mask & 4 else x
        py = 1 - y if mask & 2 else y
        pc = 1 - c if mask & 1 else c
        out.append(((px, py, pc), 4 * px + 2 * py + pc))
    return out


def _all_to_all(arrays, gather, name, after=None):
    n = len(arrays)

    def body(*refs):
        ins, outs = refs[:n], refs[n:2 * n]
        send_sems, recv_sems, local_sems = refs[2 * n:]
        x, y, c, me = _place()
        local, sends, recvs = [], [], []
        for a in range(n):
            src_own = ins[a] if gather[a] else ins[a].at[me]
            local.append(pltpu.make_async_copy(src_own, outs[a].at[me], local_sems.at[a]))
            for k, (peer, pid) in enumerate(_peers(x, y, c)):
                sem = a * (N_DEV - 1) + k
                src = ins[a] if gather[a] else ins[a].at[pid]
                sends.append(pltpu.make_async_remote_copy(
                    src_ref=src, dst_ref=outs[a].at[me], send_sem=send_sems.at[sem], recv_sem=recv_sems.at[sem],
                    device_id=peer, device_id_type=MESH))
                recvs.append(pltpu.make_async_remote_copy(
                    src_ref=src, dst_ref=outs[a].at[pid], send_sem=send_sems.at[sem], recv_sem=recv_sems.at[sem],
                    device_id=peer, device_id_type=MESH))
        for cp in local + sends:
            cp.start()
        for cp in recvs:
            cp.wait_recv()
        for cp in sends:
            cp.wait_send()
        for cp in local:
            cp.wait()

    out_shape = [SDS((N_DEV,) + a.shape if gt else a.shape, a.dtype) for a, gt in zip(arrays, gather)]
    nsem = n * (N_DEV - 1)
    body, dep_specs, deps = _after(body, n, after)
    return pl.pallas_call(
        body, name=name,
        in_specs=[pl.BlockSpec(memory_space=pl.ANY)] * n + dep_specs,
        out_specs=[pl.BlockSpec(memory_space=pl.ANY)] * n,
        out_shape=out_shape,
        scratch_shapes=[pltpu.SemaphoreType.DMA((nsem,)), pltpu.SemaphoreType.DMA((nsem,)), pltpu.SemaphoreType.DMA((n,))],
    )(*arrays, *deps)


_HBM = pl.BlockSpec(memory_space=pltpu.HBM)
_SEM = pl.BlockSpec(memory_space=pltpu.SEMAPHORE)
_EFFECT = pltpu.SideEffectType.DATAFLOW_SIDE_EFFECTING
GATHER = "gather"
SCATTER = "scatter"
SPREAD = "spread"


def _zone_shape(a, mode):
    if mode == GATHER:
        return (N_DEV,) + a.shape
    return (N_DEV - 1,) + (a.shape[1:] if mode == SCATTER else a.shape)


def _start_copies(arrays, modes, name, after=None):
    n = len(arrays)
    zones = [lax.empty(_zone_shape(a, m), a.dtype) for a, m in zip(arrays, modes)]

    def body(*refs):
        ins, lands = refs[:n], refs[n:2 * n]
        send_sems, recv_sems = refs[-2 * n - 3], refs[-2 * n - 2]
        token = refs[-1]
        x, y, c, me = _place()
        for a in range(n):
            for k, (peer, pid) in enumerate(_peers(x, y, c)):
                src = ins[a].at[pid] if modes[a] == SCATTER else ins[a]
                dst = lands[a].at[me] if modes[a] == GATHER else lands[a].at[k]
                pltpu.make_async_remote_copy(src_ref=src, dst_ref=dst, send_sem=send_sems.at[a], recv_sem=recv_sems.at[a],
                                             device_id=peer, device_id_type=MESH).start()
        token[...] = jnp.zeros_like(token)

    hbm = lambda a: pltpu.HBM(a.shape, a.dtype)
    sems = pltpu.SemaphoreType.DMA((n,))
    extra = [] if after is None else [after]
    operands = [pltpu.with_memory_space_constraint(a, pltpu.HBM) for a in list(arrays) + zones]
    res = pl.pallas_call(
        body, name=name,
        out_shape=(sems, sems, *[hbm(a) for a in arrays], *[hbm(z) for z in zones], SDS((8, 128), F32)),
        in_specs=[_HBM] * (2 * n) + [_ANY] * len(extra),
        out_specs=(_SEM, _SEM, *[_HBM] * (2 * n), pl.BlockSpec(memory_space=pltpu.VMEM)),
        input_output_aliases={i: 2 + i for i in range(2 * n)},
        compiler_params=pltpu.CompilerParams(has_side_effects=_EFFECT),
    )(*operands, *extra)
    return res[0], res[1], list(res[2:2 + n]), list(res[2 + n:2 + 2 * n]), res[-1]


def _wait_copies(started, after, name, count=N_DEV - 1):
    send_sems, recv_sems, thru, zones, _ = started
    nt, nz = len(thru), len(zones)

    def body(*refs):
        lands = refs[nt:nt + nz]
        send_ref, recv_ref = refs[nt + nz], refs[nt + nz + 1]
        x, y, c, _ = _place()
        for a in range(nz):
            blocks = lands[a].at[pl.ds(0, count)]
            cp = pltpu.make_async_remote_copy(src_ref=blocks, dst_ref=blocks, send_sem=send_ref.at[a], recv_sem=recv_ref.at[a],
                                              device_id=(x, y, 1 - c), device_id_type=MESH)
            cp.wait_send()
            cp.wait_recv()

    hbm = lambda a: pltpu.HBM(a.shape, a.dtype)
    res = pl.pallas_call(
        body, name=name,
        out_shape=tuple(hbm(a) for a in thru + zones),
        in_specs=[_HBM] * (nt + nz) + [_SEM, _SEM, _ANY],
        out_specs=tuple([_HBM] * (nt + nz)),
        input_output_aliases={i: i for i in range(nt + nz)},
        compiler_params=pltpu.CompilerParams(has_side_effects=_EFFECT),
    )(*thru, *zones, send_sems, recv_sems, after)
    return list(res[:nt]), list(res[nt:])


def _split_start(body, arrays, zones, name, after):
    n = len(arrays) + len(zones)
    hbm = lambda a: pltpu.HBM(a.shape, a.dtype)
    sems = pltpu.SemaphoreType.DMA((max(len(zones), 1),))
    extra = [] if after is None else [after]
    operands = [pltpu.with_memory_space_constraint(a, pltpu.HBM) for a in list(arrays) + list(zones)]
    res = pl.pallas_call(
        body, name=name,
        out_shape=(sems, sems, *[hbm(a) for a in operands], SDS((8, 128), F32)),
        in_specs=[_HBM] * n + [_ANY] * len(extra),
        out_specs=(_SEM, _SEM, *[_HBM] * n, pl.BlockSpec(memory_space=pltpu.VMEM)),
        input_output_aliases={i: 2 + i for i in range(n)},
        compiler_params=pltpu.CompilerParams(has_side_effects=_EFFECT),
    )(*operands, *extra)
    return res[0], res[1], list(res[2:2 + len(arrays)]), list(res[2 + len(arrays):2 + n]), res[-1]


def _gather_first_leg(shard, name, after=None):
    zone = lax.empty((N_DEV,) + shard.shape, shard.dtype)
    extra = 0 if after is None else 1

    def body(*refs):
        src, land = refs[0], refs[1]
        send_sem, recv_sem, token = refs[2 + extra], refs[3 + extra], refs[-1]
        x, y, c, me = _place()
        for peer in ((x, y, 1 - c), (1 - x, y, c), (x, 1 - y, c), (1 - x, 1 - y, c)):
            pltpu.make_async_remote_copy(src_ref=src, dst_ref=land.at[me], send_sem=send_sem.at[0], recv_sem=recv_sem.at[0],
                                         device_id=peer, device_id_type=MESH).start()
        token[...] = jnp.zeros_like(token)

    return _split_start(body, [shard], [zone], name, after)


def _gather_second_leg(zone, name, after=None):
    extra = 0 if after is None else 1

    def body(*refs):
        land = refs[0]
        send_sem, recv_sem, token = refs[1 + extra], refs[2 + extra], refs[-1]
        x, y, c, _ = _place()
        for px, py in ((1 - x, y), (x, 1 - y), (1 - x, 1 - y)):
            slot = 4 * px + 2 * py + c
            pltpu.make_async_remote_copy(src_ref=land.at[slot], dst_ref=land.at[slot], send_sem=send_sem.at[0],
                                         recv_sem=recv_sem.at[0], device_id=(x, y, 1 - c), device_id_type=MESH).start()
        token[...] = jnp.zeros_like(token)

    return _split_start(body, [], [zone], name, after)


UPDATE_BLOCK_ELEMS = 256 * 1024


def _update_rows(R, C):
    fits = [t for t in range(8, R + 1, 8) if R % t == 0 and t * C <= UPDATE_BLOCK_ELEMS]
    whole = [t for t in fits if t % 16 == 0]
    return max(whole or fits)


def _adamw_math(g, w, m, v):
    m2 = ADAM_B1 * m + (1.0 - ADAM_B1) * g
    v2 = ADAM_B2 * v + (1.0 - ADAM_B2) * (g * g)
    m_hat = m2 / (1.0 - ADAM_B1 ** ADAM_STEP)
    v_hat = v2 / (1.0 - ADAM_B2 ** ADAM_STEP)
    delta = -ADAM_LR * (m_hat / (jnp.sqrt(v_hat) + ADAM_EPS) + ADAM_WD * w)
    return delta, m2, v2


def _sum_adamw(parts, w, m, v, name):
    R, C = w.shape
    tr = _update_rows(R, C)

    def body(p_ref, w_ref, m_ref, v_ref, g_ref, d_ref, m2_ref, v2_ref):
        g = p_ref[0]
        for k in range(1, N_DEV):
            g = g + p_ref[k]
        g_ref[...] = g
        d_ref[...], m2_ref[...], v2_ref[...] = _adamw_math(g, w_ref[...], m_ref[...], v_ref[...])

    blk = pl.BlockSpec((tr, C), lambda i: (i, 0))
    return pl.pallas_call(
        body, name=name, grid=(R // tr,),
        in_specs=[pl.BlockSpec((N_DEV, tr, C), lambda i: (0, i, 0)), blk, blk, blk],
        out_specs=[blk] * 4,
        out_shape=[SDS((R, C), F32)] * 4,
        compiler_params=_params(1),
    )(parts, w, m, v)


def _sum_adamw_peers(me, own, parts, w, m, v, name, replicated):
    R, C = w.shape
    tr = _update_rows(R, C)

    def body(me_ref, own_ref, p_ref, w_ref, m_ref, v_ref, g_ref, d_ref, m2_ref, v2_ref):
        if replicated:
            mine = me_ref[0]
            g = None
            for j in range(N_DEV):
                k = jnp.maximum(jnp.bitwise_xor(mine, j) - 1, 0)
                term = jnp.where(mine == j, own_ref[...], p_ref[k])
                g = term if g is None else g + term
        else:
            g = own_ref[...].astype(F32)
            for k in range(N_DEV - 1):
                g = g + p_ref[k].astype(F32)
        g_ref[...] = g
        d_ref[...], m2_ref[...], v2_ref[...] = _adamw_math(g, w_ref[...], m_ref[...], v_ref[...])

    blk = pl.BlockSpec((tr, C), lambda i, me_ref: (i, 0))
    own_spec = blk if replicated else pl.BlockSpec((None, tr, C), lambda i, me_ref: (me_ref[0], i, 0))
    return pl.pallas_call(
        body, name=name,
        grid_spec=pltpu.PrefetchScalarGridSpec(
            num_scalar_prefetch=1, grid=(R // tr,),
            in_specs=[own_spec, pl.BlockSpec((N_DEV - 1, tr, C), lambda i, me_ref: (0, i, 0)), blk, blk, blk],
            out_specs=[blk] * 4),
        out_shape=[SDS((R, C), F32)] * 4,
        compiler_params=_params(1),
    )(me, own, parts, w, m, v)


SMALL = ("ln_v_gain", "ln_v_bias", "w_spatial", "b_spatial", "sinks", "norm_mix_post", "norm_ff_pre", "norm_ff_post")
SMALL_ROWS = {"ln_v_gain": 8, "ln_v_bias": 8, "w_spatial": 1024, "b_spatial": 8, "sinks": 8,
              "norm_mix_post": 8, "norm_ff_pre": 8, "norm_ff_post": 8}
SMALL_PACK_ROWS = 1152


def _pack_small(vals):
    rows = []
    for name in SMALL:
        flat = vals[name].reshape(-1)
        pad = SMALL_ROWS[name] * 128 - flat.shape[0]
        if pad:
            flat = jnp.concatenate([flat, jnp.zeros((pad,), F32)])
        rows.append(flat.reshape(SMALL_ROWS[name], 128))
    rows.append(jnp.zeros((SMALL_PACK_ROWS - sum(SMALL_ROWS.values()), 128), F32))
    return jnp.concatenate(rows, axis=0)


def _unpack_small(packed, shapes):
    out, r = {}, 0
    for name in SMALL:
        n = 1
        for s in shapes[name]:
            n *= s
        out[name] = packed[r:r + SMALL_ROWS[name]].reshape(-1)[:n].reshape(shapes[name])
        r += SMALL_ROWS[name]
    return out


def _rope_rows():
    d = jnp.arange(128) % HEAD
    inv = ROPE_THETA ** (-(2.0 * (d % (ROPE // 2))).astype(F32) / ROPE)
    invf = jnp.where(d < ROPE, inv, 0.0).astype(F32).reshape(1, 128)
    sgn = jnp.where(d < ROPE // 2, -1.0, jnp.where(d < ROPE, 1.0, 0.0)).astype(F32).reshape(1, 128)
    return invf, sgn


def kernel(x, positions, w_in, ln_v_gain, ln_v_bias, w_spatial, b_spatial, sinks, w_a, w_b, w_o, norm_mix_pre, norm_mix_post, w_ff_in, w_ff_out, norm_ff_pre, norm_ff_post, loss_target, m_w_in, m_ln_v_gain, m_ln_v_bias, m_w_spatial, m_b_spatial, m_sinks, m_w_a, m_w_b, m_w_o, m_norm_mix_pre, m_norm_mix_post, m_w_ff_in, m_w_ff_out, m_norm_ff_pre, m_norm_ff_post, v_w_in, v_ln_v_gain, v_ln_v_bias, v_w_spatial, v_b_spatial, v_sinks, v_w_a, v_w_b, v_w_o, v_norm_mix_pre, v_norm_mix_post, v_w_ff_in, v_w_ff_out, v_norm_ff_pre, v_norm_ff_post):
    given = dict(locals())
    T = x.shape[1]
    xt = x[0]
    tgt = loss_target[0]
    bst = b_spatial[0].T
    ws = w_spatial[0]

    me = 4 * lax.axis_index("x") + 2 * lax.axis_index("y") + lax.axis_index("c")
    me_arr = me.astype(jnp.int32).reshape(1)

    def with_own(zone, shard):
        return lax.dynamic_update_slice(zone, shard[None], (me,) + (0,) * shard.ndim)

    rest = ("w_a", "w_b", "w_o", "w_ff_in", "w_ff_out")
    shard = {n: given[n][0].astype(BF16) for n in rest}
    g_one = _gather_first_leg(w_in[0].T.astype(BF16), "gather_in_start")
    cos, sin = _rope_tables(positions.astype(F32).reshape(T, 1), *_rope_rows(), after=g_one[-1])
    h = _rms_pre(xt, norm_mix_pre, after=cos)
    (own_win,), (win8,) = _wait_copies(g_one, h, "gather_in_wait", count=4)
    g_two = _gather_second_leg(win8, "gather_in_pass_start")
    g_rest = _start_copies([shard[n] for n in rest], [GATHER] * len(rest), "gather_rest_start", after=g_two[-1])
    _, (win8,) = _wait_copies(g_two, g_rest[-1], "gather_in_pass_wait", count=3)
    win = with_own(win8, own_win).reshape(IN_W, D)

    proj = _fwd_in(h, win)
    att, qr, kr, probs, psink = _fwd_attn(proj, cos, sin, sinks[0])
    a = _fwd_sgu(proj, ln_v_gain, ln_v_bias, ws, bst, after=att)
    gw = {n: with_own(z, own) for n, own, z in zip(rest, *_wait_copies(g_rest, a, "gather_rest_wait"))}
    wa, wb, wo = (gw[n].reshape(D, D) for n in ("w_a", "w_b", "w_o"))
    wfi3 = gw["w_ff_in"]
    wfo = gw["w_ff_out"].reshape(D_FF, D)
    merged, a2, b2, mix, x1, hf = _fwd_mix(a, att, proj, xt, wa, wb, wo, norm_mix_post, norm_ff_pre)
    f, dy, dff, dg3, loss_part = _fwd_ff(hf, wfi3, wfo, x1, tgt, norm_ff_post)

    df, dx1, dmix, dg2, dg1 = _bwd_ff(dff, f, wfi3, wfo, x1, dy, mix, norm_mix_post, norm_ff_pre)
    dwfi3, dwfo = _wgrad_ff(hf, df, f, dff)
    own_ff = [dwfi3, dwfo.reshape(N_DEV, D_FF // N_DEV, D)]
    x_ff = _start_copies(own_ff, [SCATTER] * 2, "exchange_ff_start")
    da2, db2, dgate, da, datt = _bwd_mix(dmix, proj, a2, b2, wo, wa, wb, after=x_ff[-1])
    dwo, dwa, dwb = _wgrad_mix(merged, dmix, a, da2, att, db2)
    own_mix = [g.reshape(N_DEV, D // N_DEV, D) for g in (dwa, dwb, dwo)]
    x_mix = _start_copies(own_mix, [SCATTER] * 3, "exchange_mix_start")
    dq, dkv, dsink = _bwd_attn(qr, kr, probs, psink, proj, cos, sin, datt, after=x_mix[-1])
    duv, dws, dbs, dlng, dlnb = _bwd_sgu(proj, da, ln_v_gain, ln_v_bias, ws, bst)
    small_grads = {"ln_v_gain": dlng, "ln_v_bias": dlnb, "w_spatial": dws, "b_spatial": dbs, "sinks": dsink[:, :N_Q],
                   "norm_mix_post": dg1, "norm_ff_pre": dg2, "norm_ff_post": dg3}
    x_small = _start_copies([_pack_small(small_grads)], [SPREAD], "exchange_small_start")
    dwin = _wgrad_in(h, duv, dq, dkv, dgate)
    own_in = [dwin.reshape(N_DEV, IN_W // N_DEV, D)]
    x_in = _start_copies(own_in, [SCATTER], "exchange_in_start", after=x_small[-1])
    grad_x, dg0 = _bwd_in(duv, dq, dkv, dgate, win, xt, dx1, norm_mix_pre, after=x_in[-1])

    results = {}

    def update(n, own, parts, transposed=False):
        state = [given[k + n][0].T if transposed else given[k + n][0] for k in ("", "m_", "v_")]
        res = _sum_adamw_peers(me_arr, own, parts, *state, "adamw_" + n, False)
        results[n] = [(r.T if transposed else r).reshape(given[n].shape) for r in res]

    own_ff, p_ff = _wait_copies(x_ff, grad_x, "exchange_ff_wait")
    update("w_ff_in", own_ff[0], p_ff[0])
    update("w_ff_out", own_ff[1], p_ff[1])
    own_mix, p_mix = _wait_copies(x_mix, results["w_ff_out"][0], "exchange_mix_wait")
    for n, own, parts in zip(("w_a", "w_b", "w_o"), own_mix, p_mix):
        update(n, own, parts)
    tail = jnp.concatenate([dg0.reshape(8, 128), jnp.tile(loss_part, (8, 1))], axis=0)
    (tail_all,) = _all_to_all([tail], [True], "exchange_tail", after=results["w_o"][0])
    dg0_all = tail_all[:, :8]
    own_small, p_small = _wait_copies(x_small, tail_all, "exchange_small_wait")
    own_in, p_in = _wait_copies(x_in, p_small[0], "exchange_in_wait")
    update("w_in", own_in[0], p_in[0], transposed=True)
    packed = _sum_adamw_peers(me_arr, own_small[0], p_small[0], _pack_small({n: given[n] for n in SMALL}),
                              _pack_small({n: given["m_" + n] for n in SMALL}),
                              _pack_small({n: given["v_" + n] for n in SMALL}), "adamw_small", True)
    shapes = {n: given[n].shape for n in SMALL}
    unpacked = [_unpack_small(p, shapes) for p in packed]
    for n in SMALL:
        results[n] = [u[n] for u in unpacked]
    n = "norm_mix_pre"
    results[n] = [r.reshape(given[n].shape) for r in _sum_adamw(
        dg0_all, given[n].reshape(8, 128), given["m_" + n].reshape(8, 128), given["v_" + n].reshape(8, 128), "adamw_" + n)]

    loss = jnp.sum(tail_all[:, 8, 0])
    order = ("w_in", "ln_v_gain", "ln_v_bias", "w_spatial", "b_spatial", "sinks", "w_a", "w_b", "w_o", "norm_mix_pre",
             "norm_mix_post", "w_ff_in", "w_ff_out", "norm_ff_pre", "norm_ff_post")
    out = [loss, grad_x.reshape(x.shape)]
    for k in range(4):
        out += [results[n][k] for n in order]
    return tuple(out)
```

```python
import jax
import jax.numpy as jnp
from jax import lax
from jax.experimental import pallas as pl
from jax.experimental.pallas import tpu as pltpu

F32 = jnp.float32
BF16 = jnp.bfloat16

N_DEV = 8
D = 1024
D_FF = 4096
IN_W = 5632
CHUNK = 128
GROUPS = 8
HEAD = 64
N_Q = 16
N_KV = 4
ROPE = 16
ROPE_THETA = 500000.0
EPS = 1e-6
OFF_Q, OFF_K, OFF_VA, OFF_GA, OFF_GB = 2048, 3072, 3328, 3584, 4608

ADAM_LR = 0.001
ADAM_B1 = 0.9
ADAM_B2 = 0.999
ADAM_EPS = 1e-08
ADAM_WD = 0.01
ADAM_STEP = 10

VMEM_LIMIT = 56 * 1024 * 1024

SDS = jax.ShapeDtypeStruct
MESH = pl.DeviceIdType.MESH


def _params(n_axes):
    return pltpu.CompilerParams(dimension_semantics=("arbitrary",) * n_axes, vmem_limit_bytes=VMEM_LIMIT)


def _nt(a, b):
    return lax.dot_general(a, b, (((1,), (1,)), ((), ())), preferred_element_type=F32)


def _tn(a, b):
    return lax.dot_general(a, b, (((0,), (0,)), ((), ())), preferred_element_type=F32)


def _nn(a, b):
    return jnp.dot(a, b, preferred_element_type=F32)


def _gelu(x):
    t = jnp.tanh(0.7978845608028654 * (x + 0.044715 * (x * x * x)))
    return 0.5 * x * (1.0 + t), t


def _gelu_grad(x, t):
    return 0.5 * (1.0 + t) + 0.5 * x * (1.0 - t * t) * (0.7978845608028654 * (1.0 + 3.0 * 0.044715 * x * x))


def _sigmoid(x):
    return 1.0 / (1.0 + jnp.exp(-x))


def _rms_stats(v):
    r = lax.rsqrt(jnp.mean(v * v, axis=-1, keepdims=True) + EPS)
    return r, v * r


def _rms_bwd(d, vhat, r, g):
    gd = g * d
    return r * (gd - vhat * jnp.mean(gd * vhat, axis=-1, keepdims=True))


def _colsum(v):
    return jnp.sum(v, axis=0, keepdims=True)


_ANY = pl.BlockSpec(memory_space=pl.ANY)


def _after(body, n_in, after):
    if after is None:
        return body, [], []

    def ordered(*refs):
        return body(*refs[:n_in], *refs[n_in + 1:])

    return ordered, [_ANY], [after]


def _rms_pre(x, g0, after=None):
    T = x.shape[0]
    tm = min(T, 1024)

    def body(x_ref, g_ref, h_ref):
        _, xh = _rms_stats(x_ref[...])
        h_ref[...] = (xh * g_ref[...]).astype(BF16)

    body, dep_specs, deps = _after(body, 2, after)
    return pl.pallas_call(
        body, name="rms_pre", grid=(T // tm,),
        in_specs=[pl.BlockSpec((tm, D), lambda i: (i, 0)), pl.BlockSpec((1, D), lambda i: (0, 0))] + dep_specs,
        out_specs=pl.BlockSpec((tm, D), lambda i: (i, 0)),
        out_shape=SDS((T, D), BF16),
        compiler_params=_params(1),
    )(x, g0, *deps)


def _fwd_in(h, win_t):
    T = h.shape[0]
    tm, tn = min(T, 512), 1408

    def body(h_ref, w_ref, p_ref):
        for j in range(IN_W // tn):
            cols = slice(j * tn, (j + 1) * tn)
            p_ref[:, cols] = _nt(h_ref[...], w_ref[cols, :]).astype(BF16)

    return pl.pallas_call(
        body, name="fwd_in", grid=(T // tm,),
        in_specs=[pl.BlockSpec((tm, D), lambda i: (i, 0)), _resident((IN_W, D))],
        out_specs=pl.BlockSpec((tm, IN_W), lambda i: (i, 0)),
        out_shape=SDS((T, IN_W), BF16),
        compiler_params=_params(1),
    )(h, win_t)


def _sgu_forward_parts(u_ref, vs_ref, lng_ref, lnb_ref):
    u = u_ref[...].astype(F32)
    vs = vs_ref[...].astype(F32)
    gu, tu = _gelu(u)
    gv, tv = _gelu(vs)
    mu = jnp.mean(gv, axis=-1, keepdims=True)
    dv = gv - mu
    rstd = lax.rsqrt(jnp.mean(dv * dv, axis=-1, keepdims=True) + EPS)
    vhat = dv * rstd
    vn = (vhat * lng_ref[...] + lnb_ref[...]).astype(BF16)
    return u, vs, gu, tu, tv, rstd, vhat, vn


def _masked_ws(ws_ref, g):
    row = lax.broadcasted_iota(jnp.int32, (CHUNK, CHUNK), 0)
    col = lax.broadcasted_iota(jnp.int32, (CHUNK, CHUNK), 1)
    return jnp.where(row >= col, ws_ref[g], 0.0).astype(BF16)


def _fwd_sgu(proj, lng, lnb, ws, bst, after=None):
    T = proj.shape[0]
    tc = min(T, 512)

    def body(u_ref, vs_ref, lng_ref, lnb_ref, ws_ref, bst_ref, a_ref):
        _, _, gu, _, _, _, _, vn = _sgu_forward_parts(u_ref, vs_ref, lng_ref, lnb_ref)
        for g in range(GROUPS):
            wm = _masked_ws(ws_ref, g)
            cols = slice(g * CHUNK, (g + 1) * CHUNK)
            for c in range(tc // CHUNK):
                rows = slice(c * CHUNK, (c + 1) * CHUNK)
                mixed = _nn(wm, vn[rows, cols]) + bst_ref[:, g:g + 1]
                a_ref[rows, cols] = (gu[rows, cols] * mixed).astype(BF16)

    body, dep_specs, deps = _after(body, 6, after)
    return pl.pallas_call(
        body, name="fwd_sgu", grid=(T // tc,),
        in_specs=[pl.BlockSpec((tc, D), lambda i: (i, 0)), pl.BlockSpec((tc, D), lambda i: (i, 1)),
                  pl.BlockSpec((1, D), lambda i: (0, 0)), pl.BlockSpec((1, D), lambda i: (0, 0)),
                  pl.BlockSpec((GROUPS, CHUNK, CHUNK), lambda i: (0, 0, 0)),
                  pl.BlockSpec((CHUNK, GROUPS), lambda i: (0, 0))] + dep_specs,
        out_specs=pl.BlockSpec((tc, D), lambda i: (i, 0)),
        out_shape=SDS((T, D), BF16),
        compiler_params=_params(1),
    )(proj, proj, lng, lnb, ws, bst, *deps)


def _rope_tables(posf, invf, sgn, after=None):
    T = posf.shape[0]
    tr = min(T, 1024)

    def body(pos_ref, invf_ref, sgn_ref, c_ref, s_ref):
        ang = pos_ref[...] * invf_ref[...]
        c_ref[...] = jnp.cos(ang)
        s = jnp.sin(ang)
        s_ref[:, :128] = jnp.where(sgn_ref[...] < 0.0, -s, 0.0)
        s_ref[:, 128:] = jnp.where(sgn_ref[...] > 0.0, s, 0.0)

    body, dep_specs, deps = _after(body, 3, after)
    return pl.pallas_call(
        body, name="rope_tables", grid=(T // tr,),
        in_specs=[pl.BlockSpec((tr, 1), lambda i: (i, 0)), pl.BlockSpec((1, 128), lambda i: (0, 0)),
                  pl.BlockSpec((1, 128), lambda i: (0, 0))] + dep_specs,
        out_specs=[pl.BlockSpec((tr, 128), lambda i: (i, 0)), pl.BlockSpec((tr, 256), lambda i: (i, 0))],
        out_shape=[SDS((T, 128), F32), SDS((T, 256), F32)],
        compiler_params=_params(1),
    )(posf, invf, sgn, *deps)


def _rope(v, c, s):
    v = v.astype(F32)
    return v * c + pltpu.roll(v, 128 - ROPE // 2, 1) * s[:, :128] + pltpu.roll(v, ROPE // 2, 1) * s[:, 128:]


def _rope_bwd(dv, c, s):
    return dv * c + pltpu.roll(dv * s[:, :128], ROPE // 2, 1) + pltpu.roll(dv * s[:, 128:], 128 - ROPE // 2, 1)


def _fold_masks(first):
    jj = lax.broadcasted_iota(jnp.int32, (CHUNK, CHUNK), 0)
    t = lax.broadcasted_iota(jnp.int32, (CHUNK, CHUNK), 1)
    prev = jj > t
    return prev, jnp.where(prev & first, -1e30, 0.0)


def _fold(band, prev):
    return jnp.where(prev, band[:CHUNK], band[CHUNK:])


def _unfold(folded, prev):
    return jnp.concatenate([jnp.where(prev, folded, 0.0), jnp.where(prev, 0.0, folded)], axis=0)


def _softmax_sink(s, sink, key_axis):
    m = jnp.maximum(jnp.max(s, axis=key_axis, keepdims=True), sink)
    p = jnp.exp(s - m)
    esink = jnp.exp(sink - m)
    inv = 1.0 / (jnp.sum(p, axis=key_axis, keepdims=True) + esink)
    return p * inv, esink * inv


def _head_pair_operand(slab, g):
    lo = lax.broadcasted_iota(jnp.int32, slab.shape, 1) < HEAD
    if g % 2 == 0:
        first = jnp.where(lo, slab, 0.0)
        second = pltpu.roll(first, HEAD, 1)
    else:
        second = jnp.where(lo, 0.0, slab)
        first = pltpu.roll(second, HEAD, 1)
    return jnp.concatenate([first, second], axis=0).astype(BF16)


def _head_pair_gradient(acc, g):
    top, bot = acc[:2 * CHUNK], acc[2 * CHUNK:]
    lo = lax.broadcasted_iota(jnp.int32, top.shape, 1) < HEAD
    if g % 2 == 0:
        return jnp.where(lo, top, 0.0) + pltpu.roll(jnp.where(lo, 0.0, bot), HEAD, 1)
    return pltpu.roll(jnp.where(lo, top, 0.0), HEAD, 1) + jnp.where(lo, 0.0, bot)


PAIRS_PER_KV = N_Q // N_KV // 2
KV_W = N_KV * HEAD


def _band(prev_ref, cur_ref, cols=slice(None)):
    return jnp.concatenate([prev_ref[:, cols], cur_ref[:, cols]], axis=0)


def _fwd_attn(proj, cos, sin, sinks):
    T = proj.shape[0]
    nb = T // CHUNK
    cur = lambda i: i
    prev = lambda i: jnp.maximum(i - 1, 0)

    def body(q_ref, kp_ref, kc_ref, vp_ref, vc_ref, cp_ref, cc_ref, sp_ref, sc_ref, sink_ref,
             o_ref, qr_ref, kr_ref, p_ref, psink_ref):
        prev_slot, bias = _fold_masks(pl.program_id(0) == 0)
        c_band, s_band = _band(cp_ref, cc_ref), _band(sp_ref, sc_ref)
        for j in range(KV_W // 128):
            cols = slice(j * 128, (j + 1) * 128)
            k_slab = _rope(_band(kp_ref, kc_ref, cols), c_band, s_band)
            kr_ref[:, cols] = k_slab[CHUNK:].astype(BF16)
            v_slab = _band(vp_ref, vc_ref, cols).astype(F32)
            for g in (2 * j, 2 * j + 1):
                k2 = _head_pair_operand(k_slab, g)
                v2 = _head_pair_operand(v_slab, g)
                pairs = [g * PAIRS_PER_KV + r for r in range(PAIRS_PER_KV)]
                qps = []
                for pair in pairs:
                    lanes = slice(pair * 128, (pair + 1) * 128)
                    qps.append((_rope(q_ref[:, lanes], cc_ref[...], sc_ref[...]) * (HEAD ** -0.5)).astype(BF16))
                    qr_ref[:, lanes] = qps[-1]
                s2 = _nt(k2, jnp.concatenate(qps, axis=0))
                pcols = []
                for r, pair in enumerate(pairs):
                    ps = []
                    for e in range(2):
                        head = 2 * pair + e
                        s = _fold(s2[e * 2 * CHUNK:(e + 1) * 2 * CHUNK, r * 128:(r + 1) * 128], prev_slot) + bias
                        p, psink = _softmax_sink(s, sink_ref[head], 0)
                        p = p.astype(BF16)
                        p_ref[head] = p
                        psink_ref[head:head + 1, :] = psink
                        ps.append(_unfold(p, prev_slot))
                    pcols.append(jnp.concatenate(ps, axis=0))
                o = _tn(jnp.concatenate(pcols, axis=1), v2).astype(BF16)
                for r, pair in enumerate(pairs):
                    o_ref[:, pair * 128:(pair + 1) * 128] = o[r * CHUNK:(r + 1) * CHUNK]

    table = lambda which, width: pl.BlockSpec((CHUNK, width), lambda i: (which(i), 0))
    return pl.pallas_call(
        body, name="fwd_attn", grid=(nb,),
        in_specs=[pl.BlockSpec((CHUNK, D), lambda i: (i, OFF_Q // D)),
                  pl.BlockSpec((CHUNK, KV_W), lambda i: (prev(i), OFF_K // KV_W)),
                  pl.BlockSpec((CHUNK, KV_W), lambda i: (i, OFF_K // KV_W)),
                  pl.BlockSpec((CHUNK, KV_W), lambda i: (prev(i), OFF_VA // KV_W)),
                  pl.BlockSpec((CHUNK, KV_W), lambda i: (i, OFF_VA // KV_W)),
                  table(prev, 128), table(cur, 128), table(prev, 256), table(cur, 256),
                  pl.BlockSpec(memory_space=pltpu.SMEM)],
        out_specs=[pl.BlockSpec((CHUNK, D), lambda i: (i, 0)), pl.BlockSpec((CHUNK, D), lambda i: (i, 0)),
                   pl.BlockSpec((CHUNK, KV_W), lambda i: (i, 0)),
                   pl.BlockSpec((None, N_Q, CHUNK, CHUNK), lambda i: (i, 0, 0, 0)),
                   pl.BlockSpec((None, N_Q, CHUNK), lambda i: (i, 0, 0))],
        out_shape=[SDS((T, D), BF16), SDS((T, D), BF16), SDS((T, KV_W), BF16),
                   SDS((nb, N_Q, CHUNK, CHUNK), BF16), SDS((nb, N_Q, CHUNK), F32)],
        compiler_params=_params(1),
    )(proj, proj, proj, proj, proj, cos, cos, sin, sin, sinks)


def _row_halves(tm):
    return [slice(0, tm // 2), slice(tm // 2, tm)] if tm % 32 == 0 else [slice(0, tm)]


def _fwd_mix(a, att, proj, x, wa, wb, wo, g1, g2):
    T = x.shape[0]
    tm = min(T, 512)
    half = D // 2

    def body(a_ref, att_ref, ga0, ga1, gb0, gb1, x_ref, wa_ref, wb_ref, wo_ref, g1_ref, g2_ref,
             mg_ref, a2_ref, b2_ref, mix_ref, x1_ref, hf_ref):
        for rows in _row_halves(tm):
            a2 = _nn(a_ref[rows, :], wa_ref[...])
            b2 = _nn(att_ref[rows, :], wb_ref[...])
            ga = jnp.concatenate([ga0[rows, :], ga1[rows, :]], axis=1).astype(F32)
            gb = jnp.concatenate([gb0[rows, :], gb1[rows, :]], axis=1).astype(F32)
            merged = (_sigmoid(ga) * a2 + _sigmoid(gb) * b2).astype(BF16)
            a2_ref[rows, :] = a2.astype(BF16)
            b2_ref[rows, :] = b2.astype(BF16)
            mg_ref[rows, :] = merged
            mix = _nn(merged, wo_ref[...])
            mix_ref[rows, :] = mix
            _, mh = _rms_stats(mix)
            x1 = x_ref[rows, :] + mh * g1_ref[...]
            x1_ref[rows, :] = x1
            _, xh = _rms_stats(x1)
            hf_ref[rows, :] = (xh * g2_ref[...]).astype(BF16)

    row = lambda i: (i, 0)
    const = lambda i: (0, 0)
    gspec = lambda off: pl.BlockSpec((tm, half), lambda i: (i, off // half))
    return pl.pallas_call(
        body, name="fwd_mix", grid=(T // tm,),
        in_specs=[pl.BlockSpec((tm, D), row), pl.BlockSpec((tm, D), row),
                  gspec(OFF_GA), gspec(OFF_GA + half), gspec(OFF_GB), gspec(OFF_GB + half),
                  pl.BlockSpec((tm, D), row), _resident((D, D)), _resident((D, D)),
                  _resident((D, D)), pl.BlockSpec((1, D), const), pl.BlockSpec((1, D), const)],
        out_specs=[pl.BlockSpec((tm, D), row)] * 6,
        out_shape=[SDS((T, D), BF16), SDS((T, D), BF16), SDS((T, D), BF16), SDS((T, D), F32), SDS((T, D), F32),
                   SDS((T, D), BF16)],
        compiler_params=_params(1),
    )(a, att, proj, proj, proj, proj, x, wa, wb, wo, g1, g2)


FF_SPLIT = N_DEV
FF_TILE = D_FF // FF_SPLIT


def _fwd_ff(hf, wfi3, wfo, x1, tgt, g3):
    T = hf.shape[0]
    tm = min(T, 512)

    def body(hf_ref, wfi_ref, wfo_ref, x1_ref, tgt_ref, g3_ref, f_ref, dy_ref, dff_ref, dg3_ref, loss_ref, r_s):
        @pl.when(pl.program_id(0) == 0)
        def _():
            dg3_ref[...] = jnp.zeros_like(dg3_ref)
            loss_ref[...] = jnp.zeros_like(loss_ref)

        hf_t = hf_ref[...]
        for s in range(FF_SPLIT):
            cols = slice(s * FF_TILE, (s + 1) * FF_TILE)
            f = _nn(hf_t, wfi_ref[s]).astype(BF16)
            f_ref[:, cols] = f
            rl = jnp.maximum(f.astype(F32), 0.0)
            r_s[:, cols] = (rl * rl).astype(BF16)
        r3, fh = _rms_stats(_nn(r_s[...], wfo_ref[...]))
        e = x1_ref[...] + fh * g3_ref[...] - tgt_ref[...]
        loss_ref[...] += jnp.sum(e * e) * (0.5 / D)
        dy = e * (1.0 / D)
        dy_ref[...] = dy
        dg3_ref[...] += _colsum(dy * fh)
        dff_ref[...] = _rms_bwd(dy, fh, r3, g3_ref[...]).astype(BF16)

    row = lambda i: (i, 0)
    const = lambda i: (0, 0)
    return pl.pallas_call(
        body, name="fwd_ff", grid=(T // tm,),
        in_specs=[pl.BlockSpec((tm, D), row), _resident((FF_SPLIT, D, FF_TILE)), _resident((D_FF, D)),
                  pl.BlockSpec((tm, D), row),
                  pl.BlockSpec((tm, D), row), pl.BlockSpec((1, D), const)],
        out_specs=[pl.BlockSpec((tm, D_FF), row), pl.BlockSpec((tm, D), row),
                   pl.BlockSpec((tm, D), row), pl.BlockSpec((1, D), const), pl.BlockSpec((1, 128), const)],
        out_shape=[SDS((T, D_FF), BF16), SDS((T, D), F32), SDS((T, D), BF16), SDS((1, D), F32), SDS((1, 128), F32)],
        scratch_shapes=[pltpu.VMEM((tm, D_FF), BF16)],
        compiler_params=_params(1),
    )(hf, wfi3, wfo, x1, tgt, g3)


def _bwd_ff(dff, f, wfi3, wfo, x1, dy, mix, g1, g2):
    T = dff.shape[0]
    tm = min(T, 256)

    def body(dff_ref, f_ref, wfi_ref, wfo_ref, x1_ref, dy_ref, mix_ref, g1_ref, g2_ref,
             df_ref, dx1_ref, dmix_ref, dg2_ref, dg1_ref):
        @pl.when(pl.program_id(0) == 0)
        def _():
            dg2_ref[...] = jnp.zeros_like(dg2_ref)
            dg1_ref[...] = jnp.zeros_like(dg1_ref)

        dff_t = dff_ref[...]
        dhf = None
        for s in range(FF_SPLIT):
            cols = slice(s * FF_TILE, (s + 1) * FF_TILE)
            dr = _nt(dff_t, wfo_ref[cols, :])
            df = (dr * (2.0 * jnp.maximum(f_ref[:, cols].astype(F32), 0.0))).astype(BF16)
            df_ref[:, cols] = df
            part = _nt(df, wfi_ref[s])
            dhf = part if dhf is None else dhf + part
        r2, xh = _rms_stats(x1_ref[...])
        dg2_ref[...] += _colsum(dhf * xh)
        dx1 = dy_ref[...] + _rms_bwd(dhf, xh, r2, g2_ref[...])
        dx1_ref[...] = dx1
        r1, mh = _rms_stats(mix_ref[...])
        dg1_ref[...] += _colsum(dx1 * mh)
        dmix_ref[...] = _rms_bwd(dx1, mh, r1, g1_ref[...]).astype(BF16)

    row = lambda i: (i, 0)
    const = lambda i: (0, 0)
    return pl.pallas_call(
        body, name="bwd_ff", grid=(T // tm,),
        in_specs=[pl.BlockSpec((tm, D), row), pl.BlockSpec((tm, D_FF), row),
                  _resident((FF_SPLIT, D, FF_TILE)), _resident((D_FF, D)),
                  pl.BlockSpec((tm, D), row), pl.BlockSpec((tm, D), row), pl.BlockSpec((tm, D), row),
                  pl.BlockSpec((1, D), const), pl.BlockSpec((1, D), const)],
        out_specs=[pl.BlockSpec((tm, D_FF), row), pl.BlockSpec((tm, D), row),
                   pl.BlockSpec((tm, D), row), pl.BlockSpec((1, D), const), pl.BlockSpec((1, D), const)],
        out_shape=[SDS((T, D_FF), BF16), SDS((T, D), F32), SDS((T, D), BF16), SDS((1, D), F32), SDS((1, D), F32)],
        compiler_params=_params(1),
    )(dff, f, wfi3, wfo, x1, dy, mix, g1, g2)


def _wgrad_ff(hf, df, f, dff):
    T = hf.shape[0]
    tt = min(T, 1024)
    wide = 2 * FF_TILE

    def body(hf_ref, df_ref, f_ref, dff_ref, dwfi_ref, dwfo_ref, acc_i, acc_o):
        t = pl.program_id(1)

        @pl.when(t == 0)
        def _():
            acc_i[...] = jnp.zeros_like(acc_i)
            acc_o[...] = jnp.zeros_like(acc_o)

        acc_i[...] += _tn(hf_ref[...], df_ref[...])
        rl = jnp.maximum(f_ref[...].astype(F32), 0.0)
        acc_o[...] += _tn((rl * rl).astype(BF16), dff_ref[...])

        @pl.when(t == T // tt - 1)
        def _():
            dwfi_ref[0] = acc_i[:, :FF_TILE].astype(BF16)
            dwfi_ref[1] = acc_i[:, FF_TILE:].astype(BF16)
            dwfo_ref[...] = acc_o[...].astype(BF16)

    return pl.pallas_call(
        body, name="wgrad_ff", grid=(D_FF // wide, T // tt),
        in_specs=[pl.BlockSpec((tt, D), lambda p, t: (t, 0)), pl.BlockSpec((tt, wide), lambda p, t: (t, p)),
                  pl.BlockSpec((tt, wide), lambda p, t: (t, p)), pl.BlockSpec((tt, D), lambda p, t: (t, 0))],
        out_specs=[pl.BlockSpec((2, D, FF_TILE), lambda p, t: (p, 0, 0)), pl.BlockSpec((wide, D), lambda p, t: (p, 0))],
        out_shape=[SDS((FF_SPLIT, D, FF_TILE), BF16), SDS((D_FF, D), BF16)],
        scratch_shapes=[pltpu.VMEM((D, wide), F32), pltpu.VMEM((wide, D), F32)],
        compiler_params=_params(2),
    )(hf, df, f, dff)


def _bwd_mix(dmix, proj, a2, b2, wo, wa, wb, after=None):
    T = dmix.shape[0]
    tm = min(T, 512)
    half = D // 2

    def body(dmix_ref, ga0, ga1, gb0, gb1, a2_ref, b2_ref, wo_ref, wa_ref, wb_ref,
             da2_ref, db2_ref, dg_ref, da_ref, datt_ref):
        for rows in _row_halves(tm):
            dmg = _nt(dmix_ref[rows, :], wo_ref[...])
            sa = _sigmoid(jnp.concatenate([ga0[rows, :], ga1[rows, :]], axis=1).astype(F32))
            sb = _sigmoid(jnp.concatenate([gb0[rows, :], gb1[rows, :]], axis=1).astype(F32))
            da2 = (dmg * sa).astype(BF16)
            db2 = (dmg * sb).astype(BF16)
            da2_ref[rows, :] = da2
            db2_ref[rows, :] = db2
            dg_ref[rows, :D] = (dmg * a2_ref[rows, :].astype(F32) * (sa * (1.0 - sa))).astype(BF16)
            dg_ref[rows, D:] = (dmg * b2_ref[rows, :].astype(F32) * (sb * (1.0 - sb))).astype(BF16)
            da_ref[rows, :] = _nt(da2, wa_ref[...]).astype(BF16)
            datt_ref[rows, :] = _nt(db2, wb_ref[...]).astype(BF16)

    row = lambda i: (i, 0)
    const = lambda i: (0, 0)
    gspec = lambda off: pl.BlockSpec((tm, half), lambda i: (i, off // half))
    body, dep_specs, deps = _after(body, 10, after)
    return pl.pallas_call(
        body, name="bwd_mix", grid=(T // tm,),
        in_specs=[pl.BlockSpec((tm, D), row), gspec(OFF_GA), gspec(OFF_GA + half), gspec(OFF_GB), gspec(OFF_GB + half),
                  pl.BlockSpec((tm, D), row), pl.BlockSpec((tm, D), row),
                  _resident((D, D)), _resident((D, D)), _resident((D, D))] + dep_specs,
        out_specs=[pl.BlockSpec((tm, D), row), pl.BlockSpec((tm, D), row), pl.BlockSpec((tm, 2 * D), row),
                   pl.BlockSpec((tm, D), row), pl.BlockSpec((tm, D), row)],
        out_shape=[SDS((T, D), BF16), SDS((T, D), BF16), SDS((T, 2 * D), BF16), SDS((T, D), BF16), SDS((T, D), BF16)],
        compiler_params=_params(1),
    )(dmix, proj, proj, proj, proj, a2, b2, wo, wa, wb, *deps)


def _wgrad_mix(merged, dmix, a, da2, att, db2):
    T = merged.shape[0]
    tt = min(T, 512)

    def body(mg_ref, dmix_ref, a_ref, da2_ref, att_ref, db2_ref, dwo_ref, dwa_ref, dwb_ref, acc):
        t = pl.program_id(0)

        @pl.when(t == 0)
        def _():
            acc[...] = jnp.zeros_like(acc)

        acc[0] += _tn(mg_ref[...], dmix_ref[...])
        acc[1] += _tn(a_ref[...], da2_ref[...])
        acc[2] += _tn(att_ref[...], db2_ref[...])

        @pl.when(t == T // tt - 1)
        def _():
            dwo_ref[...] = acc[0].astype(BF16)
            dwa_ref[...] = acc[1].astype(BF16)
            dwb_ref[...] = acc[2].astype(BF16)

    return pl.pallas_call(
        body, name="wgrad_mix", grid=(T // tt,),
        in_specs=[pl.BlockSpec((tt, D), lambda t: (t, 0))] * 6,
        out_specs=[pl.BlockSpec((D, D), lambda t: (0, 0))] * 3,
        out_shape=[SDS((D, D), BF16)] * 3,
        scratch_shapes=[pltpu.VMEM((3, D, D), F32)],
        compiler_params=_params(1),
    )(merged, dmix, a, da2, att, db2)


def _bwd_attn(qr, kr, probs, psink, proj, cos, sin, datt, after=None):
    T = proj.shape[0]
    nb = T // CHUNK
    cur = lambda i: jnp.minimum(i, nb - 1)
    prev = lambda i: jnp.maximum(jnp.minimum(i, nb - 1) - 1, 0)

    def body(q_ref, kp_ref, kc_ref, vp_ref, vc_ref, cp_ref, cc_ref, sp_ref, sc_ref, p_ref, psink_ref, do_ref,
             dq_ref, dkv_ref, dsink_ref, carry_k, carry_v):
        i = pl.program_id(0)

        @pl.when(i == 0)
        def _():
            carry_k[...] = jnp.zeros_like(carry_k)
            carry_v[...] = jnp.zeros_like(carry_v)
            dsink_ref[...] = jnp.zeros_like(dsink_ref)

        @pl.when(i < nb)
        def _():
            prev_slot, _ = _fold_masks(i == 0)
            c_band, s_band = _band(cp_ref, cc_ref), _band(sp_ref, sc_ref)
            lane = lax.broadcasted_iota(jnp.int32, (1, 128), 1)
            dsink = jnp.zeros((1, 128), F32)
            for j in range(KV_W // 128):
                cols = slice(j * 128, (j + 1) * 128)
                k_slab = _band(kp_ref, kc_ref, cols).astype(F32)
                v_slab = _band(vp_ref, vc_ref, cols).astype(F32)
                dk_slab = jnp.zeros((2 * CHUNK, 128), F32)
                dv_slab = jnp.zeros((2 * CHUNK, 128), F32)
                for g in (2 * j, 2 * j + 1):
                    k2 = _head_pair_operand(k_slab, g)
                    v2 = _head_pair_operand(v_slab, g)
                    pairs = [g * PAIRS_PER_KV + r for r in range(PAIRS_PER_KV)]
                    q_stack = jnp.concatenate([q_ref[:, pr * 128:(pr + 1) * 128] for pr in pairs], axis=0)
                    do_stack = jnp.concatenate([do_ref[:, pr * 128:(pr + 1) * 128] for pr in pairs], axis=0)
                    dp2 = _nt(v2, do_stack)
                    pcols, dscols = [], []
                    for r, pair in enumerate(pairs):
                        ps, dss = [], []
                        for e in range(2):
                            head = 2 * pair + e
                            p_b = p_ref[head]
                            p = p_b.astype(F32)
                            dp = _fold(dp2[e * 2 * CHUNK:(e + 1) * 2 * CHUNK, r * 128:(r + 1) * 128], prev_slot)
                            delta = jnp.sum(p * dp, axis=0, keepdims=True)
                            ps.append(_unfold(p_b, prev_slot))
                            dss.append(_unfold((p * (dp - delta)).astype(BF16), prev_slot))
                            dsink = dsink + jnp.where(lane == head, -jnp.sum(psink_ref[head:head + 1, :] * delta), 0.0)
                        pcols.append(jnp.concatenate(ps, axis=0))
                        dscols.append(jnp.concatenate(dss, axis=0))
                    ds2 = jnp.concatenate(dscols, axis=1)
                    dq = _tn(ds2, k2) * (HEAD ** -0.5)
                    for r, pair in enumerate(pairs):
                        dq_ref[:, pair * 128:(pair + 1) * 128] = _rope_bwd(
                            dq[r * CHUNK:(r + 1) * CHUNK], cc_ref[...], sc_ref[...]).astype(BF16)
                    dk_slab = dk_slab + _head_pair_gradient(_nn(ds2, q_stack), g)
                    dv_slab = dv_slab + _head_pair_gradient(_nn(jnp.concatenate(pcols, axis=1), do_stack), g)
                dk_slab = _rope_bwd(dk_slab, c_band, s_band)
                vcols = slice(KV_W + j * 128, KV_W + (j + 1) * 128)
                dkv_ref[:, cols] = (carry_k[:, cols] + dk_slab[:CHUNK]).astype(BF16)
                dkv_ref[:, vcols] = (carry_v[:, cols] + dv_slab[:CHUNK]).astype(BF16)
                carry_k[:, cols] = dk_slab[CHUNK:]
                carry_v[:, cols] = dv_slab[CHUNK:]
            dsink_ref[...] += dsink

        @pl.when(i == nb)
        def _():
            dkv_ref[:, :KV_W] = carry_k[...].astype(BF16)
            dkv_ref[:, KV_W:] = carry_v[...].astype(BF16)

    table = lambda which, width: pl.BlockSpec((CHUNK, width), lambda i: (which(i), 0))
    body, dep_specs, deps = _after(body, 12, after)
    return pl.pallas_call(
        body, name="bwd_attn", grid=(nb + 1,),
        in_specs=[pl.BlockSpec((CHUNK, D), lambda i: (cur(i), 0)),
                  pl.BlockSpec((CHUNK, KV_W), lambda i: (prev(i), 0)),
                  pl.BlockSpec((CHUNK, KV_W), lambda i: (cur(i), 0)),
                  pl.BlockSpec((CHUNK, KV_W), lambda i: (prev(i), OFF_VA // KV_W)),
                  pl.BlockSpec((CHUNK, KV_W), lambda i: (cur(i), OFF_VA // KV_W)),
                  table(prev, 128), table(cur, 128), table(prev, 256), table(cur, 256),
                  pl.BlockSpec((None, N_Q, CHUNK, CHUNK), lambda i: (cur(i), 0, 0, 0)),
                  pl.BlockSpec((None, N_Q, CHUNK), lambda i: (cur(i), 0, 0)),
                  pl.BlockSpec((CHUNK, D), lambda i: (cur(i), 0))] + dep_specs,
        out_specs=[pl.BlockSpec((CHUNK, D), lambda i: (cur(i), 0)),
                   pl.BlockSpec((CHUNK, 2 * KV_W), lambda i: (jnp.maximum(i - 1, 0), 0)),
                   pl.BlockSpec((1, 128), lambda i: (0, 0))],
        out_shape=[SDS((T, D), BF16), SDS((T, 2 * KV_W), BF16), SDS((1, 128), F32)],
        scratch_shapes=[pltpu.VMEM((CHUNK, KV_W), F32), pltpu.VMEM((CHUNK, KV_W), F32)],
        compiler_params=_params(1),
    )(qr, kr, kr, proj, proj, cos, cos, sin, sin, probs, psink, datt, *deps)


def _bwd_sgu(proj, da, lng, lnb, ws, bst):
    T = proj.shape[0]
    tc = min(T, 512)
    nsteps = T // tc

    def body(u_ref, vs_ref, da_ref, lng_ref, lnb_ref, ws_ref, bst_ref,
             duv_ref, dws_ref, dbs_ref, dlng_ref, dlnb_ref, dvn_s, dgu_s, dmx_sum):
        i = pl.program_id(0)

        @pl.when(i == 0)
        def _():
            dws_ref[...] = jnp.zeros_like(dws_ref)
            dlng_ref[...] = jnp.zeros_like(dlng_ref)
            dlnb_ref[...] = jnp.zeros_like(dlnb_ref)
            dmx_sum[...] = jnp.zeros_like(dmx_sum)

        u, vs, gu, tu, tv, rstd, vhat, vn = _sgu_forward_parts(u_ref, vs_ref, lng_ref, lnb_ref)
        da = da_ref[...].astype(F32)
        for g in range(GROUPS):
            wm = _masked_ws(ws_ref, g)
            cols = slice(g * CHUNK, (g + 1) * CHUNK)
            dws = jnp.zeros((CHUNK, CHUNK), F32)
            dsum = jnp.zeros((CHUNK, CHUNK), F32)
            for c in range(tc // CHUNK):
                rows = slice(c * CHUNK, (c + 1) * CHUNK)
                vn_cg = vn[rows, cols]
                mixed = _nn(wm, vn_cg) + bst_ref[:, g:g + 1]
                dgu_s[rows, cols] = da[rows, cols] * mixed
                dmx = da[rows, cols] * gu[rows, cols]
                dmxb = dmx.astype(BF16)
                dws = dws + _nt(dmxb, vn_cg)
                dsum = dsum + dmx
                dvn_s[rows, cols] = _tn(wm, dmxb)
            dws_ref[g] += dws
            dmx_sum[:, cols] += dsum
        dvn = dvn_s[...]
        dlng_ref[...] += _colsum(dvn * vhat)
        dlnb_ref[...] += _colsum(dvn)
        dvh = dvn * lng_ref[...]
        dgv = rstd * (dvh - jnp.mean(dvh, axis=-1, keepdims=True) - vhat * jnp.mean(dvh * vhat, axis=-1, keepdims=True))
        duv_ref[:, :D] = (dgu_s[...] * _gelu_grad(u, tu)).astype(BF16)
        duv_ref[:, D:] = (dgv * _gelu_grad(vs, tv)).astype(BF16)

        @pl.when(i == nsteps - 1)
        def _():
            row = lax.broadcasted_iota(jnp.int32, (CHUNK, CHUNK), 0)
            col = lax.broadcasted_iota(jnp.int32, (CHUNK, CHUNK), 1)
            for g in range(GROUPS):
                dws_ref[g] = jnp.where(row >= col, dws_ref[g], 0.0)
                dbs_ref[g:g + 1, :] = _colsum(dmx_sum[:, g * CHUNK:(g + 1) * CHUNK].T)

    const2 = lambda i: (0, 0)
    return pl.pallas_call(
        body, name="bwd_sgu", grid=(nsteps,),
        in_specs=[pl.BlockSpec((tc, D), lambda i: (i, 0)), pl.BlockSpec((tc, D), lambda i: (i, 1)),
                  pl.BlockSpec((tc, D), lambda i: (i, 0)), pl.BlockSpec((1, D), const2), pl.BlockSpec((1, D), const2),
                  pl.BlockSpec((GROUPS, CHUNK, CHUNK), lambda i: (0, 0, 0)), pl.BlockSpec((CHUNK, GROUPS), const2)],
        out_specs=[pl.BlockSpec((tc, 2 * D), lambda i: (i, 0)), pl.BlockSpec((GROUPS, CHUNK, CHUNK), lambda i: (0, 0, 0)),
                   pl.BlockSpec((GROUPS, CHUNK), const2), pl.BlockSpec((1, D), const2), pl.BlockSpec((1, D), const2)],
        out_shape=[SDS((T, 2 * D), BF16), SDS((GROUPS, CHUNK, CHUNK), F32), SDS((GROUPS, CHUNK), F32),
                   SDS((1, D), F32), SDS((1, D), F32)],
        scratch_shapes=[pltpu.VMEM((tc, D), F32), pltpu.VMEM((tc, D), F32), pltpu.VMEM((CHUNK, D), F32)],
        compiler_params=_params(1),
    )(proj, proj, da, lng, lnb, ws, bst)


IN_SEG_WIDTHS = (2 * D, D, 2 * N_KV * HEAD, 2 * D)


def _resident(shape):
    return pl.BlockSpec(shape, lambda *_: (0,) * len(shape), pipeline_mode=pl.Buffered(1))


def _bwd_in(duv, dq, dkv, dg, win_t, x, dx1, g0, after=None):
    T = x.shape[0]
    tm = min(T, 512)

    def body(duv_ref, dq_ref, dkv_ref, dg_ref, w_ref, x_ref, dx1_ref, g0_ref, gx_ref, dg0_ref):
        @pl.when(pl.program_id(0) == 0)
        def _():
            dg0_ref[...] = jnp.zeros_like(dg0_ref)

        dh, off = None, 0
        for ref, width in zip((duv_ref, dq_ref, dkv_ref, dg_ref), IN_SEG_WIDTHS):
            part = _nn(ref[...], w_ref[off:off + width, :])
            dh = part if dh is None else dh + part
            off += width
        r0, xh = _rms_stats(x_ref[...])
        dg0_ref[...] += _colsum(dh * xh)
        gx_ref[...] = dx1_ref[...] + _rms_bwd(dh, xh, r0, g0_ref[...])

    row = lambda i: (i, 0)
    body, dep_specs, deps = _after(body, 8, after)
    return pl.pallas_call(
        body, name="bwd_in", grid=(T // tm,),
        in_specs=[pl.BlockSpec((tm, w), row) for w in IN_SEG_WIDTHS] + [
            _resident((IN_W, D)), pl.BlockSpec((tm, D), row), pl.BlockSpec((tm, D), row),
            pl.BlockSpec((1, D), lambda i: (0, 0))] + dep_specs,
        out_specs=[pl.BlockSpec((tm, D), row), pl.BlockSpec((1, D), lambda i: (0, 0))],
        out_shape=[SDS((T, D), F32), SDS((1, D), F32)],
        compiler_params=_params(1),
    )(duv, dq, dkv, dg, win_t, x, dx1, g0, *deps)


def _wgrad_rows(h, segs, first_row, into, name):
    T = h.shape[0]
    tt = min(T, 1024)
    widths = [s.shape[1] for s in segs]
    rows = sum(widths)
    n_in = 1 + len(segs) + (into is not None)

    def body(*refs):
        h_ref, seg_refs = refs[0], refs[1:1 + len(segs)]
        dw_ref, acc, stage, sem = refs[n_in], refs[n_in + 1], refs[n_in + 2], refs[n_in + 3]
        t = pl.program_id(0)

        @pl.when(t == 0)
        def _():
            acc[...] = jnp.zeros_like(acc)

        off = 0
        for ref, width in zip(seg_refs, widths):
            acc[off:off + width, :] += _tn(ref[...], h_ref[...])
            off += width

        @pl.when(t == T // tt - 1)
        def _():
            stage[...] = acc[...].astype(BF16)
            out = pltpu.make_async_copy(stage, dw_ref.at[pl.ds(first_row, rows)], sem)
            out.start()
            out.wait()

    row = lambda t: (t, 0)
    return pl.pallas_call(
        body, name=name, grid=(T // tt,),
        in_specs=[pl.BlockSpec((tt, D), row)] + [pl.BlockSpec((tt, w), row) for w in widths] + [_ANY] * (into is not None),
        out_specs=_ANY,
        out_shape=SDS((IN_W, D), BF16),
        input_output_aliases={} if into is None else {n_in - 1: 0},
        scratch_shapes=[pltpu.VMEM((rows, D), F32), pltpu.VMEM((rows, D), BF16), pltpu.SemaphoreType.DMA],
        compiler_params=_params(1),
    )(h, *segs, *([] if into is None else [into]))


def _wgrad_in(h, duv, dq, dkv, dg):
    dw = _wgrad_rows(h, [dg], IN_SEG_WIDTHS[0] + IN_SEG_WIDTHS[1] + IN_SEG_WIDTHS[2], None, "wgrad_in_gates")
    dw = _wgrad_rows(h, [duv], 0, dw, "wgrad_in_uv")
    return _wgrad_rows(h, [dq, dkv], IN_SEG_WIDTHS[0], dw, "wgrad_in_qkv")


def _place():
    x, y, c = lax.axis_index("x"), lax.axis_index("y"), lax.axis_index("c")
    return x, y, c, 4 * x + 2 * y + c


def _peers(x, y, c):
    out = []
    for mask in range(1, N_DEV):
        px = 1 - x if mask & 4 else x
        py = 1 - y if mask & 2 else y
        pc = 1 - c if mask & 1 else c
        out.append(((px, py, pc), 4 * px + 2 * py + pc))
    return out


def _all_to_all(arrays, gather, name, after=None):
    n = len(arrays)

    def body(*refs):
        ins, outs = refs[:n], refs[n:2 * n]
        send_sems, recv_sems, local_sems = refs[2 * n:]
        x, y, c, me = _place()
        local, sends, recvs = [], [], []
        for a in range(n):
            src_own = ins[a] if gather[a] else ins[a].at[me]
            local.append(pltpu.make_async_copy(src_own, outs[a].at[me], local_sems.at[a]))
            for k, (peer, pid) in enumerate(_peers(x, y, c)):
                sem = a * (N_DEV - 1) + k
                src = ins[a] if gather[a] else ins[a].at[pid]
                sends.append(pltpu.make_async_remote_copy(
                    src_ref=src, dst_ref=outs[a].at[me], send_sem=send_sems.at[sem], recv_sem=recv_sems.at[sem],
                    device_id=peer, device_id_type=MESH))
                recvs.append(pltpu.make_async_remote_copy(
                    src_ref=src, dst_ref=outs[a].at[pid], send_sem=send_sems.at[sem], recv_sem=recv_sems.at[sem],
                    device_id=peer, device_id_type=MESH))
        for cp in local + sends:
            cp.start()
        for cp in recvs:
            cp.wait_recv()
        for cp in sends:
            cp.wait_send()
        for cp in local:
            cp.wait()

    out_shape = [SDS((N_DEV,) + a.shape if gt else a.shape, a.dtype) for a, gt in zip(arrays, gather)]
    nsem = n * (N_DEV - 1)
    body, dep_specs, deps = _after(body, n, after)
    return pl.pallas_call(
        body, name=name,
        in_specs=[pl.BlockSpec(memory_space=pl.ANY)] * n + dep_specs,
        out_specs=[pl.BlockSpec(memory_space=pl.ANY)] * n,
        out_shape=out_shape,
        scratch_shapes=[pltpu.SemaphoreType.DMA((nsem,)), pltpu.SemaphoreType.DMA((nsem,)), pltpu.SemaphoreType.DMA((n,))],
    )(*arrays, *deps)


_HBM = pl.BlockSpec(memory_space=pltpu.HBM)
_SEM = pl.BlockSpec(memory_space=pltpu.SEMAPHORE)
_EFFECT = pltpu.SideEffectType.DATAFLOW_SIDE_EFFECTING
GATHER = "gather"
SCATTER = "scatter"
SPREAD = "spread"


def _zone_shape(a, mode):
    if mode == GATHER:
        return (N_DEV,) + a.shape
    return (N_DEV - 1,) + (a.shape[1:] if mode == SCATTER else a.shape)


def _start_copies(arrays, modes, name, after=None):
    n = len(arrays)
    zones = [lax.empty(_zone_shape(a, m), a.dtype) for a, m in zip(arrays, modes)]

    def body(*refs):
        ins, lands = refs[:n], refs[n:2 * n]
        send_sems, recv_sems = refs[-2 * n - 3], refs[-2 * n - 2]
        token = refs[-1]
        x, y, c, me = _place()
        for a in range(n):
            for k, (peer, pid) in enumerate(_peers(x, y, c)):
                src = ins[a].at[pid] if modes[a] == SCATTER else ins[a]
                dst = lands[a].at[me] if modes[a] == GATHER else lands[a].at[k]
                pltpu.make_async_remote_copy(src_ref=src, dst_ref=dst, send_sem=send_sems.at[a], recv_sem=recv_sems.at[a],
                                             device_id=peer, device_id_type=MESH).start()
        token[...] = jnp.zeros_like(token)

    hbm = lambda a: pltpu.HBM(a.shape, a.dtype)
    sems = pltpu.SemaphoreType.DMA((n,))
    extra = [] if after is None else [after]
    operands = [pltpu.with_memory_space_constraint(a, pltpu.HBM) for a in list(arrays) + zones]
    res = pl.pallas_call(
        body, name=name,
        out_shape=(sems, sems, *[hbm(a) for a in arrays], *[hbm(z) for z in zones], SDS((8, 128), F32)),
        in_specs=[_HBM] * (2 * n) + [_ANY] * len(extra),
        out_specs=(_SEM, _SEM, *[_HBM] * (2 * n), pl.BlockSpec(memory_space=pltpu.VMEM)),
        input_output_aliases={i: 2 + i for i in range(2 * n)},
        compiler_params=pltpu.CompilerParams(has_side_effects=_EFFECT),
    )(*operands, *extra)
    return res[0], res[1], list(res[2:2 + n]), list(res[2 + n:2 + 2 * n]), res[-1]


def _wait_copies(started, after, name, count=N_DEV - 1):
    send_sems, recv_sems, thru, zones, _ = started
    nt, nz = len(thru), len(zones)

    def body(*refs):
        lands = refs[nt:nt + nz]
        send_ref, recv_ref = refs[nt + nz], refs[nt + nz + 1]
        x, y, c, _ = _place()
        for a in range(nz):
            blocks = lands[a].at[pl.ds(0, count)]
            cp = pltpu.make_async_remote_copy(src_ref=blocks, dst_ref=blocks, send_sem=send_ref.at[a], recv_sem=recv_ref.at[a],
                                              device_id=(x, y, 1 - c), device_id_type=MESH)
            cp.wait_send()
            cp.wait_recv()

    hbm = lambda a: pltpu.HBM(a.shape, a.dtype)
    res = pl.pallas_call(
        body, name=name,
        out_shape=tuple(hbm(a) for a in thru + zones),
        in_specs=[_HBM] * (nt + nz) + [_SEM, _SEM, _ANY],
        out_specs=tuple([_HBM] * (nt + nz)),
        input_output_aliases={i: i for i in range(nt + nz)},
        compiler_params=pltpu.CompilerParams(has_side_effects=_EFFECT),
    )(*thru, *zones, send_sems, recv_sems, after)
    return list(res[:nt]), list(res[nt:])


def _split_start(body, arrays, zones, name, after):
    n = len(arrays) + len(zones)
    hbm = lambda a: pltpu.HBM(a.shape, a.dtype)
    sems = pltpu.SemaphoreType.DMA((max(len(zones), 1),))
    extra = [] if after is None else [after]
    operands = [pltpu.with_memory_space_constraint(a, pltpu.HBM) for a in list(arrays) + list(zones)]
    res = pl.pallas_call(
        body, name=name,
        out_shape=(sems, sems, *[hbm(a) for a in operands], SDS((8, 128), F32)),
        in_specs=[_HBM] * n + [_ANY] * len(extra),
        out_specs=(_SEM, _SEM, *[_HBM] * n, pl.BlockSpec(memory_space=pltpu.VMEM)),
        input_output_aliases={i: 2 + i for i in range(n)},
        compiler_params=pltpu.CompilerParams(has_side_effects=_EFFECT),
    )(*operands, *extra)
    return res[0], res[1], list(res[2:2 + len(arrays)]), list(res[2 + len(arrays):2 + n]), res[-1]


def _gather_first_leg(shard, name, after=None):
    zone = lax.empty((N_DEV,) + shard.shape, shard.dtype)
    extra = 0 if after is None else 1

    def body(*refs):
        src, land = refs[0], refs[1]
        send_sem, recv_sem, token = refs[2 + extra], refs[3 + extra], refs[-1]
        x, y, c, me = _place()
        for peer in ((x, y, 1 - c), (1 - x, y, c), (x, 1 - y, c), (1 - x, 1 - y, c)):
            pltpu.make_async_remote_copy(src_ref=src, dst_ref=land.at[me], send_sem=send_sem.at[0], recv_sem=recv_sem.at[0],
                                         device_id=peer, device_id_type=MESH).start()
        token[...] = jnp.zeros_like(token)

    return _split_start(body, [shard], [zone], name, after)


def _gather_second_leg(zone, name, after=None):
    extra = 0 if after is None else 1

    def body(*refs):
        land = refs[0]
        send_sem, recv_sem, token = refs[1 + extra], refs[2 + extra], refs[-1]
        x, y, c, _ = _place()
        for px, py in ((1 - x, y), (x, 1 - y), (1 - x, 1 - y)):
            slot = 4 * px + 2 * py + c
            pltpu.make_async_remote_copy(src_ref=land.at[slot], dst_ref=land.at[slot], send_sem=send_sem.at[0],
                                         recv_sem=recv_sem.at[0], device_id=(x, y, 1 - c), device_id_type=MESH).start()
        token[...] = jnp.zeros_like(token)

    return _split_start(body, [], [zone], name, after)


UPDATE_BLOCK_ELEMS = 256 * 1024


def _update_rows(R, C):
    fits = [t for t in range(8, R + 1, 8) if R % t == 0 and t * C <= UPDATE_BLOCK_ELEMS]
    whole = [t for t in fits if t % 16 == 0]
    return max(whole or fits)


def _adamw_math(g, w, m, v):
    m2 = ADAM_B1 * m + (1.0 - ADAM_B1) * g
    v2 = ADAM_B2 * v + (1.0 - ADAM_B2) * (g * g)
    m_hat = m2 / (1.0 - ADAM_B1 ** ADAM_STEP)
    v_hat = v2 / (1.0 - ADAM_B2 ** ADAM_STEP)
    delta = -ADAM_LR * (m_hat / (jnp.sqrt(v_hat) + ADAM_EPS) + ADAM_WD * w)
    return delta, m2, v2


def _sum_adamw(parts, w, m, v, name):
    R, C = w.shape
    tr = _update_rows(R, C)

    def body(p_ref, w_ref, m_ref, v_ref, g_ref, d_ref, m2_ref, v2_ref):
        g = p_ref[0]
        for k in range(1, N_DEV):
            g = g + p_ref[k]
        g_ref[...] = g
        d_ref[...], m2_ref[...], v2_ref[...] = _adamw_math(g, w_ref[...], m_ref[...], v_ref[...])

    blk = pl.BlockSpec((tr, C), lambda i: (i, 0))
    return pl.pallas_call(
        body, name=name, grid=(R // tr,),
        in_specs=[pl.BlockSpec((N_DEV, tr, C), lambda i: (0, i, 0)), blk, blk, blk],
        out_specs=[blk] * 4,
        out_shape=[SDS((R, C), F32)] * 4,
        compiler_params=_params(1),
    )(parts, w, m, v)


def _sum_adamw_peers(me, own, parts, w, m, v, name, replicated):
    R, C = w.shape
    tr = _update_rows(R, C)

    def body(me_ref, own_ref, p_ref, w_ref, m_ref, v_ref, g_ref, d_ref, m2_ref, v2_ref):
        if replicated:
            mine = me_ref[0]
            g = None
            for j in range(N_DEV):
                k = jnp.maximum(jnp.bitwise_xor(mine, j) - 1, 0)
                term = jnp.where(mine == j, own_ref[...], p_ref[k])
                g = term if g is None else g + term
        else:
            g = own_ref[...].astype(F32)
            for k in range(N_DEV - 1):
                g = g + p_ref[k].astype(F32)
        g_ref[...] = g
        d_ref[...], m2_ref[...], v2_ref[...] = _adamw_math(g, w_ref[...], m_ref[...], v_ref[...])

    blk = pl.BlockSpec((tr, C), lambda i, me_ref: (i, 0))
    own_spec = blk if replicated else pl.BlockSpec((None, tr, C), lambda i, me_ref: (me_ref[0], i, 0))
    return pl.pallas_call(
        body, name=name,
        grid_spec=pltpu.PrefetchScalarGridSpec(
            num_scalar_prefetch=1, grid=(R // tr,),
            in_specs=[own_spec, pl.BlockSpec((N_DEV - 1, tr, C), lambda i, me_ref: (0, i, 0)), blk, blk, blk],
            out_specs=[blk] * 4),
        out_shape=[SDS((R, C), F32)] * 4,
        compiler_params=_params(1),
    )(me, own, parts, w, m, v)


SMALL = ("ln_v_gain", "ln_v_bias", "w_spatial", "b_spatial", "sinks", "norm_mix_post", "norm_ff_pre", "norm_ff_post")
SMALL_ROWS = {"ln_v_gain": 8, "ln_v_bias": 8, "w_spatial": 1024, "b_spatial": 8, "sinks": 8,
              "norm_mix_post": 8, "norm_ff_pre": 8, "norm_ff_post": 8}
SMALL_PACK_ROWS = 1152


def _pack_small(vals):
    rows = []
    for name in SMALL:
        flat = vals[name].reshape(-1)
        pad = SMALL_ROWS[name] * 128 - flat.shape[0]
        if pad:
            flat = jnp.concatenate([flat, jnp.zeros((pad,), F32)])
        rows.append(flat.reshape(SMALL_ROWS[name], 128))
    rows.append(jnp.zeros((SMALL_PACK_ROWS - sum(SMALL_ROWS.values()), 128), F32))
    return jnp.concatenate(rows, axis=0)


def _unpack_small(packed, shapes):
    out, r = {}, 0
    for name in SMALL:
        n = 1
        for s in shapes[name]:
            n *= s
        out[name] = packed[r:r + SMALL_ROWS[name]].reshape(-1)[:n].reshape(shapes[name])
        r += SMALL_ROWS[name]
    return out


def _rope_rows():
    d = jnp.arange(128) % HEAD
    inv = ROPE_THETA ** (-(2.0 * (d % (ROPE // 2))).astype(F32) / ROPE)
    invf = jnp.where(d < ROPE, inv, 0.0).astype(F32).reshape(1, 128)
    sgn = jnp.where(d < ROPE // 2, -1.0, jnp.where(d < ROPE, 1.0, 0.0)).astype(F32).reshape(1, 128)
    return invf, sgn


def kernel(x, positions, w_in, ln_v_gain, ln_v_bias, w_spatial, b_spatial, sinks, w_a, w_b, w_o, norm_mix_pre, norm_mix_post, w_ff_in, w_ff_out, norm_ff_pre, norm_ff_post, loss_target, m_w_in, m_ln_v_gain, m_ln_v_bias, m_w_spatial, m_b_spatial, m_sinks, m_w_a, m_w_b, m_w_o, m_norm_mix_pre, m_norm_mix_post, m_w_ff_in, m_w_ff_out, m_norm_ff_pre, m_norm_ff_post, v_w_in, v_ln_v_gain, v_ln_v_bias, v_w_spatial, v_b_spatial, v_sinks, v_w_a, v_w_b, v_w_o, v_norm_mix_pre, v_norm_mix_post, v_w_ff_in, v_w_ff_out, v_norm_ff_pre, v_norm_ff_post):
    given = dict(locals())
    T = x.shape[1]
    xt = x[0]
    tgt = loss_target[0]
    bst = b_spatial[0].T
    ws = w_spatial[0]

    me = 4 * lax.axis_index("x") + 2 * lax.axis_index("y") + lax.axis_index("c")
    me_arr = me.astype(jnp.int32).reshape(1)

    def with_own(zone, shard):
        return lax.dynamic_update_slice(zone, shard[None], (me,) + (0,) * shard.ndim)

    rest = ("w_a", "w_b", "w_o", "w_ff_in", "w_ff_out")
    shard = {n: given[n][0].astype(BF16) for n in rest}
    g_one = _gather_first_leg(w_in[0].T.astype(BF16), "gather_in_start")
    cos, sin = _rope_tables(positions.astype(F32).reshape(T, 1), *_rope_rows(), after=g_one[-1])
    h = _rms_pre(xt, norm_mix_pre, after=cos)
    (own_win,), (win8,) = _wait_copies(g_one, h, "gather_in_wait", count=4)
    g_two = _gather_second_leg(win8, "gather_in_pass_start")
    g_rest = _start_copies([shard[n] for n in rest], [GATHER] * len(rest), "gather_rest_start", after=g_two[-1])
    _, (win8,) = _wait_copies(g_two, g_rest[-1], "gather_in_pass_wait", count=3)
    win = with_own(win8, own_win).reshape(IN_W, D)

    proj = _fwd_in(h, win)
    att, qr, kr, probs, psink = _fwd_attn(proj, cos, sin, sinks[0])
    a = _fwd_sgu(proj, ln_v_gain, ln_v_bias, ws, bst, after=att)
    gw = {n: with_own(z, own) for n, own, z in zip(rest, *_wait_copies(g_rest, a, "gather_rest_wait"))}
    wa, wb, wo = (gw[n].reshape(D, D) for n in ("w_a", "w_b", "w_o"))
    wfi3 = gw["w_ff_in"]
    wfo = gw["w_ff_out"].reshape(D_FF, D)
    merged, a2, b2, mix, x1, hf = _fwd_mix(a, att, proj, xt, wa, wb, wo, norm_mix_post, norm_ff_pre)
    f, dy, dff, dg3, loss_part = _fwd_ff(hf, wfi3, wfo, x1, tgt, norm_ff_post)

    df, dx1, dmix, dg2, dg1 = _bwd_ff(dff, f, wfi3, wfo, x1, dy, mix, norm_mix_post, norm_ff_pre)
    dwfi3, dwfo = _wgrad_ff(hf, df, f, dff)
    own_ff = [dwfi3, dwfo.reshape(N_DEV, D_FF // N_DEV, D)]
    x_ff = _start_copies(own_ff, [SCATTER] * 2, "exchange_ff_start")
    da2, db2, dgate, da, datt = _bwd_mix(dmix, proj, a2, b2, wo, wa, wb, after=x_ff[-1])
    dwo, dwa, dwb = _wgrad_mix(merged, dmix, a, da2, att, db2)
    own_mix = [g.reshape(N_DEV, D // N_DEV, D) for g in (dwa, dwb, dwo)]
    x_mix = _start_copies(own_mix, [SCATTER] * 3, "exchange_mix_start")
    dq, dkv, dsink = _bwd_attn(qr, kr, probs, psink, proj, cos, sin, datt, after=x_mix[-1])
    duv, dws, dbs, dlng, dlnb = _bwd_sgu(proj, da, ln_v_gain, ln_v_bias, ws, bst)
    small_grads = {"ln_v_gain": dlng, "ln_v_bias": dlnb, "w_spatial": dws, "b_spatial": dbs, "sinks": dsink[:, :N_Q],
                   "norm_mix_post": dg1, "norm_ff_pre": dg2, "norm_ff_post": dg3}
    x_small = _start_copies([_pack_small(small_grads)], [SPREAD], "exchange_small_start")
    dwin = _wgrad_in(h, duv, dq, dkv, dgate)
    own_in = [dwin.reshape(N_DEV, IN_W // N_DEV, D)]
    x_in = _start_copies(own_in, [SCATTER], "exchange_in_start", after=x_small[-1])
    grad_x, dg0 = _bwd_in(duv, dq, dkv, dgate, win, xt, dx1, norm_mix_pre, after=x_in[-1])

    results = {}

    def update(n, own, parts, transposed=False):
        state = [given[k + n][0].T if transposed else given[k + n][0] for k in ("", "m_", "v_")]
        res = _sum_adamw_peers(me_arr, own, parts, *state, "adamw_" + n, False)
        results[n] = [(r.T if transposed else r).reshape(given[n].shape) for r in res]

    own_ff, p_ff = _wait_copies(x_ff, grad_x, "exchange_ff_wait")
    update("w_ff_in", own_ff[0], p_ff[0])
    update("w_ff_out", own_ff[1], p_ff[1])
    own_mix, p_mix = _wait_copies(x_mix, results["w_ff_out"][0], "exchange_mix_wait")
    for n, own, parts in zip(("w_a", "w_b", "w_o"), own_mix, p_mix):
        update(n, own, parts)
    tail = jnp.concatenate([dg0.reshape(8, 128), jnp.tile(loss_part, (8, 1))], axis=0)
    (tail_all,) = _all_to_all([tail], [True], "exchange_tail", after=results["w_o"][0])
    dg0_all = tail_all[:, :8]
    own_small, p_small = _wait_copies(x_small, tail_all, "exchange_small_wait")
    own_in, p_in = _wait_copies(x_in, p_small[0], "exchange_in_wait")
    update("w_in", own_in[0], p_in[0], transposed=True)
    packed = _sum_adamw_peers(me_arr, own_small[0], p_small[0], _pack_small({n: given[n] for n in SMALL}),
                              _pack_small({n: given["m_" + n] for n in SMALL}),
                              _pack_small({n: given["v_" + n] for n in SMALL}), "adamw_small", True)
    shapes = {n: given[n].shape for n in SMALL}
    unpacked = [_unpack_small(p, shapes) for p in packed]
    for n in SMALL:
        results[n] = [u[n] for u in unpacked]
    n = "norm_mix_pre"
    results[n] = [r.reshape(given[n].shape) for r in _sum_adamw(
        dg0_all, given[n].reshape(8, 128), given["m_" + n].reshape(8, 128), given["v_" + n].reshape(8, 128), "adamw_" + n)]

    loss = jnp.sum(tail_all[:, 8, 0])
    order = ("w_in", "ln_v_gain", "ln_v_bias", "w_spatial", "b_spatial", "sinks", "w_a", "w_b", "w_o", "norm_mix_pre",
             "norm_mix_post", "w_ff_in", "w_ff_out", "norm_ff_pre", "norm_ff_post")
    out = [loss, grad_x.reshape(x.shape)]
    for k in range(4):
        out += [results[n][k] for n in order]
    return tuple(out)
```

```python
import jax
import jax.numpy as jnp
from jax import lax
from jax.experimental import pallas as pl
from jax.experimental.pallas import tpu as pltpu

F32 = jnp.float32
BF16 = jnp.bfloat16

N_DEV = 8
D = 1024
D_FF = 4096
IN_W = 5632
CHUNK = 128
GROUPS = 8
HEAD = 64
N_Q = 16
N_KV = 4
ROPE = 16
ROPE_THETA = 500000.0
EPS = 1e-6
OFF_Q, OFF_K, OFF_VA, OFF_GA, OFF_GB = 2048, 3072, 3328, 3584, 4608

ADAM_LR = 0.001
ADAM_B1 = 0.9
ADAM_B2 = 0.999
ADAM_EPS = 1e-08
ADAM_WD = 0.01
ADAM_STEP = 10

VMEM_LIMIT = 62 * 1024 * 1024

SDS = jax.ShapeDtypeStruct
MESH = pl.DeviceIdType.MESH


def _params(n_axes):
    return pltpu.CompilerParams(dimension_semantics=("arbitrary",) * n_axes, vmem_limit_bytes=VMEM_LIMIT)


def _nt(a, b):
    return lax.dot_general(a, b, (((1,), (1,)), ((), ())), preferred_element_type=F32)


def _tn(a, b):
    return lax.dot_general(a, b, (((0,), (0,)), ((), ())), preferred_element_type=F32)


def _nn(a, b):
    return jnp.dot(a, b, preferred_element_type=F32)


def _gelu(x):
    t = jnp.tanh(0.7978845608028654 * (x + 0.044715 * (x * x * x)))
    return 0.5 * x * (1.0 + t), t


def _gelu_grad(x, t):
    return 0.5 * (1.0 + t) + 0.5 * x * (1.0 - t * t) * (0.7978845608028654 * (1.0 + 3.0 * 0.044715 * x * x))


def _sigmoid(x):
    return 1.0 / (1.0 + jnp.exp(-x))


def _rms_stats(v):
    r = lax.rsqrt(jnp.mean(v * v, axis=-1, keepdims=True) + EPS)
    return r, v * r


def _rms_bwd(d, vhat, r, g):
    gd = g * d
    return r * (gd - vhat * jnp.mean(gd * vhat, axis=-1, keepdims=True))


def _colsum(v):
    return jnp.sum(v, axis=0, keepdims=True)


_ANY = pl.BlockSpec(memory_space=pl.ANY)


def _after(body, n_in, after):
    if after is None:
        return body, [], []

    def ordered(*refs):
        return body(*refs[:n_in], *refs[n_in + 1:])

    return ordered, [_ANY], [after]


def _rms_pre(x, g0, after=None):
    T = x.shape[0]
    tm = min(T, 1024)

    def body(x_ref, g_ref, h_ref):
        _, xh = _rms_stats(x_ref[...])
        h_ref[...] = (xh * g_ref[...]).astype(BF16)

    body, dep_specs, deps = _after(body, 2, after)
    return pl.pallas_call(
        body, name="rms_pre", grid=(T // tm,),
        in_specs=[pl.BlockSpec((tm, D), lambda i: (i, 0)), pl.BlockSpec((1, D), lambda i: (0, 0))] + dep_specs,
        out_specs=pl.BlockSpec((tm, D), lambda i: (i, 0)),
        out_shape=SDS((T, D), BF16),
        compiler_params=_params(1),
    )(x, g0, *deps)


def _fwd_in(h, win_t):
    T = h.shape[0]
    tm, tn = min(T, 512), 1408

    def body(h_ref, w_ref, p_ref):
        for j in range(IN_W // tn):
            cols = slice(j * tn, (j + 1) * tn)
            p_ref[:, cols] = _nt(h_ref[...], w_ref[cols, :]).astype(BF16)

    return pl.pallas_call(
        body, name="fwd_in", grid=(T // tm,),
        in_specs=[pl.BlockSpec((tm, D), lambda i: (i, 0)), _resident((IN_W, D))],
        out_specs=pl.BlockSpec((tm, IN_W), lambda i: (i, 0)),
        out_shape=SDS((T, IN_W), BF16),
        compiler_params=_params(1),
    )(h, win_t)


def _sgu_forward_parts(u_ref, vs_ref, lng_ref, lnb_ref):
    u = u_ref[...].astype(F32)
    vs = vs_ref[...].astype(F32)
    gu, tu = _gelu(u)
    gv, tv = _gelu(vs)
    mu = jnp.mean(gv, axis=-1, keepdims=True)
    dv = gv - mu
    rstd = lax.rsqrt(jnp.mean(dv * dv, axis=-1, keepdims=True) + EPS)
    vhat = dv * rstd
    vn = (vhat * lng_ref[...] + lnb_ref[...]).astype(BF16)
    return u, vs, gu, tu, tv, rstd, vhat, vn


def _masked_ws(ws_ref, g):
    row = lax.broadcasted_iota(jnp.int32, (CHUNK, CHUNK), 0)
    col = lax.broadcasted_iota(jnp.int32, (CHUNK, CHUNK), 1)
    return jnp.where(row >= col, ws_ref[g], 0.0).astype(BF16)


def _fwd_sgu(proj, lng, lnb, ws, bst, after=None):
    T = proj.shape[0]
    tc = min(T, 512)

    def body(u_ref, vs_ref, lng_ref, lnb_ref, ws_ref, bst_ref, a_ref):
        _, _, gu, _, _, _, _, vn = _sgu_forward_parts(u_ref, vs_ref, lng_ref, lnb_ref)
        for g in range(GROUPS):
            wm = _masked_ws(ws_ref, g)
            cols = slice(g * CHUNK, (g + 1) * CHUNK)
            for c in range(tc // CHUNK):
                rows = slice(c * CHUNK, (c + 1) * CHUNK)
                mixed = _nn(wm, vn[rows, cols]) + bst_ref[:, g:g + 1]
                a_ref[rows, cols] = (gu[rows, cols] * mixed).astype(BF16)

    body, dep_specs, deps = _after(body, 6, after)
    return pl.pallas_call(
        body, name="fwd_sgu", grid=(T // tc,),
        in_specs=[pl.BlockSpec((tc, D), lambda i: (i, 0)), pl.BlockSpec((tc, D), lambda i: (i, 1)),
                  pl.BlockSpec((1, D), lambda i: (0, 0)), pl.BlockSpec((1, D), lambda i: (0, 0)),
                  pl.BlockSpec((GROUPS, CHUNK, CHUNK), lambda i: (0, 0, 0)),
                  pl.BlockSpec((CHUNK, GROUPS), lambda i: (0, 0))] + dep_specs,
        out_specs=pl.BlockSpec((tc, D), lambda i: (i, 0)),
        out_shape=SDS((T, D), BF16),
        compiler_params=_params(1),
    )(proj, proj, lng, lnb, ws, bst, *deps)


def _rope_tables(posf, invf, sgn, after=None):
    T = posf.shape[0]
    tr = min(T, 1024)

    def body(pos_ref, invf_ref, sgn_ref, c_ref, s_ref):
        ang = pos_ref[...] * invf_ref[...]
        c_ref[...] = jnp.cos(ang)
        s = jnp.sin(ang)
        s_ref[:, :128] = jnp.where(sgn_ref[...] < 0.0, -s, 0.0)
        s_ref[:, 128:] = jnp.where(sgn_ref[...] > 0.0, s, 0.0)

    body, dep_specs, deps = _after(body, 3, after)
    return pl.pallas_call(
        body, name="rope_tables", grid=(T // tr,),
        in_specs=[pl.BlockSpec((tr, 1), lambda i: (i, 0)), pl.BlockSpec((1, 128), lambda i: (0, 0)),
                  pl.BlockSpec((1, 128), lambda i: (0, 0))] + dep_specs,
        out_specs=[pl.BlockSpec((tr, 128), lambda i: (i, 0)), pl.BlockSpec((tr, 256), lambda i: (i, 0))],
        out_shape=[SDS((T, 128), F32), SDS((T, 256), F32)],
        compiler_params=_params(1),
    )(posf, invf, sgn, *deps)


def _rope(v, c, s):
    v = v.astype(F32)
    return v * c + pltpu.roll(v, 128 - ROPE // 2, 1) * s[:, :128] + pltpu.roll(v, ROPE // 2, 1) * s[:, 128:]


def _rope_bwd(dv, c, s):
    return dv * c + pltpu.roll(dv * s[:, :128], ROPE // 2, 1) + pltpu.roll(dv * s[:, 128:], 128 - ROPE // 2, 1)


def _fold_masks(first):
    jj = lax.broadcasted_iota(jnp.int32, (CHUNK, CHUNK), 0)
    t = lax.broadcasted_iota(jnp.int32, (CHUNK, CHUNK), 1)
    prev = jj > t
    return prev, jnp.where(prev & first, -1e30, 0.0)


def _fold(band, prev):
    return jnp.where(prev, band[:CHUNK], band[CHUNK:])


def _unfold(folded, prev):
    return jnp.concatenate([jnp.where(prev, folded, 0.0), jnp.where(prev, 0.0, folded)], axis=0)


def _softmax_sink(s, sink, key_axis):
    m = jnp.maximum(jnp.max(s, axis=key_axis, keepdims=True), sink)
    p = jnp.exp(s - m)
    esink = jnp.exp(sink - m)
    inv = 1.0 / (jnp.sum(p, axis=key_axis, keepdims=True) + esink)
    return p * inv, esink * inv


def _head_pair_operand(slab, g):
    lo = lax.broadcasted_iota(jnp.int32, slab.shape, 1) < HEAD
    if g % 2 == 0:
        first = jnp.where(lo, slab, 0.0)
        second = pltpu.roll(first, HEAD, 1)
    else:
        second = jnp.where(lo, 0.0, slab)
        first = pltpu.roll(second, HEAD, 1)
    return jnp.concatenate([first, second], axis=0).astype(BF16)


def _head_pair_gradient(acc, g):
    top, bot = acc[:2 * CHUNK], acc[2 * CHUNK:]
    lo = lax.broadcasted_iota(jnp.int32, top.shape, 1) < HEAD
    if g % 2 == 0:
        return jnp.where(lo, top, 0.0) + pltpu.roll(jnp.where(lo, 0.0, bot), HEAD, 1)
    return pltpu.roll(jnp.where(lo, top, 0.0), HEAD, 1) + jnp.where(lo, 0.0, bot)


PAIRS_PER_KV = N_Q // N_KV // 2
KV_W = N_KV * HEAD


def _band(prev_ref, cur_ref, cols=slice(None)):
    return jnp.concatenate([prev_ref[:, cols], cur_ref[:, cols]], axis=0)


def _fwd_attn(proj, cos, sin, sinks):
    T = proj.shape[0]
    nb = T // CHUNK
    cur = lambda i: i
    prev = lambda i: jnp.maximum(i - 1, 0)

    def body(q_ref, kp_ref, kc_ref, vp_ref, vc_ref, cp_ref, cc_ref, sp_ref, sc_ref, sink_ref,
             o_ref, qr_ref, kr_ref, p_ref, psink_ref):
        prev_slot, bias = _fold_masks(pl.program_id(0) == 0)
        c_band, s_band = _band(cp_ref, cc_ref), _band(sp_ref, sc_ref)
        for j in range(KV_W // 128):
            cols = slice(j * 128, (j + 1) * 128)
            k_slab = _rope(_band(kp_ref, kc_ref, cols), c_band, s_band)
            kr_ref[:, cols] = k_slab[CHUNK:].astype(BF16)
            v_slab = _band(vp_ref, vc_ref, cols).astype(F32)
            for g in (2 * j, 2 * j + 1):
                k2 = _head_pair_operand(k_slab, g)
                v2 = _head_pair_operand(v_slab, g)
                pairs = [g * PAIRS_PER_KV + r for r in range(PAIRS_PER_KV)]
                qps = []
                for pair in pairs:
                    lanes = slice(pair * 128, (pair + 1) * 128)
                    qps.append((_rope(q_ref[:, lanes], cc_ref[...], sc_ref[...]) * (HEAD ** -0.5)).astype(BF16))
                    qr_ref[:, lanes] = qps[-1]
                s2 = _nt(k2, jnp.concatenate(qps, axis=0))
                pcols = []
                for r, pair in enumerate(pairs):
                    ps = []
                    for e in range(2):
                        head = 2 * pair + e
                        s = _fold(s2[e * 2 * CHUNK:(e + 1) * 2 * CHUNK, r * 128:(r + 1) * 128], prev_slot) + bias
                        p, psink = _softmax_sink(s, sink_ref[head], 0)
                        p = p.astype(BF16)
                        p_ref[head] = p
                        psink_ref[head:head + 1, :] = psink
                        ps.append(_unfold(p, prev_slot))
                    pcols.append(jnp.concatenate(ps, axis=0))
                o = _tn(jnp.concatenate(pcols, axis=1), v2).astype(BF16)
                for r, pair in enumerate(pairs):
                    o_ref[:, pair * 128:(pair + 1) * 128] = o[r * CHUNK:(r + 1) * CHUNK]

    table = lambda which, width: pl.BlockSpec((CHUNK, width), lambda i: (which(i), 0))
    return pl.pallas_call(
        body, name="fwd_attn", grid=(nb,),
        in_specs=[pl.BlockSpec((CHUNK, D), lambda i: (i, OFF_Q // D)),
                  pl.BlockSpec((CHUNK, KV_W), lambda i: (prev(i), OFF_K // KV_W)),
                  pl.BlockSpec((CHUNK, KV_W), lambda i: (i, OFF_K // KV_W)),
                  pl.BlockSpec((CHUNK, KV_W), lambda i: (prev(i), OFF_VA // KV_W)),
                  pl.BlockSpec((CHUNK, KV_W), lambda i: (i, OFF_VA // KV_W)),
                  table(prev, 128), table(cur, 128), table(prev, 256), table(cur, 256),
                  pl.BlockSpec(memory_space=pltpu.SMEM)],
        out_specs=[pl.BlockSpec((CHUNK, D), lambda i: (i, 0)), pl.BlockSpec((CHUNK, D), lambda i: (i, 0)),
                   pl.BlockSpec((CHUNK, KV_W), lambda i: (i, 0)),
                   pl.BlockSpec((None, N_Q, CHUNK, CHUNK), lambda i: (i, 0, 0, 0)),
                   pl.BlockSpec((None, N_Q, CHUNK), lambda i: (i, 0, 0))],
        out_shape=[SDS((T, D), BF16), SDS((T, D), BF16), SDS((T, KV_W), BF16),
                   SDS((nb, N_Q, CHUNK, CHUNK), BF16), SDS((nb, N_Q, CHUNK), F32)],
        compiler_params=_params(1),
    )(proj, proj, proj, proj, proj, cos, cos, sin, sin, sinks)


def _row_halves(tm):
    return [slice(0, tm // 2), slice(tm // 2, tm)] if tm % 32 == 0 else [slice(0, tm)]


def _fwd_mix(a, att, proj, x, wa, wb, wo, g1, g2):
    T = x.shape[0]
    tm = min(T, 512)
    half = D // 2

    def body(a_ref, att_ref, ga0, ga1, gb0, gb1, x_ref, wa_ref, wb_ref, wo_ref, g1_ref, g2_ref,
             mg_ref, a2_ref, b2_ref, mix_ref, x1_ref, hf_ref):
        for rows in _row_halves(tm):
            a2 = _nn(a_ref[rows, :], wa_ref[...])
            b2 = _nn(att_ref[rows, :], wb_ref[...])
            ga = jnp.concatenate([ga0[rows, :], ga1[rows, :]], axis=1).astype(F32)
            gb = jnp.concatenate([gb0[rows, :], gb1[rows, :]], axis=1).astype(F32)
            merged = (_sigmoid(ga) * a2 + _sigmoid(gb) * b2).astype(BF16)
            a2_ref[rows, :] = a2.astype(BF16)
            b2_ref[rows, :] = b2.astype(BF16)
            mg_ref[rows, :] = merged
            mix = _nn(merged, wo_ref[...])
            mix_ref[rows, :] = mix
            _, mh = _rms_stats(mix)
            x1 = x_ref[rows, :] + mh * g1_ref[...]
            x1_ref[rows, :] = x1
            _, xh = _rms_stats(x1)
            hf_ref[rows, :] = (xh * g2_ref[...]).astype(BF16)

    row = lambda i: (i, 0)
    const = lambda i: (0, 0)
    gspec = lambda off: pl.BlockSpec((tm, half), lambda i: (i, off // half))
    return pl.pallas_call(
        body, name="fwd_mix", grid=(T // tm,),
        in_specs=[pl.BlockSpec((tm, D), row), pl.BlockSpec((tm, D), row),
                  gspec(OFF_GA), gspec(OFF_GA + half), gspec(OFF_GB), gspec(OFF_GB + half),
                  pl.BlockSpec((tm, D), row), _resident((D, D)), _resident((D, D)),
                  _resident((D, D)), pl.BlockSpec((1, D), const), pl.BlockSpec((1, D), const)],
        out_specs=[pl.BlockSpec((tm, D), row)] * 6,
        out_shape=[SDS((T, D), BF16), SDS((T, D), BF16), SDS((T, D), BF16), SDS((T, D), F32), SDS((T, D), F32),
                   SDS((T, D), BF16)],
        compiler_params=_params(1),
    )(a, att, proj, proj, proj, proj, x, wa, wb, wo, g1, g2)


FF_SPLIT = N_DEV
FF_TILE = D_FF // FF_SPLIT


def _fwd_ff(hf, wfi3, wfo, x1, tgt, g3):
    T = hf.shape[0]
    tm = min(T, 512)

    def body(hf_ref, wfi_ref, wfo_ref, x1_ref, tgt_ref, g3_ref, f_ref, dy_ref, dff_ref, dg3_ref, loss_ref, r_s):
        @pl.when(pl.program_id(0) == 0)
        def _():
            dg3_ref[...] = jnp.zeros_like(dg3_ref)
            loss_ref[...] = jnp.zeros_like(loss_ref)

        hf_t = hf_ref[...]
        for s in range(FF_SPLIT):
            cols = slice(s * FF_TILE, (s + 1) * FF_TILE)
            f = _nn(hf_t, wfi_ref[s]).astype(BF16)
            f_ref[:, cols] = f
            rl = jnp.maximum(f.astype(F32), 0.0)
            r_s[:, cols] = (rl * rl).astype(BF16)
        r3, fh = _rms_stats(_nn(r_s[...], wfo_ref[...]))
        e = x1_ref[...] + fh * g3_ref[...] - tgt_ref[...]
        loss_ref[...] += jnp.sum(e * e) * (0.5 / D)
        dy = e * (1.0 / D)
        dy_ref[...] = dy
        dg3_ref[...] += _colsum(dy * fh)
        dff_ref[...] = _rms_bwd(dy, fh, r3, g3_ref[...]).astype(BF16)

    row = lambda i: (i, 0)
    const = lambda i: (0, 0)
    return pl.pallas_call(
        body, name="fwd_ff", grid=(T // tm,),
        in_specs=[pl.BlockSpec((tm, D), row), _resident((FF_SPLIT, D, FF_TILE)), _resident((D_FF, D)),
                  pl.BlockSpec((tm, D), row),
                  pl.BlockSpec((tm, D), row), pl.BlockSpec((1, D), const)],
        out_specs=[pl.BlockSpec((tm, D_FF), row), pl.BlockSpec((tm, D), row),
                   pl.BlockSpec((tm, D), row), pl.BlockSpec((1, D), const), pl.BlockSpec((1, 128), const)],
        out_shape=[SDS((T, D_FF), BF16), SDS((T, D), F32), SDS((T, D), BF16), SDS((1, D), F32), SDS((1, 128), F32)],
        scratch_shapes=[pltpu.VMEM((tm, D_FF), BF16)],
        compiler_params=_params(1),
    )(hf, wfi3, wfo, x1, tgt, g3)


def _bwd_ff(dff, f, wfi3, wfo, x1, dy, mix, g1, g2):
    T = dff.shape[0]
    tm = min(T, 512)

    def body(dff_ref, f_ref, wfi_ref, wfo_ref, x1_ref, dy_ref, mix_ref, g1_ref, g2_ref,
             df_ref, dx1_ref, dmix_ref, dg2_ref, dg1_ref):
        @pl.when(pl.program_id(0) == 0)
        def _():
            dg2_ref[...] = jnp.zeros_like(dg2_ref)
            dg1_ref[...] = jnp.zeros_like(dg1_ref)

        dff_t = dff_ref[...]
        dhf = None
        for s in range(FF_SPLIT):
            cols = slice(s * FF_TILE, (s + 1) * FF_TILE)
            dr = _nt(dff_t, wfo_ref[cols, :])
            df = (dr * (2.0 * jnp.maximum(f_ref[:, cols].astype(F32), 0.0))).astype(BF16)
            df_ref[:, cols] = df
            part = _nt(df, wfi_ref[s])
            dhf = part if dhf is None else dhf + part
        r2, xh = _rms_stats(x1_ref[...])
        dg2_ref[...] += _colsum(dhf * xh)
        dx1 = dy_ref[...] + _rms_bwd(dhf, xh, r2, g2_ref[...])
        dx1_ref[...] = dx1
        r1, mh = _rms_stats(mix_ref[...])
        dg1_ref[...] += _colsum(dx1 * mh)
        dmix_ref[...] = _rms_bwd(dx1, mh, r1, g1_ref[...]).astype(BF16)

    row = lambda i: (i, 0)
    const = lambda i: (0, 0)
    return pl.pallas_call(
        body, name="bwd_ff", grid=(T // tm,),
        in_specs=[pl.BlockSpec((tm, D), row), pl.BlockSpec((tm, D_FF), row),
                  _resident((FF_SPLIT, D, FF_TILE)), _resident((D_FF, D)),
                  pl.BlockSpec((tm, D), row), pl.BlockSpec((tm, D), row), pl.BlockSpec((tm, D), row),
                  pl.BlockSpec((1, D), const), pl.BlockSpec((1, D), const)],
        out_specs=[pl.BlockSpec((tm, D_FF), row), pl.BlockSpec((tm, D), row),
                   pl.BlockSpec((tm, D), row), pl.BlockSpec((1, D), const), pl.BlockSpec((1, D), const)],
        out_shape=[SDS((T, D_FF), BF16), SDS((T, D), F32), SDS((T, D), BF16), SDS((1, D), F32), SDS((1, D), F32)],
        compiler_params=_params(1),
    )(dff, f, wfi3, wfo, x1, dy, mix, g1, g2)


def _wgrad_ff(hf, df, f, dff):
    T = hf.shape[0]
    tt = min(T, 1024)
    wide = 2 * FF_TILE

    def body(hf_ref, df_ref, f_ref, dff_ref, dwfi_ref, dwfo_ref, acc_i, acc_o):
        t = pl.program_id(1)

        @pl.when(t == 0)
        def _():
            acc_i[...] = jnp.zeros_like(acc_i)
            acc_o[...] = jnp.zeros_like(acc_o)

        acc_i[...] += _tn(hf_ref[...], df_ref[...])
        rl = jnp.maximum(f_ref[...].astype(F32), 0.0)
        acc_o[...] += _tn((rl * rl).astype(BF16), dff_ref[...])

        @pl.when(t == T // tt - 1)
        def _():
            dwfi_ref[0] = acc_i[:, :FF_TILE].astype(BF16)
            dwfi_ref[1] = acc_i[:, FF_TILE:].astype(BF16)
            dwfo_ref[...] = acc_o[...].astype(BF16)

    return pl.pallas_call(
        body, name="wgrad_ff", grid=(D_FF // wide, T // tt),
        in_specs=[pl.BlockSpec((tt, D), lambda p, t: (t, 0)), pl.BlockSpec((tt, wide), lambda p, t: (t, p)),
                  pl.BlockSpec((tt, wide), lambda p, t: (t, p)), pl.BlockSpec((tt, D), lambda p, t: (t, 0))],
        out_specs=[pl.BlockSpec((2, D, FF_TILE), lambda p, t: (p, 0, 0)), pl.BlockSpec((wide, D), lambda p, t: (p, 0))],
        out_shape=[SDS((FF_SPLIT, D, FF_TILE), BF16), SDS((D_FF, D), BF16)],
        scratch_shapes=[pltpu.VMEM((D, wide), F32), pltpu.VMEM((wide, D), F32)],
        compiler_params=_params(2),
    )(hf, df, f, dff)


def _bwd_mix(dmix, proj, a2, b2, wo, wa, wb, after=None):
    T = dmix.shape[0]
    tm = min(T, 512)
    half = D // 2

    def body(dmix_ref, ga0, ga1, gb0, gb1, a2_ref, b2_ref, wo_ref, wa_ref, wb_ref,
             da2_ref, db2_ref, dg_ref, da_ref, datt_ref):
        for rows in _row_halves(tm):
            dmg = _nt(dmix_ref[rows, :], wo_ref[...])
            sa = _sigmoid(jnp.concatenate([ga0[rows, :], ga1[rows, :]], axis=1).astype(F32))
            sb = _sigmoid(jnp.concatenate([gb0[rows, :], gb1[rows, :]], axis=1).astype(F32))
            da2 = (dmg * sa).astype(BF16)
            db2 = (dmg * sb).astype(BF16)
            da2_ref[rows, :] = da2
            db2_ref[rows, :] = db2
            dg_ref[rows, :D] = (dmg * a2_ref[rows, :].astype(F32) * (sa * (1.0 - sa))).astype(BF16)
            dg_ref[rows, D:] = (dmg * b2_ref[rows, :].astype(F32) * (sb * (1.0 - sb))).astype(BF16)
            da_ref[rows, :] = _nt(da2, wa_ref[...]).astype(BF16)
            datt_ref[rows, :] = _nt(db2, wb_ref[...]).astype(BF16)

    row = lambda i: (i, 0)
    const = lambda i: (0, 0)
    gspec = lambda off: pl.BlockSpec((tm, half), lambda i: (i, off // half))
    body, dep_specs, deps = _after(body, 10, after)
    return pl.pallas_call(
        body, name="bwd_mix", grid=(T // tm,),
        in_specs=[pl.BlockSpec((tm, D), row), gspec(OFF_GA), gspec(OFF_GA + half), gspec(OFF_GB), gspec(OFF_GB + half),
                  pl.BlockSpec((tm, D), row), pl.BlockSpec((tm, D), row),
                  _resident((D, D)), _resident((D, D)), _resident((D, D))] + dep_specs,
        out_specs=[pl.BlockSpec((tm, D), row), pl.BlockSpec((tm, D), row), pl.BlockSpec((tm, 2 * D), row),
                   pl.BlockSpec((tm, D), row), pl.BlockSpec((tm, D), row)],
        out_shape=[SDS((T, D), BF16), SDS((T, D), BF16), SDS((T, 2 * D), BF16), SDS((T, D), BF16), SDS((T, D), BF16)],
        compiler_params=_params(1),
    )(dmix, proj, proj, proj, proj, a2, b2, wo, wa, wb, *deps)


def _wgrad_mix(merged, dmix, a, da2, att, db2):
    T = merged.shape[0]
    tt = min(T, 512)

    def body(mg_ref, dmix_ref, a_ref, da2_ref, att_ref, db2_ref, dwo_ref, dwa_ref, dwb_ref, acc):
        t = pl.program_id(0)

        @pl.when(t == 0)
        def _():
            acc[...] = jnp.zeros_like(acc)

        acc[0] += _tn(mg_ref[...], dmix_ref[...])
        acc[1] += _tn(a_ref[...], da2_ref[...])
        acc[2] += _tn(att_ref[...], db2_ref[...])

        @pl.when(t == T // tt - 1)
        def _():
            dwo_ref[...] = acc[0].astype(BF16)
            dwa_ref[...] = acc[1].astype(BF16)
            dwb_ref[...] = acc[2].astype(BF16)

    return pl.pallas_call(
        body, name="wgrad_mix", grid=(T // tt,),
        in_specs=[pl.BlockSpec((tt, D), lambda t: (t, 0))] * 6,
        out_specs=[pl.BlockSpec((D, D), lambda t: (0, 0))] * 3,
        out_shape=[SDS((D, D), BF16)] * 3,
        scratch_shapes=[pltpu.VMEM((3, D, D), F32)],
        compiler_params=_params(1),
    )(merged, dmix, a, da2, att, db2)


def _bwd_attn(qr, kr, probs, psink, proj, cos, sin, datt, after=None):
    T = proj.shape[0]
    nb = T // CHUNK
    cur = lambda i: jnp.minimum(i, nb - 1)
    prev = lambda i: jnp.maximum(jnp.minimum(i, nb - 1) - 1, 0)

    def body(q_ref, kp_ref, kc_ref, vp_ref, vc_ref, cp_ref, cc_ref, sp_ref, sc_ref, p_ref, psink_ref, do_ref,
             dq_ref, dkv_ref, dsink_ref, carry_k, carry_v):
        i = pl.program_id(0)

        @pl.when(i == 0)
        def _():
            carry_k[...] = jnp.zeros_like(carry_k)
            carry_v[...] = jnp.zeros_like(carry_v)
            dsink_ref[...] = jnp.zeros_like(dsink_ref)

        @pl.when(i < nb)
        def _():
            prev_slot, _ = _fold_masks(i == 0)
            c_band, s_band = _band(cp_ref, cc_ref), _band(sp_ref, sc_ref)
            lane = lax.broadcasted_iota(jnp.int32, (1, 128), 1)
            dsink = jnp.zeros((1, 128), F32)
            for j in range(KV_W // 128):
                cols = slice(j * 128, (j + 1) * 128)
                k_slab = _band(kp_ref, kc_ref, cols).astype(F32)
                v_slab = _band(vp_ref, vc_ref, cols).astype(F32)
                dk_slab = jnp.zeros((2 * CHUNK, 128), F32)
                dv_slab = jnp.zeros((2 * CHUNK, 128), F32)
                for g in (2 * j, 2 * j + 1):
                    k2 = _head_pair_operand(k_slab, g)
                    v2 = _head_pair_operand(v_slab, g)
                    pairs = [g * PAIRS_PER_KV + r for r in range(PAIRS_PER_KV)]
                    q_stack = jnp.concatenate([q_ref[:, pr * 128:(pr + 1) * 128] for pr in pairs], axis=0)
                    do_stack = jnp.concatenate([do_ref[:, pr * 128:(pr + 1) * 128] for pr in pairs], axis=0)
                    dp2 = _nt(v2, do_stack)
                    pcols, dscols = [], []
                    for r, pair in enumerate(pairs):
                        ps, dss = [], []
                        for e in range(2):
                            head = 2 * pair + e
                            p_b = p_ref[head]
                            p = p_b.astype(F32)
                            dp = _fold(dp2[e * 2 * CHUNK:(e + 1) * 2 * CHUNK, r * 128:(r + 1) * 128], prev_slot)
                            delta = jnp.sum(p * dp, axis=0, keepdims=True)
                            ps.append(_unfold(p_b, prev_slot))
                            dss.append(_unfold((p * (dp - delta)).astype(BF16), prev_slot))
                            dsink = dsink + jnp.where(lane == head, -jnp.sum(psink_ref[head:head + 1, :] * delta), 0.0)
                        pcols.append(jnp.concatenate(ps, axis=0))
                        dscols.append(jnp.concatenate(dss, axis=0))
                    ds2 = jnp.concatenate(dscols, axis=1)
                    dq = _tn(ds2, k2) * (HEAD ** -0.5)
                    for r, pair in enumerate(pairs):
                        dq_ref[:, pair * 128:(pair + 1) * 128] = _rope_bwd(
                            dq[r * CHUNK:(r + 1) * CHUNK], cc_ref[...], sc_ref[...]).astype(BF16)
                    dk_slab = dk_slab + _head_pair_gradient(_nn(ds2, q_stack), g)
                    dv_slab = dv_slab + _head_pair_gradient(_nn(jnp.concatenate(pcols, axis=1), do_stack), g)
                dk_slab = _rope_bwd(dk_slab, c_band, s_band)
                vcols = slice(KV_W + j * 128, KV_W + (j + 1) * 128)
                dkv_ref[:, cols] = (carry_k[:, cols] + dk_slab[:CHUNK]).astype(BF16)
                dkv_ref[:, vcols] = (carry_v[:, cols] + dv_slab[:CHUNK]).astype(BF16)
                carry_k[:, cols] = dk_slab[CHUNK:]
                carry_v[:, cols] = dv_slab[CHUNK:]
            dsink_ref[...] += dsink

        @pl.when(i == nb)
        def _():
            dkv_ref[:, :KV_W] = carry_k[...].astype(BF16)
            dkv_ref[:, KV_W:] = carry_v[...].astype(BF16)

    table = lambda which, width: pl.BlockSpec((CHUNK, width), lambda i: (which(i), 0))
    body, dep_specs, deps = _after(body, 12, after)
    return pl.pallas_call(
        body, name="bwd_attn", grid=(nb + 1,),
        in_specs=[pl.BlockSpec((CHUNK, D), lambda i: (cur(i), 0)),
                  pl.BlockSpec((CHUNK, KV_W), lambda i: (prev(i), 0)),
                  pl.BlockSpec((CHUNK, KV_W), lambda i: (cur(i), 0)),
                  pl.BlockSpec((CHUNK, KV_W), lambda i: (prev(i), OFF_VA // KV_W)),
                  pl.BlockSpec((CHUNK, KV_W), lambda i: (cur(i), OFF_VA // KV_W)),
                  table(prev, 128), table(cur, 128), table(prev, 256), table(cur, 256),
                  pl.BlockSpec((None, N_Q, CHUNK, CHUNK), lambda i: (cur(i), 0, 0, 0)),
                  pl.BlockSpec((None, N_Q, CHUNK), lambda i: (cur(i), 0, 0)),
                  pl.BlockSpec((CHUNK, D), lambda i: (cur(i), 0))] + dep_specs,
        out_specs=[pl.BlockSpec((CHUNK, D), lambda i: (cur(i), 0)),
                   pl.BlockSpec((CHUNK, 2 * KV_W), lambda i: (jnp.maximum(i - 1, 0), 0)),
                   pl.BlockSpec((1, 128), lambda i: (0, 0))],
        out_shape=[SDS((T, D), BF16), SDS((T, 2 * KV_W), BF16), SDS((1, 128), F32)],
        scratch_shapes=[pltpu.VMEM((CHUNK, KV_W), F32), pltpu.VMEM((CHUNK, KV_W), F32)],
        compiler_params=_params(1),
    )(qr, kr, kr, proj, proj, cos, cos, sin, sin, probs, psink, datt, *deps)


def _bwd_sgu(proj, da, lng, lnb, ws, bst):
    T = proj.shape[0]
    tc = min(T, 512)
    nsteps = T // tc

    def body(u_ref, vs_ref, da_ref, lng_ref, lnb_ref, ws_ref, bst_ref,
             duv_ref, dws_ref, dbs_ref, dlng_ref, dlnb_ref, dvn_s, dgu_s, dmx_sum):
        i = pl.program_id(0)

        @pl.when(i == 0)
        def _():
            dws_ref[...] = jnp.zeros_like(dws_ref)
            dlng_ref[...] = jnp.zeros_like(dlng_ref)
            dlnb_ref[...] = jnp.zeros_like(dlnb_ref)
            dmx_sum[...] = jnp.zeros_like(dmx_sum)

        u, vs, gu, tu, tv, rstd, vhat, vn = _sgu_forward_parts(u_ref, vs_ref, lng_ref, lnb_ref)
        da = da_ref[...].astype(F32)
        for g in range(GROUPS):
            wm = _masked_ws(ws_ref, g)
            cols = slice(g * CHUNK, (g + 1) * CHUNK)
            dws = jnp.zeros((CHUNK, CHUNK), F32)
            dsum = jnp.zeros((CHUNK, CHUNK), F32)
            for c in range(tc // CHUNK):
                rows = slice(c * CHUNK, (c + 1) * CHUNK)
                vn_cg = vn[rows, cols]
                mixed = _nn(wm, vn_cg) + bst_ref[:, g:g + 1]
                dgu_s[rows, cols] = da[rows, cols] * mixed
                dmx = da[rows, cols] * gu[rows, cols]
                dmxb = dmx.astype(BF16)
                dws = dws + _nt(dmxb, vn_cg)
                dsum = dsum + dmx
                dvn_s[rows, cols] = _tn(wm, dmxb)
            dws_ref[g] += dws
            dmx_sum[:, cols] += dsum
        dvn = dvn_s[...]
        dlng_ref[...] += _colsum(dvn * vhat)
        dlnb_ref[...] += _colsum(dvn)
        dvh = dvn * lng_ref[...]
        dgv = rstd * (dvh - jnp.mean(dvh, axis=-1, keepdims=True) - vhat * jnp.mean(dvh * vhat, axis=-1, keepdims=True))
        duv_ref[:, :D] = (dgu_s[...] * _gelu_grad(u, tu)).astype(BF16)
        duv_ref[:, D:] = (dgv * _gelu_grad(vs, tv)).astype(BF16)

        @pl.when(i == nsteps - 1)
        def _():
            row = lax.broadcasted_iota(jnp.int32, (CHUNK, CHUNK), 0)
            col = lax.broadcasted_iota(jnp.int32, (CHUNK, CHUNK), 1)
            for g in range(GROUPS):
                dws_ref[g] = jnp.where(row >= col, dws_ref[g], 0.0)
                dbs_ref[g:g + 1, :] = _colsum(dmx_sum[:, g * CHUNK:(g + 1) * CHUNK].T)

    const2 = lambda i: (0, 0)
    return pl.pallas_call(
        body, name="bwd_sgu", grid=(nsteps,),
        in_specs=[pl.BlockSpec((tc, D), lambda i: (i, 0)), pl.BlockSpec((tc, D), lambda i: (i, 1)),
                  pl.BlockSpec((tc, D), lambda i: (i, 0)), pl.BlockSpec((1, D), const2), pl.BlockSpec((1, D), const2),
                  pl.BlockSpec((GROUPS, CHUNK, CHUNK), lambda i: (0, 0, 0)), pl.BlockSpec((CHUNK, GROUPS), const2)],
        out_specs=[pl.BlockSpec((tc, 2 * D), lambda i: (i, 0)), pl.BlockSpec((GROUPS, CHUNK, CHUNK), lambda i: (0, 0, 0)),
                   pl.BlockSpec((GROUPS, CHUNK), const2), pl.BlockSpec((1, D), const2), pl.BlockSpec((1, D), const2)],
        out_shape=[SDS((T, 2 * D), BF16), SDS((GROUPS, CHUNK, CHUNK), F32), SDS((GROUPS, CHUNK), F32),
                   SDS((1, D), F32), SDS((1, D), F32)],
        scratch_shapes=[pltpu.VMEM((tc, D), F32), pltpu.VMEM((tc, D), F32), pltpu.VMEM((CHUNK, D), F32)],
        compiler_params=_params(1),
    )(proj, proj, da, lng, lnb, ws, bst)


IN_SEG_WIDTHS = (2 * D, D, 2 * N_KV * HEAD, 2 * D)


def _resident(shape):
    return pl.BlockSpec(shape, lambda *_: (0,) * len(shape), pipeline_mode=pl.Buffered(1))


def _bwd_in(duv, dq, dkv, dg, win_t, x, dx1, g0, after=None):
    T = x.shape[0]
    tm = min(T, 512)

    def body(duv_ref, dq_ref, dkv_ref, dg_ref, w_ref, x_ref, dx1_ref, g0_ref, gx_ref, dg0_ref):
        @pl.when(pl.program_id(0) == 0)
        def _():
            dg0_ref[...] = jnp.zeros_like(dg0_ref)

        dh, off = None, 0
        for ref, width in zip((duv_ref, dq_ref, dkv_ref, dg_ref), IN_SEG_WIDTHS):
            part = _nn(ref[...], w_ref[off:off + width, :])
            dh = part if dh is None else dh + part
            off += width
        r0, xh = _rms_stats(x_ref[...])
        dg0_ref[...] += _colsum(dh * xh)
        gx_ref[...] = dx1_ref[...] + _rms_bwd(dh, xh, r0, g0_ref[...])

    row = lambda i: (i, 0)
    body, dep_specs, deps = _after(body, 8, after)
    return pl.pallas_call(
        body, name="bwd_in", grid=(T // tm,),
        in_specs=[pl.BlockSpec((tm, w), row) for w in IN_SEG_WIDTHS] + [
            _resident((IN_W, D)), pl.BlockSpec((tm, D), row), pl.BlockSpec((tm, D), row),
            pl.BlockSpec((1, D), lambda i: (0, 0))] + dep_specs,
        out_specs=[pl.BlockSpec((tm, D), row), pl.BlockSpec((1, D), lambda i: (0, 0))],
        out_shape=[SDS((T, D), F32), SDS((1, D), F32)],
        compiler_params=_params(1),
    )(duv, dq, dkv, dg, win_t, x, dx1, g0, *deps)


def _wgrad_rows(h, segs, first_row, into, name):
    T = h.shape[0]
    tt = min(T, 1024)
    widths = [s.shape[1] for s in segs]
    rows = sum(widths)
    n_in = 1 + len(segs) + (into is not None)

    def body(*refs):
        h_ref, seg_refs = refs[0], refs[1:1 + len(segs)]
        dw_ref, acc, stage, sem = refs[n_in], refs[n_in + 1], refs[n_in + 2], refs[n_in + 3]
        t = pl.program_id(0)

        @pl.when(t == 0)
        def _():
            acc[...] = jnp.zeros_like(acc)

        off = 0
        for ref, width in zip(seg_refs, widths):
            acc[off:off + width, :] += _tn(ref[...], h_ref[...])
            off += width

        @pl.when(t == T // tt - 1)
        def _():
            stage[...] = acc[...].astype(BF16)
            out = pltpu.make_async_copy(stage, dw_ref.at[pl.ds(first_row, rows)], sem)
            out.start()
            out.wait()

    row = lambda t: (t, 0)
    return pl.pallas_call(
        body, name=name, grid=(T // tt,),
        in_specs=[pl.BlockSpec((tt, D), row)] + [pl.BlockSpec((tt, w), row) for w in widths] + [_ANY] * (into is not None),
        out_specs=_ANY,
        out_shape=SDS((IN_W, D), BF16),
        input_output_aliases={} if into is None else {n_in - 1: 0},
        scratch_shapes=[pltpu.VMEM((rows, D), F32), pltpu.VMEM((rows, D), BF16), pltpu.SemaphoreType.DMA],
        compiler_params=_params(1),
    )(h, *segs, *([] if into is None else [into]))


def _wgrad_in(h, duv, dq, dkv, dg):
    dw = _wgrad_rows(h, [dg], IN_SEG_WIDTHS[0] + IN_SEG_WIDTHS[1] + IN_SEG_WIDTHS[2], None, "wgrad_in_gates")
    dw = _wgrad_rows(h, [duv], 0, dw, "wgrad_in_uv")
    return _wgrad_rows(h, [dq, dkv], IN_SEG_WIDTHS[0], dw, "wgrad_in_qkv")


def _place():
    x, y, c = lax.axis_index("x"), lax.axis_index("y"), lax.axis_index("c")
    return x, y, c, 4 * x + 2 * y + c


def _peers(x, y, c):
    out = []
    for mask in range(1, N_DEV):
        px = 1 - x if mask & 4 else x
        py = 1 - y if mask & 2 else y
        pc = 1 - c if mask & 1 else c
        out.append(((px, py, pc), 4 * px + 2 * py + pc))
    return out


def _all_to_all(arrays, gather, name, after=None):
    n = len(arrays)

    def body(*refs):
        ins, outs = refs[:n], refs[n:2 * n]
        send_sems, recv_sems, local_sems = refs[2 * n:]
        x, y, c, me = _place()
        local, sends, recvs = [], [], []
        for a in range(n):
            src_own = ins[a] if gather[a] else ins[a].at[me]
            local.append(pltpu.make_async_copy(src_own, outs[a].at[me], local_sems.at[a]))
            for k, (peer, pid) in enumerate(_peers(x, y, c)):
                sem = a * (N_DEV - 1) + k
                src = ins[a] if gather[a] else ins[a].at[pid]
                sends.append(pltpu.make_async_remote_copy(
                    src_ref=src, dst_ref=outs[a].at[me], send_sem=send_sems.at[sem], recv_sem=recv_sems.at[sem],
                    device_id=peer, device_id_type=MESH))
                recvs.append(pltpu.make_async_remote_copy(
                    src_ref=src, dst_ref=outs[a].at[pid], send_sem=send_sems.at[sem], recv_sem=recv_sems.at[sem],
                    device_id=peer, device_id_type=MESH))
        for cp in local + sends:
            cp.start()
        for cp in recvs:
            cp.wait_recv()
        for cp in sends:
            cp.wait_send()
        for cp in local:
            cp.wait()

    out_shape = [SDS((N_DEV,) + a.shape if gt else a.shape, a.dtype) for a, gt in zip(arrays, gather)]
    nsem = n * (N_DEV - 1)
    body, dep_specs, deps = _after(body, n, after)
    return pl.pallas_call(
        body, name=name,
        in_specs=[pl.BlockSpec(memory_space=pl.ANY)] * n + dep_specs,
        out_specs=[pl.BlockSpec(memory_space=pl.ANY)] * n,
        out_shape=out_shape,
        scratch_shapes=[pltpu.SemaphoreType.DMA((nsem,)), pltpu.SemaphoreType.DMA((nsem,)), pltpu.SemaphoreType.DMA((n,))],
    )(*arrays, *deps)


_HBM = pl.BlockSpec(memory_space=pltpu.HBM)
_SEM = pl.BlockSpec(memory_space=pltpu.SEMAPHORE)
_EFFECT = pltpu.SideEffectType.DATAFLOW_SIDE_EFFECTING
GATHER = "gather"
SCATTER = "scatter"
SPREAD = "spread"


def _zone_shape(a, mode):
    if mode == GATHER:
        return (N_DEV,) + a.shape
    return (N_DEV - 1,) + (a.shape[1:] if mode == SCATTER else a.shape)


def _start_copies(arrays, modes, name, after=None):
    n = len(arrays)
    zones = [lax.empty(_zone_shape(a, m), a.dtype) for a, m in zip(arrays, modes)]

    def body(*refs):
        ins, lands = refs[:n], refs[n:2 * n]
        send_sems, recv_sems = refs[-2 * n - 3], refs[-2 * n - 2]
        token = refs[-1]
        x, y, c, me = _place()
        for a in range(n):
            for k, (peer, pid) in enumerate(_peers(x, y, c)):
                src = ins[a].at[pid] if modes[a] == SCATTER else ins[a]
                dst = lands[a].at[me] if modes[a] == GATHER else lands[a].at[k]
                pltpu.make_async_remote_copy(src_ref=src, dst_ref=dst, send_sem=send_sems.at[a], recv_sem=recv_sems.at[a],
                                             device_id=peer, device_id_type=MESH).start()
        token[...] = jnp.zeros_like(token)

    hbm = lambda a: pltpu.HBM(a.shape, a.dtype)
    sems = pltpu.SemaphoreType.DMA((n,))
    extra = [] if after is None else [after]
    operands = [pltpu.with_memory_space_constraint(a, pltpu.HBM) for a in list(arrays) + zones]
    res = pl.pallas_call(
        body, name=name,
        out_shape=(sems, sems, *[hbm(a) for a in arrays], *[hbm(z) for z in zones], SDS((8, 128), F32)),
        in_specs=[_HBM] * (2 * n) + [_ANY] * len(extra),
        out_specs=(_SEM, _SEM, *[_HBM] * (2 * n), pl.BlockSpec(memory_space=pltpu.VMEM)),
        input_output_aliases={i: 2 + i for i in range(2 * n)},
        compiler_params=pltpu.CompilerParams(has_side_effects=_EFFECT),
    )(*operands, *extra)
    return res[0], res[1], list(res[2:2 + n]), list(res[2 + n:2 + 2 * n]), res[-1]


def _wait_copies(started, after, name, count=N_DEV - 1):
    send_sems, recv_sems, thru, zones, _ = started
    nt, nz = len(thru), len(zones)

    def body(*refs):
        lands = refs[nt:nt + nz]
        send_ref, recv_ref = refs[nt + nz], refs[nt + nz + 1]
        x, y, c, _ = _place()
        for a in range(nz):
            blocks = lands[a].at[pl.ds(0, count)]
            cp = pltpu.make_async_remote_copy(src_ref=blocks, dst_ref=blocks, send_sem=send_ref.at[a], recv_sem=recv_ref.at[a],
                                              device_id=(x, y, 1 - c), device_id_type=MESH)
            cp.wait_send()
            cp.wait_recv()

    hbm = lambda a: pltpu.HBM(a.shape, a.dtype)
    res = pl.pallas_call(
        body, name=name,
        out_shape=tuple(hbm(a) for a in thru + zones),
        in_specs=[_HBM] * (nt + nz) + [_SEM, _SEM, _ANY],
        out_specs=tuple([_HBM] * (nt + nz)),
        input_output_aliases={i: i for i in range(nt + nz)},
        compiler_params=pltpu.CompilerParams(has_side_effects=_EFFECT),
    )(*thru, *zones, send_sems, recv_sems, after)
    return list(res[:nt]), list(res[nt:])


def _split_start(body, arrays, zones, name, after):
    n = len(arrays) + len(zones)
    hbm = lambda a: pltpu.HBM(a.shape, a.dtype)
    sems = pltpu.SemaphoreType.DMA((max(len(zones), 1),))
    extra = [] if after is None else [after]
    operands = [pltpu.with_memory_space_constraint(a, pltpu.HBM) for a in list(arrays) + list(zones)]
    res = pl.pallas_call(
        body, name=name,
        out_shape=(sems, sems, *[hbm(a) for a in operands], SDS((8, 128), F32)),
        in_specs=[_HBM] * n + [_ANY] * len(extra),
        out_specs=(_SEM, _SEM, *[_HBM] * n, pl.BlockSpec(memory_space=pltpu.VMEM)),
        input_output_aliases={i: 2 + i for i in range(n)},
        compiler_params=pltpu.CompilerParams(has_side_effects=_EFFECT),
    )(*operands, *extra)
    return res[0], res[1], list(res[2:2 + len(arrays)]), list(res[2 + len(arrays):2 + n]), res[-1]


def _gather_first_leg(shard, name, after=None):
    zone = lax.empty((N_DEV,) + shard.shape, shard.dtype)
    extra = 0 if after is None else 1

    def body(*refs):
        src, land = refs[0], refs[1]
        send_sem, recv_sem, token = refs[2 + extra], refs[3 + extra], refs[-1]
        x, y, c, me = _place()
        for peer in ((x, y, 1 - c), (1 - x, y, c), (x, 1 - y, c), (1 - x, 1 - y, c)):
            pltpu.make_async_remote_copy(src_ref=src, dst_ref=land.at[me], send_sem=send_sem.at[0], recv_sem=recv_sem.at[0],
                                         device_id=peer, device_id_type=MESH).start()
        token[...] = jnp.zeros_like(token)

    return _split_start(body, [shard], [zone], name, after)


def _gather_second_leg(zone, name, after=None):
    extra = 0 if after is None else 1

    def body(*refs):
        land = refs[0]
        send_sem, recv_sem, token = refs[1 + extra], refs[2 + extra], refs[-1]
        x, y, c, _ = _place()
        for px, py in ((1 - x, y), (x, 1 - y), (1 - x, 1 - y)):
            slot = 4 * px + 2 * py + c
            pltpu.make_async_remote_copy(src_ref=land.at[slot], dst_ref=land.at[slot], send_sem=send_sem.at[0],
                                         recv_sem=recv_sem.at[0], device_id=(x, y, 1 - c), device_id_type=MESH).start()
        token[...] = jnp.zeros_like(token)

    return _split_start(body, [], [zone], name, after)


UPDATE_BLOCK_ELEMS = 256 * 1024


def _update_rows(R, C):
    fits = [t for t in range(8, R + 1, 8) if R % t == 0 and t * C <= UPDATE_BLOCK_ELEMS]
    whole = [t for t in fits if t % 16 == 0]
    return max(whole or fits)


def _adamw_math(g, w, m, v):
    m2 = ADAM_B1 * m + (1.0 - ADAM_B1) * g
    v2 = ADAM_B2 * v + (1.0 - ADAM_B2) * (g * g)
    m_hat = m2 / (1.0 - ADAM_B1 ** ADAM_STEP)
    v_hat = v2 / (1.0 - ADAM_B2 ** ADAM_STEP)
    delta = -ADAM_LR * (m_hat / (jnp.sqrt(v_hat) + ADAM_EPS) + ADAM_WD * w)
    return delta, m2, v2


def _sum_adamw(parts, w, m, v, name):
    R, C = w.shape
    tr = _update_rows(R, C)

    def body(p_ref, w_ref, m_ref, v_ref, g_ref, d_ref, m2_ref, v2_ref):
        g = p_ref[0]
        for k in range(1, N_DEV):
            g = g + p_ref[k]
        g_ref[...] = g
        d_ref[...], m2_ref[...], v2_ref[...] = _adamw_math(g, w_ref[...], m_ref[...], v_ref[...])

    blk = pl.BlockSpec((tr, C), lambda i: (i, 0))
    return pl.pallas_call(
        body, name=name, grid=(R // tr,),
        in_specs=[pl.BlockSpec((N_DEV, tr, C), lambda i: (0, i, 0)), blk, blk, blk],
        out_specs=[blk] * 4,
        out_shape=[SDS((R, C), F32)] * 4,
        compiler_params=_params(1),
    )(parts, w, m, v)


def _sum_adamw_peers(me, own, parts, w, m, v, name, replicated):
    R, C = w.shape
    tr = _update_rows(R, C)

    def body(me_ref, own_ref, p_ref, w_ref, m_ref, v_ref, g_ref, d_ref, m2_ref, v2_ref):
        if replicated:
            mine = me_ref[0]
            g = None
            for j in range(N_DEV):
                k = jnp.maximum(jnp.bitwise_xor(mine, j) - 1, 0)
                term = jnp.where(mine == j, own_ref[...], p_ref[k])
                g = term if g is None else g + term
        else:
            g = own_ref[...].astype(F32)
            for k in range(N_DEV - 1):
                g = g + p_ref[k].astype(F32)
        g_ref[...] = g
        d_ref[...], m2_ref[...], v2_ref[...] = _adamw_math(g, w_ref[...], m_ref[...], v_ref[...])

    blk = pl.BlockSpec((tr, C), lambda i, me_ref: (i, 0))
    own_spec = blk if replicated else pl.BlockSpec((None, tr, C), lambda i, me_ref: (me_ref[0], i, 0))
    return pl.pallas_call(
        body, name=name,
        grid_spec=pltpu.PrefetchScalarGridSpec(
            num_scalar_prefetch=1, grid=(R // tr,),
            in_specs=[own_spec, pl.BlockSpec((N_DEV - 1, tr, C), lambda i, me_ref: (0, i, 0)), blk, blk, blk],
            out_specs=[blk] * 4),
        out_shape=[SDS((R, C), F32)] * 4,
        compiler_params=_params(1),
    )(me, own, parts, w, m, v)


SMALL = ("ln_v_gain", "ln_v_bias", "w_spatial", "b_spatial", "sinks", "norm_mix_post", "norm_ff_pre", "norm_ff_post")
SMALL_ROWS = {"ln_v_gain": 8, "ln_v_bias": 8, "w_spatial": 1024, "b_spatial": 8, "sinks": 8,
              "norm_mix_post": 8, "norm_ff_pre": 8, "norm_ff_post": 8}
SMALL_PACK_ROWS = 1152


def _pack_small(vals):
    rows = []
    for name in SMALL:
        flat = vals[name].reshape(-1)
        pad = SMALL_ROWS[name] * 128 - flat.shape[0]
        if pad:
            flat = jnp.concatenate([flat, jnp.zeros((pad,), F32)])
        rows.append(flat.reshape(SMALL_ROWS[name], 128))
    rows.append(jnp.zeros((SMALL_PACK_ROWS - sum(SMALL_ROWS.values()), 128), F32))
    return jnp.concatenate(rows, axis=0)


def _unpack_small(packed, shapes):
    out, r = {}, 0
    for name in SMALL:
        n = 1
        for s in shapes[name]:
            n *= s
        out[name] = packed[r:r + SMALL_ROWS[name]].reshape(-1)[:n].reshape(shapes[name])
        r += SMALL_ROWS[name]
    return out


def _rope_rows():
    d = jnp.arange(128) % HEAD
    inv = ROPE_THETA ** (-(2.0 * (d % (ROPE // 2))).astype(F32) / ROPE)
    invf = jnp.where(d < ROPE, inv, 0.0).astype(F32).reshape(1, 128)
    sgn = jnp.where(d < ROPE // 2, -1.0, jnp.where(d < ROPE, 1.0, 0.0)).astype(F32).reshape(1, 128)
    return invf, sgn


def kernel(x, positions, w_in, ln_v_gain, ln_v_bias, w_spatial, b_spatial, sinks, w_a, w_b, w_o, norm_mix_pre, norm_mix_post, w_ff_in, w_ff_out, norm_ff_pre, norm_ff_post, loss_target, m_w_in, m_ln_v_gain, m_ln_v_bias, m_w_spatial, m_b_spatial, m_sinks, m_w_a, m_w_b, m_w_o, m_norm_mix_pre, m_norm_mix_post, m_w_ff_in, m_w_ff_out, m_norm_ff_pre, m_norm_ff_post, v_w_in, v_ln_v_gain, v_ln_v_bias, v_w_spatial, v_b_spatial, v_sinks, v_w_a, v_w_b, v_w_o, v_norm_mix_pre, v_norm_mix_post, v_w_ff_in, v_w_ff_out, v_norm_ff_pre, v_norm_ff_post):
    given = dict(locals())
    T = x.shape[1]
    xt = x[0]
    tgt = loss_target[0]
    bst = b_spatial[0].T
    ws = w_spatial[0]

    me = 4 * lax.axis_index("x") + 2 * lax.axis_index("y") + lax.axis_index("c")
    me_arr = me.astype(jnp.int32).reshape(1)

    def with_own(zone, shard):
        return lax.dynamic_update_slice(zone, shard[None], (me,) + (0,) * shard.ndim)

    rest = ("w_a", "w_b", "w_o", "w_ff_in", "w_ff_out")
    shard = {n: given[n][0].astype(BF16) for n in rest}
    g_one = _gather_first_leg(w_in[0].T.astype(BF16), "gather_in_start")
    cos, sin = _rope_tables(positions.astype(F32).reshape(T, 1), *_rope_rows(), after=g_one[-1])
    h = _rms_pre(xt, norm_mix_pre, after=cos)
    (own_win,), (win8,) = _wait_copies(g_one, h, "gather_in_wait", count=4)
    g_two = _gather_second_leg(win8, "gather_in_pass_start")
    g_rest = _start_copies([shard[n] for n in rest], [GATHER] * len(rest), "gather_rest_start", after=g_two[-1])
    _, (win8,) = _wait_copies(g_two, g_rest[-1], "gather_in_pass_wait", count=3)
    win = with_own(win8, own_win).reshape(IN_W, D)

    proj = _fwd_in(h, win)
    att, qr, kr, probs, psink = _fwd_attn(proj, cos, sin, sinks[0])
    a = _fwd_sgu(proj, ln_v_gain, ln_v_bias, ws, bst, after=att)
    gw = {n: with_own(z, own) for n, own, z in zip(rest, *_wait_copies(g_rest, a, "gather_rest_wait"))}
    wa, wb, wo = (gw[n].reshape(D, D) for n in ("w_a", "w_b", "w_o"))
    wfi3 = gw["w_ff_in"]
    wfo = gw["w_ff_out"].reshape(D_FF, D)
    merged, a2, b2, mix, x1, hf = _fwd_mix(a, att, proj, xt, wa, wb, wo, norm_mix_post, norm_ff_pre)
    f, dy, dff, dg3, loss_part = _fwd_ff(hf, wfi3, wfo, x1, tgt, norm_ff_post)

    df, dx1, dmix, dg2, dg1 = _bwd_ff(dff, f, wfi3, wfo, x1, dy, mix, norm_mix_post, norm_ff_pre)
    dwfi3, dwfo = _wgrad_ff(hf, df, f, dff)
    own_ff = [dwfi3, dwfo.reshape(N_DEV, D_FF // N_DEV, D)]
    x_ff = _start_copies(own_ff, [SCATTER] * 2, "exchange_ff_start")
    da2, db2, dgate, da, datt = _bwd_mix(dmix, proj, a2, b2, wo, wa, wb, after=x_ff[-1])
    dwo, dwa, dwb = _wgrad_mix(merged, dmix, a, da2, att, db2)
    own_mix = [g.reshape(N_DEV, D // N_DEV, D) for g in (dwa, dwb, dwo)]
    x_mix = _start_copies(own_mix, [SCATTER] * 3, "exchange_mix_start")
    dq, dkv, dsink = _bwd_attn(qr, kr, probs, psink, proj, cos, sin, datt, after=x_mix[-1])
    duv, dws, dbs, dlng, dlnb = _bwd_sgu(proj, da, ln_v_gain, ln_v_bias, ws, bst)
    small_grads = {"ln_v_gain": dlng, "ln_v_bias": dlnb, "w_spatial": dws, "b_spatial": dbs, "sinks": dsink[:, :N_Q],
                   "norm_mix_post": dg1, "norm_ff_pre": dg2, "norm_ff_post": dg3}
    x_small = _start_copies([_pack_small(small_grads)], [SPREAD], "exchange_small_start")
    dwin = _wgrad_in(h, duv, dq, dkv, dgate)
    own_in = [dwin.reshape(N_DEV, IN_W // N_DEV, D)]
    x_in = _start_copies(own_in, [SCATTER], "exchange_in_start", after=x_small[-1])
    grad_x, dg0 = _bwd_in(duv, dq, dkv, dgate, win, xt, dx1, norm_mix_pre, after=x_in[-1])

    results = {}

    def update(n, own, parts, transposed=False):
        state = [given[k + n][0].T if transposed else given[k + n][0] for k in ("", "m_", "v_")]
        res = _sum_adamw_peers(me_arr, own, parts, *state, "adamw_" + n, False)
        results[n] = [(r.T if transposed else r).reshape(given[n].shape) for r in res]

    own_ff, p_ff = _wait_copies(x_ff, grad_x, "exchange_ff_wait")
    update("w_ff_in", own_ff[0], p_ff[0])
    update("w_ff_out", own_ff[1], p_ff[1])
    own_mix, p_mix = _wait_copies(x_mix, results["w_ff_out"][0], "exchange_mix_wait")
    for n, own, parts in zip(("w_a", "w_b", "w_o"), own_mix, p_mix):
        update(n, own, parts)
    tail = jnp.concatenate([dg0.reshape(8, 128), jnp.tile(loss_part, (8, 1))], axis=0)
    (tail_all,) = _all_to_all([tail], [True], "exchange_tail", after=results["w_o"][0])
    dg0_all = tail_all[:, :8]
    own_small, p_small = _wait_copies(x_small, tail_all, "exchange_small_wait")
    own_in, p_in = _wait_copies(x_in, p_small[0], "exchange_in_wait")
    update("w_in", own_in[0], p_in[0], transposed=True)
    packed = _sum_adamw_peers(me_arr, own_small[0], p_small[0], _pack_small({n: given[n] for n in SMALL}),
                              _pack_small({n: given["m_" + n] for n in SMALL}),
                              _pack_small({n: given["v_" + n] for n in SMALL}), "adamw_small", True)
    shapes = {n: given[n].shape for n in SMALL}
    unpacked = [_unpack_small(p, shapes) for p in packed]
    for n in SMALL:
        results[n] = [u[n] for u in unpacked]
    n = "norm_mix_pre"
    results[n] = [r.reshape(given[n].shape) for r in _sum_adamw(
        dg0_all, given[n].reshape(8, 128), given["m_" + n].reshape(8, 128), given["v_" + n].reshape(8, 128), "adamw_" + n)]

    loss = jnp.sum(tail_all[:, 8, 0])
    order = ("w_in", "ln_v_gain", "ln_v_bias", "w_spatial", "b_spatial", "sinks", "w_a", "w_b", "w_o", "norm_mix_pre",
             "norm_mix_post", "w_ff_in", "w_ff_out", "norm_ff_pre", "norm_ff_post")
    out = [loss, grad_x.reshape(x.shape)]
    for k in range(4):
        out += [results[n][k] for n in order]
    return tuple(out)
```

```python
import jax
import jax.numpy as jnp
from jax import lax
from jax.experimental import pallas as pl
from jax.experimental.pallas import tpu as pltpu

F32 = jnp.float32
BF16 = jnp.bfloat16

N_DEV = 8
D = 1024
D_FF = 4096
IN_W = 5632
CHUNK = 128
GROUPS = 8
HEAD = 64
N_Q = 16
N_KV = 4
ROPE = 16
ROPE_THETA = 500000.0
EPS = 1e-6
OFF_Q, OFF_K, OFF_VA, OFF_GA, OFF_GB = 2048, 3072, 3328, 3584, 4608

ADAM_LR = 0.001
ADAM_B1 = 0.9
ADAM_B2 = 0.999
ADAM_EPS = 1e-08
ADAM_WD = 0.01
ADAM_STEP = 10

VMEM_LIMIT = 62 * 1024 * 1024

SDS = jax.ShapeDtypeStruct
MESH = pl.DeviceIdType.MESH


def _params(n_axes):
    return pltpu.CompilerParams(dimension_semantics=("arbitrary",) * n_axes, vmem_limit_bytes=VMEM_LIMIT)


def _nt(a, b):
    return lax.dot_general(a, b, (((1,), (1,)), ((), ())), preferred_element_type=F32)


def _tn(a, b):
    return lax.dot_general(a, b, (((0,), (0,)), ((), ())), preferred_element_type=F32)


def _nn(a, b):
    return jnp.dot(a, b, preferred_element_type=F32)


def _gelu(x):
    t = jnp.tanh(0.7978845608028654 * (x + 0.044715 * (x * x * x)))
    return 0.5 * x * (1.0 + t), t


def _gelu_grad(x, t):
    return 0.5 * (1.0 + t) + 0.5 * x * (1.0 - t * t) * (0.7978845608028654 * (1.0 + 3.0 * 0.044715 * x * x))


def _sigmoid(x):
    return 1.0 / (1.0 + jnp.exp(-x))


def _rms_stats(v):
    r = lax.rsqrt(jnp.mean(v * v, axis=-1, keepdims=True) + EPS)
    return r, v * r


def _rms_bwd(d, vhat, r, g):
    gd = g * d
    return r * (gd - vhat * jnp.mean(gd * vhat, axis=-1, keepdims=True))


def _colsum(v):
    return jnp.sum(v, axis=0, keepdims=True)


_ANY = pl.BlockSpec(memory_space=pl.ANY)


def _after(body, n_in, after):
    if after is None:
        return body, [], []

    def ordered(*refs):
        return body(*refs[:n_in], *refs[n_in + 1:])

    return ordered, [_ANY], [after]


def _rms_pre(x, g0, after=None):
    T = x.shape[0]
    tm = min(T, 1024)

    def body(x_ref, g_ref, h_ref):
        _, xh = _rms_stats(x_ref[...])
        h_ref[...] = (xh * g_ref[...]).astype(BF16)

    body, dep_specs, deps = _after(body, 2, after)
    return pl.pallas_call(
        body, name="rms_pre", grid=(T // tm,),
        in_specs=[pl.BlockSpec((tm, D), lambda i: (i, 0)), pl.BlockSpec((1, D), lambda i: (0, 0))] + dep_specs,
        out_specs=pl.BlockSpec((tm, D), lambda i: (i, 0)),
        out_shape=SDS((T, D), BF16),
        compiler_params=_params(1),
    )(x, g0, *deps)


def _fwd_in(h, win_t):
    T = h.shape[0]
    tm, tn = min(T, 512), 1408

    def body(h_ref, w_ref, p_ref):
        for j in range(IN_W // tn):
            cols = slice(j * tn, (j + 1) * tn)
            p_ref[:, cols] = _nt(h_ref[...], w_ref[cols, :]).astype(BF16)

    return pl.pallas_call(
        body, name="fwd_in", grid=(T // tm,),
        in_specs=[pl.BlockSpec((tm, D), lambda i: (i, 0)), _resident((IN_W, D))],
        out_specs=pl.BlockSpec((tm, IN_W), lambda i: (i, 0)),
        out_shape=SDS((T, IN_W), BF16),
        compiler_params=_params(1),
    )(h, win_t)


def _sgu_forward_parts(u_ref, vs_ref, lng_ref, lnb_ref):
    u = u_ref[...].astype(F32)
    vs = vs_ref[...].astype(F32)
    gu, tu = _gelu(u)
    gv, tv = _gelu(vs)
    mu = jnp.mean(gv, axis=-1, keepdims=True)
    dv = gv - mu
    rstd = lax.rsqrt(jnp.mean(dv * dv, axis=-1, keepdims=True) + EPS)
    vhat = dv * rstd
    vn = (vhat * lng_ref[...] + lnb_ref[...]).astype(BF16)
    return u, vs, gu, tu, tv, rstd, vhat, vn


def _masked_ws(ws_ref, g):
    row = lax.broadcasted_iota(jnp.int32, (CHUNK, CHUNK), 0)
    col = lax.broadcasted_iota(jnp.int32, (CHUNK, CHUNK), 1)
    return jnp.where(row >= col, ws_ref[g], 0.0).astype(BF16)


def _fwd_sgu(proj, lng, lnb, ws, bst, after=None):
    T = proj.shape[0]
    tc = min(T, 512)

    def body(u_ref, vs_ref, lng_ref, lnb_ref, ws_ref, bst_ref, a_ref):
        _, _, gu, _, _, _, _, vn = _sgu_forward_parts(u_ref, vs_ref, lng_ref, lnb_ref)
        for g in range(GROUPS):
            wm = _masked_ws(ws_ref, g)
            cols = slice(g * CHUNK, (g + 1) * CHUNK)
            for c in range(tc // CHUNK):
                rows = slice(c * CHUNK, (c + 1) * CHUNK)
                mixed = _nn(wm, vn[rows, cols]) + bst_ref[:, g:g + 1]
                a_ref[rows, cols] = (gu[rows, cols] * mixed).astype(BF16)

    body, dep_specs, deps = _after(body, 6, after)
    return pl.pallas_call(
        body, name="fwd_sgu", grid=(T // tc,),
        in_specs=[pl.BlockSpec((tc, D), lambda i: (i, 0)), pl.BlockSpec((tc, D), lambda i: (i, 1)),
                  pl.BlockSpec((1, D), lambda i: (0, 0)), pl.BlockSpec((1, D), lambda i: (0, 0)),
                  pl.BlockSpec((GROUPS, CHUNK, CHUNK), lambda i: (0, 0, 0)),
                  pl.BlockSpec((CHUNK, GROUPS), lambda i: (0, 0))] + dep_specs,
        out_specs=pl.BlockSpec((tc, D), lambda i: (i, 0)),
        out_shape=SDS((T, D), BF16),
        compiler_params=_params(1),
    )(proj, proj, lng, lnb, ws, bst, *deps)


def _rope_tables(posf, invf, sgn, after=None):
    T = posf.shape[0]
    tr = min(T, 1024)

    def body(pos_ref, invf_ref, sgn_ref, c_ref, s_ref):
        ang = pos_ref[...] * invf_ref[...]
        c_ref[...] = jnp.cos(ang)
        s = jnp.sin(ang)
        s_ref[:, :128] = jnp.where(sgn_ref[...] < 0.0, -s, 0.0)
        s_ref[:, 128:] = jnp.where(sgn_ref[...] > 0.0, s, 0.0)

    body, dep_specs, deps = _after(body, 3, after)
    return pl.pallas_call(
        body, name="rope_tables", grid=(T // tr,),
        in_specs=[pl.BlockSpec((tr, 1), lambda i: (i, 0)), pl.BlockSpec((1, 128), lambda i: (0, 0)),
                  pl.BlockSpec((1, 128), lambda i: (0, 0))] + dep_specs,
        out_specs=[pl.BlockSpec((tr, 128), lambda i: (i, 0)), pl.BlockSpec((tr, 256), lambda i: (i, 0))],
        out_shape=[SDS((T, 128), F32), SDS((T, 256), F32)],
        compiler_params=_params(1),
    )(posf, invf, sgn, *deps)


def _rope(v, c, s):
    v = v.astype(F32)
    return v * c + pltpu.roll(v, 128 - ROPE // 2, 1) * s[:, :128] + pltpu.roll(v, ROPE // 2, 1) * s[:, 128:]


def _rope_bwd(dv, c, s):
    return dv * c + pltpu.roll(dv * s[:, :128], ROPE // 2, 1) + pltpu.roll(dv * s[:, 128:], 128 - ROPE // 2, 1)


def _fold_masks(first):
    jj = lax.broadcasted_iota(jnp.int32, (CHUNK, CHUNK), 0)
    t = lax.broadcasted_iota(jnp.int32, (CHUNK, CHUNK), 1)
    prev = jj > t
    return prev, jnp.where(prev & first, -1e30, 0.0)


def _fold(band, prev):
    return jnp.where(prev, band[:CHUNK], band[CHUNK:])


def _unfold(folded, prev):
    return jnp.concatenate([jnp.where(prev, folded, 0.0), jnp.where(prev, 0.0, folded)], axis=0)


def _softmax_sink(s, sink, key_axis):
    m = jnp.maximum(jnp.max(s, axis=key_axis, keepdims=True), sink)
    p = jnp.exp(s - m)
    esink = jnp.exp(sink - m)
    inv = 1.0 / (jnp.sum(p, axis=key_axis, keepdims=True) + esink)
    return p * inv, esink * inv


def _head_pair_operand(slab, g):
    lo = lax.broadcasted_iota(jnp.int32, slab.shape, 1) < HEAD
    if g % 2 == 0:
        first = jnp.where(lo, slab, 0.0)
        second = pltpu.roll(first, HEAD, 1)
    else:
        second = jnp.where(lo, 0.0, slab)
        first = pltpu.roll(second, HEAD, 1)
    return jnp.concatenate([first, second], axis=0).astype(BF16)


def _head_pair_gradient(acc, g):
    top, bot = acc[:2 * CHUNK], acc[2 * CHUNK:]
    lo = lax.broadcasted_iota(jnp.int32, top.shape, 1) < HEAD
    if g % 2 == 0:
        return jnp.where(lo, top, 0.0) + pltpu.roll(jnp.where(lo, 0.0, bot), HEAD, 1)
    return pltpu.roll(jnp.where(lo, top, 0.0), HEAD, 1) + jnp.where(lo, 0.0, bot)


PAIRS_PER_KV = N_Q // N_KV // 2
KV_W = N_KV * HEAD


def _band(prev_ref, cur_ref, cols=slice(None)):
    return jnp.concatenate([prev_ref[:, cols], cur_ref[:, cols]], axis=0)


def _fwd_attn(proj, cos, sin, sinks):
    T = proj.shape[0]
    nb = T // CHUNK
    cur = lambda i: i
    prev = lambda i: jnp.maximum(i - 1, 0)

    def body(q_ref, kp_ref, kc_ref, vp_ref, vc_ref, cp_ref, cc_ref, sp_ref, sc_ref, sink_ref,
             o_ref, qr_ref, kr_ref, p_ref, psink_ref):
        prev_slot, bias = _fold_masks(pl.program_id(0) == 0)
        c_band, s_band = _band(cp_ref, cc_ref), _band(sp_ref, sc_ref)
        for j in range(KV_W // 128):
            cols = slice(j * 128, (j + 1) * 128)
            k_slab = _rope(_band(kp_ref, kc_ref, cols), c_band, s_band)
            kr_ref[:, cols] = k_slab[CHUNK:].astype(BF16)
            v_slab = _band(vp_ref, vc_ref, cols).astype(F32)
            for g in (2 * j, 2 * j + 1):
                k2 = _head_pair_operand(k_slab, g)
                v2 = _head_pair_operand(v_slab, g)
                pairs = [g * PAIRS_PER_KV + r for r in range(PAIRS_PER_KV)]
                qps = []
                for pair in pairs:
                    lanes = slice(pair * 128, (pair + 1) * 128)
                    qps.append((_rope(q_ref[:, lanes], cc_ref[...], sc_ref[...]) * (HEAD ** -0.5)).astype(BF16))
                    qr_ref[:, lanes] = qps[-1]
                s2 = _nt(k2, jnp.concatenate(qps, axis=0))
                pcols = []
                for r, pair in enumerate(pairs):
                    ps = []
                    for e in range(2):
                        head = 2 * pair + e
                        s = _fold(s2[e * 2 * CHUNK:(e + 1) * 2 * CHUNK, r * 128:(r + 1) * 128], prev_slot) + bias
                        p, psink = _softmax_sink(s, sink_ref[head], 0)
                        p = p.astype(BF16)
                        p_ref[head] = p
                        psink_ref[head:head + 1, :] = psink
                        ps.append(_unfold(p, prev_slot))
                    pcols.append(jnp.concatenate(ps, axis=0))
                o = _tn(jnp.concatenate(pcols, axis=1), v2).astype(BF16)
                for r, pair in enumerate(pairs):
                    o_ref[:, pair * 128:(pair + 1) * 128] = o[r * CHUNK:(r + 1) * CHUNK]

    table = lambda which, width: pl.BlockSpec((CHUNK, width), lambda i: (which(i), 0))
    return pl.pallas_call(
        body, name="fwd_attn", grid=(nb,),
        in_specs=[pl.BlockSpec((CHUNK, D), lambda i: (i, OFF_Q // D)),
                  pl.BlockSpec((CHUNK, KV_W), lambda i: (prev(i), OFF_K // KV_W)),
                  pl.BlockSpec((CHUNK, KV_W), lambda i: (i, OFF_K // KV_W)),
                  pl.BlockSpec((CHUNK, KV_W), lambda i: (prev(i), OFF_VA // KV_W)),
                  pl.BlockSpec((CHUNK, KV_W), lambda i: (i, OFF_VA // KV_W)),
                  table(prev, 128), table(cur, 128), table(prev, 256), table(cur, 256),
                  pl.BlockSpec(memory_space=pltpu.SMEM)],
        out_specs=[pl.BlockSpec((CHUNK, D), lambda i: (i, 0)), pl.BlockSpec((CHUNK, D), lambda i: (i, 0)),
                   pl.BlockSpec((CHUNK, KV_W), lambda i: (i, 0)),
                   pl.BlockSpec((None, N_Q, CHUNK, CHUNK), lambda i: (i, 0, 0, 0)),
                   pl.BlockSpec((None, N_Q, CHUNK), lambda i: (i, 0, 0))],
        out_shape=[SDS((T, D), BF16), SDS((T, D), BF16), SDS((T, KV_W), BF16),
                   SDS((nb, N_Q, CHUNK, CHUNK), BF16), SDS((nb, N_Q, CHUNK), F32)],
        compiler_params=_params(1),
    )(proj, proj, proj, proj, proj, cos, cos, sin, sin, sinks)


def _row_halves(tm):
    return [slice(0, tm // 2), slice(tm // 2, tm)] if tm % 32 == 0 else [slice(0, tm)]


def _fwd_mix(a, att, proj, x, wa, wb, wo, g1, g2):
    T = x.shape[0]
    tm = min(T, 512)
    half = D // 2

    def body(a_ref, att_ref, ga0, ga1, gb0, gb1, x_ref, wa_ref, wb_ref, wo_ref, g1_ref, g2_ref,
             mg_ref, a2_ref, b2_ref, mix_ref, x1_ref, hf_ref):
        for rows in _row_halves(tm):
            a2 = _nn(a_ref[rows, :], wa_ref[...])
            b2 = _nn(att_ref[rows, :], wb_ref[...])
            ga = jnp.concatenate([ga0[rows, :], ga1[rows, :]], axis=1).astype(F32)
            gb = jnp.concatenate([gb0[rows, :], gb1[rows, :]], axis=1).astype(F32)
            merged = (_sigmoid(ga) * a2 + _sigmoid(gb) * b2).astype(BF16)
            a2_ref[rows, :] = a2.astype(BF16)
            b2_ref[rows, :] = b2.astype(BF16)
            mg_ref[rows, :] = merged
            mix = _nn(merged, wo_ref[...])
            mix_ref[rows, :] = mix
            _, mh = _rms_stats(mix)
            x1 = x_ref[rows, :] + mh * g1_ref[...]
            x1_ref[rows, :] = x1
            _, xh = _rms_stats(x1)
            hf_ref[rows, :] = (xh * g2_ref[...]).astype(BF16)

    row = lambda i: (i, 0)
    const = lambda i: (0, 0)
    gspec = lambda off: pl.BlockSpec((tm, half), lambda i: (i, off // half))
    return pl.pallas_call(
        body, name="fwd_mix", grid=(T // tm,),
        in_specs=[pl.BlockSpec((tm, D), row), pl.BlockSpec((tm, D), row),
                  gspec(OFF_GA), gspec(OFF_GA + half), gspec(OFF_GB), gspec(OFF_GB + half),
                  pl.BlockSpec((tm, D), row), _resident((D, D)), _resident((D, D)),
                  _resident((D, D)), pl.BlockSpec((1, D), const), pl.BlockSpec((1, D), const)],
        out_specs=[pl.BlockSpec((tm, D), row)] * 6,
        out_shape=[SDS((T, D), BF16), SDS((T, D), BF16), SDS((T, D), BF16), SDS((T, D), F32), SDS((T, D), F32),
                   SDS((T, D), BF16)],
        compiler_params=_params(1),
    )(a, att, proj, proj, proj, proj, x, wa, wb, wo, g1, g2)


FF_SPLIT = N_DEV
FF_TILE = D_FF // FF_SPLIT


def _fwd_ff(hf, wfi3, wfo, x1, tgt, g3):
    T = hf.shape[0]
    tm = min(T, 512)

    def body(hf_ref, wfi_ref, wfo_ref, x1_ref, tgt_ref, g3_ref, f_ref, dy_ref, dff_ref, dg3_ref, loss_ref, r_s):
        @pl.when(pl.program_id(0) == 0)
        def _():
            dg3_ref[...] = jnp.zeros_like(dg3_ref)
            loss_ref[...] = jnp.zeros_like(loss_ref)

        hf_t = hf_ref[...]
        for s in range(FF_SPLIT):
            cols = slice(s * FF_TILE, (s + 1) * FF_TILE)
            f = _nn(hf_t, wfi_ref[s]).astype(BF16)
            f_ref[:, cols] = f
            rl = jnp.maximum(f.astype(F32), 0.0)
            r_s[:, cols] = (rl * rl).astype(BF16)
        r3, fh = _rms_stats(_nn(r_s[...], wfo_ref[...]))
        e = x1_ref[...] + fh * g3_ref[...] - tgt_ref[...]
        loss_ref[...] += jnp.sum(e * e) * (0.5 / D)
        dy = e * (1.0 / D)
        dy_ref[...] = dy
        dg3_ref[...] += _colsum(dy * fh)
        dff_ref[...] = _rms_bwd(dy, fh, r3, g3_ref[...]).astype(BF16)

    row = lambda i: (i, 0)
    const = lambda i: (0, 0)
    return pl.pallas_call(
        body, name="fwd_ff", grid=(T // tm,),
        in_specs=[pl.BlockSpec((tm, D), row), _resident((FF_SPLIT, D, FF_TILE)), _resident((D_FF, D)),
                  pl.BlockSpec((tm, D), row),
                  pl.BlockSpec((tm, D), row), pl.BlockSpec((1, D), const)],
        out_specs=[pl.BlockSpec((tm, D_FF), row), pl.BlockSpec((tm, D), row),
                   pl.BlockSpec((tm, D), row), pl.BlockSpec((1, D), const), pl.BlockSpec((1, 128), const)],
        out_shape=[SDS((T, D_FF), BF16), SDS((T, D), F32), SDS((T, D), BF16), SDS((1, D), F32), SDS((1, 128), F32)],
        scratch_shapes=[pltpu.VMEM((tm, D_FF), BF16)],
        compiler_params=_params(1),
    )(hf, wfi3, wfo, x1, tgt, g3)


def _bwd_ff(dff, f, wfi3, wfo, x1, dy, mix, g1, g2):
    T = dff.shape[0]
    tm = min(T, 512)

    def body(dff_ref, f_ref, wfi_ref, wfo_ref, x1_ref, dy_ref, mix_ref, g1_ref, g2_ref,
             df_ref, dx1_ref, dmix_ref, dg2_ref, dg1_ref):
        @pl.when(pl.program_id(0) == 0)
        def _():
            dg2_ref[...] = jnp.zeros_like(dg2_ref)
            dg1_ref[...] = jnp.zeros_like(dg1_ref)

        dff_t = dff_ref[...]
        dhf = None
        for s in range(FF_SPLIT):
            cols = slice(s * FF_TILE, (s + 1) * FF_TILE)
            dr = _nt(dff_t, wfo_ref[cols, :])
            df = (dr * (2.0 * jnp.maximum(f_ref[:, cols].astype(F32), 0.0))).astype(BF16)
            df_ref[:, cols] = df
            part = _nt(df, wfi_ref[s])
            dhf = part if dhf is None else dhf + part
        r2, xh = _rms_stats(x1_ref[...])
        dg2_ref[...] += _colsum(dhf * xh)
        dx1 = dy_ref[...] + _rms_bwd(dhf, xh, r2, g2_ref[...])
        dx1_ref[...] = dx1
        r1, mh = _rms_stats(mix_ref[...])
        dg1_ref[...] += _colsum(dx1 * mh)
        dmix_ref[...] = _rms_bwd(dx1, mh, r1, g1_ref[...]).astype(BF16)

    row = lambda i: (i, 0)
    const = lambda i: (0, 0)
    return pl.pallas_call(
        body, name="bwd_ff", grid=(T // tm,),
        in_specs=[pl.BlockSpec((tm, D), row), pl.BlockSpec((tm, D_FF), row),
                  _resident((FF_SPLIT, D, FF_TILE)), _resident((D_FF, D)),
                  pl.BlockSpec((tm, D), row), pl.BlockSpec((tm, D), row), pl.BlockSpec((tm, D), row),
                  pl.BlockSpec((1, D), const), pl.BlockSpec((1, D), const)],
        out_specs=[pl.BlockSpec((tm, D_FF), row), pl.BlockSpec((tm, D), row),
                   pl.BlockSpec((tm, D), row), pl.BlockSpec((1, D), const), pl.BlockSpec((1, D), const)],
        out_shape=[SDS((T, D_FF), BF16), SDS((T, D), F32), SDS((T, D), BF16), SDS((1, D), F32), SDS((1, D), F32)],
        compiler_params=_params(1),
    )(dff, f, wfi3, wfo, x1, dy, mix, g1, g2)


def _wgrad_ff(hf, df, f, dff):
    T = hf.shape[0]
    tt = min(T, 2048)
    slabs = 2
    wide = slabs * FF_TILE

    def body(hf_ref, df_ref, f_ref, dff_ref, dwfi_ref, dwfo_ref, acc_i, acc_o):
        t = pl.program_id(1)

        @pl.when(t == 0)
        def _():
            acc_i[...] = jnp.zeros_like(acc_i)
            acc_o[...] = jnp.zeros_like(acc_o)

        acc_i[...] += _tn(hf_ref[...], df_ref[...])
        rl = jnp.maximum(f_ref[...].astype(F32), 0.0)
        acc_o[...] += _tn((rl * rl).astype(BF16), dff_ref[...])

        @pl.when(t == T // tt - 1)
        def _():
            for s in range(slabs):
                dwfi_ref[s] = acc_i[:, s * FF_TILE:(s + 1) * FF_TILE].astype(BF16)
            dwfo_ref[...] = acc_o[...].astype(BF16)

    return pl.pallas_call(
        body, name="wgrad_ff", grid=(D_FF // wide, T // tt),
        in_specs=[pl.BlockSpec((tt, D), lambda p, t: (t, 0)), pl.BlockSpec((tt, wide), lambda p, t: (t, p)),
                  pl.BlockSpec((tt, wide), lambda p, t: (t, p)), pl.BlockSpec((tt, D), lambda p, t: (t, 0))],
        out_specs=[pl.BlockSpec((slabs, D, FF_TILE), lambda p, t: (p, 0, 0)), pl.BlockSpec((wide, D), lambda p, t: (p, 0))],
        out_shape=[SDS((FF_SPLIT, D, FF_TILE), BF16), SDS((D_FF, D), BF16)],
        scratch_shapes=[pltpu.VMEM((D, wide), F32), pltpu.VMEM((wide, D), F32)],
        compiler_params=_params(2),
    )(hf, df, f, dff)


def _bwd_mix(dmix, proj, a2, b2, wo, wa, wb, after=None):
    T = dmix.shape[0]
    tm = min(T, 512)
    half = D // 2

    def body(dmix_ref, ga0, ga1, gb0, gb1, a2_ref, b2_ref, wo_ref, wa_ref, wb_ref,
             da2_ref, db2_ref, dg_ref, da_ref, datt_ref):
        for rows in _row_halves(tm):
            dmg = _nt(dmix_ref[rows, :], wo_ref[...])
            sa = _sigmoid(jnp.concatenate([ga0[rows, :], ga1[rows, :]], axis=1).astype(F32))
            sb = _sigmoid(jnp.concatenate([gb0[rows, :], gb1[rows, :]], axis=1).astype(F32))
            da2 = (dmg * sa).astype(BF16)
            db2 = (dmg * sb).astype(BF16)
            da2_ref[rows, :] = da2
            db2_ref[rows, :] = db2
            dg_ref[rows, :D] = (dmg * a2_ref[rows, :].astype(F32) * (sa * (1.0 - sa))).astype(BF16)
            dg_ref[rows, D:] = (dmg * b2_ref[rows, :].astype(F32) * (sb * (1.0 - sb))).astype(BF16)
            da_ref[rows, :] = _nt(da2, wa_ref[...]).astype(BF16)
            datt_ref[rows, :] = _nt(db2, wb_ref[...]).astype(BF16)

    row = lambda i: (i, 0)
    const = lambda i: (0, 0)
    gspec = lambda off: pl.BlockSpec((tm, half), lambda i: (i, off // half))
    body, dep_specs, deps = _after(body, 10, after)
    return pl.pallas_call(
        body, name="bwd_mix", grid=(T // tm,),
        in_specs=[pl.BlockSpec((tm, D), row), gspec(OFF_GA), gspec(OFF_GA + half), gspec(OFF_GB), gspec(OFF_GB + half),
                  pl.BlockSpec((tm, D), row), pl.BlockSpec((tm, D), row),
                  _resident((D, D)), _resident((D, D)), _resident((D, D))] + dep_specs,
        out_specs=[pl.BlockSpec((tm, D), row), pl.BlockSpec((tm, D), row), pl.BlockSpec((tm, 2 * D), row),
                   pl.BlockSpec((tm, D), row), pl.BlockSpec((tm, D), row)],
        out_shape=[SDS((T, D), BF16), SDS((T, D), BF16), SDS((T, 2 * D), BF16), SDS((T, D), BF16), SDS((T, D), BF16)],
        compiler_params=_params(1),
    )(dmix, proj, proj, proj, proj, a2, b2, wo, wa, wb, *deps)


def _wgrad_mix(merged, dmix, a, da2, att, db2):
    T = merged.shape[0]
    tt = min(T, 1024)

    def body(mg_ref, dmix_ref, a_ref, da2_ref, att_ref, db2_ref, dwo_ref, dwa_ref, dwb_ref, acc):
        t = pl.program_id(0)

        @pl.when(t == 0)
        def _():
            acc[...] = jnp.zeros_like(acc)

        acc[0] += _tn(mg_ref[...], dmix_ref[...])
        acc[1] += _tn(a_ref[...], da2_ref[...])
        acc[2] += _tn(att_ref[...], db2_ref[...])

        @pl.when(t == T // tt - 1)
        def _():
            dwo_ref[...] = acc[0].astype(BF16)
            dwa_ref[...] = acc[1].astype(BF16)
            dwb_ref[...] = acc[2].astype(BF16)

    return pl.pallas_call(
        body, name="wgrad_mix", grid=(T // tt,),
        in_specs=[pl.BlockSpec((tt, D), lambda t: (t, 0))] * 6,
        out_specs=[pl.BlockSpec((D, D), lambda t: (0, 0))] * 3,
        out_shape=[SDS((D, D), BF16)] * 3,
        scratch_shapes=[pltpu.VMEM((3, D, D), F32)],
        compiler_params=_params(1),
    )(merged, dmix, a, da2, att, db2)


def _bwd_attn(qr, kr, probs, psink, proj, cos, sin, datt, after=None):
    T = proj.shape[0]
    nb = T // CHUNK
    cur = lambda i: jnp.minimum(i, nb - 1)
    prev = lambda i: jnp.maximum(jnp.minimum(i, nb - 1) - 1, 0)

    def body(q_ref, kp_ref, kc_ref, vp_ref, vc_ref, cp_ref, cc_ref, sp_ref, sc_ref, p_ref, psink_ref, do_ref,
             dq_ref, dkv_ref, dsink_ref, carry_k, carry_v):
        i = pl.program_id(0)

        @pl.when(i == 0)
        def _():
            carry_k[...] = jnp.zeros_like(carry_k)
            carry_v[...] = jnp.zeros_like(carry_v)
            dsink_ref[...] = jnp.zeros_like(dsink_ref)

        @pl.when(i < nb)
        def _():
            prev_slot, _ = _fold_masks(i == 0)
            c_band, s_band = _band(cp_ref, cc_ref), _band(sp_ref, sc_ref)
            lane = lax.broadcasted_iota(jnp.int32, (1, 128), 1)
            dsink = jnp.zeros((1, 128), F32)
            for j in range(KV_W // 128):
                cols = slice(j * 128, (j + 1) * 128)
                k_slab = _band(kp_ref, kc_ref, cols).astype(F32)
                v_slab = _band(vp_ref, vc_ref, cols).astype(F32)
                dk_slab = jnp.zeros((2 * CHUNK, 128), F32)
                dv_slab = jnp.zeros((2 * CHUNK, 128), F32)
                for g in (2 * j, 2 * j + 1):
                    k2 = _head_pair_operand(k_slab, g)
                    v2 = _head_pair_operand(v_slab, g)
                    pairs = [g * PAIRS_PER_KV + r for r in range(PAIRS_PER_KV)]
                    q_stack = jnp.concatenate([q_ref[:, pr * 128:(pr + 1) * 128] for pr in pairs], axis=0)
                    do_stack = jnp.concatenate([do_ref[:, pr * 128:(pr + 1) * 128] for pr in pairs], axis=0)
                    dp2 = _nt(v2, do_stack)
                    pcols, dscols = [], []
                    for r, pair in enumerate(pairs):
                        ps, dss = [], []
                        for e in range(2):
                            head = 2 * pair + e
                            p_b = p_ref[head]
                            p = p_b.astype(F32)
                            dp = _fold(dp2[e * 2 * CHUNK:(e + 1) * 2 * CHUNK, r * 128:(r + 1) * 128], prev_slot)
                            delta = jnp.sum(p * dp, axis=0, keepdims=True)
                            ps.append(_unfold(p_b, prev_slot))
                            dss.append(_unfold((p * (dp - delta)).astype(BF16), prev_slot))
                            dsink = dsink + jnp.where(lane == head, -jnp.sum(psink_ref[head:head + 1, :] * delta), 0.0)
                        pcols.append(jnp.concatenate(ps, axis=0))
                        dscols.append(jnp.concatenate(dss, axis=0))
                    ds2 = jnp.concatenate(dscols, axis=1)
                    dq = _tn(ds2, k2) * (HEAD ** -0.5)
                    for r, pair in enumerate(pairs):
                        dq_ref[:, pair * 128:(pair + 1) * 128] = _rope_bwd(
                            dq[r * CHUNK:(r + 1) * CHUNK], cc_ref[...], sc_ref[...]).astype(BF16)
                    dk_slab = dk_slab + _head_pair_gradient(_nn(ds2, q_stack), g)
                    dv_slab = dv_slab + _head_pair_gradient(_nn(jnp.concatenate(pcols, axis=1), do_stack), g)
                dk_slab = _rope_bwd(dk_slab, c_band, s_band)
                vcols = slice(KV_W + j * 128, KV_W + (j + 1) * 128)
                dkv_ref[:, cols] = (carry_k[:, cols] + dk_slab[:CHUNK]).astype(BF16)
                dkv_ref[:, vcols] = (carry_v[:, cols] + dv_slab[:CHUNK]).astype(BF16)
                carry_k[:, cols] = dk_slab[CHUNK:]
                carry_v[:, cols] = dv_slab[CHUNK:]
            dsink_ref[...] += dsink

        @pl.when(i == nb)
        def _():
            dkv_ref[:, :KV_W] = carry_k[...].astype(BF16)
            dkv_ref[:, KV_W:] = carry_v[...].astype(BF16)

    table = lambda which, width: pl.BlockSpec((CHUNK, width), lambda i: (which(i), 0))
    body, dep_specs, deps = _after(body, 12, after)
    return pl.pallas_call(
        body, name="bwd_attn", grid=(nb + 1,),
        in_specs=[pl.BlockSpec((CHUNK, D), lambda i: (cur(i), 0)),
                  pl.BlockSpec((CHUNK, KV_W), lambda i: (prev(i), 0)),
                  pl.BlockSpec((CHUNK, KV_W), lambda i: (cur(i), 0)),
                  pl.BlockSpec((CHUNK, KV_W), lambda i: (prev(i), OFF_VA // KV_W)),
                  pl.BlockSpec((CHUNK, KV_W), lambda i: (cur(i), OFF_VA // KV_W)),
                  table(prev, 128), table(cur, 128), table(prev, 256), table(cur, 256),
                  pl.BlockSpec((None, N_Q, CHUNK, CHUNK), lambda i: (cur(i), 0, 0, 0)),
                  pl.BlockSpec((None, N_Q, CHUNK), lambda i: (cur(i), 0, 0)),
                  pl.BlockSpec((CHUNK, D), lambda i: (cur(i), 0))] + dep_specs,
        out_specs=[pl.BlockSpec((CHUNK, D), lambda i: (cur(i), 0)),
                   pl.BlockSpec((CHUNK, 2 * KV_W), lambda i: (jnp.maximum(i - 1, 0), 0)),
                   pl.BlockSpec((1, 128), lambda i: (0, 0))],
        out_shape=[SDS((T, D), BF16), SDS((T, 2 * KV_W), BF16), SDS((1, 128), F32)],
        scratch_shapes=[pltpu.VMEM((CHUNK, KV_W), F32), pltpu.VMEM((CHUNK, KV_W), F32)],
        compiler_params=_params(1),
    )(qr, kr, kr, proj, proj, cos, cos, sin, sin, probs, psink, datt, *deps)


def _bwd_sgu(proj, da, lng, lnb, ws, bst):
    T = proj.shape[0]
    tc = min(T, 512)
    nsteps = T // tc

    def body(u_ref, vs_ref, da_ref, lng_ref, lnb_ref, ws_ref, bst_ref,
             duv_ref, dws_ref, dbs_ref, dlng_ref, dlnb_ref, dvn_s, dgu_s, dmx_sum):
        i = pl.program_id(0)

        @pl.when(i == 0)
        def _():
            dws_ref[...] = jnp.zeros_like(dws_ref)
            dlng_ref[...] = jnp.zeros_like(dlng_ref)
            dlnb_ref[...] = jnp.zeros_like(dlnb_ref)
            dmx_sum[...] = jnp.zeros_like(dmx_sum)

        u, vs, gu, tu, tv, rstd, vhat, vn = _sgu_forward_parts(u_ref, vs_ref, lng_ref, lnb_ref)
        da = da_ref[...].astype(F32)
        for g in range(GROUPS):
            wm = _masked_ws(ws_ref, g)
            cols = slice(g * CHUNK, (g + 1) * CHUNK)
            dws = jnp.zeros((CHUNK, CHUNK), F32)
            dsum = jnp.zeros((CHUNK, CHUNK), F32)
            for c in range(tc // CHUNK):
                rows = slice(c * CHUNK, (c + 1) * CHUNK)
                vn_cg = vn[rows, cols]
                mixed = _nn(wm, vn_cg) + bst_ref[:, g:g + 1]
                dgu_s[rows, cols] = da[rows, cols] * mixed
                dmx = da[rows, cols] * gu[rows, cols]
                dmxb = dmx.astype(BF16)
                dws = dws + _nt(dmxb, vn_cg)
                dsum = dsum + dmx
                dvn_s[rows, cols] = _tn(wm, dmxb)
            dws_ref[g] += dws
            dmx_sum[:, cols] += dsum
        dvn = dvn_s[...]
        dlng_ref[...] += _colsum(dvn * vhat)
        dlnb_ref[...] += _colsum(dvn)
        dvh = dvn * lng_ref[...]
        dgv = rstd * (dvh - jnp.mean(dvh, axis=-1, keepdims=True) - vhat * jnp.mean(dvh * vhat, axis=-1, keepdims=True))
        duv_ref[:, :D] = (dgu_s[...] * _gelu_grad(u, tu)).astype(BF16)
        duv_ref[:, D:] = (dgv * _gelu_grad(vs, tv)).astype(BF16)

        @pl.when(i == nsteps - 1)
        def _():
            row = lax.broadcasted_iota(jnp.int32, (CHUNK, CHUNK), 0)
            col = lax.broadcasted_iota(jnp.int32, (CHUNK, CHUNK), 1)
            for g in range(GROUPS):
                dws_ref[g] = jnp.where(row >= col, dws_ref[g], 0.0)
                dbs_ref[g:g + 1, :] = _colsum(dmx_sum[:, g * CHUNK:(g + 1) * CHUNK].T)

    const2 = lambda i: (0, 0)
    return pl.pallas_call(
        body, name="bwd_sgu", grid=(nsteps,),
        in_specs=[pl.BlockSpec((tc, D), lambda i: (i, 0)), pl.BlockSpec((tc, D), lambda i: (i, 1)),
                  pl.BlockSpec((tc, D), lambda i: (i, 0)), pl.BlockSpec((1, D), const2), pl.BlockSpec((1, D), const2),
                  pl.BlockSpec((GROUPS, CHUNK, CHUNK), lambda i: (0, 0, 0)), pl.BlockSpec((CHUNK, GROUPS), const2)],
        out_specs=[pl.BlockSpec((tc, 2 * D), lambda i: (i, 0)), pl.BlockSpec((GROUPS, CHUNK, CHUNK), lambda i: (0, 0, 0)),
                   pl.BlockSpec((GROUPS, CHUNK), const2), pl.BlockSpec((1, D), const2), pl.BlockSpec((1, D), const2)],
        out_shape=[SDS((T, 2 * D), BF16), SDS((GROUPS, CHUNK, CHUNK), F32), SDS((GROUPS, CHUNK), F32),
                   SDS((1, D), F32), SDS((1, D), F32)],
        scratch_shapes=[pltpu.VMEM((tc, D), F32), pltpu.VMEM((tc, D), F32), pltpu.VMEM((CHUNK, D), F32)],
        compiler_params=_params(1),
    )(proj, proj, da, lng, lnb, ws, bst)


IN_SEG_WIDTHS = (2 * D, D, 2 * N_KV * HEAD, 2 * D)


def _resident(shape):
    return pl.BlockSpec(shape, lambda *_: (0,) * len(shape), pipeline_mode=pl.Buffered(1))


def _bwd_in(duv, dq, dkv, dg, win_t, x, dx1, g0, after=None):
    T = x.shape[0]
    tm = min(T, 512)

    def body(duv_ref, dq_ref, dkv_ref, dg_ref, w_ref, x_ref, dx1_ref, g0_ref, gx_ref, dg0_ref):
        @pl.when(pl.program_id(0) == 0)
        def _():
            dg0_ref[...] = jnp.zeros_like(dg0_ref)

        dh, off = None, 0
        for ref, width in zip((duv_ref, dq_ref, dkv_ref, dg_ref), IN_SEG_WIDTHS):
            part = _nn(ref[...], w_ref[off:off + width, :])
            dh = part if dh is None else dh + part
            off += width
        r0, xh = _rms_stats(x_ref[...])
        dg0_ref[...] += _colsum(dh * xh)
        gx_ref[...] = dx1_ref[...] + _rms_bwd(dh, xh, r0, g0_ref[...])

    row = lambda i: (i, 0)
    body, dep_specs, deps = _after(body, 8, after)
    return pl.pallas_call(
        body, name="bwd_in", grid=(T // tm,),
        in_specs=[pl.BlockSpec((tm, w), row) for w in IN_SEG_WIDTHS] + [
            _resident((IN_W, D)), pl.BlockSpec((tm, D), row), pl.BlockSpec((tm, D), row),
            pl.BlockSpec((1, D), lambda i: (0, 0))] + dep_specs,
        out_specs=[pl.BlockSpec((tm, D), row), pl.BlockSpec((1, D), lambda i: (0, 0))],
        out_shape=[SDS((T, D), F32), SDS((1, D), F32)],
        compiler_params=_params(1),
    )(duv, dq, dkv, dg, win_t, x, dx1, g0, *deps)


def _wgrad_rows(h, segs, first_row, into, name):
    T = h.shape[0]
    tt = min(T, 2048)
    widths = [s.shape[1] for s in segs]
    rows = sum(widths)
    n_in = 1 + len(segs) + (into is not None)

    def body(*refs):
        h_ref, seg_refs = refs[0], refs[1:1 + len(segs)]
        dw_ref, acc, stage, sem = refs[n_in], refs[n_in + 1], refs[n_in + 2], refs[n_in + 3]
        t = pl.program_id(0)

        @pl.when(t == 0)
        def _():
            acc[...] = jnp.zeros_like(acc)

        off = 0
        for ref, width in zip(seg_refs, widths):
            acc[off:off + width, :] += _tn(ref[...], h_ref[...])
            off += width

        @pl.when(t == T // tt - 1)
        def _():
            stage[...] = acc[...].astype(BF16)
            out = pltpu.make_async_copy(stage, dw_ref.at[pl.ds(first_row, rows)], sem)
            out.start()
            out.wait()

    row = lambda t: (t, 0)
    return pl.pallas_call(
        body, name=name, grid=(T // tt,),
        in_specs=[pl.BlockSpec((tt, D), row)] + [pl.BlockSpec((tt, w), row) for w in widths] + [_ANY] * (into is not None),
        out_specs=_ANY,
        out_shape=SDS((IN_W, D), BF16),
        input_output_aliases={} if into is None else {n_in - 1: 0},
        scratch_shapes=[pltpu.VMEM((rows, D), F32), pltpu.VMEM((rows, D), BF16), pltpu.SemaphoreType.DMA],
        compiler_params=_params(1),
    )(h, *segs, *([] if into is None else [into]))


def _wgrad_in(h, duv, dq, dkv, dg):
    dw = _wgrad_rows(h, [dg], IN_SEG_WIDTHS[0] + IN_SEG_WIDTHS[1] + IN_SEG_WIDTHS[2], None, "wgrad_in_gates")
    dw = _wgrad_rows(h, [duv], 0, dw, "wgrad_in_uv")
    return _wgrad_rows(h, [dq, dkv], IN_SEG_WIDTHS[0], dw, "wgrad_in_qkv")


def _place():
    x, y, c = lax.axis_index("x"), lax.axis_index("y"), lax.axis_index("c")
    return x, y, c, 4 * x + 2 * y + c


def _peers(x, y, c):
    out = []
    for mask in range(1, N_DEV):
        px = 1 - x if mask & 4 else x
        py = 1 - y if mask & 2 else y
        pc = 1 - c if mask & 1 else c
        out.append(((px, py, pc), 4 * px + 2 * py + pc))
    return out


def _all_to_all(arrays, gather, name, after=None):
    n = len(arrays)

    def body(*refs):
        ins, outs = refs[:n], refs[n:2 * n]
        send_sems, recv_sems, local_sems = refs[2 * n:]
        x, y, c, me = _place()
        local, sends, recvs = [], [], []
        for a in range(n):
            src_own = ins[a] if gather[a] else ins[a].at[me]
            local.append(pltpu.make_async_copy(src_own, outs[a].at[me], local_sems.at[a]))
            for k, (peer, pid) in enumerate(_peers(x, y, c)):
                sem = a * (N_DEV - 1) + k
                src = ins[a] if gather[a] else ins[a].at[pid]
                sends.append(pltpu.make_async_remote_copy(
                    src_ref=src, dst_ref=outs[a].at[me], send_sem=send_sems.at[sem], recv_sem=recv_sems.at[sem],
                    device_id=peer, device_id_type=MESH))
                recvs.append(pltpu.make_async_remote_copy(
                    src_ref=src, dst_ref=outs[a].at[pid], send_sem=send_sems.at[sem], recv_sem=recv_sems.at[sem],
                    device_id=peer, device_id_type=MESH))
        for cp in local + sends:
            cp.start()
        for cp in recvs:
            cp.wait_recv()
        for cp in sends:
            cp.wait_send()
        for cp in local:
            cp.wait()

    out_shape = [SDS((N_DEV,) + a.shape if gt else a.shape, a.dtype) for a, gt in zip(arrays, gather)]
    nsem = n * (N_DEV - 1)
    body, dep_specs, deps = _after(body, n, after)
    return pl.pallas_call(
        body, name=name,
        in_specs=[pl.BlockSpec(memory_space=pl.ANY)] * n + dep_specs,
        out_specs=[pl.BlockSpec(memory_space=pl.ANY)] * n,
        out_shape=out_shape,
        scratch_shapes=[pltpu.SemaphoreType.DMA((nsem,)), pltpu.SemaphoreType.DMA((nsem,)), pltpu.SemaphoreType.DMA((n,))],
    )(*arrays, *deps)


_HBM = pl.BlockSpec(memory_space=pltpu.HBM)
_SEM = pl.BlockSpec(memory_space=pltpu.SEMAPHORE)
_EFFECT = pltpu.SideEffectType.DATAFLOW_SIDE_EFFECTING
GATHER = "gather"
SCATTER = "scatter"
SPREAD = "spread"


def _zone_shape(a, mode):
    if mode == GATHER:
        return (N_DEV,) + a.shape
    return (N_DEV - 1,) + (a.shape[1:] if mode == SCATTER else a.shape)


def _start_copies(arrays, modes, name, after=None):
    n = len(arrays)
    zones = [lax.empty(_zone_shape(a, m), a.dtype) for a, m in zip(arrays, modes)]

    def body(*refs):
        ins, lands = refs[:n], refs[n:2 * n]
        send_sems, recv_sems = refs[-2 * n - 3], refs[-2 * n - 2]
        token = refs[-1]
        x, y, c, me = _place()
        for a in range(n):
            for k, (peer, pid) in enumerate(_peers(x, y, c)):
                src = ins[a].at[pid] if modes[a] == SCATTER else ins[a]
                dst = lands[a].at[me] if modes[a] == GATHER else lands[a].at[k]
                pltpu.make_async_remote_copy(src_ref=src, dst_ref=dst, send_sem=send_sems.at[a], recv_sem=recv_sems.at[a],
                                             device_id=peer, device_id_type=MESH).start()
        token[...] = jnp.zeros_like(token)

    hbm = lambda a: pltpu.HBM(a.shape, a.dtype)
    sems = pltpu.SemaphoreType.DMA((n,))
    extra = [] if after is None else [after]
    operands = [pltpu.with_memory_space_constraint(a, pltpu.HBM) for a in list(arrays) + zones]
    res = pl.pallas_call(
        body, name=name,
        out_shape=(sems, sems, *[hbm(a) for a in arrays], *[hbm(z) for z in zones], SDS((8, 128), F32)),
        in_specs=[_HBM] * (2 * n) + [_ANY] * len(extra),
        out_specs=(_SEM, _SEM, *[_HBM] * (2 * n), pl.BlockSpec(memory_space=pltpu.VMEM)),
        input_output_aliases={i: 2 + i for i in range(2 * n)},
        compiler_params=pltpu.CompilerParams(has_side_effects=_EFFECT),
    )(*operands, *extra)
    return res[0], res[1], list(res[2:2 + n]), list(res[2 + n:2 + 2 * n]), res[-1]


def _wait_copies(started, after, name, count=N_DEV - 1):
    send_sems, recv_sems, thru, zones, _ = started
    nt, nz = len(thru), len(zones)

    def body(*refs):
        lands = refs[nt:nt + nz]
        send_ref, recv_ref = refs[nt + nz], refs[nt + nz + 1]
        x, y, c, _ = _place()
        for a in range(nz):
            blocks = lands[a].at[pl.ds(0, count)]
            cp = pltpu.make_async_remote_copy(src_ref=blocks, dst_ref=blocks, send_sem=send_ref.at[a], recv_sem=recv_ref.at[a],
                                              device_id=(x, y, 1 - c), device_id_type=MESH)
            cp.wait_send()
            cp.wait_recv()

    hbm = lambda a: pltpu.HBM(a.shape, a.dtype)
    res = pl.pallas_call(
        body, name=name,
        out_shape=tuple(hbm(a) for a in thru + zones),
        in_specs=[_HBM] * (nt + nz) + [_SEM, _SEM, _ANY],
        out_specs=tuple([_HBM] * (nt + nz)),
        input_output_aliases={i: i for i in range(nt + nz)},
        compiler_params=pltpu.CompilerParams(has_side_effects=_EFFECT),
    )(*thru, *zones, send_sems, recv_sems, after)
    return list(res[:nt]), list(res[nt:])


def _split_start(body, arrays, zones, name, after):
    n = len(arrays) + len(zones)
    hbm = lambda a: pltpu.HBM(a.shape, a.dtype)
    sems = pltpu.SemaphoreType.DMA((max(len(zones), 1),))
    extra = [] if after is None else [after]
    operands = [pltpu.with_memory_space_constraint(a, pltpu.HBM) for a in list(arrays) + list(zones)]
    res = pl.pallas_call(
        body, name=name,
        out_shape=(sems, sems, *[hbm(a) for a in operands], SDS((8, 128), F32)),
        in_specs=[_HBM] * n + [_ANY] * len(extra),
        out_specs=(_SEM, _SEM, *[_HBM] * n, pl.BlockSpec(memory_space=pltpu.VMEM)),
        input_output_aliases={i: 2 + i for i in range(n)},
        compiler_params=pltpu.CompilerParams(has_side_effects=_EFFECT),
    )(*operands, *extra)
    return res[0], res[1], list(res[2:2 + len(arrays)]), list(res[2 + len(arrays):2 + n]), res[-1]


def _gather_first_leg(shard, name, after=None):
    zone = lax.empty((N_DEV,) + shard.shape, shard.dtype)
    extra = 0 if after is None else 1

    def body(*refs):
        src, land = refs[0], refs[1]
        send_sem, recv_sem, token = refs[2 + extra], refs[3 + extra], refs[-1]
        x, y, c, me = _place()
        for peer in ((x, y, 1 - c), (1 - x, y, c), (x, 1 - y, c), (1 - x, 1 - y, c)):
            pltpu.make_async_remote_copy(src_ref=src, dst_ref=land.at[me], send_sem=send_sem.at[0], recv_sem=recv_sem.at[0],
                                         device_id=peer, device_id_type=MESH).start()
        token[...] = jnp.zeros_like(token)

    return _split_start(body, [shard], [zone], name, after)


def _gather_second_leg(zone, name, after=None):
    extra = 0 if after is None else 1

    def body(*refs):
        land = refs[0]
        send_sem, recv_sem, token = refs[1 + extra], refs[2 + extra], refs[-1]
        x, y, c, _ = _place()
        for px, py in ((1 - x, y), (x, 1 - y), (1 - x, 1 - y)):
            slot = 4 * px + 2 * py + c
            pltpu.make_async_remote_copy(src_ref=land.at[slot], dst_ref=land.at[slot], send_sem=send_sem.at[0],
                                         recv_sem=recv_sem.at[0], device_id=(x, y, 1 - c), device_id_type=MESH).start()
        token[...] = jnp.zeros_like(token)

    return _split_start(body, [], [zone], name, after)


UPDATE_BLOCK_ELEMS = 256 * 1024


def _update_rows(R, C):
    fits = [t for t in range(8, R + 1, 8) if R % t == 0 and t * C <= UPDATE_BLOCK_ELEMS]
    whole = [t for t in fits if t % 16 == 0]
    return max(whole or fits)


def _adamw_math(g, w, m, v):
    m2 = ADAM_B1 * m + (1.0 - ADAM_B1) * g
    v2 = ADAM_B2 * v + (1.0 - ADAM_B2) * (g * g)
    m_hat = m2 / (1.0 - ADAM_B1 ** ADAM_STEP)
    v_hat = v2 / (1.0 - ADAM_B2 ** ADAM_STEP)
    delta = -ADAM_LR * (m_hat / (jnp.sqrt(v_hat) + ADAM_EPS) + ADAM_WD * w)
    return delta, m2, v2


def _sum_adamw(parts, w, m, v, name):
    R, C = w.shape
    tr = _update_rows(R, C)

    def body(p_ref, w_ref, m_ref, v_ref, g_ref, d_ref, m2_ref, v2_ref):
        g = p_ref[0]
        for k in range(1, N_DEV):
            g = g + p_ref[k]
        g_ref[...] = g
        d_ref[...], m2_ref[...], v2_ref[...] = _adamw_math(g, w_ref[...], m_ref[...], v_ref[...])

    blk = pl.BlockSpec((tr, C), lambda i: (i, 0))
    return pl.pallas_call(
        body, name=name, grid=(R // tr,),
        in_specs=[pl.BlockSpec((N_DEV, tr, C), lambda i: (0, i, 0)), blk, blk, blk],
        out_specs=[blk] * 4,
        out_shape=[SDS((R, C), F32)] * 4,
        compiler_params=_params(1),
    )(parts, w, m, v)


def _sum_adamw_peers(me, own, parts, w, m, v, name, replicated):
    R, C = w.shape
    tr = _update_rows(R, C)

    def body(me_ref, own_ref, p_ref, w_ref, m_ref, v_ref, g_ref, d_ref, m2_ref, v2_ref):
        if replicated:
            mine = me_ref[0]
            g = None
            for j in range(N_DEV):
                k = jnp.maximum(jnp.bitwise_xor(mine, j) - 1, 0)
                term = jnp.where(mine == j, own_ref[...], p_ref[k])
                g = term if g is None else g + term
        else:
            g = own_ref[...].astype(F32)
            for k in range(N_DEV - 1):
                g = g + p_ref[k].astype(F32)
        g_ref[...] = g
        d_ref[...], m2_ref[...], v2_ref[...] = _adamw_math(g, w_ref[...], m_ref[...], v_ref[...])

    blk = pl.BlockSpec((tr, C), lambda i, me_ref: (i, 0))
    own_spec = blk if replicated else pl.BlockSpec((None, tr, C), lambda i, me_ref: (me_ref[0], i, 0))
    return pl.pallas_call(
        body, name=name,
        grid_spec=pltpu.PrefetchScalarGridSpec(
            num_scalar_prefetch=1, grid=(R // tr,),
            in_specs=[own_spec, pl.BlockSpec((N_DEV - 1, tr, C), lambda i, me_ref: (0, i, 0)), blk, blk, blk],
            out_specs=[blk] * 4),
        out_shape=[SDS((R, C), F32)] * 4,
        compiler_params=_params(1),
    )(me, own, parts, w, m, v)


SMALL = ("ln_v_gain", "ln_v_bias", "w_spatial", "b_spatial", "sinks", "norm_mix_post", "norm_ff_pre", "norm_ff_post")
SMALL_ROWS = {"ln_v_gain": 8, "ln_v_bias": 8, "w_spatial": 1024, "b_spatial": 8, "sinks": 8,
              "norm_mix_post": 8, "norm_ff_pre": 8, "norm_ff_post": 8}
SMALL_PACK_ROWS = 1152


def _pack_small(vals):
    rows = []
    for name in SMALL:
        flat = vals[name].reshape(-1)
        pad = SMALL_ROWS[name] * 128 - flat.shape[0]
        if pad:
            flat = jnp.concatenate([flat, jnp.zeros((pad,), F32)])
        rows.append(flat.reshape(SMALL_ROWS[name], 128))
    rows.append(jnp.zeros((SMALL_PACK_ROWS - sum(SMALL_ROWS.values()), 128), F32))
    return jnp.concatenate(rows, axis=0)


def _unpack_small(packed, shapes):
    out, r = {}, 0
    for name in SMALL:
        n = 1
        for s in shapes[name]:
            n *= s
        out[name] = packed[r:r + SMALL_ROWS[name]].reshape(-1)[:n].reshape(shapes[name])
        r += SMALL_ROWS[name]
    return out


def _rope_rows():
    d = jnp.arange(128) % HEAD
    inv = ROPE_THETA ** (-(2.0 * (d % (ROPE // 2))).astype(F32) / ROPE)
    invf = jnp.where(d < ROPE, inv, 0.0).astype(F32).reshape(1, 128)
    sgn = jnp.where(d < ROPE // 2, -1.0, jnp.where(d < ROPE, 1.0, 0.0)).astype(F32).reshape(1, 128)
    return invf, sgn


def kernel(x, positions, w_in, ln_v_gain, ln_v_bias, w_spatial, b_spatial, sinks, w_a, w_b, w_o, norm_mix_pre, norm_mix_post, w_ff_in, w_ff_out, norm_ff_pre, norm_ff_post, loss_target, m_w_in, m_ln_v_gain, m_ln_v_bias, m_w_spatial, m_b_spatial, m_sinks, m_w_a, m_w_b, m_w_o, m_norm_mix_pre, m_norm_mix_post, m_w_ff_in, m_w_ff_out, m_norm_ff_pre, m_norm_ff_post, v_w_in, v_ln_v_gain, v_ln_v_bias, v_w_spatial, v_b_spatial, v_sinks, v_w_a, v_w_b, v_w_o, v_norm_mix_pre, v_norm_mix_post, v_w_ff_in, v_w_ff_out, v_norm_ff_pre, v_norm_ff_post):
    given = dict(locals())
    T = x.shape[1]
    xt = x[0]
    tgt = loss_target[0]
    bst = b_spatial[0].T
    ws = w_spatial[0]

    me = 4 * lax.axis_index("x") + 2 * lax.axis_index("y") + lax.axis_index("c")
    me_arr = me.astype(jnp.int32).reshape(1)

    def with_own(zone, shard):
        return lax.dynamic_update_slice(zone, shard[None], (me,) + (0,) * shard.ndim)

    rest = ("w_a", "w_b", "w_o", "w_ff_in", "w_ff_out")
    shard = {n: given[n][0].astype(BF16) for n in rest}
    g_one = _gather_first_leg(w_in[0].T.astype(BF16), "gather_in_start")
    cos, sin = _rope_tables(positions.astype(F32).reshape(T, 1), *_rope_rows(), after=g_one[-1])
    h = _rms_pre(xt, norm_mix_pre, after=cos)
    (own_win,), (win8,) = _wait_copies(g_one, h, "gather_in_wait", count=4)
    g_two = _gather_second_leg(win8, "gather_in_pass_start")
    g_rest = _start_copies([shard[n] for n in rest], [GATHER] * len(rest), "gather_rest_start", after=g_two[-1])
    _, (win8,) = _wait_copies(g_two, g_rest[-1], "gather_in_pass_wait", count=3)
    win = with_own(win8, own_win).reshape(IN_W, D)

    proj = _fwd_in(h, win)
    att, qr, kr, probs, psink = _fwd_attn(proj, cos, sin, sinks[0])
    a = _fwd_sgu(proj, ln_v_gain, ln_v_bias, ws, bst, after=att)
    gw = {n: with_own(z, own) for n, own, z in zip(rest, *_wait_copies(g_rest, a, "gather_rest_wait"))}
    wa, wb, wo = (gw[n].reshape(D, D) for n in ("w_a", "w_b", "w_o"))
    wfi3 = gw["w_ff_in"]
    wfo = gw["w_ff_out"].reshape(D_FF, D)
    merged, a2, b2, mix, x1, hf = _fwd_mix(a, att, proj, xt, wa, wb, wo, norm_mix_post, norm_ff_pre)
    f, dy, dff, dg3, loss_part = _fwd_ff(hf, wfi3, wfo, x1, tgt, norm_ff_post)

    df, dx1, dmix, dg2, dg1 = _bwd_ff(dff, f, wfi3, wfo, x1, dy, mix, norm_mix_post, norm_ff_pre)
    dwfi3, dwfo = _wgrad_ff(hf, df, f, dff)
    own_ff = [dwfi3, dwfo.reshape(N_DEV, D_FF // N_DEV, D)]
    x_ff = _start_copies(own_ff, [SCATTER] * 2, "exchange_ff_start")
    da2, db2, dgate, da, datt = _bwd_mix(dmix, proj, a2, b2, wo, wa, wb, after=x_ff[-1])
    dwo, dwa, dwb = _wgrad_mix(merged, dmix, a, da2, att, db2)
    own_mix = [g.reshape(N_DEV, D // N_DEV, D) for g in (dwa, dwb, dwo)]
    x_mix = _start_copies(own_mix, [SCATTER] * 3, "exchange_mix_start")
    dq, dkv, dsink = _bwd_attn(qr, kr, probs, psink, proj, cos, sin, datt, after=x_mix[-1])
    duv, dws, dbs, dlng, dlnb = _bwd_sgu(proj, da, ln_v_gain, ln_v_bias, ws, bst)
    small_grads = {"ln_v_gain": dlng, "ln_v_bias": dlnb, "w_spatial": dws, "b_spatial": dbs, "sinks": dsink[:, :N_Q],
                   "norm_mix_post": dg1, "norm_ff_pre": dg2, "norm_ff_post": dg3}
    x_small = _start_copies([_pack_small(small_grads)], [SPREAD], "exchange_small_start")
    dwin = _wgrad_in(h, duv, dq, dkv, dgate)
    own_in = [dwin.reshape(N_DEV, IN_W // N_DEV, D)]
    x_in = _start_copies(own_in, [SCATTER], "exchange_in_start", after=x_small[-1])
    grad_x, dg0 = _bwd_in(duv, dq, dkv, dgate, win, xt, dx1, norm_mix_pre, after=x_in[-1])

    results = {}

    def update(n, own, parts, transposed=False):
        state = [given[k + n][0].T if transposed else given[k + n][0] for k in ("", "m_", "v_")]
        res = _sum_adamw_peers(me_arr, own, parts, *state, "adamw_" + n, False)
        results[n] = [(r.T if transposed else r).reshape(given[n].shape) for r in res]

    own_ff, p_ff = _wait_copies(x_ff, grad_x, "exchange_ff_wait")
    update("w_ff_in", own_ff[0], p_ff[0])
    update("w_ff_out", own_ff[1], p_ff[1])
    own_mix, p_mix = _wait_copies(x_mix, results["w_ff_out"][0], "exchange_mix_wait")
    for n, own, parts in zip(("w_a", "w_b", "w_o"), own_mix, p_mix):
        update(n, own, parts)
    tail = jnp.concatenate([dg0.reshape(8, 128), jnp.tile(loss_part, (8, 1))], axis=0)
    (tail_all,) = _all_to_all([tail], [True], "exchange_tail", after=results["w_o"][0])
    dg0_all = tail_all[:, :8]
    own_small, p_small = _wait_copies(x_small, tail_all, "exchange_small_wait")
    own_in, p_in = _wait_copies(x_in, p_small[0], "exchange_in_wait")
    update("w_in", own_in[0], p_in[0], transposed=True)
    packed = _sum_adamw_peers(me_arr, own_small[0], p_small[0], _pack_small({n: given[n] for n in SMALL}),
                              _pack_small({n: given["m_" + n] for n in SMALL}),
                              _pack_small({n: given["v_" + n] for n in SMALL}), "adamw_small", True)
    shapes = {n: given[n].shape for n in SMALL}
    unpacked = [_unpack_small(p, shapes) for p in packed]
    for n in SMALL:
        results[n] = [u[n] for u in unpacked]
    n = "norm_mix_pre"
    results[n] = [r.reshape(given[n].shape) for r in _sum_adamw(
        dg0_all, given[n].reshape(8, 128), given["m_" + n].reshape(8, 128), given["v_" + n].reshape(8, 128), "adamw_" + n)]

    loss = jnp.sum(tail_all[:, 8, 0])
    order = ("w_in", "ln_v_gain", "ln_v_bias", "w_spatial", "b_spatial", "sinks", "w_a", "w_b", "w_o", "norm_mix_pre",
             "norm_mix_post", "w_ff_in", "w_ff_out", "norm_ff_pre", "norm_ff_post")
    out = [loss, grad_x.reshape(x.shape)]
    for k in range(4):
        out += [results[n][k] for n in order]
    return tuple(out)
```

```python
import jax
import jax.numpy as jnp
from jax import lax
from jax.experimental import pallas as pl
from jax.experimental.pallas import tpu as pltpu

F32 = jnp.float32
BF16 = jnp.bfloat16

N_DEV = 8
D = 1024
D_FF = 4096
IN_W = 5632
CHUNK = 128
GROUPS = 8
HEAD = 64
N_Q = 16
N_KV = 4
ROPE = 16
ROPE_THETA = 500000.0
EPS = 1e-6
OFF_Q, OFF_K, OFF_VA, OFF_GA, OFF_GB = 2048, 3072, 3328, 3584, 4608

ADAM_LR = 0.001
ADAM_B1 = 0.9
ADAM_B2 = 0.999
ADAM_EPS = 1e-08
ADAM_WD = 0.01
ADAM_STEP = 10

VMEM_LIMIT = 62 * 1024 * 1024

SDS = jax.ShapeDtypeStruct
MESH = pl.DeviceIdType.MESH


def _params(n_axes):
    return pltpu.CompilerParams(dimension_semantics=("arbitrary",) * n_axes, vmem_limit_bytes=VMEM_LIMIT)


def _nt(a, b):
    return lax.dot_general(a, b, (((1,), (1,)), ((), ())), preferred_element_type=F32)


def _tn(a, b):
    return lax.dot_general(a, b, (((0,), (0,)), ((), ())), preferred_element_type=F32)


def _nn(a, b):
    return jnp.dot(a, b, preferred_element_type=F32)


def _gelu(x):
    t = jnp.tanh(0.7978845608028654 * (x + 0.044715 * (x * x * x)))
    return 0.5 * x * (1.0 + t), t


def _gelu_grad(x, t):
    return 0.5 * (1.0 + t) + 0.5 * x * (1.0 - t * t) * (0.7978845608028654 * (1.0 + 3.0 * 0.044715 * x * x))


def _sigmoid(x):
    return 1.0 / (1.0 + jnp.exp(-x))


def _rms_stats(v):
    r = lax.rsqrt(jnp.mean(v * v, axis=-1, keepdims=True) + EPS)
    return r, v * r


def _rms_bwd(d, vhat, r, g):
    gd = g * d
    return r * (gd - vhat * jnp.mean(gd * vhat, axis=-1, keepdims=True))


def _colsum(v):
    return jnp.sum(v, axis=0, keepdims=True)


_ANY = pl.BlockSpec(memory_space=pl.ANY)


def _after(body, n_in, after):
    if after is None:
        return body, [], []

    def ordered(*refs):
        return body(*refs[:n_in], *refs[n_in + 1:])

    return ordered, [_ANY], [after]


def _rms_pre(x, g0, after=None):
    T = x.shape[0]
    tm = min(T, 1024)

    def body(x_ref, g_ref, h_ref):
        _, xh = _rms_stats(x_ref[...])
        h_ref[...] = (xh * g_ref[...]).astype(BF16)

    body, dep_specs, deps = _after(body, 2, after)
    return pl.pallas_call(
        body, name="rms_pre", grid=(T // tm,),
        in_specs=[pl.BlockSpec((tm, D), lambda i: (i, 0)), pl.BlockSpec((1, D), lambda i: (0, 0))] + dep_specs,
        out_specs=pl.BlockSpec((tm, D), lambda i: (i, 0)),
        out_shape=SDS((T, D), BF16),
        compiler_params=_params(1),
    )(x, g0, *deps)


def _fwd_in(h, win_t):
    T = h.shape[0]
    tm, tn = min(T, 512), 1408

    def body(h_ref, w_ref, p_ref):
        for j in range(IN_W // tn):
            cols = slice(j * tn, (j + 1) * tn)
            p_ref[:, cols] = _nt(h_ref[...], w_ref[cols, :]).astype(BF16)

    return pl.pallas_call(
        body, name="fwd_in", grid=(T // tm,),
        in_specs=[pl.BlockSpec((tm, D), lambda i: (i, 0)), _resident((IN_W, D))],
        out_specs=pl.BlockSpec((tm, IN_W), lambda i: (i, 0)),
        out_shape=SDS((T, IN_W), BF16),
        compiler_params=_params(1),
    )(h, win_t)


def _sgu_forward_parts(u_ref, vs_ref, lng_ref, lnb_ref):
    u = u_ref[...].astype(F32)
    vs = vs_ref[...].astype(F32)
    gu, tu = _gelu(u)
    gv, tv = _gelu(vs)
    mu = jnp.mean(gv, axis=-1, keepdims=True)
    dv = gv - mu
    rstd = lax.rsqrt(jnp.mean(dv * dv, axis=-1, keepdims=True) + EPS)
    vhat = dv * rstd
    vn = (vhat * lng_ref[...] + lnb_ref[...]).astype(BF16)
    return u, vs, gu, tu, tv, rstd, vhat, vn


def _masked_ws(ws_ref, g):
    row = lax.broadcasted_iota(jnp.int32, (CHUNK, CHUNK), 0)
    col = lax.broadcasted_iota(jnp.int32, (CHUNK, CHUNK), 1)
    return jnp.where(row >= col, ws_ref[g], 0.0).astype(BF16)


def _fwd_sgu(proj, lng, lnb, ws, bst, after=None):
    T = proj.shape[0]
    tc = min(T, 512)

    def body(u_ref, vs_ref, lng_ref, lnb_ref, ws_ref, bst_ref, a_ref):
        _, _, gu, _, _, _, _, vn = _sgu_forward_parts(u_ref, vs_ref, lng_ref, lnb_ref)
        for g in range(GROUPS):
            wm = _masked_ws(ws_ref, g)
            cols = slice(g * CHUNK, (g + 1) * CHUNK)
            for c in range(tc // CHUNK):
                rows = slice(c * CHUNK, (c + 1) * CHUNK)
                mixed = _nn(wm, vn[rows, cols]) + bst_ref[:, g:g + 1]
                a_ref[rows, cols] = (gu[rows, cols] * mixed).astype(BF16)

    body, dep_specs, deps = _after(body, 6, after)
    return pl.pallas_call(
        body, name="fwd_sgu", grid=(T // tc,),
        in_specs=[pl.BlockSpec((tc, D), lambda i: (i, 0)), pl.BlockSpec((tc, D), lambda i: (i, 1)),
                  pl.BlockSpec((1, D), lambda i: (0, 0)), pl.BlockSpec((1, D), lambda i: (0, 0)),
                  pl.BlockSpec((GROUPS, CHUNK, CHUNK), lambda i: (0, 0, 0)),
                  pl.BlockSpec((CHUNK, GROUPS), lambda i: (0, 0))] + dep_specs,
        out_specs=pl.BlockSpec((tc, D), lambda i: (i, 0)),
        out_shape=SDS((T, D), BF16),
        compiler_params=_params(1),
    )(proj, proj, lng, lnb, ws, bst, *deps)


def _rope_tables(posf, invf, sgn, after=None):
    T = posf.shape[0]
    tr = min(T, 1024)

    def body(pos_ref, invf_ref, sgn_ref, c_ref, s_ref):
        ang = pos_ref[...] * invf_ref[...]
        c_ref[...] = jnp.cos(ang)
        s = jnp.sin(ang)
        s_ref[:, :128] = jnp.where(sgn_ref[...] < 0.0, -s, 0.0)
        s_ref[:, 128:] = jnp.where(sgn_ref[...] > 0.0, s, 0.0)

    body, dep_specs, deps = _after(body, 3, after)
    return pl.pallas_call(
        body, name="rope_tables", grid=(T // tr,),
        in_specs=[pl.BlockSpec((tr, 1), lambda i: (i, 0)), pl.BlockSpec((1, 128), lambda i: (0, 0)),
                  pl.BlockSpec((1, 128), lambda i: (0, 0))] + dep_specs,
        out_specs=[pl.BlockSpec((tr, 128), lambda i: (i, 0)), pl.BlockSpec((tr, 256), lambda i: (i, 0))],
        out_shape=[SDS((T, 128), F32), SDS((T, 256), F32)],
        compiler_params=_params(1),
    )(posf, invf, sgn, *deps)


def _rope(v, c, s):
    v = v.astype(F32)
    return v * c + pltpu.roll(v, 128 - ROPE // 2, 1) * s[:, :128] + pltpu.roll(v, ROPE // 2, 1) * s[:, 128:]


def _rope_bwd(dv, c, s):
    return dv * c + pltpu.roll(dv * s[:, :128], ROPE // 2, 1) + pltpu.roll(dv * s[:, 128:], 128 - ROPE // 2, 1)


def _fold_masks(first):
    jj = lax.broadcasted_iota(jnp.int32, (CHUNK, CHUNK), 0)
    t = lax.broadcasted_iota(jnp.int32, (CHUNK, CHUNK), 1)
    prev = jj > t
    return prev, jnp.where(prev & first, -1e30, 0.0)


def _fold(band, prev):
    return jnp.where(prev, band[:CHUNK], band[CHUNK:])


def _unfold(folded, prev):
    return jnp.concatenate([jnp.where(prev, folded, 0.0), jnp.where(prev, 0.0, folded)], axis=0)


def _softmax_sink(s, sink, key_axis):
    m = jnp.maximum(jnp.max(s, axis=key_axis, keepdims=True), sink)
    p = jnp.exp(s - m)
    esink = jnp.exp(sink - m)
    inv = 1.0 / (jnp.sum(p, axis=key_axis, keepdims=True) + esink)
    return p * inv, esink * inv


def _head_pair_operand(slab, g):
    lo = lax.broadcasted_iota(jnp.int32, slab.shape, 1) < HEAD
    if g % 2 == 0:
        first = jnp.where(lo, slab, 0.0)
        second = pltpu.roll(first, HEAD, 1)
    else:
        second = jnp.where(lo, 0.0, slab)
        first = pltpu.roll(second, HEAD, 1)
    return jnp.concatenate([first, second], axis=0).astype(BF16)


def _head_pair_gradient(acc, g):
    top, bot = acc[:2 * CHUNK], acc[2 * CHUNK:]
    lo = lax.broadcasted_iota(jnp.int32, top.shape, 1) < HEAD
    if g % 2 == 0:
        return jnp.where(lo, top, 0.0) + pltpu.roll(jnp.where(lo, 0.0, bot), HEAD, 1)
    return pltpu.roll(jnp.where(lo, top, 0.0), HEAD, 1) + jnp.where(lo, 0.0, bot)


PAIRS_PER_KV = N_Q // N_KV // 2
KV_W = N_KV * HEAD


def _band(prev_ref, cur_ref, cols=slice(None)):
    return jnp.concatenate([prev_ref[:, cols], cur_ref[:, cols]], axis=0)


def _fwd_attn(proj, cos, sin, sinks):
    T = proj.shape[0]
    nb = T // CHUNK
    cur = lambda i: i
    prev = lambda i: jnp.maximum(i - 1, 0)

    def body(q_ref, kp_ref, kc_ref, vp_ref, vc_ref, cp_ref, cc_ref, sp_ref, sc_ref, sink_ref,
             o_ref, qr_ref, kr_ref, p_ref, psink_ref):
        prev_slot, bias = _fold_masks(pl.program_id(0) == 0)
        c_band, s_band = _band(cp_ref, cc_ref), _band(sp_ref, sc_ref)
        for j in range(KV_W // 128):
            cols = slice(j * 128, (j + 1) * 128)
            k_slab = _rope(_band(kp_ref, kc_ref, cols), c_band, s_band)
            kr_ref[:, cols] = k_slab[CHUNK:].astype(BF16)
            v_slab = _band(vp_ref, vc_ref, cols).astype(F32)
            for g in (2 * j, 2 * j + 1):
                k2 = _head_pair_operand(k_slab, g)
                v2 = _head_pair_operand(v_slab, g)
                pairs = [g * PAIRS_PER_KV + r for r in range(PAIRS_PER_KV)]
                qps = []
                for pair in pairs:
                    lanes = slice(pair * 128, (pair + 1) * 128)
                    qps.append((_rope(q_ref[:, lanes], cc_ref[...], sc_ref[...]) * (HEAD ** -0.5)).astype(BF16))
                    qr_ref[:, lanes] = qps[-1]
                s2 = _nt(k2, jnp.concatenate(qps, axis=0))
                pcols = []
                for r, pair in enumerate(pairs):
                    ps = []
                    for e in range(2):
                        head = 2 * pair + e
                        s = _fold(s2[e * 2 * CHUNK:(e + 1) * 2 * CHUNK, r * 128:(r + 1) * 128], prev_slot) + bias
                        p, psink = _softmax_sink(s, sink_ref[head], 0)
                        p = p.astype(BF16)
                        p_ref[head] = p
                        psink_ref[head:head + 1, :] = psink
                        ps.append(_unfold(p, prev_slot))
                    pcols.append(jnp.concatenate(ps, axis=0))
                o = _tn(jnp.concatenate(pcols, axis=1), v2).astype(BF16)
                for r, pair in enumerate(pairs):
                    o_ref[:, pair * 128:(pair + 1) * 128] = o[r * CHUNK:(r + 1) * CHUNK]

    table = lambda which, width: pl.BlockSpec((CHUNK, width), lambda i: (which(i), 0))
    return pl.pallas_call(
        body, name="fwd_attn", grid=(nb,),
        in_specs=[pl.BlockSpec((CHUNK, D), lambda i: (i, OFF_Q // D)),
                  pl.BlockSpec((CHUNK, KV_W), lambda i: (prev(i), OFF_K // KV_W)),
                  pl.BlockSpec((CHUNK, KV_W), lambda i: (i, OFF_K // KV_W)),
                  pl.BlockSpec((CHUNK, KV_W), lambda i: (prev(i), OFF_VA // KV_W)),
                  pl.BlockSpec((CHUNK, KV_W), lambda i: (i, OFF_VA // KV_W)),
                  table(prev, 128), table(cur, 128), table(prev, 256), table(cur, 256),
                  pl.BlockSpec(memory_space=pltpu.SMEM)],
        out_specs=[pl.BlockSpec((CHUNK, D), lambda i: (i, 0)), pl.BlockSpec((CHUNK, D), lambda i: (i, 0)),
                   pl.BlockSpec((CHUNK, KV_W), lambda i: (i, 0)),
                   pl.BlockSpec((None, N_Q, CHUNK, CHUNK), lambda i: (i, 0, 0, 0)),
                   pl.BlockSpec((None, N_Q, CHUNK), lambda i: (i, 0, 0))],
        out_shape=[SDS((T, D), BF16), SDS((T, D), BF16), SDS((T, KV_W), BF16),
                   SDS((nb, N_Q, CHUNK, CHUNK), BF16), SDS((nb, N_Q, CHUNK), F32)],
        compiler_params=_params(1),
    )(proj, proj, proj, proj, proj, cos, cos, sin, sin, sinks)


def _row_halves(tm):
    return [slice(0, tm // 2), slice(tm // 2, tm)] if tm % 32 == 0 else [slice(0, tm)]


def _fwd_mix(a, att, proj, x, wa, wb, wo, g1, g2):
    T = x.shape[0]
    tm = min(T, 512)
    half = D // 2

    def body(a_ref, att_ref, ga0, ga1, gb0, gb1, x_ref, wa_ref, wb_ref, wo_ref, g1_ref, g2_ref,
             mg_ref, a2_ref, b2_ref, mix_ref, x1_ref, hf_ref):
        for rows in _row_halves(tm):
            a2 = _nn(a_ref[rows, :], wa_ref[...])
            b2 = _nn(att_ref[rows, :], wb_ref[...])
            ga = jnp.concatenate([ga0[rows, :], ga1[rows, :]], axis=1).astype(F32)
            gb = jnp.concatenate([gb0[rows, :], gb1[rows, :]], axis=1).astype(F32)
            merged = (_sigmoid(ga) * a2 + _sigmoid(gb) * b2).astype(BF16)
            a2_ref[rows, :] = a2.astype(BF16)
            b2_ref[rows, :] = b2.astype(BF16)
            mg_ref[rows, :] = merged
            mix = _nn(merged, wo_ref[...])
            mix_ref[rows, :] = mix
            _, mh = _rms_stats(mix)
            x1 = x_ref[rows, :] + mh * g1_ref[...]
            x1_ref[rows, :] = x1
            _, xh = _rms_stats(x1)
            hf_ref[rows, :] = (xh * g2_ref[...]).astype(BF16)

    row = lambda i: (i, 0)
    const = lambda i: (0, 0)
    gspec = lambda off: pl.BlockSpec((tm, half), lambda i: (i, off // half))
    return pl.pallas_call(
        body, name="fwd_mix", grid=(T // tm,),
        in_specs=[pl.BlockSpec((tm, D), row), pl.BlockSpec((tm, D), row),
                  gspec(OFF_GA), gspec(OFF_GA + half), gspec(OFF_GB), gspec(OFF_GB + half),
                  pl.BlockSpec((tm, D), row), _resident((D, D)), _resident((D, D)),
                  _resident((D, D)), pl.BlockSpec((1, D), const), pl.BlockSpec((1, D), const)],
        out_specs=[pl.BlockSpec((tm, D), row)] * 6,
        out_shape=[SDS((T, D), BF16), SDS((T, D), BF16), SDS((T, D), BF16), SDS((T, D), F32), SDS((T, D), F32),
                   SDS((T, D), BF16)],
        compiler_params=_params(1),
    )(a, att, proj, proj, proj, proj, x, wa, wb, wo, g1, g2)


FF_SPLIT = N_DEV
FF_TILE = D_FF // FF_SPLIT


def _fwd_ff(hf, wfi3, wfo, x1, tgt, g3):
    T = hf.shape[0]
    tm = min(T, 512)

    def body(hf_ref, wfi_ref, wfo_ref, x1_ref, tgt_ref, g3_ref, f_ref, dy_ref, dff_ref, dg3_ref, loss_ref, r_s):
        @pl.when(pl.program_id(0) == 0)
        def _():
            dg3_ref[...] = jnp.zeros_like(dg3_ref)
            loss_ref[...] = jnp.zeros_like(loss_ref)

        hf_t = hf_ref[...]
        for s in range(FF_SPLIT):
            cols = slice(s * FF_TILE, (s + 1) * FF_TILE)
            f = _nn(hf_t, wfi_ref[s]).astype(BF16)
            f_ref[:, cols] = f
            rl = jnp.maximum(f.astype(F32), 0.0)
            r_s[:, cols] = (rl * rl).astype(BF16)
        r3, fh = _rms_stats(_nn(r_s[...], wfo_ref[...]))
        e = x1_ref[...] + fh * g3_ref[...] - tgt_ref[...]
        loss_ref[...] += jnp.sum(e * e) * (0.5 / D)
        dy = e * (1.0 / D)
        dy_ref[...] = dy
        dg3_ref[...] += _colsum(dy * fh)
        dff_ref[...] = _rms_bwd(dy, fh, r3, g3_ref[...]).astype(BF16)

    row = lambda i: (i, 0)
    const = lambda i: (0, 0)
    return pl.pallas_call(
        body, name="fwd_ff", grid=(T // tm,),
        in_specs=[pl.BlockSpec((tm, D), row), _resident((FF_SPLIT, D, FF_TILE)), _resident((D_FF, D)),
                  pl.BlockSpec((tm, D), row),
                  pl.BlockSpec((tm, D), row), pl.BlockSpec((1, D), const)],
        out_specs=[pl.BlockSpec((tm, D_FF), row), pl.BlockSpec((tm, D), row),
                   pl.BlockSpec((tm, D), row), pl.BlockSpec((1, D), const), pl.BlockSpec((1, 128), const)],
        out_shape=[SDS((T, D_FF), BF16), SDS((T, D), F32), SDS((T, D), BF16), SDS((1, D), F32), SDS((1, 128), F32)],
        scratch_shapes=[pltpu.VMEM((tm, D_FF), BF16)],
        compiler_params=_params(1),
    )(hf, wfi3, wfo, x1, tgt, g3)


def _bwd_ff(dff, f, wfi3, wfo, x1, dy, mix, g1, g2):
    T = dff.shape[0]
    tm = min(T, 512)

    def body(dff_ref, f_ref, wfi_ref, wfo_ref, x1_ref, dy_ref, mix_ref, g1_ref, g2_ref,
             df_ref, dx1_ref, dmix_ref, dg2_ref, dg1_ref):
        @pl.when(pl.program_id(0) == 0)
        def _():
            dg2_ref[...] = jnp.zeros_like(dg2_ref)
            dg1_ref[...] = jnp.zeros_like(dg1_ref)

        dff_t = dff_ref[...]
        dhf = None
        for s in range(FF_SPLIT):
            cols = slice(s * FF_TILE, (s + 1) * FF_TILE)
            dr = _nt(dff_t, wfo_ref[cols, :])
            df = (dr * (2.0 * jnp.maximum(f_ref[:, cols].astype(F32), 0.0))).astype(BF16)
            df_ref[:, cols] = df
            part = _nt(df, wfi_ref[s])
            dhf = part if dhf is None else dhf + part
        r2, xh = _rms_stats(x1_ref[...])
        dg2_ref[...] += _colsum(dhf * xh)
        dx1 = dy_ref[...] + _rms_bwd(dhf, xh, r2, g2_ref[...])
        dx1_ref[...] = dx1
        r1, mh = _rms_stats(mix_ref[...])
        dg1_ref[...] += _colsum(dx1 * mh)
        dmix_ref[...] = _rms_bwd(dx1, mh, r1, g1_ref[...]).astype(BF16)

    row = lambda i: (i, 0)
    const = lambda i: (0, 0)
    return pl.pallas_call(
        body, name="bwd_ff", grid=(T // tm,),
        in_specs=[pl.BlockSpec((tm, D), row), pl.BlockSpec((tm, D_FF), row),
                  _resident((FF_SPLIT, D, FF_TILE)), _resident((D_FF, D)),
                  pl.BlockSpec((tm, D), row), pl.BlockSpec((tm, D), row), pl.BlockSpec((tm, D), row),
                  pl.BlockSpec((1, D), const), pl.BlockSpec((1, D), const)],
        out_specs=[pl.BlockSpec((tm, D_FF), row), pl.BlockSpec((tm, D), row),
                   pl.BlockSpec((tm, D), row), pl.BlockSpec((1, D), const), pl.BlockSpec((1, D), const)],
        out_shape=[SDS((T, D_FF), BF16), SDS((T, D), F32), SDS((T, D), BF16), SDS((1, D), F32), SDS((1, D), F32)],
        compiler_params=_params(1),
    )(dff, f, wfi3, wfo, x1, dy, mix, g1, g2)


def _wgrad_ff(hf, df, f, dff):
    T = hf.shape[0]
    tt = min(T, 2048)
    slabs = 2
    wide = slabs * FF_TILE

    def body(hf_ref, df_ref, f_ref, dff_ref, dwfi_ref, dwfo_ref, acc_i, acc_o):
        t = pl.program_id(1)

        @pl.when(t == 0)
        def _():
            acc_i[...] = jnp.zeros_like(acc_i)
            acc_o[...] = jnp.zeros_like(acc_o)

        acc_i[...] += _tn(hf_ref[...], df_ref[...])
        rl = jnp.maximum(f_ref[...].astype(F32), 0.0)
        acc_o[...] += _tn((rl * rl).astype(BF16), dff_ref[...])

        @pl.when(t == T // tt - 1)
        def _():
            for s in range(slabs):
                dwfi_ref[s] = acc_i[:, s * FF_TILE:(s + 1) * FF_TILE].astype(BF16)
            dwfo_ref[...] = acc_o[...].astype(BF16)

    return pl.pallas_call(
        body, name="wgrad_ff", grid=(D_FF // wide, T // tt),
        in_specs=[pl.BlockSpec((tt, D), lambda p, t: (t, 0)), pl.BlockSpec((tt, wide), lambda p, t: (t, p)),
                  pl.BlockSpec((tt, wide), lambda p, t: (t, p)), pl.BlockSpec((tt, D), lambda p, t: (t, 0))],
        out_specs=[pl.BlockSpec((slabs, D, FF_TILE), lambda p, t: (p, 0, 0)), pl.BlockSpec((wide, D), lambda p, t: (p, 0))],
        out_shape=[SDS((FF_SPLIT, D, FF_TILE), BF16), SDS((D_FF, D), BF16)],
        scratch_shapes=[pltpu.VMEM((D, wide), F32), pltpu.VMEM((wide, D), F32)],
        compiler_params=_params(2),
    )(hf, df, f, dff)


def _bwd_mix(dmix, proj, a2, b2, wo, wa, wb, after=None):
    T = dmix.shape[0]
    tm = min(T, 512)
    half = D // 2

    def body(dmix_ref, ga0, ga1, gb0, gb1, a2_ref, b2_ref, wo_ref, wa_ref, wb_ref,
             da2_ref, db2_ref, dg_ref, da_ref, datt_ref):
        for rows in _row_halves(tm):
            dmg = _nt(dmix_ref[rows, :], wo_ref[...])
            sa = _sigmoid(jnp.concatenate([ga0[rows, :], ga1[rows, :]], axis=1).astype(F32))
            sb = _sigmoid(jnp.concatenate([gb0[rows, :], gb1[rows, :]], axis=1).astype(F32))
            da2 = (dmg * sa).astype(BF16)
            db2 = (dmg * sb).astype(BF16)
            da2_ref[rows, :] = da2
            db2_ref[rows, :] = db2
            dg_ref[rows, :D] = (dmg * a2_ref[rows, :].astype(F32) * (sa * (1.0 - sa))).astype(BF16)
            dg_ref[rows, D:] = (dmg * b2_ref[rows, :].astype(F32) * (sb * (1.0 - sb))).astype(BF16)
            da_ref[rows, :] = _nt(da2, wa_ref[...]).astype(BF16)
            datt_ref[rows, :] = _nt(db2, wb_ref[...]).astype(BF16)

    row = lambda i: (i, 0)
    const = lambda i: (0, 0)
    gspec = lambda off: pl.BlockSpec((tm, half), lambda i: (i, off // half))
    body, dep_specs, deps = _after(body, 10, after)
    return pl.pallas_call(
        body, name="bwd_mix", grid=(T // tm,),
        in_specs=[pl.BlockSpec((tm, D), row), gspec(OFF_GA), gspec(OFF_GA + half), gspec(OFF_GB), gspec(OFF_GB + half),
                  pl.BlockSpec((tm, D), row), pl.BlockSpec((tm, D), row),
                  _resident((D, D)), _resident((D, D)), _resident((D, D))] + dep_specs,
        out_specs=[pl.BlockSpec((tm, D), row), pl.BlockSpec((tm, D), row), pl.BlockSpec((tm, 2 * D), row),
                   pl.BlockSpec((tm, D), row), pl.BlockSpec((tm, D), row)],
        out_shape=[SDS((T, D), BF16), SDS((T, D), BF16), SDS((T, 2 * D), BF16), SDS((T, D), BF16), SDS((T, D), BF16)],
        compiler_params=_params(1),
    )(dmix, proj, proj, proj, proj, a2, b2, wo, wa, wb, *deps)


def _wgrad_mix(merged, dmix, a, da2, att, db2):
    T = merged.shape[0]
    tt = min(T, 1024)

    def body(mg_ref, dmix_ref, a_ref, da2_ref, att_ref, db2_ref, dwo_ref, dwa_ref, dwb_ref, acc):
        t = pl.program_id(0)

        @pl.when(t == 0)
        def _():
            acc[...] = jnp.zeros_like(acc)

        acc[0] += _tn(mg_ref[...], dmix_ref[...])
        acc[1] += _tn(a_ref[...], da2_ref[...])
        acc[2] += _tn(att_ref[...], db2_ref[...])

        @pl.when(t == T // tt - 1)
        def _():
            dwo_ref[...] = acc[0].astype(BF16)
            dwa_ref[...] = acc[1].astype(BF16)
            dwb_ref[...] = acc[2].astype(BF16)

    return pl.pallas_call(
        body, name="wgrad_mix", grid=(T // tt,),
        in_specs=[pl.BlockSpec((tt, D), lambda t: (t, 0))] * 6,
        out_specs=[pl.BlockSpec((D, D), lambda t: (0, 0))] * 3,
        out_shape=[SDS((D, D), BF16)] * 3,
        scratch_shapes=[pltpu.VMEM((3, D, D), F32)],
        compiler_params=_params(1),
    )(merged, dmix, a, da2, att, db2)


def _bwd_attn(qr, kr, probs, psink, proj, cos, sin, datt, after=None):
    T = proj.shape[0]
    nb = T // CHUNK
    cur = lambda i: jnp.minimum(i, nb - 1)
    prev = lambda i: jnp.maximum(jnp.minimum(i, nb - 1) - 1, 0)

    def body(q_ref, kp_ref, kc_ref, vp_ref, vc_ref, cp_ref, cc_ref, sp_ref, sc_ref, p_ref, psink_ref, do_ref,
             dq_ref, dkv_ref, dsink_ref, carry_k, carry_v):
        i = pl.program_id(0)

        @pl.when(i == 0)
        def _():
            carry_k[...] = jnp.zeros_like(carry_k)
            carry_v[...] = jnp.zeros_like(carry_v)
            dsink_ref[...] = jnp.zeros_like(dsink_ref)

        @pl.when(i < nb)
        def _():
            prev_slot, _ = _fold_masks(i == 0)
            c_band, s_band = _band(cp_ref, cc_ref), _band(sp_ref, sc_ref)
            lane = lax.broadcasted_iota(jnp.int32, (1, 128), 1)
            dsink = jnp.zeros((1, 128), F32)
            for j in range(KV_W // 128):
                cols = slice(j * 128, (j + 1) * 128)
                k_slab = _band(kp_ref, kc_ref, cols).astype(F32)
                v_slab = _band(vp_ref, vc_ref, cols).astype(F32)
                dk_slab = jnp.zeros((2 * CHUNK, 128), F32)
                dv_slab = jnp.zeros((2 * CHUNK, 128), F32)
                for g in (2 * j, 2 * j + 1):
                    k2 = _head_pair_operand(k_slab, g)
                    v2 = _head_pair_operand(v_slab, g)
                    pairs = [g * PAIRS_PER_KV + r for r in range(PAIRS_PER_KV)]
                    q_stack = jnp.concatenate([q_ref[:, pr * 128:(pr + 1) * 128] for pr in pairs], axis=0)
                    do_stack = jnp.concatenate([do_ref[:, pr * 128:(pr + 1) * 128] for pr in pairs], axis=0)
                    dp2 = _nt(v2, do_stack)
                    pcols, dscols = [], []
                    for r, pair in enumerate(pairs):
                        ps, dss = [], []
                        for e in range(2):
                            head = 2 * pair + e
                            p_b = p_ref[head]
                            p = p_b.astype(F32)
                            dp = _fold(dp2[e * 2 * CHUNK:(e + 1) * 2 * CHUNK, r * 128:(r + 1) * 128], prev_slot)
                            delta = jnp.sum(p * dp, axis=0, keepdims=True)
                            ps.append(_unfold(p_b, prev_slot))
                            dss.append(_unfold((p * (dp - delta)).astype(BF16), prev_slot))
                            dsink = dsink + jnp.where(lane == head, -jnp.sum(psink_ref[head:head + 1, :] * delta), 0.0)
                        pcols.append(jnp.concatenate(ps, axis=0))
                        dscols.append(jnp.concatenate(dss, axis=0))
                    ds2 = jnp.concatenate(dscols, axis=1)
                    dq = _tn(ds2, k2) * (HEAD ** -0.5)
                    for r, pair in enumerate(pairs):
                        dq_ref[:, pair * 128:(pair + 1) * 128] = _rope_bwd(
                            dq[r * CHUNK:(r + 1) * CHUNK], cc_ref[...], sc_ref[...]).astype(BF16)
                    dk_slab = dk_slab + _head_pair_gradient(_nn(ds2, q_stack), g)
                    dv_slab = dv_slab + _head_pair_gradient(_nn(jnp.concatenate(pcols, axis=1), do_stack), g)
                dk_slab = _rope_bwd(dk_slab, c_band, s_band)
                vcols = slice(KV_W + j * 128, KV_W + (j + 1) * 128)
                dkv_ref[:, cols] = (carry_k[:, cols] + dk_slab[:CHUNK]).astype(BF16)
                dkv_ref[:, vcols] = (carry_v[:, cols] + dv_slab[:CHUNK]).astype(BF16)
                carry_k[:, cols] = dk_slab[CHUNK:]
                carry_v[:, cols] = dv_slab[CHUNK:]
            dsink_ref[...] += dsink

        @pl.when(i == nb)
        def _():
            dkv_ref[:, :KV_W] = carry_k[...].astype(BF16)
            dkv_ref[:, KV_W:] = carry_v[...].astype(BF16)

    table = lambda which, width: pl.BlockSpec((CHUNK, width), lambda i: (which(i), 0))
    body, dep_specs, deps = _after(body, 12, after)
    return pl.pallas_call(
        body, name="bwd_attn", grid=(nb + 1,),
        in_specs=[pl.BlockSpec((CHUNK, D), lambda i: (cur(i), 0)),
                  pl.BlockSpec((CHUNK, KV_W), lambda i: (prev(i), 0)),
                  pl.BlockSpec((CHUNK, KV_W), lambda i: (cur(i), 0)),
                  pl.BlockSpec((CHUNK, KV_W), lambda i: (prev(i), OFF_VA // KV_W)),
                  pl.BlockSpec((CHUNK, KV_W), lambda i: (cur(i), OFF_VA // KV_W)),
                  table(prev, 128), table(cur, 128), table(prev, 256), table(cur, 256),
                  pl.BlockSpec((None, N_Q, CHUNK, CHUNK), lambda i: (cur(i), 0, 0, 0)),
                  pl.BlockSpec((None, N_Q, CHUNK), lambda i: (cur(i), 0, 0)),
                  pl.BlockSpec((CHUNK, D), lambda i: (cur(i), 0))] + dep_specs,
        out_specs=[pl.BlockSpec((CHUNK, D), lambda i: (cur(i), 0)),
                   pl.BlockSpec((CHUNK, 2 * KV_W), lambda i: (jnp.maximum(i - 1, 0), 0)),
                   pl.BlockSpec((1, 128), lambda i: (0, 0))],
        out_shape=[SDS((T, D), BF16), SDS((T, 2 * KV_W), BF16), SDS((1, 128), F32)],
        scratch_shapes=[pltpu.VMEM((CHUNK, KV_W), F32), pltpu.VMEM((CHUNK, KV_W), F32)],
        compiler_params=_params(1),
    )(qr, kr, kr, proj, proj, cos, cos, sin, sin, probs, psink, datt, *deps)


def _bwd_sgu(proj, da, lng, lnb, ws, bst):
    T = proj.shape[0]
    tc = min(T, 512)
    nsteps = T // tc

    def body(u_ref, vs_ref, da_ref, lng_ref, lnb_ref, ws_ref, bst_ref,
             duv_ref, dws_ref, dbs_ref, dlng_ref, dlnb_ref, dvn_s, dgu_s, dmx_sum):
        i = pl.program_id(0)

        @pl.when(i == 0)
        def _():
            dws_ref[...] = jnp.zeros_like(dws_ref)
            dlng_ref[...] = jnp.zeros_like(dlng_ref)
            dlnb_ref[...] = jnp.zeros_like(dlnb_ref)
            dmx_sum[...] = jnp.zeros_like(dmx_sum)

        u, vs, gu, tu, tv, rstd, vhat, vn = _sgu_forward_parts(u_ref, vs_ref, lng_ref, lnb_ref)
        da = da_ref[...].astype(F32)
        for g in range(GROUPS):
            wm = _masked_ws(ws_ref, g)
            cols = slice(g * CHUNK, (g + 1) * CHUNK)
            dws = jnp.zeros((CHUNK, CHUNK), F32)
            dsum = jnp.zeros((CHUNK, CHUNK), F32)
            for c in range(tc // CHUNK):
                rows = slice(c * CHUNK, (c + 1) * CHUNK)
                vn_cg = vn[rows, cols]
                mixed = _nn(wm, vn_cg) + bst_ref[:, g:g + 1]
                dgu_s[rows, cols] = da[rows, cols] * mixed
                dmx = da[rows, cols] * gu[rows, cols]
                dmxb = dmx.astype(BF16)
                dws = dws + _nt(dmxb, vn_cg)
                dsum = dsum + dmx
                dvn_s[rows, cols] = _tn(wm, dmxb)
            dws_ref[g] += dws
            dmx_sum[:, cols] += dsum
        dvn = dvn_s[...]
        dlng_ref[...] += _colsum(dvn * vhat)
        dlnb_ref[...] += _colsum(dvn)
        dvh = dvn * lng_ref[...]
        dgv = rstd * (dvh - jnp.mean(dvh, axis=-1, keepdims=True) - vhat * jnp.mean(dvh * vhat, axis=-1, keepdims=True))
        duv_ref[:, :D] = (dgu_s[...] * _gelu_grad(u, tu)).astype(BF16)
        duv_ref[:, D:] = (dgv * _gelu_grad(vs, tv)).astype(BF16)

        @pl.when(i == nsteps - 1)
        def _():
            row = lax.broadcasted_iota(jnp.int32, (CHUNK, CHUNK), 0)
            col = lax.broadcasted_iota(jnp.int32, (CHUNK, CHUNK), 1)
            for g in range(GROUPS):
                dws_ref[g] = jnp.where(row >= col, dws_ref[g], 0.0)
                dbs_ref[g:g + 1, :] = _colsum(dmx_sum[:, g * CHUNK:(g + 1) * CHUNK].T)

    const2 = lambda i: (0, 0)
    return pl.pallas_call(
        body, name="bwd_sgu", grid=(nsteps,),
        in_specs=[pl.BlockSpec((tc, D), lambda i: (i, 0)), pl.BlockSpec((tc, D), lambda i: (i, 1)),
                  pl.BlockSpec((tc, D), lambda i: (i, 0)), pl.BlockSpec((1, D), const2), pl.BlockSpec((1, D), const2),
                  pl.BlockSpec((GROUPS, CHUNK, CHUNK), lambda i: (0, 0, 0)), pl.BlockSpec((CHUNK, GROUPS), const2)],
        out_specs=[pl.BlockSpec((tc, 2 * D), lambda i: (i, 0)), pl.BlockSpec((GROUPS, CHUNK, CHUNK), lambda i: (0, 0, 0)),
                   pl.BlockSpec((GROUPS, CHUNK), const2), pl.BlockSpec((1, D), const2), pl.BlockSpec((1, D), const2)],
        out_shape=[SDS((T, 2 * D), BF16), SDS((GROUPS, CHUNK, CHUNK), F32), SDS((GROUPS, CHUNK), F32),
                   SDS((1, D), F32), SDS((1, D), F32)],
        scratch_shapes=[pltpu.VMEM((tc, D), F32), pltpu.VMEM((tc, D), F32), pltpu.VMEM((CHUNK, D), F32)],
        compiler_params=_params(1),
    )(proj, proj, da, lng, lnb, ws, bst)


IN_SEG_WIDTHS = (2 * D, D, 2 * N_KV * HEAD, 2 * D)


def _resident(shape):
    return pl.BlockSpec(shape, lambda *_: (0,) * len(shape), pipeline_mode=pl.Buffered(1))


def _bwd_in(duv, dq, dkv, dg, win_t, x, dx1, g0, after=None):
    T = x.shape[0]
    tm = min(T, 512)

    def body(duv_ref, dq_ref, dkv_ref, dg_ref, w_ref, x_ref, dx1_ref, g0_ref, gx_ref, dg0_ref):
        @pl.when(pl.program_id(0) == 0)
        def _():
            dg0_ref[...] = jnp.zeros_like(dg0_ref)

        dh, off = None, 0
        for ref, width in zip((duv_ref, dq_ref, dkv_ref, dg_ref), IN_SEG_WIDTHS):
            part = _nn(ref[...], w_ref[off:off + width, :])
            dh = part if dh is None else dh + part
            off += width
        r0, xh = _rms_stats(x_ref[...])
        dg0_ref[...] += _colsum(dh * xh)
        gx_ref[...] = dx1_ref[...] + _rms_bwd(dh, xh, r0, g0_ref[...])

    row = lambda i: (i, 0)
    body, dep_specs, deps = _after(body, 8, after)
    return pl.pallas_call(
        body, name="bwd_in", grid=(T // tm,),
        in_specs=[pl.BlockSpec((tm, w), row) for w in IN_SEG_WIDTHS] + [
            _resident((IN_W, D)), pl.BlockSpec((tm, D), row), pl.BlockSpec((tm, D), row),
            pl.BlockSpec((1, D), lambda i: (0, 0))] + dep_specs,
        out_specs=[pl.BlockSpec((tm, D), row), pl.BlockSpec((1, D), lambda i: (0, 0))],
        out_shape=[SDS((T, D), F32), SDS((1, D), F32)],
        compiler_params=_params(1),
    )(duv, dq, dkv, dg, win_t, x, dx1, g0, *deps)


def _wgrad_rows(h, segs, first_row, into, name):
    T = h.shape[0]
    tt = min(T, 2048)
    widths = [s.shape[1] for s in segs]
    rows = sum(widths)
    n_in = 1 + len(segs) + (into is not None)

    def body(*refs):
        h_ref, seg_refs = refs[0], refs[1:1 + len(segs)]
        dw_ref, acc, stage, sem = refs[n_in], refs[n_in + 1], refs[n_in + 2], refs[n_in + 3]
        t = pl.program_id(0)

        @pl.when(t == 0)
        def _():
            acc[...] = jnp.zeros_like(acc)

        off = 0
        for ref, width in zip(seg_refs, widths):
            acc[off:off + width, :] += _tn(ref[...], h_ref[...])
            off += width

        @pl.when(t == T // tt - 1)
        def _():
            stage[...] = acc[...].astype(BF16)
            out = pltpu.make_async_copy(stage, dw_ref.at[pl.ds(first_row, rows)], sem)
            out.start()
            out.wait()

    row = lambda t: (t, 0)
    return pl.pallas_call(
        body, name=name, grid=(T // tt,),
        in_specs=[pl.BlockSpec((tt, D), row)] + [pl.BlockSpec((tt, w), row) for w in widths] + [_ANY] * (into is not None),
        out_specs=_ANY,
        out_shape=SDS((IN_W, D), BF16),
        input_output_aliases={} if into is None else {n_in - 1: 0},
        scratch_shapes=[pltpu.VMEM((rows, D), F32), pltpu.VMEM((rows, D), BF16), pltpu.SemaphoreType.DMA],
        compiler_params=_params(1),
    )(h, *segs, *([] if into is None else [into]))


def _wgrad_in(h, duv, dq, dkv, dg):
    dw = _wgrad_rows(h, [dg], IN_SEG_WIDTHS[0] + IN_SEG_WIDTHS[1] + IN_SEG_WIDTHS[2], None, "wgrad_in_gates")
    dw = _wgrad_rows(h, [duv], 0, dw, "wgrad_in_uv")
    return _wgrad_rows(h, [dq, dkv], IN_SEG_WIDTHS[0], dw, "wgrad_in_qkv")


def _place():
    x, y, c = lax.axis_index("x"), lax.axis_index("y"), lax.axis_index("c")
    return x, y, c, 4 * x + 2 * y + c


def _peers(x, y, c):
    out = []
    for mask in range(1, N_DEV):
        px = 1 - x if mask & 4 else x
        py = 1 - y if mask & 2 else y
        pc = 1 - c if mask & 1 else c
        out.append(((px, py, pc), 4 * px + 2 * py + pc))
    return out


def _all_to_all(arrays, gather, name, after=None):
    n = len(arrays)

    def body(*refs):
        ins, outs = refs[:n], refs[n:2 * n]
        send_sems, recv_sems, local_sems = refs[2 * n:]
        x, y, c, me = _place()
        local, sends, recvs = [], [], []
        for a in range(n):
            src_own = ins[a] if gather[a] else ins[a].at[me]
            local.append(pltpu.make_async_copy(src_own, outs[a].at[me], local_sems.at[a]))
            for k, (peer, pid) in enumerate(_peers(x, y, c)):
                sem = a * (N_DEV - 1) + k
                src = ins[a] if gather[a] else ins[a].at[pid]
                sends.append(pltpu.make_async_remote_copy(
                    src_ref=src, dst_ref=outs[a].at[me], send_sem=send_sems.at[sem], recv_sem=recv_sems.at[sem],
                    device_id=peer, device_id_type=MESH))
                recvs.append(pltpu.make_async_remote_copy(
                    src_ref=src, dst_ref=outs[a].at[pid], send_sem=send_sems.at[sem], recv_sem=recv_sems.at[sem],
                    device_id=peer, device_id_type=MESH))
        for cp in local + sends:
            cp.start()
        for cp in recvs:
            cp.wait_recv()
        for cp in sends:
            cp.wait_send()
        for cp in local:
            cp.wait()

    out_shape = [SDS((N_DEV,) + a.shape if gt else a.shape, a.dtype) for a, gt in zip(arrays, gather)]
    nsem = n * (N_DEV - 1)
    body, dep_specs, deps = _after(body, n, after)
    return pl.pallas_call(
        body, name=name,
        in_specs=[pl.BlockSpec(memory_space=pl.ANY)] * n + dep_specs,
        out_specs=[pl.BlockSpec(memory_space=pl.ANY)] * n,
        out_shape=out_shape,
        scratch_shapes=[pltpu.SemaphoreType.DMA((nsem,)), pltpu.SemaphoreType.DMA((nsem,)), pltpu.SemaphoreType.DMA((n,))],
    )(*arrays, *deps)


_HBM = pl.BlockSpec(memory_space=pltpu.HBM)
_SEM = pl.BlockSpec(memory_space=pltpu.SEMAPHORE)
_EFFECT = pltpu.SideEffectType.DATAFLOW_SIDE_EFFECTING
GATHER = "gather"
SCATTER = "scatter"
SPREAD = "spread"


def _zone_shape(a, mode):
    if mode == GATHER:
        return (N_DEV,) + a.shape
    return (N_DEV - 1,) + (a.shape[1:] if mode == SCATTER else a.shape)


def _start_copies(arrays, modes, name, after=None):
    n = len(arrays)
    zones = [lax.empty(_zone_shape(a, m), a.dtype) for a, m in zip(arrays, modes)]

    def body(*refs):
        ins, lands = refs[:n], refs[n:2 * n]
        send_sems, recv_sems = refs[-2 * n - 3], refs[-2 * n - 2]
        token = refs[-1]
        x, y, c, me = _place()
        for a in range(n):
            for k, (peer, pid) in enumerate(_peers(x, y, c)):
                src = ins[a].at[pid] if modes[a] == SCATTER else ins[a]
                dst = lands[a].at[me] if modes[a] == GATHER else lands[a].at[k]
                pltpu.make_async_remote_copy(src_ref=src, dst_ref=dst, send_sem=send_sems.at[a], recv_sem=recv_sems.at[a],
                                             device_id=peer, device_id_type=MESH).start()
            if modes[a] == GATHER:
                pltpu.make_async_remote_copy(src_ref=ins[a], dst_ref=lands[a].at[me], send_sem=send_sems.at[a],
                                             recv_sem=recv_sems.at[a], device_id=(x, y, c), device_id_type=MESH).start()
        token[...] = jnp.zeros_like(token)

    hbm = lambda a: pltpu.HBM(a.shape, a.dtype)
    sems = pltpu.SemaphoreType.DMA((n,))
    extra = [] if after is None else [after]
    operands = [pltpu.with_memory_space_constraint(a, pltpu.HBM) for a in list(arrays) + zones]
    res = pl.pallas_call(
        body, name=name,
        out_shape=(sems, sems, *[hbm(a) for a in arrays], *[hbm(z) for z in zones], SDS((8, 128), F32)),
        in_specs=[_HBM] * (2 * n) + [_ANY] * len(extra),
        out_specs=(_SEM, _SEM, *[_HBM] * (2 * n), pl.BlockSpec(memory_space=pltpu.VMEM)),
        input_output_aliases={i: 2 + i for i in range(2 * n)},
        compiler_params=pltpu.CompilerParams(has_side_effects=_EFFECT),
    )(*operands, *extra)
    return res[0], res[1], list(res[2:2 + n]), list(res[2 + n:2 + 2 * n]), res[-1]


def _wait_copies(started, after, name, count=N_DEV - 1):
    send_sems, recv_sems, thru, zones, _ = started
    nt, nz = len(thru), len(zones)

    def body(*refs):
        lands = refs[nt:nt + nz]
        send_ref, recv_ref = refs[nt + nz], refs[nt + nz + 1]
        x, y, c, _ = _place()
        for a in range(nz):
            blocks = lands[a].at[pl.ds(0, count)]
            cp = pltpu.make_async_remote_copy(src_ref=blocks, dst_ref=blocks, send_sem=send_ref.at[a], recv_sem=recv_ref.at[a],
                                              device_id=(x, y, 1 - c), device_id_type=MESH)
            cp.wait_send()
            cp.wait_recv()

    hbm = lambda a: pltpu.HBM(a.shape, a.dtype)
    res = pl.pallas_call(
        body, name=name,
        out_shape=tuple(hbm(a) for a in thru + zones),
        in_specs=[_HBM] * (nt + nz) + [_SEM, _SEM, _ANY],
        out_specs=tuple([_HBM] * (nt + nz)),
        input_output_aliases={i: i for i in range(nt + nz)},
        compiler_params=pltpu.CompilerParams(has_side_effects=_EFFECT),
    )(*thru, *zones, send_sems, recv_sems, after)
    return list(res[:nt]), list(res[nt:])


def _split_start(body, arrays, zones, name, after):
    n = len(arrays) + len(zones)
    hbm = lambda a: pltpu.HBM(a.shape, a.dtype)
    sems = pltpu.SemaphoreType.DMA((max(len(zones), 1),))
    extra = [] if after is None else [after]
    operands = [pltpu.with_memory_space_constraint(a, pltpu.HBM) for a in list(arrays) + list(zones)]
    res = pl.pallas_call(
        body, name=name,
        out_shape=(sems, sems, *[hbm(a) for a in operands], SDS((8, 128), F32)),
        in_specs=[_HBM] * n + [_ANY] * len(extra),
        out_specs=(_SEM, _SEM, *[_HBM] * n, pl.BlockSpec(memory_space=pltpu.VMEM)),
        input_output_aliases={i: 2 + i for i in range(n)},
        compiler_params=pltpu.CompilerParams(has_side_effects=_EFFECT),
    )(*operands, *extra)
    return res[0], res[1], list(res[2:2 + len(arrays)]), list(res[2 + len(arrays):2 + n]), res[-1]


def _gather_first_leg(shard, name, after=None):
    zone = lax.empty((N_DEV,) + shard.shape, shard.dtype)
    extra = 0 if after is None else 1

    def body(*refs):
        src, land = refs[0], refs[1]
        send_sem, recv_sem, token = refs[2 + extra], refs[3 + extra], refs[-1]
        x, y, c, me = _place()
        for peer in ((x, y, c), (x, y, 1 - c), (1 - x, y, c), (x, 1 - y, c), (1 - x, 1 - y, c)):
            pltpu.make_async_remote_copy(src_ref=src, dst_ref=land.at[me], send_sem=send_sem.at[0], recv_sem=recv_sem.at[0],
                                         device_id=peer, device_id_type=MESH).start()
        token[...] = jnp.zeros_like(token)

    return _split_start(body, [shard], [zone], name, after)


def _gather_second_leg(zone, name, after=None):
    extra = 0 if after is None else 1

    def body(*refs):
        land = refs[0]
        send_sem, recv_sem, token = refs[1 + extra], refs[2 + extra], refs[-1]
        x, y, c, _ = _place()
        for px, py in ((1 - x, y), (x, 1 - y), (1 - x, 1 - y)):
            slot = 4 * px + 2 * py + c
            pltpu.make_async_remote_copy(src_ref=land.at[slot], dst_ref=land.at[slot], send_sem=send_sem.at[0],
                                         recv_sem=recv_sem.at[0], device_id=(x, y, 1 - c), device_id_type=MESH).start()
        token[...] = jnp.zeros_like(token)

    return _split_start(body, [], [zone], name, after)


UPDATE_BLOCK_ELEMS = 256 * 1024


def _update_rows(R, C):
    fits = [t for t in range(8, R + 1, 8) if R % t == 0 and t * C <= UPDATE_BLOCK_ELEMS]
    whole = [t for t in fits if t % 16 == 0]
    return max(whole or fits)


def _adamw_math(g, w, m, v):
    m2 = ADAM_B1 * m + (1.0 - ADAM_B1) * g
    v2 = ADAM_B2 * v + (1.0 - ADAM_B2) * (g * g)
    m_hat = m2 / (1.0 - ADAM_B1 ** ADAM_STEP)
    v_hat = v2 / (1.0 - ADAM_B2 ** ADAM_STEP)
    delta = -ADAM_LR * (m_hat / (jnp.sqrt(v_hat) + ADAM_EPS) + ADAM_WD * w)
    return delta, m2, v2


def _sum_adamw(parts, w, m, v, name):
    R, C = w.shape
    tr = _update_rows(R, C)

    def body(p_ref, w_ref, m_ref, v_ref, g_ref, d_ref, m2_ref, v2_ref):
        g = p_ref[0]
        for k in range(1, N_DEV):
            g = g + p_ref[k]
        g_ref[...] = g
        d_ref[...], m2_ref[...], v2_ref[...] = _adamw_math(g, w_ref[...], m_ref[...], v_ref[...])

    blk = pl.BlockSpec((tr, C), lambda i: (i, 0))
    return pl.pallas_call(
        body, name=name, grid=(R // tr,),
        in_specs=[pl.BlockSpec((N_DEV, tr, C), lambda i: (0, i, 0)), blk, blk, blk],
        out_specs=[blk] * 4,
        out_shape=[SDS((R, C), F32)] * 4,
        compiler_params=_params(1),
    )(parts, w, m, v)


def _sum_adamw_peers(me, own, parts, w, m, v, name, replicated):
    R, C = w.shape
    tr = _update_rows(R, C)

    def body(me_ref, own_ref, p_ref, w_ref, m_ref, v_ref, g_ref, d_ref, m2_ref, v2_ref):
        if replicated:
            mine = me_ref[0]
            g = None
            for j in range(N_DEV):
                k = jnp.maximum(jnp.bitwise_xor(mine, j) - 1, 0)
                term = jnp.where(mine == j, own_ref[...], p_ref[k])
                g = term if g is None else g + term
        else:
            g = own_ref[...].astype(F32)
            for k in range(N_DEV - 1):
                g = g + p_ref[k].astype(F32)
        g_ref[...] = g
        d_ref[...], m2_ref[...], v2_ref[...] = _adamw_math(g, w_ref[...], m_ref[...], v_ref[...])

    blk = pl.BlockSpec((tr, C), lambda i, me_ref: (i, 0))
    own_spec = blk if replicated else pl.BlockSpec((None, tr, C), lambda i, me_ref: (me_ref[0], i, 0))
    return pl.pallas_call(
        body, name=name,
        grid_spec=pltpu.PrefetchScalarGridSpec(
            num_scalar_prefetch=1, grid=(R // tr,),
            in_specs=[own_spec, pl.BlockSpec((N_DEV - 1, tr, C), lambda i, me_ref: (0, i, 0)), blk, blk, blk],
            out_specs=[blk] * 4),
        out_shape=[SDS((R, C), F32)] * 4,
        compiler_params=_params(1),
    )(me, own, parts, w, m, v)


SMALL = ("ln_v_gain", "ln_v_bias", "w_spatial", "b_spatial", "sinks", "norm_mix_post", "norm_ff_pre", "norm_ff_post")
SMALL_ROWS = {"ln_v_gain": 8, "ln_v_bias": 8, "w_spatial": 1024, "b_spatial": 8, "sinks": 8,
              "norm_mix_post": 8, "norm_ff_pre": 8, "norm_ff_post": 8}
SMALL_PACK_ROWS = 1152


def _pack_small(vals):
    rows = []
    for name in SMALL:
        flat = vals[name].reshape(-1)
        pad = SMALL_ROWS[name] * 128 - flat.shape[0]
        if pad:
            flat = jnp.concatenate([flat, jnp.zeros((pad,), F32)])
        rows.append(flat.reshape(SMALL_ROWS[name], 128))
    rows.append(jnp.zeros((SMALL_PACK_ROWS - sum(SMALL_ROWS.values()), 128), F32))
    return jnp.concatenate(rows, axis=0)


def _unpack_small(packed, shapes):
    out, r = {}, 0
    for name in SMALL:
        n = 1
        for s in shapes[name]:
            n *= s
        out[name] = packed[r:r + SMALL_ROWS[name]].reshape(-1)[:n].reshape(shapes[name])
        r += SMALL_ROWS[name]
    return out


def _rope_rows():
    d = jnp.arange(128) % HEAD
    inv = ROPE_THETA ** (-(2.0 * (d % (ROPE // 2))).astype(F32) / ROPE)
    invf = jnp.where(d < ROPE, inv, 0.0).astype(F32).reshape(1, 128)
    sgn = jnp.where(d < ROPE // 2, -1.0, jnp.where(d < ROPE, 1.0, 0.0)).astype(F32).reshape(1, 128)
    return invf, sgn


def kernel(x, positions, w_in, ln_v_gain, ln_v_bias, w_spatial, b_spatial, sinks, w_a, w_b, w_o, norm_mix_pre, norm_mix_post, w_ff_in, w_ff_out, norm_ff_pre, norm_ff_post, loss_target, m_w_in, m_ln_v_gain, m_ln_v_bias, m_w_spatial, m_b_spatial, m_sinks, m_w_a, m_w_b, m_w_o, m_norm_mix_pre, m_norm_mix_post, m_w_ff_in, m_w_ff_out, m_norm_ff_pre, m_norm_ff_post, v_w_in, v_ln_v_gain, v_ln_v_bias, v_w_spatial, v_b_spatial, v_sinks, v_w_a, v_w_b, v_w_o, v_norm_mix_pre, v_norm_mix_post, v_w_ff_in, v_w_ff_out, v_norm_ff_pre, v_norm_ff_post):
    given = dict(locals())
    T = x.shape[1]
    xt = x[0]
    tgt = loss_target[0]
    bst = b_spatial[0].T
    ws = w_spatial[0]

    me = 4 * lax.axis_index("x") + 2 * lax.axis_index("y") + lax.axis_index("c")
    me_arr = me.astype(jnp.int32).reshape(1)

    rest = ("w_a", "w_b", "w_o", "w_ff_in", "w_ff_out")
    shard = {n: given[n][0].astype(BF16) for n in rest}
    g_one = _gather_first_leg(w_in[0].T.astype(BF16), "gather_in_start")
    cos, sin = _rope_tables(positions.astype(F32).reshape(T, 1), *_rope_rows(), after=g_one[-1])
    h = _rms_pre(xt, norm_mix_pre, after=cos)
    _, (win8,) = _wait_copies(g_one, h, "gather_in_wait", count=5)
    g_two = _gather_second_leg(win8, "gather_in_pass_start")
    g_rest = _start_copies([shard[n] for n in rest], [GATHER] * len(rest), "gather_rest_start", after=g_two[-1])
    _, (win8,) = _wait_copies(g_two, g_rest[-1], "gather_in_pass_wait", count=3)
    win = win8.reshape(IN_W, D)

    proj = _fwd_in(h, win)
    att, qr, kr, probs, psink = _fwd_attn(proj, cos, sin, sinks[0])
    a = _fwd_sgu(proj, ln_v_gain, ln_v_bias, ws, bst, after=att)
    gw = dict(zip(rest, _wait_copies(g_rest, a, "gather_rest_wait", count=N_DEV)[1]))
    wa, wb, wo = (gw[n].reshape(D, D) for n in ("w_a", "w_b", "w_o"))
    wfi3 = gw["w_ff_in"]
    wfo = gw["w_ff_out"].reshape(D_FF, D)
    merged, a2, b2, mix, x1, hf = _fwd_mix(a, att, proj, xt, wa, wb, wo, norm_mix_post, norm_ff_pre)
    f, dy, dff, dg3, loss_part = _fwd_ff(hf, wfi3, wfo, x1, tgt, norm_ff_post)

    df, dx1, dmix, dg2, dg1 = _bwd_ff(dff, f, wfi3, wfo, x1, dy, mix, norm_mix_post, norm_ff_pre)
    dwfi3, dwfo = _wgrad_ff(hf, df, f, dff)
    own_ff = [dwfi3, dwfo.reshape(N_DEV, D_FF // N_DEV, D)]
    x_ff = _start_copies(own_ff, [SCATTER] * 2, "exchange_ff_start")
    da2, db2, dgate, da, datt = _bwd_mix(dmix, proj, a2, b2, wo, wa, wb, after=x_ff[-1])
    dwo, dwa, dwb = _wgrad_mix(merged, dmix, a, da2, att, db2)
    own_mix = [g.reshape(N_DEV, D // N_DEV, D) for g in (dwa, dwb, dwo)]
    x_mix = _start_copies(own_mix, [SCATTER] * 3, "exchange_mix_start")
    dq, dkv, dsink = _bwd_attn(qr, kr, probs, psink, proj, cos, sin, datt, after=x_mix[-1])
    duv, dws, dbs, dlng, dlnb = _bwd_sgu(proj, da, ln_v_gain, ln_v_bias, ws, bst)
    small_grads = {"ln_v_gain": dlng, "ln_v_bias": dlnb, "w_spatial": dws, "b_spatial": dbs, "sinks": dsink[:, :N_Q],
                   "norm_mix_post": dg1, "norm_ff_pre": dg2, "norm_ff_post": dg3}
    x_small = _start_copies([_pack_small(small_grads)], [SPREAD], "exchange_small_start")
    dwin = _wgrad_in(h, duv, dq, dkv, dgate)
    own_in = [dwin.reshape(N_DEV, IN_W // N_DEV, D)]
    x_in = _start_copies(own_in, [SCATTER], "exchange_in_start", after=x_small[-1])
    grad_x, dg0 = _bwd_in(duv, dq, dkv, dgate, win, xt, dx1, norm_mix_pre, after=x_in[-1])

    results = {}

    def update(n, own, parts, transposed=False):
        state = [given[k + n][0].T if transposed else given[k + n][0] for k in ("", "m_", "v_")]
        res = _sum_adamw_peers(me_arr, own, parts, *state, "adamw_" + n, False)
        results[n] = [(r.T if transposed else r).reshape(given[n].shape) for r in res]

    own_ff, p_ff = _wait_copies(x_ff, grad_x, "exchange_ff_wait")
    update("w_ff_in", own_ff[0], p_ff[0])
    update("w_ff_out", own_ff[1], p_ff[1])
    own_mix, p_mix = _wait_copies(x_mix, results["w_ff_out"][0], "exchange_mix_wait")
    for n, own, parts in zip(("w_a", "w_b", "w_o"), own_mix, p_mix):
        update(n, own, parts)
    tail = jnp.concatenate([dg0.reshape(8, 128), jnp.tile(loss_part, (8, 1))], axis=0)
    (tail_all,) = _all_to_all([tail], [True], "exchange_tail", after=results["w_o"][0])
    dg0_all = tail_all[:, :8]
    own_small, p_small = _wait_copies(x_small, tail_all, "exchange_small_wait")
    own_in, p_in = _wait_copies(x_in, p_small[0], "exchange_in_wait")
    update("w_in", own_in[0], p_in[0], transposed=True)
    packed = _sum_adamw_peers(me_arr, own_small[0], p_small[0], _pack_small({n: given[n] for n in SMALL}),
                              _pack_small({n: given["m_" + n] for n in SMALL}),
                              _pack_small({n: given["v_" + n] for n in SMALL}), "adamw_small", True)
    shapes = {n: given[n].shape for n in SMALL}
    unpacked = [_unpack_small(p, shapes) for p in packed]
    for n in SMALL:
        results[n] = [u[n] for u in unpacked]
    n = "norm_mix_pre"
    results[n] = [r.reshape(given[n].shape) for r in _sum_adamw(
        dg0_all, given[n].reshape(8, 128), given["m_" + n].reshape(8, 128), given["v_" + n].reshape(8, 128), "adamw_" + n)]

    loss = jnp.sum(tail_all[:, 8, 0])
    order = ("w_in", "ln_v_gain", "ln_v_bias", "w_spatial", "b_spatial", "sinks", "w_a", "w_b", "w_o", "norm_mix_pre",
             "norm_mix_post", "w_ff_in", "w_ff_out", "norm_ff_pre", "norm_ff_post")
    out = [loss, grad_x.reshape(x.shape)]
    for k in range(4):
        out += [results[n][k] for n in order]
    return tuple(out)
```

```python
import jax
import jax.numpy as jnp
from jax import lax
from jax.experimental import pallas as pl
from jax.experimental.pallas import tpu as pltpu

F32 = jnp.float32
BF16 = jnp.bfloat16

N_DEV = 8
D = 1024
D_FF = 4096
IN_W = 5632
CHUNK = 128
GROUPS = 8
HEAD = 64
N_Q = 16
N_KV = 4
ROPE = 16
ROPE_THETA = 500000.0
EPS = 1e-6
OFF_Q, OFF_K, OFF_VA, OFF_GA, OFF_GB = 2048, 3072, 3328, 3584, 4608

ADAM_LR = 0.001
ADAM_B1 = 0.9
ADAM_B2 = 0.999
ADAM_EPS = 1e-08
ADAM_WD = 0.01
ADAM_STEP = 10

VMEM_LIMIT = 62 * 1024 * 1024

SDS = jax.ShapeDtypeStruct
MESH = pl.DeviceIdType.MESH


def _params(n_axes):
    return pltpu.CompilerParams(dimension_semantics=("arbitrary",) * n_axes, vmem_limit_bytes=VMEM_LIMIT)


def _nt(a, b):
    return lax.dot_general(a, b, (((1,), (1,)), ((), ())), preferred_element_type=F32)


def _tn(a, b):
    return lax.dot_general(a, b, (((0,), (0,)), ((), ())), preferred_element_type=F32)


def _nn(a, b):
    return jnp.dot(a, b, preferred_element_type=F32)


def _gelu(x):
    t = jnp.tanh(0.7978845608028654 * (x + 0.044715 * (x * x * x)))
    return 0.5 * x * (1.0 + t), t


def _gelu_grad(x, t):
    return 0.5 * (1.0 + t) + 0.5 * x * (1.0 - t * t) * (0.7978845608028654 * (1.0 + 3.0 * 0.044715 * x * x))


def _sigmoid(x):
    return 1.0 / (1.0 + jnp.exp(-x))


def _rms_stats(v):
    r = lax.rsqrt(jnp.mean(v * v, axis=-1, keepdims=True) + EPS)
    return r, v * r


def _rms_bwd(d, vhat, r, g):
    gd = g * d
    return r * (gd - vhat * jnp.mean(gd * vhat, axis=-1, keepdims=True))


def _colsum(v):
    return jnp.sum(v, axis=0, keepdims=True)


_ANY = pl.BlockSpec(memory_space=pl.ANY)


def _after(body, n_in, after):
    if after is None:
        return body, [], []
    deps = list(after) if isinstance(after, (list, tuple)) else [after]

    def ordered(*refs):
        return body(*refs[:n_in], *refs[n_in + len(deps):])

    return ordered, [_ANY] * len(deps), deps


def _rms_pre(x, g0, after=None):
    T = x.shape[0]
    tm = min(T, 1024)

    def body(x_ref, g_ref, h_ref):
        _, xh = _rms_stats(x_ref[...])
        h_ref[...] = (xh * g_ref[...]).astype(BF16)

    body, dep_specs, deps = _after(body, 2, after)
    return pl.pallas_call(
        body, name="rms_pre", grid=(T // tm,),
        in_specs=[pl.BlockSpec((tm, D), lambda i: (i, 0)), pl.BlockSpec((1, D), lambda i: (0, 0))] + dep_specs,
        out_specs=pl.BlockSpec((tm, D), lambda i: (i, 0)),
        out_shape=SDS((T, D), BF16),
        compiler_params=_params(1),
    )(x, g0, *deps)


def _fwd_in(h, win_t):
    T = h.shape[0]
    tm, tn = min(T, 512), 1408

    def body(h_ref, w_ref, p_ref):
        for j in range(IN_W // tn):
            cols = slice(j * tn, (j + 1) * tn)
            p_ref[:, cols] = _nt(h_ref[...], w_ref[cols, :]).astype(BF16)

    return pl.pallas_call(
        body, name="fwd_in", grid=(T // tm,),
        in_specs=[pl.BlockSpec((tm, D), lambda i: (i, 0)), _resident((IN_W, D))],
        out_specs=pl.BlockSpec((tm, IN_W), lambda i: (i, 0)),
        out_shape=SDS((T, IN_W), BF16),
        compiler_params=_params(1),
    )(h, win_t)


def _sgu_forward_parts(u_ref, vs_ref, lng_ref, lnb_ref):
    u = u_ref[...].astype(F32)
    vs = vs_ref[...].astype(F32)
    gu, tu = _gelu(u)
    gv, tv = _gelu(vs)
    mu = jnp.mean(gv, axis=-1, keepdims=True)
    dv = gv - mu
    rstd = lax.rsqrt(jnp.mean(dv * dv, axis=-1, keepdims=True) + EPS)
    vhat = dv * rstd
    vn = (vhat * lng_ref[...] + lnb_ref[...]).astype(BF16)
    return u, vs, gu, tu, tv, rstd, vhat, vn


def _masked_ws(ws_ref, g):
    row = lax.broadcasted_iota(jnp.int32, (CHUNK, CHUNK), 0)
    col = lax.broadcasted_iota(jnp.int32, (CHUNK, CHUNK), 1)
    return jnp.where(row >= col, ws_ref[g], 0.0).astype(BF16)


def _fwd_sgu(proj, lng, lnb, ws, bst, after=None):
    T = proj.shape[0]
    tc = min(T, 512)

    def body(u_ref, vs_ref, lng_ref, lnb_ref, ws_ref, bst_ref, a_ref):
        _, _, gu, _, _, _, _, vn = _sgu_forward_parts(u_ref, vs_ref, lng_ref, lnb_ref)
        for g in range(GROUPS):
            wm = _masked_ws(ws_ref, g)
            cols = slice(g * CHUNK, (g + 1) * CHUNK)
            for c in range(tc // CHUNK):
                rows = slice(c * CHUNK, (c + 1) * CHUNK)
                mixed = _nn(wm, vn[rows, cols]) + bst_ref[:, g:g + 1]
                a_ref[rows, cols] = (gu[rows, cols] * mixed).astype(BF16)

    body, dep_specs, deps = _after(body, 6, after)
    return pl.pallas_call(
        body, name="fwd_sgu", grid=(T // tc,),
        in_specs=[pl.BlockSpec((tc, D), lambda i: (i, 0)), pl.BlockSpec((tc, D), lambda i: (i, 1)),
                  pl.BlockSpec((1, D), lambda i: (0, 0)), pl.BlockSpec((1, D), lambda i: (0, 0)),
                  pl.BlockSpec((GROUPS, CHUNK, CHUNK), lambda i: (0, 0, 0)),
                  pl.BlockSpec((CHUNK, GROUPS), lambda i: (0, 0))] + dep_specs,
        out_specs=pl.BlockSpec((tc, D), lambda i: (i, 0)),
        out_shape=SDS((T, D), BF16),
        compiler_params=_params(1),
    )(proj, proj, lng, lnb, ws, bst, *deps)


def _rope_tables(posf, invf, sgn, after=None):
    T = posf.shape[0]
    tr = min(T, 1024)

    def body(pos_ref, invf_ref, sgn_ref, c_ref, s_ref):
        ang = pos_ref[...] * invf_ref[...]
        c_ref[...] = jnp.cos(ang)
        s = jnp.sin(ang)
        s_ref[:, :128] = jnp.where(sgn_ref[...] < 0.0, -s, 0.0)
        s_ref[:, 128:] = jnp.where(sgn_ref[...] > 0.0, s, 0.0)

    body, dep_specs, deps = _after(body, 3, after)
    return pl.pallas_call(
        body, name="rope_tables", grid=(T // tr,),
        in_specs=[pl.BlockSpec((tr, 1), lambda i: (i, 0)), pl.BlockSpec((1, 128), lambda i: (0, 0)),
                  pl.BlockSpec((1, 128), lambda i: (0, 0))] + dep_specs,
        out_specs=[pl.BlockSpec((tr, 128), lambda i: (i, 0)), pl.BlockSpec((tr, 256), lambda i: (i, 0))],
        out_shape=[SDS((T, 128), F32), SDS((T, 256), F32)],
        compiler_params=_params(1),
    )(posf, invf, sgn, *deps)


def _rope(v, c, s):
    v = v.astype(F32)
    return v * c + pltpu.roll(v, 128 - ROPE // 2, 1) * s[:, :128] + pltpu.roll(v, ROPE // 2, 1) * s[:, 128:]


def _rope_bwd(dv, c, s):
    return dv * c + pltpu.roll(dv * s[:, :128], ROPE // 2, 1) + pltpu.roll(dv * s[:, 128:], 128 - ROPE // 2, 1)


def _fold_masks(first):
    jj = lax.broadcasted_iota(jnp.int32, (CHUNK, CHUNK), 0)
    t = lax.broadcasted_iota(jnp.int32, (CHUNK, CHUNK), 1)
    prev = jj > t
    return prev, jnp.where(prev & first, -1e30, 0.0)


def _fold(band, prev):
    return jnp.where(prev, band[:CHUNK], band[CHUNK:])


def _unfold(folded, prev):
    return jnp.concatenate([jnp.where(prev, folded, 0.0), jnp.where(prev, 0.0, folded)], axis=0)


def _softmax_sink(s, sink, key_axis):
    m = jnp.maximum(jnp.max(s, axis=key_axis, keepdims=True), sink)
    p = jnp.exp(s - m)
    esink = jnp.exp(sink - m)
    inv = 1.0 / (jnp.sum(p, axis=key_axis, keepdims=True) + esink)
    return p * inv, esink * inv


def _head_pair_operand(slab, g):
    lo = lax.broadcasted_iota(jnp.int32, slab.shape, 1) < HEAD
    if g % 2 == 0:
        first = jnp.where(lo, slab, 0.0)
        second = pltpu.roll(first, HEAD, 1)
    else:
        second = jnp.where(lo, 0.0, slab)
        first = pltpu.roll(second, HEAD, 1)
    return jnp.concatenate([first, second], axis=0).astype(BF16)


def _head_pair_gradient(acc, g):
    top, bot = acc[:2 * CHUNK], acc[2 * CHUNK:]
    lo = lax.broadcasted_iota(jnp.int32, top.shape, 1) < HEAD
    if g % 2 == 0:
        return jnp.where(lo, top, 0.0) + pltpu.roll(jnp.where(lo, 0.0, bot), HEAD, 1)
    return pltpu.roll(jnp.where(lo, top, 0.0), HEAD, 1) + jnp.where(lo, 0.0, bot)


PAIRS_PER_KV = N_Q // N_KV // 2
KV_W = N_KV * HEAD


def _band(prev_ref, cur_ref, cols=slice(None)):
    return jnp.concatenate([prev_ref[:, cols], cur_ref[:, cols]], axis=0)


def _fwd_attn(proj, cos, sin, sinks):
    T = proj.shape[0]
    nb = T // CHUNK
    cur = lambda i: i
    prev = lambda i: jnp.maximum(i - 1, 0)

    def body(q_ref, kp_ref, kc_ref, vp_ref, vc_ref, cp_ref, cc_ref, sp_ref, sc_ref, sink_ref,
             o_ref, qr_ref, kr_ref, p_ref, psink_ref):
        prev_slot, bias = _fold_masks(pl.program_id(0) == 0)
        c_band, s_band = _band(cp_ref, cc_ref), _band(sp_ref, sc_ref)
        for j in range(KV_W // 128):
            cols = slice(j * 128, (j + 1) * 128)
            k_slab = _rope(_band(kp_ref, kc_ref, cols), c_band, s_band)
            kr_ref[:, cols] = k_slab[CHUNK:].astype(BF16)
            v_slab = _band(vp_ref, vc_ref, cols).astype(F32)
            for g in (2 * j, 2 * j + 1):
                k2 = _head_pair_operand(k_slab, g)
                v2 = _head_pair_operand(v_slab, g)
                pairs = [g * PAIRS_PER_KV + r for r in range(PAIRS_PER_KV)]
                qps = []
                for pair in pairs:
                    lanes = slice(pair * 128, (pair + 1) * 128)
                    qps.append((_rope(q_ref[:, lanes], cc_ref[...], sc_ref[...]) * (HEAD ** -0.5)).astype(BF16))
                    qr_ref[:, lanes] = qps[-1]
                s2 = _nt(k2, jnp.concatenate(qps, axis=0))
                pcols = []
                for r, pair in enumerate(pairs):
                    ps = []
                    for e in range(2):
                        head = 2 * pair + e
                        s = _fold(s2[e * 2 * CHUNK:(e + 1) * 2 * CHUNK, r * 128:(r + 1) * 128], prev_slot) + bias
                        p, psink = _softmax_sink(s, sink_ref[head], 0)
                        p = p.astype(BF16)
                        p_ref[head] = p
                        psink_ref[head:head + 1, :] = psink
                        ps.append(_unfold(p, prev_slot))
                    pcols.append(jnp.concatenate(ps, axis=0))
                o = _tn(jnp.concatenate(pcols, axis=1), v2).astype(BF16)
                for r, pair in enumerate(pairs):
                    o_ref[:, pair * 128:(pair + 1) * 128] = o[r * CHUNK:(r + 1) * CHUNK]

    table = lambda which, width: pl.BlockSpec((CHUNK, width), lambda i: (which(i), 0))
    return pl.pallas_call(
        body, name="fwd_attn", grid=(nb,),
        in_specs=[pl.BlockSpec((CHUNK, D), lambda i: (i, OFF_Q // D)),
                  pl.BlockSpec((CHUNK, KV_W), lambda i: (prev(i), OFF_K // KV_W)),
                  pl.BlockSpec((CHUNK, KV_W), lambda i: (i, OFF_K // KV_W)),
                  pl.BlockSpec((CHUNK, KV_W), lambda i: (prev(i), OFF_VA // KV_W)),
                  pl.BlockSpec((CHUNK, KV_W), lambda i: (i, OFF_VA // KV_W)),
                  table(prev, 128), table(cur, 128), table(prev, 256), table(cur, 256),
                  pl.BlockSpec(memory_space=pltpu.SMEM)],
        out_specs=[pl.BlockSpec((CHUNK, D), lambda i: (i, 0)), pl.BlockSpec((CHUNK, D), lambda i: (i, 0)),
                   pl.BlockSpec((CHUNK, KV_W), lambda i: (i, 0)),
                   pl.BlockSpec((None, N_Q, CHUNK, CHUNK), lambda i: (i, 0, 0, 0)),
                   pl.BlockSpec((None, N_Q, CHUNK), lambda i: (i, 0, 0))],
        out_shape=[SDS((T, D), BF16), SDS((T, D), BF16), SDS((T, KV_W), BF16),
                   SDS((nb, N_Q, CHUNK, CHUNK), BF16), SDS((nb, N_Q, CHUNK), F32)],
        compiler_params=_params(1),
    )(proj, proj, proj, proj, proj, cos, cos, sin, sin, sinks)


def _row_halves(tm):
    return [slice(0, tm // 2), slice(tm // 2, tm)] if tm % 32 == 0 else [slice(0, tm)]


def _fwd_mix(a, att, proj, x, wa, wb, wo, g1, g2):
    T = x.shape[0]
    tm = min(T, 512)
    half = D // 2

    def body(a_ref, att_ref, ga0, ga1, gb0, gb1, x_ref, wa_ref, wb_ref, wo_ref, g1_ref, g2_ref,
             mg_ref, a2_ref, b2_ref, mix_ref, x1_ref, hf_ref):
        for rows in _row_halves(tm):
            a2 = _nn(a_ref[rows, :], wa_ref[...])
            b2 = _nn(att_ref[rows, :], wb_ref[...])
            ga = jnp.concatenate([ga0[rows, :], ga1[rows, :]], axis=1).astype(F32)
            gb = jnp.concatenate([gb0[rows, :], gb1[rows, :]], axis=1).astype(F32)
            merged = (_sigmoid(ga) * a2 + _sigmoid(gb) * b2).astype(BF16)
            a2_ref[rows, :] = a2.astype(BF16)
            b2_ref[rows, :] = b2.astype(BF16)
            mg_ref[rows, :] = merged
            mix = _nn(merged, wo_ref[...])
            mix_ref[rows, :] = mix
            _, mh = _rms_stats(mix)
            x1 = x_ref[rows, :] + mh * g1_ref[...]
            x1_ref[rows, :] = x1
            _, xh = _rms_stats(x1)
            hf_ref[rows, :] = (xh * g2_ref[...]).astype(BF16)

    row = lambda i: (i, 0)
    const = lambda i: (0, 0)
    gspec = lambda off: pl.BlockSpec((tm, half), lambda i: (i, off // half))
    return pl.pallas_call(
        body, name="fwd_mix", grid=(T // tm,),
        in_specs=[pl.BlockSpec((tm, D), row), pl.BlockSpec((tm, D), row),
                  gspec(OFF_GA), gspec(OFF_GA + half), gspec(OFF_GB), gspec(OFF_GB + half),
                  pl.BlockSpec((tm, D), row), _resident((D, D)), _resident((D, D)),
                  _resident((D, D)), pl.BlockSpec((1, D), const), pl.BlockSpec((1, D), const)],
        out_specs=[pl.BlockSpec((tm, D), row)] * 6,
        out_shape=[SDS((T, D), BF16), SDS((T, D), BF16), SDS((T, D), BF16), SDS((T, D), F32), SDS((T, D), F32),
                   SDS((T, D), BF16)],
        compiler_params=_params(1),
    )(a, att, proj, proj, proj, proj, x, wa, wb, wo, g1, g2)


FF_SPLIT = N_DEV
FF_TILE = D_FF // FF_SPLIT


def _fwd_ff(hf, wfi3, wfo, x1, tgt, g3):
    T = hf.shape[0]
    tm = min(T, 512)

    def body(hf_ref, wfi_ref, wfo_ref, x1_ref, tgt_ref, g3_ref, f_ref, dy_ref, dff_ref, dg3_ref, loss_ref, r_s):
        @pl.when(pl.program_id(0) == 0)
        def _():
            dg3_ref[...] = jnp.zeros_like(dg3_ref)
            loss_ref[...] = jnp.zeros_like(loss_ref)

        hf_t = hf_ref[...]
        for s in range(FF_SPLIT):
            cols = slice(s * FF_TILE, (s + 1) * FF_TILE)
            f = _nn(hf_t, wfi_ref[s]).astype(BF16)
            f_ref[:, cols] = f
            rl = jnp.maximum(f.astype(F32), 0.0)
            r_s[:, cols] = (rl * rl).astype(BF16)
        r3, fh = _rms_stats(_nn(r_s[...], wfo_ref[...]))
        e = x1_ref[...] + fh * g3_ref[...] - tgt_ref[...]
        loss_ref[...] += jnp.sum(e * e) * (0.5 / D)
        dy = e * (1.0 / D)
        dy_ref[...] = dy
        dg3_ref[...] += _colsum(dy * fh)
        dff_ref[...] = _rms_bwd(dy, fh, r3, g3_ref[...]).astype(BF16)

    row = lambda i: (i, 0)
    const = lambda i: (0, 0)
    return pl.pallas_call(
        body, name="fwd_ff", grid=(T // tm,),
        in_specs=[pl.BlockSpec((tm, D), row), _resident((FF_SPLIT, D, FF_TILE)), _resident((D_FF, D)),
                  pl.BlockSpec((tm, D), row),
                  pl.BlockSpec((tm, D), row), pl.BlockSpec((1, D), const)],
        out_specs=[pl.BlockSpec((tm, D_FF), row), pl.BlockSpec((tm, D), row),
                   pl.BlockSpec((tm, D), row), pl.BlockSpec((1, D), const), pl.BlockSpec((1, 128), const)],
        out_shape=[SDS((T, D_FF), BF16), SDS((T, D), F32), SDS((T, D), BF16), SDS((1, D), F32), SDS((1, 128), F32)],
        scratch_shapes=[pltpu.VMEM((tm, D_FF), BF16)],
        compiler_params=_params(1),
    )(hf, wfi3, wfo, x1, tgt, g3)


def _bwd_ff(dff, f, wfi3, wfo, x1, dy, mix, g1, g2):
    T = dff.shape[0]
    tm = min(T, 512)

    def body(dff_ref, f_ref, wfi_ref, wfo_ref, x1_ref, dy_ref, mix_ref, g1_ref, g2_ref,
             df_ref, dx1_ref, dmix_ref, dg2_ref, dg1_ref):
        @pl.when(pl.program_id(0) == 0)
        def _():
            dg2_ref[...] = jnp.zeros_like(dg2_ref)
            dg1_ref[...] = jnp.zeros_like(dg1_ref)

        dff_t = dff_ref[...]
        dhf = None
        for s in range(FF_SPLIT):
            cols = slice(s * FF_TILE, (s + 1) * FF_TILE)
            dr = _nt(dff_t, wfo_ref[cols, :])
            df = (dr * (2.0 * jnp.maximum(f_ref[:, cols].astype(F32), 0.0))).astype(BF16)
            df_ref[:, cols] = df
            part = _nt(df, wfi_ref[s])
            dhf = part if dhf is None else dhf + part
        r2, xh = _rms_stats(x1_ref[...])
        dg2_ref[...] += _colsum(dhf * xh)
        dx1 = dy_ref[...] + _rms_bwd(dhf, xh, r2, g2_ref[...])
        dx1_ref[...] = dx1
        r1, mh = _rms_stats(mix_ref[...])
        dg1_ref[...] += _colsum(dx1 * mh)
        dmix_ref[...] = _rms_bwd(dx1, mh, r1, g1_ref[...]).astype(BF16)

    row = lambda i: (i, 0)
    const = lambda i: (0, 0)
    return pl.pallas_call(
        body, name="bwd_ff", grid=(T // tm,),
        in_specs=[pl.BlockSpec((tm, D), row), pl.BlockSpec((tm, D_FF), row),
                  _resident((FF_SPLIT, D, FF_TILE)), _resident((D_FF, D)),
                  pl.BlockSpec((tm, D), row), pl.BlockSpec((tm, D), row), pl.BlockSpec((tm, D), row),
                  pl.BlockSpec((1, D), const), pl.BlockSpec((1, D), const)],
        out_specs=[pl.BlockSpec((tm, D_FF), row), pl.BlockSpec((tm, D), row),
                   pl.BlockSpec((tm, D), row), pl.BlockSpec((1, D), const), pl.BlockSpec((1, D), const)],
        out_shape=[SDS((T, D_FF), BF16), SDS((T, D), F32), SDS((T, D), BF16), SDS((1, D), F32), SDS((1, D), F32)],
        compiler_params=_params(1),
    )(dff, f, wfi3, wfo, x1, dy, mix, g1, g2)


def _wgrad_ff(hf, df, f, dff):
    T = hf.shape[0]
    tt = min(T, 2048)
    slabs = 2
    wide = slabs * FF_TILE

    def body(hf_ref, df_ref, f_ref, dff_ref, dwfi_ref, dwfo_ref, acc_i, acc_o):
        t = pl.program_id(1)

        @pl.when(t == 0)
        def _():
            acc_i[...] = jnp.zeros_like(acc_i)
            acc_o[...] = jnp.zeros_like(acc_o)

        acc_i[...] += _tn(hf_ref[...], df_ref[...])
        rl = jnp.maximum(f_ref[...].astype(F32), 0.0)
        acc_o[...] += _tn((rl * rl).astype(BF16), dff_ref[...])

        @pl.when(t == T // tt - 1)
        def _():
            for s in range(slabs):
                dwfi_ref[s] = acc_i[:, s * FF_TILE:(s + 1) * FF_TILE].astype(BF16)
            dwfo_ref[...] = acc_o[...].astype(BF16)

    return pl.pallas_call(
        body, name="wgrad_ff", grid=(D_FF // wide, T // tt),
        in_specs=[pl.BlockSpec((tt, D), lambda p, t: (t, 0)), pl.BlockSpec((tt, wide), lambda p, t: (t, p)),
                  pl.BlockSpec((tt, wide), lambda p, t: (t, p)), pl.BlockSpec((tt, D), lambda p, t: (t, 0))],
        out_specs=[pl.BlockSpec((slabs, D, FF_TILE), lambda p, t: (p, 0, 0)), pl.BlockSpec((wide, D), lambda p, t: (p, 0))],
        out_shape=[SDS((FF_SPLIT, D, FF_TILE), BF16), SDS((D_FF, D), BF16)],
        scratch_shapes=[pltpu.VMEM((D, wide), F32), pltpu.VMEM((wide, D), F32)],
        compiler_params=_params(2),
    )(hf, df, f, dff)


def _bwd_mix(dmix, proj, a2, b2, wo, wa, wb, after=None):
    T = dmix.shape[0]
    tm = min(T, 512)
    half = D // 2

    def body(dmix_ref, ga0, ga1, gb0, gb1, a2_ref, b2_ref, wo_ref, wa_ref, wb_ref,
             da2_ref, db2_ref, dg_ref, da_ref, datt_ref):
        for rows in _row_halves(tm):
            dmg = _nt(dmix_ref[rows, :], wo_ref[...])
            sa = _sigmoid(jnp.concatenate([ga0[rows, :], ga1[rows, :]], axis=1).astype(F32))
            sb = _sigmoid(jnp.concatenate([gb0[rows, :], gb1[rows, :]], axis=1).astype(F32))
            da2 = (dmg * sa).astype(BF16)
            db2 = (dmg * sb).astype(BF16)
            da2_ref[rows, :] = da2
            db2_ref[rows, :] = db2
            dg_ref[rows, :D] = (dmg * a2_ref[rows, :].astype(F32) * (sa * (1.0 - sa))).astype(BF16)
            dg_ref[rows, D:] = (dmg * b2_ref[rows, :].astype(F32) * (sb * (1.0 - sb))).astype(BF16)
            da_ref[rows, :] = _nt(da2, wa_ref[...]).astype(BF16)
            datt_ref[rows, :] = _nt(db2, wb_ref[...]).astype(BF16)

    row = lambda i: (i, 0)
    const = lambda i: (0, 0)
    gspec = lambda off: pl.BlockSpec((tm, half), lambda i: (i, off // half))
    body, dep_specs, deps = _after(body, 10, after)
    return pl.pallas_call(
        body, name="bwd_mix", grid=(T // tm,),
        in_specs=[pl.BlockSpec((tm, D), row), gspec(OFF_GA), gspec(OFF_GA + half), gspec(OFF_GB), gspec(OFF_GB + half),
                  pl.BlockSpec((tm, D), row), pl.BlockSpec((tm, D), row),
                  _resident((D, D)), _resident((D, D)), _resident((D, D))] + dep_specs,
        out_specs=[pl.BlockSpec((tm, D), row), pl.BlockSpec((tm, D), row), pl.BlockSpec((tm, 2 * D), row),
                   pl.BlockSpec((tm, D), row), pl.BlockSpec((tm, D), row)],
        out_shape=[SDS((T, D), BF16), SDS((T, D), BF16), SDS((T, 2 * D), BF16), SDS((T, D), BF16), SDS((T, D), BF16)],
        compiler_params=_params(1),
    )(dmix, proj, proj, proj, proj, a2, b2, wo, wa, wb, *deps)


def _wgrad_mix(merged, dmix, a, da2, att, db2):
    T = merged.shape[0]
    tt = min(T, 1024)

    def body(mg_ref, dmix_ref, a_ref, da2_ref, att_ref, db2_ref, dwo_ref, dwa_ref, dwb_ref, acc):
        t = pl.program_id(0)

        @pl.when(t == 0)
        def _():
            acc[...] = jnp.zeros_like(acc)

        acc[0] += _tn(mg_ref[...], dmix_ref[...])
        acc[1] += _tn(a_ref[...], da2_ref[...])
        acc[2] += _tn(att_ref[...], db2_ref[...])

        @pl.when(t == T // tt - 1)
        def _():
            dwo_ref[...] = acc[0].astype(BF16)
            dwa_ref[...] = acc[1].astype(BF16)
            dwb_ref[...] = acc[2].astype(BF16)

    return pl.pallas_call(
        body, name="wgrad_mix", grid=(T // tt,),
        in_specs=[pl.BlockSpec((tt, D), lambda t: (t, 0))] * 6,
        out_specs=[pl.BlockSpec((D, D), lambda t: (0, 0))] * 3,
        out_shape=[SDS((D, D), BF16)] * 3,
        scratch_shapes=[pltpu.VMEM((3, D, D), F32)],
        compiler_params=_params(1),
    )(merged, dmix, a, da2, att, db2)


def _bwd_attn(qr, kr, probs, psink, proj, cos, sin, datt, after=None):
    T = proj.shape[0]
    nb = T // CHUNK
    cur = lambda i: jnp.minimum(i, nb - 1)
    prev = lambda i: jnp.maximum(jnp.minimum(i, nb - 1) - 1, 0)

    def body(q_ref, kp_ref, kc_ref, vp_ref, vc_ref, cp_ref, cc_ref, sp_ref, sc_ref, p_ref, psink_ref, do_ref,
             dq_ref, dkv_ref, dsink_ref, carry_k, carry_v):
        i = pl.program_id(0)

        @pl.when(i == 0)
        def _():
            carry_k[...] = jnp.zeros_like(carry_k)
            carry_v[...] = jnp.zeros_like(carry_v)
            dsink_ref[...] = jnp.zeros_like(dsink_ref)

        @pl.when(i < nb)
        def _():
            prev_slot, _ = _fold_masks(i == 0)
            c_band, s_band = _band(cp_ref, cc_ref), _band(sp_ref, sc_ref)
            lane = lax.broadcasted_iota(jnp.int32, (1, 128), 1)
            dsink = jnp.zeros((1, 128), F32)
            for j in range(KV_W // 128):
                cols = slice(j * 128, (j + 1) * 128)
                k_slab = _band(kp_ref, kc_ref, cols).astype(F32)
                v_slab = _band(vp_ref, vc_ref, cols).astype(F32)
                dk_slab = jnp.zeros((2 * CHUNK, 128), F32)
                dv_slab = jnp.zeros((2 * CHUNK, 128), F32)
                for g in (2 * j, 2 * j + 1):
                    k2 = _head_pair_operand(k_slab, g)
                    v2 = _head_pair_operand(v_slab, g)
                    pairs = [g * PAIRS_PER_KV + r for r in range(PAIRS_PER_KV)]
                    q_stack = jnp.concatenate([q_ref[:, pr * 128:(pr + 1) * 128] for pr in pairs], axis=0)
                    do_stack = jnp.concatenate([do_ref[:, pr * 128:(pr + 1) * 128] for pr in pairs], axis=0)
                    dp2 = _nt(v2, do_stack)
                    pcols, dscols = [], []
                    for r, pair in enumerate(pairs):
                        ps, dss = [], []
                        for e in range(2):
                            head = 2 * pair + e
                            p_b = p_ref[head]
                            p = p_b.astype(F32)
                            dp = _fold(dp2[e * 2 * CHUNK:(e + 1) * 2 * CHUNK, r * 128:(r + 1) * 128], prev_slot)
                            delta = jnp.sum(p * dp, axis=0, keepdims=True)
                            ps.append(_unfold(p_b, prev_slot))
                            dss.append(_unfold((p * (dp - delta)).astype(BF16), prev_slot))
                            dsink = dsink + jnp.where(lane == head, -jnp.sum(psink_ref[head:head + 1, :] * delta), 0.0)
                        pcols.append(jnp.concatenate(ps, axis=0))
                        dscols.append(jnp.concatenate(dss, axis=0))
                    ds2 = jnp.concatenate(dscols, axis=1)
                    dq = _tn(ds2, k2) * (HEAD ** -0.5)
                    for r, pair in enumerate(pairs):
                        dq_ref[:, pair * 128:(pair + 1) * 128] = _rope_bwd(
                            dq[r * CHUNK:(r + 1) * CHUNK], cc_ref[...], sc_ref[...]).astype(BF16)
                    dk_slab = dk_slab + _head_pair_gradient(_nn(ds2, q_stack), g)
                    dv_slab = dv_slab + _head_pair_gradient(_nn(jnp.concatenate(pcols, axis=1), do_stack), g)
                dk_slab = _rope_bwd(dk_slab, c_band, s_band)
                vcols = slice(KV_W + j * 128, KV_W + (j + 1) * 128)
                dkv_ref[:, cols] = (carry_k[:, cols] + dk_slab[:CHUNK]).astype(BF16)
                dkv_ref[:, vcols] = (carry_v[:, cols] + dv_slab[:CHUNK]).astype(BF16)
                carry_k[:, cols] = dk_slab[CHUNK:]
                carry_v[:, cols] = dv_slab[CHUNK:]
            dsink_ref[...] += dsink

        @pl.when(i == nb)
        def _():
            dkv_ref[:, :KV_W] = carry_k[...].astype(BF16)
            dkv_ref[:, KV_W:] = carry_v[...].astype(BF16)

    table = lambda which, width: pl.BlockSpec((CHUNK, width), lambda i: (which(i), 0))
    body, dep_specs, deps = _after(body, 12, after)
    return pl.pallas_call(
        body, name="bwd_attn", grid=(nb + 1,),
        in_specs=[pl.BlockSpec((CHUNK, D), lambda i: (cur(i), 0)),
                  pl.BlockSpec((CHUNK, KV_W), lambda i: (prev(i), 0)),
                  pl.BlockSpec((CHUNK, KV_W), lambda i: (cur(i), 0)),
                  pl.BlockSpec((CHUNK, KV_W), lambda i: (prev(i), OFF_VA // KV_W)),
                  pl.BlockSpec((CHUNK, KV_W), lambda i: (cur(i), OFF_VA // KV_W)),
                  table(prev, 128), table(cur, 128), table(prev, 256), table(cur, 256),
                  pl.BlockSpec((None, N_Q, CHUNK, CHUNK), lambda i: (cur(i), 0, 0, 0)),
                  pl.BlockSpec((None, N_Q, CHUNK), lambda i: (cur(i), 0, 0)),
                  pl.BlockSpec((CHUNK, D), lambda i: (cur(i), 0))] + dep_specs,
        out_specs=[pl.BlockSpec((CHUNK, D), lambda i: (cur(i), 0)),
                   pl.BlockSpec((CHUNK, 2 * KV_W), lambda i: (jnp.maximum(i - 1, 0), 0)),
                   pl.BlockSpec((1, 128), lambda i: (0, 0))],
        out_shape=[SDS((T, D), BF16), SDS((T, 2 * KV_W), BF16), SDS((1, 128), F32)],
        scratch_shapes=[pltpu.VMEM((CHUNK, KV_W), F32), pltpu.VMEM((CHUNK, KV_W), F32)],
        compiler_params=_params(1),
    )(qr, kr, kr, proj, proj, cos, cos, sin, sin, probs, psink, datt, *deps)


def _bwd_sgu(proj, da, lng, lnb, ws, bst):
    T = proj.shape[0]
    tc = min(T, 512)
    nsteps = T // tc

    def body(u_ref, vs_ref, da_ref, lng_ref, lnb_ref, ws_ref, bst_ref,
             duv_ref, dws_ref, dbs_ref, dlng_ref, dlnb_ref, dvn_s, dgu_s, dmx_sum):
        i = pl.program_id(0)

        @pl.when(i == 0)
        def _():
            dws_ref[...] = jnp.zeros_like(dws_ref)
            dlng_ref[...] = jnp.zeros_like(dlng_ref)
            dlnb_ref[...] = jnp.zeros_like(dlnb_ref)
            dmx_sum[...] = jnp.zeros_like(dmx_sum)

        u, vs, gu, tu, tv, rstd, vhat, vn = _sgu_forward_parts(u_ref, vs_ref, lng_ref, lnb_ref)
        da = da_ref[...].astype(F32)
        for g in range(GROUPS):
            wm = _masked_ws(ws_ref, g)
            cols = slice(g * CHUNK, (g + 1) * CHUNK)
            dws = jnp.zeros((CHUNK, CHUNK), F32)
            dsum = jnp.zeros((CHUNK, CHUNK), F32)
            for c in range(tc // CHUNK):
                rows = slice(c * CHUNK, (c + 1) * CHUNK)
                vn_cg = vn[rows, cols]
                mixed = _nn(wm, vn_cg) + bst_ref[:, g:g + 1]
                dgu_s[rows, cols] = da[rows, cols] * mixed
                dmx = da[rows, cols] * gu[rows, cols]
                dmxb = dmx.astype(BF16)
                dws = dws + _nt(dmxb, vn_cg)
                dsum = dsum + dmx
                dvn_s[rows, cols] = _tn(wm, dmxb)
            dws_ref[g] += dws
            dmx_sum[:, cols] += dsum
        dvn = dvn_s[...]
        dlng_ref[...] += _colsum(dvn * vhat)
        dlnb_ref[...] += _colsum(dvn)
        dvh = dvn * lng_ref[...]
        dgv = rstd * (dvh - jnp.mean(dvh, axis=-1, keepdims=True) - vhat * jnp.mean(dvh * vhat, axis=-1, keepdims=True))
        duv_ref[:, :D] = (dgu_s[...] * _gelu_grad(u, tu)).astype(BF16)
        duv_ref[:, D:] = (dgv * _gelu_grad(vs, tv)).astype(BF16)

        @pl.when(i == nsteps - 1)
        def _():
            row = lax.broadcasted_iota(jnp.int32, (CHUNK, CHUNK), 0)
            col = lax.broadcasted_iota(jnp.int32, (CHUNK, CHUNK), 1)
            for g in range(GROUPS):
                dws_ref[g] = jnp.where(row >= col, dws_ref[g], 0.0)
                dbs_ref[g:g + 1, :] = _colsum(dmx_sum[:, g * CHUNK:(g + 1) * CHUNK].T)

    const2 = lambda i: (0, 0)
    return pl.pallas_call(
        body, name="bwd_sgu", grid=(nsteps,),
        in_specs=[pl.BlockSpec((tc, D), lambda i: (i, 0)), pl.BlockSpec((tc, D), lambda i: (i, 1)),
                  pl.BlockSpec((tc, D), lambda i: (i, 0)), pl.BlockSpec((1, D), const2), pl.BlockSpec((1, D), const2),
                  pl.BlockSpec((GROUPS, CHUNK, CHUNK), lambda i: (0, 0, 0)), pl.BlockSpec((CHUNK, GROUPS), const2)],
        out_specs=[pl.BlockSpec((tc, 2 * D), lambda i: (i, 0)), pl.BlockSpec((GROUPS, CHUNK, CHUNK), lambda i: (0, 0, 0)),
                   pl.BlockSpec((GROUPS, CHUNK), const2), pl.BlockSpec((1, D), const2), pl.BlockSpec((1, D), const2)],
        out_shape=[SDS((T, 2 * D), BF16), SDS((GROUPS, CHUNK, CHUNK), F32), SDS((GROUPS, CHUNK), F32),
                   SDS((1, D), F32), SDS((1, D), F32)],
        scratch_shapes=[pltpu.VMEM((tc, D), F32), pltpu.VMEM((tc, D), F32), pltpu.VMEM((CHUNK, D), F32)],
        compiler_params=_params(1),
    )(proj, proj, da, lng, lnb, ws, bst)


IN_SEG_WIDTHS = (2 * D, D, 2 * N_KV * HEAD, 2 * D)


def _resident(shape):
    return pl.BlockSpec(shape, lambda *_: (0,) * len(shape), pipeline_mode=pl.Buffered(1))


def _bwd_in(duv, dq, dkv, dg, win_t, x, dx1, g0, after=None):
    T = x.shape[0]
    tm = min(T, 512)

    def body(duv_ref, dq_ref, dkv_ref, dg_ref, w_ref, x_ref, dx1_ref, g0_ref, gx_ref, dg0_ref):
        @pl.when(pl.program_id(0) == 0)
        def _():
            dg0_ref[...] = jnp.zeros_like(dg0_ref)

        dh, off = None, 0
        for ref, width in zip((duv_ref, dq_ref, dkv_ref, dg_ref), IN_SEG_WIDTHS):
            part = _nn(ref[...], w_ref[off:off + width, :])
            dh = part if dh is None else dh + part
            off += width
        r0, xh = _rms_stats(x_ref[...])
        dg0_ref[...] += _colsum(dh * xh)
        gx_ref[...] = dx1_ref[...] + _rms_bwd(dh, xh, r0, g0_ref[...])

    row = lambda i: (i, 0)
    body, dep_specs, deps = _after(body, 8, after)
    return pl.pallas_call(
        body, name="bwd_in", grid=(T // tm,),
        in_specs=[pl.BlockSpec((tm, w), row) for w in IN_SEG_WIDTHS] + [
            _resident((IN_W, D)), pl.BlockSpec((tm, D), row), pl.BlockSpec((tm, D), row),
            pl.BlockSpec((1, D), lambda i: (0, 0))] + dep_specs,
        out_specs=[pl.BlockSpec((tm, D), row), pl.BlockSpec((1, D), lambda i: (0, 0))],
        out_shape=[SDS((T, D), F32), SDS((1, D), F32)],
        compiler_params=_params(1),
    )(duv, dq, dkv, dg, win_t, x, dx1, g0, *deps)


def _wgrad_rows(h, segs, first_row, into, name):
    T = h.shape[0]
    tt = min(T, 2048)
    widths = [s.shape[1] for s in segs]
    rows = sum(widths)
    n_in = 1 + len(segs) + (into is not None)

    def body(*refs):
        h_ref, seg_refs = refs[0], refs[1:1 + len(segs)]
        dw_ref, acc, stage, sem = refs[n_in], refs[n_in + 1], refs[n_in + 2], refs[n_in + 3]
        t = pl.program_id(0)

        @pl.when(t == 0)
        def _():
            acc[...] = jnp.zeros_like(acc)

        off = 0
        for ref, width in zip(seg_refs, widths):
            acc[off:off + width, :] += _tn(ref[...], h_ref[...])
            off += width

        @pl.when(t == T // tt - 1)
        def _():
            stage[...] = acc[...].astype(BF16)
            out = pltpu.make_async_copy(stage, dw_ref.at[pl.ds(first_row, rows)], sem)
            out.start()
            out.wait()

    row = lambda t: (t, 0)
    return pl.pallas_call(
        body, name=name, grid=(T // tt,),
        in_specs=[pl.BlockSpec((tt, D), row)] + [pl.BlockSpec((tt, w), row) for w in widths] + [_ANY] * (into is not None),
        out_specs=_ANY,
        out_shape=SDS((IN_W, D), BF16),
        input_output_aliases={} if into is None else {n_in - 1: 0},
        scratch_shapes=[pltpu.VMEM((rows, D), F32), pltpu.VMEM((rows, D), BF16), pltpu.SemaphoreType.DMA],
        compiler_params=_params(1),
    )(h, *segs, *([] if into is None else [into]))


def _wgrad_in(h, duv, dq, dkv, dg):
    dw = _wgrad_rows(h, [dg], IN_SEG_WIDTHS[0] + IN_SEG_WIDTHS[1] + IN_SEG_WIDTHS[2], None, "wgrad_in_gates")
    dw = _wgrad_rows(h, [duv], 0, dw, "wgrad_in_uv")
    return _wgrad_rows(h, [dq, dkv], IN_SEG_WIDTHS[0], dw, "wgrad_in_qkv")


def _place():
    x, y, c = lax.axis_index("x"), lax.axis_index("y"), lax.axis_index("c")
    return x, y, c, 4 * x + 2 * y + c


def _peers(x, y, c):
    out = []
    for mask in range(1, N_DEV):
        px = 1 - x if mask & 4 else x
        py = 1 - y if mask & 2 else y
        pc = 1 - c if mask & 1 else c
        out.append(((px, py, pc), 4 * px + 2 * py + pc))
    return out


def _all_to_all(arrays, gather, name, after=None):
    n = len(arrays)

    def body(*refs):
        ins, outs = refs[:n], refs[n:2 * n]
        send_sems, recv_sems, local_sems = refs[2 * n:]
        x, y, c, me = _place()
        local, sends, recvs = [], [], []
        for a in range(n):
            src_own = ins[a] if gather[a] else ins[a].at[me]
            local.append(pltpu.make_async_copy(src_own, outs[a].at[me], local_sems.at[a]))
            for k, (peer, pid) in enumerate(_peers(x, y, c)):
                sem = a * (N_DEV - 1) + k
                src = ins[a] if gather[a] else ins[a].at[pid]
                sends.append(pltpu.make_async_remote_copy(
                    src_ref=src, dst_ref=outs[a].at[me], send_sem=send_sems.at[sem], recv_sem=recv_sems.at[sem],
                    device_id=peer, device_id_type=MESH))
                recvs.append(pltpu.make_async_remote_copy(
                    src_ref=src, dst_ref=outs[a].at[pid], send_sem=send_sems.at[sem], recv_sem=recv_sems.at[sem],
                    device_id=peer, device_id_type=MESH))
        for cp in local + sends:
            cp.start()
        for cp in recvs:
            cp.wait_recv()
        for cp in sends:
            cp.wait_send()
        for cp in local:
            cp.wait()

    out_shape = [SDS((N_DEV,) + a.shape if gt else a.shape, a.dtype) for a, gt in zip(arrays, gather)]
    nsem = n * (N_DEV - 1)
    body, dep_specs, deps = _after(body, n, after)
    return pl.pallas_call(
        body, name=name,
        in_specs=[pl.BlockSpec(memory_space=pl.ANY)] * n + dep_specs,
        out_specs=[pl.BlockSpec(memory_space=pl.ANY)] * n,
        out_shape=out_shape,
        scratch_shapes=[pltpu.SemaphoreType.DMA((nsem,)), pltpu.SemaphoreType.DMA((nsem,)), pltpu.SemaphoreType.DMA((n,))],
    )(*arrays, *deps)


_HBM = pl.BlockSpec(memory_space=pltpu.HBM)
_SEM = pl.BlockSpec(memory_space=pltpu.SEMAPHORE)
_EFFECT = pltpu.SideEffectType.DATAFLOW_SIDE_EFFECTING
GATHER = "gather"
SCATTER = "scatter"
SPREAD = "spread"


def _zone_shape(a, mode):
    if mode == GATHER:
        return (N_DEV,) + a.shape
    return (N_DEV - 1,) + (a.shape[1:] if mode == SCATTER else a.shape)


def _start_copies(arrays, modes, name, after=None):
    n = len(arrays)
    zones = [lax.empty(_zone_shape(a, m), a.dtype) for a, m in zip(arrays, modes)]

    def body(*refs):
        ins, lands = refs[:n], refs[n:2 * n]
        send_sems, recv_sems = refs[-2 * n - 3], refs[-2 * n - 2]
        token = refs[-1]
        x, y, c, me = _place()
        for a in range(n):
            for k, (peer, pid) in enumerate(_peers(x, y, c)):
                src = ins[a].at[pid] if modes[a] == SCATTER else ins[a]
                dst = lands[a].at[me] if modes[a] == GATHER else lands[a].at[k]
                pltpu.make_async_remote_copy(src_ref=src, dst_ref=dst, send_sem=send_sems.at[a], recv_sem=recv_sems.at[a],
                                             device_id=peer, device_id_type=MESH).start()
            if modes[a] == GATHER:
                pltpu.make_async_remote_copy(src_ref=ins[a], dst_ref=lands[a].at[me], send_sem=send_sems.at[a],
                                             recv_sem=recv_sems.at[a], device_id=(x, y, c), device_id_type=MESH).start()
        token[...] = jnp.zeros_like(token)

    hbm = lambda a: pltpu.HBM(a.shape, a.dtype)
    sems = pltpu.SemaphoreType.DMA((n,))
    extra = [] if after is None else [after]
    operands = [pltpu.with_memory_space_constraint(a, pltpu.HBM) for a in list(arrays) + zones]
    res = pl.pallas_call(
        body, name=name,
        out_shape=(sems, sems, *[hbm(a) for a in arrays], *[hbm(z) for z in zones], SDS((8, 128), F32)),
        in_specs=[_HBM] * (2 * n) + [_ANY] * len(extra),
        out_specs=(_SEM, _SEM, *[_HBM] * (2 * n), pl.BlockSpec(memory_space=pltpu.VMEM)),
        input_output_aliases={i: 2 + i for i in range(2 * n)},
        compiler_params=pltpu.CompilerParams(has_side_effects=_EFFECT),
    )(*operands, *extra)
    return res[0], res[1], list(res[2:2 + n]), list(res[2 + n:2 + 2 * n]), res[-1]


def _wait_copies(started, after, name, count=N_DEV - 1):
    send_sems, recv_sems, thru, zones, _ = started
    nt, nz = len(thru), len(zones)

    def body(*refs):
        lands = refs[nt:nt + nz]
        send_ref, recv_ref = refs[nt + nz], refs[nt + nz + 1]
        x, y, c, _ = _place()
        for a in range(nz):
            blocks = lands[a].at[pl.ds(0, count)]
            cp = pltpu.make_async_remote_copy(src_ref=blocks, dst_ref=blocks, send_sem=send_ref.at[a], recv_sem=recv_ref.at[a],
                                              device_id=(x, y, 1 - c), device_id_type=MESH)
            cp.wait_send()
            cp.wait_recv()

    hbm = lambda a: pltpu.HBM(a.shape, a.dtype)
    res = pl.pallas_call(
        body, name=name,
        out_shape=tuple(hbm(a) for a in thru + zones),
        in_specs=[_HBM] * (nt + nz) + [_SEM, _SEM, _ANY],
        out_specs=tuple([_HBM] * (nt + nz)),
        input_output_aliases={i: i for i in range(nt + nz)},
        compiler_params=pltpu.CompilerParams(has_side_effects=_EFFECT),
    )(*thru, *zones, send_sems, recv_sems, after)
    return list(res[:nt]), list(res[nt:])


def _split_start(body, arrays, zones, name, after):
    n = len(arrays) + len(zones)
    hbm = lambda a: pltpu.HBM(a.shape, a.dtype)
    sems = pltpu.SemaphoreType.DMA((max(len(zones), 1),))
    extra = [] if after is None else [after]
    operands = [pltpu.with_memory_space_constraint(a, pltpu.HBM) for a in list(arrays) + list(zones)]
    res = pl.pallas_call(
        body, name=name,
        out_shape=(sems, sems, *[hbm(a) for a in operands], SDS((8, 128), F32)),
        in_specs=[_HBM] * n + [_ANY] * len(extra),
        out_specs=(_SEM, _SEM, *[_HBM] * n, pl.BlockSpec(memory_space=pltpu.VMEM)),
        input_output_aliases={i: 2 + i for i in range(n)},
        compiler_params=pltpu.CompilerParams(has_side_effects=_EFFECT),
    )(*operands, *extra)
    return res[0], res[1], list(res[2:2 + len(arrays)]), list(res[2 + len(arrays):2 + n]), res[-1]


def _gather_first_leg(shard, name, after=None):
    zone = lax.empty((N_DEV,) + shard.shape, shard.dtype)
    extra = 0 if after is None else 1

    def body(*refs):
        src, land = refs[0], refs[1]
        send_sem, recv_sem, token = refs[2 + extra], refs[3 + extra], refs[-1]
        x, y, c, me = _place()
        for peer in ((x, y, c), (x, y, 1 - c), (1 - x, y, c), (x, 1 - y, c), (1 - x, 1 - y, c)):
            pltpu.make_async_remote_copy(src_ref=src, dst_ref=land.at[me], send_sem=send_sem.at[0], recv_sem=recv_sem.at[0],
                                         device_id=peer, device_id_type=MESH).start()
        token[...] = jnp.zeros_like(token)

    return _split_start(body, [shard], [zone], name, after)


def _gather_second_leg(zone, name, after=None):
    extra = 0 if after is None else 1

    def body(*refs):
        land = refs[0]
        send_sem, recv_sem, token = refs[1 + extra], refs[2 + extra], refs[-1]
        x, y, c, _ = _place()
        for px, py in ((1 - x, y), (x, 1 - y), (1 - x, 1 - y)):
            slot = 4 * px + 2 * py + c
            pltpu.make_async_remote_copy(src_ref=land.at[slot], dst_ref=land.at[slot], send_sem=send_sem.at[0],
                                         recv_sem=recv_sem.at[0], device_id=(x, y, 1 - c), device_id_type=MESH).start()
        token[...] = jnp.zeros_like(token)

    return _split_start(body, [], [zone], name, after)


UPDATE_BLOCK_ELEMS = 256 * 1024


def _update_rows(R, C):
    fits = [t for t in range(8, R + 1, 8) if R % t == 0 and t * C <= UPDATE_BLOCK_ELEMS]
    whole = [t for t in fits if t % 16 == 0]
    return max(whole or fits)


def _adamw_math(g, w, m, v):
    m2 = ADAM_B1 * m + (1.0 - ADAM_B1) * g
    v2 = ADAM_B2 * v + (1.0 - ADAM_B2) * (g * g)
    m_hat = m2 / (1.0 - ADAM_B1 ** ADAM_STEP)
    v_hat = v2 / (1.0 - ADAM_B2 ** ADAM_STEP)
    delta = -ADAM_LR * (m_hat / (jnp.sqrt(v_hat) + ADAM_EPS) + ADAM_WD * w)
    return delta, m2, v2


def _sum_adamw(parts, w, m, v, name):
    R, C = w.shape
    tr = _update_rows(R, C)

    def body(p_ref, w_ref, m_ref, v_ref, g_ref, d_ref, m2_ref, v2_ref):
        g = p_ref[0]
        for k in range(1, N_DEV):
            g = g + p_ref[k]
        g_ref[...] = g
        d_ref[...], m2_ref[...], v2_ref[...] = _adamw_math(g, w_ref[...], m_ref[...], v_ref[...])

    blk = pl.BlockSpec((tr, C), lambda i: (i, 0))
    return pl.pallas_call(
        body, name=name, grid=(R // tr,),
        in_specs=[pl.BlockSpec((N_DEV, tr, C), lambda i: (0, i, 0)), blk, blk, blk],
        out_specs=[blk] * 4,
        out_shape=[SDS((R, C), F32)] * 4,
        compiler_params=_params(1),
    )(parts, w, m, v)


def _sum_adamw_peers(me, own, parts, w, m, v, name, replicated):
    R, C = w.shape
    tr = _update_rows(R, C)

    def body(me_ref, own_ref, p_ref, w_ref, m_ref, v_ref, g_ref, d_ref, m2_ref, v2_ref):
        if replicated:
            mine = me_ref[0]
            g = None
            for j in range(N_DEV):
                k = jnp.maximum(jnp.bitwise_xor(mine, j) - 1, 0)
                term = jnp.where(mine == j, own_ref[...], p_ref[k])
                g = term if g is None else g + term
        else:
            g = own_ref[...].astype(F32)
            for k in range(N_DEV - 1):
                g = g + p_ref[k].astype(F32)
        g_ref[...] = g
        d_ref[...], m2_ref[...], v2_ref[...] = _adamw_math(g, w_ref[...], m_ref[...], v_ref[...])

    blk = pl.BlockSpec((tr, C), lambda i, me_ref: (i, 0))
    own_spec = blk if replicated else pl.BlockSpec((None, tr, C), lambda i, me_ref: (me_ref[0], i, 0))
    return pl.pallas_call(
        body, name=name,
        grid_spec=pltpu.PrefetchScalarGridSpec(
            num_scalar_prefetch=1, grid=(R // tr,),
            in_specs=[own_spec, pl.BlockSpec((N_DEV - 1, tr, C), lambda i, me_ref: (0, i, 0)), blk, blk, blk],
            out_specs=[blk] * 4),
        out_shape=[SDS((R, C), F32)] * 4,
        compiler_params=_params(1),
    )(me, own, parts, w, m, v)


SMALL = ("ln_v_gain", "ln_v_bias", "w_spatial", "b_spatial", "sinks", "norm_mix_post", "norm_ff_pre", "norm_ff_post")
SMALL_ROWS = {"ln_v_gain": 8, "ln_v_bias": 8, "w_spatial": 1024, "b_spatial": 8, "sinks": 8,
              "norm_mix_post": 8, "norm_ff_pre": 8, "norm_ff_post": 8}
SMALL_PACK_ROWS = 1152


def _pack_small(vals):
    rows = []
    for name in SMALL:
        flat = vals[name].reshape(-1)
        pad = SMALL_ROWS[name] * 128 - flat.shape[0]
        if pad:
            flat = jnp.concatenate([flat, jnp.zeros((pad,), F32)])
        rows.append(flat.reshape(SMALL_ROWS[name], 128))
    rows.append(jnp.zeros((SMALL_PACK_ROWS - sum(SMALL_ROWS.values()), 128), F32))
    return jnp.concatenate(rows, axis=0)


def _unpack_small(packed, shapes):
    out, r = {}, 0
    for name in SMALL:
        n = 1
        for s in shapes[name]:
            n *= s
        out[name] = packed[r:r + SMALL_ROWS[name]].reshape(-1)[:n].reshape(shapes[name])
        r += SMALL_ROWS[name]
    return out


def _rope_rows():
    d = jnp.arange(128) % HEAD
    inv = ROPE_THETA ** (-(2.0 * (d % (ROPE // 2))).astype(F32) / ROPE)
    invf = jnp.where(d < ROPE, inv, 0.0).astype(F32).reshape(1, 128)
    sgn = jnp.where(d < ROPE // 2, -1.0, jnp.where(d < ROPE, 1.0, 0.0)).astype(F32).reshape(1, 128)
    return invf, sgn


def kernel(x, positions, w_in, ln_v_gain, ln_v_bias, w_spatial, b_spatial, sinks, w_a, w_b, w_o, norm_mix_pre, norm_mix_post, w_ff_in, w_ff_out, norm_ff_pre, norm_ff_post, loss_target, m_w_in, m_ln_v_gain, m_ln_v_bias, m_w_spatial, m_b_spatial, m_sinks, m_w_a, m_w_b, m_w_o, m_norm_mix_pre, m_norm_mix_post, m_w_ff_in, m_w_ff_out, m_norm_ff_pre, m_norm_ff_post, v_w_in, v_ln_v_gain, v_ln_v_bias, v_w_spatial, v_b_spatial, v_sinks, v_w_a, v_w_b, v_w_o, v_norm_mix_pre, v_norm_mix_post, v_w_ff_in, v_w_ff_out, v_norm_ff_pre, v_norm_ff_post):
    given = dict(locals())
    T = x.shape[1]
    xt = x[0]
    tgt = loss_target[0]
    bst = b_spatial[0].T
    ws = w_spatial[0]

    me = 4 * lax.axis_index("x") + 2 * lax.axis_index("y") + lax.axis_index("c")
    me_arr = me.astype(jnp.int32).reshape(1)

    rest = ("w_a", "w_b", "w_o", "w_ff_in", "w_ff_out")
    shard = {n: given[n][0].astype(BF16) for n in rest}
    g_one = _gather_first_leg(w_in[0].T.astype(BF16), "gather_in_start")
    cos, sin = _rope_tables(positions.astype(F32).reshape(T, 1), *_rope_rows(), after=g_one[-1])
    small_state = [_pack_small({n: given[k + n] for n in SMALL}) for k in ("", "m_", "v_")]
    h = _rms_pre(xt, norm_mix_pre, after=[cos, *small_state, *[shard[n] for n in rest]])
    _, (win8,) = _wait_copies(g_one, h, "gather_in_wait", count=5)
    g_two = _gather_second_leg(win8, "gather_in_pass_start")
    g_rest = _start_copies([shard[n] for n in rest], [GATHER] * len(rest), "gather_rest_start", after=g_two[-1])
    _, (win8,) = _wait_copies(g_two, g_rest[-1], "gather_in_pass_wait", count=3)
    win = win8.reshape(IN_W, D)

    proj = _fwd_in(h, win)
    att, qr, kr, probs, psink = _fwd_attn(proj, cos, sin, sinks[0])
    a = _fwd_sgu(proj, ln_v_gain, ln_v_bias, ws, bst, after=att)
    gw = dict(zip(rest, _wait_copies(g_rest, a, "gather_rest_wait", count=N_DEV)[1]))
    wa, wb, wo = (gw[n].reshape(D, D) for n in ("w_a", "w_b", "w_o"))
    wfi3 = gw["w_ff_in"]
    wfo = gw["w_ff_out"].reshape(D_FF, D)
    merged, a2, b2, mix, x1, hf = _fwd_mix(a, att, proj, xt, wa, wb, wo, norm_mix_post, norm_ff_pre)
    f, dy, dff, dg3, loss_part = _fwd_ff(hf, wfi3, wfo, x1, tgt, norm_ff_post)

    df, dx1, dmix, dg2, dg1 = _bwd_ff(dff, f, wfi3, wfo, x1, dy, mix, norm_mix_post, norm_ff_pre)
    dwfi3, dwfo = _wgrad_ff(hf, df, f, dff)
    own_ff = [dwfi3, dwfo.reshape(N_DEV, D_FF // N_DEV, D)]
    x_ff = _start_copies(own_ff, [SCATTER] * 2, "exchange_ff_start")
    da2, db2, dgate, da, datt = _bwd_mix(dmix, proj, a2, b2, wo, wa, wb, after=x_ff[-1])
    dwo, dwa, dwb = _wgrad_mix(merged, dmix, a, da2, att, db2)
    own_mix = [g.reshape(N_DEV, D // N_DEV, D) for g in (dwa, dwb, dwo)]
    x_mix = _start_copies(own_mix, [SCATTER] * 3, "exchange_mix_start")
    dq, dkv, dsink = _bwd_attn(qr, kr, probs, psink, proj, cos, sin, datt, after=x_mix[-1])
    duv, dws, dbs, dlng, dlnb = _bwd_sgu(proj, da, ln_v_gain, ln_v_bias, ws, bst)
    small_grads = {"ln_v_gain": dlng, "ln_v_bias": dlnb, "w_spatial": dws, "b_spatial": dbs, "sinks": dsink[:, :N_Q],
                   "norm_mix_post": dg1, "norm_ff_pre": dg2, "norm_ff_post": dg3}
    x_small = _start_copies([_pack_small(small_grads)], [SPREAD], "exchange_small_start")
    dwin = _wgrad_in(h, duv, dq, dkv, dgate)
    own_in = [dwin.reshape(N_DEV, IN_W // N_DEV, D)]
    x_in = _start_copies(own_in, [SCATTER], "exchange_in_start", after=x_small[-1])
    grad_x, dg0 = _bwd_in(duv, dq, dkv, dgate, win, xt, dx1, norm_mix_pre, after=x_in[-1])

    results = {}

    def update(n, own, parts, transposed=False):
        state = [given[k + n][0].T if transposed else given[k + n][0] for k in ("", "m_", "v_")]
        res = _sum_adamw_peers(me_arr, own, parts, *state, "adamw_" + n, False)
        results[n] = [(r.T if transposed else r).reshape(given[n].shape) for r in res]

    own_ff, p_ff = _wait_copies(x_ff, grad_x, "exchange_ff_wait")
    update("w_ff_in", own_ff[0], p_ff[0])
    update("w_ff_out", own_ff[1], p_ff[1])
    own_mix, p_mix = _wait_copies(x_mix, results["w_ff_out"][0], "exchange_mix_wait")
    for n, own, parts in zip(("w_a", "w_b", "w_o"), own_mix, p_mix):
        update(n, own, parts)
    tail = jnp.concatenate([dg0.reshape(8, 128), jnp.tile(loss_part, (8, 1))], axis=0)
    (tail_all,) = _all_to_all([tail], [True], "exchange_tail", after=results["w_o"][0])
    dg0_all = tail_all[:, :8]
    own_small, p_small = _wait_copies(x_small, tail_all, "exchange_small_wait")
    own_in, p_in = _wait_copies(x_in, p_small[0], "exchange_in_wait")
    update("w_in", own_in[0], p_in[0], transposed=True)
    packed = _sum_adamw_peers(me_arr, own_small[0], p_small[0], *small_state, "adamw_small", True)
    shapes = {n: given[n].shape for n in SMALL}
    unpacked = [_unpack_small(p, shapes) for p in packed]
    for n in SMALL:
        results[n] = [u[n] for u in unpacked]
    n = "norm_mix_pre"
    results[n] = [r.reshape(given[n].shape) for r in _sum_adamw(
        dg0_all, given[n].reshape(8, 128), given["m_" + n].reshape(8, 128), given["v_" + n].reshape(8, 128), "adamw_" + n)]

    loss = jnp.sum(tail_all[:, 8, 0])
    order = ("w_in", "ln_v_gain", "ln_v_bias", "w_spatial", "b_spatial", "sinks", "w_a", "w_b", "w_o", "norm_mix_pre",
             "norm_mix_post", "w_ff_in", "w_ff_out", "norm_ff_pre", "norm_ff_post")
    out = [loss, grad_x.reshape(x.shape)]
    for k in range(4):
        out += [results[n][k] for n in order]
    return tuple(out)
```

```python
import jax
import jax.numpy as jnp
from jax import lax
from jax.experimental import pallas as pl
from jax.experimental.pallas import tpu as pltpu

F32 = jnp.float32
BF16 = jnp.bfloat16

N_DEV = 8
D = 1024
D_FF = 4096
IN_W = 5632
CHUNK = 128
GROUPS = 8
HEAD = 64
N_Q = 16
N_KV = 4
ROPE = 16
ROPE_THETA = 500000.0
EPS = 1e-6
OFF_Q, OFF_K, OFF_VA, OFF_GA, OFF_GB = 2048, 3072, 3328, 3584, 4608

ADAM_LR = 0.001
ADAM_B1 = 0.9
ADAM_B2 = 0.999
ADAM_EPS = 1e-08
ADAM_WD = 0.01
ADAM_STEP = 10

VMEM_LIMIT = 62 * 1024 * 1024

SDS = jax.ShapeDtypeStruct
MESH = pl.DeviceIdType.MESH


def _params(n_axes):
    return pltpu.CompilerParams(dimension_semantics=("arbitrary",) * n_axes, vmem_limit_bytes=VMEM_LIMIT)


def _nt(a, b):
    return lax.dot_general(a, b, (((1,), (1,)), ((), ())), preferred_element_type=F32)


def _tn(a, b):
    return lax.dot_general(a, b, (((0,), (0,)), ((), ())), preferred_element_type=F32)


def _nn(a, b):
    return jnp.dot(a, b, preferred_element_type=F32)


def _gelu(x):
    t = jnp.tanh(0.7978845608028654 * (x + 0.044715 * (x * x * x)))
    return 0.5 * x * (1.0 + t), t


def _gelu_grad(x, t):
    return 0.5 * (1.0 + t) + 0.5 * x * (1.0 - t * t) * (0.7978845608028654 * (1.0 + 3.0 * 0.044715 * x * x))


def _sigmoid(x):
    return 1.0 / (1.0 + jnp.exp(-x))


def _rms_stats(v):
    r = lax.rsqrt(jnp.mean(v * v, axis=-1, keepdims=True) + EPS)
    return r, v * r


def _rms_bwd(d, vhat, r, g):
    gd = g * d
    return r * (gd - vhat * jnp.mean(gd * vhat, axis=-1, keepdims=True))


def _colsum(v):
    return jnp.sum(v, axis=0, keepdims=True)


_ANY = pl.BlockSpec(memory_space=pl.ANY)


def _after(body, n_in, after):
    if after is None:
        return body, [], []
    deps = list(after) if isinstance(after, (list, tuple)) else [after]

    def ordered(*refs):
        return body(*refs[:n_in], *refs[n_in + len(deps):])

    return ordered, [_ANY] * len(deps), deps


def _rms_pre(x, g0, after=None):
    T = x.shape[0]
    tm = min(T, 1024)

    def body(x_ref, g_ref, h_ref):
        _, xh = _rms_stats(x_ref[...])
        h_ref[...] = (xh * g_ref[...]).astype(BF16)

    body, dep_specs, deps = _after(body, 2, after)
    return pl.pallas_call(
        body, name="rms_pre", grid=(T // tm,),
        in_specs=[pl.BlockSpec((tm, D), lambda i: (i, 0)), pl.BlockSpec((1, D), lambda i: (0, 0))] + dep_specs,
        out_specs=pl.BlockSpec((tm, D), lambda i: (i, 0)),
        out_shape=SDS((T, D), BF16),
        compiler_params=_params(1),
    )(x, g0, *deps)


def _fwd_in(h, win_t):
    T = h.shape[0]
    tm, tn = min(T, 512), 1408

    def body(h_ref, w_ref, p_ref):
        for j in range(IN_W // tn):
            cols = slice(j * tn, (j + 1) * tn)
            p_ref[:, cols] = _nt(h_ref[...], w_ref[cols, :]).astype(BF16)

    return pl.pallas_call(
        body, name="fwd_in", grid=(T // tm,),
        in_specs=[pl.BlockSpec((tm, D), lambda i: (i, 0)), _resident((IN_W, D))],
        out_specs=pl.BlockSpec((tm, IN_W), lambda i: (i, 0)),
        out_shape=SDS((T, IN_W), BF16),
        compiler_params=_params(1),
    )(h, win_t)


def _sgu_forward_parts(u_ref, vs_ref, lng_ref, lnb_ref):
    u = u_ref[...].astype(F32)
    vs = vs_ref[...].astype(F32)
    gu, tu = _gelu(u)
    gv, tv = _gelu(vs)
    mu = jnp.mean(gv, axis=-1, keepdims=True)
    dv = gv - mu
    rstd = lax.rsqrt(jnp.mean(dv * dv, axis=-1, keepdims=True) + EPS)
    vhat = dv * rstd
    vn = (vhat * lng_ref[...] + lnb_ref[...]).astype(BF16)
    return u, vs, gu, tu, tv, rstd, vhat, vn


def _masked_ws(ws_ref, g):
    row = lax.broadcasted_iota(jnp.int32, (CHUNK, CHUNK), 0)
    col = lax.broadcasted_iota(jnp.int32, (CHUNK, CHUNK), 1)
    return jnp.where(row >= col, ws_ref[g], 0.0).astype(BF16)


def _fwd_sgu(proj, lng, lnb, ws, bst, after=None):
    T = proj.shape[0]
    tc = min(T, 512)

    def body(u_ref, vs_ref, lng_ref, lnb_ref, ws_ref, bst_ref, a_ref):
        _, _, gu, _, _, _, _, vn = _sgu_forward_parts(u_ref, vs_ref, lng_ref, lnb_ref)
        for g in range(GROUPS):
            wm = _masked_ws(ws_ref, g)
            cols = slice(g * CHUNK, (g + 1) * CHUNK)
            for c in range(tc // CHUNK):
                rows = slice(c * CHUNK, (c + 1) * CHUNK)
                mixed = _nn(wm, vn[rows, cols]) + bst_ref[:, g:g + 1]
                a_ref[rows, cols] = (gu[rows, cols] * mixed).astype(BF16)

    body, dep_specs, deps = _after(body, 6, after)
    return pl.pallas_call(
        body, name="fwd_sgu", grid=(T // tc,),
        in_specs=[pl.BlockSpec((tc, D), lambda i: (i, 0)), pl.BlockSpec((tc, D), lambda i: (i, 1)),
                  pl.BlockSpec((1, D), lambda i: (0, 0)), pl.BlockSpec((1, D), lambda i: (0, 0)),
                  pl.BlockSpec((GROUPS, CHUNK, CHUNK), lambda i: (0, 0, 0)),
                  pl.BlockSpec((CHUNK, GROUPS), lambda i: (0, 0))] + dep_specs,
        out_specs=pl.BlockSpec((tc, D), lambda i: (i, 0)),
        out_shape=SDS((T, D), BF16),
        compiler_params=_params(1),
    )(proj, proj, lng, lnb, ws, bst, *deps)


def _rope_tables(posf, invf, sgn, after=None):
    T = posf.shape[0]
    tr = min(T, 1024)

    def body(pos_ref, invf_ref, sgn_ref, c_ref, s_ref):
        ang = pos_ref[...] * invf_ref[...]
        c_ref[...] = jnp.cos(ang)
        s = jnp.sin(ang)
        s_ref[:, :128] = jnp.where(sgn_ref[...] < 0.0, -s, 0.0)
        s_ref[:, 128:] = jnp.where(sgn_ref[...] > 0.0, s, 0.0)

    body, dep_specs, deps = _after(body, 3, after)
    return pl.pallas_call(
        body, name="rope_tables", grid=(T // tr,),
        in_specs=[pl.BlockSpec((tr, 1), lambda i: (i, 0)), pl.BlockSpec((1, 128), lambda i: (0, 0)),
                  pl.BlockSpec((1, 128), lambda i: (0, 0))] + dep_specs,
        out_specs=[pl.BlockSpec((tr, 128), lambda i: (i, 0)), pl.BlockSpec((tr, 256), lambda i: (i, 0))],
        out_shape=[SDS((T, 128), F32), SDS((T, 256), F32)],
        compiler_params=_params(1),
    )(posf, invf, sgn, *deps)


def _rope(v, c, s):
    v = v.astype(F32)
    return v * c + pltpu.roll(v, 128 - ROPE // 2, 1) * s[:, :128] + pltpu.roll(v, ROPE // 2, 1) * s[:, 128:]


def _rope_bwd(dv, c, s):
    return dv * c + pltpu.roll(dv * s[:, :128], ROPE // 2, 1) + pltpu.roll(dv * s[:, 128:], 128 - ROPE // 2, 1)


def _fold_masks(first):
    jj = lax.broadcasted_iota(jnp.int32, (CHUNK, CHUNK), 0)
    t = lax.broadcasted_iota(jnp.int32, (CHUNK, CHUNK), 1)
    prev = jj > t
    return prev, jnp.where(prev & first, -1e30, 0.0)


def _fold(band, prev):
    return jnp.where(prev, band[:CHUNK], band[CHUNK:])


def _unfold(folded, prev):
    return jnp.concatenate([jnp.where(prev, folded, 0.0), jnp.where(prev, 0.0, folded)], axis=0)


def _softmax_sink(s, sink, key_axis):
    m = jnp.maximum(jnp.max(s, axis=key_axis, keepdims=True), sink)
    p = jnp.exp(s - m)
    esink = jnp.exp(sink - m)
    inv = 1.0 / (jnp.sum(p, axis=key_axis, keepdims=True) + esink)
    return p * inv, esink * inv


def _head_pair_operand(slab, g):
    lo = lax.broadcasted_iota(jnp.int32, slab.shape, 1) < HEAD
    if g % 2 == 0:
        first = jnp.where(lo, slab, 0.0)
        second = pltpu.roll(first, HEAD, 1)
    else:
        second = jnp.where(lo, 0.0, slab)
        first = pltpu.roll(second, HEAD, 1)
    return jnp.concatenate([first, second], axis=0).astype(BF16)


def _head_pair_gradient(acc, g):
    top, bot = acc[:2 * CHUNK], acc[2 * CHUNK:]
    lo = lax.broadcasted_iota(jnp.int32, top.shape, 1) < HEAD
    if g % 2 == 0:
        return jnp.where(lo, top, 0.0) + pltpu.roll(jnp.where(lo, 0.0, bot), HEAD, 1)
    return pltpu.roll(jnp.where(lo, top, 0.0), HEAD, 1) + jnp.where(lo, 0.0, bot)


PAIRS_PER_KV = N_Q // N_KV // 2
KV_W = N_KV * HEAD


def _band(prev_ref, cur_ref, cols=slice(None)):
    return jnp.concatenate([prev_ref[:, cols], cur_ref[:, cols]], axis=0)


def _fwd_attn(proj, cos, sin, sinks):
    T = proj.shape[0]
    nb = T // CHUNK
    cur = lambda i: i
    prev = lambda i: jnp.maximum(i - 1, 0)

    def body(q_ref, kp_ref, kc_ref, vp_ref, vc_ref, cp_ref, cc_ref, sp_ref, sc_ref, sink_ref,
             o_ref, qr_ref, kr_ref, p_ref, psink_ref):
        prev_slot, bias = _fold_masks(pl.program_id(0) == 0)
        c_band, s_band = _band(cp_ref, cc_ref), _band(sp_ref, sc_ref)
        for j in range(KV_W // 128):
            cols = slice(j * 128, (j + 1) * 128)
            k_slab = _rope(_band(kp_ref, kc_ref, cols), c_band, s_band)
            kr_ref[:, cols] = k_slab[CHUNK:].astype(BF16)
            v_slab = _band(vp_ref, vc_ref, cols).astype(F32)
            for g in (2 * j, 2 * j + 1):
                k2 = _head_pair_operand(k_slab, g)
                v2 = _head_pair_operand(v_slab, g)
                pairs = [g * PAIRS_PER_KV + r for r in range(PAIRS_PER_KV)]
                qps = []
                for pair in pairs:
                    lanes = slice(pair * 128, (pair + 1) * 128)
                    qps.append((_rope(q_ref[:, lanes], cc_ref[...], sc_ref[...]) * (HEAD ** -0.5)).astype(BF16))
                    qr_ref[:, lanes] = qps[-1]
                s2 = _nt(k2, jnp.concatenate(qps, axis=0))
                pcols = []
                for r, pair in enumerate(pairs):
                    ps = []
                    for e in range(2):
                        head = 2 * pair + e
                        s = _fold(s2[e * 2 * CHUNK:(e + 1) * 2 * CHUNK, r * 128:(r + 1) * 128], prev_slot) + bias
                        p, psink = _softmax_sink(s, sink_ref[head], 0)
                        p = p.astype(BF16)
                        p_ref[head] = p
                        psink_ref[head:head + 1, :] = psink
                        ps.append(_unfold(p, prev_slot))
                    pcols.append(jnp.concatenate(ps, axis=0))
                o = _tn(jnp.concatenate(pcols, axis=1), v2).astype(BF16)
                for r, pair in enumerate(pairs):
                    o_ref[:, pair * 128:(pair + 1) * 128] = o[r * CHUNK:(r + 1) * CHUNK]

    table = lambda which, width: pl.BlockSpec((CHUNK, width), lambda i: (which(i), 0))
    return pl.pallas_call(
        body, name="fwd_attn", grid=(nb,),
        in_specs=[pl.BlockSpec((CHUNK, D), lambda i: (i, OFF_Q // D)),
                  pl.BlockSpec((CHUNK, KV_W), lambda i: (prev(i), OFF_K // KV_W)),
                  pl.BlockSpec((CHUNK, KV_W), lambda i: (i, OFF_K // KV_W)),
                  pl.BlockSpec((CHUNK, KV_W), lambda i: (prev(i), OFF_VA // KV_W)),
                  pl.BlockSpec((CHUNK, KV_W), lambda i: (i, OFF_VA // KV_W)),
                  table(prev, 128), table(cur, 128), table(prev, 256), table(cur, 256),
                  pl.BlockSpec(memory_space=pltpu.SMEM)],
        out_specs=[pl.BlockSpec((CHUNK, D), lambda i: (i, 0)), pl.BlockSpec((CHUNK, D), lambda i: (i, 0)),
                   pl.BlockSpec((CHUNK, KV_W), lambda i: (i, 0)),
                   pl.BlockSpec((None, N_Q, CHUNK, CHUNK), lambda i: (i, 0, 0, 0)),
                   pl.BlockSpec((None, N_Q, CHUNK), lambda i: (i, 0, 0))],
        out_shape=[SDS((T, D), BF16), SDS((T, D), BF16), SDS((T, KV_W), BF16),
                   SDS((nb, N_Q, CHUNK, CHUNK), BF16), SDS((nb, N_Q, CHUNK), F32)],
        compiler_params=_params(1),
    )(proj, proj, proj, proj, proj, cos, cos, sin, sin, sinks)


def _row_halves(tm):
    return [slice(0, tm // 2), slice(tm // 2, tm)] if tm % 32 == 0 else [slice(0, tm)]


def _fwd_mix(a, att, proj, x, wa, wb, wo, g1, g2):
    T = x.shape[0]
    tm = min(T, 512)
    half = D // 2

    def body(a_ref, att_ref, ga0, ga1, gb0, gb1, x_ref, wa_ref, wb_ref, wo_ref, g1_ref, g2_ref,
             mg_ref, a2_ref, b2_ref, mix_ref, x1_ref, hf_ref):
        for rows in _row_halves(tm):
            a2 = _nn(a_ref[rows, :], wa_ref[...])
            b2 = _nn(att_ref[rows, :], wb_ref[...])
            ga = jnp.concatenate([ga0[rows, :], ga1[rows, :]], axis=1).astype(F32)
            gb = jnp.concatenate([gb0[rows, :], gb1[rows, :]], axis=1).astype(F32)
            merged = (_sigmoid(ga) * a2 + _sigmoid(gb) * b2).astype(BF16)
            a2_ref[rows, :] = a2.astype(BF16)
            b2_ref[rows, :] = b2.astype(BF16)
            mg_ref[rows, :] = merged
            mix = _nn(merged, wo_ref[...])
            mix_ref[rows, :] = mix
            _, mh = _rms_stats(mix)
            x1 = x_ref[rows, :] + mh * g1_ref[...]
            x1_ref[rows, :] = x1
            _, xh = _rms_stats(x1)
            hf_ref[rows, :] = (xh * g2_ref[...]).astype(BF16)

    row = lambda i: (i, 0)
    const = lambda i: (0, 0)
    gspec = lambda off: pl.BlockSpec((tm, half), lambda i: (i, off // half))
    return pl.pallas_call(
        body, name="fwd_mix", grid=(T // tm,),
        in_specs=[pl.BlockSpec((tm, D), row), pl.BlockSpec((tm, D), row),
                  gspec(OFF_GA), gspec(OFF_GA + half), gspec(OFF_GB), gspec(OFF_GB + half),
                  pl.BlockSpec((tm, D), row), _resident((D, D)), _resident((D, D)),
                  _resident((D, D)), pl.BlockSpec((1, D), const), pl.BlockSpec((1, D), const)],
        out_specs=[pl.BlockSpec((tm, D), row)] * 6,
        out_shape=[SDS((T, D), BF16), SDS((T, D), BF16), SDS((T, D), BF16), SDS((T, D), F32), SDS((T, D), F32),
                   SDS((T, D), BF16)],
        compiler_params=_params(1),
    )(a, att, proj, proj, proj, proj, x, wa, wb, wo, g1, g2)


FF_SPLIT = N_DEV
FF_TILE = D_FF // FF_SPLIT


def _fwd_ff(hf, wfi3, wfo, x1, tgt, g3):
    T = hf.shape[0]
    tm = min(T, 512)

    def body(hf_ref, wfi_ref, wfo_ref, x1_ref, tgt_ref, g3_ref, f_ref, dy_ref, dff_ref, dg3_ref, loss_ref, r_s):
        @pl.when(pl.program_id(0) == 0)
        def _():
            dg3_ref[...] = jnp.zeros_like(dg3_ref)
            loss_ref[...] = jnp.zeros_like(loss_ref)

        hf_t = hf_ref[...]
        for s in range(FF_SPLIT):
            cols = slice(s * FF_TILE, (s + 1) * FF_TILE)
            f = _nn(hf_t, wfi_ref[s]).astype(BF16)
            f_ref[:, cols] = f
            rl = jnp.maximum(f.astype(F32), 0.0)
            r_s[:, cols] = (rl * rl).astype(BF16)
        r3, fh = _rms_stats(_nn(r_s[...], wfo_ref[...]))
        e = x1_ref[...] + fh * g3_ref[...] - tgt_ref[...]
        loss_ref[...] += jnp.sum(e * e) * (0.5 / D)
        dy = e * (1.0 / D)
        dy_ref[...] = dy
        dg3_ref[...] += _colsum(dy * fh)
        dff_ref[...] = _rms_bwd(dy, fh, r3, g3_ref[...]).astype(BF16)

    row = lambda i: (i, 0)
    const = lambda i: (0, 0)
    return pl.pallas_call(
        body, name="fwd_ff", grid=(T // tm,),
        in_specs=[pl.BlockSpec((tm, D), row), _resident((FF_SPLIT, D, FF_TILE)), _resident((D_FF, D)),
                  pl.BlockSpec((tm, D), row),
                  pl.BlockSpec((tm, D), row), pl.BlockSpec((1, D), const)],
        out_specs=[pl.BlockSpec((tm, D_FF), row), pl.BlockSpec((tm, D), row),
                   pl.BlockSpec((tm, D), row), pl.BlockSpec((1, D), const), pl.BlockSpec((1, 128), const)],
        out_shape=[SDS((T, D_FF), BF16), SDS((T, D), F32), SDS((T, D), BF16), SDS((1, D), F32), SDS((1, 128), F32)],
        scratch_shapes=[pltpu.VMEM((tm, D_FF), BF16)],
        compiler_params=_params(1),
    )(hf, wfi3, wfo, x1, tgt, g3)


def _bwd_ff(dff, f, wfi3, wfo, x1, dy, mix, g1, g2):
    T = dff.shape[0]
    tm = min(T, 512)

    def body(dff_ref, f_ref, wfi_ref, wfo_ref, x1_ref, dy_ref, mix_ref, g1_ref, g2_ref,
             df_ref, dx1_ref, dmix_ref, dg2_ref, dg1_ref):
        @pl.when(pl.program_id(0) == 0)
        def _():
            dg2_ref[...] = jnp.zeros_like(dg2_ref)
            dg1_ref[...] = jnp.zeros_like(dg1_ref)

        dff_t = dff_ref[...]
        dhf = None
        for s in range(FF_SPLIT):
            cols = slice(s * FF_TILE, (s + 1) * FF_TILE)
            dr = _nt(dff_t, wfo_ref[cols, :])
            df = (dr * (2.0 * jnp.maximum(f_ref[:, cols].astype(F32), 0.0))).astype(BF16)
            df_ref[:, cols] = df
            part = _nt(df, wfi_ref[s])
            dhf = part if dhf is None else dhf + part
        r2, xh = _rms_stats(x1_ref[...])
        dg2_ref[...] += _colsum(dhf * xh)
        dx1 = dy_ref[...] + _rms_bwd(dhf, xh, r2, g2_ref[...])
        dx1_ref[...] = dx1
        r1, mh = _rms_stats(mix_ref[...])
        dg1_ref[...] += _colsum(dx1 * mh)
        dmix_ref[...] = _rms_bwd(dx1, mh, r1, g1_ref[...]).astype(BF16)

    row = lambda i: (i, 0)
    const = lambda i: (0, 0)
    return pl.pallas_call(
        body, name="bwd_ff", grid=(T // tm,),
        in_specs=[pl.BlockSpec((tm, D), row), pl.BlockSpec((tm, D_FF), row),
                  _resident((FF_SPLIT, D, FF_TILE)), _resident((D_FF, D)),
                  pl.BlockSpec((tm, D), row), pl.BlockSpec((tm, D), row), pl.BlockSpec((tm, D), row),
                  pl.BlockSpec((1, D), const), pl.BlockSpec((1, D), const)],
        out_specs=[pl.BlockSpec((tm, D_FF), row), pl.BlockSpec((tm, D), row),
                   pl.BlockSpec((tm, D), row), pl.BlockSpec((1, D), const), pl.BlockSpec((1, D), const)],
        out_shape=[SDS((T, D_FF), BF16), SDS((T, D), F32), SDS((T, D), BF16), SDS((1, D), F32), SDS((1, D), F32)],
        compiler_params=_params(1),
    )(dff, f, wfi3, wfo, x1, dy, mix, g1, g2)


def _wgrad_ff(hf, df, f, dff):
    T = hf.shape[0]
    tt = min(T, 2048)
    slabs = 2
    wide = slabs * FF_TILE

    def body(hf_ref, df_ref, f_ref, dff_ref, dwfi_ref, dwfo_ref, acc_i, acc_o):
        t = pl.program_id(1)

        @pl.when(t == 0)
        def _():
            acc_i[...] = jnp.zeros_like(acc_i)
            acc_o[...] = jnp.zeros_like(acc_o)

        acc_i[...] += _tn(hf_ref[...], df_ref[...])
        rl = jnp.maximum(f_ref[...].astype(F32), 0.0)
        acc_o[...] += _tn((rl * rl).astype(BF16), dff_ref[...])

        @pl.when(t == T // tt - 1)
        def _():
            for s in range(slabs):
                dwfi_ref[s] = acc_i[:, s * FF_TILE:(s + 1) * FF_TILE].astype(BF16)
            dwfo_ref[...] = acc_o[...].astype(BF16)

    return pl.pallas_call(
        body, name="wgrad_ff", grid=(D_FF // wide, T // tt),
        in_specs=[pl.BlockSpec((tt, D), lambda p, t: (t, 0)), pl.BlockSpec((tt, wide), lambda p, t: (t, p)),
                  pl.BlockSpec((tt, wide), lambda p, t: (t, p)), pl.BlockSpec((tt, D), lambda p, t: (t, 0))],
        out_specs=[pl.BlockSpec((slabs, D, FF_TILE), lambda p, t: (p, 0, 0)), pl.BlockSpec((wide, D), lambda p, t: (p, 0))],
        out_shape=[SDS((FF_SPLIT, D, FF_TILE), BF16), SDS((D_FF, D), BF16)],
        scratch_shapes=[pltpu.VMEM((D, wide), F32), pltpu.VMEM((wide, D), F32)],
        compiler_params=_params(2),
    )(hf, df, f, dff)


def _bwd_mix(dmix, proj, a2, b2, merged, a, att, wo, wa, wb, after=None):
    T = dmix.shape[0]
    tm = min(T, 512)
    half = D // 2
    last = T // tm - 1

    def body(dmix_ref, ga0, ga1, gb0, gb1, a2_ref, b2_ref, mg_ref, a_ref, att_ref, wo_ref, wa_ref, wb_ref,
             dg_ref, da_ref, datt_ref, dwo_ref, dwa_ref, dwb_ref, acc, stage, sem):
        t = pl.program_id(0)

        @pl.when(t == 0)
        def _():
            acc[...] = jnp.zeros_like(acc)

        dmix_t = dmix_ref[...]
        dmg = _nt(dmix_t, wo_ref[...])
        sa = _sigmoid(jnp.concatenate([ga0[...], ga1[...]], axis=1).astype(F32))
        sb = _sigmoid(jnp.concatenate([gb0[...], gb1[...]], axis=1).astype(F32))
        da2 = (dmg * sa).astype(BF16)
        db2 = (dmg * sb).astype(BF16)
        dg_ref[:, :D] = (dmg * a2_ref[...].astype(F32) * (sa * (1.0 - sa))).astype(BF16)
        dg_ref[:, D:] = (dmg * b2_ref[...].astype(F32) * (sb * (1.0 - sb))).astype(BF16)
        da_ref[...] = _nt(da2, wa_ref[...]).astype(BF16)
        datt_ref[...] = _nt(db2, wb_ref[...]).astype(BF16)
        acc[0] += _tn(mg_ref[...], dmix_t)
        acc[1] += _tn(a_ref[...], da2)
        acc[2] += _tn(att_ref[...], db2)

        @pl.when(t == last)
        def _():
            for k, dw_ref in enumerate((dwo_ref, dwa_ref, dwb_ref)):
                stage[...] = acc[k].astype(BF16)
                out = pltpu.make_async_copy(stage, dw_ref, sem)
                out.start()
                out.wait()

    row = lambda i: (i, 0)
    gspec = lambda off: pl.BlockSpec((tm, half), lambda i: (i, off // half))
    body, dep_specs, deps = _after(body, 13, after)
    return pl.pallas_call(
        body, name="bwd_mix", grid=(T // tm,),
        in_specs=[pl.BlockSpec((tm, D), row), gspec(OFF_GA), gspec(OFF_GA + half), gspec(OFF_GB), gspec(OFF_GB + half)]
        + [pl.BlockSpec((tm, D), row)] * 5 + [_resident((D, D))] * 3 + dep_specs,
        out_specs=[pl.BlockSpec((tm, 2 * D), row), pl.BlockSpec((tm, D), row), pl.BlockSpec((tm, D), row)] + [_ANY] * 3,
        out_shape=[SDS((T, 2 * D), BF16), SDS((T, D), BF16), SDS((T, D), BF16)] + [SDS((D, D), BF16)] * 3,
        scratch_shapes=[pltpu.VMEM((3, D, D), F32), pltpu.VMEM((D, D), BF16), pltpu.SemaphoreType.DMA],
        compiler_params=_params(1),
    )(dmix, proj, proj, proj, proj, a2, b2, merged, a, att, wo, wa, wb, *deps)


def _bwd_attn(qr, kr, probs, psink, proj, cos, sin, datt, after=None):
    T = proj.shape[0]
    nb = T // CHUNK
    cur = lambda i: jnp.minimum(i, nb - 1)
    prev = lambda i: jnp.maximum(jnp.minimum(i, nb - 1) - 1, 0)

    def body(q_ref, kp_ref, kc_ref, vp_ref, vc_ref, cp_ref, cc_ref, sp_ref, sc_ref, p_ref, psink_ref, do_ref,
             dq_ref, dkv_ref, dsink_ref, carry_k, carry_v):
        i = pl.program_id(0)

        @pl.when(i == 0)
        def _():
            carry_k[...] = jnp.zeros_like(carry_k)
            carry_v[...] = jnp.zeros_like(carry_v)
            dsink_ref[...] = jnp.zeros_like(dsink_ref)

        @pl.when(i < nb)
        def _():
            prev_slot, _ = _fold_masks(i == 0)
            c_band, s_band = _band(cp_ref, cc_ref), _band(sp_ref, sc_ref)
            lane = lax.broadcasted_iota(jnp.int32, (1, 128), 1)
            dsink = jnp.zeros((1, 128), F32)
            for j in range(KV_W // 128):
                cols = slice(j * 128, (j + 1) * 128)
                k_slab = _band(kp_ref, kc_ref, cols).astype(F32)
                v_slab = _band(vp_ref, vc_ref, cols).astype(F32)
                dk_slab = jnp.zeros((2 * CHUNK, 128), F32)
                dv_slab = jnp.zeros((2 * CHUNK, 128), F32)
                for g in (2 * j, 2 * j + 1):
                    k2 = _head_pair_operand(k_slab, g)
                    v2 = _head_pair_operand(v_slab, g)
                    pairs = [g * PAIRS_PER_KV + r for r in range(PAIRS_PER_KV)]
                    q_stack = jnp.concatenate([q_ref[:, pr * 128:(pr + 1) * 128] for pr in pairs], axis=0)
                    do_stack = jnp.concatenate([do_ref[:, pr * 128:(pr + 1) * 128] for pr in pairs], axis=0)
                    dp2 = _nt(v2, do_stack)
                    pcols, dscols = [], []
                    for r, pair in enumerate(pairs):
                        ps, dss = [], []
                        for e in range(2):
                            head = 2 * pair + e
                            p_b = p_ref[head]
                            p = p_b.astype(F32)
                            dp = _fold(dp2[e * 2 * CHUNK:(e + 1) * 2 * CHUNK, r * 128:(r + 1) * 128], prev_slot)
                            delta = jnp.sum(p * dp, axis=0, keepdims=True)
                            ps.append(_unfold(p_b, prev_slot))
                            dss.append(_unfold((p * (dp - delta)).astype(BF16), prev_slot))
                            dsink = dsink + jnp.where(lane == head, -jnp.sum(psink_ref[head:head + 1, :] * delta), 0.0)
                        pcols.append(jnp.concatenate(ps, axis=0))
                        dscols.append(jnp.concatenate(dss, axis=0))
                    ds2 = jnp.concatenate(dscols, axis=1)
                    dq = _tn(ds2, k2) * (HEAD ** -0.5)
                    for r, pair in enumerate(pairs):
                        dq_ref[:, pair * 128:(pair + 1) * 128] = _rope_bwd(
                            dq[r * CHUNK:(r + 1) * CHUNK], cc_ref[...], sc_ref[...]).astype(BF16)
                    dk_slab = dk_slab + _head_pair_gradient(_nn(ds2, q_stack), g)
                    dv_slab = dv_slab + _head_pair_gradient(_nn(jnp.concatenate(pcols, axis=1), do_stack), g)
                dk_slab = _rope_bwd(dk_slab, c_band, s_band)
                vcols = slice(KV_W + j * 128, KV_W + (j + 1) * 128)
                dkv_ref[:, cols] = (carry_k[:, cols] + dk_slab[:CHUNK]).astype(BF16)
                dkv_ref[:, vcols] = (carry_v[:, cols] + dv_slab[:CHUNK]).astype(BF16)
                carry_k[:, cols] = dk_slab[CHUNK:]
                carry_v[:, cols] = dv_slab[CHUNK:]
            dsink_ref[...] += dsink

        @pl.when(i == nb)
        def _():
            dkv_ref[:, :KV_W] = carry_k[...].astype(BF16)
            dkv_ref[:, KV_W:] = carry_v[...].astype(BF16)

    table = lambda which, width: pl.BlockSpec((CHUNK, width), lambda i: (which(i), 0))
    body, dep_specs, deps = _after(body, 12, after)
    return pl.pallas_call(
        body, name="bwd_attn", grid=(nb + 1,),
        in_specs=[pl.BlockSpec((CHUNK, D), lambda i: (cur(i), 0)),
                  pl.BlockSpec((CHUNK, KV_W), lambda i: (prev(i), 0)),
                  pl.BlockSpec((CHUNK, KV_W), lambda i: (cur(i), 0)),
                  pl.BlockSpec((CHUNK, KV_W), lambda i: (prev(i), OFF_VA // KV_W)),
                  pl.BlockSpec((CHUNK, KV_W), lambda i: (cur(i), OFF_VA // KV_W)),
                  table(prev, 128), table(cur, 128), table(prev, 256), table(cur, 256),
                  pl.BlockSpec((None, N_Q, CHUNK, CHUNK), lambda i: (cur(i), 0, 0, 0)),
                  pl.BlockSpec((None, N_Q, CHUNK), lambda i: (cur(i), 0, 0)),
                  pl.BlockSpec((CHUNK, D), lambda i: (cur(i), 0))] + dep_specs,
        out_specs=[pl.BlockSpec((CHUNK, D), lambda i: (cur(i), 0)),
                   pl.BlockSpec((CHUNK, 2 * KV_W), lambda i: (jnp.maximum(i - 1, 0), 0)),
                   pl.BlockSpec((1, 128), lambda i: (0, 0))],
        out_shape=[SDS((T, D), BF16), SDS((T, 2 * KV_W), BF16), SDS((1, 128), F32)],
        scratch_shapes=[pltpu.VMEM((CHUNK, KV_W), F32), pltpu.VMEM((CHUNK, KV_W), F32)],
        compiler_params=_params(1),
    )(qr, kr, kr, proj, proj, cos, cos, sin, sin, probs, psink, datt, *deps)


def _bwd_sgu(proj, da, lng, lnb, ws, bst):
    T = proj.shape[0]
    tc = min(T, 512)
    nsteps = T // tc

    def body(u_ref, vs_ref, da_ref, lng_ref, lnb_ref, ws_ref, bst_ref,
             duv_ref, dws_ref, dbs_ref, dlng_ref, dlnb_ref, dvn_s, dgu_s, dmx_sum):
        i = pl.program_id(0)

        @pl.when(i == 0)
        def _():
            dws_ref[...] = jnp.zeros_like(dws_ref)
            dlng_ref[...] = jnp.zeros_like(dlng_ref)
            dlnb_ref[...] = jnp.zeros_like(dlnb_ref)
            dmx_sum[...] = jnp.zeros_like(dmx_sum)

        u, vs, gu, tu, tv, rstd, vhat, vn = _sgu_forward_parts(u_ref, vs_ref, lng_ref, lnb_ref)
        da = da_ref[...].astype(F32)
        for g in range(GROUPS):
            wm = _masked_ws(ws_ref, g)
            cols = slice(g * CHUNK, (g + 1) * CHUNK)
            dws = jnp.zeros((CHUNK, CHUNK), F32)
            dsum = jnp.zeros((CHUNK, CHUNK), F32)
            for c in range(tc // CHUNK):
                rows = slice(c * CHUNK, (c + 1) * CHUNK)
                vn_cg = vn[rows, cols]
                mixed = _nn(wm, vn_cg) + bst_ref[:, g:g + 1]
                dgu_s[rows, cols] = da[rows, cols] * mixed
                dmx = da[rows, cols] * gu[rows, cols]
                dmxb = dmx.astype(BF16)
                dws = dws + _nt(dmxb, vn_cg)
                dsum = dsum + dmx
                dvn_s[rows, cols] = _tn(wm, dmxb)
            dws_ref[g] += dws
            dmx_sum[:, cols] += dsum
        dvn = dvn_s[...]
        dlng_ref[...] += _colsum(dvn * vhat)
        dlnb_ref[...] += _colsum(dvn)
        dvh = dvn * lng_ref[...]
        dgv = rstd * (dvh - jnp.mean(dvh, axis=-1, keepdims=True) - vhat * jnp.mean(dvh * vhat, axis=-1, keepdims=True))
        duv_ref[:, :D] = (dgu_s[...] * _gelu_grad(u, tu)).astype(BF16)
        duv_ref[:, D:] = (dgv * _gelu_grad(vs, tv)).astype(BF16)

        @pl.when(i == nsteps - 1)
        def _():
            row = lax.broadcasted_iota(jnp.int32, (CHUNK, CHUNK), 0)
            col = lax.broadcasted_iota(jnp.int32, (CHUNK, CHUNK), 1)
            for g in range(GROUPS):
                dws_ref[g] = jnp.where(row >= col, dws_ref[g], 0.0)
                dbs_ref[g:g + 1, :] = _colsum(dmx_sum[:, g * CHUNK:(g + 1) * CHUNK].T)

    const2 = lambda i: (0, 0)
    return pl.pallas_call(
        body, name="bwd_sgu", grid=(nsteps,),
        in_specs=[pl.BlockSpec((tc, D), lambda i: (i, 0)), pl.BlockSpec((tc, D), lambda i: (i, 1)),
                  pl.BlockSpec((tc, D), lambda i: (i, 0)), pl.BlockSpec((1, D), const2), pl.BlockSpec((1, D), const2),
                  pl.BlockSpec((GROUPS, CHUNK, CHUNK), lambda i: (0, 0, 0)), pl.BlockSpec((CHUNK, GROUPS), const2)],
        out_specs=[pl.BlockSpec((tc, 2 * D), lambda i: (i, 0)), pl.BlockSpec((GROUPS, CHUNK, CHUNK), lambda i: (0, 0, 0)),
                   pl.BlockSpec((GROUPS, CHUNK), const2), pl.BlockSpec((1, D), const2), pl.BlockSpec((1, D), const2)],
        out_shape=[SDS((T, 2 * D), BF16), SDS((GROUPS, CHUNK, CHUNK), F32), SDS((GROUPS, CHUNK), F32),
                   SDS((1, D), F32), SDS((1, D), F32)],
        scratch_shapes=[pltpu.VMEM((tc, D), F32), pltpu.VMEM((tc, D), F32), pltpu.VMEM((CHUNK, D), F32)],
        compiler_params=_params(1),
    )(proj, proj, da, lng, lnb, ws, bst)


IN_SEG_WIDTHS = (2 * D, D, 2 * N_KV * HEAD, 2 * D)


def _resident(shape):
    return pl.BlockSpec(shape, lambda *_: (0,) * len(shape), pipeline_mode=pl.Buffered(1))


def _bwd_in(duv, dq, dkv, dg, win_t, x, dx1, g0, after=None):
    T = x.shape[0]
    tm = min(T, 512)

    def body(duv_ref, dq_ref, dkv_ref, dg_ref, w_ref, x_ref, dx1_ref, g0_ref, gx_ref, dg0_ref):
        @pl.when(pl.program_id(0) == 0)
        def _():
            dg0_ref[...] = jnp.zeros_like(dg0_ref)

        dh, off = None, 0
        for ref, width in zip((duv_ref, dq_ref, dkv_ref, dg_ref), IN_SEG_WIDTHS):
            part = _nn(ref[...], w_ref[off:off + width, :])
            dh = part if dh is None else dh + part
            off += width
        r0, xh = _rms_stats(x_ref[...])
        dg0_ref[...] += _colsum(dh * xh)
        gx_ref[...] = dx1_ref[...] + _rms_bwd(dh, xh, r0, g0_ref[...])

    row = lambda i: (i, 0)
    body, dep_specs, deps = _after(body, 8, after)
    return pl.pallas_call(
        body, name="bwd_in", grid=(T // tm,),
        in_specs=[pl.BlockSpec((tm, w), row) for w in IN_SEG_WIDTHS] + [
            _resident((IN_W, D)), pl.BlockSpec((tm, D), row), pl.BlockSpec((tm, D), row),
            pl.BlockSpec((1, D), lambda i: (0, 0))] + dep_specs,
        out_specs=[pl.BlockSpec((tm, D), row), pl.BlockSpec((1, D), lambda i: (0, 0))],
        out_shape=[SDS((T, D), F32), SDS((1, D), F32)],
        compiler_params=_params(1),
    )(duv, dq, dkv, dg, win_t, x, dx1, g0, *deps)


def _wgrad_rows(h, segs, first_row, into, name):
    T = h.shape[0]
    tt = min(T, 2048)
    widths = [s.shape[1] for s in segs]
    rows = sum(widths)
    n_in = 1 + len(segs) + (into is not None)

    def body(*refs):
        h_ref, seg_refs = refs[0], refs[1:1 + len(segs)]
        dw_ref, acc, stage, sem = refs[n_in], refs[n_in + 1], refs[n_in + 2], refs[n_in + 3]
        t = pl.program_id(0)

        @pl.when(t == 0)
        def _():
            acc[...] = jnp.zeros_like(acc)

        off = 0
        for ref, width in zip(seg_refs, widths):
            acc[off:off + width, :] += _tn(ref[...], h_ref[...])
            off += width

        @pl.when(t == T // tt - 1)
        def _():
            stage[...] = acc[...].astype(BF16)
            out = pltpu.make_async_copy(stage, dw_ref.at[pl.ds(first_row, rows)], sem)
            out.start()
            out.wait()

    row = lambda t: (t, 0)
    return pl.pallas_call(
        body, name=name, grid=(T // tt,),
        in_specs=[pl.BlockSpec((tt, D), row)] + [pl.BlockSpec((tt, w), row) for w in widths] + [_ANY] * (into is not None),
        out_specs=_ANY,
        out_shape=SDS((IN_W, D), BF16),
        input_output_aliases={} if into is None else {n_in - 1: 0},
        scratch_shapes=[pltpu.VMEM((rows, D), F32), pltpu.VMEM((rows, D), BF16), pltpu.SemaphoreType.DMA],
        compiler_params=_params(1),
    )(h, *segs, *([] if into is None else [into]))


def _wgrad_in(h, duv, dq, dkv, dg):
    dw = _wgrad_rows(h, [dg], IN_SEG_WIDTHS[0] + IN_SEG_WIDTHS[1] + IN_SEG_WIDTHS[2], None, "wgrad_in_gates")
    dw = _wgrad_rows(h, [duv], 0, dw, "wgrad_in_uv")
    return _wgrad_rows(h, [dq, dkv], IN_SEG_WIDTHS[0], dw, "wgrad_in_qkv")


def _place():
    x, y, c = lax.axis_index("x"), lax.axis_index("y"), lax.axis_index("c")
    return x, y, c, 4 * x + 2 * y + c


def _peers(x, y, c):
    out = []
    for mask in range(1, N_DEV):
        px = 1 - x if mask & 4 else x
        py = 1 - y if mask & 2 else y
        pc = 1 - c if mask & 1 else c
        out.append(((px, py, pc), 4 * px + 2 * py + pc))
    return out


def _all_to_all(arrays, gather, name, after=None):
    n = len(arrays)

    def body(*refs):
        ins, outs = refs[:n], refs[n:2 * n]
        send_sems, recv_sems, local_sems = refs[2 * n:]
        x, y, c, me = _place()
        local, sends, recvs = [], [], []
        for a in range(n):
            src_own = ins[a] if gather[a] else ins[a].at[me]
            local.append(pltpu.make_async_copy(src_own, outs[a].at[me], local_sems.at[a]))
            for k, (peer, pid) in enumerate(_peers(x, y, c)):
                sem = a * (N_DEV - 1) + k
                src = ins[a] if gather[a] else ins[a].at[pid]
                sends.append(pltpu.make_async_remote_copy(
                    src_ref=src, dst_ref=outs[a].at[me], send_sem=send_sems.at[sem], recv_sem=recv_sems.at[sem],
                    device_id=peer, device_id_type=MESH))
                recvs.append(pltpu.make_async_remote_copy(
                    src_ref=src, dst_ref=outs[a].at[pid], send_sem=send_sems.at[sem], recv_sem=recv_sems.at[sem],
                    device_id=peer, device_id_type=MESH))
        for cp in local + sends:
            cp.start()
        for cp in recvs:
            cp.wait_recv()
        for cp in sends:
            cp.wait_send()
        for cp in local:
            cp.wait()

    out_shape = [SDS((N_DEV,) + a.shape if gt else a.shape, a.dtype) for a, gt in zip(arrays, gather)]
    nsem = n * (N_DEV - 1)
    body, dep_specs, deps = _after(body, n, after)
    return pl.pallas_call(
        body, name=name,
        in_specs=[pl.BlockSpec(memory_space=pl.ANY)] * n + dep_specs,
        out_specs=[pl.BlockSpec(memory_space=pl.ANY)] * n,
        out_shape=out_shape,
        scratch_shapes=[pltpu.SemaphoreType.DMA((nsem,)), pltpu.SemaphoreType.DMA((nsem,)), pltpu.SemaphoreType.DMA((n,))],
    )(*arrays, *deps)


_HBM = pl.BlockSpec(memory_space=pltpu.HBM)
_SEM = pl.BlockSpec(memory_space=pltpu.SEMAPHORE)
_EFFECT = pltpu.SideEffectType.DATAFLOW_SIDE_EFFECTING
GATHER = "gather"
SCATTER = "scatter"
SPREAD = "spread"


def _zone_shape(a, mode):
    if mode == GATHER:
        return (N_DEV,) + a.shape
    return (N_DEV - 1,) + (a.shape[1:] if mode == SCATTER else a.shape)


def _start_copies(arrays, modes, name, after=None):
    n = len(arrays)
    zones = [lax.empty(_zone_shape(a, m), a.dtype) for a, m in zip(arrays, modes)]

    def body(*refs):
        ins, lands = refs[:n], refs[n:2 * n]
        send_sems, recv_sems = refs[-2 * n - 3], refs[-2 * n - 2]
        token = refs[-1]
        x, y, c, me = _place()
        for a in range(n):
            for k, (peer, pid) in enumerate(_peers(x, y, c)):
                src = ins[a].at[pid] if modes[a] == SCATTER else ins[a]
                dst = lands[a].at[me] if modes[a] == GATHER else lands[a].at[k]
                pltpu.make_async_remote_copy(src_ref=src, dst_ref=dst, send_sem=send_sems.at[a], recv_sem=recv_sems.at[a],
                                             device_id=peer, device_id_type=MESH).start()
            if modes[a] == GATHER:
                pltpu.make_async_remote_copy(src_ref=ins[a], dst_ref=lands[a].at[me], send_sem=send_sems.at[a],
                                             recv_sem=recv_sems.at[a], device_id=(x, y, c), device_id_type=MESH).start()
        token[...] = jnp.zeros_like(token)

    hbm = lambda a: pltpu.HBM(a.shape, a.dtype)
    sems = pltpu.SemaphoreType.DMA((n,))
    extra = [] if after is None else [after]
    operands = [pltpu.with_memory_space_constraint(a, pltpu.HBM) for a in list(arrays) + zones]
    res = pl.pallas_call(
        body, name=name,
        out_shape=(sems, sems, *[hbm(a) for a in arrays], *[hbm(z) for z in zones], SDS((8, 128), F32)),
        in_specs=[_HBM] * (2 * n) + [_ANY] * len(extra),
        out_specs=(_SEM, _SEM, *[_HBM] * (2 * n), pl.BlockSpec(memory_space=pltpu.VMEM)),
        input_output_aliases={i: 2 + i for i in range(2 * n)},
        compiler_params=pltpu.CompilerParams(has_side_effects=_EFFECT),
    )(*operands, *extra)
    return res[0], res[1], list(res[2:2 + n]), list(res[2 + n:2 + 2 * n]), res[-1]


def _wait_copies(started, after, name, count=N_DEV - 1):
    send_sems, recv_sems, thru, zones, _ = started
    nt, nz = len(thru), len(zones)

    def body(*refs):
        lands = refs[nt:nt + nz]
        send_ref, recv_ref = refs[nt + nz], refs[nt + nz + 1]
        x, y, c, _ = _place()
        for a in range(nz):
            blocks = lands[a].at[pl.ds(0, count)]
            cp = pltpu.make_async_remote_copy(src_ref=blocks, dst_ref=blocks, send_sem=send_ref.at[a], recv_sem=recv_ref.at[a],
                                              device_id=(x, y, 1 - c), device_id_type=MESH)
            cp.wait_send()
            cp.wait_recv()

    hbm = lambda a: pltpu.HBM(a.shape, a.dtype)
    res = pl.pallas_call(
        body, name=name,
        out_shape=tuple(hbm(a) for a in thru + zones),
        in_specs=[_HBM] * (nt + nz) + [_SEM, _SEM, _ANY],
        out_specs=tuple([_HBM] * (nt + nz)),
        input_output_aliases={i: i for i in range(nt + nz)},
        compiler_params=pltpu.CompilerParams(has_side_effects=_EFFECT),
    )(*thru, *zones, send_sems, recv_sems, after)
    return list(res[:nt]), list(res[nt:])


def _split_start(body, arrays, zones, name, after):
    n = len(arrays) + len(zones)
    hbm = lambda a: pltpu.HBM(a.shape, a.dtype)
    sems = pltpu.SemaphoreType.DMA((max(len(zones), 1),))
    extra = [] if after is None else [after]
    operands = [pltpu.with_memory_space_constraint(a, pltpu.HBM) for a in list(arrays) + list(zones)]
    res = pl.pallas_call(
        body, name=name,
        out_shape=(sems, sems, *[hbm(a) for a in operands], SDS((8, 128), F32)),
        in_specs=[_HBM] * n + [_ANY] * len(extra),
        out_specs=(_SEM, _SEM, *[_HBM] * n, pl.BlockSpec(memory_space=pltpu.VMEM)),
        input_output_aliases={i: 2 + i for i in range(n)},
        compiler_params=pltpu.CompilerParams(has_side_effects=_EFFECT),
    )(*operands, *extra)
    return res[0], res[1], list(res[2:2 + len(arrays)]), list(res[2 + len(arrays):2 + n]), res[-1]


def _gather_first_leg(shard, name, after=None):
    zone = lax.empty((N_DEV,) + shard.shape, shard.dtype)
    extra = 0 if after is None else 1

    def body(*refs):
        src, land = refs[0], refs[1]
        send_sem, recv_sem, token = refs[2 + extra], refs[3 + extra], refs[-1]
        x, y, c, me = _place()
        for peer in ((x, y, c), (x, y, 1 - c), (1 - x, y, c), (x, 1 - y, c), (1 - x, 1 - y, c)):
            pltpu.make_async_remote_copy(src_ref=src, dst_ref=land.at[me], send_sem=send_sem.at[0], recv_sem=recv_sem.at[0],
                                         device_id=peer, device_id_type=MESH).start()
        token[...] = jnp.zeros_like(token)

    return _split_start(body, [shard], [zone], name, after)


def _gather_second_leg(zone, name, after=None):
    extra = 0 if after is None else 1

    def body(*refs):
        land = refs[0]
        send_sem, recv_sem, token = refs[1 + extra], refs[2 + extra], refs[-1]
        x, y, c, _ = _place()
        for px, py in ((1 - x, y), (x, 1 - y), (1 - x, 1 - y)):
            slot = 4 * px + 2 * py + c
            pltpu.make_async_remote_copy(src_ref=land.at[slot], dst_ref=land.at[slot], send_sem=send_sem.at[0],
                                         recv_sem=recv_sem.at[0], device_id=(x, y, 1 - c), device_id_type=MESH).start()
        token[...] = jnp.zeros_like(token)

    return _split_start(body, [], [zone], name, after)


UPDATE_BLOCK_ELEMS = 256 * 1024


def _update_rows(R, C):
    fits = [t for t in range(8, R + 1, 8) if R % t == 0 and t * C <= UPDATE_BLOCK_ELEMS]
    whole = [t for t in fits if t % 16 == 0]
    return max(whole or fits)


def _adamw_math(g, w, m, v):
    m2 = ADAM_B1 * m + (1.0 - ADAM_B1) * g
    v2 = ADAM_B2 * v + (1.0 - ADAM_B2) * (g * g)
    m_hat = m2 / (1.0 - ADAM_B1 ** ADAM_STEP)
    v_hat = v2 / (1.0 - ADAM_B2 ** ADAM_STEP)
    delta = -ADAM_LR * (m_hat / (jnp.sqrt(v_hat) + ADAM_EPS) + ADAM_WD * w)
    return delta, m2, v2


def _sum_adamw(parts, w, m, v, name):
    R, C = w.shape
    tr = _update_rows(R, C)

    def body(p_ref, w_ref, m_ref, v_ref, g_ref, d_ref, m2_ref, v2_ref):
        g = p_ref[0]
        for k in range(1, N_DEV):
            g = g + p_ref[k]
        g_ref[...] = g
        d_ref[...], m2_ref[...], v2_ref[...] = _adamw_math(g, w_ref[...], m_ref[...], v_ref[...])

    blk = pl.BlockSpec((tr, C), lambda i: (i, 0))
    return pl.pallas_call(
        body, name=name, grid=(R // tr,),
        in_specs=[pl.BlockSpec((N_DEV, tr, C), lambda i: (0, i, 0)), blk, blk, blk],
        out_specs=[blk] * 4,
        out_shape=[SDS((R, C), F32)] * 4,
        compiler_params=_params(1),
    )(parts, w, m, v)


def _sum_adamw_peers(me, own, parts, w, m, v, name, replicated):
    R, C = w.shape
    tr = _update_rows(R, C)

    def body(me_ref, own_ref, p_ref, w_ref, m_ref, v_ref, g_ref, d_ref, m2_ref, v2_ref):
        if replicated:
            mine = me_ref[0]
            g = None
            for j in range(N_DEV):
                k = jnp.maximum(jnp.bitwise_xor(mine, j) - 1, 0)
                term = jnp.where(mine == j, own_ref[...], p_ref[k])
                g = term if g is None else g + term
        else:
            g = own_ref[...].astype(F32)
            for k in range(N_DEV - 1):
                g = g + p_ref[k].astype(F32)
        g_ref[...] = g
        d_ref[...], m2_ref[...], v2_ref[...] = _adamw_math(g, w_ref[...], m_ref[...], v_ref[...])

    blk = pl.BlockSpec((tr, C), lambda i, me_ref: (i, 0))
    own_spec = blk if replicated else pl.BlockSpec((None, tr, C), lambda i, me_ref: (me_ref[0], i, 0))
    return pl.pallas_call(
        body, name=name,
        grid_spec=pltpu.PrefetchScalarGridSpec(
            num_scalar_prefetch=1, grid=(R // tr,),
            in_specs=[own_spec, pl.BlockSpec((N_DEV - 1, tr, C), lambda i, me_ref: (0, i, 0)), blk, blk, blk],
            out_specs=[blk] * 4),
        out_shape=[SDS((R, C), F32)] * 4,
        compiler_params=_params(1),
    )(me, own, parts, w, m, v)


SMALL = ("ln_v_gain", "ln_v_bias", "w_spatial", "b_spatial", "sinks", "norm_mix_post", "norm_ff_pre", "norm_ff_post")
SMALL_ROWS = {"ln_v_gain": 8, "ln_v_bias": 8, "w_spatial": 1024, "b_spatial": 8, "sinks": 8,
              "norm_mix_post": 8, "norm_ff_pre": 8, "norm_ff_post": 8}
SMALL_PACK_ROWS = 1152


def _pack_small(vals):
    rows = []
    for name in SMALL:
        flat = vals[name].reshape(-1)
        pad = SMALL_ROWS[name] * 128 - flat.shape[0]
        if pad:
            flat = jnp.concatenate([flat, jnp.zeros((pad,), F32)])
        rows.append(flat.reshape(SMALL_ROWS[name], 128))
    rows.append(jnp.zeros((SMALL_PACK_ROWS - sum(SMALL_ROWS.values()), 128), F32))
    return jnp.concatenate(rows, axis=0)


def _unpack_small(packed, shapes):
    out, r = {}, 0
    for name in SMALL:
        n = 1
        for s in shapes[name]:
            n *= s
        out[name] = packed[r:r + SMALL_ROWS[name]].reshape(-1)[:n].reshape(shapes[name])
        r += SMALL_ROWS[name]
    return out


def _rope_rows():
    d = jnp.arange(128) % HEAD
    inv = ROPE_THETA ** (-(2.0 * (d % (ROPE // 2))).astype(F32) / ROPE)
    invf = jnp.where(d < ROPE, inv, 0.0).astype(F32).reshape(1, 128)
    sgn = jnp.where(d < ROPE // 2, -1.0, jnp.where(d < ROPE, 1.0, 0.0)).astype(F32).reshape(1, 128)
    return invf, sgn


def kernel(x, positions, w_in, ln_v_gain, ln_v_bias, w_spatial, b_spatial, sinks, w_a, w_b, w_o, norm_mix_pre, norm_mix_post, w_ff_in, w_ff_out, norm_ff_pre, norm_ff_post, loss_target, m_w_in, m_ln_v_gain, m_ln_v_bias, m_w_spatial, m_b_spatial, m_sinks, m_w_a, m_w_b, m_w_o, m_norm_mix_pre, m_norm_mix_post, m_w_ff_in, m_w_ff_out, m_norm_ff_pre, m_norm_ff_post, v_w_in, v_ln_v_gain, v_ln_v_bias, v_w_spatial, v_b_spatial, v_sinks, v_w_a, v_w_b, v_w_o, v_norm_mix_pre, v_norm_mix_post, v_w_ff_in, v_w_ff_out, v_norm_ff_pre, v_norm_ff_post):
    given = dict(locals())
    T = x.shape[1]
    xt = x[0]
    tgt = loss_target[0]
    bst = b_spatial[0].T
    ws = w_spatial[0]

    me = 4 * lax.axis_index("x") + 2 * lax.axis_index("y") + lax.axis_index("c")
    me_arr = me.astype(jnp.int32).reshape(1)

    rest = ("w_a", "w_b", "w_o", "w_ff_in", "w_ff_out")
    shard = {n: given[n][0].astype(BF16) for n in rest}
    g_one = _gather_first_leg(w_in[0].T.astype(BF16), "gather_in_start")
    cos, sin = _rope_tables(positions.astype(F32).reshape(T, 1), *_rope_rows(), after=g_one[-1])
    small_state = [_pack_small({n: given[k + n] for n in SMALL}) for k in ("", "m_", "v_")]
    h = _rms_pre(xt, norm_mix_pre, after=[cos, *small_state, *[shard[n] for n in rest]])
    _, (win8,) = _wait_copies(g_one, h, "gather_in_wait", count=5)
    g_two = _gather_second_leg(win8, "gather_in_pass_start")
    g_rest = _start_copies([shard[n] for n in rest], [GATHER] * len(rest), "gather_rest_start", after=g_two[-1])
    _, (win8,) = _wait_copies(g_two, g_rest[-1], "gather_in_pass_wait", count=3)
    win = win8.reshape(IN_W, D)

    proj = _fwd_in(h, win)
    att, qr, kr, probs, psink = _fwd_attn(proj, cos, sin, sinks[0])
    a = _fwd_sgu(proj, ln_v_gain, ln_v_bias, ws, bst, after=att)
    gw = dict(zip(rest, _wait_copies(g_rest, a, "gather_rest_wait", count=N_DEV)[1]))
    wa, wb, wo = (gw[n].reshape(D, D) for n in ("w_a", "w_b", "w_o"))
    wfi3 = gw["w_ff_in"]
    wfo = gw["w_ff_out"].reshape(D_FF, D)
    merged, a2, b2, mix, x1, hf = _fwd_mix(a, att, proj, xt, wa, wb, wo, norm_mix_post, norm_ff_pre)
    f, dy, dff, dg3, loss_part = _fwd_ff(hf, wfi3, wfo, x1, tgt, norm_ff_post)

    df, dx1, dmix, dg2, dg1 = _bwd_ff(dff, f, wfi3, wfo, x1, dy, mix, norm_mix_post, norm_ff_pre)
    dwfi3, dwfo = _wgrad_ff(hf, df, f, dff)
    own_ff = [dwfi3, dwfo.reshape(N_DEV, D_FF // N_DEV, D)]
    x_ff = _start_copies(own_ff, [SCATTER] * 2, "exchange_ff_start")
    dgate, da, datt, dwo, dwa, dwb = _bwd_mix(dmix, proj, a2, b2, merged, a, att, wo, wa, wb, after=x_ff[-1])
    own_mix = [g.reshape(N_DEV, D // N_DEV, D) for g in (dwa, dwb, dwo)]
    x_mix = _start_copies(own_mix, [SCATTER] * 3, "exchange_mix_start")
    dq, dkv, dsink = _bwd_attn(qr, kr, probs, psink, proj, cos, sin, datt, after=x_mix[-1])
    duv, dws, dbs, dlng, dlnb = _bwd_sgu(proj, da, ln_v_gain, ln_v_bias, ws, bst)
    small_grads = {"ln_v_gain": dlng, "ln_v_bias": dlnb, "w_spatial": dws, "b_spatial": dbs, "sinks": dsink[:, :N_Q],
                   "norm_mix_post": dg1, "norm_ff_pre": dg2, "norm_ff_post": dg3}
    x_small = _start_copies([_pack_small(small_grads)], [SPREAD], "exchange_small_start")
    dwin = _wgrad_in(h, duv, dq, dkv, dgate)
    own_in = [dwin.reshape(N_DEV, IN_W // N_DEV, D)]
    x_in = _start_copies(own_in, [SCATTER], "exchange_in_start", after=x_small[-1])
    grad_x, dg0 = _bwd_in(duv, dq, dkv, dgate, win, xt, dx1, norm_mix_pre, after=x_in[-1])

    results = {}

    def update(n, own, parts, transposed=False):
        state = [given[k + n][0].T if transposed else given[k + n][0] for k in ("", "m_", "v_")]
        res = _sum_adamw_peers(me_arr, own, parts, *state, "adamw_" + n, False)
        results[n] = [(r.T if transposed else r).reshape(given[n].shape) for r in res]

    own_ff, p_ff = _wait_copies(x_ff, grad_x, "exchange_ff_wait")
    update("w_ff_in", own_ff[0], p_ff[0])
    update("w_ff_out", own_ff[1], p_ff[1])
    own_mix, p_mix = _wait_copies(x_mix, results["w_ff_out"][0], "exchange_mix_wait")
    for n, own, parts in zip(("w_a", "w_b", "w_o"), own_mix, p_mix):
        update(n, own, parts)
    tail = jnp.concatenate([dg0.reshape(8, 128), jnp.tile(loss_part, (8, 1))], axis=0)
    (tail_all,) = _all_to_all([tail], [True], "exchange_tail", after=results["w_o"][0])
    dg0_all = tail_all[:, :8]
    own_small, p_small = _wait_copies(x_small, tail_all, "exchange_small_wait")
    own_in, p_in = _wait_copies(x_in, p_small[0], "exchange_in_wait")
    update("w_in", own_in[0], p_in[0], transposed=True)
    packed = _sum_adamw_peers(me_arr, own_small[0], p_small[0], *small_state, "adamw_small", True)
    shapes = {n: given[n].shape for n in SMALL}
    unpacked = [_unpack_small(p, shapes) for p in packed]
    for n in SMALL:
        results[n] = [u[n] for u in unpacked]
    n = "norm_mix_pre"
    results[n] = [r.reshape(given[n].shape) for r in _sum_adamw(
        dg0_all, given[n].reshape(8, 128), given["m_" + n].reshape(8, 128), given["v_" + n].reshape(8, 128), "adamw_" + n)]

    loss = jnp.sum(tail_all[:, 8, 0])
    order = ("w_in", "ln_v_gain", "ln_v_bias", "w_spatial", "b_spatial", "sinks", "w_a", "w_b", "w_o", "norm_mix_pre",
             "norm_mix_post", "w_ff_in", "w_ff_out", "norm_ff_pre", "norm_ff_post")
    out = [loss, grad_x.reshape(x.shape)]
    for k in range(4):
        out += [results[n][k] for n in order]
    return tuple(out)
```

```python
import jax
import jax.numpy as jnp
from jax import lax
from jax.experimental import pallas as pl
from jax.experimental.pallas import tpu as pltpu

F32 = jnp.float32
BF16 = jnp.bfloat16

N_DEV = 8
D = 1024
D_FF = 4096
IN_W = 5632
CHUNK = 128
GROUPS = 8
HEAD = 64
N_Q = 16
N_KV = 4
ROPE = 16
ROPE_THETA = 500000.0
EPS = 1e-6
OFF_Q, OFF_K, OFF_VA, OFF_GA, OFF_GB = 2048, 3072, 3328, 3584, 4608

ADAM_LR = 0.001
ADAM_B1 = 0.9
ADAM_B2 = 0.999
ADAM_EPS = 1e-08
ADAM_WD = 0.01
ADAM_STEP = 10

VMEM_LIMIT = 62 * 1024 * 1024

SDS = jax.ShapeDtypeStruct
MESH = pl.DeviceIdType.MESH


def _params(n_axes):
    return pltpu.CompilerParams(dimension_semantics=("arbitrary",) * n_axes, vmem_limit_bytes=VMEM_LIMIT)


def _nt(a, b):
    return lax.dot_general(a, b, (((1,), (1,)), ((), ())), preferred_element_type=F32)


def _tn(a, b):
    return lax.dot_general(a, b, (((0,), (0,)), ((), ())), preferred_element_type=F32)


def _nn(a, b):
    return jnp.dot(a, b, preferred_element_type=F32)


def _gelu(x):
    t = jnp.tanh(0.7978845608028654 * (x + 0.044715 * (x * x * x)))
    return 0.5 * x * (1.0 + t), t


def _gelu_grad(x, t):
    return 0.5 * (1.0 + t) + 0.5 * x * (1.0 - t * t) * (0.7978845608028654 * (1.0 + 3.0 * 0.044715 * x * x))


def _sigmoid(x):
    return 1.0 / (1.0 + jnp.exp(-x))


def _rms_stats(v):
    r = lax.rsqrt(jnp.mean(v * v, axis=-1, keepdims=True) + EPS)
    return r, v * r


def _rms_bwd(d, vhat, r, g):
    gd = g * d
    return r * (gd - vhat * jnp.mean(gd * vhat, axis=-1, keepdims=True))


def _colsum(v):
    return jnp.sum(v, axis=0, keepdims=True)


_ANY = pl.BlockSpec(memory_space=pl.ANY)


def _after(body, n_in, after):
    if after is None:
        return body, [], []
    deps = list(after) if isinstance(after, (list, tuple)) else [after]

    def ordered(*refs):
        return body(*refs[:n_in], *refs[n_in + len(deps):])

    return ordered, [_ANY] * len(deps), deps


def _rms_pre(x, g0, after=None):
    T = x.shape[0]
    tm = min(T, 1024)

    def body(x_ref, g_ref, h_ref):
        _, xh = _rms_stats(x_ref[...])
        h_ref[...] = (xh * g_ref[...]).astype(BF16)

    body, dep_specs, deps = _after(body, 2, after)
    return pl.pallas_call(
        body, name="rms_pre", grid=(T // tm,),
        in_specs=[pl.BlockSpec((tm, D), lambda i: (i, 0)), pl.BlockSpec((1, D), lambda i: (0, 0))] + dep_specs,
        out_specs=pl.BlockSpec((tm, D), lambda i: (i, 0)),
        out_shape=SDS((T, D), BF16),
        compiler_params=_params(1),
    )(x, g0, *deps)


def _fwd_in(h, win_t):
    T = h.shape[0]
    tm, tn = min(T, 512), 1408

    def body(h_ref, w_ref, p_ref):
        for j in range(IN_W // tn):
            cols = slice(j * tn, (j + 1) * tn)
            p_ref[:, cols] = _nt(h_ref[...], w_ref[cols, :]).astype(BF16)

    return pl.pallas_call(
        body, name="fwd_in", grid=(T // tm,),
        in_specs=[pl.BlockSpec((tm, D), lambda i: (i, 0)), _resident((IN_W, D))],
        out_specs=pl.BlockSpec((tm, IN_W), lambda i: (i, 0)),
        out_shape=SDS((T, IN_W), BF16),
        compiler_params=_params(1),
    )(h, win_t)


def _sgu_forward_parts(u_ref, vs_ref, lng_ref, lnb_ref):
    u = u_ref[...].astype(F32)
    vs = vs_ref[...].astype(F32)
    gu, tu = _gelu(u)
    gv, tv = _gelu(vs)
    mu = jnp.mean(gv, axis=-1, keepdims=True)
    dv = gv - mu
    rstd = lax.rsqrt(jnp.mean(dv * dv, axis=-1, keepdims=True) + EPS)
    vhat = dv * rstd
    vn = (vhat * lng_ref[...] + lnb_ref[...]).astype(BF16)
    return u, vs, gu, tu, tv, rstd, vhat, vn


def _masked_ws(ws_ref, g):
    row = lax.broadcasted_iota(jnp.int32, (CHUNK, CHUNK), 0)
    col = lax.broadcasted_iota(jnp.int32, (CHUNK, CHUNK), 1)
    return jnp.where(row >= col, ws_ref[g], 0.0).astype(BF16)


def _fwd_sgu(proj, lng, lnb, ws, bst, after=None):
    T = proj.shape[0]
    tc = min(T, 512)

    def body(u_ref, vs_ref, lng_ref, lnb_ref, ws_ref, bst_ref, a_ref):
        _, _, gu, _, _, _, _, vn = _sgu_forward_parts(u_ref, vs_ref, lng_ref, lnb_ref)
        for g in range(GROUPS):
            wm = _masked_ws(ws_ref, g)
            cols = slice(g * CHUNK, (g + 1) * CHUNK)
            for c in range(tc // CHUNK):
                rows = slice(c * CHUNK, (c + 1) * CHUNK)
                mixed = _nn(wm, vn[rows, cols]) + bst_ref[:, g:g + 1]
                a_ref[rows, cols] = (gu[rows, cols] * mixed).astype(BF16)

    body, dep_specs, deps = _after(body, 6, after)
    return pl.pallas_call(
        body, name="fwd_sgu", grid=(T // tc,),
        in_specs=[pl.BlockSpec((tc, D), lambda i: (i, 0)), pl.BlockSpec((tc, D), lambda i: (i, 1)),
                  pl.BlockSpec((1, D), lambda i: (0, 0)), pl.BlockSpec((1, D), lambda i: (0, 0)),
                  pl.BlockSpec((GROUPS, CHUNK, CHUNK), lambda i: (0, 0, 0)),
                  pl.BlockSpec((CHUNK, GROUPS), lambda i: (0, 0))] + dep_specs,
        out_specs=pl.BlockSpec((tc, D), lambda i: (i, 0)),
        out_shape=SDS((T, D), BF16),
        compiler_params=_params(1),
    )(proj, proj, lng, lnb, ws, bst, *deps)


def _rope_tables(posf, invf, sgn, after=None):
    T = posf.shape[0]
    tr = min(T, 1024)

    def body(pos_ref, invf_ref, sgn_ref, c_ref, s_ref):
        ang = pos_ref[...] * invf_ref[...]
        c_ref[...] = jnp.cos(ang)
        s = jnp.sin(ang)
        s_ref[:, :128] = jnp.where(sgn_ref[...] < 0.0, -s, 0.0)
        s_ref[:, 128:] = jnp.where(sgn_ref[...] > 0.0, s, 0.0)

    body, dep_specs, deps = _after(body, 3, after)
    return pl.pallas_call(
        body, name="rope_tables", grid=(T // tr,),
        in_specs=[pl.BlockSpec((tr, 1), lambda i: (i, 0)), pl.BlockSpec((1, 128), lambda i: (0, 0)),
                  pl.BlockSpec((1, 128), lambda i: (0, 0))] + dep_specs,
        out_specs=[pl.BlockSpec((tr, 128), lambda i: (i, 0)), pl.BlockSpec((tr, 256), lambda i: (i, 0))],
        out_shape=[SDS((T, 128), F32), SDS((T, 256), F32)],
        compiler_params=_params(1),
    )(posf, invf, sgn, *deps)


def _rope(v, c, s):
    v = v.astype(F32)
    return v * c + pltpu.roll(v, 128 - ROPE // 2, 1) * s[:, :128] + pltpu.roll(v, ROPE // 2, 1) * s[:, 128:]


def _rope_bwd(dv, c, s):
    return dv * c + pltpu.roll(dv * s[:, :128], ROPE // 2, 1) + pltpu.roll(dv * s[:, 128:], 128 - ROPE // 2, 1)


def _fold_masks(first):
    jj = lax.broadcasted_iota(jnp.int32, (CHUNK, CHUNK), 0)
    t = lax.broadcasted_iota(jnp.int32, (CHUNK, CHUNK), 1)
    prev = jj > t
    return prev, jnp.where(prev & first, -1e30, 0.0)


def _fold(band, prev):
    return jnp.where(prev, band[:CHUNK], band[CHUNK:])


def _unfold(folded, prev):
    return jnp.concatenate([jnp.where(prev, folded, 0.0), jnp.where(prev, 0.0, folded)], axis=0)


def _softmax_sink(s, sink, key_axis):
    m = jnp.maximum(jnp.max(s, axis=key_axis, keepdims=True), sink)
    p = jnp.exp(s - m)
    esink = jnp.exp(sink - m)
    inv = 1.0 / (jnp.sum(p, axis=key_axis, keepdims=True) + esink)
    return p * inv, esink * inv


def _head_pair_operand(slab, g):
    lo = lax.broadcasted_iota(jnp.int32, slab.shape, 1) < HEAD
    if g % 2 == 0:
        first = jnp.where(lo, slab, 0.0)
        second = pltpu.roll(first, HEAD, 1)
    else:
        second = jnp.where(lo, 0.0, slab)
        first = pltpu.roll(second, HEAD, 1)
    return jnp.concatenate([first, second], axis=0).astype(BF16)


def _head_pair_gradient(acc, g):
    top, bot = acc[:2 * CHUNK], acc[2 * CHUNK:]
    lo = lax.broadcasted_iota(jnp.int32, top.shape, 1) < HEAD
    if g % 2 == 0:
        return jnp.where(lo, top, 0.0) + pltpu.roll(jnp.where(lo, 0.0, bot), HEAD, 1)
    return pltpu.roll(jnp.where(lo, top, 0.0), HEAD, 1) + jnp.where(lo, 0.0, bot)


PAIRS_PER_KV = N_Q // N_KV // 2
KV_W = N_KV * HEAD


def _band(prev_ref, cur_ref, cols=slice(None)):
    return jnp.concatenate([prev_ref[:, cols], cur_ref[:, cols]], axis=0)


def _fwd_attn(proj, cos, sin, sinks):
    T = proj.shape[0]
    nb = T // CHUNK
    cur = lambda i: i
    prev = lambda i: jnp.maximum(i - 1, 0)

    def body(q_ref, kp_ref, kc_ref, vp_ref, vc_ref, cp_ref, cc_ref, sp_ref, sc_ref, sink_ref,
             o_ref, qr_ref, kr_ref, p_ref, psink_ref):
        prev_slot, bias = _fold_masks(pl.program_id(0) == 0)
        c_band, s_band = _band(cp_ref, cc_ref), _band(sp_ref, sc_ref)
        for j in range(KV_W // 128):
            cols = slice(j * 128, (j + 1) * 128)
            k_slab = _rope(_band(kp_ref, kc_ref, cols), c_band, s_band)
            kr_ref[:, cols] = k_slab[CHUNK:].astype(BF16)
            v_slab = _band(vp_ref, vc_ref, cols).astype(F32)
            for g in (2 * j, 2 * j + 1):
                k2 = _head_pair_operand(k_slab, g)
                v2 = _head_pair_operand(v_slab, g)
                pairs = [g * PAIRS_PER_KV + r for r in range(PAIRS_PER_KV)]
                qps = []
                for pair in pairs:
                    lanes = slice(pair * 128, (pair + 1) * 128)
                    qps.append((_rope(q_ref[:, lanes], cc_ref[...], sc_ref[...]) * (HEAD ** -0.5)).astype(BF16))
                    qr_ref[:, lanes] = qps[-1]
                s2 = _nt(k2, jnp.concatenate(qps, axis=0))
                pcols = []
                for r, pair in enumerate(pairs):
                    ps = []
                    for e in range(2):
                        head = 2 * pair + e
                        s = _fold(s2[e * 2 * CHUNK:(e + 1) * 2 * CHUNK, r * 128:(r + 1) * 128], prev_slot) + bias
                        p, psink = _softmax_sink(s, sink_ref[head], 0)
                        p = p.astype(BF16)
                        p_ref[head] = p
                        psink_ref[head:head + 1, :] = psink
                        ps.append(_unfold(p, prev_slot))
                    pcols.append(jnp.concatenate(ps, axis=0))
                o = _tn(jnp.concatenate(pcols, axis=1), v2).astype(BF16)
                for r, pair in enumerate(pairs):
                    o_ref[:, pair * 128:(pair + 1) * 128] = o[r * CHUNK:(r + 1) * CHUNK]

    table = lambda which, width: pl.BlockSpec((CHUNK, width), lambda i: (which(i), 0))
    return pl.pallas_call(
        body, name="fwd_attn", grid=(nb,),
        in_specs=[pl.BlockSpec((CHUNK, D), lambda i: (i, OFF_Q // D)),
                  pl.BlockSpec((CHUNK, KV_W), lambda i: (prev(i), OFF_K // KV_W)),
                  pl.BlockSpec((CHUNK, KV_W), lambda i: (i, OFF_K // KV_W)),
                  pl.BlockSpec((CHUNK, KV_W), lambda i: (prev(i), OFF_VA // KV_W)),
                  pl.BlockSpec((CHUNK, KV_W), lambda i: (i, OFF_VA // KV_W)),
                  table(prev, 128), table(cur, 128), table(prev, 256), table(cur, 256),
                  pl.BlockSpec(memory_space=pltpu.SMEM)],
        out_specs=[pl.BlockSpec((CHUNK, D), lambda i: (i, 0)), pl.BlockSpec((CHUNK, D), lambda i: (i, 0)),
                   pl.BlockSpec((CHUNK, KV_W), lambda i: (i, 0)),
                   pl.BlockSpec((None, N_Q, CHUNK, CHUNK), lambda i: (i, 0, 0, 0)),
                   pl.BlockSpec((None, N_Q, CHUNK), lambda i: (i, 0, 0))],
        out_shape=[SDS((T, D), BF16), SDS((T, D), BF16), SDS((T, KV_W), BF16),
                   SDS((nb, N_Q, CHUNK, CHUNK), BF16), SDS((nb, N_Q, CHUNK), F32)],
        compiler_params=_params(1),
    )(proj, proj, proj, proj, proj, cos, cos, sin, sin, sinks)


def _row_halves(tm):
    return [slice(0, tm // 2), slice(tm // 2, tm)] if tm % 32 == 0 else [slice(0, tm)]


def _fwd_mix(a, att, proj, x, wa, wb, wo, g1, g2):
    T = x.shape[0]
    tm = min(T, 512)
    half = D // 2

    def body(a_ref, att_ref, ga0, ga1, gb0, gb1, x_ref, wa_ref, wb_ref, wo_ref, g1_ref, g2_ref,
             mg_ref, a2_ref, b2_ref, mix_ref, x1_ref, hf_ref):
        for rows in _row_halves(tm):
            a2 = _nn(a_ref[rows, :], wa_ref[...])
            b2 = _nn(att_ref[rows, :], wb_ref[...])
            ga = jnp.concatenate([ga0[rows, :], ga1[rows, :]], axis=1).astype(F32)
            gb = jnp.concatenate([gb0[rows, :], gb1[rows, :]], axis=1).astype(F32)
            merged = (_sigmoid(ga) * a2 + _sigmoid(gb) * b2).astype(BF16)
            a2_ref[rows, :] = a2.astype(BF16)
            b2_ref[rows, :] = b2.astype(BF16)
            mg_ref[rows, :] = merged
            mix = _nn(merged, wo_ref[...])
            mix_ref[rows, :] = mix
            _, mh = _rms_stats(mix)
            x1 = x_ref[rows, :] + mh * g1_ref[...]
            x1_ref[rows, :] = x1
            _, xh = _rms_stats(x1)
            hf_ref[rows, :] = (xh * g2_ref[...]).astype(BF16)

    row = lambda i: (i, 0)
    const = lambda i: (0, 0)
    gspec = lambda off: pl.BlockSpec((tm, half), lambda i: (i, off // half))
    return pl.pallas_call(
        body, name="fwd_mix", grid=(T // tm,),
        in_specs=[pl.BlockSpec((tm, D), row), pl.BlockSpec((tm, D), row),
                  gspec(OFF_GA), gspec(OFF_GA + half), gspec(OFF_GB), gspec(OFF_GB + half),
                  pl.BlockSpec((tm, D), row), _resident((D, D)), _resident((D, D)),
                  _resident((D, D)), pl.BlockSpec((1, D), const), pl.BlockSpec((1, D), const)],
        out_specs=[pl.BlockSpec((tm, D), row)] * 6,
        out_shape=[SDS((T, D), BF16), SDS((T, D), BF16), SDS((T, D), BF16), SDS((T, D), F32), SDS((T, D), F32),
                   SDS((T, D), BF16)],
        compiler_params=_params(1),
    )(a, att, proj, proj, proj, proj, x, wa, wb, wo, g1, g2)


FF_SPLIT = N_DEV
FF_TILE = D_FF // FF_SPLIT


def _fwd_ff(hf, wfi3, wfo, x1, tgt, g3):
    T = hf.shape[0]
    tm = min(T, 512)

    def body(hf_ref, wfi_ref, wfo_ref, x1_ref, tgt_ref, g3_ref, f_ref, dy_ref, dff_ref, dg3_ref, loss_ref, r_s):
        @pl.when(pl.program_id(0) == 0)
        def _():
            dg3_ref[...] = jnp.zeros_like(dg3_ref)
            loss_ref[...] = jnp.zeros_like(loss_ref)

        hf_t = hf_ref[...]
        for s in range(FF_SPLIT):
            cols = slice(s * FF_TILE, (s + 1) * FF_TILE)
            f = _nn(hf_t, wfi_ref[s]).astype(BF16)
            f_ref[:, cols] = f
            rl = jnp.maximum(f.astype(F32), 0.0)
            r_s[:, cols] = (rl * rl).astype(BF16)
        r3, fh = _rms_stats(_nn(r_s[...], wfo_ref[...]))
        e = x1_ref[...] + fh * g3_ref[...] - tgt_ref[...]
        loss_ref[...] += jnp.sum(e * e) * (0.5 / D)
        dy = e * (1.0 / D)
        dy_ref[...] = dy
        dg3_ref[...] += _colsum(dy * fh)
        dff_ref[...] = _rms_bwd(dy, fh, r3, g3_ref[...]).astype(BF16)

    row = lambda i: (i, 0)
    const = lambda i: (0, 0)
    return pl.pallas_call(
        body, name="fwd_ff", grid=(T // tm,),
        in_specs=[pl.BlockSpec((tm, D), row), _resident((FF_SPLIT, D, FF_TILE)), _resident((D_FF, D)),
                  pl.BlockSpec((tm, D), row),
                  pl.BlockSpec((tm, D), row), pl.BlockSpec((1, D), const)],
        out_specs=[pl.BlockSpec((tm, D_FF), row), pl.BlockSpec((tm, D), row),
                   pl.BlockSpec((tm, D), row), pl.BlockSpec((1, D), const), pl.BlockSpec((1, 128), const)],
        out_shape=[SDS((T, D_FF), BF16), SDS((T, D), F32), SDS((T, D), BF16), SDS((1, D), F32), SDS((1, 128), F32)],
        scratch_shapes=[pltpu.VMEM((tm, D_FF), BF16)],
        compiler_params=_params(1),
    )(hf, wfi3, wfo, x1, tgt, g3)


def _bwd_ff(dff, f, wfi3, wfo, x1, dy, mix, g1, g2):
    T = dff.shape[0]
    tm = min(T, 512)

    def body(dff_ref, f_ref, wfi_ref, wfo_ref, x1_ref, dy_ref, mix_ref, g1_ref, g2_ref,
             df_ref, dx1_ref, dmix_ref, dg2_ref, dg1_ref):
        @pl.when(pl.program_id(0) == 0)
        def _():
            dg2_ref[...] = jnp.zeros_like(dg2_ref)
            dg1_ref[...] = jnp.zeros_like(dg1_ref)

        dff_t = dff_ref[...]
        dhf = None
        for s in range(FF_SPLIT):
            cols = slice(s * FF_TILE, (s + 1) * FF_TILE)
            dr = _nt(dff_t, wfo_ref[cols, :])
            df = (dr * (2.0 * jnp.maximum(f_ref[:, cols].astype(F32), 0.0))).astype(BF16)
            df_ref[:, cols] = df
            part = _nt(df, wfi_ref[s])
            dhf = part if dhf is None else dhf + part
        r2, xh = _rms_stats(x1_ref[...])
        dg2_ref[...] += _colsum(dhf * xh)
        dx1 = dy_ref[...] + _rms_bwd(dhf, xh, r2, g2_ref[...])
        dx1_ref[...] = dx1
        r1, mh = _rms_stats(mix_ref[...])
        dg1_ref[...] += _colsum(dx1 * mh)
        dmix_ref[...] = _rms_bwd(dx1, mh, r1, g1_ref[...]).astype(BF16)

    row = lambda i: (i, 0)
    const = lambda i: (0, 0)
    return pl.pallas_call(
        body, name="bwd_ff", grid=(T // tm,),
        in_specs=[pl.BlockSpec((tm, D), row), pl.BlockSpec((tm, D_FF), row),
                  _resident((FF_SPLIT, D, FF_TILE)), _resident((D_FF, D)),
                  pl.BlockSpec((tm, D), row), pl.BlockSpec((tm, D), row), pl.BlockSpec((tm, D), row),
                  pl.BlockSpec((1, D), const), pl.BlockSpec((1, D), const)],
        out_specs=[pl.BlockSpec((tm, D_FF), row), pl.BlockSpec((tm, D), row),
                   pl.BlockSpec((tm, D), row), pl.BlockSpec((1, D), const), pl.BlockSpec((1, D), const)],
        out_shape=[SDS((T, D_FF), BF16), SDS((T, D), F32), SDS((T, D), BF16), SDS((1, D), F32), SDS((1, D), F32)],
        compiler_params=_params(1),
    )(dff, f, wfi3, wfo, x1, dy, mix, g1, g2)


def _wgrad_ff(hf, df, f, dff):
    T = hf.shape[0]
    tt = min(T, 2048)
    slabs = 2
    wide = slabs * FF_TILE

    def body(hf_ref, df_ref, f_ref, dff_ref, dwfi_ref, dwfo_ref, acc_i, acc_o):
        t = pl.program_id(1)

        @pl.when(t == 0)
        def _():
            acc_i[...] = jnp.zeros_like(acc_i)
            acc_o[...] = jnp.zeros_like(acc_o)

        acc_i[...] += _tn(hf_ref[...], df_ref[...])
        rl = jnp.maximum(f_ref[...].astype(F32), 0.0)
        acc_o[...] += _tn((rl * rl).astype(BF16), dff_ref[...])

        @pl.when(t == T // tt - 1)
        def _():
            for s in range(slabs):
                dwfi_ref[s] = acc_i[:, s * FF_TILE:(s + 1) * FF_TILE].astype(BF16)
            dwfo_ref[...] = acc_o[...].astype(BF16)

    return pl.pallas_call(
        body, name="wgrad_ff", grid=(D_FF // wide, T // tt),
        in_specs=[pl.BlockSpec((tt, D), lambda p, t: (t, 0)), pl.BlockSpec((tt, wide), lambda p, t: (t, p)),
                  pl.BlockSpec((tt, wide), lambda p, t: (t, p)), pl.BlockSpec((tt, D), lambda p, t: (t, 0))],
        out_specs=[pl.BlockSpec((slabs, D, FF_TILE), lambda p, t: (p, 0, 0)), pl.BlockSpec((wide, D), lambda p, t: (p, 0))],
        out_shape=[SDS((FF_SPLIT, D, FF_TILE), BF16), SDS((D_FF, D), BF16)],
        scratch_shapes=[pltpu.VMEM((D, wide), F32), pltpu.VMEM((wide, D), F32)],
        compiler_params=_params(2),
    )(hf, df, f, dff)


def _bwd_mix(dmix, proj, a2, b2, merged, a, att, wo, wa, wb, after=None):
    T = dmix.shape[0]
    tm = min(T, 512)
    half = D // 2
    last = T // tm - 1

    def body(dmix_ref, ga0, ga1, gb0, gb1, a2_ref, b2_ref, mg_ref, a_ref, att_ref, wo_ref, wa_ref, wb_ref,
             dg_ref, da_ref, datt_ref, dwo_ref, dwa_ref, dwb_ref, acc, stage, sem):
        t = pl.program_id(0)

        @pl.when(t == 0)
        def _():
            acc[...] = jnp.zeros_like(acc)

        dmix_t = dmix_ref[...]
        dmg = _nt(dmix_t, wo_ref[...])
        sa = _sigmoid(jnp.concatenate([ga0[...], ga1[...]], axis=1).astype(F32))
        sb = _sigmoid(jnp.concatenate([gb0[...], gb1[...]], axis=1).astype(F32))
        da2 = (dmg * sa).astype(BF16)
        db2 = (dmg * sb).astype(BF16)
        dg_ref[:, :D] = (dmg * a2_ref[...].astype(F32) * (sa * (1.0 - sa))).astype(BF16)
        dg_ref[:, D:] = (dmg * b2_ref[...].astype(F32) * (sb * (1.0 - sb))).astype(BF16)
        da_ref[...] = _nt(da2, wa_ref[...]).astype(BF16)
        datt_ref[...] = _nt(db2, wb_ref[...]).astype(BF16)
        acc[0] += _tn(mg_ref[...], dmix_t)
        acc[1] += _tn(a_ref[...], da2)
        acc[2] += _tn(att_ref[...], db2)

        @pl.when(t == last)
        def _():
            for k, dw_ref in enumerate((dwo_ref, dwa_ref, dwb_ref)):
                stage[...] = acc[k].astype(BF16)
                out = pltpu.make_async_copy(stage, dw_ref, sem)
                out.start()
                out.wait()

    row = lambda i: (i, 0)
    gspec = lambda off: pl.BlockSpec((tm, half), lambda i: (i, off // half))
    body, dep_specs, deps = _after(body, 13, after)
    return pl.pallas_call(
        body, name="bwd_mix", grid=(T // tm,),
        in_specs=[pl.BlockSpec((tm, D), row), gspec(OFF_GA), gspec(OFF_GA + half), gspec(OFF_GB), gspec(OFF_GB + half)]
        + [pl.BlockSpec((tm, D), row)] * 5 + [_resident((D, D))] * 3 + dep_specs,
        out_specs=[pl.BlockSpec((tm, 2 * D), row), pl.BlockSpec((tm, D), row), pl.BlockSpec((tm, D), row)] + [_ANY] * 3,
        out_shape=[SDS((T, 2 * D), BF16), SDS((T, D), BF16), SDS((T, D), BF16)] + [SDS((D, D), BF16)] * 3,
        scratch_shapes=[pltpu.VMEM((3, D, D), F32), pltpu.VMEM((D, D), BF16), pltpu.SemaphoreType.DMA],
        compiler_params=_params(1),
    )(dmix, proj, proj, proj, proj, a2, b2, merged, a, att, wo, wa, wb, *deps)


def _bwd_attn(qr, kr, probs, psink, proj, cos, sin, datt, after=None):
    T = proj.shape[0]
    nb = T // CHUNK
    cur = lambda i: jnp.minimum(i, nb - 1)
    prev = lambda i: jnp.maximum(jnp.minimum(i, nb - 1) - 1, 0)

    def body(q_ref, kp_ref, kc_ref, vp_ref, vc_ref, cp_ref, cc_ref, sp_ref, sc_ref, p_ref, psink_ref, do_ref,
             dq_ref, dkv_ref, dsink_ref, carry_k, carry_v):
        i = pl.program_id(0)

        @pl.when(i == 0)
        def _():
            carry_k[...] = jnp.zeros_like(carry_k)
            carry_v[...] = jnp.zeros_like(carry_v)
            dsink_ref[...] = jnp.zeros_like(dsink_ref)

        @pl.when(i < nb)
        def _():
            prev_slot, _ = _fold_masks(i == 0)
            c_band, s_band = _band(cp_ref, cc_ref), _band(sp_ref, sc_ref)
            lane = lax.broadcasted_iota(jnp.int32, (1, 128), 1)
            dsink = jnp.zeros((1, 128), F32)
            for j in range(KV_W // 128):
                cols = slice(j * 128, (j + 1) * 128)
                k_slab = _band(kp_ref, kc_ref, cols).astype(F32)
                v_slab = _band(vp_ref, vc_ref, cols).astype(F32)
                dk_slab = jnp.zeros((2 * CHUNK, 128), F32)
                dv_slab = jnp.zeros((2 * CHUNK, 128), F32)
                for g in (2 * j, 2 * j + 1):
                    k2 = _head_pair_operand(k_slab, g)
                    v2 = _head_pair_operand(v_slab, g)
                    pairs = [g * PAIRS_PER_KV + r for r in range(PAIRS_PER_KV)]
                    q_stack = jnp.concatenate([q_ref[:, pr * 128:(pr + 1) * 128] for pr in pairs], axis=0)
                    do_stack = jnp.concatenate([do_ref[:, pr * 128:(pr + 1) * 128] for pr in pairs], axis=0)
                    dp2 = _nt(v2, do_stack)
                    pcols, dscols = [], []
                    for r, pair in enumerate(pairs):
                        ps, dss = [], []
                        for e in range(2):
                            head = 2 * pair + e
                            p_b = p_ref[head]
                            p = p_b.astype(F32)
                            dp = _fold(dp2[e * 2 * CHUNK:(e + 1) * 2 * CHUNK, r * 128:(r + 1) * 128], prev_slot)
                            delta = jnp.sum(p * dp, axis=0, keepdims=True)
                            ps.append(_unfold(p_b, prev_slot))
                            dss.append(_unfold((p * (dp - delta)).astype(BF16), prev_slot))
                            dsink = dsink + jnp.where(lane == head, -jnp.sum(psink_ref[head:head + 1, :] * delta), 0.0)
                        pcols.append(jnp.concatenate(ps, axis=0))
                        dscols.append(jnp.concatenate(dss, axis=0))
                    ds2 = jnp.concatenate(dscols, axis=1)
                    dq = _tn(ds2, k2) * (HEAD ** -0.5)
                    for r, pair in enumerate(pairs):
                        dq_ref[:, pair * 128:(pair + 1) * 128] = _rope_bwd(
                            dq[r * CHUNK:(r + 1) * CHUNK], cc_ref[...], sc_ref[...]).astype(BF16)
                    dk_slab = dk_slab + _head_pair_gradient(_nn(ds2, q_stack), g)
                    dv_slab = dv_slab + _head_pair_gradient(_nn(jnp.concatenate(pcols, axis=1), do_stack), g)
                dk_slab = _rope_bwd(dk_slab, c_band, s_band)
                vcols = slice(KV_W + j * 128, KV_W + (j + 1) * 128)
                dkv_ref[:, cols] = (carry_k[:, cols] + dk_slab[:CHUNK]).astype(BF16)
                dkv_ref[:, vcols] = (carry_v[:, cols] + dv_slab[:CHUNK]).astype(BF16)
                carry_k[:, cols] = dk_slab[CHUNK:]
                carry_v[:, cols] = dv_slab[CHUNK:]
            dsink_ref[...] += dsink

        @pl.when(i == nb)
        def _():
            dkv_ref[:, :KV_W] = carry_k[...].astype(BF16)
            dkv_ref[:, KV_W:] = carry_v[...].astype(BF16)

    table = lambda which, width: pl.BlockSpec((CHUNK, width), lambda i: (which(i), 0))
    body, dep_specs, deps = _after(body, 12, after)
    return pl.pallas_call(
        body, name="bwd_attn", grid=(nb + 1,),
        in_specs=[pl.BlockSpec((CHUNK, D), lambda i: (cur(i), 0)),
                  pl.BlockSpec((CHUNK, KV_W), lambda i: (prev(i), 0)),
                  pl.BlockSpec((CHUNK, KV_W), lambda i: (cur(i), 0)),
                  pl.BlockSpec((CHUNK, KV_W), lambda i: (prev(i), OFF_VA // KV_W)),
                  pl.BlockSpec((CHUNK, KV_W), lambda i: (cur(i), OFF_VA // KV_W)),
                  table(prev, 128), table(cur, 128), table(prev, 256), table(cur, 256),
                  pl.BlockSpec((None, N_Q, CHUNK, CHUNK), lambda i: (cur(i), 0, 0, 0)),
                  pl.BlockSpec((None, N_Q, CHUNK), lambda i: (cur(i), 0, 0)),
                  pl.BlockSpec((CHUNK, D), lambda i: (cur(i), 0))] + dep_specs,
        out_specs=[pl.BlockSpec((CHUNK, D), lambda i: (cur(i), 0)),
                   pl.BlockSpec((CHUNK, 2 * KV_W), lambda i: (jnp.maximum(i - 1, 0), 0)),
                   pl.BlockSpec((1, 128), lambda i: (0, 0))],
        out_shape=[SDS((T, D), BF16), SDS((T, 2 * KV_W), BF16), SDS((1, 128), F32)],
        scratch_shapes=[pltpu.VMEM((CHUNK, KV_W), F32), pltpu.VMEM((CHUNK, KV_W), F32)],
        compiler_params=_params(1),
    )(qr, kr, kr, proj, proj, cos, cos, sin, sin, probs, psink, datt, *deps)


def _bwd_sgu(proj, da, lng, lnb, ws, bst):
    T = proj.shape[0]
    tc = min(T, 512)
    nsteps = T // tc

    def body(u_ref, vs_ref, da_ref, lng_ref, lnb_ref, ws_ref, bst_ref,
             duv_ref, dws_ref, dbs_ref, dlng_ref, dlnb_ref, dvn_s, dgu_s, dmx_sum):
        i = pl.program_id(0)

        @pl.when(i == 0)
        def _():
            dws_ref[...] = jnp.zeros_like(dws_ref)
            dlng_ref[...] = jnp.zeros_like(dlng_ref)
            dlnb_ref[...] = jnp.zeros_like(dlnb_ref)
            dmx_sum[...] = jnp.zeros_like(dmx_sum)

        u, vs, gu, tu, tv, rstd, vhat, vn = _sgu_forward_parts(u_ref, vs_ref, lng_ref, lnb_ref)
        da = da_ref[...].astype(F32)
        for g in range(GROUPS):
            wm = _masked_ws(ws_ref, g)
            cols = slice(g * CHUNK, (g + 1) * CHUNK)
            dws = jnp.zeros((CHUNK, CHUNK), F32)
            dsum = jnp.zeros((CHUNK, CHUNK), F32)
            for c in range(tc // CHUNK):
                rows = slice(c * CHUNK, (c + 1) * CHUNK)
                vn_cg = vn[rows, cols]
                mixed = _nn(wm, vn_cg) + bst_ref[:, g:g + 1]
                dgu_s[rows, cols] = da[rows, cols] * mixed
                dmx = da[rows, cols] * gu[rows, cols]
                dmxb = dmx.astype(BF16)
                dws = dws + _nt(dmxb, vn_cg)
                dsum = dsum + dmx
                dvn_s[rows, cols] = _tn(wm, dmxb)
            dws_ref[g] += dws
            dmx_sum[:, cols] += dsum
        dvn = dvn_s[...]
        dlng_ref[...] += _colsum(dvn * vhat)
        dlnb_ref[...] += _colsum(dvn)
        dvh = dvn * lng_ref[...]
        dgv = rstd * (dvh - jnp.mean(dvh, axis=-1, keepdims=True) - vhat * jnp.mean(dvh * vhat, axis=-1, keepdims=True))
        duv_ref[:, :D] = (dgu_s[...] * _gelu_grad(u, tu)).astype(BF16)
        duv_ref[:, D:] = (dgv * _gelu_grad(vs, tv)).astype(BF16)

        @pl.when(i == nsteps - 1)
        def _():
            row = lax.broadcasted_iota(jnp.int32, (CHUNK, CHUNK), 0)
            col = lax.broadcasted_iota(jnp.int32, (CHUNK, CHUNK), 1)
            for g in range(GROUPS):
                dws_ref[g] = jnp.where(row >= col, dws_ref[g], 0.0)
                dbs_ref[g:g + 1, :] = _colsum(dmx_sum[:, g * CHUNK:(g + 1) * CHUNK].T)

    const2 = lambda i: (0, 0)
    return pl.pallas_call(
        body, name="bwd_sgu", grid=(nsteps,),
        in_specs=[pl.BlockSpec((tc, D), lambda i: (i, 0)), pl.BlockSpec((tc, D), lambda i: (i, 1)),
                  pl.BlockSpec((tc, D), lambda i: (i, 0)), pl.BlockSpec((1, D), const2), pl.BlockSpec((1, D), const2),
                  pl.BlockSpec((GROUPS, CHUNK, CHUNK), lambda i: (0, 0, 0)), pl.BlockSpec((CHUNK, GROUPS), const2)],
        out_specs=[pl.BlockSpec((tc, 2 * D), lambda i: (i, 0)), pl.BlockSpec((GROUPS, CHUNK, CHUNK), lambda i: (0, 0, 0)),
                   pl.BlockSpec((GROUPS, CHUNK), const2), pl.BlockSpec((1, D), const2), pl.BlockSpec((1, D), const2)],
        out_shape=[SDS((T, 2 * D), BF16), SDS((GROUPS, CHUNK, CHUNK), F32), SDS((GROUPS, CHUNK), F32),
                   SDS((1, D), F32), SDS((1, D), F32)],
        scratch_shapes=[pltpu.VMEM((tc, D), F32), pltpu.VMEM((tc, D), F32), pltpu.VMEM((CHUNK, D), F32)],
        compiler_params=_params(1),
    )(proj, proj, da, lng, lnb, ws, bst)


IN_SEG_WIDTHS = (2 * D, D, 2 * N_KV * HEAD, 2 * D)


def _resident(shape):
    return pl.BlockSpec(shape, lambda *_: (0,) * len(shape), pipeline_mode=pl.Buffered(1))


def _bwd_in(duv, dq, dkv, dg, win_t, x, dx1, g0, after=None):
    T = x.shape[0]
    tm = min(T, 512)

    def body(duv_ref, dq_ref, dkv_ref, dg_ref, w_ref, x_ref, dx1_ref, g0_ref, gx_ref, dg0_ref):
        @pl.when(pl.program_id(0) == 0)
        def _():
            dg0_ref[...] = jnp.zeros_like(dg0_ref)

        dh, off = None, 0
        for ref, width in zip((duv_ref, dq_ref, dkv_ref, dg_ref), IN_SEG_WIDTHS):
            part = _nn(ref[...], w_ref[off:off + width, :])
            dh = part if dh is None else dh + part
            off += width
        r0, xh = _rms_stats(x_ref[...])
        dg0_ref[...] += _colsum(dh * xh)
        gx_ref[...] = dx1_ref[...] + _rms_bwd(dh, xh, r0, g0_ref[...])

    row = lambda i: (i, 0)
    body, dep_specs, deps = _after(body, 8, after)
    return pl.pallas_call(
        body, name="bwd_in", grid=(T // tm,),
        in_specs=[pl.BlockSpec((tm, w), row) for w in IN_SEG_WIDTHS] + [
            _resident((IN_W, D)), pl.BlockSpec((tm, D), row), pl.BlockSpec((tm, D), row),
            pl.BlockSpec((1, D), lambda i: (0, 0))] + dep_specs,
        out_specs=[pl.BlockSpec((tm, D), row), pl.BlockSpec((1, D), lambda i: (0, 0))],
        out_shape=[SDS((T, D), F32), SDS((1, D), F32)],
        compiler_params=_params(1),
    )(duv, dq, dkv, dg, win_t, x, dx1, g0, *deps)


def _wgrad_rows(h, segs, first_row, into, name):
    T = h.shape[0]
    tt = min(T, 2048)
    widths = [s.shape[1] for s in segs]
    rows = sum(widths)
    n_in = 1 + len(segs) + (into is not None)

    def body(*refs):
        h_ref, seg_refs = refs[0], refs[1:1 + len(segs)]
        dw_ref, acc, stage, sem = refs[n_in], refs[n_in + 1], refs[n_in + 2], refs[n_in + 3]
        t = pl.program_id(0)

        @pl.when(t == 0)
        def _():
            acc[...] = jnp.zeros_like(acc)

        off = 0
        for ref, width in zip(seg_refs, widths):
            acc[off:off + width, :] += _tn(ref[...], h_ref[...])
            off += width

        @pl.when(t == T // tt - 1)
        def _():
            stage[...] = acc[...].astype(BF16)
            out = pltpu.make_async_copy(stage, dw_ref.at[pl.ds(first_row, rows)], sem)
            out.start()
            out.wait()

    row = lambda t: (t, 0)
    return pl.pallas_call(
        body, name=name, grid=(T // tt,),
        in_specs=[pl.BlockSpec((tt, D), row)] + [pl.BlockSpec((tt, w), row) for w in widths] + [_ANY] * (into is not None),
        out_specs=_ANY,
        out_shape=SDS((IN_W, D), BF16),
        input_output_aliases={} if into is None else {n_in - 1: 0},
        scratch_shapes=[pltpu.VMEM((rows, D), F32), pltpu.VMEM((rows, D), BF16), pltpu.SemaphoreType.DMA],
        compiler_params=_params(1),
    )(h, *segs, *([] if into is None else [into]))


def _wgrad_in(h, duv, dq, dkv, dg):
    dw = _wgrad_rows(h, [dg], IN_SEG_WIDTHS[0] + IN_SEG_WIDTHS[1] + IN_SEG_WIDTHS[2], None, "wgrad_in_gates")
    dw = _wgrad_rows(h, [duv], 0, dw, "wgrad_in_uv")
    return _wgrad_rows(h, [dq, dkv], IN_SEG_WIDTHS[0], dw, "wgrad_in_qkv")


def _place():
    x, y, c = lax.axis_index("x"), lax.axis_index("y"), lax.axis_index("c")
    return x, y, c, 4 * x + 2 * y + c


def _peers(x, y, c):
    out = []
    for mask in range(1, N_DEV):
        px = 1 - x if mask & 4 else x
        py = 1 - y if mask & 2 else y
        pc = 1 - c if mask & 1 else c
        out.append(((px, py, pc), 4 * px + 2 * py + pc))
    return out


def _all_to_all(arrays, gather, name, after=None):
    n = len(arrays)

    def body(*refs):
        ins, outs = refs[:n], refs[n:2 * n]
        send_sems, recv_sems, local_sems = refs[2 * n:]
        x, y, c, me = _place()
        local, sends, recvs = [], [], []
        for a in range(n):
            src_own = ins[a] if gather[a] else ins[a].at[me]
            local.append(pltpu.make_async_copy(src_own, outs[a].at[me], local_sems.at[a]))
            for k, (peer, pid) in enumerate(_peers(x, y, c)):
                sem = a * (N_DEV - 1) + k
                src = ins[a] if gather[a] else ins[a].at[pid]
                sends.append(pltpu.make_async_remote_copy(
                    src_ref=src, dst_ref=outs[a].at[me], send_sem=send_sems.at[sem], recv_sem=recv_sems.at[sem],
                    device_id=peer, device_id_type=MESH))
                recvs.append(pltpu.make_async_remote_copy(
                    src_ref=src, dst_ref=outs[a].at[pid], send_sem=send_sems.at[sem], recv_sem=recv_sems.at[sem],
                    device_id=peer, device_id_type=MESH))
        for cp in local + sends:
            cp.start()
        for cp in recvs:
            cp.wait_recv()
        for cp in sends:
            cp.wait_send()
        for cp in local:
            cp.wait()

    out_shape = [SDS((N_DEV,) + a.shape if gt else a.shape, a.dtype) for a, gt in zip(arrays, gather)]
    nsem = n * (N_DEV - 1)
    body, dep_specs, deps = _after(body, n, after)
    return pl.pallas_call(
        body, name=name,
        in_specs=[pl.BlockSpec(memory_space=pl.ANY)] * n + dep_specs,
        out_specs=[pl.BlockSpec(memory_space=pl.ANY)] * n,
        out_shape=out_shape,
        scratch_shapes=[pltpu.SemaphoreType.DMA((nsem,)), pltpu.SemaphoreType.DMA((nsem,)), pltpu.SemaphoreType.DMA((n,))],
    )(*arrays, *deps)


_HBM = pl.BlockSpec(memory_space=pltpu.HBM)
_SEM = pl.BlockSpec(memory_space=pltpu.SEMAPHORE)
_EFFECT = pltpu.SideEffectType.DATAFLOW_SIDE_EFFECTING
GATHER = "gather"
SCATTER = "scatter"
SPREAD = "spread"


def _zone_shape(a, mode):
    if mode == GATHER:
        return (N_DEV,) + a.shape
    return (N_DEV - 1,) + (a.shape[1:] if mode == SCATTER else a.shape)


def _start_copies(arrays, modes, name, after=None):
    n = len(arrays)
    zones = [lax.empty(_zone_shape(a, m), a.dtype) for a, m in zip(arrays, modes)]

    def body(*refs):
        ins, lands = refs[:n], refs[n:2 * n]
        send_sems, recv_sems = refs[-2 * n - 3], refs[-2 * n - 2]
        token = refs[-1]
        x, y, c, me = _place()
        for a in range(n):
            for k, (peer, pid) in enumerate(_peers(x, y, c)):
                src = ins[a].at[pid] if modes[a] == SCATTER else ins[a]
                dst = lands[a].at[me] if modes[a] == GATHER else lands[a].at[k]
                pltpu.make_async_remote_copy(src_ref=src, dst_ref=dst, send_sem=send_sems.at[a], recv_sem=recv_sems.at[a],
                                             device_id=peer, device_id_type=MESH).start()
            if modes[a] == GATHER:
                pltpu.make_async_remote_copy(src_ref=ins[a], dst_ref=lands[a].at[me], send_sem=send_sems.at[a],
                                             recv_sem=recv_sems.at[a], device_id=(x, y, c), device_id_type=MESH).start()
        token[...] = jnp.zeros_like(token)

    hbm = lambda a: pltpu.HBM(a.shape, a.dtype)
    sems = pltpu.SemaphoreType.DMA((n,))
    extra = [] if after is None else [after]
    operands = [pltpu.with_memory_space_constraint(a, pltpu.HBM) for a in list(arrays) + zones]
    res = pl.pallas_call(
        body, name=name,
        out_shape=(sems, sems, *[hbm(a) for a in arrays], *[hbm(z) for z in zones], SDS((8, 128), F32)),
        in_specs=[_HBM] * (2 * n) + [_ANY] * len(extra),
        out_specs=(_SEM, _SEM, *[_HBM] * (2 * n), pl.BlockSpec(memory_space=pltpu.VMEM)),
        input_output_aliases={i: 2 + i for i in range(2 * n)},
        compiler_params=pltpu.CompilerParams(has_side_effects=_EFFECT),
    )(*operands, *extra)
    return res[0], res[1], list(res[2:2 + n]), list(res[2 + n:2 + 2 * n]), res[-1]


def _wait_copies(started, after, name, count=N_DEV - 1):
    send_sems, recv_sems, thru, zones, _ = started
    nt, nz = len(thru), len(zones)

    def body(*refs):
        lands = refs[nt:nt + nz]
        send_ref, recv_ref = refs[nt + nz], refs[nt + nz + 1]
        x, y, c, _ = _place()
        for a in range(nz):
            blocks = lands[a].at[pl.ds(0, count)]
            cp = pltpu.make_async_remote_copy(src_ref=blocks, dst_ref=blocks, send_sem=send_ref.at[a], recv_sem=recv_ref.at[a],
                                              device_id=(x, y, 1 - c), device_id_type=MESH)
            cp.wait_send()
            cp.wait_recv()

    hbm = lambda a: pltpu.HBM(a.shape, a.dtype)
    res = pl.pallas_call(
        body, name=name,
        out_shape=tuple(hbm(a) for a in thru + zones),
        in_specs=[_HBM] * (nt + nz) + [_SEM, _SEM, _ANY],
        out_specs=tuple([_HBM] * (nt + nz)),
        input_output_aliases={i: i for i in range(nt + nz)},
        compiler_params=pltpu.CompilerParams(has_side_effects=_EFFECT),
    )(*thru, *zones, send_sems, recv_sems, after)
    return list(res[:nt]), list(res[nt:])


def _split_start(body, arrays, zones, name, after):
    n = len(arrays) + len(zones)
    hbm = lambda a: pltpu.HBM(a.shape, a.dtype)
    sems = pltpu.SemaphoreType.DMA((max(len(zones), 1),))
    extra = [] if after is None else [after]
    operands = [pltpu.with_memory_space_constraint(a, pltpu.HBM) for a in list(arrays) + list(zones)]
    res = pl.pallas_call(
        body, name=name,
        out_shape=(sems, sems, *[hbm(a) for a in operands], SDS((8, 128), F32)),
        in_specs=[_HBM] * n + [_ANY] * len(extra),
        out_specs=(_SEM, _SEM, *[_HBM] * n, pl.BlockSpec(memory_space=pltpu.VMEM)),
        input_output_aliases={i: 2 + i for i in range(n)},
        compiler_params=pltpu.CompilerParams(has_side_effects=_EFFECT),
    )(*operands, *extra)
    return res[0], res[1], list(res[2:2 + len(arrays)]), list(res[2 + len(arrays):2 + n]), res[-1]


def _gather_first_leg(shard, name, after=None):
    zone = lax.empty((N_DEV,) + shard.shape, shard.dtype)
    extra = 0 if after is None else 1

    def body(*refs):
        src, land = refs[0], refs[1]
        send_sem, recv_sem, token = refs[2 + extra], refs[3 + extra], refs[-1]
        x, y, c, me = _place()
        for peer in ((x, y, c), (x, y, 1 - c), (1 - x, y, c), (x, 1 - y, c), (1 - x, 1 - y, c)):
            pltpu.make_async_remote_copy(src_ref=src, dst_ref=land.at[me], send_sem=send_sem.at[0], recv_sem=recv_sem.at[0],
                                         device_id=peer, device_id_type=MESH).start()
        token[...] = jnp.zeros_like(token)

    return _split_start(body, [shard], [zone], name, after)


def _gather_second_leg(zone, name, after=None):
    extra = 0 if after is None else 1

    def body(*refs):
        land = refs[0]
        send_sem, recv_sem, token = refs[1 + extra], refs[2 + extra], refs[-1]
        x, y, c, _ = _place()
        for px, py in ((1 - x, y), (x, 1 - y), (1 - x, 1 - y)):
            slot = 4 * px + 2 * py + c
            pltpu.make_async_remote_copy(src_ref=land.at[slot], dst_ref=land.at[slot], send_sem=send_sem.at[0],
                                         recv_sem=recv_sem.at[0], device_id=(x, y, 1 - c), device_id_type=MESH).start()
        token[...] = jnp.zeros_like(token)

    return _split_start(body, [], [zone], name, after)


UPDATE_BLOCK_ELEMS = 256 * 1024


def _update_rows(R, C):
    fits = [t for t in range(8, R + 1, 8) if R % t == 0 and t * C <= UPDATE_BLOCK_ELEMS]
    whole = [t for t in fits if t % 16 == 0]
    return max(whole or fits)


def _adamw_math(g, w, m, v):
    m2 = ADAM_B1 * m + (1.0 - ADAM_B1) * g
    v2 = ADAM_B2 * v + (1.0 - ADAM_B2) * (g * g)
    m_hat = m2 / (1.0 - ADAM_B1 ** ADAM_STEP)
    v_hat = v2 / (1.0 - ADAM_B2 ** ADAM_STEP)
    delta = -ADAM_LR * (m_hat / (jnp.sqrt(v_hat) + ADAM_EPS) + ADAM_WD * w)
    return delta, m2, v2


def _sum_adamw(parts, w, m, v, name):
    R, C = w.shape
    tr = _update_rows(R, C)

    def body(p_ref, w_ref, m_ref, v_ref, g_ref, d_ref, m2_ref, v2_ref):
        g = p_ref[0]
        for k in range(1, N_DEV):
            g = g + p_ref[k]
        g_ref[...] = g
        d_ref[...], m2_ref[...], v2_ref[...] = _adamw_math(g, w_ref[...], m_ref[...], v_ref[...])

    blk = pl.BlockSpec((tr, C), lambda i: (i, 0))
    return pl.pallas_call(
        body, name=name, grid=(R // tr,),
        in_specs=[pl.BlockSpec((N_DEV, tr, C), lambda i: (0, i, 0)), blk, blk, blk],
        out_specs=[blk] * 4,
        out_shape=[SDS((R, C), F32)] * 4,
        compiler_params=_params(1),
    )(parts, w, m, v)


def _sum_adamw_peers(me, own, parts, w, m, v, name, replicated, also_rows=None):
    R, C = w.shape
    tr = _update_rows(R, C)
    assert also_rows is None or tr == R

    def body(me_ref, own_ref, p_ref, w_ref, m_ref, v_ref, g_ref, d_ref, m2_ref, v2_ref, *extra):
        if replicated:
            mine = me_ref[0]
            g = None
            for j in range(N_DEV):
                k = jnp.maximum(jnp.bitwise_xor(mine, j) - 1, 0)
                term = jnp.where(mine == j, own_ref[...], p_ref[k])
                g = term if g is None else g + term
        else:
            g = own_ref[...].astype(F32)
            for k in range(N_DEV - 1):
                g = g + p_ref[k].astype(F32)
        results = (g,) + _adamw_math(g, w_ref[...], m_ref[...], v_ref[...])
        for ref, val in zip((g_ref, d_ref, m2_ref, v2_ref), results):
            ref[...] = val
        for ref, val in zip(extra, results):
            ref[...] = val[also_rows[0]:also_rows[1]]

    blk = pl.BlockSpec((tr, C), lambda i, me_ref: (i, 0))
    own_spec = blk if replicated else pl.BlockSpec((None, tr, C), lambda i, me_ref: (me_ref[0], i, 0))
    n_also = 0 if also_rows is None else also_rows[1] - also_rows[0]
    also_specs = [pl.BlockSpec((n_also, C), lambda i, me_ref: (0, 0))] * (4 if also_rows else 0)
    return pl.pallas_call(
        body, name=name,
        grid_spec=pltpu.PrefetchScalarGridSpec(
            num_scalar_prefetch=1, grid=(R // tr,),
            in_specs=[own_spec, pl.BlockSpec((N_DEV - 1, tr, C), lambda i, me_ref: (0, i, 0)), blk, blk, blk],
            out_specs=[blk] * 4 + also_specs),
        out_shape=[SDS((R, C), F32)] * 4 + [SDS((n_also, C), F32)] * len(also_specs),
        compiler_params=_params(1),
    )(me, own, parts, w, m, v)


SMALL = ("ln_v_gain", "ln_v_bias", "w_spatial", "b_spatial", "sinks", "norm_mix_post", "norm_ff_pre", "norm_ff_post")
SMALL_ROWS = {"ln_v_gain": 8, "ln_v_bias": 8, "w_spatial": 1024, "b_spatial": 8, "sinks": 8,
              "norm_mix_post": 8, "norm_ff_pre": 8, "norm_ff_post": 8}
SMALL_PACK_ROWS = 1152


def _pack_small(vals):
    rows = []
    for name in SMALL:
        flat = vals[name].reshape(-1)
        pad = SMALL_ROWS[name] * 128 - flat.shape[0]
        if pad:
            flat = jnp.concatenate([flat, jnp.zeros((pad,), F32)])
        rows.append(flat.reshape(SMALL_ROWS[name], 128))
    rows.append(jnp.zeros((SMALL_PACK_ROWS - sum(SMALL_ROWS.values()), 128), F32))
    return jnp.concatenate(rows, axis=0)


def _unpack_small(packed, shapes):
    out, r = {}, 0
    for name in SMALL:
        n = 1
        for s in shapes[name]:
            n *= s
        out[name] = packed[r:r + SMALL_ROWS[name]].reshape(-1)[:n].reshape(shapes[name])
        r += SMALL_ROWS[name]
    return out


def _rope_rows():
    d = jnp.arange(128) % HEAD
    inv = ROPE_THETA ** (-(2.0 * (d % (ROPE // 2))).astype(F32) / ROPE)
    invf = jnp.where(d < ROPE, inv, 0.0).astype(F32).reshape(1, 128)
    sgn = jnp.where(d < ROPE // 2, -1.0, jnp.where(d < ROPE, 1.0, 0.0)).astype(F32).reshape(1, 128)
    return invf, sgn


def kernel(x, positions, w_in, ln_v_gain, ln_v_bias, w_spatial, b_spatial, sinks, w_a, w_b, w_o, norm_mix_pre, norm_mix_post, w_ff_in, w_ff_out, norm_ff_pre, norm_ff_post, loss_target, m_w_in, m_ln_v_gain, m_ln_v_bias, m_w_spatial, m_b_spatial, m_sinks, m_w_a, m_w_b, m_w_o, m_norm_mix_pre, m_norm_mix_post, m_w_ff_in, m_w_ff_out, m_norm_ff_pre, m_norm_ff_post, v_w_in, v_ln_v_gain, v_ln_v_bias, v_w_spatial, v_b_spatial, v_sinks, v_w_a, v_w_b, v_w_o, v_norm_mix_pre, v_norm_mix_post, v_w_ff_in, v_w_ff_out, v_norm_ff_pre, v_norm_ff_post):
    given = dict(locals())
    T = x.shape[1]
    xt = x[0]
    tgt = loss_target[0]
    bst = b_spatial[0].T
    ws = w_spatial[0]

    me = 4 * lax.axis_index("x") + 2 * lax.axis_index("y") + lax.axis_index("c")
    me_arr = me.astype(jnp.int32).reshape(1)

    rest = ("w_a", "w_b", "w_o", "w_ff_in", "w_ff_out")
    shard = {n: given[n][0].astype(BF16) for n in rest}
    g_one = _gather_first_leg(w_in[0].T.astype(BF16), "gather_in_start")
    cos, sin = _rope_tables(positions.astype(F32).reshape(T, 1), *_rope_rows(), after=g_one[-1])
    small_state = [_pack_small({n: given[k + n] for n in SMALL}) for k in ("", "m_", "v_")]
    h = _rms_pre(xt, norm_mix_pre, after=[cos, *small_state, *[shard[n] for n in rest]])
    _, (win8,) = _wait_copies(g_one, h, "gather_in_wait", count=5)
    g_two = _gather_second_leg(win8, "gather_in_pass_start")
    g_rest = _start_copies([shard[n] for n in rest], [GATHER] * len(rest), "gather_rest_start", after=g_two[-1])
    _, (win8,) = _wait_copies(g_two, g_rest[-1], "gather_in_pass_wait", count=3)
    win = win8.reshape(IN_W, D)

    proj = _fwd_in(h, win)
    att, qr, kr, probs, psink = _fwd_attn(proj, cos, sin, sinks[0])
    a = _fwd_sgu(proj, ln_v_gain, ln_v_bias, ws, bst, after=att)
    gw = dict(zip(rest, _wait_copies(g_rest, a, "gather_rest_wait", count=N_DEV)[1]))
    wa, wb, wo = (gw[n].reshape(D, D) for n in ("w_a", "w_b", "w_o"))
    wfi3 = gw["w_ff_in"]
    wfo = gw["w_ff_out"].reshape(D_FF, D)
    merged, a2, b2, mix, x1, hf = _fwd_mix(a, att, proj, xt, wa, wb, wo, norm_mix_post, norm_ff_pre)
    f, dy, dff, dg3, loss_part = _fwd_ff(hf, wfi3, wfo, x1, tgt, norm_ff_post)

    df, dx1, dmix, dg2, dg1 = _bwd_ff(dff, f, wfi3, wfo, x1, dy, mix, norm_mix_post, norm_ff_pre)
    dwfi3, dwfo = _wgrad_ff(hf, df, f, dff)
    own_ff = [dwfi3, dwfo.reshape(N_DEV, D_FF // N_DEV, D)]
    x_ff = _start_copies(own_ff, [SCATTER] * 2, "exchange_ff_start")
    dgate, da, datt, dwo, dwa, dwb = _bwd_mix(dmix, proj, a2, b2, merged, a, att, wo, wa, wb, after=x_ff[-1])
    own_mix = [g.reshape(N_DEV, D // N_DEV, D) for g in (dwa, dwb, dwo)]
    x_mix = _start_copies(own_mix, [SCATTER] * 3, "exchange_mix_start")
    dq, dkv, dsink = _bwd_attn(qr, kr, probs, psink, proj, cos, sin, datt, after=x_mix[-1])
    duv, dws, dbs, dlng, dlnb = _bwd_sgu(proj, da, ln_v_gain, ln_v_bias, ws, bst)
    small_grads = {"ln_v_gain": dlng, "ln_v_bias": dlnb, "w_spatial": dws, "b_spatial": dbs, "sinks": dsink[:, :N_Q],
                   "norm_mix_post": dg1, "norm_ff_pre": dg2, "norm_ff_post": dg3}
    x_small = _start_copies([_pack_small(small_grads)], [SPREAD], "exchange_small_start")
    dwin = _wgrad_in(h, duv, dq, dkv, dgate)
    own_in = [dwin.reshape(N_DEV, IN_W // N_DEV, D)]
    x_in = _start_copies(own_in, [SCATTER], "exchange_in_start", after=x_small[-1])
    grad_x, dg0 = _bwd_in(duv, dq, dkv, dgate, win, xt, dx1, norm_mix_pre, after=x_in[-1])

    results = {}

    def update(n, own, parts, transposed=False):
        state = [given[k + n][0].T if transposed else given[k + n][0] for k in ("", "m_", "v_")]
        res = _sum_adamw_peers(me_arr, own, parts, *state, "adamw_" + n, False)
        results[n] = [(r.T if transposed else r).reshape(given[n].shape) for r in res]

    own_ff, p_ff = _wait_copies(x_ff, grad_x, "exchange_ff_wait")
    update("w_ff_in", own_ff[0], p_ff[0])
    update("w_ff_out", own_ff[1], p_ff[1])
    own_mix, p_mix = _wait_copies(x_mix, results["w_ff_out"][0], "exchange_mix_wait")
    for n, own, parts in zip(("w_a", "w_b", "w_o"), own_mix, p_mix):
        update(n, own, parts)
    tail = jnp.concatenate([dg0.reshape(8, 128), jnp.tile(loss_part, (8, 1))], axis=0)
    (tail_all,) = _all_to_all([tail], [True], "exchange_tail", after=results["w_o"][0])
    dg0_all = tail_all[:, :8]
    own_small, p_small = _wait_copies(x_small, tail_all, "exchange_small_wait")
    own_in, p_in = _wait_copies(x_in, p_small[0], "exchange_in_wait")
    update("w_in", own_in[0], p_in[0], transposed=True)
    first = sum(SMALL_ROWS[n] for n in SMALL[:SMALL.index("w_spatial")])
    packed = _sum_adamw_peers(me_arr, own_small[0], p_small[0], *small_state, "adamw_small", True,
                              also_rows=(first, first + SMALL_ROWS["w_spatial"]))
    shapes = {n: given[n].shape for n in SMALL}
    unpacked = [_unpack_small(p, shapes) for p in packed[:4]]
    for n in SMALL:
        results[n] = [u[n] for u in unpacked]
    results["w_spatial"] = [r.reshape(w_spatial.shape) for r in packed[4:]]
    n = "norm_mix_pre"
    results[n] = [r.reshape(given[n].shape) for r in _sum_adamw(
        dg0_all, given[n].reshape(8, 128), given["m_" + n].reshape(8, 128), given["v_" + n].reshape(8, 128), "adamw_" + n)]

    loss = jnp.sum(tail_all[:, 8, 0])
    order = ("w_in", "ln_v_gain", "ln_v_bias", "w_spatial", "b_spatial", "sinks", "w_a", "w_b", "w_o", "norm_mix_pre",
             "norm_mix_post", "w_ff_in", "w_ff_out", "norm_ff_pre", "norm_ff_post")
    out = [loss, grad_x.reshape(x.shape)]
    for k in range(4):
        out += [results[n][k] for n in order]
    return tuple(out)
```

```python
import jax
import jax.numpy as jnp
from jax import lax
from jax.experimental import pallas as pl
from jax.experimental.pallas import tpu as pltpu

F32 = jnp.float32
BF16 = jnp.bfloat16

N_DEV = 8
D = 1024
D_FF = 4096
IN_W = 5632
CHUNK = 128
GROUPS = 8
HEAD = 64
N_Q = 16
N_KV = 4
ROPE = 16
ROPE_THETA = 500000.0
EPS = 1e-6
OFF_Q, OFF_K, OFF_VA, OFF_GA, OFF_GB = 2048, 3072, 3328, 3584, 4608

ADAM_LR = 0.001
ADAM_B1 = 0.9
ADAM_B2 = 0.999
ADAM_EPS = 1e-08
ADAM_WD = 0.01
ADAM_STEP = 10

VMEM_LIMIT = 62 * 1024 * 1024

SDS = jax.ShapeDtypeStruct
MESH = pl.DeviceIdType.MESH


def _params(n_axes):
    return pltpu.CompilerParams(dimension_semantics=("arbitrary",) * n_axes, vmem_limit_bytes=VMEM_LIMIT)


def _nt(a, b):
    return lax.dot_general(a, b, (((1,), (1,)), ((), ())), preferred_element_type=F32)


def _tn(a, b):
    return lax.dot_general(a, b, (((0,), (0,)), ((), ())), preferred_element_type=F32)


def _nn(a, b):
    return jnp.dot(a, b, preferred_element_type=F32)


def _gelu(x):
    t = jnp.tanh(0.7978845608028654 * (x + 0.044715 * (x * x * x)))
    return 0.5 * x * (1.0 + t), t


def _gelu_grad(x, t):
    return 0.5 * (1.0 + t) + 0.5 * x * (1.0 - t * t) * (0.7978845608028654 * (1.0 + 3.0 * 0.044715 * x * x))


def _sigmoid(x):
    return 1.0 / (1.0 + jnp.exp(-x))


def _rms_stats(v):
    r = lax.rsqrt(jnp.mean(v * v, axis=-1, keepdims=True) + EPS)
    return r, v * r


def _rms_bwd(d, vhat, r, g):
    gd = g * d
    return r * (gd - vhat * jnp.mean(gd * vhat, axis=-1, keepdims=True))


def _colsum(v):
    return jnp.sum(v, axis=0, keepdims=True)


_ANY = pl.BlockSpec(memory_space=pl.ANY)


def _after(body, n_in, after):
    if after is None:
        return body, [], []
    deps = list(after) if isinstance(after, (list, tuple)) else [after]

    def ordered(*refs):
        return body(*refs[:n_in], *refs[n_in + len(deps):])

    return ordered, [_ANY] * len(deps), deps


def _rms_pre(x, g0, after=None):
    T = x.shape[0]
    tm = min(T, 1024)

    def body(x_ref, g_ref, h_ref):
        _, xh = _rms_stats(x_ref[...])
        h_ref[...] = (xh * g_ref[...]).astype(BF16)

    body, dep_specs, deps = _after(body, 2, after)
    return pl.pallas_call(
        body, name="rms_pre", grid=(T // tm,),
        in_specs=[pl.BlockSpec((tm, D), lambda i: (i, 0)), pl.BlockSpec((1, D), lambda i: (0, 0))] + dep_specs,
        out_specs=pl.BlockSpec((tm, D), lambda i: (i, 0)),
        out_shape=SDS((T, D), BF16),
        compiler_params=_params(1),
    )(x, g0, *deps)


def _fwd_in(h, win_t):
    T = h.shape[0]
    tm, tn = min(T, 512), 1408

    def body(h_ref, w_ref, p_ref):
        for j in range(IN_W // tn):
            cols = slice(j * tn, (j + 1) * tn)
            p_ref[:, cols] = _nt(h_ref[...], w_ref[cols, :]).astype(BF16)

    return pl.pallas_call(
        body, name="fwd_in", grid=(T // tm,),
        in_specs=[pl.BlockSpec((tm, D), lambda i: (i, 0)), _resident((IN_W, D))],
        out_specs=pl.BlockSpec((tm, IN_W), lambda i: (i, 0)),
        out_shape=SDS((T, IN_W), BF16),
        compiler_params=_params(1),
    )(h, win_t)


def _sgu_forward_parts(u_ref, vs_ref, lng_ref, lnb_ref):
    u = u_ref[...].astype(F32)
    vs = vs_ref[...].astype(F32)
    gu, tu = _gelu(u)
    gv, tv = _gelu(vs)
    mu = jnp.mean(gv, axis=-1, keepdims=True)
    dv = gv - mu
    rstd = lax.rsqrt(jnp.mean(dv * dv, axis=-1, keepdims=True) + EPS)
    vhat = dv * rstd
    vn = (vhat * lng_ref[...] + lnb_ref[...]).astype(BF16)
    return u, vs, gu, tu, tv, rstd, vhat, vn


def _masked_ws(ws_ref, g):
    row = lax.broadcasted_iota(jnp.int32, (CHUNK, CHUNK), 0)
    col = lax.broadcasted_iota(jnp.int32, (CHUNK, CHUNK), 1)
    return jnp.where(row >= col, ws_ref[g], 0.0).astype(BF16)


def _fwd_sgu(proj, lng, lnb, ws, bst, after=None):
    T = proj.shape[0]
    tc = min(T, 512)

    def body(u_ref, vs_ref, lng_ref, lnb_ref, ws_ref, bst_ref, a_ref):
        _, _, gu, _, _, _, _, vn = _sgu_forward_parts(u_ref, vs_ref, lng_ref, lnb_ref)
        for g in range(GROUPS):
            wm = _masked_ws(ws_ref, g)
            cols = slice(g * CHUNK, (g + 1) * CHUNK)
            for c in range(tc // CHUNK):
                rows = slice(c * CHUNK, (c + 1) * CHUNK)
                mixed = _nn(wm, vn[rows, cols]) + bst_ref[:, g:g + 1]
                a_ref[rows, cols] = (gu[rows, cols] * mixed).astype(BF16)

    body, dep_specs, deps = _after(body, 6, after)
    return pl.pallas_call(
        body, name="fwd_sgu", grid=(T // tc,),
        in_specs=[pl.BlockSpec((tc, D), lambda i: (i, 0)), pl.BlockSpec((tc, D), lambda i: (i, 1)),
                  pl.BlockSpec((1, D), lambda i: (0, 0)), pl.BlockSpec((1, D), lambda i: (0, 0)),
                  pl.BlockSpec((GROUPS, CHUNK, CHUNK), lambda i: (0, 0, 0)),
                  pl.BlockSpec((CHUNK, GROUPS), lambda i: (0, 0))] + dep_specs,
        out_specs=pl.BlockSpec((tc, D), lambda i: (i, 0)),
        out_shape=SDS((T, D), BF16),
        compiler_params=_params(1),
    )(proj, proj, lng, lnb, ws, bst, *deps)


def _rope_tables(posf, invf, sgn, after=None):
    T = posf.shape[0]
    tr = min(T, 1024)

    def body(pos_ref, invf_ref, sgn_ref, c_ref, s_ref):
        ang = pos_ref[...] * invf_ref[...]
        c_ref[...] = jnp.cos(ang)
        s = jnp.sin(ang)
        s_ref[:, :128] = jnp.where(sgn_ref[...] < 0.0, -s, 0.0)
        s_ref[:, 128:] = jnp.where(sgn_ref[...] > 0.0, s, 0.0)

    body, dep_specs, deps = _after(body, 3, after)
    return pl.pallas_call(
        body, name="rope_tables", grid=(T // tr,),
        in_specs=[pl.BlockSpec((tr, 1), lambda i: (i, 0)), pl.BlockSpec((1, 128), lambda i: (0, 0)),
                  pl.BlockSpec((1, 128), lambda i: (0, 0))] + dep_specs,
        out_specs=[pl.BlockSpec((tr, 128), lambda i: (i, 0)), pl.BlockSpec((tr, 256), lambda i: (i, 0))],
        out_shape=[SDS((T, 128), F32), SDS((T, 256), F32)],
        compiler_params=_params(1),
    )(posf, invf, sgn, *deps)


def _rope(v, c, s):
    v = v.astype(F32)
    return v * c + pltpu.roll(v, 128 - ROPE // 2, 1) * s[:, :128] + pltpu.roll(v, ROPE // 2, 1) * s[:, 128:]


def _rope_bwd(dv, c, s):
    return dv * c + pltpu.roll(dv * s[:, :128], ROPE // 2, 1) + pltpu.roll(dv * s[:, 128:], 128 - ROPE // 2, 1)


def _fold_masks(first):
    jj = lax.broadcasted_iota(jnp.int32, (CHUNK, CHUNK), 0)
    t = lax.broadcasted_iota(jnp.int32, (CHUNK, CHUNK), 1)
    prev = jj > t
    return prev, jnp.where(prev & first, -1e30, 0.0)


def _fold(band, prev):
    return jnp.where(prev, band[:CHUNK], band[CHUNK:])


def _unfold(folded, prev):
    return jnp.concatenate([jnp.where(prev, folded, 0.0), jnp.where(prev, 0.0, folded)], axis=0)


def _softmax_sink(s, sink, key_axis):
    m = jnp.maximum(jnp.max(s, axis=key_axis, keepdims=True), sink)
    p = jnp.exp(s - m)
    esink = jnp.exp(sink - m)
    inv = 1.0 / (jnp.sum(p, axis=key_axis, keepdims=True) + esink)
    return p * inv, esink * inv


def _head_pair_operand(slab, g):
    lo = lax.broadcasted_iota(jnp.int32, slab.shape, 1) < HEAD
    if g % 2 == 0:
        first = jnp.where(lo, slab, 0.0)
        second = pltpu.roll(first, HEAD, 1)
    else:
        second = jnp.where(lo, 0.0, slab)
        first = pltpu.roll(second, HEAD, 1)
    return jnp.concatenate([first, second], axis=0).astype(BF16)


def _head_pair_gradient(acc, g):
    top, bot = acc[:2 * CHUNK], acc[2 * CHUNK:]
    lo = lax.broadcasted_iota(jnp.int32, top.shape, 1) < HEAD
    if g % 2 == 0:
        return jnp.where(lo, top, 0.0) + pltpu.roll(jnp.where(lo, 0.0, bot), HEAD, 1)
    return pltpu.roll(jnp.where(lo, top, 0.0), HEAD, 1) + jnp.where(lo, 0.0, bot)


PAIRS_PER_KV = N_Q // N_KV // 2
KV_W = N_KV * HEAD


def _band(prev_ref, cur_ref, cols=slice(None)):
    return jnp.concatenate([prev_ref[:, cols], cur_ref[:, cols]], axis=0)


def _fwd_attn(proj, cos, sin, sinks):
    T = proj.shape[0]
    nb = T // CHUNK
    cur = lambda i: i
    prev = lambda i: jnp.maximum(i - 1, 0)

    def body(q_ref, kp_ref, kc_ref, vp_ref, vc_ref, cp_ref, cc_ref, sp_ref, sc_ref, sink_ref,
             o_ref, qr_ref, kr_ref, p_ref, psink_ref):
        prev_slot, bias = _fold_masks(pl.program_id(0) == 0)
        c_band, s_band = _band(cp_ref, cc_ref), _band(sp_ref, sc_ref)
        for j in range(KV_W // 128):
            cols = slice(j * 128, (j + 1) * 128)
            k_slab = _rope(_band(kp_ref, kc_ref, cols), c_band, s_band)
            kr_ref[:, cols] = k_slab[CHUNK:].astype(BF16)
            v_slab = _band(vp_ref, vc_ref, cols).astype(F32)
            for g in (2 * j, 2 * j + 1):
                k2 = _head_pair_operand(k_slab, g)
                v2 = _head_pair_operand(v_slab, g)
                pairs = [g * PAIRS_PER_KV + r for r in range(PAIRS_PER_KV)]
                qps = []
                for pair in pairs:
                    lanes = slice(pair * 128, (pair + 1) * 128)
                    qps.append((_rope(q_ref[:, lanes], cc_ref[...], sc_ref[...]) * (HEAD ** -0.5)).astype(BF16))
                    qr_ref[:, lanes] = qps[-1]
                s2 = _nt(k2, jnp.concatenate(qps, axis=0))
                pcols = []
                for r, pair in enumerate(pairs):
                    ps = []
                    for e in range(2):
                        head = 2 * pair + e
                        s = _fold(s2[e * 2 * CHUNK:(e + 1) * 2 * CHUNK, r * 128:(r + 1) * 128], prev_slot) + bias
                        p, psink = _softmax_sink(s, sink_ref[head], 0)
                        p = p.astype(BF16)
                        p_ref[head] = p
                        psink_ref[head:head + 1, :] = psink
                        ps.append(_unfold(p, prev_slot))
                    pcols.append(jnp.concatenate(ps, axis=0))
                o = _tn(jnp.concatenate(pcols, axis=1), v2).astype(BF16)
                for r, pair in enumerate(pairs):
                    o_ref[:, pair * 128:(pair + 1) * 128] = o[r * CHUNK:(r + 1) * CHUNK]

    table = lambda which, width: pl.BlockSpec((CHUNK, width), lambda i: (which(i), 0))
    return pl.pallas_call(
        body, name="fwd_attn", grid=(nb,),
        in_specs=[pl.BlockSpec((CHUNK, D), lambda i: (i, OFF_Q // D)),
                  pl.BlockSpec((CHUNK, KV_W), lambda i: (prev(i), OFF_K // KV_W)),
                  pl.BlockSpec((CHUNK, KV_W), lambda i: (i, OFF_K // KV_W)),
                  pl.BlockSpec((CHUNK, KV_W), lambda i: (prev(i), OFF_VA // KV_W)),
                  pl.BlockSpec((CHUNK, KV_W), lambda i: (i, OFF_VA // KV_W)),
                  table(prev, 128), table(cur, 128), table(prev, 256), table(cur, 256),
                  pl.BlockSpec(memory_space=pltpu.SMEM)],
        out_specs=[pl.BlockSpec((CHUNK, D), lambda i: (i, 0)), pl.BlockSpec((CHUNK, D), lambda i: (i, 0)),
                   pl.BlockSpec((CHUNK, KV_W), lambda i: (i, 0)),
                   pl.BlockSpec((None, N_Q, CHUNK, CHUNK), lambda i: (i, 0, 0, 0)),
                   pl.BlockSpec((None, N_Q, CHUNK), lambda i: (i, 0, 0))],
        out_shape=[SDS((T, D), BF16), SDS((T, D), BF16), SDS((T, KV_W), BF16),
                   SDS((nb, N_Q, CHUNK, CHUNK), BF16), SDS((nb, N_Q, CHUNK), F32)],
        compiler_params=_params(1),
    )(proj, proj, proj, proj, proj, cos, cos, sin, sin, sinks)


def _fwd_mix(a, att, proj, x, wa, wb, wo, g1, g2):
    T = x.shape[0]
    tm = min(T, 512)
    half = D // 2

    def body(a_ref, att_ref, ga0, ga1, gb0, gb1, x_ref, wa_ref, wb_ref, wo_ref, g1_ref, g2_ref,
             mg_ref, a2_ref, b2_ref, mix_ref, x1_ref, hf_ref):
        a2 = _nn(a_ref[...], wa_ref[...])
        b2 = _nn(att_ref[...], wb_ref[...])
        ga = jnp.concatenate([ga0[...], ga1[...]], axis=1).astype(F32)
        gb = jnp.concatenate([gb0[...], gb1[...]], axis=1).astype(F32)
        merged = (_sigmoid(ga) * a2 + _sigmoid(gb) * b2).astype(BF16)
        a2_ref[...] = a2.astype(BF16)
        b2_ref[...] = b2.astype(BF16)
        mg_ref[...] = merged
        mix = _nn(merged, wo_ref[...])
        mix_ref[...] = mix
        _, mh = _rms_stats(mix)
        x1 = x_ref[...] + mh * g1_ref[...]
        x1_ref[...] = x1
        _, xh = _rms_stats(x1)
        hf_ref[...] = (xh * g2_ref[...]).astype(BF16)

    row = lambda i: (i, 0)
    const = lambda i: (0, 0)
    gspec = lambda off: pl.BlockSpec((tm, half), lambda i: (i, off // half))
    return pl.pallas_call(
        body, name="fwd_mix", grid=(T // tm,),
        in_specs=[pl.BlockSpec((tm, D), row), pl.BlockSpec((tm, D), row),
                  gspec(OFF_GA), gspec(OFF_GA + half), gspec(OFF_GB), gspec(OFF_GB + half),
                  pl.BlockSpec((tm, D), row), _resident((D, D)), _resident((D, D)),
                  _resident((D, D)), pl.BlockSpec((1, D), const), pl.BlockSpec((1, D), const)],
        out_specs=[pl.BlockSpec((tm, D), row)] * 6,
        out_shape=[SDS((T, D), BF16), SDS((T, D), BF16), SDS((T, D), BF16), SDS((T, D), F32), SDS((T, D), F32),
                   SDS((T, D), BF16)],
        compiler_params=_params(1),
    )(a, att, proj, proj, proj, proj, x, wa, wb, wo, g1, g2)


FF_SPLIT = N_DEV
FF_TILE = D_FF // FF_SPLIT


def _fwd_ff(hf, wfi3, wfo, x1, tgt, g3):
    T = hf.shape[0]
    tm = min(T, 512)

    def body(hf_ref, wfi_ref, wfo_ref, x1_ref, tgt_ref, g3_ref, f_ref, dy_ref, dff_ref, dg3_ref, loss_ref, r_s):
        @pl.when(pl.program_id(0) == 0)
        def _():
            dg3_ref[...] = jnp.zeros_like(dg3_ref)
            loss_ref[...] = jnp.zeros_like(loss_ref)

        hf_t = hf_ref[...]
        for s in range(FF_SPLIT):
            cols = slice(s * FF_TILE, (s + 1) * FF_TILE)
            f = _nn(hf_t, wfi_ref[s]).astype(BF16)
            f_ref[:, cols] = f
            rl = jnp.maximum(f.astype(F32), 0.0)
            r_s[:, cols] = (rl * rl).astype(BF16)
        r3, fh = _rms_stats(_nn(r_s[...], wfo_ref[...]))
        e = x1_ref[...] + fh * g3_ref[...] - tgt_ref[...]
        loss_ref[...] += jnp.sum(e * e) * (0.5 / D)
        dy = e * (1.0 / D)
        dy_ref[...] = dy
        dg3_ref[...] += _colsum(dy * fh)
        dff_ref[...] = _rms_bwd(dy, fh, r3, g3_ref[...]).astype(BF16)

    row = lambda i: (i, 0)
    const = lambda i: (0, 0)
    return pl.pallas_call(
        body, name="fwd_ff", grid=(T // tm,),
        in_specs=[pl.BlockSpec((tm, D), row), _resident((FF_SPLIT, D, FF_TILE)), _resident((D_FF, D)),
                  pl.BlockSpec((tm, D), row),
                  pl.BlockSpec((tm, D), row), pl.BlockSpec((1, D), const)],
        out_specs=[pl.BlockSpec((tm, D_FF), row), pl.BlockSpec((tm, D), row),
                   pl.BlockSpec((tm, D), row), pl.BlockSpec((1, D), const), pl.BlockSpec((1, 128), const)],
        out_shape=[SDS((T, D_FF), BF16), SDS((T, D), F32), SDS((T, D), BF16), SDS((1, D), F32), SDS((1, 128), F32)],
        scratch_shapes=[pltpu.VMEM((tm, D_FF), BF16)],
        compiler_params=_params(1),
    )(hf, wfi3, wfo, x1, tgt, g3)


def _bwd_ff(dff, f, wfi3, wfo, x1, dy, mix, g1, g2):
    T = dff.shape[0]
    tm = min(T, 512)

    def body(dff_ref, f_ref, wfi_ref, wfo_ref, x1_ref, dy_ref, mix_ref, g1_ref, g2_ref,
             df_ref, dx1_ref, dmix_ref, dg2_ref, dg1_ref):
        @pl.when(pl.program_id(0) == 0)
        def _():
            dg2_ref[...] = jnp.zeros_like(dg2_ref)
            dg1_ref[...] = jnp.zeros_like(dg1_ref)

        dff_t = dff_ref[...]
        dhf = None
        for s in range(FF_SPLIT):
            cols = slice(s * FF_TILE, (s + 1) * FF_TILE)
            dr = _nt(dff_t, wfo_ref[cols, :])
            df = (dr * (2.0 * jnp.maximum(f_ref[:, cols].astype(F32), 0.0))).astype(BF16)
            df_ref[:, cols] = df
            part = _nt(df, wfi_ref[s])
            dhf = part if dhf is None else dhf + part
        r2, xh = _rms_stats(x1_ref[...])
        dg2_ref[...] += _colsum(dhf * xh)
        dx1 = dy_ref[...] + _rms_bwd(dhf, xh, r2, g2_ref[...])
        dx1_ref[...] = dx1
        r1, mh = _rms_stats(mix_ref[...])
        dg1_ref[...] += _colsum(dx1 * mh)
        dmix_ref[...] = _rms_bwd(dx1, mh, r1, g1_ref[...]).astype(BF16)

    row = lambda i: (i, 0)
    const = lambda i: (0, 0)
    return pl.pallas_call(
        body, name="bwd_ff", grid=(T // tm,),
        in_specs=[pl.BlockSpec((tm, D), row), pl.BlockSpec((tm, D_FF), row),
                  _resident((FF_SPLIT, D, FF_TILE)), _resident((D_FF, D)),
                  pl.BlockSpec((tm, D), row), pl.BlockSpec((tm, D), row), pl.BlockSpec((tm, D), row),
                  pl.BlockSpec((1, D), const), pl.BlockSpec((1, D), const)],
        out_specs=[pl.BlockSpec((tm, D_FF), row), pl.BlockSpec((tm, D), row),
                   pl.BlockSpec((tm, D), row), pl.BlockSpec((1, D), const), pl.BlockSpec((1, D), const)],
        out_shape=[SDS((T, D_FF), BF16), SDS((T, D), F32), SDS((T, D), BF16), SDS((1, D), F32), SDS((1, D), F32)],
        compiler_params=_params(1),
    )(dff, f, wfi3, wfo, x1, dy, mix, g1, g2)


def _wgrad_ff(hf, df, f, dff):
    T = hf.shape[0]
    tt = min(T, 2048)
    slabs = 2
    wide = slabs * FF_TILE

    def body(hf_ref, df_ref, f_ref, dff_ref, dwfi_ref, dwfo_ref, acc_i, acc_o):
        t = pl.program_id(1)

        @pl.when(t == 0)
        def _():
            acc_i[...] = jnp.zeros_like(acc_i)
            acc_o[...] = jnp.zeros_like(acc_o)

        acc_i[...] += _tn(hf_ref[...], df_ref[...])
        rl = jnp.maximum(f_ref[...].astype(F32), 0.0)
        acc_o[...] += _tn((rl * rl).astype(BF16), dff_ref[...])

        @pl.when(t == T // tt - 1)
        def _():
            for s in range(slabs):
                dwfi_ref[s] = acc_i[:, s * FF_TILE:(s + 1) * FF_TILE].astype(BF16)
            dwfo_ref[...] = acc_o[...].astype(BF16)

    return pl.pallas_call(
        body, name="wgrad_ff", grid=(D_FF // wide, T // tt),
        in_specs=[pl.BlockSpec((tt, D), lambda p, t: (t, 0)), pl.BlockSpec((tt, wide), lambda p, t: (t, p)),
                  pl.BlockSpec((tt, wide), lambda p, t: (t, p)), pl.BlockSpec((tt, D), lambda p, t: (t, 0))],
        out_specs=[pl.BlockSpec((slabs, D, FF_TILE), lambda p, t: (p, 0, 0)), pl.BlockSpec((wide, D), lambda p, t: (p, 0))],
        out_shape=[SDS((FF_SPLIT, D, FF_TILE), BF16), SDS((D_FF, D), BF16)],
        scratch_shapes=[pltpu.VMEM((D, wide), F32), pltpu.VMEM((wide, D), F32)],
        compiler_params=_params(2),
    )(hf, df, f, dff)


def _bwd_mix(dmix, proj, a2, b2, merged, a, att, wo, wa, wb, after=None):
    T = dmix.shape[0]
    tm = min(T, 512)
    half = D // 2
    last = T // tm - 1

    def body(dmix_ref, ga0, ga1, gb0, gb1, a2_ref, b2_ref, mg_ref, a_ref, att_ref, wo_ref, wa_ref, wb_ref,
             dg_ref, da_ref, datt_ref, dwo_ref, dwa_ref, dwb_ref, acc, stage, sem):
        t = pl.program_id(0)

        @pl.when(t == 0)
        def _():
            acc[...] = jnp.zeros_like(acc)

        dmix_t = dmix_ref[...]
        dmg = _nt(dmix_t, wo_ref[...])
        sa = _sigmoid(jnp.concatenate([ga0[...], ga1[...]], axis=1).astype(F32))
        sb = _sigmoid(jnp.concatenate([gb0[...], gb1[...]], axis=1).astype(F32))
        da2 = (dmg * sa).astype(BF16)
        db2 = (dmg * sb).astype(BF16)
        dg_ref[:, :D] = (dmg * a2_ref[...].astype(F32) * (sa * (1.0 - sa))).astype(BF16)
        dg_ref[:, D:] = (dmg * b2_ref[...].astype(F32) * (sb * (1.0 - sb))).astype(BF16)
        da_ref[...] = _nt(da2, wa_ref[...]).astype(BF16)
        datt_ref[...] = _nt(db2, wb_ref[...]).astype(BF16)
        acc[0] += _tn(mg_ref[...], dmix_t)
        acc[1] += _tn(a_ref[...], da2)
        acc[2] += _tn(att_ref[...], db2)

        @pl.when(t == last)
        def _():
            for k, dw_ref in enumerate((dwo_ref, dwa_ref, dwb_ref)):
                stage[...] = acc[k].astype(BF16)
                out = pltpu.make_async_copy(stage, dw_ref, sem)
                out.start()
                out.wait()

    row = lambda i: (i, 0)
    gspec = lambda off: pl.BlockSpec((tm, half), lambda i: (i, off // half))
    body, dep_specs, deps = _after(body, 13, after)
    return pl.pallas_call(
        body, name="bwd_mix", grid=(T // tm,),
        in_specs=[pl.BlockSpec((tm, D), row), gspec(OFF_GA), gspec(OFF_GA + half), gspec(OFF_GB), gspec(OFF_GB + half)]
        + [pl.BlockSpec((tm, D), row)] * 5 + [_resident((D, D))] * 3 + dep_specs,
        out_specs=[pl.BlockSpec((tm, 2 * D), row), pl.BlockSpec((tm, D), row), pl.BlockSpec((tm, D), row)] + [_ANY] * 3,
        out_shape=[SDS((T, 2 * D), BF16), SDS((T, D), BF16), SDS((T, D), BF16)] + [SDS((D, D), BF16)] * 3,
        scratch_shapes=[pltpu.VMEM((3, D, D), F32), pltpu.VMEM((D, D), BF16), pltpu.SemaphoreType.DMA],
        compiler_params=_params(1),
    )(dmix, proj, proj, proj, proj, a2, b2, merged, a, att, wo, wa, wb, *deps)


def _bwd_attn(qr, kr, probs, psink, proj, cos, sin, datt, after=None):
    T = proj.shape[0]
    nb = T // CHUNK
    cur = lambda i: jnp.minimum(i, nb - 1)
    prev = lambda i: jnp.maximum(jnp.minimum(i, nb - 1) - 1, 0)

    def body(q_ref, kp_ref, kc_ref, vp_ref, vc_ref, cp_ref, cc_ref, sp_ref, sc_ref, p_ref, psink_ref, do_ref,
             dq_ref, dkv_ref, dsink_ref, carry_k, carry_v):
        i = pl.program_id(0)

        @pl.when(i == 0)
        def _():
            carry_k[...] = jnp.zeros_like(carry_k)
            carry_v[...] = jnp.zeros_like(carry_v)
            dsink_ref[...] = jnp.zeros_like(dsink_ref)

        @pl.when(i < nb)
        def _():
            prev_slot, _ = _fold_masks(i == 0)
            c_band, s_band = _band(cp_ref, cc_ref), _band(sp_ref, sc_ref)
            lane = lax.broadcasted_iota(jnp.int32, (1, 128), 1)
            dsink = jnp.zeros((1, 128), F32)
            for j in range(KV_W // 128):
                cols = slice(j * 128, (j + 1) * 128)
                k_slab = _band(kp_ref, kc_ref, cols).astype(F32)
                v_slab = _band(vp_ref, vc_ref, cols).astype(F32)
                dk_slab = jnp.zeros((2 * CHUNK, 128), F32)
                dv_slab = jnp.zeros((2 * CHUNK, 128), F32)
                for g in (2 * j, 2 * j + 1):
                    k2 = _head_pair_operand(k_slab, g)
                    v2 = _head_pair_operand(v_slab, g)
                    pairs = [g * PAIRS_PER_KV + r for r in range(PAIRS_PER_KV)]
                    q_stack = jnp.concatenate([q_ref[:, pr * 128:(pr + 1) * 128] for pr in pairs], axis=0)
                    do_stack = jnp.concatenate([do_ref[:, pr * 128:(pr + 1) * 128] for pr in pairs], axis=0)
                    dp2 = _nt(v2, do_stack)
                    pcols, dscols = [], []
                    for r, pair in enumerate(pairs):
                        ps, dss = [], []
                        for e in range(2):
                            head = 2 * pair + e
                            p_b = p_ref[head]
                            p = p_b.astype(F32)
                            dp = _fold(dp2[e * 2 * CHUNK:(e + 1) * 2 * CHUNK, r * 128:(r + 1) * 128], prev_slot)
                            delta = jnp.sum(p * dp, axis=0, keepdims=True)
                            ps.append(_unfold(p_b, prev_slot))
                            dss.append(_unfold((p * (dp - delta)).astype(BF16), prev_slot))
                            dsink = dsink + jnp.where(lane == head, -jnp.sum(psink_ref[head:head + 1, :] * delta), 0.0)
                        pcols.append(jnp.concatenate(ps, axis=0))
                        dscols.append(jnp.concatenate(dss, axis=0))
                    ds2 = jnp.concatenate(dscols, axis=1)
                    dq = _tn(ds2, k2) * (HEAD ** -0.5)
                    for r, pair in enumerate(pairs):
                        dq_ref[:, pair * 128:(pair + 1) * 128] = _rope_bwd(
                            dq[r * CHUNK:(r + 1) * CHUNK], cc_ref[...], sc_ref[...]).astype(BF16)
                    dk_slab = dk_slab + _head_pair_gradient(_nn(ds2, q_stack), g)
                    dv_slab = dv_slab + _head_pair_gradient(_nn(jnp.concatenate(pcols, axis=1), do_stack), g)
                dk_slab = _rope_bwd(dk_slab, c_band, s_band)
                vcols = slice(KV_W + j * 128, KV_W + (j + 1) * 128)
                dkv_ref[:, cols] = (carry_k[:, cols] + dk_slab[:CHUNK]).astype(BF16)
                dkv_ref[:, vcols] = (carry_v[:, cols] + dv_slab[:CHUNK]).astype(BF16)
                carry_k[:, cols] = dk_slab[CHUNK:]
                carry_v[:, cols] = dv_slab[CHUNK:]
            dsink_ref[...] += dsink

        @pl.when(i == nb)
        def _():
            dkv_ref[:, :KV_W] = carry_k[...].astype(BF16)
            dkv_ref[:, KV_W:] = carry_v[...].astype(BF16)

    table = lambda which, width: pl.BlockSpec((CHUNK, width), lambda i: (which(i), 0))
    body, dep_specs, deps = _after(body, 12, after)
    return pl.pallas_call(
        body, name="bwd_attn", grid=(nb + 1,),
        in_specs=[pl.BlockSpec((CHUNK, D), lambda i: (cur(i), 0)),
                  pl.BlockSpec((CHUNK, KV_W), lambda i: (prev(i), 0)),
                  pl.BlockSpec((CHUNK, KV_W), lambda i: (cur(i), 0)),
                  pl.BlockSpec((CHUNK, KV_W), lambda i: (prev(i), OFF_VA // KV_W)),
                  pl.BlockSpec((CHUNK, KV_W), lambda i: (cur(i), OFF_VA // KV_W)),
                  table(prev, 128), table(cur, 128), table(prev, 256), table(cur, 256),
                  pl.BlockSpec((None, N_Q, CHUNK, CHUNK), lambda i: (cur(i), 0, 0, 0)),
                  pl.BlockSpec((None, N_Q, CHUNK), lambda i: (cur(i), 0, 0)),
                  pl.BlockSpec((CHUNK, D), lambda i: (cur(i), 0))] + dep_specs,
        out_specs=[pl.BlockSpec((CHUNK, D), lambda i: (cur(i), 0)),
                   pl.BlockSpec((CHUNK, 2 * KV_W), lambda i: (jnp.maximum(i - 1, 0), 0)),
                   pl.BlockSpec((1, 128), lambda i: (0, 0))],
        out_shape=[SDS((T, D), BF16), SDS((T, 2 * KV_W), BF16), SDS((1, 128), F32)],
        scratch_shapes=[pltpu.VMEM((CHUNK, KV_W), F32), pltpu.VMEM((CHUNK, KV_W), F32)],
        compiler_params=_params(1),
    )(qr, kr, kr, proj, proj, cos, cos, sin, sin, probs, psink, datt, *deps)


def _bwd_sgu(proj, da, lng, lnb, ws, bst):
    T = proj.shape[0]
    tc = min(T, 512)
    nsteps = T // tc

    def body(u_ref, vs_ref, da_ref, lng_ref, lnb_ref, ws_ref, bst_ref,
             duv_ref, dws_ref, dbs_ref, dlng_ref, dlnb_ref, dvn_s, dgu_s, dmx_sum):
        i = pl.program_id(0)

        @pl.when(i == 0)
        def _():
            dws_ref[...] = jnp.zeros_like(dws_ref)
            dlng_ref[...] = jnp.zeros_like(dlng_ref)
            dlnb_ref[...] = jnp.zeros_like(dlnb_ref)
            dmx_sum[...] = jnp.zeros_like(dmx_sum)

        u, vs, gu, tu, tv, rstd, vhat, vn = _sgu_forward_parts(u_ref, vs_ref, lng_ref, lnb_ref)
        da = da_ref[...].astype(F32)
        for g in range(GROUPS):
            wm = _masked_ws(ws_ref, g)
            cols = slice(g * CHUNK, (g + 1) * CHUNK)
            dws = jnp.zeros((CHUNK, CHUNK), F32)
            dsum = jnp.zeros((CHUNK, CHUNK), F32)
            for c in range(tc // CHUNK):
                rows = slice(c * CHUNK, (c + 1) * CHUNK)
                vn_cg = vn[rows, cols]
                mixed = _nn(wm, vn_cg) + bst_ref[:, g:g + 1]
                dgu_s[rows, cols] = da[rows, cols] * mixed
                dmx = da[rows, cols] * gu[rows, cols]
                dmxb = dmx.astype(BF16)
                dws = dws + _nt(dmxb, vn_cg)
                dsum = dsum + dmx
                dvn_s[rows, cols] = _tn(wm, dmxb)
            dws_ref[g] += dws
            dmx_sum[:, cols] += dsum
        dvn = dvn_s[...]
        dlng_ref[...] += _colsum(dvn * vhat)
        dlnb_ref[...] += _colsum(dvn)
        dvh = dvn * lng_ref[...]
        dgv = rstd * (dvh - jnp.mean(dvh, axis=-1, keepdims=True) - vhat * jnp.mean(dvh * vhat, axis=-1, keepdims=True))
        duv_ref[:, :D] = (dgu_s[...] * _gelu_grad(u, tu)).astype(BF16)
        duv_ref[:, D:] = (dgv * _gelu_grad(vs, tv)).astype(BF16)

        @pl.when(i == nsteps - 1)
        def _():
            row = lax.broadcasted_iota(jnp.int32, (CHUNK, CHUNK), 0)
            col = lax.broadcasted_iota(jnp.int32, (CHUNK, CHUNK), 1)
            for g in range(GROUPS):
                dws_ref[g] = jnp.where(row >= col, dws_ref[g], 0.0)
                dbs_ref[g:g + 1, :] = _colsum(dmx_sum[:, g * CHUNK:(g + 1) * CHUNK].T)

    const2 = lambda i: (0, 0)
    return pl.pallas_call(
        body, name="bwd_sgu", grid=(nsteps,),
        in_specs=[pl.BlockSpec((tc, D), lambda i: (i, 0)), pl.BlockSpec((tc, D), lambda i: (i, 1)),
                  pl.BlockSpec((tc, D), lambda i: (i, 0)), pl.BlockSpec((1, D), const2), pl.BlockSpec((1, D), const2),
                  pl.BlockSpec((GROUPS, CHUNK, CHUNK), lambda i: (0, 0, 0)), pl.BlockSpec((CHUNK, GROUPS), const2)],
        out_specs=[pl.BlockSpec((tc, 2 * D), lambda i: (i, 0)), pl.BlockSpec((GROUPS, CHUNK, CHUNK), lambda i: (0, 0, 0)),
                   pl.BlockSpec((GROUPS, CHUNK), const2), pl.BlockSpec((1, D), const2), pl.BlockSpec((1, D), const2)],
        out_shape=[SDS((T, 2 * D), BF16), SDS((GROUPS, CHUNK, CHUNK), F32), SDS((GROUPS, CHUNK), F32),
                   SDS((1, D), F32), SDS((1, D), F32)],
        scratch_shapes=[pltpu.VMEM((tc, D), F32), pltpu.VMEM((tc, D), F32), pltpu.VMEM((CHUNK, D), F32)],
        compiler_params=_params(1),
    )(proj, proj, da, lng, lnb, ws, bst)


IN_SEG_WIDTHS = (2 * D, D, 2 * N_KV * HEAD, 2 * D)


def _resident(shape):
    return pl.BlockSpec(shape, lambda *_: (0,) * len(shape), pipeline_mode=pl.Buffered(1))


def _bwd_in(duv, dq, dkv, dg, win_t, x, dx1, g0, after=None):
    T = x.shape[0]
    tm = min(T, 512)

    def body(duv_ref, dq_ref, dkv_ref, dg_ref, w_ref, x_ref, dx1_ref, g0_ref, gx_ref, dg0_ref):
        @pl.when(pl.program_id(0) == 0)
        def _():
            dg0_ref[...] = jnp.zeros_like(dg0_ref)

        dh, off = None, 0
        for ref, width in zip((duv_ref, dq_ref, dkv_ref, dg_ref), IN_SEG_WIDTHS):
            part = _nn(ref[...], w_ref[off:off + width, :])
            dh = part if dh is None else dh + part
            off += width
        r0, xh = _rms_stats(x_ref[...])
        dg0_ref[...] += _colsum(dh * xh)
        gx_ref[...] = dx1_ref[...] + _rms_bwd(dh, xh, r0, g0_ref[...])

    row = lambda i: (i, 0)
    body, dep_specs, deps = _after(body, 8, after)
    return pl.pallas_call(
        body, name="bwd_in", grid=(T // tm,),
        in_specs=[pl.BlockSpec((tm, w), row) for w in IN_SEG_WIDTHS] + [
            _resident((IN_W, D)), pl.BlockSpec((tm, D), row), pl.BlockSpec((tm, D), row),
            pl.BlockSpec((1, D), lambda i: (0, 0))] + dep_specs,
        out_specs=[pl.BlockSpec((tm, D), row), pl.BlockSpec((1, D), lambda i: (0, 0))],
        out_shape=[SDS((T, D), F32), SDS((1, D), F32)],
        compiler_params=_params(1),
    )(duv, dq, dkv, dg, win_t, x, dx1, g0, *deps)


def _wgrad_rows(h, segs, first_row, into, name):
    T = h.shape[0]
    tt = min(T, 2048)
    widths = [s.shape[1] for s in segs]
    rows = sum(widths)
    n_in = 1 + len(segs) + (into is not None)

    def body(*refs):
        h_ref, seg_refs = refs[0], refs[1:1 + len(segs)]
        dw_ref, acc, stage, sem = refs[n_in], refs[n_in + 1], refs[n_in + 2], refs[n_in + 3]
        t = pl.program_id(0)

        @pl.when(t == 0)
        def _():
            acc[...] = jnp.zeros_like(acc)

        off = 0
        for ref, width in zip(seg_refs, widths):
            acc[off:off + width, :] += _tn(ref[...], h_ref[...])
            off += width

        @pl.when(t == T // tt - 1)
        def _():
            stage[...] = acc[...].astype(BF16)
            out = pltpu.make_async_copy(stage, dw_ref.at[pl.ds(first_row, rows)], sem)
            out.start()
            out.wait()

    row = lambda t: (t, 0)
    return pl.pallas_call(
        body, name=name, grid=(T // tt,),
        in_specs=[pl.BlockSpec((tt, D), row)] + [pl.BlockSpec((tt, w), row) for w in widths] + [_ANY] * (into is not None),
        out_specs=_ANY,
        out_shape=SDS((IN_W, D), BF16),
        input_output_aliases={} if into is None else {n_in - 1: 0},
        scratch_shapes=[pltpu.VMEM((rows, D), F32), pltpu.VMEM((rows, D), BF16), pltpu.SemaphoreType.DMA],
        compiler_params=_params(1),
    )(h, *segs, *([] if into is None else [into]))


def _place():
    x, y, c = lax.axis_index("x"), lax.axis_index("y"), lax.axis_index("c")
    return x, y, c, 4 * x + 2 * y + c


def _peers(x, y, c):
    out = []
    for mask in range(1, N_DEV):
        px = 1 - x if mask & 4 else x
        py = 1 - y if mask & 2 else y
        pc = 1 - c if mask & 1 else c
        out.append(((px, py, pc), 4 * px + 2 * py + pc))
    return out


def _all_to_all(arrays, gather, name, after=None):
    n = len(arrays)

    def body(*refs):
        ins, outs = refs[:n], refs[n:2 * n]
        send_sems, recv_sems, local_sems = refs[2 * n:]
        x, y, c, me = _place()
        local, sends, recvs = [], [], []
        for a in range(n):
            src_own = ins[a] if gather[a] else ins[a].at[me]
            local.append(pltpu.make_async_copy(src_own, outs[a].at[me], local_sems.at[a]))
            for k, (peer, pid) in enumerate(_peers(x, y, c)):
                sem = a * (N_DEV - 1) + k
                src = ins[a] if gather[a] else ins[a].at[pid]
                sends.append(pltpu.make_async_remote_copy(
                    src_ref=src, dst_ref=outs[a].at[me], send_sem=send_sems.at[sem], recv_sem=recv_sems.at[sem],
                    device_id=peer, device_id_type=MESH))
                recvs.append(pltpu.make_async_remote_copy(
                    src_ref=src, dst_ref=outs[a].at[pid], send_sem=send_sems.at[sem], recv_sem=recv_sems.at[sem],
                    device_id=peer, device_id_type=MESH))
        for cp in local + sends:
            cp.start()
        for cp in recvs:
            cp.wait_recv()
        for cp in sends:
            cp.wait_send()
        for cp in local:
            cp.wait()

    out_shape = [SDS((N_DEV,) + a.shape if gt else a.shape, a.dtype) for a, gt in zip(arrays, gather)]
    nsem = n * (N_DEV - 1)
    body, dep_specs, deps = _after(body, n, after)
    return pl.pallas_call(
        body, name=name,
        in_specs=[pl.BlockSpec(memory_space=pl.ANY)] * n + dep_specs,
        out_specs=[pl.BlockSpec(memory_space=pl.ANY)] * n,
        out_shape=out_shape,
        scratch_shapes=[pltpu.SemaphoreType.DMA((nsem,)), pltpu.SemaphoreType.DMA((nsem,)), pltpu.SemaphoreType.DMA((n,))],
    )(*arrays, *deps)


_HBM = pl.BlockSpec(memory_space=pltpu.HBM)
_SEM = pl.BlockSpec(memory_space=pltpu.SEMAPHORE)
_EFFECT = pltpu.SideEffectType.DATAFLOW_SIDE_EFFECTING
GATHER = "gather"
SCATTER = "scatter"
SPREAD = "spread"


def _zone_shape(a, mode):
    if mode == GATHER:
        return (N_DEV,) + a.shape
    return (N_DEV - 1,) + (a.shape[1:] if mode == SCATTER else a.shape)


def _start_copies(arrays, modes, name, after=None):
    n = len(arrays)
    zones = [lax.empty(_zone_shape(a, m), a.dtype) for a, m in zip(arrays, modes)]

    def body(*refs):
        ins, lands = refs[:n], refs[n:2 * n]
        send_sems, recv_sems = refs[-2 * n - 3], refs[-2 * n - 2]
        token = refs[-1]
        x, y, c, me = _place()
        for a in range(n):
            for k, (peer, pid) in enumerate(_peers(x, y, c)):
                src = ins[a].at[pid] if modes[a] == SCATTER else ins[a]
                dst = lands[a].at[me] if modes[a] == GATHER else lands[a].at[k]
                pltpu.make_async_remote_copy(src_ref=src, dst_ref=dst, send_sem=send_sems.at[a], recv_sem=recv_sems.at[a],
                                             device_id=peer, device_id_type=MESH).start()
            if modes[a] == GATHER:
                pltpu.make_async_remote_copy(src_ref=ins[a], dst_ref=lands[a].at[me], send_sem=send_sems.at[a],
                                             recv_sem=recv_sems.at[a], device_id=(x, y, c), device_id_type=MESH).start()
        token[...] = jnp.zeros_like(token)

    hbm = lambda a: pltpu.HBM(a.shape, a.dtype)
    sems = pltpu.SemaphoreType.DMA((n,))
    extra = [] if after is None else [after]
    operands = [pltpu.with_memory_space_constraint(a, pltpu.HBM) for a in list(arrays) + zones]
    res = pl.pallas_call(
        body, name=name,
        out_shape=(sems, sems, *[hbm(a) for a in arrays], *[hbm(z) for z in zones], SDS((8, 128), F32)),
        in_specs=[_HBM] * (2 * n) + [_ANY] * len(extra),
        out_specs=(_SEM, _SEM, *[_HBM] * (2 * n), pl.BlockSpec(memory_space=pltpu.VMEM)),
        input_output_aliases={i: 2 + i for i in range(2 * n)},
        compiler_params=pltpu.CompilerParams(has_side_effects=_EFFECT),
    )(*operands, *extra)
    return res[0], res[1], list(res[2:2 + n]), list(res[2 + n:2 + 2 * n]), res[-1]


def _wait_copies(started, after, name, count=N_DEV - 1):
    send_sems, recv_sems, thru, zones, _ = started
    nt, nz = len(thru), len(zones)

    def body(*refs):
        lands = refs[nt:nt + nz]
        send_ref, recv_ref = refs[nt + nz], refs[nt + nz + 1]
        x, y, c, _ = _place()
        for a in range(nz):
            blocks = lands[a].at[pl.ds(0, count)]
            cp = pltpu.make_async_remote_copy(src_ref=blocks, dst_ref=blocks, send_sem=send_ref.at[a], recv_sem=recv_ref.at[a],
                                              device_id=(x, y, 1 - c), device_id_type=MESH)
            cp.wait_send()
            cp.wait_recv()

    hbm = lambda a: pltpu.HBM(a.shape, a.dtype)
    res = pl.pallas_call(
        body, name=name,
        out_shape=tuple(hbm(a) for a in thru + zones),
        in_specs=[_HBM] * (nt + nz) + [_SEM, _SEM, _ANY],
        out_specs=tuple([_HBM] * (nt + nz)),
        input_output_aliases={i: i for i in range(nt + nz)},
        compiler_params=pltpu.CompilerParams(has_side_effects=_EFFECT),
    )(*thru, *zones, send_sems, recv_sems, after)
    return list(res[:nt]), list(res[nt:])


def _split_start(body, arrays, zones, name, after):
    n = len(arrays) + len(zones)
    hbm = lambda a: pltpu.HBM(a.shape, a.dtype)
    sems = pltpu.SemaphoreType.DMA((max(len(zones), 1),))
    extra = [] if after is None else [after]
    operands = [pltpu.with_memory_space_constraint(a, pltpu.HBM) for a in list(arrays) + list(zones)]
    res = pl.pallas_call(
        body, name=name,
        out_shape=(sems, sems, *[hbm(a) for a in operands], SDS((8, 128), F32)),
        in_specs=[_HBM] * n + [_ANY] * len(extra),
        out_specs=(_SEM, _SEM, *[_HBM] * n, pl.BlockSpec(memory_space=pltpu.VMEM)),
        input_output_aliases={i: 2 + i for i in range(n)},
        compiler_params=pltpu.CompilerParams(has_side_effects=_EFFECT),
    )(*operands, *extra)
    return res[0], res[1], list(res[2:2 + len(arrays)]), list(res[2 + len(arrays):2 + n]), res[-1]


def _gather_first_leg(shard, name, after=None):
    zone = lax.empty((N_DEV,) + shard.shape, shard.dtype)
    extra = 0 if after is None else 1

    def body(*refs):
        src, land = refs[0], refs[1]
        send_sem, recv_sem, token = refs[2 + extra], refs[3 + extra], refs[-1]
        x, y, c, me = _place()
        for peer in ((x, y, c), (x, y, 1 - c), (1 - x, y, c), (x, 1 - y, c), (1 - x, 1 - y, c)):
            pltpu.make_async_remote_copy(src_ref=src, dst_ref=land.at[me], send_sem=send_sem.at[0], recv_sem=recv_sem.at[0],
                                         device_id=peer, device_id_type=MESH).start()
        token[...] = jnp.zeros_like(token)

    return _split_start(body, [shard], [zone], name, after)


def _gather_second_leg(zone, name, after=None):
    extra = 0 if after is None else 1

    def body(*refs):
        land = refs[0]
        send_sem, recv_sem, token = refs[1 + extra], refs[2 + extra], refs[-1]
        x, y, c, _ = _place()
        for px, py in ((1 - x, y), (x, 1 - y), (1 - x, 1 - y)):
            slot = 4 * px + 2 * py + c
            pltpu.make_async_remote_copy(src_ref=land.at[slot], dst_ref=land.at[slot], send_sem=send_sem.at[0],
                                         recv_sem=recv_sem.at[0], device_id=(x, y, 1 - c), device_id_type=MESH).start()
        token[...] = jnp.zeros_like(token)

    return _split_start(body, [], [zone], name, after)


UPDATE_BLOCK_ELEMS = 256 * 1024


def _update_rows(R, C):
    fits = [t for t in range(8, R + 1, 8) if R % t == 0 and t * C <= UPDATE_BLOCK_ELEMS]
    whole = [t for t in fits if t % 16 == 0]
    return max(whole or fits)


def _adamw_math(g, w, m, v):
    m2 = ADAM_B1 * m + (1.0 - ADAM_B1) * g
    v2 = ADAM_B2 * v + (1.0 - ADAM_B2) * (g * g)
    m_hat = m2 / (1.0 - ADAM_B1 ** ADAM_STEP)
    v_hat = v2 / (1.0 - ADAM_B2 ** ADAM_STEP)
    delta = -ADAM_LR * (m_hat / (jnp.sqrt(v_hat) + ADAM_EPS) + ADAM_WD * w)
    return delta, m2, v2


def _sum_adamw(parts, w, m, v, name):
    R, C = w.shape
    tr = _update_rows(R, C)

    def body(p_ref, w_ref, m_ref, v_ref, g_ref, d_ref, m2_ref, v2_ref):
        g = p_ref[0]
        for k in range(1, N_DEV):
            g = g + p_ref[k]
        g_ref[...] = g
        d_ref[...], m2_ref[...], v2_ref[...] = _adamw_math(g, w_ref[...], m_ref[...], v_ref[...])

    blk = pl.BlockSpec((tr, C), lambda i: (i, 0))
    return pl.pallas_call(
        body, name=name, grid=(R // tr,),
        in_specs=[pl.BlockSpec((N_DEV, tr, C), lambda i: (0, i, 0)), blk, blk, blk],
        out_specs=[blk] * 4,
        out_shape=[SDS((R, C), F32)] * 4,
        compiler_params=_params(1),
    )(parts, w, m, v)


def _sum_adamw_peers(me, own, parts, w, m, v, name, replicated, also_rows=None):
    R, C = w.shape
    tr = _update_rows(R, C)
    assert also_rows is None or tr == R

    def body(me_ref, own_ref, p_ref, w_ref, m_ref, v_ref, g_ref, d_ref, m2_ref, v2_ref, *extra):
        if replicated:
            mine = me_ref[0]
            g = None
            for j in range(N_DEV):
                k = jnp.maximum(jnp.bitwise_xor(mine, j) - 1, 0)
                term = jnp.where(mine == j, own_ref[...], p_ref[k])
                g = term if g is None else g + term
        else:
            g = own_ref[...].astype(F32)
            for k in range(N_DEV - 1):
                g = g + p_ref[k].astype(F32)
        results = (g,) + _adamw_math(g, w_ref[...], m_ref[...], v_ref[...])
        for ref, val in zip((g_ref, d_ref, m2_ref, v2_ref), results):
            ref[...] = val
        for ref, val in zip(extra, results):
            ref[...] = val[also_rows[0]:also_rows[1]]

    blk = pl.BlockSpec((tr, C), lambda i, me_ref: (i, 0))
    own_spec = blk if replicated else pl.BlockSpec((None, tr, C), lambda i, me_ref: (me_ref[0], i, 0))
    n_also = 0 if also_rows is None else also_rows[1] - also_rows[0]
    also_specs = [pl.BlockSpec((n_also, C), lambda i, me_ref: (0, 0))] * (4 if also_rows else 0)
    return pl.pallas_call(
        body, name=name,
        grid_spec=pltpu.PrefetchScalarGridSpec(
            num_scalar_prefetch=1, grid=(R // tr,),
            in_specs=[own_spec, pl.BlockSpec((N_DEV - 1, tr, C), lambda i, me_ref: (0, i, 0)), blk, blk, blk],
            out_specs=[blk] * 4 + also_specs),
        out_shape=[SDS((R, C), F32)] * 4 + [SDS((n_also, C), F32)] * len(also_specs),
        compiler_params=_params(1),
    )(me, own, parts, w, m, v)


SMALL = ("ln_v_gain", "ln_v_bias", "w_spatial", "b_spatial", "sinks", "norm_mix_post", "norm_ff_pre", "norm_ff_post")
SMALL_ROWS = {"ln_v_gain": 8, "ln_v_bias": 8, "w_spatial": 1024, "b_spatial": 8, "sinks": 8,
              "norm_mix_post": 8, "norm_ff_pre": 8, "norm_ff_post": 8}
SMALL_PACK_ROWS = 1152


def _pack_small(vals):
    rows = []
    for name in SMALL:
        flat = vals[name].reshape(-1)
        pad = SMALL_ROWS[name] * 128 - flat.shape[0]
        if pad:
            flat = jnp.concatenate([flat, jnp.zeros((pad,), F32)])
        rows.append(flat.reshape(SMALL_ROWS[name], 128))
    rows.append(jnp.zeros((SMALL_PACK_ROWS - sum(SMALL_ROWS.values()), 128), F32))
    return jnp.concatenate(rows, axis=0)


def _unpack_small(packed, shapes):
    out, r = {}, 0
    for name in SMALL:
        n = 1
        for s in shapes[name]:
            n *= s
        out[name] = packed[r:r + SMALL_ROWS[name]].reshape(-1)[:n].reshape(shapes[name])
        r += SMALL_ROWS[name]
    return out


def _rope_rows():
    d = jnp.arange(128) % HEAD
    inv = ROPE_THETA ** (-(2.0 * (d % (ROPE // 2))).astype(F32) / ROPE)
    invf = jnp.where(d < ROPE, inv, 0.0).astype(F32).reshape(1, 128)
    sgn = jnp.where(d < ROPE // 2, -1.0, jnp.where(d < ROPE, 1.0, 0.0)).astype(F32).reshape(1, 128)
    return invf, sgn


def kernel(x, positions, w_in, ln_v_gain, ln_v_bias, w_spatial, b_spatial, sinks, w_a, w_b, w_o, norm_mix_pre, norm_mix_post, w_ff_in, w_ff_out, norm_ff_pre, norm_ff_post, loss_target, m_w_in, m_ln_v_gain, m_ln_v_bias, m_w_spatial, m_b_spatial, m_sinks, m_w_a, m_w_b, m_w_o, m_norm_mix_pre, m_norm_mix_post, m_w_ff_in, m_w_ff_out, m_norm_ff_pre, m_norm_ff_post, v_w_in, v_ln_v_gain, v_ln_v_bias, v_w_spatial, v_b_spatial, v_sinks, v_w_a, v_w_b, v_w_o, v_norm_mix_pre, v_norm_mix_post, v_w_ff_in, v_w_ff_out, v_norm_ff_pre, v_norm_ff_post):
    given = dict(locals())
    T = x.shape[1]
    xt = x[0]
    tgt = loss_target[0]
    bst = b_spatial[0].T
    ws = w_spatial[0]

    me = 4 * lax.axis_index("x") + 2 * lax.axis_index("y") + lax.axis_index("c")
    me_arr = me.astype(jnp.int32).reshape(1)

    rest = ("w_a", "w_b", "w_o", "w_ff_in", "w_ff_out")
    shard = {n: given[n][0].astype(BF16) for n in rest}
    g_one = _gather_first_leg(w_in[0].T.astype(BF16), "gather_in_start")
    cos, sin = _rope_tables(positions.astype(F32).reshape(T, 1), *_rope_rows(), after=g_one[-1])
    small_state = [_pack_small({n: given[k + n] for n in SMALL}) for k in ("", "m_", "v_")]
    h = _rms_pre(xt, norm_mix_pre, after=[cos, *small_state, *[shard[n] for n in rest]])
    _, (win8,) = _wait_copies(g_one, h, "gather_in_wait", count=5)
    g_two = _gather_second_leg(win8, "gather_in_pass_start")
    g_rest = _start_copies([shard[n] for n in rest], [GATHER] * len(rest), "gather_rest_start", after=g_two[-1])
    _, (win8,) = _wait_copies(g_two, g_rest[-1], "gather_in_pass_wait", count=3)
    win = win8.reshape(IN_W, D)

    proj = _fwd_in(h, win)
    att, qr, kr, probs, psink = _fwd_attn(proj, cos, sin, sinks[0])
    a = _fwd_sgu(proj, ln_v_gain, ln_v_bias, ws, bst, after=att)
    gw = dict(zip(rest, _wait_copies(g_rest, a, "gather_rest_wait", count=N_DEV)[1]))
    wa, wb, wo = (gw[n].reshape(D, D) for n in ("w_a", "w_b", "w_o"))
    wfi3 = gw["w_ff_in"]
    wfo = gw["w_ff_out"].reshape(D_FF, D)
    merged, a2, b2, mix, x1, hf = _fwd_mix(a, att, proj, xt, wa, wb, wo, norm_mix_post, norm_ff_pre)
    f, dy, dff, dg3, loss_part = _fwd_ff(hf, wfi3, wfo, x1, tgt, norm_ff_post)

    df, dx1, dmix, dg2, dg1 = _bwd_ff(dff, f, wfi3, wfo, x1, dy, mix, norm_mix_post, norm_ff_pre)
    dwfi3, dwfo = _wgrad_ff(hf, df, f, dff)
    own_ff = [dwfi3, dwfo.reshape(N_DEV, D_FF // N_DEV, D)]
    x_ff = _start_copies(own_ff, [SCATTER] * 2, "exchange_ff_start")
    dgate, da, datt, dwo, dwa, dwb = _bwd_mix(dmix, proj, a2, b2, merged, a, att, wo, wa, wb, after=x_ff[-1])
    own_mix = [g.reshape(N_DEV, D // N_DEV, D) for g in (dwa, dwb, dwo)]
    x_mix = _start_copies(own_mix, [SCATTER] * 3, "exchange_mix_start")
    dq, dkv, dsink = _bwd_attn(qr, kr, probs, psink, proj, cos, sin, datt, after=x_mix[-1])
    duv, dws, dbs, dlng, dlnb = _bwd_sgu(proj, da, ln_v_gain, ln_v_bias, ws, bst)
    small_grads = {"ln_v_gain": dlng, "ln_v_bias": dlnb, "w_spatial": dws, "b_spatial": dbs, "sinks": dsink[:, :N_Q],
                   "norm_mix_post": dg1, "norm_ff_pre": dg2, "norm_ff_post": dg3}
    x_small = _start_copies([_pack_small(small_grads)], [SPREAD], "exchange_small_start")
    dwin = _wgrad_rows(h, [dgate], sum(IN_SEG_WIDTHS[:3]), None, "wgrad_in_gates")
    dwin = _wgrad_rows(h, [duv], 0, dwin, "wgrad_in_uv")
    dwin = _wgrad_rows(h, [dq, dkv], IN_SEG_WIDTHS[0], dwin, "wgrad_in_qkv")
    own_in = [dwin.reshape(N_DEV, IN_W // N_DEV, D)]
    x_in = _start_copies(own_in, [SCATTER], "exchange_in_start", after=x_small[-1])
    grad_x, dg0 = _bwd_in(duv, dq, dkv, dgate, win, xt, dx1, norm_mix_pre, after=x_in[-1])

    results = {}

    def update(n, own, parts, transposed=False):
        state = [given[k + n][0].T if transposed else given[k + n][0] for k in ("", "m_", "v_")]
        res = _sum_adamw_peers(me_arr, own, parts, *state, "adamw_" + n, False)
        results[n] = [(r.T if transposed else r).reshape(given[n].shape) for r in res]

    own_ff, p_ff = _wait_copies(x_ff, grad_x, "exchange_ff_wait")
    update("w_ff_in", own_ff[0], p_ff[0])
    update("w_ff_out", own_ff[1], p_ff[1])
    own_mix, p_mix = _wait_copies(x_mix, results["w_ff_out"][0], "exchange_mix_wait")
    for n, own, parts in zip(("w_a", "w_b", "w_o"), own_mix, p_mix):
        update(n, own, parts)
    tail = jnp.concatenate([dg0.reshape(8, 128), jnp.tile(loss_part, (8, 1))], axis=0)
    (tail_all,) = _all_to_all([tail], [True], "exchange_tail", after=results["w_o"][0])
    dg0_all = tail_all[:, :8]
    own_small, p_small = _wait_copies(x_small, tail_all, "exchange_small_wait")
    own_in, p_in = _wait_copies(x_in, p_small[0], "exchange_in_wait")
    update("w_in", own_in[0], p_in[0], transposed=True)
    first = sum(SMALL_ROWS[n] for n in SMALL[:SMALL.index("w_spatial")])
    packed = _sum_adamw_peers(me_arr, own_small[0], p_small[0], *small_state, "adamw_small", True,
                              also_rows=(first, first + SMALL_ROWS["w_spatial"]))
    shapes = {n: given[n].shape for n in SMALL}
    unpacked = [_unpack_small(p, shapes) for p in packed[:4]]
    for n in SMALL:
        results[n] = [u[n] for u in unpacked]
    results["w_spatial"] = [r.reshape(w_spatial.shape) for r in packed[4:]]
    n = "norm_mix_pre"
    results[n] = [r.reshape(given[n].shape) for r in _sum_adamw(
        dg0_all, given[n].reshape(8, 128), given["m_" + n].reshape(8, 128), given["v_" + n].reshape(8, 128), "adamw_" + n)]

    loss = jnp.sum(tail_all[:, 8, 0])
    order = ("w_in", "ln_v_gain", "ln_v_bias", "w_spatial", "b_spatial", "sinks", "w_a", "w_b", "w_o", "norm_mix_pre",
             "norm_mix_post", "w_ff_in", "w_ff_out", "norm_ff_pre", "norm_ff_post")
    out = [loss, grad_x.reshape(x.shape)]
    for k in range(4):
        out += [results[n][k] for n in order]
    return tuple(out)
```

```python
import jax
import jax.numpy as jnp
from jax import lax
from jax.experimental import pallas as pl
from jax.experimental.pallas import tpu as pltpu

F32 = jnp.float32
BF16 = jnp.bfloat16

N_DEV = 8
D = 1024
D_FF = 4096
IN_W = 5632
CHUNK = 128
GROUPS = 8
HEAD = 64
N_Q = 16
N_KV = 4
ROPE = 16
ROPE_THETA = 500000.0
EPS = 1e-6
OFF_Q, OFF_K, OFF_VA, OFF_GA, OFF_GB = 2048, 3072, 3328, 3584, 4608

ADAM_LR = 0.001
ADAM_B1 = 0.9
ADAM_B2 = 0.999
ADAM_EPS = 1e-08
ADAM_WD = 0.01
ADAM_STEP = 10

VMEM_LIMIT = 62 * 1024 * 1024

SDS = jax.ShapeDtypeStruct
MESH = pl.DeviceIdType.MESH


def _params(n_axes):
    return pltpu.CompilerParams(dimension_semantics=("arbitrary",) * n_axes, vmem_limit_bytes=VMEM_LIMIT)


def _nt(a, b):
    return lax.dot_general(a, b, (((1,), (1,)), ((), ())), preferred_element_type=F32)


def _tn(a, b):
    return lax.dot_general(a, b, (((0,), (0,)), ((), ())), preferred_element_type=F32)


def _nn(a, b):
    return jnp.dot(a, b, preferred_element_type=F32)


def _gelu(x):
    t = jnp.tanh(0.7978845608028654 * (x + 0.044715 * (x * x * x)))
    return 0.5 * x * (1.0 + t), t


def _gelu_grad(x, t):
    return 0.5 * (1.0 + t) + 0.5 * x * (1.0 - t * t) * (0.7978845608028654 * (1.0 + 3.0 * 0.044715 * x * x))


def _sigmoid(x):
    return 1.0 / (1.0 + jnp.exp(-x))


def _rms_stats(v):
    r = lax.rsqrt(jnp.mean(v * v, axis=-1, keepdims=True) + EPS)
    return r, v * r


def _rms_bwd(d, vhat, r, g):
    gd = g * d
    return r * (gd - vhat * jnp.mean(gd * vhat, axis=-1, keepdims=True))


def _colsum(v):
    return jnp.sum(v, axis=0, keepdims=True)


_ANY = pl.BlockSpec(memory_space=pl.ANY)


def _after(body, n_in, after):
    if after is None:
        return body, [], []
    deps = list(after) if isinstance(after, (list, tuple)) else [after]

    def ordered(*refs):
        return body(*refs[:n_in], *refs[n_in + len(deps):])

    return ordered, [_ANY] * len(deps), deps


def _rms_pre(x, g0, after=None):
    T = x.shape[0]
    tm = min(T, 1024)

    def body(x_ref, g_ref, h_ref):
        _, xh = _rms_stats(x_ref[...])
        h_ref[...] = (xh * g_ref[...]).astype(BF16)

    body, dep_specs, deps = _after(body, 2, after)
    return pl.pallas_call(
        body, name="rms_pre", grid=(T // tm,),
        in_specs=[pl.BlockSpec((tm, D), lambda i: (i, 0)), pl.BlockSpec((1, D), lambda i: (0, 0))] + dep_specs,
        out_specs=pl.BlockSpec((tm, D), lambda i: (i, 0)),
        out_shape=SDS((T, D), BF16),
        compiler_params=_params(1),
    )(x, g0, *deps)


def _fwd_in(h, win_t):
    T = h.shape[0]
    tm, tn = min(T, 512), 1408

    def body(h_ref, w_ref, p_ref):
        for j in range(IN_W // tn):
            cols = slice(j * tn, (j + 1) * tn)
            p_ref[:, cols] = _nt(h_ref[...], w_ref[cols, :]).astype(BF16)

    return pl.pallas_call(
        body, name="fwd_in", grid=(T // tm,),
        in_specs=[pl.BlockSpec((tm, D), lambda i: (i, 0)), _resident((IN_W, D))],
        out_specs=pl.BlockSpec((tm, IN_W), lambda i: (i, 0)),
        out_shape=SDS((T, IN_W), BF16),
        compiler_params=_params(1),
    )(h, win_t)


def _sgu_forward_parts(u_ref, vs_ref, lng_ref, lnb_ref):
    u = u_ref[...].astype(F32)
    vs = vs_ref[...].astype(F32)
    gu, tu = _gelu(u)
    gv, tv = _gelu(vs)
    mu = jnp.mean(gv, axis=-1, keepdims=True)
    dv = gv - mu
    rstd = lax.rsqrt(jnp.mean(dv * dv, axis=-1, keepdims=True) + EPS)
    vhat = dv * rstd
    vn = (vhat * lng_ref[...] + lnb_ref[...]).astype(BF16)
    return gu, tu, tv, mu, rstd, vn


def _sgu_forward_replay(u_ref, vs_ref, t_ref, stat_ref, lng_ref, lnb_ref):
    u = u_ref[...].astype(F32)
    vs = vs_ref[...].astype(F32)
    tu = t_ref[:, :D].astype(F32)
    tv = t_ref[:, D:].astype(F32)
    gu = 0.5 * u * (1.0 + tu)
    rstd = stat_ref[:, 1:2]
    vhat = (0.5 * vs * (1.0 + tv) - stat_ref[:, 0:1]) * rstd
    vn = (vhat * lng_ref[...] + lnb_ref[...]).astype(BF16)
    return u, vs, gu, tu, tv, rstd, vhat, vn


def _masked_ws(ws_ref, g):
    row = lax.broadcasted_iota(jnp.int32, (CHUNK, CHUNK), 0)
    col = lax.broadcasted_iota(jnp.int32, (CHUNK, CHUNK), 1)
    return jnp.where(row >= col, ws_ref[g], 0.0).astype(BF16)


def _fwd_sgu(proj, lng, lnb, ws, bst, after=None):
    T = proj.shape[0]
    tc = min(T, 512)

    def body(u_ref, vs_ref, lng_ref, lnb_ref, ws_ref, bst_ref, a_ref, t_ref, stat_ref):
        gu, tu, tv, mu, rstd, vn = _sgu_forward_parts(u_ref, vs_ref, lng_ref, lnb_ref)
        t_ref[:, :D] = tu.astype(BF16)
        t_ref[:, D:] = tv.astype(BF16)
        lane = lax.broadcasted_iota(jnp.int32, (tc, 128), 1)
        stat_ref[...] = jnp.where(lane == 0, mu, jnp.where(lane == 1, rstd, 0.0))
        for g in range(GROUPS):
            wm = _masked_ws(ws_ref, g)
            cols = slice(g * CHUNK, (g + 1) * CHUNK)
            for c in range(tc // CHUNK):
                rows = slice(c * CHUNK, (c + 1) * CHUNK)
                mixed = _nn(wm, vn[rows, cols]) + bst_ref[:, g:g + 1]
                a_ref[rows, cols] = (gu[rows, cols] * mixed).astype(BF16)

    body, dep_specs, deps = _after(body, 6, after)
    return pl.pallas_call(
        body, name="fwd_sgu", grid=(T // tc,),
        in_specs=[pl.BlockSpec((tc, D), lambda i: (i, 0)), pl.BlockSpec((tc, D), lambda i: (i, 1)),
                  pl.BlockSpec((1, D), lambda i: (0, 0)), pl.BlockSpec((1, D), lambda i: (0, 0)),
                  pl.BlockSpec((GROUPS, CHUNK, CHUNK), lambda i: (0, 0, 0)),
                  pl.BlockSpec((CHUNK, GROUPS), lambda i: (0, 0))] + dep_specs,
        out_specs=[pl.BlockSpec((tc, D), lambda i: (i, 0)), pl.BlockSpec((tc, 2 * D), lambda i: (i, 0)),
                   pl.BlockSpec((tc, 128), lambda i: (i, 0))],
        out_shape=[SDS((T, D), BF16), SDS((T, 2 * D), BF16), SDS((T, 128), F32)],
        compiler_params=_params(1),
    )(proj, proj, lng, lnb, ws, bst, *deps)


def _rope_tables(posf, invf, sgn, after=None):
    T = posf.shape[0]
    tr = min(T, 1024)

    def body(pos_ref, invf_ref, sgn_ref, c_ref, s_ref):
        ang = pos_ref[...] * invf_ref[...]
        c_ref[...] = jnp.cos(ang)
        s = jnp.sin(ang)
        s_ref[:, :128] = jnp.where(sgn_ref[...] < 0.0, -s, 0.0)
        s_ref[:, 128:] = jnp.where(sgn_ref[...] > 0.0, s, 0.0)

    body, dep_specs, deps = _after(body, 3, after)
    return pl.pallas_call(
        body, name="rope_tables", grid=(T // tr,),
        in_specs=[pl.BlockSpec((tr, 1), lambda i: (i, 0)), pl.BlockSpec((1, 128), lambda i: (0, 0)),
                  pl.BlockSpec((1, 128), lambda i: (0, 0))] + dep_specs,
        out_specs=[pl.BlockSpec((tr, 128), lambda i: (i, 0)), pl.BlockSpec((tr, 256), lambda i: (i, 0))],
        out_shape=[SDS((T, 128), F32), SDS((T, 256), F32)],
        compiler_params=_params(1),
    )(posf, invf, sgn, *deps)


def _rope(v, c, s):
    v = v.astype(F32)
    return v * c + pltpu.roll(v, 128 - ROPE // 2, 1) * s[:, :128] + pltpu.roll(v, ROPE // 2, 1) * s[:, 128:]


def _rope_bwd(dv, c, s):
    return dv * c + pltpu.roll(dv * s[:, :128], ROPE // 2, 1) + pltpu.roll(dv * s[:, 128:], 128 - ROPE // 2, 1)


def _fold_masks(first):
    jj = lax.broadcasted_iota(jnp.int32, (CHUNK, CHUNK), 0)
    t = lax.broadcasted_iota(jnp.int32, (CHUNK, CHUNK), 1)
    prev = jj > t
    return prev, jnp.where(prev & first, -1e30, 0.0)


def _fold(band, prev):
    return jnp.where(prev, band[:CHUNK], band[CHUNK:])


def _unfold(folded, prev):
    return jnp.concatenate([jnp.where(prev, folded, 0.0), jnp.where(prev, 0.0, folded)], axis=0)


def _softmax_sink(s, sink, key_axis):
    m = jnp.maximum(jnp.max(s, axis=key_axis, keepdims=True), sink)
    p = jnp.exp(s - m)
    esink = jnp.exp(sink - m)
    inv = 1.0 / (jnp.sum(p, axis=key_axis, keepdims=True) + esink)
    return p * inv, esink * inv


def _head_pair_operand(slab, g):
    lo = lax.broadcasted_iota(jnp.int32, slab.shape, 1) < HEAD
    if g % 2 == 0:
        first = jnp.where(lo, slab, 0.0)
        second = pltpu.roll(first, HEAD, 1)
    else:
        second = jnp.where(lo, 0.0, slab)
        first = pltpu.roll(second, HEAD, 1)
    return jnp.concatenate([first, second], axis=0).astype(BF16)


def _head_pair_gradient(acc, g):
    top, bot = acc[:2 * CHUNK], acc[2 * CHUNK:]
    lo = lax.broadcasted_iota(jnp.int32, top.shape, 1) < HEAD
    if g % 2 == 0:
        return jnp.where(lo, top, 0.0) + pltpu.roll(jnp.where(lo, 0.0, bot), HEAD, 1)
    return pltpu.roll(jnp.where(lo, top, 0.0), HEAD, 1) + jnp.where(lo, 0.0, bot)


PAIRS_PER_KV = N_Q // N_KV // 2
KV_W = N_KV * HEAD


def _band(prev_ref, cur_ref, cols=slice(None)):
    return jnp.concatenate([prev_ref[:, cols], cur_ref[:, cols]], axis=0)


def _fwd_attn(proj, cos, sin, sinks):
    T = proj.shape[0]
    nb = T // CHUNK
    cur = lambda i: i
    prev = lambda i: jnp.maximum(i - 1, 0)

    def body(q_ref, kp_ref, kc_ref, vp_ref, vc_ref, cp_ref, cc_ref, sp_ref, sc_ref, sink_ref,
             o_ref, qr_ref, kr_ref, p_ref, psink_ref):
        prev_slot, bias = _fold_masks(pl.program_id(0) == 0)
        c_band, s_band = _band(cp_ref, cc_ref), _band(sp_ref, sc_ref)
        for j in range(KV_W // 128):
            cols = slice(j * 128, (j + 1) * 128)
            k_slab = _rope(_band(kp_ref, kc_ref, cols), c_band, s_band)
            kr_ref[:, cols] = k_slab[CHUNK:].astype(BF16)
            v_slab = _band(vp_ref, vc_ref, cols).astype(F32)
            for g in (2 * j, 2 * j + 1):
                k2 = _head_pair_operand(k_slab, g)
                v2 = _head_pair_operand(v_slab, g)
                pairs = [g * PAIRS_PER_KV + r for r in range(PAIRS_PER_KV)]
                qps = []
                for pair in pairs:
                    lanes = slice(pair * 128, (pair + 1) * 128)
                    qps.append((_rope(q_ref[:, lanes], cc_ref[...], sc_ref[...]) * (HEAD ** -0.5)).astype(BF16))
                    qr_ref[:, lanes] = qps[-1]
                s2 = _nt(k2, jnp.concatenate(qps, axis=0))
                pcols = []
                for r, pair in enumerate(pairs):
                    ps = []
                    for e in range(2):
                        head = 2 * pair + e
                        s = _fold(s2[e * 2 * CHUNK:(e + 1) * 2 * CHUNK, r * 128:(r + 1) * 128], prev_slot) + bias
                        p, psink = _softmax_sink(s, sink_ref[head], 0)
                        p = p.astype(BF16)
                        p_ref[head] = p
                        psink_ref[head:head + 1, :] = psink
                        ps.append(_unfold(p, prev_slot))
                    pcols.append(jnp.concatenate(ps, axis=0))
                o = _tn(jnp.concatenate(pcols, axis=1), v2).astype(BF16)
                for r, pair in enumerate(pairs):
                    o_ref[:, pair * 128:(pair + 1) * 128] = o[r * CHUNK:(r + 1) * CHUNK]

    table = lambda which, width: pl.BlockSpec((CHUNK, width), lambda i: (which(i), 0))
    return pl.pallas_call(
        body, name="fwd_attn", grid=(nb,),
        in_specs=[pl.BlockSpec((CHUNK, D), lambda i: (i, OFF_Q // D)),
                  pl.BlockSpec((CHUNK, KV_W), lambda i: (prev(i), OFF_K // KV_W)),
                  pl.BlockSpec((CHUNK, KV_W), lambda i: (i, OFF_K // KV_W)),
                  pl.BlockSpec((CHUNK, KV_W), lambda i: (prev(i), OFF_VA // KV_W)),
                  pl.BlockSpec((CHUNK, KV_W), lambda i: (i, OFF_VA // KV_W)),
                  table(prev, 128), table(cur, 128), table(prev, 256), table(cur, 256),
                  pl.BlockSpec(memory_space=pltpu.SMEM)],
        out_specs=[pl.BlockSpec((CHUNK, D), lambda i: (i, 0)), pl.BlockSpec((CHUNK, D), lambda i: (i, 0)),
                   pl.BlockSpec((CHUNK, KV_W), lambda i: (i, 0)),
                   pl.BlockSpec((None, N_Q, CHUNK, CHUNK), lambda i: (i, 0, 0, 0)),
                   pl.BlockSpec((None, N_Q, CHUNK), lambda i: (i, 0, 0))],
        out_shape=[SDS((T, D), BF16), SDS((T, D), BF16), SDS((T, KV_W), BF16),
                   SDS((nb, N_Q, CHUNK, CHUNK), BF16), SDS((nb, N_Q, CHUNK), F32)],
        compiler_params=_params(1),
    )(proj, proj, proj, proj, proj, cos, cos, sin, sin, sinks)


def _fwd_mix(a, att, proj, x, wa, wb, wo, g1, g2):
    T = x.shape[0]
    tm = min(T, 512)
    half = D // 2

    def body(a_ref, att_ref, ga0, ga1, gb0, gb1, x_ref, wa_ref, wb_ref, wo_ref, g1_ref, g2_ref,
             mg_ref, a2_ref, b2_ref, mix_ref, x1_ref, hf_ref):
        a2 = _nn(a_ref[...], wa_ref[...])
        b2 = _nn(att_ref[...], wb_ref[...])
        ga = jnp.concatenate([ga0[...], ga1[...]], axis=1).astype(F32)
        gb = jnp.concatenate([gb0[...], gb1[...]], axis=1).astype(F32)
        merged = (_sigmoid(ga) * a2 + _sigmoid(gb) * b2).astype(BF16)
        a2_ref[...] = a2.astype(BF16)
        b2_ref[...] = b2.astype(BF16)
        mg_ref[...] = merged
        mix = _nn(merged, wo_ref[...])
        mix_ref[...] = mix
        _, mh = _rms_stats(mix)
        x1 = x_ref[...] + mh * g1_ref[...]
        x1_ref[...] = x1
        _, xh = _rms_stats(x1)
        hf_ref[...] = (xh * g2_ref[...]).astype(BF16)

    row = lambda i: (i, 0)
    const = lambda i: (0, 0)
    gspec = lambda off: pl.BlockSpec((tm, half), lambda i: (i, off // half))
    return pl.pallas_call(
        body, name="fwd_mix", grid=(T // tm,),
        in_specs=[pl.BlockSpec((tm, D), row), pl.BlockSpec((tm, D), row),
                  gspec(OFF_GA), gspec(OFF_GA + half), gspec(OFF_GB), gspec(OFF_GB + half),
                  pl.BlockSpec((tm, D), row), _resident((D, D)), _resident((D, D)),
                  _resident((D, D)), pl.BlockSpec((1, D), const), pl.BlockSpec((1, D), const)],
        out_specs=[pl.BlockSpec((tm, D), row)] * 6,
        out_shape=[SDS((T, D), BF16), SDS((T, D), BF16), SDS((T, D), BF16), SDS((T, D), F32), SDS((T, D), F32),
                   SDS((T, D), BF16)],
        compiler_params=_params(1),
    )(a, att, proj, proj, proj, proj, x, wa, wb, wo, g1, g2)


FF_SPLIT = N_DEV
FF_TILE = D_FF // FF_SPLIT


def _fwd_ff(hf, wfi3, wfo, x1, tgt, g3):
    T = hf.shape[0]
    tm = min(T, 512)

    def body(hf_ref, wfi_ref, wfo_ref, x1_ref, tgt_ref, g3_ref, f_ref, dy_ref, dff_ref, dg3_ref, loss_ref, r_s):
        @pl.when(pl.program_id(0) == 0)
        def _():
            dg3_ref[...] = jnp.zeros_like(dg3_ref)
            loss_ref[...] = jnp.zeros_like(loss_ref)

        hf_t = hf_ref[...]
        for s in range(FF_SPLIT):
            cols = slice(s * FF_TILE, (s + 1) * FF_TILE)
            f = _nn(hf_t, wfi_ref[s]).astype(BF16)
            f_ref[:, cols] = f
            rl = jnp.maximum(f.astype(F32), 0.0)
            r_s[:, cols] = (rl * rl).astype(BF16)
        r3, fh = _rms_stats(_nn(r_s[...], wfo_ref[...]))
        e = x1_ref[...] + fh * g3_ref[...] - tgt_ref[...]
        loss_ref[...] += jnp.sum(e * e) * (0.5 / D)
        dy = e * (1.0 / D)
        dy_ref[...] = dy
        dg3_ref[...] += _colsum(dy * fh)
        dff_ref[...] = _rms_bwd(dy, fh, r3, g3_ref[...]).astype(BF16)

    row = lambda i: (i, 0)
    const = lambda i: (0, 0)
    return pl.pallas_call(
        body, name="fwd_ff", grid=(T // tm,),
        in_specs=[pl.BlockSpec((tm, D), row), _resident((FF_SPLIT, D, FF_TILE)), _resident((D_FF, D)),
                  pl.BlockSpec((tm, D), row),
                  pl.BlockSpec((tm, D), row), pl.BlockSpec((1, D), const)],
        out_specs=[pl.BlockSpec((tm, D_FF), row), pl.BlockSpec((tm, D), row),
                   pl.BlockSpec((tm, D), row), pl.BlockSpec((1, D), const), pl.BlockSpec((1, 128), const)],
        out_shape=[SDS((T, D_FF), BF16), SDS((T, D), F32), SDS((T, D), BF16), SDS((1, D), F32), SDS((1, 128), F32)],
        scratch_shapes=[pltpu.VMEM((tm, D_FF), BF16)],
        compiler_params=_params(1),
    )(hf, wfi3, wfo, x1, tgt, g3)


def _bwd_ff(dff, f, wfi3, wfo, x1, dy, mix, g1, g2):
    T = dff.shape[0]
    tm = min(T, 512)

    def body(dff_ref, f_ref, wfi_ref, wfo_ref, x1_ref, dy_ref, mix_ref, g1_ref, g2_ref,
             df_ref, dx1_ref, dmix_ref, dg2_ref, dg1_ref):
        @pl.when(pl.program_id(0) == 0)
        def _():
            dg2_ref[...] = jnp.zeros_like(dg2_ref)
            dg1_ref[...] = jnp.zeros_like(dg1_ref)

        dff_t = dff_ref[...]
        dhf = None
        for s in range(FF_SPLIT):
            cols = slice(s * FF_TILE, (s + 1) * FF_TILE)
            dr = _nt(dff_t, wfo_ref[cols, :])
            df = (dr * (2.0 * jnp.maximum(f_ref[:, cols].astype(F32), 0.0))).astype(BF16)
            df_ref[:, cols] = df
            part = _nt(df, wfi_ref[s])
            dhf = part if dhf is None else dhf + part
        r2, xh = _rms_stats(x1_ref[...])
        dg2_ref[...] += _colsum(dhf * xh)
        dx1 = dy_ref[...] + _rms_bwd(dhf, xh, r2, g2_ref[...])
        dx1_ref[...] = dx1
        r1, mh = _rms_stats(mix_ref[...])
        dg1_ref[...] += _colsum(dx1 * mh)
        dmix_ref[...] = _rms_bwd(dx1, mh, r1, g1_ref[...]).astype(BF16)

    row = lambda i: (i, 0)
    const = lambda i: (0, 0)
    return pl.pallas_call(
        body, name="bwd_ff", grid=(T // tm,),
        in_specs=[pl.BlockSpec((tm, D), row), pl.BlockSpec((tm, D_FF), row),
                  _resident((FF_SPLIT, D, FF_TILE)), _resident((D_FF, D)),
                  pl.BlockSpec((tm, D), row), pl.BlockSpec((tm, D), row), pl.BlockSpec((tm, D), row),
                  pl.BlockSpec((1, D), const), pl.BlockSpec((1, D), const)],
        out_specs=[pl.BlockSpec((tm, D_FF), row), pl.BlockSpec((tm, D), row),
                   pl.BlockSpec((tm, D), row), pl.BlockSpec((1, D), const), pl.BlockSpec((1, D), const)],
        out_shape=[SDS((T, D_FF), BF16), SDS((T, D), F32), SDS((T, D), BF16), SDS((1, D), F32), SDS((1, D), F32)],
        compiler_params=_params(1),
    )(dff, f, wfi3, wfo, x1, dy, mix, g1, g2)


def _wgrad_ff(hf, df, f, dff):
    T = hf.shape[0]
    tt = min(T, 2048)
    slabs = 2
    wide = slabs * FF_TILE

    def body(hf_ref, df_ref, f_ref, dff_ref, dwfi_ref, dwfo_ref, acc_i, acc_o):
        t = pl.program_id(1)

        @pl.when(t == 0)
        def _():
            acc_i[...] = jnp.zeros_like(acc_i)
            acc_o[...] = jnp.zeros_like(acc_o)

        acc_i[...] += _tn(hf_ref[...], df_ref[...])
        rl = jnp.maximum(f_ref[...].astype(F32), 0.0)
        acc_o[...] += _tn((rl * rl).astype(BF16), dff_ref[...])

        @pl.when(t == T // tt - 1)
        def _():
            for s in range(slabs):
                dwfi_ref[s] = acc_i[:, s * FF_TILE:(s + 1) * FF_TILE].astype(BF16)
            dwfo_ref[...] = acc_o[...].astype(BF16)

    return pl.pallas_call(
        body, name="wgrad_ff", grid=(D_FF // wide, T // tt),
        in_specs=[pl.BlockSpec((tt, D), lambda p, t: (t, 0)), pl.BlockSpec((tt, wide), lambda p, t: (t, p)),
                  pl.BlockSpec((tt, wide), lambda p, t: (t, p)), pl.BlockSpec((tt, D), lambda p, t: (t, 0))],
        out_specs=[pl.BlockSpec((slabs, D, FF_TILE), lambda p, t: (p, 0, 0)), pl.BlockSpec((wide, D), lambda p, t: (p, 0))],
        out_shape=[SDS((FF_SPLIT, D, FF_TILE), BF16), SDS((D_FF, D), BF16)],
        scratch_shapes=[pltpu.VMEM((D, wide), F32), pltpu.VMEM((wide, D), F32)],
        compiler_params=_params(2),
    )(hf, df, f, dff)


def _bwd_mix(dmix, proj, a2, b2, merged, a, att, wo, wa, wb, after=None):
    T = dmix.shape[0]
    tm = min(T, 512)
    half = D // 2
    last = T // tm - 1

    def body(dmix_ref, ga0, ga1, gb0, gb1, a2_ref, b2_ref, mg_ref, a_ref, att_ref, wo_ref, wa_ref, wb_ref,
             dg_ref, da_ref, datt_ref, dwo_ref, dwa_ref, dwb_ref, acc, stage, sem):
        t = pl.program_id(0)

        @pl.when(t == 0)
        def _():
            acc[...] = jnp.zeros_like(acc)

        dmix_t = dmix_ref[...]
        dmg = _nt(dmix_t, wo_ref[...])
        sa = _sigmoid(jnp.concatenate([ga0[...], ga1[...]], axis=1).astype(F32))
        sb = _sigmoid(jnp.concatenate([gb0[...], gb1[...]], axis=1).astype(F32))
        da2 = (dmg * sa).astype(BF16)
        db2 = (dmg * sb).astype(BF16)
        dg_ref[:, :D] = (dmg * a2_ref[...].astype(F32) * (sa * (1.0 - sa))).astype(BF16)
        dg_ref[:, D:] = (dmg * b2_ref[...].astype(F32) * (sb * (1.0 - sb))).astype(BF16)
        da_ref[...] = _nt(da2, wa_ref[...]).astype(BF16)
        datt_ref[...] = _nt(db2, wb_ref[...]).astype(BF16)
        acc[0] += _tn(mg_ref[...], dmix_t)
        acc[1] += _tn(a_ref[...], da2)
        acc[2] += _tn(att_ref[...], db2)

        @pl.when(t == last)
        def _():
            for k, dw_ref in enumerate((dwo_ref, dwa_ref, dwb_ref)):
                stage[...] = acc[k].astype(BF16)
                out = pltpu.make_async_copy(stage, dw_ref, sem)
                out.start()
                out.wait()

    row = lambda i: (i, 0)
    gspec = lambda off: pl.BlockSpec((tm, half), lambda i: (i, off // half))
    body, dep_specs, deps = _after(body, 13, after)
    return pl.pallas_call(
        body, name="bwd_mix", grid=(T // tm,),
        in_specs=[pl.BlockSpec((tm, D), row), gspec(OFF_GA), gspec(OFF_GA + half), gspec(OFF_GB), gspec(OFF_GB + half)]
        + [pl.BlockSpec((tm, D), row)] * 5 + [_resident((D, D))] * 3 + dep_specs,
        out_specs=[pl.BlockSpec((tm, 2 * D), row), pl.BlockSpec((tm, D), row), pl.BlockSpec((tm, D), row)] + [_ANY] * 3,
        out_shape=[SDS((T, 2 * D), BF16), SDS((T, D), BF16), SDS((T, D), BF16)] + [SDS((D, D), BF16)] * 3,
        scratch_shapes=[pltpu.VMEM((3, D, D), F32), pltpu.VMEM((D, D), BF16), pltpu.SemaphoreType.DMA],
        compiler_params=_params(1),
    )(dmix, proj, proj, proj, proj, a2, b2, merged, a, att, wo, wa, wb, *deps)


def _bwd_attn(qr, kr, probs, psink, proj, cos, sin, datt, after=None):
    T = proj.shape[0]
    nb = T // CHUNK
    cur = lambda i: jnp.minimum(i, nb - 1)
    prev = lambda i: jnp.maximum(jnp.minimum(i, nb - 1) - 1, 0)

    def body(q_ref, kp_ref, kc_ref, vp_ref, vc_ref, cp_ref, cc_ref, sp_ref, sc_ref, p_ref, psink_ref, do_ref,
             dq_ref, dkv_ref, dsink_ref, carry_k, carry_v):
        i = pl.program_id(0)

        @pl.when(i == 0)
        def _():
            carry_k[...] = jnp.zeros_like(carry_k)
            carry_v[...] = jnp.zeros_like(carry_v)
            dsink_ref[...] = jnp.zeros_like(dsink_ref)

        @pl.when(i < nb)
        def _():
            prev_slot, _ = _fold_masks(i == 0)
            c_band, s_band = _band(cp_ref, cc_ref), _band(sp_ref, sc_ref)
            lane = lax.broadcasted_iota(jnp.int32, (1, 128), 1)
            dsink = jnp.zeros((1, 128), F32)
            for j in range(KV_W // 128):
                cols = slice(j * 128, (j + 1) * 128)
                k_slab = _band(kp_ref, kc_ref, cols).astype(F32)
                v_slab = _band(vp_ref, vc_ref, cols).astype(F32)
                dk_slab = jnp.zeros((2 * CHUNK, 128), F32)
                dv_slab = jnp.zeros((2 * CHUNK, 128), F32)
                for g in (2 * j, 2 * j + 1):
                    k2 = _head_pair_operand(k_slab, g)
                    v2 = _head_pair_operand(v_slab, g)
                    pairs = [g * PAIRS_PER_KV + r for r in range(PAIRS_PER_KV)]
                    q_stack = jnp.concatenate([q_ref[:, pr * 128:(pr + 1) * 128] for pr in pairs], axis=0)
                    do_stack = jnp.concatenate([do_ref[:, pr * 128:(pr + 1) * 128] for pr in pairs], axis=0)
                    dp2 = _nt(v2, do_stack)
                    pcols, dscols = [], []
                    for r, pair in enumerate(pairs):
                        ps, dss = [], []
                        for e in range(2):
                            head = 2 * pair + e
                            p_b = p_ref[head]
                            p = p_b.astype(F32)
                            dp = _fold(dp2[e * 2 * CHUNK:(e + 1) * 2 * CHUNK, r * 128:(r + 1) * 128], prev_slot)
                            delta = jnp.sum(p * dp, axis=0, keepdims=True)
                            ps.append(_unfold(p_b, prev_slot))
                            dss.append(_unfold((p * (dp - delta)).astype(BF16), prev_slot))
                            dsink = dsink + jnp.where(lane == head, -jnp.sum(psink_ref[head:head + 1, :] * delta), 0.0)
                        pcols.append(jnp.concatenate(ps, axis=0))
                        dscols.append(jnp.concatenate(dss, axis=0))
                    ds2 = jnp.concatenate(dscols, axis=1)
                    dq = _tn(ds2, k2) * (HEAD ** -0.5)
                    for r, pair in enumerate(pairs):
                        dq_ref[:, pair * 128:(pair + 1) * 128] = _rope_bwd(
                            dq[r * CHUNK:(r + 1) * CHUNK], cc_ref[...], sc_ref[...]).astype(BF16)
                    dk_slab = dk_slab + _head_pair_gradient(_nn(ds2, q_stack), g)
                    dv_slab = dv_slab + _head_pair_gradient(_nn(jnp.concatenate(pcols, axis=1), do_stack), g)
                dk_slab = _rope_bwd(dk_slab, c_band, s_band)
                vcols = slice(KV_W + j * 128, KV_W + (j + 1) * 128)
                dkv_ref[:, cols] = (carry_k[:, cols] + dk_slab[:CHUNK]).astype(BF16)
                dkv_ref[:, vcols] = (carry_v[:, cols] + dv_slab[:CHUNK]).astype(BF16)
                carry_k[:, cols] = dk_slab[CHUNK:]
                carry_v[:, cols] = dv_slab[CHUNK:]
            dsink_ref[...] += dsink

        @pl.when(i == nb)
        def _():
            dkv_ref[:, :KV_W] = carry_k[...].astype(BF16)
            dkv_ref[:, KV_W:] = carry_v[...].astype(BF16)

    table = lambda which, width: pl.BlockSpec((CHUNK, width), lambda i: (which(i), 0))
    body, dep_specs, deps = _after(body, 12, after)
    return pl.pallas_call(
        body, name="bwd_attn", grid=(nb + 1,),
        in_specs=[pl.BlockSpec((CHUNK, D), lambda i: (cur(i), 0)),
                  pl.BlockSpec((CHUNK, KV_W), lambda i: (prev(i), 0)),
                  pl.BlockSpec((CHUNK, KV_W), lambda i: (cur(i), 0)),
                  pl.BlockSpec((CHUNK, KV_W), lambda i: (prev(i), OFF_VA // KV_W)),
                  pl.BlockSpec((CHUNK, KV_W), lambda i: (cur(i), OFF_VA // KV_W)),
                  table(prev, 128), table(cur, 128), table(prev, 256), table(cur, 256),
                  pl.BlockSpec((None, N_Q, CHUNK, CHUNK), lambda i: (cur(i), 0, 0, 0)),
                  pl.BlockSpec((None, N_Q, CHUNK), lambda i: (cur(i), 0, 0)),
                  pl.BlockSpec((CHUNK, D), lambda i: (cur(i), 0))] + dep_specs,
        out_specs=[pl.BlockSpec((CHUNK, D), lambda i: (cur(i), 0)),
                   pl.BlockSpec((CHUNK, 2 * KV_W), lambda i: (jnp.maximum(i - 1, 0), 0)),
                   pl.BlockSpec((1, 128), lambda i: (0, 0))],
        out_shape=[SDS((T, D), BF16), SDS((T, 2 * KV_W), BF16), SDS((1, 128), F32)],
        scratch_shapes=[pltpu.VMEM((CHUNK, KV_W), F32), pltpu.VMEM((CHUNK, KV_W), F32)],
        compiler_params=_params(1),
    )(qr, kr, kr, proj, proj, cos, cos, sin, sin, probs, psink, datt, *deps)


def _bwd_sgu(proj, tanhs, stats, da, lng, lnb, ws, bst):
    T = proj.shape[0]
    tc = min(T, 512)
    nsteps = T // tc

    def body(u_ref, vs_ref, t_ref, stat_ref, da_ref, lng_ref, lnb_ref, ws_ref, bst_ref,
             duv_ref, dws_ref, dbs_ref, dlng_ref, dlnb_ref, dvn_s, dgu_s, dmx_sum):
        i = pl.program_id(0)

        @pl.when(i == 0)
        def _():
            dws_ref[...] = jnp.zeros_like(dws_ref)
            dlng_ref[...] = jnp.zeros_like(dlng_ref)
            dlnb_ref[...] = jnp.zeros_like(dlnb_ref)
            dmx_sum[...] = jnp.zeros_like(dmx_sum)

        u, vs, gu, tu, tv, rstd, vhat, vn = _sgu_forward_replay(u_ref, vs_ref, t_ref, stat_ref, lng_ref, lnb_ref)
        da = da_ref[...].astype(F32)
        for g in range(GROUPS):
            wm = _masked_ws(ws_ref, g)
            cols = slice(g * CHUNK, (g + 1) * CHUNK)
            dws = jnp.zeros((CHUNK, CHUNK), F32)
            dsum = jnp.zeros((CHUNK, CHUNK), F32)
            for c in range(tc // CHUNK):
                rows = slice(c * CHUNK, (c + 1) * CHUNK)
                vn_cg = vn[rows, cols]
                mixed = _nn(wm, vn_cg) + bst_ref[:, g:g + 1]
                dgu_s[rows, cols] = da[rows, cols] * mixed
                dmx = da[rows, cols] * gu[rows, cols]
                dmxb = dmx.astype(BF16)
                dws = dws + _nt(dmxb, vn_cg)
                dsum = dsum + dmx
                dvn_s[rows, cols] = _tn(wm, dmxb)
            dws_ref[g] += dws
            dmx_sum[:, cols] += dsum
        dvn = dvn_s[...]
        dlng_ref[...] += _colsum(dvn * vhat)
        dlnb_ref[...] += _colsum(dvn)
        dvh = dvn * lng_ref[...]
        dgv = rstd * (dvh - jnp.mean(dvh, axis=-1, keepdims=True) - vhat * jnp.mean(dvh * vhat, axis=-1, keepdims=True))
        duv_ref[:, :D] = (dgu_s[...] * _gelu_grad(u, tu)).astype(BF16)
        duv_ref[:, D:] = (dgv * _gelu_grad(vs, tv)).astype(BF16)

        @pl.when(i == nsteps - 1)
        def _():
            row = lax.broadcasted_iota(jnp.int32, (CHUNK, CHUNK), 0)
            col = lax.broadcasted_iota(jnp.int32, (CHUNK, CHUNK), 1)
            for g in range(GROUPS):
                dws_ref[g] = jnp.where(row >= col, dws_ref[g], 0.0)
                dbs_ref[g:g + 1, :] = _colsum(dmx_sum[:, g * CHUNK:(g + 1) * CHUNK].T)

    const2 = lambda i: (0, 0)
    return pl.pallas_call(
        body, name="bwd_sgu", grid=(nsteps,),
        in_specs=[pl.BlockSpec((tc, D), lambda i: (i, 0)), pl.BlockSpec((tc, D), lambda i: (i, 1)),
                  pl.BlockSpec((tc, 2 * D), lambda i: (i, 0)), pl.BlockSpec((tc, 128), lambda i: (i, 0)),
                  pl.BlockSpec((tc, D), lambda i: (i, 0)), pl.BlockSpec((1, D), const2), pl.BlockSpec((1, D), const2),
                  pl.BlockSpec((GROUPS, CHUNK, CHUNK), lambda i: (0, 0, 0)), pl.BlockSpec((CHUNK, GROUPS), const2)],
        out_specs=[pl.BlockSpec((tc, 2 * D), lambda i: (i, 0)), pl.BlockSpec((GROUPS, CHUNK, CHUNK), lambda i: (0, 0, 0)),
                   pl.BlockSpec((GROUPS, CHUNK), const2), pl.BlockSpec((1, D), const2), pl.BlockSpec((1, D), const2)],
        out_shape=[SDS((T, 2 * D), BF16), SDS((GROUPS, CHUNK, CHUNK), F32), SDS((GROUPS, CHUNK), F32),
                   SDS((1, D), F32), SDS((1, D), F32)],
        scratch_shapes=[pltpu.VMEM((tc, D), F32), pltpu.VMEM((tc, D), F32), pltpu.VMEM((CHUNK, D), F32)],
        compiler_params=_params(1),
    )(proj, proj, tanhs, stats, da, lng, lnb, ws, bst)


IN_SEG_WIDTHS = (2 * D, D, 2 * N_KV * HEAD, 2 * D)


def _resident(shape):
    return pl.BlockSpec(shape, lambda *_: (0,) * len(shape), pipeline_mode=pl.Buffered(1))


def _bwd_in(duv, dq, dkv, dg, win_t, x, dx1, g0, after=None):
    T = x.shape[0]
    tm = min(T, 512)

    def body(duv_ref, dq_ref, dkv_ref, dg_ref, w_ref, x_ref, dx1_ref, g0_ref, gx_ref, dg0_ref):
        @pl.when(pl.program_id(0) == 0)
        def _():
            dg0_ref[...] = jnp.zeros_like(dg0_ref)

        dh, off = None, 0
        for ref, width in zip((duv_ref, dq_ref, dkv_ref, dg_ref), IN_SEG_WIDTHS):
            part = _nn(ref[...], w_ref[off:off + width, :])
            dh = part if dh is None else dh + part
            off += width
        r0, xh = _rms_stats(x_ref[...])
        dg0_ref[...] += _colsum(dh * xh)
        gx_ref[...] = dx1_ref[...] + _rms_bwd(dh, xh, r0, g0_ref[...])

    row = lambda i: (i, 0)
    body, dep_specs, deps = _after(body, 8, after)
    return pl.pallas_call(
        body, name="bwd_in", grid=(T // tm,),
        in_specs=[pl.BlockSpec((tm, w), row) for w in IN_SEG_WIDTHS] + [
            _resident((IN_W, D)), pl.BlockSpec((tm, D), row), pl.BlockSpec((tm, D), row),
            pl.BlockSpec((1, D), lambda i: (0, 0))] + dep_specs,
        out_specs=[pl.BlockSpec((tm, D), row), pl.BlockSpec((1, D), lambda i: (0, 0))],
        out_shape=[SDS((T, D), F32), SDS((1, D), F32)],
        compiler_params=_params(1),
    )(duv, dq, dkv, dg, win_t, x, dx1, g0, *deps)


def _wgrad_rows(h, segs, first_row, into, name):
    T = h.shape[0]
    tt = min(T, 2048)
    widths = [s.shape[1] for s in segs]
    rows = sum(widths)
    n_in = 1 + len(segs) + (into is not None)

    def body(*refs):
        h_ref, seg_refs = refs[0], refs[1:1 + len(segs)]
        dw_ref, acc, stage, sem = refs[n_in], refs[n_in + 1], refs[n_in + 2], refs[n_in + 3]
        t = pl.program_id(0)

        @pl.when(t == 0)
        def _():
            acc[...] = jnp.zeros_like(acc)

        off = 0
        for ref, width in zip(seg_refs, widths):
            acc[off:off + width, :] += _tn(ref[...], h_ref[...])
            off += width

        @pl.when(t == T // tt - 1)
        def _():
            stage[...] = acc[...].astype(BF16)
            out = pltpu.make_async_copy(stage, dw_ref.at[pl.ds(first_row, rows)], sem)
            out.start()
            out.wait()

    row = lambda t: (t, 0)
    return pl.pallas_call(
        body, name=name, grid=(T // tt,),
        in_specs=[pl.BlockSpec((tt, D), row)] + [pl.BlockSpec((tt, w), row) for w in widths] + [_ANY] * (into is not None),
        out_specs=_ANY,
        out_shape=SDS((IN_W, D), BF16),
        input_output_aliases={} if into is None else {n_in - 1: 0},
        scratch_shapes=[pltpu.VMEM((rows, D), F32), pltpu.VMEM((rows, D), BF16), pltpu.SemaphoreType.DMA],
        compiler_params=_params(1),
    )(h, *segs, *([] if into is None else [into]))


def _place():
    x, y, c = lax.axis_index("x"), lax.axis_index("y"), lax.axis_index("c")
    return x, y, c, 4 * x + 2 * y + c


def _peers(x, y, c):
    out = []
    for mask in range(1, N_DEV):
        px = 1 - x if mask & 4 else x
        py = 1 - y if mask & 2 else y
        pc = 1 - c if mask & 1 else c
        out.append(((px, py, pc), 4 * px + 2 * py + pc))
    return out


def _all_to_all(arrays, gather, name, after=None):
    n = len(arrays)

    def body(*refs):
        ins, outs = refs[:n], refs[n:2 * n]
        send_sems, recv_sems, local_sems = refs[2 * n:]
        x, y, c, me = _place()
        local, sends, recvs = [], [], []
        for a in range(n):
            src_own = ins[a] if gather[a] else ins[a].at[me]
            local.append(pltpu.make_async_copy(src_own, outs[a].at[me], local_sems.at[a]))
            for k, (peer, pid) in enumerate(_peers(x, y, c)):
                sem = a * (N_DEV - 1) + k
                src = ins[a] if gather[a] else ins[a].at[pid]
                sends.append(pltpu.make_async_remote_copy(
                    src_ref=src, dst_ref=outs[a].at[me], send_sem=send_sems.at[sem], recv_sem=recv_sems.at[sem],
                    device_id=peer, device_id_type=MESH))
                recvs.append(pltpu.make_async_remote_copy(
                    src_ref=src, dst_ref=outs[a].at[pid], send_sem=send_sems.at[sem], recv_sem=recv_sems.at[sem],
                    device_id=peer, device_id_type=MESH))
        for cp in local + sends:
            cp.start()
        for cp in recvs:
            cp.wait_recv()
        for cp in sends:
            cp.wait_send()
        for cp in local:
            cp.wait()

    out_shape = [SDS((N_DEV,) + a.shape if gt else a.shape, a.dtype) for a, gt in zip(arrays, gather)]
    nsem = n * (N_DEV - 1)
    body, dep_specs, deps = _after(body, n, after)
    return pl.pallas_call(
        body, name=name,
        in_specs=[pl.BlockSpec(memory_space=pl.ANY)] * n + dep_specs,
        out_specs=[pl.BlockSpec(memory_space=pl.ANY)] * n,
        out_shape=out_shape,
        scratch_shapes=[pltpu.SemaphoreType.DMA((nsem,)), pltpu.SemaphoreType.DMA((nsem,)), pltpu.SemaphoreType.DMA((n,))],
    )(*arrays, *deps)


_HBM = pl.BlockSpec(memory_space=pltpu.HBM)
_SEM = pl.BlockSpec(memory_space=pltpu.SEMAPHORE)
_EFFECT = pltpu.SideEffectType.DATAFLOW_SIDE_EFFECTING
GATHER = "gather"
SCATTER = "scatter"
SPREAD = "spread"


def _zone_shape(a, mode):
    if mode == GATHER:
        return (N_DEV,) + a.shape
    return (N_DEV - 1,) + (a.shape[1:] if mode == SCATTER else a.shape)


def _start_copies(arrays, modes, name, after=None):
    n = len(arrays)
    zones = [lax.empty(_zone_shape(a, m), a.dtype) for a, m in zip(arrays, modes)]

    def body(*refs):
        ins, lands = refs[:n], refs[n:2 * n]
        send_sems, recv_sems = refs[-2 * n - 3], refs[-2 * n - 2]
        token = refs[-1]
        x, y, c, me = _place()
        for a in range(n):
            for k, (peer, pid) in enumerate(_peers(x, y, c)):
                src = ins[a].at[pid] if modes[a] == SCATTER else ins[a]
                dst = lands[a].at[me] if modes[a] == GATHER else lands[a].at[k]
                pltpu.make_async_remote_copy(src_ref=src, dst_ref=dst, send_sem=send_sems.at[a], recv_sem=recv_sems.at[a],
                                             device_id=peer, device_id_type=MESH).start()
            if modes[a] == GATHER:
                pltpu.make_async_remote_copy(src_ref=ins[a], dst_ref=lands[a].at[me], send_sem=send_sems.at[a],
                                             recv_sem=recv_sems.at[a], device_id=(x, y, c), device_id_type=MESH).start()
        token[...] = jnp.zeros_like(token)

    hbm = lambda a: pltpu.HBM(a.shape, a.dtype)
    sems = pltpu.SemaphoreType.DMA((n,))
    extra = [] if after is None else [after]
    operands = [pltpu.with_memory_space_constraint(a, pltpu.HBM) for a in list(arrays) + zones]
    res = pl.pallas_call(
        body, name=name,
        out_shape=(sems, sems, *[hbm(a) for a in arrays], *[hbm(z) for z in zones], SDS((8, 128), F32)),
        in_specs=[_HBM] * (2 * n) + [_ANY] * len(extra),
        out_specs=(_SEM, _SEM, *[_HBM] * (2 * n), pl.BlockSpec(memory_space=pltpu.VMEM)),
        input_output_aliases={i: 2 + i for i in range(2 * n)},
        compiler_params=pltpu.CompilerParams(has_side_effects=_EFFECT),
    )(*operands, *extra)
    return res[0], res[1], list(res[2:2 + n]), list(res[2 + n:2 + 2 * n]), res[-1]


def _wait_copies(started, after, name, count=N_DEV - 1):
    send_sems, recv_sems, thru, zones, _ = started
    nt, nz = len(thru), len(zones)

    def body(*refs):
        lands = refs[nt:nt + nz]
        send_ref, recv_ref = refs[nt + nz], refs[nt + nz + 1]
        x, y, c, _ = _place()
        for a in range(nz):
            blocks = lands[a].at[pl.ds(0, count)]
            cp = pltpu.make_async_remote_copy(src_ref=blocks, dst_ref=blocks, send_sem=send_ref.at[a], recv_sem=recv_ref.at[a],
                                              device_id=(x, y, 1 - c), device_id_type=MESH)
            cp.wait_send()
            cp.wait_recv()

    hbm = lambda a: pltpu.HBM(a.shape, a.dtype)
    res = pl.pallas_call(
        body, name=name,
        out_shape=tuple(hbm(a) for a in thru + zones),
        in_specs=[_HBM] * (nt + nz) + [_SEM, _SEM, _ANY],
        out_specs=tuple([_HBM] * (nt + nz)),
        input_output_aliases={i: i for i in range(nt + nz)},
        compiler_params=pltpu.CompilerParams(has_side_effects=_EFFECT),
    )(*thru, *zones, send_sems, recv_sems, after)
    return list(res[:nt]), list(res[nt:])


def _split_start(body, arrays, zones, name, after):
    n = len(arrays) + len(zones)
    hbm = lambda a: pltpu.HBM(a.shape, a.dtype)
    sems = pltpu.SemaphoreType.DMA((max(len(zones), 1),))
    extra = [] if after is None else [after]
    operands = [pltpu.with_memory_space_constraint(a, pltpu.HBM) for a in list(arrays) + list(zones)]
    res = pl.pallas_call(
        body, name=name,
        out_shape=(sems, sems, *[hbm(a) for a in operands], SDS((8, 128), F32)),
        in_specs=[_HBM] * n + [_ANY] * len(extra),
        out_specs=(_SEM, _SEM, *[_HBM] * n, pl.BlockSpec(memory_space=pltpu.VMEM)),
        input_output_aliases={i: 2 + i for i in range(n)},
        compiler_params=pltpu.CompilerParams(has_side_effects=_EFFECT),
    )(*operands, *extra)
    return res[0], res[1], list(res[2:2 + len(arrays)]), list(res[2 + len(arrays):2 + n]), res[-1]


def _gather_first_leg(shard, name, after=None):
    zone = lax.empty((N_DEV,) + shard.shape, shard.dtype)
    extra = 0 if after is None else 1

    def body(*refs):
        src, land = refs[0], refs[1]
        send_sem, recv_sem, token = refs[2 + extra], refs[3 + extra], refs[-1]
        x, y, c, me = _place()
        for peer in ((x, y, c), (x, y, 1 - c), (1 - x, y, c), (x, 1 - y, c), (1 - x, 1 - y, c)):
            pltpu.make_async_remote_copy(src_ref=src, dst_ref=land.at[me], send_sem=send_sem.at[0], recv_sem=recv_sem.at[0],
                                         device_id=peer, device_id_type=MESH).start()
        token[...] = jnp.zeros_like(token)

    return _split_start(body, [shard], [zone], name, after)


def _gather_second_leg(zone, name, after=None):
    extra = 0 if after is None else 1

    def body(*refs):
        land = refs[0]
        send_sem, recv_sem, token = refs[1 + extra], refs[2 + extra], refs[-1]
        x, y, c, _ = _place()
        for px, py in ((1 - x, y), (x, 1 - y), (1 - x, 1 - y)):
            slot = 4 * px + 2 * py + c
            pltpu.make_async_remote_copy(src_ref=land.at[slot], dst_ref=land.at[slot], send_sem=send_sem.at[0],
                                         recv_sem=recv_sem.at[0], device_id=(x, y, 1 - c), device_id_type=MESH).start()
        token[...] = jnp.zeros_like(token)

    return _split_start(body, [], [zone], name, after)


UPDATE_BLOCK_ELEMS = 256 * 1024


def _update_rows(R, C):
    fits = [t for t in range(8, R + 1, 8) if R % t == 0 and t * C <= UPDATE_BLOCK_ELEMS]
    whole = [t for t in fits if t % 16 == 0]
    return max(whole or fits)


def _adamw_math(g, w, m, v):
    m2 = ADAM_B1 * m + (1.0 - ADAM_B1) * g
    v2 = ADAM_B2 * v + (1.0 - ADAM_B2) * (g * g)
    m_hat = m2 / (1.0 - ADAM_B1 ** ADAM_STEP)
    v_hat = v2 / (1.0 - ADAM_B2 ** ADAM_STEP)
    delta = -ADAM_LR * (m_hat / (jnp.sqrt(v_hat) + ADAM_EPS) + ADAM_WD * w)
    return delta, m2, v2


def _sum_adamw(parts, w, m, v, name):
    R, C = w.shape
    tr = _update_rows(R, C)

    def body(p_ref, w_ref, m_ref, v_ref, g_ref, d_ref, m2_ref, v2_ref):
        g = p_ref[0]
        for k in range(1, N_DEV):
            g = g + p_ref[k]
        g_ref[...] = g
        d_ref[...], m2_ref[...], v2_ref[...] = _adamw_math(g, w_ref[...], m_ref[...], v_ref[...])

    blk = pl.BlockSpec((tr, C), lambda i: (i, 0))
    return pl.pallas_call(
        body, name=name, grid=(R // tr,),
        in_specs=[pl.BlockSpec((N_DEV, tr, C), lambda i: (0, i, 0)), blk, blk, blk],
        out_specs=[blk] * 4,
        out_shape=[SDS((R, C), F32)] * 4,
        compiler_params=_params(1),
    )(parts, w, m, v)


def _sum_adamw_peers(me, own, parts, w, m, v, name, replicated, also_rows=None):
    R, C = w.shape
    tr = _update_rows(R, C)
    assert also_rows is None or tr == R

    def body(me_ref, own_ref, p_ref, w_ref, m_ref, v_ref, g_ref, d_ref, m2_ref, v2_ref, *extra):
        if replicated:
            mine = me_ref[0]
            g = None
            for j in range(N_DEV):
                k = jnp.maximum(jnp.bitwise_xor(mine, j) - 1, 0)
                term = jnp.where(mine == j, own_ref[...], p_ref[k])
                g = term if g is None else g + term
        else:
            g = own_ref[...].astype(F32)
            for k in range(N_DEV - 1):
                g = g + p_ref[k].astype(F32)
        results = (g,) + _adamw_math(g, w_ref[...], m_ref[...], v_ref[...])
        for ref, val in zip((g_ref, d_ref, m2_ref, v2_ref), results):
            ref[...] = val
        for ref, val in zip(extra, results):
            ref[...] = val[also_rows[0]:also_rows[1]]

    blk = pl.BlockSpec((tr, C), lambda i, me_ref: (i, 0))
    own_spec = blk if replicated else pl.BlockSpec((None, tr, C), lambda i, me_ref: (me_ref[0], i, 0))
    n_also = 0 if also_rows is None else also_rows[1] - also_rows[0]
    also_specs = [pl.BlockSpec((n_also, C), lambda i, me_ref: (0, 0))] * (4 if also_rows else 0)
    return pl.pallas_call(
        body, name=name,
        grid_spec=pltpu.PrefetchScalarGridSpec(
            num_scalar_prefetch=1, grid=(R // tr,),
            in_specs=[own_spec, pl.BlockSpec((N_DEV - 1, tr, C), lambda i, me_ref: (0, i, 0)), blk, blk, blk],
            out_specs=[blk] * 4 + also_specs),
        out_shape=[SDS((R, C), F32)] * 4 + [SDS((n_also, C), F32)] * len(also_specs),
        compiler_params=_params(1),
    )(me, own, parts, w, m, v)


SMALL = ("ln_v_gain", "ln_v_bias", "w_spatial", "b_spatial", "sinks", "norm_mix_post", "norm_ff_pre", "norm_ff_post")
SMALL_ROWS = {"ln_v_gain": 8, "ln_v_bias": 8, "w_spatial": 1024, "b_spatial": 8, "sinks": 8,
              "norm_mix_post": 8, "norm_ff_pre": 8, "norm_ff_post": 8}
SMALL_PACK_ROWS = 1152


def _pack_small(vals):
    rows = []
    for name in SMALL:
        flat = vals[name].reshape(-1)
        pad = SMALL_ROWS[name] * 128 - flat.shape[0]
        if pad:
            flat = jnp.concatenate([flat, jnp.zeros((pad,), F32)])
        rows.append(flat.reshape(SMALL_ROWS[name], 128))
    rows.append(jnp.zeros((SMALL_PACK_ROWS - sum(SMALL_ROWS.values()), 128), F32))
    return jnp.concatenate(rows, axis=0)


def _unpack_small(packed, shapes):
    out, r = {}, 0
    for name in SMALL:
        n = 1
        for s in shapes[name]:
            n *= s
        out[name] = packed[r:r + SMALL_ROWS[name]].reshape(-1)[:n].reshape(shapes[name])
        r += SMALL_ROWS[name]
    return out


def _rope_rows():
    d = jnp.arange(128) % HEAD
    inv = ROPE_THETA ** (-(2.0 * (d % (ROPE // 2))).astype(F32) / ROPE)
    invf = jnp.where(d < ROPE, inv, 0.0).astype(F32).reshape(1, 128)
    sgn = jnp.where(d < ROPE // 2, -1.0, jnp.where(d < ROPE, 1.0, 0.0)).astype(F32).reshape(1, 128)
    return invf, sgn


def kernel(x, positions, w_in, ln_v_gain, ln_v_bias, w_spatial, b_spatial, sinks, w_a, w_b, w_o, norm_mix_pre, norm_mix_post, w_ff_in, w_ff_out, norm_ff_pre, norm_ff_post, loss_target, m_w_in, m_ln_v_gain, m_ln_v_bias, m_w_spatial, m_b_spatial, m_sinks, m_w_a, m_w_b, m_w_o, m_norm_mix_pre, m_norm_mix_post, m_w_ff_in, m_w_ff_out, m_norm_ff_pre, m_norm_ff_post, v_w_in, v_ln_v_gain, v_ln_v_bias, v_w_spatial, v_b_spatial, v_sinks, v_w_a, v_w_b, v_w_o, v_norm_mix_pre, v_norm_mix_post, v_w_ff_in, v_w_ff_out, v_norm_ff_pre, v_norm_ff_post):
    given = dict(locals())
    T = x.shape[1]
    xt = x[0]
    tgt = loss_target[0]
    bst = b_spatial[0].T
    ws = w_spatial[0]

    me = 4 * lax.axis_index("x") + 2 * lax.axis_index("y") + lax.axis_index("c")
    me_arr = me.astype(jnp.int32).reshape(1)

    rest = ("w_a", "w_b", "w_o", "w_ff_in", "w_ff_out")
    shard = {n: given[n][0].astype(BF16) for n in rest}
    g_one = _gather_first_leg(w_in[0].T.astype(BF16), "gather_in_start")
    cos, sin = _rope_tables(positions.astype(F32).reshape(T, 1), *_rope_rows(), after=g_one[-1])
    small_state = [_pack_small({n: given[k + n] for n in SMALL}) for k in ("", "m_", "v_")]
    h = _rms_pre(xt, norm_mix_pre, after=[cos, *small_state, *[shard[n] for n in rest]])
    _, (win8,) = _wait_copies(g_one, h, "gather_in_wait", count=5)
    g_two = _gather_second_leg(win8, "gather_in_pass_start")
    g_rest = _start_copies([shard[n] for n in rest], [GATHER] * len(rest), "gather_rest_start", after=g_two[-1])
    _, (win8,) = _wait_copies(g_two, g_rest[-1], "gather_in_pass_wait", count=3)
    win = win8.reshape(IN_W, D)

    proj = _fwd_in(h, win)
    att, qr, kr, probs, psink = _fwd_attn(proj, cos, sin, sinks[0])
    a, tanhs, ln_stats = _fwd_sgu(proj, ln_v_gain, ln_v_bias, ws, bst, after=att)
    gw = dict(zip(rest, _wait_copies(g_rest, a, "gather_rest_wait", count=N_DEV)[1]))
    wa, wb, wo = (gw[n].reshape(D, D) for n in ("w_a", "w_b", "w_o"))
    wfi3 = gw["w_ff_in"]
    wfo = gw["w_ff_out"].reshape(D_FF, D)
    merged, a2, b2, mix, x1, hf = _fwd_mix(a, att, proj, xt, wa, wb, wo, norm_mix_post, norm_ff_pre)
    f, dy, dff, dg3, loss_part = _fwd_ff(hf, wfi3, wfo, x1, tgt, norm_ff_post)

    df, dx1, dmix, dg2, dg1 = _bwd_ff(dff, f, wfi3, wfo, x1, dy, mix, norm_mix_post, norm_ff_pre)
    dwfi3, dwfo = _wgrad_ff(hf, df, f, dff)
    own_ff = [dwfi3, dwfo.reshape(N_DEV, D_FF // N_DEV, D)]
    x_ff = _start_copies(own_ff, [SCATTER] * 2, "exchange_ff_start")
    dgate, da, datt, dwo, dwa, dwb = _bwd_mix(dmix, proj, a2, b2, merged, a, att, wo, wa, wb, after=x_ff[-1])
    own_mix = [g.reshape(N_DEV, D // N_DEV, D) for g in (dwa, dwb, dwo)]
    x_mix = _start_copies(own_mix, [SCATTER] * 3, "exchange_mix_start")
    dq, dkv, dsink = _bwd_attn(qr, kr, probs, psink, proj, cos, sin, datt, after=x_mix[-1])
    duv, dws, dbs, dlng, dlnb = _bwd_sgu(proj, tanhs, ln_stats, da, ln_v_gain, ln_v_bias, ws, bst)
    small_grads = {"ln_v_gain": dlng, "ln_v_bias": dlnb, "w_spatial": dws, "b_spatial": dbs, "sinks": dsink[:, :N_Q],
                   "norm_mix_post": dg1, "norm_ff_pre": dg2, "norm_ff_post": dg3}
    x_small = _start_copies([_pack_small(small_grads)], [SPREAD], "exchange_small_start")
    dwin = _wgrad_rows(h, [dgate], sum(IN_SEG_WIDTHS[:3]), None, "wgrad_in_gates")
    dwin = _wgrad_rows(h, [duv], 0, dwin, "wgrad_in_uv")
    dwin = _wgrad_rows(h, [dq, dkv], IN_SEG_WIDTHS[0], dwin, "wgrad_in_qkv")
    own_in = [dwin.reshape(N_DEV, IN_W // N_DEV, D)]
    x_in = _start_copies(own_in, [SCATTER], "exchange_in_start", after=x_small[-1])
    grad_x, dg0 = _bwd_in(duv, dq, dkv, dgate, win, xt, dx1, norm_mix_pre, after=x_in[-1])

    results = {}

    def update(n, own, parts, transposed=False):
        state = [given[k + n][0].T if transposed else given[k + n][0] for k in ("", "m_", "v_")]
        res = _sum_adamw_peers(me_arr, own, parts, *state, "adamw_" + n, False)
        results[n] = [(r.T if transposed else r).reshape(given[n].shape) for r in res]

    own_ff, p_ff = _wait_copies(x_ff, grad_x, "exchange_ff_wait")
    update("w_ff_in", own_ff[0], p_ff[0])
    update("w_ff_out", own_ff[1], p_ff[1])
    own_mix, p_mix = _wait_copies(x_mix, results["w_ff_out"][0], "exchange_mix_wait")
    for n, own, parts in zip(("w_a", "w_b", "w_o"), own_mix, p_mix):
        update(n, own, parts)
    tail = jnp.concatenate([dg0.reshape(8, 128), jnp.tile(loss_part, (8, 1))], axis=0)
    (tail_all,) = _all_to_all([tail], [True], "exchange_tail", after=results["w_o"][0])
    dg0_all = tail_all[:, :8]
    own_small, p_small = _wait_copies(x_small, tail_all, "exchange_small_wait")
    own_in, p_in = _wait_copies(x_in, p_small[0], "exchange_in_wait")
    update("w_in", own_in[0], p_in[0], transposed=True)
    first = sum(SMALL_ROWS[n] for n in SMALL[:SMALL.index("w_spatial")])
    packed = _sum_adamw_peers(me_arr, own_small[0], p_small[0], *small_state, "adamw_small", True,
                              also_rows=(first, first + SMALL_ROWS["w_spatial"]))
    shapes = {n: given[n].shape for n in SMALL}
    unpacked = [_unpack_small(p, shapes) for p in packed[:4]]
    for n in SMALL:
        results[n] = [u[n] for u in unpacked]
    results["w_spatial"] = [r.reshape(w_spatial.shape) for r in packed[4:]]
    n = "norm_mix_pre"
    results[n] = [r.reshape(given[n].shape) for r in _sum_adamw(
        dg0_all, given[n].reshape(8, 128), given["m_" + n].reshape(8, 128), given["v_" + n].reshape(8, 128), "adamw_" + n)]

    loss = jnp.sum(tail_all[:, 8, 0])
    order = ("w_in", "ln_v_gain", "ln_v_bias", "w_spatial", "b_spatial", "sinks", "w_a", "w_b", "w_o", "norm_mix_pre",
             "norm_mix_post", "w_ff_in", "w_ff_out", "norm_ff_pre", "norm_ff_post")
    out = [loss, grad_x.reshape(x.shape)]
    for k in range(4):
        out += [results[n][k] for n in order]
    return tuple(out)
```

```python
import jax
import jax.numpy as jnp
from jax import lax
from jax.experimental import pallas as pl
from jax.experimental.pallas import tpu as pltpu

F32 = jnp.float32
BF16 = jnp.bfloat16

N_DEV = 8
D = 1024
D_FF = 4096
IN_W = 5632
CHUNK = 128
GROUPS = 8
HEAD = 64
N_Q = 16
N_KV = 4
ROPE = 16
ROPE_THETA = 500000.0
EPS = 1e-6
OFF_Q, OFF_K, OFF_VA, OFF_GA, OFF_GB = 2048, 3072, 3328, 3584, 4608

ADAM_LR = 0.001
ADAM_B1 = 0.9
ADAM_B2 = 0.999
ADAM_EPS = 1e-08
ADAM_WD = 0.01
ADAM_STEP = 10

VMEM_LIMIT = 62 * 1024 * 1024

SDS = jax.ShapeDtypeStruct
MESH = pl.DeviceIdType.MESH


def _params(n_axes):
    return pltpu.CompilerParams(dimension_semantics=("arbitrary",) * n_axes, vmem_limit_bytes=VMEM_LIMIT)


def _nt(a, b):
    return lax.dot_general(a, b, (((1,), (1,)), ((), ())), preferred_element_type=F32)


def _tn(a, b):
    return lax.dot_general(a, b, (((0,), (0,)), ((), ())), preferred_element_type=F32)


def _nn(a, b):
    return jnp.dot(a, b, preferred_element_type=F32)


def _gelu(x):
    t = jnp.tanh(0.7978845608028654 * (x + 0.044715 * (x * x * x)))
    return 0.5 * x * (1.0 + t), t


def _gelu_grad(x, t):
    return 0.5 * (1.0 + t) + 0.5 * x * (1.0 - t * t) * (0.7978845608028654 * (1.0 + 3.0 * 0.044715 * x * x))


def _sigmoid(x):
    return 1.0 / (1.0 + jnp.exp(-x))


def _rms_stats(v):
    r = lax.rsqrt(jnp.mean(v * v, axis=-1, keepdims=True) + EPS)
    return r, v * r


def _rms_bwd(d, vhat, r, g):
    gd = g * d
    return r * (gd - vhat * jnp.mean(gd * vhat, axis=-1, keepdims=True))


def _colsum(v):
    return jnp.sum(v, axis=0, keepdims=True)


_ANY = pl.BlockSpec(memory_space=pl.ANY)


def _after(body, n_in, after):
    if after is None:
        return body, [], []
    deps = list(after) if isinstance(after, (list, tuple)) else [after]

    def ordered(*refs):
        return body(*refs[:n_in], *refs[n_in + len(deps):])

    return ordered, [_ANY] * len(deps), deps


def _rms_pre(x, g0, after=None):
    T = x.shape[0]
    tm = min(T, 1024)

    def body(x_ref, g_ref, h_ref):
        _, xh = _rms_stats(x_ref[...])
        h_ref[...] = (xh * g_ref[...]).astype(BF16)

    body, dep_specs, deps = _after(body, 2, after)
    return pl.pallas_call(
        body, name="rms_pre", grid=(T // tm,),
        in_specs=[pl.BlockSpec((tm, D), lambda i: (i, 0)), pl.BlockSpec((1, D), lambda i: (0, 0))] + dep_specs,
        out_specs=pl.BlockSpec((tm, D), lambda i: (i, 0)),
        out_shape=SDS((T, D), BF16),
        compiler_params=_params(1),
    )(x, g0, *deps)


def _fwd_in(h, win_t):
    T = h.shape[0]
    tm, tn = min(T, 1024), 1408

    def body(h_ref, w_ref, p_ref):
        for j in range(IN_W // tn):
            cols = slice(j * tn, (j + 1) * tn)
            p_ref[:, cols] = _nt(h_ref[...], w_ref[cols, :]).astype(BF16)

    return pl.pallas_call(
        body, name="fwd_in", grid=(T // tm,),
        in_specs=[pl.BlockSpec((tm, D), lambda i: (i, 0)), _resident((IN_W, D))],
        out_specs=pl.BlockSpec((tm, IN_W), lambda i: (i, 0)),
        out_shape=SDS((T, IN_W), BF16),
        compiler_params=_params(1),
    )(h, win_t)


def _sgu_forward_parts(u_ref, vs_ref, lng_ref, lnb_ref):
    u = u_ref[...].astype(F32)
    vs = vs_ref[...].astype(F32)
    gu, tu = _gelu(u)
    gv, tv = _gelu(vs)
    mu = jnp.mean(gv, axis=-1, keepdims=True)
    dv = gv - mu
    rstd = lax.rsqrt(jnp.mean(dv * dv, axis=-1, keepdims=True) + EPS)
    vhat = dv * rstd
    vn = (vhat * lng_ref[...] + lnb_ref[...]).astype(BF16)
    return gu, tu, tv, mu, rstd, vn


def _sgu_forward_replay(u_ref, vs_ref, t_ref, stat_ref, lng_ref, lnb_ref):
    u = u_ref[...].astype(F32)
    vs = vs_ref[...].astype(F32)
    tu = t_ref[:, :D].astype(F32)
    tv = t_ref[:, D:].astype(F32)
    gu = 0.5 * u * (1.0 + tu)
    rstd = stat_ref[:, 1:2]
    vhat = (0.5 * vs * (1.0 + tv) - stat_ref[:, 0:1]) * rstd
    vn = (vhat * lng_ref[...] + lnb_ref[...]).astype(BF16)
    return u, vs, gu, tu, tv, rstd, vhat, vn


def _masked_ws(ws_ref, g):
    row = lax.broadcasted_iota(jnp.int32, (CHUNK, CHUNK), 0)
    col = lax.broadcasted_iota(jnp.int32, (CHUNK, CHUNK), 1)
    return jnp.where(row >= col, ws_ref[g], 0.0).astype(BF16)


def _fwd_sgu(proj, lng, lnb, ws, bst, after=None):
    T = proj.shape[0]
    tc = min(T, 512)

    def body(u_ref, vs_ref, lng_ref, lnb_ref, ws_ref, bst_ref, a_ref, t_ref, stat_ref):
        gu, tu, tv, mu, rstd, vn = _sgu_forward_parts(u_ref, vs_ref, lng_ref, lnb_ref)
        t_ref[:, :D] = tu.astype(BF16)
        t_ref[:, D:] = tv.astype(BF16)
        lane = lax.broadcasted_iota(jnp.int32, (tc, 128), 1)
        stat_ref[...] = jnp.where(lane == 0, mu, jnp.where(lane == 1, rstd, 0.0))
        for g in range(GROUPS):
            wm = _masked_ws(ws_ref, g)
            cols = slice(g * CHUNK, (g + 1) * CHUNK)
            for c in range(tc // CHUNK):
                rows = slice(c * CHUNK, (c + 1) * CHUNK)
                mixed = _nn(wm, vn[rows, cols]) + bst_ref[:, g:g + 1]
                a_ref[rows, cols] = (gu[rows, cols] * mixed).astype(BF16)

    body, dep_specs, deps = _after(body, 6, after)
    return pl.pallas_call(
        body, name="fwd_sgu", grid=(T // tc,),
        in_specs=[pl.BlockSpec((tc, D), lambda i: (i, 0)), pl.BlockSpec((tc, D), lambda i: (i, 1)),
                  pl.BlockSpec((1, D), lambda i: (0, 0)), pl.BlockSpec((1, D), lambda i: (0, 0)),
                  pl.BlockSpec((GROUPS, CHUNK, CHUNK), lambda i: (0, 0, 0)),
                  pl.BlockSpec((CHUNK, GROUPS), lambda i: (0, 0))] + dep_specs,
        out_specs=[pl.BlockSpec((tc, D), lambda i: (i, 0)), pl.BlockSpec((tc, 2 * D), lambda i: (i, 0)),
                   pl.BlockSpec((tc, 128), lambda i: (i, 0))],
        out_shape=[SDS((T, D), BF16), SDS((T, 2 * D), BF16), SDS((T, 128), F32)],
        compiler_params=_params(1),
    )(proj, proj, lng, lnb, ws, bst, *deps)


def _rope_tables(posf, invf, sgn, after=None):
    T = posf.shape[0]
    tr = min(T, 1024)

    def body(pos_ref, invf_ref, sgn_ref, c_ref, s_ref):
        ang = pos_ref[...] * invf_ref[...]
        c_ref[...] = jnp.cos(ang)
        s = jnp.sin(ang)
        s_ref[:, :128] = jnp.where(sgn_ref[...] < 0.0, -s, 0.0)
        s_ref[:, 128:] = jnp.where(sgn_ref[...] > 0.0, s, 0.0)

    body, dep_specs, deps = _after(body, 3, after)
    return pl.pallas_call(
        body, name="rope_tables", grid=(T // tr,),
        in_specs=[pl.BlockSpec((tr, 1), lambda i: (i, 0)), pl.BlockSpec((1, 128), lambda i: (0, 0)),
                  pl.BlockSpec((1, 128), lambda i: (0, 0))] + dep_specs,
        out_specs=[pl.BlockSpec((tr, 128), lambda i: (i, 0)), pl.BlockSpec((tr, 256), lambda i: (i, 0))],
        out_shape=[SDS((T, 128), F32), SDS((T, 256), F32)],
        compiler_params=_params(1),
    )(posf, invf, sgn, *deps)


def _rope(v, c, s):
    v = v.astype(F32)
    return v * c + pltpu.roll(v, 128 - ROPE // 2, 1) * s[:, :128] + pltpu.roll(v, ROPE // 2, 1) * s[:, 128:]


def _rope_bwd(dv, c, s):
    return dv * c + pltpu.roll(dv * s[:, :128], ROPE // 2, 1) + pltpu.roll(dv * s[:, 128:], 128 - ROPE // 2, 1)


def _fold_masks(first):
    jj = lax.broadcasted_iota(jnp.int32, (CHUNK, CHUNK), 0)
    t = lax.broadcasted_iota(jnp.int32, (CHUNK, CHUNK), 1)
    prev = jj > t
    return prev, jnp.where(prev & first, -1e30, 0.0)


def _fold(band, prev):
    return jnp.where(prev, band[:CHUNK], band[CHUNK:])


def _unfold(folded, prev):
    return jnp.concatenate([jnp.where(prev, folded, 0.0), jnp.where(prev, 0.0, folded)], axis=0)


def _softmax_sink(s, sink, key_axis):
    m = jnp.maximum(jnp.max(s, axis=key_axis, keepdims=True), sink)
    p = jnp.exp(s - m)
    esink = jnp.exp(sink - m)
    inv = 1.0 / (jnp.sum(p, axis=key_axis, keepdims=True) + esink)
    return p * inv, esink * inv


def _head_pair_operand(slab, g):
    lo = lax.broadcasted_iota(jnp.int32, slab.shape, 1) < HEAD
    if g % 2 == 0:
        first = jnp.where(lo, slab, 0.0)
        second = pltpu.roll(first, HEAD, 1)
    else:
        second = jnp.where(lo, 0.0, slab)
        first = pltpu.roll(second, HEAD, 1)
    return jnp.concatenate([first, second], axis=0).astype(BF16)


def _head_pair_gradient(acc, g):
    top, bot = acc[:2 * CHUNK], acc[2 * CHUNK:]
    lo = lax.broadcasted_iota(jnp.int32, top.shape, 1) < HEAD
    if g % 2 == 0:
        return jnp.where(lo, top, 0.0) + pltpu.roll(jnp.where(lo, 0.0, bot), HEAD, 1)
    return pltpu.roll(jnp.where(lo, top, 0.0), HEAD, 1) + jnp.where(lo, 0.0, bot)


PAIRS_PER_KV = N_Q // N_KV // 2
KV_W = N_KV * HEAD


def _band(prev_ref, cur_ref, cols=slice(None)):
    return jnp.concatenate([prev_ref[:, cols], cur_ref[:, cols]], axis=0)


def _fwd_attn(proj, cos, sin, sinks):
    T = proj.shape[0]
    nb = T // CHUNK
    cur = lambda i: i
    prev = lambda i: jnp.maximum(i - 1, 0)

    def body(q_ref, kp_ref, kc_ref, vp_ref, vc_ref, cp_ref, cc_ref, sp_ref, sc_ref, sink_ref,
             o_ref, qr_ref, kr_ref, p_ref, psink_ref):
        prev_slot, bias = _fold_masks(pl.program_id(0) == 0)
        c_band, s_band = _band(cp_ref, cc_ref), _band(sp_ref, sc_ref)
        for j in range(KV_W // 128):
            cols = slice(j * 128, (j + 1) * 128)
            k_slab = _rope(_band(kp_ref, kc_ref, cols), c_band, s_band)
            kr_ref[:, cols] = k_slab[CHUNK:].astype(BF16)
            v_slab = _band(vp_ref, vc_ref, cols).astype(F32)
            for g in (2 * j, 2 * j + 1):
                k2 = _head_pair_operand(k_slab, g)
                v2 = _head_pair_operand(v_slab, g)
                pairs = [g * PAIRS_PER_KV + r for r in range(PAIRS_PER_KV)]
                qps = []
                for pair in pairs:
                    lanes = slice(pair * 128, (pair + 1) * 128)
                    qps.append((_rope(q_ref[:, lanes], cc_ref[...], sc_ref[...]) * (HEAD ** -0.5)).astype(BF16))
                    qr_ref[:, lanes] = qps[-1]
                s2 = _nt(k2, jnp.concatenate(qps, axis=0))
                pcols = []
                for r, pair in enumerate(pairs):
                    ps = []
                    for e in range(2):
                        head = 2 * pair + e
                        s = _fold(s2[e * 2 * CHUNK:(e + 1) * 2 * CHUNK, r * 128:(r + 1) * 128], prev_slot) + bias
                        p, psink = _softmax_sink(s, sink_ref[head], 0)
                        p = p.astype(BF16)
                        p_ref[head] = p
                        psink_ref[head:head + 1, :] = psink
                        ps.append(_unfold(p, prev_slot))
                    pcols.append(jnp.concatenate(ps, axis=0))
                o = _tn(jnp.concatenate(pcols, axis=1), v2).astype(BF16)
                for r, pair in enumerate(pairs):
                    o_ref[:, pair * 128:(pair + 1) * 128] = o[r * CHUNK:(r + 1) * CHUNK]

    table = lambda which, width: pl.BlockSpec((CHUNK, width), lambda i: (which(i), 0))
    return pl.pallas_call(
        body, name="fwd_attn", grid=(nb,),
        in_specs=[pl.BlockSpec((CHUNK, D), lambda i: (i, OFF_Q // D)),
                  pl.BlockSpec((CHUNK, KV_W), lambda i: (prev(i), OFF_K // KV_W)),
                  pl.BlockSpec((CHUNK, KV_W), lambda i: (i, OFF_K // KV_W)),
                  pl.BlockSpec((CHUNK, KV_W), lambda i: (prev(i), OFF_VA // KV_W)),
                  pl.BlockSpec((CHUNK, KV_W), lambda i: (i, OFF_VA // KV_W)),
                  table(prev, 128), table(cur, 128), table(prev, 256), table(cur, 256),
                  pl.BlockSpec(memory_space=pltpu.SMEM)],
        out_specs=[pl.BlockSpec((CHUNK, D), lambda i: (i, 0)), pl.BlockSpec((CHUNK, D), lambda i: (i, 0)),
                   pl.BlockSpec((CHUNK, KV_W), lambda i: (i, 0)),
                   pl.BlockSpec((None, N_Q, CHUNK, CHUNK), lambda i: (i, 0, 0, 0)),
                   pl.BlockSpec((None, N_Q, CHUNK), lambda i: (i, 0, 0))],
        out_shape=[SDS((T, D), BF16), SDS((T, D), BF16), SDS((T, KV_W), BF16),
                   SDS((nb, N_Q, CHUNK, CHUNK), BF16), SDS((nb, N_Q, CHUNK), F32)],
        compiler_params=_params(1),
    )(proj, proj, proj, proj, proj, cos, cos, sin, sin, sinks)


def _fwd_mix(a, att, proj, x, wa, wb, wo, g1, g2):
    T = x.shape[0]
    tm = min(T, 512)
    half = D // 2

    def body(a_ref, att_ref, ga0, ga1, gb0, gb1, x_ref, wa_ref, wb_ref, wo_ref, g1_ref, g2_ref,
             mg_ref, a2_ref, b2_ref, mix_ref, x1_ref, hf_ref):
        a2 = _nn(a_ref[...], wa_ref[...])
        b2 = _nn(att_ref[...], wb_ref[...])
        ga = jnp.concatenate([ga0[...], ga1[...]], axis=1).astype(F32)
        gb = jnp.concatenate([gb0[...], gb1[...]], axis=1).astype(F32)
        merged = (_sigmoid(ga) * a2 + _sigmoid(gb) * b2).astype(BF16)
        a2_ref[...] = a2.astype(BF16)
        b2_ref[...] = b2.astype(BF16)
        mg_ref[...] = merged
        mix = _nn(merged, wo_ref[...])
        mix_ref[...] = mix
        _, mh = _rms_stats(mix)
        x1 = x_ref[...] + mh * g1_ref[...]
        x1_ref[...] = x1
        _, xh = _rms_stats(x1)
        hf_ref[...] = (xh * g2_ref[...]).astype(BF16)

    row = lambda i: (i, 0)
    const = lambda i: (0, 0)
    gspec = lambda off: pl.BlockSpec((tm, half), lambda i: (i, off // half))
    return pl.pallas_call(
        body, name="fwd_mix", grid=(T // tm,),
        in_specs=[pl.BlockSpec((tm, D), row), pl.BlockSpec((tm, D), row),
                  gspec(OFF_GA), gspec(OFF_GA + half), gspec(OFF_GB), gspec(OFF_GB + half),
                  pl.BlockSpec((tm, D), row), _resident((D, D)), _resident((D, D)),
                  _resident((D, D)), pl.BlockSpec((1, D), const), pl.BlockSpec((1, D), const)],
        out_specs=[pl.BlockSpec((tm, D), row)] * 6,
        out_shape=[SDS((T, D), BF16), SDS((T, D), BF16), SDS((T, D), BF16), SDS((T, D), F32), SDS((T, D), F32),
                   SDS((T, D), BF16)],
        compiler_params=_params(1),
    )(a, att, proj, proj, proj, proj, x, wa, wb, wo, g1, g2)


FF_SPLIT = N_DEV
FF_TILE = D_FF // FF_SPLIT


def _fwd_ff(hf, wfi3, wfo, x1, tgt, g3):
    T = hf.shape[0]
    tm = min(T, 512)

    def body(hf_ref, wfi_ref, wfo_ref, x1_ref, tgt_ref, g3_ref, f_ref, dy_ref, dff_ref, dg3_ref, loss_ref, r_s):
        @pl.when(pl.program_id(0) == 0)
        def _():
            dg3_ref[...] = jnp.zeros_like(dg3_ref)
            loss_ref[...] = jnp.zeros_like(loss_ref)

        hf_t = hf_ref[...]
        for s in range(FF_SPLIT):
            cols = slice(s * FF_TILE, (s + 1) * FF_TILE)
            f = _nn(hf_t, wfi_ref[s]).astype(BF16)
            f_ref[:, cols] = f
            rl = jnp.maximum(f.astype(F32), 0.0)
            r_s[:, cols] = (rl * rl).astype(BF16)
        r3, fh = _rms_stats(_nn(r_s[...], wfo_ref[...]))
        e = x1_ref[...] + fh * g3_ref[...] - tgt_ref[...]
        loss_ref[...] += jnp.sum(e * e) * (0.5 / D)
        dy = e * (1.0 / D)
        dy_ref[...] = dy
        dg3_ref[...] += _colsum(dy * fh)
        dff_ref[...] = _rms_bwd(dy, fh, r3, g3_ref[...]).astype(BF16)

    row = lambda i: (i, 0)
    const = lambda i: (0, 0)
    return pl.pallas_call(
        body, name="fwd_ff", grid=(T // tm,),
        in_specs=[pl.BlockSpec((tm, D), row), _resident((FF_SPLIT, D, FF_TILE)), _resident((D_FF, D)),
                  pl.BlockSpec((tm, D), row),
                  pl.BlockSpec((tm, D), row), pl.BlockSpec((1, D), const)],
        out_specs=[pl.BlockSpec((tm, D_FF), row), pl.BlockSpec((tm, D), row),
                   pl.BlockSpec((tm, D), row), pl.BlockSpec((1, D), const), pl.BlockSpec((1, 128), const)],
        out_shape=[SDS((T, D_FF), BF16), SDS((T, D), F32), SDS((T, D), BF16), SDS((1, D), F32), SDS((1, 128), F32)],
        scratch_shapes=[pltpu.VMEM((tm, D_FF), BF16)],
        compiler_params=_params(1),
    )(hf, wfi3, wfo, x1, tgt, g3)


def _bwd_ff(dff, f, wfi3, wfo, x1, dy, mix, g1, g2):
    T = dff.shape[0]
    tm = min(T, 512)

    def body(dff_ref, f_ref, wfi_ref, wfo_ref, x1_ref, dy_ref, mix_ref, g1_ref, g2_ref,
             df_ref, dx1_ref, dmix_ref, dg2_ref, dg1_ref):
        @pl.when(pl.program_id(0) == 0)
        def _():
            dg2_ref[...] = jnp.zeros_like(dg2_ref)
            dg1_ref[...] = jnp.zeros_like(dg1_ref)

        dff_t = dff_ref[...]
        dhf = None
        for s in range(FF_SPLIT):
            cols = slice(s * FF_TILE, (s + 1) * FF_TILE)
            dr = _nt(dff_t, wfo_ref[cols, :])
            df = (dr * (2.0 * jnp.maximum(f_ref[:, cols].astype(F32), 0.0))).astype(BF16)
            df_ref[:, cols] = df
            part = _nt(df, wfi_ref[s])
            dhf = part if dhf is None else dhf + part
        r2, xh = _rms_stats(x1_ref[...])
        dg2_ref[...] += _colsum(dhf * xh)
        dx1 = dy_ref[...] + _rms_bwd(dhf, xh, r2, g2_ref[...])
        dx1_ref[...] = dx1
        r1, mh = _rms_stats(mix_ref[...])
        dg1_ref[...] += _colsum(dx1 * mh)
        dmix_ref[...] = _rms_bwd(dx1, mh, r1, g1_ref[...]).astype(BF16)

    row = lambda i: (i, 0)
    const = lambda i: (0, 0)
    return pl.pallas_call(
        body, name="bwd_ff", grid=(T // tm,),
        in_specs=[pl.BlockSpec((tm, D), row), pl.BlockSpec((tm, D_FF), row),
                  _resident((FF_SPLIT, D, FF_TILE)), _resident((D_FF, D)),
                  pl.BlockSpec((tm, D), row), pl.BlockSpec((tm, D), row), pl.BlockSpec((tm, D), row),
                  pl.BlockSpec((1, D), const), pl.BlockSpec((1, D), const)],
        out_specs=[pl.BlockSpec((tm, D_FF), row), pl.BlockSpec((tm, D), row),
                   pl.BlockSpec((tm, D), row), pl.BlockSpec((1, D), const), pl.BlockSpec((1, D), const)],
        out_shape=[SDS((T, D_FF), BF16), SDS((T, D), F32), SDS((T, D), BF16), SDS((1, D), F32), SDS((1, D), F32)],
        compiler_params=_params(1),
    )(dff, f, wfi3, wfo, x1, dy, mix, g1, g2)


def _wgrad_ff(hf, df, f, dff):
    T = hf.shape[0]
    tt = min(T, 2048)
    slabs = 2
    wide = slabs * FF_TILE

    def body(hf_ref, df_ref, f_ref, dff_ref, dwfi_ref, dwfo_ref, acc_i, acc_o):
        t = pl.program_id(1)

        @pl.when(t == 0)
        def _():
            acc_i[...] = jnp.zeros_like(acc_i)
            acc_o[...] = jnp.zeros_like(acc_o)

        acc_i[...] += _tn(hf_ref[...], df_ref[...])
        rl = jnp.maximum(f_ref[...].astype(F32), 0.0)
        acc_o[...] += _tn((rl * rl).astype(BF16), dff_ref[...])

        @pl.when(t == T // tt - 1)
        def _():
            for s in range(slabs):
                dwfi_ref[s] = acc_i[:, s * FF_TILE:(s + 1) * FF_TILE].astype(BF16)
            dwfo_ref[...] = acc_o[...].astype(BF16)

    return pl.pallas_call(
        body, name="wgrad_ff", grid=(D_FF // wide, T // tt),
        in_specs=[pl.BlockSpec((tt, D), lambda p, t: (t, 0)), pl.BlockSpec((tt, wide), lambda p, t: (t, p)),
                  pl.BlockSpec((tt, wide), lambda p, t: (t, p)), pl.BlockSpec((tt, D), lambda p, t: (t, 0))],
        out_specs=[pl.BlockSpec((slabs, D, FF_TILE), lambda p, t: (p, 0, 0)), pl.BlockSpec((wide, D), lambda p, t: (p, 0))],
        out_shape=[SDS((FF_SPLIT, D, FF_TILE), BF16), SDS((D_FF, D), BF16)],
        scratch_shapes=[pltpu.VMEM((D, wide), F32), pltpu.VMEM((wide, D), F32)],
        compiler_params=_params(2),
    )(hf, df, f, dff)


def _bwd_mix(dmix, proj, a2, b2, merged, a, att, wo, wa, wb, after=None):
    T = dmix.shape[0]
    tm = min(T, 512)
    half = D // 2
    last = T // tm - 1

    def body(dmix_ref, ga0, ga1, gb0, gb1, a2_ref, b2_ref, mg_ref, a_ref, att_ref, wo_ref, wa_ref, wb_ref,
             dg_ref, da_ref, datt_ref, dwo_ref, dwa_ref, dwb_ref, acc, stage, sem):
        t = pl.program_id(0)

        @pl.when(t == 0)
        def _():
            acc[...] = jnp.zeros_like(acc)

        dmix_t = dmix_ref[...]
        dmg = _nt(dmix_t, wo_ref[...])
        sa = _sigmoid(jnp.concatenate([ga0[...], ga1[...]], axis=1).astype(F32))
        sb = _sigmoid(jnp.concatenate([gb0[...], gb1[...]], axis=1).astype(F32))
        da2 = (dmg * sa).astype(BF16)
        db2 = (dmg * sb).astype(BF16)
        dg_ref[:, :D] = (dmg * a2_ref[...].astype(F32) * (sa * (1.0 - sa))).astype(BF16)
        dg_ref[:, D:] = (dmg * b2_ref[...].astype(F32) * (sb * (1.0 - sb))).astype(BF16)
        da_ref[...] = _nt(da2, wa_ref[...]).astype(BF16)
        datt_ref[...] = _nt(db2, wb_ref[...]).astype(BF16)
        acc[0] += _tn(mg_ref[...], dmix_t)
        acc[1] += _tn(a_ref[...], da2)
        acc[2] += _tn(att_ref[...], db2)

        @pl.when(t == last)
        def _():
            for k, dw_ref in enumerate((dwo_ref, dwa_ref, dwb_ref)):
                stage[...] = acc[k].astype(BF16)
                out = pltpu.make_async_copy(stage, dw_ref, sem)
                out.start()
                out.wait()

    row = lambda i: (i, 0)
    gspec = lambda off: pl.BlockSpec((tm, half), lambda i: (i, off // half))
    body, dep_specs, deps = _after(body, 13, after)
    return pl.pallas_call(
        body, name="bwd_mix", grid=(T // tm,),
        in_specs=[pl.BlockSpec((tm, D), row), gspec(OFF_GA), gspec(OFF_GA + half), gspec(OFF_GB), gspec(OFF_GB + half)]
        + [pl.BlockSpec((tm, D), row)] * 5 + [_resident((D, D))] * 3 + dep_specs,
        out_specs=[pl.BlockSpec((tm, 2 * D), row), pl.BlockSpec((tm, D), row), pl.BlockSpec((tm, D), row)] + [_ANY] * 3,
        out_shape=[SDS((T, 2 * D), BF16), SDS((T, D), BF16), SDS((T, D), BF16)] + [SDS((D, D), BF16)] * 3,
        scratch_shapes=[pltpu.VMEM((3, D, D), F32), pltpu.VMEM((D, D), BF16), pltpu.SemaphoreType.DMA],
        compiler_params=_params(1),
    )(dmix, proj, proj, proj, proj, a2, b2, merged, a, att, wo, wa, wb, *deps)


def _bwd_attn(qr, kr, probs, psink, proj, cos, sin, datt, after=None):
    T = proj.shape[0]
    nb = T // CHUNK
    cur = lambda i: jnp.minimum(i, nb - 1)
    prev = lambda i: jnp.maximum(jnp.minimum(i, nb - 1) - 1, 0)

    def body(q_ref, kp_ref, kc_ref, vp_ref, vc_ref, cp_ref, cc_ref, sp_ref, sc_ref, p_ref, psink_ref, do_ref,
             dq_ref, dkv_ref, dsink_ref, carry_k, carry_v):
        i = pl.program_id(0)

        @pl.when(i == 0)
        def _():
            carry_k[...] = jnp.zeros_like(carry_k)
            carry_v[...] = jnp.zeros_like(carry_v)
            dsink_ref[...] = jnp.zeros_like(dsink_ref)

        @pl.when(i < nb)
        def _():
            prev_slot, _ = _fold_masks(i == 0)
            c_band, s_band = _band(cp_ref, cc_ref), _band(sp_ref, sc_ref)
            lane = lax.broadcasted_iota(jnp.int32, (1, 128), 1)
            dsink = jnp.zeros((1, 128), F32)
            for j in range(KV_W // 128):
                cols = slice(j * 128, (j + 1) * 128)
                k_slab = _band(kp_ref, kc_ref, cols).astype(F32)
                v_slab = _band(vp_ref, vc_ref, cols).astype(F32)
                dk_slab = jnp.zeros((2 * CHUNK, 128), F32)
                dv_slab = jnp.zeros((2 * CHUNK, 128), F32)
                for g in (2 * j, 2 * j + 1):
                    k2 = _head_pair_operand(k_slab, g)
                    v2 = _head_pair_operand(v_slab, g)
                    pairs = [g * PAIRS_PER_KV + r for r in range(PAIRS_PER_KV)]
                    q_stack = jnp.concatenate([q_ref[:, pr * 128:(pr + 1) * 128] for pr in pairs], axis=0)
                    do_stack = jnp.concatenate([do_ref[:, pr * 128:(pr + 1) * 128] for pr in pairs], axis=0)
                    dp2 = _nt(v2, do_stack)
                    pcols, dscols = [], []
                    for r, pair in enumerate(pairs):
                        ps, dss = [], []
                        for e in range(2):
                            head = 2 * pair + e
                            p_b = p_ref[head]
                            p = p_b.astype(F32)
                            dp = _fold(dp2[e * 2 * CHUNK:(e + 1) * 2 * CHUNK, r * 128:(r + 1) * 128], prev_slot)
                            delta = jnp.sum(p * dp, axis=0, keepdims=True)
                            ps.append(_unfold(p_b, prev_slot))
                            dss.append(_unfold((p * (dp - delta)).astype(BF16), prev_slot))
                            dsink = dsink + jnp.where(lane == head, -jnp.sum(psink_ref[head:head + 1, :] * delta), 0.0)
                        pcols.append(jnp.concatenate(ps, axis=0))
                        dscols.append(jnp.concatenate(dss, axis=0))
                    ds2 = jnp.concatenate(dscols, axis=1)
                    dq = _tn(ds2, k2) * (HEAD ** -0.5)
                    for r, pair in enumerate(pairs):
                        dq_ref[:, pair * 128:(pair + 1) * 128] = _rope_bwd(
                            dq[r * CHUNK:(r + 1) * CHUNK], cc_ref[...], sc_ref[...]).astype(BF16)
                    dk_slab = dk_slab + _head_pair_gradient(_nn(ds2, q_stack), g)
                    dv_slab = dv_slab + _head_pair_gradient(_nn(jnp.concatenate(pcols, axis=1), do_stack), g)
                dk_slab = _rope_bwd(dk_slab, c_band, s_band)
                vcols = slice(KV_W + j * 128, KV_W + (j + 1) * 128)
                dkv_ref[:, cols] = (carry_k[:, cols] + dk_slab[:CHUNK]).astype(BF16)
                dkv_ref[:, vcols] = (carry_v[:, cols] + dv_slab[:CHUNK]).astype(BF16)
                carry_k[:, cols] = dk_slab[CHUNK:]
                carry_v[:, cols] = dv_slab[CHUNK:]
            dsink_ref[...] += dsink

        @pl.when(i == nb)
        def _():
            dkv_ref[:, :KV_W] = carry_k[...].astype(BF16)
            dkv_ref[:, KV_W:] = carry_v[...].astype(BF16)

    table = lambda which, width: pl.BlockSpec((CHUNK, width), lambda i: (which(i), 0))
    body, dep_specs, deps = _after(body, 12, after)
    return pl.pallas_call(
        body, name="bwd_attn", grid=(nb + 1,),
        in_specs=[pl.BlockSpec((CHUNK, D), lambda i: (cur(i), 0)),
                  pl.BlockSpec((CHUNK, KV_W), lambda i: (prev(i), 0)),
                  pl.BlockSpec((CHUNK, KV_W), lambda i: (cur(i), 0)),
                  pl.BlockSpec((CHUNK, KV_W), lambda i: (prev(i), OFF_VA // KV_W)),
                  pl.BlockSpec((CHUNK, KV_W), lambda i: (cur(i), OFF_VA // KV_W)),
                  table(prev, 128), table(cur, 128), table(prev, 256), table(cur, 256),
                  pl.BlockSpec((None, N_Q, CHUNK, CHUNK), lambda i: (cur(i), 0, 0, 0)),
                  pl.BlockSpec((None, N_Q, CHUNK), lambda i: (cur(i), 0, 0)),
                  pl.BlockSpec((CHUNK, D), lambda i: (cur(i), 0))] + dep_specs,
        out_specs=[pl.BlockSpec((CHUNK, D), lambda i: (cur(i), 0)),
                   pl.BlockSpec((CHUNK, 2 * KV_W), lambda i: (jnp.maximum(i - 1, 0), 0)),
                   pl.BlockSpec((1, 128), lambda i: (0, 0))],
        out_shape=[SDS((T, D), BF16), SDS((T, 2 * KV_W), BF16), SDS((1, 128), F32)],
        scratch_shapes=[pltpu.VMEM((CHUNK, KV_W), F32), pltpu.VMEM((CHUNK, KV_W), F32)],
        compiler_params=_params(1),
    )(qr, kr, kr, proj, proj, cos, cos, sin, sin, probs, psink, datt, *deps)


def _bwd_sgu(proj, tanhs, stats, da, lng, lnb, ws, bst):
    T = proj.shape[0]
    tc = min(T, 512)
    nsteps = T // tc

    def body(u_ref, vs_ref, t_ref, stat_ref, da_ref, lng_ref, lnb_ref, ws_ref, bst_ref,
             duv_ref, dws_ref, dbs_ref, dlng_ref, dlnb_ref, dvn_s, dgu_s, dmx_sum):
        i = pl.program_id(0)

        @pl.when(i == 0)
        def _():
            dws_ref[...] = jnp.zeros_like(dws_ref)
            dlng_ref[...] = jnp.zeros_like(dlng_ref)
            dlnb_ref[...] = jnp.zeros_like(dlnb_ref)
            dmx_sum[...] = jnp.zeros_like(dmx_sum)

        u, vs, gu, tu, tv, rstd, vhat, vn = _sgu_forward_replay(u_ref, vs_ref, t_ref, stat_ref, lng_ref, lnb_ref)
        da = da_ref[...].astype(F32)
        for g in range(GROUPS):
            wm = _masked_ws(ws_ref, g)
            cols = slice(g * CHUNK, (g + 1) * CHUNK)
            dws = jnp.zeros((CHUNK, CHUNK), F32)
            dsum = jnp.zeros((CHUNK, CHUNK), F32)
            for c in range(tc // CHUNK):
                rows = slice(c * CHUNK, (c + 1) * CHUNK)
                vn_cg = vn[rows, cols]
                mixed = _nn(wm, vn_cg) + bst_ref[:, g:g + 1]
                dgu_s[rows, cols] = da[rows, cols] * mixed
                dmx = da[rows, cols] * gu[rows, cols]
                dmxb = dmx.astype(BF16)
                dws = dws + _nt(dmxb, vn_cg)
                dsum = dsum + dmx
                dvn_s[rows, cols] = _tn(wm, dmxb)
            dws_ref[g] += dws
            dmx_sum[:, cols] += dsum
        dvn = dvn_s[...]
        dlng_ref[...] += _colsum(dvn * vhat)
        dlnb_ref[...] += _colsum(dvn)
        dvh = dvn * lng_ref[...]
        dgv = rstd * (dvh - jnp.mean(dvh, axis=-1, keepdims=True) - vhat * jnp.mean(dvh * vhat, axis=-1, keepdims=True))
        duv_ref[:, :D] = (dgu_s[...] * _gelu_grad(u, tu)).astype(BF16)
        duv_ref[:, D:] = (dgv * _gelu_grad(vs, tv)).astype(BF16)

        @pl.when(i == nsteps - 1)
        def _():
            row = lax.broadcasted_iota(jnp.int32, (CHUNK, CHUNK), 0)
            col = lax.broadcasted_iota(jnp.int32, (CHUNK, CHUNK), 1)
            for g in range(GROUPS):
                dws_ref[g] = jnp.where(row >= col, dws_ref[g], 0.0)
                dbs_ref[g:g + 1, :] = _colsum(dmx_sum[:, g * CHUNK:(g + 1) * CHUNK].T)

    const2 = lambda i: (0, 0)
    return pl.pallas_call(
        body, name="bwd_sgu", grid=(nsteps,),
        in_specs=[pl.BlockSpec((tc, D), lambda i: (i, 0)), pl.BlockSpec((tc, D), lambda i: (i, 1)),
                  pl.BlockSpec((tc, 2 * D), lambda i: (i, 0)), pl.BlockSpec((tc, 128), lambda i: (i, 0)),
                  pl.BlockSpec((tc, D), lambda i: (i, 0)), pl.BlockSpec((1, D), const2), pl.BlockSpec((1, D), const2),
                  pl.BlockSpec((GROUPS, CHUNK, CHUNK), lambda i: (0, 0, 0)), pl.BlockSpec((CHUNK, GROUPS), const2)],
        out_specs=[pl.BlockSpec((tc, 2 * D), lambda i: (i, 0)), pl.BlockSpec((GROUPS, CHUNK, CHUNK), lambda i: (0, 0, 0)),
                   pl.BlockSpec((GROUPS, CHUNK), const2), pl.BlockSpec((1, D), const2), pl.BlockSpec((1, D), const2)],
        out_shape=[SDS((T, 2 * D), BF16), SDS((GROUPS, CHUNK, CHUNK), F32), SDS((GROUPS, CHUNK), F32),
                   SDS((1, D), F32), SDS((1, D), F32)],
        scratch_shapes=[pltpu.VMEM((tc, D), F32), pltpu.VMEM((tc, D), F32), pltpu.VMEM((CHUNK, D), F32)],
        compiler_params=_params(1),
    )(proj, proj, tanhs, stats, da, lng, lnb, ws, bst)


IN_SEG_WIDTHS = (2 * D, D, 2 * N_KV * HEAD, 2 * D)


def _resident(shape):
    return pl.BlockSpec(shape, lambda *_: (0,) * len(shape), pipeline_mode=pl.Buffered(1))


def _bwd_in(duv, dq, dkv, dg, win_t, x, dx1, g0, after=None):
    T = x.shape[0]
    tm = min(T, 512)

    def body(duv_ref, dq_ref, dkv_ref, dg_ref, w_ref, x_ref, dx1_ref, g0_ref, gx_ref, dg0_ref):
        @pl.when(pl.program_id(0) == 0)
        def _():
            dg0_ref[...] = jnp.zeros_like(dg0_ref)

        dh, off = None, 0
        for ref, width in zip((duv_ref, dq_ref, dkv_ref, dg_ref), IN_SEG_WIDTHS):
            part = _nn(ref[...], w_ref[off:off + width, :])
            dh = part if dh is None else dh + part
            off += width
        r0, xh = _rms_stats(x_ref[...])
        dg0_ref[...] += _colsum(dh * xh)
        gx_ref[...] = dx1_ref[...] + _rms_bwd(dh, xh, r0, g0_ref[...])

    row = lambda i: (i, 0)
    body, dep_specs, deps = _after(body, 8, after)
    return pl.pallas_call(
        body, name="bwd_in", grid=(T // tm,),
        in_specs=[pl.BlockSpec((tm, w), row) for w in IN_SEG_WIDTHS] + [
            _resident((IN_W, D)), pl.BlockSpec((tm, D), row), pl.BlockSpec((tm, D), row),
            pl.BlockSpec((1, D), lambda i: (0, 0))] + dep_specs,
        out_specs=[pl.BlockSpec((tm, D), row), pl.BlockSpec((1, D), lambda i: (0, 0))],
        out_shape=[SDS((T, D), F32), SDS((1, D), F32)],
        compiler_params=_params(1),
    )(duv, dq, dkv, dg, win_t, x, dx1, g0, *deps)


def _wgrad_rows(h, segs, first_row, into, name):
    T = h.shape[0]
    tt = min(T, 2048)
    widths = [s.shape[1] for s in segs]
    rows = sum(widths)
    n_in = 1 + len(segs) + (into is not None)

    def body(*refs):
        h_ref, seg_refs = refs[0], refs[1:1 + len(segs)]
        dw_ref, acc, stage, sem = refs[n_in], refs[n_in + 1], refs[n_in + 2], refs[n_in + 3]
        t = pl.program_id(0)

        @pl.when(t == 0)
        def _():
            acc[...] = jnp.zeros_like(acc)

        off = 0
        for ref, width in zip(seg_refs, widths):
            acc[off:off + width, :] += _tn(ref[...], h_ref[...])
            off += width

        @pl.when(t == T // tt - 1)
        def _():
            stage[...] = acc[...].astype(BF16)
            out = pltpu.make_async_copy(stage, dw_ref.at[pl.ds(first_row, rows)], sem)
            out.start()
            out.wait()

    row = lambda t: (t, 0)
    return pl.pallas_call(
        body, name=name, grid=(T // tt,),
        in_specs=[pl.BlockSpec((tt, D), row)] + [pl.BlockSpec((tt, w), row) for w in widths] + [_ANY] * (into is not None),
        out_specs=_ANY,
        out_shape=SDS((IN_W, D), BF16),
        input_output_aliases={} if into is None else {n_in - 1: 0},
        scratch_shapes=[pltpu.VMEM((rows, D), F32), pltpu.VMEM((rows, D), BF16), pltpu.SemaphoreType.DMA],
        compiler_params=_params(1),
    )(h, *segs, *([] if into is None else [into]))


def _place():
    x, y, c = lax.axis_index("x"), lax.axis_index("y"), lax.axis_index("c")
    return x, y, c, 4 * x + 2 * y + c


def _peers(x, y, c):
    out = []
    for mask in range(1, N_DEV):
        px = 1 - x if mask & 4 else x
        py = 1 - y if mask & 2 else y
        pc = 1 - c if mask & 1 else c
        out.append(((px, py, pc), 4 * px + 2 * py + pc))
    return out


def _all_to_all(arrays, gather, name, after=None):
    n = len(arrays)

    def body(*refs):
        ins, outs = refs[:n], refs[n:2 * n]
        send_sems, recv_sems, local_sems = refs[2 * n:]
        x, y, c, me = _place()
        local, sends, recvs = [], [], []
        for a in range(n):
            src_own = ins[a] if gather[a] else ins[a].at[me]
            local.append(pltpu.make_async_copy(src_own, outs[a].at[me], local_sems.at[a]))
            for k, (peer, pid) in enumerate(_peers(x, y, c)):
                sem = a * (N_DEV - 1) + k
                src = ins[a] if gather[a] else ins[a].at[pid]
                sends.append(pltpu.make_async_remote_copy(
                    src_ref=src, dst_ref=outs[a].at[me], send_sem=send_sems.at[sem], recv_sem=recv_sems.at[sem],
                    device_id=peer, device_id_type=MESH))
                recvs.append(pltpu.make_async_remote_copy(
                    src_ref=src, dst_ref=outs[a].at[pid], send_sem=send_sems.at[sem], recv_sem=recv_sems.at[sem],
                    device_id=peer, device_id_type=MESH))
        for cp in local + sends:
            cp.start()
        for cp in recvs:
            cp.wait_recv()
        for cp in sends:
            cp.wait_send()
        for cp in local:
            cp.wait()

    out_shape = [SDS((N_DEV,) + a.shape if gt else a.shape, a.dtype) for a, gt in zip(arrays, gather)]
    nsem = n * (N_DEV - 1)
    body, dep_specs, deps = _after(body, n, after)
    return pl.pallas_call(
        body, name=name,
        in_specs=[pl.BlockSpec(memory_space=pl.ANY)] * n + dep_specs,
        out_specs=[pl.BlockSpec(memory_space=pl.ANY)] * n,
        out_shape=out_shape,
        scratch_shapes=[pltpu.SemaphoreType.DMA((nsem,)), pltpu.SemaphoreType.DMA((nsem,)), pltpu.SemaphoreType.DMA((n,))],
    )(*arrays, *deps)


_HBM = pl.BlockSpec(memory_space=pltpu.HBM)
_SEM = pl.BlockSpec(memory_space=pltpu.SEMAPHORE)
_EFFECT = pltpu.SideEffectType.DATAFLOW_SIDE_EFFECTING
GATHER = "gather"
SCATTER = "scatter"
SPREAD = "spread"


def _zone_shape(a, mode):
    if mode == GATHER:
        return (N_DEV,) + a.shape
    return (N_DEV - 1,) + (a.shape[1:] if mode == SCATTER else a.shape)


def _start_copies(arrays, modes, name, after=None):
    n = len(arrays)
    zones = [lax.empty(_zone_shape(a, m), a.dtype) for a, m in zip(arrays, modes)]

    def body(*refs):
        ins, lands = refs[:n], refs[n:2 * n]
        send_sems, recv_sems = refs[-2 * n - 3], refs[-2 * n - 2]
        token = refs[-1]
        x, y, c, me = _place()
        for a in range(n):
            for k, (peer, pid) in enumerate(_peers(x, y, c)):
                src = ins[a].at[pid] if modes[a] == SCATTER else ins[a]
                dst = lands[a].at[me] if modes[a] == GATHER else lands[a].at[k]
                pltpu.make_async_remote_copy(src_ref=src, dst_ref=dst, send_sem=send_sems.at[a], recv_sem=recv_sems.at[a],
                                             device_id=peer, device_id_type=MESH).start()
            if modes[a] == GATHER:
                pltpu.make_async_remote_copy(src_ref=ins[a], dst_ref=lands[a].at[me], send_sem=send_sems.at[a],
                                             recv_sem=recv_sems.at[a], device_id=(x, y, c), device_id_type=MESH).start()
        token[...] = jnp.zeros_like(token)

    hbm = lambda a: pltpu.HBM(a.shape, a.dtype)
    sems = pltpu.SemaphoreType.DMA((n,))
    extra = [] if after is None else [after]
    operands = [pltpu.with_memory_space_constraint(a, pltpu.HBM) for a in list(arrays) + zones]
    res = pl.pallas_call(
        body, name=name,
        out_shape=(sems, sems, *[hbm(a) for a in arrays], *[hbm(z) for z in zones], SDS((8, 128), F32)),
        in_specs=[_HBM] * (2 * n) + [_ANY] * len(extra),
        out_specs=(_SEM, _SEM, *[_HBM] * (2 * n), pl.BlockSpec(memory_space=pltpu.VMEM)),
        input_output_aliases={i: 2 + i for i in range(2 * n)},
        compiler_params=pltpu.CompilerParams(has_side_effects=_EFFECT),
    )(*operands, *extra)
    return res[0], res[1], list(res[2:2 + n]), list(res[2 + n:2 + 2 * n]), res[-1]


def _wait_copies(started, after, name, count=N_DEV - 1):
    send_sems, recv_sems, thru, zones, _ = started
    nt, nz = len(thru), len(zones)

    def body(*refs):
        lands = refs[nt:nt + nz]
        send_ref, recv_ref = refs[nt + nz], refs[nt + nz + 1]
        x, y, c, _ = _place()
        for a in range(nz):
            blocks = lands[a].at[pl.ds(0, count)]
            cp = pltpu.make_async_remote_copy(src_ref=blocks, dst_ref=blocks, send_sem=send_ref.at[a], recv_sem=recv_ref.at[a],
                                              device_id=(x, y, 1 - c), device_id_type=MESH)
            cp.wait_send()
            cp.wait_recv()

    hbm = lambda a: pltpu.HBM(a.shape, a.dtype)
    res = pl.pallas_call(
        body, name=name,
        out_shape=tuple(hbm(a) for a in thru + zones),
        in_specs=[_HBM] * (nt + nz) + [_SEM, _SEM, _ANY],
        out_specs=tuple([_HBM] * (nt + nz)),
        input_output_aliases={i: i for i in range(nt + nz)},
        compiler_params=pltpu.CompilerParams(has_side_effects=_EFFECT),
    )(*thru, *zones, send_sems, recv_sems, after)
    return list(res[:nt]), list(res[nt:])


def _split_start(body, arrays, zones, name, after):
    n = len(arrays) + len(zones)
    hbm = lambda a: pltpu.HBM(a.shape, a.dtype)
    sems = pltpu.SemaphoreType.DMA((max(len(zones), 1),))
    extra = [] if after is None else [after]
    operands = [pltpu.with_memory_space_constraint(a, pltpu.HBM) for a in list(arrays) + list(zones)]
    res = pl.pallas_call(
        body, name=name,
        out_shape=(sems, sems, *[hbm(a) for a in operands], SDS((8, 128), F32)),
        in_specs=[_HBM] * n + [_ANY] * len(extra),
        out_specs=(_SEM, _SEM, *[_HBM] * n, pl.BlockSpec(memory_space=pltpu.VMEM)),
        input_output_aliases={i: 2 + i for i in range(n)},
        compiler_params=pltpu.CompilerParams(has_side_effects=_EFFECT),
    )(*operands, *extra)
    return res[0], res[1], list(res[2:2 + len(arrays)]), list(res[2 + len(arrays):2 + n]), res[-1]


def _gather_first_leg(shard, name, after=None):
    zone = lax.empty((N_DEV,) + shard.shape, shard.dtype)
    extra = 0 if after is None else 1

    def body(*refs):
        src, land = refs[0], refs[1]
        send_sem, recv_sem, token = refs[2 + extra], refs[3 + extra], refs[-1]
        x, y, c, me = _place()
        for peer in ((x, y, c), (x, y, 1 - c), (1 - x, y, c), (x, 1 - y, c), (1 - x, 1 - y, c)):
            pltpu.make_async_remote_copy(src_ref=src, dst_ref=land.at[me], send_sem=send_sem.at[0], recv_sem=recv_sem.at[0],
                                         device_id=peer, device_id_type=MESH).start()
        token[...] = jnp.zeros_like(token)

    return _split_start(body, [shard], [zone], name, after)


def _gather_second_leg(zone, name, after=None):
    extra = 0 if after is None else 1

    def body(*refs):
        land = refs[0]
        send_sem, recv_sem, token = refs[1 + extra], refs[2 + extra], refs[-1]
        x, y, c, _ = _place()
        for px, py in ((1 - x, y), (x, 1 - y), (1 - x, 1 - y)):
            slot = 4 * px + 2 * py + c
            pltpu.make_async_remote_copy(src_ref=land.at[slot], dst_ref=land.at[slot], send_sem=send_sem.at[0],
                                         recv_sem=recv_sem.at[0], device_id=(x, y, 1 - c), device_id_type=MESH).start()
        token[...] = jnp.zeros_like(token)

    return _split_start(body, [], [zone], name, after)


UPDATE_BLOCK_ELEMS = 512 * 1024


def _update_rows(R, C):
    fits = [t for t in range(8, R + 1, 8) if R % t == 0 and t * C <= UPDATE_BLOCK_ELEMS]
    whole = [t for t in fits if t % 16 == 0]
    return max(whole or fits)


def _adamw_math(g, w, m, v):
    m2 = ADAM_B1 * m + (1.0 - ADAM_B1) * g
    v2 = ADAM_B2 * v + (1.0 - ADAM_B2) * (g * g)
    m_hat = m2 / (1.0 - ADAM_B1 ** ADAM_STEP)
    v_hat = v2 / (1.0 - ADAM_B2 ** ADAM_STEP)
    delta = -ADAM_LR * (m_hat / (jnp.sqrt(v_hat) + ADAM_EPS) + ADAM_WD * w)
    return delta, m2, v2


def _sum_adamw(parts, w, m, v, name):
    R, C = w.shape
    tr = _update_rows(R, C)

    def body(p_ref, w_ref, m_ref, v_ref, g_ref, d_ref, m2_ref, v2_ref):
        g = p_ref[0]
        for k in range(1, N_DEV):
            g = g + p_ref[k]
        g_ref[...] = g
        d_ref[...], m2_ref[...], v2_ref[...] = _adamw_math(g, w_ref[...], m_ref[...], v_ref[...])

    blk = pl.BlockSpec((tr, C), lambda i: (i, 0))
    return pl.pallas_call(
        body, name=name, grid=(R // tr,),
        in_specs=[pl.BlockSpec((N_DEV, tr, C), lambda i: (0, i, 0)), blk, blk, blk],
        out_specs=[blk] * 4,
        out_shape=[SDS((R, C), F32)] * 4,
        compiler_params=_params(1),
    )(parts, w, m, v)


def _sum_adamw_peers(me, own, parts, w, m, v, name, replicated, also_rows=None):
    R, C = w.shape
    tr = _update_rows(R, C)
    assert also_rows is None or tr == R

    def body(me_ref, own_ref, p_ref, w_ref, m_ref, v_ref, g_ref, d_ref, m2_ref, v2_ref, *extra):
        if replicated:
            mine = me_ref[0]
            g = None
            for j in range(N_DEV):
                k = jnp.maximum(jnp.bitwise_xor(mine, j) - 1, 0)
                term = jnp.where(mine == j, own_ref[...], p_ref[k])
                g = term if g is None else g + term
        else:
            g = own_ref[...].astype(F32)
            for k in range(N_DEV - 1):
                g = g + p_ref[k].astype(F32)
        results = (g,) + _adamw_math(g, w_ref[...], m_ref[...], v_ref[...])
        for ref, val in zip((g_ref, d_ref, m2_ref, v2_ref), results):
            ref[...] = val
        for ref, val in zip(extra, results):
            ref[...] = val[also_rows[0]:also_rows[1]]

    blk = pl.BlockSpec((tr, C), lambda i, me_ref: (i, 0))
    own_spec = blk if replicated else pl.BlockSpec((None, tr, C), lambda i, me_ref: (me_ref[0], i, 0))
    n_also = 0 if also_rows is None else also_rows[1] - also_rows[0]
    also_specs = [pl.BlockSpec((n_also, C), lambda i, me_ref: (0, 0))] * (4 if also_rows else 0)
    return pl.pallas_call(
        body, name=name,
        grid_spec=pltpu.PrefetchScalarGridSpec(
            num_scalar_prefetch=1, grid=(R // tr,),
            in_specs=[own_spec, pl.BlockSpec((N_DEV - 1, tr, C), lambda i, me_ref: (0, i, 0)), blk, blk, blk],
            out_specs=[blk] * 4 + also_specs),
        out_shape=[SDS((R, C), F32)] * 4 + [SDS((n_also, C), F32)] * len(also_specs),
        compiler_params=_params(1),
    )(me, own, parts, w, m, v)


SMALL = ("ln_v_gain", "ln_v_bias", "w_spatial", "b_spatial", "sinks", "norm_mix_post", "norm_ff_pre", "norm_ff_post")
SMALL_ROWS = {"ln_v_gain": 8, "ln_v_bias": 8, "w_spatial": 1024, "b_spatial": 8, "sinks": 8,
              "norm_mix_post": 8, "norm_ff_pre": 8, "norm_ff_post": 8}
SMALL_PACK_ROWS = 1152


def _pack_small(vals):
    rows = []
    for name in SMALL:
        flat = vals[name].reshape(-1)
        pad = SMALL_ROWS[name] * 128 - flat.shape[0]
        if pad:
            flat = jnp.concatenate([flat, jnp.zeros((pad,), F32)])
        rows.append(flat.reshape(SMALL_ROWS[name], 128))
    rows.append(jnp.zeros((SMALL_PACK_ROWS - sum(SMALL_ROWS.values()), 128), F32))
    return jnp.concatenate(rows, axis=0)


def _unpack_small(packed, shapes):
    out, r = {}, 0
    for name in SMALL:
        n = 1
        for s in shapes[name]:
            n *= s
        out[name] = packed[r:r + SMALL_ROWS[name]].reshape(-1)[:n].reshape(shapes[name])
        r += SMALL_ROWS[name]
    return out


def _rope_rows():
    d = jnp.arange(128) % HEAD
    inv = ROPE_THETA ** (-(2.0 * (d % (ROPE // 2))).astype(F32) / ROPE)
    invf = jnp.where(d < ROPE, inv, 0.0).astype(F32).reshape(1, 128)
    sgn = jnp.where(d < ROPE // 2, -1.0, jnp.where(d < ROPE, 1.0, 0.0)).astype(F32).reshape(1, 128)
    return invf, sgn


def kernel(x, positions, w_in, ln_v_gain, ln_v_bias, w_spatial, b_spatial, sinks, w_a, w_b, w_o, norm_mix_pre, norm_mix_post, w_ff_in, w_ff_out, norm_ff_pre, norm_ff_post, loss_target, m_w_in, m_ln_v_gain, m_ln_v_bias, m_w_spatial, m_b_spatial, m_sinks, m_w_a, m_w_b, m_w_o, m_norm_mix_pre, m_norm_mix_post, m_w_ff_in, m_w_ff_out, m_norm_ff_pre, m_norm_ff_post, v_w_in, v_ln_v_gain, v_ln_v_bias, v_w_spatial, v_b_spatial, v_sinks, v_w_a, v_w_b, v_w_o, v_norm_mix_pre, v_norm_mix_post, v_w_ff_in, v_w_ff_out, v_norm_ff_pre, v_norm_ff_post):
    given = dict(locals())
    T = x.shape[1]
    xt = x[0]
    tgt = loss_target[0]
    bst = b_spatial[0].T
    ws = w_spatial[0]

    me = 4 * lax.axis_index("x") + 2 * lax.axis_index("y") + lax.axis_index("c")
    me_arr = me.astype(jnp.int32).reshape(1)

    rest = ("w_a", "w_b", "w_o", "w_ff_in", "w_ff_out")
    shard = {n: given[n][0].astype(BF16) for n in rest}
    g_one = _gather_first_leg(w_in[0].T.astype(BF16), "gather_in_start")
    cos, sin = _rope_tables(positions.astype(F32).reshape(T, 1), *_rope_rows(), after=g_one[-1])
    small_state = [_pack_small({n: given[k + n] for n in SMALL}) for k in ("", "m_", "v_")]
    h = _rms_pre(xt, norm_mix_pre, after=[cos, *small_state, *[shard[n] for n in rest]])
    _, (win8,) = _wait_copies(g_one, h, "gather_in_wait", count=5)
    g_two = _gather_second_leg(win8, "gather_in_pass_start")
    g_rest = _start_copies([shard[n] for n in rest], [GATHER] * len(rest), "gather_rest_start", after=g_two[-1])
    _, (win8,) = _wait_copies(g_two, g_rest[-1], "gather_in_pass_wait", count=3)
    win = win8.reshape(IN_W, D)

    proj = _fwd_in(h, win)
    att, qr, kr, probs, psink = _fwd_attn(proj, cos, sin, sinks[0])
    a, tanhs, ln_stats = _fwd_sgu(proj, ln_v_gain, ln_v_bias, ws, bst, after=att)
    gw = dict(zip(rest, _wait_copies(g_rest, a, "gather_rest_wait", count=N_DEV)[1]))
    wa, wb, wo = (gw[n].reshape(D, D) for n in ("w_a", "w_b", "w_o"))
    wfi3 = gw["w_ff_in"]
    wfo = gw["w_ff_out"].reshape(D_FF, D)
    merged, a2, b2, mix, x1, hf = _fwd_mix(a, att, proj, xt, wa, wb, wo, norm_mix_post, norm_ff_pre)
    f, dy, dff, dg3, loss_part = _fwd_ff(hf, wfi3, wfo, x1, tgt, norm_ff_post)

    df, dx1, dmix, dg2, dg1 = _bwd_ff(dff, f, wfi3, wfo, x1, dy, mix, norm_mix_post, norm_ff_pre)
    dwfi3, dwfo = _wgrad_ff(hf, df, f, dff)
    own_ff = [dwfi3, dwfo.reshape(N_DEV, D_FF // N_DEV, D)]
    x_ff = _start_copies(own_ff, [SCATTER] * 2, "exchange_ff_start")
    dgate, da, datt, dwo, dwa, dwb = _bwd_mix(dmix, proj, a2, b2, merged, a, att, wo, wa, wb, after=x_ff[-1])
    own_mix = [g.reshape(N_DEV, D // N_DEV, D) for g in (dwa, dwb, dwo)]
    x_mix = _start_copies(own_mix, [SCATTER] * 3, "exchange_mix_start")
    dq, dkv, dsink = _bwd_attn(qr, kr, probs, psink, proj, cos, sin, datt, after=x_mix[-1])
    duv, dws, dbs, dlng, dlnb = _bwd_sgu(proj, tanhs, ln_stats, da, ln_v_gain, ln_v_bias, ws, bst)
    small_grads = {"ln_v_gain": dlng, "ln_v_bias": dlnb, "w_spatial": dws, "b_spatial": dbs, "sinks": dsink[:, :N_Q],
                   "norm_mix_post": dg1, "norm_ff_pre": dg2, "norm_ff_post": dg3}
    x_small = _start_copies([_pack_small(small_grads)], [SPREAD], "exchange_small_start")
    dwin = _wgrad_rows(h, [dgate], sum(IN_SEG_WIDTHS[:3]), None, "wgrad_in_gates")
    dwin = _wgrad_rows(h, [duv], 0, dwin, "wgrad_in_uv")
    dwin = _wgrad_rows(h, [dq, dkv], IN_SEG_WIDTHS[0], dwin, "wgrad_in_qkv")
    own_in = [dwin.reshape(N_DEV, IN_W // N_DEV, D)]
    x_in = _start_copies(own_in, [SCATTER], "exchange_in_start", after=x_small[-1])
    grad_x, dg0 = _bwd_in(duv, dq, dkv, dgate, win, xt, dx1, norm_mix_pre, after=x_in[-1])

    results = {}

    def update(n, own, parts, transposed=False):
        state = [given[k + n][0].T if transposed else given[k + n][0] for k in ("", "m_", "v_")]
        res = _sum_adamw_peers(me_arr, own, parts, *state, "adamw_" + n, False)
        results[n] = [(r.T if transposed else r).reshape(given[n].shape) for r in res]

    own_ff, p_ff = _wait_copies(x_ff, grad_x, "exchange_ff_wait")
    update("w_ff_in", own_ff[0], p_ff[0])
    update("w_ff_out", own_ff[1], p_ff[1])
    own_mix, p_mix = _wait_copies(x_mix, results["w_ff_out"][0], "exchange_mix_wait")
    for n, own, parts in zip(("w_a", "w_b", "w_o"), own_mix, p_mix):
        update(n, own, parts)
    tail = jnp.concatenate([dg0.reshape(8, 128), jnp.tile(loss_part, (8, 1))], axis=0)
    (tail_all,) = _all_to_all([tail], [True], "exchange_tail", after=results["w_o"][0])
    dg0_all = tail_all[:, :8]
    own_small, p_small = _wait_copies(x_small, tail_all, "exchange_small_wait")
    own_in, p_in = _wait_copies(x_in, p_small[0], "exchange_in_wait")
    update("w_in", own_in[0], p_in[0], transposed=True)
    first = sum(SMALL_ROWS[n] for n in SMALL[:SMALL.index("w_spatial")])
    packed = _sum_adamw_peers(me_arr, own_small[0], p_small[0], *small_state, "adamw_small", True,
                              also_rows=(first, first + SMALL_ROWS["w_spatial"]))
    shapes = {n: given[n].shape for n in SMALL}
    unpacked = [_unpack_small(p, shapes) for p in packed[:4]]
    for n in SMALL:
        results[n] = [u[n] for u in unpacked]
    results["w_spatial"] = [r.reshape(w_spatial.shape) for r in packed[4:]]
    n = "norm_mix_pre"
    results[n] = [r.reshape(given[n].shape) for r in _sum_adamw(
        dg0_all, given[n].reshape(8, 128), given["m_" + n].reshape(8, 128), given["v_" + n].reshape(8, 128), "adamw_" + n)]

    loss = jnp.sum(tail_all[:, 8, 0])
    order = ("w_in", "ln_v_gain", "ln_v_bias", "w_spatial", "b_spatial", "sinks", "w_a", "w_b", "w_o", "norm_mix_pre",
             "norm_mix_post", "w_ff_in", "w_ff_out", "norm_ff_pre", "norm_ff_post")
    out = [loss, grad_x.reshape(x.shape)]
    for k in range(4):
        out += [results[n][k] for n in order]
    return tuple(out)
```

```python
import jax
import jax.numpy as jnp
from jax import lax
from jax.experimental import pallas as pl
from jax.experimental.pallas import tpu as pltpu

F32 = jnp.float32
BF16 = jnp.bfloat16

N_DEV = 8
D = 1024
D_FF = 4096
IN_W = 5632
CHUNK = 128
GROUPS = 8
HEAD = 64
N_Q = 16
N_KV = 4
ROPE = 16
ROPE_THETA = 500000.0
EPS = 1e-6
OFF_Q, OFF_K, OFF_VA, OFF_GA, OFF_GB = 2048, 3072, 3328, 3584, 4608

ADAM_LR = 0.001
ADAM_B1 = 0.9
ADAM_B2 = 0.999
ADAM_EPS = 1e-08
ADAM_WD = 0.01
ADAM_STEP = 10

VMEM_LIMIT = 62 * 1024 * 1024

SDS = jax.ShapeDtypeStruct
MESH = pl.DeviceIdType.MESH


def _params(n_axes):
    return pltpu.CompilerParams(dimension_semantics=("arbitrary",) * n_axes, vmem_limit_bytes=VMEM_LIMIT)


def _nt(a, b):
    return lax.dot_general(a, b, (((1,), (1,)), ((), ())), preferred_element_type=F32)


def _tn(a, b):
    return lax.dot_general(a, b, (((0,), (0,)), ((), ())), preferred_element_type=F32)


def _nn(a, b):
    return jnp.dot(a, b, preferred_element_type=F32)


def _gelu(x):
    t = jnp.tanh(0.7978845608028654 * (x + 0.044715 * (x * x * x)))
    return 0.5 * x * (1.0 + t), t


def _gelu_grad(x, t):
    return 0.5 * (1.0 + t) + 0.5 * x * (1.0 - t * t) * (0.7978845608028654 * (1.0 + 3.0 * 0.044715 * x * x))


def _sigmoid(x):
    return 1.0 / (1.0 + jnp.exp(-x))


def _rms_stats(v):
    r = lax.rsqrt(jnp.mean(v * v, axis=-1, keepdims=True) + EPS)
    return r, v * r


def _rms_bwd(d, vhat, r, g):
    gd = g * d
    return r * (gd - vhat * jnp.mean(gd * vhat, axis=-1, keepdims=True))


def _colsum(v):
    return jnp.sum(v, axis=0, keepdims=True)


_ANY = pl.BlockSpec(memory_space=pl.ANY)


def _after(body, n_in, after):
    if after is None:
        return body, [], []
    deps = list(after) if isinstance(after, (list, tuple)) else [after]

    def ordered(*refs):
        return body(*refs[:n_in], *refs[n_in + len(deps):])

    return ordered, [_ANY] * len(deps), deps


def _rms_pre(x, g0, after=None):
    T = x.shape[0]
    tm = min(T, 1024)

    def body(x_ref, g_ref, h_ref):
        _, xh = _rms_stats(x_ref[...])
        h_ref[...] = (xh * g_ref[...]).astype(BF16)

    body, dep_specs, deps = _after(body, 2, after)
    return pl.pallas_call(
        body, name="rms_pre", grid=(T // tm,),
        in_specs=[pl.BlockSpec((tm, D), lambda i: (i, 0)), pl.BlockSpec((1, D), lambda i: (0, 0))] + dep_specs,
        out_specs=pl.BlockSpec((tm, D), lambda i: (i, 0)),
        out_shape=SDS((T, D), BF16),
        compiler_params=_params(1),
    )(x, g0, *deps)


def _fwd_in(h, win_t):
    T = h.shape[0]
    tm, tn = min(T, 1024), 1408

    def body(h_ref, w_ref, p_ref):
        for j in range(IN_W // tn):
            cols = slice(j * tn, (j + 1) * tn)
            p_ref[:, cols] = _nt(h_ref[...], w_ref[cols, :]).astype(BF16)

    return pl.pallas_call(
        body, name="fwd_in", grid=(T // tm,),
        in_specs=[pl.BlockSpec((tm, D), lambda i: (i, 0)), _resident((IN_W, D))],
        out_specs=pl.BlockSpec((tm, IN_W), lambda i: (i, 0)),
        out_shape=SDS((T, IN_W), BF16),
        compiler_params=_params(1),
    )(h, win_t)


def _sgu_forward_parts(u_ref, vs_ref, lng_ref, lnb_ref):
    u = u_ref[...].astype(F32)
    vs = vs_ref[...].astype(F32)
    gu, tu = _gelu(u)
    gv, tv = _gelu(vs)
    mu = jnp.mean(gv, axis=-1, keepdims=True)
    dv = gv - mu
    rstd = lax.rsqrt(jnp.mean(dv * dv, axis=-1, keepdims=True) + EPS)
    vhat = dv * rstd
    vn = (vhat * lng_ref[...] + lnb_ref[...]).astype(BF16)
    return gu, tu, tv, mu, rstd, vn


def _sgu_forward_replay(u_ref, vs_ref, t_ref, stat_ref, lng_ref, lnb_ref):
    u = u_ref[...].astype(F32)
    vs = vs_ref[...].astype(F32)
    tu = t_ref[:, :D].astype(F32)
    tv = t_ref[:, D:].astype(F32)
    gu = 0.5 * u * (1.0 + tu)
    rstd = stat_ref[:, 1:2]
    vhat = (0.5 * vs * (1.0 + tv) - stat_ref[:, 0:1]) * rstd
    vn = (vhat * lng_ref[...] + lnb_ref[...]).astype(BF16)
    return u, vs, gu, tu, tv, rstd, vhat, vn


def _masked_ws(ws_ref, g):
    row = lax.broadcasted_iota(jnp.int32, (CHUNK, CHUNK), 0)
    col = lax.broadcasted_iota(jnp.int32, (CHUNK, CHUNK), 1)
    return jnp.where(row >= col, ws_ref[g], 0.0).astype(BF16)


def _fwd_sgu(proj, lng, lnb, ws, bst, after=None):
    T = proj.shape[0]
    tc = min(T, 512)

    def body(u_ref, vs_ref, lng_ref, lnb_ref, ws_ref, bst_ref, a_ref, t_ref, stat_ref):
        gu, tu, tv, mu, rstd, vn = _sgu_forward_parts(u_ref, vs_ref, lng_ref, lnb_ref)
        t_ref[:, :D] = tu.astype(BF16)
        t_ref[:, D:] = tv.astype(BF16)
        lane = lax.broadcasted_iota(jnp.int32, (tc, 128), 1)
        stat_ref[...] = jnp.where(lane == 0, mu, jnp.where(lane == 1, rstd, 0.0))
        for g in range(GROUPS):
            wm = _masked_ws(ws_ref, g)
            cols = slice(g * CHUNK, (g + 1) * CHUNK)
            for c in range(tc // CHUNK):
                rows = slice(c * CHUNK, (c + 1) * CHUNK)
                mixed = _nn(wm, vn[rows, cols]) + bst_ref[:, g:g + 1]
                a_ref[rows, cols] = (gu[rows, cols] * mixed).astype(BF16)

    body, dep_specs, deps = _after(body, 6, after)
    return pl.pallas_call(
        body, name="fwd_sgu", grid=(T // tc,),
        in_specs=[pl.BlockSpec((tc, D), lambda i: (i, 0)), pl.BlockSpec((tc, D), lambda i: (i, 1)),
                  pl.BlockSpec((1, D), lambda i: (0, 0)), pl.BlockSpec((1, D), lambda i: (0, 0)),
                  pl.BlockSpec((GROUPS, CHUNK, CHUNK), lambda i: (0, 0, 0)),
                  pl.BlockSpec((CHUNK, GROUPS), lambda i: (0, 0))] + dep_specs,
        out_specs=[pl.BlockSpec((tc, D), lambda i: (i, 0)), pl.BlockSpec((tc, 2 * D), lambda i: (i, 0)),
                   pl.BlockSpec((tc, 128), lambda i: (i, 0))],
        out_shape=[SDS((T, D), BF16), SDS((T, 2 * D), BF16), SDS((T, 128), F32)],
        compiler_params=_params(1),
    )(proj, proj, lng, lnb, ws, bst, *deps)


def _rope_tables(posf, invf, sgn, after=None):
    T = posf.shape[0]
    tr = min(T, 1024)

    def body(pos_ref, invf_ref, sgn_ref, c_ref, s_ref):
        ang = pos_ref[...] * invf_ref[...]
        c_ref[...] = jnp.cos(ang)
        s = jnp.sin(ang)
        s_ref[:, :128] = jnp.where(sgn_ref[...] < 0.0, -s, 0.0)
        s_ref[:, 128:] = jnp.where(sgn_ref[...] > 0.0, s, 0.0)

    body, dep_specs, deps = _after(body, 3, after)
    return pl.pallas_call(
        body, name="rope_tables", grid=(T // tr,),
        in_specs=[pl.BlockSpec((tr, 1), lambda i: (i, 0)), pl.BlockSpec((1, 128), lambda i: (0, 0)),
                  pl.BlockSpec((1, 128), lambda i: (0, 0))] + dep_specs,
        out_specs=[pl.BlockSpec((tr, 128), lambda i: (i, 0)), pl.BlockSpec((tr, 256), lambda i: (i, 0))],
        out_shape=[SDS((T, 128), F32), SDS((T, 256), F32)],
        compiler_params=_params(1),
    )(posf, invf, sgn, *deps)


def _rope(v, c, s):
    v = v.astype(F32)
    return v * c + pltpu.roll(v, 128 - ROPE // 2, 1) * s[:, :128] + pltpu.roll(v, ROPE // 2, 1) * s[:, 128:]


def _rope_bwd(dv, c, s):
    return dv * c + pltpu.roll(dv * s[:, :128], ROPE // 2, 1) + pltpu.roll(dv * s[:, 128:], 128 - ROPE // 2, 1)


def _fold_masks(first):
    jj = lax.broadcasted_iota(jnp.int32, (CHUNK, CHUNK), 0)
    t = lax.broadcasted_iota(jnp.int32, (CHUNK, CHUNK), 1)
    prev = jj > t
    return prev, jnp.where(prev & first, -1e30, 0.0)


def _fold(band, prev):
    return jnp.where(prev, band[:CHUNK], band[CHUNK:])


def _unfold(folded, prev):
    return jnp.concatenate([jnp.where(prev, folded, 0.0), jnp.where(prev, 0.0, folded)], axis=0)


def _softmax_sink(s, sink, key_axis):
    m = jnp.maximum(jnp.max(s, axis=key_axis, keepdims=True), sink)
    p = jnp.exp(s - m)
    esink = jnp.exp(sink - m)
    inv = 1.0 / (jnp.sum(p, axis=key_axis, keepdims=True) + esink)
    return p * inv, esink * inv


def _head_pair_operand(slab, g):
    lo = lax.broadcasted_iota(jnp.int32, slab.shape, 1) < HEAD
    if g % 2 == 0:
        first = jnp.where(lo, slab, 0.0)
        second = pltpu.roll(first, HEAD, 1)
    else:
        second = jnp.where(lo, 0.0, slab)
        first = pltpu.roll(second, HEAD, 1)
    return jnp.concatenate([first, second], axis=0).astype(BF16)


def _head_pair_gradient(acc, g):
    top, bot = acc[:2 * CHUNK], acc[2 * CHUNK:]
    lo = lax.broadcasted_iota(jnp.int32, top.shape, 1) < HEAD
    if g % 2 == 0:
        return jnp.where(lo, top, 0.0) + pltpu.roll(jnp.where(lo, 0.0, bot), HEAD, 1)
    return pltpu.roll(jnp.where(lo, top, 0.0), HEAD, 1) + jnp.where(lo, 0.0, bot)


PAIRS_PER_KV = N_Q // N_KV // 2
KV_W = N_KV * HEAD


def _band(prev_ref, cur_ref, cols=slice(None)):
    return jnp.concatenate([prev_ref[:, cols], cur_ref[:, cols]], axis=0)


def _fwd_attn(proj, cos, sin, sinks):
    T = proj.shape[0]
    nb = T // CHUNK
    cur = lambda i: i
    prev = lambda i: jnp.maximum(i - 1, 0)

    def body(q_ref, kp_ref, kc_ref, vp_ref, vc_ref, cp_ref, cc_ref, sp_ref, sc_ref, sink_ref,
             o_ref, qr_ref, kr_ref, p_ref, psink_ref):
        prev_slot, bias = _fold_masks(pl.program_id(0) == 0)
        c_band, s_band = _band(cp_ref, cc_ref), _band(sp_ref, sc_ref)
        for j in range(KV_W // 128):
            cols = slice(j * 128, (j + 1) * 128)
            k_slab = _rope(_band(kp_ref, kc_ref, cols), c_band, s_band)
            kr_ref[:, cols] = k_slab[CHUNK:].astype(BF16)
            v_slab = _band(vp_ref, vc_ref, cols).astype(F32)
            for g in (2 * j, 2 * j + 1):
                k2 = _head_pair_operand(k_slab, g)
                v2 = _head_pair_operand(v_slab, g)
                pairs = [g * PAIRS_PER_KV + r for r in range(PAIRS_PER_KV)]
                qps = []
                for pair in pairs:
                    lanes = slice(pair * 128, (pair + 1) * 128)
                    qps.append((_rope(q_ref[:, lanes], cc_ref[...], sc_ref[...]) * (HEAD ** -0.5)).astype(BF16))
                    qr_ref[:, lanes] = qps[-1]
                s2 = _nt(k2, jnp.concatenate(qps, axis=0))
                pcols = []
                for r, pair in enumerate(pairs):
                    ps = []
                    for e in range(2):
                        head = 2 * pair + e
                        s = _fold(s2[e * 2 * CHUNK:(e + 1) * 2 * CHUNK, r * 128:(r + 1) * 128], prev_slot) + bias
                        p, psink = _softmax_sink(s, sink_ref[head], 0)
                        p = p.astype(BF16)
                        p_ref[head] = p
                        psink_ref[head:head + 1, :] = psink
                        ps.append(_unfold(p, prev_slot))
                    pcols.append(jnp.concatenate(ps, axis=0))
                o = _tn(jnp.concatenate(pcols, axis=1), v2).astype(BF16)
                for r, pair in enumerate(pairs):
                    o_ref[:, pair * 128:(pair + 1) * 128] = o[r * CHUNK:(r + 1) * CHUNK]

    table = lambda which, width: pl.BlockSpec((CHUNK, width), lambda i: (which(i), 0))
    return pl.pallas_call(
        body, name="fwd_attn", grid=(nb,),
        in_specs=[pl.BlockSpec((CHUNK, D), lambda i: (i, OFF_Q // D)),
                  pl.BlockSpec((CHUNK, KV_W), lambda i: (prev(i), OFF_K // KV_W)),
                  pl.BlockSpec((CHUNK, KV_W), lambda i: (i, OFF_K // KV_W)),
                  pl.BlockSpec((CHUNK, KV_W), lambda i: (prev(i), OFF_VA // KV_W)),
                  pl.BlockSpec((CHUNK, KV_W), lambda i: (i, OFF_VA // KV_W)),
                  table(prev, 128), table(cur, 128), table(prev, 256), table(cur, 256),
                  pl.BlockSpec(memory_space=pltpu.SMEM)],
        out_specs=[pl.BlockSpec((CHUNK, D), lambda i: (i, 0)), pl.BlockSpec((CHUNK, D), lambda i: (i, 0)),
                   pl.BlockSpec((CHUNK, KV_W), lambda i: (i, 0)),
                   pl.BlockSpec((None, N_Q, CHUNK, CHUNK), lambda i: (i, 0, 0, 0)),
                   pl.BlockSpec((None, N_Q, CHUNK), lambda i: (i, 0, 0))],
        out_shape=[SDS((T, D), BF16), SDS((T, D), BF16), SDS((T, KV_W), BF16),
                   SDS((nb, N_Q, CHUNK, CHUNK), BF16), SDS((nb, N_Q, CHUNK), F32)],
        compiler_params=_params(1),
    )(proj, proj, proj, proj, proj, cos, cos, sin, sin, sinks)


def _fwd_mix(a, att, proj, x, wa, wb, wo, g1, g2):
    T = x.shape[0]
    tm = min(T, 512)
    half = D // 2

    def body(a_ref, att_ref, ga0, ga1, gb0, gb1, x_ref, wa_ref, wb_ref, wo_ref, g1_ref, g2_ref,
             mg_ref, a2_ref, b2_ref, mix_ref, x1_ref, hf_ref):
        a2 = _nn(a_ref[...], wa_ref[...])
        b2 = _nn(att_ref[...], wb_ref[...])
        ga = jnp.concatenate([ga0[...], ga1[...]], axis=1).astype(F32)
        gb = jnp.concatenate([gb0[...], gb1[...]], axis=1).astype(F32)
        merged = (_sigmoid(ga) * a2 + _sigmoid(gb) * b2).astype(BF16)
        a2_ref[...] = a2.astype(BF16)
        b2_ref[...] = b2.astype(BF16)
        mg_ref[...] = merged
        mix = _nn(merged, wo_ref[...])
        mix_ref[...] = mix
        _, mh = _rms_stats(mix)
        x1 = x_ref[...] + mh * g1_ref[...]
        x1_ref[...] = x1
        _, xh = _rms_stats(x1)
        hf_ref[...] = (xh * g2_ref[...]).astype(BF16)

    row = lambda i: (i, 0)
    const = lambda i: (0, 0)
    gspec = lambda off: pl.BlockSpec((tm, half), lambda i: (i, off // half))
    return pl.pallas_call(
        body, name="fwd_mix", grid=(T // tm,),
        in_specs=[pl.BlockSpec((tm, D), row), pl.BlockSpec((tm, D), row),
                  gspec(OFF_GA), gspec(OFF_GA + half), gspec(OFF_GB), gspec(OFF_GB + half),
                  pl.BlockSpec((tm, D), row), _resident((D, D)), _resident((D, D)),
                  _resident((D, D)), pl.BlockSpec((1, D), const), pl.BlockSpec((1, D), const)],
        out_specs=[pl.BlockSpec((tm, D), row)] * 6,
        out_shape=[SDS((T, D), BF16), SDS((T, D), BF16), SDS((T, D), BF16), SDS((T, D), F32), SDS((T, D), F32),
                   SDS((T, D), BF16)],
        compiler_params=_params(1),
    )(a, att, proj, proj, proj, proj, x, wa, wb, wo, g1, g2)


FF_SPLIT = N_DEV
FF_TILE = D_FF // FF_SPLIT


def _fwd_ff(hf, wfi3, wfo, x1, tgt, g3):
    T = hf.shape[0]
    tm = min(T, 512)

    def body(hf_ref, wfi_ref, wfo_ref, x1_ref, tgt_ref, g3_ref, f_ref, dy_ref, dff_ref, dg3_ref, loss_ref, r_s):
        @pl.when(pl.program_id(0) == 0)
        def _():
            dg3_ref[...] = jnp.zeros_like(dg3_ref)
            loss_ref[...] = jnp.zeros_like(loss_ref)

        hf_t = hf_ref[...]
        for s in range(FF_SPLIT):
            cols = slice(s * FF_TILE, (s + 1) * FF_TILE)
            f = _nn(hf_t, wfi_ref[s]).astype(BF16)
            f_ref[:, cols] = f
            rl = jnp.maximum(f.astype(F32), 0.0)
            r_s[:, cols] = (rl * rl).astype(BF16)
        r3, fh = _rms_stats(_nn(r_s[...], wfo_ref[...]))
        e = x1_ref[...] + fh * g3_ref[...] - tgt_ref[...]
        loss_ref[...] += jnp.sum(e * e) * (0.5 / D)
        dy = e * (1.0 / D)
        dy_ref[...] = dy
        dg3_ref[...] += _colsum(dy * fh)
        dff_ref[...] = _rms_bwd(dy, fh, r3, g3_ref[...]).astype(BF16)

    row = lambda i: (i, 0)
    const = lambda i: (0, 0)
    return pl.pallas_call(
        body, name="fwd_ff", grid=(T // tm,),
        in_specs=[pl.BlockSpec((tm, D), row), _resident((FF_SPLIT, D, FF_TILE)), _resident((D_FF, D)),
                  pl.BlockSpec((tm, D), row),
                  pl.BlockSpec((tm, D), row), pl.BlockSpec((1, D), const)],
        out_specs=[pl.BlockSpec((tm, D_FF), row), pl.BlockSpec((tm, D), row),
                   pl.BlockSpec((tm, D), row), pl.BlockSpec((1, D), const), pl.BlockSpec((1, 128), const)],
        out_shape=[SDS((T, D_FF), BF16), SDS((T, D), F32), SDS((T, D), BF16), SDS((1, D), F32), SDS((1, 128), F32)],
        scratch_shapes=[pltpu.VMEM((tm, D_FF), BF16)],
        compiler_params=_params(1),
    )(hf, wfi3, wfo, x1, tgt, g3)


def _bwd_ff(dff, f, wfi3, wfo, x1, dy, mix, g1, g2):
    T = dff.shape[0]
    tm = min(T, 512)

    def body(dff_ref, f_ref, wfi_ref, wfo_ref, x1_ref, dy_ref, mix_ref, g1_ref, g2_ref,
             df_ref, dx1_ref, dmix_ref, dg2_ref, dg1_ref):
        @pl.when(pl.program_id(0) == 0)
        def _():
            dg2_ref[...] = jnp.zeros_like(dg2_ref)
            dg1_ref[...] = jnp.zeros_like(dg1_ref)

        dff_t = dff_ref[...]
        dhf = None
        for s in range(FF_SPLIT):
            cols = slice(s * FF_TILE, (s + 1) * FF_TILE)
            dr = _nt(dff_t, wfo_ref[cols, :])
            df = (dr * (2.0 * jnp.maximum(f_ref[:, cols].astype(F32), 0.0))).astype(BF16)
            df_ref[:, cols] = df
            part = _nt(df, wfi_ref[s])
            dhf = part if dhf is None else dhf + part
        r2, xh = _rms_stats(x1_ref[...])
        dg2_ref[...] += _colsum(dhf * xh)
        dx1 = dy_ref[...] + _rms_bwd(dhf, xh, r2, g2_ref[...])
        dx1_ref[...] = dx1
        r1, mh = _rms_stats(mix_ref[...])
        dg1_ref[...] += _colsum(dx1 * mh)
        dmix_ref[...] = _rms_bwd(dx1, mh, r1, g1_ref[...]).astype(BF16)

    row = lambda i: (i, 0)
    const = lambda i: (0, 0)
    return pl.pallas_call(
        body, name="bwd_ff", grid=(T // tm,),
        in_specs=[pl.BlockSpec((tm, D), row), pl.BlockSpec((tm, D_FF), row),
                  _resident((FF_SPLIT, D, FF_TILE)), _resident((D_FF, D)),
                  pl.BlockSpec((tm, D), row), pl.BlockSpec((tm, D), row), pl.BlockSpec((tm, D), row),
                  pl.BlockSpec((1, D), const), pl.BlockSpec((1, D), const)],
        out_specs=[pl.BlockSpec((tm, D_FF), row), pl.BlockSpec((tm, D), row),
                   pl.BlockSpec((tm, D), row), pl.BlockSpec((1, D), const), pl.BlockSpec((1, D), const)],
        out_shape=[SDS((T, D_FF), BF16), SDS((T, D), F32), SDS((T, D), BF16), SDS((1, D), F32), SDS((1, D), F32)],
        compiler_params=_params(1),
    )(dff, f, wfi3, wfo, x1, dy, mix, g1, g2)


def _wgrad_ff(hf, df, f, dff):
    T = hf.shape[0]
    tt = min(T, 2048)
    slabs = 2
    wide = slabs * FF_TILE

    def body(hf_ref, df_ref, f_ref, dff_ref, dwfi_ref, dwfo_ref, acc_i, acc_o):
        t = pl.program_id(1)

        @pl.when(t == 0)
        def _():
            acc_i[...] = jnp.zeros_like(acc_i)
            acc_o[...] = jnp.zeros_like(acc_o)

        acc_i[...] += _tn(hf_ref[...], df_ref[...])
        rl = jnp.maximum(f_ref[...].astype(F32), 0.0)
        acc_o[...] += _tn((rl * rl).astype(BF16), dff_ref[...])

        @pl.when(t == T // tt - 1)
        def _():
            for s in range(slabs):
                dwfi_ref[s] = acc_i[:, s * FF_TILE:(s + 1) * FF_TILE].astype(BF16)
            dwfo_ref[...] = acc_o[...].astype(BF16)

    return pl.pallas_call(
        body, name="wgrad_ff", grid=(D_FF // wide, T // tt),
        in_specs=[pl.BlockSpec((tt, D), lambda p, t: (t, 0)), pl.BlockSpec((tt, wide), lambda p, t: (t, p)),
                  pl.BlockSpec((tt, wide), lambda p, t: (t, p)), pl.BlockSpec((tt, D), lambda p, t: (t, 0))],
        out_specs=[pl.BlockSpec((slabs, D, FF_TILE), lambda p, t: (p, 0, 0)), pl.BlockSpec((wide, D), lambda p, t: (p, 0))],
        out_shape=[SDS((FF_SPLIT, D, FF_TILE), BF16), SDS((D_FF, D), BF16)],
        scratch_shapes=[pltpu.VMEM((D, wide), F32), pltpu.VMEM((wide, D), F32)],
        compiler_params=_params(2),
    )(hf, df, f, dff)


def _bwd_mix(dmix, proj, a2, b2, merged, a, att, wo, wa, wb, after=None):
    T = dmix.shape[0]
    tm = min(T, 512)
    half = D // 2
    last = T // tm - 1

    def body(dmix_ref, ga0, ga1, gb0, gb1, a2_ref, b2_ref, mg_ref, a_ref, att_ref, wo_ref, wa_ref, wb_ref,
             dg_ref, da_ref, datt_ref, dwo_ref, dwa_ref, dwb_ref, acc, stage, sem):
        t = pl.program_id(0)

        @pl.when(t == 0)
        def _():
            acc[...] = jnp.zeros_like(acc)

        dmix_t = dmix_ref[...]
        dmg = _nt(dmix_t, wo_ref[...])
        sa = _sigmoid(jnp.concatenate([ga0[...], ga1[...]], axis=1).astype(F32))
        sb = _sigmoid(jnp.concatenate([gb0[...], gb1[...]], axis=1).astype(F32))
        da2 = (dmg * sa).astype(BF16)
        db2 = (dmg * sb).astype(BF16)
        dg_ref[:, :D] = (dmg * a2_ref[...].astype(F32) * (sa * (1.0 - sa))).astype(BF16)
        dg_ref[:, D:] = (dmg * b2_ref[...].astype(F32) * (sb * (1.0 - sb))).astype(BF16)
        da_ref[...] = _nt(da2, wa_ref[...]).astype(BF16)
        datt_ref[...] = _nt(db2, wb_ref[...]).astype(BF16)
        acc[0] += _tn(mg_ref[...], dmix_t)
        acc[1] += _tn(a_ref[...], da2)
        acc[2] += _tn(att_ref[...], db2)

        @pl.when(t == last)
        def _():
            for k, dw_ref in enumerate((dwo_ref, dwa_ref, dwb_ref)):
                stage[...] = acc[k].astype(BF16)
                out = pltpu.make_async_copy(stage, dw_ref, sem)
                out.start()
                out.wait()

    row = lambda i: (i, 0)
    gspec = lambda off: pl.BlockSpec((tm, half), lambda i: (i, off // half))
    body, dep_specs, deps = _after(body, 13, after)
    return pl.pallas_call(
        body, name="bwd_mix", grid=(T // tm,),
        in_specs=[pl.BlockSpec((tm, D), row), gspec(OFF_GA), gspec(OFF_GA + half), gspec(OFF_GB), gspec(OFF_GB + half)]
        + [pl.BlockSpec((tm, D), row)] * 5 + [_resident((D, D))] * 3 + dep_specs,
        out_specs=[pl.BlockSpec((tm, 2 * D), row), pl.BlockSpec((tm, D), row), pl.BlockSpec((tm, D), row)] + [_ANY] * 3,
        out_shape=[SDS((T, 2 * D), BF16), SDS((T, D), BF16), SDS((T, D), BF16)] + [SDS((D, D), BF16)] * 3,
        scratch_shapes=[pltpu.VMEM((3, D, D), F32), pltpu.VMEM((D, D), BF16), pltpu.SemaphoreType.DMA],
        compiler_params=_params(1),
    )(dmix, proj, proj, proj, proj, a2, b2, merged, a, att, wo, wa, wb, *deps)


def _bwd_attn(qr, kr, probs, psink, proj, cos, sin, datt, after=None):
    T = proj.shape[0]
    nb = T // CHUNK
    cur = lambda i: jnp.minimum(i, nb - 1)
    prev = lambda i: jnp.maximum(jnp.minimum(i, nb - 1) - 1, 0)

    def body(q_ref, kp_ref, kc_ref, vp_ref, vc_ref, cp_ref, cc_ref, sp_ref, sc_ref, p_ref, psink_ref, do_ref,
             dq_ref, dkv_ref, dsink_ref, carry_k, carry_v):
        i = pl.program_id(0)

        @pl.when(i == 0)
        def _():
            carry_k[...] = jnp.zeros_like(carry_k)
            carry_v[...] = jnp.zeros_like(carry_v)
            dsink_ref[...] = jnp.zeros_like(dsink_ref)

        @pl.when(i < nb)
        def _():
            prev_slot, _ = _fold_masks(i == 0)
            c_band, s_band = _band(cp_ref, cc_ref), _band(sp_ref, sc_ref)
            for j in range(KV_W // 128):
                cols = slice(j * 128, (j + 1) * 128)
                k_slab = _band(kp_ref, kc_ref, cols).astype(F32)
                v_slab = _band(vp_ref, vc_ref, cols).astype(F32)
                dk_slab = jnp.zeros((2 * CHUNK, 128), F32)
                dv_slab = jnp.zeros((2 * CHUNK, 128), F32)
                for g in (2 * j, 2 * j + 1):
                    k2 = _head_pair_operand(k_slab, g)
                    v2 = _head_pair_operand(v_slab, g)
                    pairs = [g * PAIRS_PER_KV + r for r in range(PAIRS_PER_KV)]
                    q_stack = jnp.concatenate([q_ref[:, pr * 128:(pr + 1) * 128] for pr in pairs], axis=0)
                    do_stack = jnp.concatenate([do_ref[:, pr * 128:(pr + 1) * 128] for pr in pairs], axis=0)
                    dp2 = _nt(v2, do_stack)
                    pcols, dscols = [], []
                    for r, pair in enumerate(pairs):
                        ps, dss = [], []
                        for e in range(2):
                            head = 2 * pair + e
                            p_b = p_ref[head]
                            p = p_b.astype(F32)
                            dp = _fold(dp2[e * 2 * CHUNK:(e + 1) * 2 * CHUNK, r * 128:(r + 1) * 128], prev_slot)
                            delta = jnp.sum(p * dp, axis=0, keepdims=True)
                            ps.append(_unfold(p_b, prev_slot))
                            dss.append(_unfold((p * (dp - delta)).astype(BF16), prev_slot))
                            dsink_ref[head:head + 1, :] -= psink_ref[head:head + 1, :] * delta
                        pcols.append(jnp.concatenate(ps, axis=0))
                        dscols.append(jnp.concatenate(dss, axis=0))
                    ds2 = jnp.concatenate(dscols, axis=1)
                    dq = _tn(ds2, k2) * (HEAD ** -0.5)
                    for r, pair in enumerate(pairs):
                        dq_ref[:, pair * 128:(pair + 1) * 128] = _rope_bwd(
                            dq[r * CHUNK:(r + 1) * CHUNK], cc_ref[...], sc_ref[...]).astype(BF16)
                    dk_slab = dk_slab + _head_pair_gradient(_nn(ds2, q_stack), g)
                    dv_slab = dv_slab + _head_pair_gradient(_nn(jnp.concatenate(pcols, axis=1), do_stack), g)
                dk_slab = _rope_bwd(dk_slab, c_band, s_band)
                vcols = slice(KV_W + j * 128, KV_W + (j + 1) * 128)
                dkv_ref[:, cols] = (carry_k[:, cols] + dk_slab[:CHUNK]).astype(BF16)
                dkv_ref[:, vcols] = (carry_v[:, cols] + dv_slab[:CHUNK]).astype(BF16)
                carry_k[:, cols] = dk_slab[CHUNK:]
                carry_v[:, cols] = dv_slab[CHUNK:]

        @pl.when(i == nb)
        def _():
            dkv_ref[:, :KV_W] = carry_k[...].astype(BF16)
            dkv_ref[:, KV_W:] = carry_v[...].astype(BF16)

    table = lambda which, width: pl.BlockSpec((CHUNK, width), lambda i: (which(i), 0))
    body, dep_specs, deps = _after(body, 12, after)
    return pl.pallas_call(
        body, name="bwd_attn", grid=(nb + 1,),
        in_specs=[pl.BlockSpec((CHUNK, D), lambda i: (cur(i), 0)),
                  pl.BlockSpec((CHUNK, KV_W), lambda i: (prev(i), 0)),
                  pl.BlockSpec((CHUNK, KV_W), lambda i: (cur(i), 0)),
                  pl.BlockSpec((CHUNK, KV_W), lambda i: (prev(i), OFF_VA // KV_W)),
                  pl.BlockSpec((CHUNK, KV_W), lambda i: (cur(i), OFF_VA // KV_W)),
                  table(prev, 128), table(cur, 128), table(prev, 256), table(cur, 256),
                  pl.BlockSpec((None, N_Q, CHUNK, CHUNK), lambda i: (cur(i), 0, 0, 0)),
                  pl.BlockSpec((None, N_Q, CHUNK), lambda i: (cur(i), 0, 0)),
                  pl.BlockSpec((CHUNK, D), lambda i: (cur(i), 0))] + dep_specs,
        out_specs=[pl.BlockSpec((CHUNK, D), lambda i: (cur(i), 0)),
                   pl.BlockSpec((CHUNK, 2 * KV_W), lambda i: (jnp.maximum(i - 1, 0), 0)),
                   pl.BlockSpec((N_Q, CHUNK), lambda i: (0, 0))],
        out_shape=[SDS((T, D), BF16), SDS((T, 2 * KV_W), BF16), SDS((N_Q, CHUNK), F32)],
        scratch_shapes=[pltpu.VMEM((CHUNK, KV_W), F32), pltpu.VMEM((CHUNK, KV_W), F32)],
        compiler_params=_params(1),
    )(qr, kr, kr, proj, proj, cos, cos, sin, sin, probs, psink, datt, *deps)


def _bwd_sgu(proj, tanhs, stats, da, lng, lnb, ws, bst):
    T = proj.shape[0]
    tc = min(T, 512)
    nsteps = T // tc

    def body(u_ref, vs_ref, t_ref, stat_ref, da_ref, lng_ref, lnb_ref, ws_ref, bst_ref,
             duv_ref, dws_ref, dbs_ref, dlng_ref, dlnb_ref, dvn_s, dgu_s, dmx_sum):
        i = pl.program_id(0)

        @pl.when(i == 0)
        def _():
            dws_ref[...] = jnp.zeros_like(dws_ref)
            dlng_ref[...] = jnp.zeros_like(dlng_ref)
            dlnb_ref[...] = jnp.zeros_like(dlnb_ref)
            dmx_sum[...] = jnp.zeros_like(dmx_sum)

        u, vs, gu, tu, tv, rstd, vhat, vn = _sgu_forward_replay(u_ref, vs_ref, t_ref, stat_ref, lng_ref, lnb_ref)
        da = da_ref[...].astype(F32)
        for g in range(GROUPS):
            wm = _masked_ws(ws_ref, g)
            cols = slice(g * CHUNK, (g + 1) * CHUNK)
            dws = jnp.zeros((CHUNK, CHUNK), F32)
            dsum = jnp.zeros((CHUNK, CHUNK), F32)
            for c in range(tc // CHUNK):
                rows = slice(c * CHUNK, (c + 1) * CHUNK)
                vn_cg = vn[rows, cols]
                mixed = _nn(wm, vn_cg) + bst_ref[:, g:g + 1]
                dgu_s[rows, cols] = da[rows, cols] * mixed
                dmx = da[rows, cols] * gu[rows, cols]
                dmxb = dmx.astype(BF16)
                dws = dws + _nt(dmxb, vn_cg)
                dsum = dsum + dmx
                dvn_s[rows, cols] = _tn(wm, dmxb)
            dws_ref[g] += dws
            dmx_sum[:, cols] += dsum
        dvn = dvn_s[...]
        dlng_ref[...] += _colsum(dvn * vhat)
        dlnb_ref[...] += _colsum(dvn)
        dvh = dvn * lng_ref[...]
        dgv = rstd * (dvh - jnp.mean(dvh, axis=-1, keepdims=True) - vhat * jnp.mean(dvh * vhat, axis=-1, keepdims=True))
        duv_ref[:, :D] = (dgu_s[...] * _gelu_grad(u, tu)).astype(BF16)
        duv_ref[:, D:] = (dgv * _gelu_grad(vs, tv)).astype(BF16)

        @pl.when(i == nsteps - 1)
        def _():
            row = lax.broadcasted_iota(jnp.int32, (CHUNK, CHUNK), 0)
            col = lax.broadcasted_iota(jnp.int32, (CHUNK, CHUNK), 1)
            for g in range(GROUPS):
                dws_ref[g] = jnp.where(row >= col, dws_ref[g], 0.0)
                dbs_ref[g:g + 1, :] = _colsum(dmx_sum[:, g * CHUNK:(g + 1) * CHUNK].T)

    const2 = lambda i: (0, 0)
    return pl.pallas_call(
        body, name="bwd_sgu", grid=(nsteps,),
        in_specs=[pl.BlockSpec((tc, D), lambda i: (i, 0)), pl.BlockSpec((tc, D), lambda i: (i, 1)),
                  pl.BlockSpec((tc, 2 * D), lambda i: (i, 0)), pl.BlockSpec((tc, 128), lambda i: (i, 0)),
                  pl.BlockSpec((tc, D), lambda i: (i, 0)), pl.BlockSpec((1, D), const2), pl.BlockSpec((1, D), const2),
                  pl.BlockSpec((GROUPS, CHUNK, CHUNK), lambda i: (0, 0, 0)), pl.BlockSpec((CHUNK, GROUPS), const2)],
        out_specs=[pl.BlockSpec((tc, 2 * D), lambda i: (i, 0)), pl.BlockSpec((GROUPS, CHUNK, CHUNK), lambda i: (0, 0, 0)),
                   pl.BlockSpec((GROUPS, CHUNK), const2), pl.BlockSpec((1, D), const2), pl.BlockSpec((1, D), const2)],
        out_shape=[SDS((T, 2 * D), BF16), SDS((GROUPS, CHUNK, CHUNK), F32), SDS((GROUPS, CHUNK), F32),
                   SDS((1, D), F32), SDS((1, D), F32)],
        scratch_shapes=[pltpu.VMEM((tc, D), F32), pltpu.VMEM((tc, D), F32), pltpu.VMEM((CHUNK, D), F32)],
        compiler_params=_params(1),
    )(proj, proj, tanhs, stats, da, lng, lnb, ws, bst)


IN_SEG_WIDTHS = (2 * D, D, 2 * N_KV * HEAD, 2 * D)


def _resident(shape):
    return pl.BlockSpec(shape, lambda *_: (0,) * len(shape), pipeline_mode=pl.Buffered(1))


def _bwd_in(duv, dq, dkv, dg, win_t, x, dx1, g0, after=None):
    T = x.shape[0]
    tm = min(T, 512)

    def body(duv_ref, dq_ref, dkv_ref, dg_ref, w_ref, x_ref, dx1_ref, g0_ref, gx_ref, dg0_ref):
        @pl.when(pl.program_id(0) == 0)
        def _():
            dg0_ref[...] = jnp.zeros_like(dg0_ref)

        dh, off = None, 0
        for ref, width in zip((duv_ref, dq_ref, dkv_ref, dg_ref), IN_SEG_WIDTHS):
            part = _nn(ref[...], w_ref[off:off + width, :])
            dh = part if dh is None else dh + part
            off += width
        r0, xh = _rms_stats(x_ref[...])
        dg0_ref[...] += _colsum(dh * xh)
        gx_ref[...] = dx1_ref[...] + _rms_bwd(dh, xh, r0, g0_ref[...])

    row = lambda i: (i, 0)
    body, dep_specs, deps = _after(body, 8, after)
    return pl.pallas_call(
        body, name="bwd_in", grid=(T // tm,),
        in_specs=[pl.BlockSpec((tm, w), row) for w in IN_SEG_WIDTHS] + [
            _resident((IN_W, D)), pl.BlockSpec((tm, D), row), pl.BlockSpec((tm, D), row),
            pl.BlockSpec((1, D), lambda i: (0, 0))] + dep_specs,
        out_specs=[pl.BlockSpec((tm, D), row), pl.BlockSpec((1, D), lambda i: (0, 0))],
        out_shape=[SDS((T, D), F32), SDS((1, D), F32)],
        compiler_params=_params(1),
    )(duv, dq, dkv, dg, win_t, x, dx1, g0, *deps)


def _wgrad_rows(h, segs, first_row, into, name):
    T = h.shape[0]
    tt = min(T, 2048)
    widths = [s.shape[1] for s in segs]
    rows = sum(widths)
    n_in = 1 + len(segs) + (into is not None)

    def body(*refs):
        h_ref, seg_refs = refs[0], refs[1:1 + len(segs)]
        dw_ref, acc, stage, sem = refs[n_in], refs[n_in + 1], refs[n_in + 2], refs[n_in + 3]
        t = pl.program_id(0)

        @pl.when(t == 0)
        def _():
            acc[...] = jnp.zeros_like(acc)

        off = 0
        for ref, width in zip(seg_refs, widths):
            acc[off:off + width, :] += _tn(ref[...], h_ref[...])
            off += width

        @pl.when(t == T // tt - 1)
        def _():
            stage[...] = acc[...].astype(BF16)
            out = pltpu.make_async_copy(stage, dw_ref.at[pl.ds(first_row, rows)], sem)
            out.start()
            out.wait()

    row = lambda t: (t, 0)
    return pl.pallas_call(
        body, name=name, grid=(T // tt,),
        in_specs=[pl.BlockSpec((tt, D), row)] + [pl.BlockSpec((tt, w), row) for w in widths] + [_ANY] * (into is not None),
        out_specs=_ANY,
        out_shape=SDS((IN_W, D), BF16),
        input_output_aliases={} if into is None else {n_in - 1: 0},
        scratch_shapes=[pltpu.VMEM((rows, D), F32), pltpu.VMEM((rows, D), BF16), pltpu.SemaphoreType.DMA],
        compiler_params=_params(1),
    )(h, *segs, *([] if into is None else [into]))


def _place():
    x, y, c = lax.axis_index("x"), lax.axis_index("y"), lax.axis_index("c")
    return x, y, c, 4 * x + 2 * y + c


def _peers(x, y, c):
    out = []
    for mask in range(1, N_DEV):
        px = 1 - x if mask & 4 else x
        py = 1 - y if mask & 2 else y
        pc = 1 - c if mask & 1 else c
        out.append(((px, py, pc), 4 * px + 2 * py + pc))
    return out


def _all_to_all(arrays, gather, name, after=None):
    n = len(arrays)

    def body(*refs):
        ins, outs = refs[:n], refs[n:2 * n]
        send_sems, recv_sems, local_sems = refs[2 * n:]
        x, y, c, me = _place()
        local, sends, recvs = [], [], []
        for a in range(n):
            src_own = ins[a] if gather[a] else ins[a].at[me]
            local.append(pltpu.make_async_copy(src_own, outs[a].at[me], local_sems.at[a]))
            for k, (peer, pid) in enumerate(_peers(x, y, c)):
                sem = a * (N_DEV - 1) + k
                src = ins[a] if gather[a] else ins[a].at[pid]
                sends.append(pltpu.make_async_remote_copy(
                    src_ref=src, dst_ref=outs[a].at[me], send_sem=send_sems.at[sem], recv_sem=recv_sems.at[sem],
                    device_id=peer, device_id_type=MESH))
                recvs.append(pltpu.make_async_remote_copy(
                    src_ref=src, dst_ref=outs[a].at[pid], send_sem=send_sems.at[sem], recv_sem=recv_sems.at[sem],
                    device_id=peer, device_id_type=MESH))
        for cp in local + sends:
            cp.start()
        for cp in recvs:
            cp.wait_recv()
        for cp in sends:
            cp.wait_send()
        for cp in local:
            cp.wait()

    out_shape = [SDS((N_DEV,) + a.shape if gt else a.shape, a.dtype) for a, gt in zip(arrays, gather)]
    nsem = n * (N_DEV - 1)
    body, dep_specs, deps = _after(body, n, after)
    return pl.pallas_call(
        body, name=name,
        in_specs=[pl.BlockSpec(memory_space=pl.ANY)] * n + dep_specs,
        out_specs=[pl.BlockSpec(memory_space=pl.ANY)] * n,
        out_shape=out_shape,
        scratch_shapes=[pltpu.SemaphoreType.DMA((nsem,)), pltpu.SemaphoreType.DMA((nsem,)), pltpu.SemaphoreType.DMA((n,))],
    )(*arrays, *deps)


_HBM = pl.BlockSpec(memory_space=pltpu.HBM)
_SEM = pl.BlockSpec(memory_space=pltpu.SEMAPHORE)
_EFFECT = pltpu.SideEffectType.DATAFLOW_SIDE_EFFECTING
GATHER = "gather"
SCATTER = "scatter"
SPREAD = "spread"


def _zone_shape(a, mode):
    if mode == GATHER:
        return (N_DEV,) + a.shape
    return (N_DEV - 1,) + (a.shape[1:] if mode == SCATTER else a.shape)


def _start_copies(arrays, modes, name, after=None):
    n = len(arrays)
    zones = [lax.empty(_zone_shape(a, m), a.dtype) for a, m in zip(arrays, modes)]

    def body(*refs):
        ins, lands = refs[:n], refs[n:2 * n]
        send_sems, recv_sems = refs[-2 * n - 3], refs[-2 * n - 2]
        token = refs[-1]
        x, y, c, me = _place()
        for a in range(n):
            for k, (peer, pid) in enumerate(_peers(x, y, c)):
                src = ins[a].at[pid] if modes[a] == SCATTER else ins[a]
                dst = lands[a].at[me] if modes[a] == GATHER else lands[a].at[k]
                pltpu.make_async_remote_copy(src_ref=src, dst_ref=dst, send_sem=send_sems.at[a], recv_sem=recv_sems.at[a],
                                             device_id=peer, device_id_type=MESH).start()
            if modes[a] == GATHER:
                pltpu.make_async_remote_copy(src_ref=ins[a], dst_ref=lands[a].at[me], send_sem=send_sems.at[a],
                                             recv_sem=recv_sems.at[a], device_id=(x, y, c), device_id_type=MESH).start()
        token[...] = jnp.zeros_like(token)

    hbm = lambda a: pltpu.HBM(a.shape, a.dtype)
    sems = pltpu.SemaphoreType.DMA((n,))
    extra = [] if after is None else [after]
    operands = [pltpu.with_memory_space_constraint(a, pltpu.HBM) for a in list(arrays) + zones]
    res = pl.pallas_call(
        body, name=name,
        out_shape=(sems, sems, *[hbm(a) for a in arrays], *[hbm(z) for z in zones], SDS((8, 128), F32)),
        in_specs=[_HBM] * (2 * n) + [_ANY] * len(extra),
        out_specs=(_SEM, _SEM, *[_HBM] * (2 * n), pl.BlockSpec(memory_space=pltpu.VMEM)),
        input_output_aliases={i: 2 + i for i in range(2 * n)},
        compiler_params=pltpu.CompilerParams(has_side_effects=_EFFECT),
    )(*operands, *extra)
    return res[0], res[1], list(res[2:2 + n]), list(res[2 + n:2 + 2 * n]), res[-1]


def _wait_copies(started, after, name, count=N_DEV - 1):
    send_sems, recv_sems, thru, zones, _ = started
    nt, nz = len(thru), len(zones)

    def body(*refs):
        lands = refs[nt:nt + nz]
        send_ref, recv_ref = refs[nt + nz], refs[nt + nz + 1]
        x, y, c, _ = _place()
        for a in range(nz):
            blocks = lands[a].at[pl.ds(0, count)]
            cp = pltpu.make_async_remote_copy(src_ref=blocks, dst_ref=blocks, send_sem=send_ref.at[a], recv_sem=recv_ref.at[a],
                                              device_id=(x, y, 1 - c), device_id_type=MESH)
            cp.wait_send()
            cp.wait_recv()

    hbm = lambda a: pltpu.HBM(a.shape, a.dtype)
    res = pl.pallas_call(
        body, name=name,
        out_shape=tuple(hbm(a) for a in thru + zones),
        in_specs=[_HBM] * (nt + nz) + [_SEM, _SEM, _ANY],
        out_specs=tuple([_HBM] * (nt + nz)),
        input_output_aliases={i: i for i in range(nt + nz)},
        compiler_params=pltpu.CompilerParams(has_side_effects=_EFFECT),
    )(*thru, *zones, send_sems, recv_sems, after)
    return list(res[:nt]), list(res[nt:])


def _split_start(body, arrays, zones, name, after):
    n = len(arrays) + len(zones)
    hbm = lambda a: pltpu.HBM(a.shape, a.dtype)
    sems = pltpu.SemaphoreType.DMA((max(len(zones), 1),))
    extra = [] if after is None else [after]
    operands = [pltpu.with_memory_space_constraint(a, pltpu.HBM) for a in list(arrays) + list(zones)]
    res = pl.pallas_call(
        body, name=name,
        out_shape=(sems, sems, *[hbm(a) for a in operands], SDS((8, 128), F32)),
        in_specs=[_HBM] * n + [_ANY] * len(extra),
        out_specs=(_SEM, _SEM, *[_HBM] * n, pl.BlockSpec(memory_space=pltpu.VMEM)),
        input_output_aliases={i: 2 + i for i in range(n)},
        compiler_params=pltpu.CompilerParams(has_side_effects=_EFFECT),
    )(*operands, *extra)
    return res[0], res[1], list(res[2:2 + len(arrays)]), list(res[2 + len(arrays):2 + n]), res[-1]


def _gather_first_leg(shard, name, after=None):
    zone = lax.empty((N_DEV,) + shard.shape, shard.dtype)
    extra = 0 if after is None else 1

    def body(*refs):
        src, land = refs[0], refs[1]
        send_sem, recv_sem, token = refs[2 + extra], refs[3 + extra], refs[-1]
        x, y, c, me = _place()
        for peer in ((x, y, c), (x, y, 1 - c), (1 - x, y, c), (x, 1 - y, c), (1 - x, 1 - y, c)):
            pltpu.make_async_remote_copy(src_ref=src, dst_ref=land.at[me], send_sem=send_sem.at[0], recv_sem=recv_sem.at[0],
                                         device_id=peer, device_id_type=MESH).start()
        token[...] = jnp.zeros_like(token)

    return _split_start(body, [shard], [zone], name, after)


def _gather_second_leg(zone, name, after=None):
    extra = 0 if after is None else 1

    def body(*refs):
        land = refs[0]
        send_sem, recv_sem, token = refs[1 + extra], refs[2 + extra], refs[-1]
        x, y, c, _ = _place()
        for px, py in ((1 - x, y), (x, 1 - y), (1 - x, 1 - y)):
            slot = 4 * px + 2 * py + c
            pltpu.make_async_remote_copy(src_ref=land.at[slot], dst_ref=land.at[slot], send_sem=send_sem.at[0],
                                         recv_sem=recv_sem.at[0], device_id=(x, y, 1 - c), device_id_type=MESH).start()
        token[...] = jnp.zeros_like(token)

    return _split_start(body, [], [zone], name, after)


UPDATE_BLOCK_ELEMS = 512 * 1024


def _update_rows(R, C):
    fits = [t for t in range(8, R + 1, 8) if R % t == 0 and t * C <= UPDATE_BLOCK_ELEMS]
    whole = [t for t in fits if t % 16 == 0]
    return max(whole or fits)


def _adamw_math(g, w, m, v):
    m2 = ADAM_B1 * m + (1.0 - ADAM_B1) * g
    v2 = ADAM_B2 * v + (1.0 - ADAM_B2) * (g * g)
    m_hat = m2 / (1.0 - ADAM_B1 ** ADAM_STEP)
    v_hat = v2 / (1.0 - ADAM_B2 ** ADAM_STEP)
    delta = -ADAM_LR * (m_hat / (jnp.sqrt(v_hat) + ADAM_EPS) + ADAM_WD * w)
    return delta, m2, v2


def _sum_adamw(parts, w, m, v, name):
    R, C = w.shape
    tr = _update_rows(R, C)

    def body(p_ref, w_ref, m_ref, v_ref, g_ref, d_ref, m2_ref, v2_ref):
        g = p_ref[0]
        for k in range(1, N_DEV):
            g = g + p_ref[k]
        g_ref[...] = g
        d_ref[...], m2_ref[...], v2_ref[...] = _adamw_math(g, w_ref[...], m_ref[...], v_ref[...])

    blk = pl.BlockSpec((tr, C), lambda i: (i, 0))
    return pl.pallas_call(
        body, name=name, grid=(R // tr,),
        in_specs=[pl.BlockSpec((N_DEV, tr, C), lambda i: (0, i, 0)), blk, blk, blk],
        out_specs=[blk] * 4,
        out_shape=[SDS((R, C), F32)] * 4,
        compiler_params=_params(1),
    )(parts, w, m, v)


def _sum_adamw_peers(me, own, parts, w, m, v, name, replicated, also_rows=None):
    R, C = w.shape
    tr = _update_rows(R, C)
    assert also_rows is None or tr == R

    def body(me_ref, own_ref, p_ref, w_ref, m_ref, v_ref, g_ref, d_ref, m2_ref, v2_ref, *extra):
        if replicated:
            mine = me_ref[0]
            g = None
            for j in range(N_DEV):
                k = jnp.maximum(jnp.bitwise_xor(mine, j) - 1, 0)
                term = jnp.where(mine == j, own_ref[...], p_ref[k])
                g = term if g is None else g + term
        else:
            g = own_ref[...].astype(F32)
            for k in range(N_DEV - 1):
                g = g + p_ref[k].astype(F32)
        results = (g,) + _adamw_math(g, w_ref[...], m_ref[...], v_ref[...])
        for ref, val in zip((g_ref, d_ref, m2_ref, v2_ref), results):
            ref[...] = val
        for ref, val in zip(extra, results):
            ref[...] = val[also_rows[0]:also_rows[1]]

    blk = pl.BlockSpec((tr, C), lambda i, me_ref: (i, 0))
    own_spec = blk if replicated else pl.BlockSpec((None, tr, C), lambda i, me_ref: (me_ref[0], i, 0))
    n_also = 0 if also_rows is None else also_rows[1] - also_rows[0]
    also_specs = [pl.BlockSpec((n_also, C), lambda i, me_ref: (0, 0))] * (4 if also_rows else 0)
    return pl.pallas_call(
        body, name=name,
        grid_spec=pltpu.PrefetchScalarGridSpec(
            num_scalar_prefetch=1, grid=(R // tr,),
            in_specs=[own_spec, pl.BlockSpec((N_DEV - 1, tr, C), lambda i, me_ref: (0, i, 0)), blk, blk, blk],
            out_specs=[blk] * 4 + also_specs),
        out_shape=[SDS((R, C), F32)] * 4 + [SDS((n_also, C), F32)] * len(also_specs),
        compiler_params=_params(1),
    )(me, own, parts, w, m, v)


SMALL = ("ln_v_gain", "ln_v_bias", "w_spatial", "b_spatial", "sinks", "norm_mix_post", "norm_ff_pre", "norm_ff_post")
SMALL_ROWS = {"ln_v_gain": 8, "ln_v_bias": 8, "w_spatial": 1024, "b_spatial": 8, "sinks": 8,
              "norm_mix_post": 8, "norm_ff_pre": 8, "norm_ff_post": 8}
SMALL_PACK_ROWS = 1152


def _pack_small(vals):
    rows = []
    for name in SMALL:
        flat = vals[name].reshape(-1)
        pad = SMALL_ROWS[name] * 128 - flat.shape[0]
        if pad:
            flat = jnp.concatenate([flat, jnp.zeros((pad,), F32)])
        rows.append(flat.reshape(SMALL_ROWS[name], 128))
    rows.append(jnp.zeros((SMALL_PACK_ROWS - sum(SMALL_ROWS.values()), 128), F32))
    return jnp.concatenate(rows, axis=0)


def _unpack_small(packed, shapes):
    out, r = {}, 0
    for name in SMALL:
        n = 1
        for s in shapes[name]:
            n *= s
        out[name] = packed[r:r + SMALL_ROWS[name]].reshape(-1)[:n].reshape(shapes[name])
        r += SMALL_ROWS[name]
    return out


def _rope_rows():
    d = jnp.arange(128) % HEAD
    inv = ROPE_THETA ** (-(2.0 * (d % (ROPE // 2))).astype(F32) / ROPE)
    invf = jnp.where(d < ROPE, inv, 0.0).astype(F32).reshape(1, 128)
    sgn = jnp.where(d < ROPE // 2, -1.0, jnp.where(d < ROPE, 1.0, 0.0)).astype(F32).reshape(1, 128)
    return invf, sgn


def kernel(x, positions, w_in, ln_v_gain, ln_v_bias, w_spatial, b_spatial, sinks, w_a, w_b, w_o, norm_mix_pre, norm_mix_post, w_ff_in, w_ff_out, norm_ff_pre, norm_ff_post, loss_target, m_w_in, m_ln_v_gain, m_ln_v_bias, m_w_spatial, m_b_spatial, m_sinks, m_w_a, m_w_b, m_w_o, m_norm_mix_pre, m_norm_mix_post, m_w_ff_in, m_w_ff_out, m_norm_ff_pre, m_norm_ff_post, v_w_in, v_ln_v_gain, v_ln_v_bias, v_w_spatial, v_b_spatial, v_sinks, v_w_a, v_w_b, v_w_o, v_norm_mix_pre, v_norm_mix_post, v_w_ff_in, v_w_ff_out, v_norm_ff_pre, v_norm_ff_post):
    given = dict(locals())
    T = x.shape[1]
    xt = x[0]
    tgt = loss_target[0]
    bst = b_spatial[0].T
    ws = w_spatial[0]

    me = 4 * lax.axis_index("x") + 2 * lax.axis_index("y") + lax.axis_index("c")
    me_arr = me.astype(jnp.int32).reshape(1)

    rest = ("w_a", "w_b", "w_o", "w_ff_in", "w_ff_out")
    shard = {n: given[n][0].astype(BF16) for n in rest}
    g_one = _gather_first_leg(w_in[0].T.astype(BF16), "gather_in_start")
    cos, sin = _rope_tables(positions.astype(F32).reshape(T, 1), *_rope_rows(), after=g_one[-1])
    small_state = [_pack_small({n: given[k + n] for n in SMALL}) for k in ("", "m_", "v_")]
    h = _rms_pre(xt, norm_mix_pre, after=[cos, *small_state, *[shard[n] for n in rest]])
    _, (win8,) = _wait_copies(g_one, h, "gather_in_wait", count=5)
    g_two = _gather_second_leg(win8, "gather_in_pass_start")
    g_rest = _start_copies([shard[n] for n in rest], [GATHER] * len(rest), "gather_rest_start", after=g_two[-1])
    _, (win8,) = _wait_copies(g_two, g_rest[-1], "gather_in_pass_wait", count=3)
    win = win8.reshape(IN_W, D)

    proj = _fwd_in(h, win)
    att, qr, kr, probs, psink = _fwd_attn(proj, cos, sin, sinks[0])
    a, tanhs, ln_stats = _fwd_sgu(proj, ln_v_gain, ln_v_bias, ws, bst, after=att)
    gw = dict(zip(rest, _wait_copies(g_rest, a, "gather_rest_wait", count=N_DEV)[1]))
    wa, wb, wo = (gw[n].reshape(D, D) for n in ("w_a", "w_b", "w_o"))
    wfi3 = gw["w_ff_in"]
    wfo = gw["w_ff_out"].reshape(D_FF, D)
    merged, a2, b2, mix, x1, hf = _fwd_mix(a, att, proj, xt, wa, wb, wo, norm_mix_post, norm_ff_pre)
    f, dy, dff, dg3, loss_part = _fwd_ff(hf, wfi3, wfo, x1, tgt, norm_ff_post)

    df, dx1, dmix, dg2, dg1 = _bwd_ff(dff, f, wfi3, wfo, x1, dy, mix, norm_mix_post, norm_ff_pre)
    dwfi3, dwfo = _wgrad_ff(hf, df, f, dff)
    own_ff = [dwfi3, dwfo.reshape(N_DEV, D_FF // N_DEV, D)]
    x_ff = _start_copies(own_ff, [SCATTER] * 2, "exchange_ff_start")
    dgate, da, datt, dwo, dwa, dwb = _bwd_mix(dmix, proj, a2, b2, merged, a, att, wo, wa, wb, after=x_ff[-1])
    own_mix = [g.reshape(N_DEV, D // N_DEV, D) for g in (dwa, dwb, dwo)]
    x_mix = _start_copies(own_mix, [SCATTER] * 3, "exchange_mix_start")
    dq, dkv, dsink = _bwd_attn(qr, kr, probs, psink, proj, cos, sin, datt, after=x_mix[-1])
    duv, dws, dbs, dlng, dlnb = _bwd_sgu(proj, tanhs, ln_stats, da, ln_v_gain, ln_v_bias, ws, bst)
    small_grads = {"ln_v_gain": dlng, "ln_v_bias": dlnb, "w_spatial": dws, "b_spatial": dbs, "sinks": jnp.sum(dsink, axis=1),
                   "norm_mix_post": dg1, "norm_ff_pre": dg2, "norm_ff_post": dg3}
    x_small = _start_copies([_pack_small(small_grads)], [SPREAD], "exchange_small_start")
    dwin = _wgrad_rows(h, [dgate], sum(IN_SEG_WIDTHS[:3]), None, "wgrad_in_gates")
    dwin = _wgrad_rows(h, [duv], 0, dwin, "wgrad_in_uv")
    dwin = _wgrad_rows(h, [dq, dkv], IN_SEG_WIDTHS[0], dwin, "wgrad_in_qkv")
    own_in = [dwin.reshape(N_DEV, IN_W // N_DEV, D)]
    x_in = _start_copies(own_in, [SCATTER], "exchange_in_start", after=x_small[-1])
    grad_x, dg0 = _bwd_in(duv, dq, dkv, dgate, win, xt, dx1, norm_mix_pre, after=x_in[-1])

    results = {}

    def update(n, own, parts, transposed=False):
        state = [given[k + n][0].T if transposed else given[k + n][0] for k in ("", "m_", "v_")]
        res = _sum_adamw_peers(me_arr, own, parts, *state, "adamw_" + n, False)
        results[n] = [(r.T if transposed else r).reshape(given[n].shape) for r in res]

    own_ff, p_ff = _wait_copies(x_ff, grad_x, "exchange_ff_wait")
    update("w_ff_in", own_ff[0], p_ff[0])
    update("w_ff_out", own_ff[1], p_ff[1])
    own_mix, p_mix = _wait_copies(x_mix, results["w_ff_out"][0], "exchange_mix_wait")
    for n, own, parts in zip(("w_a", "w_b", "w_o"), own_mix, p_mix):
        update(n, own, parts)
    tail = jnp.concatenate([dg0.reshape(8, 128), jnp.tile(loss_part, (8, 1))], axis=0)
    (tail_all,) = _all_to_all([tail], [True], "exchange_tail", after=results["w_o"][0])
    dg0_all = tail_all[:, :8]
    own_small, p_small = _wait_copies(x_small, tail_all, "exchange_small_wait")
    own_in, p_in = _wait_copies(x_in, p_small[0], "exchange_in_wait")
    update("w_in", own_in[0], p_in[0], transposed=True)
    first = sum(SMALL_ROWS[n] for n in SMALL[:SMALL.index("w_spatial")])
    packed = _sum_adamw_peers(me_arr, own_small[0], p_small[0], *small_state, "adamw_small", True,
                              also_rows=(first, first + SMALL_ROWS["w_spatial"]))
    shapes = {n: given[n].shape for n in SMALL}
    unpacked = [_unpack_small(p, shapes) for p in packed[:4]]
    for n in SMALL:
        results[n] = [u[n] for u in unpacked]
    results["w_spatial"] = [r.reshape(w_spatial.shape) for r in packed[4:]]
    n = "norm_mix_pre"
    results[n] = [r.reshape(given[n].shape) for r in _sum_adamw(
        dg0_all, given[n].reshape(8, 128), given["m_" + n].reshape(8, 128), given["v_" + n].reshape(8, 128), "adamw_" + n)]

    loss = jnp.sum(tail_all[:, 8, 0])
    order = ("w_in", "ln_v_gain", "ln_v_bias", "w_spatial", "b_spatial", "sinks", "w_a", "w_b", "w_o", "norm_mix_pre",
             "norm_mix_post", "w_ff_in", "w_ff_out", "norm_ff_pre", "norm_ff_post")
    out = [loss, grad_x.reshape(x.shape)]
    for k in range(4):
        out += [results[n][k] for n in order]
    return tuple(out)
```

```python
import jax
import jax.numpy as jnp
from jax import lax
from jax.experimental import pallas as pl
from jax.experimental.pallas import tpu as pltpu

F32 = jnp.float32
BF16 = jnp.bfloat16

N_DEV = 8
D = 1024
D_FF = 4096
IN_W = 5632
CHUNK = 128
GROUPS = 8
HEAD = 64
N_Q = 16
N_KV = 4
ROPE = 16
ROPE_THETA = 500000.0
EPS = 1e-6
OFF_Q, OFF_K, OFF_VA, OFF_GA, OFF_GB = 2048, 3072, 3328, 3584, 4608

ADAM_LR = 0.001
ADAM_B1 = 0.9
ADAM_B2 = 0.999
ADAM_EPS = 1e-08
ADAM_WD = 0.01
ADAM_STEP = 10

VMEM_LIMIT = 62 * 1024 * 1024

SDS = jax.ShapeDtypeStruct
MESH = pl.DeviceIdType.MESH


def _params(n_axes):
    return pltpu.CompilerParams(dimension_semantics=("arbitrary",) * n_axes, vmem_limit_bytes=VMEM_LIMIT)


def _nt(a, b):
    return lax.dot_general(a, b, (((1,), (1,)), ((), ())), preferred_element_type=F32)


def _tn(a, b):
    return lax.dot_general(a, b, (((0,), (0,)), ((), ())), preferred_element_type=F32)


def _nn(a, b):
    return jnp.dot(a, b, preferred_element_type=F32)


def _gelu(x):
    t = jnp.tanh(0.7978845608028654 * (x + 0.044715 * (x * x * x)))
    return 0.5 * x * (1.0 + t), t


def _gelu_grad(x, t):
    return 0.5 * (1.0 + t) + 0.5 * x * (1.0 - t * t) * (0.7978845608028654 * (1.0 + 3.0 * 0.044715 * x * x))


def _sigmoid(x):
    return 1.0 / (1.0 + jnp.exp(-x))


def _rms_stats(v):
    r = lax.rsqrt(jnp.mean(v * v, axis=-1, keepdims=True) + EPS)
    return r, v * r


def _rms_bwd(d, vhat, r, g):
    gd = g * d
    return r * (gd - vhat * jnp.mean(gd * vhat, axis=-1, keepdims=True))


def _colsum(v):
    return jnp.sum(v, axis=0, keepdims=True)


_ANY = pl.BlockSpec(memory_space=pl.ANY)


def _after(body, n_in, after):
    if after is None:
        return body, [], []
    deps = list(after) if isinstance(after, (list, tuple)) else [after]

    def ordered(*refs):
        return body(*refs[:n_in], *refs[n_in + len(deps):])

    return ordered, [_ANY] * len(deps), deps


def _rms_pre(x, g0, after=None):
    T = x.shape[0]
    tm = min(T, 1024)

    def body(x_ref, g_ref, h_ref):
        _, xh = _rms_stats(x_ref[...])
        h_ref[...] = (xh * g_ref[...]).astype(BF16)

    body, dep_specs, deps = _after(body, 2, after)
    return pl.pallas_call(
        body, name="rms_pre", grid=(T // tm,),
        in_specs=[pl.BlockSpec((tm, D), lambda i: (i, 0)), pl.BlockSpec((1, D), lambda i: (0, 0))] + dep_specs,
        out_specs=pl.BlockSpec((tm, D), lambda i: (i, 0)),
        out_shape=SDS((T, D), BF16),
        compiler_params=_params(1),
    )(x, g0, *deps)


def _fwd_in(h, win_t):
    T = h.shape[0]
    tm, tn = min(T, 1024), 1408

    def body(h_ref, w_ref, p_ref):
        for j in range(IN_W // tn):
            cols = slice(j * tn, (j + 1) * tn)
            p_ref[:, cols] = _nt(h_ref[...], w_ref[cols, :]).astype(BF16)

    return pl.pallas_call(
        body, name="fwd_in", grid=(T // tm,),
        in_specs=[pl.BlockSpec((tm, D), lambda i: (i, 0)), _resident((IN_W, D))],
        out_specs=pl.BlockSpec((tm, IN_W), lambda i: (i, 0)),
        out_shape=SDS((T, IN_W), BF16),
        compiler_params=_params(1),
    )(h, win_t)


def _sgu_forward_parts(u_ref, vs_ref, lng_ref, lnb_ref):
    u = u_ref[...].astype(F32)
    vs = vs_ref[...].astype(F32)
    gu, tu = _gelu(u)
    gv, tv = _gelu(vs)
    mu = jnp.mean(gv, axis=-1, keepdims=True)
    dv = gv - mu
    rstd = lax.rsqrt(jnp.mean(dv * dv, axis=-1, keepdims=True) + EPS)
    vhat = dv * rstd
    vn = (vhat * lng_ref[...] + lnb_ref[...]).astype(BF16)
    return gu, tu, tv, mu, rstd, vn


def _sgu_forward_replay(u_ref, vs_ref, t_ref, stat_ref, lng_ref, lnb_ref):
    u = u_ref[...].astype(F32)
    vs = vs_ref[...].astype(F32)
    tu = t_ref[:, :D].astype(F32)
    tv = t_ref[:, D:].astype(F32)
    gu = 0.5 * u * (1.0 + tu)
    rstd = stat_ref[:, 1:2]
    vhat = (0.5 * vs * (1.0 + tv) - stat_ref[:, 0:1]) * rstd
    vn = (vhat * lng_ref[...] + lnb_ref[...]).astype(BF16)
    return u, vs, gu, tu, tv, rstd, vhat, vn


def _masked_ws(ws_ref, g):
    row = lax.broadcasted_iota(jnp.int32, (CHUNK, CHUNK), 0)
    col = lax.broadcasted_iota(jnp.int32, (CHUNK, CHUNK), 1)
    return jnp.where(row >= col, ws_ref[g], 0.0).astype(BF16)


def _fwd_sgu(proj, lng, lnb, ws, bst, after=None):
    T = proj.shape[0]
    tc = min(T, 512)

    def body(u_ref, vs_ref, lng_ref, lnb_ref, ws_ref, bst_ref, a_ref, t_ref, stat_ref):
        gu, tu, tv, mu, rstd, vn = _sgu_forward_parts(u_ref, vs_ref, lng_ref, lnb_ref)
        t_ref[:, :D] = tu.astype(BF16)
        t_ref[:, D:] = tv.astype(BF16)
        lane = lax.broadcasted_iota(jnp.int32, (tc, 128), 1)
        stat_ref[...] = jnp.where(lane == 0, mu, jnp.where(lane == 1, rstd, 0.0))
        for g in range(GROUPS):
            wm = _masked_ws(ws_ref, g)
            cols = slice(g * CHUNK, (g + 1) * CHUNK)
            for c in range(tc // CHUNK):
                rows = slice(c * CHUNK, (c + 1) * CHUNK)
                mixed = _nn(wm, vn[rows, cols]) + bst_ref[:, g:g + 1]
                a_ref[rows, cols] = (gu[rows, cols] * mixed).astype(BF16)

    body, dep_specs, deps = _after(body, 6, after)
    return pl.pallas_call(
        body, name="fwd_sgu", grid=(T // tc,),
        in_specs=[pl.BlockSpec((tc, D), lambda i: (i, 0)), pl.BlockSpec((tc, D), lambda i: (i, 1)),
                  pl.BlockSpec((1, D), lambda i: (0, 0)), pl.BlockSpec((1, D), lambda i: (0, 0)),
                  pl.BlockSpec((GROUPS, CHUNK, CHUNK), lambda i: (0, 0, 0)),
                  pl.BlockSpec((CHUNK, GROUPS), lambda i: (0, 0))] + dep_specs,
        out_specs=[pl.BlockSpec((tc, D), lambda i: (i, 0)), pl.BlockSpec((tc, 2 * D), lambda i: (i, 0)),
                   pl.BlockSpec((tc, 128), lambda i: (i, 0))],
        out_shape=[SDS((T, D), BF16), SDS((T, 2 * D), BF16), SDS((T, 128), F32)],
        compiler_params=_params(1),
    )(proj, proj, lng, lnb, ws, bst, *deps)


def _rope_tables(posf, invf, sgn, after=None):
    T = posf.shape[0]
    tr = min(T, 1024)

    def body(pos_ref, invf_ref, sgn_ref, c_ref, s_ref):
        ang = pos_ref[...] * invf_ref[...]
        c_ref[...] = jnp.cos(ang)
        s = jnp.sin(ang)
        s_ref[:, :128] = jnp.where(sgn_ref[...] < 0.0, -s, 0.0)
        s_ref[:, 128:] = jnp.where(sgn_ref[...] > 0.0, s, 0.0)

    body, dep_specs, deps = _after(body, 3, after)
    return pl.pallas_call(
        body, name="rope_tables", grid=(T // tr,),
        in_specs=[pl.BlockSpec((tr, 1), lambda i: (i, 0)), pl.BlockSpec((1, 128), lambda i: (0, 0)),
                  pl.BlockSpec((1, 128), lambda i: (0, 0))] + dep_specs,
        out_specs=[pl.BlockSpec((tr, 128), lambda i: (i, 0)), pl.BlockSpec((tr, 256), lambda i: (i, 0))],
        out_shape=[SDS((T, 128), F32), SDS((T, 256), F32)],
        compiler_params=_params(1),
    )(posf, invf, sgn, *deps)


def _rope(v, c, s):
    v = v.astype(F32)
    return v * c + pltpu.roll(v, 128 - ROPE // 2, 1) * s[:, :128] + pltpu.roll(v, ROPE // 2, 1) * s[:, 128:]


def _rope_bwd(dv, c, s):
    return dv * c + pltpu.roll(dv * s[:, :128], ROPE // 2, 1) + pltpu.roll(dv * s[:, 128:], 128 - ROPE // 2, 1)


def _fold_masks(first):
    jj = lax.broadcasted_iota(jnp.int32, (CHUNK, CHUNK), 0)
    t = lax.broadcasted_iota(jnp.int32, (CHUNK, CHUNK), 1)
    prev = jj > t
    return prev, jnp.where(prev & first, -1e30, 0.0)


def _fold(band, prev):
    return jnp.where(prev, band[:CHUNK], band[CHUNK:])


def _unfold(folded, prev):
    return jnp.concatenate([jnp.where(prev, folded, 0.0), jnp.where(prev, 0.0, folded)], axis=0)


def _softmax_sink(s, sink, key_axis):
    m = jnp.maximum(jnp.max(s, axis=key_axis, keepdims=True), sink)
    p = jnp.exp(s - m)
    esink = jnp.exp(sink - m)
    inv = 1.0 / (jnp.sum(p, axis=key_axis, keepdims=True) + esink)
    return p * inv, esink * inv


def _head_pair_operand(slab, g):
    lo = lax.broadcasted_iota(jnp.int32, slab.shape, 1) < HEAD
    if g % 2 == 0:
        first = jnp.where(lo, slab, 0.0)
        second = pltpu.roll(first, HEAD, 1)
    else:
        second = jnp.where(lo, 0.0, slab)
        first = pltpu.roll(second, HEAD, 1)
    return jnp.concatenate([first, second], axis=0).astype(BF16)


def _head_pair_gradient(acc, g):
    top, bot = acc[:2 * CHUNK], acc[2 * CHUNK:]
    lo = lax.broadcasted_iota(jnp.int32, top.shape, 1) < HEAD
    if g % 2 == 0:
        return jnp.where(lo, top, 0.0) + pltpu.roll(jnp.where(lo, 0.0, bot), HEAD, 1)
    return pltpu.roll(jnp.where(lo, top, 0.0), HEAD, 1) + jnp.where(lo, 0.0, bot)


PAIRS_PER_KV = N_Q // N_KV // 2
KV_W = N_KV * HEAD


def _band(prev_ref, cur_ref, cols=slice(None)):
    return jnp.concatenate([prev_ref[:, cols], cur_ref[:, cols]], axis=0)


def _fwd_attn(proj, cos, sin, sinks):
    T = proj.shape[0]
    nb = T // CHUNK
    cur = lambda i: i
    prev = lambda i: jnp.maximum(i - 1, 0)

    def body(q_ref, kp_ref, kc_ref, vp_ref, vc_ref, cp_ref, cc_ref, sp_ref, sc_ref, sink_ref,
             o_ref, qr_ref, kr_ref, p_ref, psink_ref):
        prev_slot, bias = _fold_masks(pl.program_id(0) == 0)
        c_band, s_band = _band(cp_ref, cc_ref), _band(sp_ref, sc_ref)
        for j in range(KV_W // 128):
            cols = slice(j * 128, (j + 1) * 128)
            k_slab = _rope(_band(kp_ref, kc_ref, cols), c_band, s_band)
            kr_ref[:, cols] = k_slab[CHUNK:].astype(BF16)
            v_slab = _band(vp_ref, vc_ref, cols).astype(F32)
            for g in (2 * j, 2 * j + 1):
                k2 = _head_pair_operand(k_slab, g)
                v2 = _head_pair_operand(v_slab, g)
                pairs = [g * PAIRS_PER_KV + r for r in range(PAIRS_PER_KV)]
                qps = []
                for pair in pairs:
                    lanes = slice(pair * 128, (pair + 1) * 128)
                    qps.append((_rope(q_ref[:, lanes], cc_ref[...], sc_ref[...]) * (HEAD ** -0.5)).astype(BF16))
                    qr_ref[:, lanes] = qps[-1]
                s2 = _nt(k2, jnp.concatenate(qps, axis=0))
                pcols = []
                for r, pair in enumerate(pairs):
                    ps = []
                    for e in range(2):
                        head = 2 * pair + e
                        s = _fold(s2[e * 2 * CHUNK:(e + 1) * 2 * CHUNK, r * 128:(r + 1) * 128], prev_slot) + bias
                        p, psink = _softmax_sink(s, sink_ref[head], 0)
                        p = p.astype(BF16)
                        p_ref[head] = p
                        psink_ref[head:head + 1, :] = psink
                        ps.append(_unfold(p, prev_slot))
                    pcols.append(jnp.concatenate(ps, axis=0))
                o = _tn(jnp.concatenate(pcols, axis=1), v2).astype(BF16)
                for r, pair in enumerate(pairs):
                    o_ref[:, pair * 128:(pair + 1) * 128] = o[r * CHUNK:(r + 1) * CHUNK]

    table = lambda which, width: pl.BlockSpec((CHUNK, width), lambda i: (which(i), 0))
    return pl.pallas_call(
        body, name="fwd_attn", grid=(nb,),
        in_specs=[pl.BlockSpec((CHUNK, D), lambda i: (i, OFF_Q // D)),
                  pl.BlockSpec((CHUNK, KV_W), lambda i: (prev(i), OFF_K // KV_W)),
                  pl.BlockSpec((CHUNK, KV_W), lambda i: (i, OFF_K // KV_W)),
                  pl.BlockSpec((CHUNK, KV_W), lambda i: (prev(i), OFF_VA // KV_W)),
                  pl.BlockSpec((CHUNK, KV_W), lambda i: (i, OFF_VA // KV_W)),
                  table(prev, 128), table(cur, 128), table(prev, 256), table(cur, 256),
                  pl.BlockSpec(memory_space=pltpu.SMEM)],
        out_specs=[pl.BlockSpec((CHUNK, D), lambda i: (i, 0)), pl.BlockSpec((CHUNK, D), lambda i: (i, 0)),
                   pl.BlockSpec((CHUNK, KV_W), lambda i: (i, 0)),
                   pl.BlockSpec((None, N_Q, CHUNK, CHUNK), lambda i: (i, 0, 0, 0)),
                   pl.BlockSpec((None, N_Q, CHUNK), lambda i: (i, 0, 0))],
        out_shape=[SDS((T, D), BF16), SDS((T, D), BF16), SDS((T, KV_W), BF16),
                   SDS((nb, N_Q, CHUNK, CHUNK), BF16), SDS((nb, N_Q, CHUNK), F32)],
        compiler_params=_params(1),
    )(proj, proj, proj, proj, proj, cos, cos, sin, sin, sinks)


def _fwd_mix(a, att, proj, x, wa, wb, wo, g1, g2):
    T = x.shape[0]
    tm = min(T, 512)
    half = D // 2

    def body(a_ref, att_ref, ga0, ga1, gb0, gb1, x_ref, wa_ref, wb_ref, wo_ref, g1_ref, g2_ref,
             mg_ref, a2_ref, b2_ref, mix_ref, x1_ref, hf_ref):
        a2 = _nn(a_ref[...], wa_ref[...])
        b2 = _nn(att_ref[...], wb_ref[...])
        ga = jnp.concatenate([ga0[...], ga1[...]], axis=1).astype(F32)
        gb = jnp.concatenate([gb0[...], gb1[...]], axis=1).astype(F32)
        merged = (_sigmoid(ga) * a2 + _sigmoid(gb) * b2).astype(BF16)
        a2_ref[...] = a2.astype(BF16)
        b2_ref[...] = b2.astype(BF16)
        mg_ref[...] = merged
        mix = _nn(merged, wo_ref[...])
        mix_ref[...] = mix
        _, mh = _rms_stats(mix)
        x1 = x_ref[...] + mh * g1_ref[...]
        x1_ref[...] = x1
        _, xh = _rms_stats(x1)
        hf_ref[...] = (xh * g2_ref[...]).astype(BF16)

    row = lambda i: (i, 0)
    const = lambda i: (0, 0)
    gspec = lambda off: pl.BlockSpec((tm, half), lambda i: (i, off // half))
    return pl.pallas_call(
        body, name="fwd_mix", grid=(T // tm,),
        in_specs=[pl.BlockSpec((tm, D), row), pl.BlockSpec((tm, D), row),
                  gspec(OFF_GA), gspec(OFF_GA + half), gspec(OFF_GB), gspec(OFF_GB + half),
                  pl.BlockSpec((tm, D), row), _resident((D, D)), _resident((D, D)),
                  _resident((D, D)), pl.BlockSpec((1, D), const), pl.BlockSpec((1, D), const)],
        out_specs=[pl.BlockSpec((tm, D), row)] * 6,
        out_shape=[SDS((T, D), BF16), SDS((T, D), BF16), SDS((T, D), BF16), SDS((T, D), F32), SDS((T, D), F32),
                   SDS((T, D), BF16)],
        compiler_params=_params(1),
    )(a, att, proj, proj, proj, proj, x, wa, wb, wo, g1, g2)


FF_SPLIT = N_DEV
FF_TILE = D_FF // FF_SPLIT


def _fwd_ff(hf, wfi3, wfo, x1, tgt, g3):
    T = hf.shape[0]
    tm = min(T, 512)

    def body(hf_ref, wfi_ref, wfo_ref, x1_ref, tgt_ref, g3_ref, f_ref, dy_ref, dff_ref, dg3_ref, loss_ref, r_s):
        @pl.when(pl.program_id(0) == 0)
        def _():
            dg3_ref[...] = jnp.zeros_like(dg3_ref)
            loss_ref[...] = jnp.zeros_like(loss_ref)

        hf_t = hf_ref[...]
        for s in range(FF_SPLIT):
            cols = slice(s * FF_TILE, (s + 1) * FF_TILE)
            f = _nn(hf_t, wfi_ref[s]).astype(BF16)
            f_ref[:, cols] = f
            rl = jnp.maximum(f.astype(F32), 0.0)
            r_s[:, cols] = (rl * rl).astype(BF16)
        r3, fh = _rms_stats(_nn(r_s[...], wfo_ref[...]))
        e = x1_ref[...] + fh * g3_ref[...] - tgt_ref[...]
        loss_ref[...] += jnp.sum(e * e) * (0.5 / D)
        dy = e * (1.0 / D)
        dy_ref[...] = dy
        dg3_ref[...] += _colsum(dy * fh)
        dff_ref[...] = _rms_bwd(dy, fh, r3, g3_ref[...]).astype(BF16)

    row = lambda i: (i, 0)
    const = lambda i: (0, 0)
    return pl.pallas_call(
        body, name="fwd_ff", grid=(T // tm,),
        in_specs=[pl.BlockSpec((tm, D), row), _resident((FF_SPLIT, D, FF_TILE)), _resident((D_FF, D)),
                  pl.BlockSpec((tm, D), row),
                  pl.BlockSpec((tm, D), row), pl.BlockSpec((1, D), const)],
        out_specs=[pl.BlockSpec((tm, D_FF), row), pl.BlockSpec((tm, D), row),
                   pl.BlockSpec((tm, D), row), pl.BlockSpec((1, D), const), pl.BlockSpec((1, 128), const)],
        out_shape=[SDS((T, D_FF), BF16), SDS((T, D), F32), SDS((T, D), BF16), SDS((1, D), F32), SDS((1, 128), F32)],
        scratch_shapes=[pltpu.VMEM((tm, D_FF), BF16)],
        compiler_params=_params(1),
    )(hf, wfi3, wfo, x1, tgt, g3)


def _bwd_ff(dff, f, wfi3, wfo, x1, dy, mix, g1, g2):
    T = dff.shape[0]
    tm = min(T, 512)

    def body(dff_ref, f_ref, wfi_ref, wfo_ref, x1_ref, dy_ref, mix_ref, g1_ref, g2_ref,
             df_ref, dx1_ref, dmix_ref, dg2_ref, dg1_ref):
        @pl.when(pl.program_id(0) == 0)
        def _():
            dg2_ref[...] = jnp.zeros_like(dg2_ref)
            dg1_ref[...] = jnp.zeros_like(dg1_ref)

        dff_t = dff_ref[...]
        dhf = None
        for s in range(FF_SPLIT):
            cols = slice(s * FF_TILE, (s + 1) * FF_TILE)
            dr = _nt(dff_t, wfo_ref[cols, :])
            df = (dr * (2.0 * jnp.maximum(f_ref[:, cols].astype(F32), 0.0))).astype(BF16)
            df_ref[:, cols] = df
            part = _nt(df, wfi_ref[s])
            dhf = part if dhf is None else dhf + part
        r2, xh = _rms_stats(x1_ref[...])
        dg2_ref[...] += _colsum(dhf * xh)
        dx1 = dy_ref[...] + _rms_bwd(dhf, xh, r2, g2_ref[...])
        dx1_ref[...] = dx1
        r1, mh = _rms_stats(mix_ref[...])
        dg1_ref[...] += _colsum(dx1 * mh)
        dmix_ref[...] = _rms_bwd(dx1, mh, r1, g1_ref[...]).astype(BF16)

    row = lambda i: (i, 0)
    const = lambda i: (0, 0)
    return pl.pallas_call(
        body, name="bwd_ff", grid=(T // tm,),
        in_specs=[pl.BlockSpec((tm, D), row), pl.BlockSpec((tm, D_FF), row),
                  _resident((FF_SPLIT, D, FF_TILE)), _resident((D_FF, D)),
                  pl.BlockSpec((tm, D), row), pl.BlockSpec((tm, D), row), pl.BlockSpec((tm, D), row),
                  pl.BlockSpec((1, D), const), pl.BlockSpec((1, D), const)],
        out_specs=[pl.BlockSpec((tm, D_FF), row), pl.BlockSpec((tm, D), row),
                   pl.BlockSpec((tm, D), row), pl.BlockSpec((1, D), const), pl.BlockSpec((1, D), const)],
        out_shape=[SDS((T, D_FF), BF16), SDS((T, D), F32), SDS((T, D), BF16), SDS((1, D), F32), SDS((1, D), F32)],
        compiler_params=_params(1),
    )(dff, f, wfi3, wfo, x1, dy, mix, g1, g2)


def _wgrad_ff(hf, df, f, dff):
    T = hf.shape[0]
    tt = min(T, 2048)
    slabs = 2
    wide = slabs * FF_TILE

    def body(hf_ref, df_ref, f_ref, dff_ref, dwfi_ref, dwfo_ref, acc_i, acc_o):
        t = pl.program_id(1)

        @pl.when(t == 0)
        def _():
            acc_i[...] = jnp.zeros_like(acc_i)
            acc_o[...] = jnp.zeros_like(acc_o)

        acc_i[...] += _tn(hf_ref[...], df_ref[...])
        rl = jnp.maximum(f_ref[...].astype(F32), 0.0)
        acc_o[...] += _tn((rl * rl).astype(BF16), dff_ref[...])

        @pl.when(t == T // tt - 1)
        def _():
            for s in range(slabs):
                dwfi_ref[s] = acc_i[:, s * FF_TILE:(s + 1) * FF_TILE].astype(BF16)
            dwfo_ref[...] = acc_o[...].astype(BF16)

    return pl.pallas_call(
        body, name="wgrad_ff", grid=(D_FF // wide, T // tt),
        in_specs=[pl.BlockSpec((tt, D), lambda p, t: (t, 0)), pl.BlockSpec((tt, wide), lambda p, t: (t, p)),
                  pl.BlockSpec((tt, wide), lambda p, t: (t, p)), pl.BlockSpec((tt, D), lambda p, t: (t, 0))],
        out_specs=[pl.BlockSpec((slabs, D, FF_TILE), lambda p, t: (p, 0, 0)), pl.BlockSpec((wide, D), lambda p, t: (p, 0))],
        out_shape=[SDS((FF_SPLIT, D, FF_TILE), BF16), SDS((D_FF, D), BF16)],
        scratch_shapes=[pltpu.VMEM((D, wide), F32), pltpu.VMEM((wide, D), F32)],
        compiler_params=_params(2),
    )(hf, df, f, dff)


def _bwd_mix(dmix, proj, a2, b2, merged, a, att, wo, wa, wb, after=None):
    T = dmix.shape[0]
    tm = min(T, 512)
    half = D // 2
    last = T // tm - 1

    def body(dmix_ref, ga0, ga1, gb0, gb1, a2_ref, b2_ref, mg_ref, a_ref, att_ref, wo_ref, wa_ref, wb_ref,
             dg_ref, da_ref, datt_ref, dwo_ref, dwa_ref, dwb_ref, acc, stage, sem):
        t = pl.program_id(0)

        @pl.when(t == 0)
        def _():
            acc[...] = jnp.zeros_like(acc)

        dmix_t = dmix_ref[...]
        dmg = _nt(dmix_t, wo_ref[...])
        sa = _sigmoid(jnp.concatenate([ga0[...], ga1[...]], axis=1).astype(F32))
        sb = _sigmoid(jnp.concatenate([gb0[...], gb1[...]], axis=1).astype(F32))
        da2 = (dmg * sa).astype(BF16)
        db2 = (dmg * sb).astype(BF16)
        dg_ref[:, :D] = (dmg * a2_ref[...].astype(F32) * (sa * (1.0 - sa))).astype(BF16)
        dg_ref[:, D:] = (dmg * b2_ref[...].astype(F32) * (sb * (1.0 - sb))).astype(BF16)
        da_ref[...] = _nt(da2, wa_ref[...]).astype(BF16)
        datt_ref[...] = _nt(db2, wb_ref[...]).astype(BF16)
        acc[0] += _tn(mg_ref[...], dmix_t)
        acc[1] += _tn(a_ref[...], da2)
        acc[2] += _tn(att_ref[...], db2)

        @pl.when(t == last)
        def _():
            for k, dw_ref in enumerate((dwo_ref, dwa_ref, dwb_ref)):
                stage[...] = acc[k].astype(BF16)
                out = pltpu.make_async_copy(stage, dw_ref, sem)
                out.start()
                out.wait()

    row = lambda i: (i, 0)
    gspec = lambda off: pl.BlockSpec((tm, half), lambda i: (i, off // half))
    body, dep_specs, deps = _after(body, 13, after)
    return pl.pallas_call(
        body, name="bwd_mix", grid=(T // tm,),
        in_specs=[pl.BlockSpec((tm, D), row), gspec(OFF_GA), gspec(OFF_GA + half), gspec(OFF_GB), gspec(OFF_GB + half)]
        + [pl.BlockSpec((tm, D), row)] * 5 + [_resident((D, D))] * 3 + dep_specs,
        out_specs=[pl.BlockSpec((tm, 2 * D), row), pl.BlockSpec((tm, D), row), pl.BlockSpec((tm, D), row)] + [_ANY] * 3,
        out_shape=[SDS((T, 2 * D), BF16), SDS((T, D), BF16), SDS((T, D), BF16)] + [SDS((D, D), BF16)] * 3,
        scratch_shapes=[pltpu.VMEM((3, D, D), F32), pltpu.VMEM((D, D), BF16), pltpu.SemaphoreType.DMA],
        compiler_params=_params(1),
    )(dmix, proj, proj, proj, proj, a2, b2, merged, a, att, wo, wa, wb, *deps)


def _bwd_attn(qr, kr, probs, psink, proj, cos, sin, datt, after=None):
    T = proj.shape[0]
    nb = T // CHUNK
    cur = lambda i: jnp.minimum(i, nb - 1)
    prev = lambda i: jnp.maximum(jnp.minimum(i, nb - 1) - 1, 0)

    def body(q_ref, kp_ref, kc_ref, vp_ref, vc_ref, cp_ref, cc_ref, sp_ref, sc_ref, p_ref, psink_ref, do_ref,
             dq_ref, dkv_ref, dsink_ref, carry_k, carry_v):
        i = pl.program_id(0)

        @pl.when(i == 0)
        def _():
            carry_k[...] = jnp.zeros_like(carry_k)
            carry_v[...] = jnp.zeros_like(carry_v)
            dsink_ref[...] = jnp.zeros_like(dsink_ref)

        @pl.when(i < nb)
        def _():
            prev_slot, _ = _fold_masks(i == 0)
            c_band, s_band = _band(cp_ref, cc_ref), _band(sp_ref, sc_ref)
            for j in range(KV_W // 128):
                cols = slice(j * 128, (j + 1) * 128)
                k_slab = _band(kp_ref, kc_ref, cols).astype(F32)
                v_slab = _band(vp_ref, vc_ref, cols).astype(F32)
                dk_slab = jnp.zeros((2 * CHUNK, 128), F32)
                dv_slab = jnp.zeros((2 * CHUNK, 128), F32)
                for g in (2 * j, 2 * j + 1):
                    k2 = _head_pair_operand(k_slab, g)
                    v2 = _head_pair_operand(v_slab, g)
                    pairs = [g * PAIRS_PER_KV + r for r in range(PAIRS_PER_KV)]
                    q_stack = jnp.concatenate([q_ref[:, pr * 128:(pr + 1) * 128] for pr in pairs], axis=0)
                    do_stack = jnp.concatenate([do_ref[:, pr * 128:(pr + 1) * 128] for pr in pairs], axis=0)
                    dp2 = _nt(v2, do_stack)
                    pcols, dscols = [], []
                    for r, pair in enumerate(pairs):
                        ps, dss = [], []
                        for e in range(2):
                            head = 2 * pair + e
                            p_b = p_ref[head]
                            p = p_b.astype(F32)
                            dp = _fold(dp2[e * 2 * CHUNK:(e + 1) * 2 * CHUNK, r * 128:(r + 1) * 128], prev_slot)
                            delta = jnp.sum(p * dp, axis=0, keepdims=True)
                            ps.append(_unfold(p_b, prev_slot))
                            dss.append(_unfold((p * (dp - delta)).astype(BF16), prev_slot))
                            dsink_ref[head:head + 1, :] -= psink_ref[head:head + 1, :] * delta
                        pcols.append(jnp.concatenate(ps, axis=0))
                        dscols.append(jnp.concatenate(dss, axis=0))
                    ds2 = jnp.concatenate(dscols, axis=1)
                    dq = _tn(ds2, k2) * (HEAD ** -0.5)
                    for r, pair in enumerate(pairs):
                        dq_ref[:, pair * 128:(pair + 1) * 128] = _rope_bwd(
                            dq[r * CHUNK:(r + 1) * CHUNK], cc_ref[...], sc_ref[...]).astype(BF16)
                    dk_slab = dk_slab + _head_pair_gradient(_nn(ds2, q_stack), g)
                    dv_slab = dv_slab + _head_pair_gradient(_nn(jnp.concatenate(pcols, axis=1), do_stack), g)
                dk_slab = _rope_bwd(dk_slab, c_band, s_band)
                vcols = slice(KV_W + j * 128, KV_W + (j + 1) * 128)
                dkv_ref[:, cols] = (carry_k[:, cols] + dk_slab[:CHUNK]).astype(BF16)
                dkv_ref[:, vcols] = (carry_v[:, cols] + dv_slab[:CHUNK]).astype(BF16)
                carry_k[:, cols] = dk_slab[CHUNK:]
                carry_v[:, cols] = dv_slab[CHUNK:]

        @pl.when(i == nb)
        def _():
            dkv_ref[:, :KV_W] = carry_k[...].astype(BF16)
            dkv_ref[:, KV_W:] = carry_v[...].astype(BF16)

    table = lambda which, width: pl.BlockSpec((CHUNK, width), lambda i: (which(i), 0))
    body, dep_specs, deps = _after(body, 12, after)
    return pl.pallas_call(
        body, name="bwd_attn", grid=(nb + 1,),
        in_specs=[pl.BlockSpec((CHUNK, D), lambda i: (cur(i), 0)),
                  pl.BlockSpec((CHUNK, KV_W), lambda i: (prev(i), 0)),
                  pl.BlockSpec((CHUNK, KV_W), lambda i: (cur(i), 0)),
                  pl.BlockSpec((CHUNK, KV_W), lambda i: (prev(i), OFF_VA // KV_W)),
                  pl.BlockSpec((CHUNK, KV_W), lambda i: (cur(i), OFF_VA // KV_W)),
                  table(prev, 128), table(cur, 128), table(prev, 256), table(cur, 256),
                  pl.BlockSpec((None, N_Q, CHUNK, CHUNK), lambda i: (cur(i), 0, 0, 0)),
                  pl.BlockSpec((None, N_Q, CHUNK), lambda i: (cur(i), 0, 0)),
                  pl.BlockSpec((CHUNK, D), lambda i: (cur(i), 0))] + dep_specs,
        out_specs=[pl.BlockSpec((CHUNK, D), lambda i: (cur(i), 0)),
                   pl.BlockSpec((CHUNK, 2 * KV_W), lambda i: (jnp.maximum(i - 1, 0), 0)),
                   pl.BlockSpec((N_Q, CHUNK), lambda i: (0, 0))],
        out_shape=[SDS((T, D), BF16), SDS((T, 2 * KV_W), BF16), SDS((N_Q, CHUNK), F32)],
        scratch_shapes=[pltpu.VMEM((CHUNK, KV_W), F32), pltpu.VMEM((CHUNK, KV_W), F32)],
        compiler_params=_params(1),
    )(qr, kr, kr, proj, proj, cos, cos, sin, sin, probs, psink, datt, *deps)


def _bwd_sgu(proj, tanhs, stats, da, lng, lnb, ws, bst):
    T = proj.shape[0]
    tc = min(T, 512)
    nsteps = T // tc

    def body(u_ref, vs_ref, t_ref, stat_ref, da_ref, lng_ref, lnb_ref, ws_ref, bst_ref,
             duv_ref, dws_ref, dbs_ref, dlng_ref, dlnb_ref, dvn_s, dgu_s, dmx_sum):
        i = pl.program_id(0)

        @pl.when(i == 0)
        def _():
            dws_ref[...] = jnp.zeros_like(dws_ref)
            dlng_ref[...] = jnp.zeros_like(dlng_ref)
            dlnb_ref[...] = jnp.zeros_like(dlnb_ref)
            dmx_sum[...] = jnp.zeros_like(dmx_sum)

        u, vs, gu, tu, tv, rstd, vhat, vn = _sgu_forward_replay(u_ref, vs_ref, t_ref, stat_ref, lng_ref, lnb_ref)
        da = da_ref[...].astype(F32)
        for g in range(GROUPS):
            wm = _masked_ws(ws_ref, g)
            cols = slice(g * CHUNK, (g + 1) * CHUNK)
            dws = jnp.zeros((CHUNK, CHUNK), F32)
            dsum = jnp.zeros((CHUNK, CHUNK), F32)
            for c in range(tc // CHUNK):
                rows = slice(c * CHUNK, (c + 1) * CHUNK)
                vn_cg = vn[rows, cols]
                mixed = _nn(wm, vn_cg) + bst_ref[:, g:g + 1]
                dgu_s[rows, cols] = da[rows, cols] * mixed
                dmx = da[rows, cols] * gu[rows, cols]
                dmxb = dmx.astype(BF16)
                dws = dws + _nt(dmxb, vn_cg)
                dsum = dsum + dmx
                dvn_s[rows, cols] = _tn(wm, dmxb)
            dws_ref[g] += dws
            dmx_sum[:, cols] += dsum
        dvn = dvn_s[...]
        dlng_ref[...] += _colsum(dvn * vhat)
        dlnb_ref[...] += _colsum(dvn)
        dvh = dvn * lng_ref[...]
        dgv = rstd * (dvh - jnp.mean(dvh, axis=-1, keepdims=True) - vhat * jnp.mean(dvh * vhat, axis=-1, keepdims=True))
        duv_ref[:, :D] = (dgu_s[...] * _gelu_grad(u, tu)).astype(BF16)
        duv_ref[:, D:] = (dgv * _gelu_grad(vs, tv)).astype(BF16)

        @pl.when(i == nsteps - 1)
        def _():
            row = lax.broadcasted_iota(jnp.int32, (CHUNK, CHUNK), 0)
            col = lax.broadcasted_iota(jnp.int32, (CHUNK, CHUNK), 1)
            for g in range(GROUPS):
                dws_ref[g] = jnp.where(row >= col, dws_ref[g], 0.0)
                dbs_ref[g:g + 1, :] = _colsum(dmx_sum[:, g * CHUNK:(g + 1) * CHUNK].T)

    const2 = lambda i: (0, 0)
    return pl.pallas_call(
        body, name="bwd_sgu", grid=(nsteps,),
        in_specs=[pl.BlockSpec((tc, D), lambda i: (i, 0)), pl.BlockSpec((tc, D), lambda i: (i, 1)),
                  pl.BlockSpec((tc, 2 * D), lambda i: (i, 0)), pl.BlockSpec((tc, 128), lambda i: (i, 0)),
                  pl.BlockSpec((tc, D), lambda i: (i, 0)), pl.BlockSpec((1, D), const2), pl.BlockSpec((1, D), const2),
                  pl.BlockSpec((GROUPS, CHUNK, CHUNK), lambda i: (0, 0, 0)), pl.BlockSpec((CHUNK, GROUPS), const2)],
        out_specs=[pl.BlockSpec((tc, 2 * D), lambda i: (i, 0)), pl.BlockSpec((GROUPS, CHUNK, CHUNK), lambda i: (0, 0, 0)),
                   pl.BlockSpec((GROUPS, CHUNK), const2), pl.BlockSpec((1, D), const2), pl.BlockSpec((1, D), const2)],
        out_shape=[SDS((T, 2 * D), BF16), SDS((GROUPS, CHUNK, CHUNK), F32), SDS((GROUPS, CHUNK), F32),
                   SDS((1, D), F32), SDS((1, D), F32)],
        scratch_shapes=[pltpu.VMEM((tc, D), F32), pltpu.VMEM((tc, D), F32), pltpu.VMEM((CHUNK, D), F32)],
        compiler_params=_params(1),
    )(proj, proj, tanhs, stats, da, lng, lnb, ws, bst)


IN_SEG_WIDTHS = (2 * D, D, 2 * N_KV * HEAD, 2 * D)


def _resident(shape):
    return pl.BlockSpec(shape, lambda *_: (0,) * len(shape), pipeline_mode=pl.Buffered(1))


def _bwd_in(duv, dq, dkv, dg, win_t, x, dx1, g0, after=None):
    T = x.shape[0]
    tm = min(T, 512)

    def body(duv_ref, dq_ref, dkv_ref, dg_ref, w_ref, x_ref, dx1_ref, g0_ref, gx_ref, dg0_ref):
        @pl.when(pl.program_id(0) == 0)
        def _():
            dg0_ref[...] = jnp.zeros_like(dg0_ref)

        dh, off = None, 0
        for ref, width in zip((duv_ref, dq_ref, dkv_ref, dg_ref), IN_SEG_WIDTHS):
            part = _nn(ref[...], w_ref[off:off + width, :])
            dh = part if dh is None else dh + part
            off += width
        r0, xh = _rms_stats(x_ref[...])
        dg0_ref[...] += _colsum(dh * xh)
        gx_ref[...] = dx1_ref[...] + _rms_bwd(dh, xh, r0, g0_ref[...])

    row = lambda i: (i, 0)
    body, dep_specs, deps = _after(body, 8, after)
    return pl.pallas_call(
        body, name="bwd_in", grid=(T // tm,),
        in_specs=[pl.BlockSpec((tm, w), row) for w in IN_SEG_WIDTHS] + [
            _resident((IN_W, D)), pl.BlockSpec((tm, D), row), pl.BlockSpec((tm, D), row),
            pl.BlockSpec((1, D), lambda i: (0, 0))] + dep_specs,
        out_specs=[pl.BlockSpec((tm, D), row), pl.BlockSpec((1, D), lambda i: (0, 0))],
        out_shape=[SDS((T, D), F32), SDS((1, D), F32)],
        compiler_params=_params(1),
    )(duv, dq, dkv, dg, win_t, x, dx1, g0, *deps)


def _wgrad_rows(h, segs, first_row, into, name):
    T = h.shape[0]
    tt = min(T, 2048)
    widths = [s.shape[1] for s in segs]
    rows = sum(widths)
    n_in = 1 + len(segs) + (into is not None)

    def body(*refs):
        h_ref, seg_refs = refs[0], refs[1:1 + len(segs)]
        dw_ref, acc, stage, sem = refs[n_in], refs[n_in + 1], refs[n_in + 2], refs[n_in + 3]
        t = pl.program_id(0)

        @pl.when(t == 0)
        def _():
            acc[...] = jnp.zeros_like(acc)

        off = 0
        for ref, width in zip(seg_refs, widths):
            acc[off:off + width, :] += _tn(ref[...], h_ref[...])
            off += width

        @pl.when(t == T // tt - 1)
        def _():
            stage[...] = acc[...].astype(BF16)
            out = pltpu.make_async_copy(stage, dw_ref.at[pl.ds(first_row, rows)], sem)
            out.start()
            out.wait()

    row = lambda t: (t, 0)
    return pl.pallas_call(
        body, name=name, grid=(T // tt,),
        in_specs=[pl.BlockSpec((tt, D), row)] + [pl.BlockSpec((tt, w), row) for w in widths] + [_ANY] * (into is not None),
        out_specs=_ANY,
        out_shape=SDS((IN_W, D), BF16),
        input_output_aliases={} if into is None else {n_in - 1: 0},
        scratch_shapes=[pltpu.VMEM((rows, D), F32), pltpu.VMEM((rows, D), BF16), pltpu.SemaphoreType.DMA],
        compiler_params=_params(1),
    )(h, *segs, *([] if into is None else [into]))


def _place():
    x, y, c = lax.axis_index("x"), lax.axis_index("y"), lax.axis_index("c")
    return x, y, c, 4 * x + 2 * y + c


def _peers(x, y, c):
    out = []
    for mask in range(1, N_DEV):
        px = 1 - x if mask & 4 else x
        py = 1 - y if mask & 2 else y
        pc = 1 - c if mask & 1 else c
        out.append(((px, py, pc), 4 * px + 2 * py + pc))
    return out


def _all_to_all(arrays, gather, name, after=None):
    n = len(arrays)

    def body(*refs):
        ins, outs = refs[:n], refs[n:2 * n]
        send_sems, recv_sems, local_sems = refs[2 * n:]
        x, y, c, me = _place()
        local, sends, recvs = [], [], []
        for a in range(n):
            src_own = ins[a] if gather[a] else ins[a].at[me]
            local.append(pltpu.make_async_copy(src_own, outs[a].at[me], local_sems.at[a]))
            for k, (peer, pid) in enumerate(_peers(x, y, c)):
                sem = a * (N_DEV - 1) + k
                src = ins[a] if gather[a] else ins[a].at[pid]
                sends.append(pltpu.make_async_remote_copy(
                    src_ref=src, dst_ref=outs[a].at[me], send_sem=send_sems.at[sem], recv_sem=recv_sems.at[sem],
                    device_id=peer, device_id_type=MESH))
                recvs.append(pltpu.make_async_remote_copy(
                    src_ref=src, dst_ref=outs[a].at[pid], send_sem=send_sems.at[sem], recv_sem=recv_sems.at[sem],
                    device_id=peer, device_id_type=MESH))
        for cp in local + sends:
            cp.start()
        for cp in recvs:
            cp.wait_recv()
        for cp in sends:
            cp.wait_send()
        for cp in local:
            cp.wait()

    out_shape = [SDS((N_DEV,) + a.shape if gt else a.shape, a.dtype) for a, gt in zip(arrays, gather)]
    nsem = n * (N_DEV - 1)
    body, dep_specs, deps = _after(body, n, after)
    return pl.pallas_call(
        body, name=name,
        in_specs=[pl.BlockSpec(memory_space=pl.ANY)] * n + dep_specs,
        out_specs=[pl.BlockSpec(memory_space=pl.ANY)] * n,
        out_shape=out_shape,
        scratch_shapes=[pltpu.SemaphoreType.DMA((nsem,)), pltpu.SemaphoreType.DMA((nsem,)), pltpu.SemaphoreType.DMA((n,))],
    )(*arrays, *deps)


_HBM = pl.BlockSpec(memory_space=pltpu.HBM)
_SEM = pl.BlockSpec(memory_space=pltpu.SEMAPHORE)
_EFFECT = pltpu.SideEffectType.DATAFLOW_SIDE_EFFECTING
GATHER = "gather"
SCATTER = "scatter"
SPREAD = "spread"


def _zone_shape(a, mode):
    if mode == GATHER:
        return (N_DEV,) + a.shape
    return (N_DEV - 1,) + (a.shape[1:] if mode == SCATTER else a.shape)


def _start_copies(arrays, modes, name, after=None):
    n = len(arrays)
    zones = [lax.empty(_zone_shape(a, m), a.dtype) for a, m in zip(arrays, modes)]

    def body(*refs):
        ins, lands = refs[:n], refs[n:2 * n]
        send_sems, recv_sems = refs[-2 * n - 3], refs[-2 * n - 2]
        token = refs[-1]
        x, y, c, me = _place()
        for a in range(n):
            for k, (peer, pid) in enumerate(_peers(x, y, c)):
                src = ins[a].at[pid] if modes[a] == SCATTER else ins[a]
                dst = lands[a].at[me] if modes[a] == GATHER else lands[a].at[k]
                pltpu.make_async_remote_copy(src_ref=src, dst_ref=dst, send_sem=send_sems.at[a], recv_sem=recv_sems.at[a],
                                             device_id=peer, device_id_type=MESH).start()
            if modes[a] == GATHER:
                pltpu.make_async_remote_copy(src_ref=ins[a], dst_ref=lands[a].at[me], send_sem=send_sems.at[a],
                                             recv_sem=recv_sems.at[a], device_id=(x, y, c), device_id_type=MESH).start()
        token[...] = jnp.zeros_like(token)

    hbm = lambda a: pltpu.HBM(a.shape, a.dtype)
    sems = pltpu.SemaphoreType.DMA((n,))
    extra = [] if after is None else [after]
    operands = [pltpu.with_memory_space_constraint(a, pltpu.HBM) for a in list(arrays) + zones]
    res = pl.pallas_call(
        body, name=name,
        out_shape=(sems, sems, *[hbm(a) for a in arrays], *[hbm(z) for z in zones], SDS((8, 128), F32)),
        in_specs=[_HBM] * (2 * n) + [_ANY] * len(extra),
        out_specs=(_SEM, _SEM, *[_HBM] * (2 * n), pl.BlockSpec(memory_space=pltpu.VMEM)),
        input_output_aliases={i: 2 + i for i in range(2 * n)},
        compiler_params=pltpu.CompilerParams(has_side_effects=_EFFECT),
    )(*operands, *extra)
    return res[0], res[1], list(res[2:2 + n]), list(res[2 + n:2 + 2 * n]), res[-1]


def _wait_copies(started, after, name, count=N_DEV - 1):
    send_sems, recv_sems, thru, zones, _ = started
    nt, nz = len(thru), len(zones)

    def body(*refs):
        lands = refs[nt:nt + nz]
        send_ref, recv_ref = refs[nt + nz], refs[nt + nz + 1]
        x, y, c, _ = _place()
        for a in range(nz):
            blocks = lands[a].at[pl.ds(0, count)]
            cp = pltpu.make_async_remote_copy(src_ref=blocks, dst_ref=blocks, send_sem=send_ref.at[a], recv_sem=recv_ref.at[a],
                                              device_id=(x, y, 1 - c), device_id_type=MESH)
            cp.wait_send()
            cp.wait_recv()

    hbm = lambda a: pltpu.HBM(a.shape, a.dtype)
    res = pl.pallas_call(
        body, name=name,
        out_shape=tuple(hbm(a) for a in thru + zones),
        in_specs=[_HBM] * (nt + nz) + [_SEM, _SEM, _ANY],
        out_specs=tuple([_HBM] * (nt + nz)),
        input_output_aliases={i: i for i in range(nt + nz)},
        compiler_params=pltpu.CompilerParams(has_side_effects=_EFFECT),
    )(*thru, *zones, send_sems, recv_sems, after)
    return list(res[:nt]), list(res[nt:])


def _split_start(body, arrays, zones, name, after):
    n = len(arrays) + len(zones)
    hbm = lambda a: pltpu.HBM(a.shape, a.dtype)
    sems = pltpu.SemaphoreType.DMA((max(len(zones), 1),))
    extra = [] if after is None else [after]
    operands = [pltpu.with_memory_space_constraint(a, pltpu.HBM) for a in list(arrays) + list(zones)]
    res = pl.pallas_call(
        body, name=name,
        out_shape=(sems, sems, *[hbm(a) for a in operands], SDS((8, 128), F32)),
        in_specs=[_HBM] * n + [_ANY] * len(extra),
        out_specs=(_SEM, _SEM, *[_HBM] * n, pl.BlockSpec(memory_space=pltpu.VMEM)),
        input_output_aliases={i: 2 + i for i in range(n)},
        compiler_params=pltpu.CompilerParams(has_side_effects=_EFFECT),
    )(*operands, *extra)
    return res[0], res[1], list(res[2:2 + len(arrays)]), list(res[2 + len(arrays):2 + n]), res[-1]


def _gather_first_leg(shard, name, after=None):
    zone = lax.empty((N_DEV,) + shard.shape, shard.dtype)
    extra = 0 if after is None else 1

    def body(*refs):
        src, land = refs[0], refs[1]
        send_sem, recv_sem, token = refs[2 + extra], refs[3 + extra], refs[-1]
        x, y, c, me = _place()
        for peer in ((x, y, c), (x, y, 1 - c), (1 - x, y, c), (x, 1 - y, c), (1 - x, 1 - y, c)):
            pltpu.make_async_remote_copy(src_ref=src, dst_ref=land.at[me], send_sem=send_sem.at[0], recv_sem=recv_sem.at[0],
                                         device_id=peer, device_id_type=MESH).start()
        token[...] = jnp.zeros_like(token)

    return _split_start(body, [shard], [zone], name, after)


def _gather_second_leg(zone, name, after=None):
    extra = 0 if after is None else 1

    def body(*refs):
        land = refs[0]
        send_sem, recv_sem, token = refs[1 + extra], refs[2 + extra], refs[-1]
        x, y, c, _ = _place()
        for px, py in ((1 - x, y), (x, 1 - y), (1 - x, 1 - y)):
            slot = 4 * px + 2 * py + c
            pltpu.make_async_remote_copy(src_ref=land.at[slot], dst_ref=land.at[slot], send_sem=send_sem.at[0],
                                         recv_sem=recv_sem.at[0], device_id=(x, y, 1 - c), device_id_type=MESH).start()
        token[...] = jnp.zeros_like(token)

    return _split_start(body, [], [zone], name, after)


UPDATE_BLOCK_ELEMS = 384 * 1024


def _update_rows(R, C):
    fits = [t for t in range(8, R + 1, 8) if R % t == 0 and t * C <= UPDATE_BLOCK_ELEMS]
    whole = [t for t in fits if t % 16 == 0]
    return max(whole or fits)


def _adamw_math(g, w, m, v):
    m2 = ADAM_B1 * m + (1.0 - ADAM_B1) * g
    v2 = ADAM_B2 * v + (1.0 - ADAM_B2) * (g * g)
    m_hat = m2 / (1.0 - ADAM_B1 ** ADAM_STEP)
    v_hat = v2 / (1.0 - ADAM_B2 ** ADAM_STEP)
    delta = -ADAM_LR * (m_hat / (jnp.sqrt(v_hat) + ADAM_EPS) + ADAM_WD * w)
    return delta, m2, v2


def _sum_adamw(parts, w, m, v, name):
    R, C = w.shape
    tr = _update_rows(R, C)

    def body(p_ref, w_ref, m_ref, v_ref, g_ref, d_ref, m2_ref, v2_ref):
        g = p_ref[0]
        for k in range(1, N_DEV):
            g = g + p_ref[k]
        g_ref[...] = g
        d_ref[...], m2_ref[...], v2_ref[...] = _adamw_math(g, w_ref[...], m_ref[...], v_ref[...])

    blk = pl.BlockSpec((tr, C), lambda i: (i, 0))
    return pl.pallas_call(
        body, name=name, grid=(R // tr,),
        in_specs=[pl.BlockSpec((N_DEV, tr, C), lambda i: (0, i, 0)), blk, blk, blk],
        out_specs=[blk] * 4,
        out_shape=[SDS((R, C), F32)] * 4,
        compiler_params=_params(1),
    )(parts, w, m, v)


def _sum_adamw_peers(me, own, parts, w, m, v, name, replicated, also_rows=None):
    R, C = w.shape
    tr = _update_rows(R, C)
    assert also_rows is None or tr == R

    def body(me_ref, own_ref, p_ref, w_ref, m_ref, v_ref, g_ref, d_ref, m2_ref, v2_ref, *extra):
        if replicated:
            mine = me_ref[0]
            g = None
            for j in range(N_DEV):
                k = jnp.maximum(jnp.bitwise_xor(mine, j) - 1, 0)
                term = jnp.where(mine == j, own_ref[...], p_ref[k])
                g = term if g is None else g + term
        else:
            g = own_ref[...].astype(F32)
            for k in range(N_DEV - 1):
                g = g + p_ref[k].astype(F32)
        results = (g,) + _adamw_math(g, w_ref[...], m_ref[...], v_ref[...])
        for ref, val in zip((g_ref, d_ref, m2_ref, v2_ref), results):
            ref[...] = val
        for ref, val in zip(extra, results):
            ref[...] = val[also_rows[0]:also_rows[1]]

    blk = pl.BlockSpec((tr, C), lambda i, me_ref: (i, 0))
    own_spec = blk if replicated else pl.BlockSpec((None, tr, C), lambda i, me_ref: (me_ref[0], i, 0))
    n_also = 0 if also_rows is None else also_rows[1] - also_rows[0]
    also_specs = [pl.BlockSpec((n_also, C), lambda i, me_ref: (0, 0))] * (4 if also_rows else 0)
    return pl.pallas_call(
        body, name=name,
        grid_spec=pltpu.PrefetchScalarGridSpec(
            num_scalar_prefetch=1, grid=(R // tr,),
            in_specs=[own_spec, pl.BlockSpec((N_DEV - 1, tr, C), lambda i, me_ref: (0, i, 0)), blk, blk, blk],
            out_specs=[blk] * 4 + also_specs),
        out_shape=[SDS((R, C), F32)] * 4 + [SDS((n_also, C), F32)] * len(also_specs),
        compiler_params=_params(1),
    )(me, own, parts, w, m, v)


SMALL = ("ln_v_gain", "ln_v_bias", "w_spatial", "b_spatial", "sinks", "norm_mix_post", "norm_ff_pre", "norm_ff_post")
SMALL_ROWS = {"ln_v_gain": 8, "ln_v_bias": 8, "w_spatial": 1024, "b_spatial": 8, "sinks": 8,
              "norm_mix_post": 8, "norm_ff_pre": 8, "norm_ff_post": 8}
SMALL_PACK_ROWS = 1152


def _pack_small(vals):
    rows = []
    for name in SMALL:
        flat = vals[name].reshape(-1)
        pad = SMALL_ROWS[name] * 128 - flat.shape[0]
        if pad:
            flat = jnp.concatenate([flat, jnp.zeros((pad,), F32)])
        rows.append(flat.reshape(SMALL_ROWS[name], 128))
    rows.append(jnp.zeros((SMALL_PACK_ROWS - sum(SMALL_ROWS.values()), 128), F32))
    return jnp.concatenate(rows, axis=0)


def _unpack_small(packed, shapes):
    out, r = {}, 0
    for name in SMALL:
        n = 1
        for s in shapes[name]:
            n *= s
        out[name] = packed[r:r + SMALL_ROWS[name]].reshape(-1)[:n].reshape(shapes[name])
        r += SMALL_ROWS[name]
    return out


def _rope_rows():
    d = jnp.arange(128) % HEAD
    inv = ROPE_THETA ** (-(2.0 * (d % (ROPE // 2))).astype(F32) / ROPE)
    invf = jnp.where(d < ROPE, inv, 0.0).astype(F32).reshape(1, 128)
    sgn = jnp.where(d < ROPE // 2, -1.0, jnp.where(d < ROPE, 1.0, 0.0)).astype(F32).reshape(1, 128)
    return invf, sgn


def kernel(x, positions, w_in, ln_v_gain, ln_v_bias, w_spatial, b_spatial, sinks, w_a, w_b, w_o, norm_mix_pre, norm_mix_post, w_ff_in, w_ff_out, norm_ff_pre, norm_ff_post, loss_target, m_w_in, m_ln_v_gain, m_ln_v_bias, m_w_spatial, m_b_spatial, m_sinks, m_w_a, m_w_b, m_w_o, m_norm_mix_pre, m_norm_mix_post, m_w_ff_in, m_w_ff_out, m_norm_ff_pre, m_norm_ff_post, v_w_in, v_ln_v_gain, v_ln_v_bias, v_w_spatial, v_b_spatial, v_sinks, v_w_a, v_w_b, v_w_o, v_norm_mix_pre, v_norm_mix_post, v_w_ff_in, v_w_ff_out, v_norm_ff_pre, v_norm_ff_post):
    given = dict(locals())
    T = x.shape[1]
    xt = x[0]
    tgt = loss_target[0]
    bst = b_spatial[0].T
    ws = w_spatial[0]

    me = 4 * lax.axis_index("x") + 2 * lax.axis_index("y") + lax.axis_index("c")
    me_arr = me.astype(jnp.int32).reshape(1)

    rest = ("w_a", "w_b", "w_o", "w_ff_in", "w_ff_out")
    shard = {n: given[n][0].astype(BF16) for n in rest}
    g_one = _gather_first_leg(w_in[0].T.astype(BF16), "gather_in_start")
    cos, sin = _rope_tables(positions.astype(F32).reshape(T, 1), *_rope_rows(), after=g_one[-1])
    small_state = [_pack_small({n: given[k + n] for n in SMALL}) for k in ("", "m_", "v_")]
    h = _rms_pre(xt, norm_mix_pre, after=[cos, *small_state, *[shard[n] for n in rest]])
    _, (win8,) = _wait_copies(g_one, h, "gather_in_wait", count=5)
    g_two = _gather_second_leg(win8, "gather_in_pass_start")
    g_rest = _start_copies([shard[n] for n in rest], [GATHER] * len(rest), "gather_rest_start", after=g_two[-1])
    _, (win8,) = _wait_copies(g_two, g_rest[-1], "gather_in_pass_wait", count=3)
    win = win8.reshape(IN_W, D)

    proj = _fwd_in(h, win)
    att, qr, kr, probs, psink = _fwd_attn(proj, cos, sin, sinks[0])
    a, tanhs, ln_stats = _fwd_sgu(proj, ln_v_gain, ln_v_bias, ws, bst, after=att)
    gw = dict(zip(rest, _wait_copies(g_rest, a, "gather_rest_wait", count=N_DEV)[1]))
    wa, wb, wo = (gw[n].reshape(D, D) for n in ("w_a", "w_b", "w_o"))
    wfi3 = gw["w_ff_in"]
    wfo = gw["w_ff_out"].reshape(D_FF, D)
    merged, a2, b2, mix, x1, hf = _fwd_mix(a, att, proj, xt, wa, wb, wo, norm_mix_post, norm_ff_pre)
    f, dy, dff, dg3, loss_part = _fwd_ff(hf, wfi3, wfo, x1, tgt, norm_ff_post)

    df, dx1, dmix, dg2, dg1 = _bwd_ff(dff, f, wfi3, wfo, x1, dy, mix, norm_mix_post, norm_ff_pre)
    dwfi3, dwfo = _wgrad_ff(hf, df, f, dff)
    own_ff = [dwfi3, dwfo.reshape(N_DEV, D_FF // N_DEV, D)]
    x_ff = _start_copies(own_ff, [SCATTER] * 2, "exchange_ff_start")
    dgate, da, datt, dwo, dwa, dwb = _bwd_mix(dmix, proj, a2, b2, merged, a, att, wo, wa, wb, after=x_ff[-1])
    own_mix = [g.reshape(N_DEV, D // N_DEV, D) for g in (dwa, dwb, dwo)]
    x_mix = _start_copies(own_mix, [SCATTER] * 3, "exchange_mix_start")
    dq, dkv, dsink = _bwd_attn(qr, kr, probs, psink, proj, cos, sin, datt, after=x_mix[-1])
    duv, dws, dbs, dlng, dlnb = _bwd_sgu(proj, tanhs, ln_stats, da, ln_v_gain, ln_v_bias, ws, bst)
    small_grads = {"ln_v_gain": dlng, "ln_v_bias": dlnb, "w_spatial": dws, "b_spatial": dbs, "sinks": jnp.sum(dsink, axis=1),
                   "norm_mix_post": dg1, "norm_ff_pre": dg2, "norm_ff_post": dg3}
    x_small = _start_copies([_pack_small(small_grads)], [SPREAD], "exchange_small_start")
    dwin = _wgrad_rows(h, [dgate], sum(IN_SEG_WIDTHS[:3]), None, "wgrad_in_gates")
    dwin = _wgrad_rows(h, [duv], 0, dwin, "wgrad_in_uv")
    dwin = _wgrad_rows(h, [dq, dkv], IN_SEG_WIDTHS[0], dwin, "wgrad_in_qkv")
    own_in = [dwin.reshape(N_DEV, IN_W // N_DEV, D)]
    x_in = _start_copies(own_in, [SCATTER], "exchange_in_start", after=x_small[-1])
    grad_x, dg0 = _bwd_in(duv, dq, dkv, dgate, win, xt, dx1, norm_mix_pre, after=x_in[-1])

    results = {}

    def update(n, own, parts, transposed=False):
        state = [given[k + n][0].T if transposed else given[k + n][0] for k in ("", "m_", "v_")]
        res = _sum_adamw_peers(me_arr, own, parts, *state, "adamw_" + n, False)
        results[n] = [(r.T if transposed else r).reshape(given[n].shape) for r in res]

    own_ff, p_ff = _wait_copies(x_ff, grad_x, "exchange_ff_wait")
    update("w_ff_in", own_ff[0], p_ff[0])
    update("w_ff_out", own_ff[1], p_ff[1])
    own_mix, p_mix = _wait_copies(x_mix, results["w_ff_out"][0], "exchange_mix_wait")
    for n, own, parts in zip(("w_a", "w_b", "w_o"), own_mix, p_mix):
        update(n, own, parts)
    tail = jnp.concatenate([dg0.reshape(8, 128), jnp.tile(loss_part, (8, 1))], axis=0)
    (tail_all,) = _all_to_all([tail], [True], "exchange_tail", after=results["w_o"][0])
    dg0_all = tail_all[:, :8]
    own_small, p_small = _wait_copies(x_small, tail_all, "exchange_small_wait")
    own_in, p_in = _wait_copies(x_in, p_small[0], "exchange_in_wait")
    update("w_in", own_in[0], p_in[0], transposed=True)
    first = sum(SMALL_ROWS[n] for n in SMALL[:SMALL.index("w_spatial")])
    packed = _sum_adamw_peers(me_arr, own_small[0], p_small[0], *small_state, "adamw_small", True,
                              also_rows=(first, first + SMALL_ROWS["w_spatial"]))
    shapes = {n: given[n].shape for n in SMALL}
    unpacked = [_unpack_small(p, shapes) for p in packed[:4]]
    for n in SMALL:
        results[n] = [u[n] for u in unpacked]
    results["w_spatial"] = [r.reshape(w_spatial.shape) for r in packed[4:]]
    n = "norm_mix_pre"
    results[n] = [r.reshape(given[n].shape) for r in _sum_adamw(
        dg0_all, given[n].reshape(8, 128), given["m_" + n].reshape(8, 128), given["v_" + n].reshape(8, 128), "adamw_" + n)]

    loss = jnp.sum(tail_all[:, 8, 0])
    order = ("w_in", "ln_v_gain", "ln_v_bias", "w_spatial", "b_spatial", "sinks", "w_a", "w_b", "w_o", "norm_mix_pre",
             "norm_mix_post", "w_ff_in", "w_ff_out", "norm_ff_pre", "norm_ff_post")
    out = [loss, grad_x.reshape(x.shape)]
    for k in range(4):
        out += [results[n][k] for n in order]
    return tuple(out)
```

```python
import jax
import jax.numpy as jnp
from jax import lax
from jax.experimental import pallas as pl
from jax.experimental.pallas import tpu as pltpu

F32 = jnp.float32
BF16 = jnp.bfloat16

N_DEV = 8
D = 1024
D_FF = 4096
IN_W = 5632
CHUNK = 128
GROUPS = 8
HEAD = 64
N_Q = 16
N_KV = 4
ROPE = 16
ROPE_THETA = 500000.0
EPS = 1e-6
OFF_Q, OFF_K, OFF_VA, OFF_GA, OFF_GB = 2048, 3072, 3328, 3584, 4608

ADAM_LR = 0.001
ADAM_B1 = 0.9
ADAM_B2 = 0.999
ADAM_EPS = 1e-08
ADAM_WD = 0.01
ADAM_STEP = 10

VMEM_LIMIT = 62 * 1024 * 1024

SDS = jax.ShapeDtypeStruct
MESH = pl.DeviceIdType.MESH


def _params(n_axes):
    return pltpu.CompilerParams(dimension_semantics=("arbitrary",) * n_axes, vmem_limit_bytes=VMEM_LIMIT)


def _nt(a, b):
    return lax.dot_general(a, b, (((1,), (1,)), ((), ())), preferred_element_type=F32)


def _tn(a, b):
    return lax.dot_general(a, b, (((0,), (0,)), ((), ())), preferred_element_type=F32)


def _nn(a, b):
    return jnp.dot(a, b, preferred_element_type=F32)


def _gelu(x):
    t = jnp.tanh(0.7978845608028654 * (x + 0.044715 * (x * x * x)))
    return 0.5 * x * (1.0 + t), t


def _gelu_grad(x, t):
    return 0.5 * (1.0 + t) + 0.5 * x * (1.0 - t * t) * (0.7978845608028654 * (1.0 + 3.0 * 0.044715 * x * x))


def _sigmoid(x):
    return 1.0 / (1.0 + jnp.exp(-x))


def _rms_stats(v):
    r = lax.rsqrt(jnp.mean(v * v, axis=-1, keepdims=True) + EPS)
    return r, v * r


def _rms_bwd(d, vhat, r, g):
    gd = g * d
    return r * (gd - vhat * jnp.mean(gd * vhat, axis=-1, keepdims=True))


def _colsum(v):
    return jnp.sum(v, axis=0, keepdims=True)


_ANY = pl.BlockSpec(memory_space=pl.ANY)


def _after(body, n_in, after):
    if after is None:
        return body, [], []
    deps = list(after) if isinstance(after, (list, tuple)) else [after]

    def ordered(*refs):
        return body(*refs[:n_in], *refs[n_in + len(deps):])

    return ordered, [_ANY] * len(deps), deps


def _rms_pre(x, g0, after=None):
    T = x.shape[0]
    tm = min(T, 1024)

    def body(x_ref, g_ref, h_ref):
        _, xh = _rms_stats(x_ref[...])
        h_ref[...] = (xh * g_ref[...]).astype(BF16)

    body, dep_specs, deps = _after(body, 2, after)
    return pl.pallas_call(
        body, name="rms_pre", grid=(T // tm,),
        in_specs=[pl.BlockSpec((tm, D), lambda i: (i, 0)), pl.BlockSpec((1, D), lambda i: (0, 0))] + dep_specs,
        out_specs=pl.BlockSpec((tm, D), lambda i: (i, 0)),
        out_shape=SDS((T, D), BF16),
        compiler_params=_params(1),
    )(x, g0, *deps)


def _fwd_in(h, win_t):
    T = h.shape[0]
    tm, tn = min(T, 1024), 1408

    def body(h_ref, w_ref, p_ref):
        for j in range(IN_W // tn):
            cols = slice(j * tn, (j + 1) * tn)
            p_ref[:, cols] = _nt(h_ref[...], w_ref[cols, :]).astype(BF16)

    return pl.pallas_call(
        body, name="fwd_in", grid=(T // tm,),
        in_specs=[pl.BlockSpec((tm, D), lambda i: (i, 0)), _resident((IN_W, D))],
        out_specs=pl.BlockSpec((tm, IN_W), lambda i: (i, 0)),
        out_shape=SDS((T, IN_W), BF16),
        compiler_params=_params(1),
    )(h, win_t)


def _sgu_forward_parts(u_ref, vs_ref, lng_ref, lnb_ref):
    u = u_ref[...].astype(F32)
    vs = vs_ref[...].astype(F32)
    gu, tu = _gelu(u)
    gv, tv = _gelu(vs)
    mu = jnp.mean(gv, axis=-1, keepdims=True)
    dv = gv - mu
    rstd = lax.rsqrt(jnp.mean(dv * dv, axis=-1, keepdims=True) + EPS)
    vhat = dv * rstd
    vn = (vhat * lng_ref[...] + lnb_ref[...]).astype(BF16)
    return gu, tu, tv, mu, rstd, vn


def _sgu_forward_replay(u_ref, vs_ref, t_ref, stat_ref, lng_ref, lnb_ref):
    u = u_ref[...].astype(F32)
    vs = vs_ref[...].astype(F32)
    tu = t_ref[:, :D].astype(F32)
    tv = t_ref[:, D:].astype(F32)
    gu = 0.5 * u * (1.0 + tu)
    rstd = stat_ref[:, 1:2]
    vhat = (0.5 * vs * (1.0 + tv) - stat_ref[:, 0:1]) * rstd
    vn = (vhat * lng_ref[...] + lnb_ref[...]).astype(BF16)
    return u, vs, gu, tu, tv, rstd, vhat, vn


def _masked_ws(ws_ref, g):
    row = lax.broadcasted_iota(jnp.int32, (CHUNK, CHUNK), 0)
    col = lax.broadcasted_iota(jnp.int32, (CHUNK, CHUNK), 1)
    return jnp.where(row >= col, ws_ref[g], 0.0).astype(BF16)


def _fwd_sgu(proj, lng, lnb, ws, bst, after=None):
    T = proj.shape[0]
    tc = min(T, 1024)

    def body(u_ref, vs_ref, lng_ref, lnb_ref, ws_ref, bst_ref, a_ref, t_ref, stat_ref):
        gu, tu, tv, mu, rstd, vn = _sgu_forward_parts(u_ref, vs_ref, lng_ref, lnb_ref)
        t_ref[:, :D] = tu.astype(BF16)
        t_ref[:, D:] = tv.astype(BF16)
        lane = lax.broadcasted_iota(jnp.int32, (tc, 128), 1)
        stat_ref[...] = jnp.where(lane == 0, mu, jnp.where(lane == 1, rstd, 0.0))
        for g in range(GROUPS):
            wm = _masked_ws(ws_ref, g)
            cols = slice(g * CHUNK, (g + 1) * CHUNK)
            for c in range(tc // CHUNK):
                rows = slice(c * CHUNK, (c + 1) * CHUNK)
                mixed = _nn(wm, vn[rows, cols]) + bst_ref[:, g:g + 1]
                a_ref[rows, cols] = (gu[rows, cols] * mixed).astype(BF16)

    body, dep_specs, deps = _after(body, 6, after)
    return pl.pallas_call(
        body, name="fwd_sgu", grid=(T // tc,),
        in_specs=[pl.BlockSpec((tc, D), lambda i: (i, 0)), pl.BlockSpec((tc, D), lambda i: (i, 1)),
                  pl.BlockSpec((1, D), lambda i: (0, 0)), pl.BlockSpec((1, D), lambda i: (0, 0)),
                  pl.BlockSpec((GROUPS, CHUNK, CHUNK), lambda i: (0, 0, 0)),
                  pl.BlockSpec((CHUNK, GROUPS), lambda i: (0, 0))] + dep_specs,
        out_specs=[pl.BlockSpec((tc, D), lambda i: (i, 0)), pl.BlockSpec((tc, 2 * D), lambda i: (i, 0)),
                   pl.BlockSpec((tc, 128), lambda i: (i, 0))],
        out_shape=[SDS((T, D), BF16), SDS((T, 2 * D), BF16), SDS((T, 128), F32)],
        compiler_params=_params(1),
    )(proj, proj, lng, lnb, ws, bst, *deps)


def _rope_tables(posf, invf, sgn, after=None):
    T = posf.shape[0]
    tr = min(T, 1024)

    def body(pos_ref, invf_ref, sgn_ref, c_ref, s_ref):
        ang = pos_ref[...] * invf_ref[...]
        c_ref[...] = jnp.cos(ang)
        s = jnp.sin(ang)
        s_ref[:, :128] = jnp.where(sgn_ref[...] < 0.0, -s, 0.0)
        s_ref[:, 128:] = jnp.where(sgn_ref[...] > 0.0, s, 0.0)

    body, dep_specs, deps = _after(body, 3, after)
    return pl.pallas_call(
        body, name="rope_tables", grid=(T // tr,),
        in_specs=[pl.BlockSpec((tr, 1), lambda i: (i, 0)), pl.BlockSpec((1, 128), lambda i: (0, 0)),
                  pl.BlockSpec((1, 128), lambda i: (0, 0))] + dep_specs,
        out_specs=[pl.BlockSpec((tr, 128), lambda i: (i, 0)), pl.BlockSpec((tr, 256), lambda i: (i, 0))],
        out_shape=[SDS((T, 128), F32), SDS((T, 256), F32)],
        compiler_params=_params(1),
    )(posf, invf, sgn, *deps)


def _rope(v, c, s):
    v = v.astype(F32)
    return v * c + pltpu.roll(v, 128 - ROPE // 2, 1) * s[:, :128] + pltpu.roll(v, ROPE // 2, 1) * s[:, 128:]


def _rope_bwd(dv, c, s):
    return dv * c + pltpu.roll(dv * s[:, :128], ROPE // 2, 1) + pltpu.roll(dv * s[:, 128:], 128 - ROPE // 2, 1)


def _fold_masks(first):
    jj = lax.broadcasted_iota(jnp.int32, (CHUNK, CHUNK), 0)
    t = lax.broadcasted_iota(jnp.int32, (CHUNK, CHUNK), 1)
    prev = jj > t
    return prev, jnp.where(prev & first, -1e30, 0.0)


def _fold(band, prev):
    return jnp.where(prev, band[:CHUNK], band[CHUNK:])


def _unfold(folded, prev):
    return jnp.concatenate([jnp.where(prev, folded, 0.0), jnp.where(prev, 0.0, folded)], axis=0)


def _softmax_sink(s, sink, key_axis):
    m = jnp.maximum(jnp.max(s, axis=key_axis, keepdims=True), sink)
    p = jnp.exp(s - m)
    esink = jnp.exp(sink - m)
    inv = 1.0 / (jnp.sum(p, axis=key_axis, keepdims=True) + esink)
    return p * inv, esink * inv


def _head_pair_operand(slab, g):
    lo = lax.broadcasted_iota(jnp.int32, slab.shape, 1) < HEAD
    if g % 2 == 0:
        first = jnp.where(lo, slab, 0.0)
        second = pltpu.roll(first, HEAD, 1)
    else:
        second = jnp.where(lo, 0.0, slab)
        first = pltpu.roll(second, HEAD, 1)
    return jnp.concatenate([first, second], axis=0).astype(BF16)


def _head_pair_gradient(acc, g):
    top, bot = acc[:2 * CHUNK], acc[2 * CHUNK:]
    lo = lax.broadcasted_iota(jnp.int32, top.shape, 1) < HEAD
    if g % 2 == 0:
        return jnp.where(lo, top, 0.0) + pltpu.roll(jnp.where(lo, 0.0, bot), HEAD, 1)
    return pltpu.roll(jnp.where(lo, top, 0.0), HEAD, 1) + jnp.where(lo, 0.0, bot)


PAIRS_PER_KV = N_Q // N_KV // 2
KV_W = N_KV * HEAD


def _band(prev_ref, cur_ref, cols=slice(None)):
    return jnp.concatenate([prev_ref[:, cols], cur_ref[:, cols]], axis=0)


def _fwd_attn(proj, cos, sin, sinks):
    T = proj.shape[0]
    nb = T // CHUNK
    cur = lambda i: i
    prev = lambda i: jnp.maximum(i - 1, 0)

    def body(q_ref, kp_ref, kc_ref, vp_ref, vc_ref, cp_ref, cc_ref, sp_ref, sc_ref, sink_ref,
             o_ref, qr_ref, kr_ref, p_ref, psink_ref):
        prev_slot, bias = _fold_masks(pl.program_id(0) == 0)
        c_band, s_band = _band(cp_ref, cc_ref), _band(sp_ref, sc_ref)
        for j in range(KV_W // 128):
            cols = slice(j * 128, (j + 1) * 128)
            k_slab = _rope(_band(kp_ref, kc_ref, cols), c_band, s_band)
            kr_ref[:, cols] = k_slab[CHUNK:].astype(BF16)
            v_slab = _band(vp_ref, vc_ref, cols).astype(F32)
            for g in (2 * j, 2 * j + 1):
                k2 = _head_pair_operand(k_slab, g)
                v2 = _head_pair_operand(v_slab, g)
                pairs = [g * PAIRS_PER_KV + r for r in range(PAIRS_PER_KV)]
                qps = []
                for pair in pairs:
                    lanes = slice(pair * 128, (pair + 1) * 128)
                    qps.append((_rope(q_ref[:, lanes], cc_ref[...], sc_ref[...]) * (HEAD ** -0.5)).astype(BF16))
                    qr_ref[:, lanes] = qps[-1]
                s2 = _nt(k2, jnp.concatenate(qps, axis=0))
                pcols = []
                for r, pair in enumerate(pairs):
                    ps = []
                    for e in range(2):
                        head = 2 * pair + e
                        s = _fold(s2[e * 2 * CHUNK:(e + 1) * 2 * CHUNK, r * 128:(r + 1) * 128], prev_slot) + bias
                        p, psink = _softmax_sink(s, sink_ref[head], 0)
                        p = p.astype(BF16)
                        p_ref[head] = p
                        psink_ref[head:head + 1, :] = psink
                        ps.append(_unfold(p, prev_slot))
                    pcols.append(jnp.concatenate(ps, axis=0))
                o = _tn(jnp.concatenate(pcols, axis=1), v2).astype(BF16)
                for r, pair in enumerate(pairs):
                    o_ref[:, pair * 128:(pair + 1) * 128] = o[r * CHUNK:(r + 1) * CHUNK]

    table = lambda which, width: pl.BlockSpec((CHUNK, width), lambda i: (which(i), 0))
    return pl.pallas_call(
        body, name="fwd_attn", grid=(nb,),
        in_specs=[pl.BlockSpec((CHUNK, D), lambda i: (i, OFF_Q // D)),
                  pl.BlockSpec((CHUNK, KV_W), lambda i: (prev(i), OFF_K // KV_W)),
                  pl.BlockSpec((CHUNK, KV_W), lambda i: (i, OFF_K // KV_W)),
                  pl.BlockSpec((CHUNK, KV_W), lambda i: (prev(i), OFF_VA // KV_W)),
                  pl.BlockSpec((CHUNK, KV_W), lambda i: (i, OFF_VA // KV_W)),
                  table(prev, 128), table(cur, 128), table(prev, 256), table(cur, 256),
                  pl.BlockSpec(memory_space=pltpu.SMEM)],
        out_specs=[pl.BlockSpec((CHUNK, D), lambda i: (i, 0)), pl.BlockSpec((CHUNK, D), lambda i: (i, 0)),
                   pl.BlockSpec((CHUNK, KV_W), lambda i: (i, 0)),
                   pl.BlockSpec((None, N_Q, CHUNK, CHUNK), lambda i: (i, 0, 0, 0)),
                   pl.BlockSpec((None, N_Q, CHUNK), lambda i: (i, 0, 0))],
        out_shape=[SDS((T, D), BF16), SDS((T, D), BF16), SDS((T, KV_W), BF16),
                   SDS((nb, N_Q, CHUNK, CHUNK), BF16), SDS((nb, N_Q, CHUNK), F32)],
        compiler_params=_params(1),
    )(proj, proj, proj, proj, proj, cos, cos, sin, sin, sinks)


def _fwd_mix(a, att, proj, x, wa, wb, wo, g1, g2):
    T = x.shape[0]
    tm = min(T, 512)
    half = D // 2

    def body(a_ref, att_ref, ga0, ga1, gb0, gb1, x_ref, wa_ref, wb_ref, wo_ref, g1_ref, g2_ref,
             mg_ref, a2_ref, b2_ref, mix_ref, x1_ref, hf_ref):
        a2 = _nn(a_ref[...], wa_ref[...])
        b2 = _nn(att_ref[...], wb_ref[...])
        ga = jnp.concatenate([ga0[...], ga1[...]], axis=1).astype(F32)
        gb = jnp.concatenate([gb0[...], gb1[...]], axis=1).astype(F32)
        merged = (_sigmoid(ga) * a2 + _sigmoid(gb) * b2).astype(BF16)
        a2_ref[...] = a2.astype(BF16)
        b2_ref[...] = b2.astype(BF16)
        mg_ref[...] = merged
        mix = _nn(merged, wo_ref[...])
        mix_ref[...] = mix
        _, mh = _rms_stats(mix)
        x1 = x_ref[...] + mh * g1_ref[...]
        x1_ref[...] = x1
        _, xh = _rms_stats(x1)
        hf_ref[...] = (xh * g2_ref[...]).astype(BF16)

    row = lambda i: (i, 0)
    const = lambda i: (0, 0)
    gspec = lambda off: pl.BlockSpec((tm, half), lambda i: (i, off // half))
    return pl.pallas_call(
        body, name="fwd_mix", grid=(T // tm,),
        in_specs=[pl.BlockSpec((tm, D), row), pl.BlockSpec((tm, D), row),
                  gspec(OFF_GA), gspec(OFF_GA + half), gspec(OFF_GB), gspec(OFF_GB + half),
                  pl.BlockSpec((tm, D), row), _resident((D, D)), _resident((D, D)),
                  _resident((D, D)), pl.BlockSpec((1, D), const), pl.BlockSpec((1, D), const)],
        out_specs=[pl.BlockSpec((tm, D), row)] * 6,
        out_shape=[SDS((T, D), BF16), SDS((T, D), BF16), SDS((T, D), BF16), SDS((T, D), F32), SDS((T, D), F32),
                   SDS((T, D), BF16)],
        compiler_params=_params(1),
    )(a, att, proj, proj, proj, proj, x, wa, wb, wo, g1, g2)


FF_SPLIT = N_DEV
FF_TILE = D_FF // FF_SPLIT


def _fwd_ff(hf, wfi3, wfo, x1, tgt, g3):
    T = hf.shape[0]
    tm = min(T, 512)

    def body(hf_ref, wfi_ref, wfo_ref, x1_ref, tgt_ref, g3_ref, f_ref, dy_ref, dff_ref, dg3_ref, loss_ref, r_s):
        @pl.when(pl.program_id(0) == 0)
        def _():
            dg3_ref[...] = jnp.zeros_like(dg3_ref)
            loss_ref[...] = jnp.zeros_like(loss_ref)

        hf_t = hf_ref[...]
        for s in range(FF_SPLIT):
            cols = slice(s * FF_TILE, (s + 1) * FF_TILE)
            f = _nn(hf_t, wfi_ref[s]).astype(BF16)
            f_ref[:, cols] = f
            rl = jnp.maximum(f.astype(F32), 0.0)
            r_s[:, cols] = (rl * rl).astype(BF16)
        r3, fh = _rms_stats(_nn(r_s[...], wfo_ref[...]))
        e = x1_ref[...] + fh * g3_ref[...] - tgt_ref[...]
        loss_ref[...] += jnp.sum(e * e) * (0.5 / D)
        dy = e * (1.0 / D)
        dy_ref[...] = dy
        dg3_ref[...] += _colsum(dy * fh)
        dff_ref[...] = _rms_bwd(dy, fh, r3, g3_ref[...]).astype(BF16)

    row = lambda i: (i, 0)
    const = lambda i: (0, 0)
    return pl.pallas_call(
        body, name="fwd_ff", grid=(T // tm,),
        in_specs=[pl.BlockSpec((tm, D), row), _resident((FF_SPLIT, D, FF_TILE)), _resident((D_FF, D)),
                  pl.BlockSpec((tm, D), row),
                  pl.BlockSpec((tm, D), row), pl.BlockSpec((1, D), const)],
        out_specs=[pl.BlockSpec((tm, D_FF), row), pl.BlockSpec((tm, D), row),
                   pl.BlockSpec((tm, D), row), pl.BlockSpec((1, D), const), pl.BlockSpec((1, 128), const)],
        out_shape=[SDS((T, D_FF), BF16), SDS((T, D), F32), SDS((T, D), BF16), SDS((1, D), F32), SDS((1, 128), F32)],
        scratch_shapes=[pltpu.VMEM((tm, D_FF), BF16)],
        compiler_params=_params(1),
    )(hf, wfi3, wfo, x1, tgt, g3)


def _bwd_ff(dff, f, wfi3, wfo, x1, dy, mix, g1, g2):
    T = dff.shape[0]
    tm = min(T, 512)

    def body(dff_ref, f_ref, wfi_ref, wfo_ref, x1_ref, dy_ref, mix_ref, g1_ref, g2_ref,
             df_ref, dx1_ref, dmix_ref, dg2_ref, dg1_ref):
        @pl.when(pl.program_id(0) == 0)
        def _():
            dg2_ref[...] = jnp.zeros_like(dg2_ref)
            dg1_ref[...] = jnp.zeros_like(dg1_ref)

        dff_t = dff_ref[...]
        dhf = None
        for s in range(FF_SPLIT):
            cols = slice(s * FF_TILE, (s + 1) * FF_TILE)
            dr = _nt(dff_t, wfo_ref[cols, :])
            df = (dr * (2.0 * jnp.maximum(f_ref[:, cols].astype(F32), 0.0))).astype(BF16)
            df_ref[:, cols] = df
            part = _nt(df, wfi_ref[s])
            dhf = part if dhf is None else dhf + part
        r2, xh = _rms_stats(x1_ref[...])
        dg2_ref[...] += _colsum(dhf * xh)
        dx1 = dy_ref[...] + _rms_bwd(dhf, xh, r2, g2_ref[...])
        dx1_ref[...] = dx1
        r1, mh = _rms_stats(mix_ref[...])
        dg1_ref[...] += _colsum(dx1 * mh)
        dmix_ref[...] = _rms_bwd(dx1, mh, r1, g1_ref[...]).astype(BF16)

    row = lambda i: (i, 0)
    const = lambda i: (0, 0)
    return pl.pallas_call(
        body, name="bwd_ff", grid=(T // tm,),
        in_specs=[pl.BlockSpec((tm, D), row), pl.BlockSpec((tm, D_FF), row),
                  _resident((FF_SPLIT, D, FF_TILE)), _resident((D_FF, D)),
                  pl.BlockSpec((tm, D), row), pl.BlockSpec((tm, D), row), pl.BlockSpec((tm, D), row),
                  pl.BlockSpec((1, D), const), pl.BlockSpec((1, D), const)],
        out_specs=[pl.BlockSpec((tm, D_FF), row), pl.BlockSpec((tm, D), row),
                   pl.BlockSpec((tm, D), row), pl.BlockSpec((1, D), const), pl.BlockSpec((1, D), const)],
        out_shape=[SDS((T, D_FF), BF16), SDS((T, D), F32), SDS((T, D), BF16), SDS((1, D), F32), SDS((1, D), F32)],
        compiler_params=_params(1),
    )(dff, f, wfi3, wfo, x1, dy, mix, g1, g2)


def _wgrad_ff(hf, df, f, dff):
    T = hf.shape[0]
    tt = min(T, 2048)
    slabs = 2
    wide = slabs * FF_TILE

    def body(hf_ref, df_ref, f_ref, dff_ref, dwfi_ref, dwfo_ref, acc_i, acc_o):
        t = pl.program_id(1)

        @pl.when(t == 0)
        def _():
            acc_i[...] = jnp.zeros_like(acc_i)
            acc_o[...] = jnp.zeros_like(acc_o)

        acc_i[...] += _tn(hf_ref[...], df_ref[...])
        rl = jnp.maximum(f_ref[...].astype(F32), 0.0)
        acc_o[...] += _tn((rl * rl).astype(BF16), dff_ref[...])

        @pl.when(t == T // tt - 1)
        def _():
            for s in range(slabs):
                dwfi_ref[s] = acc_i[:, s * FF_TILE:(s + 1) * FF_TILE].astype(BF16)
            dwfo_ref[...] = acc_o[...].astype(BF16)

    return pl.pallas_call(
        body, name="wgrad_ff", grid=(D_FF // wide, T // tt),
        in_specs=[pl.BlockSpec((tt, D), lambda p, t: (t, 0)), pl.BlockSpec((tt, wide), lambda p, t: (t, p)),
                  pl.BlockSpec((tt, wide), lambda p, t: (t, p)), pl.BlockSpec((tt, D), lambda p, t: (t, 0))],
        out_specs=[pl.BlockSpec((slabs, D, FF_TILE), lambda p, t: (p, 0, 0)), pl.BlockSpec((wide, D), lambda p, t: (p, 0))],
        out_shape=[SDS((FF_SPLIT, D, FF_TILE), BF16), SDS((D_FF, D), BF16)],
        scratch_shapes=[pltpu.VMEM((D, wide), F32), pltpu.VMEM((wide, D), F32)],
        compiler_params=_params(2),
    )(hf, df, f, dff)


def _bwd_mix(dmix, proj, a2, b2, merged, a, att, wo, wa, wb, after=None):
    T = dmix.shape[0]
    tm = min(T, 512)
    half = D // 2
    last = T // tm - 1

    def body(dmix_ref, ga0, ga1, gb0, gb1, a2_ref, b2_ref, mg_ref, a_ref, att_ref, wo_ref, wa_ref, wb_ref,
             dg_ref, da_ref, datt_ref, dwo_ref, dwa_ref, dwb_ref, acc, stage, sem):
        t = pl.program_id(0)

        @pl.when(t == 0)
        def _():
            acc[...] = jnp.zeros_like(acc)

        dmix_t = dmix_ref[...]
        dmg = _nt(dmix_t, wo_ref[...])
        sa = _sigmoid(jnp.concatenate([ga0[...], ga1[...]], axis=1).astype(F32))
        sb = _sigmoid(jnp.concatenate([gb0[...], gb1[...]], axis=1).astype(F32))
        da2 = (dmg * sa).astype(BF16)
        db2 = (dmg * sb).astype(BF16)
        dg_ref[:, :D] = (dmg * a2_ref[...].astype(F32) * (sa * (1.0 - sa))).astype(BF16)
        dg_ref[:, D:] = (dmg * b2_ref[...].astype(F32) * (sb * (1.0 - sb))).astype(BF16)
        da_ref[...] = _nt(da2, wa_ref[...]).astype(BF16)
        datt_ref[...] = _nt(db2, wb_ref[...]).astype(BF16)
        acc[0] += _tn(mg_ref[...], dmix_t)
        acc[1] += _tn(a_ref[...], da2)
        acc[2] += _tn(att_ref[...], db2)

        @pl.when(t == last)
        def _():
            for k, dw_ref in enumerate((dwo_ref, dwa_ref, dwb_ref)):
                stage[...] = acc[k].astype(BF16)
                out = pltpu.make_async_copy(stage, dw_ref, sem)
                out.start()
                out.wait()

    row = lambda i: (i, 0)
    gspec = lambda off: pl.BlockSpec((tm, half), lambda i: (i, off // half))
    body, dep_specs, deps = _after(body, 13, after)
    return pl.pallas_call(
        body, name="bwd_mix", grid=(T // tm,),
        in_specs=[pl.BlockSpec((tm, D), row), gspec(OFF_GA), gspec(OFF_GA + half), gspec(OFF_GB), gspec(OFF_GB + half)]
        + [pl.BlockSpec((tm, D), row)] * 5 + [_resident((D, D))] * 3 + dep_specs,
        out_specs=[pl.BlockSpec((tm, 2 * D), row), pl.BlockSpec((tm, D), row), pl.BlockSpec((tm, D), row)] + [_ANY] * 3,
        out_shape=[SDS((T, 2 * D), BF16), SDS((T, D), BF16), SDS((T, D), BF16)] + [SDS((D, D), BF16)] * 3,
        scratch_shapes=[pltpu.VMEM((3, D, D), F32), pltpu.VMEM((D, D), BF16), pltpu.SemaphoreType.DMA],
        compiler_params=_params(1),
    )(dmix, proj, proj, proj, proj, a2, b2, merged, a, att, wo, wa, wb, *deps)


def _bwd_attn(qr, kr, probs, psink, proj, cos, sin, datt, after=None):
    T = proj.shape[0]
    nb = T // CHUNK
    cur = lambda i: jnp.minimum(i, nb - 1)
    prev = lambda i: jnp.maximum(jnp.minimum(i, nb - 1) - 1, 0)

    def body(q_ref, kp_ref, kc_ref, vp_ref, vc_ref, cp_ref, cc_ref, sp_ref, sc_ref, p_ref, psink_ref, do_ref,
             dq_ref, dkv_ref, dsink_ref, carry_k, carry_v):
        i = pl.program_id(0)

        @pl.when(i == 0)
        def _():
            carry_k[...] = jnp.zeros_like(carry_k)
            carry_v[...] = jnp.zeros_like(carry_v)
            dsink_ref[...] = jnp.zeros_like(dsink_ref)

        @pl.when(i < nb)
        def _():
            prev_slot, _ = _fold_masks(i == 0)
            c_band, s_band = _band(cp_ref, cc_ref), _band(sp_ref, sc_ref)
            for j in range(KV_W // 128):
                cols = slice(j * 128, (j + 1) * 128)
                k_slab = _band(kp_ref, kc_ref, cols).astype(F32)
                v_slab = _band(vp_ref, vc_ref, cols).astype(F32)
                dk_slab = jnp.zeros((2 * CHUNK, 128), F32)
                dv_slab = jnp.zeros((2 * CHUNK, 128), F32)
                for g in (2 * j, 2 * j + 1):
                    k2 = _head_pair_operand(k_slab, g)
                    v2 = _head_pair_operand(v_slab, g)
                    pairs = [g * PAIRS_PER_KV + r for r in range(PAIRS_PER_KV)]
                    q_stack = jnp.concatenate([q_ref[:, pr * 128:(pr + 1) * 128] for pr in pairs], axis=0)
                    do_stack = jnp.concatenate([do_ref[:, pr * 128:(pr + 1) * 128] for pr in pairs], axis=0)
                    dp2 = _nt(v2, do_stack)
                    pcols, dscols = [], []
                    for r, pair in enumerate(pairs):
                        ps, dss = [], []
                        for e in range(2):
                            head = 2 * pair + e
                            p_b = p_ref[head]
                            p = p_b.astype(F32)
                            dp = _fold(dp2[e * 2 * CHUNK:(e + 1) * 2 * CHUNK, r * 128:(r + 1) * 128], prev_slot)
                            delta = jnp.sum(p * dp, axis=0, keepdims=True)
                            ps.append(_unfold(p_b, prev_slot))
                            dss.append(_unfold((p * (dp - delta)).astype(BF16), prev_slot))
                            dsink_ref[head:head + 1, :] -= psink_ref[head:head + 1, :] * delta
                        pcols.append(jnp.concatenate(ps, axis=0))
                        dscols.append(jnp.concatenate(dss, axis=0))
                    ds2 = jnp.concatenate(dscols, axis=1)
                    dq = _tn(ds2, k2) * (HEAD ** -0.5)
                    for r, pair in enumerate(pairs):
                        dq_ref[:, pair * 128:(pair + 1) * 128] = _rope_bwd(
                            dq[r * CHUNK:(r + 1) * CHUNK], cc_ref[...], sc_ref[...]).astype(BF16)
                    dk_slab = dk_slab + _head_pair_gradient(_nn(ds2, q_stack), g)
                    dv_slab = dv_slab + _head_pair_gradient(_nn(jnp.concatenate(pcols, axis=1), do_stack), g)
                dk_slab = _rope_bwd(dk_slab, c_band, s_band)
                vcols = slice(KV_W + j * 128, KV_W + (j + 1) * 128)
                dkv_ref[:, cols] = (carry_k[:, cols] + dk_slab[:CHUNK]).astype(BF16)
                dkv_ref[:, vcols] = (carry_v[:, cols] + dv_slab[:CHUNK]).astype(BF16)
                carry_k[:, cols] = dk_slab[CHUNK:]
                carry_v[:, cols] = dv_slab[CHUNK:]

        @pl.when(i == nb)
        def _():
            dkv_ref[:, :KV_W] = carry_k[...].astype(BF16)
            dkv_ref[:, KV_W:] = carry_v[...].astype(BF16)

    table = lambda which, width: pl.BlockSpec((CHUNK, width), lambda i: (which(i), 0))
    body, dep_specs, deps = _after(body, 12, after)
    return pl.pallas_call(
        body, name="bwd_attn", grid=(nb + 1,),
        in_specs=[pl.BlockSpec((CHUNK, D), lambda i: (cur(i), 0)),
                  pl.BlockSpec((CHUNK, KV_W), lambda i: (prev(i), 0)),
                  pl.BlockSpec((CHUNK, KV_W), lambda i: (cur(i), 0)),
                  pl.BlockSpec((CHUNK, KV_W), lambda i: (prev(i), OFF_VA // KV_W)),
                  pl.BlockSpec((CHUNK, KV_W), lambda i: (cur(i), OFF_VA // KV_W)),
                  table(prev, 128), table(cur, 128), table(prev, 256), table(cur, 256),
                  pl.BlockSpec((None, N_Q, CHUNK, CHUNK), lambda i: (cur(i), 0, 0, 0)),
                  pl.BlockSpec((None, N_Q, CHUNK), lambda i: (cur(i), 0, 0)),
                  pl.BlockSpec((CHUNK, D), lambda i: (cur(i), 0))] + dep_specs,
        out_specs=[pl.BlockSpec((CHUNK, D), lambda i: (cur(i), 0)),
                   pl.BlockSpec((CHUNK, 2 * KV_W), lambda i: (jnp.maximum(i - 1, 0), 0)),
                   pl.BlockSpec((N_Q, CHUNK), lambda i: (0, 0))],
        out_shape=[SDS((T, D), BF16), SDS((T, 2 * KV_W), BF16), SDS((N_Q, CHUNK), F32)],
        scratch_shapes=[pltpu.VMEM((CHUNK, KV_W), F32), pltpu.VMEM((CHUNK, KV_W), F32)],
        compiler_params=_params(1),
    )(qr, kr, kr, proj, proj, cos, cos, sin, sin, probs, psink, datt, *deps)


def _bwd_sgu(proj, tanhs, stats, da, lng, lnb, ws, bst):
    T = proj.shape[0]
    tc = min(T, 512)
    nsteps = T // tc

    def body(u_ref, vs_ref, t_ref, stat_ref, da_ref, lng_ref, lnb_ref, ws_ref, bst_ref,
             duv_ref, dws_ref, dbs_ref, dlng_ref, dlnb_ref, dvn_s, dgu_s, dmx_sum):
        i = pl.program_id(0)

        @pl.when(i == 0)
        def _():
            dws_ref[...] = jnp.zeros_like(dws_ref)
            dlng_ref[...] = jnp.zeros_like(dlng_ref)
            dlnb_ref[...] = jnp.zeros_like(dlnb_ref)
            dmx_sum[...] = jnp.zeros_like(dmx_sum)

        u, vs, gu, tu, tv, rstd, vhat, vn = _sgu_forward_replay(u_ref, vs_ref, t_ref, stat_ref, lng_ref, lnb_ref)
        da = da_ref[...].astype(F32)
        for g in range(GROUPS):
            wm = _masked_ws(ws_ref, g)
            cols = slice(g * CHUNK, (g + 1) * CHUNK)
            dws = jnp.zeros((CHUNK, CHUNK), F32)
            dsum = jnp.zeros((CHUNK, CHUNK), F32)
            for c in range(tc // CHUNK):
                rows = slice(c * CHUNK, (c + 1) * CHUNK)
                vn_cg = vn[rows, cols]
                mixed = _nn(wm, vn_cg) + bst_ref[:, g:g + 1]
                dgu_s[rows, cols] = da[rows, cols] * mixed
                dmx = da[rows, cols] * gu[rows, cols]
                dmxb = dmx.astype(BF16)
                dws = dws + _nt(dmxb, vn_cg)
                dsum = dsum + dmx
                dvn_s[rows, cols] = _tn(wm, dmxb)
            dws_ref[g] += dws
            dmx_sum[:, cols] += dsum
        dvn = dvn_s[...]
        dlng_ref[...] += _colsum(dvn * vhat)
        dlnb_ref[...] += _colsum(dvn)
        dvh = dvn * lng_ref[...]
        dgv = rstd * (dvh - jnp.mean(dvh, axis=-1, keepdims=True) - vhat * jnp.mean(dvh * vhat, axis=-1, keepdims=True))
        duv_ref[:, :D] = (dgu_s[...] * _gelu_grad(u, tu)).astype(BF16)
        duv_ref[:, D:] = (dgv * _gelu_grad(vs, tv)).astype(BF16)

        @pl.when(i == nsteps - 1)
        def _():
            row = lax.broadcasted_iota(jnp.int32, (CHUNK, CHUNK), 0)
            col = lax.broadcasted_iota(jnp.int32, (CHUNK, CHUNK), 1)
            for g in range(GROUPS):
                dws_ref[g] = jnp.where(row >= col, dws_ref[g], 0.0)
                dbs_ref[g:g + 1, :] = _colsum(dmx_sum[:, g * CHUNK:(g + 1) * CHUNK].T)

    const2 = lambda i: (0, 0)
    return pl.pallas_call(
        body, name="bwd_sgu", grid=(nsteps,),
        in_specs=[pl.BlockSpec((tc, D), lambda i: (i, 0)), pl.BlockSpec((tc, D), lambda i: (i, 1)),
                  pl.BlockSpec((tc, 2 * D), lambda i: (i, 0)), pl.BlockSpec((tc, 128), lambda i: (i, 0)),
                  pl.BlockSpec((tc, D), lambda i: (i, 0)), pl.BlockSpec((1, D), const2), pl.BlockSpec((1, D), const2),
                  pl.BlockSpec((GROUPS, CHUNK, CHUNK), lambda i: (0, 0, 0)), pl.BlockSpec((CHUNK, GROUPS), const2)],
        out_specs=[pl.BlockSpec((tc, 2 * D), lambda i: (i, 0)), pl.BlockSpec((GROUPS, CHUNK, CHUNK), lambda i: (0, 0, 0)),
                   pl.BlockSpec((GROUPS, CHUNK), const2), pl.BlockSpec((1, D), const2), pl.BlockSpec((1, D), const2)],
        out_shape=[SDS((T, 2 * D), BF16), SDS((GROUPS, CHUNK, CHUNK), F32), SDS((GROUPS, CHUNK), F32),
                   SDS((1, D), F32), SDS((1, D), F32)],
        scratch_shapes=[pltpu.VMEM((tc, D), F32), pltpu.VMEM((tc, D), F32), pltpu.VMEM((CHUNK, D), F32)],
        compiler_params=_params(1),
    )(proj, proj, tanhs, stats, da, lng, lnb, ws, bst)


IN_SEG_WIDTHS = (2 * D, D, 2 * N_KV * HEAD, 2 * D)


def _resident(shape):
    return pl.BlockSpec(shape, lambda *_: (0,) * len(shape), pipeline_mode=pl.Buffered(1))


def _bwd_in(duv, dq, dkv, dg, win_t, x, dx1, g0, after=None):
    T = x.shape[0]
    tm = min(T, 512)

    def body(duv_ref, dq_ref, dkv_ref, dg_ref, w_ref, x_ref, dx1_ref, g0_ref, gx_ref, dg0_ref):
        @pl.when(pl.program_id(0) == 0)
        def _():
            dg0_ref[...] = jnp.zeros_like(dg0_ref)

        dh, off = None, 0
        for ref, width in zip((duv_ref, dq_ref, dkv_ref, dg_ref), IN_SEG_WIDTHS):
            part = _nn(ref[...], w_ref[off:off + width, :])
            dh = part if dh is None else dh + part
            off += width
        r0, xh = _rms_stats(x_ref[...])
        dg0_ref[...] += _colsum(dh * xh)
        gx_ref[...] = dx1_ref[...] + _rms_bwd(dh, xh, r0, g0_ref[...])

    row = lambda i: (i, 0)
    body, dep_specs, deps = _after(body, 8, after)
    return pl.pallas_call(
        body, name="bwd_in", grid=(T // tm,),
        in_specs=[pl.BlockSpec((tm, w), row) for w in IN_SEG_WIDTHS] + [
            _resident((IN_W, D)), pl.BlockSpec((tm, D), row), pl.BlockSpec((tm, D), row),
            pl.BlockSpec((1, D), lambda i: (0, 0))] + dep_specs,
        out_specs=[pl.BlockSpec((tm, D), row), pl.BlockSpec((1, D), lambda i: (0, 0))],
        out_shape=[SDS((T, D), F32), SDS((1, D), F32)],
        compiler_params=_params(1),
    )(duv, dq, dkv, dg, win_t, x, dx1, g0, *deps)


def _wgrad_rows(h, segs, first_row, into, name):
    T = h.shape[0]
    tt = min(T, 2048)
    widths = [s.shape[1] for s in segs]
    rows = sum(widths)
    n_in = 1 + len(segs) + (into is not None)

    def body(*refs):
        h_ref, seg_refs = refs[0], refs[1:1 + len(segs)]
        dw_ref, acc, stage, sem = refs[n_in], refs[n_in + 1], refs[n_in + 2], refs[n_in + 3]
        t = pl.program_id(0)

        @pl.when(t == 0)
        def _():
            acc[...] = jnp.zeros_like(acc)

        off = 0
        for ref, width in zip(seg_refs, widths):
            acc[off:off + width, :] += _tn(ref[...], h_ref[...])
            off += width

        @pl.when(t == T // tt - 1)
        def _():
            stage[...] = acc[...].astype(BF16)
            out = pltpu.make_async_copy(stage, dw_ref.at[pl.ds(first_row, rows)], sem)
            out.start()
            out.wait()

    row = lambda t: (t, 0)
    return pl.pallas_call(
        body, name=name, grid=(T // tt,),
        in_specs=[pl.BlockSpec((tt, D), row)] + [pl.BlockSpec((tt, w), row) for w in widths] + [_ANY] * (into is not None),
        out_specs=_ANY,
        out_shape=SDS((IN_W, D), BF16),
        input_output_aliases={} if into is None else {n_in - 1: 0},
        scratch_shapes=[pltpu.VMEM((rows, D), F32), pltpu.VMEM((rows, D), BF16), pltpu.SemaphoreType.DMA],
        compiler_params=_params(1),
    )(h, *segs, *([] if into is None else [into]))


def _place():
    x, y, c = lax.axis_index("x"), lax.axis_index("y"), lax.axis_index("c")
    return x, y, c, 4 * x + 2 * y + c


def _peers(x, y, c):
    out = []
    for mask in range(1, N_DEV):
        px = 1 - x if mask & 4 else x
        py = 1 - y if mask & 2 else y
        pc = 1 - c if mask & 1 else c
        out.append(((px, py, pc), 4 * px + 2 * py + pc))
    return out


def _all_to_all(arrays, gather, name, after=None):
    n = len(arrays)

    def body(*refs):
        ins, outs = refs[:n], refs[n:2 * n]
        send_sems, recv_sems, local_sems = refs[2 * n:]
        x, y, c, me = _place()
        local, sends, recvs = [], [], []
        for a in range(n):
            src_own = ins[a] if gather[a] else ins[a].at[me]
            local.append(pltpu.make_async_copy(src_own, outs[a].at[me], local_sems.at[a]))
            for k, (peer, pid) in enumerate(_peers(x, y, c)):
                sem = a * (N_DEV - 1) + k
                src = ins[a] if gather[a] else ins[a].at[pid]
                sends.append(pltpu.make_async_remote_copy(
                    src_ref=src, dst_ref=outs[a].at[me], send_sem=send_sems.at[sem], recv_sem=recv_sems.at[sem],
                    device_id=peer, device_id_type=MESH))
                recvs.append(pltpu.make_async_remote_copy(
                    src_ref=src, dst_ref=outs[a].at[pid], send_sem=send_sems.at[sem], recv_sem=recv_sems.at[sem],
                    device_id=peer, device_id_type=MESH))
        for cp in local + sends:
            cp.start()
        for cp in recvs:
            cp.wait_recv()
        for cp in sends:
            cp.wait_send()
        for cp in local:
            cp.wait()

    out_shape = [SDS((N_DEV,) + a.shape if gt else a.shape, a.dtype) for a, gt in zip(arrays, gather)]
    nsem = n * (N_DEV - 1)
    body, dep_specs, deps = _after(body, n, after)
    return pl.pallas_call(
        body, name=name,
        in_specs=[pl.BlockSpec(memory_space=pl.ANY)] * n + dep_specs,
        out_specs=[pl.BlockSpec(memory_space=pl.ANY)] * n,
        out_shape=out_shape,
        scratch_shapes=[pltpu.SemaphoreType.DMA((nsem,)), pltpu.SemaphoreType.DMA((nsem,)), pltpu.SemaphoreType.DMA((n,))],
    )(*arrays, *deps)


_HBM = pl.BlockSpec(memory_space=pltpu.HBM)
_SEM = pl.BlockSpec(memory_space=pltpu.SEMAPHORE)
_EFFECT = pltpu.SideEffectType.DATAFLOW_SIDE_EFFECTING
GATHER = "gather"
SCATTER = "scatter"
SPREAD = "spread"


def _zone_shape(a, mode):
    if mode == GATHER:
        return (N_DEV,) + a.shape
    return (N_DEV - 1,) + (a.shape[1:] if mode == SCATTER else a.shape)


def _start_copies(arrays, modes, name, after=None):
    n = len(arrays)
    zones = [lax.empty(_zone_shape(a, m), a.dtype) for a, m in zip(arrays, modes)]

    def body(*refs):
        ins, lands = refs[:n], refs[n:2 * n]
        send_sems, recv_sems = refs[-2 * n - 3], refs[-2 * n - 2]
        token = refs[-1]
        x, y, c, me = _place()
        for a in range(n):
            for k, (peer, pid) in enumerate(_peers(x, y, c)):
                src = ins[a].at[pid] if modes[a] == SCATTER else ins[a]
                dst = lands[a].at[me] if modes[a] == GATHER else lands[a].at[k]
                pltpu.make_async_remote_copy(src_ref=src, dst_ref=dst, send_sem=send_sems.at[a], recv_sem=recv_sems.at[a],
                                             device_id=peer, device_id_type=MESH).start()
            if modes[a] == GATHER:
                pltpu.make_async_remote_copy(src_ref=ins[a], dst_ref=lands[a].at[me], send_sem=send_sems.at[a],
                                             recv_sem=recv_sems.at[a], device_id=(x, y, c), device_id_type=MESH).start()
        token[...] = jnp.zeros_like(token)

    hbm = lambda a: pltpu.HBM(a.shape, a.dtype)
    sems = pltpu.SemaphoreType.DMA((n,))
    extra = [] if after is None else [after]
    operands = [pltpu.with_memory_space_constraint(a, pltpu.HBM) for a in list(arrays) + zones]
    res = pl.pallas_call(
        body, name=name,
        out_shape=(sems, sems, *[hbm(a) for a in arrays], *[hbm(z) for z in zones], SDS((8, 128), F32)),
        in_specs=[_HBM] * (2 * n) + [_ANY] * len(extra),
        out_specs=(_SEM, _SEM, *[_HBM] * (2 * n), pl.BlockSpec(memory_space=pltpu.VMEM)),
        input_output_aliases={i: 2 + i for i in range(2 * n)},
        compiler_params=pltpu.CompilerParams(has_side_effects=_EFFECT),
    )(*operands, *extra)
    return res[0], res[1], list(res[2:2 + n]), list(res[2 + n:2 + 2 * n]), res[-1]


def _wait_copies(started, after, name, count=N_DEV - 1):
    send_sems, recv_sems, thru, zones, _ = started
    nt, nz = len(thru), len(zones)

    def body(*refs):
        lands = refs[nt:nt + nz]
        send_ref, recv_ref = refs[nt + nz], refs[nt + nz + 1]
        x, y, c, _ = _place()
        for a in range(nz):
            blocks = lands[a].at[pl.ds(0, count)]
            cp = pltpu.make_async_remote_copy(src_ref=blocks, dst_ref=blocks, send_sem=send_ref.at[a], recv_sem=recv_ref.at[a],
                                              device_id=(x, y, 1 - c), device_id_type=MESH)
            cp.wait_send()
            cp.wait_recv()

    hbm = lambda a: pltpu.HBM(a.shape, a.dtype)
    res = pl.pallas_call(
        body, name=name,
        out_shape=tuple(hbm(a) for a in thru + zones),
        in_specs=[_HBM] * (nt + nz) + [_SEM, _SEM, _ANY],
        out_specs=tuple([_HBM] * (nt + nz)),
        input_output_aliases={i: i for i in range(nt + nz)},
        compiler_params=pltpu.CompilerParams(has_side_effects=_EFFECT),
    )(*thru, *zones, send_sems, recv_sems, after)
    return list(res[:nt]), list(res[nt:])


def _split_start(body, arrays, zones, name, after):
    n = len(arrays) + len(zones)
    hbm = lambda a: pltpu.HBM(a.shape, a.dtype)
    sems = pltpu.SemaphoreType.DMA((max(len(zones), 1),))
    extra = [] if after is None else [after]
    operands = [pltpu.with_memory_space_constraint(a, pltpu.HBM) for a in list(arrays) + list(zones)]
    res = pl.pallas_call(
        body, name=name,
        out_shape=(sems, sems, *[hbm(a) for a in operands], SDS((8, 128), F32)),
        in_specs=[_HBM] * n + [_ANY] * len(extra),
        out_specs=(_SEM, _SEM, *[_HBM] * n, pl.BlockSpec(memory_space=pltpu.VMEM)),
        input_output_aliases={i: 2 + i for i in range(n)},
        compiler_params=pltpu.CompilerParams(has_side_effects=_EFFECT),
    )(*operands, *extra)
    return res[0], res[1], list(res[2:2 + len(arrays)]), list(res[2 + len(arrays):2 + n]), res[-1]


def _gather_first_leg(shard, name, after=None):
    zone = lax.empty((N_DEV,) + shard.shape, shard.dtype)
    extra = 0 if after is None else 1

    def body(*refs):
        src, land = refs[0], refs[1]
        send_sem, recv_sem, token = refs[2 + extra], refs[3 + extra], refs[-1]
        x, y, c, me = _place()
        for peer in ((x, y, c), (x, y, 1 - c), (1 - x, y, c), (x, 1 - y, c), (1 - x, 1 - y, c)):
            pltpu.make_async_remote_copy(src_ref=src, dst_ref=land.at[me], send_sem=send_sem.at[0], recv_sem=recv_sem.at[0],
                                         device_id=peer, device_id_type=MESH).start()
        token[...] = jnp.zeros_like(token)

    return _split_start(body, [shard], [zone], name, after)


def _gather_second_leg(zone, name, after=None):
    extra = 0 if after is None else 1

    def body(*refs):
        land = refs[0]
        send_sem, recv_sem, token = refs[1 + extra], refs[2 + extra], refs[-1]
        x, y, c, _ = _place()
        for px, py in ((1 - x, y), (x, 1 - y), (1 - x, 1 - y)):
            slot = 4 * px + 2 * py + c
            pltpu.make_async_remote_copy(src_ref=land.at[slot], dst_ref=land.at[slot], send_sem=send_sem.at[0],
                                         recv_sem=recv_sem.at[0], device_id=(x, y, 1 - c), device_id_type=MESH).start()
        token[...] = jnp.zeros_like(token)

    return _split_start(body, [], [zone], name, after)


UPDATE_BLOCK_ELEMS = 384 * 1024


def _update_rows(R, C):
    fits = [t for t in range(8, R + 1, 8) if R % t == 0 and t * C <= UPDATE_BLOCK_ELEMS]
    whole = [t for t in fits if t % 16 == 0]
    return max(whole or fits)


def _adamw_math(g, w, m, v):
    m2 = ADAM_B1 * m + (1.0 - ADAM_B1) * g
    v2 = ADAM_B2 * v + (1.0 - ADAM_B2) * (g * g)
    m_hat = m2 / (1.0 - ADAM_B1 ** ADAM_STEP)
    v_hat = v2 / (1.0 - ADAM_B2 ** ADAM_STEP)
    delta = -ADAM_LR * (m_hat / (jnp.sqrt(v_hat) + ADAM_EPS) + ADAM_WD * w)
    return delta, m2, v2


def _sum_adamw(parts, w, m, v, name):
    R, C = w.shape
    tr = _update_rows(R, C)

    def body(p_ref, w_ref, m_ref, v_ref, g_ref, d_ref, m2_ref, v2_ref):
        g = p_ref[0]
        for k in range(1, N_DEV):
            g = g + p_ref[k]
        g_ref[...] = g
        d_ref[...], m2_ref[...], v2_ref[...] = _adamw_math(g, w_ref[...], m_ref[...], v_ref[...])

    blk = pl.BlockSpec((tr, C), lambda i: (i, 0))
    return pl.pallas_call(
        body, name=name, grid=(R // tr,),
        in_specs=[pl.BlockSpec((N_DEV, tr, C), lambda i: (0, i, 0)), blk, blk, blk],
        out_specs=[blk] * 4,
        out_shape=[SDS((R, C), F32)] * 4,
        compiler_params=_params(1),
    )(parts, w, m, v)


def _sum_adamw_peers(me, own, parts, w, m, v, name, replicated, also_rows=None):
    R, C = w.shape
    tr = _update_rows(R, C)
    assert also_rows is None or tr == R

    def body(me_ref, own_ref, p_ref, w_ref, m_ref, v_ref, g_ref, d_ref, m2_ref, v2_ref, *extra):
        if replicated:
            mine = me_ref[0]
            g = None
            for j in range(N_DEV):
                k = jnp.maximum(jnp.bitwise_xor(mine, j) - 1, 0)
                term = jnp.where(mine == j, own_ref[...], p_ref[k])
                g = term if g is None else g + term
        else:
            g = own_ref[...].astype(F32)
            for k in range(N_DEV - 1):
                g = g + p_ref[k].astype(F32)
        results = (g,) + _adamw_math(g, w_ref[...], m_ref[...], v_ref[...])
        for ref, val in zip((g_ref, d_ref, m2_ref, v2_ref), results):
            ref[...] = val
        for ref, val in zip(extra, results):
            ref[...] = val[also_rows[0]:also_rows[1]]

    blk = pl.BlockSpec((tr, C), lambda i, me_ref: (i, 0))
    own_spec = blk if replicated else pl.BlockSpec((None, tr, C), lambda i, me_ref: (me_ref[0], i, 0))
    n_also = 0 if also_rows is None else also_rows[1] - also_rows[0]
    also_specs = [pl.BlockSpec((n_also, C), lambda i, me_ref: (0, 0))] * (4 if also_rows else 0)
    return pl.pallas_call(
        body, name=name,
        grid_spec=pltpu.PrefetchScalarGridSpec(
            num_scalar_prefetch=1, grid=(R // tr,),
            in_specs=[own_spec, pl.BlockSpec((N_DEV - 1, tr, C), lambda i, me_ref: (0, i, 0)), blk, blk, blk],
            out_specs=[blk] * 4 + also_specs),
        out_shape=[SDS((R, C), F32)] * 4 + [SDS((n_also, C), F32)] * len(also_specs),
        compiler_params=_params(1),
    )(me, own, parts, w, m, v)


SMALL = ("ln_v_gain", "ln_v_bias", "w_spatial", "b_spatial", "sinks", "norm_mix_post", "norm_ff_pre", "norm_ff_post")
SMALL_ROWS = {"ln_v_gain": 8, "ln_v_bias": 8, "w_spatial": 1024, "b_spatial": 8, "sinks": 8,
              "norm_mix_post": 8, "norm_ff_pre": 8, "norm_ff_post": 8}
SMALL_PACK_ROWS = 1152


def _pack_small(vals):
    rows = []
    for name in SMALL:
        flat = vals[name].reshape(-1)
        pad = SMALL_ROWS[name] * 128 - flat.shape[0]
        if pad:
            flat = jnp.concatenate([flat, jnp.zeros((pad,), F32)])
        rows.append(flat.reshape(SMALL_ROWS[name], 128))
    rows.append(jnp.zeros((SMALL_PACK_ROWS - sum(SMALL_ROWS.values()), 128), F32))
    return jnp.concatenate(rows, axis=0)


def _unpack_small(packed, shapes):
    out, r = {}, 0
    for name in SMALL:
        n = 1
        for s in shapes[name]:
            n *= s
        out[name] = packed[r:r + SMALL_ROWS[name]].reshape(-1)[:n].reshape(shapes[name])
        r += SMALL_ROWS[name]
    return out


def _rope_rows():
    d = jnp.arange(128) % HEAD
    inv = ROPE_THETA ** (-(2.0 * (d % (ROPE // 2))).astype(F32) / ROPE)
    invf = jnp.where(d < ROPE, inv, 0.0).astype(F32).reshape(1, 128)
    sgn = jnp.where(d < ROPE // 2, -1.0, jnp.where(d < ROPE, 1.0, 0.0)).astype(F32).reshape(1, 128)
    return invf, sgn


def kernel(x, positions, w_in, ln_v_gain, ln_v_bias, w_spatial, b_spatial, sinks, w_a, w_b, w_o, norm_mix_pre, norm_mix_post, w_ff_in, w_ff_out, norm_ff_pre, norm_ff_post, loss_target, m_w_in, m_ln_v_gain, m_ln_v_bias, m_w_spatial, m_b_spatial, m_sinks, m_w_a, m_w_b, m_w_o, m_norm_mix_pre, m_norm_mix_post, m_w_ff_in, m_w_ff_out, m_norm_ff_pre, m_norm_ff_post, v_w_in, v_ln_v_gain, v_ln_v_bias, v_w_spatial, v_b_spatial, v_sinks, v_w_a, v_w_b, v_w_o, v_norm_mix_pre, v_norm_mix_post, v_w_ff_in, v_w_ff_out, v_norm_ff_pre, v_norm_ff_post):
    given = dict(locals())
    T = x.shape[1]
    xt = x[0]
    tgt = loss_target[0]
    bst = b_spatial[0].T
    ws = w_spatial[0]

    me = 4 * lax.axis_index("x") + 2 * lax.axis_index("y") + lax.axis_index("c")
    me_arr = me.astype(jnp.int32).reshape(1)

    rest = ("w_a", "w_b", "w_o", "w_ff_in", "w_ff_out")
    shard = {n: given[n][0].astype(BF16) for n in rest}
    g_one = _gather_first_leg(w_in[0].T.astype(BF16), "gather_in_start")
    cos, sin = _rope_tables(positions.astype(F32).reshape(T, 1), *_rope_rows(), after=g_one[-1])
    small_state = [_pack_small({n: given[k + n] for n in SMALL}) for k in ("", "m_", "v_")]
    h = _rms_pre(xt, norm_mix_pre, after=[cos, *small_state, *[shard[n] for n in rest]])
    _, (win8,) = _wait_copies(g_one, h, "gather_in_wait", count=5)
    g_two = _gather_second_leg(win8, "gather_in_pass_start")
    g_rest = _start_copies([shard[n] for n in rest], [GATHER] * len(rest), "gather_rest_start", after=g_two[-1])
    _, (win8,) = _wait_copies(g_two, g_rest[-1], "gather_in_pass_wait", count=3)
    win = win8.reshape(IN_W, D)

    proj = _fwd_in(h, win)
    att, qr, kr, probs, psink = _fwd_attn(proj, cos, sin, sinks[0])
    a, tanhs, ln_stats = _fwd_sgu(proj, ln_v_gain, ln_v_bias, ws, bst, after=att)
    gw = dict(zip(rest, _wait_copies(g_rest, a, "gather_rest_wait", count=N_DEV)[1]))
    wa, wb, wo = (gw[n].reshape(D, D) for n in ("w_a", "w_b", "w_o"))
    wfi3 = gw["w_ff_in"]
    wfo = gw["w_ff_out"].reshape(D_FF, D)
    merged, a2, b2, mix, x1, hf = _fwd_mix(a, att, proj, xt, wa, wb, wo, norm_mix_post, norm_ff_pre)
    f, dy, dff, dg3, loss_part = _fwd_ff(hf, wfi3, wfo, x1, tgt, norm_ff_post)

    df, dx1, dmix, dg2, dg1 = _bwd_ff(dff, f, wfi3, wfo, x1, dy, mix, norm_mix_post, norm_ff_pre)
    dwfi3, dwfo = _wgrad_ff(hf, df, f, dff)
    own_ff = [dwfi3, dwfo.reshape(N_DEV, D_FF // N_DEV, D)]
    x_ff = _start_copies(own_ff, [SCATTER] * 2, "exchange_ff_start")
    dgate, da, datt, dwo, dwa, dwb = _bwd_mix(dmix, proj, a2, b2, merged, a, att, wo, wa, wb, after=x_ff[-1])
    own_mix = [g.reshape(N_DEV, D // N_DEV, D) for g in (dwa, dwb, dwo)]
    x_mix = _start_copies(own_mix, [SCATTER] * 3, "exchange_mix_start")
    dq, dkv, dsink = _bwd_attn(qr, kr, probs, psink, proj, cos, sin, datt, after=x_mix[-1])
    duv, dws, dbs, dlng, dlnb = _bwd_sgu(proj, tanhs, ln_stats, da, ln_v_gain, ln_v_bias, ws, bst)
    small_grads = {"ln_v_gain": dlng, "ln_v_bias": dlnb, "w_spatial": dws, "b_spatial": dbs, "sinks": jnp.sum(dsink, axis=1),
                   "norm_mix_post": dg1, "norm_ff_pre": dg2, "norm_ff_post": dg3}
    x_small = _start_copies([_pack_small(small_grads)], [SPREAD], "exchange_small_start")
    dwin = _wgrad_rows(h, [dgate], sum(IN_SEG_WIDTHS[:3]), None, "wgrad_in_gates")
    dwin = _wgrad_rows(h, [duv], 0, dwin, "wgrad_in_uv")
    dwin = _wgrad_rows(h, [dq, dkv], IN_SEG_WIDTHS[0], dwin, "wgrad_in_qkv")
    own_in = [dwin.reshape(N_DEV, IN_W // N_DEV, D)]
    x_in = _start_copies(own_in, [SCATTER], "exchange_in_start", after=x_small[-1])
    grad_x, dg0 = _bwd_in(duv, dq, dkv, dgate, win, xt, dx1, norm_mix_pre, after=x_in[-1])

    results = {}

    def update(n, own, parts, transposed=False):
        state = [given[k + n][0].T if transposed else given[k + n][0] for k in ("", "m_", "v_")]
        res = _sum_adamw_peers(me_arr, own, parts, *state, "adamw_" + n, False)
        results[n] = [(r.T if transposed else r).reshape(given[n].shape) for r in res]

    own_ff, p_ff = _wait_copies(x_ff, grad_x, "exchange_ff_wait")
    update("w_ff_in", own_ff[0], p_ff[0])
    update("w_ff_out", own_ff[1], p_ff[1])
    own_mix, p_mix = _wait_copies(x_mix, results["w_ff_out"][0], "exchange_mix_wait")
    for n, own, parts in zip(("w_a", "w_b", "w_o"), own_mix, p_mix):
        update(n, own, parts)
    tail = jnp.concatenate([dg0.reshape(8, 128), jnp.tile(loss_part, (8, 1))], axis=0)
    (tail_all,) = _all_to_all([tail], [True], "exchange_tail", after=results["w_o"][0])
    dg0_all = tail_all[:, :8]
    own_small, p_small = _wait_copies(x_small, tail_all, "exchange_small_wait")
    own_in, p_in = _wait_copies(x_in, p_small[0], "exchange_in_wait")
    update("w_in", own_in[0], p_in[0], transposed=True)
    first = sum(SMALL_ROWS[n] for n in SMALL[:SMALL.index("w_spatial")])
    packed = _sum_adamw_peers(me_arr, own_small[0], p_small[0], *small_state, "adamw_small", True,
                              also_rows=(first, first + SMALL_ROWS["w_spatial"]))
    shapes = {n: given[n].shape for n in SMALL}
    unpacked = [_unpack_small(p, shapes) for p in packed[:4]]
    for n in SMALL:
        results[n] = [u[n] for u in unpacked]
    results["w_spatial"] = [r.reshape(w_spatial.shape) for r in packed[4:]]
    n = "norm_mix_pre"
    results[n] = [r.reshape(given[n].shape) for r in _sum_adamw(
        dg0_all, given[n].reshape(8, 128), given["m_" + n].reshape(8, 128), given["v_" + n].reshape(8, 128), "adamw_" + n)]

    loss = jnp.sum(tail_all[:, 8, 0])
    order = ("w_in", "ln_v_gain", "ln_v_bias", "w_spatial", "b_spatial", "sinks", "w_a", "w_b", "w_o", "norm_mix_pre",
             "norm_mix_post", "w_ff_in", "w_ff_out", "norm_ff_pre", "norm_ff_post")
    out = [loss, grad_x.reshape(x.shape)]
    for k in range(4):
        out += [results[n][k] for n in order]
    return tuple(out)
```

```python
import jax
import jax.numpy as jnp
from jax import lax
from jax.experimental import pallas as pl
from jax.experimental.pallas import tpu as pltpu

F32 = jnp.float32
BF16 = jnp.bfloat16

N_DEV = 8
D = 1024
D_FF = 4096
IN_W = 5632
CHUNK = 128
GROUPS = 8
HEAD = 64
N_Q = 16
N_KV = 4
ROPE = 16
ROPE_THETA = 500000.0
EPS = 1e-6
OFF_Q, OFF_K, OFF_VA, OFF_GA, OFF_GB = 2048, 3072, 3328, 3584, 4608

ADAM_LR = 0.001
ADAM_B1 = 0.9
ADAM_B2 = 0.999
ADAM_EPS = 1e-08
ADAM_WD = 0.01
ADAM_STEP = 10

VMEM_LIMIT = 62 * 1024 * 1024

SDS = jax.ShapeDtypeStruct
MESH = pl.DeviceIdType.MESH


def _params(n_axes):
    return pltpu.CompilerParams(dimension_semantics=("arbitrary",) * n_axes, vmem_limit_bytes=VMEM_LIMIT)


def _nt(a, b):
    return lax.dot_general(a, b, (((1,), (1,)), ((), ())), preferred_element_type=F32)


def _tn(a, b):
    return lax.dot_general(a, b, (((0,), (0,)), ((), ())), preferred_element_type=F32)


def _nn(a, b):
    return jnp.dot(a, b, preferred_element_type=F32)


def _gelu(x):
    t = jnp.tanh(0.7978845608028654 * (x + 0.044715 * (x * x * x)))
    return 0.5 * x * (1.0 + t), t


def _gelu_grad(x, t):
    return 0.5 * (1.0 + t) + 0.5 * x * (1.0 - t * t) * (0.7978845608028654 * (1.0 + 3.0 * 0.044715 * x * x))


def _sigmoid(x):
    return 1.0 / (1.0 + jnp.exp(-x))


def _rms_stats(v):
    r = lax.rsqrt(jnp.mean(v * v, axis=-1, keepdims=True) + EPS)
    return r, v * r


def _rms_bwd(d, vhat, r, g):
    gd = g * d
    return r * (gd - vhat * jnp.mean(gd * vhat, axis=-1, keepdims=True))


def _colsum(v):
    return jnp.sum(v, axis=0, keepdims=True)


_ANY = pl.BlockSpec(memory_space=pl.ANY)


def _after(body, n_in, after):
    if after is None:
        return body, [], []
    deps = list(after) if isinstance(after, (list, tuple)) else [after]

    def ordered(*refs):
        return body(*refs[:n_in], *refs[n_in + len(deps):])

    return ordered, [_ANY] * len(deps), deps


def _rms_pre(x, g0, after=None):
    T = x.shape[0]
    tm = min(T, 1024)

    def body(x_ref, g_ref, h_ref):
        _, xh = _rms_stats(x_ref[...])
        h_ref[...] = (xh * g_ref[...]).astype(BF16)

    body, dep_specs, deps = _after(body, 2, after)
    return pl.pallas_call(
        body, name="rms_pre", grid=(T // tm,),
        in_specs=[pl.BlockSpec((tm, D), lambda i: (i, 0)), pl.BlockSpec((1, D), lambda i: (0, 0))] + dep_specs,
        out_specs=pl.BlockSpec((tm, D), lambda i: (i, 0)),
        out_shape=SDS((T, D), BF16),
        compiler_params=_params(1),
    )(x, g0, *deps)


def _fwd_in(h, win_t):
    T = h.shape[0]
    tm, tn = min(T, 1024), 1408

    def body(h_ref, w_ref, p_ref):
        for j in range(IN_W // tn):
            cols = slice(j * tn, (j + 1) * tn)
            p_ref[:, cols] = _nt(h_ref[...], w_ref[cols, :]).astype(BF16)

    return pl.pallas_call(
        body, name="fwd_in", grid=(T // tm,),
        in_specs=[pl.BlockSpec((tm, D), lambda i: (i, 0)), _resident((IN_W, D))],
        out_specs=pl.BlockSpec((tm, IN_W), lambda i: (i, 0)),
        out_shape=SDS((T, IN_W), BF16),
        compiler_params=_params(1),
    )(h, win_t)


def _sgu_forward_parts(u_ref, vs_ref, lng_ref, lnb_ref):
    u = u_ref[...].astype(F32)
    vs = vs_ref[...].astype(F32)
    gu, tu = _gelu(u)
    gv, tv = _gelu(vs)
    mu = jnp.mean(gv, axis=-1, keepdims=True)
    dv = gv - mu
    rstd = lax.rsqrt(jnp.mean(dv * dv, axis=-1, keepdims=True) + EPS)
    vhat = dv * rstd
    vn = (vhat * lng_ref[...] + lnb_ref[...]).astype(BF16)
    return gu, tu, tv, mu, rstd, vn


def _sgu_forward_replay(u_ref, vs_ref, t_ref, stat_ref, lng_ref, lnb_ref):
    u = u_ref[...].astype(F32)
    vs = vs_ref[...].astype(F32)
    tu = t_ref[:, :D].astype(F32)
    tv = t_ref[:, D:].astype(F32)
    gu = 0.5 * u * (1.0 + tu)
    rstd = stat_ref[:, 1:2]
    vhat = (0.5 * vs * (1.0 + tv) - stat_ref[:, 0:1]) * rstd
    vn = (vhat * lng_ref[...] + lnb_ref[...]).astype(BF16)
    return u, vs, gu, tu, tv, rstd, vhat, vn


def _masked_ws(ws_ref, g):
    row = lax.broadcasted_iota(jnp.int32, (CHUNK, CHUNK), 0)
    col = lax.broadcasted_iota(jnp.int32, (CHUNK, CHUNK), 1)
    return jnp.where(row >= col, ws_ref[g], 0.0).astype(BF16)


def _fwd_sgu(proj, lng, lnb, ws, bst, after=None):
    T = proj.shape[0]
    tc = min(T, 512)

    def body(u_ref, vs_ref, lng_ref, lnb_ref, ws_ref, bst_ref, a_ref, t_ref, stat_ref):
        gu, tu, tv, mu, rstd, vn = _sgu_forward_parts(u_ref, vs_ref, lng_ref, lnb_ref)
        t_ref[:, :D] = tu.astype(BF16)
        t_ref[:, D:] = tv.astype(BF16)
        lane = lax.broadcasted_iota(jnp.int32, (tc, 128), 1)
        stat_ref[...] = jnp.where(lane == 0, mu, jnp.where(lane == 1, rstd, 0.0))
        for g in range(GROUPS):
            wm = _masked_ws(ws_ref, g)
            cols = slice(g * CHUNK, (g + 1) * CHUNK)
            for c in range(tc // CHUNK):
                rows = slice(c * CHUNK, (c + 1) * CHUNK)
                mixed = _nn(wm, vn[rows, cols]) + bst_ref[:, g:g + 1]
                a_ref[rows, cols] = (gu[rows, cols] * mixed).astype(BF16)

    body, dep_specs, deps = _after(body, 6, after)
    return pl.pallas_call(
        body, name="fwd_sgu", grid=(T // tc,),
        in_specs=[pl.BlockSpec((tc, D), lambda i: (i, 0)), pl.BlockSpec((tc, D), lambda i: (i, 1)),
                  pl.BlockSpec((1, D), lambda i: (0, 0)), pl.BlockSpec((1, D), lambda i: (0, 0)),
                  pl.BlockSpec((GROUPS, CHUNK, CHUNK), lambda i: (0, 0, 0)),
                  pl.BlockSpec((CHUNK, GROUPS), lambda i: (0, 0))] + dep_specs,
        out_specs=[pl.BlockSpec((tc, D), lambda i: (i, 0)), pl.BlockSpec((tc, 2 * D), lambda i: (i, 0)),
                   pl.BlockSpec((tc, 128), lambda i: (i, 0))],
        out_shape=[SDS((T, D), BF16), SDS((T, 2 * D), BF16), SDS((T, 128), F32)],
        compiler_params=_params(1),
    )(proj, proj, lng, lnb, ws, bst, *deps)


def _rope_tables(posf, invf, sgn, after=None):
    T = posf.shape[0]
    tr = min(T, 1024)

    def body(pos_ref, invf_ref, sgn_ref, c_ref, s_ref):
        ang = pos_ref[...] * invf_ref[...]
        c_ref[...] = jnp.cos(ang)
        s = jnp.sin(ang)
        s_ref[:, :128] = jnp.where(sgn_ref[...] < 0.0, -s, 0.0)
        s_ref[:, 128:] = jnp.where(sgn_ref[...] > 0.0, s, 0.0)

    body, dep_specs, deps = _after(body, 3, after)
    return pl.pallas_call(
        body, name="rope_tables", grid=(T // tr,),
        in_specs=[pl.BlockSpec((tr, 1), lambda i: (i, 0)), pl.BlockSpec((1, 128), lambda i: (0, 0)),
                  pl.BlockSpec((1, 128), lambda i: (0, 0))] + dep_specs,
        out_specs=[pl.BlockSpec((tr, 128), lambda i: (i, 0)), pl.BlockSpec((tr, 256), lambda i: (i, 0))],
        out_shape=[SDS((T, 128), F32), SDS((T, 256), F32)],
        compiler_params=_params(1),
    )(posf, invf, sgn, *deps)


def _rope(v, c, s):
    v = v.astype(F32)
    return v * c + pltpu.roll(v, 128 - ROPE // 2, 1) * s[:, :128] + pltpu.roll(v, ROPE // 2, 1) * s[:, 128:]


def _rope_bwd(dv, c, s):
    return dv * c + pltpu.roll(dv * s[:, :128], ROPE // 2, 1) + pltpu.roll(dv * s[:, 128:], 128 - ROPE // 2, 1)


def _fold_masks(first):
    jj = lax.broadcasted_iota(jnp.int32, (CHUNK, CHUNK), 0)
    t = lax.broadcasted_iota(jnp.int32, (CHUNK, CHUNK), 1)
    prev = jj > t
    return prev, jnp.where(prev & first, -1e30, 0.0)


def _fold(band, prev):
    return jnp.where(prev, band[:CHUNK], band[CHUNK:])


def _unfold(folded, prev):
    return jnp.concatenate([jnp.where(prev, folded, 0.0), jnp.where(prev, 0.0, folded)], axis=0)


def _softmax_sink(s, sink, key_axis):
    m = jnp.maximum(jnp.max(s, axis=key_axis, keepdims=True), sink)
    p = jnp.exp(s - m)
    esink = jnp.exp(sink - m)
    inv = 1.0 / (jnp.sum(p, axis=key_axis, keepdims=True) + esink)
    return p * inv, esink * inv


def _head_pair_operand(slab, g):
    lo = lax.broadcasted_iota(jnp.int32, slab.shape, 1) < HEAD
    if g % 2 == 0:
        first = jnp.where(lo, slab, 0.0)
        second = pltpu.roll(first, HEAD, 1)
    else:
        second = jnp.where(lo, 0.0, slab)
        first = pltpu.roll(second, HEAD, 1)
    return jnp.concatenate([first, second], axis=0).astype(BF16)


def _head_pair_gradient(acc, g):
    top, bot = acc[:2 * CHUNK], acc[2 * CHUNK:]
    lo = lax.broadcasted_iota(jnp.int32, top.shape, 1) < HEAD
    if g % 2 == 0:
        return jnp.where(lo, top, 0.0) + pltpu.roll(jnp.where(lo, 0.0, bot), HEAD, 1)
    return pltpu.roll(jnp.where(lo, top, 0.0), HEAD, 1) + jnp.where(lo, 0.0, bot)


PAIRS_PER_KV = N_Q // N_KV // 2
KV_W = N_KV * HEAD


def _band(prev_ref, cur_ref, cols=slice(None)):
    return jnp.concatenate([prev_ref[:, cols], cur_ref[:, cols]], axis=0)


def _fwd_attn(proj, cos, sin, sinks):
    T = proj.shape[0]
    nb = T // CHUNK
    cur = lambda i: i
    prev = lambda i: jnp.maximum(i - 1, 0)

    def body(q_ref, kp_ref, kc_ref, vp_ref, vc_ref, cp_ref, cc_ref, sp_ref, sc_ref, sink_ref,
             o_ref, qr_ref, kr_ref, p_ref, psink_ref):
        prev_slot, bias = _fold_masks(pl.program_id(0) == 0)
        c_band, s_band = _band(cp_ref, cc_ref), _band(sp_ref, sc_ref)
        for j in range(KV_W // 128):
            cols = slice(j * 128, (j + 1) * 128)
            k_slab = _rope(_band(kp_ref, kc_ref, cols), c_band, s_band)
            kr_ref[:, cols] = k_slab[CHUNK:].astype(BF16)
            v_slab = _band(vp_ref, vc_ref, cols).astype(F32)
            for g in (2 * j, 2 * j + 1):
                k2 = _head_pair_operand(k_slab, g)
                v2 = _head_pair_operand(v_slab, g)
                pairs = [g * PAIRS_PER_KV + r for r in range(PAIRS_PER_KV)]
                qps = []
                for pair in pairs:
                    lanes = slice(pair * 128, (pair + 1) * 128)
                    qps.append((_rope(q_ref[:, lanes], cc_ref[...], sc_ref[...]) * (HEAD ** -0.5)).astype(BF16))
                    qr_ref[:, lanes] = qps[-1]
                s2 = _nt(k2, jnp.concatenate(qps, axis=0))
                pcols = []
                for r, pair in enumerate(pairs):
                    ps = []
                    for e in range(2):
                        head = 2 * pair + e
                        s = _fold(s2[e * 2 * CHUNK:(e + 1) * 2 * CHUNK, r * 128:(r + 1) * 128], prev_slot) + bias
                        p, psink = _softmax_sink(s, sink_ref[head], 0)
                        p = p.astype(BF16)
                        p_ref[head] = p
                        psink_ref[head:head + 1, :] = psink
                        ps.append(_unfold(p, prev_slot))
                    pcols.append(jnp.concatenate(ps, axis=0))
                o = _tn(jnp.concatenate(pcols, axis=1), v2).astype(BF16)
                for r, pair in enumerate(pairs):
                    o_ref[:, pair * 128:(pair + 1) * 128] = o[r * CHUNK:(r + 1) * CHUNK]

    table = lambda which, width: pl.BlockSpec((CHUNK, width), lambda i: (which(i), 0))
    return pl.pallas_call(
        body, name="fwd_attn", grid=(nb,),
        in_specs=[pl.BlockSpec((CHUNK, D), lambda i: (i, OFF_Q // D)),
                  pl.BlockSpec((CHUNK, KV_W), lambda i: (prev(i), OFF_K // KV_W)),
                  pl.BlockSpec((CHUNK, KV_W), lambda i: (i, OFF_K // KV_W)),
                  pl.BlockSpec((CHUNK, KV_W), lambda i: (prev(i), OFF_VA // KV_W)),
                  pl.BlockSpec((CHUNK, KV_W), lambda i: (i, OFF_VA // KV_W)),
                  table(prev, 128), table(cur, 128), table(prev, 256), table(cur, 256),
                  pl.BlockSpec(memory_space=pltpu.SMEM)],
        out_specs=[pl.BlockSpec((CHUNK, D), lambda i: (i, 0)), pl.BlockSpec((CHUNK, D), lambda i: (i, 0)),
                   pl.BlockSpec((CHUNK, KV_W), lambda i: (i, 0)),
                   pl.BlockSpec((None, N_Q, CHUNK, CHUNK), lambda i: (i, 0, 0, 0)),
                   pl.BlockSpec((None, N_Q, CHUNK), lambda i: (i, 0, 0))],
        out_shape=[SDS((T, D), BF16), SDS((T, D), BF16), SDS((T, KV_W), BF16),
                   SDS((nb, N_Q, CHUNK, CHUNK), BF16), SDS((nb, N_Q, CHUNK), F32)],
        compiler_params=_params(1),
    )(proj, proj, proj, proj, proj, cos, cos, sin, sin, sinks)


def _fwd_mix(a, att, proj, x, wa, wb, wo, g1, g2):
    T = x.shape[0]
    tm = min(T, 512)
    half = D // 2

    def body(a_ref, att_ref, ga0, ga1, gb0, gb1, x_ref, wa_ref, wb_ref, wo_ref, g1_ref, g2_ref,
             mg_ref, a2_ref, b2_ref, mix_ref, x1_ref, hf_ref):
        a2 = _nn(a_ref[...], wa_ref[...])
        b2 = _nn(att_ref[...], wb_ref[...])
        ga = jnp.concatenate([ga0[...], ga1[...]], axis=1).astype(F32)
        gb = jnp.concatenate([gb0[...], gb1[...]], axis=1).astype(F32)
        merged = (_sigmoid(ga) * a2 + _sigmoid(gb) * b2).astype(BF16)
        a2_ref[...] = a2.astype(BF16)
        b2_ref[...] = b2.astype(BF16)
        mg_ref[...] = merged
        mix = _nn(merged, wo_ref[...])
        mix_ref[...] = mix
        _, mh = _rms_stats(mix)
        x1 = x_ref[...] + mh * g1_ref[...]
        x1_ref[...] = x1
        _, xh = _rms_stats(x1)
        hf_ref[...] = (xh * g2_ref[...]).astype(BF16)

    row = lambda i: (i, 0)
    const = lambda i: (0, 0)
    gspec = lambda off: pl.BlockSpec((tm, half), lambda i: (i, off // half))
    return pl.pallas_call(
        body, name="fwd_mix", grid=(T // tm,),
        in_specs=[pl.BlockSpec((tm, D), row), pl.BlockSpec((tm, D), row),
                  gspec(OFF_GA), gspec(OFF_GA + half), gspec(OFF_GB), gspec(OFF_GB + half),
                  pl.BlockSpec((tm, D), row), _resident((D, D)), _resident((D, D)),
                  _resident((D, D)), pl.BlockSpec((1, D), const), pl.BlockSpec((1, D), const)],
        out_specs=[pl.BlockSpec((tm, D), row)] * 6,
        out_shape=[SDS((T, D), BF16), SDS((T, D), BF16), SDS((T, D), BF16), SDS((T, D), F32), SDS((T, D), F32),
                   SDS((T, D), BF16)],
        compiler_params=_params(1),
    )(a, att, proj, proj, proj, proj, x, wa, wb, wo, g1, g2)


FF_SPLIT = N_DEV
FF_TILE = D_FF // FF_SPLIT


def _fwd_ff(hf, wfi3, wfo, x1, tgt, g3):
    T = hf.shape[0]
    tm = min(T, 512)

    def body(hf_ref, wfi_ref, wfo_ref, x1_ref, tgt_ref, g3_ref, f_ref, dy_ref, dff_ref, dg3_ref, loss_ref, r_s):
        @pl.when(pl.program_id(0) == 0)
        def _():
            dg3_ref[...] = jnp.zeros_like(dg3_ref)
            loss_ref[...] = jnp.zeros_like(loss_ref)

        hf_t = hf_ref[...]
        for s in range(FF_SPLIT):
            cols = slice(s * FF_TILE, (s + 1) * FF_TILE)
            f = _nn(hf_t, wfi_ref[s]).astype(BF16)
            f_ref[:, cols] = f
            rl = jnp.maximum(f.astype(F32), 0.0)
            r_s[:, cols] = (rl * rl).astype(BF16)
        r3, fh = _rms_stats(_nn(r_s[...], wfo_ref[...]))
        e = x1_ref[...] + fh * g3_ref[...] - tgt_ref[...]
        loss_ref[...] += jnp.sum(e * e) * (0.5 / D)
        dy = e * (1.0 / D)
        dy_ref[...] = dy
        dg3_ref[...] += _colsum(dy * fh)
        dff_ref[...] = _rms_bwd(dy, fh, r3, g3_ref[...]).astype(BF16)

    row = lambda i: (i, 0)
    const = lambda i: (0, 0)
    return pl.pallas_call(
        body, name="fwd_ff", grid=(T // tm,),
        in_specs=[pl.BlockSpec((tm, D), row), _resident((FF_SPLIT, D, FF_TILE)), _resident((D_FF, D)),
                  pl.BlockSpec((tm, D), row),
                  pl.BlockSpec((tm, D), row), pl.BlockSpec((1, D), const)],
        out_specs=[pl.BlockSpec((tm, D_FF), row), pl.BlockSpec((tm, D), row),
                   pl.BlockSpec((tm, D), row), pl.BlockSpec((1, D), const), pl.BlockSpec((1, 128), const)],
        out_shape=[SDS((T, D_FF), BF16), SDS((T, D), F32), SDS((T, D), BF16), SDS((1, D), F32), SDS((1, 128), F32)],
        scratch_shapes=[pltpu.VMEM((tm, D_FF), BF16)],
        compiler_params=_params(1),
    )(hf, wfi3, wfo, x1, tgt, g3)


def _bwd_ff(dff, f, wfi3, wfo, x1, dy, mix, g1, g2):
    T = dff.shape[0]
    tm = min(T, 512)

    def body(dff_ref, f_ref, wfi_ref, wfo_ref, x1_ref, dy_ref, mix_ref, g1_ref, g2_ref,
             df_ref, dx1_ref, dmix_ref, dg2_ref, dg1_ref):
        @pl.when(pl.program_id(0) == 0)
        def _():
            dg2_ref[...] = jnp.zeros_like(dg2_ref)
            dg1_ref[...] = jnp.zeros_like(dg1_ref)

        dff_t = dff_ref[...]
        dhf = None
        for s in range(FF_SPLIT):
            cols = slice(s * FF_TILE, (s + 1) * FF_TILE)
            dr = _nt(dff_t, wfo_ref[cols, :])
            df = (dr * (2.0 * jnp.maximum(f_ref[:, cols].astype(F32), 0.0))).astype(BF16)
            df_ref[:, cols] = df
            part = _nt(df, wfi_ref[s])
            dhf = part if dhf is None else dhf + part
        r2, xh = _rms_stats(x1_ref[...])
        dg2_ref[...] += _colsum(dhf * xh)
        dx1 = dy_ref[...] + _rms_bwd(dhf, xh, r2, g2_ref[...])
        dx1_ref[...] = dx1
        r1, mh = _rms_stats(mix_ref[...])
        dg1_ref[...] += _colsum(dx1 * mh)
        dmix_ref[...] = _rms_bwd(dx1, mh, r1, g1_ref[...]).astype(BF16)

    row = lambda i: (i, 0)
    const = lambda i: (0, 0)
    return pl.pallas_call(
        body, name="bwd_ff", grid=(T // tm,),
        in_specs=[pl.BlockSpec((tm, D), row), pl.BlockSpec((tm, D_FF), row),
                  _resident((FF_SPLIT, D, FF_TILE)), _resident((D_FF, D)),
                  pl.BlockSpec((tm, D), row), pl.BlockSpec((tm, D), row), pl.BlockSpec((tm, D), row),
                  pl.BlockSpec((1, D), const), pl.BlockSpec((1, D), const)],
        out_specs=[pl.BlockSpec((tm, D_FF), row), pl.BlockSpec((tm, D), row),
                   pl.BlockSpec((tm, D), row), pl.BlockSpec((1, D), const), pl.BlockSpec((1, D), const)],
        out_shape=[SDS((T, D_FF), BF16), SDS((T, D), F32), SDS((T, D), BF16), SDS((1, D), F32), SDS((1, D), F32)],
        compiler_params=_params(1),
    )(dff, f, wfi3, wfo, x1, dy, mix, g1, g2)


def _wgrad_ff(hf, df, f, dff):
    T = hf.shape[0]
    tt = min(T, 2048)
    slabs = 2
    wide = slabs * FF_TILE

    def body(hf_ref, df_ref, f_ref, dff_ref, dwfi_ref, dwfo_ref, acc_i, acc_o):
        t = pl.program_id(1)

        @pl.when(t == 0)
        def _():
            acc_i[...] = jnp.zeros_like(acc_i)
            acc_o[...] = jnp.zeros_like(acc_o)

        acc_i[...] += _tn(hf_ref[...], df_ref[...])
        rl = jnp.maximum(f_ref[...].astype(F32), 0.0)
        acc_o[...] += _tn((rl * rl).astype(BF16), dff_ref[...])

        @pl.when(t == T // tt - 1)
        def _():
            for s in range(slabs):
                dwfi_ref[s] = acc_i[:, s * FF_TILE:(s + 1) * FF_TILE].astype(BF16)
            dwfo_ref[...] = acc_o[...].astype(BF16)

    return pl.pallas_call(
        body, name="wgrad_ff", grid=(D_FF // wide, T // tt),
        in_specs=[pl.BlockSpec((tt, D), lambda p, t: (t, 0)), pl.BlockSpec((tt, wide), lambda p, t: (t, p)),
                  pl.BlockSpec((tt, wide), lambda p, t: (t, p)), pl.BlockSpec((tt, D), lambda p, t: (t, 0))],
        out_specs=[pl.BlockSpec((slabs, D, FF_TILE), lambda p, t: (p, 0, 0)), pl.BlockSpec((wide, D), lambda p, t: (p, 0))],
        out_shape=[SDS((FF_SPLIT, D, FF_TILE), BF16), SDS((D_FF, D), BF16)],
        scratch_shapes=[pltpu.VMEM((D, wide), F32), pltpu.VMEM((wide, D), F32)],
        compiler_params=_params(2),
    )(hf, df, f, dff)


def _bwd_mix(dmix, proj, a2, b2, merged, a, att, wo, wa, wb, after=None):
    T = dmix.shape[0]
    tm = min(T, 512)
    half = D // 2
    last = T // tm - 1

    def body(dmix_ref, ga0, ga1, gb0, gb1, a2_ref, b2_ref, mg_ref, a_ref, att_ref, wo_ref, wa_ref, wb_ref,
             dg_ref, da_ref, datt_ref, dwo_ref, dwa_ref, dwb_ref, acc, stage, sem):
        t = pl.program_id(0)

        @pl.when(t == 0)
        def _():
            acc[...] = jnp.zeros_like(acc)

        dmix_t = dmix_ref[...]
        dmg = _nt(dmix_t, wo_ref[...])
        sa = _sigmoid(jnp.concatenate([ga0[...], ga1[...]], axis=1).astype(F32))
        sb = _sigmoid(jnp.concatenate([gb0[...], gb1[...]], axis=1).astype(F32))
        da2 = (dmg * sa).astype(BF16)
        db2 = (dmg * sb).astype(BF16)
        dg_ref[:, :D] = (dmg * a2_ref[...].astype(F32) * (sa * (1.0 - sa))).astype(BF16)
        dg_ref[:, D:] = (dmg * b2_ref[...].astype(F32) * (sb * (1.0 - sb))).astype(BF16)
        da_ref[...] = _nt(da2, wa_ref[...]).astype(BF16)
        datt_ref[...] = _nt(db2, wb_ref[...]).astype(BF16)
        acc[0] += _tn(mg_ref[...], dmix_t)
        acc[1] += _tn(a_ref[...], da2)
        acc[2] += _tn(att_ref[...], db2)

        @pl.when(t == last)
        def _():
            for k, dw_ref in enumerate((dwo_ref, dwa_ref, dwb_ref)):
                stage[...] = acc[k].astype(BF16)
                out = pltpu.make_async_copy(stage, dw_ref, sem)
                out.start()
                out.wait()

    row = lambda i: (i, 0)
    gspec = lambda off: pl.BlockSpec((tm, half), lambda i: (i, off // half))
    body, dep_specs, deps = _after(body, 13, after)
    return pl.pallas_call(
        body, name="bwd_mix", grid=(T // tm,),
        in_specs=[pl.BlockSpec((tm, D), row), gspec(OFF_GA), gspec(OFF_GA + half), gspec(OFF_GB), gspec(OFF_GB + half)]
        + [pl.BlockSpec((tm, D), row)] * 5 + [_resident((D, D))] * 3 + dep_specs,
        out_specs=[pl.BlockSpec((tm, 2 * D), row), pl.BlockSpec((tm, D), row), pl.BlockSpec((tm, D), row)] + [_ANY] * 3,
        out_shape=[SDS((T, 2 * D), BF16), SDS((T, D), BF16), SDS((T, D), BF16)] + [SDS((D, D), BF16)] * 3,
        scratch_shapes=[pltpu.VMEM((3, D, D), F32), pltpu.VMEM((D, D), BF16), pltpu.SemaphoreType.DMA],
        compiler_params=_params(1),
    )(dmix, proj, proj, proj, proj, a2, b2, merged, a, att, wo, wa, wb, *deps)


def _bwd_attn(qr, kr, probs, psink, proj, cos, sin, datt, after=None):
    T = proj.shape[0]
    nb = T // CHUNK
    cur = lambda i: jnp.minimum(i, nb - 1)
    prev = lambda i: jnp.maximum(jnp.minimum(i, nb - 1) - 1, 0)

    def body(q_ref, kp_ref, kc_ref, vp_ref, vc_ref, cp_ref, cc_ref, sp_ref, sc_ref, p_ref, psink_ref, do_ref,
             dq_ref, dkv_ref, dsink_ref, carry_k, carry_v):
        i = pl.program_id(0)

        @pl.when(i == 0)
        def _():
            carry_k[...] = jnp.zeros_like(carry_k)
            carry_v[...] = jnp.zeros_like(carry_v)
            dsink_ref[...] = jnp.zeros_like(dsink_ref)

        @pl.when(i < nb)
        def _():
            prev_slot, _ = _fold_masks(i == 0)
            c_band, s_band = _band(cp_ref, cc_ref), _band(sp_ref, sc_ref)
            for j in range(KV_W // 128):
                cols = slice(j * 128, (j + 1) * 128)
                k_slab = _band(kp_ref, kc_ref, cols).astype(F32)
                v_slab = _band(vp_ref, vc_ref, cols).astype(F32)
                dk_slab = jnp.zeros((2 * CHUNK, 128), F32)
                dv_slab = jnp.zeros((2 * CHUNK, 128), F32)
                for g in (2 * j, 2 * j + 1):
                    k2 = _head_pair_operand(k_slab, g)
                    v2 = _head_pair_operand(v_slab, g)
                    pairs = [g * PAIRS_PER_KV + r for r in range(PAIRS_PER_KV)]
                    q_stack = jnp.concatenate([q_ref[:, pr * 128:(pr + 1) * 128] for pr in pairs], axis=0)
                    do_stack = jnp.concatenate([do_ref[:, pr * 128:(pr + 1) * 128] for pr in pairs], axis=0)
                    dp2 = _nt(v2, do_stack)
                    pcols, dscols = [], []
                    for r, pair in enumerate(pairs):
                        ps, dss = [], []
                        for e in range(2):
                            head = 2 * pair + e
                            p_b = p_ref[head]
                            p = p_b.astype(F32)
                            dp = _fold(dp2[e * 2 * CHUNK:(e + 1) * 2 * CHUNK, r * 128:(r + 1) * 128], prev_slot)
                            delta = jnp.sum(p * dp, axis=0, keepdims=True)
                            ps.append(_unfold(p_b, prev_slot))
                            dss.append(_unfold((p * (dp - delta)).astype(BF16), prev_slot))
                            dsink_ref[head:head + 1, :] -= psink_ref[head:head + 1, :] * delta
                        pcols.append(jnp.concatenate(ps, axis=0))
                        dscols.append(jnp.concatenate(dss, axis=0))
                    ds2 = jnp.concatenate(dscols, axis=1)
                    dq = _tn(ds2, k2) * (HEAD ** -0.5)
                    for r, pair in enumerate(pairs):
                        dq_ref[:, pair * 128:(pair + 1) * 128] = _rope_bwd(
                            dq[r * CHUNK:(r + 1) * CHUNK], cc_ref[...], sc_ref[...]).astype(BF16)
                    dk_slab = dk_slab + _head_pair_gradient(_nn(ds2, q_stack), g)
                    dv_slab = dv_slab + _head_pair_gradient(_nn(jnp.concatenate(pcols, axis=1), do_stack), g)
                dk_slab = _rope_bwd(dk_slab, c_band, s_band)
                vcols = slice(KV_W + j * 128, KV_W + (j + 1) * 128)
                dkv_ref[:, cols] = (carry_k[:, cols] + dk_slab[:CHUNK]).astype(BF16)
                dkv_ref[:, vcols] = (carry_v[:, cols] + dv_slab[:CHUNK]).astype(BF16)
                carry_k[:, cols] = dk_slab[CHUNK:]
                carry_v[:, cols] = dv_slab[CHUNK:]

        @pl.when(i == nb)
        def _():
            dkv_ref[:, :KV_W] = carry_k[...].astype(BF16)
            dkv_ref[:, KV_W:] = carry_v[...].astype(BF16)

    table = lambda which, width: pl.BlockSpec((CHUNK, width), lambda i: (which(i), 0))
    body, dep_specs, deps = _after(body, 12, after)
    return pl.pallas_call(
        body, name="bwd_attn", grid=(nb + 1,),
        in_specs=[pl.BlockSpec((CHUNK, D), lambda i: (cur(i), 0)),
                  pl.BlockSpec((CHUNK, KV_W), lambda i: (prev(i), 0)),
                  pl.BlockSpec((CHUNK, KV_W), lambda i: (cur(i), 0)),
                  pl.BlockSpec((CHUNK, KV_W), lambda i: (prev(i), OFF_VA // KV_W)),
                  pl.BlockSpec((CHUNK, KV_W), lambda i: (cur(i), OFF_VA // KV_W)),
                  table(prev, 128), table(cur, 128), table(prev, 256), table(cur, 256),
                  pl.BlockSpec((None, N_Q, CHUNK, CHUNK), lambda i: (cur(i), 0, 0, 0)),
                  pl.BlockSpec((None, N_Q, CHUNK), lambda i: (cur(i), 0, 0)),
                  pl.BlockSpec((CHUNK, D), lambda i: (cur(i), 0))] + dep_specs,
        out_specs=[pl.BlockSpec((CHUNK, D), lambda i: (cur(i), 0)),
                   pl.BlockSpec((CHUNK, 2 * KV_W), lambda i: (jnp.maximum(i - 1, 0), 0)),
                   pl.BlockSpec((N_Q, CHUNK), lambda i: (0, 0))],
        out_shape=[SDS((T, D), BF16), SDS((T, 2 * KV_W), BF16), SDS((N_Q, CHUNK), F32)],
        scratch_shapes=[pltpu.VMEM((CHUNK, KV_W), F32), pltpu.VMEM((CHUNK, KV_W), F32)],
        compiler_params=_params(1),
    )(qr, kr, kr, proj, proj, cos, cos, sin, sin, probs, psink, datt, *deps)


def _bwd_sgu(proj, tanhs, stats, da, lng, lnb, ws, bst):
    T = proj.shape[0]
    tc = min(T, 512)
    nsteps = T // tc

    def body(u_ref, vs_ref, t_ref, stat_ref, da_ref, lng_ref, lnb_ref, ws_ref, bst_ref,
             duv_ref, dws_ref, dbs_ref, dlng_ref, dlnb_ref, dvn_s, dgu_s, dmx_sum):
        i = pl.program_id(0)

        @pl.when(i == 0)
        def _():
            dws_ref[...] = jnp.zeros_like(dws_ref)
            dlng_ref[...] = jnp.zeros_like(dlng_ref)
            dlnb_ref[...] = jnp.zeros_like(dlnb_ref)
            dmx_sum[...] = jnp.zeros_like(dmx_sum)

        u, vs, gu, tu, tv, rstd, vhat, vn = _sgu_forward_replay(u_ref, vs_ref, t_ref, stat_ref, lng_ref, lnb_ref)
        da = da_ref[...].astype(F32)
        for g in range(GROUPS):
            wm = _masked_ws(ws_ref, g)
            cols = slice(g * CHUNK, (g + 1) * CHUNK)
            dws = jnp.zeros((CHUNK, CHUNK), F32)
            dsum = jnp.zeros((CHUNK, CHUNK), F32)
            for c in range(tc // CHUNK):
                rows = slice(c * CHUNK, (c + 1) * CHUNK)
                vn_cg = vn[rows, cols]
                mixed = _nn(wm, vn_cg) + bst_ref[:, g:g + 1]
                dgu_s[rows, cols] = da[rows, cols] * mixed
                dmx = da[rows, cols] * gu[rows, cols]
                dmxb = dmx.astype(BF16)
                dws = dws + _nt(dmxb, vn_cg)
                dsum = dsum + dmx
                dvn_s[rows, cols] = _tn(wm, dmxb)
            dws_ref[g] += dws
            dmx_sum[:, cols] += dsum
        dvn = dvn_s[...]
        dlng_ref[...] += _colsum(dvn * vhat)
        dlnb_ref[...] += _colsum(dvn)
        dvh = dvn * lng_ref[...]
        dgv = rstd * (dvh - jnp.mean(dvh, axis=-1, keepdims=True) - vhat * jnp.mean(dvh * vhat, axis=-1, keepdims=True))
        duv_ref[:, :D] = (dgu_s[...] * _gelu_grad(u, tu)).astype(BF16)
        duv_ref[:, D:] = (dgv * _gelu_grad(vs, tv)).astype(BF16)

        @pl.when(i == nsteps - 1)
        def _():
            row = lax.broadcasted_iota(jnp.int32, (CHUNK, CHUNK), 0)
            col = lax.broadcasted_iota(jnp.int32, (CHUNK, CHUNK), 1)
            for g in range(GROUPS):
                dws_ref[g] = jnp.where(row >= col, dws_ref[g], 0.0)
                dbs_ref[g:g + 1, :] = _colsum(dmx_sum[:, g * CHUNK:(g + 1) * CHUNK].T)

    const2 = lambda i: (0, 0)
    return pl.pallas_call(
        body, name="bwd_sgu", grid=(nsteps,),
        in_specs=[pl.BlockSpec((tc, D), lambda i: (i, 0)), pl.BlockSpec((tc, D), lambda i: (i, 1)),
                  pl.BlockSpec((tc, 2 * D), lambda i: (i, 0)), pl.BlockSpec((tc, 128), lambda i: (i, 0)),
                  pl.BlockSpec((tc, D), lambda i: (i, 0)), pl.BlockSpec((1, D), const2), pl.BlockSpec((1, D), const2),
                  pl.BlockSpec((GROUPS, CHUNK, CHUNK), lambda i: (0, 0, 0)), pl.BlockSpec((CHUNK, GROUPS), const2)],
        out_specs=[pl.BlockSpec((tc, 2 * D), lambda i: (i, 0)), pl.BlockSpec((GROUPS, CHUNK, CHUNK), lambda i: (0, 0, 0)),
                   pl.BlockSpec((GROUPS, CHUNK), const2), pl.BlockSpec((1, D), const2), pl.BlockSpec((1, D), const2)],
        out_shape=[SDS((T, 2 * D), BF16), SDS((GROUPS, CHUNK, CHUNK), F32), SDS((GROUPS, CHUNK), F32),
                   SDS((1, D), F32), SDS((1, D), F32)],
        scratch_shapes=[pltpu.VMEM((tc, D), F32), pltpu.VMEM((tc, D), F32), pltpu.VMEM((CHUNK, D), F32)],
        compiler_params=_params(1),
    )(proj, proj, tanhs, stats, da, lng, lnb, ws, bst)


IN_SEG_WIDTHS = (2 * D, D, 2 * N_KV * HEAD, 2 * D)


def _resident(shape):
    return pl.BlockSpec(shape, lambda *_: (0,) * len(shape), pipeline_mode=pl.Buffered(1))


def _bwd_in(duv, dq, dkv, dg, win_t, x, dx1, g0, after=None):
    T = x.shape[0]
    tm = min(T, 512)

    def body(duv_ref, dq_ref, dkv_ref, dg_ref, w_ref, x_ref, dx1_ref, g0_ref, gx_ref, dg0_ref):
        @pl.when(pl.program_id(0) == 0)
        def _():
            dg0_ref[...] = jnp.zeros_like(dg0_ref)

        dh, off = None, 0
        for ref, width in zip((duv_ref, dq_ref, dkv_ref, dg_ref), IN_SEG_WIDTHS):
            part = _nn(ref[...], w_ref[off:off + width, :])
            dh = part if dh is None else dh + part
            off += width
        r0, xh = _rms_stats(x_ref[...])
        dg0_ref[...] += _colsum(dh * xh)
        gx_ref[...] = dx1_ref[...] + _rms_bwd(dh, xh, r0, g0_ref[...])

    row = lambda i: (i, 0)
    body, dep_specs, deps = _after(body, 8, after)
    return pl.pallas_call(
        body, name="bwd_in", grid=(T // tm,),
        in_specs=[pl.BlockSpec((tm, w), row) for w in IN_SEG_WIDTHS] + [
            _resident((IN_W, D)), pl.BlockSpec((tm, D), row), pl.BlockSpec((tm, D), row),
            pl.BlockSpec((1, D), lambda i: (0, 0))] + dep_specs,
        out_specs=[pl.BlockSpec((tm, D), row), pl.BlockSpec((1, D), lambda i: (0, 0))],
        out_shape=[SDS((T, D), F32), SDS((1, D), F32)],
        compiler_params=_params(1),
    )(duv, dq, dkv, dg, win_t, x, dx1, g0, *deps)


def _wgrad_rows(h, segs, first_row, into, name):
    T = h.shape[0]
    tt = min(T, 2048)
    widths = [s.shape[1] for s in segs]
    rows = sum(widths)
    n_in = 1 + len(segs) + (into is not None)

    def body(*refs):
        h_ref, seg_refs = refs[0], refs[1:1 + len(segs)]
        dw_ref, acc, stage, sem = refs[n_in], refs[n_in + 1], refs[n_in + 2], refs[n_in + 3]
        t = pl.program_id(0)

        @pl.when(t == 0)
        def _():
            acc[...] = jnp.zeros_like(acc)

        off = 0
        for ref, width in zip(seg_refs, widths):
            acc[off:off + width, :] += _tn(ref[...], h_ref[...])
            off += width

        @pl.when(t == T // tt - 1)
        def _():
            stage[...] = acc[...].astype(BF16)
            out = pltpu.make_async_copy(stage, dw_ref.at[pl.ds(first_row, rows)], sem)
            out.start()
            out.wait()

    row = lambda t: (t, 0)
    return pl.pallas_call(
        body, name=name, grid=(T // tt,),
        in_specs=[pl.BlockSpec((tt, D), row)] + [pl.BlockSpec((tt, w), row) for w in widths] + [_ANY] * (into is not None),
        out_specs=_ANY,
        out_shape=SDS((IN_W, D), BF16),
        input_output_aliases={} if into is None else {n_in - 1: 0},
        scratch_shapes=[pltpu.VMEM((rows, D), F32), pltpu.VMEM((rows, D), BF16), pltpu.SemaphoreType.DMA],
        compiler_params=_params(1),
    )(h, *segs, *([] if into is None else [into]))


def _place():
    x, y, c = lax.axis_index("x"), lax.axis_index("y"), lax.axis_index("c")
    return x, y, c, 4 * x + 2 * y + c


def _peers(x, y, c):
    out = []
    for mask in range(1, N_DEV):
        px = 1 - x if mask & 4 else x
        py = 1 - y if mask & 2 else y
        pc = 1 - c if mask & 1 else c
        out.append(((px, py, pc), 4 * px + 2 * py + pc))
    return out


def _all_to_all(arrays, gather, name, after=None):
    n = len(arrays)

    def body(*refs):
        ins, outs = refs[:n], refs[n:2 * n]
        send_sems, recv_sems, local_sems = refs[2 * n:]
        x, y, c, me = _place()
        local, sends, recvs = [], [], []
        for a in range(n):
            src_own = ins[a] if gather[a] else ins[a].at[me]
            local.append(pltpu.make_async_copy(src_own, outs[a].at[me], local_sems.at[a]))
            for k, (peer, pid) in enumerate(_peers(x, y, c)):
                sem = a * (N_DEV - 1) + k
                src = ins[a] if gather[a] else ins[a].at[pid]
                sends.append(pltpu.make_async_remote_copy(
                    src_ref=src, dst_ref=outs[a].at[me], send_sem=send_sems.at[sem], recv_sem=recv_sems.at[sem],
                    device_id=peer, device_id_type=MESH))
                recvs.append(pltpu.make_async_remote_copy(
                    src_ref=src, dst_ref=outs[a].at[pid], send_sem=send_sems.at[sem], recv_sem=recv_sems.at[sem],
                    device_id=peer, device_id_type=MESH))
        for cp in local + sends:
            cp.start()
        for cp in recvs:
            cp.wait_recv()
        for cp in sends:
            cp.wait_send()
        for cp in local:
            cp.wait()

    out_shape = [SDS((N_DEV,) + a.shape if gt else a.shape, a.dtype) for a, gt in zip(arrays, gather)]
    nsem = n * (N_DEV - 1)
    body, dep_specs, deps = _after(body, n, after)
    return pl.pallas_call(
        body, name=name,
        in_specs=[pl.BlockSpec(memory_space=pl.ANY)] * n + dep_specs,
        out_specs=[pl.BlockSpec(memory_space=pl.ANY)] * n,
        out_shape=out_shape,
        scratch_shapes=[pltpu.SemaphoreType.DMA((nsem,)), pltpu.SemaphoreType.DMA((nsem,)), pltpu.SemaphoreType.DMA((n,))],
    )(*arrays, *deps)


_HBM = pl.BlockSpec(memory_space=pltpu.HBM)
_SEM = pl.BlockSpec(memory_space=pltpu.SEMAPHORE)
_EFFECT = pltpu.SideEffectType.DATAFLOW_SIDE_EFFECTING
GATHER = "gather"
SCATTER = "scatter"
SPREAD = "spread"


def _zone_shape(a, mode):
    if mode == GATHER:
        return (N_DEV,) + a.shape
    return (N_DEV - 1,) + (a.shape[1:] if mode == SCATTER else a.shape)


def _start_copies(arrays, modes, name, after=None):
    n = len(arrays)
    zones = [lax.empty(_zone_shape(a, m), a.dtype) for a, m in zip(arrays, modes)]

    def body(*refs):
        ins, lands = refs[:n], refs[n:2 * n]
        send_sems, recv_sems = refs[-2 * n - 3], refs[-2 * n - 2]
        token = refs[-1]
        x, y, c, me = _place()
        for a in range(n):
            for k, (peer, pid) in enumerate(_peers(x, y, c)):
                src = ins[a].at[pid] if modes[a] == SCATTER else ins[a]
                dst = lands[a].at[me] if modes[a] == GATHER else lands[a].at[k]
                pltpu.make_async_remote_copy(src_ref=src, dst_ref=dst, send_sem=send_sems.at[a], recv_sem=recv_sems.at[a],
                                             device_id=peer, device_id_type=MESH).start()
            if modes[a] == GATHER:
                pltpu.make_async_remote_copy(src_ref=ins[a], dst_ref=lands[a].at[me], send_sem=send_sems.at[a],
                                             recv_sem=recv_sems.at[a], device_id=(x, y, c), device_id_type=MESH).start()
        token[...] = jnp.zeros_like(token)

    hbm = lambda a: pltpu.HBM(a.shape, a.dtype)
    sems = pltpu.SemaphoreType.DMA((n,))
    extra = [] if after is None else [after]
    operands = [pltpu.with_memory_space_constraint(a, pltpu.HBM) for a in list(arrays) + zones]
    res = pl.pallas_call(
        body, name=name,
        out_shape=(sems, sems, *[hbm(a) for a in arrays], *[hbm(z) for z in zones], SDS((8, 128), F32)),
        in_specs=[_HBM] * (2 * n) + [_ANY] * len(extra),
        out_specs=(_SEM, _SEM, *[_HBM] * (2 * n), pl.BlockSpec(memory_space=pltpu.VMEM)),
        input_output_aliases={i: 2 + i for i in range(2 * n)},
        compiler_params=pltpu.CompilerParams(has_side_effects=_EFFECT),
    )(*operands, *extra)
    return res[0], res[1], list(res[2:2 + n]), list(res[2 + n:2 + 2 * n]), res[-1]


def _wait_copies(started, after, name, count=N_DEV - 1):
    send_sems, recv_sems, thru, zones, _ = started
    nt, nz = len(thru), len(zones)

    def body(*refs):
        lands = refs[nt:nt + nz]
        send_ref, recv_ref = refs[nt + nz], refs[nt + nz + 1]
        x, y, c, _ = _place()
        for a in range(nz):
            blocks = lands[a].at[pl.ds(0, count)]
            cp = pltpu.make_async_remote_copy(src_ref=blocks, dst_ref=blocks, send_sem=send_ref.at[a], recv_sem=recv_ref.at[a],
                                              device_id=(x, y, 1 - c), device_id_type=MESH)
            cp.wait_send()
            cp.wait_recv()

    hbm = lambda a: pltpu.HBM(a.shape, a.dtype)
    res = pl.pallas_call(
        body, name=name,
        out_shape=tuple(hbm(a) for a in thru + zones),
        in_specs=[_HBM] * (nt + nz) + [_SEM, _SEM, _ANY],
        out_specs=tuple([_HBM] * (nt + nz)),
        input_output_aliases={i: i for i in range(nt + nz)},
        compiler_params=pltpu.CompilerParams(has_side_effects=_EFFECT),
    )(*thru, *zones, send_sems, recv_sems, after)
    return list(res[:nt]), list(res[nt:])


def _split_start(body, arrays, zones, name, after):
    n = len(arrays) + len(zones)
    hbm = lambda a: pltpu.HBM(a.shape, a.dtype)
    sems = pltpu.SemaphoreType.DMA((max(len(zones), 1),))
    extra = [] if after is None else [after]
    operands = [pltpu.with_memory_space_constraint(a, pltpu.HBM) for a in list(arrays) + list(zones)]
    res = pl.pallas_call(
        body, name=name,
        out_shape=(sems, sems, *[hbm(a) for a in operands], SDS((8, 128), F32)),
        in_specs=[_HBM] * n + [_ANY] * len(extra),
        out_specs=(_SEM, _SEM, *[_HBM] * n, pl.BlockSpec(memory_space=pltpu.VMEM)),
        input_output_aliases={i: 2 + i for i in range(n)},
        compiler_params=pltpu.CompilerParams(has_side_effects=_EFFECT),
    )(*operands, *extra)
    return res[0], res[1], list(res[2:2 + len(arrays)]), list(res[2 + len(arrays):2 + n]), res[-1]


def _gather_first_leg(shard, name, after=None):
    zone = lax.empty((N_DEV,) + shard.shape, shard.dtype)
    extra = 0 if after is None else 1

    def body(*refs):
        src, land = refs[0], refs[1]
        send_sem, recv_sem, token = refs[2 + extra], refs[3 + extra], refs[-1]
        x, y, c, me = _place()
        for peer in ((x, y, c), (x, y, 1 - c), (1 - x, y, c), (x, 1 - y, c), (1 - x, 1 - y, c)):
            pltpu.make_async_remote_copy(src_ref=src, dst_ref=land.at[me], send_sem=send_sem.at[0], recv_sem=recv_sem.at[0],
                                         device_id=peer, device_id_type=MESH).start()
        token[...] = jnp.zeros_like(token)

    return _split_start(body, [shard], [zone], name, after)


def _gather_second_leg(zone, name, after=None):
    extra = 0 if after is None else 1

    def body(*refs):
        land = refs[0]
        send_sem, recv_sem, token = refs[1 + extra], refs[2 + extra], refs[-1]
        x, y, c, _ = _place()
        for px, py in ((1 - x, y), (x, 1 - y), (1 - x, 1 - y)):
            slot = 4 * px + 2 * py + c
            pltpu.make_async_remote_copy(src_ref=land.at[slot], dst_ref=land.at[slot], send_sem=send_sem.at[0],
                                         recv_sem=recv_sem.at[0], device_id=(x, y, 1 - c), device_id_type=MESH).start()
        token[...] = jnp.zeros_like(token)

    return _split_start(body, [], [zone], name, after)


UPDATE_BLOCK_ELEMS = 384 * 1024


def _update_rows(R, C):
    fits = [t for t in range(8, R + 1, 8) if R % t == 0 and t * C <= UPDATE_BLOCK_ELEMS]
    whole = [t for t in fits if t % 16 == 0]
    return max(whole or fits)


def _adamw_math(g, w, m, v):
    m2 = ADAM_B1 * m + (1.0 - ADAM_B1) * g
    v2 = ADAM_B2 * v + (1.0 - ADAM_B2) * (g * g)
    m_hat = m2 / (1.0 - ADAM_B1 ** ADAM_STEP)
    v_hat = v2 / (1.0 - ADAM_B2 ** ADAM_STEP)
    delta = -ADAM_LR * (m_hat / (jnp.sqrt(v_hat) + ADAM_EPS) + ADAM_WD * w)
    return delta, m2, v2


def _sum_adamw(parts, w, m, v, name):
    R, C = w.shape
    tr = _update_rows(R, C)

    def body(p_ref, w_ref, m_ref, v_ref, g_ref, d_ref, m2_ref, v2_ref):
        g = p_ref[0]
        for k in range(1, N_DEV):
            g = g + p_ref[k]
        g_ref[...] = g
        d_ref[...], m2_ref[...], v2_ref[...] = _adamw_math(g, w_ref[...], m_ref[...], v_ref[...])

    blk = pl.BlockSpec((tr, C), lambda i: (i, 0))
    return pl.pallas_call(
        body, name=name, grid=(R // tr,),
        in_specs=[pl.BlockSpec((N_DEV, tr, C), lambda i: (0, i, 0)), blk, blk, blk],
        out_specs=[blk] * 4,
        out_shape=[SDS((R, C), F32)] * 4,
        compiler_params=_params(1),
    )(parts, w, m, v)


def _sum_adamw_peers(me, own, parts, w, m, v, name, replicated, also_rows=None):
    R, C = w.shape
    tr = _update_rows(R, C)
    assert also_rows is None or tr == R

    def body(me_ref, own_ref, p_ref, w_ref, m_ref, v_ref, g_ref, d_ref, m2_ref, v2_ref, *extra):
        if replicated:
            mine = me_ref[0]
            g = None
            for j in range(N_DEV):
                k = jnp.maximum(jnp.bitwise_xor(mine, j) - 1, 0)
                term = jnp.where(mine == j, own_ref[...], p_ref[k])
                g = term if g is None else g + term
        else:
            g = own_ref[...].astype(F32)
            for k in range(N_DEV - 1):
                g = g + p_ref[k].astype(F32)
        results = (g,) + _adamw_math(g, w_ref[...], m_ref[...], v_ref[...])
        for ref, val in zip((g_ref, d_ref, m2_ref, v2_ref), results):
            ref[...] = val
        for ref, val in zip(extra, results):
            ref[...] = val[also_rows[0]:also_rows[1]]

    blk = pl.BlockSpec((tr, C), lambda i, me_ref: (i, 0))
    own_spec = blk if replicated else pl.BlockSpec((None, tr, C), lambda i, me_ref: (me_ref[0], i, 0))
    n_also = 0 if also_rows is None else also_rows[1] - also_rows[0]
    also_specs = [pl.BlockSpec((n_also, C), lambda i, me_ref: (0, 0))] * (4 if also_rows else 0)
    return pl.pallas_call(
        body, name=name,
        grid_spec=pltpu.PrefetchScalarGridSpec(
            num_scalar_prefetch=1, grid=(R // tr,),
            in_specs=[own_spec, pl.BlockSpec((N_DEV - 1, tr, C), lambda i, me_ref: (0, i, 0)), blk, blk, blk],
            out_specs=[blk] * 4 + also_specs),
        out_shape=[SDS((R, C), F32)] * 4 + [SDS((n_also, C), F32)] * len(also_specs),
        compiler_params=_params(1),
    )(me, own, parts, w, m, v)


UPDATE_CHUNK_ELEMS = 64 * 1024


def _chunk_rows(R, C):
    fits = [t for t in range(16, R + 1, 16) if R % t == 0 and t * C <= UPDATE_CHUNK_ELEMS]
    return max(fits) if fits else R


def _sum_adamw_stream(me, own, parts, w, m, v, name):
    R, C = w.shape
    tr = _chunk_rows(R, C)
    n = R // tr

    def body(me_ref, own_ref, p_ref, w_ref, m_ref, v_ref, g_ref, d_ref, m2_ref, v2_ref,
             own_v, p_v, w_v, m_v, v_v, g_v, d_v, m2_v, v2_v, in_sems, out_sems):
        mine = me_ref[0]
        loads = []
        for i in range(n):
            r = pl.ds(i * tr, tr)
            cps = [pltpu.make_async_copy(own_ref.at[mine, r], own_v.at[r], in_sems.at[i, 0]),
                   pltpu.make_async_copy(p_ref.at[:, r], p_v.at[:, r], in_sems.at[i, 1]),
                   pltpu.make_async_copy(w_ref.at[r], w_v.at[r], in_sems.at[i, 2]),
                   pltpu.make_async_copy(m_ref.at[r], m_v.at[r], in_sems.at[i, 3]),
                   pltpu.make_async_copy(v_ref.at[r], v_v.at[r], in_sems.at[i, 4])]
            for cp in cps:
                cp.start()
            loads.append(cps)
        stores = []
        for i in range(n):
            r = pl.ds(i * tr, tr)
            for cp in loads[i]:
                cp.wait()
            g = own_v[r].astype(F32)
            for k in range(N_DEV - 1):
                g = g + p_v[k, r].astype(F32)
            results = (g,) + _adamw_math(g, w_v[r], m_v[r], v_v[r])
            for j, (stage, out, val) in enumerate(zip((g_v, d_v, m2_v, v2_v), (g_ref, d_ref, m2_ref, v2_ref), results)):
                stage[r] = val
                cp = pltpu.make_async_copy(stage.at[r], out.at[r], out_sems.at[i, j])
                cp.start()
                stores.append(cp)
        for cp in stores:
            cp.wait()

    hbm = pl.BlockSpec(memory_space=pl.ANY)
    return pl.pallas_call(
        body, name=name,
        in_specs=[pl.BlockSpec(memory_space=pltpu.SMEM)] + [hbm] * 5,
        out_specs=[hbm] * 4,
        out_shape=[SDS((R, C), F32)] * 4,
        scratch_shapes=[pltpu.VMEM((R, C), own.dtype), pltpu.VMEM((N_DEV - 1, R, C), parts.dtype)]
        + [pltpu.VMEM((R, C), F32)] * 7
        + [pltpu.SemaphoreType.DMA((n, 5)), pltpu.SemaphoreType.DMA((n, 4))],
        compiler_params=_params(0),
    )(me, own, parts, w, m, v)


SMALL = ("ln_v_gain", "ln_v_bias", "w_spatial", "b_spatial", "sinks", "norm_mix_post", "norm_ff_pre", "norm_ff_post")
SMALL_ROWS = {"ln_v_gain": 8, "ln_v_bias": 8, "w_spatial": 1024, "b_spatial": 8, "sinks": 8,
              "norm_mix_post": 8, "norm_ff_pre": 8, "norm_ff_post": 8}
SMALL_PACK_ROWS = 1152


def _pack_small(vals):
    rows = []
    for name in SMALL:
        flat = vals[name].reshape(-1)
        pad = SMALL_ROWS[name] * 128 - flat.shape[0]
        if pad:
            flat = jnp.concatenate([flat, jnp.zeros((pad,), F32)])
        rows.append(flat.reshape(SMALL_ROWS[name], 128))
    rows.append(jnp.zeros((SMALL_PACK_ROWS - sum(SMALL_ROWS.values()), 128), F32))
    return jnp.concatenate(rows, axis=0)


def _unpack_small(packed, shapes):
    out, r = {}, 0
    for name in SMALL:
        n = 1
        for s in shapes[name]:
            n *= s
        out[name] = packed[r:r + SMALL_ROWS[name]].reshape(-1)[:n].reshape(shapes[name])
        r += SMALL_ROWS[name]
    return out


def _rope_rows():
    d = jnp.arange(128) % HEAD
    inv = ROPE_THETA ** (-(2.0 * (d % (ROPE // 2))).astype(F32) / ROPE)
    invf = jnp.where(d < ROPE, inv, 0.0).astype(F32).reshape(1, 128)
    sgn = jnp.where(d < ROPE // 2, -1.0, jnp.where(d < ROPE, 1.0, 0.0)).astype(F32).reshape(1, 128)
    return invf, sgn


def kernel(x, positions, w_in, ln_v_gain, ln_v_bias, w_spatial, b_spatial, sinks, w_a, w_b, w_o, norm_mix_pre, norm_mix_post, w_ff_in, w_ff_out, norm_ff_pre, norm_ff_post, loss_target, m_w_in, m_ln_v_gain, m_ln_v_bias, m_w_spatial, m_b_spatial, m_sinks, m_w_a, m_w_b, m_w_o, m_norm_mix_pre, m_norm_mix_post, m_w_ff_in, m_w_ff_out, m_norm_ff_pre, m_norm_ff_post, v_w_in, v_ln_v_gain, v_ln_v_bias, v_w_spatial, v_b_spatial, v_sinks, v_w_a, v_w_b, v_w_o, v_norm_mix_pre, v_norm_mix_post, v_w_ff_in, v_w_ff_out, v_norm_ff_pre, v_norm_ff_post):
    given = dict(locals())
    T = x.shape[1]
    xt = x[0]
    tgt = loss_target[0]
    bst = b_spatial[0].T
    ws = w_spatial[0]

    me = 4 * lax.axis_index("x") + 2 * lax.axis_index("y") + lax.axis_index("c")
    me_arr = me.astype(jnp.int32).reshape(1)

    rest = ("w_a", "w_b", "w_o", "w_ff_in", "w_ff_out")
    shard = {n: given[n][0].astype(BF16) for n in rest}
    g_one = _gather_first_leg(w_in[0].T.astype(BF16), "gather_in_start")
    cos, sin = _rope_tables(positions.astype(F32).reshape(T, 1), *_rope_rows(), after=g_one[-1])
    small_state = [_pack_small({n: given[k + n] for n in SMALL}) for k in ("", "m_", "v_")]
    h = _rms_pre(xt, norm_mix_pre, after=[cos, *small_state, *[shard[n] for n in rest]])
    _, (win8,) = _wait_copies(g_one, h, "gather_in_wait", count=5)
    g_two = _gather_second_leg(win8, "gather_in_pass_start")
    g_rest = _start_copies([shard[n] for n in rest], [GATHER] * len(rest), "gather_rest_start", after=g_two[-1])
    _, (win8,) = _wait_copies(g_two, g_rest[-1], "gather_in_pass_wait", count=3)
    win = win8.reshape(IN_W, D)

    proj = _fwd_in(h, win)
    att, qr, kr, probs, psink = _fwd_attn(proj, cos, sin, sinks[0])
    a, tanhs, ln_stats = _fwd_sgu(proj, ln_v_gain, ln_v_bias, ws, bst, after=att)
    gw = dict(zip(rest, _wait_copies(g_rest, a, "gather_rest_wait", count=N_DEV)[1]))
    wa, wb, wo = (gw[n].reshape(D, D) for n in ("w_a", "w_b", "w_o"))
    wfi3 = gw["w_ff_in"]
    wfo = gw["w_ff_out"].reshape(D_FF, D)
    merged, a2, b2, mix, x1, hf = _fwd_mix(a, att, proj, xt, wa, wb, wo, norm_mix_post, norm_ff_pre)
    f, dy, dff, dg3, loss_part = _fwd_ff(hf, wfi3, wfo, x1, tgt, norm_ff_post)

    df, dx1, dmix, dg2, dg1 = _bwd_ff(dff, f, wfi3, wfo, x1, dy, mix, norm_mix_post, norm_ff_pre)
    dwfi3, dwfo = _wgrad_ff(hf, df, f, dff)
    own_ff = [dwfi3, dwfo.reshape(N_DEV, D_FF // N_DEV, D)]
    x_ff = _start_copies(own_ff, [SCATTER] * 2, "exchange_ff_start")
    dgate, da, datt, dwo, dwa, dwb = _bwd_mix(dmix, proj, a2, b2, merged, a, att, wo, wa, wb, after=x_ff[-1])
    own_mix = [g.reshape(N_DEV, D // N_DEV, D) for g in (dwa, dwb, dwo)]
    x_mix = _start_copies(own_mix, [SCATTER] * 3, "exchange_mix_start")
    dq, dkv, dsink = _bwd_attn(qr, kr, probs, psink, proj, cos, sin, datt, after=x_mix[-1])
    duv, dws, dbs, dlng, dlnb = _bwd_sgu(proj, tanhs, ln_stats, da, ln_v_gain, ln_v_bias, ws, bst)
    small_grads = {"ln_v_gain": dlng, "ln_v_bias": dlnb, "w_spatial": dws, "b_spatial": dbs, "sinks": jnp.sum(dsink, axis=1),
                   "norm_mix_post": dg1, "norm_ff_pre": dg2, "norm_ff_post": dg3}
    x_small = _start_copies([_pack_small(small_grads)], [SPREAD], "exchange_small_start")
    dwin = _wgrad_rows(h, [dgate], sum(IN_SEG_WIDTHS[:3]), None, "wgrad_in_gates")
    dwin = _wgrad_rows(h, [duv], 0, dwin, "wgrad_in_uv")
    dwin = _wgrad_rows(h, [dq, dkv], IN_SEG_WIDTHS[0], dwin, "wgrad_in_qkv")
    own_in = [dwin.reshape(N_DEV, IN_W // N_DEV, D)]
    x_in = _start_copies(own_in, [SCATTER], "exchange_in_start", after=x_small[-1])
    grad_x, dg0 = _bwd_in(duv, dq, dkv, dgate, win, xt, dx1, norm_mix_pre, after=x_in[-1])

    results = {}

    def update(n, own, parts, transposed=False):
        state = [given[k + n][0].T if transposed else given[k + n][0] for k in ("", "m_", "v_")]
        res = _sum_adamw_stream(me_arr, own, parts, *state, "adamw_" + n)
        results[n] = [(r.T if transposed else r).reshape(given[n].shape) for r in res]

    own_ff, p_ff = _wait_copies(x_ff, grad_x, "exchange_ff_wait")
    update("w_ff_in", own_ff[0], p_ff[0])
    update("w_ff_out", own_ff[1], p_ff[1])
    own_mix, p_mix = _wait_copies(x_mix, results["w_ff_out"][0], "exchange_mix_wait")
    for n, own, parts in zip(("w_a", "w_b", "w_o"), own_mix, p_mix):
        update(n, own, parts)
    tail = jnp.concatenate([dg0.reshape(8, 128), jnp.tile(loss_part, (8, 1))], axis=0)
    (tail_all,) = _all_to_all([tail], [True], "exchange_tail", after=results["w_o"][0])
    dg0_all = tail_all[:, :8]
    own_small, p_small = _wait_copies(x_small, tail_all, "exchange_small_wait")
    own_in, p_in = _wait_copies(x_in, p_small[0], "exchange_in_wait")
    update("w_in", own_in[0], p_in[0], transposed=True)
    first = sum(SMALL_ROWS[n] for n in SMALL[:SMALL.index("w_spatial")])
    packed = _sum_adamw_peers(me_arr, own_small[0], p_small[0], *small_state, "adamw_small", True,
                              also_rows=(first, first + SMALL_ROWS["w_spatial"]))
    shapes = {n: given[n].shape for n in SMALL}
    unpacked = [_unpack_small(p, shapes) for p in packed[:4]]
    for n in SMALL:
        results[n] = [u[n] for u in unpacked]
    results["w_spatial"] = [r.reshape(w_spatial.shape) for r in packed[4:]]
    n = "norm_mix_pre"
    results[n] = [r.reshape(given[n].shape) for r in _sum_adamw(
        dg0_all, given[n].reshape(8, 128), given["m_" + n].reshape(8, 128), given["v_" + n].reshape(8, 128), "adamw_" + n)]

    loss = jnp.sum(tail_all[:, 8, 0])
    order = ("w_in", "ln_v_gain", "ln_v_bias", "w_spatial", "b_spatial", "sinks", "w_a", "w_b", "w_o", "norm_mix_pre",
             "norm_mix_post", "w_ff_in", "w_ff_out", "norm_ff_pre", "norm_ff_post")
    out = [loss, grad_x.reshape(x.shape)]
    for k in range(4):
        out += [results[n][k] for n in order]
    return tuple(out)
```

```python
import jax
import jax.numpy as jnp
from jax import lax
from jax.experimental import pallas as pl
from jax.experimental.pallas import tpu as pltpu

F32 = jnp.float32
BF16 = jnp.bfloat16

N_DEV = 8
D = 1024
D_FF = 4096
IN_W = 5632
CHUNK = 128
GROUPS = 8
HEAD = 64
N_Q = 16
N_KV = 4
ROPE = 16
ROPE_THETA = 500000.0
EPS = 1e-6
OFF_Q, OFF_K, OFF_VA, OFF_GA, OFF_GB = 2048, 3072, 3328, 3584, 4608

ADAM_LR = 0.001
ADAM_B1 = 0.9
ADAM_B2 = 0.999
ADAM_EPS = 1e-08
ADAM_WD = 0.01
ADAM_STEP = 10

VMEM_LIMIT = 62 * 1024 * 1024

SDS = jax.ShapeDtypeStruct
MESH = pl.DeviceIdType.MESH


def _params(n_axes):
    return pltpu.CompilerParams(dimension_semantics=("arbitrary",) * n_axes, vmem_limit_bytes=VMEM_LIMIT)


def _nt(a, b):
    return lax.dot_general(a, b, (((1,), (1,)), ((), ())), preferred_element_type=F32)


def _tn(a, b):
    return lax.dot_general(a, b, (((0,), (0,)), ((), ())), preferred_element_type=F32)


def _nn(a, b):
    return jnp.dot(a, b, preferred_element_type=F32)


def _gelu(x):
    t = jnp.tanh(0.7978845608028654 * (x + 0.044715 * (x * x * x)))
    return 0.5 * x * (1.0 + t), t


def _gelu_grad(x, t):
    return 0.5 * (1.0 + t) + 0.5 * x * (1.0 - t * t) * (0.7978845608028654 * (1.0 + 3.0 * 0.044715 * x * x))


def _sigmoid(x):
    return 1.0 / (1.0 + jnp.exp(-x))


def _rms_stats(v):
    r = lax.rsqrt(jnp.mean(v * v, axis=-1, keepdims=True) + EPS)
    return r, v * r


def _rms_bwd(d, vhat, r, g):
    gd = g * d
    return r * (gd - vhat * jnp.mean(gd * vhat, axis=-1, keepdims=True))


def _colsum(v):
    return jnp.sum(v, axis=0, keepdims=True)


_ANY = pl.BlockSpec(memory_space=pl.ANY)


def _after(body, n_in, after):
    if after is None:
        return body, [], []
    deps = list(after) if isinstance(after, (list, tuple)) else [after]

    def ordered(*refs):
        return body(*refs[:n_in], *refs[n_in + len(deps):])

    return ordered, [_ANY] * len(deps), deps


def _rms_pre(x, g0, after=None):
    T = x.shape[0]
    tm = min(T, 1024)

    def body(x_ref, g_ref, h_ref):
        _, xh = _rms_stats(x_ref[...])
        h_ref[...] = (xh * g_ref[...]).astype(BF16)

    body, dep_specs, deps = _after(body, 2, after)
    return pl.pallas_call(
        body, name="rms_pre", grid=(T // tm,),
        in_specs=[pl.BlockSpec((tm, D), lambda i: (i, 0)), pl.BlockSpec((1, D), lambda i: (0, 0))] + dep_specs,
        out_specs=pl.BlockSpec((tm, D), lambda i: (i, 0)),
        out_shape=SDS((T, D), BF16),
        compiler_params=_params(1),
    )(x, g0, *deps)


def _fwd_in(h, win_t):
    T = h.shape[0]
    tm, tn = min(T, 1024), 1408

    def body(h_ref, w_ref, p_ref):
        for j in range(IN_W // tn):
            cols = slice(j * tn, (j + 1) * tn)
            p_ref[:, cols] = _nt(h_ref[...], w_ref[cols, :]).astype(BF16)

    return pl.pallas_call(
        body, name="fwd_in", grid=(T // tm,),
        in_specs=[pl.BlockSpec((tm, D), lambda i: (i, 0)), _resident((IN_W, D))],
        out_specs=pl.BlockSpec((tm, IN_W), lambda i: (i, 0)),
        out_shape=SDS((T, IN_W), BF16),
        compiler_params=_params(1),
    )(h, win_t)


def _sgu_forward_parts(u_ref, vs_ref, lng_ref, lnb_ref):
    u = u_ref[...].astype(F32)
    vs = vs_ref[...].astype(F32)
    gu, tu = _gelu(u)
    gv, tv = _gelu(vs)
    mu = jnp.mean(gv, axis=-1, keepdims=True)
    dv = gv - mu
    rstd = lax.rsqrt(jnp.mean(dv * dv, axis=-1, keepdims=True) + EPS)
    vhat = dv * rstd
    vn = (vhat * lng_ref[...] + lnb_ref[...]).astype(BF16)
    return gu, tu, tv, mu, rstd, vn


def _sgu_forward_replay(u_ref, vs_ref, t_ref, stat_ref, lng_ref, lnb_ref):
    u = u_ref[...].astype(F32)
    vs = vs_ref[...].astype(F32)
    tu = t_ref[:, :D].astype(F32)
    tv = t_ref[:, D:].astype(F32)
    gu = 0.5 * u * (1.0 + tu)
    rstd = stat_ref[:, 1:2]
    vhat = (0.5 * vs * (1.0 + tv) - stat_ref[:, 0:1]) * rstd
    vn = (vhat * lng_ref[...] + lnb_ref[...]).astype(BF16)
    return u, vs, gu, tu, tv, rstd, vhat, vn


def _masked_ws(ws_ref, g):
    row = lax.broadcasted_iota(jnp.int32, (CHUNK, CHUNK), 0)
    col = lax.broadcasted_iota(jnp.int32, (CHUNK, CHUNK), 1)
    return jnp.where(row >= col, ws_ref[g], 0.0).astype(BF16)


def _fwd_sgu(proj, lng, lnb, ws, bst, after=None):
    T = proj.shape[0]
    tc = min(T, 512)

    def body(u_ref, vs_ref, lng_ref, lnb_ref, ws_ref, bst_ref, a_ref, t_ref, stat_ref):
        gu, tu, tv, mu, rstd, vn = _sgu_forward_parts(u_ref, vs_ref, lng_ref, lnb_ref)
        t_ref[:, :D] = tu.astype(BF16)
        t_ref[:, D:] = tv.astype(BF16)
        lane = lax.broadcasted_iota(jnp.int32, (tc, 128), 1)
        stat_ref[...] = jnp.where(lane == 0, mu, jnp.where(lane == 1, rstd, 0.0))
        for g in range(GROUPS):
            wm = _masked_ws(ws_ref, g)
            cols = slice(g * CHUNK, (g + 1) * CHUNK)
            for c in range(tc // CHUNK):
                rows = slice(c * CHUNK, (c + 1) * CHUNK)
                mixed = _nn(wm, vn[rows, cols]) + bst_ref[:, g:g + 1]
                a_ref[rows, cols] = (gu[rows, cols] * mixed).astype(BF16)

    body, dep_specs, deps = _after(body, 6, after)
    return pl.pallas_call(
        body, name="fwd_sgu", grid=(T // tc,),
        in_specs=[pl.BlockSpec((tc, D), lambda i: (i, 0)), pl.BlockSpec((tc, D), lambda i: (i, 1)),
                  pl.BlockSpec((1, D), lambda i: (0, 0)), pl.BlockSpec((1, D), lambda i: (0, 0)),
                  pl.BlockSpec((GROUPS, CHUNK, CHUNK), lambda i: (0, 0, 0)),
                  pl.BlockSpec((CHUNK, GROUPS), lambda i: (0, 0))] + dep_specs,
        out_specs=[pl.BlockSpec((tc, D), lambda i: (i, 0)), pl.BlockSpec((tc, 2 * D), lambda i: (i, 0)),
                   pl.BlockSpec((tc, 128), lambda i: (i, 0))],
        out_shape=[SDS((T, D), BF16), SDS((T, 2 * D), BF16), SDS((T, 128), F32)],
        compiler_params=_params(1),
    )(proj, proj, lng, lnb, ws, bst, *deps)


def _rope_tables(posf, invf, sgn, after=None):
    T = posf.shape[0]
    tr = min(T, 1024)

    def body(pos_ref, invf_ref, sgn_ref, c_ref, s_ref):
        ang = pos_ref[...] * invf_ref[...]
        c_ref[...] = jnp.cos(ang)
        s = jnp.sin(ang)
        s_ref[:, :128] = jnp.where(sgn_ref[...] < 0.0, -s, 0.0)
        s_ref[:, 128:] = jnp.where(sgn_ref[...] > 0.0, s, 0.0)

    body, dep_specs, deps = _after(body, 3, after)
    return pl.pallas_call(
        body, name="rope_tables", grid=(T // tr,),
        in_specs=[pl.BlockSpec((tr, 1), lambda i: (i, 0)), pl.BlockSpec((1, 128), lambda i: (0, 0)),
                  pl.BlockSpec((1, 128), lambda i: (0, 0))] + dep_specs,
        out_specs=[pl.BlockSpec((tr, 128), lambda i: (i, 0)), pl.BlockSpec((tr, 256), lambda i: (i, 0))],
        out_shape=[SDS((T, 128), F32), SDS((T, 256), F32)],
        compiler_params=_params(1),
    )(posf, invf, sgn, *deps)


def _rope(v, c, s):
    v = v.astype(F32)
    return v * c + pltpu.roll(v, 128 - ROPE // 2, 1) * s[:, :128] + pltpu.roll(v, ROPE // 2, 1) * s[:, 128:]


def _rope_bwd(dv, c, s):
    return dv * c + pltpu.roll(dv * s[:, :128], ROPE // 2, 1) + pltpu.roll(dv * s[:, 128:], 128 - ROPE // 2, 1)


def _fold_masks(first):
    jj = lax.broadcasted_iota(jnp.int32, (CHUNK, CHUNK), 0)
    t = lax.broadcasted_iota(jnp.int32, (CHUNK, CHUNK), 1)
    prev = jj > t
    return prev, jnp.where(prev & first, -1e30, 0.0)


def _fold(band, prev):
    return jnp.where(prev, band[:CHUNK], band[CHUNK:])


def _unfold(folded, prev):
    return jnp.concatenate([jnp.where(prev, folded, 0.0), jnp.where(prev, 0.0, folded)], axis=0)


def _softmax_sink(s, sink, key_axis):
    m = jnp.maximum(jnp.max(s, axis=key_axis, keepdims=True), sink)
    p = jnp.exp(s - m)
    esink = jnp.exp(sink - m)
    inv = 1.0 / (jnp.sum(p, axis=key_axis, keepdims=True) + esink)
    return p * inv, esink * inv


def _head_pair_operand(slab, g):
    lo = lax.broadcasted_iota(jnp.int32, slab.shape, 1) < HEAD
    if g % 2 == 0:
        first = jnp.where(lo, slab, 0.0)
        second = pltpu.roll(first, HEAD, 1)
    else:
        second = jnp.where(lo, 0.0, slab)
        first = pltpu.roll(second, HEAD, 1)
    return jnp.concatenate([first, second], axis=0).astype(BF16)


def _head_pair_gradient(acc, g):
    top, bot = acc[:2 * CHUNK], acc[2 * CHUNK:]
    lo = lax.broadcasted_iota(jnp.int32, top.shape, 1) < HEAD
    if g % 2 == 0:
        return jnp.where(lo, top, 0.0) + pltpu.roll(jnp.where(lo, 0.0, bot), HEAD, 1)
    return pltpu.roll(jnp.where(lo, top, 0.0), HEAD, 1) + jnp.where(lo, 0.0, bot)


PAIRS_PER_KV = N_Q // N_KV // 2
KV_W = N_KV * HEAD


def _band(prev_ref, cur_ref, cols=slice(None)):
    return jnp.concatenate([prev_ref[:, cols], cur_ref[:, cols]], axis=0)


def _fwd_attn(proj, cos, sin, sinks):
    T = proj.shape[0]
    nb = T // CHUNK
    cur = lambda i: i
    prev = lambda i: jnp.maximum(i - 1, 0)

    def body(q_ref, kp_ref, kc_ref, vp_ref, vc_ref, cp_ref, cc_ref, sp_ref, sc_ref, sink_ref,
             o_ref, qr_ref, kr_ref, p_ref, psink_ref):
        prev_slot, bias = _fold_masks(pl.program_id(0) == 0)
        c_band, s_band = _band(cp_ref, cc_ref), _band(sp_ref, sc_ref)
        for j in range(KV_W // 128):
            cols = slice(j * 128, (j + 1) * 128)
            k_slab = _rope(_band(kp_ref, kc_ref, cols), c_band, s_band)
            kr_ref[:, cols] = k_slab[CHUNK:].astype(BF16)
            v_slab = _band(vp_ref, vc_ref, cols).astype(F32)
            for g in (2 * j, 2 * j + 1):
                k2 = _head_pair_operand(k_slab, g)
                v2 = _head_pair_operand(v_slab, g)
                pairs = [g * PAIRS_PER_KV + r for r in range(PAIRS_PER_KV)]
                qps = []
                for pair in pairs:
                    lanes = slice(pair * 128, (pair + 1) * 128)
                    qps.append((_rope(q_ref[:, lanes], cc_ref[...], sc_ref[...]) * (HEAD ** -0.5)).astype(BF16))
                    qr_ref[:, lanes] = qps[-1]
                s2 = _nt(k2, jnp.concatenate(qps, axis=0))
                pcols = []
                for r, pair in enumerate(pairs):
                    ps = []
                    for e in range(2):
                        head = 2 * pair + e
                        s = _fold(s2[e * 2 * CHUNK:(e + 1) * 2 * CHUNK, r * 128:(r + 1) * 128], prev_slot) + bias
                        p, psink = _softmax_sink(s, sink_ref[head], 0)
                        p = p.astype(BF16)
                        p_ref[head] = p
                        psink_ref[head:head + 1, :] = psink
                        ps.append(_unfold(p, prev_slot))
                    pcols.append(jnp.concatenate(ps, axis=0))
                o = _tn(jnp.concatenate(pcols, axis=1), v2).astype(BF16)
                for r, pair in enumerate(pairs):
                    o_ref[:, pair * 128:(pair + 1) * 128] = o[r * CHUNK:(r + 1) * CHUNK]

    table = lambda which, width: pl.BlockSpec((CHUNK, width), lambda i: (which(i), 0))
    return pl.pallas_call(
        body, name="fwd_attn", grid=(nb,),
        in_specs=[pl.BlockSpec((CHUNK, D), lambda i: (i, OFF_Q // D)),
                  pl.BlockSpec((CHUNK, KV_W), lambda i: (prev(i), OFF_K // KV_W)),
                  pl.BlockSpec((CHUNK, KV_W), lambda i: (i, OFF_K // KV_W)),
                  pl.BlockSpec((CHUNK, KV_W), lambda i: (prev(i), OFF_VA // KV_W)),
                  pl.BlockSpec((CHUNK, KV_W), lambda i: (i, OFF_VA // KV_W)),
                  table(prev, 128), table(cur, 128), table(prev, 256), table(cur, 256),
                  pl.BlockSpec(memory_space=pltpu.SMEM)],
        out_specs=[pl.BlockSpec((CHUNK, D), lambda i: (i, 0)), pl.BlockSpec((CHUNK, D), lambda i: (i, 0)),
                   pl.BlockSpec((CHUNK, KV_W), lambda i: (i, 0)),
                   pl.BlockSpec((None, N_Q, CHUNK, CHUNK), lambda i: (i, 0, 0, 0)),
                   pl.BlockSpec((None, N_Q, CHUNK), lambda i: (i, 0, 0))],
        out_shape=[SDS((T, D), BF16), SDS((T, D), BF16), SDS((T, KV_W), BF16),
                   SDS((nb, N_Q, CHUNK, CHUNK), BF16), SDS((nb, N_Q, CHUNK), F32)],
        compiler_params=_params(1),
    )(proj, proj, proj, proj, proj, cos, cos, sin, sin, sinks)


def _fwd_mix(a, att, proj, x, wa, wb, wo, g1, g2):
    T = x.shape[0]
    tm = min(T, 512)
    half = D // 2

    def body(a_ref, att_ref, ga0, ga1, gb0, gb1, x_ref, wa_ref, wb_ref, wo_ref, g1_ref, g2_ref,
             mg_ref, a2_ref, b2_ref, mix_ref, x1_ref, hf_ref):
        a2 = _nn(a_ref[...], wa_ref[...])
        b2 = _nn(att_ref[...], wb_ref[...])
        ga = jnp.concatenate([ga0[...], ga1[...]], axis=1).astype(F32)
        gb = jnp.concatenate([gb0[...], gb1[...]], axis=1).astype(F32)
        merged = (_sigmoid(ga) * a2 + _sigmoid(gb) * b2).astype(BF16)
        a2_ref[...] = a2.astype(BF16)
        b2_ref[...] = b2.astype(BF16)
        mg_ref[...] = merged
        mix = _nn(merged, wo_ref[...])
        mix_ref[...] = mix
        _, mh = _rms_stats(mix)
        x1 = x_ref[...] + mh * g1_ref[...]
        x1_ref[...] = x1
        _, xh = _rms_stats(x1)
        hf_ref[...] = (xh * g2_ref[...]).astype(BF16)

    row = lambda i: (i, 0)
    const = lambda i: (0, 0)
    gspec = lambda off: pl.BlockSpec((tm, half), lambda i: (i, off // half))
    return pl.pallas_call(
        body, name="fwd_mix", grid=(T // tm,),
        in_specs=[pl.BlockSpec((tm, D), row), pl.BlockSpec((tm, D), row),
                  gspec(OFF_GA), gspec(OFF_GA + half), gspec(OFF_GB), gspec(OFF_GB + half),
                  pl.BlockSpec((tm, D), row), _resident((D, D)), _resident((D, D)),
                  _resident((D, D)), pl.BlockSpec((1, D), const), pl.BlockSpec((1, D), const)],
        out_specs=[pl.BlockSpec((tm, D), row)] * 6,
        out_shape=[SDS((T, D), BF16), SDS((T, D), BF16), SDS((T, D), BF16), SDS((T, D), F32), SDS((T, D), F32),
                   SDS((T, D), BF16)],
        compiler_params=_params(1),
    )(a, att, proj, proj, proj, proj, x, wa, wb, wo, g1, g2)


FF_SPLIT = N_DEV
FF_TILE = D_FF // FF_SPLIT


def _fwd_ff(hf, wfi3, wfo, x1, tgt, g3):
    T = hf.shape[0]
    tm = min(T, 512)

    def body(hf_ref, wfi_ref, wfo_ref, x1_ref, tgt_ref, g3_ref, f_ref, dy_ref, dff_ref, dg3_ref, loss_ref, r_s):
        @pl.when(pl.program_id(0) == 0)
        def _():
            dg3_ref[...] = jnp.zeros_like(dg3_ref)
            loss_ref[...] = jnp.zeros_like(loss_ref)

        hf_t = hf_ref[...]
        for s in range(FF_SPLIT):
            cols = slice(s * FF_TILE, (s + 1) * FF_TILE)
            f = _nn(hf_t, wfi_ref[s]).astype(BF16)
            f_ref[:, cols] = f
            rl = jnp.maximum(f.astype(F32), 0.0)
            r_s[:, cols] = (rl * rl).astype(BF16)
        r3, fh = _rms_stats(_nn(r_s[...], wfo_ref[...]))
        e = x1_ref[...] + fh * g3_ref[...] - tgt_ref[...]
        loss_ref[...] += jnp.sum(e * e) * (0.5 / D)
        dy = e * (1.0 / D)
        dy_ref[...] = dy
        dg3_ref[...] += _colsum(dy * fh)
        dff_ref[...] = _rms_bwd(dy, fh, r3, g3_ref[...]).astype(BF16)

    row = lambda i: (i, 0)
    const = lambda i: (0, 0)
    return pl.pallas_call(
        body, name="fwd_ff", grid=(T // tm,),
        in_specs=[pl.BlockSpec((tm, D), row), _resident((FF_SPLIT, D, FF_TILE)), _resident((D_FF, D)),
                  pl.BlockSpec((tm, D), row),
                  pl.BlockSpec((tm, D), row), pl.BlockSpec((1, D), const)],
        out_specs=[pl.BlockSpec((tm, D_FF), row), pl.BlockSpec((tm, D), row),
                   pl.BlockSpec((tm, D), row), pl.BlockSpec((1, D), const), pl.BlockSpec((1, 128), const)],
        out_shape=[SDS((T, D_FF), BF16), SDS((T, D), F32), SDS((T, D), BF16), SDS((1, D), F32), SDS((1, 128), F32)],
        scratch_shapes=[pltpu.VMEM((tm, D_FF), BF16)],
        compiler_params=_params(1),
    )(hf, wfi3, wfo, x1, tgt, g3)


def _bwd_ff(dff, f, wfi3, wfo, x1, dy, mix, g1, g2):
    T = dff.shape[0]
    tm = min(T, 512)

    def body(dff_ref, f_ref, wfi_ref, wfo_ref, x1_ref, dy_ref, mix_ref, g1_ref, g2_ref,
             df_ref, dx1_ref, dmix_ref, dg2_ref, dg1_ref):
        @pl.when(pl.program_id(0) == 0)
        def _():
            dg2_ref[...] = jnp.zeros_like(dg2_ref)
            dg1_ref[...] = jnp.zeros_like(dg1_ref)

        dff_t = dff_ref[...]
        dhf = None
        for s in range(FF_SPLIT):
            cols = slice(s * FF_TILE, (s + 1) * FF_TILE)
            dr = _nt(dff_t, wfo_ref[cols, :])
            df = (dr * (2.0 * jnp.maximum(f_ref[:, cols].astype(F32), 0.0))).astype(BF16)
            df_ref[:, cols] = df
            part = _nt(df, wfi_ref[s])
            dhf = part if dhf is None else dhf + part
        r2, xh = _rms_stats(x1_ref[...])
        dg2_ref[...] += _colsum(dhf * xh)
        dx1 = dy_ref[...] + _rms_bwd(dhf, xh, r2, g2_ref[...])
        dx1_ref[...] = dx1
        r1, mh = _rms_stats(mix_ref[...])
        dg1_ref[...] += _colsum(dx1 * mh)
        dmix_ref[...] = _rms_bwd(dx1, mh, r1, g1_ref[...]).astype(BF16)

    row = lambda i: (i, 0)
    const = lambda i: (0, 0)
    return pl.pallas_call(
        body, name="bwd_ff", grid=(T // tm,),
        in_specs=[pl.BlockSpec((tm, D), row), pl.BlockSpec((tm, D_FF), row),
                  _resident((FF_SPLIT, D, FF_TILE)), _resident((D_FF, D)),
                  pl.BlockSpec((tm, D), row), pl.BlockSpec((tm, D), row), pl.BlockSpec((tm, D), row),
                  pl.BlockSpec((1, D), const), pl.BlockSpec((1, D), const)],
        out_specs=[pl.BlockSpec((tm, D_FF), row), pl.BlockSpec((tm, D), row),
                   pl.BlockSpec((tm, D), row), pl.BlockSpec((1, D), const), pl.BlockSpec((1, D), const)],
        out_shape=[SDS((T, D_FF), BF16), SDS((T, D), F32), SDS((T, D), BF16), SDS((1, D), F32), SDS((1, D), F32)],
        compiler_params=_params(1),
    )(dff, f, wfi3, wfo, x1, dy, mix, g1, g2)


def _wgrad_ff(hf, df, f, dff):
    T = hf.shape[0]
    tt = min(T, 2048)
    slabs = 2
    wide = slabs * FF_TILE

    def body(hf_ref, df_ref, f_ref, dff_ref, dwfi_ref, dwfo_ref, acc_i, acc_o):
        t = pl.program_id(1)

        @pl.when(t == 0)
        def _():
            acc_i[...] = jnp.zeros_like(acc_i)
            acc_o[...] = jnp.zeros_like(acc_o)

        acc_i[...] += _tn(hf_ref[...], df_ref[...])
        rl = jnp.maximum(f_ref[...].astype(F32), 0.0)
        acc_o[...] += _tn((rl * rl).astype(BF16), dff_ref[...])

        @pl.when(t == T // tt - 1)
        def _():
            for s in range(slabs):
                dwfi_ref[s] = acc_i[:, s * FF_TILE:(s + 1) * FF_TILE].astype(BF16)
            dwfo_ref[...] = acc_o[...].astype(BF16)

    return pl.pallas_call(
        body, name="wgrad_ff", grid=(D_FF // wide, T // tt),
        in_specs=[pl.BlockSpec((tt, D), lambda p, t: (t, 0)), pl.BlockSpec((tt, wide), lambda p, t: (t, p)),
                  pl.BlockSpec((tt, wide), lambda p, t: (t, p)), pl.BlockSpec((tt, D), lambda p, t: (t, 0))],
        out_specs=[pl.BlockSpec((slabs, D, FF_TILE), lambda p, t: (p, 0, 0)), pl.BlockSpec((wide, D), lambda p, t: (p, 0))],
        out_shape=[SDS((FF_SPLIT, D, FF_TILE), BF16), SDS((D_FF, D), BF16)],
        scratch_shapes=[pltpu.VMEM((D, wide), F32), pltpu.VMEM((wide, D), F32)],
        compiler_params=_params(2),
    )(hf, df, f, dff)


def _bwd_mix(dmix, proj, a2, b2, merged, a, att, wo, wa, wb, after=None):
    T = dmix.shape[0]
    tm = min(T, 512)
    half = D // 2
    last = T // tm - 1

    def body(dmix_ref, ga0, ga1, gb0, gb1, a2_ref, b2_ref, mg_ref, a_ref, att_ref, wo_ref, wa_ref, wb_ref,
             dg_ref, da_ref, datt_ref, dwo_ref, dwa_ref, dwb_ref, acc, stage, sem):
        t = pl.program_id(0)

        @pl.when(t == 0)
        def _():
            acc[...] = jnp.zeros_like(acc)

        dmix_t = dmix_ref[...]
        dmg = _nt(dmix_t, wo_ref[...])
        sa = _sigmoid(jnp.concatenate([ga0[...], ga1[...]], axis=1).astype(F32))
        sb = _sigmoid(jnp.concatenate([gb0[...], gb1[...]], axis=1).astype(F32))
        da2 = (dmg * sa).astype(BF16)
        db2 = (dmg * sb).astype(BF16)
        dg_ref[:, :D] = (dmg * a2_ref[...].astype(F32) * (sa * (1.0 - sa))).astype(BF16)
        dg_ref[:, D:] = (dmg * b2_ref[...].astype(F32) * (sb * (1.0 - sb))).astype(BF16)
        da_ref[...] = _nt(da2, wa_ref[...]).astype(BF16)
        datt_ref[...] = _nt(db2, wb_ref[...]).astype(BF16)
        acc[0] += _tn(mg_ref[...], dmix_t)
        acc[1] += _tn(a_ref[...], da2)
        acc[2] += _tn(att_ref[...], db2)

        @pl.when(t == last)
        def _():
            for k, dw_ref in enumerate((dwo_ref, dwa_ref, dwb_ref)):
                stage[...] = acc[k].astype(BF16)
                out = pltpu.make_async_copy(stage, dw_ref, sem)
                out.start()
                out.wait()

    row = lambda i: (i, 0)
    gspec = lambda off: pl.BlockSpec((tm, half), lambda i: (i, off // half))
    body, dep_specs, deps = _after(body, 13, after)
    return pl.pallas_call(
        body, name="bwd_mix", grid=(T // tm,),
        in_specs=[pl.BlockSpec((tm, D), row), gspec(OFF_GA), gspec(OFF_GA + half), gspec(OFF_GB), gspec(OFF_GB + half)]
        + [pl.BlockSpec((tm, D), row)] * 5 + [_resident((D, D))] * 3 + dep_specs,
        out_specs=[pl.BlockSpec((tm, 2 * D), row), pl.BlockSpec((tm, D), row), pl.BlockSpec((tm, D), row)] + [_ANY] * 3,
        out_shape=[SDS((T, 2 * D), BF16), SDS((T, D), BF16), SDS((T, D), BF16)] + [SDS((D, D), BF16)] * 3,
        scratch_shapes=[pltpu.VMEM((3, D, D), F32), pltpu.VMEM((D, D), BF16), pltpu.SemaphoreType.DMA],
        compiler_params=_params(1),
    )(dmix, proj, proj, proj, proj, a2, b2, merged, a, att, wo, wa, wb, *deps)


def _bwd_attn(qr, kr, probs, psink, proj, cos, sin, datt, after=None):
    T = proj.shape[0]
    nb = T // CHUNK
    cur = lambda i: jnp.minimum(i, nb - 1)
    prev = lambda i: jnp.maximum(jnp.minimum(i, nb - 1) - 1, 0)

    def body(q_ref, kp_ref, kc_ref, vp_ref, vc_ref, cp_ref, cc_ref, sp_ref, sc_ref, p_ref, psink_ref, do_ref,
             dq_ref, dkv_ref, dsink_ref, carry_k, carry_v):
        i = pl.program_id(0)

        @pl.when(i == 0)
        def _():
            carry_k[...] = jnp.zeros_like(carry_k)
            carry_v[...] = jnp.zeros_like(carry_v)
            dsink_ref[...] = jnp.zeros_like(dsink_ref)

        @pl.when(i < nb)
        def _():
            prev_slot, _ = _fold_masks(i == 0)
            c_band, s_band = _band(cp_ref, cc_ref), _band(sp_ref, sc_ref)
            for j in range(KV_W // 128):
                cols = slice(j * 128, (j + 1) * 128)
                k_slab = _band(kp_ref, kc_ref, cols).astype(F32)
                v_slab = _band(vp_ref, vc_ref, cols).astype(F32)
                dk_slab = jnp.zeros((2 * CHUNK, 128), F32)
                dv_slab = jnp.zeros((2 * CHUNK, 128), F32)
                for g in (2 * j, 2 * j + 1):
                    k2 = _head_pair_operand(k_slab, g)
                    v2 = _head_pair_operand(v_slab, g)
                    pairs = [g * PAIRS_PER_KV + r for r in range(PAIRS_PER_KV)]
                    q_stack = jnp.concatenate([q_ref[:, pr * 128:(pr + 1) * 128] for pr in pairs], axis=0)
                    do_stack = jnp.concatenate([do_ref[:, pr * 128:(pr + 1) * 128] for pr in pairs], axis=0)
                    dp2 = _nt(v2, do_stack)
                    pcols, dscols = [], []
                    for r, pair in enumerate(pairs):
                        ps, dss = [], []
                        for e in range(2):
                            head = 2 * pair + e
                            p_b = p_ref[head]
                            p = p_b.astype(F32)
                            dp = _fold(dp2[e * 2 * CHUNK:(e + 1) * 2 * CHUNK, r * 128:(r + 1) * 128], prev_slot)
                            delta = jnp.sum(p * dp, axis=0, keepdims=True)
                            ps.append(_unfold(p_b, prev_slot))
                            dss.append(_unfold((p * (dp - delta)).astype(BF16), prev_slot))
                            dsink_ref[head:head + 1, :] -= psink_ref[head:head + 1, :] * delta
                        pcols.append(jnp.concatenate(ps, axis=0))
                        dscols.append(jnp.concatenate(dss, axis=0))
                    ds2 = jnp.concatenate(dscols, axis=1)
                    dq = _tn(ds2, k2) * (HEAD ** -0.5)
                    for r, pair in enumerate(pairs):
                        dq_ref[:, pair * 128:(pair + 1) * 128] = _rope_bwd(
                            dq[r * CHUNK:(r + 1) * CHUNK], cc_ref[...], sc_ref[...]).astype(BF16)
                    dk_slab = dk_slab + _head_pair_gradient(_nn(ds2, q_stack), g)
                    dv_slab = dv_slab + _head_pair_gradient(_nn(jnp.concatenate(pcols, axis=1), do_stack), g)
                dk_slab = _rope_bwd(dk_slab, c_band, s_band)
                vcols = slice(KV_W + j * 128, KV_W + (j + 1) * 128)
                dkv_ref[:, cols] = (carry_k[:, cols] + dk_slab[:CHUNK]).astype(BF16)
                dkv_ref[:, vcols] = (carry_v[:, cols] + dv_slab[:CHUNK]).astype(BF16)
                carry_k[:, cols] = dk_slab[CHUNK:]
                carry_v[:, cols] = dv_slab[CHUNK:]

        @pl.when(i == nb)
        def _():
            dkv_ref[:, :KV_W] = carry_k[...].astype(BF16)
            dkv_ref[:, KV_W:] = carry_v[...].astype(BF16)

    table = lambda which, width: pl.BlockSpec((CHUNK, width), lambda i: (which(i), 0))
    body, dep_specs, deps = _after(body, 12, after)
    return pl.pallas_call(
        body, name="bwd_attn", grid=(nb + 1,),
        in_specs=[pl.BlockSpec((CHUNK, D), lambda i: (cur(i), 0)),
                  pl.BlockSpec((CHUNK, KV_W), lambda i: (prev(i), 0)),
                  pl.BlockSpec((CHUNK, KV_W), lambda i: (cur(i), 0)),
                  pl.BlockSpec((CHUNK, KV_W), lambda i: (prev(i), OFF_VA // KV_W)),
                  pl.BlockSpec((CHUNK, KV_W), lambda i: (cur(i), OFF_VA // KV_W)),
                  table(prev, 128), table(cur, 128), table(prev, 256), table(cur, 256),
                  pl.BlockSpec((None, N_Q, CHUNK, CHUNK), lambda i: (cur(i), 0, 0, 0)),
                  pl.BlockSpec((None, N_Q, CHUNK), lambda i: (cur(i), 0, 0)),
                  pl.BlockSpec((CHUNK, D), lambda i: (cur(i), 0))] + dep_specs,
        out_specs=[pl.BlockSpec((CHUNK, D), lambda i: (cur(i), 0)),
                   pl.BlockSpec((CHUNK, 2 * KV_W), lambda i: (jnp.maximum(i - 1, 0), 0)),
                   pl.BlockSpec((N_Q, CHUNK), lambda i: (0, 0))],
        out_shape=[SDS((T, D), BF16), SDS((T, 2 * KV_W), BF16), SDS((N_Q, CHUNK), F32)],
        scratch_shapes=[pltpu.VMEM((CHUNK, KV_W), F32), pltpu.VMEM((CHUNK, KV_W), F32)],
        compiler_params=_params(1),
    )(qr, kr, kr, proj, proj, cos, cos, sin, sin, probs, psink, datt, *deps)


def _bwd_sgu(proj, tanhs, stats, da, lng, lnb, ws, bst):
    T = proj.shape[0]
    tc = min(T, 512)
    nsteps = T // tc

    def body(u_ref, vs_ref, t_ref, stat_ref, da_ref, lng_ref, lnb_ref, ws_ref, bst_ref,
             duv_ref, dws_ref, dbs_ref, dlng_ref, dlnb_ref, dvn_s, dgu_s, dmx_sum):
        i = pl.program_id(0)

        @pl.when(i == 0)
        def _():
            dws_ref[...] = jnp.zeros_like(dws_ref)
            dlng_ref[...] = jnp.zeros_like(dlng_ref)
            dlnb_ref[...] = jnp.zeros_like(dlnb_ref)
            dmx_sum[...] = jnp.zeros_like(dmx_sum)

        u, vs, gu, tu, tv, rstd, vhat, vn = _sgu_forward_replay(u_ref, vs_ref, t_ref, stat_ref, lng_ref, lnb_ref)
        da = da_ref[...].astype(F32)
        for g in range(GROUPS):
            wm = _masked_ws(ws_ref, g)
            cols = slice(g * CHUNK, (g + 1) * CHUNK)
            dws = jnp.zeros((CHUNK, CHUNK), F32)
            dsum = jnp.zeros((CHUNK, CHUNK), F32)
            for c in range(tc // CHUNK):
                rows = slice(c * CHUNK, (c + 1) * CHUNK)
                vn_cg = vn[rows, cols]
                mixed = _nn(wm, vn_cg) + bst_ref[:, g:g + 1]
                dgu_s[rows, cols] = da[rows, cols] * mixed
                dmx = da[rows, cols] * gu[rows, cols]
                dmxb = dmx.astype(BF16)
                dws = dws + _nt(dmxb, vn_cg)
                dsum = dsum + dmx
                dvn_s[rows, cols] = _tn(wm, dmxb)
            dws_ref[g] += dws
            dmx_sum[:, cols] += dsum
        dvn = dvn_s[...]
        dlng_ref[...] += _colsum(dvn * vhat)
        dlnb_ref[...] += _colsum(dvn)
        dvh = dvn * lng_ref[...]
        dgv = rstd * (dvh - jnp.mean(dvh, axis=-1, keepdims=True) - vhat * jnp.mean(dvh * vhat, axis=-1, keepdims=True))
        duv_ref[:, :D] = (dgu_s[...] * _gelu_grad(u, tu)).astype(BF16)
        duv_ref[:, D:] = (dgv * _gelu_grad(vs, tv)).astype(BF16)

        @pl.when(i == nsteps - 1)
        def _():
            row = lax.broadcasted_iota(jnp.int32, (CHUNK, CHUNK), 0)
            col = lax.broadcasted_iota(jnp.int32, (CHUNK, CHUNK), 1)
            for g in range(GROUPS):
                dws_ref[g] = jnp.where(row >= col, dws_ref[g], 0.0)
                dbs_ref[g:g + 1, :] = _colsum(dmx_sum[:, g * CHUNK:(g + 1) * CHUNK].T)

    const2 = lambda i: (0, 0)
    return pl.pallas_call(
        body, name="bwd_sgu", grid=(nsteps,),
        in_specs=[pl.BlockSpec((tc, D), lambda i: (i, 0)), pl.BlockSpec((tc, D), lambda i: (i, 1)),
                  pl.BlockSpec((tc, 2 * D), lambda i: (i, 0)), pl.BlockSpec((tc, 128), lambda i: (i, 0)),
                  pl.BlockSpec((tc, D), lambda i: (i, 0)), pl.BlockSpec((1, D), const2), pl.BlockSpec((1, D), const2),
                  pl.BlockSpec((GROUPS, CHUNK, CHUNK), lambda i: (0, 0, 0)), pl.BlockSpec((CHUNK, GROUPS), const2)],
        out_specs=[pl.BlockSpec((tc, 2 * D), lambda i: (i, 0)), pl.BlockSpec((GROUPS, CHUNK, CHUNK), lambda i: (0, 0, 0)),
                   pl.BlockSpec((GROUPS, CHUNK), const2), pl.BlockSpec((1, D), const2), pl.BlockSpec((1, D), const2)],
        out_shape=[SDS((T, 2 * D), BF16), SDS((GROUPS, CHUNK, CHUNK), F32), SDS((GROUPS, CHUNK), F32),
                   SDS((1, D), F32), SDS((1, D), F32)],
        scratch_shapes=[pltpu.VMEM((tc, D), F32), pltpu.VMEM((tc, D), F32), pltpu.VMEM((CHUNK, D), F32)],
        compiler_params=_params(1),
    )(proj, proj, tanhs, stats, da, lng, lnb, ws, bst)


IN_SEG_WIDTHS = (2 * D, D, 2 * N_KV * HEAD, 2 * D)


def _resident(shape):
    return pl.BlockSpec(shape, lambda *_: (0,) * len(shape), pipeline_mode=pl.Buffered(1))


def _bwd_in(duv, dq, dkv, dg, win_t, x, dx1, g0, after=None):
    T = x.shape[0]
    tm = min(T, 512)

    def body(duv_ref, dq_ref, dkv_ref, dg_ref, w_ref, x_ref, dx1_ref, g0_ref, gx_ref, dg0_ref):
        @pl.when(pl.program_id(0) == 0)
        def _():
            dg0_ref[...] = jnp.zeros_like(dg0_ref)

        dh, off = None, 0
        for ref, width in zip((duv_ref, dq_ref, dkv_ref, dg_ref), IN_SEG_WIDTHS):
            part = _nn(ref[...], w_ref[off:off + width, :])
            dh = part if dh is None else dh + part
            off += width
        r0, xh = _rms_stats(x_ref[...])
        dg0_ref[...] += _colsum(dh * xh)
        gx_ref[...] = dx1_ref[...] + _rms_bwd(dh, xh, r0, g0_ref[...])

    row = lambda i: (i, 0)
    body, dep_specs, deps = _after(body, 8, after)
    return pl.pallas_call(
        body, name="bwd_in", grid=(T // tm,),
        in_specs=[pl.BlockSpec((tm, w), row) for w in IN_SEG_WIDTHS] + [
            _resident((IN_W, D)), pl.BlockSpec((tm, D), row), pl.BlockSpec((tm, D), row),
            pl.BlockSpec((1, D), lambda i: (0, 0))] + dep_specs,
        out_specs=[pl.BlockSpec((tm, D), row), pl.BlockSpec((1, D), lambda i: (0, 0))],
        out_shape=[SDS((T, D), F32), SDS((1, D), F32)],
        compiler_params=_params(1),
    )(duv, dq, dkv, dg, win_t, x, dx1, g0, *deps)


def _wgrad_rows(h, segs, first_row, into, name):
    T = h.shape[0]
    tt = min(T, 2048)
    widths = [s.shape[1] for s in segs]
    rows = sum(widths)
    n_in = 1 + len(segs) + (into is not None)

    def body(*refs):
        h_ref, seg_refs = refs[0], refs[1:1 + len(segs)]
        dw_ref, acc, stage, sem = refs[n_in], refs[n_in + 1], refs[n_in + 2], refs[n_in + 3]
        t = pl.program_id(0)

        @pl.when(t == 0)
        def _():
            acc[...] = jnp.zeros_like(acc)

        off = 0
        for ref, width in zip(seg_refs, widths):
            acc[off:off + width, :] += _tn(ref[...], h_ref[...])
            off += width

        @pl.when(t == T // tt - 1)
        def _():
            stage[...] = acc[...].astype(BF16)
            out = pltpu.make_async_copy(stage, dw_ref.at[pl.ds(first_row, rows)], sem)
            out.start()
            out.wait()

    row = lambda t: (t, 0)
    return pl.pallas_call(
        body, name=name, grid=(T // tt,),
        in_specs=[pl.BlockSpec((tt, D), row)] + [pl.BlockSpec((tt, w), row) for w in widths] + [_ANY] * (into is not None),
        out_specs=_ANY,
        out_shape=SDS((IN_W, D), BF16),
        input_output_aliases={} if into is None else {n_in - 1: 0},
        scratch_shapes=[pltpu.VMEM((rows, D), F32), pltpu.VMEM((rows, D), BF16), pltpu.SemaphoreType.DMA],
        compiler_params=_params(1),
    )(h, *segs, *([] if into is None else [into]))


def _place():
    x, y, c = lax.axis_index("x"), lax.axis_index("y"), lax.axis_index("c")
    return x, y, c, 4 * x + 2 * y + c


def _peers(x, y, c):
    out = []
    for mask in range(1, N_DEV):
        px = 1 - x if mask & 4 else x
        py = 1 - y if mask & 2 else y
        pc = 1 - c if mask & 1 else c
        out.append(((px, py, pc), 4 * px + 2 * py + pc))
    return out


def _all_to_all(arrays, gather, name, after=None):
    n = len(arrays)

    def body(*refs):
        ins, outs = refs[:n], refs[n:2 * n]
        send_sems, recv_sems, local_sems = refs[2 * n:]
        x, y, c, me = _place()
        local, sends, recvs = [], [], []
        for a in range(n):
            src_own = ins[a] if gather[a] else ins[a].at[me]
            local.append(pltpu.make_async_copy(src_own, outs[a].at[me], local_sems.at[a]))
            for k, (peer, pid) in enumerate(_peers(x, y, c)):
                sem = a * (N_DEV - 1) + k
                src = ins[a] if gather[a] else ins[a].at[pid]
                sends.append(pltpu.make_async_remote_copy(
                    src_ref=src, dst_ref=outs[a].at[me], send_sem=send_sems.at[sem], recv_sem=recv_sems.at[sem],
                    device_id=peer, device_id_type=MESH))
                recvs.append(pltpu.make_async_remote_copy(
                    src_ref=src, dst_ref=outs[a].at[pid], send_sem=send_sems.at[sem], recv_sem=recv_sems.at[sem],
                    device_id=peer, device_id_type=MESH))
        for cp in local + sends:
            cp.start()
        for cp in recvs:
            cp.wait_recv()
        for cp in sends:
            cp.wait_send()
        for cp in local:
            cp.wait()

    out_shape = [SDS((N_DEV,) + a.shape if gt else a.shape, a.dtype) for a, gt in zip(arrays, gather)]
    nsem = n * (N_DEV - 1)
    body, dep_specs, deps = _after(body, n, after)
    return pl.pallas_call(
        body, name=name,
        in_specs=[pl.BlockSpec(memory_space=pl.ANY)] * n + dep_specs,
        out_specs=[pl.BlockSpec(memory_space=pl.ANY)] * n,
        out_shape=out_shape,
        scratch_shapes=[pltpu.SemaphoreType.DMA((nsem,)), pltpu.SemaphoreType.DMA((nsem,)), pltpu.SemaphoreType.DMA((n,))],
    )(*arrays, *deps)


_HBM = pl.BlockSpec(memory_space=pltpu.HBM)
_SEM = pl.BlockSpec(memory_space=pltpu.SEMAPHORE)
_EFFECT = pltpu.SideEffectType.DATAFLOW_SIDE_EFFECTING
GATHER = "gather"
SCATTER = "scatter"
SPREAD = "spread"


def _zone_shape(a, mode):
    if mode == GATHER:
        return (N_DEV,) + a.shape
    return (N_DEV - 1,) + (a.shape[1:] if mode == SCATTER else a.shape)


def _start_copies(arrays, modes, name, after=None):
    n = len(arrays)
    zones = [lax.empty(_zone_shape(a, m), a.dtype) for a, m in zip(arrays, modes)]

    def body(*refs):
        ins, lands = refs[:n], refs[n:2 * n]
        send_sems, recv_sems = refs[-2 * n - 3], refs[-2 * n - 2]
        token = refs[-1]
        x, y, c, me = _place()
        for a in range(n):
            for k, (peer, pid) in enumerate(_peers(x, y, c)):
                src = ins[a].at[pid] if modes[a] == SCATTER else ins[a]
                dst = lands[a].at[me] if modes[a] == GATHER else lands[a].at[k]
                pltpu.make_async_remote_copy(src_ref=src, dst_ref=dst, send_sem=send_sems.at[a], recv_sem=recv_sems.at[a],
                                             device_id=peer, device_id_type=MESH).start()
            if modes[a] == GATHER:
                pltpu.make_async_remote_copy(src_ref=ins[a], dst_ref=lands[a].at[me], send_sem=send_sems.at[a],
                                             recv_sem=recv_sems.at[a], device_id=(x, y, c), device_id_type=MESH).start()
        token[...] = jnp.zeros_like(token)

    hbm = lambda a: pltpu.HBM(a.shape, a.dtype)
    sems = pltpu.SemaphoreType.DMA((n,))
    extra = [] if after is None else [after]
    operands = [pltpu.with_memory_space_constraint(a, pltpu.HBM) for a in list(arrays) + zones]
    res = pl.pallas_call(
        body, name=name,
        out_shape=(sems, sems, *[hbm(a) for a in arrays], *[hbm(z) for z in zones], SDS((8, 128), F32)),
        in_specs=[_HBM] * (2 * n) + [_ANY] * len(extra),
        out_specs=(_SEM, _SEM, *[_HBM] * (2 * n), pl.BlockSpec(memory_space=pltpu.VMEM)),
        input_output_aliases={i: 2 + i for i in range(2 * n)},
        compiler_params=pltpu.CompilerParams(has_side_effects=_EFFECT),
    )(*operands, *extra)
    return res[0], res[1], list(res[2:2 + n]), list(res[2 + n:2 + 2 * n]), res[-1]


def _wait_copies(started, after, name, count=N_DEV - 1):
    send_sems, recv_sems, thru, zones, _ = started
    nt, nz = len(thru), len(zones)

    def body(*refs):
        lands = refs[nt:nt + nz]
        send_ref, recv_ref = refs[nt + nz], refs[nt + nz + 1]
        x, y, c, _ = _place()
        for a in range(nz):
            blocks = lands[a].at[pl.ds(0, count)]
            cp = pltpu.make_async_remote_copy(src_ref=blocks, dst_ref=blocks, send_sem=send_ref.at[a], recv_sem=recv_ref.at[a],
                                              device_id=(x, y, 1 - c), device_id_type=MESH)
            cp.wait_send()
            cp.wait_recv()

    hbm = lambda a: pltpu.HBM(a.shape, a.dtype)
    res = pl.pallas_call(
        body, name=name,
        out_shape=tuple(hbm(a) for a in thru + zones),
        in_specs=[_HBM] * (nt + nz) + [_SEM, _SEM, _ANY],
        out_specs=tuple([_HBM] * (nt + nz)),
        input_output_aliases={i: i for i in range(nt + nz)},
        compiler_params=pltpu.CompilerParams(has_side_effects=_EFFECT),
    )(*thru, *zones, send_sems, recv_sems, after)
    return list(res[:nt]), list(res[nt:])


def _split_start(body, arrays, zones, name, after):
    n = len(arrays) + len(zones)
    hbm = lambda a: pltpu.HBM(a.shape, a.dtype)
    sems = pltpu.SemaphoreType.DMA((max(len(zones), 1),))
    extra = [] if after is None else [after]
    operands = [pltpu.with_memory_space_constraint(a, pltpu.HBM) for a in list(arrays) + list(zones)]
    res = pl.pallas_call(
        body, name=name,
        out_shape=(sems, sems, *[hbm(a) for a in operands], SDS((8, 128), F32)),
        in_specs=[_HBM] * n + [_ANY] * len(extra),
        out_specs=(_SEM, _SEM, *[_HBM] * n, pl.BlockSpec(memory_space=pltpu.VMEM)),
        input_output_aliases={i: 2 + i for i in range(n)},
        compiler_params=pltpu.CompilerParams(has_side_effects=_EFFECT),
    )(*operands, *extra)
    return res[0], res[1], list(res[2:2 + len(arrays)]), list(res[2 + len(arrays):2 + n]), res[-1]


def _gather_first_leg(shard, name, after=None):
    zone = lax.empty((N_DEV,) + shard.shape, shard.dtype)
    extra = 0 if after is None else 1

    def body(*refs):
        src, land = refs[0], refs[1]
        send_sem, recv_sem, token = refs[2 + extra], refs[3 + extra], refs[-1]
        x, y, c, me = _place()
        for peer in ((x, y, c), (x, y, 1 - c), (1 - x, y, c), (x, 1 - y, c), (1 - x, 1 - y, c)):
            pltpu.make_async_remote_copy(src_ref=src, dst_ref=land.at[me], send_sem=send_sem.at[0], recv_sem=recv_sem.at[0],
                                         device_id=peer, device_id_type=MESH).start()
        token[...] = jnp.zeros_like(token)

    return _split_start(body, [shard], [zone], name, after)


def _gather_second_leg(zone, name, after=None):
    extra = 0 if after is None else 1

    def body(*refs):
        land = refs[0]
        send_sem, recv_sem, token = refs[1 + extra], refs[2 + extra], refs[-1]
        x, y, c, _ = _place()
        for px, py in ((1 - x, y), (x, 1 - y), (1 - x, 1 - y)):
            slot = 4 * px + 2 * py + c
            pltpu.make_async_remote_copy(src_ref=land.at[slot], dst_ref=land.at[slot], send_sem=send_sem.at[0],
                                         recv_sem=recv_sem.at[0], device_id=(x, y, 1 - c), device_id_type=MESH).start()
        token[...] = jnp.zeros_like(token)

    return _split_start(body, [], [zone], name, after)


UPDATE_BLOCK_ELEMS = 384 * 1024


def _update_rows(R, C):
    fits = [t for t in range(8, R + 1, 8) if R % t == 0 and t * C <= UPDATE_BLOCK_ELEMS]
    whole = [t for t in fits if t % 16 == 0]
    return max(whole or fits)


def _adamw_math(g, w, m, v):
    m2 = ADAM_B1 * m + (1.0 - ADAM_B1) * g
    v2 = ADAM_B2 * v + (1.0 - ADAM_B2) * (g * g)
    m_hat = m2 / (1.0 - ADAM_B1 ** ADAM_STEP)
    v_hat = v2 / (1.0 - ADAM_B2 ** ADAM_STEP)
    delta = -ADAM_LR * (m_hat / (jnp.sqrt(v_hat) + ADAM_EPS) + ADAM_WD * w)
    return delta, m2, v2


def _sum_adamw(parts, w, m, v, name):
    R, C = w.shape
    tr = _update_rows(R, C)

    def body(p_ref, w_ref, m_ref, v_ref, g_ref, d_ref, m2_ref, v2_ref):
        g = p_ref[0]
        for k in range(1, N_DEV):
            g = g + p_ref[k]
        g_ref[...] = g
        d_ref[...], m2_ref[...], v2_ref[...] = _adamw_math(g, w_ref[...], m_ref[...], v_ref[...])

    blk = pl.BlockSpec((tr, C), lambda i: (i, 0))
    return pl.pallas_call(
        body, name=name, grid=(R // tr,),
        in_specs=[pl.BlockSpec((N_DEV, tr, C), lambda i: (0, i, 0)), blk, blk, blk],
        out_specs=[blk] * 4,
        out_shape=[SDS((R, C), F32)] * 4,
        compiler_params=_params(1),
    )(parts, w, m, v)


def _sum_adamw_peers(me, own, parts, w, m, v, name, replicated, also_rows=None):
    R, C = w.shape
    tr = _update_rows(R, C)
    assert also_rows is None or tr == R

    def body(me_ref, own_ref, p_ref, w_ref, m_ref, v_ref, g_ref, d_ref, m2_ref, v2_ref, *extra):
        if replicated:
            mine = me_ref[0]
            g = None
            for j in range(N_DEV):
                k = jnp.maximum(jnp.bitwise_xor(mine, j) - 1, 0)
                term = jnp.where(mine == j, own_ref[...], p_ref[k])
                g = term if g is None else g + term
        else:
            g = own_ref[...].astype(F32)
            for k in range(N_DEV - 1):
                g = g + p_ref[k].astype(F32)
        results = (g,) + _adamw_math(g, w_ref[...], m_ref[...], v_ref[...])
        for ref, val in zip((g_ref, d_ref, m2_ref, v2_ref), results):
            ref[...] = val
        for ref, val in zip(extra, results):
            ref[...] = val[also_rows[0]:also_rows[1]]

    blk = pl.BlockSpec((tr, C), lambda i, me_ref: (i, 0))
    own_spec = blk if replicated else pl.BlockSpec((None, tr, C), lambda i, me_ref: (me_ref[0], i, 0))
    n_also = 0 if also_rows is None else also_rows[1] - also_rows[0]
    also_specs = [pl.BlockSpec((n_also, C), lambda i, me_ref: (0, 0))] * (4 if also_rows else 0)
    return pl.pallas_call(
        body, name=name,
        grid_spec=pltpu.PrefetchScalarGridSpec(
            num_scalar_prefetch=1, grid=(R // tr,),
            in_specs=[own_spec, pl.BlockSpec((N_DEV - 1, tr, C), lambda i, me_ref: (0, i, 0)), blk, blk, blk],
            out_specs=[blk] * 4 + also_specs),
        out_shape=[SDS((R, C), F32)] * 4 + [SDS((n_also, C), F32)] * len(also_specs),
        compiler_params=_params(1),
    )(me, own, parts, w, m, v)


UPDATE_CHUNK_ELEMS = 64 * 1024


def _chunk_rows(R, C):
    fits = [t for t in range(16, R + 1, 16) if R % t == 0 and t * C <= UPDATE_CHUNK_ELEMS]
    return max(fits) if fits else R


def _sum_adamw_stream(me, own, parts, w, m, v, name):
    R, C = w.shape
    tr = _chunk_rows(R, C)
    n = R // tr

    def body(me_ref, own_ref, p_ref, w_ref, m_ref, v_ref, g_ref, d_ref, m2_ref, v2_ref,
             own_v, p_v, w_v, m_v, v_v, g_v, d_v, m2_v, v2_v, in_sems, out_sems):
        mine = me_ref[0]
        loads = []
        for i in range(n):
            r = pl.ds(i * tr, tr)
            cps = [pltpu.make_async_copy(own_ref.at[mine, r], own_v.at[r], in_sems.at[i, 0]),
                   pltpu.make_async_copy(p_ref.at[:, r], p_v.at[:, r], in_sems.at[i, 1]),
                   pltpu.make_async_copy(w_ref.at[r], w_v.at[r], in_sems.at[i, 2]),
                   pltpu.make_async_copy(m_ref.at[r], m_v.at[r], in_sems.at[i, 3]),
                   pltpu.make_async_copy(v_ref.at[r], v_v.at[r], in_sems.at[i, 4])]
            for cp in cps:
                cp.start()
            loads.append(cps)
        stores = []
        for i in range(n):
            r = pl.ds(i * tr, tr)
            for cp in loads[i]:
                cp.wait()
            g = own_v[r].astype(F32)
            for k in range(N_DEV - 1):
                g = g + p_v[k, r].astype(F32)
            results = (g,) + _adamw_math(g, w_v[r], m_v[r], v_v[r])
            for j, (stage, out, val) in enumerate(zip((g_v, d_v, m2_v, v2_v), (g_ref, d_ref, m2_ref, v2_ref), results)):
                stage[r] = val
                cp = pltpu.make_async_copy(stage.at[r], out.at[r], out_sems.at[i, j])
                cp.start()
                stores.append(cp)
        for cp in stores:
            cp.wait()

    hbm = pl.BlockSpec(memory_space=pl.ANY)
    return pl.pallas_call(
        body, name=name,
        in_specs=[pl.BlockSpec(memory_space=pltpu.SMEM)] + [hbm] * 5,
        out_specs=[hbm] * 4,
        out_shape=[SDS((R, C), F32)] * 4,
        scratch_shapes=[pltpu.VMEM((R, C), own.dtype), pltpu.VMEM((N_DEV - 1, R, C), parts.dtype)]
        + [pltpu.VMEM((R, C), F32)] * 7
        + [pltpu.SemaphoreType.DMA((n, 5)), pltpu.SemaphoreType.DMA((n, 4))],
        compiler_params=_params(0),
    )(me, own, parts, w, m, v)


SMALL = ("ln_v_gain", "ln_v_bias", "w_spatial", "b_spatial", "sinks", "norm_mix_post", "norm_ff_pre", "norm_ff_post")
SMALL_ROWS = {"ln_v_gain": 8, "ln_v_bias": 8, "w_spatial": 1024, "b_spatial": 8, "sinks": 8,
              "norm_mix_post": 8, "norm_ff_pre": 8, "norm_ff_post": 8}
SMALL_PACK_ROWS = 1152


def _pack_small(vals):
    rows = []
    for name in SMALL:
        flat = vals[name].reshape(-1)
        pad = SMALL_ROWS[name] * 128 - flat.shape[0]
        if pad:
            flat = jnp.concatenate([flat, jnp.zeros((pad,), F32)])
        rows.append(flat.reshape(SMALL_ROWS[name], 128))
    rows.append(jnp.zeros((SMALL_PACK_ROWS - sum(SMALL_ROWS.values()), 128), F32))
    return jnp.concatenate(rows, axis=0)


def _unpack_small(packed, shapes):
    out, r = {}, 0
    for name in SMALL:
        n = 1
        for s in shapes[name]:
            n *= s
        out[name] = packed[r:r + SMALL_ROWS[name]].reshape(-1)[:n].reshape(shapes[name])
        r += SMALL_ROWS[name]
    return out


def _rope_rows():
    d = jnp.arange(128) % HEAD
    inv = ROPE_THETA ** (-(2.0 * (d % (ROPE // 2))).astype(F32) / ROPE)
    invf = jnp.where(d < ROPE, inv, 0.0).astype(F32).reshape(1, 128)
    sgn = jnp.where(d < ROPE // 2, -1.0, jnp.where(d < ROPE, 1.0, 0.0)).astype(F32).reshape(1, 128)
    return invf, sgn


def kernel(x, positions, w_in, ln_v_gain, ln_v_bias, w_spatial, b_spatial, sinks, w_a, w_b, w_o, norm_mix_pre, norm_mix_post, w_ff_in, w_ff_out, norm_ff_pre, norm_ff_post, loss_target, m_w_in, m_ln_v_gain, m_ln_v_bias, m_w_spatial, m_b_spatial, m_sinks, m_w_a, m_w_b, m_w_o, m_norm_mix_pre, m_norm_mix_post, m_w_ff_in, m_w_ff_out, m_norm_ff_pre, m_norm_ff_post, v_w_in, v_ln_v_gain, v_ln_v_bias, v_w_spatial, v_b_spatial, v_sinks, v_w_a, v_w_b, v_w_o, v_norm_mix_pre, v_norm_mix_post, v_w_ff_in, v_w_ff_out, v_norm_ff_pre, v_norm_ff_post):
    given = dict(locals())
    T = x.shape[1]
    xt = x[0]
    tgt = loss_target[0]
    bst = b_spatial[0].T
    ws = w_spatial[0]

    me = 4 * lax.axis_index("x") + 2 * lax.axis_index("y") + lax.axis_index("c")
    me_arr = me.astype(jnp.int32).reshape(1)

    rest = ("w_a", "w_b", "w_o", "w_ff_in", "w_ff_out")
    shard = {n: given[n][0].astype(BF16) for n in rest}
    g_one = _gather_first_leg(w_in[0].T.astype(BF16), "gather_in_start")
    cos, sin = _rope_tables(positions.astype(F32).reshape(T, 1), *_rope_rows(), after=g_one[-1])
    small_state = [_pack_small({n: given[k + n] for n in SMALL}) for k in ("", "m_", "v_")]
    h = _rms_pre(xt, norm_mix_pre, after=[cos, *small_state, *[shard[n] for n in rest]])
    _, (win8,) = _wait_copies(g_one, h, "gather_in_wait", count=5)
    g_two = _gather_second_leg(win8, "gather_in_pass_start")
    g_rest = _start_copies([shard[n] for n in rest], [GATHER] * len(rest), "gather_rest_start", after=g_two[-1])
    _, (win8,) = _wait_copies(g_two, g_rest[-1], "gather_in_pass_wait", count=3)
    win = win8.reshape(IN_W, D)

    proj = _fwd_in(h, win)
    att, qr, kr, probs, psink = _fwd_attn(proj, cos, sin, sinks[0])
    a, tanhs, ln_stats = _fwd_sgu(proj, ln_v_gain, ln_v_bias, ws, bst, after=att)
    gw = dict(zip(rest, _wait_copies(g_rest, a, "gather_rest_wait", count=N_DEV)[1]))
    wa, wb, wo = (gw[n].reshape(D, D) for n in ("w_a", "w_b", "w_o"))
    wfi3 = gw["w_ff_in"]
    wfo = gw["w_ff_out"].reshape(D_FF, D)
    merged, a2, b2, mix, x1, hf = _fwd_mix(a, att, proj, xt, wa, wb, wo, norm_mix_post, norm_ff_pre)
    f, dy, dff, dg3, loss_part = _fwd_ff(hf, wfi3, wfo, x1, tgt, norm_ff_post)

    df, dx1, dmix, dg2, dg1 = _bwd_ff(dff, f, wfi3, wfo, x1, dy, mix, norm_mix_post, norm_ff_pre)
    dwfi3, dwfo = _wgrad_ff(hf, df, f, dff)
    own_ff = [dwfi3, dwfo.reshape(N_DEV, D_FF // N_DEV, D)]
    x_ff = _start_copies(own_ff, [SCATTER] * 2, "exchange_ff_start")
    dgate, da, datt, dwo, dwa, dwb = _bwd_mix(dmix, proj, a2, b2, merged, a, att, wo, wa, wb, after=x_ff[-1])
    own_mix = [g.reshape(N_DEV, D // N_DEV, D) for g in (dwa, dwb, dwo)]
    x_mix = _start_copies(own_mix, [SCATTER] * 3, "exchange_mix_start")
    dq, dkv, dsink = _bwd_attn(qr, kr, probs, psink, proj, cos, sin, datt, after=x_mix[-1])
    duv, dws, dbs, dlng, dlnb = _bwd_sgu(proj, tanhs, ln_stats, da, ln_v_gain, ln_v_bias, ws, bst)
    small_grads = {"ln_v_gain": dlng, "ln_v_bias": dlnb, "w_spatial": dws, "b_spatial": dbs, "sinks": jnp.sum(dsink, axis=1),
                   "norm_mix_post": dg1, "norm_ff_pre": dg2, "norm_ff_post": dg3}
    x_small = _start_copies([_pack_small(small_grads)], [SPREAD], "exchange_small_start")
    dwin = _wgrad_rows(h, [dgate], sum(IN_SEG_WIDTHS[:3]), None, "wgrad_in_gates")
    dwin = _wgrad_rows(h, [duv], 0, dwin, "wgrad_in_uv")
    dwin = _wgrad_rows(h, [dq, dkv], IN_SEG_WIDTHS[0], dwin, "wgrad_in_qkv")
    own_in = [dwin.reshape(N_DEV, IN_W // N_DEV, D)]
    x_in = _start_copies(own_in, [SCATTER], "exchange_in_start", after=x_small[-1])
    grad_x, dg0 = _bwd_in(duv, dq, dkv, dgate, win, xt, dx1, norm_mix_pre, after=x_in[-1])

    results = {}

    def update(n, own, parts, transposed=False):
        state = [given[k + n][0].T if transposed else given[k + n][0] for k in ("", "m_", "v_")]
        res = _sum_adamw_stream(me_arr, own, parts, *state, "adamw_" + n)
        results[n] = [(r.T if transposed else r).reshape(given[n].shape) for r in res]

    own_ff, p_ff = _wait_copies(x_ff, grad_x, "exchange_ff_wait")
    update("w_ff_in", own_ff[0], p_ff[0])
    update("w_ff_out", own_ff[1], p_ff[1])
    own_mix, p_mix = _wait_copies(x_mix, results["w_ff_out"][0], "exchange_mix_wait")
    for n, own, parts in zip(("w_a", "w_b", "w_o"), own_mix, p_mix):
        update(n, own, parts)
    tail = jnp.concatenate([dg0.reshape(8, 128), jnp.tile(loss_part, (8, 1))], axis=0)
    x_tail = _start_copies([tail], [GATHER], "exchange_tail_start", after=results["w_o"][0])
    own_small, p_small = _wait_copies(x_small, x_tail[-1], "exchange_small_wait")
    own_in, p_in = _wait_copies(x_in, p_small[0], "exchange_in_wait")
    update("w_in", own_in[0], p_in[0], transposed=True)
    first = sum(SMALL_ROWS[n] for n in SMALL[:SMALL.index("w_spatial")])
    packed = _sum_adamw_peers(me_arr, own_small[0], p_small[0], *small_state, "adamw_small", True,
                              also_rows=(first, first + SMALL_ROWS["w_spatial"]))
    shapes = {n: given[n].shape for n in SMALL}
    unpacked = [_unpack_small(p, shapes) for p in packed[:4]]
    for n in SMALL:
        results[n] = [u[n] for u in unpacked]
    results["w_spatial"] = [r.reshape(w_spatial.shape) for r in packed[4:]]
    (tail_all,) = _wait_copies(x_tail, results["w_in"][0], "exchange_tail_wait", count=N_DEV)[1]
    dg0_all = tail_all[:, :8]
    n = "norm_mix_pre"
    results[n] = [r.reshape(given[n].shape) for r in _sum_adamw(
        dg0_all, given[n].reshape(8, 128), given["m_" + n].reshape(8, 128), given["v_" + n].reshape(8, 128), "adamw_" + n)]

    loss = jnp.sum(tail_all[:, 8, 0])
    order = ("w_in", "ln_v_gain", "ln_v_bias", "w_spatial", "b_spatial", "sinks", "w_a", "w_b", "w_o", "norm_mix_pre",
             "norm_mix_post", "w_ff_in", "w_ff_out", "norm_ff_pre", "norm_ff_post")
    out = [loss, grad_x.reshape(x.shape)]
    for k in range(4):
        out += [results[n][k] for n in order]
    return tuple(out)
```

```python
import jax
import jax.numpy as jnp
from jax import lax
from jax.experimental import pallas as pl
from jax.experimental.pallas import tpu as pltpu

F32 = jnp.float32
BF16 = jnp.bfloat16

N_DEV = 8
D = 1024
D_FF = 4096
IN_W = 5632
CHUNK = 128
GROUPS = 8
HEAD = 64
N_Q = 16
N_KV = 4
ROPE = 16
ROPE_THETA = 500000.0
EPS = 1e-6
OFF_Q, OFF_K, OFF_VA, OFF_GA, OFF_GB = 2048, 3072, 3328, 3584, 4608

ADAM_LR = 0.001
ADAM_B1 = 0.9
ADAM_B2 = 0.999
ADAM_EPS = 1e-08
ADAM_WD = 0.01
ADAM_STEP = 10

VMEM_LIMIT = 62 * 1024 * 1024

SDS = jax.ShapeDtypeStruct
MESH = pl.DeviceIdType.MESH


def _params(n_axes):
    return pltpu.CompilerParams(dimension_semantics=("arbitrary",) * n_axes, vmem_limit_bytes=VMEM_LIMIT)


def _nt(a, b):
    return lax.dot_general(a, b, (((1,), (1,)), ((), ())), preferred_element_type=F32)


def _tn(a, b):
    return lax.dot_general(a, b, (((0,), (0,)), ((), ())), preferred_element_type=F32)


def _nn(a, b):
    return jnp.dot(a, b, preferred_element_type=F32)


def _gelu(x):
    t = jnp.tanh(0.7978845608028654 * (x + 0.044715 * (x * x * x)))
    return 0.5 * x * (1.0 + t), t


def _gelu_grad(x, t):
    return 0.5 * (1.0 + t) + 0.5 * x * (1.0 - t * t) * (0.7978845608028654 * (1.0 + 3.0 * 0.044715 * x * x))


def _sigmoid(x):
    return 1.0 / (1.0 + jnp.exp(-x))


def _rms_stats(v):
    r = lax.rsqrt(jnp.mean(v * v, axis=-1, keepdims=True) + EPS)
    return r, v * r


def _rms_bwd(d, vhat, r, g):
    gd = g * d
    return r * (gd - vhat * jnp.mean(gd * vhat, axis=-1, keepdims=True))


def _colsum(v):
    return jnp.sum(v, axis=0, keepdims=True)


_ANY = pl.BlockSpec(memory_space=pl.ANY)


def _after(body, n_in, after):
    if after is None:
        return body, [], []
    deps = list(after) if isinstance(after, (list, tuple)) else [after]

    def ordered(*refs):
        return body(*refs[:n_in], *refs[n_in + len(deps):])

    return ordered, [_ANY] * len(deps), deps


def _rms_pre(x, g0, after=None):
    T = x.shape[0]
    tm = min(T, 1024)

    def body(x_ref, g_ref, h_ref):
        _, xh = _rms_stats(x_ref[...])
        h_ref[...] = (xh * g_ref[...]).astype(BF16)

    body, dep_specs, deps = _after(body, 2, after)
    return pl.pallas_call(
        body, name="rms_pre", grid=(T // tm,),
        in_specs=[pl.BlockSpec((tm, D), lambda i: (i, 0)), pl.BlockSpec((1, D), lambda i: (0, 0))] + dep_specs,
        out_specs=pl.BlockSpec((tm, D), lambda i: (i, 0)),
        out_shape=SDS((T, D), BF16),
        compiler_params=_params(1),
    )(x, g0, *deps)


def _fwd_in(h, win_t):
    T = h.shape[0]
    tm, tn = min(T, 1024), 1408

    def body(h_ref, w_ref, p_ref):
        for j in range(IN_W // tn):
            cols = slice(j * tn, (j + 1) * tn)
            p_ref[:, cols] = _nt(h_ref[...], w_ref[cols, :]).astype(BF16)

    return pl.pallas_call(
        body, name="fwd_in", grid=(T // tm,),
        in_specs=[pl.BlockSpec((tm, D), lambda i: (i, 0)), _resident((IN_W, D))],
        out_specs=pl.BlockSpec((tm, IN_W), lambda i: (i, 0)),
        out_shape=SDS((T, IN_W), BF16),
        compiler_params=_params(1),
    )(h, win_t)


def _sgu_forward_parts(u_ref, vs_ref, lng_ref, lnb_ref):
    u = u_ref[...].astype(F32)
    vs = vs_ref[...].astype(F32)
    gu, tu = _gelu(u)
    gv, tv = _gelu(vs)
    mu = jnp.mean(gv, axis=-1, keepdims=True)
    dv = gv - mu
    rstd = lax.rsqrt(jnp.mean(dv * dv, axis=-1, keepdims=True) + EPS)
    vhat = dv * rstd
    vn = (vhat * lng_ref[...] + lnb_ref[...]).astype(BF16)
    return gu, tu, tv, mu, rstd, vn


def _sgu_forward_replay(u_ref, vs_ref, t_ref, stat_ref, lng_ref, lnb_ref):
    u = u_ref[...].astype(F32)
    vs = vs_ref[...].astype(F32)
    tu = t_ref[:, :D].astype(F32)
    tv = t_ref[:, D:].astype(F32)
    gu = 0.5 * u * (1.0 + tu)
    rstd = stat_ref[:, 1:2]
    vhat = (0.5 * vs * (1.0 + tv) - stat_ref[:, 0:1]) * rstd
    vn = (vhat * lng_ref[...] + lnb_ref[...]).astype(BF16)
    return u, vs, gu, tu, tv, rstd, vhat, vn


def _masked_ws(ws_ref, g):
    row = lax.broadcasted_iota(jnp.int32, (CHUNK, CHUNK), 0)
    col = lax.broadcasted_iota(jnp.int32, (CHUNK, CHUNK), 1)
    return jnp.where(row >= col, ws_ref[g], 0.0).astype(BF16)


def _fwd_sgu(proj, lng, lnb, ws, bst, after=None):
    T = proj.shape[0]
    tc = min(T, 512)

    def body(u_ref, vs_ref, lng_ref, lnb_ref, ws_ref, bst_ref, a_ref, t_ref, stat_ref):
        gu, tu, tv, mu, rstd, vn = _sgu_forward_parts(u_ref, vs_ref, lng_ref, lnb_ref)
        t_ref[:, :D] = tu.astype(BF16)
        t_ref[:, D:] = tv.astype(BF16)
        lane = lax.broadcasted_iota(jnp.int32, (tc, 128), 1)
        stat_ref[...] = jnp.where(lane == 0, mu, jnp.where(lane == 1, rstd, 0.0))
        for g in range(GROUPS):
            wm = _masked_ws(ws_ref, g)
            cols = slice(g * CHUNK, (g + 1) * CHUNK)
            for c in range(tc // CHUNK):
                rows = slice(c * CHUNK, (c + 1) * CHUNK)
                mixed = _nn(wm, vn[rows, cols]) + bst_ref[:, g:g + 1]
                a_ref[rows, cols] = (gu[rows, cols] * mixed).astype(BF16)

    body, dep_specs, deps = _after(body, 6, after)
    return pl.pallas_call(
        body, name="fwd_sgu", grid=(T // tc,),
        in_specs=[pl.BlockSpec((tc, D), lambda i: (i, 0)), pl.BlockSpec((tc, D), lambda i: (i, 1)),
                  pl.BlockSpec((1, D), lambda i: (0, 0)), pl.BlockSpec((1, D), lambda i: (0, 0)),
                  pl.BlockSpec((GROUPS, CHUNK, CHUNK), lambda i: (0, 0, 0)),
                  pl.BlockSpec((CHUNK, GROUPS), lambda i: (0, 0))] + dep_specs,
        out_specs=[pl.BlockSpec((tc, D), lambda i: (i, 0)), pl.BlockSpec((tc, 2 * D), lambda i: (i, 0)),
                   pl.BlockSpec((tc, 128), lambda i: (i, 0))],
        out_shape=[SDS((T, D), BF16), SDS((T, 2 * D), BF16), SDS((T, 128), F32)],
        compiler_params=_params(1),
    )(proj, proj, lng, lnb, ws, bst, *deps)


def _rope_tables(posf, invf, sgn, after=None):
    T = posf.shape[0]
    tr = min(T, 1024)

    def body(pos_ref, invf_ref, sgn_ref, c_ref, s_ref):
        ang = pos_ref[...] * invf_ref[...]
        c_ref[...] = jnp.cos(ang)
        s = jnp.sin(ang)
        s_ref[:, :128] = jnp.where(sgn_ref[...] < 0.0, -s, 0.0)
        s_ref[:, 128:] = jnp.where(sgn_ref[...] > 0.0, s, 0.0)

    body, dep_specs, deps = _after(body, 3, after)
    return pl.pallas_call(
        body, name="rope_tables", grid=(T // tr,),
        in_specs=[pl.BlockSpec((tr, 1), lambda i: (i, 0)), pl.BlockSpec((1, 128), lambda i: (0, 0)),
                  pl.BlockSpec((1, 128), lambda i: (0, 0))] + dep_specs,
        out_specs=[pl.BlockSpec((tr, 128), lambda i: (i, 0)), pl.BlockSpec((tr, 256), lambda i: (i, 0))],
        out_shape=[SDS((T, 128), F32), SDS((T, 256), F32)],
        compiler_params=_params(1),
    )(posf, invf, sgn, *deps)


def _rope(v, c, s):
    v = v.astype(F32)
    return v * c + pltpu.roll(v, 128 - ROPE // 2, 1) * s[:, :128] + pltpu.roll(v, ROPE // 2, 1) * s[:, 128:]


def _rope_bwd(dv, c, s):
    return dv * c + pltpu.roll(dv * s[:, :128], ROPE // 2, 1) + pltpu.roll(dv * s[:, 128:], 128 - ROPE // 2, 1)


def _fold_masks(first):
    jj = lax.broadcasted_iota(jnp.int32, (CHUNK, CHUNK), 0)
    t = lax.broadcasted_iota(jnp.int32, (CHUNK, CHUNK), 1)
    prev = jj > t
    return prev, jnp.where(prev & first, -1e30, 0.0)


def _fold(band, prev):
    return jnp.where(prev, band[:CHUNK], band[CHUNK:])


def _unfold(folded, prev):
    return jnp.concatenate([jnp.where(prev, folded, 0.0), jnp.where(prev, 0.0, folded)], axis=0)


def _softmax_sink(s, sink, key_axis):
    m = jnp.maximum(jnp.max(s, axis=key_axis, keepdims=True), sink)
    p = jnp.exp(s - m)
    esink = jnp.exp(sink - m)
    inv = 1.0 / (jnp.sum(p, axis=key_axis, keepdims=True) + esink)
    return p * inv, esink * inv


def _head_pair_operand(slab, g):
    lo = lax.broadcasted_iota(jnp.int32, slab.shape, 1) < HEAD
    if g % 2 == 0:
        first = jnp.where(lo, slab, 0.0)
        second = pltpu.roll(first, HEAD, 1)
    else:
        second = jnp.where(lo, 0.0, slab)
        first = pltpu.roll(second, HEAD, 1)
    return jnp.concatenate([first, second], axis=0).astype(BF16)


def _head_pair_gradient(acc, g):
    top, bot = acc[:2 * CHUNK], acc[2 * CHUNK:]
    lo = lax.broadcasted_iota(jnp.int32, top.shape, 1) < HEAD
    if g % 2 == 0:
        return jnp.where(lo, top, 0.0) + pltpu.roll(jnp.where(lo, 0.0, bot), HEAD, 1)
    return pltpu.roll(jnp.where(lo, top, 0.0), HEAD, 1) + jnp.where(lo, 0.0, bot)


PAIRS_PER_KV = N_Q // N_KV // 2
KV_W = N_KV * HEAD


def _band(prev_ref, cur_ref, cols=slice(None)):
    return jnp.concatenate([prev_ref[:, cols], cur_ref[:, cols]], axis=0)


def _fwd_attn(proj, cos, sin, sinks):
    T = proj.shape[0]
    nb = T // CHUNK
    cur = lambda i: i
    prev = lambda i: jnp.maximum(i - 1, 0)

    def body(q_ref, kp_ref, kc_ref, vp_ref, vc_ref, cp_ref, cc_ref, sp_ref, sc_ref, sink_ref,
             o_ref, qr_ref, kr_ref, p_ref, psink_ref):
        prev_slot, bias = _fold_masks(pl.program_id(0) == 0)
        c_band, s_band = _band(cp_ref, cc_ref), _band(sp_ref, sc_ref)
        for j in range(KV_W // 128):
            cols = slice(j * 128, (j + 1) * 128)
            k_slab = _rope(_band(kp_ref, kc_ref, cols), c_band, s_band)
            kr_ref[:, cols] = k_slab[CHUNK:].astype(BF16)
            v_slab = _band(vp_ref, vc_ref, cols).astype(F32)
            for g in (2 * j, 2 * j + 1):
                k2 = _head_pair_operand(k_slab, g)
                v2 = _head_pair_operand(v_slab, g)
                pairs = [g * PAIRS_PER_KV + r for r in range(PAIRS_PER_KV)]
                qps = []
                for pair in pairs:
                    lanes = slice(pair * 128, (pair + 1) * 128)
                    qps.append((_rope(q_ref[:, lanes], cc_ref[...], sc_ref[...]) * (HEAD ** -0.5)).astype(BF16))
                    qr_ref[:, lanes] = qps[-1]
                s2 = _nt(k2, jnp.concatenate(qps, axis=0))
                pcols = []
                for r, pair in enumerate(pairs):
                    ps = []
                    for e in range(2):
                        head = 2 * pair + e
                        s = _fold(s2[e * 2 * CHUNK:(e + 1) * 2 * CHUNK, r * 128:(r + 1) * 128], prev_slot) + bias
                        p, psink = _softmax_sink(s, sink_ref[head], 0)
                        p = p.astype(BF16)
                        p_ref[head] = p
                        psink_ref[head:head + 1, :] = psink
                        ps.append(_unfold(p, prev_slot))
                    pcols.append(jnp.concatenate(ps, axis=0))
                o = _tn(jnp.concatenate(pcols, axis=1), v2).astype(BF16)
                for r, pair in enumerate(pairs):
                    o_ref[:, pair * 128:(pair + 1) * 128] = o[r * CHUNK:(r + 1) * CHUNK]

    table = lambda which, width: pl.BlockSpec((CHUNK, width), lambda i: (which(i), 0))
    return pl.pallas_call(
        body, name="fwd_attn", grid=(nb,),
        in_specs=[pl.BlockSpec((CHUNK, D), lambda i: (i, OFF_Q // D)),
                  pl.BlockSpec((CHUNK, KV_W), lambda i: (prev(i), OFF_K // KV_W)),
                  pl.BlockSpec((CHUNK, KV_W), lambda i: (i, OFF_K // KV_W)),
                  pl.BlockSpec((CHUNK, KV_W), lambda i: (prev(i), OFF_VA // KV_W)),
                  pl.BlockSpec((CHUNK, KV_W), lambda i: (i, OFF_VA // KV_W)),
                  table(prev, 128), table(cur, 128), table(prev, 256), table(cur, 256),
                  pl.BlockSpec(memory_space=pltpu.SMEM)],
        out_specs=[pl.BlockSpec((CHUNK, D), lambda i: (i, 0)), pl.BlockSpec((CHUNK, D), lambda i: (i, 0)),
                   pl.BlockSpec((CHUNK, KV_W), lambda i: (i, 0)),
                   pl.BlockSpec((None, N_Q, CHUNK, CHUNK), lambda i: (i, 0, 0, 0)),
                   pl.BlockSpec((None, N_Q, CHUNK), lambda i: (i, 0, 0))],
        out_shape=[SDS((T, D), BF16), SDS((T, D), BF16), SDS((T, KV_W), BF16),
                   SDS((nb, N_Q, CHUNK, CHUNK), BF16), SDS((nb, N_Q, CHUNK), F32)],
        compiler_params=_params(1),
    )(proj, proj, proj, proj, proj, cos, cos, sin, sin, sinks)


def _fwd_mix(a, att, proj, x, wa, wb, wo, g1, g2):
    T = x.shape[0]
    tm = min(T, 512)
    half = D // 2

    def body(a_ref, att_ref, ga0, ga1, gb0, gb1, x_ref, wa_ref, wb_ref, wo_ref, g1_ref, g2_ref,
             mg_ref, a2_ref, b2_ref, mix_ref, x1_ref, hf_ref):
        a2 = _nn(a_ref[...], wa_ref[...])
        b2 = _nn(att_ref[...], wb_ref[...])
        ga = jnp.concatenate([ga0[...], ga1[...]], axis=1).astype(F32)
        gb = jnp.concatenate([gb0[...], gb1[...]], axis=1).astype(F32)
        merged = (_sigmoid(ga) * a2 + _sigmoid(gb) * b2).astype(BF16)
        a2_ref[...] = a2.astype(BF16)
        b2_ref[...] = b2.astype(BF16)
        mg_ref[...] = merged
        mix = _nn(merged, wo_ref[...])
        mix_ref[...] = mix
        _, mh = _rms_stats(mix)
        x1 = x_ref[...] + mh * g1_ref[...]
        x1_ref[...] = x1
        _, xh = _rms_stats(x1)
        hf_ref[...] = (xh * g2_ref[...]).astype(BF16)

    row = lambda i: (i, 0)
    const = lambda i: (0, 0)
    gspec = lambda off: pl.BlockSpec((tm, half), lambda i: (i, off // half))
    return pl.pallas_call(
        body, name="fwd_mix", grid=(T // tm,),
        in_specs=[pl.BlockSpec((tm, D), row), pl.BlockSpec((tm, D), row),
                  gspec(OFF_GA), gspec(OFF_GA + half), gspec(OFF_GB), gspec(OFF_GB + half),
                  pl.BlockSpec((tm, D), row), _resident((D, D)), _resident((D, D)),
                  _resident((D, D)), pl.BlockSpec((1, D), const), pl.BlockSpec((1, D), const)],
        out_specs=[pl.BlockSpec((tm, D), row)] * 6,
        out_shape=[SDS((T, D), BF16), SDS((T, D), BF16), SDS((T, D), BF16), SDS((T, D), F32), SDS((T, D), F32),
                   SDS((T, D), BF16)],
        compiler_params=_params(1),
    )(a, att, proj, proj, proj, proj, x, wa, wb, wo, g1, g2)


FF_SPLIT = N_DEV
FF_TILE = D_FF // FF_SPLIT


def _fwd_ff(hf, wfi3, wfo, x1, tgt, g3):
    T = hf.shape[0]
    tm = min(T, 512)

    def body(hf_ref, wfi_ref, wfo_ref, x1_ref, tgt_ref, g3_ref, f_ref, dy_ref, dff_ref, dg3_ref, loss_ref, r_s):
        @pl.when(pl.program_id(0) == 0)
        def _():
            dg3_ref[...] = jnp.zeros_like(dg3_ref)
            loss_ref[...] = jnp.zeros_like(loss_ref)

        hf_t = hf_ref[...]
        for s in range(FF_SPLIT):
            cols = slice(s * FF_TILE, (s + 1) * FF_TILE)
            f = _nn(hf_t, wfi_ref[s]).astype(BF16)
            f_ref[:, cols] = f
            rl = jnp.maximum(f.astype(F32), 0.0)
            r_s[:, cols] = (rl * rl).astype(BF16)
        r3, fh = _rms_stats(_nn(r_s[...], wfo_ref[...]))
        e = x1_ref[...] + fh * g3_ref[...] - tgt_ref[...]
        loss_ref[...] += jnp.sum(e * e) * (0.5 / D)
        dy = e * (1.0 / D)
        dy_ref[...] = dy
        dg3_ref[...] += _colsum(dy * fh)
        dff_ref[...] = _rms_bwd(dy, fh, r3, g3_ref[...]).astype(BF16)

    row = lambda i: (i, 0)
    const = lambda i: (0, 0)
    return pl.pallas_call(
        body, name="fwd_ff", grid=(T // tm,),
        in_specs=[pl.BlockSpec((tm, D), row), _resident((FF_SPLIT, D, FF_TILE)), _resident((D_FF, D)),
                  pl.BlockSpec((tm, D), row),
                  pl.BlockSpec((tm, D), row), pl.BlockSpec((1, D), const)],
        out_specs=[pl.BlockSpec((tm, D_FF), row), pl.BlockSpec((tm, D), row),
                   pl.BlockSpec((tm, D), row), pl.BlockSpec((1, D), const), pl.BlockSpec((1, 128), const)],
        out_shape=[SDS((T, D_FF), BF16), SDS((T, D), F32), SDS((T, D), BF16), SDS((1, D), F32), SDS((1, 128), F32)],
        scratch_shapes=[pltpu.VMEM((tm, D_FF), BF16)],
        compiler_params=_params(1),
    )(hf, wfi3, wfo, x1, tgt, g3)


def _bwd_ff(dff, f, wfi3, wfo, x1, dy, mix, g1, g2):
    T = dff.shape[0]
    tm = min(T, 512)

    def body(dff_ref, f_ref, wfi_ref, wfo_ref, x1_ref, dy_ref, mix_ref, g1_ref, g2_ref,
             df_ref, dx1_ref, dmix_ref, dg2_ref, dg1_ref):
        @pl.when(pl.program_id(0) == 0)
        def _():
            dg2_ref[...] = jnp.zeros_like(dg2_ref)
            dg1_ref[...] = jnp.zeros_like(dg1_ref)

        dff_t = dff_ref[...]
        dhf = None
        for s in range(FF_SPLIT):
            cols = slice(s * FF_TILE, (s + 1) * FF_TILE)
            dr = _nt(dff_t, wfo_ref[cols, :])
            df = (dr * (2.0 * jnp.maximum(f_ref[:, cols].astype(F32), 0.0))).astype(BF16)
            df_ref[:, cols] = df
            part = _nt(df, wfi_ref[s])
            dhf = part if dhf is None else dhf + part
        r2, xh = _rms_stats(x1_ref[...])
        dg2_ref[...] += _colsum(dhf * xh)
        dx1 = dy_ref[...] + _rms_bwd(dhf, xh, r2, g2_ref[...])
        dx1_ref[...] = dx1
        r1, mh = _rms_stats(mix_ref[...])
        dg1_ref[...] += _colsum(dx1 * mh)
        dmix_ref[...] = _rms_bwd(dx1, mh, r1, g1_ref[...]).astype(BF16)

    row = lambda i: (i, 0)
    const = lambda i: (0, 0)
    return pl.pallas_call(
        body, name="bwd_ff", grid=(T // tm,),
        in_specs=[pl.BlockSpec((tm, D), row), pl.BlockSpec((tm, D_FF), row),
                  _resident((FF_SPLIT, D, FF_TILE)), _resident((D_FF, D)),
                  pl.BlockSpec((tm, D), row), pl.BlockSpec((tm, D), row), pl.BlockSpec((tm, D), row),
                  pl.BlockSpec((1, D), const), pl.BlockSpec((1, D), const)],
        out_specs=[pl.BlockSpec((tm, D_FF), row), pl.BlockSpec((tm, D), row),
                   pl.BlockSpec((tm, D), row), pl.BlockSpec((1, D), const), pl.BlockSpec((1, D), const)],
        out_shape=[SDS((T, D_FF), BF16), SDS((T, D), F32), SDS((T, D), BF16), SDS((1, D), F32), SDS((1, D), F32)],
        compiler_params=_params(1),
    )(dff, f, wfi3, wfo, x1, dy, mix, g1, g2)


def _wgrad_ff(hf, df, f, dff):
    T = hf.shape[0]
    tt = min(T, 2048)
    slabs = 2
    wide = slabs * FF_TILE

    def body(hf_ref, df_ref, f_ref, dff_ref, dwfi_ref, dwfo_ref, acc_i, acc_o):
        t = pl.program_id(1)

        @pl.when(t == 0)
        def _():
            acc_i[...] = jnp.zeros_like(acc_i)
            acc_o[...] = jnp.zeros_like(acc_o)

        acc_i[...] += _tn(hf_ref[...], df_ref[...])
        rl = jnp.maximum(f_ref[...].astype(F32), 0.0)
        acc_o[...] += _tn((rl * rl).astype(BF16), dff_ref[...])

        @pl.when(t == T // tt - 1)
        def _():
            for s in range(slabs):
                dwfi_ref[s] = acc_i[:, s * FF_TILE:(s + 1) * FF_TILE].astype(BF16)
            dwfo_ref[...] = acc_o[...].astype(BF16)

    return pl.pallas_call(
        body, name="wgrad_ff", grid=(D_FF // wide, T // tt),
        in_specs=[pl.BlockSpec((tt, D), lambda p, t: (t, 0)), pl.BlockSpec((tt, wide), lambda p, t: (t, p)),
                  pl.BlockSpec((tt, wide), lambda p, t: (t, p)), pl.BlockSpec((tt, D), lambda p, t: (t, 0))],
        out_specs=[pl.BlockSpec((slabs, D, FF_TILE), lambda p, t: (p, 0, 0)), pl.BlockSpec((wide, D), lambda p, t: (p, 0))],
        out_shape=[SDS((FF_SPLIT, D, FF_TILE), BF16), SDS((D_FF, D), BF16)],
        scratch_shapes=[pltpu.VMEM((D, wide), F32), pltpu.VMEM((wide, D), F32)],
        compiler_params=_params(2),
    )(hf, df, f, dff)


def _bwd_mix(dmix, proj, a2, b2, merged, a, att, wo, wa, wb, after=None):
    T = dmix.shape[0]
    tm = min(T, 512)
    half = D // 2
    last = T // tm - 1

    def body(dmix_ref, ga0, ga1, gb0, gb1, a2_ref, b2_ref, mg_ref, a_ref, att_ref, wo_ref, wa_ref, wb_ref,
             dg_ref, da_ref, datt_ref, dwo_ref, dwa_ref, dwb_ref, acc, stage, sem):
        t = pl.program_id(0)

        @pl.when(t == 0)
        def _():
            acc[...] = jnp.zeros_like(acc)

        dmix_t = dmix_ref[...]
        dmg = _nt(dmix_t, wo_ref[...])
        sa = _sigmoid(jnp.concatenate([ga0[...], ga1[...]], axis=1).astype(F32))
        sb = _sigmoid(jnp.concatenate([gb0[...], gb1[...]], axis=1).astype(F32))
        da2 = (dmg * sa).astype(BF16)
        db2 = (dmg * sb).astype(BF16)
        dg_ref[:, :D] = (dmg * a2_ref[...].astype(F32) * (sa * (1.0 - sa))).astype(BF16)
        dg_ref[:, D:] = (dmg * b2_ref[...].astype(F32) * (sb * (1.0 - sb))).astype(BF16)
        da_ref[...] = _nt(da2, wa_ref[...]).astype(BF16)
        datt_ref[...] = _nt(db2, wb_ref[...]).astype(BF16)
        acc[0] += _tn(mg_ref[...], dmix_t)
        acc[1] += _tn(a_ref[...], da2)
        acc[2] += _tn(att_ref[...], db2)

        @pl.when(t == last)
        def _():
            for k, dw_ref in enumerate((dwo_ref, dwa_ref, dwb_ref)):
                stage[...] = acc[k].astype(BF16)
                out = pltpu.make_async_copy(stage, dw_ref, sem)
                out.start()
                out.wait()

    row = lambda i: (i, 0)
    gspec = lambda off: pl.BlockSpec((tm, half), lambda i: (i, off // half))
    body, dep_specs, deps = _after(body, 13, after)
    return pl.pallas_call(
        body, name="bwd_mix", grid=(T // tm,),
        in_specs=[pl.BlockSpec((tm, D), row), gspec(OFF_GA), gspec(OFF_GA + half), gspec(OFF_GB), gspec(OFF_GB + half)]
        + [pl.BlockSpec((tm, D), row)] * 5 + [_resident((D, D))] * 3 + dep_specs,
        out_specs=[pl.BlockSpec((tm, 2 * D), row), pl.BlockSpec((tm, D), row), pl.BlockSpec((tm, D), row)] + [_ANY] * 3,
        out_shape=[SDS((T, 2 * D), BF16), SDS((T, D), BF16), SDS((T, D), BF16)] + [SDS((D, D), BF16)] * 3,
        scratch_shapes=[pltpu.VMEM((3, D, D), F32), pltpu.VMEM((D, D), BF16), pltpu.SemaphoreType.DMA],
        compiler_params=_params(1),
    )(dmix, proj, proj, proj, proj, a2, b2, merged, a, att, wo, wa, wb, *deps)


def _bwd_attn(qr, kr, probs, psink, proj, cos, sin, datt, after=None):
    T = proj.shape[0]
    nb = T // CHUNK
    cur = lambda i: jnp.minimum(i, nb - 1)
    prev = lambda i: jnp.maximum(jnp.minimum(i, nb - 1) - 1, 0)

    def body(q_ref, kp_ref, kc_ref, vp_ref, vc_ref, cp_ref, cc_ref, sp_ref, sc_ref, p_ref, psink_ref, do_ref,
             dq_ref, dkv_ref, dsink_ref, carry_k, carry_v):
        i = pl.program_id(0)

        @pl.when(i == 0)
        def _():
            carry_k[...] = jnp.zeros_like(carry_k)
            carry_v[...] = jnp.zeros_like(carry_v)
            dsink_ref[...] = jnp.zeros_like(dsink_ref)

        @pl.when(i < nb)
        def _():
            prev_slot, _ = _fold_masks(i == 0)
            c_band, s_band = _band(cp_ref, cc_ref), _band(sp_ref, sc_ref)
            for j in range(KV_W // 128):
                cols = slice(j * 128, (j + 1) * 128)
                k_slab = _band(kp_ref, kc_ref, cols).astype(F32)
                v_slab = _band(vp_ref, vc_ref, cols).astype(F32)
                dk_slab = jnp.zeros((2 * CHUNK, 128), F32)
                dv_slab = jnp.zeros((2 * CHUNK, 128), F32)
                for g in (2 * j, 2 * j + 1):
                    k2 = _head_pair_operand(k_slab, g)
                    v2 = _head_pair_operand(v_slab, g)
                    pairs = [g * PAIRS_PER_KV + r for r in range(PAIRS_PER_KV)]
                    q_stack = jnp.concatenate([q_ref[:, pr * 128:(pr + 1) * 128] for pr in pairs], axis=0)
                    do_stack = jnp.concatenate([do_ref[:, pr * 128:(pr + 1) * 128] for pr in pairs], axis=0)
                    dp2 = _nt(v2, do_stack)
                    pcols, dscols = [], []
                    for r, pair in enumerate(pairs):
                        ps, dss = [], []
                        for e in range(2):
                            head = 2 * pair + e
                            p_b = p_ref[head]
                            p = p_b.astype(F32)
                            dp = _fold(dp2[e * 2 * CHUNK:(e + 1) * 2 * CHUNK, r * 128:(r + 1) * 128], prev_slot)
                            delta = jnp.sum(p * dp, axis=0, keepdims=True)
                            ps.append(_unfold(p_b, prev_slot))
                            dss.append(_unfold((p * (dp - delta)).astype(BF16), prev_slot))
                            dsink_ref[head:head + 1, :] -= psink_ref[head:head + 1, :] * delta
                        pcols.append(jnp.concatenate(ps, axis=0))
                        dscols.append(jnp.concatenate(dss, axis=0))
                    ds2 = jnp.concatenate(dscols, axis=1)
                    dq = _tn(ds2, k2) * (HEAD ** -0.5)
                    for r, pair in enumerate(pairs):
                        dq_ref[:, pair * 128:(pair + 1) * 128] = _rope_bwd(
                            dq[r * CHUNK:(r + 1) * CHUNK], cc_ref[...], sc_ref[...]).astype(BF16)
                    dk_slab = dk_slab + _head_pair_gradient(_nn(ds2, q_stack), g)
                    dv_slab = dv_slab + _head_pair_gradient(_nn(jnp.concatenate(pcols, axis=1), do_stack), g)
                dk_slab = _rope_bwd(dk_slab, c_band, s_band)
                vcols = slice(KV_W + j * 128, KV_W + (j + 1) * 128)
                dkv_ref[:, cols] = (carry_k[:, cols] + dk_slab[:CHUNK]).astype(BF16)
                dkv_ref[:, vcols] = (carry_v[:, cols] + dv_slab[:CHUNK]).astype(BF16)
                carry_k[:, cols] = dk_slab[CHUNK:]
                carry_v[:, cols] = dv_slab[CHUNK:]

        @pl.when(i == nb)
        def _():
            dkv_ref[:, :KV_W] = carry_k[...].astype(BF16)
            dkv_ref[:, KV_W:] = carry_v[...].astype(BF16)

    table = lambda which, width: pl.BlockSpec((CHUNK, width), lambda i: (which(i), 0))
    body, dep_specs, deps = _after(body, 12, after)
    return pl.pallas_call(
        body, name="bwd_attn", grid=(nb + 1,),
        in_specs=[pl.BlockSpec((CHUNK, D), lambda i: (cur(i), 0)),
                  pl.BlockSpec((CHUNK, KV_W), lambda i: (prev(i), 0)),
                  pl.BlockSpec((CHUNK, KV_W), lambda i: (cur(i), 0)),
                  pl.BlockSpec((CHUNK, KV_W), lambda i: (prev(i), OFF_VA // KV_W)),
                  pl.BlockSpec((CHUNK, KV_W), lambda i: (cur(i), OFF_VA // KV_W)),
                  table(prev, 128), table(cur, 128), table(prev, 256), table(cur, 256),
                  pl.BlockSpec((None, N_Q, CHUNK, CHUNK), lambda i: (cur(i), 0, 0, 0)),
                  pl.BlockSpec((None, N_Q, CHUNK), lambda i: (cur(i), 0, 0)),
                  pl.BlockSpec((CHUNK, D), lambda i: (cur(i), 0))] + dep_specs,
        out_specs=[pl.BlockSpec((CHUNK, D), lambda i: (cur(i), 0)),
                   pl.BlockSpec((CHUNK, 2 * KV_W), lambda i: (jnp.maximum(i - 1, 0), 0)),
                   pl.BlockSpec((N_Q, CHUNK), lambda i: (0, 0))],
        out_shape=[SDS((T, D), BF16), SDS((T, 2 * KV_W), BF16), SDS((N_Q, CHUNK), F32)],
        scratch_shapes=[pltpu.VMEM((CHUNK, KV_W), F32), pltpu.VMEM((CHUNK, KV_W), F32)],
        compiler_params=_params(1),
    )(qr, kr, kr, proj, proj, cos, cos, sin, sin, probs, psink, datt, *deps)


def _bwd_sgu(proj, tanhs, stats, da, lng, lnb, ws, bst):
    T = proj.shape[0]
    tc = min(T, 512)
    nsteps = T // tc

    def body(u_ref, vs_ref, t_ref, stat_ref, da_ref, lng_ref, lnb_ref, ws_ref, bst_ref,
             duv_ref, dws_ref, dbs_ref, dlng_ref, dlnb_ref, dvn_s, dgu_s, dmx_sum):
        i = pl.program_id(0)

        @pl.when(i == 0)
        def _():
            dws_ref[...] = jnp.zeros_like(dws_ref)
            dlng_ref[...] = jnp.zeros_like(dlng_ref)
            dlnb_ref[...] = jnp.zeros_like(dlnb_ref)
            dmx_sum[...] = jnp.zeros_like(dmx_sum)

        u, vs, gu, tu, tv, rstd, vhat, vn = _sgu_forward_replay(u_ref, vs_ref, t_ref, stat_ref, lng_ref, lnb_ref)
        da = da_ref[...].astype(F32)
        for g in range(GROUPS):
            wm = _masked_ws(ws_ref, g)
            cols = slice(g * CHUNK, (g + 1) * CHUNK)
            dws = jnp.zeros((CHUNK, CHUNK), F32)
            dsum = jnp.zeros((CHUNK, CHUNK), F32)
            for c in range(tc // CHUNK):
                rows = slice(c * CHUNK, (c + 1) * CHUNK)
                vn_cg = vn[rows, cols]
                mixed = _nn(wm, vn_cg) + bst_ref[:, g:g + 1]
                dgu_s[rows, cols] = da[rows, cols] * mixed
                dmx = da[rows, cols] * gu[rows, cols]
                dmxb = dmx.astype(BF16)
                dws = dws + _nt(dmxb, vn_cg)
                dsum = dsum + dmx
                dvn_s[rows, cols] = _tn(wm, dmxb)
            dws_ref[g] += dws
            dmx_sum[:, cols] += dsum
        dvn = dvn_s[...]
        dlng_ref[...] += _colsum(dvn * vhat)
        dlnb_ref[...] += _colsum(dvn)
        dvh = dvn * lng_ref[...]
        dgv = rstd * (dvh - jnp.mean(dvh, axis=-1, keepdims=True) - vhat * jnp.mean(dvh * vhat, axis=-1, keepdims=True))
        duv_ref[:, :D] = (dgu_s[...] * _gelu_grad(u, tu)).astype(BF16)
        duv_ref[:, D:] = (dgv * _gelu_grad(vs, tv)).astype(BF16)

        @pl.when(i == nsteps - 1)
        def _():
            row = lax.broadcasted_iota(jnp.int32, (CHUNK, CHUNK), 0)
            col = lax.broadcasted_iota(jnp.int32, (CHUNK, CHUNK), 1)
            for g in range(GROUPS):
                dws_ref[g] = jnp.where(row >= col, dws_ref[g], 0.0)
                dbs_ref[g:g + 1, :] = _colsum(dmx_sum[:, g * CHUNK:(g + 1) * CHUNK].T)

    const2 = lambda i: (0, 0)
    return pl.pallas_call(
        body, name="bwd_sgu", grid=(nsteps,),
        in_specs=[pl.BlockSpec((tc, D), lambda i: (i, 0)), pl.BlockSpec((tc, D), lambda i: (i, 1)),
                  pl.BlockSpec((tc, 2 * D), lambda i: (i, 0)), pl.BlockSpec((tc, 128), lambda i: (i, 0)),
                  pl.BlockSpec((tc, D), lambda i: (i, 0)), pl.BlockSpec((1, D), const2), pl.BlockSpec((1, D), const2),
                  pl.BlockSpec((GROUPS, CHUNK, CHUNK), lambda i: (0, 0, 0)), pl.BlockSpec((CHUNK, GROUPS), const2)],
        out_specs=[pl.BlockSpec((tc, 2 * D), lambda i: (i, 0)), pl.BlockSpec((GROUPS, CHUNK, CHUNK), lambda i: (0, 0, 0)),
                   pl.BlockSpec((GROUPS, CHUNK), const2), pl.BlockSpec((1, D), const2), pl.BlockSpec((1, D), const2)],
        out_shape=[SDS((T, 2 * D), BF16), SDS((GROUPS, CHUNK, CHUNK), F32), SDS((GROUPS, CHUNK), F32),
                   SDS((1, D), F32), SDS((1, D), F32)],
        scratch_shapes=[pltpu.VMEM((tc, D), F32), pltpu.VMEM((tc, D), F32), pltpu.VMEM((CHUNK, D), F32)],
        compiler_params=_params(1),
    )(proj, proj, tanhs, stats, da, lng, lnb, ws, bst)


IN_SEG_WIDTHS = (2 * D, D, 2 * N_KV * HEAD, 2 * D)


def _resident(shape):
    return pl.BlockSpec(shape, lambda *_: (0,) * len(shape), pipeline_mode=pl.Buffered(1))


def _bwd_in(duv, dq, dkv, dg, win_t, x, dx1, g0, after=None):
    T = x.shape[0]
    tm = min(T, 512)

    def body(duv_ref, dq_ref, dkv_ref, dg_ref, w_ref, x_ref, dx1_ref, g0_ref, gx_ref, dg0_ref):
        @pl.when(pl.program_id(0) == 0)
        def _():
            dg0_ref[...] = jnp.zeros_like(dg0_ref)

        dh, off = None, 0
        for ref, width in zip((duv_ref, dq_ref, dkv_ref, dg_ref), IN_SEG_WIDTHS):
            part = _nn(ref[...], w_ref[off:off + width, :])
            dh = part if dh is None else dh + part
            off += width
        r0, xh = _rms_stats(x_ref[...])
        dg0_ref[...] += _colsum(dh * xh)
        gx_ref[...] = dx1_ref[...] + _rms_bwd(dh, xh, r0, g0_ref[...])

    row = lambda i: (i, 0)
    body, dep_specs, deps = _after(body, 8, after)
    return pl.pallas_call(
        body, name="bwd_in", grid=(T // tm,),
        in_specs=[pl.BlockSpec((tm, w), row) for w in IN_SEG_WIDTHS] + [
            _resident((IN_W, D)), pl.BlockSpec((tm, D), row), pl.BlockSpec((tm, D), row),
            pl.BlockSpec((1, D), lambda i: (0, 0))] + dep_specs,
        out_specs=[pl.BlockSpec((tm, D), row), pl.BlockSpec((1, D), lambda i: (0, 0))],
        out_shape=[SDS((T, D), F32), SDS((1, D), F32)],
        compiler_params=_params(1),
    )(duv, dq, dkv, dg, win_t, x, dx1, g0, *deps)


def _wgrad_rows(h, segs, first_row, into, name):
    T = h.shape[0]
    tt = min(T, 2048)
    widths = [s.shape[1] for s in segs]
    rows = sum(widths)
    n_in = 1 + len(segs) + (into is not None)

    def body(*refs):
        h_ref, seg_refs = refs[0], refs[1:1 + len(segs)]
        dw_ref, acc, stage, sem = refs[n_in], refs[n_in + 1], refs[n_in + 2], refs[n_in + 3]
        t = pl.program_id(0)

        @pl.when(t == 0)
        def _():
            acc[...] = jnp.zeros_like(acc)

        off = 0
        for ref, width in zip(seg_refs, widths):
            acc[off:off + width, :] += _tn(ref[...], h_ref[...])
            off += width

        @pl.when(t == T // tt - 1)
        def _():
            stage[...] = acc[...].astype(BF16)
            out = pltpu.make_async_copy(stage, dw_ref.at[pl.ds(first_row, rows)], sem)
            out.start()
            out.wait()

    row = lambda t: (t, 0)
    return pl.pallas_call(
        body, name=name, grid=(T // tt,),
        in_specs=[pl.BlockSpec((tt, D), row)] + [pl.BlockSpec((tt, w), row) for w in widths] + [_ANY] * (into is not None),
        out_specs=_ANY,
        out_shape=SDS((IN_W, D), BF16),
        input_output_aliases={} if into is None else {n_in - 1: 0},
        scratch_shapes=[pltpu.VMEM((rows, D), F32), pltpu.VMEM((rows, D), BF16), pltpu.SemaphoreType.DMA],
        compiler_params=_params(1),
    )(h, *segs, *([] if into is None else [into]))


def _place():
    x, y, c = lax.axis_index("x"), lax.axis_index("y"), lax.axis_index("c")
    return x, y, c, 4 * x + 2 * y + c


def _peers(x, y, c):
    out = []
    for mask in range(1, N_DEV):
        px = 1 - x if mask & 4 else x
        py = 1 - y if mask & 2 else y
        pc = 1 - c if mask & 1 else c
        out.append(((px, py, pc), 4 * px + 2 * py + pc))
    return out


def _all_to_all(arrays, gather, name, after=None):
    n = len(arrays)

    def body(*refs):
        ins, outs = refs[:n], refs[n:2 * n]
        send_sems, recv_sems, local_sems = refs[2 * n:]
        x, y, c, me = _place()
        local, sends, recvs = [], [], []
        for a in range(n):
            src_own = ins[a] if gather[a] else ins[a].at[me]
            local.append(pltpu.make_async_copy(src_own, outs[a].at[me], local_sems.at[a]))
            for k, (peer, pid) in enumerate(_peers(x, y, c)):
                sem = a * (N_DEV - 1) + k
                src = ins[a] if gather[a] else ins[a].at[pid]
                sends.append(pltpu.make_async_remote_copy(
                    src_ref=src, dst_ref=outs[a].at[me], send_sem=send_sems.at[sem], recv_sem=recv_sems.at[sem],
                    device_id=peer, device_id_type=MESH))
                recvs.append(pltpu.make_async_remote_copy(
                    src_ref=src, dst_ref=outs[a].at[pid], send_sem=send_sems.at[sem], recv_sem=recv_sems.at[sem],
                    device_id=peer, device_id_type=MESH))
        for cp in local + sends:
            cp.start()
        for cp in recvs:
            cp.wait_recv()
        for cp in sends:
            cp.wait_send()
        for cp in local:
            cp.wait()

    out_shape = [SDS((N_DEV,) + a.shape if gt else a.shape, a.dtype) for a, gt in zip(arrays, gather)]
    nsem = n * (N_DEV - 1)
    body, dep_specs, deps = _after(body, n, after)
    return pl.pallas_call(
        body, name=name,
        in_specs=[pl.BlockSpec(memory_space=pl.ANY)] * n + dep_specs,
        out_specs=[pl.BlockSpec(memory_space=pl.ANY)] * n,
        out_shape=out_shape,
        scratch_shapes=[pltpu.SemaphoreType.DMA((nsem,)), pltpu.SemaphoreType.DMA((nsem,)), pltpu.SemaphoreType.DMA((n,))],
    )(*arrays, *deps)


_HBM = pl.BlockSpec(memory_space=pltpu.HBM)
_SEM = pl.BlockSpec(memory_space=pltpu.SEMAPHORE)
_EFFECT = pltpu.SideEffectType.DATAFLOW_SIDE_EFFECTING
GATHER = "gather"
SCATTER = "scatter"
SPREAD = "spread"


def _zone_shape(a, mode):
    if mode == GATHER:
        return (N_DEV,) + a.shape
    return (N_DEV - 1,) + (a.shape[1:] if mode == SCATTER else a.shape)


def _start_copies(arrays, modes, name, after=None):
    n = len(arrays)
    zones = [lax.empty(_zone_shape(a, m), a.dtype) for a, m in zip(arrays, modes)]

    def body(*refs):
        ins, lands = refs[:n], refs[n:2 * n]
        send_sems, recv_sems = refs[-2 * n - 3], refs[-2 * n - 2]
        token = refs[-1]
        x, y, c, me = _place()
        for a in range(n):
            for k, (peer, pid) in enumerate(_peers(x, y, c)):
                src = ins[a].at[pid] if modes[a] == SCATTER else ins[a]
                dst = lands[a].at[me] if modes[a] == GATHER else lands[a].at[k]
                pltpu.make_async_remote_copy(src_ref=src, dst_ref=dst, send_sem=send_sems.at[a], recv_sem=recv_sems.at[a],
                                             device_id=peer, device_id_type=MESH).start()
            if modes[a] == GATHER:
                pltpu.make_async_remote_copy(src_ref=ins[a], dst_ref=lands[a].at[me], send_sem=send_sems.at[a],
                                             recv_sem=recv_sems.at[a], device_id=(x, y, c), device_id_type=MESH).start()
        token[...] = jnp.zeros_like(token)

    hbm = lambda a: pltpu.HBM(a.shape, a.dtype)
    sems = pltpu.SemaphoreType.DMA((n,))
    extra = [] if after is None else [after]
    operands = [pltpu.with_memory_space_constraint(a, pltpu.HBM) for a in list(arrays) + zones]
    res = pl.pallas_call(
        body, name=name,
        out_shape=(sems, sems, *[hbm(a) for a in arrays], *[hbm(z) for z in zones], SDS((8, 128), F32)),
        in_specs=[_HBM] * (2 * n) + [_ANY] * len(extra),
        out_specs=(_SEM, _SEM, *[_HBM] * (2 * n), pl.BlockSpec(memory_space=pltpu.VMEM)),
        input_output_aliases={i: 2 + i for i in range(2 * n)},
        compiler_params=pltpu.CompilerParams(has_side_effects=_EFFECT),
    )(*operands, *extra)
    return res[0], res[1], list(res[2:2 + n]), list(res[2 + n:2 + 2 * n]), res[-1]


def _wait_copies(started, after, name, count=N_DEV - 1):
    send_sems, recv_sems, thru, zones, _ = started
    nt, nz = len(thru), len(zones)

    def body(*refs):
        lands = refs[nt:nt + nz]
        send_ref, recv_ref = refs[nt + nz], refs[nt + nz + 1]
        x, y, c, _ = _place()
        for a in range(nz):
            blocks = lands[a].at[pl.ds(0, count)]
            cp = pltpu.make_async_remote_copy(src_ref=blocks, dst_ref=blocks, send_sem=send_ref.at[a], recv_sem=recv_ref.at[a],
                                              device_id=(x, y, 1 - c), device_id_type=MESH)
            cp.wait_send()
            cp.wait_recv()

    hbm = lambda a: pltpu.HBM(a.shape, a.dtype)
    res = pl.pallas_call(
        body, name=name,
        out_shape=tuple(hbm(a) for a in thru + zones),
        in_specs=[_HBM] * (nt + nz) + [_SEM, _SEM, _ANY],
        out_specs=tuple([_HBM] * (nt + nz)),
        input_output_aliases={i: i for i in range(nt + nz)},
        compiler_params=pltpu.CompilerParams(has_side_effects=_EFFECT),
    )(*thru, *zones, send_sems, recv_sems, after)
    return list(res[:nt]), list(res[nt:])


def _split_start(body, arrays, zones, name, after):
    n = len(arrays) + len(zones)
    hbm = lambda a: pltpu.HBM(a.shape, a.dtype)
    sems = pltpu.SemaphoreType.DMA((max(len(zones), 1),))
    extra = [] if after is None else [after]
    operands = [pltpu.with_memory_space_constraint(a, pltpu.HBM) for a in list(arrays) + list(zones)]
    res = pl.pallas_call(
        body, name=name,
        out_shape=(sems, sems, *[hbm(a) for a in operands], SDS((8, 128), F32)),
        in_specs=[_HBM] * n + [_ANY] * len(extra),
        out_specs=(_SEM, _SEM, *[_HBM] * n, pl.BlockSpec(memory_space=pltpu.VMEM)),
        input_output_aliases={i: 2 + i for i in range(n)},
        compiler_params=pltpu.CompilerParams(has_side_effects=_EFFECT),
    )(*operands, *extra)
    return res[0], res[1], list(res[2:2 + len(arrays)]), list(res[2 + len(arrays):2 + n]), res[-1]


def _gather_first_leg(shard, name, after=None):
    zone = lax.empty((N_DEV,) + shard.shape, shard.dtype)
    extra = 0 if after is None else 1

    def body(*refs):
        src, land = refs[0], refs[1]
        send_sem, recv_sem, token = refs[2 + extra], refs[3 + extra], refs[-1]
        x, y, c, me = _place()
        for peer in ((x, y, c), (x, y, 1 - c), (1 - x, y, c), (x, 1 - y, c), (1 - x, 1 - y, c)):
            pltpu.make_async_remote_copy(src_ref=src, dst_ref=land.at[me], send_sem=send_sem.at[0], recv_sem=recv_sem.at[0],
                                         device_id=peer, device_id_type=MESH).start()
        token[...] = jnp.zeros_like(token)

    return _split_start(body, [shard], [zone], name, after)


def _gather_second_leg(zone, name, after=None):
    extra = 0 if after is None else 1

    def body(*refs):
        land = refs[0]
        send_sem, recv_sem, token = refs[1 + extra], refs[2 + extra], refs[-1]
        x, y, c, _ = _place()
        for px, py in ((1 - x, y), (x, 1 - y), (1 - x, 1 - y)):
            slot = 4 * px + 2 * py + c
            pltpu.make_async_remote_copy(src_ref=land.at[slot], dst_ref=land.at[slot], send_sem=send_sem.at[0],
                                         recv_sem=recv_sem.at[0], device_id=(x, y, 1 - c), device_id_type=MESH).start()
        token[...] = jnp.zeros_like(token)

    return _split_start(body, [], [zone], name, after)


UPDATE_BLOCK_ELEMS = 384 * 1024


def _update_rows(R, C):
    fits = [t for t in range(8, R + 1, 8) if R % t == 0 and t * C <= UPDATE_BLOCK_ELEMS]
    whole = [t for t in fits if t % 16 == 0]
    return max(whole or fits)


def _adamw_math(g, w, m, v):
    m2 = ADAM_B1 * m + (1.0 - ADAM_B1) * g
    v2 = ADAM_B2 * v + (1.0 - ADAM_B2) * (g * g)
    m_hat = m2 / (1.0 - ADAM_B1 ** ADAM_STEP)
    v_hat = v2 / (1.0 - ADAM_B2 ** ADAM_STEP)
    delta = -ADAM_LR * (m_hat / (jnp.sqrt(v_hat) + ADAM_EPS) + ADAM_WD * w)
    return delta, m2, v2


def _sum_adamw(parts, w, m, v, name):
    R, C = w.shape
    tr = _update_rows(R, C)

    def body(p_ref, w_ref, m_ref, v_ref, g_ref, d_ref, m2_ref, v2_ref):
        g = p_ref[0]
        for k in range(1, N_DEV):
            g = g + p_ref[k]
        g_ref[...] = g
        d_ref[...], m2_ref[...], v2_ref[...] = _adamw_math(g, w_ref[...], m_ref[...], v_ref[...])

    blk = pl.BlockSpec((tr, C), lambda i: (i, 0))
    return pl.pallas_call(
        body, name=name, grid=(R // tr,),
        in_specs=[pl.BlockSpec((N_DEV, tr, C), lambda i: (0, i, 0)), blk, blk, blk],
        out_specs=[blk] * 4,
        out_shape=[SDS((R, C), F32)] * 4,
        compiler_params=_params(1),
    )(parts, w, m, v)


def _sum_adamw_peers(me, own, parts, w, m, v, name, replicated, also_rows=None):
    R, C = w.shape
    tr = _update_rows(R, C)
    assert also_rows is None or tr == R

    def body(me_ref, own_ref, p_ref, w_ref, m_ref, v_ref, g_ref, d_ref, m2_ref, v2_ref, *extra):
        if replicated:
            mine = me_ref[0]
            g = None
            for j in range(N_DEV):
                k = jnp.maximum(jnp.bitwise_xor(mine, j) - 1, 0)
                term = jnp.where(mine == j, own_ref[...], p_ref[k])
                g = term if g is None else g + term
        else:
            g = own_ref[...].astype(F32)
            for k in range(N_DEV - 1):
                g = g + p_ref[k].astype(F32)
        results = (g,) + _adamw_math(g, w_ref[...], m_ref[...], v_ref[...])
        for ref, val in zip((g_ref, d_ref, m2_ref, v2_ref), results):
            ref[...] = val
        for ref, val in zip(extra, results):
            ref[...] = val[also_rows[0]:also_rows[1]]

    blk = pl.BlockSpec((tr, C), lambda i, me_ref: (i, 0))
    own_spec = blk if replicated else pl.BlockSpec((None, tr, C), lambda i, me_ref: (me_ref[0], i, 0))
    n_also = 0 if also_rows is None else also_rows[1] - also_rows[0]
    also_specs = [pl.BlockSpec((n_also, C), lambda i, me_ref: (0, 0))] * (4 if also_rows else 0)
    return pl.pallas_call(
        body, name=name,
        grid_spec=pltpu.PrefetchScalarGridSpec(
            num_scalar_prefetch=1, grid=(R // tr,),
            in_specs=[own_spec, pl.BlockSpec((N_DEV - 1, tr, C), lambda i, me_ref: (0, i, 0)), blk, blk, blk],
            out_specs=[blk] * 4 + also_specs),
        out_shape=[SDS((R, C), F32)] * 4 + [SDS((n_also, C), F32)] * len(also_specs),
        compiler_params=_params(1),
    )(me, own, parts, w, m, v)


UPDATE_CHUNK_ELEMS = 64 * 1024


def _chunk_rows(R, C):
    fits = [t for t in range(16, R + 1, 16) if R % t == 0 and t * C <= UPDATE_CHUNK_ELEMS]
    return max(fits) if fits else R


def _sum_adamw_stream(me, own, parts, w, m, v, name):
    R, C = w.shape
    tr = _chunk_rows(R, C)
    n = R // tr

    def body(me_ref, own_ref, p_ref, w_ref, m_ref, v_ref, g_ref, d_ref, m2_ref, v2_ref,
             own_v, p_v, w_v, m_v, v_v, g_v, d_v, m2_v, v2_v, in_sems, out_sems):
        mine = me_ref[0]
        loads = []
        for i in range(n):
            r = pl.ds(i * tr, tr)
            cps = [pltpu.make_async_copy(own_ref.at[mine, r], own_v.at[r], in_sems.at[i, 0]),
                   pltpu.make_async_copy(p_ref.at[:, r], p_v.at[:, r], in_sems.at[i, 1]),
                   pltpu.make_async_copy(w_ref.at[r], w_v.at[r], in_sems.at[i, 2]),
                   pltpu.make_async_copy(m_ref.at[r], m_v.at[r], in_sems.at[i, 3]),
                   pltpu.make_async_copy(v_ref.at[r], v_v.at[r], in_sems.at[i, 4])]
            for cp in cps:
                cp.start()
            loads.append(cps)
        stores = []
        for i in range(n):
            r = pl.ds(i * tr, tr)
            for cp in loads[i]:
                cp.wait()
            g = own_v[r].astype(F32)
            for k in range(N_DEV - 1):
                g = g + p_v[k, r].astype(F32)
            results = (g,) + _adamw_math(g, w_v[r], m_v[r], v_v[r])
            for j, (stage, out, val) in enumerate(zip((g_v, d_v, m2_v, v2_v), (g_ref, d_ref, m2_ref, v2_ref), results)):
                stage[r] = val
                cp = pltpu.make_async_copy(stage.at[r], out.at[r], out_sems.at[i, j])
                cp.start()
                stores.append(cp)
        for cp in stores:
            cp.wait()

    hbm = pl.BlockSpec(memory_space=pl.ANY)
    return pl.pallas_call(
        body, name=name,
        in_specs=[pl.BlockSpec(memory_space=pltpu.SMEM)] + [hbm] * 5,
        out_specs=[hbm] * 4,
        out_shape=[SDS((R, C), F32)] * 4,
        scratch_shapes=[pltpu.VMEM((R, C), own.dtype), pltpu.VMEM((N_DEV - 1, R, C), parts.dtype)]
        + [pltpu.VMEM((R, C), F32)] * 7
        + [pltpu.SemaphoreType.DMA((n, 5)), pltpu.SemaphoreType.DMA((n, 4))],
        compiler_params=_params(0),
    )(me, own, parts, w, m, v)


SMALL = ("ln_v_gain", "ln_v_bias", "w_spatial", "b_spatial", "sinks", "norm_mix_post", "norm_ff_pre", "norm_ff_post")
SMALL_ROWS = {"ln_v_gain": 8, "ln_v_bias": 8, "w_spatial": 1024, "b_spatial": 8, "sinks": 8,
              "norm_mix_post": 8, "norm_ff_pre": 8, "norm_ff_post": 8}
SMALL_PACK_ROWS = 1152


def _pack_small(vals):
    rows = []
    for name in SMALL:
        flat = vals[name].reshape(-1)
        pad = SMALL_ROWS[name] * 128 - flat.shape[0]
        if pad:
            flat = jnp.concatenate([flat, jnp.zeros((pad,), F32)])
        rows.append(flat.reshape(SMALL_ROWS[name], 128))
    rows.append(jnp.zeros((SMALL_PACK_ROWS - sum(SMALL_ROWS.values()), 128), F32))
    return jnp.concatenate(rows, axis=0)


def _unpack_small(packed, shapes):
    out, r = {}, 0
    for name in SMALL:
        n = 1
        for s in shapes[name]:
            n *= s
        out[name] = packed[r:r + SMALL_ROWS[name]].reshape(-1)[:n].reshape(shapes[name])
        r += SMALL_ROWS[name]
    return out


def _rope_rows():
    d = jnp.arange(128) % HEAD
    inv = ROPE_THETA ** (-(2.0 * (d % (ROPE // 2))).astype(F32) / ROPE)
    invf = jnp.where(d < ROPE, inv, 0.0).astype(F32).reshape(1, 128)
    sgn = jnp.where(d < ROPE // 2, -1.0, jnp.where(d < ROPE, 1.0, 0.0)).astype(F32).reshape(1, 128)
    return invf, sgn


def kernel(x, positions, w_in, ln_v_gain, ln_v_bias, w_spatial, b_spatial, sinks, w_a, w_b, w_o, norm_mix_pre, norm_mix_post, w_ff_in, w_ff_out, norm_ff_pre, norm_ff_post, loss_target, m_w_in, m_ln_v_gain, m_ln_v_bias, m_w_spatial, m_b_spatial, m_sinks, m_w_a, m_w_b, m_w_o, m_norm_mix_pre, m_norm_mix_post, m_w_ff_in, m_w_ff_out, m_norm_ff_pre, m_norm_ff_post, v_w_in, v_ln_v_gain, v_ln_v_bias, v_w_spatial, v_b_spatial, v_sinks, v_w_a, v_w_b, v_w_o, v_norm_mix_pre, v_norm_mix_post, v_w_ff_in, v_w_ff_out, v_norm_ff_pre, v_norm_ff_post):
    given = dict(locals())
    T = x.shape[1]
    xt = x[0]
    tgt = loss_target[0]
    bst = b_spatial[0].T
    ws = w_spatial[0]

    me = 4 * lax.axis_index("x") + 2 * lax.axis_index("y") + lax.axis_index("c")
    me_arr = me.astype(jnp.int32).reshape(1)

    rest = ("w_a", "w_b", "w_o", "w_ff_in", "w_ff_out")
    shard = {n: given[n][0].astype(BF16) for n in rest}
    g_one = _gather_first_leg(w_in[0].T.astype(BF16), "gather_in_start")
    cos, sin = _rope_tables(positions.astype(F32).reshape(T, 1), *_rope_rows(), after=g_one[-1])
    small_state = [_pack_small({n: given[k + n] for n in SMALL}) for k in ("", "m_", "v_")]
    h = _rms_pre(xt, norm_mix_pre, after=[cos, *small_state, *[shard[n] for n in rest]])
    _, (win8,) = _wait_copies(g_one, h, "gather_in_wait", count=5)
    g_two = _gather_second_leg(win8, "gather_in_pass_start")
    g_rest = _start_copies([shard[n] for n in rest], [GATHER] * len(rest), "gather_rest_start", after=g_two[-1])
    _, (win8,) = _wait_copies(g_two, g_rest[-1], "gather_in_pass_wait", count=3)
    win = win8.reshape(IN_W, D)

    proj = _fwd_in(h, win)
    att, qr, kr, probs, psink = _fwd_attn(proj, cos, sin, sinks[0])
    a, tanhs, ln_stats = _fwd_sgu(proj, ln_v_gain, ln_v_bias, ws, bst, after=att)
    gw = dict(zip(rest, _wait_copies(g_rest, a, "gather_rest_wait", count=N_DEV)[1]))
    wa, wb, wo = (gw[n].reshape(D, D) for n in ("w_a", "w_b", "w_o"))
    wfi3 = gw["w_ff_in"]
    wfo = gw["w_ff_out"].reshape(D_FF, D)
    merged, a2, b2, mix, x1, hf = _fwd_mix(a, att, proj, xt, wa, wb, wo, norm_mix_post, norm_ff_pre)
    f, dy, dff, dg3, loss_part = _fwd_ff(hf, wfi3, wfo, x1, tgt, norm_ff_post)

    df, dx1, dmix, dg2, dg1 = _bwd_ff(dff, f, wfi3, wfo, x1, dy, mix, norm_mix_post, norm_ff_pre)
    dwfi3, dwfo = _wgrad_ff(hf, df, f, dff)
    own_ff = [dwfi3, dwfo.reshape(N_DEV, D_FF // N_DEV, D)]
    x_ff = _start_copies(own_ff, [SCATTER] * 2, "exchange_ff_start")
    dgate, da, datt, dwo, dwa, dwb = _bwd_mix(dmix, proj, a2, b2, merged, a, att, wo, wa, wb, after=x_ff[-1])
    own_mix = [g.reshape(N_DEV, D // N_DEV, D) for g in (dwa, dwb, dwo)]
    x_mix = _start_copies(own_mix, [SCATTER] * 3, "exchange_mix_start")
    dq, dkv, dsink = _bwd_attn(qr, kr, probs, psink, proj, cos, sin, datt, after=x_mix[-1])
    duv, dws, dbs, dlng, dlnb = _bwd_sgu(proj, tanhs, ln_stats, da, ln_v_gain, ln_v_bias, ws, bst)
    small_grads = {"ln_v_gain": dlng, "ln_v_bias": dlnb, "w_spatial": dws, "b_spatial": dbs, "sinks": jnp.sum(dsink, axis=1),
                   "norm_mix_post": dg1, "norm_ff_pre": dg2, "norm_ff_post": dg3}
    x_small = _start_copies([_pack_small(small_grads)], [SPREAD], "exchange_small_start")
    dwin = _wgrad_rows(h, [dgate], sum(IN_SEG_WIDTHS[:3]), None, "wgrad_in_gates")
    dwin = _wgrad_rows(h, [duv], 0, dwin, "wgrad_in_uv")
    dwin = _wgrad_rows(h, [dq, dkv], IN_SEG_WIDTHS[0], dwin, "wgrad_in_qkv")
    own_in = [dwin.reshape(N_DEV, IN_W // N_DEV, D)]
    x_in = _start_copies(own_in, [SCATTER], "exchange_in_start", after=x_small[-1])
    grad_x, dg0 = _bwd_in(duv, dq, dkv, dgate, win, xt, dx1, norm_mix_pre, after=x_in[-1])

    results = {}

    def update(n, own, parts, transposed=False):
        state = [given[k + n][0].T if transposed else given[k + n][0] for k in ("", "m_", "v_")]
        res = _sum_adamw_stream(me_arr, own, parts, *state, "adamw_" + n)
        results[n] = [(r.T if transposed else r).reshape(given[n].shape) for r in res]

    own_ff, p_ff = _wait_copies(x_ff, grad_x, "exchange_ff_wait")
    update("w_ff_in", own_ff[0], p_ff[0])
    update("w_ff_out", own_ff[1], p_ff[1])
    own_mix, p_mix = _wait_copies(x_mix, results["w_ff_out"][0], "exchange_mix_wait")
    for n, own, parts in zip(("w_a", "w_b", "w_o"), own_mix, p_mix):
        update(n, own, parts)
    tail = jnp.concatenate([dg0.reshape(8, 128), jnp.tile(loss_part, (8, 1))], axis=0)
    x_tail = _start_copies([tail], [GATHER], "exchange_tail_start", after=results["w_o"][0])
    own_small, p_small = _wait_copies(x_small, x_tail[-1], "exchange_small_wait")
    own_in, p_in = _wait_copies(x_in, p_small[0], "exchange_in_wait")
    update("w_in", own_in[0], p_in[0], transposed=True)
    first = sum(SMALL_ROWS[n] for n in SMALL[:SMALL.index("w_spatial")])
    packed = _sum_adamw_peers(me_arr, own_small[0], p_small[0], *small_state, "adamw_small", True,
                              also_rows=(first, first + SMALL_ROWS["w_spatial"]))
    shapes = {n: given[n].shape for n in SMALL}
    unpacked = [_unpack_small(p, shapes) for p in packed[:4]]
    for n in SMALL:
        results[n] = [u[n] for u in unpacked]
    results["w_spatial"] = [r.reshape(w_spatial.shape) for r in packed[4:]]
    (tail_all,) = _wait_copies(x_tail, packed[0], "exchange_tail_wait", count=N_DEV)[1]
    dg0_all = tail_all[:, :8]
    n = "norm_mix_pre"
    results[n] = [r.reshape(given[n].shape) for r in _sum_adamw(
        dg0_all, given[n].reshape(8, 128), given["m_" + n].reshape(8, 128), given["v_" + n].reshape(8, 128), "adamw_" + n)]

    loss = jnp.sum(tail_all[:, 8, 0])
    order = ("w_in", "ln_v_gain", "ln_v_bias", "w_spatial", "b_spatial", "sinks", "w_a", "w_b", "w_o", "norm_mix_pre",
             "norm_mix_post", "w_ff_in", "w_ff_out", "norm_ff_pre", "norm_ff_post")
    out = [loss, grad_x.reshape(x.shape)]
    for k in range(4):
        out += [results[n][k] for n in order]
    return tuple(out)
```
